```python
import jax, jax.numpy as jnp
from jax import lax
import numpy as np

D_MODEL = 1024
BATCH = 16
SEQ = 2048
DEPTH = 2

N_A_LAYERS = DEPTH // 2
N_B_LAYERS = DEPTH - N_A_LAYERS

CHUNK = 64
EPS = 1e-6
NEG_INF = -1e30

HEADS_A = 16
HEAD_DIM_A = D_MODEL // HEADS_A
LEFT_CHUNKS = 8
BAND = (LEFT_CHUNKS + 1) * CHUNK
MAX_REL = 128

HEADS_B = D_MODEL // 128
NOPE_DIM = 128
ROPE_DIM = 64
V_DIM = 128
Q_LORA = 768
KV_LORA = 256
ROPE_THETA = 10000.0
Q_BLOCK = 128

D_FF = ((8 * D_MODEL // 3 + 127) // 128) * 128

kernel_name = "yoco_chunked_relpos_mla_macaron"


def rms_norm(x, g):
    xf = x.astype(jnp.float32)
    y = xf * lax.rsqrt(jnp.mean(xf * xf, axis=-1, keepdims=True) + EPS)
    return (y * g.astype(jnp.float32)).astype(x.dtype)


def swiglu(h, w_in, w_out):
    u = h @ w_in
    return (jax.nn.silu(u[..., :D_FF]) * u[..., D_FF:]) @ w_out


def rope_tables(seq_len):
    half = ROPE_DIM // 2
    freqs = ROPE_THETA ** (-jnp.arange(half, dtype=jnp.float32) / half)
    ang = jnp.arange(seq_len, dtype=jnp.float32)[:, None] * freqs[None, :]
    return jnp.cos(ang), jnp.sin(ang)


def apply_rope(x, cos, sin):
    half = ROPE_DIM // 2
    cos = cos.astype(x.dtype)
    sin = sin.astype(x.dtype)
    x1, x2 = x[..., :half], x[..., half:]
    return jnp.concatenate([x1 * cos - x2 * sin, x2 * cos + x1 * sin], axis=-1)


def chunked_relpos_attention(h, w_qkv, w_o, rel_table):
    B, S, _ = h.shape
    nc = S // CHUNK
    qkv = (h @ w_qkv).reshape(B, S, 3, HEADS_A, HEAD_DIM_A)
    q, k, v = qkv[:, :, 0], qkv[:, :, 1], qkv[:, :, 2]
    pad = LEFT_CHUNKS * CHUNK
    kp = jnp.pad(k, ((0, 0), (pad, 0), (0, 0), (0, 0)))
    vp = jnp.pad(v, ((0, 0), (pad, 0), (0, 0), (0, 0)))
    qi = jnp.arange(CHUNK)[:, None]
    kj = jnp.arange(BAND)[None, :]
    rel_idx = jnp.clip(pad + qi - kj, -MAX_REL, MAX_REL) + MAX_REL
    bias = rel_table[:, rel_idx].astype(jnp.float32)
    q_chunks = jnp.moveaxis(q.reshape(B, nc, CHUNK, HEADS_A, HEAD_DIM_A), 1, 0)
    scale = HEAD_DIM_A ** -0.5

    def one_chunk(args):
        qc, c = args
        start = c * CHUNK
        kb = lax.dynamic_slice_in_dim(kp, start, BAND, axis=1)
        vb = lax.dynamic_slice_in_dim(vp, start, BAND, axis=1)
        s = jnp.einsum('bqhd,bkhd->bhqk', qc, kb).astype(jnp.float32) * scale + bias
        valid = kj >= pad - start
        s = jnp.where(valid[None, None], s, NEG_INF)
        p = jax.nn.softmax(s, axis=-1).astype(vb.dtype)
        return jnp.einsum('bhqk,bkhd->bqhd', p, vb)

    out = lax.map(one_chunk, (q_chunks, jnp.arange(nc)))
    out = jnp.moveaxis(out, 0, 1).reshape(B, S, HEADS_A * HEAD_DIM_A)
    return out @ w_o


def mla_shared_kv(h_kv, w_down, latent_norm, w_up, cos, sin):
    B, S, _ = h_kv.shape
    ckr = h_kv @ w_down
    c_kv = rms_norm(ckr[..., :KV_LORA], latent_norm)
    k_rope = apply_rope(ckr[..., KV_LORA:], cos, sin)
    kv = (c_kv @ w_up).reshape(B, S, HEADS_B, NOPE_DIM + V_DIM)
    return kv[..., :NOPE_DIM], k_rope, kv[..., NOPE_DIM:]


def mla_attention(h, w_dq, q_norm, w_uq, w_o, k_nope, k_rope, v, cos, sin):
    B, S, _ = h.shape
    cq = rms_norm(h @ w_dq, q_norm)
    q = (cq @ w_uq).reshape(B, S, HEADS_B, NOPE_DIM + ROPE_DIM)
    q_nope = q[..., :NOPE_DIM]
    q_rope = apply_rope(q[..., NOPE_DIM:], cos[:, None], sin[:, None])
    nb = S // Q_BLOCK
    qn_blocks = jnp.moveaxis(q_nope.reshape(B, nb, Q_BLOCK, HEADS_B, NOPE_DIM), 1, 0)
    qr_blocks = jnp.moveaxis(q_rope.reshape(B, nb, Q_BLOCK, HEADS_B, ROPE_DIM), 1, 0)
    key_chunk = jnp.arange(S) // CHUNK
    scale = (NOPE_DIM + ROPE_DIM) ** -0.5

    def one_block(args):
        qn, qr, bidx = args
        s = (jnp.einsum('bqhd,bkhd->bhqk', qn, k_nope)
             + jnp.einsum('bqhr,bkr->bhqk', qr, k_rope)).astype(jnp.float32) * scale
        q_chunk = (bidx * Q_BLOCK + jnp.arange(Q_BLOCK)) // CHUNK
        mask = key_chunk[None, :] <= q_chunk[:, None]
        s = jnp.where(mask[None, None], s, NEG_INF)
        p = jax.nn.softmax(s, axis=-1).astype(v.dtype)
        return jnp.einsum('bhqk,bkhd->bqhd', p, v)

    out = lax.map(one_block, (qn_blocks, qr_blocks, jnp.arange(nb)))
    out = jnp.moveaxis(out, 0, 1).reshape(B, S, HEADS_B * V_DIM)
    return out @ w_o


def _fwd_setup_inputs(seed: int = 0) -> dict:
    key = jax.random.key(seed)
    ks = jax.random.split(key, 24)

    def w(k, shape, fan_in):
        return jax.random.normal(k, shape, jnp.float32) * fan_in ** -0.5

    def gain(k, shape):
        return 1.0 + 0.05 * jax.random.normal(k, shape, jnp.float32)

    return {
        "x": jax.random.normal(ks[0], (BATCH, SEQ, D_MODEL), jnp.float32),
        "ffn1_norm": gain(ks[1], (DEPTH, D_MODEL)),
        "ffn1_w_in": w(ks[2], (DEPTH, D_MODEL, 2 * D_FF), D_MODEL),
        "ffn1_w_out": w(ks[3], (DEPTH, D_FF, D_MODEL), D_FF),
        "mix_norm": gain(ks[4], (DEPTH, D_MODEL)),
        "ffn2_norm": gain(ks[5], (DEPTH, D_MODEL)),
        "ffn2_w_in": w(ks[6], (DEPTH, D_MODEL, 2 * D_FF), D_MODEL),
        "ffn2_w_out": w(ks[7], (DEPTH, D_FF, D_MODEL), D_FF),
        "a_w_qkv": w(ks[8], (N_A_LAYERS, D_MODEL, 3 * HEADS_A * HEAD_DIM_A), D_MODEL),
        "a_rel_bias": 0.5 * jax.random.normal(ks[9], (N_A_LAYERS, HEADS_A, 2 * MAX_REL + 1), jnp.float32),
        "a_w_o": w(ks[10], (N_A_LAYERS, HEADS_A * HEAD_DIM_A, D_MODEL), HEADS_A * HEAD_DIM_A),
        "kv_norm": gain(ks[11], (D_MODEL,)),
        "kv_w_down": w(ks[12], (D_MODEL, KV_LORA + ROPE_DIM), D_MODEL),
        "kv_latent_norm": gain(ks[13], (KV_LORA,)),
        "kv_w_up": w(ks[14], (KV_LORA, HEADS_B * (NOPE_DIM + V_DIM)), KV_LORA),
        "b_w_dq": w(ks[15], (N_B_LAYERS, D_MODEL, Q_LORA), D_MODEL),
        "b_q_norm": gain(ks[16], (N_B_LAYERS, Q_LORA)),
        "b_w_uq": w(ks[17], (N_B_LAYERS, Q_LORA, HEADS_B * (NOPE_DIM + ROPE_DIM)), Q_LORA),
        "b_w_o": w(ks[18], (N_B_LAYERS, HEADS_B * V_DIM, D_MODEL), HEADS_B * V_DIM),
        "final_norm": gain(ks[19], (D_MODEL,)),
    }


def _fwd_reference(x, ffn1_norm, ffn1_w_in, ffn1_w_out, mix_norm, ffn2_norm, ffn2_w_in,
              ffn2_w_out, a_w_qkv, a_rel_bias, a_w_o, kv_norm, kv_w_down,
              kv_latent_norm, kv_w_up, b_w_dq, b_q_norm, b_w_uq, b_w_o, final_norm):
    S = x.shape[1]
    cos, sin = rope_tables(S)
    h = x
    k_nope = k_rope = v_shared = None
    for layer in range(DEPTH):
        h = h + 0.5 * swiglu(rms_norm(h, ffn1_norm[layer]), ffn1_w_in[layer], ffn1_w_out[layer])
        hn = rms_norm(h, mix_norm[layer])
        if layer < N_A_LAYERS:
            h = h + chunked_relpos_attention(hn, a_w_qkv[layer], a_w_o[layer], a_rel_bias[layer])
        else:
            li = layer - N_A_LAYERS
            h = h + mla_attention(hn, b_w_dq[li], b_q_norm[li], b_w_uq[li], b_w_o[li],
                                  k_nope, k_rope, v_shared, cos, sin)
        h = h + 0.5 * swiglu(rms_norm(h, ffn2_norm[layer]), ffn2_w_in[layer], ffn2_w_out[layer])
        if layer == N_A_LAYERS - 1:
            k_nope, k_rope, v_shared = mla_shared_kv(rms_norm(h, kv_norm), kv_w_down,
                                                     kv_latent_norm, kv_w_up, cos, sin)
    return rms_norm(h, final_norm)


import jax as _jax
import jax.numpy as _jnp

TWIN_FORMAT = 'train_step'
FWD_PARAMS = ['x', 'ffn1_norm', 'ffn1_w_in', 'ffn1_w_out', 'mix_norm', 'ffn2_norm', 'ffn2_w_in', 'ffn2_w_out', 'a_w_qkv', 'a_rel_bias', 'a_w_o', 'kv_norm', 'kv_w_down', 'kv_latent_norm', 'kv_w_up', 'b_w_dq', 'b_q_norm', 'b_w_uq', 'b_w_o', 'final_norm']
TWIN_WEIGHTS = ['ffn1_norm', 'ffn1_w_in', 'ffn1_w_out', 'mix_norm', 'ffn2_norm', 'ffn2_w_in', 'ffn2_w_out', 'a_w_qkv', 'a_rel_bias', 'a_w_o', 'kv_norm', 'kv_w_down', 'kv_latent_norm', 'kv_w_up', 'b_w_dq', 'b_q_norm', 'b_w_uq', 'b_w_o', 'final_norm']
TWIN_DIFF_INPUT = 'x'
TWIN_INPUTS = ['x', 'ffn1_norm', 'ffn1_w_in', 'ffn1_w_out', 'mix_norm', 'ffn2_norm', 'ffn2_w_in', 'ffn2_w_out', 'a_w_qkv', 'a_rel_bias', 'a_w_o', 'kv_norm', 'kv_w_down', 'kv_latent_norm', 'kv_w_up', 'b_w_dq', 'b_q_norm', 'b_w_uq', 'b_w_o', 'final_norm', 'loss_target', 'm_ffn1_norm', 'm_ffn1_w_in', 'm_ffn1_w_out', 'm_mix_norm', 'm_ffn2_norm', 'm_ffn2_w_in', 'm_ffn2_w_out', 'm_a_w_qkv', 'm_a_rel_bias', 'm_a_w_o', 'm_kv_norm', 'm_kv_w_down', 'm_kv_latent_norm', 'm_kv_w_up', 'm_b_w_dq', 'm_b_q_norm', 'm_b_w_uq', 'm_b_w_o', 'm_final_norm', 'v_ffn1_norm', 'v_ffn1_w_in', 'v_ffn1_w_out', 'v_mix_norm', 'v_ffn2_norm', 'v_ffn2_w_in', 'v_ffn2_w_out', 'v_a_w_qkv', 'v_a_rel_bias', 'v_a_w_o', 'v_kv_norm', 'v_kv_w_down', 'v_kv_latent_norm', 'v_kv_w_up', 'v_b_w_dq', 'v_b_q_norm', 'v_b_w_uq', 'v_b_w_o', 'v_final_norm']
TWIN_OUTPUTS = ['loss', 'grad_x', 'grad_ffn1_norm', 'grad_ffn1_w_in', 'grad_ffn1_w_out', 'grad_mix_norm', 'grad_ffn2_norm', 'grad_ffn2_w_in', 'grad_ffn2_w_out', 'grad_a_w_qkv', 'grad_a_rel_bias', 'grad_a_w_o', 'grad_kv_norm', 'grad_kv_w_down', 'grad_kv_latent_norm', 'grad_kv_w_up', 'grad_b_w_dq', 'grad_b_q_norm', 'grad_b_w_uq', 'grad_b_w_o', 'grad_final_norm', 'delta_ffn1_norm', 'delta_ffn1_w_in', 'delta_ffn1_w_out', 'delta_mix_norm', 'delta_ffn2_norm', 'delta_ffn2_w_in', 'delta_ffn2_w_out', 'delta_a_w_qkv', 'delta_a_rel_bias', 'delta_a_w_o', 'delta_kv_norm', 'delta_kv_w_down', 'delta_kv_latent_norm', 'delta_kv_w_up', 'delta_b_w_dq', 'delta_b_q_norm', 'delta_b_w_uq', 'delta_b_w_o', 'delta_final_norm', 'new_m_ffn1_norm', 'new_m_ffn1_w_in', 'new_m_ffn1_w_out', 'new_m_mix_norm', 'new_m_ffn2_norm', 'new_m_ffn2_w_in', 'new_m_ffn2_w_out', 'new_m_a_w_qkv', 'new_m_a_rel_bias', 'new_m_a_w_o', 'new_m_kv_norm', 'new_m_kv_w_down', 'new_m_kv_latent_norm', 'new_m_kv_w_up', 'new_m_b_w_dq', 'new_m_b_q_norm', 'new_m_b_w_uq', 'new_m_b_w_o', 'new_m_final_norm', 'new_v_ffn1_norm', 'new_v_ffn1_w_in', 'new_v_ffn1_w_out', 'new_v_mix_norm', 'new_v_ffn2_norm', 'new_v_ffn2_w_in', 'new_v_ffn2_w_out', 'new_v_a_w_qkv', 'new_v_a_rel_bias', 'new_v_a_w_o', 'new_v_kv_norm', 'new_v_kv_w_down', 'new_v_kv_latent_norm', 'new_v_kv_w_up', 'new_v_b_w_dq', 'new_v_b_q_norm', 'new_v_b_w_uq', 'new_v_b_w_o', 'new_v_final_norm']
TWIN_LEAF_KINDS = {'loss': 'loss', 'grad_x': 'grad_x', 'grad_ffn1_norm': 'grad_w', 'grad_ffn1_w_in': 'grad_w', 'grad_ffn1_w_out': 'grad_w', 'grad_mix_norm': 'grad_w', 'grad_ffn2_norm': 'grad_w', 'grad_ffn2_w_in': 'grad_w', 'grad_ffn2_w_out': 'grad_w', 'grad_a_w_qkv': 'grad_w', 'grad_a_rel_bias': 'grad_w', 'grad_a_w_o': 'grad_w', 'grad_kv_norm': 'grad_w', 'grad_kv_w_down': 'grad_w', 'grad_kv_latent_norm': 'grad_w', 'grad_kv_w_up': 'grad_w', 'grad_b_w_dq': 'grad_w', 'grad_b_q_norm': 'grad_w', 'grad_b_w_uq': 'grad_w', 'grad_b_w_o': 'grad_w', 'grad_final_norm': 'grad_w', 'delta_ffn1_norm': 'delta_w', 'delta_ffn1_w_in': 'delta_w', 'delta_ffn1_w_out': 'delta_w', 'delta_mix_norm': 'delta_w', 'delta_ffn2_norm': 'delta_w', 'delta_ffn2_w_in': 'delta_w', 'delta_ffn2_w_out': 'delta_w', 'delta_a_w_qkv': 'delta_w', 'delta_a_rel_bias': 'delta_w', 'delta_a_w_o': 'delta_w', 'delta_kv_norm': 'delta_w', 'delta_kv_w_down': 'delta_w', 'delta_kv_latent_norm': 'delta_w', 'delta_kv_w_up': 'delta_w', 'delta_b_w_dq': 'delta_w', 'delta_b_q_norm': 'delta_w', 'delta_b_w_uq': 'delta_w', 'delta_b_w_o': 'delta_w', 'delta_final_norm': 'delta_w', 'new_m_ffn1_norm': 'new_m', 'new_m_ffn1_w_in': 'new_m', 'new_m_ffn1_w_out': 'new_m', 'new_m_mix_norm': 'new_m', 'new_m_ffn2_norm': 'new_m', 'new_m_ffn2_w_in': 'new_m', 'new_m_ffn2_w_out': 'new_m', 'new_m_a_w_qkv': 'new_m', 'new_m_a_rel_bias': 'new_m', 'new_m_a_w_o': 'new_m', 'new_m_kv_norm': 'new_m', 'new_m_kv_w_down': 'new_m', 'new_m_kv_latent_norm': 'new_m', 'new_m_kv_w_up': 'new_m', 'new_m_b_w_dq': 'new_m', 'new_m_b_q_norm': 'new_m', 'new_m_b_w_uq': 'new_m', 'new_m_b_w_o': 'new_m', 'new_m_final_norm': 'new_m', 'new_v_ffn1_norm': 'new_v', 'new_v_ffn1_w_in': 'new_v', 'new_v_ffn1_w_out': 'new_v', 'new_v_mix_norm': 'new_v', 'new_v_ffn2_norm': 'new_v', 'new_v_ffn2_w_in': 'new_v', 'new_v_ffn2_w_out': 'new_v', 'new_v_a_w_qkv': 'new_v', 'new_v_a_rel_bias': 'new_v', 'new_v_a_w_o': 'new_v', 'new_v_kv_norm': 'new_v', 'new_v_kv_w_down': 'new_v', 'new_v_kv_latent_norm': 'new_v', 'new_v_kv_w_up': 'new_v', 'new_v_b_w_dq': 'new_v', 'new_v_b_q_norm': 'new_v', 'new_v_b_w_uq': 'new_v', 'new_v_b_w_o': 'new_v', 'new_v_final_norm': 'new_v'}


def _forward(args):
    return _fwd_reference(*[args[k] for k in FWD_PARAMS])


def _output_shape():
    out = _jax.eval_shape(lambda: _forward(_fwd_setup_inputs(0)))
    return out.shape, out.dtype

N_MICROBATCH = 1
ADAM_LR = 0.001
ADAM_B1 = 0.9
ADAM_B2 = 0.999
ADAM_EPS = 1e-08
ADAM_WD = 0.01
ADAM_STEP = 10
PER_EXAMPLE_BATCH_AXIS = {'x': 0, 'loss_target': 0}
SHARED_INPUTS = []
_WEIGHT_DTYPES = {'ffn1_norm': _jnp.float32, 'ffn1_w_in': _jnp.float32, 'ffn1_w_out': _jnp.float32, 'mix_norm': _jnp.float32, 'ffn2_norm': _jnp.float32, 'ffn2_w_in': _jnp.float32, 'ffn2_w_out': _jnp.float32, 'a_w_qkv': _jnp.float32, 'a_rel_bias': _jnp.float32, 'a_w_o': _jnp.float32, 'kv_norm': _jnp.float32, 'kv_w_down': _jnp.float32, 'kv_latent_norm': _jnp.float32, 'kv_w_up': _jnp.float32, 'b_w_dq': _jnp.float32, 'b_q_norm': _jnp.float32, 'b_w_uq': _jnp.float32, 'b_w_o': _jnp.float32, 'final_norm': _jnp.float32}
MOMENT_SCALE = {'ffn1_norm': 7.711781e-02, 'ffn1_w_in': 3.290293e-02, 'ffn1_w_out': 5.355782e-02, 'mix_norm': 4.047687e-02, 'ffn2_norm': 7.402668e-02, 'ffn2_w_in': 3.030647e-02, 'ffn2_w_out': 4.961762e-02, 'a_w_qkv': 3.057967e-02, 'a_rel_bias': 1.406168e-02, 'a_w_o': 3.362213e-02, 'kv_norm': 3.678256e-02, 'kv_w_down': 6.381335e-02, 'kv_latent_norm': 6.725588e-02, 'kv_w_up': 2.460422e-02, 'b_w_dq': 2.734734e-02, 'b_q_norm': 2.745782e-02, 'b_w_uq': 1.931339e-02, 'b_w_o': 2.842557e-02, 'final_norm': 3.198547e+01}


def _to_microbatches(a, axis):
    t = _jnp.moveaxis(a, axis, 0)
    t = t.reshape((N_MICROBATCH, t.shape[0] // N_MICROBATCH) + t.shape[1:])
    return _jnp.moveaxis(t, 1, axis + 1)


def setup_inputs(seed: int = 0) -> dict:
    inp = _fwd_setup_inputs(seed)
    key = _jax.random.fold_in(_jax.random.key(seed), 7919)
    shape, _ = _output_shape()
    out = dict(inp)
    out["loss_target"] = _jax.random.normal(_jax.random.fold_in(key, 0), shape, _jnp.float32)
    for i, name in enumerate(TWIN_WEIGHTS):
        w = inp[name].astype(_jnp.float32)
        if MOMENT_SCALE is None:
            s = _jnp.sqrt(_jnp.mean(_jnp.square(w)) + 1e-30)
        else:
            s = MOMENT_SCALE[name]
        km, kv = _jax.random.split(_jax.random.fold_in(key, i + 1))
        out[name] = w
        out["m_" + name] = s * _jax.random.normal(km, w.shape, _jnp.float32)
        out["v_" + name] = (s * s) * _jax.random.uniform(kv, w.shape, _jnp.float32, 0.5, 1.5)
    if N_MICROBATCH > 1:
        for name, axis in PER_EXAMPLE_BATCH_AXIS.items():
            out[name] = _to_microbatches(out[name], axis)
    return {'x': out['x'], 'ffn1_norm': out['ffn1_norm'], 'ffn1_w_in': out['ffn1_w_in'], 'ffn1_w_out': out['ffn1_w_out'], 'mix_norm': out['mix_norm'], 'ffn2_norm': out['ffn2_norm'], 'ffn2_w_in': out['ffn2_w_in'], 'ffn2_w_out': out['ffn2_w_out'], 'a_w_qkv': out['a_w_qkv'], 'a_rel_bias': out['a_rel_bias'], 'a_w_o': out['a_w_o'], 'kv_norm': out['kv_norm'], 'kv_w_down': out['kv_w_down'], 'kv_latent_norm': out['kv_latent_norm'], 'kv_w_up': out['kv_w_up'], 'b_w_dq': out['b_w_dq'], 'b_q_norm': out['b_q_norm'], 'b_w_uq': out['b_w_uq'], 'b_w_o': out['b_w_o'], 'final_norm': out['final_norm'], 'loss_target': out['loss_target'], 'm_ffn1_norm': out['m_ffn1_norm'], 'm_ffn1_w_in': out['m_ffn1_w_in'], 'm_ffn1_w_out': out['m_ffn1_w_out'], 'm_mix_norm': out['m_mix_norm'], 'm_ffn2_norm': out['m_ffn2_norm'], 'm_ffn2_w_in': out['m_ffn2_w_in'], 'm_ffn2_w_out': out['m_ffn2_w_out'], 'm_a_w_qkv': out['m_a_w_qkv'], 'm_a_rel_bias': out['m_a_rel_bias'], 'm_a_w_o': out['m_a_w_o'], 'm_kv_norm': out['m_kv_norm'], 'm_kv_w_down': out['m_kv_w_down'], 'm_kv_latent_norm': out['m_kv_latent_norm'], 'm_kv_w_up': out['m_kv_w_up'], 'm_b_w_dq': out['m_b_w_dq'], 'm_b_q_norm': out['m_b_q_norm'], 'm_b_w_uq': out['m_b_w_uq'], 'm_b_w_o': out['m_b_w_o'], 'm_final_norm': out['m_final_norm'], 'v_ffn1_norm': out['v_ffn1_norm'], 'v_ffn1_w_in': out['v_ffn1_w_in'], 'v_ffn1_w_out': out['v_ffn1_w_out'], 'v_mix_norm': out['v_mix_norm'], 'v_ffn2_norm': out['v_ffn2_norm'], 'v_ffn2_w_in': out['v_ffn2_w_in'], 'v_ffn2_w_out': out['v_ffn2_w_out'], 'v_a_w_qkv': out['v_a_w_qkv'], 'v_a_rel_bias': out['v_a_rel_bias'], 'v_a_w_o': out['v_a_w_o'], 'v_kv_norm': out['v_kv_norm'], 'v_kv_w_down': out['v_kv_w_down'], 'v_kv_latent_norm': out['v_kv_latent_norm'], 'v_kv_w_up': out['v_kv_w_up'], 'v_b_w_dq': out['v_b_w_dq'], 'v_b_q_norm': out['v_b_q_norm'], 'v_b_w_uq': out['v_b_w_uq'], 'v_b_w_o': out['v_b_w_o'], 'v_final_norm': out['v_final_norm']}


def _loss(weights, diff, rest, loss_target):
    with _jax.named_scope("forward"):
        args = {**rest, TWIN_DIFF_INPUT: diff, **{k: w.astype(_WEIGHT_DTYPES[k]) for k, w in weights.items()}}
        y = _forward(args)
    with _jax.named_scope("loss_head"):
        err = _jnp.square(y.astype(_jnp.float32) - loss_target)
        return 0.5 * _jnp.sum(_jnp.mean(err, axis=-1)) if err.ndim else 0.5 * err


def _adamw(w, g, m, v):
    m = ADAM_B1 * m + (1.0 - ADAM_B1) * g
    v = ADAM_B2 * v + (1.0 - ADAM_B2) * _jnp.square(g)
    m_hat = m / (1.0 - ADAM_B1 ** ADAM_STEP)
    v_hat = v / (1.0 - ADAM_B2 ** ADAM_STEP)
    delta = -ADAM_LR * (m_hat / (_jnp.sqrt(v_hat) + ADAM_EPS) + ADAM_WD * w)
    return delta, m, v


def reference(x, ffn1_norm, ffn1_w_in, ffn1_w_out, mix_norm, ffn2_norm, ffn2_w_in, ffn2_w_out, a_w_qkv, a_rel_bias, a_w_o, kv_norm, kv_w_down, kv_latent_norm, kv_w_up, b_w_dq, b_q_norm, b_w_uq, b_w_o, final_norm, loss_target, m_ffn1_norm, m_ffn1_w_in, m_ffn1_w_out, m_mix_norm, m_ffn2_norm, m_ffn2_w_in, m_ffn2_w_out, m_a_w_qkv, m_a_rel_bias, m_a_w_o, m_kv_norm, m_kv_w_down, m_kv_latent_norm, m_kv_w_up, m_b_w_dq, m_b_q_norm, m_b_w_uq, m_b_w_o, m_final_norm, v_ffn1_norm, v_ffn1_w_in, v_ffn1_w_out, v_mix_norm, v_ffn2_norm, v_ffn2_w_in, v_ffn2_w_out, v_a_w_qkv, v_a_rel_bias, v_a_w_o, v_kv_norm, v_kv_w_down, v_kv_latent_norm, v_kv_w_up, v_b_w_dq, v_b_q_norm, v_b_w_uq, v_b_w_o, v_final_norm):
    given = dict(x=x, ffn1_norm=ffn1_norm, ffn1_w_in=ffn1_w_in, ffn1_w_out=ffn1_w_out, mix_norm=mix_norm, ffn2_norm=ffn2_norm, ffn2_w_in=ffn2_w_in, ffn2_w_out=ffn2_w_out, a_w_qkv=a_w_qkv, a_rel_bias=a_rel_bias, a_w_o=a_w_o, kv_norm=kv_norm, kv_w_down=kv_w_down, kv_latent_norm=kv_latent_norm, kv_w_up=kv_w_up, b_w_dq=b_w_dq, b_q_norm=b_q_norm, b_w_uq=b_w_uq, b_w_o=b_w_o, final_norm=final_norm, loss_target=loss_target, m_ffn1_norm=m_ffn1_norm, m_ffn1_w_in=m_ffn1_w_in, m_ffn1_w_out=m_ffn1_w_out, m_mix_norm=m_mix_norm, m_ffn2_norm=m_ffn2_norm, m_ffn2_w_in=m_ffn2_w_in, m_ffn2_w_out=m_ffn2_w_out, m_a_w_qkv=m_a_w_qkv, m_a_rel_bias=m_a_rel_bias, m_a_w_o=m_a_w_o, m_kv_norm=m_kv_norm, m_kv_w_down=m_kv_w_down, m_kv_latent_norm=m_kv_latent_norm, m_kv_w_up=m_kv_w_up, m_b_w_dq=m_b_w_dq, m_b_q_norm=m_b_q_norm, m_b_w_uq=m_b_w_uq, m_b_w_o=m_b_w_o, m_final_norm=m_final_norm, v_ffn1_norm=v_ffn1_norm, v_ffn1_w_in=v_ffn1_w_in, v_ffn1_w_out=v_ffn1_w_out, v_mix_norm=v_mix_norm, v_ffn2_norm=v_ffn2_norm, v_ffn2_w_in=v_ffn2_w_in, v_ffn2_w_out=v_ffn2_w_out, v_a_w_qkv=v_a_w_qkv, v_a_rel_bias=v_a_rel_bias, v_a_w_o=v_a_w_o, v_kv_norm=v_kv_norm, v_kv_w_down=v_kv_w_down, v_kv_latent_norm=v_kv_latent_norm, v_kv_w_up=v_kv_w_up, v_b_w_dq=v_b_w_dq, v_b_q_norm=v_b_q_norm, v_b_w_uq=v_b_w_uq, v_b_w_o=v_b_w_o, v_final_norm=v_final_norm)
    weights = {n: given[n] for n in TWIN_WEIGHTS}
    shared = {n: given[n] for n in SHARED_INPUTS}
    per_example = {n: given[n] for n in ['x']}
    grad_fn = _jax.value_and_grad(_loss, argnums=(0, 1))

    def one_microbatch(ex, loss_target):
        ex = dict(ex)
        diff = ex.pop(TWIN_DIFF_INPUT)
        return grad_fn(weights, diff, {**shared, **ex}, loss_target)

    if N_MICROBATCH == 1:
        loss, (grad_w, grad_x) = one_microbatch(per_example, given["loss_target"])
    else:
        def body(carry, xs):
            loss_sum, grad_sum = carry
            l_k, (gw_k, gx_k) = one_microbatch(xs[0], xs[1])
            with _jax.named_scope("update"):
                return (loss_sum + l_k, _jax.tree.map(_jnp.add, grad_sum, gw_k)), gx_k

        init = (_jnp.zeros((), _jnp.float32), _jax.tree.map(_jnp.zeros_like, weights))
        (loss, grad_w), grad_x = _jax.lax.scan(body, init, (per_example, given["loss_target"]))
    with _jax.named_scope("update"):
        delta_w, new_m, new_v = {}, {}, {}
        for n in TWIN_WEIGHTS:
            delta_w[n], new_m[n], new_v[n] = _adamw(weights[n], grad_w[n], given["m_" + n], given["v_" + n])
    return (loss, grad_x, *[grad_w[n] for n in TWIN_WEIGHTS], *[delta_w[n] for n in TWIN_WEIGHTS],
            *[new_m[n] for n in TWIN_WEIGHTS], *[new_v[n] for n in TWIN_WEIGHTS])
```

```python
import functools

import jax
import jax.numpy as jnp
import numpy as np
from jax import lax
from jax.experimental import pallas as pl
from jax.experimental.pallas import tpu as pltpu

NDEV = 8
D_MODEL = 1024
D_FF = 2816
FB = 2 * D_FF // NDEV
NJ = D_FF // FB
CHUNK = 64
LEFT_CHUNKS = 8
PAD = LEFT_CHUNKS * CHUNK
BAND = PAD + CHUNK
MAX_REL = 128
NREL = 2 * MAX_REL + 1
NREL_PAD = 384
HEADS_A = 16
HEADS_B = 8
NOPE = 128
ROPE = 64
QK_B = NOPE + ROPE
V_DIM = 128
Q_LORA = 768
KV_LORA = 256
ROPE_THETA = 10000.0
EPS = 1e-6
NEG_INF = -1e30
MLA_TQ = 256
ADAM_LR = 0.001
ADAM_B1 = 0.9
ADAM_B2 = 0.999
ADAM_EPS = 1e-08
ADAM_WD = 0.01
ADAM_STEP = 10
PACK_ROWS = 16
VMEM_LIMIT_BYTES = 56 * 1024 * 1024

F32 = jnp.float32
BF16 = jnp.bfloat16
SDS = jax.ShapeDtypeStruct
BS = pl.BlockSpec
HIGHEST = lax.Precision.HIGHEST
MESH = pl.DeviceIdType.MESH


def _cparams(n_axes):
    return pltpu.CompilerParams(dimension_semantics=("arbitrary",) * n_axes, vmem_limit_bytes=VMEM_LIMIT_BYTES)


def _tile(t, want=512):
    return want if t % want == 0 else t


def _dot(a, b):
    return jnp.dot(a, b, preferred_element_type=F32)


def _dot_nt(a, b):
    return lax.dot_general(a, b, (((1,), (1,)), ((), ())), preferred_element_type=F32)


def _dot_tn(a, b):
    return lax.dot_general(a, b, (((0,), (0,)), ((), ())), preferred_element_type=F32)


def _dot_exact(a, b):
    return jnp.dot(a, b, precision=HIGHEST, preferred_element_type=F32)


def _rms_scale(h):
    return lax.rsqrt(jnp.mean(h * h, axis=-1, keepdims=True) + EPS)


def _acc_rows(ref, val, step, n_steps):
    part = val.reshape(val.shape[0] // 8, 8, val.shape[1]).sum(axis=0)

    @pl.when(step == 0)
    def _():
        ref[...] = part

    @pl.when(step > 0)
    def _():
        ref[...] += part

    @pl.when(step == n_steps - 1)
    def _():
        ref[...] = jnp.broadcast_to(jnp.sum(ref[...], axis=0, keepdims=True), ref.shape)


def _exchange(name, entries):
    ins, out_shapes, items = [], [], []
    for entry in entries:
        oi = len(out_shapes)
        if entry[0] == "gather":
            arr, layers = entry[1], entry[2]
            if layers is None:
                out_shapes.append(SDS((NDEV,) + arr.shape, arr.dtype))
                items.append((len(ins), oi, "gather", None))
            else:
                out_shapes.append(SDS((layers, NDEV) + arr.shape[1:], arr.dtype))
                items.extend((len(ins), oi, "gather", layer) for layer in range(layers))
            ins.append(arr)
        else:
            arrs = entry[1]
            out_shapes.append(SDS((NDEV, len(arrs)) + arrs[0].shape[1:], arrs[0].dtype))
            for layer, arr in enumerate(arrs):
                items.append((len(ins), oi, "scatter", layer))
                ins.append(arr)
    n_in, n_out, n_items = len(ins), len(out_shapes), len(items)

    def body(*refs):
        in_refs, out_refs = refs[:n_in], refs[n_in : n_in + n_out]
        send_sems, recv_sems, local_sems = refs[n_in + n_out :]
        x, y, c = lax.axis_index("x"), lax.axis_index("y"), lax.axis_index("c")
        me = 4 * x + 2 * y + c

        def flipped(p):
            px = 1 - x if p & 4 else x
            py = 1 - y if p & 2 else y
            pc = 1 - c if p & 1 else c
            return (px, py, pc), 4 * px + 2 * py + pc

        def ends(item, origin, target):
            ii, oi, kind, layer = item
            if kind == "gather":
                src = in_refs[ii] if layer is None else in_refs[ii].at[layer]
                dst = out_refs[oi].at[origin] if layer is None else out_refs[oi].at[layer, origin]
            else:
                src = in_refs[ii].at[target]
                dst = out_refs[oi].at[origin, layer]
            return src, dst

        local = []
        for k, item in enumerate(items):
            src, dst = ends(item, me, me)
            local.append(pltpu.make_async_copy(src, dst, local_sems.at[k]))
            local[-1].start()
        sends = []
        for p in range(1, NDEV):
            peer_pos, peer = flipped(p)
            for k, item in enumerate(items):
                src, dst = ends(item, me, peer)
                sends.append(
                    pltpu.make_async_remote_copy(
                        src_ref=src, dst_ref=dst, send_sem=send_sems.at[k, p - 1], recv_sem=recv_sems.at[k, p - 1],
                        device_id=peer_pos, device_id_type=MESH,
                    )
                )
                sends[-1].start()
        for p in range(1, NDEV):
            peer_pos, peer = flipped(p)
            for k, item in enumerate(items):
                src, dst = ends(item, peer, me)
                src_mine, _ = ends(item, me, peer)
                pltpu.make_async_remote_copy(
                    src_ref=src_mine, dst_ref=dst, send_sem=send_sems.at[k, p - 1], recv_sem=recv_sems.at[k, p - 1],
                    device_id=peer_pos, device_id_type=MESH,
                ).wait_recv()
        for cp in sends:
            cp.wait_send()
        for cp in local:
            cp.wait()

    any_spec = BS(memory_space=pl.ANY)
    return pl.pallas_call(
        body,
        name=name,
        out_shape=out_shapes,
        in_specs=[any_spec] * n_in,
        out_specs=[any_spec] * n_out,
        scratch_shapes=[
            pltpu.SemaphoreType.DMA((n_items, NDEV - 1)),
            pltpu.SemaphoreType.DMA((n_items, NDEV - 1)),
            pltpu.SemaphoreType.DMA((n_items,)),
        ],
    )(*ins)


def _norm_fwd(name, h, gammas):
    t, dn = h.shape
    ng = gammas.shape[0]
    tm = _tile(t)

    def body(h_ref, g_ref, *outs):
        hv = h_ref[...]
        hh = hv * _rms_scale(hv)
        for i, o_ref in enumerate(outs):
            o_ref[...] = (hh * g_ref[i : i + 1, :]).astype(BF16)

    row = BS((tm, dn), lambda i: (i, 0))
    return pl.pallas_call(
        body, name=name, grid=(t // tm,),
        in_specs=[row, BS((ng, dn), lambda i: (0, 0))],
        out_specs=[row] * ng, out_shape=[SDS((t, dn), BF16)] * ng,
        compiler_params=_cparams(1),
    )(h, gammas)


def _ffn_in(name, n, w_in, layer):
    t, dn = n.shape
    tm = _tile(t)

    def body(n_ref, wg_ref, wu_ref, gu_ref, a_ref):
        xv = n_ref[...]
        g = _dot(xv, wg_ref[...])
        u = _dot(xv, wu_ref[...])
        gu_ref[0] = g
        gu_ref[1] = u
        a_ref[...] = (g * jax.nn.sigmoid(g) * u).astype(BF16)

    return pl.pallas_call(
        body, name=name, grid=(NJ, t // tm),
        in_specs=[
            BS((tm, dn), lambda j, i: (i, 0)),
            BS((None, None, dn, FB), lambda j, i: (layer, j, 0, 0)),
            BS((None, None, dn, FB), lambda j, i: (layer, j + NJ, 0, 0)),
        ],
        out_specs=[BS((None, 2, tm, FB), lambda j, i: (j, 0, i, 0)), BS((None, tm, FB), lambda j, i: (j, i, 0))],
        out_shape=[SDS((NJ, 2, t, FB), F32), SDS((NJ, t, FB), BF16)],
        compiler_params=_cparams(2),
    )(n, w_in, w_in)


def _mm_res_norm(name, a, w, layer, h_in, gammas, scale):
    nk, t, kb = a.shape
    dn = w.shape[-1]
    ng = 0 if gammas is None else gammas.shape[0]
    tm = _tile(t)

    def body(*refs):
        a_ref, w_ref, h_ref = refs[:3]
        g_ref = refs[3] if ng else None
        outs = refs[3 + (1 if ng else 0) :]
        acc = _dot(a_ref[0], w_ref[0])
        for k in range(1, nk):
            acc += _dot(a_ref[k], w_ref[k])
        ho = h_ref[...] + scale * acc
        outs[0][...] = ho
        if ng:
            hh = ho * _rms_scale(ho)
            for i in range(ng):
                outs[1 + i][...] = (hh * g_ref[i : i + 1, :]).astype(BF16)

    row = BS((tm, dn), lambda i: (i, 0))
    in_specs = [BS((nk, tm, kb), lambda i: (0, i, 0)), BS((None, nk, kb, dn), lambda i: (layer, 0, 0, 0)), row]
    args = [a, w, h_in]
    if ng:
        in_specs.append(BS((ng, dn), lambda i: (0, 0)))
        args.append(gammas)
    return pl.pallas_call(
        body, name=name, grid=(t // tm,),
        in_specs=in_specs,
        out_specs=[row] * (1 + ng), out_shape=[SDS((t, dn), F32)] + [SDS((t, dn), BF16)] * ng,
        compiler_params=_cparams(1),
    )(*args)


def _qkv_proj(name, hn, w_qkv):
    t, dn = hn.shape
    wb = w_qkv.shape[-1]
    per = wb // 128
    tm = _tile(t)

    def body(x_ref, w_ref, o_ref):
        xv = x_ref[...]
        for j in range(NDEV):
            yv = _dot(xv, w_ref[j]).astype(BF16)
            for i in range(per):
                n = per * j + i
                o_ref[n // 8, :, (n % 8) * 128 : (n % 8 + 1) * 128] = yv[:, i * 128 : (i + 1) * 128]

    return pl.pallas_call(
        body, name=name, grid=(t // tm,),
        in_specs=[BS((tm, dn), lambda i: (i, 0)), BS((NDEV, dn, wb), lambda i: (0, 0, 0))],
        out_specs=BS((3, tm, dn), lambda i: (0, i, 0)), out_shape=SDS((3, t, dn), BF16),
        compiler_params=_cparams(1),
    )(hn, w_qkv)


def _rel_onehot(i):
    r = lax.broadcasted_iota(jnp.int32, (NREL_PAD, BAND), 0)
    j = lax.broadcasted_iota(jnp.int32, (NREL_PAD, BAND), 1)
    idx = jnp.clip(PAD + i - j, -MAX_REL, MAX_REL) + MAX_REL
    return (idx == r).astype(F32)


def _rel_bias_fwd(table):
    def body(t_ref, o_ref):
        i8 = pl.program_id(0)
        for ii in range(8):
            o_ref[:, ii, :] = _dot_exact(t_ref[...], _rel_onehot(i8 * 8 + ii))

    return pl.pallas_call(
        body, name="rel_bias_fwd", grid=(CHUNK // 8,),
        in_specs=[BS((HEADS_A, NREL_PAD), lambda i: (0, 0))],
        out_specs=BS((HEADS_A, 8, BAND), lambda i: (0, i, 0)), out_shape=SDS((HEADS_A, CHUNK, BAND), F32),
        compiler_params=_cparams(1),
    )(table)


def _rel_bias_bwd(dbias):
    def body(d_ref, o_ref):
        i8 = pl.program_id(0)
        acc = jnp.zeros((HEADS_A, NREL_PAD), F32)
        for ii in range(8):
            acc += lax.dot_general(
                d_ref[:, ii, :], _rel_onehot(i8 * 8 + ii), (((1,), (1,)), ((), ())), precision=HIGHEST,
                preferred_element_type=F32,
            )

        @pl.when(i8 == 0)
        def _():
            o_ref[...] = acc

        @pl.when(i8 > 0)
        def _():
            o_ref[...] += acc

    return pl.pallas_call(
        body, name="rel_bias_bwd", grid=(CHUNK // 8,),
        in_specs=[BS((HEADS_A, 8, BAND), lambda i: (0, i, 0))],
        out_specs=BS((HEADS_A, NREL_PAD), lambda i: (0, 0)), out_shape=SDS((HEADS_A, NREL_PAD), F32),
        compiler_params=_cparams(1),
    )(dbias)


def _chunk_scores(qm, kb, bias, valid):
    s = _dot_nt(qm, kb) * (CHUNK ** -0.5) + bias
    s = jnp.where(valid, s, NEG_INF)
    e = jnp.exp(s - jnp.max(s, axis=-1, keepdims=True))
    return e / jnp.sum(e, axis=-1, keepdims=True)


def _attn_a_fwd(qkv3, bias, bl, seq):
    t, dn = qkv3.shape[1:]
    npair = dn // 128
    nchunk = seq // CHUNK

    def body(q_ref, k_ref, v_ref, b_ref, o_ref, kpad, vpad):
        kpad[0:PAD, :] = jnp.zeros((PAD, 128), BF16)
        vpad[0:PAD, :] = jnp.zeros((PAD, 128), BF16)
        kpad[PAD:, :] = k_ref[...]
        vpad[PAD:, :] = v_ref[...]
        lane = lax.broadcasted_iota(jnp.int32, (CHUNK, 128), 1)
        col = lax.broadcasted_iota(jnp.int32, (CHUNK, BAND), 1)

        def chunk(cidx, carry):
            r0 = pl.multiple_of(cidx * CHUNK, CHUNK)
            qc = q_ref[pl.ds(r0, CHUNK), :]
            kb = kpad[pl.ds(r0, BAND), :]
            vb = vpad[pl.ds(r0, BAND), :]
            valid = col >= PAD - cidx * CHUNK
            outs = []
            for hh in range(2):
                in_head = (lane < 64) if hh == 0 else (lane >= 64)
                qm = jnp.where(in_head, qc, jnp.zeros_like(qc))
                p = _chunk_scores(qm, kb, b_ref[hh], valid)
                outs.append(_dot(p.astype(BF16), vb))
            o_ref[pl.ds(r0, CHUNK), :] = jnp.where(lane < 64, outs[0], outs[1]).astype(BF16)
            return carry

        lax.fori_loop(0, nchunk, chunk, 0)

    return pl.pallas_call(
        body, name="attn_a_fwd", grid=(bl, npair),
        in_specs=[
            BS((None, seq, 128), lambda b, h: (0, b, h)),
            BS((None, seq, 128), lambda b, h: (1, b, h)),
            BS((None, seq, 128), lambda b, h: (2, b, h)),
            BS((2, CHUNK, BAND), lambda b, h: (h, 0, 0)),
        ],
        out_specs=BS((seq, 128), lambda b, h: (b, h)), out_shape=SDS((t, dn), BF16),
        scratch_shapes=[pltpu.VMEM((PAD + seq, 128), BF16), pltpu.VMEM((PAD + seq, 128), BF16)],
        compiler_params=_cparams(2),
    )(qkv3, qkv3, qkv3, bias)


def _attn_a_bwd(qkv3, do, bias, bl, seq):
    t, dn = qkv3.shape[1:]
    npair = dn // 128
    nchunk = seq // CHUNK

    def body(q_ref, k_ref, v_ref, do_ref, b_ref, dqkv_ref, db_ref, kpad, vpad, dkacc, dvacc):
        b = pl.program_id(1)
        kpad[0:PAD, :] = jnp.zeros((PAD, 128), BF16)
        vpad[0:PAD, :] = jnp.zeros((PAD, 128), BF16)
        kpad[PAD:, :] = k_ref[...]
        vpad[PAD:, :] = v_ref[...]
        dkacc[...] = jnp.zeros_like(dkacc)
        dvacc[...] = jnp.zeros_like(dvacc)

        @pl.when(b == 0)
        def _():
            db_ref[...] = jnp.zeros_like(db_ref)

        lane = lax.broadcasted_iota(jnp.int32, (CHUNK, 128), 1)
        col = lax.broadcasted_iota(jnp.int32, (CHUNK, BAND), 1)

        def chunk(cidx, carry):
            r0 = pl.multiple_of(cidx * CHUNK, CHUNK)
            qc = q_ref[pl.ds(r0, CHUNK), :]
            doc = do_ref[pl.ds(r0, CHUNK), :]
            kb = kpad[pl.ds(r0, BAND), :]
            vb = vpad[pl.ds(r0, BAND), :]
            valid = col >= PAD - cidx * CHUNK
            dqs = []
            dk_band = jnp.zeros((BAND, 128), F32)
            dv_band = jnp.zeros((BAND, 128), F32)
            for hh in range(2):
                in_head = (lane < 64) if hh == 0 else (lane >= 64)
                qm = jnp.where(in_head, qc, jnp.zeros_like(qc))
                dom = jnp.where(in_head, doc, jnp.zeros_like(doc))
                p = _chunk_scores(qm, kb, b_ref[hh], valid)
                dp = _dot_nt(dom, vb)
                ds = p * (dp - jnp.sum(p * dp, axis=-1, keepdims=True))
                db_ref[hh] += ds
                dsb = (ds * (CHUNK ** -0.5)).astype(BF16)
                dqs.append(_dot(dsb, kb))
                dk_band += _dot_tn(dsb, qm)
                dv_band += _dot_tn(p.astype(BF16), dom)
            dqkv_ref[0, pl.ds(r0, CHUNK), :] = jnp.where(lane < 64, dqs[0], dqs[1]).astype(BF16)
            dkacc[pl.ds(r0, BAND), :] += dk_band
            dvacc[pl.ds(r0, BAND), :] += dv_band
            return carry

        lax.fori_loop(0, nchunk, chunk, 0)
        dqkv_ref[1] = dkacc[PAD:, :].astype(BF16)
        dqkv_ref[2] = dvacc[PAD:, :].astype(BF16)

    return pl.pallas_call(
        body, name="attn_a_bwd", grid=(npair, bl),
        in_specs=[
            BS((None, seq, 128), lambda h, b: (0, b, h)),
            BS((None, seq, 128), lambda h, b: (1, b, h)),
            BS((None, seq, 128), lambda h, b: (2, b, h)),
            BS((seq, 128), lambda h, b: (b, h)),
            BS((2, CHUNK, BAND), lambda h, b: (h, 0, 0)),
        ],
        out_specs=[BS((3, seq, 128), lambda h, b: (0, b, h)), BS((2, CHUNK, BAND), lambda h, b: (h, 0, 0))],
        out_shape=[SDS((3, t, dn), BF16), SDS((HEADS_A, CHUNK, BAND), F32)],
        scratch_shapes=[
            pltpu.VMEM((PAD + seq, 128), BF16), pltpu.VMEM((PAD + seq, 128), BF16),
            pltpu.VMEM((PAD + seq, 128), F32), pltpu.VMEM((PAD + seq, 128), F32),
        ],
        compiler_params=_cparams(2),
    )(qkv3, qkv3, qkv3, do, bias)


def _rope_tables(seq):
    half = ROPE // 2
    freqs = ROPE_THETA ** (-jnp.arange(half, dtype=F32) / half)
    ang = jnp.arange(seq, dtype=F32)[:, None] * freqs[None, :]
    cos, sin = jnp.cos(ang), jnp.sin(ang)
    c64 = jnp.concatenate([cos, cos], axis=1)
    s64 = jnp.concatenate([-sin, sin], axis=1)
    c192 = jnp.concatenate([jnp.ones((seq, NOPE), F32), c64], axis=1)
    s192 = jnp.concatenate([jnp.zeros((seq, NOPE), F32), s64], axis=1)
    p64 = np.zeros((ROPE, ROPE), np.float32)
    for col in range(ROPE):
        p64[(col + half) % ROPE, col] = 1.0
    p192 = np.zeros((QK_B, QK_B), np.float32)
    p192[NOPE:, NOPE:] = p64
    return c64, s64, jnp.asarray(p64), c192, s192, jnp.asarray(p192)


def _rope(xv, cos, sin_signed, swap):
    return xv * cos + _dot_exact(xv, swap) * sin_signed


def _rope_bwd(dy, cos, sin_signed, swap):
    return dy * cos + _dot_exact(dy * sin_signed, swap)


def _q_down(hn, w_dq, q_norm):
    t, dn = hn.shape
    ql = w_dq.shape[1]
    tm = _tile(t)

    def body(x_ref, w_ref, g_ref, pre_ref, cq_ref):
        pre = _dot(x_ref[...], w_ref[...])
        pre_ref[...] = pre
        cq_ref[...] = (pre * _rms_scale(pre) * g_ref[...]).astype(BF16)

    return pl.pallas_call(
        body, name="q_down", grid=(t // tm,),
        in_specs=[BS((tm, dn), lambda i: (i, 0)), BS((dn, ql), lambda i: (0, 0)), BS((1, ql), lambda i: (0, 0))],
        out_specs=[BS((tm, ql), lambda i: (i, 0))] * 2, out_shape=[SDS((t, ql), F32), SDS((t, ql), BF16)],
        compiler_params=_cparams(1),
    )(hn, w_dq, q_norm)


def _q_up(cq, w_uq, c192, s192, p192, seq):
    t, ql = cq.shape
    tm = _tile(min(seq, 512), min(seq, 512))
    nseq = seq // tm

    def body(x_ref, w_ref, c_ref, s_ref, p_ref, o_ref):
        qf = _dot(x_ref[...], w_ref[...])
        o_ref[...] = _rope(qf, c_ref[...], s_ref[...], p_ref[...]).astype(BF16)

    pos = BS((tm, QK_B), lambda h, i: (i % nseq, 0))
    return pl.pallas_call(
        body, name="q_up", grid=(HEADS_B, t // tm),
        in_specs=[
            BS((tm, ql), lambda h, i: (i, 0)), BS((None, ql, QK_B), lambda h, i: (h, 0, 0)), pos, pos,
            BS((QK_B, QK_B), lambda h, i: (0, 0)),
        ],
        out_specs=BS((None, tm, QK_B), lambda h, i: (h, i, 0)), out_shape=SDS((HEADS_B, t, QK_B), BF16),
        compiler_params=_cparams(2),
    )(cq, w_uq, c192, s192, p192)


def _kv_down(hk, w_down, latent_norm, c64, s64, p64, seq):
    t, dn = hk.shape
    wd = w_down.shape[1]
    tm = _tile(min(seq, 512), min(seq, 512))
    nseq = seq // tm

    def body(x_ref, w_ref, g_ref, c_ref, s_ref, p_ref, ckr_ref, ckv_ref, kr_ref):
        ckr = _dot(x_ref[...], w_ref[...])
        ckr_ref[...] = ckr
        lat = ckr[:, :KV_LORA]
        ckv_ref[...] = (lat * _rms_scale(lat) * g_ref[...]).astype(BF16)
        kr_ref[...] = _rope(ckr[:, KV_LORA:], c_ref[...], s_ref[...], p_ref[...]).astype(BF16)

    pos = BS((tm, ROPE), lambda i: (i % nseq, 0))
    return pl.pallas_call(
        body, name="kv_down", grid=(t // tm,),
        in_specs=[
            BS((tm, dn), lambda i: (i, 0)), BS((dn, wd), lambda i: (0, 0)), BS((1, KV_LORA), lambda i: (0, 0)), pos, pos,
            BS((ROPE, ROPE), lambda i: (0, 0)),
        ],
        out_specs=[BS((tm, wd), lambda i: (i, 0)), BS((tm, KV_LORA), lambda i: (i, 0)), BS((tm, ROPE), lambda i: (i, 0))],
        out_shape=[SDS((t, wd), F32), SDS((t, KV_LORA), BF16), SDS((t, ROPE), BF16)],
        compiler_params=_cparams(1),
    )(hk, w_down, latent_norm, c64, s64, p64)


def _kv_up(ckv, w_up):
    t, kl = ckv.shape
    hb = w_up.shape[-1]
    tm = _tile(t)

    def body(x_ref, w_ref, o_ref):
        o_ref[...] = _dot(x_ref[...], w_ref[...]).astype(BF16)

    return pl.pallas_call(
        body, name="kv_up", grid=(HEADS_B, t // tm),
        in_specs=[BS((tm, kl), lambda h, i: (i, 0)), BS((None, kl, hb), lambda h, i: (h, 0, 0))],
        out_specs=BS((tm, hb), lambda h, i: (i, h)), out_shape=SDS((t, HEADS_B * hb), BF16),
        compiler_params=_cparams(2),
    )(ckv, w_up)


def _mla_probs(qi, kcat, row0, n_keys):
    s = _dot_nt(qi, kcat) * (QK_B ** -0.5)
    rows = lax.broadcasted_iota(jnp.int32, (qi.shape[0], n_keys), 0) + row0
    cols = lax.broadcasted_iota(jnp.int32, (qi.shape[0], n_keys), 1)
    s = jnp.where(jnp.right_shift(cols, 6) <= jnp.right_shift(rows, 6), s, NEG_INF)
    e = jnp.exp(s - jnp.max(s, axis=-1, keepdims=True))
    return e / jnp.sum(e, axis=-1, keepdims=True)


def _mla_fwd(q, kv, kr, bl, seq):
    t = kv.shape[0]
    tq = min(MLA_TQ, seq)

    def body(q_ref, kn_ref, v_ref, kr_ref, o_ref):
        kcat = jnp.concatenate([kn_ref[...], kr_ref[...]], axis=1)
        vv = v_ref[...]
        for i in range(seq // tq):
            n_keys = (i + 1) * tq
            p = _mla_probs(q_ref[i * tq : (i + 1) * tq, :], kcat[:n_keys], i * tq, n_keys)
            o_ref[i * tq : (i + 1) * tq, :] = _dot(p.astype(BF16), vv[:n_keys]).astype(BF16)

    return pl.pallas_call(
        body, name="mla_fwd", grid=(bl, HEADS_B),
        in_specs=[
            BS((None, seq, QK_B), lambda b, h: (h, b, 0)),
            BS((seq, NOPE), lambda b, h: (b, 2 * h)),
            BS((seq, V_DIM), lambda b, h: (b, 2 * h + 1)),
            BS((seq, ROPE), lambda b, h: (b, 0)),
        ],
        out_specs=BS((seq, V_DIM), lambda b, h: (b, h)), out_shape=SDS((t, HEADS_B * V_DIM), BF16),
        compiler_params=_cparams(2),
    )(q, kv, kv, kr)


def _mla_bwd(q, kv, kr, do, c192, s192, p192, bl, seq):
    t = kv.shape[0]
    tq = min(MLA_TQ, seq)

    def body(q_ref, kn_ref, v_ref, kr_ref, do_ref, c_ref, s_ref, p_ref, dq_ref, dkv_ref, dkr_ref, dkacc, dvacc):
        h = pl.program_id(1)
        kcat = jnp.concatenate([kn_ref[...], kr_ref[...]], axis=1)
        vv = v_ref[...]
        dkacc[...] = jnp.zeros_like(dkacc)
        dvacc[...] = jnp.zeros_like(dvacc)
        for i in range(seq // tq):
            n_keys = (i + 1) * tq
            rows = slice(i * tq, (i + 1) * tq)
            qi = q_ref[rows, :]
            doi = do_ref[rows, :]
            p = _mla_probs(qi, kcat[:n_keys], i * tq, n_keys)
            dp = _dot_nt(doi, vv[:n_keys])
            ds = p * (dp - jnp.sum(p * dp, axis=-1, keepdims=True))
            dsb = (ds * (QK_B ** -0.5)).astype(BF16)
            dq = _dot(dsb, kcat[:n_keys])
            dq_ref[rows, :] = _rope_bwd(dq, c_ref[rows, :], s_ref[rows, :], p_ref[...]).astype(BF16)
            dkacc[0:n_keys, :] += _dot_tn(dsb, qi)
            dvacc[0:n_keys, :] += _dot_tn(p.astype(BF16), doi)
        dk = dkacc[...]
        dkv_ref[:, :NOPE] = dk[:, :NOPE].astype(BF16)
        dkv_ref[:, NOPE:] = dvacc[...].astype(BF16)

        @pl.when(h == 0)
        def _():
            dkr_ref[...] = dk[:, NOPE:]

        @pl.when(h > 0)
        def _():
            dkr_ref[...] += dk[:, NOPE:]

    return pl.pallas_call(
        body, name="mla_bwd", grid=(bl, HEADS_B),
        in_specs=[
            BS((None, seq, QK_B), lambda b, h: (h, b, 0)),
            BS((seq, NOPE), lambda b, h: (b, 2 * h)),
            BS((seq, V_DIM), lambda b, h: (b, 2 * h + 1)),
            BS((seq, ROPE), lambda b, h: (b, 0)),
            BS((seq, V_DIM), lambda b, h: (b, h)),
            BS((seq, QK_B), lambda b, h: (0, 0)),
            BS((seq, QK_B), lambda b, h: (0, 0)),
            BS((QK_B, QK_B), lambda b, h: (0, 0)),
        ],
        out_specs=[
            BS((None, seq, QK_B), lambda b, h: (h, b, 0)),
            BS((seq, NOPE + V_DIM), lambda b, h: (b, h)),
            BS((seq, ROPE), lambda b, h: (b, 0)),
        ],
        out_shape=[SDS((HEADS_B, t, QK_B), BF16), SDS((t, HEADS_B * (NOPE + V_DIM)), BF16), SDS((t, ROPE), F32)],
        scratch_shapes=[pltpu.VMEM((seq, QK_B), F32), pltpu.VMEM((seq, V_DIM), F32)],
        compiler_params=_cparams(2),
    )(q, kv, kv, kr, do, c192, s192, p192)


def _loss_final(h, target, gamma):
    t, dn = h.shape
    tm = _tile(t)
    nt = t // tm

    def body(h_ref, t_ref, g_ref, dh_ref, dg_ref, loss_ref):
        i = pl.program_id(0)
        hv = h_ref[...]
        r = _rms_scale(hv)
        hh = hv * r
        gam = g_ref[...]
        err = hh * gam - t_ref[...]
        part = 0.5 * jnp.sum(jnp.mean(err * err, axis=-1, keepdims=True))

        @pl.when(i == 0)
        def _():
            loss_ref[...] = jnp.zeros_like(loss_ref)

        loss_ref[...] += part
        dy = err * (1.0 / dn)
        _acc_rows(dg_ref, dy * hh, i, nt)
        t1 = dy * gam
        dh_ref[...] = r * (t1 - hh * jnp.mean(t1 * hh, axis=-1, keepdims=True))

    row = BS((tm, dn), lambda i: (i, 0))
    return pl.pallas_call(
        body, name="loss_final", grid=(nt,),
        in_specs=[row, row, BS((1, dn), lambda i: (0, 0))],
        out_specs=[row, BS((8, dn), lambda i: (0, 0)), BS((8, 128), lambda i: (0, 0))],
        out_shape=[SDS((t, dn), F32), SDS((8, dn), F32), SDS((8, 128), F32)],
        compiler_params=_cparams(1),
    )(h, target, gamma)


def _ffn_bwd_in(name, dh, w_out, layer, gu):
    t, dn = dh.shape
    tm = _tile(t)

    def body(dh_ref, w_ref, gu_ref, o_ref):
        xv = (0.5 * dh_ref[...]).astype(BF16)
        da = _dot_nt(xv, w_ref[...])
        g = gu_ref[0]
        u = gu_ref[1]
        sg = jax.nn.sigmoid(g)
        o_ref[0] = (da * u * (sg * (1.0 + g * (1.0 - sg)))).astype(BF16)
        o_ref[1] = (da * (g * sg)).astype(BF16)

    blk = BS((None, 2, tm, FB), lambda j, i: (j, 0, i, 0))
    return pl.pallas_call(
        body, name=name, grid=(NJ, t // tm),
        in_specs=[BS((tm, dn), lambda j, i: (i, 0)), BS((None, None, FB, dn), lambda j, i: (layer, j, 0, 0)), blk],
        out_specs=blk, out_shape=SDS((NJ, 2, t, FB), BF16),
        compiler_params=_cparams(2),
    )(dh, w_out, gu)


def _mm_nt_plain(name, xf, w):
    t, dn = xf.shape
    n = w.shape[0]
    tm = _tile(t)

    def body(x_ref, w_ref, o_ref):
        o_ref[...] = _dot_nt(x_ref[...].astype(BF16), w_ref[...]).astype(BF16)

    return pl.pallas_call(
        body, name=name, grid=(t // tm,),
        in_specs=[BS((tm, dn), lambda i: (i, 0)), BS((n, dn), lambda i: (0, 0))],
        out_specs=BS((tm, n), lambda i: (i, 0)), out_shape=SDS((t, n), BF16),
        compiler_params=_cparams(1),
    )(xf, w)


def _mm_tn(name, xa, x_spec, ya, y_spec, out_shape, out_spec, nj, nt, y_scale=None):
    def body(x_ref, y_ref, o_ref):
        i = pl.program_id(1)
        yv = y_ref[...]
        if yv.dtype != BF16:
            yv = (yv if y_scale is None else y_scale * yv).astype(BF16)
        part = _dot_tn(x_ref[...], yv)

        @pl.when(i == 0)
        def _():
            o_ref[...] = part

        @pl.when(i > 0)
        def _():
            o_ref[...] += part

    return pl.pallas_call(
        body, name=name, grid=(nj, nt),
        in_specs=[x_spec, y_spec], out_specs=out_spec, out_shape=out_shape,
        compiler_params=_cparams(2),
    )(xa, ya)


def _dw_qkv(hn, dqkv3, wb):
    t, dn = hn.shape
    per = wb // 128
    tm = _tile(t)

    def body(x_ref, y_ref, o_ref):
        i = pl.program_id(0)
        xv = x_ref[...]
        for j in range(NDEV):
            cols = [y_ref[(per * j + k) // 8, :, ((per * j + k) % 8) * 128 : ((per * j + k) % 8 + 1) * 128] for k in range(per)]
            part = _dot_tn(xv, jnp.concatenate(cols, axis=1))

            @pl.when(i == 0)
            def _():
                o_ref[j] = part

            @pl.when(i > 0)
            def _():
                o_ref[j] += part

    return pl.pallas_call(
        body, name="dw_qkv", grid=(t // tm,),
        in_specs=[BS((tm, dn), lambda i: (i, 0)), BS((3, tm, dn), lambda i: (0, i, 0))],
        out_specs=BS((NDEV, dn, wb), lambda i: (0, 0, 0)), out_shape=SDS((NDEV, dn, wb), F32),
        compiler_params=_cparams(1),
    )(hn, dqkv3)


def _mm_nt_epi(name, ya, y_spec, wa, w_spec, nj, n_out, extra, out_shapes, out_specs, epilogue, tm, nt, mm_fn=None):
    n_extra = len(extra)
    n_outs = len(out_shapes)

    def body(*refs):
        y_ref, w_ref = refs[:2]
        ex = refs[2 : 2 + n_extra]
        outs = refs[2 + n_extra : 2 + n_extra + n_outs]
        i = pl.program_id(0)
        j = pl.program_id(1)
        part = _dot_nt(y_ref[...], w_ref[...]) if mm_fn is None else mm_fn(y_ref, w_ref)
        if nj == 1:
            epilogue(part, ex, outs, i, nt)
            return
        acc = refs[-1]

        @pl.when(j == 0)
        def _():
            acc[...] = part

        @pl.when(j > 0)
        def _():
            acc[...] += part

        @pl.when(j == nj - 1)
        def _():
            epilogue(acc[...], ex, outs, i, nt)

    return pl.pallas_call(
        body, name=name, grid=(nt, nj),
        in_specs=[y_spec, w_spec] + [spec for _, spec in extra],
        out_specs=out_specs, out_shape=out_shapes,
        scratch_shapes=[] if nj == 1 else [pltpu.VMEM((tm, n_out), F32)],
        compiler_params=_cparams(2),
    )(ya, wa, *[arr for arr, _ in extra])


def _norm_bwd(dn, hv, gam):
    r = _rms_scale(hv)
    hh = hv * r
    t1 = dn * gam
    return r * (t1 - hh * jnp.mean(t1 * hh, axis=-1, keepdims=True)), dn * hh


def _norm_bwd_epilogue(has_res, out_dtype):
    def epilogue(dn, ex, outs, i, nt):
        dh, dg_rows = _norm_bwd(dn, ex[0][...], ex[1][...])
        _acc_rows(outs[1], dg_rows, i, nt)
        if has_res:
            dh = dh + ex[2][...]
        outs[0][...] = dh.astype(out_dtype)

    return epilogue


def _mm_nt_norm_bwd(name, ya, y_spec, wa, w_spec, nj, h, gamma, res, out_dtype, mm_fn=None, want_tm=512):
    t, n = h.shape
    tm = _tile(t, want_tm)
    nt = t // tm
    row = BS((tm, n), lambda i, j: (i, 0))
    extra = [(h, row), (gamma, BS((1, n), lambda i, j: (0, 0)))]
    if res is not None:
        extra.append((res, row))
    return _mm_nt_epi(
        name, ya, y_spec, wa, w_spec, nj, n, extra, [SDS((t, n), out_dtype), SDS((8, n), F32)],
        [row, BS((8, n), lambda i, j: (0, 0))], _norm_bwd_epilogue(res is not None, out_dtype), tm, nt, mm_fn,
    )


def _dev_block(jj):
    return jj // 2 + NJ * (jj % 2)


def _ffn_bwd(tag, layer, dh, n_in, h_in, gamma, gu, a, w_in, w_out):
    t, dn = dh.shape
    tm = _tile(t)
    nt = t // tm
    dgu = _ffn_bwd_in(f"{tag}_bwd_in", dh, w_out, layer, gu).reshape(2 * NJ, t, FB)
    dw_out = _mm_tn(
        f"{tag}_dw_out", a, BS((None, tm, FB), lambda j, i: (j, i, 0)), dh, BS((tm, dn), lambda j, i: (i, 0)),
        SDS((NJ, FB, dn), F32), BS((None, FB, dn), lambda j, i: (j, 0, 0)), NJ, nt, y_scale=0.5,
    )
    dw_in = _mm_tn(
        f"{tag}_dw_in", n_in, BS((tm, dn), lambda j, i: (i, 0)), dgu, BS((None, tm, FB), lambda j, i: (j, i, 0)),
        SDS((NDEV, dn, FB), F32), BS((None, dn, FB), lambda j, i: (_dev_block(j), 0, 0)), NDEV, nt,
    )
    dh_in, dgam = _mm_nt_norm_bwd(
        f"{tag}_dn", dgu, BS((None, tm, FB), lambda i, j: (j, i, 0)),
        w_in, BS((None, None, dn, FB), lambda i, j: (layer, _dev_block(j), 0, 0)), NDEV, h_in, gamma, dh, F32,
    )
    return dh_in, dgam, dw_in, dw_out


def _dqkv_mm(per):
    def mm(y_ref, w_ref):
        acc = None
        for j in range(NDEV):
            cols = [y_ref[(per * j + k) // 8, :, ((per * j + k) % 8) * 128 : ((per * j + k) % 8 + 1) * 128] for k in range(per)]
            part = _dot_nt(jnp.concatenate(cols, axis=1), w_ref[j])
            acc = part if acc is None else acc + part
        return acc

    return mm


def _kv_latent_bwd(dkv, w_up, ckr, latent_norm, dkr, c64, s64, p64, seq):
    t, wd = ckr.shape
    hb = w_up.shape[-1]
    tm = _tile(min(seq, 512), min(seq, 512))
    nt = t // tm
    nseq = seq // tm

    def epilogue(dn, ex, outs, i, nt_):
        dlat, dg_rows = _norm_bwd(dn, ex[0][...], ex[1][...])
        _acc_rows(outs[1], dg_rows, i, nt_)
        outs[0][:, :KV_LORA] = dlat.astype(BF16)
        outs[0][:, KV_LORA:] = _rope_bwd(ex[2][...], ex[3][...], ex[4][...], ex[5][...]).astype(BF16)

    pos = BS((tm, ROPE), lambda i, j: (i % nseq, 0))
    extra = [
        (ckr, BS((tm, KV_LORA), lambda i, j: (i, 0))), (latent_norm, BS((1, KV_LORA), lambda i, j: (0, 0))),
        (dkr, BS((tm, ROPE), lambda i, j: (i, 0))), (c64, pos), (s64, pos), (p64, BS((ROPE, ROPE), lambda i, j: (0, 0))),
    ]
    return _mm_nt_epi(
        "kv_latent_bwd", dkv, BS((tm, hb), lambda i, j: (i, j)), w_up, BS((None, KV_LORA, hb), lambda i, j: (j, 0, 0)),
        HEADS_B, KV_LORA, extra, [SDS((t, wd), BF16), SDS((8, KV_LORA), F32)],
        [BS((tm, wd), lambda i, j: (i, 0)), BS((8, KV_LORA), lambda i, j: (0, 0))], epilogue, tm, nt,
    )


def _adamw(name, parts, w, m, v):
    rows, cols = w.shape
    tr = max(d for d in range(8, min(rows, 256) + 1, 8) if rows % d == 0)

    def body(p_ref, w_ref, m_ref, v_ref, g_ref, d_ref, nm_ref, nv_ref):
        g = p_ref[0]
        for k in range(1, NDEV):
            g = g + p_ref[k]
        g_ref[...] = g
        nm = ADAM_B1 * m_ref[...] + (1.0 - ADAM_B1) * g
        nv = ADAM_B2 * v_ref[...] + (1.0 - ADAM_B2) * (g * g)
        nm_ref[...] = nm
        nv_ref[...] = nv
        m_hat = nm / (1.0 - ADAM_B1 ** ADAM_STEP)
        v_hat = nv / (1.0 - ADAM_B2 ** ADAM_STEP)
        d_ref[...] = -ADAM_LR * (m_hat / (jnp.sqrt(v_hat) + ADAM_EPS) + ADAM_WD * w_ref[...])

    row = BS((tr, cols), lambda i: (i, 0))
    return pl.pallas_call(
        body, name=name, grid=(rows // tr,),
        in_specs=[BS((NDEV, tr, cols), lambda i: (0, i, 0)), row, row, row],
        out_specs=[row] * 4, out_shape=[SDS((rows, cols), F32)] * 4,
        compiler_params=_cparams(1),
    )(parts, w, m, v)


def _pack_small(ffn1_norm, mix_norm, ffn2_norm, kv_norm, final_norm, q_norm, latent_norm, rel_bias, last_row):
    dn = ffn1_norm.shape[-1]

    def rows_of(a, n_rows):
        flat = a.reshape(-1)
        return jnp.pad(flat, (0, n_rows * dn - flat.shape[0])).reshape(n_rows, dn)

    return jnp.concatenate(
        [
            ffn1_norm.reshape(2, dn), mix_norm.reshape(2, dn), ffn2_norm.reshape(2, dn), kv_norm.reshape(1, dn),
            final_norm.reshape(1, dn), rows_of(q_norm, 1), rows_of(latent_norm, 1), rows_of(rel_bias, 5), rows_of(last_row, 1),
        ],
        axis=0,
    )


def _unpack_small(pack):
    dn = pack.shape[-1]
    return dict(
        ffn1_norm=pack[0:2], mix_norm=pack[2:4], ffn2_norm=pack[4:6], kv_norm=pack[6], final_norm=pack[7],
        b_q_norm=pack[8, :Q_LORA].reshape(1, Q_LORA), kv_latent_norm=pack[9, :KV_LORA],
        a_rel_bias=pack[10:15].reshape(-1)[: HEADS_A * NREL].reshape(1, HEADS_A, NREL), last=pack[15],
    )


def kernel(x, ffn1_norm, ffn1_w_in, ffn1_w_out, mix_norm, ffn2_norm, ffn2_w_in, ffn2_w_out, a_w_qkv, a_rel_bias, a_w_o, kv_norm, kv_w_down, kv_latent_norm, kv_w_up, b_w_dq, b_q_norm, b_w_uq, b_w_o, final_norm, loss_target, m_ffn1_norm, m_ffn1_w_in, m_ffn1_w_out, m_mix_norm, m_ffn2_norm, m_ffn2_w_in, m_ffn2_w_out, m_a_w_qkv, m_a_rel_bias, m_a_w_o, m_kv_norm, m_kv_w_down, m_kv_latent_norm, m_kv_w_up, m_b_w_dq, m_b_q_norm, m_b_w_uq, m_b_w_o, m_final_norm, v_ffn1_norm, v_ffn1_w_in, v_ffn1_w_out, v_mix_norm, v_ffn2_norm, v_ffn2_w_in, v_ffn2_w_out, v_a_w_qkv, v_a_rel_bias, v_a_w_o, v_kv_norm, v_kv_w_down, v_kv_latent_norm, v_kv_w_up, v_b_w_dq, v_b_q_norm, v_b_w_uq, v_b_w_o, v_final_norm):
    bl, seq, dn = x.shape
    t = bl * seq
    tm = _tile(t)
    nt = t // tm
    x2 = x.reshape(t, dn)
    target2 = loss_target.reshape(t, dn)

    (w_in1, w_in2, w_out1, w_out2, w_qkv, w_o_a, w_down, w_up, w_dq, w_uq, w_o_b) = _exchange(
        "gather_weights",
        [
            ("gather", ffn1_w_in.astype(BF16), 2), ("gather", ffn2_w_in.astype(BF16), 2),
            ("gather", ffn1_w_out.astype(BF16), 2), ("gather", ffn2_w_out.astype(BF16), 2),
            ("gather", a_w_qkv[0].astype(BF16), None), ("gather", a_w_o[0].astype(BF16), None),
            ("gather", kv_w_down.astype(BF16), None), ("gather", kv_w_up.astype(BF16), None),
            ("gather", b_w_dq[0].astype(BF16), None), ("gather", b_w_uq[0].astype(BF16), None),
            ("gather", b_w_o[0].astype(BF16), None),
        ],
    )
    w_out1 = w_out1.reshape(2, NJ, FB, dn)
    w_out2 = w_out2.reshape(2, NJ, FB, dn)
    qkv_wb = w_qkv.shape[-1]
    w_o_a = w_o_a.reshape(1, 1, dn, dn)
    w_o_b = w_o_b.reshape(1, 1, dn, dn)
    w_down = w_down.reshape(dn, KV_LORA + ROPE)
    w_dq = w_dq.reshape(dn, Q_LORA)
    c64, s64, p64, c192, s192, p192 = _rope_tables(seq)
    q_norm = b_q_norm.reshape(1, Q_LORA)
    latent_norm = kv_latent_norm.reshape(1, KV_LORA)
    bias = _rel_bias_fwd(jnp.pad(a_rel_bias[0], ((0, 0), (0, NREL_PAD - NREL))))

    h0, h1, h2, n1, hn, n2, gu1, gu2, a1, a2 = ([None, None] for _ in range(10))
    h0[0] = x2
    (n1[0],) = _norm_fwd("norm_x", x2, ffn1_norm[0:1])
    gu1[0], a1[0] = _ffn_in("ffn1_in_0", n1[0], w_in1, 0)
    h1[0], hn[0] = _mm_res_norm("ffn1_out_0", a1[0], w_out1, 0, h0[0], mix_norm[0:1], 0.5)
    qkv3 = _qkv_proj("qkv_proj", hn[0], w_qkv)
    o_a = _attn_a_fwd(qkv3, bias, bl, seq)
    h2[0], n2[0] = _mm_res_norm("attn_a_out", o_a.reshape(1, t, dn), w_o_a, 0, h1[0], ffn2_norm[0:1], 1.0)
    gu2[0], a2[0] = _ffn_in("ffn2_in_0", n2[0], w_in2, 0)
    h0[1], hk, n1[1] = _mm_res_norm(
        "ffn2_out_0", a2[0], w_out2, 0, h2[0], jnp.concatenate([kv_norm.reshape(1, dn), ffn1_norm[1:2]], axis=0), 0.5
    )
    ckr, ckv, kr = _kv_down(hk, w_down, latent_norm, c64, s64, p64, seq)
    kv = _kv_up(ckv, w_up)
    gu1[1], a1[1] = _ffn_in("ffn1_in_1", n1[1], w_in1, 1)
    h1[1], hn[1] = _mm_res_norm("ffn1_out_1", a1[1], w_out1, 1, h0[1], mix_norm[1:2], 0.5)
    cq_pre, cq = _q_down(hn[1], w_dq, q_norm)
    q = _q_up(cq, w_uq, c192, s192, p192, seq)
    o_b = _mla_fwd(q, kv, kr, bl, seq)
    h2[1], n2[1] = _mm_res_norm("attn_b_out", o_b.reshape(1, t, dn), w_o_b, 0, h1[1], ffn2_norm[1:2], 1.0)
    gu2[1], a2[1] = _ffn_in("ffn2_in_1", n2[1], w_in2, 1)
    (h_last,) = _mm_res_norm("ffn2_out_1", a2[1], w_out2, 1, h2[1], None, 0.5)
    dh, dg_final, loss_part = _loss_final(h_last, target2, final_norm.reshape(1, dn))

    dg_ffn1, dg_mix, dg_ffn2, dw_in1, dw_in2, dw_out1, dw_out2 = ([None, None] for _ in range(7))
    col128 = BS((tm, 128), lambda j, i: (i, j))
    full_row = BS((tm, dn), lambda j, i: (i, 0))
    dh, dg_ffn2[1], dw_in2[1], dw_out2[1] = _ffn_bwd("ffn2_1", 1, dh, n2[1], h2[1], ffn2_norm[1:2], gu2[1], a2[1], w_in2, w_out2)
    do_b = _mm_nt_plain("attn_b_do", dh, w_o_b.reshape(dn, dn))
    dw_o_b = _mm_tn("attn_b_dwo", o_b, col128, dh, full_row, SDS((NDEV, 128, dn), F32), BS((None, 128, dn), lambda j, i: (j, 0, 0)), NDEV, nt)
    dq_pre, dkv, dkr = _mla_bwd(q, kv, kr, do_b, c192, s192, p192, bl, seq)
    dw_uq = _mm_tn(
        "dw_uq", cq, BS((tm, Q_LORA), lambda j, i: (i, 0)), dq_pre, BS((None, tm, QK_B), lambda j, i: (j, i, 0)),
        SDS((HEADS_B, Q_LORA, QK_B), F32), BS((None, Q_LORA, QK_B), lambda j, i: (j, 0, 0)), HEADS_B, nt,
    )
    dcq_pre, dg_q = _mm_nt_norm_bwd(
        "dcq", dq_pre, BS((None, tm, QK_B), lambda i, j: (j, i, 0)), w_uq, BS((None, Q_LORA, QK_B), lambda i, j: (j, 0, 0)),
        HEADS_B, cq_pre, q_norm, None, BF16,
    )
    dw_dq = _mm_tn(
        "dw_dq", hn[1], col128, dcq_pre, BS((tm, Q_LORA), lambda j, i: (i, 0)),
        SDS((NDEV, 128, Q_LORA), F32), BS((None, 128, Q_LORA), lambda j, i: (j, 0, 0)), NDEV, nt,
    )
    dh, dg_mix[1] = _mm_nt_norm_bwd(
        "dhn_b", dcq_pre, BS((tm, Q_LORA), lambda i, j: (i, 0)), w_dq, BS((dn, Q_LORA), lambda i, j: (0, 0)),
        1, h1[1], mix_norm[1:2], dh, F32,
    )
    dh, dg_ffn1[1], dw_in1[1], dw_out1[1] = _ffn_bwd("ffn1_1", 1, dh, n1[1], h0[1], ffn1_norm[1:2], gu1[1], a1[1], w_in1, w_out1)
    dw_up = _mm_tn(
        "dw_up", ckv, BS((tm, KV_LORA), lambda j, i: (i, 0)), dkv, BS((tm, NOPE + V_DIM), lambda j, i: (i, j)),
        SDS((HEADS_B, KV_LORA, NOPE + V_DIM), F32), BS((None, KV_LORA, NOPE + V_DIM), lambda j, i: (j, 0, 0)), HEADS_B, nt,
    )
    dckr, dg_latent = _kv_latent_bwd(dkv, w_up, ckr, latent_norm, dkr, c64, s64, p64, seq)
    dw_down = _mm_tn(
        "dw_down", hk, col128, dckr, BS((tm, KV_LORA + ROPE), lambda j, i: (i, 0)),
        SDS((NDEV, 128, KV_LORA + ROPE), F32), BS((None, 128, KV_LORA + ROPE), lambda j, i: (j, 0, 0)), NDEV, nt,
    )
    dh, dg_kv = _mm_nt_norm_bwd(
        "dhk", dckr, BS((tm, KV_LORA + ROPE), lambda i, j: (i, 0)), w_down, BS((dn, KV_LORA + ROPE), lambda i, j: (0, 0)),
        1, h0[1], kv_norm.reshape(1, dn), dh, F32,
    )
    dh, dg_ffn2[0], dw_in2[0], dw_out2[0] = _ffn_bwd("ffn2_0", 0, dh, n2[0], h2[0], ffn2_norm[0:1], gu2[0], a2[0], w_in2, w_out2)
    do_a = _mm_nt_plain("attn_a_do", dh, w_o_a.reshape(dn, dn))
    dw_o_a = _mm_tn("attn_a_dwo", o_a, col128, dh, full_row, SDS((NDEV, 128, dn), F32), BS((None, 128, dn), lambda j, i: (j, 0, 0)), NDEV, nt)
    dqkv3, dbias = _attn_a_bwd(qkv3, do_a, bias, bl, seq)
    dw_qkv = _dw_qkv(hn[0], dqkv3, qkv_wb)
    dh, dg_mix[0] = _mm_nt_norm_bwd(
        "dhn_a", dqkv3, BS((3, tm, dn), lambda i, j: (0, i, 0)), w_qkv, BS((NDEV, dn, qkv_wb), lambda i, j: (0, 0, 0)),
        1, h1[0], mix_norm[0:1], dh, F32, mm_fn=_dqkv_mm(qkv_wb // 128),
    )
    dh, dg_ffn1[0], dw_in1[0], dw_out1[0] = _ffn_bwd("ffn1_0", 0, dh, n1[0], h0[0], ffn1_norm[0:1], gu1[0], a1[0], w_in1, w_out1)
    grad_x = dh.reshape(bl, seq, dn)
    dtable = _rel_bias_bwd(dbias)[:, :NREL]

    small = _pack_small(
        jnp.stack([dg_ffn1[0][0], dg_ffn1[1][0]]), jnp.stack([dg_mix[0][0], dg_mix[1][0]]), jnp.stack([dg_ffn2[0][0], dg_ffn2[1][0]]),
        dg_kv[0], dg_final[0], dg_q[0], dg_latent[0], dtable, loss_part[0],
    )
    rows_out = dw_out1[0].shape[0] * dw_out1[0].shape[1] // NDEV
    (r_in1, r_in2, r_out1, r_out2, r_qkv, r_o_a, r_down, r_up, r_dq, r_uq, r_o_b, r_small) = _exchange(
        "reduce_grads",
        [
            ("scatter", dw_in1), ("scatter", dw_in2),
            ("scatter", [g.reshape(NDEV, rows_out, dn) for g in dw_out1]), ("scatter", [g.reshape(NDEV, rows_out, dn) for g in dw_out2]),
            ("scatter", [dw_qkv]), ("scatter", [dw_o_a]), ("scatter", [dw_down]), ("scatter", [dw_up]),
            ("scatter", [dw_dq]), ("scatter", [dw_uq]), ("scatter", [dw_o_b]),
            ("gather", small, None),
        ],
    )

    def update(name, parts, w, m, v):
        rows = int(np.prod(w.shape[:-1]))
        cols = w.shape[-1]
        outs = _adamw(name, parts.reshape(NDEV, rows, cols), w.reshape(rows, cols), m.reshape(rows, cols), v.reshape(rows, cols))
        return [o.reshape(w.shape) for o in outs]

    res = {}
    res["ffn1_w_in"] = update("adamw_ffn1_w_in", r_in1, ffn1_w_in, m_ffn1_w_in, v_ffn1_w_in)
    res["ffn1_w_out"] = update("adamw_ffn1_w_out", r_out1, ffn1_w_out, m_ffn1_w_out, v_ffn1_w_out)
    res["ffn2_w_in"] = update("adamw_ffn2_w_in", r_in2, ffn2_w_in, m_ffn2_w_in, v_ffn2_w_in)
    res["ffn2_w_out"] = update("adamw_ffn2_w_out", r_out2, ffn2_w_out, m_ffn2_w_out, v_ffn2_w_out)
    res["a_w_qkv"] = update("adamw_a_w_qkv", r_qkv, a_w_qkv, m_a_w_qkv, v_a_w_qkv)
    res["a_w_o"] = update("adamw_a_w_o", r_o_a, a_w_o, m_a_w_o, v_a_w_o)
    res["kv_w_down"] = update("adamw_kv_w_down", r_down, kv_w_down, m_kv_w_down, v_kv_w_down)
    res["kv_w_up"] = update("adamw_kv_w_up", r_up, kv_w_up, m_kv_w_up, v_kv_w_up)
    res["b_w_dq"] = update("adamw_b_w_dq", r_dq, b_w_dq, m_b_w_dq, v_b_w_dq)
    res["b_w_uq"] = update("adamw_b_w_uq", r_uq, b_w_uq, m_b_w_uq, v_b_w_uq)
    res["b_w_o"] = update("adamw_b_w_o", r_o_b, b_w_o, m_b_w_o, v_b_w_o)
    zero_row = jnp.zeros((dn,), F32)
    packs = [
        _pack_small(f1, mx, f2, kvn, fin, qn, lat, rel, zero_row)
        for f1, mx, f2, kvn, fin, qn, lat, rel in (
            (ffn1_norm, mix_norm, ffn2_norm, kv_norm, final_norm, b_q_norm, kv_latent_norm, a_rel_bias),
            (m_ffn1_norm, m_mix_norm, m_ffn2_norm, m_kv_norm, m_final_norm, m_b_q_norm, m_kv_latent_norm, m_a_rel_bias),
            (v_ffn1_norm, v_mix_norm, v_ffn2_norm, v_kv_norm, v_final_norm, v_b_q_norm, v_kv_latent_norm, v_a_rel_bias),
        )
    ]
    small_out = [_unpack_small(o) for o in _adamw("adamw_small", r_small, *packs)]
    for name in ("ffn1_norm", "mix_norm", "ffn2_norm", "a_rel_bias", "kv_norm", "kv_latent_norm", "b_q_norm", "final_norm"):
        res[name] = [so[name] for so in small_out]
    loss = small_out[0]["last"][0]

    order = [
        "ffn1_norm", "ffn1_w_in", "ffn1_w_out", "mix_norm", "ffn2_norm", "ffn2_w_in", "ffn2_w_out", "a_w_qkv", "a_rel_bias",
        "a_w_o", "kv_norm", "kv_w_down", "kv_latent_norm", "kv_w_up", "b_w_dq", "b_q_norm", "b_w_uq", "b_w_o", "final_norm",
    ]
    return (loss, grad_x, *[res[n][0] for n in order], *[res[n][1] for n in order], *[res[n][2] for n in order], *[res[n][3] for n in order])
```

```python
import functools

import jax
import jax.numpy as jnp
import numpy as np
from jax import lax
from jax.experimental import pallas as pl
from jax.experimental.pallas import tpu as pltpu

NDEV = 8
D_MODEL = 1024
D_FF = 2816
FB = 2 * D_FF // NDEV
NJ = D_FF // FB
CHUNK = 64
LEFT_CHUNKS = 8
PAD = LEFT_CHUNKS * CHUNK
BAND = PAD + CHUNK
MAX_REL = 128
NREL = 2 * MAX_REL + 1
NREL_PAD = 384
HEADS_A = 16
HEADS_B = 8
NOPE = 128
ROPE = 64
QK_B = NOPE + ROPE
V_DIM = 128
Q_LORA = 768
KV_LORA = 256
ROPE_THETA = 10000.0
EPS = 1e-6
NEG_INF = -1e30
MLA_TQ = 256
ADAM_LR = 0.001
ADAM_B1 = 0.9
ADAM_B2 = 0.999
ADAM_EPS = 1e-08
ADAM_WD = 0.01
ADAM_STEP = 10
PACK_ROWS = 16
VMEM_LIMIT_BYTES = 56 * 1024 * 1024

F32 = jnp.float32
BF16 = jnp.bfloat16
SDS = jax.ShapeDtypeStruct
BS = pl.BlockSpec
HIGHEST = lax.Precision.HIGHEST
MESH = pl.DeviceIdType.MESH


def _cparams(n_axes):
    return pltpu.CompilerParams(dimension_semantics=("arbitrary",) * n_axes, vmem_limit_bytes=VMEM_LIMIT_BYTES)


def _tile(t, want=512):
    return want if t % want == 0 else t


def _dot(a, b):
    return jnp.dot(a, b, preferred_element_type=F32)


def _dot_nt(a, b):
    return lax.dot_general(a, b, (((1,), (1,)), ((), ())), preferred_element_type=F32)


def _dot_tn(a, b):
    return lax.dot_general(a, b, (((0,), (0,)), ((), ())), preferred_element_type=F32)


def _dot_exact(a, b):
    return jnp.dot(a, b, precision=HIGHEST, preferred_element_type=F32)


def _rms_scale(h):
    return lax.rsqrt(jnp.mean(h * h, axis=-1, keepdims=True) + EPS)


def _acc_rows(ref, val, step, n_steps):
    part = val.reshape(val.shape[0] // 8, 8, val.shape[1]).sum(axis=0)

    @pl.when(step == 0)
    def _():
        ref[...] = part

    @pl.when(step > 0)
    def _():
        ref[...] += part

    @pl.when(step == n_steps - 1)
    def _():
        ref[...] = jnp.broadcast_to(jnp.sum(ref[...], axis=0, keepdims=True), ref.shape)


def _exchange_plan(entries):
    ins = [e[1] for e in entries]
    kinds = [e[0] for e in entries]
    lands = [SDS((NDEV,) + a.shape if k == "gather" else a.shape, a.dtype) for k, a in zip(kinds, ins)]
    return ins, lands, kinds


def _mesh_place():
    x, y, c = lax.axis_index("x"), lax.axis_index("y"), lax.axis_index("c")
    return (x, y, c), 4 * x + 2 * y + c


def _flipped(place, p):
    x, y, c = place
    px = 1 - x if p & 4 else x
    py = 1 - y if p & 2 else y
    pc = 1 - c if p & 1 else c
    return (px, py, pc), 4 * px + 2 * py + pc


def _ends(kind, src_ref, land_ref, origin, target):
    if kind == "gather":
        return src_ref, land_ref.at[origin]
    return src_ref.at[target], land_ref.at[origin]


def _remote(kind, src_ref, land_ref, send_sems, recv_sems, k, p, place, me, arriving):
    peer_pos, peer = _flipped(place, p)
    src, dst = _ends(kind, src_ref, land_ref, me, peer)
    if arriving:
        dst = _ends(kind, src_ref, land_ref, peer, me)[1]
    sem = k * (NDEV - 1) + p - 1
    return pltpu.make_async_remote_copy(
        src_ref=src, dst_ref=dst, send_sem=send_sems.at[sem], recv_sem=recv_sems.at[sem], device_id=peer_pos, device_id_type=MESH,
    )


def _exchange(name, entries):
    ins, lands, kinds = _exchange_plan(entries)
    n = len(ins)

    def body(*refs):
        in_refs, land_refs = refs[:n], refs[n : 2 * n]
        send_sems, recv_sems, local_sems = refs[2 * n :]
        place, me = _mesh_place()
        local = []
        for k in range(n):
            src, dst = _ends(kinds[k], in_refs[k], land_refs[k], me, me)
            local.append(pltpu.make_async_copy(src, dst, local_sems.at[k]))
            local[-1].start()
        sends = []
        for p in range(1, NDEV):
            for k in range(n):
                sends.append(_remote(kinds[k], in_refs[k], land_refs[k], send_sems, recv_sems, k, p, place, me, False))
                sends[-1].start()
        for p in range(1, NDEV):
            for k in range(n):
                _remote(kinds[k], in_refs[k], land_refs[k], send_sems, recv_sems, k, p, place, me, True).wait_recv()
        for cp in sends:
            cp.wait_send()
        for cp in local:
            cp.wait()

    any_spec = BS(memory_space=pl.ANY)
    return pl.pallas_call(
        body, name=name, out_shape=lands, in_specs=[any_spec] * n, out_specs=[any_spec] * n,
        scratch_shapes=[
            pltpu.SemaphoreType.DMA((n * (NDEV - 1),)), pltpu.SemaphoreType.DMA((n * (NDEV - 1),)), pltpu.SemaphoreType.DMA((n,)),
        ],
    )(*ins)


HBM_SPEC = BS(memory_space=pltpu.HBM)
SEM_SPEC = BS(memory_space=pltpu.SEMAPHORE)
DATAFLOW = pltpu.SideEffectType.DATAFLOW_SIDE_EFFECTING


def _exchange_start(name, groups):
    plans = [_exchange_plan(g) for g in groups]
    ins = [a for plan in plans for a in plan[0]]
    lands = [s for plan in plans for s in plan[1]]
    kinds = [kind for plan in plans for kind in plan[2]]
    n_in, n_groups = len(ins), len(groups)

    def body(*refs):
        in_refs, land_refs = refs[:n_in], refs[n_in : 2 * n_in]
        sems = refs[2 * n_in : 2 * n_in + 2 * n_groups]
        token = refs[4 * n_in + 2 * n_groups]
        local_sems = refs[4 * n_in + 2 * n_groups + 1]
        place, me = _mesh_place()
        local = []
        for k, kind in enumerate(kinds):
            src, dst = _ends(kind, in_refs[k], land_refs[k], me, me)
            local.append(pltpu.make_async_copy(src, dst, local_sems.at[k]))
            local[-1].start()
        base = 0
        for g, plan in enumerate(plans):
            for p in range(1, NDEV):
                for k, kind in enumerate(plan[2]):
                    _remote(kind, in_refs[base + k], land_refs[base + k], sems[2 * g], sems[2 * g + 1], k, p, place, me, False).start()
            base += len(plan[2])
        for cp in local:
            cp.wait()
        token[...] = jnp.zeros_like(token)

    sem_shapes = []
    for plan in plans:
        sem_shapes += [pltpu.SemaphoreType.DMA((len(plan[2]) * (NDEV - 1),))] * 2
    outs = pl.pallas_call(
        body, name=name,
        out_shape=sem_shapes + [pltpu.HBM(a.shape, a.dtype) for a in ins] + [pltpu.HBM(s.shape, s.dtype) for s in lands] + [SDS((8, 128), F32)],
        in_specs=[HBM_SPEC] * (2 * n_in),
        out_specs=[SEM_SPEC] * (2 * n_groups) + [HBM_SPEC] * (2 * n_in) + [BS(memory_space=pltpu.VMEM)],
        input_output_aliases={i: 2 * n_groups + i for i in range(2 * n_in)},
        scratch_shapes=[pltpu.SemaphoreType.DMA((n_in,))],
        compiler_params=pltpu.CompilerParams(has_side_effects=DATAFLOW),
    )(
        *[pltpu.with_memory_space_constraint(a, pltpu.HBM) for a in ins],
        *[pltpu.with_memory_space_constraint(lax.empty(s.shape, s.dtype), pltpu.HBM) for s in lands],
    )
    sems, srcs, landed, token = outs[: 2 * n_groups], outs[2 * n_groups : 2 * n_groups + n_in], outs[2 * n_groups + n_in : -1], outs[-1]
    started, base = [], 0
    for g, plan in enumerate(plans):
        n = len(plan[2])
        started.append((sems[2 * g], sems[2 * g + 1], srcs[base : base + n], landed[base : base + n], plan[2]))
        base += n
    return started, token


def _exchange_wait(name, started, after):
    send_sems, recv_sems, srcs, landed, kinds = started
    n = len(kinds)

    def body(*refs):
        in_refs, land_refs = refs[:n], refs[n : 2 * n]
        send_ref, recv_ref = refs[2 * n], refs[2 * n + 1]
        place, me = _mesh_place()
        for p in range(1, NDEV):
            for k in range(n):
                _remote(kinds[k], in_refs[k], land_refs[k], send_ref, recv_ref, k, p, place, me, True).wait_recv()
        for p in range(1, NDEV):
            for k in range(n):
                _remote(kinds[k], in_refs[k], land_refs[k], send_ref, recv_ref, k, p, place, me, False).wait_send()

    outs = pl.pallas_call(
        body, name=name,
        out_shape=[pltpu.HBM(a.shape, a.dtype) for a in srcs] + [pltpu.HBM(a.shape, a.dtype) for a in landed],
        in_specs=[HBM_SPEC] * (2 * n) + [SEM_SPEC, SEM_SPEC, BS(memory_space=pl.ANY)],
        out_specs=[HBM_SPEC] * (2 * n),
        input_output_aliases={i: i for i in range(2 * n)},
        compiler_params=pltpu.CompilerParams(has_side_effects=DATAFLOW),
    )(*srcs, *landed, send_sems, recv_sems, after)
    return outs[n:]


def _dep_spec(n_axes):
    return BS((8, 128), (lambda i: (0, 0)) if n_axes == 1 else (lambda i, j: (0, 0)))


def _norm_fwd(name, h, gammas, dep):
    t, dn = h.shape
    ng = gammas.shape[0]
    tm = _tile(t)

    def body(h_ref, g_ref, dep_ref, *outs):
        hv = h_ref[...]
        hh = hv * _rms_scale(hv)
        for i, o_ref in enumerate(outs):
            o_ref[...] = (hh * g_ref[i : i + 1, :]).astype(BF16)

    row = BS((tm, dn), lambda i: (i, 0))
    return pl.pallas_call(
        body, name=name, grid=(t // tm,),
        in_specs=[row, BS((ng, dn), lambda i: (0, 0)), _dep_spec(1)],
        out_specs=[row] * ng, out_shape=[SDS((t, dn), BF16)] * ng,
        compiler_params=_cparams(1),
    )(h, gammas, dep)


def _ffn_in(name, n, w_in, layer):
    t, dn = n.shape
    tm = _tile(t)

    def body(n_ref, wg_ref, wu_ref, gu_ref, a_ref):
        xv = n_ref[...]
        g = _dot(xv, wg_ref[...])
        u = _dot(xv, wu_ref[...])
        gu_ref[0] = g
        gu_ref[1] = u
        a_ref[...] = (g * jax.nn.sigmoid(g) * u).astype(BF16)

    return pl.pallas_call(
        body, name=name, grid=(NJ, t // tm),
        in_specs=[
            BS((tm, dn), lambda j, i: (i, 0)),
            BS((None, None, dn, FB), lambda j, i: (layer, j, 0, 0)),
            BS((None, None, dn, FB), lambda j, i: (layer, j + NJ, 0, 0)),
        ],
        out_specs=[BS((None, 2, tm, FB), lambda j, i: (j, 0, i, 0)), BS((None, tm, FB), lambda j, i: (j, i, 0))],
        out_shape=[SDS((NJ, 2, t, FB), F32), SDS((NJ, t, FB), BF16)],
        compiler_params=_cparams(2),
    )(n, w_in, w_in)


def _mm_res_norm(name, a, w, layer, h_in, gammas, scale):
    nk, t, kb = a.shape
    dn = w.shape[-1]
    ng = 0 if gammas is None else gammas.shape[0]
    tm = _tile(t)

    def body(*refs):
        a_ref, w_ref, h_ref = refs[:3]
        g_ref = refs[3] if ng else None
        outs = refs[3 + (1 if ng else 0) :]
        acc = _dot(a_ref[0], w_ref[0])
        for k in range(1, nk):
            acc += _dot(a_ref[k], w_ref[k])
        ho = h_ref[...] + scale * acc
        outs[0][...] = ho
        if ng:
            hh = ho * _rms_scale(ho)
            for i in range(ng):
                outs[1 + i][...] = (hh * g_ref[i : i + 1, :]).astype(BF16)

    row = BS((tm, dn), lambda i: (i, 0))
    in_specs = [BS((nk, tm, kb), lambda i: (0, i, 0)), BS((None, nk, kb, dn), lambda i: (layer, 0, 0, 0)), row]
    args = [a, w, h_in]
    if ng:
        in_specs.append(BS((ng, dn), lambda i: (0, 0)))
        args.append(gammas)
    return pl.pallas_call(
        body, name=name, grid=(t // tm,),
        in_specs=in_specs,
        out_specs=[row] * (1 + ng), out_shape=[SDS((t, dn), F32)] + [SDS((t, dn), BF16)] * ng,
        compiler_params=_cparams(1),
    )(*args)


def _qkv_proj(name, hn, w_qkv):
    t, dn = hn.shape
    wb = w_qkv.shape[-1]
    per = wb // 128
    tm = _tile(t)

    def body(x_ref, w_ref, o_ref):
        xv = x_ref[...]
        for j in range(NDEV):
            yv = _dot(xv, w_ref[j]).astype(BF16)
            for i in range(per):
                n = per * j + i
                o_ref[n // 8, :, (n % 8) * 128 : (n % 8 + 1) * 128] = yv[:, i * 128 : (i + 1) * 128]

    return pl.pallas_call(
        body, name=name, grid=(t // tm,),
        in_specs=[BS((tm, dn), lambda i: (i, 0)), BS((NDEV, dn, wb), lambda i: (0, 0, 0))],
        out_specs=BS((3, tm, dn), lambda i: (0, i, 0)), out_shape=SDS((3, t, dn), BF16),
        compiler_params=_cparams(1),
    )(hn, w_qkv)


def _rel_onehot(i):
    r = lax.broadcasted_iota(jnp.int32, (NREL_PAD, BAND), 0)
    j = lax.broadcasted_iota(jnp.int32, (NREL_PAD, BAND), 1)
    idx = jnp.clip(PAD + i - j, -MAX_REL, MAX_REL) + MAX_REL
    return (idx == r).astype(F32)


def _rel_bias_fwd(table):
    def body(t_ref, o_ref):
        i8 = pl.program_id(0)
        for ii in range(8):
            o_ref[:, ii, :] = _dot_exact(t_ref[...], _rel_onehot(i8 * 8 + ii))

    return pl.pallas_call(
        body, name="rel_bias_fwd", grid=(CHUNK // 8,),
        in_specs=[BS((HEADS_A, NREL_PAD), lambda i: (0, 0))],
        out_specs=BS((HEADS_A, 8, BAND), lambda i: (0, i, 0)), out_shape=SDS((HEADS_A, CHUNK, BAND), F32),
        compiler_params=_cparams(1),
    )(table)


def _rel_bias_bwd(dbias):
    def body(d_ref, o_ref):
        i8 = pl.program_id(0)
        acc = jnp.zeros((HEADS_A, NREL_PAD), F32)
        for ii in range(8):
            acc += lax.dot_general(
                d_ref[:, ii, :], _rel_onehot(i8 * 8 + ii), (((1,), (1,)), ((), ())), precision=HIGHEST,
                preferred_element_type=F32,
            )

        @pl.when(i8 == 0)
        def _():
            o_ref[...] = acc

        @pl.when(i8 > 0)
        def _():
            o_ref[...] += acc

    return pl.pallas_call(
        body, name="rel_bias_bwd", grid=(CHUNK // 8,),
        in_specs=[BS((HEADS_A, 8, BAND), lambda i: (0, i, 0))],
        out_specs=BS((HEADS_A, NREL_PAD), lambda i: (0, 0)), out_shape=SDS((HEADS_A, NREL_PAD), F32),
        compiler_params=_cparams(1),
    )(dbias)


def _chunk_scores(qm, kb, bias, valid):
    s = _dot_nt(qm, kb) * (CHUNK ** -0.5) + bias
    s = jnp.where(valid, s, NEG_INF)
    e = jnp.exp(s - jnp.max(s, axis=-1, keepdims=True))
    return e / jnp.sum(e, axis=-1, keepdims=True)


def _attn_a_fwd(qkv3, bias, bl, seq):
    t, dn = qkv3.shape[1:]
    npair = dn // 128
    nchunk = seq // CHUNK

    def body(q_ref, k_ref, v_ref, b_ref, o_ref, kpad, vpad):
        kpad[0:PAD, :] = jnp.zeros((PAD, 128), BF16)
        vpad[0:PAD, :] = jnp.zeros((PAD, 128), BF16)
        kpad[PAD:, :] = k_ref[...]
        vpad[PAD:, :] = v_ref[...]
        lane = lax.broadcasted_iota(jnp.int32, (CHUNK, 128), 1)
        col = lax.broadcasted_iota(jnp.int32, (CHUNK, BAND), 1)

        def chunk(cidx, carry):
            r0 = pl.multiple_of(cidx * CHUNK, CHUNK)
            qc = q_ref[pl.ds(r0, CHUNK), :]
            kb = kpad[pl.ds(r0, BAND), :]
            vb = vpad[pl.ds(r0, BAND), :]
            valid = col >= PAD - cidx * CHUNK
            outs = []
            for hh in range(2):
                in_head = (lane < 64) if hh == 0 else (lane >= 64)
                qm = jnp.where(in_head, qc, jnp.zeros_like(qc))
                p = _chunk_scores(qm, kb, b_ref[hh], valid)
                outs.append(_dot(p.astype(BF16), vb))
            o_ref[pl.ds(r0, CHUNK), :] = jnp.where(lane < 64, outs[0], outs[1]).astype(BF16)
            return carry

        lax.fori_loop(0, nchunk, chunk, 0)

    return pl.pallas_call(
        body, name="attn_a_fwd", grid=(bl, npair),
        in_specs=[
            BS((None, seq, 128), lambda b, h: (0, b, h)),
            BS((None, seq, 128), lambda b, h: (1, b, h)),
            BS((None, seq, 128), lambda b, h: (2, b, h)),
            BS((2, CHUNK, BAND), lambda b, h: (h, 0, 0)),
        ],
        out_specs=BS((seq, 128), lambda b, h: (b, h)), out_shape=SDS((t, dn), BF16),
        scratch_shapes=[pltpu.VMEM((PAD + seq, 128), BF16), pltpu.VMEM((PAD + seq, 128), BF16)],
        compiler_params=_cparams(2),
    )(qkv3, qkv3, qkv3, bias)


def _attn_a_bwd(qkv3, do, bias, bl, seq):
    t, dn = qkv3.shape[1:]
    npair = dn // 128
    nchunk = seq // CHUNK

    def body(q_ref, k_ref, v_ref, do_ref, b_ref, dqkv_ref, db_ref, kpad, vpad, dkacc, dvacc):
        b = pl.program_id(1)
        kpad[0:PAD, :] = jnp.zeros((PAD, 128), BF16)
        vpad[0:PAD, :] = jnp.zeros((PAD, 128), BF16)
        kpad[PAD:, :] = k_ref[...]
        vpad[PAD:, :] = v_ref[...]
        dkacc[...] = jnp.zeros_like(dkacc)
        dvacc[...] = jnp.zeros_like(dvacc)

        @pl.when(b == 0)
        def _():
            db_ref[...] = jnp.zeros_like(db_ref)

        lane = lax.broadcasted_iota(jnp.int32, (CHUNK, 128), 1)
        col = lax.broadcasted_iota(jnp.int32, (CHUNK, BAND), 1)

        def chunk(cidx, carry):
            r0 = pl.multiple_of(cidx * CHUNK, CHUNK)
            qc = q_ref[pl.ds(r0, CHUNK), :]
            doc = do_ref[pl.ds(r0, CHUNK), :]
            kb = kpad[pl.ds(r0, BAND), :]
            vb = vpad[pl.ds(r0, BAND), :]
            valid = col >= PAD - cidx * CHUNK
            dqs = []
            dk_band = jnp.zeros((BAND, 128), F32)
            dv_band = jnp.zeros((BAND, 128), F32)
            for hh in range(2):
                in_head = (lane < 64) if hh == 0 else (lane >= 64)
                qm = jnp.where(in_head, qc, jnp.zeros_like(qc))
                dom = jnp.where(in_head, doc, jnp.zeros_like(doc))
                p = _chunk_scores(qm, kb, b_ref[hh], valid)
                dp = _dot_nt(dom, vb)
                ds = p * (dp - jnp.sum(p * dp, axis=-1, keepdims=True))
                db_ref[hh] += ds
                dsb = (ds * (CHUNK ** -0.5)).astype(BF16)
                dqs.append(_dot(dsb, kb))
                dk_band += _dot_tn(dsb, qm)
                dv_band += _dot_tn(p.astype(BF16), dom)
            dqkv_ref[0, pl.ds(r0, CHUNK), :] = jnp.where(lane < 64, dqs[0], dqs[1]).astype(BF16)
            dkacc[pl.ds(r0, BAND), :] += dk_band
            dvacc[pl.ds(r0, BAND), :] += dv_band
            return carry

        lax.fori_loop(0, nchunk, chunk, 0)
        dqkv_ref[1] = dkacc[PAD:, :].astype(BF16)
        dqkv_ref[2] = dvacc[PAD:, :].astype(BF16)

    return pl.pallas_call(
        body, name="attn_a_bwd", grid=(npair, bl),
        in_specs=[
            BS((None, seq, 128), lambda h, b: (0, b, h)),
            BS((None, seq, 128), lambda h, b: (1, b, h)),
            BS((None, seq, 128), lambda h, b: (2, b, h)),
            BS((seq, 128), lambda h, b: (b, h)),
            BS((2, CHUNK, BAND), lambda h, b: (h, 0, 0)),
        ],
        out_specs=[BS((3, seq, 128), lambda h, b: (0, b, h)), BS((2, CHUNK, BAND), lambda h, b: (h, 0, 0))],
        out_shape=[SDS((3, t, dn), BF16), SDS((HEADS_A, CHUNK, BAND), F32)],
        scratch_shapes=[
            pltpu.VMEM((PAD + seq, 128), BF16), pltpu.VMEM((PAD + seq, 128), BF16),
            pltpu.VMEM((PAD + seq, 128), F32), pltpu.VMEM((PAD + seq, 128), F32),
        ],
        compiler_params=_cparams(2),
    )(qkv3, qkv3, qkv3, do, bias)


def _rope_tables(seq):
    half = ROPE // 2
    freqs = ROPE_THETA ** (-jnp.arange(half, dtype=F32) / half)
    ang = jnp.arange(seq, dtype=F32)[:, None] * freqs[None, :]
    cos, sin = jnp.cos(ang), jnp.sin(ang)
    c64 = jnp.concatenate([cos, cos], axis=1)
    s64 = jnp.concatenate([-sin, sin], axis=1)
    c192 = jnp.concatenate([jnp.ones((seq, NOPE), F32), c64], axis=1)
    s192 = jnp.concatenate([jnp.zeros((seq, NOPE), F32), s64], axis=1)
    p64 = np.zeros((ROPE, ROPE), np.float32)
    for col in range(ROPE):
        p64[(col + half) % ROPE, col] = 1.0
    p192 = np.zeros((QK_B, QK_B), np.float32)
    p192[NOPE:, NOPE:] = p64
    return c64, s64, jnp.asarray(p64), c192, s192, jnp.asarray(p192)


def _rope(xv, cos, sin_signed, swap):
    return xv * cos + _dot_exact(xv, swap) * sin_signed


def _rope_bwd(dy, cos, sin_signed, swap):
    return dy * cos + _dot_exact(dy * sin_signed, swap)


def _q_down(hn, w_dq, q_norm):
    t, dn = hn.shape
    ql = w_dq.shape[1]
    tm = _tile(t)

    def body(x_ref, w_ref, g_ref, pre_ref, cq_ref):
        pre = _dot(x_ref[...], w_ref[...])
        pre_ref[...] = pre
        cq_ref[...] = (pre * _rms_scale(pre) * g_ref[...]).astype(BF16)

    return pl.pallas_call(
        body, name="q_down", grid=(t // tm,),
        in_specs=[BS((tm, dn), lambda i: (i, 0)), BS((dn, ql), lambda i: (0, 0)), BS((1, ql), lambda i: (0, 0))],
        out_specs=[BS((tm, ql), lambda i: (i, 0))] * 2, out_shape=[SDS((t, ql), F32), SDS((t, ql), BF16)],
        compiler_params=_cparams(1),
    )(hn, w_dq, q_norm)


def _q_up(cq, w_uq, c192, s192, p192, seq):
    t, ql = cq.shape
    tm = _tile(min(seq, 512), min(seq, 512))
    nseq = seq // tm

    def body(x_ref, w_ref, c_ref, s_ref, p_ref, o_ref):
        qf = _dot(x_ref[...], w_ref[...])
        o_ref[...] = _rope(qf, c_ref[...], s_ref[...], p_ref[...]).astype(BF16)

    pos = BS((tm, QK_B), lambda h, i: (i % nseq, 0))
    return pl.pallas_call(
        body, name="q_up", grid=(HEADS_B, t // tm),
        in_specs=[
            BS((tm, ql), lambda h, i: (i, 0)), BS((None, ql, QK_B), lambda h, i: (h, 0, 0)), pos, pos,
            BS((QK_B, QK_B), lambda h, i: (0, 0)),
        ],
        out_specs=BS((None, tm, QK_B), lambda h, i: (h, i, 0)), out_shape=SDS((HEADS_B, t, QK_B), BF16),
        compiler_params=_cparams(2),
    )(cq, w_uq, c192, s192, p192)


def _kv_down(hk, w_down, latent_norm, c64, s64, p64, seq):
    t, dn = hk.shape
    wd = w_down.shape[1]
    tm = _tile(min(seq, 512), min(seq, 512))
    nseq = seq // tm

    def body(x_ref, w_ref, g_ref, c_ref, s_ref, p_ref, ckr_ref, ckv_ref, kr_ref):
        ckr = _dot(x_ref[...], w_ref[...])
        ckr_ref[...] = ckr
        lat = ckr[:, :KV_LORA]
        ckv_ref[...] = (lat * _rms_scale(lat) * g_ref[...]).astype(BF16)
        kr_ref[...] = _rope(ckr[:, KV_LORA:], c_ref[...], s_ref[...], p_ref[...]).astype(BF16)

    pos = BS((tm, ROPE), lambda i: (i % nseq, 0))
    return pl.pallas_call(
        body, name="kv_down", grid=(t // tm,),
        in_specs=[
            BS((tm, dn), lambda i: (i, 0)), BS((dn, wd), lambda i: (0, 0)), BS((1, KV_LORA), lambda i: (0, 0)), pos, pos,
            BS((ROPE, ROPE), lambda i: (0, 0)),
        ],
        out_specs=[BS((tm, wd), lambda i: (i, 0)), BS((tm, KV_LORA), lambda i: (i, 0)), BS((tm, ROPE), lambda i: (i, 0))],
        out_shape=[SDS((t, wd), F32), SDS((t, KV_LORA), BF16), SDS((t, ROPE), BF16)],
        compiler_params=_cparams(1),
    )(hk, w_down, latent_norm, c64, s64, p64)


def _kv_up(ckv, w_up):
    t, kl = ckv.shape
    hb = w_up.shape[-1]
    tm = _tile(t)

    def body(x_ref, w_ref, o_ref):
        o_ref[...] = _dot(x_ref[...], w_ref[...]).astype(BF16)

    return pl.pallas_call(
        body, name="kv_up", grid=(HEADS_B, t // tm),
        in_specs=[BS((tm, kl), lambda h, i: (i, 0)), BS((None, kl, hb), lambda h, i: (h, 0, 0))],
        out_specs=BS((tm, hb), lambda h, i: (i, h)), out_shape=SDS((t, HEADS_B * hb), BF16),
        compiler_params=_cparams(2),
    )(ckv, w_up)


def _mla_probs(qi, kcat, row0, n_keys):
    s = _dot_nt(qi, kcat) * (QK_B ** -0.5)
    rows = lax.broadcasted_iota(jnp.int32, (qi.shape[0], n_keys), 0) + row0
    cols = lax.broadcasted_iota(jnp.int32, (qi.shape[0], n_keys), 1)
    s = jnp.where(jnp.right_shift(cols, 6) <= jnp.right_shift(rows, 6), s, NEG_INF)
    e = jnp.exp(s - jnp.max(s, axis=-1, keepdims=True))
    return e / jnp.sum(e, axis=-1, keepdims=True)


def _mla_fwd(q, kv, kr, bl, seq):
    t = kv.shape[0]
    tq = min(MLA_TQ, seq)

    def body(q_ref, kn_ref, v_ref, kr_ref, o_ref):
        kcat = jnp.concatenate([kn_ref[...], kr_ref[...]], axis=1)
        vv = v_ref[...]
        for i in range(seq // tq):
            n_keys = (i + 1) * tq
            p = _mla_probs(q_ref[i * tq : (i + 1) * tq, :], kcat[:n_keys], i * tq, n_keys)
            o_ref[i * tq : (i + 1) * tq, :] = _dot(p.astype(BF16), vv[:n_keys]).astype(BF16)

    return pl.pallas_call(
        body, name="mla_fwd", grid=(bl, HEADS_B),
        in_specs=[
            BS((None, seq, QK_B), lambda b, h: (h, b, 0)),
            BS((seq, NOPE), lambda b, h: (b, 2 * h)),
            BS((seq, V_DIM), lambda b, h: (b, 2 * h + 1)),
            BS((seq, ROPE), lambda b, h: (b, 0)),
        ],
        out_specs=BS((seq, V_DIM), lambda b, h: (b, h)), out_shape=SDS((t, HEADS_B * V_DIM), BF16),
        compiler_params=_cparams(2),
    )(q, kv, kv, kr)


def _mla_bwd(q, kv, kr, do, c192, s192, p192, bl, seq):
    t = kv.shape[0]
    tq = min(MLA_TQ, seq)

    def body(q_ref, kn_ref, v_ref, kr_ref, do_ref, c_ref, s_ref, p_ref, dq_ref, dkv_ref, dkr_ref, dkacc, dvacc):
        h = pl.program_id(1)
        kcat = jnp.concatenate([kn_ref[...], kr_ref[...]], axis=1)
        vv = v_ref[...]
        dkacc[...] = jnp.zeros_like(dkacc)
        dvacc[...] = jnp.zeros_like(dvacc)
        for i in range(seq // tq):
            n_keys = (i + 1) * tq
            rows = slice(i * tq, (i + 1) * tq)
            qi = q_ref[rows, :]
            doi = do_ref[rows, :]
            p = _mla_probs(qi, kcat[:n_keys], i * tq, n_keys)
            dp = _dot_nt(doi, vv[:n_keys])
            ds = p * (dp - jnp.sum(p * dp, axis=-1, keepdims=True))
            dsb = (ds * (QK_B ** -0.5)).astype(BF16)
            dq = _dot(dsb, kcat[:n_keys])
            dq_ref[rows, :] = _rope_bwd(dq, c_ref[rows, :], s_ref[rows, :], p_ref[...]).astype(BF16)
            dkacc[0:n_keys, :] += _dot_tn(dsb, qi)
            dvacc[0:n_keys, :] += _dot_tn(p.astype(BF16), doi)
        dk = dkacc[...]
        dkv_ref[:, :NOPE] = dk[:, :NOPE].astype(BF16)
        dkv_ref[:, NOPE:] = dvacc[...].astype(BF16)

        @pl.when(h == 0)
        def _():
            dkr_ref[...] = dk[:, NOPE:]

        @pl.when(h > 0)
        def _():
            dkr_ref[...] += dk[:, NOPE:]

    return pl.pallas_call(
        body, name="mla_bwd", grid=(bl, HEADS_B),
        in_specs=[
            BS((None, seq, QK_B), lambda b, h: (h, b, 0)),
            BS((seq, NOPE), lambda b, h: (b, 2 * h)),
            BS((seq, V_DIM), lambda b, h: (b, 2 * h + 1)),
            BS((seq, ROPE), lambda b, h: (b, 0)),
            BS((seq, V_DIM), lambda b, h: (b, h)),
            BS((seq, QK_B), lambda b, h: (0, 0)),
            BS((seq, QK_B), lambda b, h: (0, 0)),
            BS((QK_B, QK_B), lambda b, h: (0, 0)),
        ],
        out_specs=[
            BS((None, seq, QK_B), lambda b, h: (h, b, 0)),
            BS((seq, NOPE + V_DIM), lambda b, h: (b, h)),
            BS((seq, ROPE), lambda b, h: (b, 0)),
        ],
        out_shape=[SDS((HEADS_B, t, QK_B), BF16), SDS((t, HEADS_B * (NOPE + V_DIM)), BF16), SDS((t, ROPE), F32)],
        scratch_shapes=[pltpu.VMEM((seq, QK_B), F32), pltpu.VMEM((seq, V_DIM), F32)],
        compiler_params=_cparams(2),
    )(q, kv, kv, kr, do, c192, s192, p192)


def _loss_final(h, target, gamma):
    t, dn = h.shape
    tm = _tile(t)
    nt = t // tm

    def body(h_ref, t_ref, g_ref, dh_ref, dg_ref, loss_ref):
        i = pl.program_id(0)
        hv = h_ref[...]
        r = _rms_scale(hv)
        hh = hv * r
        gam = g_ref[...]
        err = hh * gam - t_ref[...]
        part = 0.5 * jnp.sum(jnp.mean(err * err, axis=-1, keepdims=True))

        @pl.when(i == 0)
        def _():
            loss_ref[...] = jnp.zeros_like(loss_ref)

        loss_ref[...] += part
        dy = err * (1.0 / dn)
        _acc_rows(dg_ref, dy * hh, i, nt)
        t1 = dy * gam
        dh_ref[...] = r * (t1 - hh * jnp.mean(t1 * hh, axis=-1, keepdims=True))

    row = BS((tm, dn), lambda i: (i, 0))
    return pl.pallas_call(
        body, name="loss_final", grid=(nt,),
        in_specs=[row, row, BS((1, dn), lambda i: (0, 0))],
        out_specs=[row, BS((8, dn), lambda i: (0, 0)), BS((8, 128), lambda i: (0, 0))],
        out_shape=[SDS((t, dn), F32), SDS((8, dn), F32), SDS((8, 128), F32)],
        compiler_params=_cparams(1),
    )(h, target, gamma)


def _ffn_bwd_in(name, dh, w_out, layer, gu, dep=None):
    t, dn = dh.shape
    tm = _tile(t)

    def body(dh_ref, w_ref, gu_ref, *rest):
        o_ref = rest[-1]
        xv = (0.5 * dh_ref[...]).astype(BF16)
        da = _dot_nt(xv, w_ref[...])
        g = gu_ref[0]
        u = gu_ref[1]
        sg = jax.nn.sigmoid(g)
        o_ref[0] = (da * u * (sg * (1.0 + g * (1.0 - sg)))).astype(BF16)
        o_ref[1] = (da * (g * sg)).astype(BF16)

    blk = BS((None, 2, tm, FB), lambda j, i: (j, 0, i, 0))
    deps = [] if dep is None else [dep]
    return pl.pallas_call(
        body, name=name, grid=(NJ, t // tm),
        in_specs=[BS((tm, dn), lambda j, i: (i, 0)), BS((None, None, FB, dn), lambda j, i: (layer, j, 0, 0)), blk]
        + [_dep_spec(2)] * len(deps),
        out_specs=blk, out_shape=SDS((NJ, 2, t, FB), BF16),
        compiler_params=_cparams(2),
    )(dh, w_out, gu, *deps)


def _mm_nt_plain(name, xf, w, dep=None):
    t, dn = xf.shape
    n = w.shape[0]
    tm = _tile(t)

    def body(x_ref, w_ref, *rest):
        rest[-1][...] = _dot_nt(x_ref[...].astype(BF16), w_ref[...]).astype(BF16)

    deps = [] if dep is None else [dep]
    return pl.pallas_call(
        body, name=name, grid=(t // tm,),
        in_specs=[BS((tm, dn), lambda i: (i, 0)), BS((n, dn), lambda i: (0, 0))] + [_dep_spec(1)] * len(deps),
        out_specs=BS((tm, n), lambda i: (i, 0)), out_shape=SDS((t, n), BF16),
        compiler_params=_cparams(1),
    )(xf, w, *deps)


def _mm_tn(name, xa, x_spec, ya, y_spec, out_shape, out_spec, nj, nt, y_scale=None):
    def body(x_ref, y_ref, o_ref):
        i = pl.program_id(1)
        yv = y_ref[...]
        if yv.dtype != BF16:
            yv = (yv if y_scale is None else y_scale * yv).astype(BF16)
        part = _dot_tn(x_ref[...], yv)

        @pl.when(i == 0)
        def _():
            o_ref[...] = part

        @pl.when(i > 0)
        def _():
            o_ref[...] += part

    return pl.pallas_call(
        body, name=name, grid=(nj, nt),
        in_specs=[x_spec, y_spec], out_specs=out_spec, out_shape=out_shape,
        compiler_params=_cparams(2),
    )(xa, ya)


def _dw_qkv(hn, dqkv3, wb):
    t, dn = hn.shape
    per = wb // 128
    tm = _tile(t)

    def body(x_ref, y_ref, o_ref):
        i = pl.program_id(0)
        xv = x_ref[...]
        for j in range(NDEV):
            cols = [y_ref[(per * j + k) // 8, :, ((per * j + k) % 8) * 128 : ((per * j + k) % 8 + 1) * 128] for k in range(per)]
            part = _dot_tn(xv, jnp.concatenate(cols, axis=1))

            @pl.when(i == 0)
            def _():
                o_ref[j] = part

            @pl.when(i > 0)
            def _():
                o_ref[j] += part

    return pl.pallas_call(
        body, name="dw_qkv", grid=(t // tm,),
        in_specs=[BS((tm, dn), lambda i: (i, 0)), BS((3, tm, dn), lambda i: (0, i, 0))],
        out_specs=BS((NDEV, dn, wb), lambda i: (0, 0, 0)), out_shape=SDS((NDEV, dn, wb), F32),
        compiler_params=_cparams(1),
    )(hn, dqkv3)


def _mm_nt_epi(name, ya, y_spec, wa, w_spec, nj, n_out, extra, out_shapes, out_specs, epilogue, tm, nt, mm_fn=None):
    n_extra = len(extra)
    n_outs = len(out_shapes)

    def body(*refs):
        y_ref, w_ref = refs[:2]
        ex = refs[2 : 2 + n_extra]
        outs = refs[2 + n_extra : 2 + n_extra + n_outs]
        i = pl.program_id(0)
        j = pl.program_id(1)
        part = _dot_nt(y_ref[...], w_ref[...]) if mm_fn is None else mm_fn(y_ref, w_ref)
        if nj == 1:
            epilogue(part, ex, outs, i, nt)
            return
        acc = refs[-1]

        @pl.when(j == 0)
        def _():
            acc[...] = part

        @pl.when(j > 0)
        def _():
            acc[...] += part

        @pl.when(j == nj - 1)
        def _():
            epilogue(acc[...], ex, outs, i, nt)

    return pl.pallas_call(
        body, name=name, grid=(nt, nj),
        in_specs=[y_spec, w_spec] + [spec for _, spec in extra],
        out_specs=out_specs, out_shape=out_shapes,
        scratch_shapes=[] if nj == 1 else [pltpu.VMEM((tm, n_out), F32)],
        compiler_params=_cparams(2),
    )(ya, wa, *[arr for arr, _ in extra])


def _norm_bwd(dn, hv, gam):
    r = _rms_scale(hv)
    hh = hv * r
    t1 = dn * gam
    return r * (t1 - hh * jnp.mean(t1 * hh, axis=-1, keepdims=True)), dn * hh


def _norm_bwd_epilogue(has_res, out_dtype):
    def epilogue(dn, ex, outs, i, nt):
        dh, dg_rows = _norm_bwd(dn, ex[0][...], ex[1][...])
        _acc_rows(outs[1], dg_rows, i, nt)
        if has_res:
            dh = dh + ex[2][...]
        outs[0][...] = dh.astype(out_dtype)

    return epilogue


def _mm_nt_norm_bwd(name, ya, y_spec, wa, w_spec, nj, h, gamma, res, out_dtype, mm_fn=None, want_tm=512, dep=None):
    t, n = h.shape
    tm = _tile(t, want_tm)
    nt = t // tm
    row = BS((tm, n), lambda i, j: (i, 0))
    extra = [(h, row), (gamma, BS((1, n), lambda i, j: (0, 0)))]
    if res is not None:
        extra.append((res, row))
    if dep is not None:
        extra.append((dep, _dep_spec(2)))
    return _mm_nt_epi(
        name, ya, y_spec, wa, w_spec, nj, n, extra, [SDS((t, n), out_dtype), SDS((8, n), F32)],
        [row, BS((8, n), lambda i, j: (0, 0))], _norm_bwd_epilogue(res is not None, out_dtype), tm, nt, mm_fn,
    )


def _dev_block(jj):
    return jj // 2 + NJ * (jj % 2)


def _ffn_bwd(tag, dh, n_in, h_in, gamma, gu, a, w_in, w_out, more_grads, dep):
    t, dn = dh.shape
    tm = _tile(t)
    nt = t // tm
    layer = 0
    dgu = _ffn_bwd_in(f"{tag}_bwd_in", dh, w_out, layer, gu, dep).reshape(2 * NJ, t, FB)
    dw_out = _mm_tn(
        f"{tag}_dw_out", a, BS((None, tm, FB), lambda j, i: (j, i, 0)), dh, BS((tm, dn), lambda j, i: (i, 0)),
        SDS((NJ, FB, dn), F32), BS((None, FB, dn), lambda j, i: (j, 0, 0)), NJ, nt, y_scale=0.5,
    )
    dw_in = _mm_tn(
        f"{tag}_dw_in", n_in, BS((tm, dn), lambda j, i: (i, 0)), dgu, BS((None, tm, FB), lambda j, i: (j, i, 0)),
        SDS((NDEV, dn, FB), F32), BS((None, dn, FB), lambda j, i: (_dev_block(j), 0, 0)), NDEV, nt,
    )
    entries = [("scatter", dw_in), ("scatter", dw_out.reshape(NDEV, NJ * FB // NDEV, dn))] + [("scatter", g) for g in more_grads]
    (started,), token = _exchange_start(f"{tag}_reduce_start", [entries])
    dh_in, dgam = _mm_nt_norm_bwd(
        f"{tag}_dn", dgu, BS((None, tm, FB), lambda i, j: (j, i, 0)),
        w_in, BS((None, None, dn, FB), lambda i, j: (layer, _dev_block(j), 0, 0)), NDEV, h_in, gamma, dh, F32, dep=token,
    )
    return dh_in, dgam, started


def _dqkv_mm(per):
    def mm(y_ref, w_ref):
        acc = None
        for j in range(NDEV):
            cols = [y_ref[(per * j + k) // 8, :, ((per * j + k) % 8) * 128 : ((per * j + k) % 8 + 1) * 128] for k in range(per)]
            part = _dot_nt(jnp.concatenate(cols, axis=1), w_ref[j])
            acc = part if acc is None else acc + part
        return acc

    return mm


def _kv_latent_bwd(dkv, w_up, ckr, latent_norm, dkr, c64, s64, p64, seq):
    t, wd = ckr.shape
    hb = w_up.shape[-1]
    tm = _tile(min(seq, 512), min(seq, 512))
    nt = t // tm
    nseq = seq // tm

    def epilogue(dn, ex, outs, i, nt_):
        dlat, dg_rows = _norm_bwd(dn, ex[0][...], ex[1][...])
        _acc_rows(outs[1], dg_rows, i, nt_)
        outs[0][:, :KV_LORA] = dlat.astype(BF16)
        outs[0][:, KV_LORA:] = _rope_bwd(ex[2][...], ex[3][...], ex[4][...], ex[5][...]).astype(BF16)

    pos = BS((tm, ROPE), lambda i, j: (i % nseq, 0))
    extra = [
        (ckr, BS((tm, KV_LORA), lambda i, j: (i, 0))), (latent_norm, BS((1, KV_LORA), lambda i, j: (0, 0))),
        (dkr, BS((tm, ROPE), lambda i, j: (i, 0))), (c64, pos), (s64, pos), (p64, BS((ROPE, ROPE), lambda i, j: (0, 0))),
    ]
    return _mm_nt_epi(
        "kv_latent_bwd", dkv, BS((tm, hb), lambda i, j: (i, j)), w_up, BS((None, KV_LORA, hb), lambda i, j: (j, 0, 0)),
        HEADS_B, KV_LORA, extra, [SDS((t, wd), BF16), SDS((8, KV_LORA), F32)],
        [BS((tm, wd), lambda i, j: (i, 0)), BS((8, KV_LORA), lambda i, j: (0, 0))], epilogue, tm, nt,
    )


def _adamw(name, parts, w, m, v):
    n_layers = len(parts)
    rows, cols = w.shape[0] // n_layers, w.shape[1]
    tr = max(d for d in range(8, min(rows, 256) + 1, 8) if rows % d == 0)
    nb = rows // tr

    def body(*refs):
        p_refs = refs[:n_layers]
        w_ref, m_ref, v_ref, g_ref, d_ref, nm_ref, nv_ref = refs[n_layers : n_layers + 7]
        layer = pl.program_id(0)
        for lp in range(n_layers):

            @pl.when(layer == lp)
            def _():
                g = p_refs[lp][0]
                for k in range(1, NDEV):
                    g = g + p_refs[lp][k]
                g_ref[...] = g

        g = g_ref[...]
        nm = ADAM_B1 * m_ref[...] + (1.0 - ADAM_B1) * g
        nv = ADAM_B2 * v_ref[...] + (1.0 - ADAM_B2) * (g * g)
        nm_ref[...] = nm
        nv_ref[...] = nv
        m_hat = nm / (1.0 - ADAM_B1 ** ADAM_STEP)
        v_hat = nv / (1.0 - ADAM_B2 ** ADAM_STEP)
        d_ref[...] = -ADAM_LR * (m_hat / (jnp.sqrt(v_hat) + ADAM_EPS) + ADAM_WD * w_ref[...])

    def part_spec(lp):
        return BS((NDEV, tr, cols), lambda l, i: (0, jnp.where(l == lp, i, jnp.where(l < lp, 0, nb - 1)), 0))

    row = BS((tr, cols), lambda l, i: (l * nb + i, 0))
    return pl.pallas_call(
        body, name=name, grid=(n_layers, nb),
        in_specs=[part_spec(lp) for lp in range(n_layers)] + [row, row, row],
        out_specs=[row] * 4, out_shape=[SDS(w.shape, F32)] * 4,
        compiler_params=_cparams(2),
    )(*parts, w, m, v)


def _pack_small(ffn1_norm, mix_norm, ffn2_norm, kv_norm, final_norm, q_norm, latent_norm, rel_bias, last_row):
    dn = ffn1_norm.shape[-1]

    def rows_of(a, n_rows):
        flat = a.reshape(-1)
        return jnp.pad(flat, (0, n_rows * dn - flat.shape[0])).reshape(n_rows, dn)

    return jnp.concatenate(
        [
            ffn1_norm.reshape(2, dn), mix_norm.reshape(2, dn), ffn2_norm.reshape(2, dn), kv_norm.reshape(1, dn),
            final_norm.reshape(1, dn), rows_of(q_norm, 1), rows_of(latent_norm, 1), rows_of(rel_bias, 5), rows_of(last_row, 1),
        ],
        axis=0,
    )


def _unpack_small(pack):
    dn = pack.shape[-1]
    return dict(
        ffn1_norm=pack[0:2], mix_norm=pack[2:4], ffn2_norm=pack[4:6], kv_norm=pack[6], final_norm=pack[7],
        b_q_norm=pack[8, :Q_LORA].reshape(1, Q_LORA), kv_latent_norm=pack[9, :KV_LORA],
        a_rel_bias=pack[10:15].reshape(-1)[: HEADS_A * NREL].reshape(1, HEADS_A, NREL), last=pack[15],
    )


def kernel(x, ffn1_norm, ffn1_w_in, ffn1_w_out, mix_norm, ffn2_norm, ffn2_w_in, ffn2_w_out, a_w_qkv, a_rel_bias, a_w_o, kv_norm, kv_w_down, kv_latent_norm, kv_w_up, b_w_dq, b_q_norm, b_w_uq, b_w_o, final_norm, loss_target, m_ffn1_norm, m_ffn1_w_in, m_ffn1_w_out, m_mix_norm, m_ffn2_norm, m_ffn2_w_in, m_ffn2_w_out, m_a_w_qkv, m_a_rel_bias, m_a_w_o, m_kv_norm, m_kv_w_down, m_kv_latent_norm, m_kv_w_up, m_b_w_dq, m_b_q_norm, m_b_w_uq, m_b_w_o, m_final_norm, v_ffn1_norm, v_ffn1_w_in, v_ffn1_w_out, v_mix_norm, v_ffn2_norm, v_ffn2_w_in, v_ffn2_w_out, v_a_w_qkv, v_a_rel_bias, v_a_w_o, v_kv_norm, v_kv_w_down, v_kv_latent_norm, v_kv_w_up, v_b_w_dq, v_b_q_norm, v_b_w_uq, v_b_w_o, v_final_norm):
    bl, seq, dn = x.shape
    t = bl * seq
    tm = _tile(t)
    nt = t // tm
    x2 = x.reshape(t, dn)
    target2 = loss_target.reshape(t, dn)

    def gathered(*ws):
        return [("gather", w.astype(BF16)) for w in ws]

    ag, token = _exchange_start(
        "gather_start",
        [
            gathered(ffn1_w_in[0]), gathered(ffn1_w_out[0]), gathered(a_w_qkv[0], a_w_o[0]), gathered(ffn2_w_in[0], ffn2_w_out[0]),
            gathered(kv_w_down, kv_w_up), gathered(ffn1_w_in[1], ffn1_w_out[1]), gathered(b_w_dq[0], b_w_uq[0], b_w_o[0]),
            gathered(ffn2_w_in[1], ffn2_w_out[1]),
        ],
    )

    def as_w_in(w):
        return w.reshape(1, NDEV, dn, FB)

    def as_w_out(w):
        return w.reshape(1, NJ, FB, dn)

    c64, s64, p64, c192, s192, p192 = _rope_tables(seq)
    q_norm = b_q_norm.reshape(1, Q_LORA)
    latent_norm = kv_latent_norm.reshape(1, KV_LORA)
    bias = _rel_bias_fwd(jnp.pad(a_rel_bias[0], ((0, 0), (0, NREL_PAD - NREL))))

    h0, h1, h2, n1, hn, n2, gu1, gu2, a1, a2, w_in1, w_in2, w_out1, w_out2 = ([None, None] for _ in range(14))
    h0[0] = x2
    (n1[0],) = _norm_fwd("norm_x", x2, ffn1_norm[0:1], token)
    w_in1[0] = as_w_in(_exchange_wait("gather_wait_0", ag[0], n1[0])[0])
    gu1[0], a1[0] = _ffn_in("ffn1_in_0", n1[0], w_in1[0], 0)
    w_out1[0] = as_w_out(_exchange_wait("gather_wait_1", ag[1], a1[0])[0])
    h1[0], hn[0] = _mm_res_norm("ffn1_out_0", a1[0], w_out1[0], 0, h0[0], mix_norm[0:1], 0.5)
    w_qkv, w_o_a = _exchange_wait("gather_wait_2", ag[2], hn[0])
    qkv_wb = w_qkv.shape[-1]
    w_o_a = w_o_a.reshape(1, 1, dn, dn)
    qkv3 = _qkv_proj("qkv_proj", hn[0], w_qkv)
    o_a = _attn_a_fwd(qkv3, bias, bl, seq)
    h2[0], n2[0] = _mm_res_norm("attn_a_out", o_a.reshape(1, t, dn), w_o_a, 0, h1[0], ffn2_norm[0:1], 1.0)
    w_in2[0], w_out2[0] = _exchange_wait("gather_wait_3", ag[3], n2[0])
    w_in2[0], w_out2[0] = as_w_in(w_in2[0]), as_w_out(w_out2[0])
    gu2[0], a2[0] = _ffn_in("ffn2_in_0", n2[0], w_in2[0], 0)
    h0[1], hk, n1[1] = _mm_res_norm(
        "ffn2_out_0", a2[0], w_out2[0], 0, h2[0], jnp.concatenate([kv_norm.reshape(1, dn), ffn1_norm[1:2]], axis=0), 0.5
    )
    w_down, w_up = _exchange_wait("gather_wait_4", ag[4], hk)
    w_down = w_down.reshape(dn, KV_LORA + ROPE)
    ckr, ckv, kr = _kv_down(hk, w_down, latent_norm, c64, s64, p64, seq)
    kv = _kv_up(ckv, w_up)
    w_in1[1], w_out1[1] = _exchange_wait("gather_wait_5", ag[5], kv)
    w_in1[1], w_out1[1] = as_w_in(w_in1[1]), as_w_out(w_out1[1])
    gu1[1], a1[1] = _ffn_in("ffn1_in_1", n1[1], w_in1[1], 0)
    h1[1], hn[1] = _mm_res_norm("ffn1_out_1", a1[1], w_out1[1], 0, h0[1], mix_norm[1:2], 0.5)
    w_dq, w_uq, w_o_b = _exchange_wait("gather_wait_6", ag[6], hn[1])
    w_dq = w_dq.reshape(dn, Q_LORA)
    w_o_b = w_o_b.reshape(1, 1, dn, dn)
    cq_pre, cq = _q_down(hn[1], w_dq, q_norm)
    q = _q_up(cq, w_uq, c192, s192, p192, seq)
    o_b = _mla_fwd(q, kv, kr, bl, seq)
    h2[1], n2[1] = _mm_res_norm("attn_b_out", o_b.reshape(1, t, dn), w_o_b, 0, h1[1], ffn2_norm[1:2], 1.0)
    w_in2[1], w_out2[1] = _exchange_wait("gather_wait_7", ag[7], n2[1])
    w_in2[1], w_out2[1] = as_w_in(w_in2[1]), as_w_out(w_out2[1])
    gu2[1], a2[1] = _ffn_in("ffn2_in_1", n2[1], w_in2[1], 0)
    (h_last,) = _mm_res_norm("ffn2_out_1", a2[1], w_out2[1], 0, h2[1], None, 0.5)
    dh, dg_final, loss_part = _loss_final(h_last, target2, final_norm.reshape(1, dn))

    dg_ffn1, dg_mix, dg_ffn2, rs_ffn1, rs_ffn2 = ([None, None] for _ in range(5))
    col128 = BS((tm, 128), lambda j, i: (i, j))
    full_row = BS((tm, dn), lambda j, i: (i, 0))
    dh, dg_ffn2[1], rs_ffn2[1] = _ffn_bwd("ffn2_1", dh, n2[1], h2[1], ffn2_norm[1:2], gu2[1], a2[1], w_in2[1], w_out2[1], [], None)
    do_b = _mm_nt_plain("attn_b_do", dh, w_o_b.reshape(dn, dn))
    dw_o_b = _mm_tn("attn_b_dwo", o_b, col128, dh, full_row, SDS((NDEV, 128, dn), F32), BS((None, 128, dn), lambda j, i: (j, 0, 0)), NDEV, nt)
    dq_pre, dkv, dkr = _mla_bwd(q, kv, kr, do_b, c192, s192, p192, bl, seq)
    dw_uq = _mm_tn(
        "dw_uq", cq, BS((tm, Q_LORA), lambda j, i: (i, 0)), dq_pre, BS((None, tm, QK_B), lambda j, i: (j, i, 0)),
        SDS((HEADS_B, Q_LORA, QK_B), F32), BS((None, Q_LORA, QK_B), lambda j, i: (j, 0, 0)), HEADS_B, nt,
    )
    dcq_pre, dg_q = _mm_nt_norm_bwd(
        "dcq", dq_pre, BS((None, tm, QK_B), lambda i, j: (j, i, 0)), w_uq, BS((None, Q_LORA, QK_B), lambda i, j: (j, 0, 0)),
        HEADS_B, cq_pre, q_norm, None, BF16,
    )
    dw_dq = _mm_tn(
        "dw_dq", hn[1], col128, dcq_pre, BS((tm, Q_LORA), lambda j, i: (i, 0)),
        SDS((NDEV, 128, Q_LORA), F32), BS((None, 128, Q_LORA), lambda j, i: (j, 0, 0)), NDEV, nt,
    )
    dh, dg_mix[1] = _mm_nt_norm_bwd(
        "dhn_b", dcq_pre, BS((tm, Q_LORA), lambda i, j: (i, 0)), w_dq, BS((dn, Q_LORA), lambda i, j: (0, 0)),
        1, h1[1], mix_norm[1:2], dh, F32,
    )
    dh, dg_ffn1[1], rs_ffn1[1] = _ffn_bwd(
        "ffn1_1", dh, n1[1], h0[1], ffn1_norm[1:2], gu1[1], a1[1], w_in1[1], w_out1[1], [dw_o_b, dw_uq, dw_dq], None
    )
    dw_up = _mm_tn(
        "dw_up", ckv, BS((tm, KV_LORA), lambda j, i: (i, 0)), dkv, BS((tm, NOPE + V_DIM), lambda j, i: (i, j)),
        SDS((HEADS_B, KV_LORA, NOPE + V_DIM), F32), BS((None, KV_LORA, NOPE + V_DIM), lambda j, i: (j, 0, 0)), HEADS_B, nt,
    )
    dckr, dg_latent = _kv_latent_bwd(dkv, w_up, ckr, latent_norm, dkr, c64, s64, p64, seq)
    dw_down = _mm_tn(
        "dw_down", hk, col128, dckr, BS((tm, KV_LORA + ROPE), lambda j, i: (i, 0)),
        SDS((NDEV, 128, KV_LORA + ROPE), F32), BS((None, 128, KV_LORA + ROPE), lambda j, i: (j, 0, 0)), NDEV, nt,
    )
    dh, dg_kv = _mm_nt_norm_bwd(
        "dhk", dckr, BS((tm, KV_LORA + ROPE), lambda i, j: (i, 0)), w_down, BS((dn, KV_LORA + ROPE), lambda i, j: (0, 0)),
        1, h0[1], kv_norm.reshape(1, dn), dh, F32,
    )
    dh, dg_ffn2[0], rs_ffn2[0] = _ffn_bwd(
        "ffn2_0", dh, n2[0], h2[0], ffn2_norm[0:1], gu2[0], a2[0], w_in2[0], w_out2[0], [dw_up, dw_down], None
    )
    do_a = _mm_nt_plain("attn_a_do", dh, w_o_a.reshape(dn, dn))
    dw_o_a = _mm_tn("attn_a_dwo", o_a, col128, dh, full_row, SDS((NDEV, 128, dn), F32), BS((None, 128, dn), lambda j, i: (j, 0, 0)), NDEV, nt)
    dqkv3, dbias = _attn_a_bwd(qkv3, do_a, bias, bl, seq)
    dw_qkv = _dw_qkv(hn[0], dqkv3, qkv_wb)
    dh, dg_mix[0] = _mm_nt_norm_bwd(
        "dhn_a", dqkv3, BS((3, tm, dn), lambda i, j: (0, i, 0)), w_qkv, BS((NDEV, dn, qkv_wb), lambda i, j: (0, 0, 0)),
        1, h1[0], mix_norm[0:1], dh, F32, mm_fn=_dqkv_mm(qkv_wb // 128),
    )
    (rs_attn_a,), token = _exchange_start("attn_a_reduce_start", [[("scatter", dw_o_a), ("scatter", dw_qkv)]])
    dh, dg_ffn1[0], rs_ffn1[0] = _ffn_bwd("ffn1_0", dh, n1[0], h0[0], ffn1_norm[0:1], gu1[0], a1[0], w_in1[0], w_out1[0], [], token)
    grad_x = dh.reshape(bl, seq, dn)
    dtable = _rel_bias_bwd(dbias)[:, :NREL]

    small = _pack_small(
        jnp.stack([dg_ffn1[0][0], dg_ffn1[1][0]]), jnp.stack([dg_mix[0][0], dg_mix[1][0]]), jnp.stack([dg_ffn2[0][0], dg_ffn2[1][0]]),
        dg_kv[0], dg_final[0], dg_q[0], dg_latent[0], dtable, loss_part[0],
    )
    (r_small,) = _exchange("gather_small_grads", [("gather", small)])

    def update(name, parts, w, m, v):
        n_layers = len(parts)
        rows = int(np.prod(w.shape[:-1]))
        cols = w.shape[-1]
        parts = [p.reshape(NDEV, rows // n_layers, cols) for p in parts]
        outs = _adamw(name, parts, w.reshape(rows, cols), m.reshape(rows, cols), v.reshape(rows, cols))
        return [o.reshape(w.shape) for o in outs]

    res = {}
    r_in2_1, r_out2_1 = _exchange_wait("ffn2_1_reduce_wait", rs_ffn2[1], dh)
    r_in1_1, r_out1_1, r_o_b, r_uq, r_dq = _exchange_wait("ffn1_1_reduce_wait", rs_ffn1[1], dh)
    r_in2_0, r_out2_0, r_up, r_down = _exchange_wait("ffn2_0_reduce_wait", rs_ffn2[0], dh)
    res["ffn2_w_in"] = update("adamw_ffn2_w_in", [r_in2_0, r_in2_1], ffn2_w_in, m_ffn2_w_in, v_ffn2_w_in)
    res["ffn2_w_out"] = update("adamw_ffn2_w_out", [r_out2_0, r_out2_1], ffn2_w_out, m_ffn2_w_out, v_ffn2_w_out)
    res["kv_w_down"] = update("adamw_kv_w_down", [r_down], kv_w_down, m_kv_w_down, v_kv_w_down)
    res["kv_w_up"] = update("adamw_kv_w_up", [r_up], kv_w_up, m_kv_w_up, v_kv_w_up)
    res["b_w_dq"] = update("adamw_b_w_dq", [r_dq], b_w_dq, m_b_w_dq, v_b_w_dq)
    res["b_w_uq"] = update("adamw_b_w_uq", [r_uq], b_w_uq, m_b_w_uq, v_b_w_uq)
    res["b_w_o"] = update("adamw_b_w_o", [r_o_b], b_w_o, m_b_w_o, v_b_w_o)
    r_o_a, r_qkv = _exchange_wait("attn_a_reduce_wait", rs_attn_a, res["ffn2_w_in"][0])
    res["a_w_qkv"] = update("adamw_a_w_qkv", [r_qkv], a_w_qkv, m_a_w_qkv, v_a_w_qkv)
    res["a_w_o"] = update("adamw_a_w_o", [r_o_a], a_w_o, m_a_w_o, v_a_w_o)
    r_in1_0, r_out1_0 = _exchange_wait("ffn1_0_reduce_wait", rs_ffn1[0], res["a_w_qkv"][0])
    res["ffn1_w_in"] = update("adamw_ffn1_w_in", [r_in1_0, r_in1_1], ffn1_w_in, m_ffn1_w_in, v_ffn1_w_in)
    res["ffn1_w_out"] = update("adamw_ffn1_w_out", [r_out1_0, r_out1_1], ffn1_w_out, m_ffn1_w_out, v_ffn1_w_out)
    zero_row = jnp.zeros((dn,), F32)
    packs = [
        _pack_small(f1, mx, f2, kvn, fin, qn, lat, rel, zero_row)
        for f1, mx, f2, kvn, fin, qn, lat, rel in (
            (ffn1_norm, mix_norm, ffn2_norm, kv_norm, final_norm, b_q_norm, kv_latent_norm, a_rel_bias),
            (m_ffn1_norm, m_mix_norm, m_ffn2_norm, m_kv_norm, m_final_norm, m_b_q_norm, m_kv_latent_norm, m_a_rel_bias),
            (v_ffn1_norm, v_mix_norm, v_ffn2_norm, v_kv_norm, v_final_norm, v_b_q_norm, v_kv_latent_norm, v_a_rel_bias),
        )
    ]
    small_out = [_unpack_small(o) for o in _adamw("adamw_small", [r_small], *packs)]
    for name in ("ffn1_norm", "mix_norm", "ffn2_norm", "a_rel_bias", "kv_norm", "kv_latent_norm", "b_q_norm", "final_norm"):
        res[name] = [so[name] for so in small_out]
    loss = small_out[0]["last"][0]

    order = [
        "ffn1_norm", "ffn1_w_in", "ffn1_w_out", "mix_norm", "ffn2_norm", "ffn2_w_in", "ffn2_w_out", "a_w_qkv", "a_rel_bias",
        "a_w_o", "kv_norm", "kv_w_down", "kv_latent_norm", "kv_w_up", "b_w_dq", "b_q_norm", "b_w_uq", "b_w_o", "final_norm",
    ]
    return (loss, grad_x, *[res[n][0] for n in order], *[res[n][1] for n in order], *[res[n][2] for n in order], *[res[n][3] for n in order])
```

```python
import functools

import jax
import jax.numpy as jnp
import numpy as np
from jax import lax
from jax.experimental import pallas as pl
from jax.experimental.pallas import tpu as pltpu

NDEV = 8
D_MODEL = 1024
D_FF = 2816
FB = 2 * D_FF // NDEV
NJ = D_FF // FB
CHUNK = 64
LEFT_CHUNKS = 8
PAD = LEFT_CHUNKS * CHUNK
BAND = PAD + CHUNK
CHUNKS_PER_STEP = 4
WINDOW = PAD + CHUNKS_PER_STEP * CHUNK
MAX_REL = 128
NREL = 2 * MAX_REL + 1
NREL_PAD = 384
HEADS_A = 16
HEADS_B = 8
NOPE = 128
ROPE = 64
QK_B = NOPE + ROPE
V_DIM = 128
Q_LORA = 768
KV_LORA = 256
ROPE_THETA = 10000.0
EPS = 1e-6
NEG_INF = -1e30
MLA_TQ = 256
ADAM_LR = 0.001
ADAM_B1 = 0.9
ADAM_B2 = 0.999
ADAM_EPS = 1e-08
ADAM_WD = 0.01
ADAM_STEP = 10
PACK_ROWS = 16
VMEM_LIMIT_BYTES = 56 * 1024 * 1024

F32 = jnp.float32
BF16 = jnp.bfloat16
SDS = jax.ShapeDtypeStruct
BS = pl.BlockSpec
HIGHEST = lax.Precision.HIGHEST
MESH = pl.DeviceIdType.MESH


def _cparams(n_axes):
    return pltpu.CompilerParams(dimension_semantics=("arbitrary",) * n_axes, vmem_limit_bytes=VMEM_LIMIT_BYTES)


def _tile(t, want=512):
    return want if t % want == 0 else t


def _dot(a, b):
    return jnp.dot(a, b, preferred_element_type=F32)


def _dot_nt(a, b):
    return lax.dot_general(a, b, (((1,), (1,)), ((), ())), preferred_element_type=F32)


def _dot_tn(a, b):
    return lax.dot_general(a, b, (((0,), (0,)), ((), ())), preferred_element_type=F32)


def _dot_exact(a, b):
    return jnp.dot(a, b, precision=HIGHEST, preferred_element_type=F32)


def _rms_scale(h):
    return lax.rsqrt(jnp.mean(h * h, axis=-1, keepdims=True) + EPS)


def _acc_rows(ref, val, step, n_steps):
    part = val.reshape(val.shape[0] // 8, 8, val.shape[1]).sum(axis=0)

    @pl.when(step == 0)
    def _():
        ref[...] = part

    @pl.when(step > 0)
    def _():
        ref[...] += part

    @pl.when(step == n_steps - 1)
    def _():
        ref[...] = jnp.broadcast_to(jnp.sum(ref[...], axis=0, keepdims=True), ref.shape)


def _exchange_plan(entries):
    ins = [e[1] for e in entries]
    kinds = [e[0] for e in entries]
    lands = [SDS((NDEV,) + a.shape if k == "gather" else a.shape, a.dtype) for k, a in zip(kinds, ins)]
    return ins, lands, kinds


def _mesh_place():
    x, y, c = lax.axis_index("x"), lax.axis_index("y"), lax.axis_index("c")
    return (x, y, c), 4 * x + 2 * y + c


def _flipped(place, p):
    x, y, c = place
    px = 1 - x if p & 4 else x
    py = 1 - y if p & 2 else y
    pc = 1 - c if p & 1 else c
    return (px, py, pc), 4 * px + 2 * py + pc


def _ends(kind, src_ref, land_ref, origin, target):
    if kind == "gather":
        return src_ref, land_ref.at[origin]
    return src_ref.at[target], land_ref.at[origin]


def _remote(kind, src_ref, land_ref, send_sems, recv_sems, k, p, place, me, arriving):
    peer_pos, peer = _flipped(place, p)
    src, dst = _ends(kind, src_ref, land_ref, me, peer)
    if arriving:
        dst = _ends(kind, src_ref, land_ref, peer, me)[1]
    sem = k * (NDEV - 1) + p - 1
    return pltpu.make_async_remote_copy(
        src_ref=src, dst_ref=dst, send_sem=send_sems.at[sem], recv_sem=recv_sems.at[sem], device_id=peer_pos, device_id_type=MESH,
    )


def _exchange(name, entries):
    ins, lands, kinds = _exchange_plan(entries)
    n = len(ins)

    def body(*refs):
        in_refs, land_refs = refs[:n], refs[n : 2 * n]
        send_sems, recv_sems, local_sems = refs[2 * n :]
        place, me = _mesh_place()
        local = []
        for k in range(n):
            src, dst = _ends(kinds[k], in_refs[k], land_refs[k], me, me)
            local.append(pltpu.make_async_copy(src, dst, local_sems.at[k]))
            local[-1].start()
        sends = []
        for p in range(1, NDEV):
            for k in range(n):
                sends.append(_remote(kinds[k], in_refs[k], land_refs[k], send_sems, recv_sems, k, p, place, me, False))
                sends[-1].start()
        for p in range(1, NDEV):
            for k in range(n):
                _remote(kinds[k], in_refs[k], land_refs[k], send_sems, recv_sems, k, p, place, me, True).wait_recv()
        for cp in sends:
            cp.wait_send()
        for cp in local:
            cp.wait()

    any_spec = BS(memory_space=pl.ANY)
    return pl.pallas_call(
        body, name=name, out_shape=lands, in_specs=[any_spec] * n, out_specs=[any_spec] * n,
        scratch_shapes=[
            pltpu.SemaphoreType.DMA((n * (NDEV - 1),)), pltpu.SemaphoreType.DMA((n * (NDEV - 1),)), pltpu.SemaphoreType.DMA((n,)),
        ],
    )(*ins)


HBM_SPEC = BS(memory_space=pltpu.HBM)
SEM_SPEC = BS(memory_space=pltpu.SEMAPHORE)
DATAFLOW = pltpu.SideEffectType.DATAFLOW_SIDE_EFFECTING


def _exchange_start(name, groups):
    plans = [_exchange_plan(g) for g in groups]
    ins = [a for plan in plans for a in plan[0]]
    lands = [s for plan in plans for s in plan[1]]
    kinds = [kind for plan in plans for kind in plan[2]]
    n_in, n_groups = len(ins), len(groups)

    def body(*refs):
        in_refs, land_refs = refs[:n_in], refs[n_in : 2 * n_in]
        sems = refs[2 * n_in : 2 * n_in + 2 * n_groups]
        token = refs[4 * n_in + 2 * n_groups]
        local_sems = refs[4 * n_in + 2 * n_groups + 1]
        place, me = _mesh_place()
        local = []
        for k, kind in enumerate(kinds):
            src, dst = _ends(kind, in_refs[k], land_refs[k], me, me)
            local.append(pltpu.make_async_copy(src, dst, local_sems.at[k]))
            local[-1].start()
        base = 0
        for g, plan in enumerate(plans):
            for p in range(1, NDEV):
                for k, kind in enumerate(plan[2]):
                    _remote(kind, in_refs[base + k], land_refs[base + k], sems[2 * g], sems[2 * g + 1], k, p, place, me, False).start()
            base += len(plan[2])
        for cp in local:
            cp.wait()
        token[...] = jnp.zeros_like(token)

    sem_shapes = []
    for plan in plans:
        sem_shapes += [pltpu.SemaphoreType.DMA((len(plan[2]) * (NDEV - 1),))] * 2
    outs = pl.pallas_call(
        body, name=name,
        out_shape=sem_shapes + [pltpu.HBM(a.shape, a.dtype) for a in ins] + [pltpu.HBM(s.shape, s.dtype) for s in lands] + [SDS((8, 128), F32)],
        in_specs=[HBM_SPEC] * (2 * n_in),
        out_specs=[SEM_SPEC] * (2 * n_groups) + [HBM_SPEC] * (2 * n_in) + [BS(memory_space=pltpu.VMEM)],
        input_output_aliases={i: 2 * n_groups + i for i in range(2 * n_in)},
        scratch_shapes=[pltpu.SemaphoreType.DMA((n_in,))],
        compiler_params=pltpu.CompilerParams(has_side_effects=DATAFLOW),
    )(
        *[pltpu.with_memory_space_constraint(a, pltpu.HBM) for a in ins],
        *[pltpu.with_memory_space_constraint(lax.empty(s.shape, s.dtype), pltpu.HBM) for s in lands],
    )
    sems, srcs, landed, token = outs[: 2 * n_groups], outs[2 * n_groups : 2 * n_groups + n_in], outs[2 * n_groups + n_in : -1], outs[-1]
    started, base = [], 0
    for g, plan in enumerate(plans):
        n = len(plan[2])
        started.append((sems[2 * g], sems[2 * g + 1], srcs[base : base + n], landed[base : base + n], plan[2]))
        base += n
    return started, token


def _exchange_wait(name, started, after):
    send_sems, recv_sems, srcs, landed, kinds = started
    n = len(kinds)

    def body(*refs):
        in_refs, land_refs = refs[:n], refs[n : 2 * n]
        send_ref, recv_ref = refs[2 * n], refs[2 * n + 1]
        place, me = _mesh_place()
        for p in range(1, NDEV):
            for k in range(n):
                _remote(kinds[k], in_refs[k], land_refs[k], send_ref, recv_ref, k, p, place, me, True).wait_recv()
        for p in range(1, NDEV):
            for k in range(n):
                _remote(kinds[k], in_refs[k], land_refs[k], send_ref, recv_ref, k, p, place, me, False).wait_send()

    outs = pl.pallas_call(
        body, name=name,
        out_shape=[pltpu.HBM(a.shape, a.dtype) for a in srcs] + [pltpu.HBM(a.shape, a.dtype) for a in landed],
        in_specs=[HBM_SPEC] * (2 * n) + [SEM_SPEC, SEM_SPEC, BS(memory_space=pl.ANY)],
        out_specs=[HBM_SPEC] * (2 * n),
        input_output_aliases={i: i for i in range(2 * n)},
        compiler_params=pltpu.CompilerParams(has_side_effects=DATAFLOW),
    )(*srcs, *landed, send_sems, recv_sems, after)
    return outs[n:]


def _dep_spec(n_axes):
    return BS((8, 128), (lambda i: (0, 0)) if n_axes == 1 else (lambda i, j: (0, 0)))


def _norm_fwd(name, h, gammas, dep):
    t, dn = h.shape
    ng = gammas.shape[0]
    tm = _tile(t)

    def body(h_ref, g_ref, dep_ref, *outs):
        hv = h_ref[...]
        hh = hv * _rms_scale(hv)
        for i, o_ref in enumerate(outs):
            o_ref[...] = (hh * g_ref[i : i + 1, :]).astype(BF16)

    row = BS((tm, dn), lambda i: (i, 0))
    return pl.pallas_call(
        body, name=name, grid=(t // tm,),
        in_specs=[row, BS((ng, dn), lambda i: (0, 0)), _dep_spec(1)],
        out_specs=[row] * ng, out_shape=[SDS((t, dn), BF16)] * ng,
        compiler_params=_cparams(1),
    )(h, gammas, dep)


def _ffn_in(name, n, w_in, layer):
    t, dn = n.shape
    tm = _tile(t)

    def body(n_ref, wg_ref, wu_ref, gu_ref, a_ref):
        xv = n_ref[...]
        g = _dot(xv, wg_ref[...])
        u = _dot(xv, wu_ref[...])
        gu_ref[0] = g
        gu_ref[1] = u
        a_ref[...] = (g * jax.nn.sigmoid(g) * u).astype(BF16)

    return pl.pallas_call(
        body, name=name, grid=(NJ, t // tm),
        in_specs=[
            BS((tm, dn), lambda j, i: (i, 0)),
            BS((None, None, dn, FB), lambda j, i: (layer, j, 0, 0)),
            BS((None, None, dn, FB), lambda j, i: (layer, j + NJ, 0, 0)),
        ],
        out_specs=[BS((None, 2, tm, FB), lambda j, i: (j, 0, i, 0)), BS((None, tm, FB), lambda j, i: (j, i, 0))],
        out_shape=[SDS((NJ, 2, t, FB), F32), SDS((NJ, t, FB), BF16)],
        compiler_params=_cparams(2),
    )(n, w_in, w_in)


def _mm_res_norm(name, a, w, layer, h_in, gammas, scale):
    nk, t, kb = a.shape
    dn = w.shape[-1]
    ng = 0 if gammas is None else gammas.shape[0]
    tm = _tile(t)

    def body(*refs):
        a_ref, w_ref, h_ref = refs[:3]
        g_ref = refs[3] if ng else None
        outs = refs[3 + (1 if ng else 0) :]
        acc = _dot(a_ref[0], w_ref[0])
        for k in range(1, nk):
            acc += _dot(a_ref[k], w_ref[k])
        ho = h_ref[...] + scale * acc
        outs[0][...] = ho
        if ng:
            hh = ho * _rms_scale(ho)
            for i in range(ng):
                outs[1 + i][...] = (hh * g_ref[i : i + 1, :]).astype(BF16)

    row = BS((tm, dn), lambda i: (i, 0))
    in_specs = [BS((nk, tm, kb), lambda i: (0, i, 0)), BS((None, nk, kb, dn), lambda i: (layer, 0, 0, 0)), row]
    args = [a, w, h_in]
    if ng:
        in_specs.append(BS((ng, dn), lambda i: (0, 0)))
        args.append(gammas)
    return pl.pallas_call(
        body, name=name, grid=(t // tm,),
        in_specs=in_specs,
        out_specs=[row] * (1 + ng), out_shape=[SDS((t, dn), F32)] + [SDS((t, dn), BF16)] * ng,
        compiler_params=_cparams(1),
    )(*args)


def _qkv_proj(name, hn, w_qkv):
    t, dn = hn.shape
    wb = w_qkv.shape[-1]
    per = wb // 128
    tm = _tile(t)

    def body(x_ref, w_ref, o_ref):
        xv = x_ref[...]
        for j in range(NDEV):
            yv = _dot(xv, w_ref[j]).astype(BF16)
            for i in range(per):
                n = per * j + i
                o_ref[n // 8, :, (n % 8) * 128 : (n % 8 + 1) * 128] = yv[:, i * 128 : (i + 1) * 128]

    return pl.pallas_call(
        body, name=name, grid=(t // tm,),
        in_specs=[BS((tm, dn), lambda i: (i, 0)), BS((NDEV, dn, wb), lambda i: (0, 0, 0))],
        out_specs=BS((3, tm, dn), lambda i: (0, i, 0)), out_shape=SDS((3, t, dn), BF16),
        compiler_params=_cparams(1),
    )(hn, w_qkv)


def _rel_onehot(i):
    r = lax.broadcasted_iota(jnp.int32, (NREL_PAD, BAND), 0)
    j = lax.broadcasted_iota(jnp.int32, (NREL_PAD, BAND), 1)
    idx = jnp.clip(PAD + i - j, -MAX_REL, MAX_REL) + MAX_REL
    return (idx == r).astype(F32)


def _rel_bias_fwd(table):
    def body(t_ref, o_ref):
        i8 = pl.program_id(0)
        for ii in range(8):
            o_ref[:, ii, :] = _dot_exact(t_ref[...], _rel_onehot(i8 * 8 + ii))

    return pl.pallas_call(
        body, name="rel_bias_fwd", grid=(CHUNK // 8,),
        in_specs=[BS((HEADS_A, NREL_PAD), lambda i: (0, 0))],
        out_specs=BS((HEADS_A, 8, BAND), lambda i: (0, i, 0)), out_shape=SDS((HEADS_A, CHUNK, BAND), F32),
        compiler_params=_cparams(1),
    )(table)


def _rel_bias_bwd(dbias):
    def body(d_ref, o_ref):
        i8 = pl.program_id(0)
        acc = jnp.zeros((HEADS_A, NREL_PAD), F32)
        for ii in range(8):
            acc += lax.dot_general(
                d_ref[:, ii, :], _rel_onehot(i8 * 8 + ii), (((1,), (1,)), ((), ())), precision=HIGHEST,
                preferred_element_type=F32,
            )

        @pl.when(i8 == 0)
        def _():
            o_ref[...] = acc

        @pl.when(i8 > 0)
        def _():
            o_ref[...] += acc

    return pl.pallas_call(
        body, name="rel_bias_bwd", grid=(CHUNK // 8,),
        in_specs=[BS((HEADS_A, 8, BAND), lambda i: (0, i, 0))],
        out_specs=BS((HEADS_A, NREL_PAD), lambda i: (0, 0)), out_shape=SDS((HEADS_A, NREL_PAD), F32),
        compiler_params=_cparams(1),
    )(dbias)


def _chunk_scores(qm, kb, bias, valid):
    s = _dot_nt(qm, kb) * (CHUNK ** -0.5) + bias
    s = jnp.where(valid, s, NEG_INF)
    e = jnp.exp(s - jnp.max(s, axis=-1, keepdims=True))
    return e / jnp.sum(e, axis=-1, keepdims=True)


def _attn_a_fwd(qkv3, bias, bl, seq):
    t, dn = qkv3.shape[1:]
    npair = dn // 128
    nchunk = seq // CHUNK

    def body(q_ref, k_ref, v_ref, b_ref, o_ref, kpad, vpad):
        kpad[0:PAD, :] = jnp.zeros((PAD, 128), BF16)
        vpad[0:PAD, :] = jnp.zeros((PAD, 128), BF16)
        kpad[PAD:, :] = k_ref[...]
        vpad[PAD:, :] = v_ref[...]
        lane = lax.broadcasted_iota(jnp.int32, (CHUNK, 128), 1)
        col = lax.broadcasted_iota(jnp.int32, (CHUNK, BAND), 1)

        def chunks(it, carry):
            r0 = pl.multiple_of(it * (CHUNKS_PER_STEP * CHUNK), CHUNKS_PER_STEP * CHUNK)
            qs = q_ref[pl.ds(r0, CHUNKS_PER_STEP * CHUNK), :]
            kwin = kpad[pl.ds(r0, WINDOW), :]
            vwin = vpad[pl.ds(r0, WINDOW), :]
            outs = []
            for cc in range(CHUNKS_PER_STEP):
                qc = qs[cc * CHUNK : (cc + 1) * CHUNK]
                kb = kwin[cc * CHUNK : cc * CHUNK + BAND]
                vb = vwin[cc * CHUNK : cc * CHUNK + BAND]
                valid = col >= PAD - (it * CHUNKS_PER_STEP + cc) * CHUNK
                heads = []
                for hh in range(2):
                    in_head = (lane < 64) if hh == 0 else (lane >= 64)
                    qm = jnp.where(in_head, qc, jnp.zeros_like(qc))
                    p = _chunk_scores(qm, kb, b_ref[hh], valid)
                    heads.append(_dot(p.astype(BF16), vb))
                outs.append(jnp.where(lane < 64, heads[0], heads[1]).astype(BF16))
            o_ref[pl.ds(r0, CHUNKS_PER_STEP * CHUNK), :] = jnp.concatenate(outs, axis=0)
            return carry

        lax.fori_loop(0, nchunk // CHUNKS_PER_STEP, chunks, 0)

    return pl.pallas_call(
        body, name="attn_a_fwd", grid=(bl, npair),
        in_specs=[
            BS((None, seq, 128), lambda b, h: (0, b, h)),
            BS((None, seq, 128), lambda b, h: (1, b, h)),
            BS((None, seq, 128), lambda b, h: (2, b, h)),
            BS((2, CHUNK, BAND), lambda b, h: (h, 0, 0)),
        ],
        out_specs=BS((seq, 128), lambda b, h: (b, h)), out_shape=SDS((t, dn), BF16),
        scratch_shapes=[pltpu.VMEM((PAD + seq, 128), BF16), pltpu.VMEM((PAD + seq, 128), BF16)],
        compiler_params=_cparams(2),
    )(qkv3, qkv3, qkv3, bias)


def _attn_a_bwd(qkv3, do, bias, bl, seq):
    t, dn = qkv3.shape[1:]
    npair = dn // 128
    nchunk = seq // CHUNK

    def body(q_ref, k_ref, v_ref, do_ref, b_ref, dqkv_ref, db_ref, kpad, vpad, dkacc, dvacc):
        b = pl.program_id(1)
        kpad[0:PAD, :] = jnp.zeros((PAD, 128), BF16)
        vpad[0:PAD, :] = jnp.zeros((PAD, 128), BF16)
        kpad[PAD:, :] = k_ref[...]
        vpad[PAD:, :] = v_ref[...]
        dkacc[...] = jnp.zeros_like(dkacc)
        dvacc[...] = jnp.zeros_like(dvacc)

        @pl.when(b == 0)
        def _():
            db_ref[...] = jnp.zeros_like(db_ref)

        lane = lax.broadcasted_iota(jnp.int32, (CHUNK, 128), 1)
        col = lax.broadcasted_iota(jnp.int32, (CHUNK, BAND), 1)

        def chunks(it, carry):
            r0 = pl.multiple_of(it * (CHUNKS_PER_STEP * CHUNK), CHUNKS_PER_STEP * CHUNK)
            qs = q_ref[pl.ds(r0, CHUNKS_PER_STEP * CHUNK), :]
            dos = do_ref[pl.ds(r0, CHUNKS_PER_STEP * CHUNK), :]
            kwin = kpad[pl.ds(r0, WINDOW), :]
            vwin = vpad[pl.ds(r0, WINDOW), :]
            dq_rows, dk_bands, dv_bands = [], [], []
            db = [None, None]
            for cc in range(CHUNKS_PER_STEP):
                qc = qs[cc * CHUNK : (cc + 1) * CHUNK]
                doc = dos[cc * CHUNK : (cc + 1) * CHUNK]
                kb = kwin[cc * CHUNK : cc * CHUNK + BAND]
                vb = vwin[cc * CHUNK : cc * CHUNK + BAND]
                valid = col >= PAD - (it * CHUNKS_PER_STEP + cc) * CHUNK
                dqs = []
                dk_band = None
                for hh in range(2):
                    in_head = (lane < 64) if hh == 0 else (lane >= 64)
                    qm = jnp.where(in_head, qc, jnp.zeros_like(qc))
                    dom = jnp.where(in_head, doc, jnp.zeros_like(doc))
                    p = _chunk_scores(qm, kb, b_ref[hh], valid)
                    dp = _dot_nt(dom, vb)
                    ds = p * (dp - jnp.sum(p * dp, axis=-1, keepdims=True))
                    db[hh] = ds if db[hh] is None else db[hh] + ds
                    dsb = (ds * (CHUNK ** -0.5)).astype(BF16)
                    dqs.append(_dot(dsb, kb))
                    dk_h, dv_h = _dot_tn(dsb, qm), _dot_tn(p.astype(BF16), dom)
                    dk_band, dv_band = (dk_h, dv_h) if dk_band is None else (dk_band + dk_h, dv_band + dv_h)
                dq_rows.append(jnp.where(lane < 64, dqs[0], dqs[1]).astype(BF16))
                dk_bands.append(dk_band)
                dv_bands.append(dv_band)
            dqkv_ref[0, pl.ds(r0, CHUNKS_PER_STEP * CHUNK), :] = jnp.concatenate(dq_rows, axis=0)
            for hh in range(2):
                db_ref[hh] += db[hh]
            for cc in range(CHUNKS_PER_STEP):
                band0 = pl.multiple_of(r0 + cc * CHUNK, CHUNK)
                dkacc[pl.ds(band0, BAND), :] += dk_bands[cc]
                dvacc[pl.ds(band0, BAND), :] += dv_bands[cc]
            return carry

        lax.fori_loop(0, nchunk // CHUNKS_PER_STEP, chunks, 0)
        dqkv_ref[1] = dkacc[PAD:, :].astype(BF16)
        dqkv_ref[2] = dvacc[PAD:, :].astype(BF16)

    return pl.pallas_call(
        body, name="attn_a_bwd", grid=(npair, bl),
        in_specs=[
            BS((None, seq, 128), lambda h, b: (0, b, h)),
            BS((None, seq, 128), lambda h, b: (1, b, h)),
            BS((None, seq, 128), lambda h, b: (2, b, h)),
            BS((seq, 128), lambda h, b: (b, h)),
            BS((2, CHUNK, BAND), lambda h, b: (h, 0, 0)),
        ],
        out_specs=[BS((3, seq, 128), lambda h, b: (0, b, h)), BS((2, CHUNK, BAND), lambda h, b: (h, 0, 0))],
        out_shape=[SDS((3, t, dn), BF16), SDS((HEADS_A, CHUNK, BAND), F32)],
        scratch_shapes=[
            pltpu.VMEM((PAD + seq, 128), BF16), pltpu.VMEM((PAD + seq, 128), BF16),
            pltpu.VMEM((PAD + seq, 128), F32), pltpu.VMEM((PAD + seq, 128), F32),
        ],
        compiler_params=_cparams(2),
    )(qkv3, qkv3, qkv3, do, bias)


def _rope_tables(seq):
    half = ROPE // 2
    freqs = ROPE_THETA ** (-jnp.arange(half, dtype=F32) / half)
    ang = jnp.arange(seq, dtype=F32)[:, None] * freqs[None, :]
    cos, sin = jnp.cos(ang), jnp.sin(ang)
    c64 = jnp.concatenate([cos, cos], axis=1)
    s64 = jnp.concatenate([-sin, sin], axis=1)
    c192 = jnp.concatenate([jnp.ones((seq, NOPE), F32), c64], axis=1)
    s192 = jnp.concatenate([jnp.zeros((seq, NOPE), F32), s64], axis=1)
    p64 = np.zeros((ROPE, ROPE), np.float32)
    for col in range(ROPE):
        p64[(col + half) % ROPE, col] = 1.0
    p192 = np.zeros((QK_B, QK_B), np.float32)
    p192[NOPE:, NOPE:] = p64
    return c64, s64, jnp.asarray(p64), c192, s192, jnp.asarray(p192)


def _rope(xv, cos, sin_signed, swap):
    return xv * cos + _dot_exact(xv, swap) * sin_signed


def _rope_bwd(dy, cos, sin_signed, swap):
    return dy * cos + _dot_exact(dy * sin_signed, swap)


def _q_down(hn, w_dq, q_norm):
    t, dn = hn.shape
    ql = w_dq.shape[1]
    tm = _tile(t)

    def body(x_ref, w_ref, g_ref, pre_ref, cq_ref):
        pre = _dot(x_ref[...], w_ref[...])
        pre_ref[...] = pre
        cq_ref[...] = (pre * _rms_scale(pre) * g_ref[...]).astype(BF16)

    return pl.pallas_call(
        body, name="q_down", grid=(t // tm,),
        in_specs=[BS((tm, dn), lambda i: (i, 0)), BS((dn, ql), lambda i: (0, 0)), BS((1, ql), lambda i: (0, 0))],
        out_specs=[BS((tm, ql), lambda i: (i, 0))] * 2, out_shape=[SDS((t, ql), F32), SDS((t, ql), BF16)],
        compiler_params=_cparams(1),
    )(hn, w_dq, q_norm)


def _q_up(cq, w_uq, c192, s192, p192, seq):
    t, ql = cq.shape
    tm = _tile(min(seq, 512), min(seq, 512))
    nseq = seq // tm

    def body(x_ref, w_ref, c_ref, s_ref, p_ref, o_ref):
        qf = _dot(x_ref[...], w_ref[...])
        o_ref[...] = _rope(qf, c_ref[...], s_ref[...], p_ref[...]).astype(BF16)

    pos = BS((tm, QK_B), lambda h, i: (i % nseq, 0))
    return pl.pallas_call(
        body, name="q_up", grid=(HEADS_B, t // tm),
        in_specs=[
            BS((tm, ql), lambda h, i: (i, 0)), BS((None, ql, QK_B), lambda h, i: (h, 0, 0)), pos, pos,
            BS((QK_B, QK_B), lambda h, i: (0, 0)),
        ],
        out_specs=BS((None, tm, QK_B), lambda h, i: (h, i, 0)), out_shape=SDS((HEADS_B, t, QK_B), BF16),
        compiler_params=_cparams(2),
    )(cq, w_uq, c192, s192, p192)


def _kv_down(hk, w_down, latent_norm, c64, s64, p64, seq):
    t, dn = hk.shape
    wd = w_down.shape[1]
    tm = _tile(min(seq, 512), min(seq, 512))
    nseq = seq // tm

    def body(x_ref, w_ref, g_ref, c_ref, s_ref, p_ref, ckr_ref, ckv_ref, kr_ref):
        ckr = _dot(x_ref[...], w_ref[...])
        ckr_ref[...] = ckr
        lat = ckr[:, :KV_LORA]
        ckv_ref[...] = (lat * _rms_scale(lat) * g_ref[...]).astype(BF16)
        kr_ref[...] = _rope(ckr[:, KV_LORA:], c_ref[...], s_ref[...], p_ref[...]).astype(BF16)

    pos = BS((tm, ROPE), lambda i: (i % nseq, 0))
    return pl.pallas_call(
        body, name="kv_down", grid=(t // tm,),
        in_specs=[
            BS((tm, dn), lambda i: (i, 0)), BS((dn, wd), lambda i: (0, 0)), BS((1, KV_LORA), lambda i: (0, 0)), pos, pos,
            BS((ROPE, ROPE), lambda i: (0, 0)),
        ],
        out_specs=[BS((tm, wd), lambda i: (i, 0)), BS((tm, KV_LORA), lambda i: (i, 0)), BS((tm, ROPE), lambda i: (i, 0))],
        out_shape=[SDS((t, wd), F32), SDS((t, KV_LORA), BF16), SDS((t, ROPE), BF16)],
        compiler_params=_cparams(1),
    )(hk, w_down, latent_norm, c64, s64, p64)


def _kv_up(ckv, w_up):
    t, kl = ckv.shape
    hb = w_up.shape[-1]
    tm = _tile(t)

    def body(x_ref, w_ref, o_ref):
        o_ref[...] = _dot(x_ref[...], w_ref[...]).astype(BF16)

    return pl.pallas_call(
        body, name="kv_up", grid=(HEADS_B, t // tm),
        in_specs=[BS((tm, kl), lambda h, i: (i, 0)), BS((None, kl, hb), lambda h, i: (h, 0, 0))],
        out_specs=BS((tm, hb), lambda h, i: (i, h)), out_shape=SDS((t, HEADS_B * hb), BF16),
        compiler_params=_cparams(2),
    )(ckv, w_up)


def _mla_probs(qi, kcat, row0, n_keys):
    s = _dot_nt(qi, kcat) * (QK_B ** -0.5)
    rows = lax.broadcasted_iota(jnp.int32, (qi.shape[0], n_keys), 0) + row0
    cols = lax.broadcasted_iota(jnp.int32, (qi.shape[0], n_keys), 1)
    s = jnp.where(jnp.right_shift(cols, 6) <= jnp.right_shift(rows, 6), s, NEG_INF)
    e = jnp.exp(s - jnp.max(s, axis=-1, keepdims=True))
    return e / jnp.sum(e, axis=-1, keepdims=True)


def _mla_fwd(q, kv, kr, bl, seq):
    t = kv.shape[0]
    tq = min(MLA_TQ, seq)

    def body(q_ref, kn_ref, v_ref, kr_ref, o_ref):
        kcat = jnp.concatenate([kn_ref[...], kr_ref[...]], axis=1)
        vv = v_ref[...]
        for i in range(seq // tq):
            n_keys = (i + 1) * tq
            p = _mla_probs(q_ref[i * tq : (i + 1) * tq, :], kcat[:n_keys], i * tq, n_keys)
            o_ref[i * tq : (i + 1) * tq, :] = _dot(p.astype(BF16), vv[:n_keys]).astype(BF16)

    return pl.pallas_call(
        body, name="mla_fwd", grid=(bl, HEADS_B),
        in_specs=[
            BS((None, seq, QK_B), lambda b, h: (h, b, 0)),
            BS((seq, NOPE), lambda b, h: (b, 2 * h)),
            BS((seq, V_DIM), lambda b, h: (b, 2 * h + 1)),
            BS((seq, ROPE), lambda b, h: (b, 0)),
        ],
        out_specs=BS((seq, V_DIM), lambda b, h: (b, h)), out_shape=SDS((t, HEADS_B * V_DIM), BF16),
        compiler_params=_cparams(2),
    )(q, kv, kv, kr)


def _mla_bwd(q, kv, kr, do, c192, s192, p192, bl, seq):
    t = kv.shape[0]
    tq = min(MLA_TQ, seq)

    def body(q_ref, kn_ref, v_ref, kr_ref, do_ref, c_ref, s_ref, p_ref, dq_ref, dkv_ref, dkr_ref, dkacc, dvacc):
        h = pl.program_id(1)
        kcat = jnp.concatenate([kn_ref[...], kr_ref[...]], axis=1)
        vv = v_ref[...]
        dkacc[...] = jnp.zeros_like(dkacc)
        dvacc[...] = jnp.zeros_like(dvacc)
        for i in range(seq // tq):
            n_keys = (i + 1) * tq
            rows = slice(i * tq, (i + 1) * tq)
            qi = q_ref[rows, :]
            doi = do_ref[rows, :]
            p = _mla_probs(qi, kcat[:n_keys], i * tq, n_keys)
            dp = _dot_nt(doi, vv[:n_keys])
            ds = p * (dp - jnp.sum(p * dp, axis=-1, keepdims=True))
            dsb = (ds * (QK_B ** -0.5)).astype(BF16)
            dq = _dot(dsb, kcat[:n_keys])
            dq_ref[rows, :] = _rope_bwd(dq, c_ref[rows, :], s_ref[rows, :], p_ref[...]).astype(BF16)
            dkacc[0:n_keys, :] += _dot_tn(dsb, qi)
            dvacc[0:n_keys, :] += _dot_tn(p.astype(BF16), doi)
        dk = dkacc[...]
        dkv_ref[:, :NOPE] = dk[:, :NOPE].astype(BF16)
        dkv_ref[:, NOPE:] = dvacc[...].astype(BF16)

        @pl.when(h == 0)
        def _():
            dkr_ref[...] = dk[:, NOPE:]

        @pl.when(h > 0)
        def _():
            dkr_ref[...] += dk[:, NOPE:]

    return pl.pallas_call(
        body, name="mla_bwd", grid=(bl, HEADS_B),
        in_specs=[
            BS((None, seq, QK_B), lambda b, h: (h, b, 0)),
            BS((seq, NOPE), lambda b, h: (b, 2 * h)),
            BS((seq, V_DIM), lambda b, h: (b, 2 * h + 1)),
            BS((seq, ROPE), lambda b, h: (b, 0)),
            BS((seq, V_DIM), lambda b, h: (b, h)),
            BS((seq, QK_B), lambda b, h: (0, 0)),
            BS((seq, QK_B), lambda b, h: (0, 0)),
            BS((QK_B, QK_B), lambda b, h: (0, 0)),
        ],
        out_specs=[
            BS((None, seq, QK_B), lambda b, h: (h, b, 0)),
            BS((seq, NOPE + V_DIM), lambda b, h: (b, h)),
            BS((seq, ROPE), lambda b, h: (b, 0)),
        ],
        out_shape=[SDS((HEADS_B, t, QK_B), BF16), SDS((t, HEADS_B * (NOPE + V_DIM)), BF16), SDS((t, ROPE), F32)],
        scratch_shapes=[pltpu.VMEM((seq, QK_B), F32), pltpu.VMEM((seq, V_DIM), F32)],
        compiler_params=_cparams(2),
    )(q, kv, kv, kr, do, c192, s192, p192)


def _loss_final(h, target, gamma):
    t, dn = h.shape
    tm = _tile(t)
    nt = t // tm

    def body(h_ref, t_ref, g_ref, dh_ref, dg_ref, loss_ref):
        i = pl.program_id(0)
        hv = h_ref[...]
        r = _rms_scale(hv)
        hh = hv * r
        gam = g_ref[...]
        err = hh * gam - t_ref[...]
        part = 0.5 * jnp.sum(jnp.mean(err * err, axis=-1, keepdims=True))

        @pl.when(i == 0)
        def _():
            loss_ref[...] = jnp.zeros_like(loss_ref)

        loss_ref[...] += part
        dy = err * (1.0 / dn)
        _acc_rows(dg_ref, dy * hh, i, nt)
        t1 = dy * gam
        dh_ref[...] = r * (t1 - hh * jnp.mean(t1 * hh, axis=-1, keepdims=True))

    row = BS((tm, dn), lambda i: (i, 0))
    return pl.pallas_call(
        body, name="loss_final", grid=(nt,),
        in_specs=[row, row, BS((1, dn), lambda i: (0, 0))],
        out_specs=[row, BS((8, dn), lambda i: (0, 0)), BS((8, 128), lambda i: (0, 0))],
        out_shape=[SDS((t, dn), F32), SDS((8, dn), F32), SDS((8, 128), F32)],
        compiler_params=_cparams(1),
    )(h, target, gamma)


def _ffn_bwd_in(name, dh, w_out, layer, gu, dep=None):
    t, dn = dh.shape
    tm = _tile(t)

    def body(dh_ref, w_ref, gu_ref, *rest):
        o_ref = rest[-1]
        xv = (0.5 * dh_ref[...]).astype(BF16)
        da = _dot_nt(xv, w_ref[...])
        g = gu_ref[0]
        u = gu_ref[1]
        sg = jax.nn.sigmoid(g)
        o_ref[0] = (da * u * (sg * (1.0 + g * (1.0 - sg)))).astype(BF16)
        o_ref[1] = (da * (g * sg)).astype(BF16)

    blk = BS((None, 2, tm, FB), lambda j, i: (j, 0, i, 0))
    deps = [] if dep is None else [dep]
    return pl.pallas_call(
        body, name=name, grid=(NJ, t // tm),
        in_specs=[BS((tm, dn), lambda j, i: (i, 0)), BS((None, None, FB, dn), lambda j, i: (layer, j, 0, 0)), blk]
        + [_dep_spec(2)] * len(deps),
        out_specs=blk, out_shape=SDS((NJ, 2, t, FB), BF16),
        compiler_params=_cparams(2),
    )(dh, w_out, gu, *deps)


def _mm_nt_plain(name, xf, w, dep=None):
    t, dn = xf.shape
    n = w.shape[0]
    tm = _tile(t)

    def body(x_ref, w_ref, *rest):
        rest[-1][...] = _dot_nt(x_ref[...].astype(BF16), w_ref[...]).astype(BF16)

    deps = [] if dep is None else [dep]
    return pl.pallas_call(
        body, name=name, grid=(t // tm,),
        in_specs=[BS((tm, dn), lambda i: (i, 0)), BS((n, dn), lambda i: (0, 0))] + [_dep_spec(1)] * len(deps),
        out_specs=BS((tm, n), lambda i: (i, 0)), out_shape=SDS((t, n), BF16),
        compiler_params=_cparams(1),
    )(xf, w, *deps)


def _mm_tn(name, xa, x_spec, ya, y_spec, out_shape, out_spec, nj, nt, y_scale=None):
    acc_shape = tuple(d for d in out_spec.block_shape if d is not None)

    def body(x_ref, y_ref, o_ref, acc):
        i = pl.program_id(1)
        yv = y_ref[...]
        if yv.dtype != BF16:
            yv = (yv if y_scale is None else y_scale * yv).astype(BF16)
        part = _dot_tn(x_ref[...], yv)

        @pl.when(i == 0)
        def _():
            acc[...] = part

        @pl.when(i > 0)
        def _():
            acc[...] += part

        @pl.when(i == nt - 1)
        def _():
            o_ref[...] = acc[...].astype(BF16)

    return pl.pallas_call(
        body, name=name, grid=(nj, nt),
        in_specs=[x_spec, y_spec], out_specs=out_spec, out_shape=SDS(out_shape.shape, BF16),
        scratch_shapes=[pltpu.VMEM(acc_shape, F32)],
        compiler_params=_cparams(2),
    )(xa, ya)


def _dw_qkv(hn, dqkv3, wb):
    t, dn = hn.shape
    per = wb // 128
    tm = _tile(t)

    def body(x_ref, y_ref, o_ref, acc):
        i = pl.program_id(0)
        xv = x_ref[...]
        for j in range(NDEV):
            cols = [y_ref[(per * j + k) // 8, :, ((per * j + k) % 8) * 128 : ((per * j + k) % 8 + 1) * 128] for k in range(per)]
            part = _dot_tn(xv, jnp.concatenate(cols, axis=1))

            @pl.when(i == 0)
            def _():
                acc[j] = part

            @pl.when(i > 0)
            def _():
                acc[j] += part

        @pl.when(i == t // tm - 1)
        def _():
            o_ref[...] = acc[...].astype(BF16)

    return pl.pallas_call(
        body, name="dw_qkv", grid=(t // tm,),
        in_specs=[BS((tm, dn), lambda i: (i, 0)), BS((3, tm, dn), lambda i: (0, i, 0))],
        out_specs=BS((NDEV, dn, wb), lambda i: (0, 0, 0)), out_shape=SDS((NDEV, dn, wb), BF16),
        scratch_shapes=[pltpu.VMEM((NDEV, dn, wb), F32)],
        compiler_params=_cparams(1),
    )(hn, dqkv3)


def _mm_nt_epi(name, ya, y_spec, wa, w_spec, nj, n_out, extra, out_shapes, out_specs, epilogue, tm, nt, mm_fn=None):
    n_extra = len(extra)
    n_outs = len(out_shapes)

    def body(*refs):
        y_ref, w_ref = refs[:2]
        ex = refs[2 : 2 + n_extra]
        outs = refs[2 + n_extra : 2 + n_extra + n_outs]
        i = pl.program_id(0)
        j = pl.program_id(1)
        part = _dot_nt(y_ref[...], w_ref[...]) if mm_fn is None else mm_fn(y_ref, w_ref)
        if nj == 1:
            epilogue(part, ex, outs, i, nt)
            return
        acc = refs[-1]

        @pl.when(j == 0)
        def _():
            acc[...] = part

        @pl.when(j > 0)
        def _():
            acc[...] += part

        @pl.when(j == nj - 1)
        def _():
            epilogue(acc[...], ex, outs, i, nt)

    return pl.pallas_call(
        body, name=name, grid=(nt, nj),
        in_specs=[y_spec, w_spec] + [spec for _, spec in extra],
        out_specs=out_specs, out_shape=out_shapes,
        scratch_shapes=[] if nj == 1 else [pltpu.VMEM((tm, n_out), F32)],
        compiler_params=_cparams(2),
    )(ya, wa, *[arr for arr, _ in extra])


def _norm_bwd(dn, hv, gam):
    r = _rms_scale(hv)
    hh = hv * r
    t1 = dn * gam
    return r * (t1 - hh * jnp.mean(t1 * hh, axis=-1, keepdims=True)), dn * hh


def _norm_bwd_epilogue(has_res, out_dtype):
    def epilogue(dn, ex, outs, i, nt):
        dh, dg_rows = _norm_bwd(dn, ex[0][...], ex[1][...])
        _acc_rows(outs[1], dg_rows, i, nt)
        if has_res:
            dh = dh + ex[2][...]
        outs[0][...] = dh.astype(out_dtype)

    return epilogue


def _mm_nt_norm_bwd(name, ya, y_spec, wa, w_spec, nj, h, gamma, res, out_dtype, mm_fn=None, want_tm=512, dep=None):
    t, n = h.shape
    tm = _tile(t, want_tm)
    nt = t // tm
    row = BS((tm, n), lambda i, j: (i, 0))
    extra = [(h, row), (gamma, BS((1, n), lambda i, j: (0, 0)))]
    if res is not None:
        extra.append((res, row))
    if dep is not None:
        extra.append((dep, _dep_spec(2)))
    return _mm_nt_epi(
        name, ya, y_spec, wa, w_spec, nj, n, extra, [SDS((t, n), out_dtype), SDS((8, n), F32)],
        [row, BS((8, n), lambda i, j: (0, 0))], _norm_bwd_epilogue(res is not None, out_dtype), tm, nt, mm_fn,
    )


def _dev_block(jj):
    return jj // 2 + NJ * (jj % 2)


def _ffn_bwd(tag, dh, n_in, h_in, gamma, gu, a, w_in, w_out, more_grads, dep):
    t, dn = dh.shape
    tm = _tile(t)
    nt = t // tm
    layer = 0
    dgu = _ffn_bwd_in(f"{tag}_bwd_in", dh, w_out, layer, gu, dep).reshape(2 * NJ, t, FB)
    dw_out = _mm_tn(
        f"{tag}_dw_out", a, BS((None, tm, FB), lambda j, i: (j, i, 0)), dh, BS((tm, dn), lambda j, i: (i, 0)),
        SDS((NJ, FB, dn), F32), BS((None, FB, dn), lambda j, i: (j, 0, 0)), NJ, nt, y_scale=0.5,
    )
    dw_in = _mm_tn(
        f"{tag}_dw_in", n_in, BS((tm, dn), lambda j, i: (i, 0)), dgu, BS((None, tm, FB), lambda j, i: (j, i, 0)),
        SDS((NDEV, dn, FB), F32), BS((None, dn, FB), lambda j, i: (_dev_block(j), 0, 0)), NDEV, nt,
    )
    entries = [("scatter", dw_in), ("scatter", dw_out.reshape(NDEV, NJ * FB // NDEV, dn))] + [("scatter", g) for g in more_grads]
    (started,), token = _exchange_start(f"{tag}_reduce_start", [entries])
    dh_in, dgam = _mm_nt_norm_bwd(
        f"{tag}_dn", dgu, BS((None, tm, FB), lambda i, j: (j, i, 0)),
        w_in, BS((None, None, dn, FB), lambda i, j: (layer, _dev_block(j), 0, 0)), NDEV, h_in, gamma, dh, F32, dep=token,
    )
    return dh_in, dgam, started


def _dqkv_mm(per):
    def mm(y_ref, w_ref):
        acc = None
        for j in range(NDEV):
            cols = [y_ref[(per * j + k) // 8, :, ((per * j + k) % 8) * 128 : ((per * j + k) % 8 + 1) * 128] for k in range(per)]
            part = _dot_nt(jnp.concatenate(cols, axis=1), w_ref[j])
            acc = part if acc is None else acc + part
        return acc

    return mm


def _kv_latent_bwd(dkv, w_up, ckr, latent_norm, dkr, c64, s64, p64, seq):
    t, wd = ckr.shape
    hb = w_up.shape[-1]
    tm = _tile(min(seq, 512), min(seq, 512))
    nt = t // tm
    nseq = seq // tm

    def epilogue(dn, ex, outs, i, nt_):
        dlat, dg_rows = _norm_bwd(dn, ex[0][...], ex[1][...])
        _acc_rows(outs[1], dg_rows, i, nt_)
        outs[0][:, :KV_LORA] = dlat.astype(BF16)
        outs[0][:, KV_LORA:] = _rope_bwd(ex[2][...], ex[3][...], ex[4][...], ex[5][...]).astype(BF16)

    pos = BS((tm, ROPE), lambda i, j: (i % nseq, 0))
    extra = [
        (ckr, BS((tm, KV_LORA), lambda i, j: (i, 0))), (latent_norm, BS((1, KV_LORA), lambda i, j: (0, 0))),
        (dkr, BS((tm, ROPE), lambda i, j: (i, 0))), (c64, pos), (s64, pos), (p64, BS((ROPE, ROPE), lambda i, j: (0, 0))),
    ]
    return _mm_nt_epi(
        "kv_latent_bwd", dkv, BS((tm, hb), lambda i, j: (i, j)), w_up, BS((None, KV_LORA, hb), lambda i, j: (j, 0, 0)),
        HEADS_B, KV_LORA, extra, [SDS((t, wd), BF16), SDS((8, KV_LORA), F32)],
        [BS((tm, wd), lambda i, j: (i, 0)), BS((8, KV_LORA), lambda i, j: (0, 0))], epilogue, tm, nt,
    )


def _adamw(name, parts, w, m, v):
    n_layers = len(parts)
    rows, cols = w.shape[0] // n_layers, w.shape[1]
    tr = max(d for d in range(8, min(rows, 256) + 1, 8) if rows % d == 0)
    nb = rows // tr

    def body(*refs):
        p_refs = refs[:n_layers]
        w_ref, m_ref, v_ref, g_ref, d_ref, nm_ref, nv_ref = refs[n_layers : n_layers + 7]
        layer = pl.program_id(0)
        for lp in range(n_layers):

            @pl.when(layer == lp)
            def _():
                g = p_refs[lp][0].astype(F32)
                for k in range(1, NDEV):
                    g = g + p_refs[lp][k].astype(F32)
                g_ref[...] = g

        g = g_ref[...]
        nm = ADAM_B1 * m_ref[...] + (1.0 - ADAM_B1) * g
        nv = ADAM_B2 * v_ref[...] + (1.0 - ADAM_B2) * (g * g)
        nm_ref[...] = nm
        nv_ref[...] = nv
        m_hat = nm / (1.0 - ADAM_B1 ** ADAM_STEP)
        v_hat = nv / (1.0 - ADAM_B2 ** ADAM_STEP)
        d_ref[...] = -ADAM_LR * (m_hat / (jnp.sqrt(v_hat) + ADAM_EPS) + ADAM_WD * w_ref[...])

    def part_spec(lp):
        return BS((NDEV, tr, cols), lambda l, i: (0, jnp.where(l == lp, i, jnp.where(l < lp, 0, nb - 1)), 0))

    row = BS((tr, cols), lambda l, i: (l * nb + i, 0))
    return pl.pallas_call(
        body, name=name, grid=(n_layers, nb),
        in_specs=[part_spec(lp) for lp in range(n_layers)] + [row, row, row],
        out_specs=[row] * 4, out_shape=[SDS(w.shape, F32)] * 4,
        compiler_params=_cparams(2),
    )(*parts, w, m, v)


def _pack_small(ffn1_norm, mix_norm, ffn2_norm, kv_norm, final_norm, q_norm, latent_norm, rel_bias, last_row):
    dn = ffn1_norm.shape[-1]

    def rows_of(a, n_rows):
        flat = a.reshape(-1)
        return jnp.pad(flat, (0, n_rows * dn - flat.shape[0])).reshape(n_rows, dn)

    return jnp.concatenate(
        [
            ffn1_norm.reshape(2, dn), mix_norm.reshape(2, dn), ffn2_norm.reshape(2, dn), kv_norm.reshape(1, dn),
            final_norm.reshape(1, dn), rows_of(q_norm, 1), rows_of(latent_norm, 1), rows_of(rel_bias, 5), rows_of(last_row, 1),
        ],
        axis=0,
    )


def _unpack_small(pack):
    dn = pack.shape[-1]
    return dict(
        ffn1_norm=pack[0:2], mix_norm=pack[2:4], ffn2_norm=pack[4:6], kv_norm=pack[6], final_norm=pack[7],
        b_q_norm=pack[8, :Q_LORA].reshape(1, Q_LORA), kv_latent_norm=pack[9, :KV_LORA],
        a_rel_bias=pack[10:15].reshape(-1)[: HEADS_A * NREL].reshape(1, HEADS_A, NREL), last=pack[15],
    )


def kernel(x, ffn1_norm, ffn1_w_in, ffn1_w_out, mix_norm, ffn2_norm, ffn2_w_in, ffn2_w_out, a_w_qkv, a_rel_bias, a_w_o, kv_norm, kv_w_down, kv_latent_norm, kv_w_up, b_w_dq, b_q_norm, b_w_uq, b_w_o, final_norm, loss_target, m_ffn1_norm, m_ffn1_w_in, m_ffn1_w_out, m_mix_norm, m_ffn2_norm, m_ffn2_w_in, m_ffn2_w_out, m_a_w_qkv, m_a_rel_bias, m_a_w_o, m_kv_norm, m_kv_w_down, m_kv_latent_norm, m_kv_w_up, m_b_w_dq, m_b_q_norm, m_b_w_uq, m_b_w_o, m_final_norm, v_ffn1_norm, v_ffn1_w_in, v_ffn1_w_out, v_mix_norm, v_ffn2_norm, v_ffn2_w_in, v_ffn2_w_out, v_a_w_qkv, v_a_rel_bias, v_a_w_o, v_kv_norm, v_kv_w_down, v_kv_latent_norm, v_kv_w_up, v_b_w_dq, v_b_q_norm, v_b_w_uq, v_b_w_o, v_final_norm):
    bl, seq, dn = x.shape
    t = bl * seq
    tm = _tile(t)
    nt = t // tm
    x2 = x.reshape(t, dn)
    target2 = loss_target.reshape(t, dn)

    def gathered(*ws):
        return [("gather", w.astype(BF16)) for w in ws]

    ag, token = _exchange_start(
        "gather_start",
        [
            gathered(ffn1_w_in[0]), gathered(ffn1_w_out[0]), gathered(a_w_qkv[0], a_w_o[0]), gathered(ffn2_w_in[0], ffn2_w_out[0]),
            gathered(kv_w_down, kv_w_up), gathered(ffn1_w_in[1], ffn1_w_out[1]), gathered(b_w_dq[0], b_w_uq[0], b_w_o[0]),
            gathered(ffn2_w_in[1], ffn2_w_out[1]),
        ],
    )

    def as_w_in(w):
        return w.reshape(1, NDEV, dn, FB)

    def as_w_out(w):
        return w.reshape(1, NJ, FB, dn)

    c64, s64, p64, c192, s192, p192 = _rope_tables(seq)
    q_norm = b_q_norm.reshape(1, Q_LORA)
    latent_norm = kv_latent_norm.reshape(1, KV_LORA)
    bias = _rel_bias_fwd(jnp.pad(a_rel_bias[0], ((0, 0), (0, NREL_PAD - NREL))))

    h0, h1, h2, n1, hn, n2, gu1, gu2, a1, a2, w_in1, w_in2, w_out1, w_out2 = ([None, None] for _ in range(14))
    h0[0] = x2
    (n1[0],) = _norm_fwd("norm_x", x2, ffn1_norm[0:1], token)
    w_in1[0] = as_w_in(_exchange_wait("gather_wait_0", ag[0], n1[0])[0])
    gu1[0], a1[0] = _ffn_in("ffn1_in_0", n1[0], w_in1[0], 0)
    w_out1[0] = as_w_out(_exchange_wait("gather_wait_1", ag[1], a1[0])[0])
    h1[0], hn[0] = _mm_res_norm("ffn1_out_0", a1[0], w_out1[0], 0, h0[0], mix_norm[0:1], 0.5)
    w_qkv, w_o_a = _exchange_wait("gather_wait_2", ag[2], hn[0])
    qkv_wb = w_qkv.shape[-1]
    w_o_a = w_o_a.reshape(1, 1, dn, dn)
    qkv3 = _qkv_proj("qkv_proj", hn[0], w_qkv)
    o_a = _attn_a_fwd(qkv3, bias, bl, seq)
    h2[0], n2[0] = _mm_res_norm("attn_a_out", o_a.reshape(1, t, dn), w_o_a, 0, h1[0], ffn2_norm[0:1], 1.0)
    w_in2[0], w_out2[0] = _exchange_wait("gather_wait_3", ag[3], n2[0])
    w_in2[0], w_out2[0] = as_w_in(w_in2[0]), as_w_out(w_out2[0])
    gu2[0], a2[0] = _ffn_in("ffn2_in_0", n2[0], w_in2[0], 0)
    h0[1], hk, n1[1] = _mm_res_norm(
        "ffn2_out_0", a2[0], w_out2[0], 0, h2[0], jnp.concatenate([kv_norm.reshape(1, dn), ffn1_norm[1:2]], axis=0), 0.5
    )
    w_down, w_up = _exchange_wait("gather_wait_4", ag[4], hk)
    w_down = w_down.reshape(dn, KV_LORA + ROPE)
    ckr, ckv, kr = _kv_down(hk, w_down, latent_norm, c64, s64, p64, seq)
    kv = _kv_up(ckv, w_up)
    w_in1[1], w_out1[1] = _exchange_wait("gather_wait_5", ag[5], kv)
    w_in1[1], w_out1[1] = as_w_in(w_in1[1]), as_w_out(w_out1[1])
    gu1[1], a1[1] = _ffn_in("ffn1_in_1", n1[1], w_in1[1], 0)
    h1[1], hn[1] = _mm_res_norm("ffn1_out_1", a1[1], w_out1[1], 0, h0[1], mix_norm[1:2], 0.5)
    w_dq, w_uq, w_o_b = _exchange_wait("gather_wait_6", ag[6], hn[1])
    w_dq = w_dq.reshape(dn, Q_LORA)
    w_o_b = w_o_b.reshape(1, 1, dn, dn)
    cq_pre, cq = _q_down(hn[1], w_dq, q_norm)
    q = _q_up(cq, w_uq, c192, s192, p192, seq)
    o_b = _mla_fwd(q, kv, kr, bl, seq)
    h2[1], n2[1] = _mm_res_norm("attn_b_out", o_b.reshape(1, t, dn), w_o_b, 0, h1[1], ffn2_norm[1:2], 1.0)
    w_in2[1], w_out2[1] = _exchange_wait("gather_wait_7", ag[7], n2[1])
    w_in2[1], w_out2[1] = as_w_in(w_in2[1]), as_w_out(w_out2[1])
    gu2[1], a2[1] = _ffn_in("ffn2_in_1", n2[1], w_in2[1], 0)
    (h_last,) = _mm_res_norm("ffn2_out_1", a2[1], w_out2[1], 0, h2[1], None, 0.5)
    dh, dg_final, loss_part = _loss_final(h_last, target2, final_norm.reshape(1, dn))

    dg_ffn1, dg_mix, dg_ffn2, rs_ffn1, rs_ffn2 = ([None, None] for _ in range(5))
    col128 = BS((tm, 128), lambda j, i: (i, j))
    full_row = BS((tm, dn), lambda j, i: (i, 0))
    dh, dg_ffn2[1], rs_ffn2[1] = _ffn_bwd("ffn2_1", dh, n2[1], h2[1], ffn2_norm[1:2], gu2[1], a2[1], w_in2[1], w_out2[1], [], None)
    do_b = _mm_nt_plain("attn_b_do", dh, w_o_b.reshape(dn, dn))
    dw_o_b = _mm_tn("attn_b_dwo", o_b, col128, dh, full_row, SDS((NDEV, 128, dn), F32), BS((None, 128, dn), lambda j, i: (j, 0, 0)), NDEV, nt)
    dq_pre, dkv, dkr = _mla_bwd(q, kv, kr, do_b, c192, s192, p192, bl, seq)
    dw_uq = _mm_tn(
        "dw_uq", cq, BS((tm, Q_LORA), lambda j, i: (i, 0)), dq_pre, BS((None, tm, QK_B), lambda j, i: (j, i, 0)),
        SDS((HEADS_B, Q_LORA, QK_B), F32), BS((None, Q_LORA, QK_B), lambda j, i: (j, 0, 0)), HEADS_B, nt,
    )
    dcq_pre, dg_q = _mm_nt_norm_bwd(
        "dcq", dq_pre, BS((None, tm, QK_B), lambda i, j: (j, i, 0)), w_uq, BS((None, Q_LORA, QK_B), lambda i, j: (j, 0, 0)),
        HEADS_B, cq_pre, q_norm, None, BF16,
    )
    dw_dq = _mm_tn(
        "dw_dq", hn[1], col128, dcq_pre, BS((tm, Q_LORA), lambda j, i: (i, 0)),
        SDS((NDEV, 128, Q_LORA), F32), BS((None, 128, Q_LORA), lambda j, i: (j, 0, 0)), NDEV, nt,
    )
    dh, dg_mix[1] = _mm_nt_norm_bwd(
        "dhn_b", dcq_pre, BS((tm, Q_LORA), lambda i, j: (i, 0)), w_dq, BS((dn, Q_LORA), lambda i, j: (0, 0)),
        1, h1[1], mix_norm[1:2], dh, F32,
    )
    dh, dg_ffn1[1], rs_ffn1[1] = _ffn_bwd(
        "ffn1_1", dh, n1[1], h0[1], ffn1_norm[1:2], gu1[1], a1[1], w_in1[1], w_out1[1], [dw_o_b, dw_uq, dw_dq], None
    )
    dw_up = _mm_tn(
        "dw_up", ckv, BS((tm, KV_LORA), lambda j, i: (i, 0)), dkv, BS((tm, NOPE + V_DIM), lambda j, i: (i, j)),
        SDS((HEADS_B, KV_LORA, NOPE + V_DIM), F32), BS((None, KV_LORA, NOPE + V_DIM), lambda j, i: (j, 0, 0)), HEADS_B, nt,
    )
    dckr, dg_latent = _kv_latent_bwd(dkv, w_up, ckr, latent_norm, dkr, c64, s64, p64, seq)
    dw_down = _mm_tn(
        "dw_down", hk, col128, dckr, BS((tm, KV_LORA + ROPE), lambda j, i: (i, 0)),
        SDS((NDEV, 128, KV_LORA + ROPE), F32), BS((None, 128, KV_LORA + ROPE), lambda j, i: (j, 0, 0)), NDEV, nt,
    )
    dh, dg_kv = _mm_nt_norm_bwd(
        "dhk", dckr, BS((tm, KV_LORA + ROPE), lambda i, j: (i, 0)), w_down, BS((dn, KV_LORA + ROPE), lambda i, j: (0, 0)),
        1, h0[1], kv_norm.reshape(1, dn), dh, F32,
    )
    dh, dg_ffn2[0], rs_ffn2[0] = _ffn_bwd(
        "ffn2_0", dh, n2[0], h2[0], ffn2_norm[0:1], gu2[0], a2[0], w_in2[0], w_out2[0], [dw_up, dw_down], None
    )
    do_a = _mm_nt_plain("attn_a_do", dh, w_o_a.reshape(dn, dn))
    dw_o_a = _mm_tn("attn_a_dwo", o_a, col128, dh, full_row, SDS((NDEV, 128, dn), F32), BS((None, 128, dn), lambda j, i: (j, 0, 0)), NDEV, nt)
    dqkv3, dbias = _attn_a_bwd(qkv3, do_a, bias, bl, seq)
    dw_qkv = _dw_qkv(hn[0], dqkv3, qkv_wb)
    dh, dg_mix[0] = _mm_nt_norm_bwd(
        "dhn_a", dqkv3, BS((3, tm, dn), lambda i, j: (0, i, 0)), w_qkv, BS((NDEV, dn, qkv_wb), lambda i, j: (0, 0, 0)),
        1, h1[0], mix_norm[0:1], dh, F32, mm_fn=_dqkv_mm(qkv_wb // 128),
    )
    (rs_attn_a,), token = _exchange_start("attn_a_reduce_start", [[("scatter", dw_o_a), ("scatter", dw_qkv)]])
    dh, dg_ffn1[0], rs_ffn1[0] = _ffn_bwd("ffn1_0", dh, n1[0], h0[0], ffn1_norm[0:1], gu1[0], a1[0], w_in1[0], w_out1[0], [], token)
    grad_x = dh.reshape(bl, seq, dn)
    dtable = _rel_bias_bwd(dbias)[:, :NREL]

    small = _pack_small(
        jnp.stack([dg_ffn1[0][0], dg_ffn1[1][0]]), jnp.stack([dg_mix[0][0], dg_mix[1][0]]), jnp.stack([dg_ffn2[0][0], dg_ffn2[1][0]]),
        dg_kv[0], dg_final[0], dg_q[0], dg_latent[0], dtable, loss_part[0],
    )
    (r_small,) = _exchange("gather_small_grads", [("gather", small)])

    def update(name, parts, w, m, v):
        n_layers = len(parts)
        rows = int(np.prod(w.shape[:-1]))
        cols = w.shape[-1]
        parts = [p.reshape(NDEV, rows // n_layers, cols) for p in parts]
        outs = _adamw(name, parts, w.reshape(rows, cols), m.reshape(rows, cols), v.reshape(rows, cols))
        return [o.reshape(w.shape) for o in outs]

    res = {}
    r_in2_1, r_out2_1 = _exchange_wait("ffn2_1_reduce_wait", rs_ffn2[1], dh)
    r_in1_1, r_out1_1, r_o_b, r_uq, r_dq = _exchange_wait("ffn1_1_reduce_wait", rs_ffn1[1], dh)
    r_in2_0, r_out2_0, r_up, r_down = _exchange_wait("ffn2_0_reduce_wait", rs_ffn2[0], dh)
    res["ffn2_w_in"] = update("adamw_ffn2_w_in", [r_in2_0, r_in2_1], ffn2_w_in, m_ffn2_w_in, v_ffn2_w_in)
    res["ffn2_w_out"] = update("adamw_ffn2_w_out", [r_out2_0, r_out2_1], ffn2_w_out, m_ffn2_w_out, v_ffn2_w_out)
    res["kv_w_down"] = update("adamw_kv_w_down", [r_down], kv_w_down, m_kv_w_down, v_kv_w_down)
    res["kv_w_up"] = update("adamw_kv_w_up", [r_up], kv_w_up, m_kv_w_up, v_kv_w_up)
    res["b_w_dq"] = update("adamw_b_w_dq", [r_dq], b_w_dq, m_b_w_dq, v_b_w_dq)
    res["b_w_uq"] = update("adamw_b_w_uq", [r_uq], b_w_uq, m_b_w_uq, v_b_w_uq)
    res["b_w_o"] = update("adamw_b_w_o", [r_o_b], b_w_o, m_b_w_o, v_b_w_o)
    r_o_a, r_qkv = _exchange_wait("attn_a_reduce_wait", rs_attn_a, res["ffn2_w_in"][0])
    res["a_w_qkv"] = update("adamw_a_w_qkv", [r_qkv], a_w_qkv, m_a_w_qkv, v_a_w_qkv)
    res["a_w_o"] = update("adamw_a_w_o", [r_o_a], a_w_o, m_a_w_o, v_a_w_o)
    r_in1_0, r_out1_0 = _exchange_wait("ffn1_0_reduce_wait", rs_ffn1[0], res["a_w_qkv"][0])
    res["ffn1_w_in"] = update("adamw_ffn1_w_in", [r_in1_0, r_in1_1], ffn1_w_in, m_ffn1_w_in, v_ffn1_w_in)
    res["ffn1_w_out"] = update("adamw_ffn1_w_out", [r_out1_0, r_out1_1], ffn1_w_out, m_ffn1_w_out, v_ffn1_w_out)
    zero_row = jnp.zeros((dn,), F32)
    packs = [
        _pack_small(f1, mx, f2, kvn, fin, qn, lat, rel, zero_row)
        for f1, mx, f2, kvn, fin, qn, lat, rel in (
            (ffn1_norm, mix_norm, ffn2_norm, kv_norm, final_norm, b_q_norm, kv_latent_norm, a_rel_bias),
            (m_ffn1_norm, m_mix_norm, m_ffn2_norm, m_kv_norm, m_final_norm, m_b_q_norm, m_kv_latent_norm, m_a_rel_bias),
            (v_ffn1_norm, v_mix_norm, v_ffn2_norm, v_kv_norm, v_final_norm, v_b_q_norm, v_kv_latent_norm, v_a_rel_bias),
        )
    ]
    small_out = [_unpack_small(o) for o in _adamw("adamw_small", [r_small], *packs)]
    for name in ("ffn1_norm", "mix_norm", "ffn2_norm", "a_rel_bias", "kv_norm", "kv_latent_norm", "b_q_norm", "final_norm"):
        res[name] = [so[name] for so in small_out]
    loss = small_out[0]["last"][0]

    order = [
        "ffn1_norm", "ffn1_w_in", "ffn1_w_out", "mix_norm", "ffn2_norm", "ffn2_w_in", "ffn2_w_out", "a_w_qkv", "a_rel_bias",
        "a_w_o", "kv_norm", "kv_w_down", "kv_latent_norm", "kv_w_up", "b_w_dq", "b_q_norm", "b_w_uq", "b_w_o", "final_norm",
    ]
    return (loss, grad_x, *[res[n][0] for n in order], *[res[n][1] for n in order], *[res[n][2] for n in order], *[res[n][3] for n in order])
```

```python
import functools

import jax
import jax.numpy as jnp
import numpy as np
from jax import lax
from jax.experimental import pallas as pl
from jax.experimental.pallas import tpu as pltpu

NDEV = 8
D_MODEL = 1024
D_FF = 2816
FB = 2 * D_FF // NDEV
NJ = D_FF // FB
CHUNK = 64
LEFT_CHUNKS = 8
PAD = LEFT_CHUNKS * CHUNK
BAND = PAD + CHUNK
CHUNKS_PER_STEP = 4
WINDOW = PAD + CHUNKS_PER_STEP * CHUNK
STEP_ROWS = CHUNKS_PER_STEP * 2 * CHUNK
MAX_REL = 128
NREL = 2 * MAX_REL + 1
NREL_PAD = 384
HEADS_A = 16
HEADS_B = 8
NOPE = 128
ROPE = 64
QK_B = NOPE + ROPE
V_DIM = 128
Q_LORA = 768
KV_LORA = 256
ROPE_THETA = 10000.0
EPS = 1e-6
NEG_INF = -1e30
MLA_TQ = 256
ADAM_LR = 0.001
ADAM_B1 = 0.9
ADAM_B2 = 0.999
ADAM_EPS = 1e-08
ADAM_WD = 0.01
ADAM_STEP = 10
PACK_ROWS = 16
VMEM_LIMIT_BYTES = 56 * 1024 * 1024

F32 = jnp.float32
BF16 = jnp.bfloat16
SDS = jax.ShapeDtypeStruct
BS = pl.BlockSpec
HIGHEST = lax.Precision.HIGHEST
MESH = pl.DeviceIdType.MESH


def _cparams(n_axes):
    return pltpu.CompilerParams(dimension_semantics=("arbitrary",) * n_axes, vmem_limit_bytes=VMEM_LIMIT_BYTES)


def _tile(t, want=512):
    return want if t % want == 0 else t


def _dot(a, b):
    return jnp.dot(a, b, preferred_element_type=F32)


def _dot_nt(a, b):
    return lax.dot_general(a, b, (((1,), (1,)), ((), ())), preferred_element_type=F32)


def _dot_tn(a, b):
    return lax.dot_general(a, b, (((0,), (0,)), ((), ())), preferred_element_type=F32)


def _dot_exact(a, b):
    return jnp.dot(a, b, precision=HIGHEST, preferred_element_type=F32)


def _rms_scale(h):
    return lax.rsqrt(jnp.mean(h * h, axis=-1, keepdims=True) + EPS)


def _acc_rows(ref, val, step, n_steps):
    part = val.reshape(val.shape[0] // 8, 8, val.shape[1]).sum(axis=0)

    @pl.when(step == 0)
    def _():
        ref[...] = part

    @pl.when(step > 0)
    def _():
        ref[...] += part

    @pl.when(step == n_steps - 1)
    def _():
        ref[...] = jnp.broadcast_to(jnp.sum(ref[...], axis=0, keepdims=True), ref.shape)


def _exchange_plan(entries):
    ins = [e[1] for e in entries]
    kinds = [e[0] for e in entries]
    lands = [SDS((NDEV,) + a.shape if k == "gather" else a.shape, a.dtype) for k, a in zip(kinds, ins)]
    return ins, lands, kinds


def _mesh_place():
    x, y, c = lax.axis_index("x"), lax.axis_index("y"), lax.axis_index("c")
    return (x, y, c), 4 * x + 2 * y + c


def _flipped(place, p):
    x, y, c = place
    px = 1 - x if p & 4 else x
    py = 1 - y if p & 2 else y
    pc = 1 - c if p & 1 else c
    return (px, py, pc), 4 * px + 2 * py + pc


def _ends(kind, src_ref, land_ref, origin, target):
    if kind == "gather":
        return src_ref, land_ref.at[origin]
    return src_ref.at[target], land_ref.at[origin]


def _remote(kind, src_ref, land_ref, send_sems, recv_sems, k, p, place, me, arriving):
    peer_pos, peer = _flipped(place, p)
    src, dst = _ends(kind, src_ref, land_ref, me, peer)
    if arriving:
        dst = _ends(kind, src_ref, land_ref, peer, me)[1]
    sem = k * (NDEV - 1) + p - 1
    return pltpu.make_async_remote_copy(
        src_ref=src, dst_ref=dst, send_sem=send_sems.at[sem], recv_sem=recv_sems.at[sem], device_id=peer_pos, device_id_type=MESH,
    )


def _exchange(name, entries):
    ins, lands, kinds = _exchange_plan(entries)
    n = len(ins)

    def body(*refs):
        in_refs, land_refs = refs[:n], refs[n : 2 * n]
        send_sems, recv_sems, local_sems = refs[2 * n :]
        place, me = _mesh_place()
        local = []
        for k in range(n):
            src, dst = _ends(kinds[k], in_refs[k], land_refs[k], me, me)
            local.append(pltpu.make_async_copy(src, dst, local_sems.at[k]))
            local[-1].start()
        sends = []
        for p in range(1, NDEV):
            for k in range(n):
                sends.append(_remote(kinds[k], in_refs[k], land_refs[k], send_sems, recv_sems, k, p, place, me, False))
                sends[-1].start()
        for p in range(1, NDEV):
            for k in range(n):
                _remote(kinds[k], in_refs[k], land_refs[k], send_sems, recv_sems, k, p, place, me, True).wait_recv()
        for cp in sends:
            cp.wait_send()
        for cp in local:
            cp.wait()

    any_spec = BS(memory_space=pl.ANY)
    return pl.pallas_call(
        body, name=name, out_shape=lands, in_specs=[any_spec] * n, out_specs=[any_spec] * n,
        scratch_shapes=[
            pltpu.SemaphoreType.DMA((n * (NDEV - 1),)), pltpu.SemaphoreType.DMA((n * (NDEV - 1),)), pltpu.SemaphoreType.DMA((n,)),
        ],
    )(*ins)


HBM_SPEC = BS(memory_space=pltpu.HBM)
SEM_SPEC = BS(memory_space=pltpu.SEMAPHORE)
DATAFLOW = pltpu.SideEffectType.DATAFLOW_SIDE_EFFECTING


def _exchange_start(name, groups):
    plans = [_exchange_plan(g) for g in groups]
    ins = [a for plan in plans for a in plan[0]]
    lands = [s for plan in plans for s in plan[1]]
    kinds = [kind for plan in plans for kind in plan[2]]
    n_in, n_groups = len(ins), len(groups)

    def body(*refs):
        in_refs, land_refs = refs[:n_in], refs[n_in : 2 * n_in]
        sems = refs[2 * n_in : 2 * n_in + 2 * n_groups]
        token = refs[4 * n_in + 2 * n_groups]
        local_sems = refs[4 * n_in + 2 * n_groups + 1]
        place, me = _mesh_place()
        local = []
        for k, kind in enumerate(kinds):
            src, dst = _ends(kind, in_refs[k], land_refs[k], me, me)
            local.append(pltpu.make_async_copy(src, dst, local_sems.at[k]))
            local[-1].start()
        base = 0
        for g, plan in enumerate(plans):
            for p in range(1, NDEV):
                for k, kind in enumerate(plan[2]):
                    _remote(kind, in_refs[base + k], land_refs[base + k], sems[2 * g], sems[2 * g + 1], k, p, place, me, False).start()
            base += len(plan[2])
        for cp in local:
            cp.wait()
        token[...] = jnp.zeros_like(token)

    sem_shapes = []
    for plan in plans:
        sem_shapes += [pltpu.SemaphoreType.DMA((len(plan[2]) * (NDEV - 1),))] * 2
    outs = pl.pallas_call(
        body, name=name,
        out_shape=sem_shapes + [pltpu.HBM(a.shape, a.dtype) for a in ins] + [pltpu.HBM(s.shape, s.dtype) for s in lands] + [SDS((8, 128), F32)],
        in_specs=[HBM_SPEC] * (2 * n_in),
        out_specs=[SEM_SPEC] * (2 * n_groups) + [HBM_SPEC] * (2 * n_in) + [BS(memory_space=pltpu.VMEM)],
        input_output_aliases={i: 2 * n_groups + i for i in range(2 * n_in)},
        scratch_shapes=[pltpu.SemaphoreType.DMA((n_in,))],
        compiler_params=pltpu.CompilerParams(has_side_effects=DATAFLOW),
    )(
        *[pltpu.with_memory_space_constraint(a, pltpu.HBM) for a in ins],
        *[pltpu.with_memory_space_constraint(lax.empty(s.shape, s.dtype), pltpu.HBM) for s in lands],
    )
    sems, srcs, landed, token = outs[: 2 * n_groups], outs[2 * n_groups : 2 * n_groups + n_in], outs[2 * n_groups + n_in : -1], outs[-1]
    started, base = [], 0
    for g, plan in enumerate(plans):
        n = len(plan[2])
        started.append((sems[2 * g], sems[2 * g + 1], srcs[base : base + n], landed[base : base + n], plan[2]))
        base += n
    return started, token


def _exchange_wait(name, started, after):
    send_sems, recv_sems, srcs, landed, kinds = started
    n = len(kinds)

    def body(*refs):
        in_refs, land_refs = refs[:n], refs[n : 2 * n]
        send_ref, recv_ref = refs[2 * n], refs[2 * n + 1]
        place, me = _mesh_place()
        for p in range(1, NDEV):
            for k in range(n):
                _remote(kinds[k], in_refs[k], land_refs[k], send_ref, recv_ref, k, p, place, me, True).wait_recv()
        for p in range(1, NDEV):
            for k in range(n):
                _remote(kinds[k], in_refs[k], land_refs[k], send_ref, recv_ref, k, p, place, me, False).wait_send()

    outs = pl.pallas_call(
        body, name=name,
        out_shape=[pltpu.HBM(a.shape, a.dtype) for a in srcs] + [pltpu.HBM(a.shape, a.dtype) for a in landed],
        in_specs=[HBM_SPEC] * (2 * n) + [SEM_SPEC, SEM_SPEC, BS(memory_space=pl.ANY)],
        out_specs=[HBM_SPEC] * (2 * n),
        input_output_aliases={i: i for i in range(2 * n)},
        compiler_params=pltpu.CompilerParams(has_side_effects=DATAFLOW),
    )(*srcs, *landed, send_sems, recv_sems, after)
    return outs[n:]


def _dep_spec(n_axes):
    return BS((8, 128), (lambda i: (0, 0)) if n_axes == 1 else (lambda i, j: (0, 0)))


def _norm_fwd(name, h, gammas, dep):
    t, dn = h.shape
    ng = gammas.shape[0]
    tm = _tile(t)

    def body(h_ref, g_ref, dep_ref, *outs):
        hv = h_ref[...]
        hh = hv * _rms_scale(hv)
        for i, o_ref in enumerate(outs):
            o_ref[...] = (hh * g_ref[i : i + 1, :]).astype(BF16)

    row = BS((tm, dn), lambda i: (i, 0))
    return pl.pallas_call(
        body, name=name, grid=(t // tm,),
        in_specs=[row, BS((ng, dn), lambda i: (0, 0)), _dep_spec(1)],
        out_specs=[row] * ng, out_shape=[SDS((t, dn), BF16)] * ng,
        compiler_params=_cparams(1),
    )(h, gammas, dep)


def _ffn_in(name, n, w_in, layer):
    t, dn = n.shape
    tm = _tile(t)

    def body(n_ref, wg_ref, wu_ref, gu_ref, a_ref):
        xv = n_ref[...]
        g = _dot(xv, wg_ref[...])
        u = _dot(xv, wu_ref[...])
        gu_ref[0] = g
        gu_ref[1] = u
        a_ref[...] = (g * jax.nn.sigmoid(g) * u).astype(BF16)

    return pl.pallas_call(
        body, name=name, grid=(NJ, t // tm),
        in_specs=[
            BS((tm, dn), lambda j, i: (i, 0)),
            BS((None, None, dn, FB), lambda j, i: (layer, j, 0, 0)),
            BS((None, None, dn, FB), lambda j, i: (layer, j + NJ, 0, 0)),
        ],
        out_specs=[BS((None, 2, tm, FB), lambda j, i: (j, 0, i, 0)), BS((None, tm, FB), lambda j, i: (j, i, 0))],
        out_shape=[SDS((NJ, 2, t, FB), F32), SDS((NJ, t, FB), BF16)],
        compiler_params=_cparams(2),
    )(n, w_in, w_in)


def _mm_res_norm(name, a, w, layer, h_in, gammas, scale):
    nk, t, kb = a.shape
    dn = w.shape[-1]
    ng = 0 if gammas is None else gammas.shape[0]
    tm = _tile(t)

    def body(*refs):
        a_ref, w_ref, h_ref = refs[:3]
        g_ref = refs[3] if ng else None
        outs = refs[3 + (1 if ng else 0) :]
        acc = _dot(a_ref[0], w_ref[0])
        for k in range(1, nk):
            acc += _dot(a_ref[k], w_ref[k])
        ho = h_ref[...] + scale * acc
        outs[0][...] = ho
        if ng:
            hh = ho * _rms_scale(ho)
            for i in range(ng):
                outs[1 + i][...] = (hh * g_ref[i : i + 1, :]).astype(BF16)

    row = BS((tm, dn), lambda i: (i, 0))
    in_specs = [BS((nk, tm, kb), lambda i: (0, i, 0)), BS((None, nk, kb, dn), lambda i: (layer, 0, 0, 0)), row]
    args = [a, w, h_in]
    if ng:
        in_specs.append(BS((ng, dn), lambda i: (0, 0)))
        args.append(gammas)
    return pl.pallas_call(
        body, name=name, grid=(t // tm,),
        in_specs=in_specs,
        out_specs=[row] * (1 + ng), out_shape=[SDS((t, dn), F32)] + [SDS((t, dn), BF16)] * ng,
        compiler_params=_cparams(1),
    )(*args)


def _qkv_proj(name, hn, w_qkv):
    t, dn = hn.shape
    wb = w_qkv.shape[-1]
    per = wb // 128
    tm = _tile(t)

    def body(x_ref, w_ref, o_ref):
        xv = x_ref[...]
        for j in range(NDEV):
            yv = _dot(xv, w_ref[j]).astype(BF16)
            for i in range(per):
                n = per * j + i
                o_ref[n // 8, :, (n % 8) * 128 : (n % 8 + 1) * 128] = yv[:, i * 128 : (i + 1) * 128]

    return pl.pallas_call(
        body, name=name, grid=(t // tm,),
        in_specs=[BS((tm, dn), lambda i: (i, 0)), BS((NDEV, dn, wb), lambda i: (0, 0, 0))],
        out_specs=BS((3, tm, dn), lambda i: (0, i, 0)), out_shape=SDS((3, t, dn), BF16),
        compiler_params=_cparams(1),
    )(hn, w_qkv)


def _rel_onehot(i):
    r = lax.broadcasted_iota(jnp.int32, (NREL_PAD, BAND), 0)
    j = lax.broadcasted_iota(jnp.int32, (NREL_PAD, BAND), 1)
    idx = jnp.clip(PAD + i - j, -MAX_REL, MAX_REL) + MAX_REL
    return (idx == r).astype(F32)


def _rel_bias_fwd(table):
    def body(t_ref, o_ref):
        i8 = pl.program_id(0)
        for ii in range(8):
            o_ref[:, ii, :] = _dot_exact(t_ref[...], _rel_onehot(i8 * 8 + ii))

    return pl.pallas_call(
        body, name="rel_bias_fwd", grid=(CHUNK // 8,),
        in_specs=[BS((HEADS_A, NREL_PAD), lambda i: (0, 0))],
        out_specs=BS((HEADS_A, 8, BAND), lambda i: (0, i, 0)), out_shape=SDS((HEADS_A, CHUNK, BAND), F32),
        compiler_params=_cparams(1),
    )(table)


def _rel_bias_bwd(dbias):
    def body(d_ref, o_ref):
        i8 = pl.program_id(0)
        acc = jnp.zeros((HEADS_A, NREL_PAD), F32)
        for ii in range(8):
            acc += lax.dot_general(
                d_ref[:, ii, :], _rel_onehot(i8 * 8 + ii), (((1,), (1,)), ((), ())), precision=HIGHEST,
                preferred_element_type=F32,
            )

        @pl.when(i8 == 0)
        def _():
            o_ref[...] = acc

        @pl.when(i8 > 0)
        def _():
            o_ref[...] += acc

    return pl.pallas_call(
        body, name="rel_bias_bwd", grid=(CHUNK // 8,),
        in_specs=[BS((HEADS_A, 8, BAND), lambda i: (0, i, 0))],
        out_specs=BS((HEADS_A, NREL_PAD), lambda i: (0, 0)), out_shape=SDS((HEADS_A, NREL_PAD), F32),
        compiler_params=_cparams(1),
    )(dbias)


def _window_bias(bias):
    b = bias.reshape(HEADS_A // 2, 2, CHUNK, BAND)
    per_chunk = [
        jnp.pad(b, ((0, 0), (0, 0), (0, 0), (cc * CHUNK, WINDOW - BAND - cc * CHUNK)), constant_values=NEG_INF)
        for cc in range(CHUNKS_PER_STEP)
    ]
    return jnp.stack(per_chunk, axis=1).reshape(HEADS_A // 2, STEP_ROWS, WINDOW)


def _window_bias_bwd(dwin):
    d = dwin.reshape(HEADS_A // 2, CHUNKS_PER_STEP, 2, CHUNK, WINDOW)
    return sum(d[:, cc, :, :, cc * CHUNK : cc * CHUNK + BAND] for cc in range(CHUNKS_PER_STEP)).reshape(HEADS_A, CHUNK, BAND)


def _step_rows(xs, lane):
    parts = []
    for cc in range(CHUNKS_PER_STEP):
        xc = xs[cc * CHUNK : (cc + 1) * CHUNK]
        parts.append(jnp.where(lane < 64, xc, jnp.zeros_like(xc)))
        parts.append(jnp.where(lane >= 64, xc, jnp.zeros_like(xc)))
    return jnp.concatenate(parts, axis=0)


def _pair_rows(ys, lane):
    parts = []
    for cc in range(CHUNKS_PER_STEP):
        y0 = ys[(2 * cc) * CHUNK : (2 * cc + 1) * CHUNK]
        y1 = ys[(2 * cc + 1) * CHUNK : (2 * cc + 2) * CHUNK]
        parts.append(jnp.where(lane < 64, y0, y1))
    return jnp.concatenate(parts, axis=0)


def _window_probs(q_rows, kwin, bias_win, first_key):
    s = _dot_nt(q_rows, kwin) * (CHUNK ** -0.5) + bias_win
    col = lax.broadcasted_iota(jnp.int32, s.shape, 1)
    s = jnp.where(col >= first_key, s, NEG_INF)
    e = jnp.exp(s - jnp.max(s, axis=-1, keepdims=True))
    return e / jnp.sum(e, axis=-1, keepdims=True)


def _attn_a_fwd(qkv3, bias_win, bl, seq):
    t, dn = qkv3.shape[1:]
    npair = dn // 128
    step = CHUNKS_PER_STEP * CHUNK

    def body(q_ref, k_ref, v_ref, b_ref, o_ref, kpad, vpad):
        kpad[0:PAD, :] = jnp.zeros((PAD, 128), BF16)
        vpad[0:PAD, :] = jnp.zeros((PAD, 128), BF16)
        kpad[PAD:, :] = k_ref[...]
        vpad[PAD:, :] = v_ref[...]
        lane = lax.broadcasted_iota(jnp.int32, (CHUNK, 128), 1)

        def chunks(it, carry):
            r0 = pl.multiple_of(it * step, step)
            q_rows = _step_rows(q_ref[pl.ds(r0, step), :], lane)
            p = _window_probs(q_rows, kpad[pl.ds(r0, WINDOW), :], b_ref[...], PAD - r0)
            o_rows = _dot(p.astype(BF16), vpad[pl.ds(r0, WINDOW), :])
            o_ref[pl.ds(r0, step), :] = _pair_rows(o_rows, lane).astype(BF16)
            return carry

        lax.fori_loop(0, seq // step, chunks, 0)

    return pl.pallas_call(
        body, name="attn_a_fwd", grid=(bl, npair),
        in_specs=[
            BS((None, seq, 128), lambda b, h: (0, b, h)),
            BS((None, seq, 128), lambda b, h: (1, b, h)),
            BS((None, seq, 128), lambda b, h: (2, b, h)),
            BS((None, STEP_ROWS, WINDOW), lambda b, h: (h, 0, 0)),
        ],
        out_specs=BS((seq, 128), lambda b, h: (b, h)), out_shape=SDS((t, dn), BF16),
        scratch_shapes=[pltpu.VMEM((PAD + seq, 128), BF16), pltpu.VMEM((PAD + seq, 128), BF16)],
        compiler_params=_cparams(2),
    )(qkv3, qkv3, qkv3, bias_win)


def _attn_a_bwd(qkv3, do, bias_win, bl, seq):
    t, dn = qkv3.shape[1:]
    npair = dn // 128
    step = CHUNKS_PER_STEP * CHUNK

    def body(q_ref, k_ref, v_ref, do_ref, b_ref, dqkv_ref, db_ref, kpad, vpad, dkacc, dvacc):
        b = pl.program_id(1)
        kpad[0:PAD, :] = jnp.zeros((PAD, 128), BF16)
        vpad[0:PAD, :] = jnp.zeros((PAD, 128), BF16)
        kpad[PAD:, :] = k_ref[...]
        vpad[PAD:, :] = v_ref[...]
        dkacc[...] = jnp.zeros_like(dkacc)
        dvacc[...] = jnp.zeros_like(dvacc)

        @pl.when(b == 0)
        def _():
            db_ref[...] = jnp.zeros_like(db_ref)

        lane = lax.broadcasted_iota(jnp.int32, (CHUNK, 128), 1)

        def chunks(it, carry):
            r0 = pl.multiple_of(it * step, step)
            q_rows = _step_rows(q_ref[pl.ds(r0, step), :], lane)
            do_rows = _step_rows(do_ref[pl.ds(r0, step), :], lane)
            kwin = kpad[pl.ds(r0, WINDOW), :]
            vwin = vpad[pl.ds(r0, WINDOW), :]
            p = _window_probs(q_rows, kwin, b_ref[...], PAD - r0)
            dp = _dot_nt(do_rows, vwin)
            ds = p * (dp - jnp.sum(p * dp, axis=-1, keepdims=True))
            db_ref[...] += ds
            dsb = (ds * (CHUNK ** -0.5)).astype(BF16)
            dqkv_ref[0, pl.ds(r0, step), :] = _pair_rows(_dot(dsb, kwin), lane).astype(BF16)
            dkacc[pl.ds(r0, WINDOW), :] += _dot_tn(dsb, q_rows)
            dvacc[pl.ds(r0, WINDOW), :] += _dot_tn(p.astype(BF16), do_rows)
            return carry

        lax.fori_loop(0, seq // step, chunks, 0)
        dqkv_ref[1] = dkacc[PAD:, :].astype(BF16)
        dqkv_ref[2] = dvacc[PAD:, :].astype(BF16)

    return pl.pallas_call(
        body, name="attn_a_bwd", grid=(npair, bl),
        in_specs=[
            BS((None, seq, 128), lambda h, b: (0, b, h)),
            BS((None, seq, 128), lambda h, b: (1, b, h)),
            BS((None, seq, 128), lambda h, b: (2, b, h)),
            BS((seq, 128), lambda h, b: (b, h)),
            BS((None, STEP_ROWS, WINDOW), lambda h, b: (h, 0, 0)),
        ],
        out_specs=[BS((3, seq, 128), lambda h, b: (0, b, h)), BS((None, STEP_ROWS, WINDOW), lambda h, b: (h, 0, 0))],
        out_shape=[SDS((3, t, dn), BF16), SDS((HEADS_A // 2, STEP_ROWS, WINDOW), F32)],
        scratch_shapes=[
            pltpu.VMEM((PAD + seq, 128), BF16), pltpu.VMEM((PAD + seq, 128), BF16),
            pltpu.VMEM((PAD + seq, 128), F32), pltpu.VMEM((PAD + seq, 128), F32),
        ],
        compiler_params=_cparams(2),
    )(qkv3, qkv3, qkv3, do, bias_win)


def _rope_tables(seq):
    half = ROPE // 2
    freqs = ROPE_THETA ** (-jnp.arange(half, dtype=F32) / half)
    ang = jnp.arange(seq, dtype=F32)[:, None] * freqs[None, :]
    cos, sin = jnp.cos(ang), jnp.sin(ang)
    c64 = jnp.concatenate([cos, cos], axis=1)
    s64 = jnp.concatenate([-sin, sin], axis=1)
    c192 = jnp.concatenate([jnp.ones((seq, NOPE), F32), c64], axis=1)
    s192 = jnp.concatenate([jnp.zeros((seq, NOPE), F32), s64], axis=1)
    p64 = np.zeros((ROPE, ROPE), np.float32)
    for col in range(ROPE):
        p64[(col + half) % ROPE, col] = 1.0
    p192 = np.zeros((QK_B, QK_B), np.float32)
    p192[NOPE:, NOPE:] = p64
    return c64, s64, jnp.asarray(p64), c192, s192, jnp.asarray(p192)


def _rope(xv, cos, sin_signed, swap):
    return xv * cos + _dot_exact(xv, swap) * sin_signed


def _rope_bwd(dy, cos, sin_signed, swap):
    return dy * cos + _dot_exact(dy * sin_signed, swap)


def _q_down(hn, w_dq, q_norm):
    t, dn = hn.shape
    ql = w_dq.shape[1]
    tm = _tile(t)

    def body(x_ref, w_ref, g_ref, pre_ref, cq_ref):
        pre = _dot(x_ref[...], w_ref[...])
        pre_ref[...] = pre
        cq_ref[...] = (pre * _rms_scale(pre) * g_ref[...]).astype(BF16)

    return pl.pallas_call(
        body, name="q_down", grid=(t // tm,),
        in_specs=[BS((tm, dn), lambda i: (i, 0)), BS((dn, ql), lambda i: (0, 0)), BS((1, ql), lambda i: (0, 0))],
        out_specs=[BS((tm, ql), lambda i: (i, 0))] * 2, out_shape=[SDS((t, ql), F32), SDS((t, ql), BF16)],
        compiler_params=_cparams(1),
    )(hn, w_dq, q_norm)


def _q_up(cq, w_uq, c192, s192, p192, seq):
    t, ql = cq.shape
    tm = _tile(min(seq, 512), min(seq, 512))
    nseq = seq // tm

    def body(x_ref, w_ref, c_ref, s_ref, p_ref, o_ref):
        qf = _dot(x_ref[...], w_ref[...])
        o_ref[...] = _rope(qf, c_ref[...], s_ref[...], p_ref[...]).astype(BF16)

    pos = BS((tm, QK_B), lambda h, i: (i % nseq, 0))
    return pl.pallas_call(
        body, name="q_up", grid=(HEADS_B, t // tm),
        in_specs=[
            BS((tm, ql), lambda h, i: (i, 0)), BS((None, ql, QK_B), lambda h, i: (h, 0, 0)), pos, pos,
            BS((QK_B, QK_B), lambda h, i: (0, 0)),
        ],
        out_specs=BS((None, tm, QK_B), lambda h, i: (h, i, 0)), out_shape=SDS((HEADS_B, t, QK_B), BF16),
        compiler_params=_cparams(2),
    )(cq, w_uq, c192, s192, p192)


def _kv_down(hk, w_down, latent_norm, c64, s64, p64, seq):
    t, dn = hk.shape
    wd = w_down.shape[1]
    tm = _tile(min(seq, 512), min(seq, 512))
    nseq = seq // tm

    def body(x_ref, w_ref, g_ref, c_ref, s_ref, p_ref, ckr_ref, ckv_ref, kr_ref):
        ckr = _dot(x_ref[...], w_ref[...])
        ckr_ref[...] = ckr
        lat = ckr[:, :KV_LORA]
        ckv_ref[...] = (lat * _rms_scale(lat) * g_ref[...]).astype(BF16)
        kr_ref[...] = _rope(ckr[:, KV_LORA:], c_ref[...], s_ref[...], p_ref[...]).astype(BF16)

    pos = BS((tm, ROPE), lambda i: (i % nseq, 0))
    return pl.pallas_call(
        body, name="kv_down", grid=(t // tm,),
        in_specs=[
            BS((tm, dn), lambda i: (i, 0)), BS((dn, wd), lambda i: (0, 0)), BS((1, KV_LORA), lambda i: (0, 0)), pos, pos,
            BS((ROPE, ROPE), lambda i: (0, 0)),
        ],
        out_specs=[BS((tm, wd), lambda i: (i, 0)), BS((tm, KV_LORA), lambda i: (i, 0)), BS((tm, ROPE), lambda i: (i, 0))],
        out_shape=[SDS((t, wd), F32), SDS((t, KV_LORA), BF16), SDS((t, ROPE), BF16)],
        compiler_params=_cparams(1),
    )(hk, w_down, latent_norm, c64, s64, p64)


def _kv_up(ckv, w_up):
    t, kl = ckv.shape
    hb = w_up.shape[-1]
    tm = _tile(t)

    def body(x_ref, w_ref, o_ref):
        o_ref[...] = _dot(x_ref[...], w_ref[...]).astype(BF16)

    return pl.pallas_call(
        body, name="kv_up", grid=(HEADS_B, t // tm),
        in_specs=[BS((tm, kl), lambda h, i: (i, 0)), BS((None, kl, hb), lambda h, i: (h, 0, 0))],
        out_specs=BS((tm, hb), lambda h, i: (i, h)), out_shape=SDS((t, HEADS_B * hb), BF16),
        compiler_params=_cparams(2),
    )(ckv, w_up)


def _mla_probs(qi, kcat, row0, n_keys):
    s = _dot_nt(qi, kcat) * (QK_B ** -0.5)
    rows = lax.broadcasted_iota(jnp.int32, (qi.shape[0], n_keys), 0) + row0
    cols = lax.broadcasted_iota(jnp.int32, (qi.shape[0], n_keys), 1)
    s = jnp.where(jnp.right_shift(cols, 6) <= jnp.right_shift(rows, 6), s, NEG_INF)
    e = jnp.exp(s - jnp.max(s, axis=-1, keepdims=True))
    return e / jnp.sum(e, axis=-1, keepdims=True)


def _mla_fwd(q, kv, kr, bl, seq):
    t = kv.shape[0]
    tq = min(MLA_TQ, seq)

    def body(q_ref, kn_ref, v_ref, kr_ref, o_ref):
        kcat = jnp.concatenate([kn_ref[...], kr_ref[...]], axis=1)
        vv = v_ref[...]
        for i in range(seq // tq):
            n_keys = (i + 1) * tq
            p = _mla_probs(q_ref[i * tq : (i + 1) * tq, :], kcat[:n_keys], i * tq, n_keys)
            o_ref[i * tq : (i + 1) * tq, :] = _dot(p.astype(BF16), vv[:n_keys]).astype(BF16)

    return pl.pallas_call(
        body, name="mla_fwd", grid=(bl, HEADS_B),
        in_specs=[
            BS((None, seq, QK_B), lambda b, h: (h, b, 0)),
            BS((seq, NOPE), lambda b, h: (b, 2 * h)),
            BS((seq, V_DIM), lambda b, h: (b, 2 * h + 1)),
            BS((seq, ROPE), lambda b, h: (b, 0)),
        ],
        out_specs=BS((seq, V_DIM), lambda b, h: (b, h)), out_shape=SDS((t, HEADS_B * V_DIM), BF16),
        compiler_params=_cparams(2),
    )(q, kv, kv, kr)


def _mla_bwd(q, kv, kr, do, c192, s192, p192, bl, seq):
    t = kv.shape[0]
    tq = min(MLA_TQ, seq)

    def body(q_ref, kn_ref, v_ref, kr_ref, do_ref, c_ref, s_ref, p_ref, dq_ref, dkv_ref, dkr_ref, dkacc, dvacc):
        h = pl.program_id(1)
        kcat = jnp.concatenate([kn_ref[...], kr_ref[...]], axis=1)
        vv = v_ref[...]
        dkacc[...] = jnp.zeros_like(dkacc)
        dvacc[...] = jnp.zeros_like(dvacc)
        for i in range(seq // tq):
            n_keys = (i + 1) * tq
            rows = slice(i * tq, (i + 1) * tq)
            qi = q_ref[rows, :]
            doi = do_ref[rows, :]
            p = _mla_probs(qi, kcat[:n_keys], i * tq, n_keys)
            dp = _dot_nt(doi, vv[:n_keys])
            ds = p * (dp - jnp.sum(p * dp, axis=-1, keepdims=True))
            dsb = (ds * (QK_B ** -0.5)).astype(BF16)
            dq = _dot(dsb, kcat[:n_keys])
            dq_ref[rows, :] = _rope_bwd(dq, c_ref[rows, :], s_ref[rows, :], p_ref[...]).astype(BF16)
            dkacc[0:n_keys, :] += _dot_tn(dsb, qi)
            dvacc[0:n_keys, :] += _dot_tn(p.astype(BF16), doi)
        dk = dkacc[...]
        dkv_ref[:, :NOPE] = dk[:, :NOPE].astype(BF16)
        dkv_ref[:, NOPE:] = dvacc[...].astype(BF16)

        @pl.when(h == 0)
        def _():
            dkr_ref[...] = dk[:, NOPE:]

        @pl.when(h > 0)
        def _():
            dkr_ref[...] += dk[:, NOPE:]

    return pl.pallas_call(
        body, name="mla_bwd", grid=(bl, HEADS_B),
        in_specs=[
            BS((None, seq, QK_B), lambda b, h: (h, b, 0)),
            BS((seq, NOPE), lambda b, h: (b, 2 * h)),
            BS((seq, V_DIM), lambda b, h: (b, 2 * h + 1)),
            BS((seq, ROPE), lambda b, h: (b, 0)),
            BS((seq, V_DIM), lambda b, h: (b, h)),
            BS((seq, QK_B), lambda b, h: (0, 0)),
            BS((seq, QK_B), lambda b, h: (0, 0)),
            BS((QK_B, QK_B), lambda b, h: (0, 0)),
        ],
        out_specs=[
            BS((None, seq, QK_B), lambda b, h: (h, b, 0)),
            BS((seq, NOPE + V_DIM), lambda b, h: (b, h)),
            BS((seq, ROPE), lambda b, h: (b, 0)),
        ],
        out_shape=[SDS((HEADS_B, t, QK_B), BF16), SDS((t, HEADS_B * (NOPE + V_DIM)), BF16), SDS((t, ROPE), F32)],
        scratch_shapes=[pltpu.VMEM((seq, QK_B), F32), pltpu.VMEM((seq, V_DIM), F32)],
        compiler_params=_cparams(2),
    )(q, kv, kv, kr, do, c192, s192, p192)


def _loss_final(h, target, gamma):
    t, dn = h.shape
    tm = _tile(t)
    nt = t // tm

    def body(h_ref, t_ref, g_ref, dh_ref, dg_ref, loss_ref):
        i = pl.program_id(0)
        hv = h_ref[...]
        r = _rms_scale(hv)
        hh = hv * r
        gam = g_ref[...]
        err = hh * gam - t_ref[...]
        part = 0.5 * jnp.sum(jnp.mean(err * err, axis=-1, keepdims=True))

        @pl.when(i == 0)
        def _():
            loss_ref[...] = jnp.zeros_like(loss_ref)

        loss_ref[...] += part
        dy = err * (1.0 / dn)
        _acc_rows(dg_ref, dy * hh, i, nt)
        t1 = dy * gam
        dh_ref[...] = r * (t1 - hh * jnp.mean(t1 * hh, axis=-1, keepdims=True))

    row = BS((tm, dn), lambda i: (i, 0))
    return pl.pallas_call(
        body, name="loss_final", grid=(nt,),
        in_specs=[row, row, BS((1, dn), lambda i: (0, 0))],
        out_specs=[row, BS((8, dn), lambda i: (0, 0)), BS((8, 128), lambda i: (0, 0))],
        out_shape=[SDS((t, dn), F32), SDS((8, dn), F32), SDS((8, 128), F32)],
        compiler_params=_cparams(1),
    )(h, target, gamma)


def _ffn_bwd_in(name, dh, w_out, layer, gu, dep=None):
    t, dn = dh.shape
    tm = _tile(t)

    def body(dh_ref, w_ref, gu_ref, *rest):
        o_ref = rest[-1]
        xv = (0.5 * dh_ref[...]).astype(BF16)
        da = _dot_nt(xv, w_ref[...])
        g = gu_ref[0]
        u = gu_ref[1]
        sg = jax.nn.sigmoid(g)
        o_ref[0] = (da * u * (sg * (1.0 + g * (1.0 - sg)))).astype(BF16)
        o_ref[1] = (da * (g * sg)).astype(BF16)

    blk = BS((None, 2, tm, FB), lambda j, i: (j, 0, i, 0))
    deps = [] if dep is None else [dep]
    return pl.pallas_call(
        body, name=name, grid=(NJ, t // tm),
        in_specs=[BS((tm, dn), lambda j, i: (i, 0)), BS((None, None, FB, dn), lambda j, i: (layer, j, 0, 0)), blk]
        + [_dep_spec(2)] * len(deps),
        out_specs=blk, out_shape=SDS((NJ, 2, t, FB), BF16),
        compiler_params=_cparams(2),
    )(dh, w_out, gu, *deps)


def _mm_nt_plain(name, xf, w, dep=None):
    t, dn = xf.shape
    n = w.shape[0]
    tm = _tile(t)

    def body(x_ref, w_ref, *rest):
        rest[-1][...] = _dot_nt(x_ref[...].astype(BF16), w_ref[...]).astype(BF16)

    deps = [] if dep is None else [dep]
    return pl.pallas_call(
        body, name=name, grid=(t // tm,),
        in_specs=[BS((tm, dn), lambda i: (i, 0)), BS((n, dn), lambda i: (0, 0))] + [_dep_spec(1)] * len(deps),
        out_specs=BS((tm, n), lambda i: (i, 0)), out_shape=SDS((t, n), BF16),
        compiler_params=_cparams(1),
    )(xf, w, *deps)


def _mm_tn(name, xa, x_spec, ya, y_spec, out_shape, out_spec, nj, nt, y_scale=None):
    acc_shape = tuple(d for d in out_spec.block_shape if d is not None)

    def body(x_ref, y_ref, o_ref, acc):
        i = pl.program_id(1)
        yv = y_ref[...]
        if yv.dtype != BF16:
            yv = (yv if y_scale is None else y_scale * yv).astype(BF16)
        part = _dot_tn(x_ref[...], yv)

        @pl.when(i == 0)
        def _():
            acc[...] = part

        @pl.when(i > 0)
        def _():
            acc[...] += part

        @pl.when(i == nt - 1)
        def _():
            o_ref[...] = acc[...].astype(BF16)

    return pl.pallas_call(
        body, name=name, grid=(nj, nt),
        in_specs=[x_spec, y_spec], out_specs=out_spec, out_shape=SDS(out_shape.shape, BF16),
        scratch_shapes=[pltpu.VMEM(acc_shape, F32)],
        compiler_params=_cparams(2),
    )(xa, ya)


def _dw_qkv(hn, dqkv3, wb):
    t, dn = hn.shape
    per = wb // 128
    tm = _tile(t)

    def body(x_ref, y_ref, o_ref, acc):
        i = pl.program_id(0)
        xv = x_ref[...]
        for j in range(NDEV):
            cols = [y_ref[(per * j + k) // 8, :, ((per * j + k) % 8) * 128 : ((per * j + k) % 8 + 1) * 128] for k in range(per)]
            part = _dot_tn(xv, jnp.concatenate(cols, axis=1))

            @pl.when(i == 0)
            def _():
                acc[j] = part

            @pl.when(i > 0)
            def _():
                acc[j] += part

        @pl.when(i == t // tm - 1)
        def _():
            o_ref[...] = acc[...].astype(BF16)

    return pl.pallas_call(
        body, name="dw_qkv", grid=(t // tm,),
        in_specs=[BS((tm, dn), lambda i: (i, 0)), BS((3, tm, dn), lambda i: (0, i, 0))],
        out_specs=BS((NDEV, dn, wb), lambda i: (0, 0, 0)), out_shape=SDS((NDEV, dn, wb), BF16),
        scratch_shapes=[pltpu.VMEM((NDEV, dn, wb), F32)],
        compiler_params=_cparams(1),
    )(hn, dqkv3)


def _mm_nt_epi(name, ya, y_spec, wa, w_spec, nj, n_out, extra, out_shapes, out_specs, epilogue, tm, nt, mm_fn=None):
    n_extra = len(extra)
    n_outs = len(out_shapes)

    def body(*refs):
        y_ref, w_ref = refs[:2]
        ex = refs[2 : 2 + n_extra]
        outs = refs[2 + n_extra : 2 + n_extra + n_outs]
        i = pl.program_id(0)
        j = pl.program_id(1)
        part = _dot_nt(y_ref[...], w_ref[...]) if mm_fn is None else mm_fn(y_ref, w_ref)
        if nj == 1:
            epilogue(part, ex, outs, i, nt)
            return
        acc = refs[-1]

        @pl.when(j == 0)
        def _():
            acc[...] = part

        @pl.when(j > 0)
        def _():
            acc[...] += part

        @pl.when(j == nj - 1)
        def _():
            epilogue(acc[...], ex, outs, i, nt)

    return pl.pallas_call(
        body, name=name, grid=(nt, nj),
        in_specs=[y_spec, w_spec] + [spec for _, spec in extra],
        out_specs=out_specs, out_shape=out_shapes,
        scratch_shapes=[] if nj == 1 else [pltpu.VMEM((tm, n_out), F32)],
        compiler_params=_cparams(2),
    )(ya, wa, *[arr for arr, _ in extra])


def _norm_bwd(dn, hv, gam):
    r = _rms_scale(hv)
    hh = hv * r
    t1 = dn * gam
    return r * (t1 - hh * jnp.mean(t1 * hh, axis=-1, keepdims=True)), dn * hh


def _norm_bwd_epilogue(has_res, out_dtype):
    def epilogue(dn, ex, outs, i, nt):
        dh, dg_rows = _norm_bwd(dn, ex[0][...], ex[1][...])
        _acc_rows(outs[1], dg_rows, i, nt)
        if has_res:
            dh = dh + ex[2][...]
        outs[0][...] = dh.astype(out_dtype)

    return epilogue


def _mm_nt_norm_bwd(name, ya, y_spec, wa, w_spec, nj, h, gamma, res, out_dtype, mm_fn=None, want_tm=512, dep=None):
    t, n = h.shape
    tm = _tile(t, want_tm)
    nt = t // tm
    row = BS((tm, n), lambda i, j: (i, 0))
    extra = [(h, row), (gamma, BS((1, n), lambda i, j: (0, 0)))]
    if res is not None:
        extra.append((res, row))
    if dep is not None:
        extra.append((dep, _dep_spec(2)))
    return _mm_nt_epi(
        name, ya, y_spec, wa, w_spec, nj, n, extra, [SDS((t, n), out_dtype), SDS((8, n), F32)],
        [row, BS((8, n), lambda i, j: (0, 0))], _norm_bwd_epilogue(res is not None, out_dtype), tm, nt, mm_fn,
    )


def _dev_block(jj):
    return jj // 2 + NJ * (jj % 2)


def _ffn_bwd(tag, dh, n_in, h_in, gamma, gu, a, w_in, w_out, more_grads, dep):
    t, dn = dh.shape
    tm = _tile(t)
    nt = t // tm
    layer = 0
    dgu = _ffn_bwd_in(f"{tag}_bwd_in", dh, w_out, layer, gu, dep).reshape(2 * NJ, t, FB)
    dw_out = _mm_tn(
        f"{tag}_dw_out", a, BS((None, tm, FB), lambda j, i: (j, i, 0)), dh, BS((tm, dn), lambda j, i: (i, 0)),
        SDS((NJ, FB, dn), F32), BS((None, FB, dn), lambda j, i: (j, 0, 0)), NJ, nt, y_scale=0.5,
    )
    dw_in = _mm_tn(
        f"{tag}_dw_in", n_in, BS((tm, dn), lambda j, i: (i, 0)), dgu, BS((None, tm, FB), lambda j, i: (j, i, 0)),
        SDS((NDEV, dn, FB), F32), BS((None, dn, FB), lambda j, i: (_dev_block(j), 0, 0)), NDEV, nt,
    )
    entries = [("scatter", dw_in), ("scatter", dw_out.reshape(NDEV, NJ * FB // NDEV, dn))] + [("scatter", g) for g in more_grads]
    (started,), token = _exchange_start(f"{tag}_reduce_start", [entries])
    dh_in, dgam = _mm_nt_norm_bwd(
        f"{tag}_dn", dgu, BS((None, tm, FB), lambda i, j: (j, i, 0)),
        w_in, BS((None, None, dn, FB), lambda i, j: (layer, _dev_block(j), 0, 0)), NDEV, h_in, gamma, dh, F32, dep=token,
    )
    return dh_in, dgam, started


def _dqkv_mm(per):
    def mm(y_ref, w_ref):
        acc = None
        for j in range(NDEV):
            cols = [y_ref[(per * j + k) // 8, :, ((per * j + k) % 8) * 128 : ((per * j + k) % 8 + 1) * 128] for k in range(per)]
            part = _dot_nt(jnp.concatenate(cols, axis=1), w_ref[j])
            acc = part if acc is None else acc + part
        return acc

    return mm


def _kv_latent_bwd(dkv, w_up, ckr, latent_norm, dkr, c64, s64, p64, seq):
    t, wd = ckr.shape
    hb = w_up.shape[-1]
    tm = _tile(min(seq, 512), min(seq, 512))
    nt = t // tm
    nseq = seq // tm

    def epilogue(dn, ex, outs, i, nt_):
        dlat, dg_rows = _norm_bwd(dn, ex[0][...], ex[1][...])
        _acc_rows(outs[1], dg_rows, i, nt_)
        outs[0][:, :KV_LORA] = dlat.astype(BF16)
        outs[0][:, KV_LORA:] = _rope_bwd(ex[2][...], ex[3][...], ex[4][...], ex[5][...]).astype(BF16)

    pos = BS((tm, ROPE), lambda i, j: (i % nseq, 0))
    extra = [
        (ckr, BS((tm, KV_LORA), lambda i, j: (i, 0))), (latent_norm, BS((1, KV_LORA), lambda i, j: (0, 0))),
        (dkr, BS((tm, ROPE), lambda i, j: (i, 0))), (c64, pos), (s64, pos), (p64, BS((ROPE, ROPE), lambda i, j: (0, 0))),
    ]
    return _mm_nt_epi(
        "kv_latent_bwd", dkv, BS((tm, hb), lambda i, j: (i, j)), w_up, BS((None, KV_LORA, hb), lambda i, j: (j, 0, 0)),
        HEADS_B, KV_LORA, extra, [SDS((t, wd), BF16), SDS((8, KV_LORA), F32)],
        [BS((tm, wd), lambda i, j: (i, 0)), BS((8, KV_LORA), lambda i, j: (0, 0))], epilogue, tm, nt,
    )


def _adamw(name, parts, w, m, v):
    n_layers = len(parts)
    rows, cols = w.shape[0] // n_layers, w.shape[1]
    tr = max(d for d in range(8, min(rows, 256) + 1, 8) if rows % d == 0)
    nb = rows // tr

    def body(*refs):
        p_refs = refs[:n_layers]
        w_ref, m_ref, v_ref, g_ref, d_ref, nm_ref, nv_ref = refs[n_layers : n_layers + 7]
        layer = pl.program_id(0)
        for lp in range(n_layers):

            @pl.when(layer == lp)
            def _():
                g = p_refs[lp][0].astype(F32)
                for k in range(1, NDEV):
                    g = g + p_refs[lp][k].astype(F32)
                g_ref[...] = g

        g = g_ref[...]
        nm = ADAM_B1 * m_ref[...] + (1.0 - ADAM_B1) * g
        nv = ADAM_B2 * v_ref[...] + (1.0 - ADAM_B2) * (g * g)
        nm_ref[...] = nm
        nv_ref[...] = nv
        m_hat = nm / (1.0 - ADAM_B1 ** ADAM_STEP)
        v_hat = nv / (1.0 - ADAM_B2 ** ADAM_STEP)
        d_ref[...] = -ADAM_LR * (m_hat / (jnp.sqrt(v_hat) + ADAM_EPS) + ADAM_WD * w_ref[...])

    def part_spec(lp):
        return BS((NDEV, tr, cols), lambda l, i: (0, jnp.where(l == lp, i, jnp.where(l < lp, 0, nb - 1)), 0))

    row = BS((tr, cols), lambda l, i: (l * nb + i, 0))
    return pl.pallas_call(
        body, name=name, grid=(n_layers, nb),
        in_specs=[part_spec(lp) for lp in range(n_layers)] + [row, row, row],
        out_specs=[row] * 4, out_shape=[SDS(w.shape, F32)] * 4,
        compiler_params=_cparams(2),
    )(*parts, w, m, v)


def _pack_small(ffn1_norm, mix_norm, ffn2_norm, kv_norm, final_norm, q_norm, latent_norm, rel_bias, last_row):
    dn = ffn1_norm.shape[-1]

    def rows_of(a, n_rows):
        flat = a.reshape(-1)
        return jnp.pad(flat, (0, n_rows * dn - flat.shape[0])).reshape(n_rows, dn)

    return jnp.concatenate(
        [
            ffn1_norm.reshape(2, dn), mix_norm.reshape(2, dn), ffn2_norm.reshape(2, dn), kv_norm.reshape(1, dn),
            final_norm.reshape(1, dn), rows_of(q_norm, 1), rows_of(latent_norm, 1), rows_of(rel_bias, 5), rows_of(last_row, 1),
        ],
        axis=0,
    )


def _unpack_small(pack):
    dn = pack.shape[-1]
    return dict(
        ffn1_norm=pack[0:2], mix_norm=pack[2:4], ffn2_norm=pack[4:6], kv_norm=pack[6], final_norm=pack[7],
        b_q_norm=pack[8, :Q_LORA].reshape(1, Q_LORA), kv_latent_norm=pack[9, :KV_LORA],
        a_rel_bias=pack[10:15].reshape(-1)[: HEADS_A * NREL].reshape(1, HEADS_A, NREL), last=pack[15],
    )


def kernel(x, ffn1_norm, ffn1_w_in, ffn1_w_out, mix_norm, ffn2_norm, ffn2_w_in, ffn2_w_out, a_w_qkv, a_rel_bias, a_w_o, kv_norm, kv_w_down, kv_latent_norm, kv_w_up, b_w_dq, b_q_norm, b_w_uq, b_w_o, final_norm, loss_target, m_ffn1_norm, m_ffn1_w_in, m_ffn1_w_out, m_mix_norm, m_ffn2_norm, m_ffn2_w_in, m_ffn2_w_out, m_a_w_qkv, m_a_rel_bias, m_a_w_o, m_kv_norm, m_kv_w_down, m_kv_latent_norm, m_kv_w_up, m_b_w_dq, m_b_q_norm, m_b_w_uq, m_b_w_o, m_final_norm, v_ffn1_norm, v_ffn1_w_in, v_ffn1_w_out, v_mix_norm, v_ffn2_norm, v_ffn2_w_in, v_ffn2_w_out, v_a_w_qkv, v_a_rel_bias, v_a_w_o, v_kv_norm, v_kv_w_down, v_kv_latent_norm, v_kv_w_up, v_b_w_dq, v_b_q_norm, v_b_w_uq, v_b_w_o, v_final_norm):
    bl, seq, dn = x.shape
    t = bl * seq
    tm = _tile(t)
    nt = t // tm
    x2 = x.reshape(t, dn)
    target2 = loss_target.reshape(t, dn)

    def gathered(*ws):
        return [("gather", w.astype(BF16)) for w in ws]

    ag, token = _exchange_start(
        "gather_start",
        [
            gathered(ffn1_w_in[0]), gathered(ffn1_w_out[0]), gathered(a_w_qkv[0], a_w_o[0]), gathered(ffn2_w_in[0], ffn2_w_out[0]),
            gathered(kv_w_down, kv_w_up), gathered(ffn1_w_in[1], ffn1_w_out[1]), gathered(b_w_dq[0], b_w_uq[0], b_w_o[0]),
            gathered(ffn2_w_in[1], ffn2_w_out[1]),
        ],
    )

    def as_w_in(w):
        return w.reshape(1, NDEV, dn, FB)

    def as_w_out(w):
        return w.reshape(1, NJ, FB, dn)

    c64, s64, p64, c192, s192, p192 = _rope_tables(seq)
    q_norm = b_q_norm.reshape(1, Q_LORA)
    latent_norm = kv_latent_norm.reshape(1, KV_LORA)
    bias = _window_bias(_rel_bias_fwd(jnp.pad(a_rel_bias[0], ((0, 0), (0, NREL_PAD - NREL)))))

    h0, h1, h2, n1, hn, n2, gu1, gu2, a1, a2, w_in1, w_in2, w_out1, w_out2 = ([None, None] for _ in range(14))
    h0[0] = x2
    (n1[0],) = _norm_fwd("norm_x", x2, ffn1_norm[0:1], token)
    w_in1[0] = as_w_in(_exchange_wait("gather_wait_0", ag[0], n1[0])[0])
    gu1[0], a1[0] = _ffn_in("ffn1_in_0", n1[0], w_in1[0], 0)
    w_out1[0] = as_w_out(_exchange_wait("gather_wait_1", ag[1], a1[0])[0])
    h1[0], hn[0] = _mm_res_norm("ffn1_out_0", a1[0], w_out1[0], 0, h0[0], mix_norm[0:1], 0.5)
    w_qkv, w_o_a = _exchange_wait("gather_wait_2", ag[2], hn[0])
    qkv_wb = w_qkv.shape[-1]
    w_o_a = w_o_a.reshape(1, 1, dn, dn)
    qkv3 = _qkv_proj("qkv_proj", hn[0], w_qkv)
    o_a = _attn_a_fwd(qkv3, bias, bl, seq)
    h2[0], n2[0] = _mm_res_norm("attn_a_out", o_a.reshape(1, t, dn), w_o_a, 0, h1[0], ffn2_norm[0:1], 1.0)
    w_in2[0], w_out2[0] = _exchange_wait("gather_wait_3", ag[3], n2[0])
    w_in2[0], w_out2[0] = as_w_in(w_in2[0]), as_w_out(w_out2[0])
    gu2[0], a2[0] = _ffn_in("ffn2_in_0", n2[0], w_in2[0], 0)
    h0[1], hk, n1[1] = _mm_res_norm(
        "ffn2_out_0", a2[0], w_out2[0], 0, h2[0], jnp.concatenate([kv_norm.reshape(1, dn), ffn1_norm[1:2]], axis=0), 0.5
    )
    w_down, w_up = _exchange_wait("gather_wait_4", ag[4], hk)
    w_down = w_down.reshape(dn, KV_LORA + ROPE)
    ckr, ckv, kr = _kv_down(hk, w_down, latent_norm, c64, s64, p64, seq)
    kv = _kv_up(ckv, w_up)
    w_in1[1], w_out1[1] = _exchange_wait("gather_wait_5", ag[5], kv)
    w_in1[1], w_out1[1] = as_w_in(w_in1[1]), as_w_out(w_out1[1])
    gu1[1], a1[1] = _ffn_in("ffn1_in_1", n1[1], w_in1[1], 0)
    h1[1], hn[1] = _mm_res_norm("ffn1_out_1", a1[1], w_out1[1], 0, h0[1], mix_norm[1:2], 0.5)
    w_dq, w_uq, w_o_b = _exchange_wait("gather_wait_6", ag[6], hn[1])
    w_dq = w_dq.reshape(dn, Q_LORA)
    w_o_b = w_o_b.reshape(1, 1, dn, dn)
    cq_pre, cq = _q_down(hn[1], w_dq, q_norm)
    q = _q_up(cq, w_uq, c192, s192, p192, seq)
    o_b = _mla_fwd(q, kv, kr, bl, seq)
    h2[1], n2[1] = _mm_res_norm("attn_b_out", o_b.reshape(1, t, dn), w_o_b, 0, h1[1], ffn2_norm[1:2], 1.0)
    w_in2[1], w_out2[1] = _exchange_wait("gather_wait_7", ag[7], n2[1])
    w_in2[1], w_out2[1] = as_w_in(w_in2[1]), as_w_out(w_out2[1])
    gu2[1], a2[1] = _ffn_in("ffn2_in_1", n2[1], w_in2[1], 0)
    (h_last,) = _mm_res_norm("ffn2_out_1", a2[1], w_out2[1], 0, h2[1], None, 0.5)
    dh, dg_final, loss_part = _loss_final(h_last, target2, final_norm.reshape(1, dn))

    dg_ffn1, dg_mix, dg_ffn2, rs_ffn1, rs_ffn2 = ([None, None] for _ in range(5))
    col128 = BS((tm, 128), lambda j, i: (i, j))
    full_row = BS((tm, dn), lambda j, i: (i, 0))
    dh, dg_ffn2[1], rs_ffn2[1] = _ffn_bwd("ffn2_1", dh, n2[1], h2[1], ffn2_norm[1:2], gu2[1], a2[1], w_in2[1], w_out2[1], [], None)
    do_b = _mm_nt_plain("attn_b_do", dh, w_o_b.reshape(dn, dn))
    dw_o_b = _mm_tn("attn_b_dwo", o_b, col128, dh, full_row, SDS((NDEV, 128, dn), F32), BS((None, 128, dn), lambda j, i: (j, 0, 0)), NDEV, nt)
    dq_pre, dkv, dkr = _mla_bwd(q, kv, kr, do_b, c192, s192, p192, bl, seq)
    dw_uq = _mm_tn(
        "dw_uq", cq, BS((tm, Q_LORA), lambda j, i: (i, 0)), dq_pre, BS((None, tm, QK_B), lambda j, i: (j, i, 0)),
        SDS((HEADS_B, Q_LORA, QK_B), F32), BS((None, Q_LORA, QK_B), lambda j, i: (j, 0, 0)), HEADS_B, nt,
    )
    dcq_pre, dg_q = _mm_nt_norm_bwd(
        "dcq", dq_pre, BS((None, tm, QK_B), lambda i, j: (j, i, 0)), w_uq, BS((None, Q_LORA, QK_B), lambda i, j: (j, 0, 0)),
        HEADS_B, cq_pre, q_norm, None, BF16,
    )
    dw_dq = _mm_tn(
        "dw_dq", hn[1], col128, dcq_pre, BS((tm, Q_LORA), lambda j, i: (i, 0)),
        SDS((NDEV, 128, Q_LORA), F32), BS((None, 128, Q_LORA), lambda j, i: (j, 0, 0)), NDEV, nt,
    )
    dh, dg_mix[1] = _mm_nt_norm_bwd(
        "dhn_b", dcq_pre, BS((tm, Q_LORA), lambda i, j: (i, 0)), w_dq, BS((dn, Q_LORA), lambda i, j: (0, 0)),
        1, h1[1], mix_norm[1:2], dh, F32,
    )
    dh, dg_ffn1[1], rs_ffn1[1] = _ffn_bwd(
        "ffn1_1", dh, n1[1], h0[1], ffn1_norm[1:2], gu1[1], a1[1], w_in1[1], w_out1[1], [dw_o_b, dw_uq, dw_dq], None
    )
    dw_up = _mm_tn(
        "dw_up", ckv, BS((tm, KV_LORA), lambda j, i: (i, 0)), dkv, BS((tm, NOPE + V_DIM), lambda j, i: (i, j)),
        SDS((HEADS_B, KV_LORA, NOPE + V_DIM), F32), BS((None, KV_LORA, NOPE + V_DIM), lambda j, i: (j, 0, 0)), HEADS_B, nt,
    )
    dckr, dg_latent = _kv_latent_bwd(dkv, w_up, ckr, latent_norm, dkr, c64, s64, p64, seq)
    dw_down = _mm_tn(
        "dw_down", hk, col128, dckr, BS((tm, KV_LORA + ROPE), lambda j, i: (i, 0)),
        SDS((NDEV, 128, KV_LORA + ROPE), F32), BS((None, 128, KV_LORA + ROPE), lambda j, i: (j, 0, 0)), NDEV, nt,
    )
    dh, dg_kv = _mm_nt_norm_bwd(
        "dhk", dckr, BS((tm, KV_LORA + ROPE), lambda i, j: (i, 0)), w_down, BS((dn, KV_LORA + ROPE), lambda i, j: (0, 0)),
        1, h0[1], kv_norm.reshape(1, dn), dh, F32,
    )
    dh, dg_ffn2[0], rs_ffn2[0] = _ffn_bwd(
        "ffn2_0", dh, n2[0], h2[0], ffn2_norm[0:1], gu2[0], a2[0], w_in2[0], w_out2[0], [dw_up, dw_down], None
    )
    do_a = _mm_nt_plain("attn_a_do", dh, w_o_a.reshape(dn, dn))
    dw_o_a = _mm_tn("attn_a_dwo", o_a, col128, dh, full_row, SDS((NDEV, 128, dn), F32), BS((None, 128, dn), lambda j, i: (j, 0, 0)), NDEV, nt)
    dqkv3, dbias = _attn_a_bwd(qkv3, do_a, bias, bl, seq)
    dw_qkv = _dw_qkv(hn[0], dqkv3, qkv_wb)
    dh, dg_mix[0] = _mm_nt_norm_bwd(
        "dhn_a", dqkv3, BS((3, tm, dn), lambda i, j: (0, i, 0)), w_qkv, BS((NDEV, dn, qkv_wb), lambda i, j: (0, 0, 0)),
        1, h1[0], mix_norm[0:1], dh, F32, mm_fn=_dqkv_mm(qkv_wb // 128),
    )
    (rs_attn_a,), token = _exchange_start("attn_a_reduce_start", [[("scatter", dw_o_a), ("scatter", dw_qkv)]])
    dh, dg_ffn1[0], rs_ffn1[0] = _ffn_bwd("ffn1_0", dh, n1[0], h0[0], ffn1_norm[0:1], gu1[0], a1[0], w_in1[0], w_out1[0], [], token)
    grad_x = dh.reshape(bl, seq, dn)
    dtable = _rel_bias_bwd(_window_bias_bwd(dbias))[:, :NREL]

    small = _pack_small(
        jnp.stack([dg_ffn1[0][0], dg_ffn1[1][0]]), jnp.stack([dg_mix[0][0], dg_mix[1][0]]), jnp.stack([dg_ffn2[0][0], dg_ffn2[1][0]]),
        dg_kv[0], dg_final[0], dg_q[0], dg_latent[0], dtable, loss_part[0],
    )
    (r_small,) = _exchange("gather_small_grads", [("gather", small)])

    def update(name, parts, w, m, v):
        n_layers = len(parts)
        rows = int(np.prod(w.shape[:-1]))
        cols = w.shape[-1]
        parts = [p.reshape(NDEV, rows // n_layers, cols) for p in parts]
        outs = _adamw(name, parts, w.reshape(rows, cols), m.reshape(rows, cols), v.reshape(rows, cols))
        return [o.reshape(w.shape) for o in outs]

    res = {}
    r_in2_1, r_out2_1 = _exchange_wait("ffn2_1_reduce_wait", rs_ffn2[1], dh)
    r_in1_1, r_out1_1, r_o_b, r_uq, r_dq = _exchange_wait("ffn1_1_reduce_wait", rs_ffn1[1], dh)
    r_in2_0, r_out2_0, r_up, r_down = _exchange_wait("ffn2_0_reduce_wait", rs_ffn2[0], dh)
    res["ffn2_w_in"] = update("adamw_ffn2_w_in", [r_in2_0, r_in2_1], ffn2_w_in, m_ffn2_w_in, v_ffn2_w_in)
    res["ffn2_w_out"] = update("adamw_ffn2_w_out", [r_out2_0, r_out2_1], ffn2_w_out, m_ffn2_w_out, v_ffn2_w_out)
    res["kv_w_down"] = update("adamw_kv_w_down", [r_down], kv_w_down, m_kv_w_down, v_kv_w_down)
    res["kv_w_up"] = update("adamw_kv_w_up", [r_up], kv_w_up, m_kv_w_up, v_kv_w_up)
    res["b_w_dq"] = update("adamw_b_w_dq", [r_dq], b_w_dq, m_b_w_dq, v_b_w_dq)
    res["b_w_uq"] = update("adamw_b_w_uq", [r_uq], b_w_uq, m_b_w_uq, v_b_w_uq)
    res["b_w_o"] = update("adamw_b_w_o", [r_o_b], b_w_o, m_b_w_o, v_b_w_o)
    r_o_a, r_qkv = _exchange_wait("attn_a_reduce_wait", rs_attn_a, res["ffn2_w_in"][0])
    res["a_w_qkv"] = update("adamw_a_w_qkv", [r_qkv], a_w_qkv, m_a_w_qkv, v_a_w_qkv)
    res["a_w_o"] = update("adamw_a_w_o", [r_o_a], a_w_o, m_a_w_o, v_a_w_o)
    r_in1_0, r_out1_0 = _exchange_wait("ffn1_0_reduce_wait", rs_ffn1[0], res["a_w_qkv"][0])
    res["ffn1_w_in"] = update("adamw_ffn1_w_in", [r_in1_0, r_in1_1], ffn1_w_in, m_ffn1_w_in, v_ffn1_w_in)
    res["ffn1_w_out"] = update("adamw_ffn1_w_out", [r_out1_0, r_out1_1], ffn1_w_out, m_ffn1_w_out, v_ffn1_w_out)
    zero_row = jnp.zeros((dn,), F32)
    packs = [
        _pack_small(f1, mx, f2, kvn, fin, qn, lat, rel, zero_row)
        for f1, mx, f2, kvn, fin, qn, lat, rel in (
            (ffn1_norm, mix_norm, ffn2_norm, kv_norm, final_norm, b_q_norm, kv_latent_norm, a_rel_bias),
            (m_ffn1_norm, m_mix_norm, m_ffn2_norm, m_kv_norm, m_final_norm, m_b_q_norm, m_kv_latent_norm, m_a_rel_bias),
            (v_ffn1_norm, v_mix_norm, v_ffn2_norm, v_kv_norm, v_final_norm, v_b_q_norm, v_kv_latent_norm, v_a_rel_bias),
        )
    ]
    small_out = [_unpack_small(o) for o in _adamw("adamw_small", [r_small], *packs)]
    for name in ("ffn1_norm", "mix_norm", "ffn2_norm", "a_rel_bias", "kv_norm", "kv_latent_norm", "b_q_norm", "final_norm"):
        res[name] = [so[name] for so in small_out]
    loss = small_out[0]["last"][0]

    order = [
        "ffn1_norm", "ffn1_w_in", "ffn1_w_out", "mix_norm", "ffn2_norm", "ffn2_w_in", "ffn2_w_out", "a_w_qkv", "a_rel_bias",
        "a_w_o", "kv_norm", "kv_w_down", "kv_latent_norm", "kv_w_up", "b_w_dq", "b_q_norm", "b_w_uq", "b_w_o", "final_norm",
    ]
    return (loss, grad_x, *[res[n][0] for n in order], *[res[n][1] for n in order], *[res[n][2] for n in order], *[res[n][3] for n in order])
```

```python
import functools

import jax
import jax.numpy as jnp
import numpy as np
from jax import lax
from jax.experimental import pallas as pl
from jax.experimental.pallas import tpu as pltpu

NDEV = 8
D_MODEL = 1024
D_FF = 2816
FB = 2 * D_FF // NDEV
NJ = D_FF // FB
CHUNK = 64
LEFT_CHUNKS = 8
PAD = LEFT_CHUNKS * CHUNK
BAND = PAD + CHUNK
CHUNKS_PER_STEP = 4
WINDOW = PAD + CHUNKS_PER_STEP * CHUNK
STEP_ROWS = CHUNKS_PER_STEP * 2 * CHUNK
MAX_REL = 128
NREL = 2 * MAX_REL + 1
NREL_PAD = 384
HEADS_A = 16
HEADS_B = 8
NOPE = 128
ROPE = 64
QK_B = NOPE + ROPE
V_DIM = 128
Q_LORA = 768
KV_LORA = 256
ROPE_THETA = 10000.0
EPS = 1e-6
NEG_INF = -1e30
MLA_TQ = 256
ADAM_LR = 0.001
ADAM_B1 = 0.9
ADAM_B2 = 0.999
ADAM_EPS = 1e-08
ADAM_WD = 0.01
ADAM_STEP = 10
PACK_ROWS = 16
VMEM_LIMIT_BYTES = 56 * 1024 * 1024

F32 = jnp.float32
BF16 = jnp.bfloat16
SDS = jax.ShapeDtypeStruct
BS = pl.BlockSpec
HIGHEST = lax.Precision.HIGHEST
MESH = pl.DeviceIdType.MESH


def _cparams(n_axes):
    return pltpu.CompilerParams(dimension_semantics=("arbitrary",) * n_axes, vmem_limit_bytes=VMEM_LIMIT_BYTES)


def _tile(t, want=512):
    return want if t % want == 0 else t


def _dot(a, b):
    return jnp.dot(a, b, preferred_element_type=F32)


def _dot_nt(a, b):
    return lax.dot_general(a, b, (((1,), (1,)), ((), ())), preferred_element_type=F32)


def _dot_tn(a, b):
    return lax.dot_general(a, b, (((0,), (0,)), ((), ())), preferred_element_type=F32)


def _dot_exact(a, b):
    return jnp.dot(a, b, precision=HIGHEST, preferred_element_type=F32)


def _rms_scale(h):
    return lax.rsqrt(jnp.mean(h * h, axis=-1, keepdims=True) + EPS)


def _acc_rows(ref, val, step, n_steps):
    part = val.reshape(val.shape[0] // 8, 8, val.shape[1]).sum(axis=0)

    @pl.when(step == 0)
    def _():
        ref[...] = part

    @pl.when(step > 0)
    def _():
        ref[...] += part

    @pl.when(step == n_steps - 1)
    def _():
        ref[...] = jnp.broadcast_to(jnp.sum(ref[...], axis=0, keepdims=True), ref.shape)


def _exchange_plan(entries):
    ins = [e[1] for e in entries]
    kinds = [e[0] for e in entries]
    lands = [SDS((NDEV,) + a.shape if k == "gather" else a.shape, a.dtype) for k, a in zip(kinds, ins)]
    return ins, lands, kinds


def _mesh_place():
    x, y, c = lax.axis_index("x"), lax.axis_index("y"), lax.axis_index("c")
    return (x, y, c), 4 * x + 2 * y + c


def _flipped(place, p):
    x, y, c = place
    px = 1 - x if p & 4 else x
    py = 1 - y if p & 2 else y
    pc = 1 - c if p & 1 else c
    return (px, py, pc), 4 * px + 2 * py + pc


def _ends(kind, src_ref, land_ref, origin, target):
    if kind == "gather":
        return src_ref, land_ref.at[origin]
    return src_ref.at[target], land_ref.at[origin]


def _remote(kind, src_ref, land_ref, send_sems, recv_sems, k, p, place, me, arriving):
    peer_pos, peer = _flipped(place, p)
    src, dst = _ends(kind, src_ref, land_ref, me, peer)
    if arriving:
        dst = _ends(kind, src_ref, land_ref, peer, me)[1]
    sem = k * (NDEV - 1) + p - 1
    return pltpu.make_async_remote_copy(
        src_ref=src, dst_ref=dst, send_sem=send_sems.at[sem], recv_sem=recv_sems.at[sem], device_id=peer_pos, device_id_type=MESH,
    )


def _exchange(name, entries):
    ins, lands, kinds = _exchange_plan(entries)
    n = len(ins)

    def body(*refs):
        in_refs, land_refs = refs[:n], refs[n : 2 * n]
        send_sems, recv_sems, local_sems = refs[2 * n :]
        place, me = _mesh_place()
        local = []
        for k in range(n):
            src, dst = _ends(kinds[k], in_refs[k], land_refs[k], me, me)
            local.append(pltpu.make_async_copy(src, dst, local_sems.at[k]))
            local[-1].start()
        sends = []
        for p in range(1, NDEV):
            for k in range(n):
                sends.append(_remote(kinds[k], in_refs[k], land_refs[k], send_sems, recv_sems, k, p, place, me, False))
                sends[-1].start()
        for p in range(1, NDEV):
            for k in range(n):
                _remote(kinds[k], in_refs[k], land_refs[k], send_sems, recv_sems, k, p, place, me, True).wait_recv()
        for cp in sends:
            cp.wait_send()
        for cp in local:
            cp.wait()

    any_spec = BS(memory_space=pl.ANY)
    return pl.pallas_call(
        body, name=name, out_shape=lands, in_specs=[any_spec] * n, out_specs=[any_spec] * n,
        scratch_shapes=[
            pltpu.SemaphoreType.DMA((n * (NDEV - 1),)), pltpu.SemaphoreType.DMA((n * (NDEV - 1),)), pltpu.SemaphoreType.DMA((n,)),
        ],
    )(*ins)


HBM_SPEC = BS(memory_space=pltpu.HBM)
SEM_SPEC = BS(memory_space=pltpu.SEMAPHORE)
DATAFLOW = pltpu.SideEffectType.DATAFLOW_SIDE_EFFECTING


def _exchange_start(name, groups):
    plans = [_exchange_plan(g) for g in groups]
    ins = [a for plan in plans for a in plan[0]]
    lands = [s for plan in plans for s in plan[1]]
    kinds = [kind for plan in plans for kind in plan[2]]
    n_in, n_groups = len(ins), len(groups)

    def body(*refs):
        in_refs, land_refs = refs[:n_in], refs[n_in : 2 * n_in]
        sems = refs[2 * n_in : 2 * n_in + 2 * n_groups]
        token = refs[4 * n_in + 2 * n_groups]
        local_sems = refs[4 * n_in + 2 * n_groups + 1]
        place, me = _mesh_place()
        local = []
        for k, kind in enumerate(kinds):
            src, dst = _ends(kind, in_refs[k], land_refs[k], me, me)
            local.append(pltpu.make_async_copy(src, dst, local_sems.at[k]))
            local[-1].start()
        base = 0
        for g, plan in enumerate(plans):
            for p in range(1, NDEV):
                for k, kind in enumerate(plan[2]):
                    _remote(kind, in_refs[base + k], land_refs[base + k], sems[2 * g], sems[2 * g + 1], k, p, place, me, False).start()
            base += len(plan[2])
        for cp in local:
            cp.wait()
        token[...] = jnp.zeros_like(token)

    sem_shapes = []
    for plan in plans:
        sem_shapes += [pltpu.SemaphoreType.DMA((len(plan[2]) * (NDEV - 1),))] * 2
    outs = pl.pallas_call(
        body, name=name,
        out_shape=sem_shapes + [pltpu.HBM(a.shape, a.dtype) for a in ins] + [pltpu.HBM(s.shape, s.dtype) for s in lands] + [SDS((8, 128), F32)],
        in_specs=[HBM_SPEC] * (2 * n_in),
        out_specs=[SEM_SPEC] * (2 * n_groups) + [HBM_SPEC] * (2 * n_in) + [BS(memory_space=pltpu.VMEM)],
        input_output_aliases={i: 2 * n_groups + i for i in range(2 * n_in)},
        scratch_shapes=[pltpu.SemaphoreType.DMA((n_in,))],
        compiler_params=pltpu.CompilerParams(has_side_effects=DATAFLOW),
    )(
        *[pltpu.with_memory_space_constraint(a, pltpu.HBM) for a in ins],
        *[pltpu.with_memory_space_constraint(lax.empty(s.shape, s.dtype), pltpu.HBM) for s in lands],
    )
    sems, srcs, landed, token = outs[: 2 * n_groups], outs[2 * n_groups : 2 * n_groups + n_in], outs[2 * n_groups + n_in : -1], outs[-1]
    started, base = [], 0
    for g, plan in enumerate(plans):
        n = len(plan[2])
        started.append((sems[2 * g], sems[2 * g + 1], srcs[base : base + n], landed[base : base + n], plan[2]))
        base += n
    return started, token


def _exchange_wait(name, started, after):
    send_sems, recv_sems, srcs, landed, kinds = started
    n = len(kinds)

    def body(*refs):
        in_refs, land_refs = refs[:n], refs[n : 2 * n]
        send_ref, recv_ref = refs[2 * n], refs[2 * n + 1]
        place, me = _mesh_place()
        for p in range(1, NDEV):
            for k in range(n):
                _remote(kinds[k], in_refs[k], land_refs[k], send_ref, recv_ref, k, p, place, me, True).wait_recv()
        for p in range(1, NDEV):
            for k in range(n):
                _remote(kinds[k], in_refs[k], land_refs[k], send_ref, recv_ref, k, p, place, me, False).wait_send()

    outs = pl.pallas_call(
        body, name=name,
        out_shape=[pltpu.HBM(a.shape, a.dtype) for a in srcs] + [pltpu.HBM(a.shape, a.dtype) for a in landed],
        in_specs=[HBM_SPEC] * (2 * n) + [SEM_SPEC, SEM_SPEC, BS(memory_space=pl.ANY)],
        out_specs=[HBM_SPEC] * (2 * n),
        input_output_aliases={i: i for i in range(2 * n)},
        compiler_params=pltpu.CompilerParams(has_side_effects=DATAFLOW),
    )(*srcs, *landed, send_sems, recv_sems, after)
    return outs[n:]


def _dep_spec(n_axes):
    return BS((8, 128), (lambda i: (0, 0)) if n_axes == 1 else (lambda i, j: (0, 0)))


def _norm_fwd(name, h, gammas, dep):
    t, dn = h.shape
    ng = gammas.shape[0]
    tm = _tile(t)

    def body(h_ref, g_ref, dep_ref, *outs):
        hv = h_ref[...]
        hh = hv * _rms_scale(hv)
        for i, o_ref in enumerate(outs):
            o_ref[...] = (hh * g_ref[i : i + 1, :]).astype(BF16)

    row = BS((tm, dn), lambda i: (i, 0))
    return pl.pallas_call(
        body, name=name, grid=(t // tm,),
        in_specs=[row, BS((ng, dn), lambda i: (0, 0)), _dep_spec(1)],
        out_specs=[row] * ng, out_shape=[SDS((t, dn), BF16)] * ng,
        compiler_params=_cparams(1),
    )(h, gammas, dep)


def _ffn_in(name, n, w_in, layer):
    t, dn = n.shape
    tm = _tile(t)

    def body(n_ref, wg_ref, wu_ref, gu_ref, a_ref):
        xv = n_ref[...]
        g = _dot(xv, wg_ref[...])
        u = _dot(xv, wu_ref[...])
        gu_ref[0] = g.astype(BF16)
        gu_ref[1] = u.astype(BF16)
        a_ref[...] = (g * jax.nn.sigmoid(g) * u).astype(BF16)

    return pl.pallas_call(
        body, name=name, grid=(NJ, t // tm),
        in_specs=[
            BS((tm, dn), lambda j, i: (i, 0)),
            BS((None, None, dn, FB), lambda j, i: (layer, j, 0, 0)),
            BS((None, None, dn, FB), lambda j, i: (layer, j + NJ, 0, 0)),
        ],
        out_specs=[BS((None, 2, tm, FB), lambda j, i: (j, 0, i, 0)), BS((None, tm, FB), lambda j, i: (j, i, 0))],
        out_shape=[SDS((NJ, 2, t, FB), BF16), SDS((NJ, t, FB), BF16)],
        compiler_params=_cparams(2),
    )(n, w_in, w_in)


def _mm_res_norm(name, a, w, layer, h_in, gammas, scale):
    nk, t, kb = a.shape
    dn = w.shape[-1]
    ng = 0 if gammas is None else gammas.shape[0]
    tm = _tile(t)

    def body(*refs):
        a_ref, w_ref, h_ref = refs[:3]
        g_ref = refs[3] if ng else None
        outs = refs[3 + (1 if ng else 0) :]
        acc = _dot(a_ref[0], w_ref[0])
        for k in range(1, nk):
            acc += _dot(a_ref[k], w_ref[k])
        ho = h_ref[...] + scale * acc
        outs[0][...] = ho
        if ng:
            hh = ho * _rms_scale(ho)
            for i in range(ng):
                outs[1 + i][...] = (hh * g_ref[i : i + 1, :]).astype(BF16)

    row = BS((tm, dn), lambda i: (i, 0))
    in_specs = [BS((nk, tm, kb), lambda i: (0, i, 0)), BS((None, nk, kb, dn), lambda i: (layer, 0, 0, 0)), row]
    args = [a, w, h_in]
    if ng:
        in_specs.append(BS((ng, dn), lambda i: (0, 0)))
        args.append(gammas)
    return pl.pallas_call(
        body, name=name, grid=(t // tm,),
        in_specs=in_specs,
        out_specs=[row] * (1 + ng), out_shape=[SDS((t, dn), F32)] + [SDS((t, dn), BF16)] * ng,
        compiler_params=_cparams(1),
    )(*args)


def _qkv_proj(name, hn, w_qkv):
    t, dn = hn.shape
    wb = w_qkv.shape[-1]
    per = wb // 128
    tm = _tile(t)

    def body(x_ref, w_ref, o_ref):
        xv = x_ref[...]
        for j in range(NDEV):
            yv = _dot(xv, w_ref[j]).astype(BF16)
            for i in range(per):
                n = per * j + i
                o_ref[n // 8, :, (n % 8) * 128 : (n % 8 + 1) * 128] = yv[:, i * 128 : (i + 1) * 128]

    return pl.pallas_call(
        body, name=name, grid=(t // tm,),
        in_specs=[BS((tm, dn), lambda i: (i, 0)), BS((NDEV, dn, wb), lambda i: (0, 0, 0))],
        out_specs=BS((3, tm, dn), lambda i: (0, i, 0)), out_shape=SDS((3, t, dn), BF16),
        compiler_params=_cparams(1),
    )(hn, w_qkv)


def _rel_onehot(i):
    r = lax.broadcasted_iota(jnp.int32, (NREL_PAD, BAND), 0)
    j = lax.broadcasted_iota(jnp.int32, (NREL_PAD, BAND), 1)
    idx = jnp.clip(PAD + i - j, -MAX_REL, MAX_REL) + MAX_REL
    return (idx == r).astype(F32)


def _rel_bias_fwd(table):
    def body(t_ref, o_ref):
        i8 = pl.program_id(0)
        for ii in range(8):
            o_ref[:, ii, :] = _dot_exact(t_ref[...], _rel_onehot(i8 * 8 + ii))

    return pl.pallas_call(
        body, name="rel_bias_fwd", grid=(CHUNK // 8,),
        in_specs=[BS((HEADS_A, NREL_PAD), lambda i: (0, 0))],
        out_specs=BS((HEADS_A, 8, BAND), lambda i: (0, i, 0)), out_shape=SDS((HEADS_A, CHUNK, BAND), F32),
        compiler_params=_cparams(1),
    )(table)


def _rel_bias_bwd(dbias):
    def body(d_ref, o_ref):
        i8 = pl.program_id(0)
        acc = jnp.zeros((HEADS_A, NREL_PAD), F32)
        for ii in range(8):
            acc += lax.dot_general(
                d_ref[:, ii, :], _rel_onehot(i8 * 8 + ii), (((1,), (1,)), ((), ())), precision=HIGHEST,
                preferred_element_type=F32,
            )

        @pl.when(i8 == 0)
        def _():
            o_ref[...] = acc

        @pl.when(i8 > 0)
        def _():
            o_ref[...] += acc

    return pl.pallas_call(
        body, name="rel_bias_bwd", grid=(CHUNK // 8,),
        in_specs=[BS((HEADS_A, 8, BAND), lambda i: (0, i, 0))],
        out_specs=BS((HEADS_A, NREL_PAD), lambda i: (0, 0)), out_shape=SDS((HEADS_A, NREL_PAD), F32),
        compiler_params=_cparams(1),
    )(dbias)


def _window_bias(bias):
    b = bias.reshape(HEADS_A // 2, 2, CHUNK, BAND)
    per_chunk = [
        jnp.pad(b, ((0, 0), (0, 0), (0, 0), (cc * CHUNK, WINDOW - BAND - cc * CHUNK)), constant_values=NEG_INF)
        for cc in range(CHUNKS_PER_STEP)
    ]
    return jnp.stack(per_chunk, axis=1).reshape(HEADS_A // 2, STEP_ROWS, WINDOW)


def _window_bias_bwd(dwin):
    d = dwin.reshape(HEADS_A // 2, CHUNKS_PER_STEP, 2, CHUNK, WINDOW)
    return sum(d[:, cc, :, :, cc * CHUNK : cc * CHUNK + BAND] for cc in range(CHUNKS_PER_STEP)).reshape(HEADS_A, CHUNK, BAND)


def _step_rows(xs, lane):
    parts = []
    for cc in range(CHUNKS_PER_STEP):
        xc = xs[cc * CHUNK : (cc + 1) * CHUNK]
        parts.append(jnp.where(lane < 64, xc, jnp.zeros_like(xc)))
        parts.append(jnp.where(lane >= 64, xc, jnp.zeros_like(xc)))
    return jnp.concatenate(parts, axis=0)


def _pair_rows(ys, lane):
    parts = []
    for cc in range(CHUNKS_PER_STEP):
        y0 = ys[(2 * cc) * CHUNK : (2 * cc + 1) * CHUNK]
        y1 = ys[(2 * cc + 1) * CHUNK : (2 * cc + 2) * CHUNK]
        parts.append(jnp.where(lane < 64, y0, y1))
    return jnp.concatenate(parts, axis=0)


def _window_probs(q_rows, kwin, bias_win, first_key):
    s = _dot_nt(q_rows, kwin) * (CHUNK ** -0.5) + bias_win
    col = lax.broadcasted_iota(jnp.int32, s.shape, 1)
    s = jnp.where(col >= first_key, s, NEG_INF)
    e = jnp.exp(s - jnp.max(s, axis=-1, keepdims=True))
    return e / jnp.sum(e, axis=-1, keepdims=True)


def _attn_a_fwd(qkv3, bias_win, bl, seq):
    t, dn = qkv3.shape[1:]
    npair = dn // 128
    step = CHUNKS_PER_STEP * CHUNK

    def body(q_ref, k_ref, v_ref, b_ref, o_ref, kpad, vpad):
        kpad[0:PAD, :] = jnp.zeros((PAD, 128), BF16)
        vpad[0:PAD, :] = jnp.zeros((PAD, 128), BF16)
        kpad[PAD:, :] = k_ref[...]
        vpad[PAD:, :] = v_ref[...]
        lane = lax.broadcasted_iota(jnp.int32, (CHUNK, 128), 1)

        def chunks(it, carry):
            r0 = pl.multiple_of(it * step, step)
            q_rows = _step_rows(q_ref[pl.ds(r0, step), :], lane)
            p = _window_probs(q_rows, kpad[pl.ds(r0, WINDOW), :], b_ref[...], PAD - r0)
            o_rows = _dot(p.astype(BF16), vpad[pl.ds(r0, WINDOW), :])
            o_ref[pl.ds(r0, step), :] = _pair_rows(o_rows, lane).astype(BF16)
            return carry

        lax.fori_loop(0, seq // step, chunks, 0)

    return pl.pallas_call(
        body, name="attn_a_fwd", grid=(bl, npair),
        in_specs=[
            BS((None, seq, 128), lambda b, h: (0, b, h)),
            BS((None, seq, 128), lambda b, h: (1, b, h)),
            BS((None, seq, 128), lambda b, h: (2, b, h)),
            BS((None, STEP_ROWS, WINDOW), lambda b, h: (h, 0, 0)),
        ],
        out_specs=BS((seq, 128), lambda b, h: (b, h)), out_shape=SDS((t, dn), BF16),
        scratch_shapes=[pltpu.VMEM((PAD + seq, 128), BF16), pltpu.VMEM((PAD + seq, 128), BF16)],
        compiler_params=_cparams(2),
    )(qkv3, qkv3, qkv3, bias_win)


def _attn_a_bwd(qkv3, do, bias_win, bl, seq):
    t, dn = qkv3.shape[1:]
    npair = dn // 128
    step = CHUNKS_PER_STEP * CHUNK

    def body(q_ref, k_ref, v_ref, do_ref, b_ref, dqkv_ref, db_ref, kpad, vpad, dkacc, dvacc):
        b = pl.program_id(1)
        kpad[0:PAD, :] = jnp.zeros((PAD, 128), BF16)
        vpad[0:PAD, :] = jnp.zeros((PAD, 128), BF16)
        kpad[PAD:, :] = k_ref[...]
        vpad[PAD:, :] = v_ref[...]
        dkacc[...] = jnp.zeros_like(dkacc)
        dvacc[...] = jnp.zeros_like(dvacc)

        @pl.when(b == 0)
        def _():
            db_ref[...] = jnp.zeros_like(db_ref)

        lane = lax.broadcasted_iota(jnp.int32, (CHUNK, 128), 1)

        def chunks(it, carry):
            r0 = pl.multiple_of(it * step, step)
            q_rows = _step_rows(q_ref[pl.ds(r0, step), :], lane)
            do_rows = _step_rows(do_ref[pl.ds(r0, step), :], lane)
            kwin = kpad[pl.ds(r0, WINDOW), :]
            vwin = vpad[pl.ds(r0, WINDOW), :]
            p = _window_probs(q_rows, kwin, b_ref[...], PAD - r0)
            dp = _dot_nt(do_rows, vwin)
            ds = p * (dp - jnp.sum(p * dp, axis=-1, keepdims=True))
            db_ref[...] += ds
            dsb = (ds * (CHUNK ** -0.5)).astype(BF16)
            dqkv_ref[0, pl.ds(r0, step), :] = _pair_rows(_dot(dsb, kwin), lane).astype(BF16)
            dkacc[pl.ds(r0, WINDOW), :] += _dot_tn(dsb, q_rows)
            dvacc[pl.ds(r0, WINDOW), :] += _dot_tn(p.astype(BF16), do_rows)
            return carry

        lax.fori_loop(0, seq // step, chunks, 0)
        dqkv_ref[1] = dkacc[PAD:, :].astype(BF16)
        dqkv_ref[2] = dvacc[PAD:, :].astype(BF16)

    return pl.pallas_call(
        body, name="attn_a_bwd", grid=(npair, bl),
        in_specs=[
            BS((None, seq, 128), lambda h, b: (0, b, h)),
            BS((None, seq, 128), lambda h, b: (1, b, h)),
            BS((None, seq, 128), lambda h, b: (2, b, h)),
            BS((seq, 128), lambda h, b: (b, h)),
            BS((None, STEP_ROWS, WINDOW), lambda h, b: (h, 0, 0)),
        ],
        out_specs=[BS((3, seq, 128), lambda h, b: (0, b, h)), BS((None, STEP_ROWS, WINDOW), lambda h, b: (h, 0, 0))],
        out_shape=[SDS((3, t, dn), BF16), SDS((HEADS_A // 2, STEP_ROWS, WINDOW), F32)],
        scratch_shapes=[
            pltpu.VMEM((PAD + seq, 128), BF16), pltpu.VMEM((PAD + seq, 128), BF16),
            pltpu.VMEM((PAD + seq, 128), F32), pltpu.VMEM((PAD + seq, 128), F32),
        ],
        compiler_params=_cparams(2),
    )(qkv3, qkv3, qkv3, do, bias_win)


def _rope_tables(seq):
    half = ROPE // 2
    freqs = ROPE_THETA ** (-jnp.arange(half, dtype=F32) / half)
    ang = jnp.arange(seq, dtype=F32)[:, None] * freqs[None, :]
    cos, sin = jnp.cos(ang), jnp.sin(ang)
    c64 = jnp.concatenate([cos, cos], axis=1)
    s64 = jnp.concatenate([-sin, sin], axis=1)
    c192 = jnp.concatenate([jnp.ones((seq, NOPE), F32), c64], axis=1)
    s192 = jnp.concatenate([jnp.zeros((seq, NOPE), F32), s64], axis=1)
    p64 = np.zeros((ROPE, ROPE), np.float32)
    for col in range(ROPE):
        p64[(col + half) % ROPE, col] = 1.0
    p192 = np.zeros((QK_B, QK_B), np.float32)
    p192[NOPE:, NOPE:] = p64
    return c64, s64, jnp.asarray(p64), c192, s192, jnp.asarray(p192)


def _rope(xv, cos, sin_signed, swap):
    return xv * cos + _dot_exact(xv, swap) * sin_signed


def _rope_bwd(dy, cos, sin_signed, swap):
    return dy * cos + _dot_exact(dy * sin_signed, swap)


def _q_down(hn, w_dq, q_norm):
    t, dn = hn.shape
    ql = w_dq.shape[1]
    tm = _tile(t)

    def body(x_ref, w_ref, g_ref, pre_ref, cq_ref):
        pre = _dot(x_ref[...], w_ref[...])
        pre_ref[...] = pre
        cq_ref[...] = (pre * _rms_scale(pre) * g_ref[...]).astype(BF16)

    return pl.pallas_call(
        body, name="q_down", grid=(t // tm,),
        in_specs=[BS((tm, dn), lambda i: (i, 0)), BS((dn, ql), lambda i: (0, 0)), BS((1, ql), lambda i: (0, 0))],
        out_specs=[BS((tm, ql), lambda i: (i, 0))] * 2, out_shape=[SDS((t, ql), F32), SDS((t, ql), BF16)],
        compiler_params=_cparams(1),
    )(hn, w_dq, q_norm)


def _q_up(cq, w_uq, c192, s192, p192, seq):
    t, ql = cq.shape
    tm = _tile(min(seq, 512), min(seq, 512))
    nseq = seq // tm

    def body(x_ref, w_ref, c_ref, s_ref, p_ref, o_ref):
        qf = _dot(x_ref[...], w_ref[...])
        o_ref[...] = _rope(qf, c_ref[...], s_ref[...], p_ref[...]).astype(BF16)

    pos = BS((tm, QK_B), lambda h, i: (i % nseq, 0))
    return pl.pallas_call(
        body, name="q_up", grid=(HEADS_B, t // tm),
        in_specs=[
            BS((tm, ql), lambda h, i: (i, 0)), BS((None, ql, QK_B), lambda h, i: (h, 0, 0)), pos, pos,
            BS((QK_B, QK_B), lambda h, i: (0, 0)),
        ],
        out_specs=BS((None, tm, QK_B), lambda h, i: (h, i, 0)), out_shape=SDS((HEADS_B, t, QK_B), BF16),
        compiler_params=_cparams(2),
    )(cq, w_uq, c192, s192, p192)


def _kv_down(hk, w_down, latent_norm, c64, s64, p64, seq):
    t, dn = hk.shape
    wd = w_down.shape[1]
    tm = _tile(min(seq, 512), min(seq, 512))
    nseq = seq // tm

    def body(x_ref, w_ref, g_ref, c_ref, s_ref, p_ref, ckr_ref, ckv_ref, kr_ref):
        ckr = _dot(x_ref[...], w_ref[...])
        ckr_ref[...] = ckr
        lat = ckr[:, :KV_LORA]
        ckv_ref[...] = (lat * _rms_scale(lat) * g_ref[...]).astype(BF16)
        kr_ref[...] = _rope(ckr[:, KV_LORA:], c_ref[...], s_ref[...], p_ref[...]).astype(BF16)

    pos = BS((tm, ROPE), lambda i: (i % nseq, 0))
    return pl.pallas_call(
        body, name="kv_down", grid=(t // tm,),
        in_specs=[
            BS((tm, dn), lambda i: (i, 0)), BS((dn, wd), lambda i: (0, 0)), BS((1, KV_LORA), lambda i: (0, 0)), pos, pos,
            BS((ROPE, ROPE), lambda i: (0, 0)),
        ],
        out_specs=[BS((tm, wd), lambda i: (i, 0)), BS((tm, KV_LORA), lambda i: (i, 0)), BS((tm, ROPE), lambda i: (i, 0))],
        out_shape=[SDS((t, wd), F32), SDS((t, KV_LORA), BF16), SDS((t, ROPE), BF16)],
        compiler_params=_cparams(1),
    )(hk, w_down, latent_norm, c64, s64, p64)


def _kv_up(ckv, w_up):
    t, kl = ckv.shape
    hb = w_up.shape[-1]
    tm = _tile(t)

    def body(x_ref, w_ref, o_ref):
        o_ref[...] = _dot(x_ref[...], w_ref[...]).astype(BF16)

    return pl.pallas_call(
        body, name="kv_up", grid=(HEADS_B, t // tm),
        in_specs=[BS((tm, kl), lambda h, i: (i, 0)), BS((None, kl, hb), lambda h, i: (h, 0, 0))],
        out_specs=BS((tm, hb), lambda h, i: (i, h)), out_shape=SDS((t, HEADS_B * hb), BF16),
        compiler_params=_cparams(2),
    )(ckv, w_up)


def _mla_probs(qi, kcat, row0, n_keys):
    s = _dot_nt(qi, kcat) * (QK_B ** -0.5)
    rows = lax.broadcasted_iota(jnp.int32, (qi.shape[0], n_keys), 0) + row0
    cols = lax.broadcasted_iota(jnp.int32, (qi.shape[0], n_keys), 1)
    s = jnp.where(jnp.right_shift(cols, 6) <= jnp.right_shift(rows, 6), s, NEG_INF)
    e = jnp.exp(s - jnp.max(s, axis=-1, keepdims=True))
    return e / jnp.sum(e, axis=-1, keepdims=True)


def _mla_fwd(q, kv, kr, bl, seq):
    t = kv.shape[0]
    tq = min(MLA_TQ, seq)

    def body(q_ref, kn_ref, v_ref, kr_ref, o_ref):
        kcat = jnp.concatenate([kn_ref[...], kr_ref[...]], axis=1)
        vv = v_ref[...]
        for i in range(seq // tq):
            n_keys = (i + 1) * tq
            p = _mla_probs(q_ref[i * tq : (i + 1) * tq, :], kcat[:n_keys], i * tq, n_keys)
            o_ref[i * tq : (i + 1) * tq, :] = _dot(p.astype(BF16), vv[:n_keys]).astype(BF16)

    return pl.pallas_call(
        body, name="mla_fwd", grid=(bl, HEADS_B),
        in_specs=[
            BS((None, seq, QK_B), lambda b, h: (h, b, 0)),
            BS((seq, NOPE), lambda b, h: (b, 2 * h)),
            BS((seq, V_DIM), lambda b, h: (b, 2 * h + 1)),
            BS((seq, ROPE), lambda b, h: (b, 0)),
        ],
        out_specs=BS((seq, V_DIM), lambda b, h: (b, h)), out_shape=SDS((t, HEADS_B * V_DIM), BF16),
        compiler_params=_cparams(2),
    )(q, kv, kv, kr)


def _mla_bwd(q, kv, kr, do, c192, s192, p192, bl, seq):
    t = kv.shape[0]
    tq = min(MLA_TQ, seq)

    def body(q_ref, kn_ref, v_ref, kr_ref, do_ref, c_ref, s_ref, p_ref, dq_ref, dkv_ref, dkr_ref, dkacc, dvacc):
        h = pl.program_id(1)
        kcat = jnp.concatenate([kn_ref[...], kr_ref[...]], axis=1)
        vv = v_ref[...]
        dkacc[...] = jnp.zeros_like(dkacc)
        dvacc[...] = jnp.zeros_like(dvacc)
        for i in range(seq // tq):
            n_keys = (i + 1) * tq
            rows = slice(i * tq, (i + 1) * tq)
            qi = q_ref[rows, :]
            doi = do_ref[rows, :]
            p = _mla_probs(qi, kcat[:n_keys], i * tq, n_keys)
            dp = _dot_nt(doi, vv[:n_keys])
            ds = p * (dp - jnp.sum(p * dp, axis=-1, keepdims=True))
            dsb = (ds * (QK_B ** -0.5)).astype(BF16)
            dq = _dot(dsb, kcat[:n_keys])
            dq_ref[rows, :] = _rope_bwd(dq, c_ref[rows, :], s_ref[rows, :], p_ref[...]).astype(BF16)
            dkacc[0:n_keys, :] += _dot_tn(dsb, qi)
            dvacc[0:n_keys, :] += _dot_tn(p.astype(BF16), doi)
        dk = dkacc[...]
        dkv_ref[:, :NOPE] = dk[:, :NOPE].astype(BF16)
        dkv_ref[:, NOPE:] = dvacc[...].astype(BF16)

        @pl.when(h == 0)
        def _():
            dkr_ref[...] = dk[:, NOPE:]

        @pl.when(h > 0)
        def _():
            dkr_ref[...] += dk[:, NOPE:]

    return pl.pallas_call(
        body, name="mla_bwd", grid=(bl, HEADS_B),
        in_specs=[
            BS((None, seq, QK_B), lambda b, h: (h, b, 0)),
            BS((seq, NOPE), lambda b, h: (b, 2 * h)),
            BS((seq, V_DIM), lambda b, h: (b, 2 * h + 1)),
            BS((seq, ROPE), lambda b, h: (b, 0)),
            BS((seq, V_DIM), lambda b, h: (b, h)),
            BS((seq, QK_B), lambda b, h: (0, 0)),
            BS((seq, QK_B), lambda b, h: (0, 0)),
            BS((QK_B, QK_B), lambda b, h: (0, 0)),
        ],
        out_specs=[
            BS((None, seq, QK_B), lambda b, h: (h, b, 0)),
            BS((seq, NOPE + V_DIM), lambda b, h: (b, h)),
            BS((seq, ROPE), lambda b, h: (b, 0)),
        ],
        out_shape=[SDS((HEADS_B, t, QK_B), BF16), SDS((t, HEADS_B * (NOPE + V_DIM)), BF16), SDS((t, ROPE), F32)],
        scratch_shapes=[pltpu.VMEM((seq, QK_B), F32), pltpu.VMEM((seq, V_DIM), F32)],
        compiler_params=_cparams(2),
    )(q, kv, kv, kr, do, c192, s192, p192)


def _loss_final(h, target, gamma):
    t, dn = h.shape
    tm = _tile(t)
    nt = t // tm

    def body(h_ref, t_ref, g_ref, dh_ref, dhb_ref, dg_ref, loss_ref):
        i = pl.program_id(0)
        hv = h_ref[...]
        r = _rms_scale(hv)
        hh = hv * r
        gam = g_ref[...]
        err = hh * gam - t_ref[...]
        part = 0.5 * jnp.sum(jnp.mean(err * err, axis=-1, keepdims=True))

        @pl.when(i == 0)
        def _():
            loss_ref[...] = jnp.zeros_like(loss_ref)

        loss_ref[...] += part
        dy = err * (1.0 / dn)
        _acc_rows(dg_ref, dy * hh, i, nt)
        t1 = dy * gam
        dh = r * (t1 - hh * jnp.mean(t1 * hh, axis=-1, keepdims=True))
        dh_ref[...] = dh
        dhb_ref[...] = dh.astype(BF16)

    row = BS((tm, dn), lambda i: (i, 0))
    return pl.pallas_call(
        body, name="loss_final", grid=(nt,),
        in_specs=[row, row, BS((1, dn), lambda i: (0, 0))],
        out_specs=[row, row, BS((8, dn), lambda i: (0, 0)), BS((8, 128), lambda i: (0, 0))],
        out_shape=[SDS((t, dn), F32), SDS((t, dn), BF16), SDS((8, dn), F32), SDS((8, 128), F32)],
        compiler_params=_cparams(1),
    )(h, target, gamma)


def _ffn_bwd_in(name, dh, w_out, layer, gu, dep=None):
    t, dn = dh.shape
    tm = _tile(t)

    def body(dh_ref, w_ref, gu_ref, *rest):
        o_ref = rest[-1]
        da = 0.5 * _dot_nt(dh_ref[...], w_ref[...])
        g = gu_ref[0].astype(F32)
        u = gu_ref[1].astype(F32)
        sg = jax.nn.sigmoid(g)
        o_ref[0] = (da * u * (sg * (1.0 + g * (1.0 - sg)))).astype(BF16)
        o_ref[1] = (da * (g * sg)).astype(BF16)

    blk = BS((None, 2, tm, FB), lambda j, i: (j, 0, i, 0))
    deps = [] if dep is None else [dep]
    return pl.pallas_call(
        body, name=name, grid=(NJ, t // tm),
        in_specs=[BS((tm, dn), lambda j, i: (i, 0)), BS((None, None, FB, dn), lambda j, i: (layer, j, 0, 0)), blk]
        + [_dep_spec(2)] * len(deps),
        out_specs=blk, out_shape=SDS((NJ, 2, t, FB), BF16),
        compiler_params=_cparams(2),
    )(dh, w_out, gu, *deps)


def _mm_nt_plain(name, xf, w, dep=None):
    t, dn = xf.shape
    n = w.shape[0]
    tm = _tile(t)

    def body(x_ref, w_ref, *rest):
        rest[-1][...] = _dot_nt(x_ref[...], w_ref[...]).astype(BF16)

    deps = [] if dep is None else [dep]
    return pl.pallas_call(
        body, name=name, grid=(t // tm,),
        in_specs=[BS((tm, dn), lambda i: (i, 0)), BS((n, dn), lambda i: (0, 0))] + [_dep_spec(1)] * len(deps),
        out_specs=BS((tm, n), lambda i: (i, 0)), out_shape=SDS((t, n), BF16),
        compiler_params=_cparams(1),
    )(xf, w, *deps)


def _mm_tn(name, xa, x_spec, ya, y_spec, out_shape, out_spec, nj, scale=None):
    def body(x_ref, y_ref, o_ref):
        acc = _dot_tn(x_ref[...], y_ref[...])
        o_ref[...] = (acc if scale is None else scale * acc).astype(BF16)

    return pl.pallas_call(
        body, name=name, grid=(nj,),
        in_specs=[x_spec, y_spec], out_specs=out_spec, out_shape=SDS(out_shape, BF16),
        compiler_params=_cparams(1),
    )(xa, ya)


def _dw_qkv(hn, dqkv3, wb):
    t, dn = hn.shape
    per = wb // 128
    tm = _tile(t)

    def body(x_ref, y_ref, o_ref, acc):
        i = pl.program_id(0)
        xv = x_ref[...]
        for j in range(NDEV):
            cols = [y_ref[(per * j + k) // 8, :, ((per * j + k) % 8) * 128 : ((per * j + k) % 8 + 1) * 128] for k in range(per)]
            part = _dot_tn(xv, jnp.concatenate(cols, axis=1))

            @pl.when(i == 0)
            def _():
                acc[j] = part

            @pl.when(i > 0)
            def _():
                acc[j] += part

        @pl.when(i == t // tm - 1)
        def _():
            o_ref[...] = acc[...].astype(BF16)

    return pl.pallas_call(
        body, name="dw_qkv", grid=(t // tm,),
        in_specs=[BS((tm, dn), lambda i: (i, 0)), BS((3, tm, dn), lambda i: (0, i, 0))],
        out_specs=BS((NDEV, dn, wb), lambda i: (0, 0, 0)), out_shape=SDS((NDEV, dn, wb), BF16),
        scratch_shapes=[pltpu.VMEM((NDEV, dn, wb), F32)],
        compiler_params=_cparams(1),
    )(hn, dqkv3)


def _mm_nt_epi(name, ya, y_spec, wa, w_spec, nj, n_out, extra, out_shapes, out_specs, epilogue, tm, nt, mm_fn=None):
    n_extra = len(extra)
    n_outs = len(out_shapes)

    def body(*refs):
        y_ref, w_ref = refs[:2]
        ex = refs[2 : 2 + n_extra]
        outs = refs[2 + n_extra : 2 + n_extra + n_outs]
        i = pl.program_id(0)
        j = pl.program_id(1)
        part = _dot_nt(y_ref[...], w_ref[...]) if mm_fn is None else mm_fn(y_ref, w_ref)
        if nj == 1:
            epilogue(part, ex, outs, i, nt)
            return
        acc = refs[-1]

        @pl.when(j == 0)
        def _():
            acc[...] = part

        @pl.when(j > 0)
        def _():
            acc[...] += part

        @pl.when(j == nj - 1)
        def _():
            epilogue(acc[...], ex, outs, i, nt)

    return pl.pallas_call(
        body, name=name, grid=(nt, nj),
        in_specs=[y_spec, w_spec] + [spec for _, spec in extra],
        out_specs=out_specs, out_shape=out_shapes,
        scratch_shapes=[] if nj == 1 else [pltpu.VMEM((tm, n_out), F32)],
        compiler_params=_cparams(2),
    )(ya, wa, *[arr for arr, _ in extra])


def _norm_bwd(dn, hv, gam):
    r = _rms_scale(hv)
    hh = hv * r
    t1 = dn * gam
    return r * (t1 - hh * jnp.mean(t1 * hh, axis=-1, keepdims=True)), dn * hh


def _norm_bwd_epilogue(has_res, out_dtype):
    def epilogue(dn, ex, outs, i, nt):
        dh, dg_rows = _norm_bwd(dn, ex[0][...], ex[1][...])
        _acc_rows(outs[1], dg_rows, i, nt)
        if has_res:
            dh = dh + ex[2][...]
        outs[0][...] = dh.astype(out_dtype)
        if has_res:
            outs[2][...] = dh.astype(BF16)

    return epilogue


def _mm_nt_norm_bwd(name, ya, y_spec, wa, w_spec, nj, h, gamma, res, out_dtype, mm_fn=None, want_tm=512, dep=None):
    t, n = h.shape
    tm = _tile(t, want_tm)
    nt = t // tm
    row = BS((tm, n), lambda i, j: (i, 0))
    extra = [(h, row), (gamma, BS((1, n), lambda i, j: (0, 0)))]
    out_shapes = [SDS((t, n), out_dtype), SDS((8, n), F32)]
    out_specs = [row, BS((8, n), lambda i, j: (0, 0))]
    if res is not None:
        extra.append((res, row))
        out_shapes.append(SDS((t, n), BF16))
        out_specs.append(row)
    if dep is not None:
        extra.append((dep, _dep_spec(2)))
    return _mm_nt_epi(
        name, ya, y_spec, wa, w_spec, nj, n, extra, out_shapes, out_specs, _norm_bwd_epilogue(res is not None, out_dtype), tm, nt, mm_fn,
    )


def _dev_block(jj):
    return jj // 2 + NJ * (jj % 2)


def _ffn_dn_mm(y_ref, w_ref):
    acc = None
    for jj in range(2 * NJ):
        part = _dot_nt(y_ref[jj], w_ref[_dev_block(jj)])
        acc = part if acc is None else acc + part
    return acc


def _ffn_bwd(tag, dh, dhb, n_in, h_in, gamma, gu, a, w_in, w_out, more_grads, dep):
    t, dn = dh.shape
    dgu = _ffn_bwd_in(f"{tag}_bwd_in", dhb, w_out, 0, gu, dep).reshape(2 * NJ, t, FB)
    dw_out = _mm_tn(
        f"{tag}_dw_out", a, BS((None, t, FB), lambda j: (j, 0, 0)), dhb, BS((t, dn), lambda j: (0, 0)),
        (NJ, FB, dn), BS((None, FB, dn), lambda j: (j, 0, 0)), NJ, scale=0.5,
    )
    dw_in = _mm_tn(
        f"{tag}_dw_in", n_in, BS((t, dn), lambda j: (0, 0)), dgu, BS((None, t, FB), lambda j: (j, 0, 0)),
        (NDEV, dn, FB), BS((None, dn, FB), lambda j: (_dev_block(j), 0, 0)), NDEV,
    )
    entries = [("scatter", dw_in), ("scatter", dw_out.reshape(NDEV, NJ * FB // NDEV, dn))] + [("scatter", g) for g in more_grads]
    (started,), token = _exchange_start(f"{tag}_reduce_start", [entries])
    tm = _tile(t, 256)
    dh_in, dgam, dhb_in = _mm_nt_norm_bwd(
        f"{tag}_dn", dgu, BS((2 * NJ, tm, FB), lambda i, j: (0, i, 0)),
        w_in, BS((None, NDEV, dn, FB), lambda i, j: (0, 0, 0, 0)), 1, h_in, gamma, dh, F32, mm_fn=_ffn_dn_mm, want_tm=256, dep=token,
    )
    return dh_in, dhb_in, dgam, started


def _dqkv_mm(per):
    def mm(y_ref, w_ref):
        acc = None
        for j in range(NDEV):
            cols = [y_ref[(per * j + k) // 8, :, ((per * j + k) % 8) * 128 : ((per * j + k) % 8 + 1) * 128] for k in range(per)]
            part = _dot_nt(jnp.concatenate(cols, axis=1), w_ref[j])
            acc = part if acc is None else acc + part
        return acc

    return mm


def _kv_latent_bwd(dkv, w_up, ckr, latent_norm, dkr, c64, s64, p64, seq):
    t, wd = ckr.shape
    hb = w_up.shape[-1]
    tm = _tile(min(seq, 512), min(seq, 512))
    nt = t // tm
    nseq = seq // tm

    def epilogue(dn, ex, outs, i, nt_):
        dlat, dg_rows = _norm_bwd(dn, ex[0][...], ex[1][...])
        _acc_rows(outs[1], dg_rows, i, nt_)
        outs[0][:, :KV_LORA] = dlat.astype(BF16)
        outs[0][:, KV_LORA:] = _rope_bwd(ex[2][...], ex[3][...], ex[4][...], ex[5][...]).astype(BF16)

    pos = BS((tm, ROPE), lambda i, j: (i % nseq, 0))
    extra = [
        (ckr, BS((tm, KV_LORA), lambda i, j: (i, 0))), (latent_norm, BS((1, KV_LORA), lambda i, j: (0, 0))),
        (dkr, BS((tm, ROPE), lambda i, j: (i, 0))), (c64, pos), (s64, pos), (p64, BS((ROPE, ROPE), lambda i, j: (0, 0))),
    ]
    return _mm_nt_epi(
        "kv_latent_bwd", dkv, BS((tm, hb), lambda i, j: (i, j)), w_up, BS((None, KV_LORA, hb), lambda i, j: (j, 0, 0)),
        HEADS_B, KV_LORA, extra, [SDS((t, wd), BF16), SDS((8, KV_LORA), F32)],
        [BS((tm, wd), lambda i, j: (i, 0)), BS((8, KV_LORA), lambda i, j: (0, 0))], epilogue, tm, nt,
    )


def _adamw(name, parts, w, m, v):
    n_layers = len(parts)
    rows, cols = w.shape[0] // n_layers, w.shape[1]
    tr = max(d for d in range(8, min(rows, 256) + 1, 8) if rows % d == 0)
    nb = rows // tr

    def body(*refs):
        p_refs = refs[:n_layers]
        w_ref, m_ref, v_ref, g_ref, d_ref, nm_ref, nv_ref = refs[n_layers : n_layers + 7]
        layer = pl.program_id(0)
        for lp in range(n_layers):

            @pl.when(layer == lp)
            def _():
                g = p_refs[lp][0].astype(F32)
                for k in range(1, NDEV):
                    g = g + p_refs[lp][k].astype(F32)
                g_ref[...] = g

        g = g_ref[...]
        nm = ADAM_B1 * m_ref[...] + (1.0 - ADAM_B1) * g
        nv = ADAM_B2 * v_ref[...] + (1.0 - ADAM_B2) * (g * g)
        nm_ref[...] = nm
        nv_ref[...] = nv
        m_hat = nm / (1.0 - ADAM_B1 ** ADAM_STEP)
        v_hat = nv / (1.0 - ADAM_B2 ** ADAM_STEP)
        d_ref[...] = -ADAM_LR * (m_hat / (jnp.sqrt(v_hat) + ADAM_EPS) + ADAM_WD * w_ref[...])

    def part_spec(lp):
        return BS((NDEV, tr, cols), lambda l, i: (0, jnp.where(l == lp, i, jnp.where(l < lp, 0, nb - 1)), 0))

    row = BS((tr, cols), lambda l, i: (l * nb + i, 0))
    return pl.pallas_call(
        body, name=name, grid=(n_layers, nb),
        in_specs=[part_spec(lp) for lp in range(n_layers)] + [row, row, row],
        out_specs=[row] * 4, out_shape=[SDS(w.shape, F32)] * 4,
        compiler_params=_cparams(2),
    )(*parts, w, m, v)


def _pack_small(ffn1_norm, mix_norm, ffn2_norm, kv_norm, final_norm, q_norm, latent_norm, rel_bias, last_row):
    dn = ffn1_norm.shape[-1]

    def rows_of(a, n_rows):
        flat = a.reshape(-1)
        return jnp.pad(flat, (0, n_rows * dn - flat.shape[0])).reshape(n_rows, dn)

    return jnp.concatenate(
        [
            ffn1_norm.reshape(2, dn), mix_norm.reshape(2, dn), ffn2_norm.reshape(2, dn), kv_norm.reshape(1, dn),
            final_norm.reshape(1, dn), rows_of(q_norm, 1), rows_of(latent_norm, 1), rows_of(rel_bias, 5), rows_of(last_row, 1),
        ],
        axis=0,
    )


def _unpack_small(pack):
    dn = pack.shape[-1]
    return dict(
        ffn1_norm=pack[0:2], mix_norm=pack[2:4], ffn2_norm=pack[4:6], kv_norm=pack[6], final_norm=pack[7],
        b_q_norm=pack[8, :Q_LORA].reshape(1, Q_LORA), kv_latent_norm=pack[9, :KV_LORA],
        a_rel_bias=pack[10:15].reshape(-1)[: HEADS_A * NREL].reshape(1, HEADS_A, NREL), last=pack[15],
    )


def kernel(x, ffn1_norm, ffn1_w_in, ffn1_w_out, mix_norm, ffn2_norm, ffn2_w_in, ffn2_w_out, a_w_qkv, a_rel_bias, a_w_o, kv_norm, kv_w_down, kv_latent_norm, kv_w_up, b_w_dq, b_q_norm, b_w_uq, b_w_o, final_norm, loss_target, m_ffn1_norm, m_ffn1_w_in, m_ffn1_w_out, m_mix_norm, m_ffn2_norm, m_ffn2_w_in, m_ffn2_w_out, m_a_w_qkv, m_a_rel_bias, m_a_w_o, m_kv_norm, m_kv_w_down, m_kv_latent_norm, m_kv_w_up, m_b_w_dq, m_b_q_norm, m_b_w_uq, m_b_w_o, m_final_norm, v_ffn1_norm, v_ffn1_w_in, v_ffn1_w_out, v_mix_norm, v_ffn2_norm, v_ffn2_w_in, v_ffn2_w_out, v_a_w_qkv, v_a_rel_bias, v_a_w_o, v_kv_norm, v_kv_w_down, v_kv_latent_norm, v_kv_w_up, v_b_w_dq, v_b_q_norm, v_b_w_uq, v_b_w_o, v_final_norm):
    bl, seq, dn = x.shape
    t = bl * seq
    tm = _tile(t)
    nt = t // tm
    x2 = x.reshape(t, dn)
    target2 = loss_target.reshape(t, dn)

    def gathered(*ws):
        return [("gather", w.astype(BF16)) for w in ws]

    ag, token = _exchange_start(
        "gather_start",
        [
            gathered(ffn1_w_in[0]), gathered(ffn1_w_out[0]), gathered(a_w_qkv[0], a_w_o[0]), gathered(ffn2_w_in[0], ffn2_w_out[0]),
            gathered(kv_w_down, kv_w_up), gathered(ffn1_w_in[1], ffn1_w_out[1]), gathered(b_w_dq[0], b_w_uq[0], b_w_o[0]),
            gathered(ffn2_w_in[1], ffn2_w_out[1]),
        ],
    )

    def as_w_in(w):
        return w.reshape(1, NDEV, dn, FB)

    def as_w_out(w):
        return w.reshape(1, NJ, FB, dn)

    c64, s64, p64, c192, s192, p192 = _rope_tables(seq)
    q_norm = b_q_norm.reshape(1, Q_LORA)
    latent_norm = kv_latent_norm.reshape(1, KV_LORA)
    bias = _window_bias(_rel_bias_fwd(jnp.pad(a_rel_bias[0], ((0, 0), (0, NREL_PAD - NREL)))))

    h0, h1, h2, n1, hn, n2, gu1, gu2, a1, a2, w_in1, w_in2, w_out1, w_out2 = ([None, None] for _ in range(14))
    h0[0] = x2
    (n1[0],) = _norm_fwd("norm_x", x2, ffn1_norm[0:1], token)
    w_in1[0] = as_w_in(_exchange_wait("gather_wait_0", ag[0], n1[0])[0])
    gu1[0], a1[0] = _ffn_in("ffn1_in_0", n1[0], w_in1[0], 0)
    w_out1[0] = as_w_out(_exchange_wait("gather_wait_1", ag[1], a1[0])[0])
    h1[0], hn[0] = _mm_res_norm("ffn1_out_0", a1[0], w_out1[0], 0, h0[0], mix_norm[0:1], 0.5)
    w_qkv, w_o_a = _exchange_wait("gather_wait_2", ag[2], hn[0])
    qkv_wb = w_qkv.shape[-1]
    w_o_a = w_o_a.reshape(1, 1, dn, dn)
    qkv3 = _qkv_proj("qkv_proj", hn[0], w_qkv)
    o_a = _attn_a_fwd(qkv3, bias, bl, seq)
    h2[0], n2[0] = _mm_res_norm("attn_a_out", o_a.reshape(1, t, dn), w_o_a, 0, h1[0], ffn2_norm[0:1], 1.0)
    w_in2[0], w_out2[0] = _exchange_wait("gather_wait_3", ag[3], n2[0])
    w_in2[0], w_out2[0] = as_w_in(w_in2[0]), as_w_out(w_out2[0])
    gu2[0], a2[0] = _ffn_in("ffn2_in_0", n2[0], w_in2[0], 0)
    h0[1], hk, n1[1] = _mm_res_norm(
        "ffn2_out_0", a2[0], w_out2[0], 0, h2[0], jnp.concatenate([kv_norm.reshape(1, dn), ffn1_norm[1:2]], axis=0), 0.5
    )
    w_down, w_up = _exchange_wait("gather_wait_4", ag[4], hk)
    w_down = w_down.reshape(dn, KV_LORA + ROPE)
    ckr, ckv, kr = _kv_down(hk, w_down, latent_norm, c64, s64, p64, seq)
    kv = _kv_up(ckv, w_up)
    w_in1[1], w_out1[1] = _exchange_wait("gather_wait_5", ag[5], kv)
    w_in1[1], w_out1[1] = as_w_in(w_in1[1]), as_w_out(w_out1[1])
    gu1[1], a1[1] = _ffn_in("ffn1_in_1", n1[1], w_in1[1], 0)
    h1[1], hn[1] = _mm_res_norm("ffn1_out_1", a1[1], w_out1[1], 0, h0[1], mix_norm[1:2], 0.5)
    w_dq, w_uq, w_o_b = _exchange_wait("gather_wait_6", ag[6], hn[1])
    w_dq = w_dq.reshape(dn, Q_LORA)
    w_o_b = w_o_b.reshape(1, 1, dn, dn)
    cq_pre, cq = _q_down(hn[1], w_dq, q_norm)
    q = _q_up(cq, w_uq, c192, s192, p192, seq)
    o_b = _mla_fwd(q, kv, kr, bl, seq)
    h2[1], n2[1] = _mm_res_norm("attn_b_out", o_b.reshape(1, t, dn), w_o_b, 0, h1[1], ffn2_norm[1:2], 1.0)
    w_in2[1], w_out2[1] = _exchange_wait("gather_wait_7", ag[7], n2[1])
    w_in2[1], w_out2[1] = as_w_in(w_in2[1]), as_w_out(w_out2[1])
    gu2[1], a2[1] = _ffn_in("ffn2_in_1", n2[1], w_in2[1], 0)
    (h_last,) = _mm_res_norm("ffn2_out_1", a2[1], w_out2[1], 0, h2[1], None, 0.5)
    dh, dhb, dg_final, loss_part = _loss_final(h_last, target2, final_norm.reshape(1, dn))

    dg_ffn1, dg_mix, dg_ffn2, rs_ffn1, rs_ffn2 = ([None, None] for _ in range(5))

    def whole(rows, cols):
        return BS((rows, cols), lambda j: (0, 0))

    def dw_rows(name, xa, ya):
        n = ya.shape[1]
        return _mm_tn(name, xa, whole(t, dn), ya, whole(t, n), (dn, n), whole(dn, n), 1).reshape(NDEV, dn // NDEV, n)

    dh, dhb, dg_ffn2[1], rs_ffn2[1] = _ffn_bwd(
        "ffn2_1", dh, dhb, n2[1], h2[1], ffn2_norm[1:2], gu2[1], a2[1], w_in2[1], w_out2[1], [], None
    )
    do_b = _mm_nt_plain("attn_b_do", dhb, w_o_b.reshape(dn, dn))
    dw_o_b = dw_rows("attn_b_dwo", o_b, dhb)
    dq_pre, dkv, dkr = _mla_bwd(q, kv, kr, do_b, c192, s192, p192, bl, seq)
    dw_uq = _mm_tn(
        "dw_uq", cq, whole(t, Q_LORA), dq_pre, BS((None, t, QK_B), lambda j: (j, 0, 0)),
        (HEADS_B, Q_LORA, QK_B), BS((None, Q_LORA, QK_B), lambda j: (j, 0, 0)), HEADS_B,
    )
    dcq_pre, dg_q = _mm_nt_norm_bwd(
        "dcq", dq_pre, BS((None, tm, QK_B), lambda i, j: (j, i, 0)), w_uq, BS((None, Q_LORA, QK_B), lambda i, j: (j, 0, 0)),
        HEADS_B, cq_pre, q_norm, None, BF16,
    )
    dw_dq = dw_rows("dw_dq", hn[1], dcq_pre)
    dh, dg_mix[1], dhb = _mm_nt_norm_bwd(
        "dhn_b", dcq_pre, BS((tm, Q_LORA), lambda i, j: (i, 0)), w_dq, BS((dn, Q_LORA), lambda i, j: (0, 0)),
        1, h1[1], mix_norm[1:2], dh, F32,
    )
    dh, dhb, dg_ffn1[1], rs_ffn1[1] = _ffn_bwd(
        "ffn1_1", dh, dhb, n1[1], h0[1], ffn1_norm[1:2], gu1[1], a1[1], w_in1[1], w_out1[1], [dw_o_b, dw_uq, dw_dq], None
    )
    dw_up = _mm_tn(
        "dw_up", ckv, whole(t, KV_LORA), dkv, BS((t, NOPE + V_DIM), lambda j: (0, j)),
        (HEADS_B, KV_LORA, NOPE + V_DIM), BS((None, KV_LORA, NOPE + V_DIM), lambda j: (j, 0, 0)), HEADS_B,
    )
    dckr, dg_latent = _kv_latent_bwd(dkv, w_up, ckr, latent_norm, dkr, c64, s64, p64, seq)
    dw_down = dw_rows("dw_down", hk, dckr)
    dh, dg_kv, dhb = _mm_nt_norm_bwd(
        "dhk", dckr, BS((tm, KV_LORA + ROPE), lambda i, j: (i, 0)), w_down, BS((dn, KV_LORA + ROPE), lambda i, j: (0, 0)),
        1, h0[1], kv_norm.reshape(1, dn), dh, F32,
    )
    dh, dhb, dg_ffn2[0], rs_ffn2[0] = _ffn_bwd(
        "ffn2_0", dh, dhb, n2[0], h2[0], ffn2_norm[0:1], gu2[0], a2[0], w_in2[0], w_out2[0], [dw_up, dw_down], None
    )
    do_a = _mm_nt_plain("attn_a_do", dhb, w_o_a.reshape(dn, dn))
    dw_o_a = dw_rows("attn_a_dwo", o_a, dhb)
    dqkv3, dbias = _attn_a_bwd(qkv3, do_a, bias, bl, seq)
    dw_qkv = _dw_qkv(hn[0], dqkv3, qkv_wb)
    dh, dg_mix[0], dhb = _mm_nt_norm_bwd(
        "dhn_a", dqkv3, BS((3, tm, dn), lambda i, j: (0, i, 0)), w_qkv, BS((NDEV, dn, qkv_wb), lambda i, j: (0, 0, 0)),
        1, h1[0], mix_norm[0:1], dh, F32, mm_fn=_dqkv_mm(qkv_wb // 128),
    )
    (rs_attn_a,), token = _exchange_start("attn_a_reduce_start", [[("scatter", dw_o_a), ("scatter", dw_qkv)]])
    dh, dhb, dg_ffn1[0], rs_ffn1[0] = _ffn_bwd(
        "ffn1_0", dh, dhb, n1[0], h0[0], ffn1_norm[0:1], gu1[0], a1[0], w_in1[0], w_out1[0], [], token
    )
    grad_x = dh.reshape(bl, seq, dn)
    dtable = _rel_bias_bwd(_window_bias_bwd(dbias))[:, :NREL]

    small = _pack_small(
        jnp.stack([dg_ffn1[0][0], dg_ffn1[1][0]]), jnp.stack([dg_mix[0][0], dg_mix[1][0]]), jnp.stack([dg_ffn2[0][0], dg_ffn2[1][0]]),
        dg_kv[0], dg_final[0], dg_q[0], dg_latent[0], dtable, loss_part[0],
    )
    (r_small,) = _exchange("gather_small_grads", [("gather", small)])

    def update(name, parts, w, m, v):
        n_layers = len(parts)
        rows = int(np.prod(w.shape[:-1]))
        cols = w.shape[-1]
        parts = [p.reshape(NDEV, rows // n_layers, cols) for p in parts]
        outs = _adamw(name, parts, w.reshape(rows, cols), m.reshape(rows, cols), v.reshape(rows, cols))
        return [o.reshape(w.shape) for o in outs]

    res = {}
    r_in2_1, r_out2_1 = _exchange_wait("ffn2_1_reduce_wait", rs_ffn2[1], dh)
    r_in1_1, r_out1_1, r_o_b, r_uq, r_dq = _exchange_wait("ffn1_1_reduce_wait", rs_ffn1[1], dh)
    r_in2_0, r_out2_0, r_up, r_down = _exchange_wait("ffn2_0_reduce_wait", rs_ffn2[0], dh)
    res["ffn2_w_in"] = update("adamw_ffn2_w_in", [r_in2_0, r_in2_1], ffn2_w_in, m_ffn2_w_in, v_ffn2_w_in)
    res["ffn2_w_out"] = update("adamw_ffn2_w_out", [r_out2_0, r_out2_1], ffn2_w_out, m_ffn2_w_out, v_ffn2_w_out)
    res["kv_w_down"] = update("adamw_kv_w_down", [r_down], kv_w_down, m_kv_w_down, v_kv_w_down)
    res["kv_w_up"] = update("adamw_kv_w_up", [r_up], kv_w_up, m_kv_w_up, v_kv_w_up)
    res["b_w_dq"] = update("adamw_b_w_dq", [r_dq], b_w_dq, m_b_w_dq, v_b_w_dq)
    res["b_w_uq"] = update("adamw_b_w_uq", [r_uq], b_w_uq, m_b_w_uq, v_b_w_uq)
    res["b_w_o"] = update("adamw_b_w_o", [r_o_b], b_w_o, m_b_w_o, v_b_w_o)
    r_o_a, r_qkv = _exchange_wait("attn_a_reduce_wait", rs_attn_a, res["ffn2_w_in"][0])
    res["a_w_qkv"] = update("adamw_a_w_qkv", [r_qkv], a_w_qkv, m_a_w_qkv, v_a_w_qkv)
    res["a_w_o"] = update("adamw_a_w_o", [r_o_a], a_w_o, m_a_w_o, v_a_w_o)
    r_in1_0, r_out1_0 = _exchange_wait("ffn1_0_reduce_wait", rs_ffn1[0], res["a_w_qkv"][0])
    res["ffn1_w_in"] = update("adamw_ffn1_w_in", [r_in1_0, r_in1_1], ffn1_w_in, m_ffn1_w_in, v_ffn1_w_in)
    res["ffn1_w_out"] = update("adamw_ffn1_w_out", [r_out1_0, r_out1_1], ffn1_w_out, m_ffn1_w_out, v_ffn1_w_out)
    zero_row = jnp.zeros((dn,), F32)
    packs = [
        _pack_small(f1, mx, f2, kvn, fin, qn, lat, rel, zero_row)
        for f1, mx, f2, kvn, fin, qn, lat, rel in (
            (ffn1_norm, mix_norm, ffn2_norm, kv_norm, final_norm, b_q_norm, kv_latent_norm, a_rel_bias),
            (m_ffn1_norm, m_mix_norm, m_ffn2_norm, m_kv_norm, m_final_norm, m_b_q_norm, m_kv_latent_norm, m_a_rel_bias),
            (v_ffn1_norm, v_mix_norm, v_ffn2_norm, v_kv_norm, v_final_norm, v_b_q_norm, v_kv_latent_norm, v_a_rel_bias),
        )
    ]
    small_out = [_unpack_small(o) for o in _adamw("adamw_small", [r_small], *packs)]
    for name in ("ffn1_norm", "mix_norm", "ffn2_norm", "a_rel_bias", "kv_norm", "kv_latent_norm", "b_q_norm", "final_norm"):
        res[name] = [so[name] for so in small_out]
    loss = small_out[0]["last"][0]

    order = [
        "ffn1_norm", "ffn1_w_in", "ffn1_w_out", "mix_norm", "ffn2_norm", "ffn2_w_in", "ffn2_w_out", "a_w_qkv", "a_rel_bias",
        "a_w_o", "kv_norm", "kv_w_down", "kv_latent_norm", "kv_w_up", "b_w_dq", "b_q_norm", "b_w_uq", "b_w_o", "final_norm",
    ]
    return (loss, grad_x, *[res[n][0] for n in order], *[res[n][1] for n in order], *[res[n][2] for n in order], *[res[n][3] for n in order])
```

```python
import functools

import jax
import jax.numpy as jnp
import numpy as np
from jax import lax
from jax.experimental import pallas as pl
from jax.experimental.pallas import tpu as pltpu
from jax.experimental.pallas import tpu_sc as plsc

NDEV = 8
D_MODEL = 1024
D_FF = 2816
FB = 2 * D_FF // NDEV
NJ = D_FF // FB
CHUNK = 64
LEFT_CHUNKS = 8
PAD = LEFT_CHUNKS * CHUNK
BAND = PAD + CHUNK
CHUNKS_PER_STEP = 4
WINDOW = PAD + CHUNKS_PER_STEP * CHUNK
STEP_ROWS = CHUNKS_PER_STEP * 2 * CHUNK
MAX_REL = 128
NREL = 2 * MAX_REL + 1
NREL_PAD = 384
HEADS_A = 16
HEADS_B = 8
NOPE = 128
ROPE = 64
QK_B = NOPE + ROPE
V_DIM = 128
Q_LORA = 768
KV_LORA = 256
ROPE_THETA = 10000.0
EPS = 1e-6
NEG_INF = -1e30
MLA_TQ = 256
ADAM_LR = 0.001
ADAM_B1 = 0.9
ADAM_B2 = 0.999
ADAM_EPS = 1e-08
ADAM_WD = 0.01
ADAM_STEP = 10
PACK_ROWS = 16
GATHER_IDS = tuple(range(1, 9))
REDUCE_IDS = tuple(range(9, 14))
VMEM_LIMIT_BYTES = 56 * 1024 * 1024

F32 = jnp.float32
BF16 = jnp.bfloat16
SDS = jax.ShapeDtypeStruct
BS = pl.BlockSpec
HIGHEST = lax.Precision.HIGHEST
MESH = pl.DeviceIdType.MESH


def _cparams(n_axes):
    return pltpu.CompilerParams(dimension_semantics=("arbitrary",) * n_axes, vmem_limit_bytes=VMEM_LIMIT_BYTES)


def _tile(t, want=512):
    return want if t % want == 0 else t


def _dot(a, b):
    return jnp.dot(a, b, preferred_element_type=F32)


def _dot_nt(a, b):
    return lax.dot_general(a, b, (((1,), (1,)), ((), ())), preferred_element_type=F32)


def _dot_tn(a, b):
    return lax.dot_general(a, b, (((0,), (0,)), ((), ())), preferred_element_type=F32)


def _dot_exact(a, b):
    return jnp.dot(a, b, precision=HIGHEST, preferred_element_type=F32)


def _rms_scale(h):
    return lax.rsqrt(jnp.mean(h * h, axis=-1, keepdims=True) + EPS)


def _acc_rows(ref, val, step, n_steps):
    part = val.reshape(val.shape[0] // 8, 8, val.shape[1]).sum(axis=0)

    @pl.when(step == 0)
    def _():
        ref[...] = part

    @pl.when(step > 0)
    def _():
        ref[...] += part

    @pl.when(step == n_steps - 1)
    def _():
        ref[...] = jnp.broadcast_to(jnp.sum(ref[...], axis=0, keepdims=True), ref.shape)


def _exchange_plan(entries):
    ins = [e[1] for e in entries]
    kinds = [e[0] for e in entries]
    lands = [SDS((NDEV,) + a.shape if k == "gather" else a.shape, a.dtype) for k, a in zip(kinds, ins)]
    return ins, lands, kinds


def _mesh_place():
    x, y, c = lax.axis_index("x"), lax.axis_index("y"), lax.axis_index("c")
    return (x, y, c), 4 * x + 2 * y + c


def _flipped(place, p):
    x, y, c = place
    px = 1 - x if p & 4 else x
    py = 1 - y if p & 2 else y
    pc = 1 - c if p & 1 else c
    return (px, py, pc), 4 * px + 2 * py + pc


def _ends(kind, src_ref, land_ref, origin, target):
    if kind == "gather":
        return src_ref, land_ref.at[origin]
    return src_ref.at[target], land_ref.at[origin]


def _remote(kind, src_ref, land_ref, send_sems, recv_sems, k, p, place, me, arriving):
    peer_pos, peer = _flipped(place, p)
    src, dst = _ends(kind, src_ref, land_ref, me, peer)
    if arriving:
        dst = _ends(kind, src_ref, land_ref, peer, me)[1]
    sem = k * (NDEV - 1) + p - 1
    return pltpu.make_async_remote_copy(
        src_ref=src, dst_ref=dst, send_sem=send_sems.at[sem], recv_sem=recv_sems.at[sem], device_id=peer_pos, device_id_type=MESH,
    )


def _exchange(name, entries):
    ins, lands, kinds = _exchange_plan(entries)
    n = len(ins)

    def body(*refs):
        in_refs, land_refs = refs[:n], refs[n : 2 * n]
        send_sems, recv_sems, local_sems = refs[2 * n :]
        place, me = _mesh_place()
        local = []
        for k in range(n):
            src, dst = _ends(kinds[k], in_refs[k], land_refs[k], me, me)
            local.append(pltpu.make_async_copy(src, dst, local_sems.at[k]))
            local[-1].start()
        sends = []
        for p in range(1, NDEV):
            for k in range(n):
                sends.append(_remote(kinds[k], in_refs[k], land_refs[k], send_sems, recv_sems, k, p, place, me, False))
                sends[-1].start()
        for p in range(1, NDEV):
            for k in range(n):
                _remote(kinds[k], in_refs[k], land_refs[k], send_sems, recv_sems, k, p, place, me, True).wait_recv()
        for cp in sends:
            cp.wait_send()
        for cp in local:
            cp.wait()

    any_spec = BS(memory_space=pl.ANY)
    return pl.pallas_call(
        body, name=name, out_shape=lands, in_specs=[any_spec] * n, out_specs=[any_spec] * n,
        scratch_shapes=[
            pltpu.SemaphoreType.DMA((n * (NDEV - 1),)), pltpu.SemaphoreType.DMA((n * (NDEV - 1),)), pltpu.SemaphoreType.DMA((n,)),
        ],
    )(*ins)


HBM_SPEC = BS(memory_space=pltpu.HBM)
SEM_SPEC = BS(memory_space=pltpu.SEMAPHORE)
DATAFLOW = pltpu.SideEffectType.DATAFLOW_SIDE_EFFECTING


def _exchange_start(name, groups):
    plans = [_exchange_plan(g) for g in groups]
    ins = [a for plan in plans for a in plan[0]]
    lands = [s for plan in plans for s in plan[1]]
    kinds = [kind for plan in plans for kind in plan[2]]
    n_in, n_groups = len(ins), len(groups)

    def body(*refs):
        in_refs, land_refs = refs[:n_in], refs[n_in : 2 * n_in]
        sems = refs[2 * n_in : 2 * n_in + 2 * n_groups]
        token = refs[4 * n_in + 2 * n_groups]
        local_sems = refs[4 * n_in + 2 * n_groups + 1]
        place, me = _mesh_place()
        local = []
        for k, kind in enumerate(kinds):
            src, dst = _ends(kind, in_refs[k], land_refs[k], me, me)
            local.append(pltpu.make_async_copy(src, dst, local_sems.at[k]))
            local[-1].start()
        base = 0
        for g, plan in enumerate(plans):
            for p in range(1, NDEV):
                for k, kind in enumerate(plan[2]):
                    _remote(kind, in_refs[base + k], land_refs[base + k], sems[2 * g], sems[2 * g + 1], k, p, place, me, False).start()
            base += len(plan[2])
        for cp in local:
            cp.wait()
        token[...] = jnp.zeros_like(token)

    sem_shapes = []
    for plan in plans:
        sem_shapes += [pltpu.SemaphoreType.DMA((len(plan[2]) * (NDEV - 1),))] * 2
    outs = pl.pallas_call(
        body, name=name,
        out_shape=sem_shapes + [pltpu.HBM(a.shape, a.dtype) for a in ins] + [pltpu.HBM(s.shape, s.dtype) for s in lands] + [SDS((8, 128), F32)],
        in_specs=[HBM_SPEC] * (2 * n_in),
        out_specs=[SEM_SPEC] * (2 * n_groups) + [HBM_SPEC] * (2 * n_in) + [BS(memory_space=pltpu.VMEM)],
        input_output_aliases={i: 2 * n_groups + i for i in range(2 * n_in)},
        scratch_shapes=[pltpu.SemaphoreType.DMA((n_in,))],
        compiler_params=pltpu.CompilerParams(has_side_effects=DATAFLOW),
    )(
        *[pltpu.with_memory_space_constraint(a, pltpu.HBM) for a in ins],
        *[pltpu.with_memory_space_constraint(lax.empty(s.shape, s.dtype), pltpu.HBM) for s in lands],
    )
    sems, srcs, landed, token = outs[: 2 * n_groups], outs[2 * n_groups : 2 * n_groups + n_in], outs[2 * n_groups + n_in : -1], outs[-1]
    started, base = [], 0
    for g, plan in enumerate(plans):
        n = len(plan[2])
        started.append((sems[2 * g], sems[2 * g + 1], srcs[base : base + n], landed[base : base + n], plan[2]))
        base += n
    return started, token


def _exchange_wait(name, started, after):
    send_sems, recv_sems, srcs, landed, kinds = started
    n = len(kinds)

    def body(*refs):
        in_refs, land_refs = refs[:n], refs[n : 2 * n]
        send_ref, recv_ref = refs[2 * n], refs[2 * n + 1]
        place, me = _mesh_place()
        for p in range(1, NDEV):
            for k in range(n):
                _remote(kinds[k], in_refs[k], land_refs[k], send_ref, recv_ref, k, p, place, me, True).wait_recv()
        for p in range(1, NDEV):
            for k in range(n):
                _remote(kinds[k], in_refs[k], land_refs[k], send_ref, recv_ref, k, p, place, me, False).wait_send()

    outs = pl.pallas_call(
        body, name=name,
        out_shape=[pltpu.HBM(a.shape, a.dtype) for a in srcs] + [pltpu.HBM(a.shape, a.dtype) for a in landed],
        in_specs=[HBM_SPEC] * (2 * n) + [SEM_SPEC, SEM_SPEC, BS(memory_space=pl.ANY)],
        out_specs=[HBM_SPEC] * (2 * n),
        input_output_aliases={i: i for i in range(2 * n)},
        compiler_params=pltpu.CompilerParams(has_side_effects=DATAFLOW),
    )(*srcs, *landed, send_sems, recv_sems, after)
    return outs[n:]


def _exchange_sc(name, entries, collective_id):
    ins, lands, kinds = _exchange_plan(entries)
    n = len(ins)
    in_refs = [jax.new_ref(a, memory_space=pltpu.MemorySpace.HBM) for a in ins]
    land_refs = [jax.empty_ref(s, memory_space=pltpu.MemorySpace.HBM) for s in lands]

    @pl.kernel(
        mesh=plsc.ScalarSubcoreMesh(axis_name="sequencer", num_cores=1), name=name,
        scratch_types=(
            pltpu.SemaphoreType.DMA((n * (NDEV - 1),)), pltpu.SemaphoreType.DMA((n * (NDEV - 1),)), pltpu.SemaphoreType.DMA((n,)),
        ),
        compiler_params=pltpu.CompilerParams(collective_id=collective_id),
    )
    def launch(send_sems, recv_sems, local_sems):
        place, me = _mesh_place()
        barrier = pltpu.get_barrier_semaphore()
        for p in range(1, NDEV):
            pl.semaphore_signal(barrier, inc=1, device_id=_flipped(place, p)[0], device_id_type=MESH)
        pl.semaphore_wait(barrier, NDEV - 1)
        local = []
        for k in range(n):
            src, dst = _ends(kinds[k], in_refs[k], land_refs[k], me, me)
            local.append(pltpu.make_async_copy(src, dst, local_sems.at[k]))
            local[-1].start()
        sends = []
        for p in range(1, NDEV):
            for k in range(n):
                sends.append(_remote(kinds[k], in_refs[k], land_refs[k], send_sems, recv_sems, k, p, place, me, False))
                sends[-1].start()
        for p in range(1, NDEV):
            for k in range(n):
                _remote(kinds[k], in_refs[k], land_refs[k], send_sems, recv_sems, k, p, place, me, True).wait_recv()
        for cp in sends:
            cp.wait_send()
        for cp in local:
            cp.wait()

    launch()
    return [r[...] for r in land_refs]


def _dep_spec(n_axes):
    return BS((8, 128), (lambda i: (0, 0)) if n_axes == 1 else (lambda i, j: (0, 0)))


def _norm_fwd(name, h, gammas):
    t, dn = h.shape
    ng = gammas.shape[0]
    tm = _tile(t)

    def body(h_ref, g_ref, *outs):
        hv = h_ref[...]
        hh = hv * _rms_scale(hv)
        for i, o_ref in enumerate(outs):
            o_ref[...] = (hh * g_ref[i : i + 1, :]).astype(BF16)

    row = BS((tm, dn), lambda i: (i, 0))
    return pl.pallas_call(
        body, name=name, grid=(t // tm,),
        in_specs=[row, BS((ng, dn), lambda i: (0, 0))],
        out_specs=[row] * ng, out_shape=[SDS((t, dn), BF16)] * ng,
        compiler_params=_cparams(1),
    )(h, gammas)


def _ffn_in(name, n, w_in, layer):
    t, dn = n.shape
    tm = _tile(t)

    def body(n_ref, wg_ref, wu_ref, gu_ref, a_ref):
        xv = n_ref[...]
        g = _dot(xv, wg_ref[...])
        u = _dot(xv, wu_ref[...])
        gu_ref[0] = g.astype(BF16)
        gu_ref[1] = u.astype(BF16)
        a_ref[...] = (g * jax.nn.sigmoid(g) * u).astype(BF16)

    return pl.pallas_call(
        body, name=name, grid=(NJ, t // tm),
        in_specs=[
            BS((tm, dn), lambda j, i: (i, 0)),
            BS((None, None, dn, FB), lambda j, i: (layer, j, 0, 0)),
            BS((None, None, dn, FB), lambda j, i: (layer, j + NJ, 0, 0)),
        ],
        out_specs=[BS((None, 2, tm, FB), lambda j, i: (j, 0, i, 0)), BS((None, tm, FB), lambda j, i: (j, i, 0))],
        out_shape=[SDS((NJ, 2, t, FB), BF16), SDS((NJ, t, FB), BF16)],
        compiler_params=_cparams(2),
    )(n, w_in, w_in)


def _mm_res_norm(name, a, w, layer, h_in, gammas, scale):
    nk, t, kb = a.shape
    dn = w.shape[-1]
    ng = 0 if gammas is None else gammas.shape[0]
    tm = _tile(t)

    def body(*refs):
        a_ref, w_ref, h_ref = refs[:3]
        g_ref = refs[3] if ng else None
        outs = refs[3 + (1 if ng else 0) :]
        acc = _dot(a_ref[0], w_ref[0])
        for k in range(1, nk):
            acc += _dot(a_ref[k], w_ref[k])
        ho = h_ref[...] + scale * acc
        outs[0][...] = ho
        if ng:
            hh = ho * _rms_scale(ho)
            for i in range(ng):
                outs[1 + i][...] = (hh * g_ref[i : i + 1, :]).astype(BF16)

    row = BS((tm, dn), lambda i: (i, 0))
    in_specs = [BS((nk, tm, kb), lambda i: (0, i, 0)), BS((None, nk, kb, dn), lambda i: (layer, 0, 0, 0)), row]
    args = [a, w, h_in]
    if ng:
        in_specs.append(BS((ng, dn), lambda i: (0, 0)))
        args.append(gammas)
    return pl.pallas_call(
        body, name=name, grid=(t // tm,),
        in_specs=in_specs,
        out_specs=[row] * (1 + ng), out_shape=[SDS((t, dn), F32)] + [SDS((t, dn), BF16)] * ng,
        compiler_params=_cparams(1),
    )(*args)


def _qkv_proj(name, hn, w_qkv):
    t, dn = hn.shape
    wb = w_qkv.shape[-1]
    per = wb // 128
    tm = _tile(t)

    def body(x_ref, w_ref, o_ref):
        xv = x_ref[...]
        for j in range(NDEV):
            yv = _dot(xv, w_ref[j]).astype(BF16)
            for i in range(per):
                n = per * j + i
                o_ref[n // 8, :, (n % 8) * 128 : (n % 8 + 1) * 128] = yv[:, i * 128 : (i + 1) * 128]

    return pl.pallas_call(
        body, name=name, grid=(t // tm,),
        in_specs=[BS((tm, dn), lambda i: (i, 0)), BS((NDEV, dn, wb), lambda i: (0, 0, 0))],
        out_specs=BS((3, tm, dn), lambda i: (0, i, 0)), out_shape=SDS((3, t, dn), BF16),
        compiler_params=_cparams(1),
    )(hn, w_qkv)


def _rel_onehot(i):
    r = lax.broadcasted_iota(jnp.int32, (NREL_PAD, BAND), 0)
    j = lax.broadcasted_iota(jnp.int32, (NREL_PAD, BAND), 1)
    idx = jnp.clip(PAD + i - j, -MAX_REL, MAX_REL) + MAX_REL
    return (idx == r).astype(F32)


def _rel_bias_fwd(table):
    def body(t_ref, o_ref):
        i8 = pl.program_id(0)
        for ii in range(8):
            o_ref[:, ii, :] = _dot_exact(t_ref[...], _rel_onehot(i8 * 8 + ii))

    return pl.pallas_call(
        body, name="rel_bias_fwd", grid=(CHUNK // 8,),
        in_specs=[BS((HEADS_A, NREL_PAD), lambda i: (0, 0))],
        out_specs=BS((HEADS_A, 8, BAND), lambda i: (0, i, 0)), out_shape=SDS((HEADS_A, CHUNK, BAND), F32),
        compiler_params=_cparams(1),
    )(table)


def _rel_bias_bwd(dbias):
    def body(d_ref, o_ref):
        i8 = pl.program_id(0)
        acc = jnp.zeros((HEADS_A, NREL_PAD), F32)
        for ii in range(8):
            acc += lax.dot_general(
                d_ref[:, ii, :], _rel_onehot(i8 * 8 + ii), (((1,), (1,)), ((), ())), precision=HIGHEST,
                preferred_element_type=F32,
            )

        @pl.when(i8 == 0)
        def _():
            o_ref[...] = acc

        @pl.when(i8 > 0)
        def _():
            o_ref[...] += acc

    return pl.pallas_call(
        body, name="rel_bias_bwd", grid=(CHUNK // 8,),
        in_specs=[BS((HEADS_A, 8, BAND), lambda i: (0, i, 0))],
        out_specs=BS((HEADS_A, NREL_PAD), lambda i: (0, 0)), out_shape=SDS((HEADS_A, NREL_PAD), F32),
        compiler_params=_cparams(1),
    )(dbias)


def _window_bias(bias):
    b = bias.reshape(HEADS_A // 2, 2, CHUNK, BAND)
    per_chunk = [
        jnp.pad(b, ((0, 0), (0, 0), (0, 0), (cc * CHUNK, WINDOW - BAND - cc * CHUNK)), constant_values=NEG_INF)
        for cc in range(CHUNKS_PER_STEP)
    ]
    return jnp.stack(per_chunk, axis=1).reshape(HEADS_A // 2, STEP_ROWS, WINDOW)


def _window_bias_bwd(dwin):
    d = dwin.reshape(HEADS_A // 2, CHUNKS_PER_STEP, 2, CHUNK, WINDOW)
    return sum(d[:, cc, :, :, cc * CHUNK : cc * CHUNK + BAND] for cc in range(CHUNKS_PER_STEP)).reshape(HEADS_A, CHUNK, BAND)


def _step_rows(xs, lane):
    parts = []
    for cc in range(CHUNKS_PER_STEP):
        xc = xs[cc * CHUNK : (cc + 1) * CHUNK]
        parts.append(jnp.where(lane < 64, xc, jnp.zeros_like(xc)))
        parts.append(jnp.where(lane >= 64, xc, jnp.zeros_like(xc)))
    return jnp.concatenate(parts, axis=0)


def _pair_rows(ys, lane):
    parts = []
    for cc in range(CHUNKS_PER_STEP):
        y0 = ys[(2 * cc) * CHUNK : (2 * cc + 1) * CHUNK]
        y1 = ys[(2 * cc + 1) * CHUNK : (2 * cc + 2) * CHUNK]
        parts.append(jnp.where(lane < 64, y0, y1))
    return jnp.concatenate(parts, axis=0)


def _window_probs(q_rows, kwin, bias_win, first_key):
    s = _dot_nt(q_rows, kwin) * (CHUNK ** -0.5) + bias_win
    col = lax.broadcasted_iota(jnp.int32, s.shape, 1)
    s = jnp.where(col >= first_key, s, NEG_INF)
    e = jnp.exp(s - jnp.max(s, axis=-1, keepdims=True))
    return e / jnp.sum(e, axis=-1, keepdims=True)


def _attn_a_fwd(qkv3, bias_win, bl, seq):
    t, dn = qkv3.shape[1:]
    npair = dn // 128
    step = CHUNKS_PER_STEP * CHUNK

    def body(q_ref, k_ref, v_ref, b_ref, o_ref, kpad, vpad):
        kpad[0:PAD, :] = jnp.zeros((PAD, 128), BF16)
        vpad[0:PAD, :] = jnp.zeros((PAD, 128), BF16)
        kpad[PAD:, :] = k_ref[...]
        vpad[PAD:, :] = v_ref[...]
        lane = lax.broadcasted_iota(jnp.int32, (CHUNK, 128), 1)

        def chunks(it, carry):
            r0 = pl.multiple_of(it * step, step)
            q_rows = _step_rows(q_ref[pl.ds(r0, step), :], lane)
            p = _window_probs(q_rows, kpad[pl.ds(r0, WINDOW), :], b_ref[...], PAD - r0)
            o_rows = _dot(p.astype(BF16), vpad[pl.ds(r0, WINDOW), :])
            o_ref[pl.ds(r0, step), :] = _pair_rows(o_rows, lane).astype(BF16)
            return carry

        lax.fori_loop(0, seq // step, chunks, 0)

    return pl.pallas_call(
        body, name="attn_a_fwd", grid=(bl, npair),
        in_specs=[
            BS((None, seq, 128), lambda b, h: (0, b, h)),
            BS((None, seq, 128), lambda b, h: (1, b, h)),
            BS((None, seq, 128), lambda b, h: (2, b, h)),
            BS((None, STEP_ROWS, WINDOW), lambda b, h: (h, 0, 0)),
        ],
        out_specs=BS((seq, 128), lambda b, h: (b, h)), out_shape=SDS((t, dn), BF16),
        scratch_shapes=[pltpu.VMEM((PAD + seq, 128), BF16), pltpu.VMEM((PAD + seq, 128), BF16)],
        compiler_params=_cparams(2),
    )(qkv3, qkv3, qkv3, bias_win)


def _attn_a_bwd(qkv3, do, bias_win, bl, seq):
    t, dn = qkv3.shape[1:]
    npair = dn // 128
    step = CHUNKS_PER_STEP * CHUNK

    def body(q_ref, k_ref, v_ref, do_ref, b_ref, dqkv_ref, db_ref, kpad, vpad, dkacc, dvacc):
        b = pl.program_id(1)
        kpad[0:PAD, :] = jnp.zeros((PAD, 128), BF16)
        vpad[0:PAD, :] = jnp.zeros((PAD, 128), BF16)
        kpad[PAD:, :] = k_ref[...]
        vpad[PAD:, :] = v_ref[...]
        dkacc[...] = jnp.zeros_like(dkacc)
        dvacc[...] = jnp.zeros_like(dvacc)

        @pl.when(b == 0)
        def _():
            db_ref[...] = jnp.zeros_like(db_ref)

        lane = lax.broadcasted_iota(jnp.int32, (CHUNK, 128), 1)

        def chunks(it, carry):
            r0 = pl.multiple_of(it * step, step)
            q_rows = _step_rows(q_ref[pl.ds(r0, step), :], lane)
            do_rows = _step_rows(do_ref[pl.ds(r0, step), :], lane)
            kwin = kpad[pl.ds(r0, WINDOW), :]
            vwin = vpad[pl.ds(r0, WINDOW), :]
            p = _window_probs(q_rows, kwin, b_ref[...], PAD - r0)
            dp = _dot_nt(do_rows, vwin)
            ds = p * (dp - jnp.sum(p * dp, axis=-1, keepdims=True))
            db_ref[...] += ds
            dsb = (ds * (CHUNK ** -0.5)).astype(BF16)
            dqkv_ref[0, pl.ds(r0, step), :] = _pair_rows(_dot(dsb, kwin), lane).astype(BF16)
            dkacc[pl.ds(r0, WINDOW), :] += _dot_tn(dsb, q_rows)
            dvacc[pl.ds(r0, WINDOW), :] += _dot_tn(p.astype(BF16), do_rows)
            return carry

        lax.fori_loop(0, seq // step, chunks, 0)
        dqkv_ref[1] = dkacc[PAD:, :].astype(BF16)
        dqkv_ref[2] = dvacc[PAD:, :].astype(BF16)

    return pl.pallas_call(
        body, name="attn_a_bwd", grid=(npair, bl),
        in_specs=[
            BS((None, seq, 128), lambda h, b: (0, b, h)),
            BS((None, seq, 128), lambda h, b: (1, b, h)),
            BS((None, seq, 128), lambda h, b: (2, b, h)),
            BS((seq, 128), lambda h, b: (b, h)),
            BS((None, STEP_ROWS, WINDOW), lambda h, b: (h, 0, 0)),
        ],
        out_specs=[BS((3, seq, 128), lambda h, b: (0, b, h)), BS((None, STEP_ROWS, WINDOW), lambda h, b: (h, 0, 0))],
        out_shape=[SDS((3, t, dn), BF16), SDS((HEADS_A // 2, STEP_ROWS, WINDOW), F32)],
        scratch_shapes=[
            pltpu.VMEM((PAD + seq, 128), BF16), pltpu.VMEM((PAD + seq, 128), BF16),
            pltpu.VMEM((PAD + seq, 128), F32), pltpu.VMEM((PAD + seq, 128), F32),
        ],
        compiler_params=_cparams(2),
    )(qkv3, qkv3, qkv3, do, bias_win)


def _rope_tables(seq):
    half = ROPE // 2
    freqs = ROPE_THETA ** (-jnp.arange(half, dtype=F32) / half)
    ang = jnp.arange(seq, dtype=F32)[:, None] * freqs[None, :]
    cos, sin = jnp.cos(ang), jnp.sin(ang)
    c64 = jnp.concatenate([cos, cos], axis=1)
    s64 = jnp.concatenate([-sin, sin], axis=1)
    c192 = jnp.concatenate([jnp.ones((seq, NOPE), F32), c64], axis=1)
    s192 = jnp.concatenate([jnp.zeros((seq, NOPE), F32), s64], axis=1)
    p64 = np.zeros((ROPE, ROPE), np.float32)
    for col in range(ROPE):
        p64[(col + half) % ROPE, col] = 1.0
    p192 = np.zeros((QK_B, QK_B), np.float32)
    p192[NOPE:, NOPE:] = p64
    return c64, s64, jnp.asarray(p64), c192, s192, jnp.asarray(p192)


def _rope(xv, cos, sin_signed, swap):
    return xv * cos + _dot_exact(xv, swap) * sin_signed


def _rope_bwd(dy, cos, sin_signed, swap):
    return dy * cos + _dot_exact(dy * sin_signed, swap)


def _q_down(hn, w_dq, q_norm):
    t, dn = hn.shape
    ql = w_dq.shape[1]
    tm = _tile(t)

    def body(x_ref, w_ref, g_ref, pre_ref, cq_ref):
        pre = _dot(x_ref[...], w_ref[...])
        pre_ref[...] = pre
        cq_ref[...] = (pre * _rms_scale(pre) * g_ref[...]).astype(BF16)

    return pl.pallas_call(
        body, name="q_down", grid=(t // tm,),
        in_specs=[BS((tm, dn), lambda i: (i, 0)), BS((dn, ql), lambda i: (0, 0)), BS((1, ql), lambda i: (0, 0))],
        out_specs=[BS((tm, ql), lambda i: (i, 0))] * 2, out_shape=[SDS((t, ql), F32), SDS((t, ql), BF16)],
        compiler_params=_cparams(1),
    )(hn, w_dq, q_norm)


def _q_up(cq, w_uq, c192, s192, p192, seq):
    t, ql = cq.shape
    tm = _tile(min(seq, 512), min(seq, 512))
    nseq = seq // tm

    def body(x_ref, w_ref, c_ref, s_ref, p_ref, o_ref):
        qf = _dot(x_ref[...], w_ref[...])
        o_ref[...] = _rope(qf, c_ref[...], s_ref[...], p_ref[...]).astype(BF16)

    pos = BS((tm, QK_B), lambda h, i: (i % nseq, 0))
    return pl.pallas_call(
        body, name="q_up", grid=(HEADS_B, t // tm),
        in_specs=[
            BS((tm, ql), lambda h, i: (i, 0)), BS((None, ql, QK_B), lambda h, i: (h, 0, 0)), pos, pos,
            BS((QK_B, QK_B), lambda h, i: (0, 0)),
        ],
        out_specs=BS((None, tm, QK_B), lambda h, i: (h, i, 0)), out_shape=SDS((HEADS_B, t, QK_B), BF16),
        compiler_params=_cparams(2),
    )(cq, w_uq, c192, s192, p192)


def _kv_down(hk, w_down, latent_norm, c64, s64, p64, seq):
    t, dn = hk.shape
    wd = w_down.shape[1]
    tm = _tile(min(seq, 512), min(seq, 512))
    nseq = seq // tm

    def body(x_ref, w_ref, g_ref, c_ref, s_ref, p_ref, ckr_ref, ckv_ref, kr_ref):
        ckr = _dot(x_ref[...], w_ref[...])
        ckr_ref[...] = ckr
        lat = ckr[:, :KV_LORA]
        ckv_ref[...] = (lat * _rms_scale(lat) * g_ref[...]).astype(BF16)
        kr_ref[...] = _rope(ckr[:, KV_LORA:], c_ref[...], s_ref[...], p_ref[...]).astype(BF16)

    pos = BS((tm, ROPE), lambda i: (i % nseq, 0))
    return pl.pallas_call(
        body, name="kv_down", grid=(t // tm,),
        in_specs=[
            BS((tm, dn), lambda i: (i, 0)), BS((dn, wd), lambda i: (0, 0)), BS((1, KV_LORA), lambda i: (0, 0)), pos, pos,
            BS((ROPE, ROPE), lambda i: (0, 0)),
        ],
        out_specs=[BS((tm, wd), lambda i: (i, 0)), BS((tm, KV_LORA), lambda i: (i, 0)), BS((tm, ROPE), lambda i: (i, 0))],
        out_shape=[SDS((t, wd), F32), SDS((t, KV_LORA), BF16), SDS((t, ROPE), BF16)],
        compiler_params=_cparams(1),
    )(hk, w_down, latent_norm, c64, s64, p64)


def _kv_up(ckv, w_up):
    t, kl = ckv.shape
    hb = w_up.shape[-1]
    tm = _tile(t)

    def body(x_ref, w_ref, o_ref):
        o_ref[...] = _dot(x_ref[...], w_ref[...]).astype(BF16)

    return pl.pallas_call(
        body, name="kv_up", grid=(HEADS_B, t // tm),
        in_specs=[BS((tm, kl), lambda h, i: (i, 0)), BS((None, kl, hb), lambda h, i: (h, 0, 0))],
        out_specs=BS((tm, hb), lambda h, i: (i, h)), out_shape=SDS((t, HEADS_B * hb), BF16),
        compiler_params=_cparams(2),
    )(ckv, w_up)


def _mla_probs(qi, kcat, row0, n_keys):
    s = _dot_nt(qi, kcat) * (QK_B ** -0.5)
    rows = lax.broadcasted_iota(jnp.int32, (qi.shape[0], n_keys), 0) + row0
    cols = lax.broadcasted_iota(jnp.int32, (qi.shape[0], n_keys), 1)
    s = jnp.where(jnp.right_shift(cols, 6) <= jnp.right_shift(rows, 6), s, NEG_INF)
    e = jnp.exp(s - jnp.max(s, axis=-1, keepdims=True))
    return e / jnp.sum(e, axis=-1, keepdims=True)


def _mla_fwd(q, kv, kr, bl, seq):
    t = kv.shape[0]
    tq = min(MLA_TQ, seq)

    def body(q_ref, kn_ref, v_ref, kr_ref, o_ref):
        kcat = jnp.concatenate([kn_ref[...], kr_ref[...]], axis=1)
        vv = v_ref[...]
        for i in range(seq // tq):
            n_keys = (i + 1) * tq
            p = _mla_probs(q_ref[i * tq : (i + 1) * tq, :], kcat[:n_keys], i * tq, n_keys)
            o_ref[i * tq : (i + 1) * tq, :] = _dot(p.astype(BF16), vv[:n_keys]).astype(BF16)

    return pl.pallas_call(
        body, name="mla_fwd", grid=(bl, HEADS_B),
        in_specs=[
            BS((None, seq, QK_B), lambda b, h: (h, b, 0)),
            BS((seq, NOPE), lambda b, h: (b, 2 * h)),
            BS((seq, V_DIM), lambda b, h: (b, 2 * h + 1)),
            BS((seq, ROPE), lambda b, h: (b, 0)),
        ],
        out_specs=BS((seq, V_DIM), lambda b, h: (b, h)), out_shape=SDS((t, HEADS_B * V_DIM), BF16),
        compiler_params=_cparams(2),
    )(q, kv, kv, kr)


def _mla_bwd(q, kv, kr, do, c192, s192, p192, bl, seq):
    t = kv.shape[0]
    tq = min(MLA_TQ, seq)

    def body(q_ref, kn_ref, v_ref, kr_ref, do_ref, c_ref, s_ref, p_ref, dq_ref, dkv_ref, dkr_ref, dkacc, dvacc):
        h = pl.program_id(1)
        kcat = jnp.concatenate([kn_ref[...], kr_ref[...]], axis=1)
        vv = v_ref[...]
        dkacc[...] = jnp.zeros_like(dkacc)
        dvacc[...] = jnp.zeros_like(dvacc)
        for i in range(seq // tq):
            n_keys = (i + 1) * tq
            rows = slice(i * tq, (i + 1) * tq)
            qi = q_ref[rows, :]
            doi = do_ref[rows, :]
            p = _mla_probs(qi, kcat[:n_keys], i * tq, n_keys)
            dp = _dot_nt(doi, vv[:n_keys])
            ds = p * (dp - jnp.sum(p * dp, axis=-1, keepdims=True))
            dsb = (ds * (QK_B ** -0.5)).astype(BF16)
            dq = _dot(dsb, kcat[:n_keys])
            dq_ref[rows, :] = _rope_bwd(dq, c_ref[rows, :], s_ref[rows, :], p_ref[...]).astype(BF16)
            dkacc[0:n_keys, :] += _dot_tn(dsb, qi)
            dvacc[0:n_keys, :] += _dot_tn(p.astype(BF16), doi)
        dk = dkacc[...]
        dkv_ref[:, :NOPE] = dk[:, :NOPE].astype(BF16)
        dkv_ref[:, NOPE:] = dvacc[...].astype(BF16)

        @pl.when(h == 0)
        def _():
            dkr_ref[...] = dk[:, NOPE:]

        @pl.when(h > 0)
        def _():
            dkr_ref[...] += dk[:, NOPE:]

    return pl.pallas_call(
        body, name="mla_bwd", grid=(bl, HEADS_B),
        in_specs=[
            BS((None, seq, QK_B), lambda b, h: (h, b, 0)),
            BS((seq, NOPE), lambda b, h: (b, 2 * h)),
            BS((seq, V_DIM), lambda b, h: (b, 2 * h + 1)),
            BS((seq, ROPE), lambda b, h: (b, 0)),
            BS((seq, V_DIM), lambda b, h: (b, h)),
            BS((seq, QK_B), lambda b, h: (0, 0)),
            BS((seq, QK_B), lambda b, h: (0, 0)),
            BS((QK_B, QK_B), lambda b, h: (0, 0)),
        ],
        out_specs=[
            BS((None, seq, QK_B), lambda b, h: (h, b, 0)),
            BS((seq, NOPE + V_DIM), lambda b, h: (b, h)),
            BS((seq, ROPE), lambda b, h: (b, 0)),
        ],
        out_shape=[SDS((HEADS_B, t, QK_B), BF16), SDS((t, HEADS_B * (NOPE + V_DIM)), BF16), SDS((t, ROPE), F32)],
        scratch_shapes=[pltpu.VMEM((seq, QK_B), F32), pltpu.VMEM((seq, V_DIM), F32)],
        compiler_params=_cparams(2),
    )(q, kv, kv, kr, do, c192, s192, p192)


def _loss_final(h, target, gamma):
    t, dn = h.shape
    tm = _tile(t)
    nt = t // tm

    def body(h_ref, t_ref, g_ref, dh_ref, dhb_ref, dg_ref, loss_ref):
        i = pl.program_id(0)
        hv = h_ref[...]
        r = _rms_scale(hv)
        hh = hv * r
        gam = g_ref[...]
        err = hh * gam - t_ref[...]
        part = 0.5 * jnp.sum(jnp.mean(err * err, axis=-1, keepdims=True))

        @pl.when(i == 0)
        def _():
            loss_ref[...] = jnp.zeros_like(loss_ref)

        loss_ref[...] += part
        dy = err * (1.0 / dn)
        _acc_rows(dg_ref, dy * hh, i, nt)
        t1 = dy * gam
        dh = r * (t1 - hh * jnp.mean(t1 * hh, axis=-1, keepdims=True))
        dh_ref[...] = dh
        dhb_ref[...] = dh.astype(BF16)

    row = BS((tm, dn), lambda i: (i, 0))
    return pl.pallas_call(
        body, name="loss_final", grid=(nt,),
        in_specs=[row, row, BS((1, dn), lambda i: (0, 0))],
        out_specs=[row, row, BS((8, dn), lambda i: (0, 0)), BS((8, 128), lambda i: (0, 0))],
        out_shape=[SDS((t, dn), F32), SDS((t, dn), BF16), SDS((8, dn), F32), SDS((8, 128), F32)],
        compiler_params=_cparams(1),
    )(h, target, gamma)


def _ffn_bwd_in(name, dh, w_out, layer, gu, dep=None):
    t, dn = dh.shape
    tm = _tile(t)

    def body(dh_ref, w_ref, gu_ref, *rest):
        o_ref = rest[-1]
        da = 0.5 * _dot_nt(dh_ref[...], w_ref[...])
        g = gu_ref[0].astype(F32)
        u = gu_ref[1].astype(F32)
        sg = jax.nn.sigmoid(g)
        o_ref[0] = (da * u * (sg * (1.0 + g * (1.0 - sg)))).astype(BF16)
        o_ref[1] = (da * (g * sg)).astype(BF16)

    blk = BS((None, 2, tm, FB), lambda j, i: (j, 0, i, 0))
    deps = [] if dep is None else [dep]
    return pl.pallas_call(
        body, name=name, grid=(NJ, t // tm),
        in_specs=[BS((tm, dn), lambda j, i: (i, 0)), BS((None, None, FB, dn), lambda j, i: (layer, j, 0, 0)), blk]
        + [_dep_spec(2)] * len(deps),
        out_specs=blk, out_shape=SDS((NJ, 2, t, FB), BF16),
        compiler_params=_cparams(2),
    )(dh, w_out, gu, *deps)


def _mm_nt_plain(name, xf, w, dep=None):
    t, dn = xf.shape
    n = w.shape[0]
    tm = _tile(t)

    def body(x_ref, w_ref, *rest):
        rest[-1][...] = _dot_nt(x_ref[...], w_ref[...]).astype(BF16)

    deps = [] if dep is None else [dep]
    return pl.pallas_call(
        body, name=name, grid=(t // tm,),
        in_specs=[BS((tm, dn), lambda i: (i, 0)), BS((n, dn), lambda i: (0, 0))] + [_dep_spec(1)] * len(deps),
        out_specs=BS((tm, n), lambda i: (i, 0)), out_shape=SDS((t, n), BF16),
        compiler_params=_cparams(1),
    )(xf, w, *deps)


def _mm_tn(name, xa, x_spec, ya, y_spec, out_shape, out_spec, nj, scale=None):
    def body(x_ref, y_ref, o_ref):
        acc = _dot_tn(x_ref[...], y_ref[...])
        o_ref[...] = (acc if scale is None else scale * acc).astype(BF16)

    return pl.pallas_call(
        body, name=name, grid=(nj,),
        in_specs=[x_spec, y_spec], out_specs=out_spec, out_shape=SDS(out_shape, BF16),
        compiler_params=_cparams(1),
    )(xa, ya)


def _dw_qkv(hn, dqkv3, wb):
    t, dn = hn.shape
    per = wb // 128
    tm = _tile(t)

    def body(x_ref, y_ref, o_ref, acc):
        i = pl.program_id(0)
        xv = x_ref[...]
        for j in range(NDEV):
            cols = [y_ref[(per * j + k) // 8, :, ((per * j + k) % 8) * 128 : ((per * j + k) % 8 + 1) * 128] for k in range(per)]
            part = _dot_tn(xv, jnp.concatenate(cols, axis=1))

            @pl.when(i == 0)
            def _():
                acc[j] = part

            @pl.when(i > 0)
            def _():
                acc[j] += part

        @pl.when(i == t // tm - 1)
        def _():
            o_ref[...] = acc[...].astype(BF16)

    return pl.pallas_call(
        body, name="dw_qkv", grid=(t // tm,),
        in_specs=[BS((tm, dn), lambda i: (i, 0)), BS((3, tm, dn), lambda i: (0, i, 0))],
        out_specs=BS((NDEV, dn, wb), lambda i: (0, 0, 0)), out_shape=SDS((NDEV, dn, wb), BF16),
        scratch_shapes=[pltpu.VMEM((NDEV, dn, wb), F32)],
        compiler_params=_cparams(1),
    )(hn, dqkv3)


def _mm_nt_epi(name, ya, y_spec, wa, w_spec, nj, n_out, extra, out_shapes, out_specs, epilogue, tm, nt, mm_fn=None):
    n_extra = len(extra)
    n_outs = len(out_shapes)

    def body(*refs):
        y_ref, w_ref = refs[:2]
        ex = refs[2 : 2 + n_extra]
        outs = refs[2 + n_extra : 2 + n_extra + n_outs]
        i = pl.program_id(0)
        j = pl.program_id(1)
        part = _dot_nt(y_ref[...], w_ref[...]) if mm_fn is None else mm_fn(y_ref, w_ref)
        if nj == 1:
            epilogue(part, ex, outs, i, nt)
            return
        acc = refs[-1]

        @pl.when(j == 0)
        def _():
            acc[...] = part

        @pl.when(j > 0)
        def _():
            acc[...] += part

        @pl.when(j == nj - 1)
        def _():
            epilogue(acc[...], ex, outs, i, nt)

    return pl.pallas_call(
        body, name=name, grid=(nt, nj),
        in_specs=[y_spec, w_spec] + [spec for _, spec in extra],
        out_specs=out_specs, out_shape=out_shapes,
        scratch_shapes=[] if nj == 1 else [pltpu.VMEM((tm, n_out), F32)],
        compiler_params=_cparams(2),
    )(ya, wa, *[arr for arr, _ in extra])


def _norm_bwd(dn, hv, gam):
    r = _rms_scale(hv)
    hh = hv * r
    t1 = dn * gam
    return r * (t1 - hh * jnp.mean(t1 * hh, axis=-1, keepdims=True)), dn * hh


def _norm_bwd_epilogue(has_res, out_dtype):
    def epilogue(dn, ex, outs, i, nt):
        dh, dg_rows = _norm_bwd(dn, ex[0][...], ex[1][...])
        _acc_rows(outs[1], dg_rows, i, nt)
        if has_res:
            dh = dh + ex[2][...]
        outs[0][...] = dh.astype(out_dtype)
        if has_res:
            outs[2][...] = dh.astype(BF16)

    return epilogue


def _mm_nt_norm_bwd(name, ya, y_spec, wa, w_spec, nj, h, gamma, res, out_dtype, mm_fn=None, want_tm=512, dep=None):
    t, n = h.shape
    tm = _tile(t, want_tm)
    nt = t // tm
    row = BS((tm, n), lambda i, j: (i, 0))
    extra = [(h, row), (gamma, BS((1, n), lambda i, j: (0, 0)))]
    out_shapes = [SDS((t, n), out_dtype), SDS((8, n), F32)]
    out_specs = [row, BS((8, n), lambda i, j: (0, 0))]
    if res is not None:
        extra.append((res, row))
        out_shapes.append(SDS((t, n), BF16))
        out_specs.append(row)
    if dep is not None:
        extra.append((dep, _dep_spec(2)))
    return _mm_nt_epi(
        name, ya, y_spec, wa, w_spec, nj, n, extra, out_shapes, out_specs, _norm_bwd_epilogue(res is not None, out_dtype), tm, nt, mm_fn,
    )


def _dev_block(jj):
    return jj // 2 + NJ * (jj % 2)


def _ffn_dn_mm(y_ref, w_ref):
    acc = None
    for jj in range(2 * NJ):
        part = _dot_nt(y_ref[jj], w_ref[_dev_block(jj)])
        acc = part if acc is None else acc + part
    return acc


def _ffn_bwd(tag, dh, dhb, n_in, h_in, gamma, gu, a, w_in, w_out, more_grads, collective_id):
    t, dn = dh.shape
    dgu = _ffn_bwd_in(f"{tag}_bwd_in", dhb, w_out, 0, gu).reshape(2 * NJ, t, FB)
    dw_out = _mm_tn(
        f"{tag}_dw_out", a, BS((None, t, FB), lambda j: (j, 0, 0)), dhb, BS((t, dn), lambda j: (0, 0)),
        (NJ, FB, dn), BS((None, FB, dn), lambda j: (j, 0, 0)), NJ, scale=0.5,
    )
    dw_in = _mm_tn(
        f"{tag}_dw_in", n_in, BS((t, dn), lambda j: (0, 0)), dgu, BS((None, t, FB), lambda j: (j, 0, 0)),
        (NDEV, dn, FB), BS((None, dn, FB), lambda j: (_dev_block(j), 0, 0)), NDEV,
    )
    entries = [("scatter", dw_in), ("scatter", dw_out.reshape(NDEV, NJ * FB // NDEV, dn))] + [("scatter", g) for g in more_grads]
    landed = _exchange_sc(f"{tag}_reduce", entries, collective_id)
    tm = _tile(t, 256)
    dh_in, dgam, dhb_in = _mm_nt_norm_bwd(
        f"{tag}_dn", dgu, BS((2 * NJ, tm, FB), lambda i, j: (0, i, 0)),
        w_in, BS((None, NDEV, dn, FB), lambda i, j: (0, 0, 0, 0)), 1, h_in, gamma, dh, F32, mm_fn=_ffn_dn_mm, want_tm=256,
    )
    return dh_in, dhb_in, dgam, landed


def _dqkv_mm(per):
    def mm(y_ref, w_ref):
        acc = None
        for j in range(NDEV):
            cols = [y_ref[(per * j + k) // 8, :, ((per * j + k) % 8) * 128 : ((per * j + k) % 8 + 1) * 128] for k in range(per)]
            part = _dot_nt(jnp.concatenate(cols, axis=1), w_ref[j])
            acc = part if acc is None else acc + part
        return acc

    return mm


def _kv_latent_bwd(dkv, w_up, ckr, latent_norm, dkr, c64, s64, p64, seq):
    t, wd = ckr.shape
    hb = w_up.shape[-1]
    tm = _tile(min(seq, 512), min(seq, 512))
    nt = t // tm
    nseq = seq // tm

    def epilogue(dn, ex, outs, i, nt_):
        dlat, dg_rows = _norm_bwd(dn, ex[0][...], ex[1][...])
        _acc_rows(outs[1], dg_rows, i, nt_)
        outs[0][:, :KV_LORA] = dlat.astype(BF16)
        outs[0][:, KV_LORA:] = _rope_bwd(ex[2][...], ex[3][...], ex[4][...], ex[5][...]).astype(BF16)

    pos = BS((tm, ROPE), lambda i, j: (i % nseq, 0))
    extra = [
        (ckr, BS((tm, KV_LORA), lambda i, j: (i, 0))), (latent_norm, BS((1, KV_LORA), lambda i, j: (0, 0))),
        (dkr, BS((tm, ROPE), lambda i, j: (i, 0))), (c64, pos), (s64, pos), (p64, BS((ROPE, ROPE), lambda i, j: (0, 0))),
    ]
    return _mm_nt_epi(
        "kv_latent_bwd", dkv, BS((tm, hb), lambda i, j: (i, j)), w_up, BS((None, KV_LORA, hb), lambda i, j: (j, 0, 0)),
        HEADS_B, KV_LORA, extra, [SDS((t, wd), BF16), SDS((8, KV_LORA), F32)],
        [BS((tm, wd), lambda i, j: (i, 0)), BS((8, KV_LORA), lambda i, j: (0, 0))], epilogue, tm, nt,
    )


def _adamw(name, parts, w, m, v):
    n_layers = len(parts)
    rows, cols = w.shape[0] // n_layers, w.shape[1]
    tr = max(d for d in range(8, min(rows, 256) + 1, 8) if rows % d == 0)
    nb = rows // tr

    def body(*refs):
        p_refs = refs[:n_layers]
        w_ref, m_ref, v_ref, g_ref, d_ref, nm_ref, nv_ref = refs[n_layers : n_layers + 7]
        layer = pl.program_id(0)
        for lp in range(n_layers):

            @pl.when(layer == lp)
            def _():
                g = p_refs[lp][0].astype(F32)
                for k in range(1, NDEV):
                    g = g + p_refs[lp][k].astype(F32)
                g_ref[...] = g

        g = g_ref[...]
        nm = ADAM_B1 * m_ref[...] + (1.0 - ADAM_B1) * g
        nv = ADAM_B2 * v_ref[...] + (1.0 - ADAM_B2) * (g * g)
        nm_ref[...] = nm
        nv_ref[...] = nv
        m_hat = nm / (1.0 - ADAM_B1 ** ADAM_STEP)
        v_hat = nv / (1.0 - ADAM_B2 ** ADAM_STEP)
        d_ref[...] = -ADAM_LR * (m_hat / (jnp.sqrt(v_hat) + ADAM_EPS) + ADAM_WD * w_ref[...])

    def part_spec(lp):
        return BS((NDEV, tr, cols), lambda l, i: (0, jnp.where(l == lp, i, jnp.where(l < lp, 0, nb - 1)), 0))

    row = BS((tr, cols), lambda l, i: (l * nb + i, 0))
    return pl.pallas_call(
        body, name=name, grid=(n_layers, nb),
        in_specs=[part_spec(lp) for lp in range(n_layers)] + [row, row, row],
        out_specs=[row] * 4, out_shape=[SDS(w.shape, F32)] * 4,
        compiler_params=_cparams(2),
    )(*parts, w, m, v)


def _pack_small(ffn1_norm, mix_norm, ffn2_norm, kv_norm, final_norm, q_norm, latent_norm, rel_bias, last_row):
    dn = ffn1_norm.shape[-1]

    def rows_of(a, n_rows):
        flat = a.reshape(-1)
        return jnp.pad(flat, (0, n_rows * dn - flat.shape[0])).reshape(n_rows, dn)

    return jnp.concatenate(
        [
            ffn1_norm.reshape(2, dn), mix_norm.reshape(2, dn), ffn2_norm.reshape(2, dn), kv_norm.reshape(1, dn),
            final_norm.reshape(1, dn), rows_of(q_norm, 1), rows_of(latent_norm, 1), rows_of(rel_bias, 5), rows_of(last_row, 1),
        ],
        axis=0,
    )


def _unpack_small(pack):
    dn = pack.shape[-1]
    return dict(
        ffn1_norm=pack[0:2], mix_norm=pack[2:4], ffn2_norm=pack[4:6], kv_norm=pack[6], final_norm=pack[7],
        b_q_norm=pack[8, :Q_LORA].reshape(1, Q_LORA), kv_latent_norm=pack[9, :KV_LORA],
        a_rel_bias=pack[10:15].reshape(-1)[: HEADS_A * NREL].reshape(1, HEADS_A, NREL), last=pack[15],
    )


def kernel(x, ffn1_norm, ffn1_w_in, ffn1_w_out, mix_norm, ffn2_norm, ffn2_w_in, ffn2_w_out, a_w_qkv, a_rel_bias, a_w_o, kv_norm, kv_w_down, kv_latent_norm, kv_w_up, b_w_dq, b_q_norm, b_w_uq, b_w_o, final_norm, loss_target, m_ffn1_norm, m_ffn1_w_in, m_ffn1_w_out, m_mix_norm, m_ffn2_norm, m_ffn2_w_in, m_ffn2_w_out, m_a_w_qkv, m_a_rel_bias, m_a_w_o, m_kv_norm, m_kv_w_down, m_kv_latent_norm, m_kv_w_up, m_b_w_dq, m_b_q_norm, m_b_w_uq, m_b_w_o, m_final_norm, v_ffn1_norm, v_ffn1_w_in, v_ffn1_w_out, v_mix_norm, v_ffn2_norm, v_ffn2_w_in, v_ffn2_w_out, v_a_w_qkv, v_a_rel_bias, v_a_w_o, v_kv_norm, v_kv_w_down, v_kv_latent_norm, v_kv_w_up, v_b_w_dq, v_b_q_norm, v_b_w_uq, v_b_w_o, v_final_norm):
    bl, seq, dn = x.shape
    t = bl * seq
    tm = _tile(t)
    nt = t // tm
    x2 = x.reshape(t, dn)
    target2 = loss_target.reshape(t, dn)

    def gathered(*ws):
        return [("gather", w.astype(BF16)) for w in ws]

    groups = [
        gathered(ffn1_w_in[0]), gathered(ffn1_w_out[0]), gathered(a_w_qkv[0], a_w_o[0]), gathered(ffn2_w_in[0], ffn2_w_out[0]),
        gathered(kv_w_down, kv_w_up), gathered(ffn1_w_in[1], ffn1_w_out[1]), gathered(b_w_dq[0], b_w_uq[0], b_w_o[0]),
        gathered(ffn2_w_in[1], ffn2_w_out[1]),
    ]
    ag = [_exchange_sc(f"gather_{k}", group, GATHER_IDS[k]) for k, group in enumerate(groups)]

    def as_w_in(w):
        return w.reshape(1, NDEV, dn, FB)

    def as_w_out(w):
        return w.reshape(1, NJ, FB, dn)

    c64, s64, p64, c192, s192, p192 = _rope_tables(seq)
    q_norm = b_q_norm.reshape(1, Q_LORA)
    latent_norm = kv_latent_norm.reshape(1, KV_LORA)
    bias = _window_bias(_rel_bias_fwd(jnp.pad(a_rel_bias[0], ((0, 0), (0, NREL_PAD - NREL)))))

    h0, h1, h2, n1, hn, n2, gu1, gu2, a1, a2, w_in1, w_in2, w_out1, w_out2 = ([None, None] for _ in range(14))
    h0[0] = x2
    (n1[0],) = _norm_fwd("norm_x", x2, ffn1_norm[0:1])
    w_in1[0] = as_w_in(ag[0][0])
    gu1[0], a1[0] = _ffn_in("ffn1_in_0", n1[0], w_in1[0], 0)
    w_out1[0] = as_w_out(ag[1][0])
    h1[0], hn[0] = _mm_res_norm("ffn1_out_0", a1[0], w_out1[0], 0, h0[0], mix_norm[0:1], 0.5)
    w_qkv, w_o_a = ag[2]
    qkv_wb = w_qkv.shape[-1]
    w_o_a = w_o_a.reshape(1, 1, dn, dn)
    qkv3 = _qkv_proj("qkv_proj", hn[0], w_qkv)
    o_a = _attn_a_fwd(qkv3, bias, bl, seq)
    h2[0], n2[0] = _mm_res_norm("attn_a_out", o_a.reshape(1, t, dn), w_o_a, 0, h1[0], ffn2_norm[0:1], 1.0)
    w_in2[0], w_out2[0] = as_w_in(ag[3][0]), as_w_out(ag[3][1])
    gu2[0], a2[0] = _ffn_in("ffn2_in_0", n2[0], w_in2[0], 0)
    h0[1], hk, n1[1] = _mm_res_norm(
        "ffn2_out_0", a2[0], w_out2[0], 0, h2[0], jnp.concatenate([kv_norm.reshape(1, dn), ffn1_norm[1:2]], axis=0), 0.5
    )
    w_down, w_up = ag[4]
    w_down = w_down.reshape(dn, KV_LORA + ROPE)
    ckr, ckv, kr = _kv_down(hk, w_down, latent_norm, c64, s64, p64, seq)
    kv = _kv_up(ckv, w_up)
    w_in1[1], w_out1[1] = as_w_in(ag[5][0]), as_w_out(ag[5][1])
    gu1[1], a1[1] = _ffn_in("ffn1_in_1", n1[1], w_in1[1], 0)
    h1[1], hn[1] = _mm_res_norm("ffn1_out_1", a1[1], w_out1[1], 0, h0[1], mix_norm[1:2], 0.5)
    w_dq, w_uq, w_o_b = ag[6]
    w_dq = w_dq.reshape(dn, Q_LORA)
    w_o_b = w_o_b.reshape(1, 1, dn, dn)
    cq_pre, cq = _q_down(hn[1], w_dq, q_norm)
    q = _q_up(cq, w_uq, c192, s192, p192, seq)
    o_b = _mla_fwd(q, kv, kr, bl, seq)
    h2[1], n2[1] = _mm_res_norm("attn_b_out", o_b.reshape(1, t, dn), w_o_b, 0, h1[1], ffn2_norm[1:2], 1.0)
    w_in2[1], w_out2[1] = as_w_in(ag[7][0]), as_w_out(ag[7][1])
    gu2[1], a2[1] = _ffn_in("ffn2_in_1", n2[1], w_in2[1], 0)
    (h_last,) = _mm_res_norm("ffn2_out_1", a2[1], w_out2[1], 0, h2[1], None, 0.5)
    dh, dhb, dg_final, loss_part = _loss_final(h_last, target2, final_norm.reshape(1, dn))

    dg_ffn1, dg_mix, dg_ffn2, rs_ffn1, rs_ffn2 = ([None, None] for _ in range(5))

    def whole(rows, cols):
        return BS((rows, cols), lambda j: (0, 0))

    def dw_rows(name, xa, ya):
        n = ya.shape[1]
        return _mm_tn(name, xa, whole(t, dn), ya, whole(t, n), (dn, n), whole(dn, n), 1).reshape(NDEV, dn // NDEV, n)

    dh, dhb, dg_ffn2[1], rs_ffn2[1] = _ffn_bwd(
        "ffn2_1", dh, dhb, n2[1], h2[1], ffn2_norm[1:2], gu2[1], a2[1], w_in2[1], w_out2[1], [], REDUCE_IDS[0]
    )
    do_b = _mm_nt_plain("attn_b_do", dhb, w_o_b.reshape(dn, dn))
    dw_o_b = dw_rows("attn_b_dwo", o_b, dhb)
    dq_pre, dkv, dkr = _mla_bwd(q, kv, kr, do_b, c192, s192, p192, bl, seq)
    dw_uq = _mm_tn(
        "dw_uq", cq, whole(t, Q_LORA), dq_pre, BS((None, t, QK_B), lambda j: (j, 0, 0)),
        (HEADS_B, Q_LORA, QK_B), BS((None, Q_LORA, QK_B), lambda j: (j, 0, 0)), HEADS_B,
    )
    dcq_pre, dg_q = _mm_nt_norm_bwd(
        "dcq", dq_pre, BS((None, tm, QK_B), lambda i, j: (j, i, 0)), w_uq, BS((None, Q_LORA, QK_B), lambda i, j: (j, 0, 0)),
        HEADS_B, cq_pre, q_norm, None, BF16,
    )
    dw_dq = dw_rows("dw_dq", hn[1], dcq_pre)
    dh, dg_mix[1], dhb = _mm_nt_norm_bwd(
        "dhn_b", dcq_pre, BS((tm, Q_LORA), lambda i, j: (i, 0)), w_dq, BS((dn, Q_LORA), lambda i, j: (0, 0)),
        1, h1[1], mix_norm[1:2], dh, F32,
    )
    dh, dhb, dg_ffn1[1], rs_ffn1[1] = _ffn_bwd(
        "ffn1_1", dh, dhb, n1[1], h0[1], ffn1_norm[1:2], gu1[1], a1[1], w_in1[1], w_out1[1], [dw_o_b, dw_uq, dw_dq], REDUCE_IDS[1]
    )
    dw_up = _mm_tn(
        "dw_up", ckv, whole(t, KV_LORA), dkv, BS((t, NOPE + V_DIM), lambda j: (0, j)),
        (HEADS_B, KV_LORA, NOPE + V_DIM), BS((None, KV_LORA, NOPE + V_DIM), lambda j: (j, 0, 0)), HEADS_B,
    )
    dckr, dg_latent = _kv_latent_bwd(dkv, w_up, ckr, latent_norm, dkr, c64, s64, p64, seq)
    dw_down = dw_rows("dw_down", hk, dckr)
    dh, dg_kv, dhb = _mm_nt_norm_bwd(
        "dhk", dckr, BS((tm, KV_LORA + ROPE), lambda i, j: (i, 0)), w_down, BS((dn, KV_LORA + ROPE), lambda i, j: (0, 0)),
        1, h0[1], kv_norm.reshape(1, dn), dh, F32,
    )
    dh, dhb, dg_ffn2[0], rs_ffn2[0] = _ffn_bwd(
        "ffn2_0", dh, dhb, n2[0], h2[0], ffn2_norm[0:1], gu2[0], a2[0], w_in2[0], w_out2[0], [dw_up, dw_down], REDUCE_IDS[2]
    )
    do_a = _mm_nt_plain("attn_a_do", dhb, w_o_a.reshape(dn, dn))
    dw_o_a = dw_rows("attn_a_dwo", o_a, dhb)
    dqkv3, dbias = _attn_a_bwd(qkv3, do_a, bias, bl, seq)
    dw_qkv = _dw_qkv(hn[0], dqkv3, qkv_wb)
    dh, dg_mix[0], dhb = _mm_nt_norm_bwd(
        "dhn_a", dqkv3, BS((3, tm, dn), lambda i, j: (0, i, 0)), w_qkv, BS((NDEV, dn, qkv_wb), lambda i, j: (0, 0, 0)),
        1, h1[0], mix_norm[0:1], dh, F32, mm_fn=_dqkv_mm(qkv_wb // 128),
    )
    r_o_a, r_qkv = _exchange_sc("attn_a_reduce", [("scatter", dw_o_a), ("scatter", dw_qkv)], REDUCE_IDS[3])
    dh, dhb, dg_ffn1[0], rs_ffn1[0] = _ffn_bwd(
        "ffn1_0", dh, dhb, n1[0], h0[0], ffn1_norm[0:1], gu1[0], a1[0], w_in1[0], w_out1[0], [], REDUCE_IDS[4]
    )
    grad_x = dh.reshape(bl, seq, dn)
    dtable = _rel_bias_bwd(_window_bias_bwd(dbias))[:, :NREL]

    small = _pack_small(
        jnp.stack([dg_ffn1[0][0], dg_ffn1[1][0]]), jnp.stack([dg_mix[0][0], dg_mix[1][0]]), jnp.stack([dg_ffn2[0][0], dg_ffn2[1][0]]),
        dg_kv[0], dg_final[0], dg_q[0], dg_latent[0], dtable, loss_part[0],
    )
    (r_small,) = _exchange("gather_small_grads", [("gather", small)])

    def update(name, parts, w, m, v):
        n_layers = len(parts)
        rows = int(np.prod(w.shape[:-1]))
        cols = w.shape[-1]
        parts = [p.reshape(NDEV, rows // n_layers, cols) for p in parts]
        outs = _adamw(name, parts, w.reshape(rows, cols), m.reshape(rows, cols), v.reshape(rows, cols))
        return [o.reshape(w.shape) for o in outs]

    res = {}
    r_in2_1, r_out2_1 = rs_ffn2[1]
    r_in1_1, r_out1_1, r_o_b, r_uq, r_dq = rs_ffn1[1]
    r_in2_0, r_out2_0, r_up, r_down = rs_ffn2[0]
    r_in1_0, r_out1_0 = rs_ffn1[0]
    res["ffn2_w_in"] = update("adamw_ffn2_w_in", [r_in2_0, r_in2_1], ffn2_w_in, m_ffn2_w_in, v_ffn2_w_in)
    res["ffn2_w_out"] = update("adamw_ffn2_w_out", [r_out2_0, r_out2_1], ffn2_w_out, m_ffn2_w_out, v_ffn2_w_out)
    res["kv_w_down"] = update("adamw_kv_w_down", [r_down], kv_w_down, m_kv_w_down, v_kv_w_down)
    res["kv_w_up"] = update("adamw_kv_w_up", [r_up], kv_w_up, m_kv_w_up, v_kv_w_up)
    res["b_w_dq"] = update("adamw_b_w_dq", [r_dq], b_w_dq, m_b_w_dq, v_b_w_dq)
    res["b_w_uq"] = update("adamw_b_w_uq", [r_uq], b_w_uq, m_b_w_uq, v_b_w_uq)
    res["b_w_o"] = update("adamw_b_w_o", [r_o_b], b_w_o, m_b_w_o, v_b_w_o)
    res["a_w_qkv"] = update("adamw_a_w_qkv", [r_qkv], a_w_qkv, m_a_w_qkv, v_a_w_qkv)
    res["a_w_o"] = update("adamw_a_w_o", [r_o_a], a_w_o, m_a_w_o, v_a_w_o)
    res["ffn1_w_in"] = update("adamw_ffn1_w_in", [r_in1_0, r_in1_1], ffn1_w_in, m_ffn1_w_in, v_ffn1_w_in)
    res["ffn1_w_out"] = update("adamw_ffn1_w_out", [r_out1_0, r_out1_1], ffn1_w_out, m_ffn1_w_out, v_ffn1_w_out)
    zero_row = jnp.zeros((dn,), F32)
    packs = [
        _pack_small(f1, mx, f2, kvn, fin, qn, lat, rel, zero_row)
        for f1, mx, f2, kvn, fin, qn, lat, rel in (
            (ffn1_norm, mix_norm, ffn2_norm, kv_norm, final_norm, b_q_norm, kv_latent_norm, a_rel_bias),
            (m_ffn1_norm, m_mix_norm, m_ffn2_norm, m_kv_norm, m_final_norm, m_b_q_norm, m_kv_latent_norm, m_a_rel_bias),
            (v_ffn1_norm, v_mix_norm, v_ffn2_norm, v_kv_norm, v_final_norm, v_b_q_norm, v_kv_latent_norm, v_a_rel_bias),
        )
    ]
    small_out = [_unpack_small(o) for o in _adamw("adamw_small", [r_small], *packs)]
    for name in ("ffn1_norm", "mix_norm", "ffn2_norm", "a_rel_bias", "kv_norm", "kv_latent_norm", "b_q_norm", "final_norm"):
        res[name] = [so[name] for so in small_out]
    loss = small_out[0]["last"][0]

    order = [
        "ffn1_norm", "ffn1_w_in", "ffn1_w_out", "mix_norm", "ffn2_norm", "ffn2_w_in", "ffn2_w_out", "a_w_qkv", "a_rel_bias",
        "a_w_o", "kv_norm", "kv_w_down", "kv_latent_norm", "kv_w_up", "b_w_dq", "b_q_norm", "b_w_uq", "b_w_o", "final_norm",
    ]
    return (loss, grad_x, *[res[n][0] for n in order], *[res[n][1] for n in order], *[res[n][2] for n in order], *[res[n][3] for n in order])
```

```python
import functools

import jax
import jax.numpy as jnp
import numpy as np
from jax import lax
from jax.experimental import pallas as pl
from jax.experimental.pallas import tpu as pltpu
from jax.experimental.pallas import tpu_sc as plsc

NDEV = 8
D_MODEL = 1024
D_FF = 2816
FB = 2 * D_FF // NDEV
NJ = D_FF // FB
CHUNK = 64
LEFT_CHUNKS = 8
PAD = LEFT_CHUNKS * CHUNK
BAND = PAD + CHUNK
CHUNKS_PER_STEP = 4
WINDOW = PAD + CHUNKS_PER_STEP * CHUNK
STEP_ROWS = CHUNKS_PER_STEP * 2 * CHUNK
MAX_REL = 128
NREL = 2 * MAX_REL + 1
NREL_PAD = 384
HEADS_A = 16
HEADS_B = 8
NOPE = 128
ROPE = 64
QK_B = NOPE + ROPE
V_DIM = 128
Q_LORA = 768
KV_LORA = 256
ROPE_THETA = 10000.0
EPS = 1e-6
NEG_INF = -1e30
MLA_TQ = 256
ADAM_LR = 0.001
ADAM_B1 = 0.9
ADAM_B2 = 0.999
ADAM_EPS = 1e-08
ADAM_WD = 0.01
ADAM_STEP = 10
PACK_ROWS = 16
GATHER_IDS = tuple(range(1, 9))
REDUCE_IDS = tuple(range(9, 14))
VMEM_LIMIT_BYTES = 56 * 1024 * 1024

F32 = jnp.float32
BF16 = jnp.bfloat16
SDS = jax.ShapeDtypeStruct
BS = pl.BlockSpec
HIGHEST = lax.Precision.HIGHEST
MESH = pl.DeviceIdType.MESH


def _cparams(n_axes):
    return pltpu.CompilerParams(dimension_semantics=("arbitrary",) * n_axes, vmem_limit_bytes=VMEM_LIMIT_BYTES)


def _tile(t, want=512):
    return want if t % want == 0 else t


def _dot(a, b):
    return jnp.dot(a, b, preferred_element_type=F32)


def _dot_nt(a, b):
    return lax.dot_general(a, b, (((1,), (1,)), ((), ())), preferred_element_type=F32)


def _dot_tn(a, b):
    return lax.dot_general(a, b, (((0,), (0,)), ((), ())), preferred_element_type=F32)


def _dot_exact(a, b):
    return jnp.dot(a, b, precision=HIGHEST, preferred_element_type=F32)


def _rms_scale(h):
    return lax.rsqrt(jnp.mean(h * h, axis=-1, keepdims=True) + EPS)


def _acc_rows(ref, val, step, n_steps):
    part = val.reshape(val.shape[0] // 8, 8, val.shape[1]).sum(axis=0)

    @pl.when(step == 0)
    def _():
        ref[...] = part

    @pl.when(step > 0)
    def _():
        ref[...] += part

    @pl.when(step == n_steps - 1)
    def _():
        ref[...] = jnp.broadcast_to(jnp.sum(ref[...], axis=0, keepdims=True), ref.shape)


def _exchange_plan(entries):
    ins = [e[1] for e in entries]
    kinds = [e[0] for e in entries]
    lands = [SDS((NDEV,) + a.shape if k == "gather" else a.shape, a.dtype) for k, a in zip(kinds, ins)]
    return ins, lands, kinds


def _mesh_place():
    x, y, c = lax.axis_index("x"), lax.axis_index("y"), lax.axis_index("c")
    return (x, y, c), 4 * x + 2 * y + c


def _flipped(place, p):
    x, y, c = place
    px = 1 - x if p & 4 else x
    py = 1 - y if p & 2 else y
    pc = 1 - c if p & 1 else c
    return (px, py, pc), 4 * px + 2 * py + pc


def _ends(kind, src_ref, land_ref, origin, target):
    if kind == "gather":
        return src_ref, land_ref.at[origin]
    return src_ref.at[target], land_ref.at[origin]


def _remote(kind, src_ref, land_ref, send_sems, recv_sems, k, p, place, me, arriving):
    peer_pos, peer = _flipped(place, p)
    src, dst = _ends(kind, src_ref, land_ref, me, peer)
    if arriving:
        dst = _ends(kind, src_ref, land_ref, peer, me)[1]
    sem = k * (NDEV - 1) + p - 1
    return pltpu.make_async_remote_copy(
        src_ref=src, dst_ref=dst, send_sem=send_sems.at[sem], recv_sem=recv_sems.at[sem], device_id=peer_pos, device_id_type=MESH,
    )


def _exchange(name, entries):
    ins, lands, kinds = _exchange_plan(entries)
    n = len(ins)

    def body(*refs):
        in_refs, land_refs = refs[:n], refs[n : 2 * n]
        send_sems, recv_sems, local_sems = refs[2 * n :]
        place, me = _mesh_place()
        local = []
        for k in range(n):
            src, dst = _ends(kinds[k], in_refs[k], land_refs[k], me, me)
            local.append(pltpu.make_async_copy(src, dst, local_sems.at[k]))
            local[-1].start()
        sends = []
        for p in range(1, NDEV):
            for k in range(n):
                sends.append(_remote(kinds[k], in_refs[k], land_refs[k], send_sems, recv_sems, k, p, place, me, False))
                sends[-1].start()
        for p in range(1, NDEV):
            for k in range(n):
                _remote(kinds[k], in_refs[k], land_refs[k], send_sems, recv_sems, k, p, place, me, True).wait_recv()
        for cp in sends:
            cp.wait_send()
        for cp in local:
            cp.wait()

    any_spec = BS(memory_space=pl.ANY)
    return pl.pallas_call(
        body, name=name, out_shape=lands, in_specs=[any_spec] * n, out_specs=[any_spec] * n,
        scratch_shapes=[
            pltpu.SemaphoreType.DMA((n * (NDEV - 1),)), pltpu.SemaphoreType.DMA((n * (NDEV - 1),)), pltpu.SemaphoreType.DMA((n,)),
        ],
    )(*ins)


HBM_SPEC = BS(memory_space=pltpu.HBM)
SEM_SPEC = BS(memory_space=pltpu.SEMAPHORE)
DATAFLOW = pltpu.SideEffectType.DATAFLOW_SIDE_EFFECTING


def _exchange_start(name, groups):
    plans = [_exchange_plan(g) for g in groups]
    ins = [a for plan in plans for a in plan[0]]
    lands = [s for plan in plans for s in plan[1]]
    kinds = [kind for plan in plans for kind in plan[2]]
    n_in, n_groups = len(ins), len(groups)

    def body(*refs):
        in_refs, land_refs = refs[:n_in], refs[n_in : 2 * n_in]
        sems = refs[2 * n_in : 2 * n_in + 2 * n_groups]
        token = refs[4 * n_in + 2 * n_groups]
        local_sems = refs[4 * n_in + 2 * n_groups + 1]
        place, me = _mesh_place()
        local = []
        for k, kind in enumerate(kinds):
            src, dst = _ends(kind, in_refs[k], land_refs[k], me, me)
            local.append(pltpu.make_async_copy(src, dst, local_sems.at[k]))
            local[-1].start()
        base = 0
        for g, plan in enumerate(plans):
            for p in range(1, NDEV):
                for k, kind in enumerate(plan[2]):
                    _remote(kind, in_refs[base + k], land_refs[base + k], sems[2 * g], sems[2 * g + 1], k, p, place, me, False).start()
            base += len(plan[2])
        for cp in local:
            cp.wait()
        token[...] = jnp.zeros_like(token)

    sem_shapes = []
    for plan in plans:
        sem_shapes += [pltpu.SemaphoreType.DMA((len(plan[2]) * (NDEV - 1),))] * 2
    outs = pl.pallas_call(
        body, name=name,
        out_shape=sem_shapes + [pltpu.HBM(a.shape, a.dtype) for a in ins] + [pltpu.HBM(s.shape, s.dtype) for s in lands] + [SDS((8, 128), F32)],
        in_specs=[HBM_SPEC] * (2 * n_in),
        out_specs=[SEM_SPEC] * (2 * n_groups) + [HBM_SPEC] * (2 * n_in) + [BS(memory_space=pltpu.VMEM)],
        input_output_aliases={i: 2 * n_groups + i for i in range(2 * n_in)},
        scratch_shapes=[pltpu.SemaphoreType.DMA((n_in,))],
        compiler_params=pltpu.CompilerParams(has_side_effects=DATAFLOW),
    )(
        *[pltpu.with_memory_space_constraint(a, pltpu.HBM) for a in ins],
        *[pltpu.with_memory_space_constraint(lax.empty(s.shape, s.dtype), pltpu.HBM) for s in lands],
    )
    sems, srcs, landed, token = outs[: 2 * n_groups], outs[2 * n_groups : 2 * n_groups + n_in], outs[2 * n_groups + n_in : -1], outs[-1]
    started, base = [], 0
    for g, plan in enumerate(plans):
        n = len(plan[2])
        started.append((sems[2 * g], sems[2 * g + 1], srcs[base : base + n], landed[base : base + n], plan[2]))
        base += n
    return started, token


def _exchange_wait(name, started, after):
    send_sems, recv_sems, srcs, landed, kinds = started
    n = len(kinds)

    def body(*refs):
        in_refs, land_refs = refs[:n], refs[n : 2 * n]
        send_ref, recv_ref = refs[2 * n], refs[2 * n + 1]
        place, me = _mesh_place()
        for p in range(1, NDEV):
            for k in range(n):
                _remote(kinds[k], in_refs[k], land_refs[k], send_ref, recv_ref, k, p, place, me, True).wait_recv()
        for p in range(1, NDEV):
            for k in range(n):
                _remote(kinds[k], in_refs[k], land_refs[k], send_ref, recv_ref, k, p, place, me, False).wait_send()

    outs = pl.pallas_call(
        body, name=name,
        out_shape=[pltpu.HBM(a.shape, a.dtype) for a in srcs] + [pltpu.HBM(a.shape, a.dtype) for a in landed],
        in_specs=[HBM_SPEC] * (2 * n) + [SEM_SPEC, SEM_SPEC, BS(memory_space=pl.ANY)],
        out_specs=[HBM_SPEC] * (2 * n),
        input_output_aliases={i: i for i in range(2 * n)},
        compiler_params=pltpu.CompilerParams(has_side_effects=DATAFLOW),
    )(*srcs, *landed, send_sems, recv_sems, after)
    return outs[n:]


def _exchange_sc(name, entries, collective_id):
    ins, lands, kinds = _exchange_plan(entries)
    n = len(ins)
    in_refs = [jax.new_ref(a, memory_space=pltpu.MemorySpace.HBM) for a in ins]
    land_refs = [jax.empty_ref(s, memory_space=pltpu.MemorySpace.HBM) for s in lands]

    @pl.kernel(
        mesh=plsc.ScalarSubcoreMesh(axis_name="sequencer", num_cores=1), name=name,
        scratch_types=(
            pltpu.SemaphoreType.DMA((n * (NDEV - 1),)), pltpu.SemaphoreType.DMA((n * (NDEV - 1),)), pltpu.SemaphoreType.DMA((n,)),
        ),
        compiler_params=pltpu.CompilerParams(collective_id=collective_id),
    )
    def launch(send_sems, recv_sems, local_sems):
        place, me = _mesh_place()
        barrier = pltpu.get_barrier_semaphore()
        for p in range(1, NDEV):
            pl.semaphore_signal(barrier, inc=1, device_id=_flipped(place, p)[0], device_id_type=MESH)
        pl.semaphore_wait(barrier, NDEV - 1)
        local = []
        for k in range(n):
            src, dst = _ends(kinds[k], in_refs[k], land_refs[k], me, me)
            local.append(pltpu.make_async_copy(src, dst, local_sems.at[k]))
            local[-1].start()
        sends = []
        if all(kind == "gather" for kind in kinds):
            for p in (1, 2, 4, 6):
                for k in range(n):
                    sends.append(_remote(kinds[k], in_refs[k], land_refs[k], send_sems, recv_sems, k, p, place, me, False))
                    sends[-1].start()
            sibling_pos, _ = _flipped(place, 1)
            for f in (2, 4, 6):
                _, origin = _flipped(place, f)
                for k in range(n):
                    _remote(kinds[k], in_refs[k], land_refs[k], send_sems, recv_sems, k, f, place, me, True).wait_recv()
                    sem = k * (NDEV - 1) + f
                    sends.append(
                        pltpu.make_async_remote_copy(
                            src_ref=land_refs[k].at[origin], dst_ref=land_refs[k].at[origin], send_sem=send_sems.at[sem],
                            recv_sem=recv_sems.at[sem], device_id=sibling_pos, device_id_type=MESH,
                        )
                    )
                    sends[-1].start()
            for p in (1, 3, 5, 7):
                for k in range(n):
                    _remote(kinds[k], in_refs[k], land_refs[k], send_sems, recv_sems, k, p, place, me, True).wait_recv()
        else:
            for p in range(1, NDEV):
                for k in range(n):
                    sends.append(_remote(kinds[k], in_refs[k], land_refs[k], send_sems, recv_sems, k, p, place, me, False))
                    sends[-1].start()
            for p in range(1, NDEV):
                for k in range(n):
                    _remote(kinds[k], in_refs[k], land_refs[k], send_sems, recv_sems, k, p, place, me, True).wait_recv()
        for cp in sends:
            cp.wait_send()
        for cp in local:
            cp.wait()

    launch()
    return [r[...] for r in land_refs]


def _dep_spec(n_axes):
    return BS((8, 128), (lambda i: (0, 0)) if n_axes == 1 else (lambda i, j: (0, 0)))


def _norm_fwd(name, h, gammas):
    t, dn = h.shape
    ng = gammas.shape[0]
    tm = _tile(t)

    def body(h_ref, g_ref, *outs):
        hv = h_ref[...]
        hh = hv * _rms_scale(hv)
        for i, o_ref in enumerate(outs):
            o_ref[...] = (hh * g_ref[i : i + 1, :]).astype(BF16)

    row = BS((tm, dn), lambda i: (i, 0))
    return pl.pallas_call(
        body, name=name, grid=(t // tm,),
        in_specs=[row, BS((ng, dn), lambda i: (0, 0))],
        out_specs=[row] * ng, out_shape=[SDS((t, dn), BF16)] * ng,
        compiler_params=_cparams(1),
    )(h, gammas)


def _ffn_in(name, n, w_in, layer):
    t, dn = n.shape
    tm = _tile(t, 1024)

    def body(n_ref, wg_ref, wu_ref, gu_ref, a_ref):
        xv = n_ref[...]
        g = _dot(xv, wg_ref[...])
        u = _dot(xv, wu_ref[...])
        gu_ref[0] = g.astype(BF16)
        gu_ref[1] = u.astype(BF16)
        a_ref[...] = (g * jax.nn.sigmoid(g) * u).astype(BF16)

    return pl.pallas_call(
        body, name=name, grid=(NJ, t // tm),
        in_specs=[
            BS((tm, dn), lambda j, i: (i, 0)),
            BS((None, None, dn, FB), lambda j, i: (layer, j, 0, 0)),
            BS((None, None, dn, FB), lambda j, i: (layer, j + NJ, 0, 0)),
        ],
        out_specs=[BS((None, 2, tm, FB), lambda j, i: (j, 0, i, 0)), BS((None, tm, FB), lambda j, i: (j, i, 0))],
        out_shape=[SDS((NJ, 2, t, FB), BF16), SDS((NJ, t, FB), BF16)],
        compiler_params=_cparams(2),
    )(n, w_in, w_in)


def _mm_res_norm(name, a, w, layer, h_in, gammas, scale):
    nk, t, kb = a.shape
    dn = w.shape[-1]
    ng = 0 if gammas is None else gammas.shape[0]
    tm = _tile(t)

    def body(*refs):
        a_ref, w_ref, h_ref = refs[:3]
        g_ref = refs[3] if ng else None
        outs = refs[3 + (1 if ng else 0) :]
        acc = _dot(a_ref[0], w_ref[0])
        for k in range(1, nk):
            acc += _dot(a_ref[k], w_ref[k])
        ho = h_ref[...] + scale * acc
        outs[0][...] = ho
        if ng:
            hh = ho * _rms_scale(ho)
            for i in range(ng):
                outs[1 + i][...] = (hh * g_ref[i : i + 1, :]).astype(BF16)

    row = BS((tm, dn), lambda i: (i, 0))
    in_specs = [BS((nk, tm, kb), lambda i: (0, i, 0)), BS((None, nk, kb, dn), lambda i: (layer, 0, 0, 0)), row]
    args = [a, w, h_in]
    if ng:
        in_specs.append(BS((ng, dn), lambda i: (0, 0)))
        args.append(gammas)
    return pl.pallas_call(
        body, name=name, grid=(t // tm,),
        in_specs=in_specs,
        out_specs=[row] * (1 + ng), out_shape=[SDS((t, dn), F32)] + [SDS((t, dn), BF16)] * ng,
        compiler_params=_cparams(1),
    )(*args)


def _qkv_proj(name, hn, w_qkv):
    t, dn = hn.shape
    wb = w_qkv.shape[-1]
    per = wb // 128
    tm = _tile(t)

    def body(x_ref, w_ref, o_ref):
        xv = x_ref[...]
        for j in range(NDEV):
            yv = _dot(xv, w_ref[j]).astype(BF16)
            for i in range(per):
                n = per * j + i
                o_ref[n // 8, :, (n % 8) * 128 : (n % 8 + 1) * 128] = yv[:, i * 128 : (i + 1) * 128]

    return pl.pallas_call(
        body, name=name, grid=(t // tm,),
        in_specs=[BS((tm, dn), lambda i: (i, 0)), BS((NDEV, dn, wb), lambda i: (0, 0, 0))],
        out_specs=BS((3, tm, dn), lambda i: (0, i, 0)), out_shape=SDS((3, t, dn), BF16),
        compiler_params=_cparams(1),
    )(hn, w_qkv)


def _rel_onehot(i):
    r = lax.broadcasted_iota(jnp.int32, (NREL_PAD, BAND), 0)
    j = lax.broadcasted_iota(jnp.int32, (NREL_PAD, BAND), 1)
    idx = jnp.clip(PAD + i - j, -MAX_REL, MAX_REL) + MAX_REL
    return (idx == r).astype(F32)


def _rel_bias_fwd(table):
    def body(t_ref, o_ref):
        i8 = pl.program_id(0)
        for ii in range(8):
            o_ref[:, ii, :] = _dot_exact(t_ref[...], _rel_onehot(i8 * 8 + ii))

    return pl.pallas_call(
        body, name="rel_bias_fwd", grid=(CHUNK // 8,),
        in_specs=[BS((HEADS_A, NREL_PAD), lambda i: (0, 0))],
        out_specs=BS((HEADS_A, 8, BAND), lambda i: (0, i, 0)), out_shape=SDS((HEADS_A, CHUNK, BAND), F32),
        compiler_params=_cparams(1),
    )(table)


def _rel_bias_bwd(dbias):
    def body(d_ref, o_ref):
        i8 = pl.program_id(0)
        acc = jnp.zeros((HEADS_A, NREL_PAD), F32)
        for ii in range(8):
            acc += lax.dot_general(
                d_ref[:, ii, :], _rel_onehot(i8 * 8 + ii), (((1,), (1,)), ((), ())), precision=HIGHEST,
                preferred_element_type=F32,
            )

        @pl.when(i8 == 0)
        def _():
            o_ref[...] = acc

        @pl.when(i8 > 0)
        def _():
            o_ref[...] += acc

    return pl.pallas_call(
        body, name="rel_bias_bwd", grid=(CHUNK // 8,),
        in_specs=[BS((HEADS_A, 8, BAND), lambda i: (0, i, 0))],
        out_specs=BS((HEADS_A, NREL_PAD), lambda i: (0, 0)), out_shape=SDS((HEADS_A, NREL_PAD), F32),
        compiler_params=_cparams(1),
    )(dbias)


def _window_bias(bias):
    b = bias.reshape(HEADS_A // 2, 2, CHUNK, BAND)
    per_chunk = [
        jnp.pad(b, ((0, 0), (0, 0), (0, 0), (cc * CHUNK, WINDOW - BAND - cc * CHUNK)), constant_values=NEG_INF)
        for cc in range(CHUNKS_PER_STEP)
    ]
    return jnp.stack(per_chunk, axis=1).reshape(HEADS_A // 2, STEP_ROWS, WINDOW)


def _window_bias_bwd(dwin):
    d = dwin.reshape(HEADS_A // 2, CHUNKS_PER_STEP, 2, CHUNK, WINDOW)
    return sum(d[:, cc, :, :, cc * CHUNK : cc * CHUNK + BAND] for cc in range(CHUNKS_PER_STEP)).reshape(HEADS_A, CHUNK, BAND)


def _step_rows(xs, lane):
    parts = []
    for cc in range(CHUNKS_PER_STEP):
        xc = xs[cc * CHUNK : (cc + 1) * CHUNK]
        parts.append(jnp.where(lane < 64, xc, jnp.zeros_like(xc)))
        parts.append(jnp.where(lane >= 64, xc, jnp.zeros_like(xc)))
    return jnp.concatenate(parts, axis=0)


def _pair_rows(ys, lane):
    parts = []
    for cc in range(CHUNKS_PER_STEP):
        y0 = ys[(2 * cc) * CHUNK : (2 * cc + 1) * CHUNK]
        y1 = ys[(2 * cc + 1) * CHUNK : (2 * cc + 2) * CHUNK]
        parts.append(jnp.where(lane < 64, y0, y1))
    return jnp.concatenate(parts, axis=0)


def _window_probs(q_rows, kwin, bias_win, first_key):
    s = _dot_nt(q_rows, kwin) * (CHUNK ** -0.5) + bias_win
    col = lax.broadcasted_iota(jnp.int32, s.shape, 1)
    s = jnp.where(col >= first_key, s, NEG_INF)
    e = jnp.exp(s - jnp.max(s, axis=-1, keepdims=True))
    return e / jnp.sum(e, axis=-1, keepdims=True)


def _attn_a_fwd(qkv3, bias_win, bl, seq):
    t, dn = qkv3.shape[1:]
    npair = dn // 128
    step = CHUNKS_PER_STEP * CHUNK

    def body(q_ref, k_ref, v_ref, b_ref, o_ref, kpad, vpad):
        kpad[0:PAD, :] = jnp.zeros((PAD, 128), BF16)
        vpad[0:PAD, :] = jnp.zeros((PAD, 128), BF16)
        kpad[PAD:, :] = k_ref[...]
        vpad[PAD:, :] = v_ref[...]
        lane = lax.broadcasted_iota(jnp.int32, (CHUNK, 128), 1)

        def chunks(it, carry):
            r0 = pl.multiple_of(it * step, step)
            q_rows = _step_rows(q_ref[pl.ds(r0, step), :], lane)
            p = _window_probs(q_rows, kpad[pl.ds(r0, WINDOW), :], b_ref[...], PAD - r0)
            o_rows = _dot(p.astype(BF16), vpad[pl.ds(r0, WINDOW), :])
            o_ref[pl.ds(r0, step), :] = _pair_rows(o_rows, lane).astype(BF16)
            return carry

        lax.fori_loop(0, seq // step, chunks, 0)

    return pl.pallas_call(
        body, name="attn_a_fwd", grid=(bl, npair),
        in_specs=[
            BS((None, seq, 128), lambda b, h: (0, b, h)),
            BS((None, seq, 128), lambda b, h: (1, b, h)),
            BS((None, seq, 128), lambda b, h: (2, b, h)),
            BS((None, STEP_ROWS, WINDOW), lambda b, h: (h, 0, 0)),
        ],
        out_specs=BS((seq, 128), lambda b, h: (b, h)), out_shape=SDS((t, dn), BF16),
        scratch_shapes=[pltpu.VMEM((PAD + seq, 128), BF16), pltpu.VMEM((PAD + seq, 128), BF16)],
        compiler_params=_cparams(2),
    )(qkv3, qkv3, qkv3, bias_win)


def _attn_a_bwd(qkv3, do, bias_win, bl, seq):
    t, dn = qkv3.shape[1:]
    npair = dn // 128
    step = CHUNKS_PER_STEP * CHUNK

    def body(q_ref, k_ref, v_ref, do_ref, b_ref, dqkv_ref, db_ref, kpad, vpad, dkacc, dvacc):
        b = pl.program_id(1)
        kpad[0:PAD, :] = jnp.zeros((PAD, 128), BF16)
        vpad[0:PAD, :] = jnp.zeros((PAD, 128), BF16)
        kpad[PAD:, :] = k_ref[...]
        vpad[PAD:, :] = v_ref[...]
        dkacc[...] = jnp.zeros_like(dkacc)
        dvacc[...] = jnp.zeros_like(dvacc)

        @pl.when(b == 0)
        def _():
            db_ref[...] = jnp.zeros_like(db_ref)

        lane = lax.broadcasted_iota(jnp.int32, (CHUNK, 128), 1)

        def chunks(it, carry):
            r0 = pl.multiple_of(it * step, step)
            q_rows = _step_rows(q_ref[pl.ds(r0, step), :], lane)
            do_rows = _step_rows(do_ref[pl.ds(r0, step), :], lane)
            kwin = kpad[pl.ds(r0, WINDOW), :]
            vwin = vpad[pl.ds(r0, WINDOW), :]
            p = _window_probs(q_rows, kwin, b_ref[...], PAD - r0)
            dp = _dot_nt(do_rows, vwin)
            ds = p * (dp - jnp.sum(p * dp, axis=-1, keepdims=True))
            db_ref[...] += ds
            dsb = (ds * (CHUNK ** -0.5)).astype(BF16)
            dqkv_ref[0, pl.ds(r0, step), :] = _pair_rows(_dot(dsb, kwin), lane).astype(BF16)
            dkacc[pl.ds(r0, WINDOW), :] += _dot_tn(dsb, q_rows)
            dvacc[pl.ds(r0, WINDOW), :] += _dot_tn(p.astype(BF16), do_rows)
            return carry

        lax.fori_loop(0, seq // step, chunks, 0)
        dqkv_ref[1] = dkacc[PAD:, :].astype(BF16)
        dqkv_ref[2] = dvacc[PAD:, :].astype(BF16)

    return pl.pallas_call(
        body, name="attn_a_bwd", grid=(npair, bl),
        in_specs=[
            BS((None, seq, 128), lambda h, b: (0, b, h)),
            BS((None, seq, 128), lambda h, b: (1, b, h)),
            BS((None, seq, 128), lambda h, b: (2, b, h)),
            BS((seq, 128), lambda h, b: (b, h)),
            BS((None, STEP_ROWS, WINDOW), lambda h, b: (h, 0, 0)),
        ],
        out_specs=[BS((3, seq, 128), lambda h, b: (0, b, h)), BS((None, STEP_ROWS, WINDOW), lambda h, b: (h, 0, 0))],
        out_shape=[SDS((3, t, dn), BF16), SDS((HEADS_A // 2, STEP_ROWS, WINDOW), F32)],
        scratch_shapes=[
            pltpu.VMEM((PAD + seq, 128), BF16), pltpu.VMEM((PAD + seq, 128), BF16),
            pltpu.VMEM((PAD + seq, 128), F32), pltpu.VMEM((PAD + seq, 128), F32),
        ],
        compiler_params=_cparams(2),
    )(qkv3, qkv3, qkv3, do, bias_win)


def _rope_tables(seq):
    half = ROPE // 2
    freqs = ROPE_THETA ** (-jnp.arange(half, dtype=F32) / half)
    ang = jnp.arange(seq, dtype=F32)[:, None] * freqs[None, :]
    cos, sin = jnp.cos(ang), jnp.sin(ang)
    c64 = jnp.concatenate([cos, cos], axis=1)
    s64 = jnp.concatenate([-sin, sin], axis=1)
    c192 = jnp.concatenate([jnp.ones((seq, NOPE), F32), c64], axis=1)
    s192 = jnp.concatenate([jnp.zeros((seq, NOPE), F32), s64], axis=1)
    p64 = np.zeros((ROPE, ROPE), np.float32)
    for col in range(ROPE):
        p64[(col + half) % ROPE, col] = 1.0
    p192 = np.zeros((QK_B, QK_B), np.float32)
    p192[NOPE:, NOPE:] = p64
    return c64, s64, jnp.asarray(p64), c192, s192, jnp.asarray(p192)


def _rope(xv, cos, sin_signed, swap):
    return xv * cos + _dot_exact(xv, swap) * sin_signed


def _rope_bwd(dy, cos, sin_signed, swap):
    return dy * cos + _dot_exact(dy * sin_signed, swap)


def _q_down(hn, w_dq, q_norm):
    t, dn = hn.shape
    ql = w_dq.shape[1]
    tm = _tile(t)

    def body(x_ref, w_ref, g_ref, pre_ref, cq_ref):
        pre = _dot(x_ref[...], w_ref[...])
        pre_ref[...] = pre
        cq_ref[...] = (pre * _rms_scale(pre) * g_ref[...]).astype(BF16)

    return pl.pallas_call(
        body, name="q_down", grid=(t // tm,),
        in_specs=[BS((tm, dn), lambda i: (i, 0)), BS((dn, ql), lambda i: (0, 0)), BS((1, ql), lambda i: (0, 0))],
        out_specs=[BS((tm, ql), lambda i: (i, 0))] * 2, out_shape=[SDS((t, ql), F32), SDS((t, ql), BF16)],
        compiler_params=_cparams(1),
    )(hn, w_dq, q_norm)


def _q_up(cq, w_uq, c192, s192, p192, seq):
    t, ql = cq.shape
    tm = _tile(min(seq, 512), min(seq, 512))
    nseq = seq // tm

    def body(x_ref, w_ref, c_ref, s_ref, p_ref, o_ref):
        qf = _dot(x_ref[...], w_ref[...])
        o_ref[...] = _rope(qf, c_ref[...], s_ref[...], p_ref[...]).astype(BF16)

    pos = BS((tm, QK_B), lambda h, i: (i % nseq, 0))
    return pl.pallas_call(
        body, name="q_up", grid=(HEADS_B, t // tm),
        in_specs=[
            BS((tm, ql), lambda h, i: (i, 0)), BS((None, ql, QK_B), lambda h, i: (h, 0, 0)), pos, pos,
            BS((QK_B, QK_B), lambda h, i: (0, 0)),
        ],
        out_specs=BS((None, tm, QK_B), lambda h, i: (h, i, 0)), out_shape=SDS((HEADS_B, t, QK_B), BF16),
        compiler_params=_cparams(2),
    )(cq, w_uq, c192, s192, p192)


def _kv_down(hk, w_down, latent_norm, c64, s64, p64, seq):
    t, dn = hk.shape
    wd = w_down.shape[1]
    tm = _tile(min(seq, 512), min(seq, 512))
    nseq = seq // tm

    def body(x_ref, w_ref, g_ref, c_ref, s_ref, p_ref, ckr_ref, ckv_ref, kr_ref):
        ckr = _dot(x_ref[...], w_ref[...])
        ckr_ref[...] = ckr
        lat = ckr[:, :KV_LORA]
        ckv_ref[...] = (lat * _rms_scale(lat) * g_ref[...]).astype(BF16)
        kr_ref[...] = _rope(ckr[:, KV_LORA:], c_ref[...], s_ref[...], p_ref[...]).astype(BF16)

    pos = BS((tm, ROPE), lambda i: (i % nseq, 0))
    return pl.pallas_call(
        body, name="kv_down", grid=(t // tm,),
        in_specs=[
            BS((tm, dn), lambda i: (i, 0)), BS((dn, wd), lambda i: (0, 0)), BS((1, KV_LORA), lambda i: (0, 0)), pos, pos,
            BS((ROPE, ROPE), lambda i: (0, 0)),
        ],
        out_specs=[BS((tm, wd), lambda i: (i, 0)), BS((tm, KV_LORA), lambda i: (i, 0)), BS((tm, ROPE), lambda i: (i, 0))],
        out_shape=[SDS((t, wd), F32), SDS((t, KV_LORA), BF16), SDS((t, ROPE), BF16)],
        compiler_params=_cparams(1),
    )(hk, w_down, latent_norm, c64, s64, p64)


def _kv_up(ckv, w_up):
    t, kl = ckv.shape
    hb = w_up.shape[-1]
    tm = _tile(t)

    def body(x_ref, w_ref, o_ref):
        xv = x_ref[...]
        for h in range(HEADS_B):
            o_ref[:, h * hb : (h + 1) * hb] = _dot(xv, w_ref[h]).astype(BF16)

    return pl.pallas_call(
        body, name="kv_up", grid=(t // tm,),
        in_specs=[BS((tm, kl), lambda i: (i, 0)), BS((HEADS_B, kl, hb), lambda i: (0, 0, 0))],
        out_specs=BS((tm, HEADS_B * hb), lambda i: (i, 0)), out_shape=SDS((t, HEADS_B * hb), BF16),
        compiler_params=_cparams(1),
    )(ckv, w_up)


def _mla_probs(qi, kcat, row0, n_keys):
    s = _dot_nt(qi, kcat) * (QK_B ** -0.5)
    rows = lax.broadcasted_iota(jnp.int32, (qi.shape[0], n_keys), 0) + row0
    cols = lax.broadcasted_iota(jnp.int32, (qi.shape[0], n_keys), 1)
    s = jnp.where(jnp.right_shift(cols, 6) <= jnp.right_shift(rows, 6), s, NEG_INF)
    e = jnp.exp(s - jnp.max(s, axis=-1, keepdims=True))
    return e / jnp.sum(e, axis=-1, keepdims=True)


def _mla_fwd(q, kv, kr, bl, seq):
    t = kv.shape[0]
    tq = min(MLA_TQ, seq)

    def body(q_ref, kn_ref, v_ref, kr_ref, o_ref):
        kcat = jnp.concatenate([kn_ref[...], kr_ref[...]], axis=1)
        vv = v_ref[...]
        for i in range(seq // tq):
            n_keys = (i + 1) * tq
            p = _mla_probs(q_ref[i * tq : (i + 1) * tq, :], kcat[:n_keys], i * tq, n_keys)
            o_ref[i * tq : (i + 1) * tq, :] = _dot(p.astype(BF16), vv[:n_keys]).astype(BF16)

    return pl.pallas_call(
        body, name="mla_fwd", grid=(bl, HEADS_B),
        in_specs=[
            BS((None, seq, QK_B), lambda b, h: (h, b, 0)),
            BS((seq, NOPE), lambda b, h: (b, 2 * h)),
            BS((seq, V_DIM), lambda b, h: (b, 2 * h + 1)),
            BS((seq, ROPE), lambda b, h: (b, 0)),
        ],
        out_specs=BS((seq, V_DIM), lambda b, h: (b, h)), out_shape=SDS((t, HEADS_B * V_DIM), BF16),
        compiler_params=_cparams(2),
    )(q, kv, kv, kr)


def _mla_bwd(q, kv, kr, do, c192, s192, p192, bl, seq):
    t = kv.shape[0]
    tq = min(MLA_TQ, seq)

    def body(q_ref, kn_ref, v_ref, kr_ref, do_ref, c_ref, s_ref, p_ref, dq_ref, dkv_ref, dkr_ref, dkacc, dvacc):
        h = pl.program_id(1)
        kcat = jnp.concatenate([kn_ref[...], kr_ref[...]], axis=1)
        vv = v_ref[...]
        dkacc[...] = jnp.zeros_like(dkacc)
        dvacc[...] = jnp.zeros_like(dvacc)
        for i in range(seq // tq):
            n_keys = (i + 1) * tq
            rows = slice(i * tq, (i + 1) * tq)
            qi = q_ref[rows, :]
            doi = do_ref[rows, :]
            p = _mla_probs(qi, kcat[:n_keys], i * tq, n_keys)
            dp = _dot_nt(doi, vv[:n_keys])
            ds = p * (dp - jnp.sum(p * dp, axis=-1, keepdims=True))
            dsb = (ds * (QK_B ** -0.5)).astype(BF16)
            dq = _dot(dsb, kcat[:n_keys])
            dq_ref[rows, :] = _rope_bwd(dq, c_ref[rows, :], s_ref[rows, :], p_ref[...]).astype(BF16)
            dkacc[0:n_keys, :] += _dot_tn(dsb, qi)
            dvacc[0:n_keys, :] += _dot_tn(p.astype(BF16), doi)
        dk = dkacc[...]
        dkv_ref[:, :NOPE] = dk[:, :NOPE].astype(BF16)
        dkv_ref[:, NOPE:] = dvacc[...].astype(BF16)

        @pl.when(h == 0)
        def _():
            dkr_ref[...] = dk[:, NOPE:]

        @pl.when(h > 0)
        def _():
            dkr_ref[...] += dk[:, NOPE:]

    return pl.pallas_call(
        body, name="mla_bwd", grid=(bl, HEADS_B),
        in_specs=[
            BS((None, seq, QK_B), lambda b, h: (h, b, 0)),
            BS((seq, NOPE), lambda b, h: (b, 2 * h)),
            BS((seq, V_DIM), lambda b, h: (b, 2 * h + 1)),
            BS((seq, ROPE), lambda b, h: (b, 0)),
            BS((seq, V_DIM), lambda b, h: (b, h)),
            BS((seq, QK_B), lambda b, h: (0, 0)),
            BS((seq, QK_B), lambda b, h: (0, 0)),
            BS((QK_B, QK_B), lambda b, h: (0, 0)),
        ],
        out_specs=[
            BS((None, seq, QK_B), lambda b, h: (h, b, 0)),
            BS((seq, NOPE + V_DIM), lambda b, h: (b, h)),
            BS((seq, ROPE), lambda b, h: (b, 0)),
        ],
        out_shape=[SDS((HEADS_B, t, QK_B), BF16), SDS((t, HEADS_B * (NOPE + V_DIM)), BF16), SDS((t, ROPE), F32)],
        scratch_shapes=[pltpu.VMEM((seq, QK_B), F32), pltpu.VMEM((seq, V_DIM), F32)],
        compiler_params=_cparams(2),
    )(q, kv, kv, kr, do, c192, s192, p192)


def _loss_final(h, target, gamma):
    t, dn = h.shape
    tm = _tile(t)
    nt = t // tm

    def body(h_ref, t_ref, g_ref, dh_ref, dhb_ref, dg_ref, loss_ref):
        i = pl.program_id(0)
        hv = h_ref[...]
        r = _rms_scale(hv)
        hh = hv * r
        gam = g_ref[...]
        err = hh * gam - t_ref[...]
        part = 0.5 * jnp.sum(jnp.mean(err * err, axis=-1, keepdims=True))

        @pl.when(i == 0)
        def _():
            loss_ref[...] = jnp.zeros_like(loss_ref)

        loss_ref[...] += part
        dy = err * (1.0 / dn)
        _acc_rows(dg_ref, dy * hh, i, nt)
        t1 = dy * gam
        dh = r * (t1 - hh * jnp.mean(t1 * hh, axis=-1, keepdims=True))
        dh_ref[...] = dh
        dhb_ref[...] = dh.astype(BF16)

    row = BS((tm, dn), lambda i: (i, 0))
    return pl.pallas_call(
        body, name="loss_final", grid=(nt,),
        in_specs=[row, row, BS((1, dn), lambda i: (0, 0))],
        out_specs=[row, row, BS((8, dn), lambda i: (0, 0)), BS((8, 128), lambda i: (0, 0))],
        out_shape=[SDS((t, dn), F32), SDS((t, dn), BF16), SDS((8, dn), F32), SDS((8, 128), F32)],
        compiler_params=_cparams(1),
    )(h, target, gamma)


def _ffn_bwd_in(name, dh, w_out, layer, gu, dep=None):
    t, dn = dh.shape
    tm = _tile(t, 1024)

    def body(dh_ref, w_ref, gu_ref, *rest):
        o_ref = rest[-1]
        da = 0.5 * _dot_nt(dh_ref[...], w_ref[...])
        g = gu_ref[0].astype(F32)
        u = gu_ref[1].astype(F32)
        sg = jax.nn.sigmoid(g)
        o_ref[0] = (da * u * (sg * (1.0 + g * (1.0 - sg)))).astype(BF16)
        o_ref[1] = (da * (g * sg)).astype(BF16)

    blk = BS((None, 2, tm, FB), lambda j, i: (j, 0, i, 0))
    deps = [] if dep is None else [dep]
    return pl.pallas_call(
        body, name=name, grid=(NJ, t // tm),
        in_specs=[BS((tm, dn), lambda j, i: (i, 0)), BS((None, None, FB, dn), lambda j, i: (layer, j, 0, 0)), blk]
        + [_dep_spec(2)] * len(deps),
        out_specs=blk, out_shape=SDS((NJ, 2, t, FB), BF16),
        compiler_params=_cparams(2),
    )(dh, w_out, gu, *deps)


def _mm_nt_plain(name, xf, w, dep=None):
    t, dn = xf.shape
    n = w.shape[0]
    tm = _tile(t)

    def body(x_ref, w_ref, *rest):
        rest[-1][...] = _dot_nt(x_ref[...], w_ref[...]).astype(BF16)

    deps = [] if dep is None else [dep]
    return pl.pallas_call(
        body, name=name, grid=(t // tm,),
        in_specs=[BS((tm, dn), lambda i: (i, 0)), BS((n, dn), lambda i: (0, 0))] + [_dep_spec(1)] * len(deps),
        out_specs=BS((tm, n), lambda i: (i, 0)), out_shape=SDS((t, n), BF16),
        compiler_params=_cparams(1),
    )(xf, w, *deps)


def _mm_tn(name, xa, x_spec, ya, y_spec, out_shape, out_spec, nj, scale=None):
    def body(x_ref, y_ref, o_ref):
        acc = _dot_tn(x_ref[...], y_ref[...])
        o_ref[...] = (acc if scale is None else scale * acc).astype(BF16)

    return pl.pallas_call(
        body, name=name, grid=(nj,),
        in_specs=[x_spec, y_spec], out_specs=out_spec, out_shape=SDS(out_shape, BF16),
        compiler_params=_cparams(1),
    )(xa, ya)


def _dw_qkv(hn, dqkv3, wb):
    t, dn = hn.shape
    per = wb // 128

    def body(x_ref, *refs):
        cols = [y_ref[...] for y_ref in refs[:per]]
        refs[per][...] = _dot_tn(x_ref[...], jnp.concatenate(cols, axis=1)).astype(BF16)

    def piece(k):
        return BS((None, t, 128), lambda j: ((per * j + k) // 8, 0, (per * j + k) % 8))

    return pl.pallas_call(
        body, name="dw_qkv", grid=(NDEV,),
        in_specs=[BS((t, dn), lambda j: (0, 0))] + [piece(k) for k in range(per)],
        out_specs=BS((None, dn, wb), lambda j: (j, 0, 0)), out_shape=SDS((NDEV, dn, wb), BF16),
        compiler_params=_cparams(1),
    )(hn, *([dqkv3] * per))


def _mm_nt_epi(name, ya, y_spec, wa, w_spec, nj, n_out, extra, out_shapes, out_specs, epilogue, tm, nt, mm_fn=None):
    n_extra = len(extra)
    n_outs = len(out_shapes)

    def body(*refs):
        y_ref, w_ref = refs[:2]
        ex = refs[2 : 2 + n_extra]
        outs = refs[2 + n_extra : 2 + n_extra + n_outs]
        i = pl.program_id(0)
        j = pl.program_id(1)
        part = _dot_nt(y_ref[...], w_ref[...]) if mm_fn is None else mm_fn(y_ref, w_ref)
        if nj == 1:
            epilogue(part, ex, outs, i, nt)
            return
        acc = refs[-1]

        @pl.when(j == 0)
        def _():
            acc[...] = part

        @pl.when(j > 0)
        def _():
            acc[...] += part

        @pl.when(j == nj - 1)
        def _():
            epilogue(acc[...], ex, outs, i, nt)

    return pl.pallas_call(
        body, name=name, grid=(nt, nj),
        in_specs=[y_spec, w_spec] + [spec for _, spec in extra],
        out_specs=out_specs, out_shape=out_shapes,
        scratch_shapes=[] if nj == 1 else [pltpu.VMEM((tm, n_out), F32)],
        compiler_params=_cparams(2),
    )(ya, wa, *[arr for arr, _ in extra])


def _norm_bwd(dn, hv, gam):
    r = _rms_scale(hv)
    hh = hv * r
    t1 = dn * gam
    return r * (t1 - hh * jnp.mean(t1 * hh, axis=-1, keepdims=True)), dn * hh


def _norm_bwd_epilogue(has_res, out_dtype):
    def epilogue(dn, ex, outs, i, nt):
        dh, dg_rows = _norm_bwd(dn, ex[0][...], ex[1][...])
        _acc_rows(outs[1], dg_rows, i, nt)
        if has_res:
            dh = dh + ex[2][...]
        outs[0][...] = dh.astype(out_dtype)
        if has_res:
            outs[2][...] = dh.astype(BF16)

    return epilogue


def _mm_nt_norm_bwd(name, ya, y_spec, wa, w_spec, nj, h, gamma, res, out_dtype, mm_fn=None, want_tm=512, dep=None):
    t, n = h.shape
    tm = _tile(t, want_tm)
    nt = t // tm
    row = BS((tm, n), lambda i, j: (i, 0))
    extra = [(h, row), (gamma, BS((1, n), lambda i, j: (0, 0)))]
    out_shapes = [SDS((t, n), out_dtype), SDS((8, n), F32)]
    out_specs = [row, BS((8, n), lambda i, j: (0, 0))]
    if res is not None:
        extra.append((res, row))
        out_shapes.append(SDS((t, n), BF16))
        out_specs.append(row)
    if dep is not None:
        extra.append((dep, _dep_spec(2)))
    return _mm_nt_epi(
        name, ya, y_spec, wa, w_spec, nj, n, extra, out_shapes, out_specs, _norm_bwd_epilogue(res is not None, out_dtype), tm, nt, mm_fn,
    )


def _dev_block(jj):
    return jj // 2 + NJ * (jj % 2)


def _ffn_dn_mm(y_ref, w_ref):
    acc = None
    for jj in range(2 * NJ):
        part = _dot_nt(y_ref[jj], w_ref[_dev_block(jj)])
        acc = part if acc is None else acc + part
    return acc


def _ffn_bwd(tag, dh, dhb, n_in, h_in, gamma, gu, a, w_in, w_out, more_grads, collective_id):
    t, dn = dh.shape
    dgu = _ffn_bwd_in(f"{tag}_bwd_in", dhb, w_out, 0, gu).reshape(2 * NJ, t, FB)
    dw_out = _mm_tn(
        f"{tag}_dw_out", a, BS((None, t, FB), lambda j: (j, 0, 0)), dhb, BS((t, dn), lambda j: (0, 0)),
        (NJ, FB, dn), BS((None, FB, dn), lambda j: (j, 0, 0)), NJ, scale=0.5,
    )
    dw_in = _mm_tn(
        f"{tag}_dw_in", n_in, BS((t, dn), lambda j: (0, 0)), dgu, BS((None, t, FB), lambda j: (j, 0, 0)),
        (NDEV, dn, FB), BS((None, dn, FB), lambda j: (_dev_block(j), 0, 0)), NDEV,
    )
    entries = [("scatter", dw_in), ("scatter", dw_out.reshape(NDEV, NJ * FB // NDEV, dn))] + [("scatter", g) for g in more_grads]
    landed = _exchange_sc(f"{tag}_reduce", entries, collective_id)
    tm = _tile(t, 256)
    dh_in, dgam, dhb_in = _mm_nt_norm_bwd(
        f"{tag}_dn", dgu, BS((2 * NJ, tm, FB), lambda i, j: (0, i, 0)),
        w_in, BS((None, NDEV, dn, FB), lambda i, j: (0, 0, 0, 0)), 1, h_in, gamma, dh, F32, mm_fn=_ffn_dn_mm, want_tm=256,
    )
    return dh_in, dhb_in, dgam, landed


def _heads_mm(y_ref, w_ref):
    acc = None
    for h in range(HEADS_B):
        part = _dot_nt(y_ref[h], w_ref[h])
        acc = part if acc is None else acc + part
    return acc


def _dqkv_mm(per):
    def mm(y_ref, w_ref):
        acc = None
        for j in range(NDEV):
            cols = [y_ref[(per * j + k) // 8, :, ((per * j + k) % 8) * 128 : ((per * j + k) % 8 + 1) * 128] for k in range(per)]
            part = _dot_nt(jnp.concatenate(cols, axis=1), w_ref[j])
            acc = part if acc is None else acc + part
        return acc

    return mm


def _kv_latent_bwd(dkv, w_up, ckr, latent_norm, dkr, c64, s64, p64, seq):
    t, wd = ckr.shape
    hb = w_up.shape[-1]
    tm = _tile(min(seq, 512), min(seq, 512))
    nt = t // tm
    nseq = seq // tm

    def epilogue(dn, ex, outs, i, nt_):
        dlat, dg_rows = _norm_bwd(dn, ex[0][...], ex[1][...])
        _acc_rows(outs[1], dg_rows, i, nt_)
        outs[0][:, :KV_LORA] = dlat.astype(BF16)
        outs[0][:, KV_LORA:] = _rope_bwd(ex[2][...], ex[3][...], ex[4][...], ex[5][...]).astype(BF16)

    pos = BS((tm, ROPE), lambda i, j: (i % nseq, 0))
    extra = [
        (ckr, BS((tm, KV_LORA), lambda i, j: (i, 0))), (latent_norm, BS((1, KV_LORA), lambda i, j: (0, 0))),
        (dkr, BS((tm, ROPE), lambda i, j: (i, 0))), (c64, pos), (s64, pos), (p64, BS((ROPE, ROPE), lambda i, j: (0, 0))),
    ]
    def heads_mm(y_ref, w_ref):
        acc = None
        for h in range(HEADS_B):
            part = _dot_nt(y_ref[:, h * hb : (h + 1) * hb], w_ref[h])
            acc = part if acc is None else acc + part
        return acc

    return _mm_nt_epi(
        "kv_latent_bwd", dkv, BS((tm, HEADS_B * hb), lambda i, j: (i, 0)), w_up, BS((HEADS_B, KV_LORA, hb), lambda i, j: (0, 0, 0)),
        1, KV_LORA, extra, [SDS((t, wd), BF16), SDS((8, KV_LORA), F32)],
        [BS((tm, wd), lambda i, j: (i, 0)), BS((8, KV_LORA), lambda i, j: (0, 0))], epilogue, tm, nt, heads_mm,
    )


def _adamw(name, parts, w, m, v):
    n_layers = len(parts)
    rows, cols = w.shape[0] // n_layers, w.shape[1]
    tr = max(d for d in range(8, min(rows, 256) + 1, 8) if rows % d == 0)
    nb = rows // tr

    def body(*refs):
        p_refs = refs[:n_layers]
        w_ref, m_ref, v_ref, g_ref, d_ref, nm_ref, nv_ref = refs[n_layers : n_layers + 7]
        layer = pl.program_id(0)
        for lp in range(n_layers):

            @pl.when(layer == lp)
            def _():
                g = p_refs[lp][0].astype(F32)
                for k in range(1, NDEV):
                    g = g + p_refs[lp][k].astype(F32)
                g_ref[...] = g

        g = g_ref[...]
        nm = ADAM_B1 * m_ref[...] + (1.0 - ADAM_B1) * g
        nv = ADAM_B2 * v_ref[...] + (1.0 - ADAM_B2) * (g * g)
        nm_ref[...] = nm
        nv_ref[...] = nv
        m_hat = nm / (1.0 - ADAM_B1 ** ADAM_STEP)
        v_hat = nv / (1.0 - ADAM_B2 ** ADAM_STEP)
        d_ref[...] = -ADAM_LR * (m_hat / (jnp.sqrt(v_hat) + ADAM_EPS) + ADAM_WD * w_ref[...])

    def part_spec(lp):
        return BS((NDEV, tr, cols), lambda l, i: (0, jnp.where(l == lp, i, jnp.where(l < lp, 0, nb - 1)), 0))

    row = BS((tr, cols), lambda l, i: (l * nb + i, 0))
    return pl.pallas_call(
        body, name=name, grid=(n_layers, nb),
        in_specs=[part_spec(lp) for lp in range(n_layers)] + [row, row, row],
        out_specs=[row] * 4, out_shape=[SDS(w.shape, F32)] * 4,
        compiler_params=_cparams(2),
    )(*parts, w, m, v)


def _pack_small(ffn1_norm, mix_norm, ffn2_norm, kv_norm, final_norm, q_norm, latent_norm, rel_bias, last_row):
    dn = ffn1_norm.shape[-1]

    def rows_of(a, n_rows):
        flat = a.reshape(-1)
        return jnp.pad(flat, (0, n_rows * dn - flat.shape[0])).reshape(n_rows, dn)

    return jnp.concatenate(
        [
            ffn1_norm.reshape(2, dn), mix_norm.reshape(2, dn), ffn2_norm.reshape(2, dn), kv_norm.reshape(1, dn),
            final_norm.reshape(1, dn), rows_of(q_norm, 1), rows_of(latent_norm, 1), rows_of(rel_bias, 5), rows_of(last_row, 1),
        ],
        axis=0,
    )


def _unpack_small(pack):
    dn = pack.shape[-1]
    return dict(
        ffn1_norm=pack[0:2], mix_norm=pack[2:4], ffn2_norm=pack[4:6], kv_norm=pack[6], final_norm=pack[7],
        b_q_norm=pack[8, :Q_LORA].reshape(1, Q_LORA), kv_latent_norm=pack[9, :KV_LORA],
        a_rel_bias=pack[10:15].reshape(-1)[: HEADS_A * NREL].reshape(1, HEADS_A, NREL), last=pack[15],
    )


def kernel(x, ffn1_norm, ffn1_w_in, ffn1_w_out, mix_norm, ffn2_norm, ffn2_w_in, ffn2_w_out, a_w_qkv, a_rel_bias, a_w_o, kv_norm, kv_w_down, kv_latent_norm, kv_w_up, b_w_dq, b_q_norm, b_w_uq, b_w_o, final_norm, loss_target, m_ffn1_norm, m_ffn1_w_in, m_ffn1_w_out, m_mix_norm, m_ffn2_norm, m_ffn2_w_in, m_ffn2_w_out, m_a_w_qkv, m_a_rel_bias, m_a_w_o, m_kv_norm, m_kv_w_down, m_kv_latent_norm, m_kv_w_up, m_b_w_dq, m_b_q_norm, m_b_w_uq, m_b_w_o, m_final_norm, v_ffn1_norm, v_ffn1_w_in, v_ffn1_w_out, v_mix_norm, v_ffn2_norm, v_ffn2_w_in, v_ffn2_w_out, v_a_w_qkv, v_a_rel_bias, v_a_w_o, v_kv_norm, v_kv_w_down, v_kv_latent_norm, v_kv_w_up, v_b_w_dq, v_b_q_norm, v_b_w_uq, v_b_w_o, v_final_norm):
    bl, seq, dn = x.shape
    t = bl * seq
    tm = _tile(t)
    nt = t // tm
    x2 = x.reshape(t, dn)
    target2 = loss_target.reshape(t, dn)

    def gathered(*ws):
        return [("gather", w.astype(BF16)) for w in ws]

    groups = [
        gathered(ffn1_w_in[0]), gathered(ffn1_w_out[0]), gathered(a_w_qkv[0], a_w_o[0]), gathered(ffn2_w_in[0], ffn2_w_out[0]),
        gathered(kv_w_down, kv_w_up), gathered(ffn1_w_in[1], ffn1_w_out[1]), gathered(b_w_dq[0], b_w_uq[0], b_w_o[0]),
        gathered(ffn2_w_in[1], ffn2_w_out[1]),
    ]
    ag = [_exchange_sc(f"gather_{k}", group, GATHER_IDS[k]) for k, group in enumerate(groups)]

    def as_w_in(w):
        return w.reshape(1, NDEV, dn, FB)

    def as_w_out(w):
        return w.reshape(1, NJ, FB, dn)

    c64, s64, p64, c192, s192, p192 = _rope_tables(seq)
    q_norm = b_q_norm.reshape(1, Q_LORA)
    latent_norm = kv_latent_norm.reshape(1, KV_LORA)
    bias = _window_bias(_rel_bias_fwd(jnp.pad(a_rel_bias[0], ((0, 0), (0, NREL_PAD - NREL)))))

    h0, h1, h2, n1, hn, n2, gu1, gu2, a1, a2, w_in1, w_in2, w_out1, w_out2 = ([None, None] for _ in range(14))
    h0[0] = x2
    (n1[0],) = _norm_fwd("norm_x", x2, ffn1_norm[0:1])
    w_in1[0] = as_w_in(ag[0][0])
    gu1[0], a1[0] = _ffn_in("ffn1_in_0", n1[0], w_in1[0], 0)
    w_out1[0] = as_w_out(ag[1][0])
    h1[0], hn[0] = _mm_res_norm("ffn1_out_0", a1[0], w_out1[0], 0, h0[0], mix_norm[0:1], 0.5)
    w_qkv, w_o_a = ag[2]
    qkv_wb = w_qkv.shape[-1]
    w_o_a = w_o_a.reshape(1, 1, dn, dn)
    qkv3 = _qkv_proj("qkv_proj", hn[0], w_qkv)
    o_a = _attn_a_fwd(qkv3, bias, bl, seq)
    h2[0], n2[0] = _mm_res_norm("attn_a_out", o_a.reshape(1, t, dn), w_o_a, 0, h1[0], ffn2_norm[0:1], 1.0)
    w_in2[0], w_out2[0] = as_w_in(ag[3][0]), as_w_out(ag[3][1])
    gu2[0], a2[0] = _ffn_in("ffn2_in_0", n2[0], w_in2[0], 0)
    h0[1], hk, n1[1] = _mm_res_norm(
        "ffn2_out_0", a2[0], w_out2[0], 0, h2[0], jnp.concatenate([kv_norm.reshape(1, dn), ffn1_norm[1:2]], axis=0), 0.5
    )
    w_down, w_up = ag[4]
    w_down = w_down.reshape(dn, KV_LORA + ROPE)
    ckr, ckv, kr = _kv_down(hk, w_down, latent_norm, c64, s64, p64, seq)
    kv = _kv_up(ckv, w_up)
    w_in1[1], w_out1[1] = as_w_in(ag[5][0]), as_w_out(ag[5][1])
    gu1[1], a1[1] = _ffn_in("ffn1_in_1", n1[1], w_in1[1], 0)
    h1[1], hn[1] = _mm_res_norm("ffn1_out_1", a1[1], w_out1[1], 0, h0[1], mix_norm[1:2], 0.5)
    w_dq, w_uq, w_o_b = ag[6]
    w_dq = w_dq.reshape(dn, Q_LORA)
    w_o_b = w_o_b.reshape(1, 1, dn, dn)
    cq_pre, cq = _q_down(hn[1], w_dq, q_norm)
    q = _q_up(cq, w_uq, c192, s192, p192, seq)
    o_b = _mla_fwd(q, kv, kr, bl, seq)
    h2[1], n2[1] = _mm_res_norm("attn_b_out", o_b.reshape(1, t, dn), w_o_b, 0, h1[1], ffn2_norm[1:2], 1.0)
    w_in2[1], w_out2[1] = as_w_in(ag[7][0]), as_w_out(ag[7][1])
    gu2[1], a2[1] = _ffn_in("ffn2_in_1", n2[1], w_in2[1], 0)
    (h_last,) = _mm_res_norm("ffn2_out_1", a2[1], w_out2[1], 0, h2[1], None, 0.5)
    dh, dhb, dg_final, loss_part = _loss_final(h_last, target2, final_norm.reshape(1, dn))

    dg_ffn1, dg_mix, dg_ffn2, rs_ffn1, rs_ffn2 = ([None, None] for _ in range(5))

    def whole(rows, cols):
        return BS((rows, cols), lambda j: (0, 0))

    def dw_rows(name, xa, ya):
        n = ya.shape[1]
        return _mm_tn(name, xa, whole(t, dn), ya, whole(t, n), (dn, n), whole(dn, n), 1).reshape(NDEV, dn // NDEV, n)

    dh, dhb, dg_ffn2[1], rs_ffn2[1] = _ffn_bwd(
        "ffn2_1", dh, dhb, n2[1], h2[1], ffn2_norm[1:2], gu2[1], a2[1], w_in2[1], w_out2[1], [], REDUCE_IDS[0]
    )
    do_b = _mm_nt_plain("attn_b_do", dhb, w_o_b.reshape(dn, dn))
    dw_o_b = dw_rows("attn_b_dwo", o_b, dhb)
    dq_pre, dkv, dkr = _mla_bwd(q, kv, kr, do_b, c192, s192, p192, bl, seq)
    dw_uq = _mm_tn(
        "dw_uq", cq, whole(t, Q_LORA), dq_pre, BS((None, t, QK_B), lambda j: (j, 0, 0)),
        (HEADS_B, Q_LORA, QK_B), BS((None, Q_LORA, QK_B), lambda j: (j, 0, 0)), HEADS_B,
    )
    dcq_pre, dg_q = _mm_nt_norm_bwd(
        "dcq", dq_pre, BS((HEADS_B, tm, QK_B), lambda i, j: (0, i, 0)), w_uq, BS((HEADS_B, Q_LORA, QK_B), lambda i, j: (0, 0, 0)),
        1, cq_pre, q_norm, None, BF16, mm_fn=_heads_mm,
    )
    dw_dq = dw_rows("dw_dq", hn[1], dcq_pre)
    dh, dg_mix[1], dhb = _mm_nt_norm_bwd(
        "dhn_b", dcq_pre, BS((tm, Q_LORA), lambda i, j: (i, 0)), w_dq, BS((dn, Q_LORA), lambda i, j: (0, 0)),
        1, h1[1], mix_norm[1:2], dh, F32,
    )
    dh, dhb, dg_ffn1[1], rs_ffn1[1] = _ffn_bwd(
        "ffn1_1", dh, dhb, n1[1], h0[1], ffn1_norm[1:2], gu1[1], a1[1], w_in1[1], w_out1[1], [dw_o_b, dw_uq, dw_dq], REDUCE_IDS[1]
    )
    dw_up = _mm_tn(
        "dw_up", ckv, whole(t, KV_LORA), dkv, BS((t, NOPE + V_DIM), lambda j: (0, j)),
        (HEADS_B, KV_LORA, NOPE + V_DIM), BS((None, KV_LORA, NOPE + V_DIM), lambda j: (j, 0, 0)), HEADS_B,
    )
    dckr, dg_latent = _kv_latent_bwd(dkv, w_up, ckr, latent_norm, dkr, c64, s64, p64, seq)
    dw_down = dw_rows("dw_down", hk, dckr)
    dh, dg_kv, dhb = _mm_nt_norm_bwd(
        "dhk", dckr, BS((tm, KV_LORA + ROPE), lambda i, j: (i, 0)), w_down, BS((dn, KV_LORA + ROPE), lambda i, j: (0, 0)),
        1, h0[1], kv_norm.reshape(1, dn), dh, F32,
    )
    dh, dhb, dg_ffn2[0], rs_ffn2[0] = _ffn_bwd(
        "ffn2_0", dh, dhb, n2[0], h2[0], ffn2_norm[0:1], gu2[0], a2[0], w_in2[0], w_out2[0], [dw_up, dw_down], REDUCE_IDS[2]
    )
    do_a = _mm_nt_plain("attn_a_do", dhb, w_o_a.reshape(dn, dn))
    dw_o_a = dw_rows("attn_a_dwo", o_a, dhb)
    dqkv3, dbias = _attn_a_bwd(qkv3, do_a, bias, bl, seq)
    dw_qkv = _dw_qkv(hn[0], dqkv3, qkv_wb)
    dh, dg_mix[0], dhb = _mm_nt_norm_bwd(
        "dhn_a", dqkv3, BS((3, tm, dn), lambda i, j: (0, i, 0)), w_qkv, BS((NDEV, dn, qkv_wb), lambda i, j: (0, 0, 0)),
        1, h1[0], mix_norm[0:1], dh, F32, mm_fn=_dqkv_mm(qkv_wb // 128),
    )
    r_o_a, r_qkv = _exchange_sc("attn_a_reduce", [("scatter", dw_o_a), ("scatter", dw_qkv)], REDUCE_IDS[3])
    dh, dhb, dg_ffn1[0], rs_ffn1[0] = _ffn_bwd(
        "ffn1_0", dh, dhb, n1[0], h0[0], ffn1_norm[0:1], gu1[0], a1[0], w_in1[0], w_out1[0], [], REDUCE_IDS[4]
    )
    grad_x = dh.reshape(bl, seq, dn)
    dtable = _rel_bias_bwd(_window_bias_bwd(dbias))[:, :NREL]

    small = _pack_small(
        jnp.stack([dg_ffn1[0][0], dg_ffn1[1][0]]), jnp.stack([dg_mix[0][0], dg_mix[1][0]]), jnp.stack([dg_ffn2[0][0], dg_ffn2[1][0]]),
        dg_kv[0], dg_final[0], dg_q[0], dg_latent[0], dtable, loss_part[0],
    )
    (r_small,) = _exchange("gather_small_grads", [("gather", small)])

    def update(name, parts, w, m, v):
        n_layers = len(parts)
        rows = int(np.prod(w.shape[:-1]))
        cols = w.shape[-1]
        parts = [p.reshape(NDEV, rows // n_layers, cols) for p in parts]
        outs = _adamw(name, parts, w.reshape(rows, cols), m.reshape(rows, cols), v.reshape(rows, cols))
        return [o.reshape(w.shape) for o in outs]

    res = {}
    r_in2_1, r_out2_1 = rs_ffn2[1]
    r_in1_1, r_out1_1, r_o_b, r_uq, r_dq = rs_ffn1[1]
    r_in2_0, r_out2_0, r_up, r_down = rs_ffn2[0]
    r_in1_0, r_out1_0 = rs_ffn1[0]
    res["ffn2_w_in"] = update("adamw_ffn2_w_in", [r_in2_0, r_in2_1], ffn2_w_in, m_ffn2_w_in, v_ffn2_w_in)
    res["ffn2_w_out"] = update("adamw_ffn2_w_out", [r_out2_0, r_out2_1], ffn2_w_out, m_ffn2_w_out, v_ffn2_w_out)
    res["kv_w_down"] = update("adamw_kv_w_down", [r_down], kv_w_down, m_kv_w_down, v_kv_w_down)
    res["kv_w_up"] = update("adamw_kv_w_up", [r_up], kv_w_up, m_kv_w_up, v_kv_w_up)
    res["b_w_dq"] = update("adamw_b_w_dq", [r_dq], b_w_dq, m_b_w_dq, v_b_w_dq)
    res["b_w_uq"] = update("adamw_b_w_uq", [r_uq], b_w_uq, m_b_w_uq, v_b_w_uq)
    res["b_w_o"] = update("adamw_b_w_o", [r_o_b], b_w_o, m_b_w_o, v_b_w_o)
    res["a_w_qkv"] = update("adamw_a_w_qkv", [r_qkv], a_w_qkv, m_a_w_qkv, v_a_w_qkv)
    res["a_w_o"] = update("adamw_a_w_o", [r_o_a], a_w_o, m_a_w_o, v_a_w_o)
    res["ffn1_w_in"] = update("adamw_ffn1_w_in", [r_in1_0, r_in1_1], ffn1_w_in, m_ffn1_w_in, v_ffn1_w_in)
    res["ffn1_w_out"] = update("adamw_ffn1_w_out", [r_out1_0, r_out1_1], ffn1_w_out, m_ffn1_w_out, v_ffn1_w_out)
    zero_row = jnp.zeros((dn,), F32)
    packs = [
        _pack_small(f1, mx, f2, kvn, fin, qn, lat, rel, zero_row)
        for f1, mx, f2, kvn, fin, qn, lat, rel in (
            (ffn1_norm, mix_norm, ffn2_norm, kv_norm, final_norm, b_q_norm, kv_latent_norm, a_rel_bias),
            (m_ffn1_norm, m_mix_norm, m_ffn2_norm, m_kv_norm, m_final_norm, m_b_q_norm, m_kv_latent_norm, m_a_rel_bias),
            (v_ffn1_norm, v_mix_norm, v_ffn2_norm, v_kv_norm, v_final_norm, v_b_q_norm, v_kv_latent_norm, v_a_rel_bias),
        )
    ]
    small_out = [_unpack_small(o) for o in _adamw("adamw_small", [r_small], *packs)]
    for name in ("ffn1_norm", "mix_norm", "ffn2_norm", "a_rel_bias", "kv_norm", "kv_latent_norm", "b_q_norm", "final_norm"):
        res[name] = [so[name] for so in small_out]
    loss = small_out[0]["last"][0]

    order = [
        "ffn1_norm", "ffn1_w_in", "ffn1_w_out", "mix_norm", "ffn2_norm", "ffn2_w_in", "ffn2_w_out", "a_w_qkv", "a_rel_bias",
        "a_w_o", "kv_norm", "kv_w_down", "kv_latent_norm", "kv_w_up", "b_w_dq", "b_q_norm", "b_w_uq", "b_w_o", "final_norm",
    ]
    return (loss, grad_x, *[res[n][0] for n in order], *[res[n][1] for n in order], *[res[n][2] for n in order], *[res[n][3] for n in order])
```

```python
import functools

import jax
import jax.numpy as jnp
import numpy as np
from jax import lax
from jax.experimental import pallas as pl
from jax.experimental.pallas import tpu as pltpu
from jax.experimental.pallas import tpu_sc as plsc

NDEV = 8
D_MODEL = 1024
D_FF = 2816
FB = 2 * D_FF // NDEV
NJ = D_FF // FB
CHUNK = 64
LEFT_CHUNKS = 8
PAD = LEFT_CHUNKS * CHUNK
BAND = PAD + CHUNK
CHUNKS_PER_STEP = 4
WINDOW = PAD + CHUNKS_PER_STEP * CHUNK
STEP_ROWS = CHUNKS_PER_STEP * 2 * CHUNK
MAX_REL = 128
NREL = 2 * MAX_REL + 1
NREL_PAD = 384
HEADS_A = 16
HEADS_B = 8
NOPE = 128
ROPE = 64
QK_B = NOPE + ROPE
V_DIM = 128
Q_LORA = 768
KV_LORA = 256
ROPE_THETA = 10000.0
EPS = 1e-6
NEG_INF = -1e30
MLA_TQ = 256
ADAM_LR = 0.001
ADAM_B1 = 0.9
ADAM_B2 = 0.999
ADAM_EPS = 1e-08
ADAM_WD = 0.01
ADAM_STEP = 10
PACK_ROWS = 16
GATHER_IDS = tuple(range(1, 9))
REDUCE_IDS = tuple(range(9, 14))
VMEM_LIMIT_BYTES = 56 * 1024 * 1024

F32 = jnp.float32
BF16 = jnp.bfloat16
SDS = jax.ShapeDtypeStruct
BS = pl.BlockSpec
MESH = pl.DeviceIdType.MESH


def _cparams(n_axes):
    return pltpu.CompilerParams(dimension_semantics=("arbitrary",) * n_axes, vmem_limit_bytes=VMEM_LIMIT_BYTES)


def _tile(t, want=512):
    return want if t % want == 0 else t


def _dot(a, b):
    return jnp.dot(a, b, preferred_element_type=F32)


def _dot_nt(a, b):
    return lax.dot_general(a, b, (((1,), (1,)), ((), ())), preferred_element_type=F32)


def _dot_tn(a, b):
    return lax.dot_general(a, b, (((0,), (0,)), ((), ())), preferred_element_type=F32)


def _split3(a):
    hi = a.astype(BF16)
    rest = a - hi.astype(F32)
    mid = rest.astype(BF16)
    return hi, mid, (rest - mid.astype(F32)).astype(BF16)


def _dot_exact(a, onehot, transposed=False):
    ob = onehot.astype(BF16)
    dot = _dot_nt if transposed else _dot
    hi, mid, lo = _split3(a)
    return dot(hi, ob) + dot(mid, ob) + dot(lo, ob)


def _rms_scale(h):
    return lax.rsqrt(jnp.mean(h * h, axis=-1, keepdims=True) + EPS)


def _acc_rows(ref, val, step, n_steps):
    part = val.reshape(val.shape[0] // 8, 8, val.shape[1]).sum(axis=0)

    @pl.when(step == 0)
    def _():
        ref[...] = part

    @pl.when(step > 0)
    def _():
        ref[...] += part

    @pl.when(step == n_steps - 1)
    def _():
        ref[...] = jnp.broadcast_to(jnp.sum(ref[...], axis=0, keepdims=True), ref.shape)


def _exchange_plan(entries):
    ins = [e[1] for e in entries]
    kinds = [e[0] for e in entries]
    lands = [SDS((NDEV,) + a.shape if k == "gather" else a.shape, a.dtype) for k, a in zip(kinds, ins)]
    return ins, lands, kinds


def _mesh_place():
    x, y, c = lax.axis_index("x"), lax.axis_index("y"), lax.axis_index("c")
    return (x, y, c), 4 * x + 2 * y + c


def _flipped(place, p):
    x, y, c = place
    px = 1 - x if p & 4 else x
    py = 1 - y if p & 2 else y
    pc = 1 - c if p & 1 else c
    return (px, py, pc), 4 * px + 2 * py + pc


def _ends(kind, src_ref, land_ref, origin, target):
    if kind == "gather":
        return src_ref, land_ref.at[origin]
    return src_ref.at[target], land_ref.at[origin]


def _remote(kind, src_ref, land_ref, send_sems, recv_sems, k, p, place, me, arriving):
    peer_pos, peer = _flipped(place, p)
    src, dst = _ends(kind, src_ref, land_ref, me, peer)
    if arriving:
        dst = _ends(kind, src_ref, land_ref, peer, me)[1]
    sem = k * (NDEV - 1) + p - 1
    return pltpu.make_async_remote_copy(
        src_ref=src, dst_ref=dst, send_sem=send_sems.at[sem], recv_sem=recv_sems.at[sem], device_id=peer_pos, device_id_type=MESH,
    )


def _exchange(name, entries):
    ins, lands, kinds = _exchange_plan(entries)
    n = len(ins)

    def body(*refs):
        in_refs, land_refs = refs[:n], refs[n : 2 * n]
        send_sems, recv_sems, local_sems = refs[2 * n :]
        place, me = _mesh_place()
        local = []
        for k in range(n):
            src, dst = _ends(kinds[k], in_refs[k], land_refs[k], me, me)
            local.append(pltpu.make_async_copy(src, dst, local_sems.at[k]))
            local[-1].start()
        sends = []
        for p in range(1, NDEV):
            for k in range(n):
                sends.append(_remote(kinds[k], in_refs[k], land_refs[k], send_sems, recv_sems, k, p, place, me, False))
                sends[-1].start()
        for p in range(1, NDEV):
            for k in range(n):
                _remote(kinds[k], in_refs[k], land_refs[k], send_sems, recv_sems, k, p, place, me, True).wait_recv()
        for cp in sends:
            cp.wait_send()
        for cp in local:
            cp.wait()

    any_spec = BS(memory_space=pl.ANY)
    return pl.pallas_call(
        body, name=name, out_shape=lands, in_specs=[any_spec] * n, out_specs=[any_spec] * n,
        scratch_shapes=[
            pltpu.SemaphoreType.DMA((n * (NDEV - 1),)), pltpu.SemaphoreType.DMA((n * (NDEV - 1),)), pltpu.SemaphoreType.DMA((n,)),
        ],
    )(*ins)


HBM_SPEC = BS(memory_space=pltpu.HBM)
SEM_SPEC = BS(memory_space=pltpu.SEMAPHORE)
DATAFLOW = pltpu.SideEffectType.DATAFLOW_SIDE_EFFECTING


def _exchange_start(name, groups):
    plans = [_exchange_plan(g) for g in groups]
    ins = [a for plan in plans for a in plan[0]]
    lands = [s for plan in plans for s in plan[1]]
    kinds = [kind for plan in plans for kind in plan[2]]
    n_in, n_groups = len(ins), len(groups)

    def body(*refs):
        in_refs, land_refs = refs[:n_in], refs[n_in : 2 * n_in]
        sems = refs[2 * n_in : 2 * n_in + 2 * n_groups]
        token = refs[4 * n_in + 2 * n_groups]
        local_sems = refs[4 * n_in + 2 * n_groups + 1]
        place, me = _mesh_place()
        local = []
        for k, kind in enumerate(kinds):
            src, dst = _ends(kind, in_refs[k], land_refs[k], me, me)
            local.append(pltpu.make_async_copy(src, dst, local_sems.at[k]))
            local[-1].start()
        base = 0
        for g, plan in enumerate(plans):
            for p in range(1, NDEV):
                for k, kind in enumerate(plan[2]):
                    _remote(kind, in_refs[base + k], land_refs[base + k], sems[2 * g], sems[2 * g + 1], k, p, place, me, False).start()
            base += len(plan[2])
        for cp in local:
            cp.wait()
        token[...] = jnp.zeros_like(token)

    sem_shapes = []
    for plan in plans:
        sem_shapes += [pltpu.SemaphoreType.DMA((len(plan[2]) * (NDEV - 1),))] * 2
    outs = pl.pallas_call(
        body, name=name,
        out_shape=sem_shapes + [pltpu.HBM(a.shape, a.dtype) for a in ins] + [pltpu.HBM(s.shape, s.dtype) for s in lands] + [SDS((8, 128), F32)],
        in_specs=[HBM_SPEC] * (2 * n_in),
        out_specs=[SEM_SPEC] * (2 * n_groups) + [HBM_SPEC] * (2 * n_in) + [BS(memory_space=pltpu.VMEM)],
        input_output_aliases={i: 2 * n_groups + i for i in range(2 * n_in)},
        scratch_shapes=[pltpu.SemaphoreType.DMA((n_in,))],
        compiler_params=pltpu.CompilerParams(has_side_effects=DATAFLOW),
    )(
        *[pltpu.with_memory_space_constraint(a, pltpu.HBM) for a in ins],
        *[pltpu.with_memory_space_constraint(lax.empty(s.shape, s.dtype), pltpu.HBM) for s in lands],
    )
    sems, srcs, landed, token = outs[: 2 * n_groups], outs[2 * n_groups : 2 * n_groups + n_in], outs[2 * n_groups + n_in : -1], outs[-1]
    started, base = [], 0
    for g, plan in enumerate(plans):
        n = len(plan[2])
        started.append((sems[2 * g], sems[2 * g + 1], srcs[base : base + n], landed[base : base + n], plan[2]))
        base += n
    return started, token


def _exchange_wait(name, started, after):
    send_sems, recv_sems, srcs, landed, kinds = started
    n = len(kinds)

    def body(*refs):
        in_refs, land_refs = refs[:n], refs[n : 2 * n]
        send_ref, recv_ref = refs[2 * n], refs[2 * n + 1]
        place, me = _mesh_place()
        for p in range(1, NDEV):
            for k in range(n):
                _remote(kinds[k], in_refs[k], land_refs[k], send_ref, recv_ref, k, p, place, me, True).wait_recv()
        for p in range(1, NDEV):
            for k in range(n):
                _remote(kinds[k], in_refs[k], land_refs[k], send_ref, recv_ref, k, p, place, me, False).wait_send()

    outs = pl.pallas_call(
        body, name=name,
        out_shape=[pltpu.HBM(a.shape, a.dtype) for a in srcs] + [pltpu.HBM(a.shape, a.dtype) for a in landed],
        in_specs=[HBM_SPEC] * (2 * n) + [SEM_SPEC, SEM_SPEC, BS(memory_space=pl.ANY)],
        out_specs=[HBM_SPEC] * (2 * n),
        input_output_aliases={i: i for i in range(2 * n)},
        compiler_params=pltpu.CompilerParams(has_side_effects=DATAFLOW),
    )(*srcs, *landed, send_sems, recv_sems, after)
    return outs[n:]


def _exchange_sc(name, entries, collective_id):
    ins, lands, kinds = _exchange_plan(entries)
    n = len(ins)

    def launch(*refs):
        in_refs, land_refs = refs[:n], refs[n : 2 * n]
        send_sems, recv_sems, local_sems = refs[2 * n :]
        place, me = _mesh_place()
        barrier = pltpu.get_barrier_semaphore()
        for p in range(1, NDEV):
            pl.semaphore_signal(barrier, inc=1, device_id=_flipped(place, p)[0], device_id_type=MESH)
        pl.semaphore_wait(barrier, NDEV - 1)
        local = []
        for k in range(n):
            src, dst = _ends(kinds[k], in_refs[k], land_refs[k], me, me)
            local.append(pltpu.make_async_copy(src, dst, local_sems.at[k]))
            local[-1].start()
        sends = []
        if all(kind == "gather" for kind in kinds):
            for p in (1, 2, 4, 6):
                for k in range(n):
                    sends.append(_remote(kinds[k], in_refs[k], land_refs[k], send_sems, recv_sems, k, p, place, me, False))
                    sends[-1].start()
            sibling_pos, _ = _flipped(place, 1)
            for f in (2, 4, 6):
                _, origin = _flipped(place, f)
                for k in range(n):
                    _remote(kinds[k], in_refs[k], land_refs[k], send_sems, recv_sems, k, f, place, me, True).wait_recv()
                    sem = k * (NDEV - 1) + f
                    sends.append(
                        pltpu.make_async_remote_copy(
                            src_ref=land_refs[k].at[origin], dst_ref=land_refs[k].at[origin], send_sem=send_sems.at[sem],
                            recv_sem=recv_sems.at[sem], device_id=sibling_pos, device_id_type=MESH,
                        )
                    )
                    sends[-1].start()
            for p in (1, 3, 5, 7):
                for k in range(n):
                    _remote(kinds[k], in_refs[k], land_refs[k], send_sems, recv_sems, k, p, place, me, True).wait_recv()
        else:
            for p in range(1, NDEV):
                for k in range(n):
                    sends.append(_remote(kinds[k], in_refs[k], land_refs[k], send_sems, recv_sems, k, p, place, me, False))
                    sends[-1].start()
            for p in range(1, NDEV):
                for k in range(n):
                    _remote(kinds[k], in_refs[k], land_refs[k], send_sems, recv_sems, k, p, place, me, True).wait_recv()
        for cp in sends:
            cp.wait_send()
        for cp in local:
            cp.wait()

    return pl.kernel(
        launch, out_type=tuple(lands), mesh=plsc.ScalarSubcoreMesh(axis_name="sequencer", num_cores=1), name=name,
        scratch_types=(
            pltpu.SemaphoreType.DMA((n * (NDEV - 1),)), pltpu.SemaphoreType.DMA((n * (NDEV - 1),)), pltpu.SemaphoreType.DMA((n,)),
        ),
        compiler_params=pltpu.CompilerParams(collective_id=collective_id),
    )(*ins)


def _dep_spec(n_axes):
    return BS((8, 128), (lambda i: (0, 0)) if n_axes == 1 else (lambda i, j: (0, 0)))


def _norm_fwd(name, h, gammas):
    t, dn = h.shape
    ng = gammas.shape[0]
    tm = _tile(t)

    def body(h_ref, g_ref, *outs):
        hv = h_ref[...]
        hh = hv * _rms_scale(hv)
        for i, o_ref in enumerate(outs):
            o_ref[...] = (hh * g_ref[i : i + 1, :]).astype(BF16)

    row = BS((tm, dn), lambda i: (i, 0))
    return pl.pallas_call(
        body, name=name, grid=(t // tm,),
        in_specs=[row, BS((ng, dn), lambda i: (0, 0))],
        out_specs=[row] * ng, out_shape=[SDS((t, dn), BF16)] * ng,
        compiler_params=_cparams(1),
    )(h, gammas)


def _ffn_in(name, n, w_in, layer):
    t, dn = n.shape
    tm = _tile(t, 1024)

    def body(n_ref, wg_ref, wu_ref, gu_ref, a_ref):
        xv = n_ref[...]
        g = _dot(xv, wg_ref[...])
        u = _dot(xv, wu_ref[...])
        gu_ref[0] = g.astype(BF16)
        gu_ref[1] = u.astype(BF16)
        a_ref[...] = (g * jax.nn.sigmoid(g) * u).astype(BF16)

    return pl.pallas_call(
        body, name=name, grid=(NJ, t // tm),
        in_specs=[
            BS((tm, dn), lambda j, i: (i, 0)),
            BS((None, None, dn, FB), lambda j, i: (layer, j, 0, 0)),
            BS((None, None, dn, FB), lambda j, i: (layer, j + NJ, 0, 0)),
        ],
        out_specs=[BS((None, 2, tm, FB), lambda j, i: (j, 0, i, 0)), BS((None, tm, FB), lambda j, i: (j, i, 0))],
        out_shape=[SDS((NJ, 2, t, FB), BF16), SDS((NJ, t, FB), BF16)],
        compiler_params=_cparams(2),
    )(n, w_in, w_in)


def _mm_res_norm(name, a, w, layer, h_in, gammas, scale):
    nk, t, kb = a.shape
    dn = w.shape[-1]
    ng = 0 if gammas is None else gammas.shape[0]
    tm = _tile(t)

    def body(*refs):
        a_ref, w_ref, h_ref = refs[:3]
        g_ref = refs[3] if ng else None
        outs = refs[3 + (1 if ng else 0) :]
        acc = _dot(a_ref[0], w_ref[0])
        for k in range(1, nk):
            acc += _dot(a_ref[k], w_ref[k])
        ho = h_ref[...] + scale * acc
        outs[0][...] = ho
        if ng:
            hh = ho * _rms_scale(ho)
            for i in range(ng):
                outs[1 + i][...] = (hh * g_ref[i : i + 1, :]).astype(BF16)

    row = BS((tm, dn), lambda i: (i, 0))
    in_specs = [BS((nk, tm, kb), lambda i: (0, i, 0)), BS((None, nk, kb, dn), lambda i: (layer, 0, 0, 0)), row]
    args = [a, w, h_in]
    if ng:
        in_specs.append(BS((ng, dn), lambda i: (0, 0)))
        args.append(gammas)
    return pl.pallas_call(
        body, name=name, grid=(t // tm,),
        in_specs=in_specs,
        out_specs=[row] * (1 + ng), out_shape=[SDS((t, dn), F32)] + [SDS((t, dn), BF16)] * ng,
        compiler_params=_cparams(1),
    )(*args)


def _qkv_proj(name, hn, w_qkv):
    t, dn = hn.shape
    wb = w_qkv.shape[-1]
    per = wb // 128
    tm = _tile(t)

    def body(x_ref, w_ref, o_ref):
        xv = x_ref[...]
        for j in range(NDEV):
            yv = _dot(xv, w_ref[j]).astype(BF16)
            for i in range(per):
                n = per * j + i
                o_ref[n // 8, :, (n % 8) * 128 : (n % 8 + 1) * 128] = yv[:, i * 128 : (i + 1) * 128]

    return pl.pallas_call(
        body, name=name, grid=(t // tm,),
        in_specs=[BS((tm, dn), lambda i: (i, 0)), BS((NDEV, dn, wb), lambda i: (0, 0, 0))],
        out_specs=BS((3, tm, dn), lambda i: (0, i, 0)), out_shape=SDS((3, t, dn), BF16),
        compiler_params=_cparams(1),
    )(hn, w_qkv)


def _rel_onehot(i):
    r = lax.broadcasted_iota(jnp.int32, (NREL_PAD, BAND), 0)
    j = lax.broadcasted_iota(jnp.int32, (NREL_PAD, BAND), 1)
    idx = jnp.clip(PAD + i - j, -MAX_REL, MAX_REL) + MAX_REL
    return (idx == r).astype(F32)


def _rel_bias_fwd(table):
    def body(t_ref, o_ref):
        i8 = pl.program_id(0)
        for ii in range(8):
            o_ref[:, ii, :] = _dot_exact(t_ref[...], _rel_onehot(i8 * 8 + ii))

    return pl.pallas_call(
        body, name="rel_bias_fwd", grid=(CHUNK // 8,),
        in_specs=[BS((HEADS_A, NREL_PAD), lambda i: (0, 0))],
        out_specs=BS((HEADS_A, 8, BAND), lambda i: (0, i, 0)), out_shape=SDS((HEADS_A, CHUNK, BAND), F32),
        compiler_params=_cparams(1),
    )(table)


def _rel_bias_bwd(dbias):
    def body(d_ref, o_ref):
        i8 = pl.program_id(0)
        acc = jnp.zeros((HEADS_A, NREL_PAD), F32)
        for ii in range(8):
            acc += _dot_exact(d_ref[:, ii, :], _rel_onehot(i8 * 8 + ii), transposed=True)

        @pl.when(i8 == 0)
        def _():
            o_ref[...] = acc

        @pl.when(i8 > 0)
        def _():
            o_ref[...] += acc

    return pl.pallas_call(
        body, name="rel_bias_bwd", grid=(CHUNK // 8,),
        in_specs=[BS((HEADS_A, 8, BAND), lambda i: (0, i, 0))],
        out_specs=BS((HEADS_A, NREL_PAD), lambda i: (0, 0)), out_shape=SDS((HEADS_A, NREL_PAD), F32),
        compiler_params=_cparams(1),
    )(dbias)


def _window_bias(bias):
    b = bias.reshape(HEADS_A // 2, 2, CHUNK, BAND)
    per_chunk = [
        jnp.pad(b, ((0, 0), (0, 0), (0, 0), (cc * CHUNK, WINDOW - BAND - cc * CHUNK)), constant_values=NEG_INF)
        for cc in range(CHUNKS_PER_STEP)
    ]
    return jnp.stack(per_chunk, axis=1).reshape(HEADS_A // 2, STEP_ROWS, WINDOW)


def _window_bias_bwd(dwin):
    d = dwin.reshape(HEADS_A // 2, CHUNKS_PER_STEP, 2, CHUNK, WINDOW)
    return sum(d[:, cc, :, :, cc * CHUNK : cc * CHUNK + BAND] for cc in range(CHUNKS_PER_STEP)).reshape(HEADS_A, CHUNK, BAND)


def _step_rows(xs, lane):
    parts = []
    for cc in range(CHUNKS_PER_STEP):
        xc = xs[cc * CHUNK : (cc + 1) * CHUNK]
        parts.append(jnp.where(lane < 64, xc, jnp.zeros_like(xc)))
        parts.append(jnp.where(lane >= 64, xc, jnp.zeros_like(xc)))
    return jnp.concatenate(parts, axis=0)


def _pair_rows(ys, lane):
    parts = []
    for cc in range(CHUNKS_PER_STEP):
        y0 = ys[(2 * cc) * CHUNK : (2 * cc + 1) * CHUNK]
        y1 = ys[(2 * cc + 1) * CHUNK : (2 * cc + 2) * CHUNK]
        parts.append(jnp.where(lane < 64, y0, y1))
    return jnp.concatenate(parts, axis=0)


def _window_probs(q_rows, kwin, bias_win, first_key):
    s = _dot_nt(q_rows, kwin) * (CHUNK ** -0.5) + bias_win
    col = lax.broadcasted_iota(jnp.int32, s.shape, 1)
    s = jnp.where(col >= first_key, s, NEG_INF)
    e = jnp.exp(s - jnp.max(s, axis=-1, keepdims=True))
    return e / jnp.sum(e, axis=-1, keepdims=True)


def _attn_a_fwd(qkv3, bias_win, bl, seq):
    t, dn = qkv3.shape[1:]
    npair = dn // 128
    step = CHUNKS_PER_STEP * CHUNK

    def body(q_ref, k_ref, v_ref, b_ref, o_ref, kpad, vpad):
        kpad[0:PAD, :] = jnp.zeros((PAD, 128), BF16)
        vpad[0:PAD, :] = jnp.zeros((PAD, 128), BF16)
        kpad[PAD:, :] = k_ref[...]
        vpad[PAD:, :] = v_ref[...]
        lane = lax.broadcasted_iota(jnp.int32, (CHUNK, 128), 1)

        def chunks(it, carry):
            r0 = pl.multiple_of(it * step, step)
            q_rows = _step_rows(q_ref[pl.ds(r0, step), :], lane)
            p = _window_probs(q_rows, kpad[pl.ds(r0, WINDOW), :], b_ref[...], PAD - r0)
            o_rows = _dot(p.astype(BF16), vpad[pl.ds(r0, WINDOW), :])
            o_ref[pl.ds(r0, step), :] = _pair_rows(o_rows, lane).astype(BF16)
            return carry

        lax.fori_loop(0, seq // step, chunks, 0)

    return pl.pallas_call(
        body, name="attn_a_fwd", grid=(bl, npair),
        in_specs=[
            BS((None, seq, 128), lambda b, h: (0, b, h)),
            BS((None, seq, 128), lambda b, h: (1, b, h)),
            BS((None, seq, 128), lambda b, h: (2, b, h)),
            BS((None, STEP_ROWS, WINDOW), lambda b, h: (h, 0, 0)),
        ],
        out_specs=BS((seq, 128), lambda b, h: (b, h)), out_shape=SDS((t, dn), BF16),
        scratch_shapes=[pltpu.VMEM((PAD + seq, 128), BF16), pltpu.VMEM((PAD + seq, 128), BF16)],
        compiler_params=_cparams(2),
    )(qkv3, qkv3, qkv3, bias_win)


def _attn_a_bwd(qkv3, do, bias_win, bl, seq):
    t, dn = qkv3.shape[1:]
    npair = dn // 128
    step = CHUNKS_PER_STEP * CHUNK

    def body(q_ref, k_ref, v_ref, do_ref, b_ref, dqkv_ref, db_ref, kpad, vpad, dkacc, dvacc):
        b = pl.program_id(1)
        kpad[0:PAD, :] = jnp.zeros((PAD, 128), BF16)
        vpad[0:PAD, :] = jnp.zeros((PAD, 128), BF16)
        kpad[PAD:, :] = k_ref[...]
        vpad[PAD:, :] = v_ref[...]
        dkacc[...] = jnp.zeros_like(dkacc)
        dvacc[...] = jnp.zeros_like(dvacc)

        @pl.when(b == 0)
        def _():
            db_ref[...] = jnp.zeros_like(db_ref)

        lane = lax.broadcasted_iota(jnp.int32, (CHUNK, 128), 1)

        def chunks(it, carry):
            r0 = pl.multiple_of(it * step, step)
            q_rows = _step_rows(q_ref[pl.ds(r0, step), :], lane)
            do_rows = _step_rows(do_ref[pl.ds(r0, step), :], lane)
            kwin = kpad[pl.ds(r0, WINDOW), :]
            vwin = vpad[pl.ds(r0, WINDOW), :]
            p = _window_probs(q_rows, kwin, b_ref[...], PAD - r0)
            dp = _dot_nt(do_rows, vwin)
            ds = p * (dp - jnp.sum(p * dp, axis=-1, keepdims=True))
            db_ref[...] += ds
            dsb = (ds * (CHUNK ** -0.5)).astype(BF16)
            dqkv_ref[0, pl.ds(r0, step), :] = _pair_rows(_dot(dsb, kwin), lane).astype(BF16)
            dkacc[pl.ds(r0, WINDOW), :] += _dot_tn(dsb, q_rows)
            dvacc[pl.ds(r0, WINDOW), :] += _dot_tn(p.astype(BF16), do_rows)
            return carry

        lax.fori_loop(0, seq // step, chunks, 0)
        dqkv_ref[1] = dkacc[PAD:, :].astype(BF16)
        dqkv_ref[2] = dvacc[PAD:, :].astype(BF16)

    return pl.pallas_call(
        body, name="attn_a_bwd", grid=(npair, bl),
        in_specs=[
            BS((None, seq, 128), lambda h, b: (0, b, h)),
            BS((None, seq, 128), lambda h, b: (1, b, h)),
            BS((None, seq, 128), lambda h, b: (2, b, h)),
            BS((seq, 128), lambda h, b: (b, h)),
            BS((None, STEP_ROWS, WINDOW), lambda h, b: (h, 0, 0)),
        ],
        out_specs=[BS((3, seq, 128), lambda h, b: (0, b, h)), BS((None, STEP_ROWS, WINDOW), lambda h, b: (h, 0, 0))],
        out_shape=[SDS((3, t, dn), BF16), SDS((HEADS_A // 2, STEP_ROWS, WINDOW), F32)],
        scratch_shapes=[
            pltpu.VMEM((PAD + seq, 128), BF16), pltpu.VMEM((PAD + seq, 128), BF16),
            pltpu.VMEM((PAD + seq, 128), F32), pltpu.VMEM((PAD + seq, 128), F32),
        ],
        compiler_params=_cparams(2),
    )(qkv3, qkv3, qkv3, do, bias_win)


def _rope_tables(seq):
    half = ROPE // 2
    freqs = ROPE_THETA ** (-jnp.arange(half, dtype=F32) / half)
    ang = jnp.arange(seq, dtype=F32)[:, None] * freqs[None, :]
    cos, sin = jnp.cos(ang), jnp.sin(ang)
    c64 = jnp.concatenate([cos, cos], axis=1)
    s64 = jnp.concatenate([-sin, sin], axis=1)
    c192 = jnp.concatenate([jnp.ones((seq, NOPE), F32), c64], axis=1)
    s192 = jnp.concatenate([jnp.zeros((seq, NOPE), F32), s64], axis=1)
    p64 = np.zeros((ROPE, ROPE), np.float32)
    for col in range(ROPE):
        p64[(col + half) % ROPE, col] = 1.0
    p192 = np.zeros((QK_B, QK_B), np.float32)
    p192[NOPE:, NOPE:] = p64
    return c64, s64, jnp.asarray(p64), c192, s192, jnp.asarray(p192)


def _rope(xv, cos, sin_signed, swap):
    return xv * cos + _dot_exact(xv, swap) * sin_signed


def _rope_bwd(dy, cos, sin_signed, swap):
    return dy * cos + _dot_exact(dy * sin_signed, swap)


def _q_down(hn, w_dq, q_norm):
    t, dn = hn.shape
    ql = w_dq.shape[1]
    tm = _tile(t)

    def body(x_ref, w_ref, g_ref, pre_ref, cq_ref):
        pre = _dot(x_ref[...], w_ref[...])
        pre_ref[...] = pre
        cq_ref[...] = (pre * _rms_scale(pre) * g_ref[...]).astype(BF16)

    return pl.pallas_call(
        body, name="q_down", grid=(t // tm,),
        in_specs=[BS((tm, dn), lambda i: (i, 0)), BS((dn, ql), lambda i: (0, 0)), BS((1, ql), lambda i: (0, 0))],
        out_specs=[BS((tm, ql), lambda i: (i, 0))] * 2, out_shape=[SDS((t, ql), F32), SDS((t, ql), BF16)],
        compiler_params=_cparams(1),
    )(hn, w_dq, q_norm)


def _q_up(cq, w_uq, c192, s192, p192, seq):
    t, ql = cq.shape
    tm = _tile(min(seq, 512), min(seq, 512))
    nseq = seq // tm

    def body(x_ref, w_ref, c_ref, s_ref, p_ref, o_ref):
        xv = x_ref[...]
        for h in range(HEADS_B):
            o_ref[h] = _rope(_dot(xv, w_ref[h]), c_ref[...], s_ref[...], p_ref[...]).astype(BF16)

    pos = BS((tm, QK_B), lambda i: (i % nseq, 0))
    return pl.pallas_call(
        body, name="q_up", grid=(t // tm,),
        in_specs=[
            BS((tm, ql), lambda i: (i, 0)), BS((HEADS_B, ql, QK_B), lambda i: (0, 0, 0)), pos, pos,
            BS((QK_B, QK_B), lambda i: (0, 0)),
        ],
        out_specs=BS((HEADS_B, tm, QK_B), lambda i: (0, i, 0)), out_shape=SDS((HEADS_B, t, QK_B), BF16),
        compiler_params=_cparams(1),
    )(cq, w_uq, c192, s192, p192)


def _kv_down(hk, w_down, latent_norm, c64, s64, p64, seq):
    t, dn = hk.shape
    wd = w_down.shape[1]
    tm = _tile(min(seq, 512), min(seq, 512))
    nseq = seq // tm

    def body(x_ref, w_ref, g_ref, c_ref, s_ref, p_ref, ckr_ref, ckv_ref, kr_ref):
        ckr = _dot(x_ref[...], w_ref[...])
        ckr_ref[...] = ckr
        lat = ckr[:, :KV_LORA]
        ckv_ref[...] = (lat * _rms_scale(lat) * g_ref[...]).astype(BF16)
        kr_ref[...] = _rope(ckr[:, KV_LORA:], c_ref[...], s_ref[...], p_ref[...]).astype(BF16)

    pos = BS((tm, ROPE), lambda i: (i % nseq, 0))
    return pl.pallas_call(
        body, name="kv_down", grid=(t // tm,),
        in_specs=[
            BS((tm, dn), lambda i: (i, 0)), BS((dn, wd), lambda i: (0, 0)), BS((1, KV_LORA), lambda i: (0, 0)), pos, pos,
            BS((ROPE, ROPE), lambda i: (0, 0)),
        ],
        out_specs=[BS((tm, wd), lambda i: (i, 0)), BS((tm, KV_LORA), lambda i: (i, 0)), BS((tm, ROPE), lambda i: (i, 0))],
        out_shape=[SDS((t, wd), F32), SDS((t, KV_LORA), BF16), SDS((t, ROPE), BF16)],
        compiler_params=_cparams(1),
    )(hk, w_down, latent_norm, c64, s64, p64)


def _kv_up(ckv, w_up):
    t, kl = ckv.shape
    hb = w_up.shape[-1]
    tm = _tile(t)

    def body(x_ref, w_ref, o_ref):
        xv = x_ref[...]
        for h in range(HEADS_B):
            o_ref[:, h * hb : (h + 1) * hb] = _dot(xv, w_ref[h]).astype(BF16)

    return pl.pallas_call(
        body, name="kv_up", grid=(t // tm,),
        in_specs=[BS((tm, kl), lambda i: (i, 0)), BS((HEADS_B, kl, hb), lambda i: (0, 0, 0))],
        out_specs=BS((tm, HEADS_B * hb), lambda i: (i, 0)), out_shape=SDS((t, HEADS_B * hb), BF16),
        compiler_params=_cparams(1),
    )(ckv, w_up)


def _mla_diagonal_mask(tq):
    rows = lax.broadcasted_iota(jnp.int32, (tq, tq), 0)
    cols = lax.broadcasted_iota(jnp.int32, (tq, tq), 1)
    return jnp.where(jnp.right_shift(cols, 6) <= jnp.right_shift(rows, 6), 0.0, NEG_INF)


def _mla_probs(qi, kcat, diagonal):
    s = _dot_nt(qi, kcat) * (QK_B ** -0.5)
    tq, n_keys = s.shape
    own = s[:, n_keys - tq :] + diagonal
    s = own if n_keys == tq else jnp.concatenate([s[:, : n_keys - tq], own], axis=1)
    e = jnp.exp(s - jnp.max(s, axis=-1, keepdims=True))
    return e / jnp.sum(e, axis=-1, keepdims=True)


def _mla_fwd(q, kv, kr, bl, seq):
    t = kv.shape[0]
    tq = min(MLA_TQ, seq)

    def body(q_ref, kn_ref, v_ref, kr_ref, o_ref):
        kcat = jnp.concatenate([kn_ref[...], kr_ref[...]], axis=1)
        vv = v_ref[...]
        diagonal = _mla_diagonal_mask(tq)
        for i in range(seq // tq):
            n_keys = (i + 1) * tq
            p = _mla_probs(q_ref[i * tq : (i + 1) * tq, :], kcat[:n_keys], diagonal)
            o_ref[i * tq : (i + 1) * tq, :] = _dot(p.astype(BF16), vv[:n_keys]).astype(BF16)

    return pl.pallas_call(
        body, name="mla_fwd", grid=(bl, HEADS_B),
        in_specs=[
            BS((None, seq, QK_B), lambda b, h: (h, b, 0)),
            BS((seq, NOPE), lambda b, h: (b, 2 * h)),
            BS((seq, V_DIM), lambda b, h: (b, 2 * h + 1)),
            BS((seq, ROPE), lambda b, h: (b, 0)),
        ],
        out_specs=BS((seq, V_DIM), lambda b, h: (b, h)), out_shape=SDS((t, HEADS_B * V_DIM), BF16),
        compiler_params=_cparams(2),
    )(q, kv, kv, kr)


def _mla_bwd(q, kv, kr, do, c192, s192, p192, bl, seq):
    t = kv.shape[0]
    tq = min(MLA_TQ, seq)

    def body(q_ref, kn_ref, v_ref, kr_ref, do_ref, c_ref, s_ref, p_ref, dq_ref, dkv_ref, dkr_ref, dkacc, dvacc):
        h = pl.program_id(1)
        kcat = jnp.concatenate([kn_ref[...], kr_ref[...]], axis=1)
        vv = v_ref[...]
        dkacc[...] = jnp.zeros_like(dkacc)
        dvacc[...] = jnp.zeros_like(dvacc)
        diagonal = _mla_diagonal_mask(tq)
        for i in range(seq // tq):
            n_keys = (i + 1) * tq
            rows = slice(i * tq, (i + 1) * tq)
            qi = q_ref[rows, :]
            doi = do_ref[rows, :]
            p = _mla_probs(qi, kcat[:n_keys], diagonal)
            dp = _dot_nt(doi, vv[:n_keys])
            ds = p * (dp - jnp.sum(p * dp, axis=-1, keepdims=True))
            dsb = (ds * (QK_B ** -0.5)).astype(BF16)
            dq = _dot(dsb, kcat[:n_keys])
            dq_ref[rows, :] = _rope_bwd(dq, c_ref[rows, :], s_ref[rows, :], p_ref[...]).astype(BF16)
            dkacc[0:n_keys, :] += _dot_tn(dsb, qi)
            dvacc[0:n_keys, :] += _dot_tn(p.astype(BF16), doi)
        dk = dkacc[...]
        dkv_ref[:, :NOPE] = dk[:, :NOPE].astype(BF16)
        dkv_ref[:, NOPE:] = dvacc[...].astype(BF16)

        @pl.when(h == 0)
        def _():
            dkr_ref[...] = dk[:, NOPE:]

        @pl.when(h > 0)
        def _():
            dkr_ref[...] += dk[:, NOPE:]

    return pl.pallas_call(
        body, name="mla_bwd", grid=(bl, HEADS_B),
        in_specs=[
            BS((None, seq, QK_B), lambda b, h: (h, b, 0)),
            BS((seq, NOPE), lambda b, h: (b, 2 * h)),
            BS((seq, V_DIM), lambda b, h: (b, 2 * h + 1)),
            BS((seq, ROPE), lambda b, h: (b, 0)),
            BS((seq, V_DIM), lambda b, h: (b, h)),
            BS((seq, QK_B), lambda b, h: (0, 0)),
            BS((seq, QK_B), lambda b, h: (0, 0)),
            BS((QK_B, QK_B), lambda b, h: (0, 0)),
        ],
        out_specs=[
            BS((None, seq, QK_B), lambda b, h: (h, b, 0)),
            BS((seq, NOPE + V_DIM), lambda b, h: (b, h)),
            BS((seq, ROPE), lambda b, h: (b, 0)),
        ],
        out_shape=[SDS((HEADS_B, t, QK_B), BF16), SDS((t, HEADS_B * (NOPE + V_DIM)), BF16), SDS((t, ROPE), F32)],
        scratch_shapes=[pltpu.VMEM((seq, QK_B), F32), pltpu.VMEM((seq, V_DIM), F32)],
        compiler_params=_cparams(2),
    )(q, kv, kv, kr, do, c192, s192, p192)


def _loss_final(h, target, gamma):
    t, dn = h.shape
    tm = _tile(t)
    nt = t // tm

    def body(h_ref, t_ref, g_ref, dh_ref, dhb_ref, dg_ref, loss_ref):
        i = pl.program_id(0)
        hv = h_ref[...]
        r = _rms_scale(hv)
        hh = hv * r
        gam = g_ref[...]
        err = hh * gam - t_ref[...]
        part = 0.5 * jnp.sum(jnp.mean(err * err, axis=-1, keepdims=True))

        @pl.when(i == 0)
        def _():
            loss_ref[...] = jnp.zeros_like(loss_ref)

        loss_ref[...] += part
        dy = err * (1.0 / dn)
        _acc_rows(dg_ref, dy * hh, i, nt)
        t1 = dy * gam
        dh = r * (t1 - hh * jnp.mean(t1 * hh, axis=-1, keepdims=True))
        dh_ref[...] = dh
        dhb_ref[...] = dh.astype(BF16)

    row = BS((tm, dn), lambda i: (i, 0))
    return pl.pallas_call(
        body, name="loss_final", grid=(nt,),
        in_specs=[row, row, BS((1, dn), lambda i: (0, 0))],
        out_specs=[row, row, BS((8, dn), lambda i: (0, 0)), BS((8, 128), lambda i: (0, 0))],
        out_shape=[SDS((t, dn), F32), SDS((t, dn), BF16), SDS((8, dn), F32), SDS((8, 128), F32)],
        compiler_params=_cparams(1),
    )(h, target, gamma)


def _ffn_bwd_in(name, dh, w_out, layer, gu, dep=None):
    t, dn = dh.shape
    tm = _tile(t, 1024)

    def body(dh_ref, w_ref, gu_ref, *rest):
        o_ref = rest[-1]
        da = 0.5 * _dot_nt(dh_ref[...], w_ref[...])
        g = gu_ref[0].astype(F32)
        u = gu_ref[1].astype(F32)
        sg = jax.nn.sigmoid(g)
        o_ref[0] = (da * u * (sg * (1.0 + g * (1.0 - sg)))).astype(BF16)
        o_ref[1] = (da * (g * sg)).astype(BF16)

    blk = BS((None, 2, tm, FB), lambda j, i: (j, 0, i, 0))
    deps = [] if dep is None else [dep]
    return pl.pallas_call(
        body, name=name, grid=(NJ, t // tm),
        in_specs=[BS((tm, dn), lambda j, i: (i, 0)), BS((None, None, FB, dn), lambda j, i: (layer, j, 0, 0)), blk]
        + [_dep_spec(2)] * len(deps),
        out_specs=blk, out_shape=SDS((NJ, 2, t, FB), BF16),
        compiler_params=_cparams(2),
    )(dh, w_out, gu, *deps)


def _mm_nt_plain(name, xf, w, dep=None):
    t, dn = xf.shape
    n = w.shape[0]
    tm = _tile(t)

    def body(x_ref, w_ref, *rest):
        rest[-1][...] = _dot_nt(x_ref[...], w_ref[...]).astype(BF16)

    deps = [] if dep is None else [dep]
    return pl.pallas_call(
        body, name=name, grid=(t // tm,),
        in_specs=[BS((tm, dn), lambda i: (i, 0)), BS((n, dn), lambda i: (0, 0))] + [_dep_spec(1)] * len(deps),
        out_specs=BS((tm, n), lambda i: (i, 0)), out_shape=SDS((t, n), BF16),
        compiler_params=_cparams(1),
    )(xf, w, *deps)


def _mm_tn(name, xa, x_spec, ya, y_spec, out_shape, out_spec, nj, scale=None):
    def body(x_ref, y_ref, o_ref):
        acc = _dot_tn(x_ref[...], y_ref[...])
        o_ref[...] = (acc if scale is None else scale * acc).astype(BF16)

    return pl.pallas_call(
        body, name=name, grid=(nj,),
        in_specs=[x_spec, y_spec], out_specs=out_spec, out_shape=SDS(out_shape, BF16),
        compiler_params=_cparams(1),
    )(xa, ya)


def _dw_qkv(hn, dqkv3, wb):
    t, dn = hn.shape
    per = wb // 128

    def body(x_ref, *refs):
        cols = [y_ref[...] for y_ref in refs[:per]]
        refs[per][...] = _dot_tn(x_ref[...], jnp.concatenate(cols, axis=1)).astype(BF16)

    def piece(k):
        return BS((None, t, 128), lambda j: ((per * j + k) // 8, 0, (per * j + k) % 8))

    return pl.pallas_call(
        body, name="dw_qkv", grid=(NDEV,),
        in_specs=[BS((t, dn), lambda j: (0, 0))] + [piece(k) for k in range(per)],
        out_specs=BS((None, dn, wb), lambda j: (j, 0, 0)), out_shape=SDS((NDEV, dn, wb), BF16),
        compiler_params=_cparams(1),
    )(hn, *([dqkv3] * per))


def _mm_nt_epi(name, ya, y_spec, wa, w_spec, nj, n_out, extra, out_shapes, out_specs, epilogue, tm, nt, mm_fn=None):
    n_extra = len(extra)
    n_outs = len(out_shapes)

    def body(*refs):
        y_ref, w_ref = refs[:2]
        ex = refs[2 : 2 + n_extra]
        outs = refs[2 + n_extra : 2 + n_extra + n_outs]
        i = pl.program_id(0)
        j = pl.program_id(1)
        part = _dot_nt(y_ref[...], w_ref[...]) if mm_fn is None else mm_fn(y_ref, w_ref)
        if nj == 1:
            epilogue(part, ex, outs, i, nt)
            return
        acc = refs[-1]

        @pl.when(j == 0)
        def _():
            acc[...] = part

        @pl.when(j > 0)
        def _():
            acc[...] += part

        @pl.when(j == nj - 1)
        def _():
            epilogue(acc[...], ex, outs, i, nt)

    return pl.pallas_call(
        body, name=name, grid=(nt, nj),
        in_specs=[y_spec, w_spec] + [spec for _, spec in extra],
        out_specs=out_specs, out_shape=out_shapes,
        scratch_shapes=[] if nj == 1 else [pltpu.VMEM((tm, n_out), F32)],
        compiler_params=_cparams(2),
    )(ya, wa, *[arr for arr, _ in extra])


def _norm_bwd(dn, hv, gam):
    r = _rms_scale(hv)
    hh = hv * r
    t1 = dn * gam
    return r * (t1 - hh * jnp.mean(t1 * hh, axis=-1, keepdims=True)), dn * hh


def _norm_bwd_epilogue(has_res, out_dtype):
    def epilogue(dn, ex, outs, i, nt):
        dh, dg_rows = _norm_bwd(dn, ex[0][...], ex[1][...])
        _acc_rows(outs[1], dg_rows, i, nt)
        if has_res:
            dh = dh + ex[2][...]
        outs[0][...] = dh.astype(out_dtype)
        if has_res:
            outs[2][...] = dh.astype(BF16)

    return epilogue


def _mm_nt_norm_bwd(name, ya, y_spec, wa, w_spec, nj, h, gamma, res, out_dtype, mm_fn=None, want_tm=512, dep=None):
    t, n = h.shape
    tm = _tile(t, want_tm)
    nt = t // tm
    row = BS((tm, n), lambda i, j: (i, 0))
    extra = [(h, row), (gamma, BS((1, n), lambda i, j: (0, 0)))]
    out_shapes = [SDS((t, n), out_dtype), SDS((8, n), F32)]
    out_specs = [row, BS((8, n), lambda i, j: (0, 0))]
    if res is not None:
        extra.append((res, row))
        out_shapes.append(SDS((t, n), BF16))
        out_specs.append(row)
    if dep is not None:
        extra.append((dep, _dep_spec(2)))
    return _mm_nt_epi(
        name, ya, y_spec, wa, w_spec, nj, n, extra, out_shapes, out_specs, _norm_bwd_epilogue(res is not None, out_dtype), tm, nt, mm_fn,
    )


def _dev_block(jj):
    return jj // 2 + NJ * (jj % 2)


def _ffn_dn_mm(y_ref, w_ref):
    acc = None
    for jj in range(2 * NJ):
        part = _dot_nt(y_ref[jj], w_ref[_dev_block(jj)])
        acc = part if acc is None else acc + part
    return acc


def _ffn_bwd(tag, dh, dhb, n_in, h_in, gamma, gu, a, w_in, w_out, more_grads, collective_id):
    t, dn = dh.shape
    dgu = _ffn_bwd_in(f"{tag}_bwd_in", dhb, w_out, 0, gu).reshape(2 * NJ, t, FB)
    dw_out = _mm_tn(
        f"{tag}_dw_out", a, BS((None, t, FB), lambda j: (j, 0, 0)), dhb, BS((t, dn), lambda j: (0, 0)),
        (NJ, FB, dn), BS((None, FB, dn), lambda j: (j, 0, 0)), NJ, scale=0.5,
    )
    dw_in = _mm_tn(
        f"{tag}_dw_in", n_in, BS((t, dn), lambda j: (0, 0)), dgu, BS((None, t, FB), lambda j: (j, 0, 0)),
        (NDEV, dn, FB), BS((None, dn, FB), lambda j: (_dev_block(j), 0, 0)), NDEV,
    )
    entries = [("scatter", dw_in), ("scatter", dw_out.reshape(NDEV, NJ * FB // NDEV, dn))] + [("scatter", g) for g in more_grads]
    landed = _exchange_sc(f"{tag}_reduce", entries, collective_id)
    tm = _tile(t, 256)
    dh_in, dgam, dhb_in = _mm_nt_norm_bwd(
        f"{tag}_dn", dgu, BS((2 * NJ, tm, FB), lambda i, j: (0, i, 0)),
        w_in, BS((None, NDEV, dn, FB), lambda i, j: (0, 0, 0, 0)), 1, h_in, gamma, dh, F32, mm_fn=_ffn_dn_mm, want_tm=256,
    )
    return dh_in, dhb_in, dgam, landed


def _heads_mm(y_ref, w_ref):
    acc = None
    for h in range(HEADS_B):
        part = _dot_nt(y_ref[h], w_ref[h])
        acc = part if acc is None else acc + part
    return acc


def _dqkv_mm(per):
    def mm(y_ref, w_ref):
        acc = None
        for j in range(NDEV):
            cols = [y_ref[(per * j + k) // 8, :, ((per * j + k) % 8) * 128 : ((per * j + k) % 8 + 1) * 128] for k in range(per)]
            part = _dot_nt(jnp.concatenate(cols, axis=1), w_ref[j])
            acc = part if acc is None else acc + part
        return acc

    return mm


def _kv_latent_bwd(dkv, w_up, ckr, latent_norm, dkr, c64, s64, p64, seq):
    t, wd = ckr.shape
    hb = w_up.shape[-1]
    tm = _tile(min(seq, 512), min(seq, 512))
    nt = t // tm
    nseq = seq // tm

    def epilogue(dn, ex, outs, i, nt_):
        dlat, dg_rows = _norm_bwd(dn, ex[0][...], ex[1][...])
        _acc_rows(outs[1], dg_rows, i, nt_)
        outs[0][:, :KV_LORA] = dlat.astype(BF16)
        outs[0][:, KV_LORA:] = _rope_bwd(ex[2][...], ex[3][...], ex[4][...], ex[5][...]).astype(BF16)

    pos = BS((tm, ROPE), lambda i, j: (i % nseq, 0))
    extra = [
        (ckr, BS((tm, KV_LORA), lambda i, j: (i, 0))), (latent_norm, BS((1, KV_LORA), lambda i, j: (0, 0))),
        (dkr, BS((tm, ROPE), lambda i, j: (i, 0))), (c64, pos), (s64, pos), (p64, BS((ROPE, ROPE), lambda i, j: (0, 0))),
    ]
    def heads_mm(y_ref, w_ref):
        acc = None
        for h in range(HEADS_B):
            part = _dot_nt(y_ref[:, h * hb : (h + 1) * hb], w_ref[h])
            acc = part if acc is None else acc + part
        return acc

    return _mm_nt_epi(
        "kv_latent_bwd", dkv, BS((tm, HEADS_B * hb), lambda i, j: (i, 0)), w_up, BS((HEADS_B, KV_LORA, hb), lambda i, j: (0, 0, 0)),
        1, KV_LORA, extra, [SDS((t, wd), BF16), SDS((8, KV_LORA), F32)],
        [BS((tm, wd), lambda i, j: (i, 0)), BS((8, KV_LORA), lambda i, j: (0, 0))], epilogue, tm, nt, heads_mm,
    )


def _adamw(name, parts, w, m, v):
    n_layers = len(parts)
    rows, cols = w.shape[0] // n_layers, w.shape[1]
    tr = max(d for d in range(8, min(rows, 256) + 1, 8) if rows % d == 0)
    nb = rows // tr

    def body(*refs):
        p_refs = refs[:n_layers]
        w_ref, m_ref, v_ref, g_ref, d_ref, nm_ref, nv_ref = refs[n_layers : n_layers + 7]
        layer = pl.program_id(0)
        for lp in range(n_layers):

            @pl.when(layer == lp)
            def _():
                g = p_refs[lp][0].astype(F32)
                for k in range(1, NDEV):
                    g = g + p_refs[lp][k].astype(F32)
                g_ref[...] = g

        g = g_ref[...]
        nm = ADAM_B1 * m_ref[...] + (1.0 - ADAM_B1) * g
        nv = ADAM_B2 * v_ref[...] + (1.0 - ADAM_B2) * (g * g)
        nm_ref[...] = nm
        nv_ref[...] = nv
        m_hat = nm / (1.0 - ADAM_B1 ** ADAM_STEP)
        v_hat = nv / (1.0 - ADAM_B2 ** ADAM_STEP)
        d_ref[...] = -ADAM_LR * (m_hat / (jnp.sqrt(v_hat) + ADAM_EPS) + ADAM_WD * w_ref[...])

    def part_spec(lp):
        return BS((NDEV, tr, cols), lambda l, i: (0, jnp.where(l == lp, i, jnp.where(l < lp, 0, nb - 1)), 0))

    row = BS((tr, cols), lambda l, i: (l * nb + i, 0))
    return pl.pallas_call(
        body, name=name, grid=(n_layers, nb),
        in_specs=[part_spec(lp) for lp in range(n_layers)] + [row, row, row],
        out_specs=[row] * 4, out_shape=[SDS(w.shape, F32)] * 4,
        compiler_params=_cparams(2),
    )(*parts, w, m, v)


def _pack_small(ffn1_norm, mix_norm, ffn2_norm, kv_norm, final_norm, q_norm, latent_norm, rel_bias, last_row):
    dn = ffn1_norm.shape[-1]

    def rows_of(a, n_rows):
        flat = a.reshape(-1)
        return jnp.pad(flat, (0, n_rows * dn - flat.shape[0])).reshape(n_rows, dn)

    return jnp.concatenate(
        [
            ffn1_norm.reshape(2, dn), mix_norm.reshape(2, dn), ffn2_norm.reshape(2, dn), kv_norm.reshape(1, dn),
            final_norm.reshape(1, dn), rows_of(q_norm, 1), rows_of(latent_norm, 1), rows_of(rel_bias, 5), rows_of(last_row, 1),
        ],
        axis=0,
    )


def _unpack_small(pack):
    dn = pack.shape[-1]
    return dict(
        ffn1_norm=pack[0:2], mix_norm=pack[2:4], ffn2_norm=pack[4:6], kv_norm=pack[6], final_norm=pack[7],
        b_q_norm=pack[8, :Q_LORA].reshape(1, Q_LORA), kv_latent_norm=pack[9, :KV_LORA],
        a_rel_bias=pack[10:15].reshape(-1)[: HEADS_A * NREL].reshape(1, HEADS_A, NREL), last=pack[15],
    )


def kernel(x, ffn1_norm, ffn1_w_in, ffn1_w_out, mix_norm, ffn2_norm, ffn2_w_in, ffn2_w_out, a_w_qkv, a_rel_bias, a_w_o, kv_norm, kv_w_down, kv_latent_norm, kv_w_up, b_w_dq, b_q_norm, b_w_uq, b_w_o, final_norm, loss_target, m_ffn1_norm, m_ffn1_w_in, m_ffn1_w_out, m_mix_norm, m_ffn2_norm, m_ffn2_w_in, m_ffn2_w_out, m_a_w_qkv, m_a_rel_bias, m_a_w_o, m_kv_norm, m_kv_w_down, m_kv_latent_norm, m_kv_w_up, m_b_w_dq, m_b_q_norm, m_b_w_uq, m_b_w_o, m_final_norm, v_ffn1_norm, v_ffn1_w_in, v_ffn1_w_out, v_mix_norm, v_ffn2_norm, v_ffn2_w_in, v_ffn2_w_out, v_a_w_qkv, v_a_rel_bias, v_a_w_o, v_kv_norm, v_kv_w_down, v_kv_latent_norm, v_kv_w_up, v_b_w_dq, v_b_q_norm, v_b_w_uq, v_b_w_o, v_final_norm):
    bl, seq, dn = x.shape
    t = bl * seq
    tm = _tile(t)
    nt = t // tm
    x2 = x.reshape(t, dn)
    target2 = loss_target.reshape(t, dn)

    def gathered(*ws):
        return [("gather", w.astype(BF16)) for w in ws]

    groups = [
        gathered(ffn1_w_in[0]), gathered(ffn1_w_out[0]), gathered(a_w_qkv[0], a_w_o[0]), gathered(ffn2_w_in[0], ffn2_w_out[0]),
        gathered(kv_w_down, kv_w_up), gathered(ffn1_w_in[1], ffn1_w_out[1]), gathered(b_w_dq[0], b_w_uq[0], b_w_o[0]),
        gathered(ffn2_w_in[1], ffn2_w_out[1]),
    ]
    ag = [_exchange_sc(f"gather_{k}", group, GATHER_IDS[k]) for k, group in enumerate(groups)]

    def as_w_in(w):
        return w.reshape(1, NDEV, dn, FB)

    def as_w_out(w):
        return w.reshape(1, NJ, FB, dn)

    c64, s64, p64, c192, s192, p192 = _rope_tables(seq)
    q_norm = b_q_norm.reshape(1, Q_LORA)
    latent_norm = kv_latent_norm.reshape(1, KV_LORA)
    bias = _window_bias(_rel_bias_fwd(jnp.pad(a_rel_bias[0], ((0, 0), (0, NREL_PAD - NREL)))))

    h0, h1, h2, n1, hn, n2, gu1, gu2, a1, a2, w_in1, w_in2, w_out1, w_out2 = ([None, None] for _ in range(14))
    h0[0] = x2
    (n1[0],) = _norm_fwd("norm_x", x2, ffn1_norm[0:1])
    w_in1[0] = as_w_in(ag[0][0])
    gu1[0], a1[0] = _ffn_in("ffn1_in_0", n1[0], w_in1[0], 0)
    w_out1[0] = as_w_out(ag[1][0])
    h1[0], hn[0] = _mm_res_norm("ffn1_out_0", a1[0], w_out1[0], 0, h0[0], mix_norm[0:1], 0.5)
    w_qkv, w_o_a = ag[2]
    qkv_wb = w_qkv.shape[-1]
    w_o_a = w_o_a.reshape(1, 1, dn, dn)
    qkv3 = _qkv_proj("qkv_proj", hn[0], w_qkv)
    o_a = _attn_a_fwd(qkv3, bias, bl, seq)
    h2[0], n2[0] = _mm_res_norm("attn_a_out", o_a.reshape(1, t, dn), w_o_a, 0, h1[0], ffn2_norm[0:1], 1.0)
    w_in2[0], w_out2[0] = as_w_in(ag[3][0]), as_w_out(ag[3][1])
    gu2[0], a2[0] = _ffn_in("ffn2_in_0", n2[0], w_in2[0], 0)
    h0[1], hk, n1[1] = _mm_res_norm(
        "ffn2_out_0", a2[0], w_out2[0], 0, h2[0], jnp.concatenate([kv_norm.reshape(1, dn), ffn1_norm[1:2]], axis=0), 0.5
    )
    w_down, w_up = ag[4]
    w_down = w_down.reshape(dn, KV_LORA + ROPE)
    ckr, ckv, kr = _kv_down(hk, w_down, latent_norm, c64, s64, p64, seq)
    kv = _kv_up(ckv, w_up)
    w_in1[1], w_out1[1] = as_w_in(ag[5][0]), as_w_out(ag[5][1])
    gu1[1], a1[1] = _ffn_in("ffn1_in_1", n1[1], w_in1[1], 0)
    h1[1], hn[1] = _mm_res_norm("ffn1_out_1", a1[1], w_out1[1], 0, h0[1], mix_norm[1:2], 0.5)
    w_dq, w_uq, w_o_b = ag[6]
    w_dq = w_dq.reshape(dn, Q_LORA)
    w_o_b = w_o_b.reshape(1, 1, dn, dn)
    cq_pre, cq = _q_down(hn[1], w_dq, q_norm)
    q = _q_up(cq, w_uq, c192, s192, p192, seq)
    o_b = _mla_fwd(q, kv, kr, bl, seq)
    h2[1], n2[1] = _mm_res_norm("attn_b_out", o_b.reshape(1, t, dn), w_o_b, 0, h1[1], ffn2_norm[1:2], 1.0)
    w_in2[1], w_out2[1] = as_w_in(ag[7][0]), as_w_out(ag[7][1])
    gu2[1], a2[1] = _ffn_in("ffn2_in_1", n2[1], w_in2[1], 0)
    (h_last,) = _mm_res_norm("ffn2_out_1", a2[1], w_out2[1], 0, h2[1], None, 0.5)
    dh, dhb, dg_final, loss_part = _loss_final(h_last, target2, final_norm.reshape(1, dn))

    dg_ffn1, dg_mix, dg_ffn2, rs_ffn1, rs_ffn2 = ([None, None] for _ in range(5))

    def whole(rows, cols):
        return BS((rows, cols), lambda j: (0, 0))

    def dw_rows(name, xa, ya):
        n = ya.shape[1]
        return _mm_tn(name, xa, whole(t, dn), ya, whole(t, n), (dn, n), whole(dn, n), 1).reshape(NDEV, dn // NDEV, n)

    dh, dhb, dg_ffn2[1], rs_ffn2[1] = _ffn_bwd(
        "ffn2_1", dh, dhb, n2[1], h2[1], ffn2_norm[1:2], gu2[1], a2[1], w_in2[1], w_out2[1], [], REDUCE_IDS[0]
    )
    do_b = _mm_nt_plain("attn_b_do", dhb, w_o_b.reshape(dn, dn))
    dw_o_b = dw_rows("attn_b_dwo", o_b, dhb)
    dq_pre, dkv, dkr = _mla_bwd(q, kv, kr, do_b, c192, s192, p192, bl, seq)
    dw_uq = _mm_tn(
        "dw_uq", cq, whole(t, Q_LORA), dq_pre, BS((None, t, QK_B), lambda j: (j, 0, 0)),
        (HEADS_B, Q_LORA, QK_B), BS((None, Q_LORA, QK_B), lambda j: (j, 0, 0)), HEADS_B,
    )
    dcq_pre, dg_q = _mm_nt_norm_bwd(
        "dcq", dq_pre, BS((HEADS_B, tm, QK_B), lambda i, j: (0, i, 0)), w_uq, BS((HEADS_B, Q_LORA, QK_B), lambda i, j: (0, 0, 0)),
        1, cq_pre, q_norm, None, BF16, mm_fn=_heads_mm,
    )
    dw_dq = dw_rows("dw_dq", hn[1], dcq_pre)
    dh, dg_mix[1], dhb = _mm_nt_norm_bwd(
        "dhn_b", dcq_pre, BS((tm, Q_LORA), lambda i, j: (i, 0)), w_dq, BS((dn, Q_LORA), lambda i, j: (0, 0)),
        1, h1[1], mix_norm[1:2], dh, F32,
    )
    dh, dhb, dg_ffn1[1], rs_ffn1[1] = _ffn_bwd(
        "ffn1_1", dh, dhb, n1[1], h0[1], ffn1_norm[1:2], gu1[1], a1[1], w_in1[1], w_out1[1], [dw_o_b, dw_uq, dw_dq], REDUCE_IDS[1]
    )
    dw_up = _mm_tn(
        "dw_up", ckv, whole(t, KV_LORA), dkv, BS((t, NOPE + V_DIM), lambda j: (0, j)),
        (HEADS_B, KV_LORA, NOPE + V_DIM), BS((None, KV_LORA, NOPE + V_DIM), lambda j: (j, 0, 0)), HEADS_B,
    )
    dckr, dg_latent = _kv_latent_bwd(dkv, w_up, ckr, latent_norm, dkr, c64, s64, p64, seq)
    dw_down = dw_rows("dw_down", hk, dckr)
    dh, dg_kv, dhb = _mm_nt_norm_bwd(
        "dhk", dckr, BS((tm, KV_LORA + ROPE), lambda i, j: (i, 0)), w_down, BS((dn, KV_LORA + ROPE), lambda i, j: (0, 0)),
        1, h0[1], kv_norm.reshape(1, dn), dh, F32,
    )
    dh, dhb, dg_ffn2[0], rs_ffn2[0] = _ffn_bwd(
        "ffn2_0", dh, dhb, n2[0], h2[0], ffn2_norm[0:1], gu2[0], a2[0], w_in2[0], w_out2[0], [dw_up, dw_down], REDUCE_IDS[2]
    )
    do_a = _mm_nt_plain("attn_a_do", dhb, w_o_a.reshape(dn, dn))
    dw_o_a = dw_rows("attn_a_dwo", o_a, dhb)
    dqkv3, dbias = _attn_a_bwd(qkv3, do_a, bias, bl, seq)
    dw_qkv = _dw_qkv(hn[0], dqkv3, qkv_wb)
    dh, dg_mix[0], dhb = _mm_nt_norm_bwd(
        "dhn_a", dqkv3, BS((3, tm, dn), lambda i, j: (0, i, 0)), w_qkv, BS((NDEV, dn, qkv_wb), lambda i, j: (0, 0, 0)),
        1, h1[0], mix_norm[0:1], dh, F32, mm_fn=_dqkv_mm(qkv_wb // 128),
    )
    r_o_a, r_qkv = _exchange_sc("attn_a_reduce", [("scatter", dw_o_a), ("scatter", dw_qkv)], REDUCE_IDS[3])
    dh, dhb, dg_ffn1[0], rs_ffn1[0] = _ffn_bwd(
        "ffn1_0", dh, dhb, n1[0], h0[0], ffn1_norm[0:1], gu1[0], a1[0], w_in1[0], w_out1[0], [], REDUCE_IDS[4]
    )
    grad_x = dh.reshape(bl, seq, dn)
    dtable = _rel_bias_bwd(_window_bias_bwd(dbias))[:, :NREL]

    small = _pack_small(
        jnp.stack([dg_ffn1[0][0], dg_ffn1[1][0]]), jnp.stack([dg_mix[0][0], dg_mix[1][0]]), jnp.stack([dg_ffn2[0][0], dg_ffn2[1][0]]),
        dg_kv[0], dg_final[0], dg_q[0], dg_latent[0], dtable, loss_part[0],
    )
    (r_small,) = _exchange("gather_small_grads", [("gather", small)])

    def update(name, parts, w, m, v):
        n_layers = len(parts)
        rows = int(np.prod(w.shape[:-1]))
        cols = w.shape[-1]
        parts = [p.reshape(NDEV, rows // n_layers, cols) for p in parts]
        outs = _adamw(name, parts, w.reshape(rows, cols), m.reshape(rows, cols), v.reshape(rows, cols))
        return [o.reshape(w.shape) for o in outs]

    res = {}
    r_in2_1, r_out2_1 = rs_ffn2[1]
    r_in1_1, r_out1_1, r_o_b, r_uq, r_dq = rs_ffn1[1]
    r_in2_0, r_out2_0, r_up, r_down = rs_ffn2[0]
    r_in1_0, r_out1_0 = rs_ffn1[0]
    res["ffn2_w_in"] = update("adamw_ffn2_w_in", [r_in2_0, r_in2_1], ffn2_w_in, m_ffn2_w_in, v_ffn2_w_in)
    res["ffn2_w_out"] = update("adamw_ffn2_w_out", [r_out2_0, r_out2_1], ffn2_w_out, m_ffn2_w_out, v_ffn2_w_out)
    res["kv_w_down"] = update("adamw_kv_w_down", [r_down], kv_w_down, m_kv_w_down, v_kv_w_down)
    res["kv_w_up"] = update("adamw_kv_w_up", [r_up], kv_w_up, m_kv_w_up, v_kv_w_up)
    res["b_w_dq"] = update("adamw_b_w_dq", [r_dq], b_w_dq, m_b_w_dq, v_b_w_dq)
    res["b_w_uq"] = update("adamw_b_w_uq", [r_uq], b_w_uq, m_b_w_uq, v_b_w_uq)
    res["b_w_o"] = update("adamw_b_w_o", [r_o_b], b_w_o, m_b_w_o, v_b_w_o)
    res["a_w_qkv"] = update("adamw_a_w_qkv", [r_qkv], a_w_qkv, m_a_w_qkv, v_a_w_qkv)
    res["a_w_o"] = update("adamw_a_w_o", [r_o_a], a_w_o, m_a_w_o, v_a_w_o)
    res["ffn1_w_in"] = update("adamw_ffn1_w_in", [r_in1_0, r_in1_1], ffn1_w_in, m_ffn1_w_in, v_ffn1_w_in)
    res["ffn1_w_out"] = update("adamw_ffn1_w_out", [r_out1_0, r_out1_1], ffn1_w_out, m_ffn1_w_out, v_ffn1_w_out)
    zero_row = jnp.zeros((dn,), F32)
    packs = [
        _pack_small(f1, mx, f2, kvn, fin, qn, lat, rel, zero_row)
        for f1, mx, f2, kvn, fin, qn, lat, rel in (
            (ffn1_norm, mix_norm, ffn2_norm, kv_norm, final_norm, b_q_norm, kv_latent_norm, a_rel_bias),
            (m_ffn1_norm, m_mix_norm, m_ffn2_norm, m_kv_norm, m_final_norm, m_b_q_norm, m_kv_latent_norm, m_a_rel_bias),
            (v_ffn1_norm, v_mix_norm, v_ffn2_norm, v_kv_norm, v_final_norm, v_b_q_norm, v_kv_latent_norm, v_a_rel_bias),
        )
    ]
    small_out = [_unpack_small(o) for o in _adamw("adamw_small", [r_small], *packs)]
    for name in ("ffn1_norm", "mix_norm", "ffn2_norm", "a_rel_bias", "kv_norm", "kv_latent_norm", "b_q_norm", "final_norm"):
        res[name] = [so[name] for so in small_out]
    loss = small_out[0]["last"][0]

    order = [
        "ffn1_norm", "ffn1_w_in", "ffn1_w_out", "mix_norm", "ffn2_norm", "ffn2_w_in", "ffn2_w_out", "a_w_qkv", "a_rel_bias",
        "a_w_o", "kv_norm", "kv_w_down", "kv_latent_norm", "kv_w_up", "b_w_dq", "b_q_norm", "b_w_uq", "b_w_o", "final_norm",
    ]
    return (loss, grad_x, *[res[n][0] for n in order], *[res[n][1] for n in order], *[res[n][2] for n in order], *[res[n][3] for n in order])
```

```python
import functools

import jax
import jax.numpy as jnp
import numpy as np
from jax import lax
from jax.experimental import pallas as pl
from jax.experimental.pallas import tpu as pltpu
from jax.experimental.pallas import tpu_sc as plsc

NDEV = 8
D_MODEL = 1024
D_FF = 2816
FB = 2 * D_FF // NDEV
NJ = D_FF // FB
CHUNK = 64
LEFT_CHUNKS = 8
PAD = LEFT_CHUNKS * CHUNK
BAND = PAD + CHUNK
CHUNKS_PER_STEP = 4
WINDOW = PAD + CHUNKS_PER_STEP * CHUNK
STEP_ROWS = CHUNKS_PER_STEP * 2 * CHUNK
MAX_REL = 128
NREL = 2 * MAX_REL + 1
NREL_PAD = 384
HEADS_A = 16
HEADS_B = 8
NOPE = 128
ROPE = 64
QK_B = NOPE + ROPE
V_DIM = 128
Q_LORA = 768
KV_LORA = 256
ROPE_THETA = 10000.0
EPS = 1e-6
NEG_INF = -1e30
MLA_TQ = 256
ADAM_LR = 0.001
ADAM_B1 = 0.9
ADAM_B2 = 0.999
ADAM_EPS = 1e-08
ADAM_WD = 0.01
ADAM_STEP = 10
PACK_ROWS = 16
GATHER_IDS = tuple(range(1, 9))
REDUCE_IDS = tuple(range(9, 14))
VMEM_LIMIT_BYTES = 56 * 1024 * 1024

F32 = jnp.float32
BF16 = jnp.bfloat16
SDS = jax.ShapeDtypeStruct
BS = pl.BlockSpec
MESH = pl.DeviceIdType.MESH


def _cparams(n_axes):
    return pltpu.CompilerParams(dimension_semantics=("arbitrary",) * n_axes, vmem_limit_bytes=VMEM_LIMIT_BYTES)


def _tile(t, want=512):
    return want if t % want == 0 else t


def _dot(a, b):
    return jnp.dot(a, b, preferred_element_type=F32)


def _dot_nt(a, b):
    return lax.dot_general(a, b, (((1,), (1,)), ((), ())), preferred_element_type=F32)


def _dot_tn(a, b):
    return lax.dot_general(a, b, (((0,), (0,)), ((), ())), preferred_element_type=F32)


def _split3(a):
    hi = a.astype(BF16)
    rest = a - hi.astype(F32)
    mid = rest.astype(BF16)
    return hi, mid, (rest - mid.astype(F32)).astype(BF16)


def _dot_exact(a, onehot, transposed=False):
    ob = onehot.astype(BF16)
    dot = _dot_nt if transposed else _dot
    hi, mid, lo = _split3(a)
    return dot(hi, ob) + dot(mid, ob) + dot(lo, ob)


def _rms_scale(h):
    return lax.rsqrt(jnp.mean(h * h, axis=-1, keepdims=True) + EPS)


def _acc_rows(ref, val, step, n_steps):
    part = val.reshape(val.shape[0] // 8, 8, val.shape[1]).sum(axis=0)

    @pl.when(step == 0)
    def _():
        ref[...] = part

    @pl.when(step > 0)
    def _():
        ref[...] += part

    @pl.when(step == n_steps - 1)
    def _():
        ref[...] = jnp.broadcast_to(jnp.sum(ref[...], axis=0, keepdims=True), ref.shape)


def _exchange_plan(entries):
    ins = [e[1] for e in entries]
    kinds = [e[0] for e in entries]
    lands = [SDS((NDEV,) + a.shape if k == "gather" else a.shape, a.dtype) for k, a in zip(kinds, ins)]
    return ins, lands, kinds


def _mesh_place():
    x, y, c = lax.axis_index("x"), lax.axis_index("y"), lax.axis_index("c")
    return (x, y, c), 4 * x + 2 * y + c


def _flipped(place, p):
    x, y, c = place
    px = 1 - x if p & 4 else x
    py = 1 - y if p & 2 else y
    pc = 1 - c if p & 1 else c
    return (px, py, pc), 4 * px + 2 * py + pc


def _ends(kind, src_ref, land_ref, origin, target):
    if kind == "gather":
        return src_ref, land_ref.at[origin]
    return src_ref.at[target], land_ref.at[origin]


def _remote(kind, src_ref, land_ref, send_sems, recv_sems, k, p, place, me, arriving):
    peer_pos, peer = _flipped(place, p)
    src, dst = _ends(kind, src_ref, land_ref, me, peer)
    if arriving:
        dst = _ends(kind, src_ref, land_ref, peer, me)[1]
    sem = k * (NDEV - 1) + p - 1
    return pltpu.make_async_remote_copy(
        src_ref=src, dst_ref=dst, send_sem=send_sems.at[sem], recv_sem=recv_sems.at[sem], device_id=peer_pos, device_id_type=MESH,
    )


def _exchange(name, entries):
    ins, lands, kinds = _exchange_plan(entries)
    n = len(ins)

    def body(*refs):
        in_refs, land_refs = refs[:n], refs[n : 2 * n]
        send_sems, recv_sems, local_sems = refs[2 * n :]
        place, me = _mesh_place()
        local = []
        for k in range(n):
            src, dst = _ends(kinds[k], in_refs[k], land_refs[k], me, me)
            local.append(pltpu.make_async_copy(src, dst, local_sems.at[k]))
            local[-1].start()
        sends = []
        for p in range(1, NDEV):
            for k in range(n):
                sends.append(_remote(kinds[k], in_refs[k], land_refs[k], send_sems, recv_sems, k, p, place, me, False))
                sends[-1].start()
        for p in range(1, NDEV):
            for k in range(n):
                _remote(kinds[k], in_refs[k], land_refs[k], send_sems, recv_sems, k, p, place, me, True).wait_recv()
        for cp in sends:
            cp.wait_send()
        for cp in local:
            cp.wait()

    any_spec = BS(memory_space=pl.ANY)
    return pl.pallas_call(
        body, name=name, out_shape=lands, in_specs=[any_spec] * n, out_specs=[any_spec] * n,
        scratch_shapes=[
            pltpu.SemaphoreType.DMA((n * (NDEV - 1),)), pltpu.SemaphoreType.DMA((n * (NDEV - 1),)), pltpu.SemaphoreType.DMA((n,)),
        ],
    )(*ins)


HBM_SPEC = BS(memory_space=pltpu.HBM)
SEM_SPEC = BS(memory_space=pltpu.SEMAPHORE)
DATAFLOW = pltpu.SideEffectType.DATAFLOW_SIDE_EFFECTING


def _exchange_start(name, groups):
    plans = [_exchange_plan(g) for g in groups]
    ins = [a for plan in plans for a in plan[0]]
    lands = [s for plan in plans for s in plan[1]]
    kinds = [kind for plan in plans for kind in plan[2]]
    n_in, n_groups = len(ins), len(groups)

    def body(*refs):
        in_refs, land_refs = refs[:n_in], refs[n_in : 2 * n_in]
        sems = refs[2 * n_in : 2 * n_in + 2 * n_groups]
        token = refs[4 * n_in + 2 * n_groups]
        local_sems = refs[4 * n_in + 2 * n_groups + 1]
        place, me = _mesh_place()
        local = []
        for k, kind in enumerate(kinds):
            src, dst = _ends(kind, in_refs[k], land_refs[k], me, me)
            local.append(pltpu.make_async_copy(src, dst, local_sems.at[k]))
            local[-1].start()
        base = 0
        for g, plan in enumerate(plans):
            for p in range(1, NDEV):
                for k, kind in enumerate(plan[2]):
                    _remote(kind, in_refs[base + k], land_refs[base + k], sems[2 * g], sems[2 * g + 1], k, p, place, me, False).start()
            base += len(plan[2])
        for cp in local:
            cp.wait()
        token[...] = jnp.zeros_like(token)

    sem_shapes = []
    for plan in plans:
        sem_shapes += [pltpu.SemaphoreType.DMA((len(plan[2]) * (NDEV - 1),))] * 2
    outs = pl.pallas_call(
        body, name=name,
        out_shape=sem_shapes + [pltpu.HBM(a.shape, a.dtype) for a in ins] + [pltpu.HBM(s.shape, s.dtype) for s in lands] + [SDS((8, 128), F32)],
        in_specs=[HBM_SPEC] * (2 * n_in),
        out_specs=[SEM_SPEC] * (2 * n_groups) + [HBM_SPEC] * (2 * n_in) + [BS(memory_space=pltpu.VMEM)],
        input_output_aliases={i: 2 * n_groups + i for i in range(2 * n_in)},
        scratch_shapes=[pltpu.SemaphoreType.DMA((n_in,))],
        compiler_params=pltpu.CompilerParams(has_side_effects=DATAFLOW),
    )(
        *[pltpu.with_memory_space_constraint(a, pltpu.HBM) for a in ins],
        *[pltpu.with_memory_space_constraint(lax.empty(s.shape, s.dtype), pltpu.HBM) for s in lands],
    )
    sems, srcs, landed, token = outs[: 2 * n_groups], outs[2 * n_groups : 2 * n_groups + n_in], outs[2 * n_groups + n_in : -1], outs[-1]
    started, base = [], 0
    for g, plan in enumerate(plans):
        n = len(plan[2])
        started.append((sems[2 * g], sems[2 * g + 1], srcs[base : base + n], landed[base : base + n], plan[2]))
        base += n
    return started, token


def _exchange_wait(name, started, after):
    send_sems, recv_sems, srcs, landed, kinds = started
    n = len(kinds)

    def body(*refs):
        in_refs, land_refs = refs[:n], refs[n : 2 * n]
        send_ref, recv_ref = refs[2 * n], refs[2 * n + 1]
        place, me = _mesh_place()
        for p in range(1, NDEV):
            for k in range(n):
                _remote(kinds[k], in_refs[k], land_refs[k], send_ref, recv_ref, k, p, place, me, True).wait_recv()
        for p in range(1, NDEV):
            for k in range(n):
                _remote(kinds[k], in_refs[k], land_refs[k], send_ref, recv_ref, k, p, place, me, False).wait_send()

    outs = pl.pallas_call(
        body, name=name,
        out_shape=[pltpu.HBM(a.shape, a.dtype) for a in srcs] + [pltpu.HBM(a.shape, a.dtype) for a in landed],
        in_specs=[HBM_SPEC] * (2 * n) + [SEM_SPEC, SEM_SPEC, BS(memory_space=pl.ANY)],
        out_specs=[HBM_SPEC] * (2 * n),
        input_output_aliases={i: i for i in range(2 * n)},
        compiler_params=pltpu.CompilerParams(has_side_effects=DATAFLOW),
    )(*srcs, *landed, send_sems, recv_sems, after)
    return outs[n:]


def _exchange_sc(name, entries, collective_id):
    ins, lands, kinds = _exchange_plan(entries)
    n = len(ins)

    def launch(*refs):
        in_refs, land_refs = refs[:n], refs[n : 2 * n]
        send_sems, recv_sems, local_sems = refs[2 * n :]
        place, me = _mesh_place()
        barrier = pltpu.get_barrier_semaphore()
        for p in range(1, NDEV):
            pl.semaphore_signal(barrier, inc=1, device_id=_flipped(place, p)[0], device_id_type=MESH)
        pl.semaphore_wait(barrier, NDEV - 1)
        local = []
        for k in range(n):
            src, dst = _ends(kinds[k], in_refs[k], land_refs[k], me, me)
            local.append(pltpu.make_async_copy(src, dst, local_sems.at[k]))
            local[-1].start()
        sends = []
        if all(kind == "gather" for kind in kinds):
            for p in (1, 2, 4, 6):
                for k in range(n):
                    sends.append(_remote(kinds[k], in_refs[k], land_refs[k], send_sems, recv_sems, k, p, place, me, False))
                    sends[-1].start()
            sibling_pos, _ = _flipped(place, 1)
            for f in (2, 4, 6):
                _, origin = _flipped(place, f)
                for k in range(n):
                    _remote(kinds[k], in_refs[k], land_refs[k], send_sems, recv_sems, k, f, place, me, True).wait_recv()
                    sem = k * (NDEV - 1) + f
                    sends.append(
                        pltpu.make_async_remote_copy(
                            src_ref=land_refs[k].at[origin], dst_ref=land_refs[k].at[origin], send_sem=send_sems.at[sem],
                            recv_sem=recv_sems.at[sem], device_id=sibling_pos, device_id_type=MESH,
                        )
                    )
                    sends[-1].start()
            for p in (1, 3, 5, 7):
                for k in range(n):
                    _remote(kinds[k], in_refs[k], land_refs[k], send_sems, recv_sems, k, p, place, me, True).wait_recv()
        else:
            for p in range(1, NDEV):
                for k in range(n):
                    sends.append(_remote(kinds[k], in_refs[k], land_refs[k], send_sems, recv_sems, k, p, place, me, False))
                    sends[-1].start()
            for p in range(1, NDEV):
                for k in range(n):
                    _remote(kinds[k], in_refs[k], land_refs[k], send_sems, recv_sems, k, p, place, me, True).wait_recv()
        for cp in sends:
            cp.wait_send()
        for cp in local:
            cp.wait()

    return pl.kernel(
        launch, out_type=tuple(lands), mesh=plsc.ScalarSubcoreMesh(axis_name="sequencer", num_cores=1), name=name,
        scratch_types=(
            pltpu.SemaphoreType.DMA((n * (NDEV - 1),)), pltpu.SemaphoreType.DMA((n * (NDEV - 1),)), pltpu.SemaphoreType.DMA((n,)),
        ),
        compiler_params=pltpu.CompilerParams(collective_id=collective_id),
    )(*ins)


def _dep_spec(n_axes):
    return BS((8, 128), (lambda i: (0, 0)) if n_axes == 1 else (lambda i, j: (0, 0)))


def _norm_fwd(name, h, gammas):
    t, dn = h.shape
    ng = gammas.shape[0]
    tm = _tile(t)

    def body(h_ref, g_ref, *outs):
        hv = h_ref[...]
        hh = hv * _rms_scale(hv)
        for i, o_ref in enumerate(outs):
            o_ref[...] = (hh * g_ref[i : i + 1, :]).astype(BF16)

    row = BS((tm, dn), lambda i: (i, 0))
    return pl.pallas_call(
        body, name=name, grid=(t // tm,),
        in_specs=[row, BS((ng, dn), lambda i: (0, 0))],
        out_specs=[row] * ng, out_shape=[SDS((t, dn), BF16)] * ng,
        compiler_params=_cparams(1),
    )(h, gammas)


def _ffn_in(name, n, w_in, layer):
    t, dn = n.shape
    tm = _tile(t, 1024)

    def body(n_ref, wg_ref, wu_ref, gu_ref, a_ref):
        xv = n_ref[...]
        g = _dot(xv, wg_ref[...])
        u = _dot(xv, wu_ref[...])
        gu_ref[0] = g.astype(BF16)
        gu_ref[1] = u.astype(BF16)
        a_ref[...] = (g * jax.nn.sigmoid(g) * u).astype(BF16)

    return pl.pallas_call(
        body, name=name, grid=(NJ, t // tm),
        in_specs=[
            BS((tm, dn), lambda j, i: (i, 0)),
            BS((None, None, dn, FB), lambda j, i: (layer, j, 0, 0)),
            BS((None, None, dn, FB), lambda j, i: (layer, j + NJ, 0, 0)),
        ],
        out_specs=[BS((None, 2, tm, FB), lambda j, i: (j, 0, i, 0)), BS((None, tm, FB), lambda j, i: (j, i, 0))],
        out_shape=[SDS((NJ, 2, t, FB), BF16), SDS((NJ, t, FB), BF16)],
        compiler_params=_cparams(2),
    )(n, w_in, w_in)


def _mm_res_norm(name, a, w, layer, h_in, gammas, scale):
    nk, t, kb = a.shape
    dn = w.shape[-1]
    ng = 0 if gammas is None else gammas.shape[0]
    tm = _tile(t)

    def body(*refs):
        a_ref, w_ref, h_ref = refs[:3]
        g_ref = refs[3] if ng else None
        outs = refs[3 + (1 if ng else 0) :]
        acc = _dot(a_ref[0], w_ref[0])
        for k in range(1, nk):
            acc += _dot(a_ref[k], w_ref[k])
        ho = h_ref[...] + scale * acc
        outs[0][...] = ho
        if ng:
            hh = ho * _rms_scale(ho)
            for i in range(ng):
                outs[1 + i][...] = (hh * g_ref[i : i + 1, :]).astype(BF16)

    row = BS((tm, dn), lambda i: (i, 0))
    in_specs = [BS((nk, tm, kb), lambda i: (0, i, 0)), BS((None, nk, kb, dn), lambda i: (layer, 0, 0, 0)), row]
    args = [a, w, h_in]
    if ng:
        in_specs.append(BS((ng, dn), lambda i: (0, 0)))
        args.append(gammas)
    return pl.pallas_call(
        body, name=name, grid=(t // tm,),
        in_specs=in_specs,
        out_specs=[row] * (1 + ng), out_shape=[SDS((t, dn), F32)] + [SDS((t, dn), BF16)] * ng,
        compiler_params=_cparams(1),
    )(*args)


def _qkv_proj(name, hn, w_qkv):
    t, dn = hn.shape
    wb = w_qkv.shape[-1]
    per = wb // 128
    tm = _tile(t)

    def body(x_ref, w_ref, o_ref):
        xv = x_ref[...]
        for j in range(NDEV):
            yv = _dot(xv, w_ref[j]).astype(BF16)
            for i in range(per):
                n = per * j + i
                o_ref[n // 8, :, (n % 8) * 128 : (n % 8 + 1) * 128] = yv[:, i * 128 : (i + 1) * 128]

    return pl.pallas_call(
        body, name=name, grid=(t // tm,),
        in_specs=[BS((tm, dn), lambda i: (i, 0)), BS((NDEV, dn, wb), lambda i: (0, 0, 0))],
        out_specs=BS((3, tm, dn), lambda i: (0, i, 0)), out_shape=SDS((3, t, dn), BF16),
        compiler_params=_cparams(1),
    )(hn, w_qkv)


def _rel_onehot(i):
    r = lax.broadcasted_iota(jnp.int32, (NREL_PAD, BAND), 0)
    j = lax.broadcasted_iota(jnp.int32, (NREL_PAD, BAND), 1)
    idx = jnp.clip(PAD + i - j, -MAX_REL, MAX_REL) + MAX_REL
    return (idx == r).astype(F32)


def _rel_bias_fwd(table):
    def body(t_ref, o_ref):
        i8 = pl.program_id(0)
        for ii in range(8):
            o_ref[:, ii, :] = _dot_exact(t_ref[...], _rel_onehot(i8 * 8 + ii))

    return pl.pallas_call(
        body, name="rel_bias_fwd", grid=(CHUNK // 8,),
        in_specs=[BS((HEADS_A, NREL_PAD), lambda i: (0, 0))],
        out_specs=BS((HEADS_A, 8, BAND), lambda i: (0, i, 0)), out_shape=SDS((HEADS_A, CHUNK, BAND), F32),
        compiler_params=_cparams(1),
    )(table)


def _rel_bias_bwd(dbias):
    def body(d_ref, o_ref):
        i8 = pl.program_id(0)
        acc = jnp.zeros((HEADS_A, NREL_PAD), F32)
        for ii in range(8):
            acc += _dot_exact(d_ref[:, ii, :], _rel_onehot(i8 * 8 + ii), transposed=True)

        @pl.when(i8 == 0)
        def _():
            o_ref[...] = acc

        @pl.when(i8 > 0)
        def _():
            o_ref[...] += acc

    return pl.pallas_call(
        body, name="rel_bias_bwd", grid=(CHUNK // 8,),
        in_specs=[BS((HEADS_A, 8, BAND), lambda i: (0, i, 0))],
        out_specs=BS((HEADS_A, NREL_PAD), lambda i: (0, 0)), out_shape=SDS((HEADS_A, NREL_PAD), F32),
        compiler_params=_cparams(1),
    )(dbias)


def _window_bias(bias):
    b = bias.reshape(HEADS_A // 2, 2, CHUNK, BAND)
    per_chunk = [
        jnp.pad(b, ((0, 0), (0, 0), (0, 0), (cc * CHUNK, WINDOW - BAND - cc * CHUNK)), constant_values=NEG_INF)
        for cc in range(CHUNKS_PER_STEP)
    ]
    return jnp.stack(per_chunk, axis=1).reshape(HEADS_A // 2, STEP_ROWS, WINDOW)


def _window_bias_bwd(dwin):
    d = dwin.reshape(HEADS_A // 2, CHUNKS_PER_STEP, 2, CHUNK, WINDOW)
    return sum(d[:, cc, :, :, cc * CHUNK : cc * CHUNK + BAND] for cc in range(CHUNKS_PER_STEP)).reshape(HEADS_A, CHUNK, BAND)


def _step_rows(xs, lane):
    parts = []
    for cc in range(CHUNKS_PER_STEP):
        xc = xs[cc * CHUNK : (cc + 1) * CHUNK]
        parts.append(jnp.where(lane < 64, xc, jnp.zeros_like(xc)))
        parts.append(jnp.where(lane >= 64, xc, jnp.zeros_like(xc)))
    return jnp.concatenate(parts, axis=0)


def _pair_rows(ys, lane):
    parts = []
    for cc in range(CHUNKS_PER_STEP):
        y0 = ys[(2 * cc) * CHUNK : (2 * cc + 1) * CHUNK]
        y1 = ys[(2 * cc + 1) * CHUNK : (2 * cc + 2) * CHUNK]
        parts.append(jnp.where(lane < 64, y0, y1))
    return jnp.concatenate(parts, axis=0)


def _window_probs(q_rows, kwin, bias_win, first_key):
    s = _dot_nt(q_rows, kwin) * (CHUNK ** -0.5) + bias_win
    col = lax.broadcasted_iota(jnp.int32, s.shape, 1)
    s = jnp.where(col >= first_key, s, NEG_INF)
    e = jnp.exp(s - jnp.max(s, axis=-1, keepdims=True))
    return e / jnp.sum(e, axis=-1, keepdims=True)


def _attn_a_fwd(qkv3, bias_win, bl, seq):
    t, dn = qkv3.shape[1:]
    npair = dn // 128
    step = CHUNKS_PER_STEP * CHUNK

    def body(q_ref, k_ref, v_ref, b_ref, o_ref, kpad, vpad):
        kpad[0:PAD, :] = jnp.zeros((PAD, 128), BF16)
        vpad[0:PAD, :] = jnp.zeros((PAD, 128), BF16)
        kpad[PAD:, :] = k_ref[...]
        vpad[PAD:, :] = v_ref[...]
        lane = lax.broadcasted_iota(jnp.int32, (CHUNK, 128), 1)

        def chunks(it, carry):
            r0 = pl.multiple_of(it * step, step)
            q_rows = _step_rows(q_ref[pl.ds(r0, step), :], lane)
            p = _window_probs(q_rows, kpad[pl.ds(r0, WINDOW), :], b_ref[...], PAD - r0)
            o_rows = _dot(p.astype(BF16), vpad[pl.ds(r0, WINDOW), :])
            o_ref[pl.ds(r0, step), :] = _pair_rows(o_rows, lane).astype(BF16)
            return carry

        lax.fori_loop(0, seq // step, chunks, 0, unroll=2)

    return pl.pallas_call(
        body, name="attn_a_fwd", grid=(bl, npair),
        in_specs=[
            BS((None, seq, 128), lambda b, h: (0, b, h)),
            BS((None, seq, 128), lambda b, h: (1, b, h)),
            BS((None, seq, 128), lambda b, h: (2, b, h)),
            BS((None, STEP_ROWS, WINDOW), lambda b, h: (h, 0, 0)),
        ],
        out_specs=BS((seq, 128), lambda b, h: (b, h)), out_shape=SDS((t, dn), BF16),
        scratch_shapes=[pltpu.VMEM((PAD + seq, 128), BF16), pltpu.VMEM((PAD + seq, 128), BF16)],
        compiler_params=_cparams(2),
    )(qkv3, qkv3, qkv3, bias_win)


def _attn_a_bwd(qkv3, do, bias_win, bl, seq):
    t, dn = qkv3.shape[1:]
    npair = dn // 128
    step = CHUNKS_PER_STEP * CHUNK

    def body(q_ref, k_ref, v_ref, do_ref, b_ref, dqkv_ref, db_ref, kpad, vpad, dkacc, dvacc):
        b = pl.program_id(1)
        kpad[0:PAD, :] = jnp.zeros((PAD, 128), BF16)
        vpad[0:PAD, :] = jnp.zeros((PAD, 128), BF16)
        kpad[PAD:, :] = k_ref[...]
        vpad[PAD:, :] = v_ref[...]
        dkacc[...] = jnp.zeros_like(dkacc)
        dvacc[...] = jnp.zeros_like(dvacc)

        @pl.when(b == 0)
        def _():
            db_ref[...] = jnp.zeros_like(db_ref)

        lane = lax.broadcasted_iota(jnp.int32, (CHUNK, 128), 1)

        def chunks(it, carry):
            r0 = pl.multiple_of(it * step, step)
            q_rows = _step_rows(q_ref[pl.ds(r0, step), :], lane)
            do_rows = _step_rows(do_ref[pl.ds(r0, step), :], lane)
            kwin = kpad[pl.ds(r0, WINDOW), :]
            vwin = vpad[pl.ds(r0, WINDOW), :]
            p = _window_probs(q_rows, kwin, b_ref[...], PAD - r0)
            dp = _dot_nt(do_rows, vwin)
            ds = p * (dp - jnp.sum(p * dp, axis=-1, keepdims=True))
            db_ref[...] += ds
            dsb = (ds * (CHUNK ** -0.5)).astype(BF16)
            dqkv_ref[0, pl.ds(r0, step), :] = _pair_rows(_dot(dsb, kwin), lane).astype(BF16)
            dkacc[pl.ds(r0, WINDOW), :] += _dot_tn(dsb, q_rows)
            dvacc[pl.ds(r0, WINDOW), :] += _dot_tn(p.astype(BF16), do_rows)
            return carry

        lax.fori_loop(0, seq // step, chunks, 0, unroll=2)
        dqkv_ref[1] = dkacc[PAD:, :].astype(BF16)
        dqkv_ref[2] = dvacc[PAD:, :].astype(BF16)

    return pl.pallas_call(
        body, name="attn_a_bwd", grid=(npair, bl),
        in_specs=[
            BS((None, seq, 128), lambda h, b: (0, b, h)),
            BS((None, seq, 128), lambda h, b: (1, b, h)),
            BS((None, seq, 128), lambda h, b: (2, b, h)),
            BS((seq, 128), lambda h, b: (b, h)),
            BS((None, STEP_ROWS, WINDOW), lambda h, b: (h, 0, 0)),
        ],
        out_specs=[BS((3, seq, 128), lambda h, b: (0, b, h)), BS((None, STEP_ROWS, WINDOW), lambda h, b: (h, 0, 0))],
        out_shape=[SDS((3, t, dn), BF16), SDS((HEADS_A // 2, STEP_ROWS, WINDOW), F32)],
        scratch_shapes=[
            pltpu.VMEM((PAD + seq, 128), BF16), pltpu.VMEM((PAD + seq, 128), BF16),
            pltpu.VMEM((PAD + seq, 128), F32), pltpu.VMEM((PAD + seq, 128), F32),
        ],
        compiler_params=_cparams(2),
    )(qkv3, qkv3, qkv3, do, bias_win)


def _rope_tables(seq):
    half = ROPE // 2
    freqs = ROPE_THETA ** (-jnp.arange(half, dtype=F32) / half)
    ang = jnp.arange(seq, dtype=F32)[:, None] * freqs[None, :]
    cos, sin = jnp.cos(ang), jnp.sin(ang)
    c64 = jnp.concatenate([cos, cos], axis=1)
    s64 = jnp.concatenate([-sin, sin], axis=1)
    c192 = jnp.concatenate([jnp.ones((seq, NOPE), F32), c64], axis=1)
    s192 = jnp.concatenate([jnp.zeros((seq, NOPE), F32), s64], axis=1)
    p64 = np.zeros((ROPE, ROPE), np.float32)
    for col in range(ROPE):
        p64[(col + half) % ROPE, col] = 1.0
    p192 = np.zeros((QK_B, QK_B), np.float32)
    p192[NOPE:, NOPE:] = p64
    return c64, s64, jnp.asarray(p64), c192, s192, jnp.asarray(p192)


def _rope(xv, cos, sin_signed, swap):
    return xv * cos + _dot_exact(xv, swap) * sin_signed


def _rope_bwd(dy, cos, sin_signed, swap):
    return dy * cos + _dot_exact(dy * sin_signed, swap)


def _q_down(hn, w_dq, q_norm):
    t, dn = hn.shape
    ql = w_dq.shape[1]
    tm = _tile(t)

    def body(x_ref, w_ref, g_ref, pre_ref, cq_ref):
        pre = _dot(x_ref[...], w_ref[...])
        pre_ref[...] = pre
        cq_ref[...] = (pre * _rms_scale(pre) * g_ref[...]).astype(BF16)

    return pl.pallas_call(
        body, name="q_down", grid=(t // tm,),
        in_specs=[BS((tm, dn), lambda i: (i, 0)), BS((dn, ql), lambda i: (0, 0)), BS((1, ql), lambda i: (0, 0))],
        out_specs=[BS((tm, ql), lambda i: (i, 0))] * 2, out_shape=[SDS((t, ql), F32), SDS((t, ql), BF16)],
        compiler_params=_cparams(1),
    )(hn, w_dq, q_norm)


def _q_up(cq, w_uq, c192, s192, p192, seq):
    t, ql = cq.shape
    tm = _tile(min(seq, 512), min(seq, 512))
    nseq = seq // tm

    def body(x_ref, w_ref, c_ref, s_ref, p_ref, o_ref):
        xv = x_ref[...]
        for h in range(HEADS_B):
            o_ref[h] = _rope(_dot(xv, w_ref[h]), c_ref[...], s_ref[...], p_ref[...]).astype(BF16)

    pos = BS((tm, QK_B), lambda i: (i % nseq, 0))
    return pl.pallas_call(
        body, name="q_up", grid=(t // tm,),
        in_specs=[
            BS((tm, ql), lambda i: (i, 0)), BS((HEADS_B, ql, QK_B), lambda i: (0, 0, 0)), pos, pos,
            BS((QK_B, QK_B), lambda i: (0, 0)),
        ],
        out_specs=BS((HEADS_B, tm, QK_B), lambda i: (0, i, 0)), out_shape=SDS((HEADS_B, t, QK_B), BF16),
        compiler_params=_cparams(1),
    )(cq, w_uq, c192, s192, p192)


def _kv_down(hk, w_down, latent_norm, c64, s64, p64, seq):
    t, dn = hk.shape
    wd = w_down.shape[1]
    tm = _tile(min(seq, 512), min(seq, 512))
    nseq = seq // tm

    def body(x_ref, w_ref, g_ref, c_ref, s_ref, p_ref, ckr_ref, ckv_ref, kr_ref):
        ckr = _dot(x_ref[...], w_ref[...])
        ckr_ref[...] = ckr
        lat = ckr[:, :KV_LORA]
        ckv_ref[...] = (lat * _rms_scale(lat) * g_ref[...]).astype(BF16)
        kr_ref[...] = _rope(ckr[:, KV_LORA:], c_ref[...], s_ref[...], p_ref[...]).astype(BF16)

    pos = BS((tm, ROPE), lambda i: (i % nseq, 0))
    return pl.pallas_call(
        body, name="kv_down", grid=(t // tm,),
        in_specs=[
            BS((tm, dn), lambda i: (i, 0)), BS((dn, wd), lambda i: (0, 0)), BS((1, KV_LORA), lambda i: (0, 0)), pos, pos,
            BS((ROPE, ROPE), lambda i: (0, 0)),
        ],
        out_specs=[BS((tm, wd), lambda i: (i, 0)), BS((tm, KV_LORA), lambda i: (i, 0)), BS((tm, ROPE), lambda i: (i, 0))],
        out_shape=[SDS((t, wd), F32), SDS((t, KV_LORA), BF16), SDS((t, ROPE), BF16)],
        compiler_params=_cparams(1),
    )(hk, w_down, latent_norm, c64, s64, p64)


def _kv_up(ckv, w_up):
    t, kl = ckv.shape
    hb = w_up.shape[-1]
    tm = _tile(t)

    def body(x_ref, w_ref, o_ref):
        xv = x_ref[...]
        for h in range(HEADS_B):
            o_ref[:, h * hb : (h + 1) * hb] = _dot(xv, w_ref[h]).astype(BF16)

    return pl.pallas_call(
        body, name="kv_up", grid=(t // tm,),
        in_specs=[BS((tm, kl), lambda i: (i, 0)), BS((HEADS_B, kl, hb), lambda i: (0, 0, 0))],
        out_specs=BS((tm, HEADS_B * hb), lambda i: (i, 0)), out_shape=SDS((t, HEADS_B * hb), BF16),
        compiler_params=_cparams(1),
    )(ckv, w_up)


def _mla_diagonal_mask(tq):
    rows = lax.broadcasted_iota(jnp.int32, (tq, tq), 0)
    cols = lax.broadcasted_iota(jnp.int32, (tq, tq), 1)
    return jnp.where(jnp.right_shift(cols, 6) <= jnp.right_shift(rows, 6), 0.0, NEG_INF)


def _mla_probs(qi, kcat, diagonal):
    s = _dot_nt(qi, kcat) * (QK_B ** -0.5)
    tq, n_keys = s.shape
    own = s[:, n_keys - tq :] + diagonal
    s = own if n_keys == tq else jnp.concatenate([s[:, : n_keys - tq], own], axis=1)
    e = jnp.exp(s - jnp.max(s, axis=-1, keepdims=True))
    return e / jnp.sum(e, axis=-1, keepdims=True)


def _mla_fwd(q, kv, kr, bl, seq):
    t = kv.shape[0]
    tq = min(MLA_TQ, seq)

    def body(q_ref, kn_ref, v_ref, kr_ref, o_ref):
        kcat = jnp.concatenate([kn_ref[...], kr_ref[...]], axis=1)
        vv = v_ref[...]
        diagonal = _mla_diagonal_mask(tq)
        for i in range(seq // tq):
            n_keys = (i + 1) * tq
            p = _mla_probs(q_ref[i * tq : (i + 1) * tq, :], kcat[:n_keys], diagonal)
            o_ref[i * tq : (i + 1) * tq, :] = _dot(p.astype(BF16), vv[:n_keys]).astype(BF16)

    return pl.pallas_call(
        body, name="mla_fwd", grid=(bl, HEADS_B),
        in_specs=[
            BS((None, seq, QK_B), lambda b, h: (h, b, 0)),
            BS((seq, NOPE), lambda b, h: (b, 2 * h)),
            BS((seq, V_DIM), lambda b, h: (b, 2 * h + 1)),
            BS((seq, ROPE), lambda b, h: (b, 0)),
        ],
        out_specs=BS((seq, V_DIM), lambda b, h: (b, h)), out_shape=SDS((t, HEADS_B * V_DIM), BF16),
        compiler_params=_cparams(2),
    )(q, kv, kv, kr)


def _mla_bwd(q, kv, kr, do, c192, s192, p192, bl, seq):
    t = kv.shape[0]
    tq = min(MLA_TQ, seq)

    def body(q_ref, kn_ref, v_ref, kr_ref, do_ref, c_ref, s_ref, p_ref, dq_ref, dkv_ref, dkr_ref, dkacc, dvacc):
        h = pl.program_id(1)
        kcat = jnp.concatenate([kn_ref[...], kr_ref[...]], axis=1)
        vv = v_ref[...]
        dkacc[...] = jnp.zeros_like(dkacc)
        dvacc[...] = jnp.zeros_like(dvacc)
        diagonal = _mla_diagonal_mask(tq)
        for i in range(seq // tq):
            n_keys = (i + 1) * tq
            rows = slice(i * tq, (i + 1) * tq)
            qi = q_ref[rows, :]
            doi = do_ref[rows, :]
            p = _mla_probs(qi, kcat[:n_keys], diagonal)
            dp = _dot_nt(doi, vv[:n_keys])
            ds = p * (dp - jnp.sum(p * dp, axis=-1, keepdims=True))
            dsb = (ds * (QK_B ** -0.5)).astype(BF16)
            dq = _dot(dsb, kcat[:n_keys])
            dq_ref[rows, :] = _rope_bwd(dq, c_ref[rows, :], s_ref[rows, :], p_ref[...]).astype(BF16)
            dkacc[0:n_keys, :] += _dot_tn(dsb, qi)
            dvacc[0:n_keys, :] += _dot_tn(p.astype(BF16), doi)
        dk = dkacc[...]
        dkv_ref[:, :NOPE] = dk[:, :NOPE].astype(BF16)
        dkv_ref[:, NOPE:] = dvacc[...].astype(BF16)

        @pl.when(h == 0)
        def _():
            dkr_ref[...] = dk[:, NOPE:]

        @pl.when(h > 0)
        def _():
            dkr_ref[...] += dk[:, NOPE:]

    return pl.pallas_call(
        body, name="mla_bwd", grid=(bl, HEADS_B),
        in_specs=[
            BS((None, seq, QK_B), lambda b, h: (h, b, 0)),
            BS((seq, NOPE), lambda b, h: (b, 2 * h)),
            BS((seq, V_DIM), lambda b, h: (b, 2 * h + 1)),
            BS((seq, ROPE), lambda b, h: (b, 0)),
            BS((seq, V_DIM), lambda b, h: (b, h)),
            BS((seq, QK_B), lambda b, h: (0, 0)),
            BS((seq, QK_B), lambda b, h: (0, 0)),
            BS((QK_B, QK_B), lambda b, h: (0, 0)),
        ],
        out_specs=[
            BS((None, seq, QK_B), lambda b, h: (h, b, 0)),
            BS((seq, NOPE + V_DIM), lambda b, h: (b, h)),
            BS((seq, ROPE), lambda b, h: (b, 0)),
        ],
        out_shape=[SDS((HEADS_B, t, QK_B), BF16), SDS((t, HEADS_B * (NOPE + V_DIM)), BF16), SDS((t, ROPE), F32)],
        scratch_shapes=[pltpu.VMEM((seq, QK_B), F32), pltpu.VMEM((seq, V_DIM), F32)],
        compiler_params=_cparams(2),
    )(q, kv, kv, kr, do, c192, s192, p192)


def _loss_final(h, target, gamma):
    t, dn = h.shape
    tm = _tile(t)
    nt = t // tm

    def body(h_ref, t_ref, g_ref, dh_ref, dhb_ref, dg_ref, loss_ref):
        i = pl.program_id(0)
        hv = h_ref[...]
        r = _rms_scale(hv)
        hh = hv * r
        gam = g_ref[...]
        err = hh * gam - t_ref[...]
        part = 0.5 * jnp.sum(jnp.mean(err * err, axis=-1, keepdims=True))

        @pl.when(i == 0)
        def _():
            loss_ref[...] = jnp.zeros_like(loss_ref)

        loss_ref[...] += part
        dy = err * (1.0 / dn)
        _acc_rows(dg_ref, dy * hh, i, nt)
        t1 = dy * gam
        dh = r * (t1 - hh * jnp.mean(t1 * hh, axis=-1, keepdims=True))
        dh_ref[...] = dh
        dhb_ref[...] = dh.astype(BF16)

    row = BS((tm, dn), lambda i: (i, 0))
    return pl.pallas_call(
        body, name="loss_final", grid=(nt,),
        in_specs=[row, row, BS((1, dn), lambda i: (0, 0))],
        out_specs=[row, row, BS((8, dn), lambda i: (0, 0)), BS((8, 128), lambda i: (0, 0))],
        out_shape=[SDS((t, dn), F32), SDS((t, dn), BF16), SDS((8, dn), F32), SDS((8, 128), F32)],
        compiler_params=_cparams(1),
    )(h, target, gamma)


def _ffn_bwd_in(name, dh, w_out, layer, gu, dep=None):
    t, dn = dh.shape
    tm = _tile(t, 1024)

    def body(dh_ref, w_ref, gu_ref, *rest):
        o_ref = rest[-1]
        da = 0.5 * _dot_nt(dh_ref[...], w_ref[...])
        g = gu_ref[0].astype(F32)
        u = gu_ref[1].astype(F32)
        sg = jax.nn.sigmoid(g)
        o_ref[0] = (da * u * (sg * (1.0 + g * (1.0 - sg)))).astype(BF16)
        o_ref[1] = (da * (g * sg)).astype(BF16)

    blk = BS((None, 2, tm, FB), lambda j, i: (j, 0, i, 0))
    deps = [] if dep is None else [dep]
    return pl.pallas_call(
        body, name=name, grid=(NJ, t // tm),
        in_specs=[BS((tm, dn), lambda j, i: (i, 0)), BS((None, None, FB, dn), lambda j, i: (layer, j, 0, 0)), blk]
        + [_dep_spec(2)] * len(deps),
        out_specs=blk, out_shape=SDS((NJ, 2, t, FB), BF16),
        compiler_params=_cparams(2),
    )(dh, w_out, gu, *deps)


def _mm_nt_plain(name, xf, w, dep=None):
    t, dn = xf.shape
    n = w.shape[0]
    tm = _tile(t)

    def body(x_ref, w_ref, *rest):
        rest[-1][...] = _dot_nt(x_ref[...], w_ref[...]).astype(BF16)

    deps = [] if dep is None else [dep]
    return pl.pallas_call(
        body, name=name, grid=(t // tm,),
        in_specs=[BS((tm, dn), lambda i: (i, 0)), BS((n, dn), lambda i: (0, 0))] + [_dep_spec(1)] * len(deps),
        out_specs=BS((tm, n), lambda i: (i, 0)), out_shape=SDS((t, n), BF16),
        compiler_params=_cparams(1),
    )(xf, w, *deps)


def _mm_tn(name, xa, x_spec, ya, y_spec, out_shape, out_spec, nj, scale=None):
    def body(x_ref, y_ref, o_ref):
        acc = _dot_tn(x_ref[...], y_ref[...])
        o_ref[...] = (acc if scale is None else scale * acc).astype(BF16)

    return pl.pallas_call(
        body, name=name, grid=(nj,),
        in_specs=[x_spec, y_spec], out_specs=out_spec, out_shape=SDS(out_shape, BF16),
        compiler_params=_cparams(1),
    )(xa, ya)


def _dw_qkv(hn, dqkv3, wb):
    t, dn = hn.shape
    per = wb // 128

    def body(x_ref, *refs):
        cols = [y_ref[...] for y_ref in refs[:per]]
        refs[per][...] = _dot_tn(x_ref[...], jnp.concatenate(cols, axis=1)).astype(BF16)

    def piece(k):
        return BS((None, t, 128), lambda j: ((per * j + k) // 8, 0, (per * j + k) % 8))

    return pl.pallas_call(
        body, name="dw_qkv", grid=(NDEV,),
        in_specs=[BS((t, dn), lambda j: (0, 0))] + [piece(k) for k in range(per)],
        out_specs=BS((None, dn, wb), lambda j: (j, 0, 0)), out_shape=SDS((NDEV, dn, wb), BF16),
        compiler_params=_cparams(1),
    )(hn, *([dqkv3] * per))


def _mm_nt_epi(name, ya, y_spec, wa, w_spec, nj, n_out, extra, out_shapes, out_specs, epilogue, tm, nt, mm_fn=None):
    n_extra = len(extra)
    n_outs = len(out_shapes)

    def body(*refs):
        y_ref, w_ref = refs[:2]
        ex = refs[2 : 2 + n_extra]
        outs = refs[2 + n_extra : 2 + n_extra + n_outs]
        i = pl.program_id(0)
        j = pl.program_id(1)
        part = _dot_nt(y_ref[...], w_ref[...]) if mm_fn is None else mm_fn(y_ref, w_ref)
        if nj == 1:
            epilogue(part, ex, outs, i, nt)
            return
        acc = refs[-1]

        @pl.when(j == 0)
        def _():
            acc[...] = part

        @pl.when(j > 0)
        def _():
            acc[...] += part

        @pl.when(j == nj - 1)
        def _():
            epilogue(acc[...], ex, outs, i, nt)

    return pl.pallas_call(
        body, name=name, grid=(nt, nj),
        in_specs=[y_spec, w_spec] + [spec for _, spec in extra],
        out_specs=out_specs, out_shape=out_shapes,
        scratch_shapes=[] if nj == 1 else [pltpu.VMEM((tm, n_out), F32)],
        compiler_params=_cparams(2),
    )(ya, wa, *[arr for arr, _ in extra])


def _norm_bwd(dn, hv, gam):
    r = _rms_scale(hv)
    hh = hv * r
    t1 = dn * gam
    return r * (t1 - hh * jnp.mean(t1 * hh, axis=-1, keepdims=True)), dn * hh


def _norm_bwd_epilogue(has_res, out_dtype):
    def epilogue(dn, ex, outs, i, nt):
        dh, dg_rows = _norm_bwd(dn, ex[0][...], ex[1][...])
        _acc_rows(outs[1], dg_rows, i, nt)
        if has_res:
            dh = dh + ex[2][...]
        outs[0][...] = dh.astype(out_dtype)
        if has_res:
            outs[2][...] = dh.astype(BF16)

    return epilogue


def _mm_nt_norm_bwd(name, ya, y_spec, wa, w_spec, nj, h, gamma, res, out_dtype, mm_fn=None, want_tm=512, dep=None):
    t, n = h.shape
    tm = _tile(t, want_tm)
    nt = t // tm
    row = BS((tm, n), lambda i, j: (i, 0))
    extra = [(h, row), (gamma, BS((1, n), lambda i, j: (0, 0)))]
    out_shapes = [SDS((t, n), out_dtype), SDS((8, n), F32)]
    out_specs = [row, BS((8, n), lambda i, j: (0, 0))]
    if res is not None:
        extra.append((res, row))
        out_shapes.append(SDS((t, n), BF16))
        out_specs.append(row)
    if dep is not None:
        extra.append((dep, _dep_spec(2)))
    return _mm_nt_epi(
        name, ya, y_spec, wa, w_spec, nj, n, extra, out_shapes, out_specs, _norm_bwd_epilogue(res is not None, out_dtype), tm, nt, mm_fn,
    )


def _dev_block(jj):
    return jj // 2 + NJ * (jj % 2)


def _ffn_dn_mm(y_ref, w_ref):
    acc = None
    for jj in range(2 * NJ):
        part = _dot_nt(y_ref[jj], w_ref[_dev_block(jj)])
        acc = part if acc is None else acc + part
    return acc


def _ffn_bwd(tag, dh, dhb, n_in, h_in, gamma, gu, a, w_in, w_out, more_grads, collective_id):
    t, dn = dh.shape
    dgu = _ffn_bwd_in(f"{tag}_bwd_in", dhb, w_out, 0, gu).reshape(2 * NJ, t, FB)
    dw_out = _mm_tn(
        f"{tag}_dw_out", a, BS((None, t, FB), lambda j: (j, 0, 0)), dhb, BS((t, dn), lambda j: (0, 0)),
        (NJ, FB, dn), BS((None, FB, dn), lambda j: (j, 0, 0)), NJ, scale=0.5,
    )
    dw_in = _mm_tn(
        f"{tag}_dw_in", n_in, BS((t, dn), lambda j: (0, 0)), dgu, BS((None, t, FB), lambda j: (j, 0, 0)),
        (NDEV, dn, FB), BS((None, dn, FB), lambda j: (_dev_block(j), 0, 0)), NDEV,
    )
    entries = [("scatter", dw_in), ("scatter", dw_out.reshape(NDEV, NJ * FB // NDEV, dn))] + [("scatter", g) for g in more_grads]
    landed = _exchange_sc(f"{tag}_reduce", entries, collective_id)
    tm = _tile(t, 256)
    dh_in, dgam, dhb_in = _mm_nt_norm_bwd(
        f"{tag}_dn", dgu, BS((2 * NJ, tm, FB), lambda i, j: (0, i, 0)),
        w_in, BS((None, NDEV, dn, FB), lambda i, j: (0, 0, 0, 0)), 1, h_in, gamma, dh, F32, mm_fn=_ffn_dn_mm, want_tm=256,
    )
    return dh_in, dhb_in, dgam, landed


def _heads_mm(y_ref, w_ref):
    acc = None
    for h in range(HEADS_B):
        part = _dot_nt(y_ref[h], w_ref[h])
        acc = part if acc is None else acc + part
    return acc


def _dqkv_mm(per):
    def mm(y_ref, w_ref):
        acc = None
        for j in range(NDEV):
            cols = [y_ref[(per * j + k) // 8, :, ((per * j + k) % 8) * 128 : ((per * j + k) % 8 + 1) * 128] for k in range(per)]
            part = _dot_nt(jnp.concatenate(cols, axis=1), w_ref[j])
            acc = part if acc is None else acc + part
        return acc

    return mm


def _kv_latent_bwd(dkv, w_up, ckr, latent_norm, dkr, c64, s64, p64, seq):
    t, wd = ckr.shape
    hb = w_up.shape[-1]
    tm = _tile(min(seq, 512), min(seq, 512))
    nt = t // tm
    nseq = seq // tm

    def epilogue(dn, ex, outs, i, nt_):
        dlat, dg_rows = _norm_bwd(dn, ex[0][...], ex[1][...])
        _acc_rows(outs[1], dg_rows, i, nt_)
        outs[0][:, :KV_LORA] = dlat.astype(BF16)
        outs[0][:, KV_LORA:] = _rope_bwd(ex[2][...], ex[3][...], ex[4][...], ex[5][...]).astype(BF16)

    pos = BS((tm, ROPE), lambda i, j: (i % nseq, 0))
    extra = [
        (ckr, BS((tm, KV_LORA), lambda i, j: (i, 0))), (latent_norm, BS((1, KV_LORA), lambda i, j: (0, 0))),
        (dkr, BS((tm, ROPE), lambda i, j: (i, 0))), (c64, pos), (s64, pos), (p64, BS((ROPE, ROPE), lambda i, j: (0, 0))),
    ]
    def heads_mm(y_ref, w_ref):
        acc = None
        for h in range(HEADS_B):
            part = _dot_nt(y_ref[:, h * hb : (h + 1) * hb], w_ref[h])
            acc = part if acc is None else acc + part
        return acc

    return _mm_nt_epi(
        "kv_latent_bwd", dkv, BS((tm, HEADS_B * hb), lambda i, j: (i, 0)), w_up, BS((HEADS_B, KV_LORA, hb), lambda i, j: (0, 0, 0)),
        1, KV_LORA, extra, [SDS((t, wd), BF16), SDS((8, KV_LORA), F32)],
        [BS((tm, wd), lambda i, j: (i, 0)), BS((8, KV_LORA), lambda i, j: (0, 0))], epilogue, tm, nt, heads_mm,
    )


def _adamw(name, parts, w, m, v):
    n_layers, rows, cols = w.shape
    tr = max(d for d in range(8, min(rows, 256) + 1, 8) if rows % d == 0)
    nb = rows // tr

    def body(*refs):
        p_refs = refs[:n_layers]
        w_ref, m_ref, v_ref, g_ref, d_ref, nm_ref, nv_ref = refs[n_layers : n_layers + 7]
        layer = pl.program_id(0)
        for lp in range(n_layers):

            @pl.when(layer == lp)
            def _():
                g = p_refs[lp][0].astype(F32)
                for k in range(1, NDEV):
                    g = g + p_refs[lp][k].astype(F32)
                g_ref[...] = g

        g = g_ref[...]
        nm = ADAM_B1 * m_ref[...] + (1.0 - ADAM_B1) * g
        nv = ADAM_B2 * v_ref[...] + (1.0 - ADAM_B2) * (g * g)
        nm_ref[...] = nm
        nv_ref[...] = nv
        m_hat = nm / (1.0 - ADAM_B1 ** ADAM_STEP)
        v_hat = nv / (1.0 - ADAM_B2 ** ADAM_STEP)
        d_ref[...] = -ADAM_LR * (m_hat / (jnp.sqrt(v_hat) + ADAM_EPS) + ADAM_WD * w_ref[...])

    def part_spec(lp):
        return BS((NDEV, tr, cols), lambda l, i: (0, jnp.where(l == lp, i, jnp.where(l < lp, 0, nb - 1)), 0))

    row = BS((None, tr, cols), lambda l, i: (l, i, 0))
    return pl.pallas_call(
        body, name=name, grid=(n_layers, nb),
        in_specs=[part_spec(lp) for lp in range(n_layers)] + [row, row, row],
        out_specs=[row] * 4, out_shape=[SDS(w.shape, F32)] * 4,
        compiler_params=_cparams(2),
    )(*parts, w, m, v)


def _pack_small(ffn1_norm, mix_norm, ffn2_norm, kv_norm, final_norm, q_norm, latent_norm, rel_bias, last_row):
    dn = ffn1_norm.shape[-1]

    def rows_of(a, n_rows):
        flat = a.reshape(-1)
        return jnp.pad(flat, (0, n_rows * dn - flat.shape[0])).reshape(n_rows, dn)

    return jnp.concatenate(
        [
            ffn1_norm.reshape(2, dn), mix_norm.reshape(2, dn), ffn2_norm.reshape(2, dn), kv_norm.reshape(1, dn),
            final_norm.reshape(1, dn), rows_of(q_norm, 1), rows_of(latent_norm, 1), rows_of(rel_bias, 5), rows_of(last_row, 1),
        ],
        axis=0,
    )


def _unpack_small(pack):
    dn = pack.shape[-1]
    return dict(
        ffn1_norm=pack[0:2], mix_norm=pack[2:4], ffn2_norm=pack[4:6], kv_norm=pack[6], final_norm=pack[7],
        b_q_norm=pack[8, :Q_LORA].reshape(1, Q_LORA), kv_latent_norm=pack[9, :KV_LORA],
        a_rel_bias=pack[10:15].reshape(-1)[: HEADS_A * NREL].reshape(1, HEADS_A, NREL), last=pack[15],
    )


def kernel(x, ffn1_norm, ffn1_w_in, ffn1_w_out, mix_norm, ffn2_norm, ffn2_w_in, ffn2_w_out, a_w_qkv, a_rel_bias, a_w_o, kv_norm, kv_w_down, kv_latent_norm, kv_w_up, b_w_dq, b_q_norm, b_w_uq, b_w_o, final_norm, loss_target, m_ffn1_norm, m_ffn1_w_in, m_ffn1_w_out, m_mix_norm, m_ffn2_norm, m_ffn2_w_in, m_ffn2_w_out, m_a_w_qkv, m_a_rel_bias, m_a_w_o, m_kv_norm, m_kv_w_down, m_kv_latent_norm, m_kv_w_up, m_b_w_dq, m_b_q_norm, m_b_w_uq, m_b_w_o, m_final_norm, v_ffn1_norm, v_ffn1_w_in, v_ffn1_w_out, v_mix_norm, v_ffn2_norm, v_ffn2_w_in, v_ffn2_w_out, v_a_w_qkv, v_a_rel_bias, v_a_w_o, v_kv_norm, v_kv_w_down, v_kv_latent_norm, v_kv_w_up, v_b_w_dq, v_b_q_norm, v_b_w_uq, v_b_w_o, v_final_norm):
    bl, seq, dn = x.shape
    t = bl * seq
    tm = _tile(t)
    nt = t // tm
    x2 = x.reshape(t, dn)
    target2 = loss_target.reshape(t, dn)

    def gathered(*ws):
        return [("gather", w.astype(BF16)) for w in ws]

    groups = [
        gathered(ffn1_w_in[0]), gathered(ffn1_w_out[0]), gathered(a_w_qkv[0], a_w_o[0]), gathered(ffn2_w_in[0], ffn2_w_out[0]),
        gathered(kv_w_down, kv_w_up), gathered(ffn1_w_in[1], ffn1_w_out[1]), gathered(b_w_dq[0], b_w_uq[0], b_w_o[0]),
        gathered(ffn2_w_in[1], ffn2_w_out[1]),
    ]
    ag = [_exchange_sc(f"gather_{k}", group, GATHER_IDS[k]) for k, group in enumerate(groups)]

    def as_w_in(w):
        return w.reshape(1, NDEV, dn, FB)

    def as_w_out(w):
        return w.reshape(1, NJ, FB, dn)

    c64, s64, p64, c192, s192, p192 = _rope_tables(seq)
    q_norm = b_q_norm.reshape(1, Q_LORA)
    latent_norm = kv_latent_norm.reshape(1, KV_LORA)
    bias = _window_bias(_rel_bias_fwd(jnp.pad(a_rel_bias[0], ((0, 0), (0, NREL_PAD - NREL)))))

    h0, h1, h2, n1, hn, n2, gu1, gu2, a1, a2, w_in1, w_in2, w_out1, w_out2 = ([None, None] for _ in range(14))
    h0[0] = x2
    (n1[0],) = _norm_fwd("norm_x", x2, ffn1_norm[0:1])
    w_in1[0] = as_w_in(ag[0][0])
    gu1[0], a1[0] = _ffn_in("ffn1_in_0", n1[0], w_in1[0], 0)
    w_out1[0] = as_w_out(ag[1][0])
    h1[0], hn[0] = _mm_res_norm("ffn1_out_0", a1[0], w_out1[0], 0, h0[0], mix_norm[0:1], 0.5)
    w_qkv, w_o_a = ag[2]
    qkv_wb = w_qkv.shape[-1]
    w_o_a = w_o_a.reshape(1, 1, dn, dn)
    qkv3 = _qkv_proj("qkv_proj", hn[0], w_qkv)
    o_a = _attn_a_fwd(qkv3, bias, bl, seq)
    h2[0], n2[0] = _mm_res_norm("attn_a_out", o_a.reshape(1, t, dn), w_o_a, 0, h1[0], ffn2_norm[0:1], 1.0)
    w_in2[0], w_out2[0] = as_w_in(ag[3][0]), as_w_out(ag[3][1])
    gu2[0], a2[0] = _ffn_in("ffn2_in_0", n2[0], w_in2[0], 0)
    h0[1], hk, n1[1] = _mm_res_norm(
        "ffn2_out_0", a2[0], w_out2[0], 0, h2[0], jnp.concatenate([kv_norm.reshape(1, dn), ffn1_norm[1:2]], axis=0), 0.5
    )
    w_down, w_up = ag[4]
    w_down = w_down.reshape(dn, KV_LORA + ROPE)
    ckr, ckv, kr = _kv_down(hk, w_down, latent_norm, c64, s64, p64, seq)
    kv = _kv_up(ckv, w_up)
    w_in1[1], w_out1[1] = as_w_in(ag[5][0]), as_w_out(ag[5][1])
    gu1[1], a1[1] = _ffn_in("ffn1_in_1", n1[1], w_in1[1], 0)
    h1[1], hn[1] = _mm_res_norm("ffn1_out_1", a1[1], w_out1[1], 0, h0[1], mix_norm[1:2], 0.5)
    w_dq, w_uq, w_o_b = ag[6]
    w_dq = w_dq.reshape(dn, Q_LORA)
    w_o_b = w_o_b.reshape(1, 1, dn, dn)
    cq_pre, cq = _q_down(hn[1], w_dq, q_norm)
    q = _q_up(cq, w_uq, c192, s192, p192, seq)
    o_b = _mla_fwd(q, kv, kr, bl, seq)
    h2[1], n2[1] = _mm_res_norm("attn_b_out", o_b.reshape(1, t, dn), w_o_b, 0, h1[1], ffn2_norm[1:2], 1.0)
    w_in2[1], w_out2[1] = as_w_in(ag[7][0]), as_w_out(ag[7][1])
    gu2[1], a2[1] = _ffn_in("ffn2_in_1", n2[1], w_in2[1], 0)
    (h_last,) = _mm_res_norm("ffn2_out_1", a2[1], w_out2[1], 0, h2[1], None, 0.5)
    dh, dhb, dg_final, loss_part = _loss_final(h_last, target2, final_norm.reshape(1, dn))

    dg_ffn1, dg_mix, dg_ffn2, rs_ffn1, rs_ffn2 = ([None, None] for _ in range(5))

    def whole(rows, cols):
        return BS((rows, cols), lambda j: (0, 0))

    def dw_rows(name, xa, ya):
        n = ya.shape[1]
        return _mm_tn(name, xa, whole(t, dn), ya, whole(t, n), (dn, n), whole(dn, n), 1).reshape(NDEV, dn // NDEV, n)

    dh, dhb, dg_ffn2[1], rs_ffn2[1] = _ffn_bwd(
        "ffn2_1", dh, dhb, n2[1], h2[1], ffn2_norm[1:2], gu2[1], a2[1], w_in2[1], w_out2[1], [], REDUCE_IDS[0]
    )
    do_b = _mm_nt_plain("attn_b_do", dhb, w_o_b.reshape(dn, dn))
    dw_o_b = dw_rows("attn_b_dwo", o_b, dhb)
    dq_pre, dkv, dkr = _mla_bwd(q, kv, kr, do_b, c192, s192, p192, bl, seq)
    dw_uq = _mm_tn(
        "dw_uq", cq, whole(t, Q_LORA), dq_pre, BS((None, t, QK_B), lambda j: (j, 0, 0)),
        (HEADS_B, Q_LORA, QK_B), BS((None, Q_LORA, QK_B), lambda j: (j, 0, 0)), HEADS_B,
    )
    dcq_pre, dg_q = _mm_nt_norm_bwd(
        "dcq", dq_pre, BS((HEADS_B, tm, QK_B), lambda i, j: (0, i, 0)), w_uq, BS((HEADS_B, Q_LORA, QK_B), lambda i, j: (0, 0, 0)),
        1, cq_pre, q_norm, None, BF16, mm_fn=_heads_mm,
    )
    dw_dq = dw_rows("dw_dq", hn[1], dcq_pre)
    dh, dg_mix[1], dhb = _mm_nt_norm_bwd(
        "dhn_b", dcq_pre, BS((tm, Q_LORA), lambda i, j: (i, 0)), w_dq, BS((dn, Q_LORA), lambda i, j: (0, 0)),
        1, h1[1], mix_norm[1:2], dh, F32,
    )
    dh, dhb, dg_ffn1[1], rs_ffn1[1] = _ffn_bwd(
        "ffn1_1", dh, dhb, n1[1], h0[1], ffn1_norm[1:2], gu1[1], a1[1], w_in1[1], w_out1[1], [dw_o_b, dw_uq, dw_dq], REDUCE_IDS[1]
    )
    dw_up = _mm_tn(
        "dw_up", ckv, whole(t, KV_LORA), dkv, BS((t, NOPE + V_DIM), lambda j: (0, j)),
        (HEADS_B, KV_LORA, NOPE + V_DIM), BS((None, KV_LORA, NOPE + V_DIM), lambda j: (j, 0, 0)), HEADS_B,
    )
    dckr, dg_latent = _kv_latent_bwd(dkv, w_up, ckr, latent_norm, dkr, c64, s64, p64, seq)
    dw_down = dw_rows("dw_down", hk, dckr)
    dh, dg_kv, dhb = _mm_nt_norm_bwd(
        "dhk", dckr, BS((tm, KV_LORA + ROPE), lambda i, j: (i, 0)), w_down, BS((dn, KV_LORA + ROPE), lambda i, j: (0, 0)),
        1, h0[1], kv_norm.reshape(1, dn), dh, F32,
    )
    dh, dhb, dg_ffn2[0], rs_ffn2[0] = _ffn_bwd(
        "ffn2_0", dh, dhb, n2[0], h2[0], ffn2_norm[0:1], gu2[0], a2[0], w_in2[0], w_out2[0], [dw_up, dw_down], REDUCE_IDS[2]
    )
    do_a = _mm_nt_plain("attn_a_do", dhb, w_o_a.reshape(dn, dn))
    dw_o_a = dw_rows("attn_a_dwo", o_a, dhb)
    dqkv3, dbias = _attn_a_bwd(qkv3, do_a, bias, bl, seq)
    dw_qkv = _dw_qkv(hn[0], dqkv3, qkv_wb)
    dh, dg_mix[0], dhb = _mm_nt_norm_bwd(
        "dhn_a", dqkv3, BS((3, tm, dn), lambda i, j: (0, i, 0)), w_qkv, BS((NDEV, dn, qkv_wb), lambda i, j: (0, 0, 0)),
        1, h1[0], mix_norm[0:1], dh, F32, mm_fn=_dqkv_mm(qkv_wb // 128),
    )
    r_o_a, r_qkv = _exchange_sc("attn_a_reduce", [("scatter", dw_o_a), ("scatter", dw_qkv)], REDUCE_IDS[3])
    dh, dhb, dg_ffn1[0], rs_ffn1[0] = _ffn_bwd(
        "ffn1_0", dh, dhb, n1[0], h0[0], ffn1_norm[0:1], gu1[0], a1[0], w_in1[0], w_out1[0], [], REDUCE_IDS[4]
    )
    grad_x = dh.reshape(bl, seq, dn)
    dtable = _rel_bias_bwd(_window_bias_bwd(dbias))[:, :NREL]

    small = _pack_small(
        jnp.stack([dg_ffn1[0][0], dg_ffn1[1][0]]), jnp.stack([dg_mix[0][0], dg_mix[1][0]]), jnp.stack([dg_ffn2[0][0], dg_ffn2[1][0]]),
        dg_kv[0], dg_final[0], dg_q[0], dg_latent[0], dtable, loss_part[0],
    )
    (r_small,) = _exchange("gather_small_grads", [("gather", small)])

    def update(name, parts, w, m, v):
        shape3 = (len(parts),) + w.shape[-2:]
        parts = [p.reshape((NDEV,) + shape3[1:]) for p in parts]
        outs = _adamw(name, parts, w.reshape(shape3), m.reshape(shape3), v.reshape(shape3))
        return [o.reshape(w.shape) for o in outs]

    res = {}
    r_in2_1, r_out2_1 = rs_ffn2[1]
    r_in1_1, r_out1_1, r_o_b, r_uq, r_dq = rs_ffn1[1]
    r_in2_0, r_out2_0, r_up, r_down = rs_ffn2[0]
    r_in1_0, r_out1_0 = rs_ffn1[0]
    res["ffn2_w_in"] = update("adamw_ffn2_w_in", [r_in2_0, r_in2_1], ffn2_w_in, m_ffn2_w_in, v_ffn2_w_in)
    res["ffn2_w_out"] = update("adamw_ffn2_w_out", [r_out2_0, r_out2_1], ffn2_w_out, m_ffn2_w_out, v_ffn2_w_out)
    res["kv_w_down"] = update("adamw_kv_w_down", [r_down], kv_w_down, m_kv_w_down, v_kv_w_down)
    res["kv_w_up"] = update("adamw_kv_w_up", [r_up], kv_w_up, m_kv_w_up, v_kv_w_up)
    res["b_w_dq"] = update("adamw_b_w_dq", [r_dq], b_w_dq, m_b_w_dq, v_b_w_dq)
    res["b_w_uq"] = update("adamw_b_w_uq", [r_uq], b_w_uq, m_b_w_uq, v_b_w_uq)
    res["b_w_o"] = update("adamw_b_w_o", [r_o_b], b_w_o, m_b_w_o, v_b_w_o)
    res["a_w_qkv"] = update("adamw_a_w_qkv", [r_qkv], a_w_qkv, m_a_w_qkv, v_a_w_qkv)
    res["a_w_o"] = update("adamw_a_w_o", [r_o_a], a_w_o, m_a_w_o, v_a_w_o)
    res["ffn1_w_in"] = update("adamw_ffn1_w_in", [r_in1_0, r_in1_1], ffn1_w_in, m_ffn1_w_in, v_ffn1_w_in)
    res["ffn1_w_out"] = update("adamw_ffn1_w_out", [r_out1_0, r_out1_1], ffn1_w_out, m_ffn1_w_out, v_ffn1_w_out)
    zero_row = jnp.zeros((dn,), F32)
    packs = [
        _pack_small(f1, mx, f2, kvn, fin, qn, lat, rel, zero_row)
        for f1, mx, f2, kvn, fin, qn, lat, rel in (
            (ffn1_norm, mix_norm, ffn2_norm, kv_norm, final_norm, b_q_norm, kv_latent_norm, a_rel_bias),
            (m_ffn1_norm, m_mix_norm, m_ffn2_norm, m_kv_norm, m_final_norm, m_b_q_norm, m_kv_latent_norm, m_a_rel_bias),
            (v_ffn1_norm, v_mix_norm, v_ffn2_norm, v_kv_norm, v_final_norm, v_b_q_norm, v_kv_latent_norm, v_a_rel_bias),
        )
    ]
    small_out = [_unpack_small(o[0]) for o in _adamw("adamw_small", [r_small], *[p[None] for p in packs])]
    for name in ("ffn1_norm", "mix_norm", "ffn2_norm", "a_rel_bias", "kv_norm", "kv_latent_norm", "b_q_norm", "final_norm"):
        res[name] = [so[name] for so in small_out]
    loss = small_out[0]["last"][0]

    order = [
        "ffn1_norm", "ffn1_w_in", "ffn1_w_out", "mix_norm", "ffn2_norm", "ffn2_w_in", "ffn2_w_out", "a_w_qkv", "a_rel_bias",
        "a_w_o", "kv_norm", "kv_w_down", "kv_latent_norm", "kv_w_up", "b_w_dq", "b_q_norm", "b_w_uq", "b_w_o", "final_norm",
    ]
    return (loss, grad_x, *[res[n][0] for n in order], *[res[n][1] for n in order], *[res[n][2] for n in order], *[res[n][3] for n in order])
```

```python
import jax
import jax.numpy as jnp
import numpy as np
from jax import lax
from jax.experimental import pallas as pl
from jax.experimental.pallas import tpu as pltpu
from jax.experimental.pallas import tpu_sc as plsc

NDEV = 8
D_MODEL = 1024
D_FF = 2816
FB = 2 * D_FF // NDEV
NJ = D_FF // FB
CHUNK = 64
LEFT_CHUNKS = 8
PAD = LEFT_CHUNKS * CHUNK
BAND = PAD + CHUNK
CHUNKS_PER_STEP = 4
WINDOW = PAD + CHUNKS_PER_STEP * CHUNK
STEP_ROWS = CHUNKS_PER_STEP * 2 * CHUNK
MAX_REL = 128
NREL = 2 * MAX_REL + 1
NREL_PAD = 384
HEADS_A = 16
HEADS_B = 8
NOPE = 128
ROPE = 64
QK_B = NOPE + ROPE
V_DIM = 128
Q_LORA = 768
KV_LORA = 256
ROPE_THETA = 10000.0
EPS = 1e-6
NEG_INF = -1e30
MLA_TQ = 256
ADAM_LR = 0.001
ADAM_B1 = 0.9
ADAM_B2 = 0.999
ADAM_EPS = 1e-08
ADAM_WD = 0.01
ADAM_STEP = 10
PACK_ROWS = 16
GATHER_IDS = tuple(range(1, 9))
REDUCE_IDS = tuple(range(9, 14))
VMEM_LIMIT_BYTES = 56 * 1024 * 1024

F32 = jnp.float32
BF16 = jnp.bfloat16
SDS = jax.ShapeDtypeStruct
BS = pl.BlockSpec
MESH = pl.DeviceIdType.MESH


def _cparams(n_axes):
    return pltpu.CompilerParams(dimension_semantics=("arbitrary",) * n_axes, vmem_limit_bytes=VMEM_LIMIT_BYTES)


def _tile(t, want=512):
    return want if t % want == 0 else t


def _dot(a, b):
    return jnp.dot(a, b, preferred_element_type=F32)


def _dot_nt(a, b):
    return lax.dot_general(a, b, (((1,), (1,)), ((), ())), preferred_element_type=F32)


def _dot_tn(a, b):
    return lax.dot_general(a, b, (((0,), (0,)), ((), ())), preferred_element_type=F32)


def _split3(a):
    hi = a.astype(BF16)
    rest = a - hi.astype(F32)
    mid = rest.astype(BF16)
    return hi, mid, (rest - mid.astype(F32)).astype(BF16)


def _dot_exact(a, onehot, transposed=False):
    ob = onehot.astype(BF16)
    dot = _dot_nt if transposed else _dot
    hi, mid, lo = _split3(a)
    return dot(hi, ob) + dot(mid, ob) + dot(lo, ob)


def _rms_scale(h):
    return lax.rsqrt(jnp.mean(h * h, axis=-1, keepdims=True) + EPS)


def _acc_rows(ref, val, step, n_steps):
    part = val.reshape(val.shape[0] // 8, 8, val.shape[1]).sum(axis=0)

    @pl.when(step == 0)
    def _():
        ref[...] = part

    @pl.when(step > 0)
    def _():
        ref[...] += part

    @pl.when(step == n_steps - 1)
    def _():
        ref[...] = jnp.broadcast_to(jnp.sum(ref[...], axis=0, keepdims=True), ref.shape)


def _exchange_plan(entries):
    ins = [e[1] for e in entries]
    kinds = [e[0] for e in entries]
    lands = [SDS((NDEV,) + a.shape if k == "gather" else a.shape, a.dtype) for k, a in zip(kinds, ins)]
    return ins, lands, kinds


def _mesh_place():
    x, y, c = lax.axis_index("x"), lax.axis_index("y"), lax.axis_index("c")
    return (x, y, c), 4 * x + 2 * y + c


def _flipped(place, p):
    x, y, c = place
    px = 1 - x if p & 4 else x
    py = 1 - y if p & 2 else y
    pc = 1 - c if p & 1 else c
    return (px, py, pc), 4 * px + 2 * py + pc


def _ends(kind, src_ref, land_ref, origin, target):
    if kind == "gather":
        return src_ref, land_ref.at[origin]
    return src_ref.at[target], land_ref.at[origin]


def _remote(kind, src_ref, land_ref, send_sems, recv_sems, k, p, place, me, arriving):
    peer_pos, peer = _flipped(place, p)
    src, dst = _ends(kind, src_ref, land_ref, me, peer)
    if arriving:
        dst = _ends(kind, src_ref, land_ref, peer, me)[1]
    sem = k * (NDEV - 1) + p - 1
    return pltpu.make_async_remote_copy(
        src_ref=src, dst_ref=dst, send_sem=send_sems.at[sem], recv_sem=recv_sems.at[sem], device_id=peer_pos, device_id_type=MESH,
    )


def _exchange(name, entries):
    ins, lands, kinds = _exchange_plan(entries)
    n = len(ins)

    def body(*refs):
        in_refs, land_refs = refs[:n], refs[n : 2 * n]
        send_sems, recv_sems, local_sems = refs[2 * n :]
        place, me = _mesh_place()
        local = []
        for k in range(n):
            src, dst = _ends(kinds[k], in_refs[k], land_refs[k], me, me)
            local.append(pltpu.make_async_copy(src, dst, local_sems.at[k]))
            local[-1].start()
        sends = []
        for p in range(1, NDEV):
            for k in range(n):
                sends.append(_remote(kinds[k], in_refs[k], land_refs[k], send_sems, recv_sems, k, p, place, me, False))
                sends[-1].start()
        for p in range(1, NDEV):
            for k in range(n):
                _remote(kinds[k], in_refs[k], land_refs[k], send_sems, recv_sems, k, p, place, me, True).wait_recv()
        for cp in sends:
            cp.wait_send()
        for cp in local:
            cp.wait()

    any_spec = BS(memory_space=pl.ANY)
    return pl.pallas_call(
        body, name=name, out_shape=lands, in_specs=[any_spec] * n, out_specs=[any_spec] * n,
        scratch_shapes=[
            pltpu.SemaphoreType.DMA((n * (NDEV - 1),)), pltpu.SemaphoreType.DMA((n * (NDEV - 1),)), pltpu.SemaphoreType.DMA((n,)),
        ],
    )(*ins)


def _exchange_sc(name, entries, collective_id):
    ins, lands, kinds = _exchange_plan(entries)
    n = len(ins)

    def launch(*refs):
        in_refs, land_refs = refs[:n], refs[n : 2 * n]
        send_sems, recv_sems, local_sems = refs[2 * n :]
        place, me = _mesh_place()
        barrier = pltpu.get_barrier_semaphore()
        for p in range(1, NDEV):
            pl.semaphore_signal(barrier, inc=1, device_id=_flipped(place, p)[0], device_id_type=MESH)
        pl.semaphore_wait(barrier, NDEV - 1)
        local = []
        for k in range(n):
            src, dst = _ends(kinds[k], in_refs[k], land_refs[k], me, me)
            local.append(pltpu.make_async_copy(src, dst, local_sems.at[k]))
            local[-1].start()
        sends = []
        if all(kind == "gather" for kind in kinds):
            for p in (1, 2, 4, 6):
                for k in range(n):
                    sends.append(_remote(kinds[k], in_refs[k], land_refs[k], send_sems, recv_sems, k, p, place, me, False))
                    sends[-1].start()
            sibling_pos, _ = _flipped(place, 1)
            for f in (2, 4, 6):
                _, origin = _flipped(place, f)
                for k in range(n):
                    _remote(kinds[k], in_refs[k], land_refs[k], send_sems, recv_sems, k, f, place, me, True).wait_recv()
                    sem = k * (NDEV - 1) + f
                    sends.append(
                        pltpu.make_async_remote_copy(
                            src_ref=land_refs[k].at[origin], dst_ref=land_refs[k].at[origin], send_sem=send_sems.at[sem],
                            recv_sem=recv_sems.at[sem], device_id=sibling_pos, device_id_type=MESH,
                        )
                    )
                    sends[-1].start()
            for p in (1, 3, 5, 7):
                for k in range(n):
                    _remote(kinds[k], in_refs[k], land_refs[k], send_sems, recv_sems, k, p, place, me, True).wait_recv()
        else:
            for p in range(1, NDEV):
                for k in range(n):
                    sends.append(_remote(kinds[k], in_refs[k], land_refs[k], send_sems, recv_sems, k, p, place, me, False))
                    sends[-1].start()
            for p in range(1, NDEV):
                for k in range(n):
                    _remote(kinds[k], in_refs[k], land_refs[k], send_sems, recv_sems, k, p, place, me, True).wait_recv()
        for cp in sends:
            cp.wait_send()
        for cp in local:
            cp.wait()

    return pl.kernel(
        launch, out_type=tuple(lands), mesh=plsc.ScalarSubcoreMesh(axis_name="sequencer", num_cores=1), name=name,
        scratch_types=(
            pltpu.SemaphoreType.DMA((n * (NDEV - 1),)), pltpu.SemaphoreType.DMA((n * (NDEV - 1),)), pltpu.SemaphoreType.DMA((n,)),
        ),
        compiler_params=pltpu.CompilerParams(collective_id=collective_id),
    )(*ins)


def _norm_fwd(name, h, gammas):
    t, dn = h.shape
    ng = gammas.shape[0]
    tm = _tile(t)

    def body(h_ref, g_ref, *outs):
        hv = h_ref[...]
        hh = hv * _rms_scale(hv)
        for i, o_ref in enumerate(outs):
            o_ref[...] = (hh * g_ref[i : i + 1, :]).astype(BF16)

    row = BS((tm, dn), lambda i: (i, 0))
    return pl.pallas_call(
        body, name=name, grid=(t // tm,),
        in_specs=[row, BS((ng, dn), lambda i: (0, 0))],
        out_specs=[row] * ng, out_shape=[SDS((t, dn), BF16)] * ng,
        compiler_params=_cparams(1),
    )(h, gammas)


def _ffn_in(name, n, w_in, layer):
    t, dn = n.shape
    tm = _tile(t, 1024)

    def body(n_ref, wg_ref, wu_ref, gu_ref, a_ref):
        xv = n_ref[...]
        g = _dot(xv, wg_ref[...])
        u = _dot(xv, wu_ref[...])
        gu_ref[0] = g.astype(BF16)
        gu_ref[1] = u.astype(BF16)
        a_ref[...] = (g * jax.nn.sigmoid(g) * u).astype(BF16)

    return pl.pallas_call(
        body, name=name, grid=(NJ, t // tm),
        in_specs=[
            BS((tm, dn), lambda j, i: (i, 0)),
            BS((None, None, dn, FB), lambda j, i: (layer, j, 0, 0)),
            BS((None, None, dn, FB), lambda j, i: (layer, j + NJ, 0, 0)),
        ],
        out_specs=[BS((None, 2, tm, FB), lambda j, i: (j, 0, i, 0)), BS((None, tm, FB), lambda j, i: (j, i, 0))],
        out_shape=[SDS((NJ, 2, t, FB), BF16), SDS((NJ, t, FB), BF16)],
        compiler_params=_cparams(2),
    )(n, w_in, w_in)


def _mm_res_norm(name, a, w, layer, h_in, gammas, scale):
    nk, t, kb = a.shape
    dn = w.shape[-1]
    ng = 0 if gammas is None else gammas.shape[0]
    tm = _tile(t)

    def body(*refs):
        a_ref, w_ref, h_ref = refs[:3]
        g_ref = refs[3] if ng else None
        outs = refs[3 + (1 if ng else 0) :]
        acc = _dot(a_ref[0], w_ref[0])
        for k in range(1, nk):
            acc += _dot(a_ref[k], w_ref[k])
        ho = h_ref[...] + scale * acc
        outs[0][...] = ho
        if ng:
            hh = ho * _rms_scale(ho)
            for i in range(ng):
                outs[1 + i][...] = (hh * g_ref[i : i + 1, :]).astype(BF16)

    row = BS((tm, dn), lambda i: (i, 0))
    in_specs = [BS((nk, tm, kb), lambda i: (0, i, 0)), BS((None, nk, kb, dn), lambda i: (layer, 0, 0, 0)), row]
    args = [a, w, h_in]
    if ng:
        in_specs.append(BS((ng, dn), lambda i: (0, 0)))
        args.append(gammas)
    return pl.pallas_call(
        body, name=name, grid=(t // tm,),
        in_specs=in_specs,
        out_specs=[row] * (1 + ng), out_shape=[SDS((t, dn), F32)] + [SDS((t, dn), BF16)] * ng,
        compiler_params=_cparams(1),
    )(*args)


def _qkv_proj(name, hn, w_qkv):
    t, dn = hn.shape
    wb = w_qkv.shape[-1]
    per = wb // 128
    tm = _tile(t)

    def body(x_ref, w_ref, o_ref):
        xv = x_ref[...]
        for j in range(NDEV):
            yv = _dot(xv, w_ref[j]).astype(BF16)
            for i in range(per):
                n = per * j + i
                o_ref[n // 8, :, (n % 8) * 128 : (n % 8 + 1) * 128] = yv[:, i * 128 : (i + 1) * 128]

    return pl.pallas_call(
        body, name=name, grid=(t // tm,),
        in_specs=[BS((tm, dn), lambda i: (i, 0)), BS((NDEV, dn, wb), lambda i: (0, 0, 0))],
        out_specs=BS((3, tm, dn), lambda i: (0, i, 0)), out_shape=SDS((3, t, dn), BF16),
        compiler_params=_cparams(1),
    )(hn, w_qkv)


def _rel_onehot(i):
    r = lax.broadcasted_iota(jnp.int32, (NREL_PAD, BAND), 0)
    j = lax.broadcasted_iota(jnp.int32, (NREL_PAD, BAND), 1)
    idx = jnp.clip(PAD + i - j, -MAX_REL, MAX_REL) + MAX_REL
    return (idx == r).astype(F32)


def _rel_bias_fwd(table):
    def body(t_ref, o_ref):
        i8 = pl.program_id(0)
        for ii in range(8):
            o_ref[:, ii, :] = _dot_exact(t_ref[...], _rel_onehot(i8 * 8 + ii))

    return pl.pallas_call(
        body, name="rel_bias_fwd", grid=(CHUNK // 8,),
        in_specs=[BS((HEADS_A, NREL_PAD), lambda i: (0, 0))],
        out_specs=BS((HEADS_A, 8, BAND), lambda i: (0, i, 0)), out_shape=SDS((HEADS_A, CHUNK, BAND), F32),
        compiler_params=_cparams(1),
    )(table)


def _rel_bias_bwd(dbias):
    def body(d_ref, o_ref):
        i8 = pl.program_id(0)
        acc = jnp.zeros((HEADS_A, NREL_PAD), F32)
        for ii in range(8):
            acc += _dot_exact(d_ref[:, ii, :], _rel_onehot(i8 * 8 + ii), transposed=True)

        @pl.when(i8 == 0)
        def _():
            o_ref[...] = acc

        @pl.when(i8 > 0)
        def _():
            o_ref[...] += acc

    return pl.pallas_call(
        body, name="rel_bias_bwd", grid=(CHUNK // 8,),
        in_specs=[BS((HEADS_A, 8, BAND), lambda i: (0, i, 0))],
        out_specs=BS((HEADS_A, NREL_PAD), lambda i: (0, 0)), out_shape=SDS((HEADS_A, NREL_PAD), F32),
        compiler_params=_cparams(1),
    )(dbias)


def _window_bias(bias):
    b = bias.reshape(HEADS_A // 2, 2, CHUNK, BAND)
    per_chunk = [
        jnp.pad(b, ((0, 0), (0, 0), (0, 0), (cc * CHUNK, WINDOW - BAND - cc * CHUNK)), constant_values=NEG_INF)
        for cc in range(CHUNKS_PER_STEP)
    ]
    return jnp.stack(per_chunk, axis=1).reshape(HEADS_A // 2, STEP_ROWS, WINDOW)


def _window_bias_bwd(dwin):
    d = dwin.reshape(HEADS_A // 2, CHUNKS_PER_STEP, 2, CHUNK, WINDOW)
    return sum(d[:, cc, :, :, cc * CHUNK : cc * CHUNK + BAND] for cc in range(CHUNKS_PER_STEP)).reshape(HEADS_A, CHUNK, BAND)


def _step_rows(xs, lane):
    parts = []
    for cc in range(CHUNKS_PER_STEP):
        xc = xs[cc * CHUNK : (cc + 1) * CHUNK]
        parts.append(jnp.where(lane < 64, xc, jnp.zeros_like(xc)))
        parts.append(jnp.where(lane >= 64, xc, jnp.zeros_like(xc)))
    return jnp.concatenate(parts, axis=0)


def _pair_rows(ys, lane):
    parts = []
    for cc in range(CHUNKS_PER_STEP):
        y0 = ys[(2 * cc) * CHUNK : (2 * cc + 1) * CHUNK]
        y1 = ys[(2 * cc + 1) * CHUNK : (2 * cc + 2) * CHUNK]
        parts.append(jnp.where(lane < 64, y0, y1))
    return jnp.concatenate(parts, axis=0)


def _window_probs(q_rows, kwin, bias_win, first_key):
    s = _dot_nt(q_rows, kwin) * (CHUNK ** -0.5) + bias_win
    col = lax.broadcasted_iota(jnp.int32, s.shape, 1)
    s = jnp.where(col >= first_key, s, NEG_INF)
    e = jnp.exp(s - jnp.max(s, axis=-1, keepdims=True))
    return e / jnp.sum(e, axis=-1, keepdims=True)


def _attn_a_fwd(qkv3, bias_win, bl, seq):
    t, dn = qkv3.shape[1:]
    npair = dn // 128
    step = CHUNKS_PER_STEP * CHUNK

    def body(q_ref, k_ref, v_ref, b_ref, o_ref, kpad, vpad):
        kpad[0:PAD, :] = jnp.zeros((PAD, 128), BF16)
        vpad[0:PAD, :] = jnp.zeros((PAD, 128), BF16)
        kpad[PAD:, :] = k_ref[...]
        vpad[PAD:, :] = v_ref[...]
        lane = lax.broadcasted_iota(jnp.int32, (CHUNK, 128), 1)

        def chunks(it, carry):
            r0 = pl.multiple_of(it * step, step)
            q_rows = _step_rows(q_ref[pl.ds(r0, step), :], lane)
            p = _window_probs(q_rows, kpad[pl.ds(r0, WINDOW), :], b_ref[...], PAD - r0)
            o_rows = _dot(p.astype(BF16), vpad[pl.ds(r0, WINDOW), :])
            o_ref[pl.ds(r0, step), :] = _pair_rows(o_rows, lane).astype(BF16)
            return carry

        lax.fori_loop(0, seq // step, chunks, 0, unroll=2)

    return pl.pallas_call(
        body, name="attn_a_fwd", grid=(bl, npair),
        in_specs=[
            BS((None, seq, 128), lambda b, h: (0, b, h)),
            BS((None, seq, 128), lambda b, h: (1, b, h)),
            BS((None, seq, 128), lambda b, h: (2, b, h)),
            BS((None, STEP_ROWS, WINDOW), lambda b, h: (h, 0, 0)),
        ],
        out_specs=BS((seq, 128), lambda b, h: (b, h)), out_shape=SDS((t, dn), BF16),
        scratch_shapes=[pltpu.VMEM((PAD + seq, 128), BF16), pltpu.VMEM((PAD + seq, 128), BF16)],
        compiler_params=_cparams(2),
    )(qkv3, qkv3, qkv3, bias_win)


def _attn_a_bwd(qkv3, do, bias_win, bl, seq):
    t, dn = qkv3.shape[1:]
    npair = dn // 128
    step = CHUNKS_PER_STEP * CHUNK

    def body(q_ref, k_ref, v_ref, do_ref, b_ref, dqkv_ref, db_ref, kpad, vpad, dkacc, dvacc):
        b = pl.program_id(1)
        kpad[0:PAD, :] = jnp.zeros((PAD, 128), BF16)
        vpad[0:PAD, :] = jnp.zeros((PAD, 128), BF16)
        kpad[PAD:, :] = k_ref[...]
        vpad[PAD:, :] = v_ref[...]
        dkacc[...] = jnp.zeros_like(dkacc)
        dvacc[...] = jnp.zeros_like(dvacc)

        @pl.when(b == 0)
        def _():
            db_ref[...] = jnp.zeros_like(db_ref)

        lane = lax.broadcasted_iota(jnp.int32, (CHUNK, 128), 1)

        def chunks(it, carry):
            r0 = pl.multiple_of(it * step, step)
            q_rows = _step_rows(q_ref[pl.ds(r0, step), :], lane)
            do_rows = _step_rows(do_ref[pl.ds(r0, step), :], lane)
            kwin = kpad[pl.ds(r0, WINDOW), :]
            vwin = vpad[pl.ds(r0, WINDOW), :]
            p = _window_probs(q_rows, kwin, b_ref[...], PAD - r0)
            dp = _dot_nt(do_rows, vwin)
            ds = p * (dp - jnp.sum(p * dp, axis=-1, keepdims=True))
            db_ref[...] += ds
            dsb = (ds * (CHUNK ** -0.5)).astype(BF16)
            dqkv_ref[0, pl.ds(r0, step), :] = _pair_rows(_dot(dsb, kwin), lane).astype(BF16)
            dkacc[pl.ds(r0, WINDOW), :] += _dot_tn(dsb, q_rows)
            dvacc[pl.ds(r0, WINDOW), :] += _dot_tn(p.astype(BF16), do_rows)
            return carry

        lax.fori_loop(0, seq // step, chunks, 0, unroll=2)
        dqkv_ref[1] = dkacc[PAD:, :].astype(BF16)
        dqkv_ref[2] = dvacc[PAD:, :].astype(BF16)

    return pl.pallas_call(
        body, name="attn_a_bwd", grid=(npair, bl),
        in_specs=[
            BS((None, seq, 128), lambda h, b: (0, b, h)),
            BS((None, seq, 128), lambda h, b: (1, b, h)),
            BS((None, seq, 128), lambda h, b: (2, b, h)),
            BS((seq, 128), lambda h, b: (b, h)),
            BS((None, STEP_ROWS, WINDOW), lambda h, b: (h, 0, 0)),
        ],
        out_specs=[BS((3, seq, 128), lambda h, b: (0, b, h)), BS((None, STEP_ROWS, WINDOW), lambda h, b: (h, 0, 0))],
        out_shape=[SDS((3, t, dn), BF16), SDS((HEADS_A // 2, STEP_ROWS, WINDOW), F32)],
        scratch_shapes=[
            pltpu.VMEM((PAD + seq, 128), BF16), pltpu.VMEM((PAD + seq, 128), BF16),
            pltpu.VMEM((PAD + seq, 128), F32), pltpu.VMEM((PAD + seq, 128), F32),
        ],
        compiler_params=_cparams(2),
    )(qkv3, qkv3, qkv3, do, bias_win)


def _rope_tables(seq):
    half = ROPE // 2
    freqs = ROPE_THETA ** (-jnp.arange(half, dtype=F32) / half)
    ang = jnp.arange(seq, dtype=F32)[:, None] * freqs[None, :]
    cos, sin = jnp.cos(ang), jnp.sin(ang)
    c64 = jnp.concatenate([cos, cos], axis=1)
    s64 = jnp.concatenate([-sin, sin], axis=1)
    c192 = jnp.concatenate([jnp.ones((seq, NOPE), F32), c64], axis=1)
    s192 = jnp.concatenate([jnp.zeros((seq, NOPE), F32), s64], axis=1)
    p64 = np.zeros((ROPE, ROPE), np.float32)
    for col in range(ROPE):
        p64[(col + half) % ROPE, col] = 1.0
    p192 = np.zeros((QK_B, QK_B), np.float32)
    p192[NOPE:, NOPE:] = p64
    return c64, s64, jnp.asarray(p64), c192, s192, jnp.asarray(p192)


def _rope(xv, cos, sin_signed, swap):
    return xv * cos + _dot_exact(xv, swap) * sin_signed


def _rope_bwd(dy, cos, sin_signed, swap):
    return dy * cos + _dot_exact(dy * sin_signed, swap)


def _q_down(hn, w_dq, q_norm):
    t, dn = hn.shape
    ql = w_dq.shape[1]
    tm = _tile(t)

    def body(x_ref, w_ref, g_ref, pre_ref, cq_ref):
        pre = _dot(x_ref[...], w_ref[...])
        pre_ref[...] = pre
        cq_ref[...] = (pre * _rms_scale(pre) * g_ref[...]).astype(BF16)

    return pl.pallas_call(
        body, name="q_down", grid=(t // tm,),
        in_specs=[BS((tm, dn), lambda i: (i, 0)), BS((dn, ql), lambda i: (0, 0)), BS((1, ql), lambda i: (0, 0))],
        out_specs=[BS((tm, ql), lambda i: (i, 0))] * 2, out_shape=[SDS((t, ql), F32), SDS((t, ql), BF16)],
        compiler_params=_cparams(1),
    )(hn, w_dq, q_norm)


def _q_up(cq, w_uq, c192, s192, p192, seq):
    t, ql = cq.shape
    tm = _tile(min(seq, 512), min(seq, 512))
    nseq = seq // tm

    def body(x_ref, w_ref, c_ref, s_ref, p_ref, o_ref):
        xv = x_ref[...]
        for h in range(HEADS_B):
            o_ref[h] = _rope(_dot(xv, w_ref[h]), c_ref[...], s_ref[...], p_ref[...]).astype(BF16)

    pos = BS((tm, QK_B), lambda i: (i % nseq, 0))
    return pl.pallas_call(
        body, name="q_up", grid=(t // tm,),
        in_specs=[
            BS((tm, ql), lambda i: (i, 0)), BS((HEADS_B, ql, QK_B), lambda i: (0, 0, 0)), pos, pos,
            BS((QK_B, QK_B), lambda i: (0, 0)),
        ],
        out_specs=BS((HEADS_B, tm, QK_B), lambda i: (0, i, 0)), out_shape=SDS((HEADS_B, t, QK_B), BF16),
        compiler_params=_cparams(1),
    )(cq, w_uq, c192, s192, p192)


def _kv_down(hk, w_down, latent_norm, c64, s64, p64, seq):
    t, dn = hk.shape
    wd = w_down.shape[1]
    tm = _tile(min(seq, 512), min(seq, 512))
    nseq = seq // tm

    def body(x_ref, w_ref, g_ref, c_ref, s_ref, p_ref, ckr_ref, ckv_ref, kr_ref):
        ckr = _dot(x_ref[...], w_ref[...])
        ckr_ref[...] = ckr
        lat = ckr[:, :KV_LORA]
        ckv_ref[...] = (lat * _rms_scale(lat) * g_ref[...]).astype(BF16)
        kr_ref[...] = _rope(ckr[:, KV_LORA:], c_ref[...], s_ref[...], p_ref[...]).astype(BF16)

    pos = BS((tm, ROPE), lambda i: (i % nseq, 0))
    return pl.pallas_call(
        body, name="kv_down", grid=(t // tm,),
        in_specs=[
            BS((tm, dn), lambda i: (i, 0)), BS((dn, wd), lambda i: (0, 0)), BS((1, KV_LORA), lambda i: (0, 0)), pos, pos,
            BS((ROPE, ROPE), lambda i: (0, 0)),
        ],
        out_specs=[BS((tm, wd), lambda i: (i, 0)), BS((tm, KV_LORA), lambda i: (i, 0)), BS((tm, ROPE), lambda i: (i, 0))],
        out_shape=[SDS((t, wd), F32), SDS((t, KV_LORA), BF16), SDS((t, ROPE), BF16)],
        compiler_params=_cparams(1),
    )(hk, w_down, latent_norm, c64, s64, p64)


def _kv_up(ckv, w_up):
    t, kl = ckv.shape
    hb = w_up.shape[-1]
    tm = _tile(t)

    def body(x_ref, w_ref, o_ref):
        xv = x_ref[...]
        for h in range(HEADS_B):
            o_ref[:, h * hb : (h + 1) * hb] = _dot(xv, w_ref[h]).astype(BF16)

    return pl.pallas_call(
        body, name="kv_up", grid=(t // tm,),
        in_specs=[BS((tm, kl), lambda i: (i, 0)), BS((HEADS_B, kl, hb), lambda i: (0, 0, 0))],
        out_specs=BS((tm, HEADS_B * hb), lambda i: (i, 0)), out_shape=SDS((t, HEADS_B * hb), BF16),
        compiler_params=_cparams(1),
    )(ckv, w_up)


def _mla_diagonal_mask(tq):
    rows = lax.broadcasted_iota(jnp.int32, (tq, tq), 0)
    cols = lax.broadcasted_iota(jnp.int32, (tq, tq), 1)
    return jnp.where(jnp.right_shift(cols, 6) <= jnp.right_shift(rows, 6), 0.0, NEG_INF)


def _mla_probs(qi, kcat, diagonal):
    s = _dot_nt(qi, kcat) * (QK_B ** -0.5)
    tq, n_keys = s.shape
    own = s[:, n_keys - tq :] + diagonal
    s = own if n_keys == tq else jnp.concatenate([s[:, : n_keys - tq], own], axis=1)
    e = jnp.exp(s - jnp.max(s, axis=-1, keepdims=True))
    return e / jnp.sum(e, axis=-1, keepdims=True)


def _mla_fwd(q, kv, kr, bl, seq):
    t = kv.shape[0]
    tq = min(MLA_TQ, seq)

    def body(q_ref, kn_ref, v_ref, kr_ref, o_ref):
        kcat = jnp.concatenate([kn_ref[...], kr_ref[...]], axis=1)
        vv = v_ref[...]
        diagonal = _mla_diagonal_mask(tq)
        for i in range(seq // tq):
            n_keys = (i + 1) * tq
            p = _mla_probs(q_ref[i * tq : (i + 1) * tq, :], kcat[:n_keys], diagonal)
            o_ref[i * tq : (i + 1) * tq, :] = _dot(p.astype(BF16), vv[:n_keys]).astype(BF16)

    return pl.pallas_call(
        body, name="mla_fwd", grid=(bl, HEADS_B),
        in_specs=[
            BS((None, seq, QK_B), lambda b, h: (h, b, 0)),
            BS((seq, NOPE), lambda b, h: (b, 2 * h)),
            BS((seq, V_DIM), lambda b, h: (b, 2 * h + 1)),
            BS((seq, ROPE), lambda b, h: (b, 0)),
        ],
        out_specs=BS((seq, V_DIM), lambda b, h: (b, h)), out_shape=SDS((t, HEADS_B * V_DIM), BF16),
        compiler_params=_cparams(2),
    )(q, kv, kv, kr)


def _mla_bwd(q, kv, kr, do, c192, s192, p192, bl, seq):
    t = kv.shape[0]
    tq = min(MLA_TQ, seq)

    def body(q_ref, kn_ref, v_ref, kr_ref, do_ref, c_ref, s_ref, p_ref, dq_ref, dkv_ref, dkr_ref, dkacc, dvacc):
        h = pl.program_id(1)
        kcat = jnp.concatenate([kn_ref[...], kr_ref[...]], axis=1)
        vv = v_ref[...]
        dkacc[...] = jnp.zeros_like(dkacc)
        dvacc[...] = jnp.zeros_like(dvacc)
        diagonal = _mla_diagonal_mask(tq)
        for i in range(seq // tq):
            n_keys = (i + 1) * tq
            rows = slice(i * tq, (i + 1) * tq)
            qi = q_ref[rows, :]
            doi = do_ref[rows, :]
            p = _mla_probs(qi, kcat[:n_keys], diagonal)
            dp = _dot_nt(doi, vv[:n_keys])
            ds = p * (dp - jnp.sum(p * dp, axis=-1, keepdims=True))
            dsb = (ds * (QK_B ** -0.5)).astype(BF16)
            dq = _dot(dsb, kcat[:n_keys])
            dq_ref[rows, :] = _rope_bwd(dq, c_ref[rows, :], s_ref[rows, :], p_ref[...]).astype(BF16)
            dkacc[0:n_keys, :] += _dot_tn(dsb, qi)
            dvacc[0:n_keys, :] += _dot_tn(p.astype(BF16), doi)
        dk = dkacc[...]
        dkv_ref[:, :NOPE] = dk[:, :NOPE].astype(BF16)
        dkv_ref[:, NOPE:] = dvacc[...].astype(BF16)

        @pl.when(h == 0)
        def _():
            dkr_ref[...] = dk[:, NOPE:]

        @pl.when(h > 0)
        def _():
            dkr_ref[...] += dk[:, NOPE:]

    return pl.pallas_call(
        body, name="mla_bwd", grid=(bl, HEADS_B),
        in_specs=[
            BS((None, seq, QK_B), lambda b, h: (h, b, 0)),
            BS((seq, NOPE), lambda b, h: (b, 2 * h)),
            BS((seq, V_DIM), lambda b, h: (b, 2 * h + 1)),
            BS((seq, ROPE), lambda b, h: (b, 0)),
            BS((seq, V_DIM), lambda b, h: (b, h)),
            BS((seq, QK_B), lambda b, h: (0, 0)),
            BS((seq, QK_B), lambda b, h: (0, 0)),
            BS((QK_B, QK_B), lambda b, h: (0, 0)),
        ],
        out_specs=[
            BS((None, seq, QK_B), lambda b, h: (h, b, 0)),
            BS((seq, NOPE + V_DIM), lambda b, h: (b, h)),
            BS((seq, ROPE), lambda b, h: (b, 0)),
        ],
        out_shape=[SDS((HEADS_B, t, QK_B), BF16), SDS((t, HEADS_B * (NOPE + V_DIM)), BF16), SDS((t, ROPE), F32)],
        scratch_shapes=[pltpu.VMEM((seq, QK_B), F32), pltpu.VMEM((seq, V_DIM), F32)],
        compiler_params=_cparams(2),
    )(q, kv, kv, kr, do, c192, s192, p192)


def _loss_final(h, target, gamma):
    t, dn = h.shape
    tm = _tile(t)
    nt = t // tm

    def body(h_ref, t_ref, g_ref, dh_ref, dhb_ref, dg_ref, loss_ref):
        i = pl.program_id(0)
        hv = h_ref[...]
        r = _rms_scale(hv)
        hh = hv * r
        gam = g_ref[...]
        err = hh * gam - t_ref[...]
        part = 0.5 * jnp.sum(jnp.mean(err * err, axis=-1, keepdims=True))

        @pl.when(i == 0)
        def _():
            loss_ref[...] = jnp.zeros_like(loss_ref)

        loss_ref[...] += part
        dy = err * (1.0 / dn)
        _acc_rows(dg_ref, dy * hh, i, nt)
        t1 = dy * gam
        dh = r * (t1 - hh * jnp.mean(t1 * hh, axis=-1, keepdims=True))
        dh_ref[...] = dh
        dhb_ref[...] = dh.astype(BF16)

    row = BS((tm, dn), lambda i: (i, 0))
    return pl.pallas_call(
        body, name="loss_final", grid=(nt,),
        in_specs=[row, row, BS((1, dn), lambda i: (0, 0))],
        out_specs=[row, row, BS((8, dn), lambda i: (0, 0)), BS((8, 128), lambda i: (0, 0))],
        out_shape=[SDS((t, dn), F32), SDS((t, dn), BF16), SDS((8, dn), F32), SDS((8, 128), F32)],
        compiler_params=_cparams(1),
    )(h, target, gamma)


def _ffn_bwd_in(name, dh, w_out, layer, gu):
    t, dn = dh.shape
    tm = _tile(t, 1024)

    def body(dh_ref, w_ref, gu_ref, o_ref):
        da = 0.5 * _dot_nt(dh_ref[...], w_ref[...])
        g = gu_ref[0].astype(F32)
        u = gu_ref[1].astype(F32)
        sg = jax.nn.sigmoid(g)
        o_ref[0] = (da * u * (sg * (1.0 + g * (1.0 - sg)))).astype(BF16)
        o_ref[1] = (da * (g * sg)).astype(BF16)

    blk = BS((None, 2, tm, FB), lambda j, i: (j, 0, i, 0))
    return pl.pallas_call(
        body, name=name, grid=(NJ, t // tm),
        in_specs=[BS((tm, dn), lambda j, i: (i, 0)), BS((None, None, FB, dn), lambda j, i: (layer, j, 0, 0)), blk],
        out_specs=blk, out_shape=SDS((NJ, 2, t, FB), BF16),
        compiler_params=_cparams(2),
    )(dh, w_out, gu)


def _mm_nt_plain(name, xf, w):
    t, dn = xf.shape
    n = w.shape[0]
    tm = _tile(t)

    def body(x_ref, w_ref, o_ref):
        o_ref[...] = _dot_nt(x_ref[...], w_ref[...]).astype(BF16)

    return pl.pallas_call(
        body, name=name, grid=(t // tm,),
        in_specs=[BS((tm, dn), lambda i: (i, 0)), BS((n, dn), lambda i: (0, 0))],
        out_specs=BS((tm, n), lambda i: (i, 0)), out_shape=SDS((t, n), BF16),
        compiler_params=_cparams(1),
    )(xf, w)


def _mm_tn(name, xa, x_spec, ya, y_spec, out_shape, out_spec, nj, scale=None):
    def body(x_ref, y_ref, o_ref):
        acc = _dot_tn(x_ref[...], y_ref[...])
        o_ref[...] = (acc if scale is None else scale * acc).astype(BF16)

    return pl.pallas_call(
        body, name=name, grid=(nj,),
        in_specs=[x_spec, y_spec], out_specs=out_spec, out_shape=SDS(out_shape, BF16),
        compiler_params=_cparams(1),
    )(xa, ya)


def _dw_qkv(hn, dqkv3, wb):
    t, dn = hn.shape
    per = wb // 128

    def body(x_ref, *refs):
        cols = [y_ref[...] for y_ref in refs[:per]]
        refs[per][...] = _dot_tn(x_ref[...], jnp.concatenate(cols, axis=1)).astype(BF16)

    def piece(k):
        return BS((None, t, 128), lambda j: ((per * j + k) // 8, 0, (per * j + k) % 8))

    return pl.pallas_call(
        body, name="dw_qkv", grid=(NDEV,),
        in_specs=[BS((t, dn), lambda j: (0, 0))] + [piece(k) for k in range(per)],
        out_specs=BS((None, dn, wb), lambda j: (j, 0, 0)), out_shape=SDS((NDEV, dn, wb), BF16),
        compiler_params=_cparams(1),
    )(hn, *([dqkv3] * per))


def _mm_nt_epi(name, ya, y_spec, wa, w_spec, nj, n_out, extra, out_shapes, out_specs, epilogue, tm, nt, mm_fn=None):
    n_extra = len(extra)
    n_outs = len(out_shapes)

    def body(*refs):
        y_ref, w_ref = refs[:2]
        ex = refs[2 : 2 + n_extra]
        outs = refs[2 + n_extra : 2 + n_extra + n_outs]
        i = pl.program_id(0)
        j = pl.program_id(1)
        part = _dot_nt(y_ref[...], w_ref[...]) if mm_fn is None else mm_fn(y_ref, w_ref)
        if nj == 1:
            epilogue(part, ex, outs, i, nt)
            return
        acc = refs[-1]

        @pl.when(j == 0)
        def _():
            acc[...] = part

        @pl.when(j > 0)
        def _():
            acc[...] += part

        @pl.when(j == nj - 1)
        def _():
            epilogue(acc[...], ex, outs, i, nt)

    return pl.pallas_call(
        body, name=name, grid=(nt, nj),
        in_specs=[y_spec, w_spec] + [spec for _, spec in extra],
        out_specs=out_specs, out_shape=out_shapes,
        scratch_shapes=[] if nj == 1 else [pltpu.VMEM((tm, n_out), F32)],
        compiler_params=_cparams(2),
    )(ya, wa, *[arr for arr, _ in extra])


def _norm_bwd(dn, hv, gam):
    r = _rms_scale(hv)
    hh = hv * r
    t1 = dn * gam
    return r * (t1 - hh * jnp.mean(t1 * hh, axis=-1, keepdims=True)), dn * hh


def _norm_bwd_epilogue(has_res, out_dtype):
    def epilogue(dn, ex, outs, i, nt):
        dh, dg_rows = _norm_bwd(dn, ex[0][...], ex[1][...])
        _acc_rows(outs[1], dg_rows, i, nt)
        if has_res:
            dh = dh + ex[2][...]
        outs[0][...] = dh.astype(out_dtype)
        if has_res:
            outs[2][...] = dh.astype(BF16)

    return epilogue


def _mm_nt_norm_bwd(name, ya, y_spec, wa, w_spec, nj, h, gamma, res, out_dtype, mm_fn=None, want_tm=512):
    t, n = h.shape
    tm = _tile(t, want_tm)
    nt = t // tm
    row = BS((tm, n), lambda i, j: (i, 0))
    extra = [(h, row), (gamma, BS((1, n), lambda i, j: (0, 0)))]
    out_shapes = [SDS((t, n), out_dtype), SDS((8, n), F32)]
    out_specs = [row, BS((8, n), lambda i, j: (0, 0))]
    if res is not None:
        extra.append((res, row))
        out_shapes.append(SDS((t, n), BF16))
        out_specs.append(row)
    return _mm_nt_epi(
        name, ya, y_spec, wa, w_spec, nj, n, extra, out_shapes, out_specs, _norm_bwd_epilogue(res is not None, out_dtype), tm, nt, mm_fn,
    )


def _dev_block(jj):
    return jj // 2 + NJ * (jj % 2)


def _ffn_dn_mm(y_ref, w_ref):
    acc = None
    for jj in range(2 * NJ):
        part = _dot_nt(y_ref[jj], w_ref[_dev_block(jj)])
        acc = part if acc is None else acc + part
    return acc


def _ffn_bwd(tag, dh, dhb, n_in, h_in, gamma, gu, a, w_in, w_out, more_grads, collective_id):
    t, dn = dh.shape
    dgu = _ffn_bwd_in(f"{tag}_bwd_in", dhb, w_out, 0, gu).reshape(2 * NJ, t, FB)
    dw_out = _mm_tn(
        f"{tag}_dw_out", a, BS((None, t, FB), lambda j: (j, 0, 0)), dhb, BS((t, dn), lambda j: (0, 0)),
        (NJ, FB, dn), BS((None, FB, dn), lambda j: (j, 0, 0)), NJ, scale=0.5,
    )
    dw_in = _mm_tn(
        f"{tag}_dw_in", dgu, BS((None, t, FB), lambda j: (j, 0, 0)), n_in, BS((t, dn), lambda j: (0, 0)),
        (NDEV, FB, dn), BS((None, FB, dn), lambda j: (_dev_block(j), 0, 0)), NDEV,
    )
    entries = [("scatter", dw_in), ("scatter", dw_out.reshape(NDEV, NJ * FB // NDEV, dn))] + [("scatter", g) for g in more_grads]
    landed = _exchange_sc(f"{tag}_reduce", entries, collective_id)
    tm = _tile(t, 256)
    dh_in, dgam, dhb_in = _mm_nt_norm_bwd(
        f"{tag}_dn", dgu, BS((2 * NJ, tm, FB), lambda i, j: (0, i, 0)),
        w_in, BS((None, NDEV, dn, FB), lambda i, j: (0, 0, 0, 0)), 1, h_in, gamma, dh, F32, mm_fn=_ffn_dn_mm, want_tm=256,
    )
    return dh_in, dhb_in, dgam, landed


def _heads_mm(y_ref, w_ref):
    acc = None
    for h in range(HEADS_B):
        part = _dot_nt(y_ref[h], w_ref[h])
        acc = part if acc is None else acc + part
    return acc


def _dqkv_mm(per):
    def mm(y_ref, w_ref):
        acc = None
        for j in range(NDEV):
            cols = [y_ref[(per * j + k) // 8, :, ((per * j + k) % 8) * 128 : ((per * j + k) % 8 + 1) * 128] for k in range(per)]
            part = _dot_nt(jnp.concatenate(cols, axis=1), w_ref[j])
            acc = part if acc is None else acc + part
        return acc

    return mm


def _kv_latent_bwd(dkv, w_up, ckr, latent_norm, dkr, c64, s64, p64, seq):
    t, wd = ckr.shape
    hb = w_up.shape[-1]
    tm = _tile(min(seq, 512), min(seq, 512))
    nt = t // tm
    nseq = seq // tm

    def epilogue(dn, ex, outs, i, nt_):
        dlat, dg_rows = _norm_bwd(dn, ex[0][...], ex[1][...])
        _acc_rows(outs[1], dg_rows, i, nt_)
        outs[0][:, :KV_LORA] = dlat.astype(BF16)
        outs[0][:, KV_LORA:] = _rope_bwd(ex[2][...], ex[3][...], ex[4][...], ex[5][...]).astype(BF16)

    pos = BS((tm, ROPE), lambda i, j: (i % nseq, 0))
    extra = [
        (ckr, BS((tm, KV_LORA), lambda i, j: (i, 0))), (latent_norm, BS((1, KV_LORA), lambda i, j: (0, 0))),
        (dkr, BS((tm, ROPE), lambda i, j: (i, 0))), (c64, pos), (s64, pos), (p64, BS((ROPE, ROPE), lambda i, j: (0, 0))),
    ]
    def heads_mm(y_ref, w_ref):
        acc = None
        for h in range(HEADS_B):
            part = _dot_nt(y_ref[:, h * hb : (h + 1) * hb], w_ref[h])
            acc = part if acc is None else acc + part
        return acc

    return _mm_nt_epi(
        "kv_latent_bwd", dkv, BS((tm, HEADS_B * hb), lambda i, j: (i, 0)), w_up, BS((HEADS_B, KV_LORA, hb), lambda i, j: (0, 0, 0)),
        1, KV_LORA, extra, [SDS((t, wd), BF16), SDS((8, KV_LORA), F32)],
        [BS((tm, wd), lambda i, j: (i, 0)), BS((8, KV_LORA), lambda i, j: (0, 0))], epilogue, tm, nt, heads_mm,
    )


def _adamw(name, parts, w, m, v):
    n_layers, rows, cols = w.shape
    tr = max(d for d in range(8, min(rows, 256) + 1, 8) if rows % d == 0)
    nb = rows // tr

    def body(*refs):
        p_refs = refs[:n_layers]
        w_ref, m_ref, v_ref, g_ref, d_ref, nm_ref, nv_ref = refs[n_layers : n_layers + 7]
        layer = pl.program_id(0)
        for lp in range(n_layers):

            @pl.when(layer == lp)
            def _():
                g = p_refs[lp][0].astype(F32)
                for k in range(1, NDEV):
                    g = g + p_refs[lp][k].astype(F32)
                g_ref[...] = g

        g = g_ref[...]
        nm = ADAM_B1 * m_ref[...] + (1.0 - ADAM_B1) * g
        nv = ADAM_B2 * v_ref[...] + (1.0 - ADAM_B2) * (g * g)
        nm_ref[...] = nm
        nv_ref[...] = nv
        m_hat = nm / (1.0 - ADAM_B1 ** ADAM_STEP)
        v_hat = nv / (1.0 - ADAM_B2 ** ADAM_STEP)
        d_ref[...] = -ADAM_LR * (m_hat / (jnp.sqrt(v_hat) + ADAM_EPS) + ADAM_WD * w_ref[...])

    def part_spec(lp):
        return BS((NDEV, tr, cols), lambda l, i: (0, jnp.where(l == lp, i, jnp.where(l < lp, 0, nb - 1)), 0))

    row = BS((None, tr, cols), lambda l, i: (l, i, 0))
    return pl.pallas_call(
        body, name=name, grid=(n_layers, nb),
        in_specs=[part_spec(lp) for lp in range(n_layers)] + [row, row, row],
        out_specs=[row] * 4, out_shape=[SDS(w.shape, F32)] * 4,
        compiler_params=_cparams(2),
    )(*parts, w, m, v)


def _pack_small(ffn1_norm, mix_norm, ffn2_norm, kv_norm, final_norm, q_norm, latent_norm, rel_bias, last_row):
    dn = ffn1_norm.shape[-1]

    def rows_of(a, n_rows):
        flat = a.reshape(-1)
        return jnp.pad(flat, (0, n_rows * dn - flat.shape[0])).reshape(n_rows, dn)

    return jnp.concatenate(
        [
            ffn1_norm.reshape(2, dn), mix_norm.reshape(2, dn), ffn2_norm.reshape(2, dn), kv_norm.reshape(1, dn),
            final_norm.reshape(1, dn), rows_of(q_norm, 1), rows_of(latent_norm, 1), rows_of(rel_bias, 5), rows_of(last_row, 1),
        ],
        axis=0,
    )


def _unpack_small(pack):
    dn = pack.shape[-1]
    return dict(
        ffn1_norm=pack[0:2], mix_norm=pack[2:4], ffn2_norm=pack[4:6], kv_norm=pack[6], final_norm=pack[7],
        b_q_norm=pack[8, :Q_LORA].reshape(1, Q_LORA), kv_latent_norm=pack[9, :KV_LORA],
        a_rel_bias=pack[10:15].reshape(-1)[: HEADS_A * NREL].reshape(1, HEADS_A, NREL), last=pack[15],
    )


def kernel(x, ffn1_norm, ffn1_w_in, ffn1_w_out, mix_norm, ffn2_norm, ffn2_w_in, ffn2_w_out, a_w_qkv, a_rel_bias, a_w_o, kv_norm, kv_w_down, kv_latent_norm, kv_w_up, b_w_dq, b_q_norm, b_w_uq, b_w_o, final_norm, loss_target, m_ffn1_norm, m_ffn1_w_in, m_ffn1_w_out, m_mix_norm, m_ffn2_norm, m_ffn2_w_in, m_ffn2_w_out, m_a_w_qkv, m_a_rel_bias, m_a_w_o, m_kv_norm, m_kv_w_down, m_kv_latent_norm, m_kv_w_up, m_b_w_dq, m_b_q_norm, m_b_w_uq, m_b_w_o, m_final_norm, v_ffn1_norm, v_ffn1_w_in, v_ffn1_w_out, v_mix_norm, v_ffn2_norm, v_ffn2_w_in, v_ffn2_w_out, v_a_w_qkv, v_a_rel_bias, v_a_w_o, v_kv_norm, v_kv_w_down, v_kv_latent_norm, v_kv_w_up, v_b_w_dq, v_b_q_norm, v_b_w_uq, v_b_w_o, v_final_norm):
    bl, seq, dn = x.shape
    t = bl * seq
    tm = _tile(t)
    nt = t // tm
    x2 = x.reshape(t, dn)
    target2 = loss_target.reshape(t, dn)

    def gathered(*ws):
        return [("gather", w.astype(BF16)) for w in ws]

    groups = [
        gathered(ffn1_w_in[0]), gathered(ffn1_w_out[0]), gathered(a_w_qkv[0], a_w_o[0]), gathered(ffn2_w_in[0], ffn2_w_out[0]),
        gathered(kv_w_down, kv_w_up), gathered(ffn1_w_in[1], ffn1_w_out[1]), gathered(b_w_dq[0], b_w_uq[0], b_w_o[0]),
        gathered(ffn2_w_in[1], ffn2_w_out[1]),
    ]
    ag = [_exchange_sc(f"gather_{k}", group, GATHER_IDS[k]) for k, group in enumerate(groups)]

    def as_w_in(w):
        return w.reshape(1, NDEV, dn, FB)

    def as_w_out(w):
        return w.reshape(1, NJ, FB, dn)

    c64, s64, p64, c192, s192, p192 = _rope_tables(seq)
    q_norm = b_q_norm.reshape(1, Q_LORA)
    latent_norm = kv_latent_norm.reshape(1, KV_LORA)
    bias = _window_bias(_rel_bias_fwd(jnp.pad(a_rel_bias[0], ((0, 0), (0, NREL_PAD - NREL)))))

    h0, h1, h2, n1, hn, n2, gu1, gu2, a1, a2, w_in1, w_in2, w_out1, w_out2 = ([None, None] for _ in range(14))
    h0[0] = x2
    (n1[0],) = _norm_fwd("norm_x", x2, ffn1_norm[0:1])
    w_in1[0] = as_w_in(ag[0][0])
    gu1[0], a1[0] = _ffn_in("ffn1_in_0", n1[0], w_in1[0], 0)
    w_out1[0] = as_w_out(ag[1][0])
    h1[0], hn[0] = _mm_res_norm("ffn1_out_0", a1[0], w_out1[0], 0, h0[0], mix_norm[0:1], 0.5)
    w_qkv, w_o_a = ag[2]
    qkv_wb = w_qkv.shape[-1]
    w_o_a = w_o_a.reshape(1, 1, dn, dn)
    qkv3 = _qkv_proj("qkv_proj", hn[0], w_qkv)
    o_a = _attn_a_fwd(qkv3, bias, bl, seq)
    h2[0], n2[0] = _mm_res_norm("attn_a_out", o_a.reshape(1, t, dn), w_o_a, 0, h1[0], ffn2_norm[0:1], 1.0)
    w_in2[0], w_out2[0] = as_w_in(ag[3][0]), as_w_out(ag[3][1])
    gu2[0], a2[0] = _ffn_in("ffn2_in_0", n2[0], w_in2[0], 0)
    h0[1], hk, n1[1] = _mm_res_norm(
        "ffn2_out_0", a2[0], w_out2[0], 0, h2[0], jnp.concatenate([kv_norm.reshape(1, dn), ffn1_norm[1:2]], axis=0), 0.5
    )
    w_down, w_up = ag[4]
    w_down = w_down.reshape(dn, KV_LORA + ROPE)
    ckr, ckv, kr = _kv_down(hk, w_down, latent_norm, c64, s64, p64, seq)
    kv = _kv_up(ckv, w_up)
    w_in1[1], w_out1[1] = as_w_in(ag[5][0]), as_w_out(ag[5][1])
    gu1[1], a1[1] = _ffn_in("ffn1_in_1", n1[1], w_in1[1], 0)
    h1[1], hn[1] = _mm_res_norm("ffn1_out_1", a1[1], w_out1[1], 0, h0[1], mix_norm[1:2], 0.5)
    w_dq, w_uq, w_o_b = ag[6]
    w_dq = w_dq.reshape(dn, Q_LORA)
    w_o_b = w_o_b.reshape(1, 1, dn, dn)
    cq_pre, cq = _q_down(hn[1], w_dq, q_norm)
    q = _q_up(cq, w_uq, c192, s192, p192, seq)
    o_b = _mla_fwd(q, kv, kr, bl, seq)
    h2[1], n2[1] = _mm_res_norm("attn_b_out", o_b.reshape(1, t, dn), w_o_b, 0, h1[1], ffn2_norm[1:2], 1.0)
    w_in2[1], w_out2[1] = as_w_in(ag[7][0]), as_w_out(ag[7][1])
    gu2[1], a2[1] = _ffn_in("ffn2_in_1", n2[1], w_in2[1], 0)
    (h_last,) = _mm_res_norm("ffn2_out_1", a2[1], w_out2[1], 0, h2[1], None, 0.5)
    dh, dhb, dg_final, loss_part = _loss_final(h_last, target2, final_norm.reshape(1, dn))

    dg_ffn1, dg_mix, dg_ffn2, rs_ffn1, rs_ffn2 = ([None, None] for _ in range(5))

    def whole(rows, cols):
        return BS((rows, cols), lambda j: (0, 0))

    def dw_rows(name, xa, ya):
        n = ya.shape[1]
        return _mm_tn(name, xa, whole(t, dn), ya, whole(t, n), (dn, n), whole(dn, n), 1).reshape(NDEV, dn // NDEV, n)

    dh, dhb, dg_ffn2[1], rs_ffn2[1] = _ffn_bwd(
        "ffn2_1", dh, dhb, n2[1], h2[1], ffn2_norm[1:2], gu2[1], a2[1], w_in2[1], w_out2[1], [], REDUCE_IDS[0]
    )
    do_b = _mm_nt_plain("attn_b_do", dhb, w_o_b.reshape(dn, dn))
    dw_o_b = dw_rows("attn_b_dwo", o_b, dhb)
    dq_pre, dkv, dkr = _mla_bwd(q, kv, kr, do_b, c192, s192, p192, bl, seq)
    dw_uq = _mm_tn(
        "dw_uq", cq, whole(t, Q_LORA), dq_pre, BS((None, t, QK_B), lambda j: (j, 0, 0)),
        (HEADS_B, Q_LORA, QK_B), BS((None, Q_LORA, QK_B), lambda j: (j, 0, 0)), HEADS_B,
    )
    dcq_pre, dg_q = _mm_nt_norm_bwd(
        "dcq", dq_pre, BS((HEADS_B, tm, QK_B), lambda i, j: (0, i, 0)), w_uq, BS((HEADS_B, Q_LORA, QK_B), lambda i, j: (0, 0, 0)),
        1, cq_pre, q_norm, None, BF16, mm_fn=_heads_mm,
    )
    dw_dq = dw_rows("dw_dq", hn[1], dcq_pre)
    dh, dg_mix[1], dhb = _mm_nt_norm_bwd(
        "dhn_b", dcq_pre, BS((tm, Q_LORA), lambda i, j: (i, 0)), w_dq, BS((dn, Q_LORA), lambda i, j: (0, 0)),
        1, h1[1], mix_norm[1:2], dh, F32,
    )
    dh, dhb, dg_ffn1[1], rs_ffn1[1] = _ffn_bwd(
        "ffn1_1", dh, dhb, n1[1], h0[1], ffn1_norm[1:2], gu1[1], a1[1], w_in1[1], w_out1[1], [dw_o_b, dw_uq, dw_dq], REDUCE_IDS[1]
    )
    dw_up = _mm_tn(
        "dw_up", ckv, whole(t, KV_LORA), dkv, BS((t, NOPE + V_DIM), lambda j: (0, j)),
        (HEADS_B, KV_LORA, NOPE + V_DIM), BS((None, KV_LORA, NOPE + V_DIM), lambda j: (j, 0, 0)), HEADS_B,
    )
    dckr, dg_latent = _kv_latent_bwd(dkv, w_up, ckr, latent_norm, dkr, c64, s64, p64, seq)
    dw_down = dw_rows("dw_down", hk, dckr)
    dh, dg_kv, dhb = _mm_nt_norm_bwd(
        "dhk", dckr, BS((tm, KV_LORA + ROPE), lambda i, j: (i, 0)), w_down, BS((dn, KV_LORA + ROPE), lambda i, j: (0, 0)),
        1, h0[1], kv_norm.reshape(1, dn), dh, F32,
    )
    dh, dhb, dg_ffn2[0], rs_ffn2[0] = _ffn_bwd(
        "ffn2_0", dh, dhb, n2[0], h2[0], ffn2_norm[0:1], gu2[0], a2[0], w_in2[0], w_out2[0], [dw_up, dw_down], REDUCE_IDS[2]
    )
    do_a = _mm_nt_plain("attn_a_do", dhb, w_o_a.reshape(dn, dn))
    dw_o_a = dw_rows("attn_a_dwo", o_a, dhb)
    dqkv3, dbias = _attn_a_bwd(qkv3, do_a, bias, bl, seq)
    dw_qkv = _dw_qkv(hn[0], dqkv3, qkv_wb)
    dh, dg_mix[0], dhb = _mm_nt_norm_bwd(
        "dhn_a", dqkv3, BS((3, tm, dn), lambda i, j: (0, i, 0)), w_qkv, BS((NDEV, dn, qkv_wb), lambda i, j: (0, 0, 0)),
        1, h1[0], mix_norm[0:1], dh, F32, mm_fn=_dqkv_mm(qkv_wb // 128),
    )
    r_o_a, r_qkv = _exchange_sc("attn_a_reduce", [("scatter", dw_o_a), ("scatter", dw_qkv)], REDUCE_IDS[3])
    dh, dhb, dg_ffn1[0], rs_ffn1[0] = _ffn_bwd(
        "ffn1_0", dh, dhb, n1[0], h0[0], ffn1_norm[0:1], gu1[0], a1[0], w_in1[0], w_out1[0], [], REDUCE_IDS[4]
    )
    grad_x = dh.reshape(bl, seq, dn)
    dtable = _rel_bias_bwd(_window_bias_bwd(dbias))[:, :NREL]

    small = _pack_small(
        jnp.stack([dg_ffn1[0][0], dg_ffn1[1][0]]), jnp.stack([dg_mix[0][0], dg_mix[1][0]]), jnp.stack([dg_ffn2[0][0], dg_ffn2[1][0]]),
        dg_kv[0], dg_final[0], dg_q[0], dg_latent[0], dtable, loss_part[0],
    )
    (r_small,) = _exchange("gather_small_grads", [("gather", small)])

    def update(name, parts, w, m, v):
        shape3 = (len(parts),) + w.shape[-2:]
        parts = [p.reshape((NDEV,) + shape3[1:]) for p in parts]
        outs = _adamw(name, parts, w.reshape(shape3), m.reshape(shape3), v.reshape(shape3))
        return [o.reshape(w.shape) for o in outs]

    res = {}
    r_in2_1, r_out2_1 = rs_ffn2[1]
    r_in1_1, r_out1_1, r_o_b, r_uq, r_dq = rs_ffn1[1]
    r_in2_0, r_out2_0, r_up, r_down = rs_ffn2[0]
    r_in1_0, r_out1_0 = rs_ffn1[0]
    def update_transposed(name, parts, w, m, v):
        outs = update(name, parts, *[jnp.swapaxes(a, 1, 2) for a in (w, m, v)])
        return [jnp.swapaxes(o, 1, 2) for o in outs]

    res["ffn2_w_in"] = update_transposed("adamw_ffn2_w_in", [r_in2_0, r_in2_1], ffn2_w_in, m_ffn2_w_in, v_ffn2_w_in)
    res["ffn2_w_out"] = update("adamw_ffn2_w_out", [r_out2_0, r_out2_1], ffn2_w_out, m_ffn2_w_out, v_ffn2_w_out)
    res["kv_w_down"] = update("adamw_kv_w_down", [r_down], kv_w_down, m_kv_w_down, v_kv_w_down)
    res["kv_w_up"] = update("adamw_kv_w_up", [r_up], kv_w_up, m_kv_w_up, v_kv_w_up)
    res["b_w_dq"] = update("adamw_b_w_dq", [r_dq], b_w_dq, m_b_w_dq, v_b_w_dq)
    res["b_w_uq"] = update("adamw_b_w_uq", [r_uq], b_w_uq, m_b_w_uq, v_b_w_uq)
    res["b_w_o"] = update("adamw_b_w_o", [r_o_b], b_w_o, m_b_w_o, v_b_w_o)
    res["a_w_qkv"] = update("adamw_a_w_qkv", [r_qkv], a_w_qkv, m_a_w_qkv, v_a_w_qkv)
    res["a_w_o"] = update("adamw_a_w_o", [r_o_a], a_w_o, m_a_w_o, v_a_w_o)
    res["ffn1_w_in"] = update_transposed("adamw_ffn1_w_in", [r_in1_0, r_in1_1], ffn1_w_in, m_ffn1_w_in, v_ffn1_w_in)
    res["ffn1_w_out"] = update("adamw_ffn1_w_out", [r_out1_0, r_out1_1], ffn1_w_out, m_ffn1_w_out, v_ffn1_w_out)
    zero_row = jnp.zeros((dn,), F32)
    packs = [
        _pack_small(f1, mx, f2, kvn, fin, qn, lat, rel, zero_row)
        for f1, mx, f2, kvn, fin, qn, lat, rel in (
            (ffn1_norm, mix_norm, ffn2_norm, kv_norm, final_norm, b_q_norm, kv_latent_norm, a_rel_bias),
            (m_ffn1_norm, m_mix_norm, m_ffn2_norm, m_kv_norm, m_final_norm, m_b_q_norm, m_kv_latent_norm, m_a_rel_bias),
            (v_ffn1_norm, v_mix_norm, v_ffn2_norm, v_kv_norm, v_final_norm, v_b_q_norm, v_kv_latent_norm, v_a_rel_bias),
        )
    ]
    small_out = [_unpack_small(o[0]) for o in _adamw("adamw_small", [r_small], *[p[None] for p in packs])]
    for name in ("ffn1_norm", "mix_norm", "ffn2_norm", "a_rel_bias", "kv_norm", "kv_latent_norm", "b_q_norm", "final_norm"):
        res[name] = [so[name] for so in small_out]
    loss = small_out[0]["last"][0]

    order = [
        "ffn1_norm", "ffn1_w_in", "ffn1_w_out", "mix_norm", "ffn2_norm", "ffn2_w_in", "ffn2_w_out", "a_w_qkv", "a_rel_bias",
        "a_w_o", "kv_norm", "kv_w_down", "kv_latent_norm", "kv_w_up", "b_w_dq", "b_q_norm", "b_w_uq", "b_w_o", "final_norm",
    ]
    return (loss, grad_x, *[res[n][0] for n in order], *[res[n][1] for n in order], *[res[n][2] for n in order], *[res[n][3] for n in order])
```

```python
import jax
import jax.numpy as jnp
import numpy as np
from jax import lax
from jax.experimental import pallas as pl
from jax.experimental.pallas import tpu as pltpu
from jax.experimental.pallas import tpu_sc as plsc

NDEV = 8
D_MODEL = 1024
D_FF = 2816
FB = 2 * D_FF // NDEV
NJ = D_FF // FB
CHUNK = 64
LEFT_CHUNKS = 8
PAD = LEFT_CHUNKS * CHUNK
BAND = PAD + CHUNK
CHUNKS_PER_STEP = 4
WINDOW = PAD + CHUNKS_PER_STEP * CHUNK
STEP_ROWS = CHUNKS_PER_STEP * 2 * CHUNK
MAX_REL = 128
NREL = 2 * MAX_REL + 1
NREL_PAD = 384
HEADS_A = 16
HEADS_B = 8
NOPE = 128
ROPE = 64
QK_B = NOPE + ROPE
V_DIM = 128
Q_LORA = 768
KV_LORA = 256
ROPE_THETA = 10000.0
EPS = 1e-6
NEG_INF = -1e30
MLA_TQ = 256
MLA_TK_FWD = 256
MLA_TK_BWD = 1024
ADAM_LR = 0.001
ADAM_B1 = 0.9
ADAM_B2 = 0.999
ADAM_EPS = 1e-08
ADAM_WD = 0.01
ADAM_STEP = 10
PACK_ROWS = 16
GATHER_IDS = tuple(range(1, 9))
REDUCE_IDS = tuple(range(9, 14))
VMEM_LIMIT_BYTES = 56 * 1024 * 1024

F32 = jnp.float32
BF16 = jnp.bfloat16
SDS = jax.ShapeDtypeStruct
BS = pl.BlockSpec
MESH = pl.DeviceIdType.MESH


def _cparams(n_axes):
    return pltpu.CompilerParams(dimension_semantics=("arbitrary",) * n_axes, vmem_limit_bytes=VMEM_LIMIT_BYTES)


def _tile(t, want=512):
    return want if t % want == 0 else t


def _dot(a, b):
    return jnp.dot(a, b, preferred_element_type=F32)


def _dot_nt(a, b):
    return lax.dot_general(a, b, (((1,), (1,)), ((), ())), preferred_element_type=F32)


def _dot_tn(a, b):
    return lax.dot_general(a, b, (((0,), (0,)), ((), ())), preferred_element_type=F32)


def _split3(a):
    hi = a.astype(BF16)
    rest = a - hi.astype(F32)
    mid = rest.astype(BF16)
    return hi, mid, (rest - mid.astype(F32)).astype(BF16)


def _dot_exact(a, onehot, transposed=False):
    ob = onehot.astype(BF16)
    dot = _dot_nt if transposed else _dot
    hi, mid, lo = _split3(a)
    return dot(hi, ob) + dot(mid, ob) + dot(lo, ob)


def _rms_scale(h):
    return lax.rsqrt(jnp.mean(h * h, axis=-1, keepdims=True) + EPS)


def _acc_rows(ref, val, step, n_steps):
    part = val.reshape(val.shape[0] // 8, 8, val.shape[1]).sum(axis=0)

    @pl.when(step == 0)
    def _():
        ref[...] = part

    @pl.when(step > 0)
    def _():
        ref[...] += part

    @pl.when(step == n_steps - 1)
    def _():
        ref[...] = jnp.broadcast_to(jnp.sum(ref[...], axis=0, keepdims=True), ref.shape)


def _exchange_plan(entries):
    ins = [e[1] for e in entries]
    kinds = [e[0] for e in entries]
    lands = [SDS((NDEV,) + a.shape if k == "gather" else a.shape, a.dtype) for k, a in zip(kinds, ins)]
    return ins, lands, kinds


def _mesh_place():
    x, y, c = lax.axis_index("x"), lax.axis_index("y"), lax.axis_index("c")
    return (x, y, c), 4 * x + 2 * y + c


def _flipped(place, p):
    x, y, c = place
    px = 1 - x if p & 4 else x
    py = 1 - y if p & 2 else y
    pc = 1 - c if p & 1 else c
    return (px, py, pc), 4 * px + 2 * py + pc


def _ends(kind, src_ref, land_ref, origin, target):
    if kind == "gather":
        return src_ref, land_ref.at[origin]
    return src_ref.at[target], land_ref.at[origin]


def _remote(kind, src_ref, land_ref, send_sems, recv_sems, k, p, place, me, arriving):
    peer_pos, peer = _flipped(place, p)
    src, dst = _ends(kind, src_ref, land_ref, me, peer)
    if arriving:
        dst = _ends(kind, src_ref, land_ref, peer, me)[1]
    sem = k * (NDEV - 1) + p - 1
    return pltpu.make_async_remote_copy(
        src_ref=src, dst_ref=dst, send_sem=send_sems.at[sem], recv_sem=recv_sems.at[sem], device_id=peer_pos, device_id_type=MESH,
    )


def _exchange(name, entries):
    ins, lands, kinds = _exchange_plan(entries)
    n = len(ins)

    def body(*refs):
        in_refs, land_refs = refs[:n], refs[n : 2 * n]
        send_sems, recv_sems, local_sems = refs[2 * n :]
        place, me = _mesh_place()
        local = []
        for k in range(n):
            src, dst = _ends(kinds[k], in_refs[k], land_refs[k], me, me)
            local.append(pltpu.make_async_copy(src, dst, local_sems.at[k]))
            local[-1].start()
        sends = []
        for p in range(1, NDEV):
            for k in range(n):
                sends.append(_remote(kinds[k], in_refs[k], land_refs[k], send_sems, recv_sems, k, p, place, me, False))
                sends[-1].start()
        for p in range(1, NDEV):
            for k in range(n):
                _remote(kinds[k], in_refs[k], land_refs[k], send_sems, recv_sems, k, p, place, me, True).wait_recv()
        for cp in sends:
            cp.wait_send()
        for cp in local:
            cp.wait()

    any_spec = BS(memory_space=pl.ANY)
    return pl.pallas_call(
        body, name=name, out_shape=lands, in_specs=[any_spec] * n, out_specs=[any_spec] * n,
        scratch_shapes=[
            pltpu.SemaphoreType.DMA((n * (NDEV - 1),)), pltpu.SemaphoreType.DMA((n * (NDEV - 1),)), pltpu.SemaphoreType.DMA((n,)),
        ],
    )(*ins)


def _exchange_sc(name, entries, collective_id):
    ins, lands, kinds = _exchange_plan(entries)
    n = len(ins)

    def launch(*refs):
        in_refs, land_refs = refs[:n], refs[n : 2 * n]
        send_sems, recv_sems, local_sems = refs[2 * n :]
        place, me = _mesh_place()
        barrier = pltpu.get_barrier_semaphore()
        for p in range(1, NDEV):
            pl.semaphore_signal(barrier, inc=1, device_id=_flipped(place, p)[0], device_id_type=MESH)
        pl.semaphore_wait(barrier, NDEV - 1)
        local = []
        for k in range(n):
            src, dst = _ends(kinds[k], in_refs[k], land_refs[k], me, me)
            local.append(pltpu.make_async_copy(src, dst, local_sems.at[k]))
            local[-1].start()
        sends = []
        if all(kind == "gather" for kind in kinds):
            for p in (1, 2, 4, 6):
                for k in range(n):
                    sends.append(_remote(kinds[k], in_refs[k], land_refs[k], send_sems, recv_sems, k, p, place, me, False))
                    sends[-1].start()
            sibling_pos, _ = _flipped(place, 1)
            for f in (2, 4, 6):
                _, origin = _flipped(place, f)
                for k in range(n):
                    _remote(kinds[k], in_refs[k], land_refs[k], send_sems, recv_sems, k, f, place, me, True).wait_recv()
                    sem = k * (NDEV - 1) + f
                    sends.append(
                        pltpu.make_async_remote_copy(
                            src_ref=land_refs[k].at[origin], dst_ref=land_refs[k].at[origin], send_sem=send_sems.at[sem],
                            recv_sem=recv_sems.at[sem], device_id=sibling_pos, device_id_type=MESH,
                        )
                    )
                    sends[-1].start()
            for p in (1, 3, 5, 7):
                for k in range(n):
                    _remote(kinds[k], in_refs[k], land_refs[k], send_sems, recv_sems, k, p, place, me, True).wait_recv()
        else:
            for p in range(1, NDEV):
                for k in range(n):
                    sends.append(_remote(kinds[k], in_refs[k], land_refs[k], send_sems, recv_sems, k, p, place, me, False))
                    sends[-1].start()
            for p in range(1, NDEV):
                for k in range(n):
                    _remote(kinds[k], in_refs[k], land_refs[k], send_sems, recv_sems, k, p, place, me, True).wait_recv()
        for cp in sends:
            cp.wait_send()
        for cp in local:
            cp.wait()

    return pl.kernel(
        launch, out_type=tuple(lands), mesh=plsc.ScalarSubcoreMesh(axis_name="sequencer", num_cores=1), name=name,
        scratch_types=(
            pltpu.SemaphoreType.DMA((n * (NDEV - 1),)), pltpu.SemaphoreType.DMA((n * (NDEV - 1),)), pltpu.SemaphoreType.DMA((n,)),
        ),
        compiler_params=pltpu.CompilerParams(collective_id=collective_id),
    )(*ins)


def _norm_fwd(name, h, gammas):
    t, dn = h.shape
    ng = gammas.shape[0]
    tm = _tile(t)

    def body(h_ref, g_ref, *outs):
        hv = h_ref[...]
        hh = hv * _rms_scale(hv)
        for i, o_ref in enumerate(outs):
            o_ref[...] = (hh * g_ref[i : i + 1, :]).astype(BF16)

    row = BS((tm, dn), lambda i: (i, 0))
    return pl.pallas_call(
        body, name=name, grid=(t // tm,),
        in_specs=[row, BS((ng, dn), lambda i: (0, 0))],
        out_specs=[row] * ng, out_shape=[SDS((t, dn), BF16)] * ng,
        compiler_params=_cparams(1),
    )(h, gammas)


def _ffn_in(name, n, w_in, layer):
    t, dn = n.shape
    tm = _tile(t, 1024)

    def body(n_ref, wg_ref, wu_ref, gu_ref, a_ref):
        xv = n_ref[...]
        g = _dot(xv, wg_ref[...])
        u = _dot(xv, wu_ref[...])
        gu_ref[0] = g.astype(BF16)
        gu_ref[1] = u.astype(BF16)
        a_ref[...] = (g * jax.nn.sigmoid(g) * u).astype(BF16)

    return pl.pallas_call(
        body, name=name, grid=(NJ, t // tm),
        in_specs=[
            BS((tm, dn), lambda j, i: (i, 0)),
            BS((None, None, dn, FB), lambda j, i: (layer, j, 0, 0)),
            BS((None, None, dn, FB), lambda j, i: (layer, j + NJ, 0, 0)),
        ],
        out_specs=[BS((None, 2, tm, FB), lambda j, i: (j, 0, i, 0)), BS((None, tm, FB), lambda j, i: (j, i, 0))],
        out_shape=[SDS((NJ, 2, t, FB), BF16), SDS((NJ, t, FB), BF16)],
        compiler_params=_cparams(2),
    )(n, w_in, w_in)


def _mm_res_norm(name, a, w, layer, h_in, gammas, scale):
    nk, t, kb = a.shape
    dn = w.shape[-1]
    ng = 0 if gammas is None else gammas.shape[0]
    tm = _tile(t)

    def body(*refs):
        a_ref, w_ref, h_ref = refs[:3]
        g_ref = refs[3] if ng else None
        outs = refs[3 + (1 if ng else 0) :]
        acc = _dot(a_ref[0], w_ref[0])
        for k in range(1, nk):
            acc += _dot(a_ref[k], w_ref[k])
        ho = h_ref[...] + scale * acc
        outs[0][...] = ho
        if ng:
            hh = ho * _rms_scale(ho)
            for i in range(ng):
                outs[1 + i][...] = (hh * g_ref[i : i + 1, :]).astype(BF16)

    row = BS((tm, dn), lambda i: (i, 0))
    in_specs = [BS((nk, tm, kb), lambda i: (0, i, 0)), BS((None, nk, kb, dn), lambda i: (layer, 0, 0, 0)), row]
    args = [a, w, h_in]
    if ng:
        in_specs.append(BS((ng, dn), lambda i: (0, 0)))
        args.append(gammas)
    return pl.pallas_call(
        body, name=name, grid=(t // tm,),
        in_specs=in_specs,
        out_specs=[row] * (1 + ng), out_shape=[SDS((t, dn), F32)] + [SDS((t, dn), BF16)] * ng,
        compiler_params=_cparams(1),
    )(*args)


def _qkv_proj(name, hn, w_qkv):
    t, dn = hn.shape
    wb = w_qkv.shape[-1]
    per = wb // 128
    tm = _tile(t)

    def body(x_ref, w_ref, o_ref):
        xv = x_ref[...]
        for j in range(NDEV):
            yv = _dot(xv, w_ref[j]).astype(BF16)
            for i in range(per):
                n = per * j + i
                o_ref[n // 8, :, (n % 8) * 128 : (n % 8 + 1) * 128] = yv[:, i * 128 : (i + 1) * 128]

    return pl.pallas_call(
        body, name=name, grid=(t // tm,),
        in_specs=[BS((tm, dn), lambda i: (i, 0)), BS((NDEV, dn, wb), lambda i: (0, 0, 0))],
        out_specs=BS((3, tm, dn), lambda i: (0, i, 0)), out_shape=SDS((3, t, dn), BF16),
        compiler_params=_cparams(1),
    )(hn, w_qkv)


def _rel_onehot(i):
    r = lax.broadcasted_iota(jnp.int32, (NREL_PAD, BAND), 0)
    j = lax.broadcasted_iota(jnp.int32, (NREL_PAD, BAND), 1)
    idx = jnp.clip(PAD + i - j, -MAX_REL, MAX_REL) + MAX_REL
    return (idx == r).astype(F32)


def _rel_bias_fwd(table):
    def body(t_ref, o_ref):
        i8 = pl.program_id(0)
        for ii in range(8):
            o_ref[:, ii, :] = _dot_exact(t_ref[...], _rel_onehot(i8 * 8 + ii))

    return pl.pallas_call(
        body, name="rel_bias_fwd", grid=(CHUNK // 8,),
        in_specs=[BS((HEADS_A, NREL_PAD), lambda i: (0, 0))],
        out_specs=BS((HEADS_A, 8, BAND), lambda i: (0, i, 0)), out_shape=SDS((HEADS_A, CHUNK, BAND), F32),
        compiler_params=_cparams(1),
    )(table)


def _rel_bias_bwd(dbias):
    def body(d_ref, o_ref):
        i8 = pl.program_id(0)
        acc = jnp.zeros((HEADS_A, NREL_PAD), F32)
        for ii in range(8):
            acc += _dot_exact(d_ref[:, ii, :], _rel_onehot(i8 * 8 + ii), transposed=True)

        @pl.when(i8 == 0)
        def _():
            o_ref[...] = acc

        @pl.when(i8 > 0)
        def _():
            o_ref[...] += acc

    return pl.pallas_call(
        body, name="rel_bias_bwd", grid=(CHUNK // 8,),
        in_specs=[BS((HEADS_A, 8, BAND), lambda i: (0, i, 0))],
        out_specs=BS((HEADS_A, NREL_PAD), lambda i: (0, 0)), out_shape=SDS((HEADS_A, NREL_PAD), F32),
        compiler_params=_cparams(1),
    )(dbias)


def _window_bias(bias):
    b = bias.reshape(HEADS_A // 2, 2, CHUNK, BAND)
    per_chunk = [
        jnp.pad(b, ((0, 0), (0, 0), (0, 0), (cc * CHUNK, WINDOW - BAND - cc * CHUNK)), constant_values=NEG_INF)
        for cc in range(CHUNKS_PER_STEP)
    ]
    return jnp.stack(per_chunk, axis=1).reshape(HEADS_A // 2, STEP_ROWS, WINDOW)


def _window_bias_bwd(dwin):
    d = dwin.reshape(HEADS_A // 2, CHUNKS_PER_STEP, 2, CHUNK, WINDOW)
    return sum(d[:, cc, :, :, cc * CHUNK : cc * CHUNK + BAND] for cc in range(CHUNKS_PER_STEP)).reshape(HEADS_A, CHUNK, BAND)


def _step_rows(xs, lane):
    parts = []
    for cc in range(CHUNKS_PER_STEP):
        xc = xs[cc * CHUNK : (cc + 1) * CHUNK]
        parts.append(jnp.where(lane < 64, xc, jnp.zeros_like(xc)))
        parts.append(jnp.where(lane >= 64, xc, jnp.zeros_like(xc)))
    return jnp.concatenate(parts, axis=0)


def _pair_rows(ys, lane):
    parts = []
    for cc in range(CHUNKS_PER_STEP):
        y0 = ys[(2 * cc) * CHUNK : (2 * cc + 1) * CHUNK]
        y1 = ys[(2 * cc + 1) * CHUNK : (2 * cc + 2) * CHUNK]
        parts.append(jnp.where(lane < 64, y0, y1))
    return jnp.concatenate(parts, axis=0)


def _window_probs(q_rows, kwin, bias_win, first_key):
    s = _dot_nt(q_rows, kwin) * (CHUNK ** -0.5) + bias_win
    col = lax.broadcasted_iota(jnp.int32, s.shape, 1)
    s = jnp.where(col >= first_key, s, NEG_INF)
    e = jnp.exp(s - jnp.max(s, axis=-1, keepdims=True))
    return e / jnp.sum(e, axis=-1, keepdims=True)


def _attn_a_fwd(qkv3, bias_win, bl, seq):
    t, dn = qkv3.shape[1:]
    npair = dn // 128
    step = CHUNKS_PER_STEP * CHUNK

    def body(q_ref, k_ref, v_ref, b_ref, o_ref, kpad, vpad):
        kpad[0:PAD, :] = jnp.zeros((PAD, 128), BF16)
        vpad[0:PAD, :] = jnp.zeros((PAD, 128), BF16)
        kpad[PAD:, :] = k_ref[...]
        vpad[PAD:, :] = v_ref[...]
        lane = lax.broadcasted_iota(jnp.int32, (CHUNK, 128), 1)

        def chunks(it, carry):
            r0 = pl.multiple_of(it * step, step)
            q_rows = _step_rows(q_ref[pl.ds(r0, step), :], lane)
            p = _window_probs(q_rows, kpad[pl.ds(r0, WINDOW), :], b_ref[...], PAD - r0)
            o_rows = _dot(p.astype(BF16), vpad[pl.ds(r0, WINDOW), :])
            o_ref[pl.ds(r0, step), :] = _pair_rows(o_rows, lane).astype(BF16)
            return carry

        lax.fori_loop(0, seq // step, chunks, 0, unroll=2)

    return pl.pallas_call(
        body, name="attn_a_fwd", grid=(bl, npair),
        in_specs=[
            BS((None, seq, 128), lambda b, h: (0, b, h)),
            BS((None, seq, 128), lambda b, h: (1, b, h)),
            BS((None, seq, 128), lambda b, h: (2, b, h)),
            BS((None, STEP_ROWS, WINDOW), lambda b, h: (h, 0, 0)),
        ],
        out_specs=BS((seq, 128), lambda b, h: (b, h)), out_shape=SDS((t, dn), BF16),
        scratch_shapes=[pltpu.VMEM((PAD + seq, 128), BF16), pltpu.VMEM((PAD + seq, 128), BF16)],
        compiler_params=_cparams(2),
    )(qkv3, qkv3, qkv3, bias_win)


def _attn_a_bwd(qkv3, do, bias_win, bl, seq):
    t, dn = qkv3.shape[1:]
    npair = dn // 128
    step = CHUNKS_PER_STEP * CHUNK

    def body(q_ref, k_ref, v_ref, do_ref, b_ref, dqkv_ref, db_ref, kpad, vpad, dkacc, dvacc):
        b = pl.program_id(1)
        kpad[0:PAD, :] = jnp.zeros((PAD, 128), BF16)
        vpad[0:PAD, :] = jnp.zeros((PAD, 128), BF16)
        kpad[PAD:, :] = k_ref[...]
        vpad[PAD:, :] = v_ref[...]
        dkacc[...] = jnp.zeros_like(dkacc)
        dvacc[...] = jnp.zeros_like(dvacc)

        @pl.when(b == 0)
        def _():
            db_ref[...] = jnp.zeros_like(db_ref)

        lane = lax.broadcasted_iota(jnp.int32, (CHUNK, 128), 1)

        def chunks(it, carry):
            r0 = pl.multiple_of(it * step, step)
            q_rows = _step_rows(q_ref[pl.ds(r0, step), :], lane)
            do_rows = _step_rows(do_ref[pl.ds(r0, step), :], lane)
            kwin = kpad[pl.ds(r0, WINDOW), :]
            vwin = vpad[pl.ds(r0, WINDOW), :]
            p = _window_probs(q_rows, kwin, b_ref[...], PAD - r0)
            dp = _dot_nt(do_rows, vwin)
            ds = p * (dp - jnp.sum(p * dp, axis=-1, keepdims=True))
            db_ref[...] += ds
            dsb = (ds * (CHUNK ** -0.5)).astype(BF16)
            dqkv_ref[0, pl.ds(r0, step), :] = _pair_rows(_dot(dsb, kwin), lane).astype(BF16)
            dkacc[pl.ds(r0, WINDOW), :] += _dot_tn(dsb, q_rows)
            dvacc[pl.ds(r0, WINDOW), :] += _dot_tn(p.astype(BF16), do_rows)
            return carry

        lax.fori_loop(0, seq // step, chunks, 0, unroll=2)
        dqkv_ref[1] = dkacc[PAD:, :].astype(BF16)
        dqkv_ref[2] = dvacc[PAD:, :].astype(BF16)

    return pl.pallas_call(
        body, name="attn_a_bwd", grid=(npair, bl),
        in_specs=[
            BS((None, seq, 128), lambda h, b: (0, b, h)),
            BS((None, seq, 128), lambda h, b: (1, b, h)),
            BS((None, seq, 128), lambda h, b: (2, b, h)),
            BS((seq, 128), lambda h, b: (b, h)),
            BS((None, STEP_ROWS, WINDOW), lambda h, b: (h, 0, 0)),
        ],
        out_specs=[BS((3, seq, 128), lambda h, b: (0, b, h)), BS((None, STEP_ROWS, WINDOW), lambda h, b: (h, 0, 0))],
        out_shape=[SDS((3, t, dn), BF16), SDS((HEADS_A // 2, STEP_ROWS, WINDOW), F32)],
        scratch_shapes=[
            pltpu.VMEM((PAD + seq, 128), BF16), pltpu.VMEM((PAD + seq, 128), BF16),
            pltpu.VMEM((PAD + seq, 128), F32), pltpu.VMEM((PAD + seq, 128), F32),
        ],
        compiler_params=_cparams(2),
    )(qkv3, qkv3, qkv3, do, bias_win)


def _rope_tables(seq):
    half = ROPE // 2
    freqs = ROPE_THETA ** (-jnp.arange(half, dtype=F32) / half)
    ang = jnp.arange(seq, dtype=F32)[:, None] * freqs[None, :]
    cos, sin = jnp.cos(ang), jnp.sin(ang)
    c64 = jnp.concatenate([cos, cos], axis=1)
    s64 = jnp.concatenate([-sin, sin], axis=1)
    c192 = jnp.concatenate([jnp.ones((seq, NOPE), F32), c64], axis=1)
    s192 = jnp.concatenate([jnp.zeros((seq, NOPE), F32), s64], axis=1)
    p64 = np.zeros((ROPE, ROPE), np.float32)
    for col in range(ROPE):
        p64[(col + half) % ROPE, col] = 1.0
    p192 = np.zeros((QK_B, QK_B), np.float32)
    p192[NOPE:, NOPE:] = p64
    return c64, s64, jnp.asarray(p64), c192, s192, jnp.asarray(p192)


def _rope(xv, cos, sin_signed, swap):
    return xv * cos + _dot_exact(xv, swap) * sin_signed


def _rope_bwd(dy, cos, sin_signed, swap):
    return dy * cos + _dot_exact(dy * sin_signed, swap)


def _q_down(hn, w_dq, q_norm):
    t, dn = hn.shape
    ql = w_dq.shape[1]
    tm = _tile(t)

    def body(x_ref, w_ref, g_ref, pre_ref, cq_ref):
        pre = _dot(x_ref[...], w_ref[...])
        pre_ref[...] = pre
        cq_ref[...] = (pre * _rms_scale(pre) * g_ref[...]).astype(BF16)

    return pl.pallas_call(
        body, name="q_down", grid=(t // tm,),
        in_specs=[BS((tm, dn), lambda i: (i, 0)), BS((dn, ql), lambda i: (0, 0)), BS((1, ql), lambda i: (0, 0))],
        out_specs=[BS((tm, ql), lambda i: (i, 0))] * 2, out_shape=[SDS((t, ql), F32), SDS((t, ql), BF16)],
        compiler_params=_cparams(1),
    )(hn, w_dq, q_norm)


def _q_up(cq, w_uq, c192, s192, p192, seq):
    t, ql = cq.shape
    tm = _tile(min(seq, 512), min(seq, 512))
    nseq = seq // tm

    def body(x_ref, w_ref, c_ref, s_ref, p_ref, o_ref):
        xv = x_ref[...]
        for h in range(HEADS_B):
            o_ref[h] = _rope(_dot(xv, w_ref[h]), c_ref[...], s_ref[...], p_ref[...]).astype(BF16)

    pos = BS((tm, QK_B), lambda i: (i % nseq, 0))
    return pl.pallas_call(
        body, name="q_up", grid=(t // tm,),
        in_specs=[
            BS((tm, ql), lambda i: (i, 0)), BS((HEADS_B, ql, QK_B), lambda i: (0, 0, 0)), pos, pos,
            BS((QK_B, QK_B), lambda i: (0, 0)),
        ],
        out_specs=BS((HEADS_B, tm, QK_B), lambda i: (0, i, 0)), out_shape=SDS((HEADS_B, t, QK_B), BF16),
        compiler_params=_cparams(1),
    )(cq, w_uq, c192, s192, p192)


def _kv_down(hk, w_down, latent_norm, c64, s64, p64, seq):
    t, dn = hk.shape
    wd = w_down.shape[1]
    tm = _tile(min(seq, 512), min(seq, 512))
    nseq = seq // tm

    def body(x_ref, w_ref, g_ref, c_ref, s_ref, p_ref, ckr_ref, ckv_ref, kr_ref):
        ckr = _dot(x_ref[...], w_ref[...])
        ckr_ref[...] = ckr
        lat = ckr[:, :KV_LORA]
        ckv_ref[...] = (lat * _rms_scale(lat) * g_ref[...]).astype(BF16)
        kr_ref[...] = _rope(ckr[:, KV_LORA:], c_ref[...], s_ref[...], p_ref[...]).astype(BF16)

    pos = BS((tm, ROPE), lambda i: (i % nseq, 0))
    return pl.pallas_call(
        body, name="kv_down", grid=(t // tm,),
        in_specs=[
            BS((tm, dn), lambda i: (i, 0)), BS((dn, wd), lambda i: (0, 0)), BS((1, KV_LORA), lambda i: (0, 0)), pos, pos,
            BS((ROPE, ROPE), lambda i: (0, 0)),
        ],
        out_specs=[BS((tm, wd), lambda i: (i, 0)), BS((tm, KV_LORA), lambda i: (i, 0)), BS((tm, ROPE), lambda i: (i, 0))],
        out_shape=[SDS((t, wd), F32), SDS((t, KV_LORA), BF16), SDS((t, ROPE), BF16)],
        compiler_params=_cparams(1),
    )(hk, w_down, latent_norm, c64, s64, p64)


def _kv_up(ckv, w_up):
    t, kl = ckv.shape
    hb = w_up.shape[-1]
    tm = _tile(t)

    def body(x_ref, w_ref, o_ref):
        xv = x_ref[...]
        for h in range(HEADS_B):
            o_ref[:, h * hb : (h + 1) * hb] = _dot(xv, w_ref[h]).astype(BF16)

    return pl.pallas_call(
        body, name="kv_up", grid=(t // tm,),
        in_specs=[BS((tm, kl), lambda i: (i, 0)), BS((HEADS_B, kl, hb), lambda i: (0, 0, 0))],
        out_specs=BS((tm, HEADS_B * hb), lambda i: (i, 0)), out_shape=SDS((t, HEADS_B * hb), BF16),
        compiler_params=_cparams(1),
    )(ckv, w_up)


def _mla_diagonal_mask(tq):
    rows = lax.broadcasted_iota(jnp.int32, (tq, tq), 0)
    cols = lax.broadcasted_iota(jnp.int32, (tq, tq), 1)
    return jnp.where(jnp.right_shift(cols, 6) <= jnp.right_shift(rows, 6), 0.0, NEG_INF)


def _mla_key_tiles(n_keys, tk):
    return [(slice(k0, min(k0 + tk, n_keys)), min(k0 + tk, n_keys) == n_keys) for k0 in range(0, n_keys, tk)]


def _mla_scores(qi, kt, diagonal):
    s = _dot_nt(qi, kt) * (QK_B ** -0.5)
    if diagonal is None:
        return s
    tq, width = s.shape
    own = s[:, width - tq :] + diagonal
    return own if width == tq else jnp.concatenate([s[:, : width - tq], own], axis=1)


def _mla_fwd(q, kv, kr, bl, seq):
    t = kv.shape[0]
    tq = min(MLA_TQ, seq)

    def body(q_ref, kn_ref, v_ref, kr_ref, o_ref, lse_ref):
        kcat = jnp.concatenate([kn_ref[...], kr_ref[...]], axis=1)
        vv = v_ref[...]
        diagonal = _mla_diagonal_mask(tq)
        for i in range(seq // tq):
            rows = slice(i * tq, (i + 1) * tq)
            qi = q_ref[rows, :]
            m = total = acc = None
            for keys, own in _mla_key_tiles((i + 1) * tq, MLA_TK_FWD):
                s = _mla_scores(qi, kcat[keys], diagonal if own else None)
                m_blk = jnp.max(s, axis=-1, keepdims=True)
                if m is None:
                    m_new = m_blk
                    e = jnp.exp(s - m_new)
                    total = jnp.sum(e, axis=-1, keepdims=True)
                    acc = _dot(e.astype(BF16), vv[keys])
                else:
                    m_new = jnp.maximum(m, m_blk)
                    keep = jnp.exp(m - m_new)
                    e = jnp.exp(s - m_new)
                    total = keep * total + jnp.sum(e, axis=-1, keepdims=True)
                    acc = keep * acc + _dot(e.astype(BF16), vv[keys])
                m = m_new
            o_ref[rows, :] = (acc / total).astype(BF16)
            lse_ref[rows, :] = m + jnp.log(total)

    return pl.pallas_call(
        body, name="mla_fwd", grid=(bl, HEADS_B),
        in_specs=[
            BS((None, seq, QK_B), lambda b, h: (h, b, 0)),
            BS((seq, NOPE), lambda b, h: (b, 2 * h)),
            BS((seq, V_DIM), lambda b, h: (b, 2 * h + 1)),
            BS((seq, ROPE), lambda b, h: (b, 0)),
        ],
        out_specs=[BS((seq, V_DIM), lambda b, h: (b, h)), BS((None, seq, 1), lambda b, h: (h, b, 0))],
        out_shape=[SDS((t, HEADS_B * V_DIM), BF16), SDS((HEADS_B, t, 1), F32)],
        compiler_params=_cparams(2),
    )(q, kv, kv, kr)


def _mla_bwd(q, kv, kr, o, lse, do, c192, s192, p192, bl, seq):
    t = kv.shape[0]
    tq = min(MLA_TQ, seq)

    def body(q_ref, kn_ref, v_ref, kr_ref, o_ref, lse_ref, do_ref, c_ref, s_ref, p_ref, dq_ref, dkv_ref, dkr_ref, dkacc, dvacc):
        h = pl.program_id(1)
        kcat = jnp.concatenate([kn_ref[...], kr_ref[...]], axis=1)
        vv = v_ref[...]
        dkacc[...] = jnp.zeros_like(dkacc)
        dvacc[...] = jnp.zeros_like(dvacc)
        diagonal = _mla_diagonal_mask(tq)
        for i in range(seq // tq):
            rows = slice(i * tq, (i + 1) * tq)
            qi = q_ref[rows, :]
            doi = do_ref[rows, :]
            lse_i = lse_ref[rows, :]
            delta = jnp.sum(doi.astype(F32) * o_ref[rows, :].astype(F32), axis=-1, keepdims=True)
            dq = None
            for keys, own in _mla_key_tiles((i + 1) * tq, MLA_TK_BWD):
                p = jnp.exp(_mla_scores(qi, kcat[keys], diagonal if own else None) - lse_i)
                ds = p * (_dot_nt(doi, vv[keys]) - delta)
                dsb = (ds * (QK_B ** -0.5)).astype(BF16)
                dq_blk = _dot(dsb, kcat[keys])
                dq = dq_blk if dq is None else dq + dq_blk
                dkacc[keys, :] += _dot_tn(dsb, qi)
                dvacc[keys, :] += _dot_tn(p.astype(BF16), doi)
            dq_ref[rows, :] = _rope_bwd(dq, c_ref[rows, :], s_ref[rows, :], p_ref[...]).astype(BF16)
        dk = dkacc[...]
        dkv_ref[:, :NOPE] = dk[:, :NOPE].astype(BF16)
        dkv_ref[:, NOPE:] = dvacc[...].astype(BF16)

        @pl.when(h == 0)
        def _():
            dkr_ref[...] = dk[:, NOPE:]

        @pl.when(h > 0)
        def _():
            dkr_ref[...] += dk[:, NOPE:]

    return pl.pallas_call(
        body, name="mla_bwd", grid=(bl, HEADS_B),
        in_specs=[
            BS((None, seq, QK_B), lambda b, h: (h, b, 0)),
            BS((seq, NOPE), lambda b, h: (b, 2 * h)),
            BS((seq, V_DIM), lambda b, h: (b, 2 * h + 1)),
            BS((seq, ROPE), lambda b, h: (b, 0)),
            BS((seq, V_DIM), lambda b, h: (b, h)),
            BS((None, seq, 1), lambda b, h: (h, b, 0)),
            BS((seq, V_DIM), lambda b, h: (b, h)),
            BS((seq, QK_B), lambda b, h: (0, 0)),
            BS((seq, QK_B), lambda b, h: (0, 0)),
            BS((QK_B, QK_B), lambda b, h: (0, 0)),
        ],
        out_specs=[
            BS((None, seq, QK_B), lambda b, h: (h, b, 0)),
            BS((seq, NOPE + V_DIM), lambda b, h: (b, h)),
            BS((seq, ROPE), lambda b, h: (b, 0)),
        ],
        out_shape=[SDS((HEADS_B, t, QK_B), BF16), SDS((t, HEADS_B * (NOPE + V_DIM)), BF16), SDS((t, ROPE), F32)],
        scratch_shapes=[pltpu.VMEM((seq, QK_B), F32), pltpu.VMEM((seq, V_DIM), F32)],
        compiler_params=_cparams(2),
    )(q, kv, kv, kr, o, lse, do, c192, s192, p192)


def _loss_final(h, target, gamma):
    t, dn = h.shape
    tm = _tile(t)
    nt = t // tm

    def body(h_ref, t_ref, g_ref, dh_ref, dhb_ref, dg_ref, loss_ref):
        i = pl.program_id(0)
        hv = h_ref[...]
        r = _rms_scale(hv)
        hh = hv * r
        gam = g_ref[...]
        err = hh * gam - t_ref[...]
        part = 0.5 * jnp.sum(jnp.mean(err * err, axis=-1, keepdims=True))

        @pl.when(i == 0)
        def _():
            loss_ref[...] = jnp.zeros_like(loss_ref)

        loss_ref[...] += part
        dy = err * (1.0 / dn)
        _acc_rows(dg_ref, dy * hh, i, nt)
        t1 = dy * gam
        dh = r * (t1 - hh * jnp.mean(t1 * hh, axis=-1, keepdims=True))
        dh_ref[...] = dh
        dhb_ref[...] = dh.astype(BF16)

    row = BS((tm, dn), lambda i: (i, 0))
    return pl.pallas_call(
        body, name="loss_final", grid=(nt,),
        in_specs=[row, row, BS((1, dn), lambda i: (0, 0))],
        out_specs=[row, row, BS((8, dn), lambda i: (0, 0)), BS((8, 128), lambda i: (0, 0))],
        out_shape=[SDS((t, dn), F32), SDS((t, dn), BF16), SDS((8, dn), F32), SDS((8, 128), F32)],
        compiler_params=_cparams(1),
    )(h, target, gamma)


def _ffn_bwd_in(name, dh, w_out, layer, gu):
    t, dn = dh.shape
    tm = _tile(t, 1024)

    def body(dh_ref, w_ref, gu_ref, o_ref):
        da = 0.5 * _dot_nt(dh_ref[...], w_ref[...])
        g = gu_ref[0].astype(F32)
        u = gu_ref[1].astype(F32)
        sg = jax.nn.sigmoid(g)
        o_ref[0] = (da * u * (sg * (1.0 + g * (1.0 - sg)))).astype(BF16)
        o_ref[1] = (da * (g * sg)).astype(BF16)

    blk = BS((None, 2, tm, FB), lambda j, i: (j, 0, i, 0))
    return pl.pallas_call(
        body, name=name, grid=(NJ, t // tm),
        in_specs=[BS((tm, dn), lambda j, i: (i, 0)), BS((None, None, FB, dn), lambda j, i: (layer, j, 0, 0)), blk],
        out_specs=blk, out_shape=SDS((NJ, 2, t, FB), BF16),
        compiler_params=_cparams(2),
    )(dh, w_out, gu)


def _mm_nt_plain(name, xf, w):
    t, dn = xf.shape
    n = w.shape[0]
    tm = _tile(t)

    def body(x_ref, w_ref, o_ref):
        o_ref[...] = _dot_nt(x_ref[...], w_ref[...]).astype(BF16)

    return pl.pallas_call(
        body, name=name, grid=(t // tm,),
        in_specs=[BS((tm, dn), lambda i: (i, 0)), BS((n, dn), lambda i: (0, 0))],
        out_specs=BS((tm, n), lambda i: (i, 0)), out_shape=SDS((t, n), BF16),
        compiler_params=_cparams(1),
    )(xf, w)


def _mm_tn(name, xa, x_spec, ya, y_spec, out_shape, out_spec, nj, scale=None):
    def body(x_ref, y_ref, o_ref):
        acc = _dot_tn(x_ref[...], y_ref[...])
        o_ref[...] = (acc if scale is None else scale * acc).astype(BF16)

    return pl.pallas_call(
        body, name=name, grid=(nj,),
        in_specs=[x_spec, y_spec], out_specs=out_spec, out_shape=SDS(out_shape, BF16),
        compiler_params=_cparams(1),
    )(xa, ya)


def _dw_qkv(hn, dqkv3, wb):
    t, dn = hn.shape
    per = wb // 128

    def body(x_ref, *refs):
        cols = [y_ref[...] for y_ref in refs[:per]]
        refs[per][...] = _dot_tn(x_ref[...], jnp.concatenate(cols, axis=1)).astype(BF16)

    def piece(k):
        return BS((None, t, 128), lambda j: ((per * j + k) // 8, 0, (per * j + k) % 8))

    return pl.pallas_call(
        body, name="dw_qkv", grid=(NDEV,),
        in_specs=[BS((t, dn), lambda j: (0, 0))] + [piece(k) for k in range(per)],
        out_specs=BS((None, dn, wb), lambda j: (j, 0, 0)), out_shape=SDS((NDEV, dn, wb), BF16),
        compiler_params=_cparams(1),
    )(hn, *([dqkv3] * per))


def _mm_nt_epi(name, ya, y_spec, wa, w_spec, nj, n_out, extra, out_shapes, out_specs, epilogue, tm, nt, mm_fn=None):
    n_extra = len(extra)
    n_outs = len(out_shapes)

    def body(*refs):
        y_ref, w_ref = refs[:2]
        ex = refs[2 : 2 + n_extra]
        outs = refs[2 + n_extra : 2 + n_extra + n_outs]
        i = pl.program_id(0)
        j = pl.program_id(1)
        part = _dot_nt(y_ref[...], w_ref[...]) if mm_fn is None else mm_fn(y_ref, w_ref)
        if nj == 1:
            epilogue(part, ex, outs, i, nt)
            return
        acc = refs[-1]

        @pl.when(j == 0)
        def _():
            acc[...] = part

        @pl.when(j > 0)
        def _():
            acc[...] += part

        @pl.when(j == nj - 1)
        def _():
            epilogue(acc[...], ex, outs, i, nt)

    return pl.pallas_call(
        body, name=name, grid=(nt, nj),
        in_specs=[y_spec, w_spec] + [spec for _, spec in extra],
        out_specs=out_specs, out_shape=out_shapes,
        scratch_shapes=[] if nj == 1 else [pltpu.VMEM((tm, n_out), F32)],
        compiler_params=_cparams(2),
    )(ya, wa, *[arr for arr, _ in extra])


def _norm_bwd(dn, hv, gam):
    r = _rms_scale(hv)
    hh = hv * r
    t1 = dn * gam
    return r * (t1 - hh * jnp.mean(t1 * hh, axis=-1, keepdims=True)), dn * hh


def _norm_bwd_epilogue(has_res, out_dtype):
    def epilogue(dn, ex, outs, i, nt):
        dh, dg_rows = _norm_bwd(dn, ex[0][...], ex[1][...])
        _acc_rows(outs[1], dg_rows, i, nt)
        if has_res:
            dh = dh + ex[2][...]
        outs[0][...] = dh.astype(out_dtype)
        if has_res:
            outs[2][...] = dh.astype(BF16)

    return epilogue


def _mm_nt_norm_bwd(name, ya, y_spec, wa, w_spec, nj, h, gamma, res, out_dtype, mm_fn=None, want_tm=512):
    t, n = h.shape
    tm = _tile(t, want_tm)
    nt = t // tm
    row = BS((tm, n), lambda i, j: (i, 0))
    extra = [(h, row), (gamma, BS((1, n), lambda i, j: (0, 0)))]
    out_shapes = [SDS((t, n), out_dtype), SDS((8, n), F32)]
    out_specs = [row, BS((8, n), lambda i, j: (0, 0))]
    if res is not None:
        extra.append((res, row))
        out_shapes.append(SDS((t, n), BF16))
        out_specs.append(row)
    return _mm_nt_epi(
        name, ya, y_spec, wa, w_spec, nj, n, extra, out_shapes, out_specs, _norm_bwd_epilogue(res is not None, out_dtype), tm, nt, mm_fn,
    )


def _dev_block(jj):
    return jj // 2 + NJ * (jj % 2)


def _ffn_dn_mm(y_ref, w_ref):
    acc = None
    for jj in range(2 * NJ):
        part = _dot_nt(y_ref[jj], w_ref[_dev_block(jj)])
        acc = part if acc is None else acc + part
    return acc


def _ffn_bwd(tag, dh, dhb, n_in, h_in, gamma, gu, a, w_in, w_out, more_grads, collective_id):
    t, dn = dh.shape
    dgu = _ffn_bwd_in(f"{tag}_bwd_in", dhb, w_out, 0, gu).reshape(2 * NJ, t, FB)
    dw_out = _mm_tn(
        f"{tag}_dw_out", a, BS((None, t, FB), lambda j: (j, 0, 0)), dhb, BS((t, dn), lambda j: (0, 0)),
        (NJ, FB, dn), BS((None, FB, dn), lambda j: (j, 0, 0)), NJ, scale=0.5,
    )
    dw_in = _mm_tn(
        f"{tag}_dw_in", dgu, BS((None, t, FB), lambda j: (j, 0, 0)), n_in, BS((t, dn), lambda j: (0, 0)),
        (NDEV, FB, dn), BS((None, FB, dn), lambda j: (_dev_block(j), 0, 0)), NDEV,
    )
    entries = [("scatter", dw_in), ("scatter", dw_out.reshape(NDEV, NJ * FB // NDEV, dn))] + [("scatter", g) for g in more_grads]
    landed = _exchange_sc(f"{tag}_reduce", entries, collective_id)
    tm = _tile(t, 256)
    dh_in, dgam, dhb_in = _mm_nt_norm_bwd(
        f"{tag}_dn", dgu, BS((2 * NJ, tm, FB), lambda i, j: (0, i, 0)),
        w_in, BS((None, NDEV, dn, FB), lambda i, j: (0, 0, 0, 0)), 1, h_in, gamma, dh, F32, mm_fn=_ffn_dn_mm, want_tm=256,
    )
    return dh_in, dhb_in, dgam, landed


def _heads_mm(y_ref, w_ref):
    acc = None
    for h in range(HEADS_B):
        part = _dot_nt(y_ref[h], w_ref[h])
        acc = part if acc is None else acc + part
    return acc


def _dqkv_mm(per):
    def mm(y_ref, w_ref):
        acc = None
        for j in range(NDEV):
            cols = [y_ref[(per * j + k) // 8, :, ((per * j + k) % 8) * 128 : ((per * j + k) % 8 + 1) * 128] for k in range(per)]
            part = _dot_nt(jnp.concatenate(cols, axis=1), w_ref[j])
            acc = part if acc is None else acc + part
        return acc

    return mm


def _kv_latent_bwd(dkv, w_up, ckr, latent_norm, dkr, c64, s64, p64, seq):
    t, wd = ckr.shape
    hb = w_up.shape[-1]
    tm = _tile(min(seq, 512), min(seq, 512))
    nt = t // tm
    nseq = seq // tm

    def epilogue(dn, ex, outs, i, nt_):
        dlat, dg_rows = _norm_bwd(dn, ex[0][...], ex[1][...])
        _acc_rows(outs[1], dg_rows, i, nt_)
        outs[0][:, :KV_LORA] = dlat.astype(BF16)
        outs[0][:, KV_LORA:] = _rope_bwd(ex[2][...], ex[3][...], ex[4][...], ex[5][...]).astype(BF16)

    pos = BS((tm, ROPE), lambda i, j: (i % nseq, 0))
    extra = [
        (ckr, BS((tm, KV_LORA), lambda i, j: (i, 0))), (latent_norm, BS((1, KV_LORA), lambda i, j: (0, 0))),
        (dkr, BS((tm, ROPE), lambda i, j: (i, 0))), (c64, pos), (s64, pos), (p64, BS((ROPE, ROPE), lambda i, j: (0, 0))),
    ]
    def heads_mm(y_ref, w_ref):
        acc = None
        for h in range(HEADS_B):
            part = _dot_nt(y_ref[:, h * hb : (h + 1) * hb], w_ref[h])
            acc = part if acc is None else acc + part
        return acc

    return _mm_nt_epi(
        "kv_latent_bwd", dkv, BS((tm, HEADS_B * hb), lambda i, j: (i, 0)), w_up, BS((HEADS_B, KV_LORA, hb), lambda i, j: (0, 0, 0)),
        1, KV_LORA, extra, [SDS((t, wd), BF16), SDS((8, KV_LORA), F32)],
        [BS((tm, wd), lambda i, j: (i, 0)), BS((8, KV_LORA), lambda i, j: (0, 0))], epilogue, tm, nt, heads_mm,
    )


def _adamw(name, parts, w, m, v):
    n_layers, rows, cols = w.shape
    tr = max(d for d in range(8, min(rows, 256) + 1, 8) if rows % d == 0)
    nb = rows // tr

    def body(*refs):
        p_refs = refs[:n_layers]
        w_ref, m_ref, v_ref, g_ref, d_ref, nm_ref, nv_ref = refs[n_layers : n_layers + 7]
        layer = pl.program_id(0)
        for lp in range(n_layers):

            @pl.when(layer == lp)
            def _():
                g = p_refs[lp][0].astype(F32)
                for k in range(1, NDEV):
                    g = g + p_refs[lp][k].astype(F32)
                g_ref[...] = g

        g = g_ref[...]
        nm = ADAM_B1 * m_ref[...] + (1.0 - ADAM_B1) * g
        nv = ADAM_B2 * v_ref[...] + (1.0 - ADAM_B2) * (g * g)
        nm_ref[...] = nm
        nv_ref[...] = nv
        m_hat = nm / (1.0 - ADAM_B1 ** ADAM_STEP)
        v_hat = nv / (1.0 - ADAM_B2 ** ADAM_STEP)
        d_ref[...] = -ADAM_LR * (m_hat / (jnp.sqrt(v_hat) + ADAM_EPS) + ADAM_WD * w_ref[...])

    def part_spec(lp):
        return BS((NDEV, tr, cols), lambda l, i: (0, jnp.where(l == lp, i, jnp.where(l < lp, 0, nb - 1)), 0))

    row = BS((None, tr, cols), lambda l, i: (l, i, 0))
    return pl.pallas_call(
        body, name=name, grid=(n_layers, nb),
        in_specs=[part_spec(lp) for lp in range(n_layers)] + [row, row, row],
        out_specs=[row] * 4, out_shape=[SDS(w.shape, F32)] * 4,
        compiler_params=_cparams(2),
    )(*parts, w, m, v)


def _pack_small(ffn1_norm, mix_norm, ffn2_norm, kv_norm, final_norm, q_norm, latent_norm, rel_bias, last_row):
    dn = ffn1_norm.shape[-1]

    def rows_of(a, n_rows):
        flat = a.reshape(-1)
        return jnp.pad(flat, (0, n_rows * dn - flat.shape[0])).reshape(n_rows, dn)

    return jnp.concatenate(
        [
            ffn1_norm.reshape(2, dn), mix_norm.reshape(2, dn), ffn2_norm.reshape(2, dn), kv_norm.reshape(1, dn),
            final_norm.reshape(1, dn), rows_of(q_norm, 1), rows_of(latent_norm, 1), rows_of(rel_bias, 5), rows_of(last_row, 1),
        ],
        axis=0,
    )


def _unpack_small(pack):
    dn = pack.shape[-1]
    return dict(
        ffn1_norm=pack[0:2], mix_norm=pack[2:4], ffn2_norm=pack[4:6], kv_norm=pack[6], final_norm=pack[7],
        b_q_norm=pack[8, :Q_LORA].reshape(1, Q_LORA), kv_latent_norm=pack[9, :KV_LORA],
        a_rel_bias=pack[10:15].reshape(-1)[: HEADS_A * NREL].reshape(1, HEADS_A, NREL), last=pack[15],
    )


def kernel(x, ffn1_norm, ffn1_w_in, ffn1_w_out, mix_norm, ffn2_norm, ffn2_w_in, ffn2_w_out, a_w_qkv, a_rel_bias, a_w_o, kv_norm, kv_w_down, kv_latent_norm, kv_w_up, b_w_dq, b_q_norm, b_w_uq, b_w_o, final_norm, loss_target, m_ffn1_norm, m_ffn1_w_in, m_ffn1_w_out, m_mix_norm, m_ffn2_norm, m_ffn2_w_in, m_ffn2_w_out, m_a_w_qkv, m_a_rel_bias, m_a_w_o, m_kv_norm, m_kv_w_down, m_kv_latent_norm, m_kv_w_up, m_b_w_dq, m_b_q_norm, m_b_w_uq, m_b_w_o, m_final_norm, v_ffn1_norm, v_ffn1_w_in, v_ffn1_w_out, v_mix_norm, v_ffn2_norm, v_ffn2_w_in, v_ffn2_w_out, v_a_w_qkv, v_a_rel_bias, v_a_w_o, v_kv_norm, v_kv_w_down, v_kv_latent_norm, v_kv_w_up, v_b_w_dq, v_b_q_norm, v_b_w_uq, v_b_w_o, v_final_norm):
    bl, seq, dn = x.shape
    t = bl * seq
    tm = _tile(t)
    nt = t // tm
    x2 = x.reshape(t, dn)
    target2 = loss_target.reshape(t, dn)

    def gathered(*ws):
        return [("gather", w.astype(BF16)) for w in ws]

    groups = [
        gathered(ffn1_w_in[0]), gathered(ffn1_w_out[0]), gathered(a_w_qkv[0], a_w_o[0]), gathered(ffn2_w_in[0], ffn2_w_out[0]),
        gathered(kv_w_down, kv_w_up), gathered(ffn1_w_in[1], ffn1_w_out[1]), gathered(b_w_dq[0], b_w_uq[0], b_w_o[0]),
        gathered(ffn2_w_in[1], ffn2_w_out[1]),
    ]
    ag = [_exchange_sc(f"gather_{k}", group, GATHER_IDS[k]) for k, group in enumerate(groups)]

    def as_w_in(w):
        return w.reshape(1, NDEV, dn, FB)

    def as_w_out(w):
        return w.reshape(1, NJ, FB, dn)

    c64, s64, p64, c192, s192, p192 = _rope_tables(seq)
    q_norm = b_q_norm.reshape(1, Q_LORA)
    latent_norm = kv_latent_norm.reshape(1, KV_LORA)
    bias = _window_bias(_rel_bias_fwd(jnp.pad(a_rel_bias[0], ((0, 0), (0, NREL_PAD - NREL)))))

    h0, h1, h2, n1, hn, n2, gu1, gu2, a1, a2, w_in1, w_in2, w_out1, w_out2 = ([None, None] for _ in range(14))
    h0[0] = x2
    (n1[0],) = _norm_fwd("norm_x", x2, ffn1_norm[0:1])
    w_in1[0] = as_w_in(ag[0][0])
    gu1[0], a1[0] = _ffn_in("ffn1_in_0", n1[0], w_in1[0], 0)
    w_out1[0] = as_w_out(ag[1][0])
    h1[0], hn[0] = _mm_res_norm("ffn1_out_0", a1[0], w_out1[0], 0, h0[0], mix_norm[0:1], 0.5)
    w_qkv, w_o_a = ag[2]
    qkv_wb = w_qkv.shape[-1]
    w_o_a = w_o_a.reshape(1, 1, dn, dn)
    qkv3 = _qkv_proj("qkv_proj", hn[0], w_qkv)
    o_a = _attn_a_fwd(qkv3, bias, bl, seq)
    h2[0], n2[0] = _mm_res_norm("attn_a_out", o_a.reshape(1, t, dn), w_o_a, 0, h1[0], ffn2_norm[0:1], 1.0)
    w_in2[0], w_out2[0] = as_w_in(ag[3][0]), as_w_out(ag[3][1])
    gu2[0], a2[0] = _ffn_in("ffn2_in_0", n2[0], w_in2[0], 0)
    h0[1], hk, n1[1] = _mm_res_norm(
        "ffn2_out_0", a2[0], w_out2[0], 0, h2[0], jnp.concatenate([kv_norm.reshape(1, dn), ffn1_norm[1:2]], axis=0), 0.5
    )
    w_down, w_up = ag[4]
    w_down = w_down.reshape(dn, KV_LORA + ROPE)
    ckr, ckv, kr = _kv_down(hk, w_down, latent_norm, c64, s64, p64, seq)
    kv = _kv_up(ckv, w_up)
    w_in1[1], w_out1[1] = as_w_in(ag[5][0]), as_w_out(ag[5][1])
    gu1[1], a1[1] = _ffn_in("ffn1_in_1", n1[1], w_in1[1], 0)
    h1[1], hn[1] = _mm_res_norm("ffn1_out_1", a1[1], w_out1[1], 0, h0[1], mix_norm[1:2], 0.5)
    w_dq, w_uq, w_o_b = ag[6]
    w_dq = w_dq.reshape(dn, Q_LORA)
    w_o_b = w_o_b.reshape(1, 1, dn, dn)
    cq_pre, cq = _q_down(hn[1], w_dq, q_norm)
    q = _q_up(cq, w_uq, c192, s192, p192, seq)
    o_b, lse_b = _mla_fwd(q, kv, kr, bl, seq)
    h2[1], n2[1] = _mm_res_norm("attn_b_out", o_b.reshape(1, t, dn), w_o_b, 0, h1[1], ffn2_norm[1:2], 1.0)
    w_in2[1], w_out2[1] = as_w_in(ag[7][0]), as_w_out(ag[7][1])
    gu2[1], a2[1] = _ffn_in("ffn2_in_1", n2[1], w_in2[1], 0)
    (h_last,) = _mm_res_norm("ffn2_out_1", a2[1], w_out2[1], 0, h2[1], None, 0.5)
    dh, dhb, dg_final, loss_part = _loss_final(h_last, target2, final_norm.reshape(1, dn))

    dg_ffn1, dg_mix, dg_ffn2, rs_ffn1, rs_ffn2 = ([None, None] for _ in range(5))

    def whole(rows, cols):
        return BS((rows, cols), lambda j: (0, 0))

    def dw_rows(name, xa, ya):
        n = ya.shape[1]
        return _mm_tn(name, xa, whole(t, dn), ya, whole(t, n), (dn, n), whole(dn, n), 1).reshape(NDEV, dn // NDEV, n)

    dh, dhb, dg_ffn2[1], rs_ffn2[1] = _ffn_bwd(
        "ffn2_1", dh, dhb, n2[1], h2[1], ffn2_norm[1:2], gu2[1], a2[1], w_in2[1], w_out2[1], [], REDUCE_IDS[0]
    )
    do_b = _mm_nt_plain("attn_b_do", dhb, w_o_b.reshape(dn, dn))
    dw_o_b = dw_rows("attn_b_dwo", o_b, dhb)
    dq_pre, dkv, dkr = _mla_bwd(q, kv, kr, o_b, lse_b, do_b, c192, s192, p192, bl, seq)
    dw_uq = _mm_tn(
        "dw_uq", cq, whole(t, Q_LORA), dq_pre, BS((None, t, QK_B), lambda j: (j, 0, 0)),
        (HEADS_B, Q_LORA, QK_B), BS((None, Q_LORA, QK_B), lambda j: (j, 0, 0)), HEADS_B,
    )
    dcq_pre, dg_q = _mm_nt_norm_bwd(
        "dcq", dq_pre, BS((HEADS_B, tm, QK_B), lambda i, j: (0, i, 0)), w_uq, BS((HEADS_B, Q_LORA, QK_B), lambda i, j: (0, 0, 0)),
        1, cq_pre, q_norm, None, BF16, mm_fn=_heads_mm,
    )
    dw_dq = dw_rows("dw_dq", hn[1], dcq_pre)
    dh, dg_mix[1], dhb = _mm_nt_norm_bwd(
        "dhn_b", dcq_pre, BS((tm, Q_LORA), lambda i, j: (i, 0)), w_dq, BS((dn, Q_LORA), lambda i, j: (0, 0)),
        1, h1[1], mix_norm[1:2], dh, F32,
    )
    dh, dhb, dg_ffn1[1], rs_ffn1[1] = _ffn_bwd(
        "ffn1_1", dh, dhb, n1[1], h0[1], ffn1_norm[1:2], gu1[1], a1[1], w_in1[1], w_out1[1], [dw_o_b, dw_uq, dw_dq], REDUCE_IDS[1]
    )
    dw_up = _mm_tn(
        "dw_up", ckv, whole(t, KV_LORA), dkv, BS((t, NOPE + V_DIM), lambda j: (0, j)),
        (HEADS_B, KV_LORA, NOPE + V_DIM), BS((None, KV_LORA, NOPE + V_DIM), lambda j: (j, 0, 0)), HEADS_B,
    )
    dckr, dg_latent = _kv_latent_bwd(dkv, w_up, ckr, latent_norm, dkr, c64, s64, p64, seq)
    dw_down = dw_rows("dw_down", hk, dckr)
    dh, dg_kv, dhb = _mm_nt_norm_bwd(
        "dhk", dckr, BS((tm, KV_LORA + ROPE), lambda i, j: (i, 0)), w_down, BS((dn, KV_LORA + ROPE), lambda i, j: (0, 0)),
        1, h0[1], kv_norm.reshape(1, dn), dh, F32,
    )
    dh, dhb, dg_ffn2[0], rs_ffn2[0] = _ffn_bwd(
        "ffn2_0", dh, dhb, n2[0], h2[0], ffn2_norm[0:1], gu2[0], a2[0], w_in2[0], w_out2[0], [dw_up, dw_down], REDUCE_IDS[2]
    )
    do_a = _mm_nt_plain("attn_a_do", dhb, w_o_a.reshape(dn, dn))
    dw_o_a = dw_rows("attn_a_dwo", o_a, dhb)
    dqkv3, dbias = _attn_a_bwd(qkv3, do_a, bias, bl, seq)
    dw_qkv = _dw_qkv(hn[0], dqkv3, qkv_wb)
    dh, dg_mix[0], dhb = _mm_nt_norm_bwd(
        "dhn_a", dqkv3, BS((3, tm, dn), lambda i, j: (0, i, 0)), w_qkv, BS((NDEV, dn, qkv_wb), lambda i, j: (0, 0, 0)),
        1, h1[0], mix_norm[0:1], dh, F32, mm_fn=_dqkv_mm(qkv_wb // 128),
    )
    r_o_a, r_qkv = _exchange_sc("attn_a_reduce", [("scatter", dw_o_a), ("scatter", dw_qkv)], REDUCE_IDS[3])
    dh, dhb, dg_ffn1[0], rs_ffn1[0] = _ffn_bwd(
        "ffn1_0", dh, dhb, n1[0], h0[0], ffn1_norm[0:1], gu1[0], a1[0], w_in1[0], w_out1[0], [], REDUCE_IDS[4]
    )
    grad_x = dh.reshape(bl, seq, dn)
    dtable = _rel_bias_bwd(_window_bias_bwd(dbias))[:, :NREL]

    small = _pack_small(
        jnp.stack([dg_ffn1[0][0], dg_ffn1[1][0]]), jnp.stack([dg_mix[0][0], dg_mix[1][0]]), jnp.stack([dg_ffn2[0][0], dg_ffn2[1][0]]),
        dg_kv[0], dg_final[0], dg_q[0], dg_latent[0], dtable, loss_part[0],
    )
    (r_small,) = _exchange("gather_small_grads", [("gather", small)])

    def update(name, parts, w, m, v):
        shape3 = (len(parts),) + w.shape[-2:]
        parts = [p.reshape((NDEV,) + shape3[1:]) for p in parts]
        outs = _adamw(name, parts, w.reshape(shape3), m.reshape(shape3), v.reshape(shape3))
        return [o.reshape(w.shape) for o in outs]

    res = {}
    r_in2_1, r_out2_1 = rs_ffn2[1]
    r_in1_1, r_out1_1, r_o_b, r_uq, r_dq = rs_ffn1[1]
    r_in2_0, r_out2_0, r_up, r_down = rs_ffn2[0]
    r_in1_0, r_out1_0 = rs_ffn1[0]
    def update_transposed(name, parts, w, m, v):
        outs = update(name, parts, *[jnp.swapaxes(a, 1, 2) for a in (w, m, v)])
        return [jnp.swapaxes(o, 1, 2) for o in outs]

    res["ffn2_w_in"] = update_transposed("adamw_ffn2_w_in", [r_in2_0, r_in2_1], ffn2_w_in, m_ffn2_w_in, v_ffn2_w_in)
    res["ffn2_w_out"] = update("adamw_ffn2_w_out", [r_out2_0, r_out2_1], ffn2_w_out, m_ffn2_w_out, v_ffn2_w_out)
    res["kv_w_down"] = update("adamw_kv_w_down", [r_down], kv_w_down, m_kv_w_down, v_kv_w_down)
    res["kv_w_up"] = update("adamw_kv_w_up", [r_up], kv_w_up, m_kv_w_up, v_kv_w_up)
    res["b_w_dq"] = update("adamw_b_w_dq", [r_dq], b_w_dq, m_b_w_dq, v_b_w_dq)
    res["b_w_uq"] = update("adamw_b_w_uq", [r_uq], b_w_uq, m_b_w_uq, v_b_w_uq)
    res["b_w_o"] = update("adamw_b_w_o", [r_o_b], b_w_o, m_b_w_o, v_b_w_o)
    res["a_w_qkv"] = update("adamw_a_w_qkv", [r_qkv], a_w_qkv, m_a_w_qkv, v_a_w_qkv)
    res["a_w_o"] = update("adamw_a_w_o", [r_o_a], a_w_o, m_a_w_o, v_a_w_o)
    res["ffn1_w_in"] = update_transposed("adamw_ffn1_w_in", [r_in1_0, r_in1_1], ffn1_w_in, m_ffn1_w_in, v_ffn1_w_in)
    res["ffn1_w_out"] = update("adamw_ffn1_w_out", [r_out1_0, r_out1_1], ffn1_w_out, m_ffn1_w_out, v_ffn1_w_out)
    zero_row = jnp.zeros((dn,), F32)
    packs = [
        _pack_small(f1, mx, f2, kvn, fin, qn, lat, rel, zero_row)
        for f1, mx, f2, kvn, fin, qn, lat, rel in (
            (ffn1_norm, mix_norm, ffn2_norm, kv_norm, final_norm, b_q_norm, kv_latent_norm, a_rel_bias),
            (m_ffn1_norm, m_mix_norm, m_ffn2_norm, m_kv_norm, m_final_norm, m_b_q_norm, m_kv_latent_norm, m_a_rel_bias),
            (v_ffn1_norm, v_mix_norm, v_ffn2_norm, v_kv_norm, v_final_norm, v_b_q_norm, v_kv_latent_norm, v_a_rel_bias),
        )
    ]
    small_out = [_unpack_small(o[0]) for o in _adamw("adamw_small", [r_small], *[p[None] for p in packs])]
    for name in ("ffn1_norm", "mix_norm", "ffn2_norm", "a_rel_bias", "kv_norm", "kv_latent_norm", "b_q_norm", "final_norm"):
        res[name] = [so[name] for so in small_out]
    loss = small_out[0]["last"][0]

    order = [
        "ffn1_norm", "ffn1_w_in", "ffn1_w_out", "mix_norm", "ffn2_norm", "ffn2_w_in", "ffn2_w_out", "a_w_qkv", "a_rel_bias",
        "a_w_o", "kv_norm", "kv_w_down", "kv_latent_norm", "kv_w_up", "b_w_dq", "b_q_norm", "b_w_uq", "b_w_o", "final_norm",
    ]
    return (loss, grad_x, *[res[n][0] for n in order], *[res[n][1] for n in order], *[res[n][2] for n in order], *[res[n][3] for n in order])
```

```python
import jax
import jax.numpy as jnp
import numpy as np
from jax import lax
from jax.experimental import pallas as pl
from jax.experimental.pallas import tpu as pltpu
from jax.experimental.pallas import tpu_sc as plsc

NDEV = 8
D_MODEL = 1024
D_FF = 2816
FB = 2 * D_FF // NDEV
NJ = D_FF // FB
CHUNK = 64
LEFT_CHUNKS = 8
PAD = LEFT_CHUNKS * CHUNK
BAND = PAD + CHUNK
CHUNKS_PER_STEP = 4
WINDOW = PAD + CHUNKS_PER_STEP * CHUNK
STEP_ROWS = CHUNKS_PER_STEP * 2 * CHUNK
MAX_REL = 128
NREL = 2 * MAX_REL + 1
NREL_PAD = 384
HEADS_A = 16
HEADS_B = 8
NOPE = 128
ROPE = 64
QK_B = NOPE + ROPE
V_DIM = 128
Q_LORA = 768
KV_LORA = 256
ROPE_THETA = 10000.0
EPS = 1e-6
NEG_INF = -1e30
MLA_TQ = 256
MLA_TK_FWD = 256
MLA_TK_BWD = 1024
ADAM_LR = 0.001
ADAM_B1 = 0.9
ADAM_B2 = 0.999
ADAM_EPS = 1e-08
ADAM_WD = 0.01
ADAM_STEP = 10
PACK_ROWS = 16
GATHER_IDS = tuple(range(1, 9))
REDUCE_IDS = tuple(range(9, 14))
VMEM_LIMIT_BYTES = 56 * 1024 * 1024

F32 = jnp.float32
BF16 = jnp.bfloat16
SDS = jax.ShapeDtypeStruct
BS = pl.BlockSpec
MESH = pl.DeviceIdType.MESH


def _cparams(n_axes):
    return pltpu.CompilerParams(dimension_semantics=("arbitrary",) * n_axes, vmem_limit_bytes=VMEM_LIMIT_BYTES)


def _tile(t, want=512):
    return want if t % want == 0 else t


def _dot(a, b):
    return jnp.dot(a, b, preferred_element_type=F32)


def _dot_nt(a, b):
    return lax.dot_general(a, b, (((1,), (1,)), ((), ())), preferred_element_type=F32)


def _dot_tn(a, b):
    return lax.dot_general(a, b, (((0,), (0,)), ((), ())), preferred_element_type=F32)


def _split3(a):
    hi = a.astype(BF16)
    rest = a - hi.astype(F32)
    mid = rest.astype(BF16)
    return hi, mid, (rest - mid.astype(F32)).astype(BF16)


def _dot_exact(a, onehot, transposed=False):
    ob = onehot.astype(BF16)
    dot = _dot_nt if transposed else _dot
    hi, mid, lo = _split3(a)
    return dot(hi, ob) + dot(mid, ob) + dot(lo, ob)


def _rms_scale(h):
    return lax.rsqrt(jnp.mean(h * h, axis=-1, keepdims=True) + EPS)


def _acc_rows(ref, val, step, n_steps):
    part = val.reshape(val.shape[0] // 8, 8, val.shape[1]).sum(axis=0)

    @pl.when(step == 0)
    def _():
        ref[...] = part

    @pl.when(step > 0)
    def _():
        ref[...] += part

    @pl.when(step == n_steps - 1)
    def _():
        ref[...] = jnp.broadcast_to(jnp.sum(ref[...], axis=0, keepdims=True), ref.shape)


def _exchange_plan(entries):
    ins = [e[1] for e in entries]
    kinds = [e[0] for e in entries]
    lands = [SDS((NDEV,) + a.shape if k == "gather" else a.shape, a.dtype) for k, a in zip(kinds, ins)]
    return ins, lands, kinds


def _mesh_place():
    x, y, c = lax.axis_index("x"), lax.axis_index("y"), lax.axis_index("c")
    return (x, y, c), 4 * x + 2 * y + c


def _flipped(place, p):
    x, y, c = place
    px = 1 - x if p & 4 else x
    py = 1 - y if p & 2 else y
    pc = 1 - c if p & 1 else c
    return (px, py, pc), 4 * px + 2 * py + pc


def _ends(kind, src_ref, land_ref, origin, target):
    if kind == "gather":
        return src_ref, land_ref.at[origin]
    return src_ref.at[target], land_ref.at[origin]


def _remote(kind, src_ref, land_ref, send_sems, recv_sems, k, p, place, me, arriving):
    peer_pos, peer = _flipped(place, p)
    src, dst = _ends(kind, src_ref, land_ref, me, peer)
    if arriving:
        dst = _ends(kind, src_ref, land_ref, peer, me)[1]
    sem = k * (NDEV - 1) + p - 1
    return pltpu.make_async_remote_copy(
        src_ref=src, dst_ref=dst, send_sem=send_sems.at[sem], recv_sem=recv_sems.at[sem], device_id=peer_pos, device_id_type=MESH,
    )


def _exchange(name, entries):
    ins, lands, kinds = _exchange_plan(entries)
    n = len(ins)

    def body(*refs):
        in_refs, land_refs = refs[:n], refs[n : 2 * n]
        send_sems, recv_sems, local_sems = refs[2 * n :]
        place, me = _mesh_place()
        local = []
        for k in range(n):
            src, dst = _ends(kinds[k], in_refs[k], land_refs[k], me, me)
            local.append(pltpu.make_async_copy(src, dst, local_sems.at[k]))
            local[-1].start()
        sends = []
        for p in range(1, NDEV):
            for k in range(n):
                sends.append(_remote(kinds[k], in_refs[k], land_refs[k], send_sems, recv_sems, k, p, place, me, False))
                sends[-1].start()
        for p in range(1, NDEV):
            for k in range(n):
                _remote(kinds[k], in_refs[k], land_refs[k], send_sems, recv_sems, k, p, place, me, True).wait_recv()
        for cp in sends:
            cp.wait_send()
        for cp in local:
            cp.wait()

    any_spec = BS(memory_space=pl.ANY)
    return pl.pallas_call(
        body, name=name, out_shape=lands, in_specs=[any_spec] * n, out_specs=[any_spec] * n,
        scratch_shapes=[
            pltpu.SemaphoreType.DMA((n * (NDEV - 1),)), pltpu.SemaphoreType.DMA((n * (NDEV - 1),)), pltpu.SemaphoreType.DMA((n,)),
        ],
    )(*ins)


def _exchange_sc(name, entries, collective_id):
    ins, lands, kinds = _exchange_plan(entries)
    n = len(ins)

    def launch(*refs):
        in_refs, land_refs = refs[:n], refs[n : 2 * n]
        send_sems, recv_sems, local_sems = refs[2 * n :]
        place, me = _mesh_place()
        barrier = pltpu.get_barrier_semaphore()
        for p in range(1, NDEV):
            pl.semaphore_signal(barrier, inc=1, device_id=_flipped(place, p)[0], device_id_type=MESH)
        pl.semaphore_wait(barrier, NDEV - 1)
        local = []
        for k in range(n):
            src, dst = _ends(kinds[k], in_refs[k], land_refs[k], me, me)
            local.append(pltpu.make_async_copy(src, dst, local_sems.at[k]))
            local[-1].start()
        sends = []
        if all(kind == "gather" for kind in kinds):
            for p in (1, 2, 4, 6):
                for k in range(n):
                    sends.append(_remote(kinds[k], in_refs[k], land_refs[k], send_sems, recv_sems, k, p, place, me, False))
                    sends[-1].start()
            sibling_pos, _ = _flipped(place, 1)
            for f in (2, 4, 6):
                _, origin = _flipped(place, f)
                for k in range(n):
                    _remote(kinds[k], in_refs[k], land_refs[k], send_sems, recv_sems, k, f, place, me, True).wait_recv()
                    sem = k * (NDEV - 1) + f
                    sends.append(
                        pltpu.make_async_remote_copy(
                            src_ref=land_refs[k].at[origin], dst_ref=land_refs[k].at[origin], send_sem=send_sems.at[sem],
                            recv_sem=recv_sems.at[sem], device_id=sibling_pos, device_id_type=MESH,
                        )
                    )
                    sends[-1].start()
            for p in (1, 3, 5, 7):
                for k in range(n):
                    _remote(kinds[k], in_refs[k], land_refs[k], send_sems, recv_sems, k, p, place, me, True).wait_recv()
        else:
            for p in range(1, NDEV):
                for k in range(n):
                    sends.append(_remote(kinds[k], in_refs[k], land_refs[k], send_sems, recv_sems, k, p, place, me, False))
                    sends[-1].start()
            for p in range(1, NDEV):
                for k in range(n):
                    _remote(kinds[k], in_refs[k], land_refs[k], send_sems, recv_sems, k, p, place, me, True).wait_recv()
        for cp in sends:
            cp.wait_send()
        for cp in local:
            cp.wait()

    return pl.kernel(
        launch, out_type=tuple(lands), mesh=plsc.ScalarSubcoreMesh(axis_name="sequencer", num_cores=1), name=name,
        scratch_types=(
            pltpu.SemaphoreType.DMA((n * (NDEV - 1),)), pltpu.SemaphoreType.DMA((n * (NDEV - 1),)), pltpu.SemaphoreType.DMA((n,)),
        ),
        compiler_params=pltpu.CompilerParams(collective_id=collective_id),
    )(*ins)


def _norm_fwd(name, h, gammas):
    t, dn = h.shape
    ng = gammas.shape[0]
    tm = _tile(t)

    def body(h_ref, g_ref, *outs):
        hv = h_ref[...]
        hh = hv * _rms_scale(hv)
        for i, o_ref in enumerate(outs):
            o_ref[...] = (hh * g_ref[i : i + 1, :]).astype(BF16)

    row = BS((tm, dn), lambda i: (i, 0))
    return pl.pallas_call(
        body, name=name, grid=(t // tm,),
        in_specs=[row, BS((ng, dn), lambda i: (0, 0))],
        out_specs=[row] * ng, out_shape=[SDS((t, dn), BF16)] * ng,
        compiler_params=_cparams(1),
    )(h, gammas)


def _ffn_in(name, n, w_in, layer):
    t, dn = n.shape
    tm = _tile(t, 1024)

    def body(n_ref, wg_ref, wu_ref, gu_ref, a_ref):
        xv = n_ref[...]
        g = _dot(xv, wg_ref[...])
        u = _dot(xv, wu_ref[...])
        gu_ref[0] = g.astype(BF16)
        gu_ref[1] = u.astype(BF16)
        a_ref[...] = (g * jax.nn.sigmoid(g) * u).astype(BF16)

    return pl.pallas_call(
        body, name=name, grid=(NJ, t // tm),
        in_specs=[
            BS((tm, dn), lambda j, i: (i, 0)),
            BS((None, None, dn, FB), lambda j, i: (layer, j, 0, 0)),
            BS((None, None, dn, FB), lambda j, i: (layer, j + NJ, 0, 0)),
        ],
        out_specs=[BS((None, 2, tm, FB), lambda j, i: (j, 0, i, 0)), BS((None, tm, FB), lambda j, i: (j, i, 0))],
        out_shape=[SDS((NJ, 2, t, FB), BF16), SDS((NJ, t, FB), BF16)],
        compiler_params=_cparams(2),
    )(n, w_in, w_in)


def _mm_res_norm(name, a, w, layer, h_in, gammas, scale):
    nk, t, kb = a.shape
    dn = w.shape[-1]
    ng = 0 if gammas is None else gammas.shape[0]
    tm = _tile(t)

    def body(*refs):
        a_ref, w_ref, h_ref = refs[:3]
        g_ref = refs[3] if ng else None
        outs = refs[3 + (1 if ng else 0) :]
        acc = _dot(a_ref[0], w_ref[0])
        for k in range(1, nk):
            acc += _dot(a_ref[k], w_ref[k])
        ho = h_ref[...] + scale * acc
        outs[0][...] = ho
        if ng:
            hh = ho * _rms_scale(ho)
            for i in range(ng):
                outs[1 + i][...] = (hh * g_ref[i : i + 1, :]).astype(BF16)

    row = BS((tm, dn), lambda i: (i, 0))
    in_specs = [BS((nk, tm, kb), lambda i: (0, i, 0)), BS((None, nk, kb, dn), lambda i: (layer, 0, 0, 0)), row]
    args = [a, w, h_in]
    if ng:
        in_specs.append(BS((ng, dn), lambda i: (0, 0)))
        args.append(gammas)
    return pl.pallas_call(
        body, name=name, grid=(t // tm,),
        in_specs=in_specs,
        out_specs=[row] * (1 + ng), out_shape=[SDS((t, dn), F32)] + [SDS((t, dn), BF16)] * ng,
        compiler_params=_cparams(1),
    )(*args)


def _qkv_proj(name, hn, w_qkv):
    t, dn = hn.shape
    wb = w_qkv.shape[-1]
    per = wb // 128
    tm = _tile(t)

    def body(x_ref, w_ref, o_ref):
        xv = x_ref[...]
        for j in range(NDEV):
            yv = _dot(xv, w_ref[j]).astype(BF16)
            for i in range(per):
                n = per * j + i
                o_ref[n // 8, :, (n % 8) * 128 : (n % 8 + 1) * 128] = yv[:, i * 128 : (i + 1) * 128]

    return pl.pallas_call(
        body, name=name, grid=(t // tm,),
        in_specs=[BS((tm, dn), lambda i: (i, 0)), BS((NDEV, dn, wb), lambda i: (0, 0, 0))],
        out_specs=BS((3, tm, dn), lambda i: (0, i, 0)), out_shape=SDS((3, t, dn), BF16),
        compiler_params=_cparams(1),
    )(hn, w_qkv)


def _rel_onehot(i):
    r = lax.broadcasted_iota(jnp.int32, (NREL_PAD, BAND), 0)
    j = lax.broadcasted_iota(jnp.int32, (NREL_PAD, BAND), 1)
    idx = jnp.clip(PAD + i - j, -MAX_REL, MAX_REL) + MAX_REL
    return (idx == r).astype(F32)


def _rel_bias_fwd(table):
    def body(t_ref, o_ref):
        i8 = pl.program_id(0)
        for ii in range(8):
            o_ref[:, ii, :] = _dot_exact(t_ref[...], _rel_onehot(i8 * 8 + ii))

    return pl.pallas_call(
        body, name="rel_bias_fwd", grid=(CHUNK // 8,),
        in_specs=[BS((HEADS_A, NREL_PAD), lambda i: (0, 0))],
        out_specs=BS((HEADS_A, 8, BAND), lambda i: (0, i, 0)), out_shape=SDS((HEADS_A, CHUNK, BAND), F32),
        compiler_params=_cparams(1),
    )(table)


def _rel_bias_bwd(dbias):
    def body(d_ref, o_ref):
        i8 = pl.program_id(0)
        acc = jnp.zeros((HEADS_A, NREL_PAD), F32)
        for ii in range(8):
            acc += _dot_exact(d_ref[:, ii, :], _rel_onehot(i8 * 8 + ii), transposed=True)

        @pl.when(i8 == 0)
        def _():
            o_ref[...] = acc

        @pl.when(i8 > 0)
        def _():
            o_ref[...] += acc

    return pl.pallas_call(
        body, name="rel_bias_bwd", grid=(CHUNK // 8,),
        in_specs=[BS((HEADS_A, 8, BAND), lambda i: (0, i, 0))],
        out_specs=BS((HEADS_A, NREL_PAD), lambda i: (0, 0)), out_shape=SDS((HEADS_A, NREL_PAD), F32),
        compiler_params=_cparams(1),
    )(dbias)


def _window_bias(bias):
    b = bias.reshape(HEADS_A // 2, 2, CHUNK, BAND)
    per_chunk = [
        jnp.pad(b, ((0, 0), (0, 0), (0, 0), (cc * CHUNK, WINDOW - BAND - cc * CHUNK)), constant_values=NEG_INF)
        for cc in range(CHUNKS_PER_STEP)
    ]
    return jnp.stack(per_chunk, axis=1).reshape(HEADS_A // 2, STEP_ROWS, WINDOW)


def _window_bias_bwd(dwin):
    d = dwin.reshape(HEADS_A // 2, CHUNKS_PER_STEP, 2, CHUNK, WINDOW)
    return sum(d[:, cc, :, :, cc * CHUNK : cc * CHUNK + BAND] for cc in range(CHUNKS_PER_STEP)).reshape(HEADS_A, CHUNK, BAND)


def _step_rows(xs, lane):
    parts = []
    for cc in range(CHUNKS_PER_STEP):
        xc = xs[cc * CHUNK : (cc + 1) * CHUNK]
        parts.append(jnp.where(lane < 64, xc, jnp.zeros_like(xc)))
        parts.append(jnp.where(lane >= 64, xc, jnp.zeros_like(xc)))
    return jnp.concatenate(parts, axis=0)


def _pair_rows(ys, lane):
    parts = []
    for cc in range(CHUNKS_PER_STEP):
        y0 = ys[(2 * cc) * CHUNK : (2 * cc + 1) * CHUNK]
        y1 = ys[(2 * cc + 1) * CHUNK : (2 * cc + 2) * CHUNK]
        parts.append(jnp.where(lane < 64, y0, y1))
    return jnp.concatenate(parts, axis=0)


def _window_scores(q_rows, kwin, bias_win, first_key):
    s = _dot_nt(q_rows, kwin) * (CHUNK ** -0.5) + bias_win
    col = lax.broadcasted_iota(jnp.int32, s.shape, 1)
    return jnp.where(col >= first_key, s, NEG_INF)


def _attn_a_fwd(qkv3, bias_win, bl, seq):
    t, dn = qkv3.shape[1:]
    npair = dn // 128
    step = CHUNKS_PER_STEP * CHUNK

    def body(q_ref, k_ref, v_ref, b_ref, o_ref, lse_ref, kpad, vpad):
        kpad[0:PAD, :] = jnp.zeros((PAD, 128), BF16)
        vpad[0:PAD, :] = jnp.zeros((PAD, 128), BF16)
        kpad[PAD:, :] = k_ref[...]
        vpad[PAD:, :] = v_ref[...]
        lane = lax.broadcasted_iota(jnp.int32, (CHUNK, 128), 1)

        def chunks(it, carry):
            r0 = pl.multiple_of(it * step, step)
            q_rows = _step_rows(q_ref[pl.ds(r0, step), :], lane)
            s = _window_scores(q_rows, kpad[pl.ds(r0, WINDOW), :], b_ref[...], PAD - r0)
            m = jnp.max(s, axis=-1, keepdims=True)
            e = jnp.exp(s - m)
            total = jnp.sum(e, axis=-1, keepdims=True)
            o_rows = _dot((e / total).astype(BF16), vpad[pl.ds(r0, WINDOW), :])
            o_ref[pl.ds(r0, step), :] = _pair_rows(o_rows, lane).astype(BF16)
            lse_ref[pl.ds(pl.multiple_of(it * STEP_ROWS, STEP_ROWS), STEP_ROWS), :] = m + jnp.log(total)
            return carry

        lax.fori_loop(0, seq // step, chunks, 0, unroll=2)

    return pl.pallas_call(
        body, name="attn_a_fwd", grid=(bl, npair),
        in_specs=[
            BS((None, seq, 128), lambda b, h: (0, b, h)),
            BS((None, seq, 128), lambda b, h: (1, b, h)),
            BS((None, seq, 128), lambda b, h: (2, b, h)),
            BS((None, STEP_ROWS, WINDOW), lambda b, h: (h, 0, 0)),
        ],
        out_specs=[BS((seq, 128), lambda b, h: (b, h)), BS((None, 2 * seq, 1), lambda b, h: (h, b, 0))],
        out_shape=[SDS((t, dn), BF16), SDS((npair, 2 * t, 1), F32)],
        scratch_shapes=[pltpu.VMEM((PAD + seq, 128), BF16), pltpu.VMEM((PAD + seq, 128), BF16)],
        compiler_params=_cparams(2),
    )(qkv3, qkv3, qkv3, bias_win)


def _attn_a_bwd(qkv3, out, lse, do, bias_win, bl, seq):
    t, dn = qkv3.shape[1:]
    npair = dn // 128
    step = CHUNKS_PER_STEP * CHUNK

    def body(q_ref, k_ref, v_ref, o_ref, lse_ref, do_ref, b_ref, dqkv_ref, db_ref, kpad, vpad, dkacc, dvacc):
        b = pl.program_id(1)
        kpad[0:PAD, :] = jnp.zeros((PAD, 128), BF16)
        vpad[0:PAD, :] = jnp.zeros((PAD, 128), BF16)
        kpad[PAD:, :] = k_ref[...]
        vpad[PAD:, :] = v_ref[...]
        dkacc[...] = jnp.zeros_like(dkacc)
        dvacc[...] = jnp.zeros_like(dvacc)

        @pl.when(b == 0)
        def _():
            db_ref[...] = jnp.zeros_like(db_ref)

        lane = lax.broadcasted_iota(jnp.int32, (CHUNK, 128), 1)

        def chunks(it, carry):
            r0 = pl.multiple_of(it * step, step)
            q_rows = _step_rows(q_ref[pl.ds(r0, step), :], lane)
            do_rows = _step_rows(do_ref[pl.ds(r0, step), :], lane)
            kwin = kpad[pl.ds(r0, WINDOW), :]
            vwin = vpad[pl.ds(r0, WINDOW), :]
            o_rows = _step_rows(o_ref[pl.ds(r0, step), :], lane)
            delta = jnp.sum(do_rows.astype(F32) * o_rows.astype(F32), axis=-1, keepdims=True)
            lse_rows = lse_ref[pl.ds(pl.multiple_of(it * STEP_ROWS, STEP_ROWS), STEP_ROWS), :]
            p = jnp.exp(_window_scores(q_rows, kwin, b_ref[...], PAD - r0) - lse_rows)
            ds = p * (_dot_nt(do_rows, vwin) - delta)
            db_ref[...] += ds
            dsb = (ds * (CHUNK ** -0.5)).astype(BF16)
            dqkv_ref[0, pl.ds(r0, step), :] = _pair_rows(_dot(dsb, kwin), lane).astype(BF16)
            dkacc[pl.ds(r0, WINDOW), :] += _dot_tn(dsb, q_rows)
            dvacc[pl.ds(r0, WINDOW), :] += _dot_tn(p.astype(BF16), do_rows)
            return carry

        lax.fori_loop(0, seq // step, chunks, 0, unroll=2)
        dqkv_ref[1] = dkacc[PAD:, :].astype(BF16)
        dqkv_ref[2] = dvacc[PAD:, :].astype(BF16)

    return pl.pallas_call(
        body, name="attn_a_bwd", grid=(npair, bl),
        in_specs=[
            BS((None, seq, 128), lambda h, b: (0, b, h)),
            BS((None, seq, 128), lambda h, b: (1, b, h)),
            BS((None, seq, 128), lambda h, b: (2, b, h)),
            BS((seq, 128), lambda h, b: (b, h)),
            BS((None, 2 * seq, 1), lambda h, b: (h, b, 0)),
            BS((seq, 128), lambda h, b: (b, h)),
            BS((None, STEP_ROWS, WINDOW), lambda h, b: (h, 0, 0)),
        ],
        out_specs=[BS((3, seq, 128), lambda h, b: (0, b, h)), BS((None, STEP_ROWS, WINDOW), lambda h, b: (h, 0, 0))],
        out_shape=[SDS((3, t, dn), BF16), SDS((HEADS_A // 2, STEP_ROWS, WINDOW), F32)],
        scratch_shapes=[
            pltpu.VMEM((PAD + seq, 128), BF16), pltpu.VMEM((PAD + seq, 128), BF16),
            pltpu.VMEM((PAD + seq, 128), F32), pltpu.VMEM((PAD + seq, 128), F32),
        ],
        compiler_params=_cparams(2),
    )(qkv3, qkv3, qkv3, out, lse, do, bias_win)


def _rope_tables(seq):
    half = ROPE // 2
    freqs = ROPE_THETA ** (-jnp.arange(half, dtype=F32) / half)
    ang = jnp.arange(seq, dtype=F32)[:, None] * freqs[None, :]
    cos, sin = jnp.cos(ang), jnp.sin(ang)
    c64 = jnp.concatenate([cos, cos], axis=1)
    s64 = jnp.concatenate([-sin, sin], axis=1)
    c192 = jnp.concatenate([jnp.ones((seq, NOPE), F32), c64], axis=1)
    s192 = jnp.concatenate([jnp.zeros((seq, NOPE), F32), s64], axis=1)
    p64 = np.zeros((ROPE, ROPE), np.float32)
    for col in range(ROPE):
        p64[(col + half) % ROPE, col] = 1.0
    p192 = np.zeros((QK_B, QK_B), np.float32)
    p192[NOPE:, NOPE:] = p64
    return c64, s64, jnp.asarray(p64), c192, s192, jnp.asarray(p192)


def _rope(xv, cos, sin_signed, swap):
    return xv * cos + _dot_exact(xv, swap) * sin_signed


def _rope_bwd(dy, cos, sin_signed, swap):
    return dy * cos + _dot_exact(dy * sin_signed, swap)


def _q_down(hn, w_dq, q_norm):
    t, dn = hn.shape
    ql = w_dq.shape[1]
    tm = _tile(t)

    def body(x_ref, w_ref, g_ref, pre_ref, cq_ref):
        pre = _dot(x_ref[...], w_ref[...])
        pre_ref[...] = pre
        cq_ref[...] = (pre * _rms_scale(pre) * g_ref[...]).astype(BF16)

    return pl.pallas_call(
        body, name="q_down", grid=(t // tm,),
        in_specs=[BS((tm, dn), lambda i: (i, 0)), BS((dn, ql), lambda i: (0, 0)), BS((1, ql), lambda i: (0, 0))],
        out_specs=[BS((tm, ql), lambda i: (i, 0))] * 2, out_shape=[SDS((t, ql), F32), SDS((t, ql), BF16)],
        compiler_params=_cparams(1),
    )(hn, w_dq, q_norm)


def _q_up(cq, w_uq, c192, s192, p192, seq):
    t, ql = cq.shape
    tm = _tile(min(seq, 512), min(seq, 512))
    nseq = seq // tm

    def body(x_ref, w_ref, c_ref, s_ref, p_ref, o_ref):
        xv = x_ref[...]
        for h in range(HEADS_B):
            o_ref[h] = _rope(_dot(xv, w_ref[h]), c_ref[...], s_ref[...], p_ref[...]).astype(BF16)

    pos = BS((tm, QK_B), lambda i: (i % nseq, 0))
    return pl.pallas_call(
        body, name="q_up", grid=(t // tm,),
        in_specs=[
            BS((tm, ql), lambda i: (i, 0)), BS((HEADS_B, ql, QK_B), lambda i: (0, 0, 0)), pos, pos,
            BS((QK_B, QK_B), lambda i: (0, 0)),
        ],
        out_specs=BS((HEADS_B, tm, QK_B), lambda i: (0, i, 0)), out_shape=SDS((HEADS_B, t, QK_B), BF16),
        compiler_params=_cparams(1),
    )(cq, w_uq, c192, s192, p192)


def _kv_down(hk, w_down, latent_norm, c64, s64, p64, seq):
    t, dn = hk.shape
    wd = w_down.shape[1]
    tm = _tile(min(seq, 512), min(seq, 512))
    nseq = seq // tm

    def body(x_ref, w_ref, g_ref, c_ref, s_ref, p_ref, ckr_ref, ckv_ref, kr_ref):
        ckr = _dot(x_ref[...], w_ref[...])
        ckr_ref[...] = ckr
        lat = ckr[:, :KV_LORA]
        ckv_ref[...] = (lat * _rms_scale(lat) * g_ref[...]).astype(BF16)
        kr_ref[...] = _rope(ckr[:, KV_LORA:], c_ref[...], s_ref[...], p_ref[...]).astype(BF16)

    pos = BS((tm, ROPE), lambda i: (i % nseq, 0))
    return pl.pallas_call(
        body, name="kv_down", grid=(t // tm,),
        in_specs=[
            BS((tm, dn), lambda i: (i, 0)), BS((dn, wd), lambda i: (0, 0)), BS((1, KV_LORA), lambda i: (0, 0)), pos, pos,
            BS((ROPE, ROPE), lambda i: (0, 0)),
        ],
        out_specs=[BS((tm, wd), lambda i: (i, 0)), BS((tm, KV_LORA), lambda i: (i, 0)), BS((tm, ROPE), lambda i: (i, 0))],
        out_shape=[SDS((t, wd), F32), SDS((t, KV_LORA), BF16), SDS((t, ROPE), BF16)],
        compiler_params=_cparams(1),
    )(hk, w_down, latent_norm, c64, s64, p64)


def _kv_up(ckv, w_up):
    t, kl = ckv.shape
    hb = w_up.shape[-1]
    tm = _tile(t)

    def body(x_ref, w_ref, o_ref):
        xv = x_ref[...]
        for h in range(HEADS_B):
            o_ref[:, h * hb : (h + 1) * hb] = _dot(xv, w_ref[h]).astype(BF16)

    return pl.pallas_call(
        body, name="kv_up", grid=(t // tm,),
        in_specs=[BS((tm, kl), lambda i: (i, 0)), BS((HEADS_B, kl, hb), lambda i: (0, 0, 0))],
        out_specs=BS((tm, HEADS_B * hb), lambda i: (i, 0)), out_shape=SDS((t, HEADS_B * hb), BF16),
        compiler_params=_cparams(1),
    )(ckv, w_up)


def _mla_diagonal_mask(tq):
    rows = lax.broadcasted_iota(jnp.int32, (tq, tq), 0)
    cols = lax.broadcasted_iota(jnp.int32, (tq, tq), 1)
    return jnp.where(jnp.right_shift(cols, 6) <= jnp.right_shift(rows, 6), 0.0, NEG_INF)


def _mla_key_tiles(n_keys, tk):
    return [(slice(k0, min(k0 + tk, n_keys)), min(k0 + tk, n_keys) == n_keys) for k0 in range(0, n_keys, tk)]


def _mla_scores(qi, kt, diagonal):
    s = _dot_nt(qi, kt) * (QK_B ** -0.5)
    if diagonal is None:
        return s
    tq, width = s.shape
    own = s[:, width - tq :] + diagonal
    return own if width == tq else jnp.concatenate([s[:, : width - tq], own], axis=1)


def _mla_fwd(q, kv, kr, bl, seq):
    t = kv.shape[0]
    tq = min(MLA_TQ, seq)

    def body(q_ref, kn_ref, v_ref, kr_ref, o_ref, lse_ref):
        kcat = jnp.concatenate([kn_ref[...], kr_ref[...]], axis=1)
        vv = v_ref[...]
        diagonal = _mla_diagonal_mask(tq)
        for i in range(seq // tq):
            rows = slice(i * tq, (i + 1) * tq)
            qi = q_ref[rows, :]
            m = total = acc = None
            for keys, own in _mla_key_tiles((i + 1) * tq, MLA_TK_FWD):
                s = _mla_scores(qi, kcat[keys], diagonal if own else None)
                m_blk = jnp.max(s, axis=-1, keepdims=True)
                if m is None:
                    m_new = m_blk
                    e = jnp.exp(s - m_new)
                    total = jnp.sum(e, axis=-1, keepdims=True)
                    acc = _dot(e.astype(BF16), vv[keys])
                else:
                    m_new = jnp.maximum(m, m_blk)
                    keep = jnp.exp(m - m_new)
                    e = jnp.exp(s - m_new)
                    total = keep * total + jnp.sum(e, axis=-1, keepdims=True)
                    acc = keep * acc + _dot(e.astype(BF16), vv[keys])
                m = m_new
            o_ref[rows, :] = (acc / total).astype(BF16)
            lse_ref[rows, :] = m + jnp.log(total)

    return pl.pallas_call(
        body, name="mla_fwd", grid=(bl, HEADS_B),
        in_specs=[
            BS((None, seq, QK_B), lambda b, h: (h, b, 0)),
            BS((seq, NOPE), lambda b, h: (b, 2 * h)),
            BS((seq, V_DIM), lambda b, h: (b, 2 * h + 1)),
            BS((seq, ROPE), lambda b, h: (b, 0)),
        ],
        out_specs=[BS((seq, V_DIM), lambda b, h: (b, h)), BS((None, seq, 1), lambda b, h: (h, b, 0))],
        out_shape=[SDS((t, HEADS_B * V_DIM), BF16), SDS((HEADS_B, t, 1), F32)],
        compiler_params=_cparams(2),
    )(q, kv, kv, kr)


def _mla_bwd(q, kv, kr, o, lse, do, c192, s192, p192, bl, seq):
    t = kv.shape[0]
    tq = min(MLA_TQ, seq)

    def body(q_ref, kn_ref, v_ref, kr_ref, o_ref, lse_ref, do_ref, c_ref, s_ref, p_ref, dq_ref, dkv_ref, dkr_ref, dkacc, dvacc):
        h = pl.program_id(1)
        kcat = jnp.concatenate([kn_ref[...], kr_ref[...]], axis=1)
        vv = v_ref[...]
        dkacc[...] = jnp.zeros_like(dkacc)
        dvacc[...] = jnp.zeros_like(dvacc)
        diagonal = _mla_diagonal_mask(tq)
        for i in range(seq // tq):
            rows = slice(i * tq, (i + 1) * tq)
            qi = q_ref[rows, :]
            doi = do_ref[rows, :]
            lse_i = lse_ref[rows, :]
            delta = jnp.sum(doi.astype(F32) * o_ref[rows, :].astype(F32), axis=-1, keepdims=True)
            dq = None
            for keys, own in _mla_key_tiles((i + 1) * tq, MLA_TK_BWD):
                p = jnp.exp(_mla_scores(qi, kcat[keys], diagonal if own else None) - lse_i)
                ds = p * (_dot_nt(doi, vv[keys]) - delta)
                dsb = (ds * (QK_B ** -0.5)).astype(BF16)
                dq_blk = _dot(dsb, kcat[keys])
                dq = dq_blk if dq is None else dq + dq_blk
                dkacc[keys, :] += _dot_tn(dsb, qi)
                dvacc[keys, :] += _dot_tn(p.astype(BF16), doi)
            dq_ref[rows, :] = _rope_bwd(dq, c_ref[rows, :], s_ref[rows, :], p_ref[...]).astype(BF16)
        dk = dkacc[...]
        dkv_ref[:, :NOPE] = dk[:, :NOPE].astype(BF16)
        dkv_ref[:, NOPE:] = dvacc[...].astype(BF16)

        @pl.when(h == 0)
        def _():
            dkr_ref[...] = dk[:, NOPE:]

        @pl.when(h > 0)
        def _():
            dkr_ref[...] += dk[:, NOPE:]

    return pl.pallas_call(
        body, name="mla_bwd", grid=(bl, HEADS_B),
        in_specs=[
            BS((None, seq, QK_B), lambda b, h: (h, b, 0)),
            BS((seq, NOPE), lambda b, h: (b, 2 * h)),
            BS((seq, V_DIM), lambda b, h: (b, 2 * h + 1)),
            BS((seq, ROPE), lambda b, h: (b, 0)),
            BS((seq, V_DIM), lambda b, h: (b, h)),
            BS((None, seq, 1), lambda b, h: (h, b, 0)),
            BS((seq, V_DIM), lambda b, h: (b, h)),
            BS((seq, QK_B), lambda b, h: (0, 0)),
            BS((seq, QK_B), lambda b, h: (0, 0)),
            BS((QK_B, QK_B), lambda b, h: (0, 0)),
        ],
        out_specs=[
            BS((None, seq, QK_B), lambda b, h: (h, b, 0)),
            BS((seq, NOPE + V_DIM), lambda b, h: (b, h)),
            BS((seq, ROPE), lambda b, h: (b, 0)),
        ],
        out_shape=[SDS((HEADS_B, t, QK_B), BF16), SDS((t, HEADS_B * (NOPE + V_DIM)), BF16), SDS((t, ROPE), F32)],
        scratch_shapes=[pltpu.VMEM((seq, QK_B), F32), pltpu.VMEM((seq, V_DIM), F32)],
        compiler_params=_cparams(2),
    )(q, kv, kv, kr, o, lse, do, c192, s192, p192)


def _loss_final(h, target, gamma):
    t, dn = h.shape
    tm = _tile(t)
    nt = t // tm

    def body(h_ref, t_ref, g_ref, dh_ref, dhb_ref, dg_ref, loss_ref):
        i = pl.program_id(0)
        hv = h_ref[...]
        r = _rms_scale(hv)
        hh = hv * r
        gam = g_ref[...]
        err = hh * gam - t_ref[...]
        part = 0.5 * jnp.sum(jnp.mean(err * err, axis=-1, keepdims=True))

        @pl.when(i == 0)
        def _():
            loss_ref[...] = jnp.zeros_like(loss_ref)

        loss_ref[...] += part
        dy = err * (1.0 / dn)
        _acc_rows(dg_ref, dy * hh, i, nt)
        t1 = dy * gam
        dh = r * (t1 - hh * jnp.mean(t1 * hh, axis=-1, keepdims=True))
        dh_ref[...] = dh
        dhb_ref[...] = dh.astype(BF16)

    row = BS((tm, dn), lambda i: (i, 0))
    return pl.pallas_call(
        body, name="loss_final", grid=(nt,),
        in_specs=[row, row, BS((1, dn), lambda i: (0, 0))],
        out_specs=[row, row, BS((8, dn), lambda i: (0, 0)), BS((8, 128), lambda i: (0, 0))],
        out_shape=[SDS((t, dn), F32), SDS((t, dn), BF16), SDS((8, dn), F32), SDS((8, 128), F32)],
        compiler_params=_cparams(1),
    )(h, target, gamma)


def _ffn_bwd_in(name, dh, w_out, layer, gu):
    t, dn = dh.shape
    tm = _tile(t, 1024)

    def body(dh_ref, w_ref, gu_ref, o_ref):
        da = 0.5 * _dot_nt(dh_ref[...], w_ref[...])
        g = gu_ref[0].astype(F32)
        u = gu_ref[1].astype(F32)
        sg = jax.nn.sigmoid(g)
        o_ref[0] = (da * u * (sg * (1.0 + g * (1.0 - sg)))).astype(BF16)
        o_ref[1] = (da * (g * sg)).astype(BF16)

    blk = BS((None, 2, tm, FB), lambda j, i: (j, 0, i, 0))
    return pl.pallas_call(
        body, name=name, grid=(NJ, t // tm),
        in_specs=[BS((tm, dn), lambda j, i: (i, 0)), BS((None, None, FB, dn), lambda j, i: (layer, j, 0, 0)), blk],
        out_specs=blk, out_shape=SDS((NJ, 2, t, FB), BF16),
        compiler_params=_cparams(2),
    )(dh, w_out, gu)


def _mm_nt_plain(name, xf, w):
    t, dn = xf.shape
    n = w.shape[0]
    tm = _tile(t)

    def body(x_ref, w_ref, o_ref):
        o_ref[...] = _dot_nt(x_ref[...], w_ref[...]).astype(BF16)

    return pl.pallas_call(
        body, name=name, grid=(t // tm,),
        in_specs=[BS((tm, dn), lambda i: (i, 0)), BS((n, dn), lambda i: (0, 0))],
        out_specs=BS((tm, n), lambda i: (i, 0)), out_shape=SDS((t, n), BF16),
        compiler_params=_cparams(1),
    )(xf, w)


def _mm_tn(name, xa, x_spec, ya, y_spec, out_shape, out_spec, nj, scale=None):
    def body(x_ref, y_ref, o_ref):
        acc = _dot_tn(x_ref[...], y_ref[...])
        o_ref[...] = (acc if scale is None else scale * acc).astype(BF16)

    return pl.pallas_call(
        body, name=name, grid=(nj,),
        in_specs=[x_spec, y_spec], out_specs=out_spec, out_shape=SDS(out_shape, BF16),
        compiler_params=_cparams(1),
    )(xa, ya)


def _dw_qkv(hn, dqkv3, wb):
    t, dn = hn.shape
    per = wb // 128

    def body(x_ref, *refs):
        cols = [y_ref[...] for y_ref in refs[:per]]
        refs[per][...] = _dot_tn(x_ref[...], jnp.concatenate(cols, axis=1)).astype(BF16)

    def piece(k):
        return BS((None, t, 128), lambda j: ((per * j + k) // 8, 0, (per * j + k) % 8))

    return pl.pallas_call(
        body, name="dw_qkv", grid=(NDEV,),
        in_specs=[BS((t, dn), lambda j: (0, 0))] + [piece(k) for k in range(per)],
        out_specs=BS((None, dn, wb), lambda j: (j, 0, 0)), out_shape=SDS((NDEV, dn, wb), BF16),
        compiler_params=_cparams(1),
    )(hn, *([dqkv3] * per))


def _mm_nt_epi(name, ya, y_spec, wa, w_spec, nj, n_out, extra, out_shapes, out_specs, epilogue, tm, nt, mm_fn=None):
    n_extra = len(extra)
    n_outs = len(out_shapes)

    def body(*refs):
        y_ref, w_ref = refs[:2]
        ex = refs[2 : 2 + n_extra]
        outs = refs[2 + n_extra : 2 + n_extra + n_outs]
        i = pl.program_id(0)
        j = pl.program_id(1)
        part = _dot_nt(y_ref[...], w_ref[...]) if mm_fn is None else mm_fn(y_ref, w_ref)
        if nj == 1:
            epilogue(part, ex, outs, i, nt)
            return
        acc = refs[-1]

        @pl.when(j == 0)
        def _():
            acc[...] = part

        @pl.when(j > 0)
        def _():
            acc[...] += part

        @pl.when(j == nj - 1)
        def _():
            epilogue(acc[...], ex, outs, i, nt)

    return pl.pallas_call(
        body, name=name, grid=(nt, nj),
        in_specs=[y_spec, w_spec] + [spec for _, spec in extra],
        out_specs=out_specs, out_shape=out_shapes,
        scratch_shapes=[] if nj == 1 else [pltpu.VMEM((tm, n_out), F32)],
        compiler_params=_cparams(2),
    )(ya, wa, *[arr for arr, _ in extra])


def _norm_bwd(dn, hv, gam):
    r = _rms_scale(hv)
    hh = hv * r
    t1 = dn * gam
    return r * (t1 - hh * jnp.mean(t1 * hh, axis=-1, keepdims=True)), dn * hh


def _norm_bwd_epilogue(has_res, out_dtype):
    def epilogue(dn, ex, outs, i, nt):
        dh, dg_rows = _norm_bwd(dn, ex[0][...], ex[1][...])
        _acc_rows(outs[1], dg_rows, i, nt)
        if has_res:
            dh = dh + ex[2][...]
        outs[0][...] = dh.astype(out_dtype)
        if has_res:
            outs[2][...] = dh.astype(BF16)

    return epilogue


def _mm_nt_norm_bwd(name, ya, y_spec, wa, w_spec, nj, h, gamma, res, out_dtype, mm_fn=None, want_tm=512):
    t, n = h.shape
    tm = _tile(t, want_tm)
    nt = t // tm
    row = BS((tm, n), lambda i, j: (i, 0))
    extra = [(h, row), (gamma, BS((1, n), lambda i, j: (0, 0)))]
    out_shapes = [SDS((t, n), out_dtype), SDS((8, n), F32)]
    out_specs = [row, BS((8, n), lambda i, j: (0, 0))]
    if res is not None:
        extra.append((res, row))
        out_shapes.append(SDS((t, n), BF16))
        out_specs.append(row)
    return _mm_nt_epi(
        name, ya, y_spec, wa, w_spec, nj, n, extra, out_shapes, out_specs, _norm_bwd_epilogue(res is not None, out_dtype), tm, nt, mm_fn,
    )


def _dev_block(jj):
    return jj // 2 + NJ * (jj % 2)


def _ffn_dn_mm(y_ref, w_ref):
    acc = None
    for jj in range(2 * NJ):
        part = _dot_nt(y_ref[jj], w_ref[_dev_block(jj)])
        acc = part if acc is None else acc + part
    return acc


def _ffn_bwd(tag, dh, dhb, n_in, h_in, gamma, gu, a, w_in, w_out, more_grads, collective_id):
    t, dn = dh.shape
    dgu = _ffn_bwd_in(f"{tag}_bwd_in", dhb, w_out, 0, gu).reshape(2 * NJ, t, FB)
    dw_out = _mm_tn(
        f"{tag}_dw_out", a, BS((None, t, FB), lambda j: (j, 0, 0)), dhb, BS((t, dn), lambda j: (0, 0)),
        (NJ, FB, dn), BS((None, FB, dn), lambda j: (j, 0, 0)), NJ, scale=0.5,
    )
    dw_in = _mm_tn(
        f"{tag}_dw_in", dgu, BS((None, t, FB), lambda j: (j, 0, 0)), n_in, BS((t, dn), lambda j: (0, 0)),
        (NDEV, FB, dn), BS((None, FB, dn), lambda j: (_dev_block(j), 0, 0)), NDEV,
    )
    entries = [("scatter", dw_in), ("scatter", dw_out.reshape(NDEV, NJ * FB // NDEV, dn))] + [("scatter", g) for g in more_grads]
    landed = _exchange_sc(f"{tag}_reduce", entries, collective_id)
    tm = _tile(t)
    resident = BS((None, NDEV, dn, FB), lambda i, j: (0, 0, 0, 0), pipeline_mode=pl.Buffered(1))
    dh_in, dgam, dhb_in = _mm_nt_norm_bwd(
        f"{tag}_dn", dgu, BS((2 * NJ, tm, FB), lambda i, j: (0, i, 0)), w_in, resident, 1, h_in, gamma, dh, F32, mm_fn=_ffn_dn_mm,
    )
    return dh_in, dhb_in, dgam, landed


def _heads_mm(y_ref, w_ref):
    acc = None
    for h in range(HEADS_B):
        part = _dot_nt(y_ref[h], w_ref[h])
        acc = part if acc is None else acc + part
    return acc


def _dqkv_mm(per):
    def mm(y_ref, w_ref):
        acc = None
        for j in range(NDEV):
            cols = [y_ref[(per * j + k) // 8, :, ((per * j + k) % 8) * 128 : ((per * j + k) % 8 + 1) * 128] for k in range(per)]
            part = _dot_nt(jnp.concatenate(cols, axis=1), w_ref[j])
            acc = part if acc is None else acc + part
        return acc

    return mm


def _kv_latent_bwd(dkv, w_up, ckr, latent_norm, dkr, c64, s64, p64, seq):
    t, wd = ckr.shape
    hb = w_up.shape[-1]
    tm = _tile(min(seq, 512), min(seq, 512))
    nt = t // tm
    nseq = seq // tm

    def epilogue(dn, ex, outs, i, nt_):
        dlat, dg_rows = _norm_bwd(dn, ex[0][...], ex[1][...])
        _acc_rows(outs[1], dg_rows, i, nt_)
        outs[0][:, :KV_LORA] = dlat.astype(BF16)
        outs[0][:, KV_LORA:] = _rope_bwd(ex[2][...], ex[3][...], ex[4][...], ex[5][...]).astype(BF16)

    pos = BS((tm, ROPE), lambda i, j: (i % nseq, 0))
    extra = [
        (ckr, BS((tm, KV_LORA), lambda i, j: (i, 0))), (latent_norm, BS((1, KV_LORA), lambda i, j: (0, 0))),
        (dkr, BS((tm, ROPE), lambda i, j: (i, 0))), (c64, pos), (s64, pos), (p64, BS((ROPE, ROPE), lambda i, j: (0, 0))),
    ]
    def heads_mm(y_ref, w_ref):
        acc = None
        for h in range(HEADS_B):
            part = _dot_nt(y_ref[:, h * hb : (h + 1) * hb], w_ref[h])
            acc = part if acc is None else acc + part
        return acc

    return _mm_nt_epi(
        "kv_latent_bwd", dkv, BS((tm, HEADS_B * hb), lambda i, j: (i, 0)), w_up, BS((HEADS_B, KV_LORA, hb), lambda i, j: (0, 0, 0)),
        1, KV_LORA, extra, [SDS((t, wd), BF16), SDS((8, KV_LORA), F32)],
        [BS((tm, wd), lambda i, j: (i, 0)), BS((8, KV_LORA), lambda i, j: (0, 0))], epilogue, tm, nt, heads_mm,
    )


def _adamw(name, parts, w, m, v):
    n_layers, rows, cols = w.shape
    tr = max(d for d in range(8, min(rows, 256) + 1, 8) if rows % d == 0)
    nb = rows // tr

    def body(*refs):
        p_refs = refs[:n_layers]
        w_ref, m_ref, v_ref, g_ref, d_ref, nm_ref, nv_ref = refs[n_layers : n_layers + 7]
        layer = pl.program_id(0)
        for lp in range(n_layers):

            @pl.when(layer == lp)
            def _():
                g = p_refs[lp][0].astype(F32)
                for k in range(1, NDEV):
                    g = g + p_refs[lp][k].astype(F32)
                g_ref[...] = g

        g = g_ref[...]
        nm = ADAM_B1 * m_ref[...] + (1.0 - ADAM_B1) * g
        nv = ADAM_B2 * v_ref[...] + (1.0 - ADAM_B2) * (g * g)
        nm_ref[...] = nm
        nv_ref[...] = nv
        m_hat = nm / (1.0 - ADAM_B1 ** ADAM_STEP)
        v_hat = nv / (1.0 - ADAM_B2 ** ADAM_STEP)
        d_ref[...] = -ADAM_LR * (m_hat / (jnp.sqrt(v_hat) + ADAM_EPS) + ADAM_WD * w_ref[...])

    def part_spec(lp):
        return BS((NDEV, tr, cols), lambda l, i: (0, jnp.where(l == lp, i, jnp.where(l < lp, 0, nb - 1)), 0))

    row = BS((None, tr, cols), lambda l, i: (l, i, 0))
    return pl.pallas_call(
        body, name=name, grid=(n_layers, nb),
        in_specs=[part_spec(lp) for lp in range(n_layers)] + [row, row, row],
        out_specs=[row] * 4, out_shape=[SDS(w.shape, F32)] * 4,
        compiler_params=_cparams(2),
    )(*parts, w, m, v)


def _pack_small(ffn1_norm, mix_norm, ffn2_norm, kv_norm, final_norm, q_norm, latent_norm, rel_bias, last_row):
    dn = ffn1_norm.shape[-1]

    def rows_of(a, n_rows):
        flat = a.reshape(-1)
        return jnp.pad(flat, (0, n_rows * dn - flat.shape[0])).reshape(n_rows, dn)

    return jnp.concatenate(
        [
            ffn1_norm.reshape(2, dn), mix_norm.reshape(2, dn), ffn2_norm.reshape(2, dn), kv_norm.reshape(1, dn),
            final_norm.reshape(1, dn), rows_of(q_norm, 1), rows_of(latent_norm, 1), rows_of(rel_bias, 5), rows_of(last_row, 1),
        ],
        axis=0,
    )


def _unpack_small(pack):
    dn = pack.shape[-1]
    return dict(
        ffn1_norm=pack[0:2], mix_norm=pack[2:4], ffn2_norm=pack[4:6], kv_norm=pack[6], final_norm=pack[7],
        b_q_norm=pack[8, :Q_LORA].reshape(1, Q_LORA), kv_latent_norm=pack[9, :KV_LORA],
        a_rel_bias=pack[10:15].reshape(-1)[: HEADS_A * NREL].reshape(1, HEADS_A, NREL), last=pack[15],
    )


def kernel(x, ffn1_norm, ffn1_w_in, ffn1_w_out, mix_norm, ffn2_norm, ffn2_w_in, ffn2_w_out, a_w_qkv, a_rel_bias, a_w_o, kv_norm, kv_w_down, kv_latent_norm, kv_w_up, b_w_dq, b_q_norm, b_w_uq, b_w_o, final_norm, loss_target, m_ffn1_norm, m_ffn1_w_in, m_ffn1_w_out, m_mix_norm, m_ffn2_norm, m_ffn2_w_in, m_ffn2_w_out, m_a_w_qkv, m_a_rel_bias, m_a_w_o, m_kv_norm, m_kv_w_down, m_kv_latent_norm, m_kv_w_up, m_b_w_dq, m_b_q_norm, m_b_w_uq, m_b_w_o, m_final_norm, v_ffn1_norm, v_ffn1_w_in, v_ffn1_w_out, v_mix_norm, v_ffn2_norm, v_ffn2_w_in, v_ffn2_w_out, v_a_w_qkv, v_a_rel_bias, v_a_w_o, v_kv_norm, v_kv_w_down, v_kv_latent_norm, v_kv_w_up, v_b_w_dq, v_b_q_norm, v_b_w_uq, v_b_w_o, v_final_norm):
    bl, seq, dn = x.shape
    t = bl * seq
    tm = _tile(t)
    nt = t // tm
    x2 = x.reshape(t, dn)
    target2 = loss_target.reshape(t, dn)

    def gathered(*ws):
        return [("gather", w.astype(BF16)) for w in ws]

    groups = [
        gathered(ffn1_w_in[0]), gathered(ffn1_w_out[0]), gathered(a_w_qkv[0], a_w_o[0]), gathered(ffn2_w_in[0], ffn2_w_out[0]),
        gathered(kv_w_down, kv_w_up), gathered(ffn1_w_in[1], ffn1_w_out[1]), gathered(b_w_dq[0], b_w_uq[0], b_w_o[0]),
        gathered(ffn2_w_in[1], ffn2_w_out[1]),
    ]
    ag = [_exchange_sc(f"gather_{k}", group, GATHER_IDS[k]) for k, group in enumerate(groups)]

    def as_w_in(w):
        return w.reshape(1, NDEV, dn, FB)

    def as_w_out(w):
        return w.reshape(1, NJ, FB, dn)

    c64, s64, p64, c192, s192, p192 = _rope_tables(seq)
    q_norm = b_q_norm.reshape(1, Q_LORA)
    latent_norm = kv_latent_norm.reshape(1, KV_LORA)
    bias = _window_bias(_rel_bias_fwd(jnp.pad(a_rel_bias[0], ((0, 0), (0, NREL_PAD - NREL)))))

    h0, h1, h2, n1, hn, n2, gu1, gu2, a1, a2, w_in1, w_in2, w_out1, w_out2 = ([None, None] for _ in range(14))
    h0[0] = x2
    (n1[0],) = _norm_fwd("norm_x", x2, ffn1_norm[0:1])
    w_in1[0] = as_w_in(ag[0][0])
    gu1[0], a1[0] = _ffn_in("ffn1_in_0", n1[0], w_in1[0], 0)
    w_out1[0] = as_w_out(ag[1][0])
    h1[0], hn[0] = _mm_res_norm("ffn1_out_0", a1[0], w_out1[0], 0, h0[0], mix_norm[0:1], 0.5)
    w_qkv, w_o_a = ag[2]
    qkv_wb = w_qkv.shape[-1]
    w_o_a = w_o_a.reshape(1, 1, dn, dn)
    qkv3 = _qkv_proj("qkv_proj", hn[0], w_qkv)
    o_a, lse_a = _attn_a_fwd(qkv3, bias, bl, seq)
    h2[0], n2[0] = _mm_res_norm("attn_a_out", o_a.reshape(1, t, dn), w_o_a, 0, h1[0], ffn2_norm[0:1], 1.0)
    w_in2[0], w_out2[0] = as_w_in(ag[3][0]), as_w_out(ag[3][1])
    gu2[0], a2[0] = _ffn_in("ffn2_in_0", n2[0], w_in2[0], 0)
    h0[1], hk, n1[1] = _mm_res_norm(
        "ffn2_out_0", a2[0], w_out2[0], 0, h2[0], jnp.concatenate([kv_norm.reshape(1, dn), ffn1_norm[1:2]], axis=0), 0.5
    )
    w_down, w_up = ag[4]
    w_down = w_down.reshape(dn, KV_LORA + ROPE)
    ckr, ckv, kr = _kv_down(hk, w_down, latent_norm, c64, s64, p64, seq)
    kv = _kv_up(ckv, w_up)
    w_in1[1], w_out1[1] = as_w_in(ag[5][0]), as_w_out(ag[5][1])
    gu1[1], a1[1] = _ffn_in("ffn1_in_1", n1[1], w_in1[1], 0)
    h1[1], hn[1] = _mm_res_norm("ffn1_out_1", a1[1], w_out1[1], 0, h0[1], mix_norm[1:2], 0.5)
    w_dq, w_uq, w_o_b = ag[6]
    w_dq = w_dq.reshape(dn, Q_LORA)
    w_o_b = w_o_b.reshape(1, 1, dn, dn)
    cq_pre, cq = _q_down(hn[1], w_dq, q_norm)
    q = _q_up(cq, w_uq, c192, s192, p192, seq)
    o_b, lse_b = _mla_fwd(q, kv, kr, bl, seq)
    h2[1], n2[1] = _mm_res_norm("attn_b_out", o_b.reshape(1, t, dn), w_o_b, 0, h1[1], ffn2_norm[1:2], 1.0)
    w_in2[1], w_out2[1] = as_w_in(ag[7][0]), as_w_out(ag[7][1])
    gu2[1], a2[1] = _ffn_in("ffn2_in_1", n2[1], w_in2[1], 0)
    (h_last,) = _mm_res_norm("ffn2_out_1", a2[1], w_out2[1], 0, h2[1], None, 0.5)
    dh, dhb, dg_final, loss_part = _loss_final(h_last, target2, final_norm.reshape(1, dn))

    dg_ffn1, dg_mix, dg_ffn2, rs_ffn1, rs_ffn2 = ([None, None] for _ in range(5))

    def whole(rows, cols):
        return BS((rows, cols), lambda j: (0, 0))

    def dw_rows(name, xa, ya):
        n = ya.shape[1]
        return _mm_tn(name, xa, whole(t, dn), ya, whole(t, n), (dn, n), whole(dn, n), 1).reshape(NDEV, dn // NDEV, n)

    dh, dhb, dg_ffn2[1], rs_ffn2[1] = _ffn_bwd(
        "ffn2_1", dh, dhb, n2[1], h2[1], ffn2_norm[1:2], gu2[1], a2[1], w_in2[1], w_out2[1], [], REDUCE_IDS[0]
    )
    do_b = _mm_nt_plain("attn_b_do", dhb, w_o_b.reshape(dn, dn))
    dw_o_b = dw_rows("attn_b_dwo", o_b, dhb)
    dq_pre, dkv, dkr = _mla_bwd(q, kv, kr, o_b, lse_b, do_b, c192, s192, p192, bl, seq)
    dw_uq = _mm_tn(
        "dw_uq", cq, whole(t, Q_LORA), dq_pre, BS((None, t, QK_B), lambda j: (j, 0, 0)),
        (HEADS_B, Q_LORA, QK_B), BS((None, Q_LORA, QK_B), lambda j: (j, 0, 0)), HEADS_B,
    )
    dcq_pre, dg_q = _mm_nt_norm_bwd(
        "dcq", dq_pre, BS((HEADS_B, tm, QK_B), lambda i, j: (0, i, 0)), w_uq, BS((HEADS_B, Q_LORA, QK_B), lambda i, j: (0, 0, 0)),
        1, cq_pre, q_norm, None, BF16, mm_fn=_heads_mm,
    )
    dw_dq = dw_rows("dw_dq", hn[1], dcq_pre)
    dh, dg_mix[1], dhb = _mm_nt_norm_bwd(
        "dhn_b", dcq_pre, BS((tm, Q_LORA), lambda i, j: (i, 0)), w_dq, BS((dn, Q_LORA), lambda i, j: (0, 0)),
        1, h1[1], mix_norm[1:2], dh, F32,
    )
    dh, dhb, dg_ffn1[1], rs_ffn1[1] = _ffn_bwd(
        "ffn1_1", dh, dhb, n1[1], h0[1], ffn1_norm[1:2], gu1[1], a1[1], w_in1[1], w_out1[1], [dw_o_b, dw_uq, dw_dq], REDUCE_IDS[1]
    )
    dw_up = _mm_tn(
        "dw_up", ckv, whole(t, KV_LORA), dkv, BS((t, NOPE + V_DIM), lambda j: (0, j)),
        (HEADS_B, KV_LORA, NOPE + V_DIM), BS((None, KV_LORA, NOPE + V_DIM), lambda j: (j, 0, 0)), HEADS_B,
    )
    dckr, dg_latent = _kv_latent_bwd(dkv, w_up, ckr, latent_norm, dkr, c64, s64, p64, seq)
    dw_down = dw_rows("dw_down", hk, dckr)
    dh, dg_kv, dhb = _mm_nt_norm_bwd(
        "dhk", dckr, BS((tm, KV_LORA + ROPE), lambda i, j: (i, 0)), w_down, BS((dn, KV_LORA + ROPE), lambda i, j: (0, 0)),
        1, h0[1], kv_norm.reshape(1, dn), dh, F32,
    )
    dh, dhb, dg_ffn2[0], rs_ffn2[0] = _ffn_bwd(
        "ffn2_0", dh, dhb, n2[0], h2[0], ffn2_norm[0:1], gu2[0], a2[0], w_in2[0], w_out2[0], [dw_up, dw_down], REDUCE_IDS[2]
    )
    do_a = _mm_nt_plain("attn_a_do", dhb, w_o_a.reshape(dn, dn))
    dw_o_a = dw_rows("attn_a_dwo", o_a, dhb)
    dqkv3, dbias = _attn_a_bwd(qkv3, o_a, lse_a, do_a, bias, bl, seq)
    dw_qkv = _dw_qkv(hn[0], dqkv3, qkv_wb)
    dh, dg_mix[0], dhb = _mm_nt_norm_bwd(
        "dhn_a", dqkv3, BS((3, tm, dn), lambda i, j: (0, i, 0)), w_qkv, BS((NDEV, dn, qkv_wb), lambda i, j: (0, 0, 0)),
        1, h1[0], mix_norm[0:1], dh, F32, mm_fn=_dqkv_mm(qkv_wb // 128),
    )
    r_o_a, r_qkv = _exchange_sc("attn_a_reduce", [("scatter", dw_o_a), ("scatter", dw_qkv)], REDUCE_IDS[3])
    dh, dhb, dg_ffn1[0], rs_ffn1[0] = _ffn_bwd(
        "ffn1_0", dh, dhb, n1[0], h0[0], ffn1_norm[0:1], gu1[0], a1[0], w_in1[0], w_out1[0], [], REDUCE_IDS[4]
    )
    grad_x = dh.reshape(bl, seq, dn)
    dtable = _rel_bias_bwd(_window_bias_bwd(dbias))[:, :NREL]

    small = _pack_small(
        jnp.stack([dg_ffn1[0][0], dg_ffn1[1][0]]), jnp.stack([dg_mix[0][0], dg_mix[1][0]]), jnp.stack([dg_ffn2[0][0], dg_ffn2[1][0]]),
        dg_kv[0], dg_final[0], dg_q[0], dg_latent[0], dtable, loss_part[0],
    )
    (r_small,) = _exchange("gather_small_grads", [("gather", small)])

    def update(name, parts, w, m, v):
        shape3 = (len(parts),) + w.shape[-2:]
        parts = [p.reshape((NDEV,) + shape3[1:]) for p in parts]
        outs = _adamw(name, parts, w.reshape(shape3), m.reshape(shape3), v.reshape(shape3))
        return [o.reshape(w.shape) for o in outs]

    res = {}
    r_in2_1, r_out2_1 = rs_ffn2[1]
    r_in1_1, r_out1_1, r_o_b, r_uq, r_dq = rs_ffn1[1]
    r_in2_0, r_out2_0, r_up, r_down = rs_ffn2[0]
    r_in1_0, r_out1_0 = rs_ffn1[0]
    def update_transposed(name, parts, w, m, v):
        outs = update(name, parts, *[jnp.swapaxes(a, 1, 2) for a in (w, m, v)])
        return [jnp.swapaxes(o, 1, 2) for o in outs]

    res["ffn2_w_in"] = update_transposed("adamw_ffn2_w_in", [r_in2_0, r_in2_1], ffn2_w_in, m_ffn2_w_in, v_ffn2_w_in)
    res["ffn2_w_out"] = update("adamw_ffn2_w_out", [r_out2_0, r_out2_1], ffn2_w_out, m_ffn2_w_out, v_ffn2_w_out)
    res["kv_w_down"] = update("adamw_kv_w_down", [r_down], kv_w_down, m_kv_w_down, v_kv_w_down)
    res["kv_w_up"] = update("adamw_kv_w_up", [r_up], kv_w_up, m_kv_w_up, v_kv_w_up)
    res["b_w_dq"] = update("adamw_b_w_dq", [r_dq], b_w_dq, m_b_w_dq, v_b_w_dq)
    res["b_w_uq"] = update("adamw_b_w_uq", [r_uq], b_w_uq, m_b_w_uq, v_b_w_uq)
    res["b_w_o"] = update("adamw_b_w_o", [r_o_b], b_w_o, m_b_w_o, v_b_w_o)
    res["a_w_qkv"] = update("adamw_a_w_qkv", [r_qkv], a_w_qkv, m_a_w_qkv, v_a_w_qkv)
    res["a_w_o"] = update("adamw_a_w_o", [r_o_a], a_w_o, m_a_w_o, v_a_w_o)
    res["ffn1_w_in"] = update_transposed("adamw_ffn1_w_in", [r_in1_0, r_in1_1], ffn1_w_in, m_ffn1_w_in, v_ffn1_w_in)
    res["ffn1_w_out"] = update("adamw_ffn1_w_out", [r_out1_0, r_out1_1], ffn1_w_out, m_ffn1_w_out, v_ffn1_w_out)
    zero_row = jnp.zeros((dn,), F32)
    packs = [
        _pack_small(f1, mx, f2, kvn, fin, qn, lat, rel, zero_row)
        for f1, mx, f2, kvn, fin, qn, lat, rel in (
            (ffn1_norm, mix_norm, ffn2_norm, kv_norm, final_norm, b_q_norm, kv_latent_norm, a_rel_bias),
            (m_ffn1_norm, m_mix_norm, m_ffn2_norm, m_kv_norm, m_final_norm, m_b_q_norm, m_kv_latent_norm, m_a_rel_bias),
            (v_ffn1_norm, v_mix_norm, v_ffn2_norm, v_kv_norm, v_final_norm, v_b_q_norm, v_kv_latent_norm, v_a_rel_bias),
        )
    ]
    small_out = [_unpack_small(o[0]) for o in _adamw("adamw_small", [r_small], *[p[None] for p in packs])]
    for name in ("ffn1_norm", "mix_norm", "ffn2_norm", "a_rel_bias", "kv_norm", "kv_latent_norm", "b_q_norm", "final_norm"):
        res[name] = [so[name] for so in small_out]
    loss = small_out[0]["last"][0]

    order = [
        "ffn1_norm", "ffn1_w_in", "ffn1_w_out", "mix_norm", "ffn2_norm", "ffn2_w_in", "ffn2_w_out", "a_w_qkv", "a_rel_bias",
        "a_w_o", "kv_norm", "kv_w_down", "kv_latent_norm", "kv_w_up", "b_w_dq", "b_q_norm", "b_w_uq", "b_w_o", "final_norm",
    ]
    return (loss, grad_x, *[res[n][0] for n in order], *[res[n][1] for n in order], *[res[n][2] for n in order], *[res[n][3] for n in order])
```

```python
import jax
import jax.numpy as jnp
import numpy as np
from jax import lax
from jax.experimental import pallas as pl
from jax.experimental.pallas import tpu as pltpu
from jax.experimental.pallas import tpu_sc as plsc

NDEV = 8
D_MODEL = 1024
D_FF = 2816
FB = 2 * D_FF // NDEV
NJ = D_FF // FB
CHUNK = 64
LEFT_CHUNKS = 8
PAD = LEFT_CHUNKS * CHUNK
BAND = PAD + CHUNK
CHUNKS_PER_STEP = 4
WINDOW = PAD + CHUNKS_PER_STEP * CHUNK
STEP_ROWS = CHUNKS_PER_STEP * 2 * CHUNK
MAX_REL = 128
NREL = 2 * MAX_REL + 1
NREL_PAD = 384
HEADS_A = 16
HEADS_B = 8
NOPE = 128
ROPE = 64
QK_B = NOPE + ROPE
V_DIM = 128
Q_LORA = 768
KV_LORA = 256
ROPE_THETA = 10000.0
EPS = 1e-6
NEG_INF = -1e30
MLA_TQ = 256
MLA_TK_FWD = 256
MLA_TK_BWD = 1024
ADAM_LR = 0.001
ADAM_B1 = 0.9
ADAM_B2 = 0.999
ADAM_EPS = 1e-08
ADAM_WD = 0.01
ADAM_STEP = 10
PACK_ROWS = 16
GATHER_IDS = tuple(range(1, 9))
REDUCE_IDS = tuple(range(9, 14))
VMEM_LIMIT_BYTES = 56 * 1024 * 1024

F32 = jnp.float32
BF16 = jnp.bfloat16
SDS = jax.ShapeDtypeStruct
BS = pl.BlockSpec
MESH = pl.DeviceIdType.MESH


def _cparams(n_axes):
    return pltpu.CompilerParams(dimension_semantics=("arbitrary",) * n_axes, vmem_limit_bytes=VMEM_LIMIT_BYTES)


def _tile(t, want=512):
    return want if t % want == 0 else t


def _dot(a, b):
    return jnp.dot(a, b, preferred_element_type=F32)


def _dot_nt(a, b):
    return lax.dot_general(a, b, (((1,), (1,)), ((), ())), preferred_element_type=F32)


def _dot_tn(a, b):
    return lax.dot_general(a, b, (((0,), (0,)), ((), ())), preferred_element_type=F32)


def _split3(a):
    hi = a.astype(BF16)
    rest = a - hi.astype(F32)
    mid = rest.astype(BF16)
    return hi, mid, (rest - mid.astype(F32)).astype(BF16)


def _dot_exact(a, onehot, transposed=False):
    ob = onehot.astype(BF16)
    dot = _dot_nt if transposed else _dot
    hi, mid, lo = _split3(a)
    return dot(hi, ob) + dot(mid, ob) + dot(lo, ob)


def _rms_scale(h):
    return lax.rsqrt(jnp.mean(h * h, axis=-1, keepdims=True) + EPS)


def _acc_rows(ref, val, step, n_steps):
    part = val.reshape(val.shape[0] // 8, 8, val.shape[1]).sum(axis=0)

    @pl.when(step == 0)
    def _():
        ref[...] = part

    @pl.when(step > 0)
    def _():
        ref[...] += part

    @pl.when(step == n_steps - 1)
    def _():
        ref[...] = jnp.broadcast_to(jnp.sum(ref[...], axis=0, keepdims=True), ref.shape)


def _exchange_plan(entries):
    ins = [e[1] for e in entries]
    kinds = [e[0] for e in entries]
    lands = [SDS((NDEV,) + a.shape if k == "gather" else a.shape, a.dtype) for k, a in zip(kinds, ins)]
    return ins, lands, kinds


def _mesh_place():
    x, y, c = lax.axis_index("x"), lax.axis_index("y"), lax.axis_index("c")
    return (x, y, c), 4 * x + 2 * y + c


def _flipped(place, p):
    x, y, c = place
    px = 1 - x if p & 4 else x
    py = 1 - y if p & 2 else y
    pc = 1 - c if p & 1 else c
    return (px, py, pc), 4 * px + 2 * py + pc


def _ends(kind, src_ref, land_ref, origin, target):
    if kind == "gather":
        return src_ref, land_ref.at[origin]
    return src_ref.at[target], land_ref.at[origin]


def _remote(kind, src_ref, land_ref, send_sems, recv_sems, k, p, place, me, arriving):
    peer_pos, peer = _flipped(place, p)
    src, dst = _ends(kind, src_ref, land_ref, me, peer)
    if arriving:
        dst = _ends(kind, src_ref, land_ref, peer, me)[1]
    sem = k * (NDEV - 1) + p - 1
    return pltpu.make_async_remote_copy(
        src_ref=src, dst_ref=dst, send_sem=send_sems.at[sem], recv_sem=recv_sems.at[sem], device_id=peer_pos, device_id_type=MESH,
    )


def _exchange(name, entries):
    ins, lands, kinds = _exchange_plan(entries)
    n = len(ins)

    def body(*refs):
        in_refs, land_refs = refs[:n], refs[n : 2 * n]
        send_sems, recv_sems, local_sems = refs[2 * n :]
        place, me = _mesh_place()
        local = []
        for k in range(n):
            src, dst = _ends(kinds[k], in_refs[k], land_refs[k], me, me)
            local.append(pltpu.make_async_copy(src, dst, local_sems.at[k]))
            local[-1].start()
        sends = []
        for p in range(1, NDEV):
            for k in range(n):
                sends.append(_remote(kinds[k], in_refs[k], land_refs[k], send_sems, recv_sems, k, p, place, me, False))
                sends[-1].start()
        for p in range(1, NDEV):
            for k in range(n):
                _remote(kinds[k], in_refs[k], land_refs[k], send_sems, recv_sems, k, p, place, me, True).wait_recv()
        for cp in sends:
            cp.wait_send()
        for cp in local:
            cp.wait()

    any_spec = BS(memory_space=pl.ANY)
    return pl.pallas_call(
        body, name=name, out_shape=lands, in_specs=[any_spec] * n, out_specs=[any_spec] * n,
        scratch_shapes=[
            pltpu.SemaphoreType.DMA((n * (NDEV - 1),)), pltpu.SemaphoreType.DMA((n * (NDEV - 1),)), pltpu.SemaphoreType.DMA((n,)),
        ],
    )(*ins)


def _exchange_sc(name, entries, collective_id, after=()):
    ins, lands, kinds = _exchange_plan(entries)
    n = len(ins)
    after = tuple(after)

    def launch(*refs):
        refs = refs[:n] + refs[n + len(after) :]
        in_refs, land_refs = refs[:n], refs[n : 2 * n]
        send_sems, recv_sems, local_sems = refs[2 * n :]
        place, me = _mesh_place()
        barrier = pltpu.get_barrier_semaphore()
        for p in range(1, NDEV):
            pl.semaphore_signal(barrier, inc=1, device_id=_flipped(place, p)[0], device_id_type=MESH)
        pl.semaphore_wait(barrier, NDEV - 1)
        local = []
        for k in range(n):
            src, dst = _ends(kinds[k], in_refs[k], land_refs[k], me, me)
            local.append(pltpu.make_async_copy(src, dst, local_sems.at[k]))
            local[-1].start()
        sends = []
        if all(kind == "gather" for kind in kinds):
            for p in (1, 2, 4, 6):
                for k in range(n):
                    sends.append(_remote(kinds[k], in_refs[k], land_refs[k], send_sems, recv_sems, k, p, place, me, False))
                    sends[-1].start()
            sibling_pos, _ = _flipped(place, 1)
            for f in (2, 4, 6):
                _, origin = _flipped(place, f)
                for k in range(n):
                    _remote(kinds[k], in_refs[k], land_refs[k], send_sems, recv_sems, k, f, place, me, True).wait_recv()
                    sem = k * (NDEV - 1) + f
                    sends.append(
                        pltpu.make_async_remote_copy(
                            src_ref=land_refs[k].at[origin], dst_ref=land_refs[k].at[origin], send_sem=send_sems.at[sem],
                            recv_sem=recv_sems.at[sem], device_id=sibling_pos, device_id_type=MESH,
                        )
                    )
                    sends[-1].start()
            for p in (1, 3, 5, 7):
                for k in range(n):
                    _remote(kinds[k], in_refs[k], land_refs[k], send_sems, recv_sems, k, p, place, me, True).wait_recv()
        else:
            for p in range(1, NDEV):
                for k in range(n):
                    sends.append(_remote(kinds[k], in_refs[k], land_refs[k], send_sems, recv_sems, k, p, place, me, False))
                    sends[-1].start()
            for p in range(1, NDEV):
                for k in range(n):
                    _remote(kinds[k], in_refs[k], land_refs[k], send_sems, recv_sems, k, p, place, me, True).wait_recv()
        for cp in sends:
            cp.wait_send()
        for cp in local:
            cp.wait()

    return pl.kernel(
        launch, out_type=tuple(lands), mesh=plsc.ScalarSubcoreMesh(axis_name="sequencer", num_cores=1), name=name,
        scratch_types=(
            pltpu.SemaphoreType.DMA((n * (NDEV - 1),)), pltpu.SemaphoreType.DMA((n * (NDEV - 1),)), pltpu.SemaphoreType.DMA((n,)),
        ),
        compiler_params=pltpu.CompilerParams(collective_id=collective_id),
    )(*ins, *after)


def _norm_fwd(name, h, gammas):
    t, dn = h.shape
    ng = gammas.shape[0]
    tm = _tile(t)

    def body(h_ref, g_ref, *outs):
        hv = h_ref[...]
        hh = hv * _rms_scale(hv)
        for i, o_ref in enumerate(outs):
            o_ref[...] = (hh * g_ref[i : i + 1, :]).astype(BF16)

    row = BS((tm, dn), lambda i: (i, 0))
    return pl.pallas_call(
        body, name=name, grid=(t // tm,),
        in_specs=[row, BS((ng, dn), lambda i: (0, 0))],
        out_specs=[row] * ng, out_shape=[SDS((t, dn), BF16)] * ng,
        compiler_params=_cparams(1),
    )(h, gammas)


def _ffn_in(name, n, w_in, layer):
    t, dn = n.shape
    tm = _tile(t, 1024)

    def body(n_ref, wg_ref, wu_ref, gu_ref, a_ref):
        xv = n_ref[...]
        g = _dot(xv, wg_ref[...])
        u = _dot(xv, wu_ref[...])
        gu_ref[0] = g.astype(BF16)
        gu_ref[1] = u.astype(BF16)
        a_ref[...] = (g * jax.nn.sigmoid(g) * u).astype(BF16)

    return pl.pallas_call(
        body, name=name, grid=(NJ, t // tm),
        in_specs=[
            BS((tm, dn), lambda j, i: (i, 0)),
            BS((None, None, dn, FB), lambda j, i: (layer, j, 0, 0)),
            BS((None, None, dn, FB), lambda j, i: (layer, j + NJ, 0, 0)),
        ],
        out_specs=[BS((None, 2, tm, FB), lambda j, i: (j, 0, i, 0)), BS((None, tm, FB), lambda j, i: (j, i, 0))],
        out_shape=[SDS((NJ, 2, t, FB), BF16), SDS((NJ, t, FB), BF16)],
        compiler_params=_cparams(2),
    )(n, w_in, w_in)


def _mm_res_norm(name, a, w, layer, h_in, gammas, scale):
    nk, t, kb = a.shape
    dn = w.shape[-1]
    ng = 0 if gammas is None else gammas.shape[0]
    tm = _tile(t)

    def body(*refs):
        a_ref, w_ref, h_ref = refs[:3]
        g_ref = refs[3] if ng else None
        outs = refs[3 + (1 if ng else 0) :]
        acc = _dot(a_ref[0], w_ref[0])
        for k in range(1, nk):
            acc += _dot(a_ref[k], w_ref[k])
        ho = h_ref[...] + scale * acc
        outs[0][...] = ho
        if ng:
            hh = ho * _rms_scale(ho)
            for i in range(ng):
                outs[1 + i][...] = (hh * g_ref[i : i + 1, :]).astype(BF16)

    row = BS((tm, dn), lambda i: (i, 0))
    in_specs = [BS((nk, tm, kb), lambda i: (0, i, 0)), BS((None, nk, kb, dn), lambda i: (layer, 0, 0, 0)), row]
    args = [a, w, h_in]
    if ng:
        in_specs.append(BS((ng, dn), lambda i: (0, 0)))
        args.append(gammas)
    return pl.pallas_call(
        body, name=name, grid=(t // tm,),
        in_specs=in_specs,
        out_specs=[row] * (1 + ng), out_shape=[SDS((t, dn), F32)] + [SDS((t, dn), BF16)] * ng,
        compiler_params=_cparams(1),
    )(*args)


def _qkv_proj(name, hn, w_qkv):
    t, dn = hn.shape
    wb = w_qkv.shape[-1]
    per = wb // 128
    tm = _tile(t)

    def body(x_ref, w_ref, o_ref):
        xv = x_ref[...]
        for j in range(NDEV):
            yv = _dot(xv, w_ref[j]).astype(BF16)
            for i in range(per):
                n = per * j + i
                o_ref[n // 8, :, (n % 8) * 128 : (n % 8 + 1) * 128] = yv[:, i * 128 : (i + 1) * 128]

    return pl.pallas_call(
        body, name=name, grid=(t // tm,),
        in_specs=[BS((tm, dn), lambda i: (i, 0)), BS((NDEV, dn, wb), lambda i: (0, 0, 0))],
        out_specs=BS((3, tm, dn), lambda i: (0, i, 0)), out_shape=SDS((3, t, dn), BF16),
        compiler_params=_cparams(1),
    )(hn, w_qkv)


def _rel_onehot(i):
    r = lax.broadcasted_iota(jnp.int32, (NREL_PAD, BAND), 0)
    j = lax.broadcasted_iota(jnp.int32, (NREL_PAD, BAND), 1)
    idx = jnp.clip(PAD + i - j, -MAX_REL, MAX_REL) + MAX_REL
    return (idx == r).astype(F32)


def _rel_bias_fwd(table):
    def body(t_ref, o_ref):
        i8 = pl.program_id(0)
        for ii in range(8):
            o_ref[:, ii, :] = _dot_exact(t_ref[...], _rel_onehot(i8 * 8 + ii))

    return pl.pallas_call(
        body, name="rel_bias_fwd", grid=(CHUNK // 8,),
        in_specs=[BS((HEADS_A, NREL_PAD), lambda i: (0, 0))],
        out_specs=BS((HEADS_A, 8, BAND), lambda i: (0, i, 0)), out_shape=SDS((HEADS_A, CHUNK, BAND), F32),
        compiler_params=_cparams(1),
    )(table)


def _rel_bias_bwd(dbias):
    def body(d_ref, o_ref):
        i8 = pl.program_id(0)
        acc = jnp.zeros((HEADS_A, NREL_PAD), F32)
        for ii in range(8):
            acc += _dot_exact(d_ref[:, ii, :], _rel_onehot(i8 * 8 + ii), transposed=True)

        @pl.when(i8 == 0)
        def _():
            o_ref[...] = acc

        @pl.when(i8 > 0)
        def _():
            o_ref[...] += acc

    return pl.pallas_call(
        body, name="rel_bias_bwd", grid=(CHUNK // 8,),
        in_specs=[BS((HEADS_A, 8, BAND), lambda i: (0, i, 0))],
        out_specs=BS((HEADS_A, NREL_PAD), lambda i: (0, 0)), out_shape=SDS((HEADS_A, NREL_PAD), F32),
        compiler_params=_cparams(1),
    )(dbias)


def _window_bias(bias):
    b = bias.reshape(HEADS_A // 2, 2, CHUNK, BAND)
    per_chunk = [
        jnp.pad(b, ((0, 0), (0, 0), (0, 0), (cc * CHUNK, WINDOW - BAND - cc * CHUNK)), constant_values=NEG_INF)
        for cc in range(CHUNKS_PER_STEP)
    ]
    return jnp.stack(per_chunk, axis=1).reshape(HEADS_A // 2, STEP_ROWS, WINDOW)


def _window_bias_bwd(dwin):
    d = dwin.reshape(HEADS_A // 2, CHUNKS_PER_STEP, 2, CHUNK, WINDOW)
    return sum(d[:, cc, :, :, cc * CHUNK : cc * CHUNK + BAND] for cc in range(CHUNKS_PER_STEP)).reshape(HEADS_A, CHUNK, BAND)


def _step_rows(xs, lane):
    parts = []
    for cc in range(CHUNKS_PER_STEP):
        xc = xs[cc * CHUNK : (cc + 1) * CHUNK]
        parts.append(jnp.where(lane < 64, xc, jnp.zeros_like(xc)))
        parts.append(jnp.where(lane >= 64, xc, jnp.zeros_like(xc)))
    return jnp.concatenate(parts, axis=0)


def _pair_rows(ys, lane):
    parts = []
    for cc in range(CHUNKS_PER_STEP):
        y0 = ys[(2 * cc) * CHUNK : (2 * cc + 1) * CHUNK]
        y1 = ys[(2 * cc + 1) * CHUNK : (2 * cc + 2) * CHUNK]
        parts.append(jnp.where(lane < 64, y0, y1))
    return jnp.concatenate(parts, axis=0)


def _window_scores(q_rows, kwin, bias_win, first_key):
    s = _dot_nt(q_rows, kwin) * (CHUNK ** -0.5) + bias_win
    col = lax.broadcasted_iota(jnp.int32, s.shape, 1)
    return jnp.where(col >= first_key, s, NEG_INF)


def _attn_a_fwd(qkv3, bias_win, bl, seq):
    t, dn = qkv3.shape[1:]
    npair = dn // 128
    step = CHUNKS_PER_STEP * CHUNK

    def body(q_ref, k_ref, v_ref, b_ref, o_ref, lse_ref, kpad, vpad):
        kpad[0:PAD, :] = jnp.zeros((PAD, 128), BF16)
        vpad[0:PAD, :] = jnp.zeros((PAD, 128), BF16)
        kpad[PAD:, :] = k_ref[...]
        vpad[PAD:, :] = v_ref[...]
        lane = lax.broadcasted_iota(jnp.int32, (CHUNK, 128), 1)

        def chunks(it, carry):
            r0 = pl.multiple_of(it * step, step)
            q_rows = _step_rows(q_ref[pl.ds(r0, step), :], lane)
            s = _window_scores(q_rows, kpad[pl.ds(r0, WINDOW), :], b_ref[...], PAD - r0)
            m = jnp.max(s, axis=-1, keepdims=True)
            e = jnp.exp(s - m)
            total = jnp.sum(e, axis=-1, keepdims=True)
            o_rows = _dot((e * (1.0 / total)).astype(BF16), vpad[pl.ds(r0, WINDOW), :])
            o_ref[pl.ds(r0, step), :] = _pair_rows(o_rows, lane).astype(BF16)
            lse_ref[pl.ds(pl.multiple_of(it * STEP_ROWS, STEP_ROWS), STEP_ROWS), :] = m + jnp.log(total)
            return carry

        lax.fori_loop(0, seq // step, chunks, 0, unroll=2)

    return pl.pallas_call(
        body, name="attn_a_fwd", grid=(bl, npair),
        in_specs=[
            BS((None, seq, 128), lambda b, h: (0, b, h)),
            BS((None, seq, 128), lambda b, h: (1, b, h)),
            BS((None, seq, 128), lambda b, h: (2, b, h)),
            BS((None, STEP_ROWS, WINDOW), lambda b, h: (h, 0, 0)),
        ],
        out_specs=[BS((seq, 128), lambda b, h: (b, h)), BS((None, 2 * seq, 1), lambda b, h: (h, b, 0))],
        out_shape=[SDS((t, dn), BF16), SDS((npair, 2 * t, 1), F32)],
        scratch_shapes=[pltpu.VMEM((PAD + seq, 128), BF16), pltpu.VMEM((PAD + seq, 128), BF16)],
        compiler_params=_cparams(2),
    )(qkv3, qkv3, qkv3, bias_win)


def _attn_a_bwd(qkv3, out, lse, do, bias_win, bl, seq):
    t, dn = qkv3.shape[1:]
    npair = dn // 128
    step = CHUNKS_PER_STEP * CHUNK

    def body(q_ref, k_ref, v_ref, o_ref, lse_ref, do_ref, b_ref, dqkv_ref, db_ref, kpad, vpad, dkacc, dvacc):
        b = pl.program_id(1)
        kpad[0:PAD, :] = jnp.zeros((PAD, 128), BF16)
        vpad[0:PAD, :] = jnp.zeros((PAD, 128), BF16)
        kpad[PAD:, :] = k_ref[...]
        vpad[PAD:, :] = v_ref[...]
        dkacc[...] = jnp.zeros_like(dkacc)
        dvacc[...] = jnp.zeros_like(dvacc)

        @pl.when(b == 0)
        def _():
            db_ref[...] = jnp.zeros_like(db_ref)

        lane = lax.broadcasted_iota(jnp.int32, (CHUNK, 128), 1)

        def chunks(it, carry):
            r0 = pl.multiple_of(it * step, step)
            q_rows = _step_rows(q_ref[pl.ds(r0, step), :], lane)
            do_rows = _step_rows(do_ref[pl.ds(r0, step), :], lane)
            kwin = kpad[pl.ds(r0, WINDOW), :]
            vwin = vpad[pl.ds(r0, WINDOW), :]
            o_rows = _step_rows(o_ref[pl.ds(r0, step), :], lane)
            delta = jnp.sum(do_rows.astype(F32) * o_rows.astype(F32), axis=-1, keepdims=True)
            lse_rows = lse_ref[pl.ds(pl.multiple_of(it * STEP_ROWS, STEP_ROWS), STEP_ROWS), :]
            p = jnp.exp(_window_scores(q_rows, kwin, b_ref[...], PAD - r0) - lse_rows)
            ds = p * (_dot_nt(do_rows, vwin) - delta)
            db_ref[...] += ds
            dsb = (ds * (CHUNK ** -0.5)).astype(BF16)
            dqkv_ref[0, pl.ds(r0, step), :] = _pair_rows(_dot(dsb, kwin), lane).astype(BF16)
            dkacc[pl.ds(r0, WINDOW), :] += _dot_tn(dsb, q_rows)
            dvacc[pl.ds(r0, WINDOW), :] += _dot_tn(p.astype(BF16), do_rows)
            return carry

        lax.fori_loop(0, seq // step, chunks, 0, unroll=2)
        dqkv_ref[1] = dkacc[PAD:, :].astype(BF16)
        dqkv_ref[2] = dvacc[PAD:, :].astype(BF16)

    return pl.pallas_call(
        body, name="attn_a_bwd", grid=(npair, bl),
        in_specs=[
            BS((None, seq, 128), lambda h, b: (0, b, h)),
            BS((None, seq, 128), lambda h, b: (1, b, h)),
            BS((None, seq, 128), lambda h, b: (2, b, h)),
            BS((seq, 128), lambda h, b: (b, h)),
            BS((None, 2 * seq, 1), lambda h, b: (h, b, 0)),
            BS((seq, 128), lambda h, b: (b, h)),
            BS((None, STEP_ROWS, WINDOW), lambda h, b: (h, 0, 0)),
        ],
        out_specs=[BS((3, seq, 128), lambda h, b: (0, b, h)), BS((None, STEP_ROWS, WINDOW), lambda h, b: (h, 0, 0))],
        out_shape=[SDS((3, t, dn), BF16), SDS((HEADS_A // 2, STEP_ROWS, WINDOW), F32)],
        scratch_shapes=[
            pltpu.VMEM((PAD + seq, 128), BF16), pltpu.VMEM((PAD + seq, 128), BF16),
            pltpu.VMEM((PAD + seq, 128), F32), pltpu.VMEM((PAD + seq, 128), F32),
        ],
        compiler_params=_cparams(2),
    )(qkv3, qkv3, qkv3, out, lse, do, bias_win)


def _rope_tables(seq):
    half = ROPE // 2
    freqs = ROPE_THETA ** (-jnp.arange(half, dtype=F32) / half)
    ang = jnp.arange(seq, dtype=F32)[:, None] * freqs[None, :]
    cos, sin = jnp.cos(ang), jnp.sin(ang)
    c64 = jnp.concatenate([cos, cos], axis=1)
    s64 = jnp.concatenate([-sin, sin], axis=1)
    c192 = jnp.concatenate([jnp.ones((seq, NOPE), F32), c64], axis=1)
    s192 = jnp.concatenate([jnp.zeros((seq, NOPE), F32), s64], axis=1)
    p64 = np.zeros((ROPE, ROPE), np.float32)
    for col in range(ROPE):
        p64[(col + half) % ROPE, col] = 1.0
    p192 = np.zeros((QK_B, QK_B), np.float32)
    p192[NOPE:, NOPE:] = p64
    return c64, s64, jnp.asarray(p64), c192, s192, jnp.asarray(p192)


def _rope(xv, cos, sin_signed, swap):
    return xv * cos + _dot_exact(xv, swap) * sin_signed


def _rope_bwd(dy, cos, sin_signed, swap):
    return dy * cos + _dot_exact(dy * sin_signed, swap)


def _q_down(hn, w_dq, q_norm):
    t, dn = hn.shape
    ql = w_dq.shape[1]
    tm = _tile(t)

    def body(x_ref, w_ref, g_ref, pre_ref, cq_ref):
        pre = _dot(x_ref[...], w_ref[...])
        pre_ref[...] = pre
        cq_ref[...] = (pre * _rms_scale(pre) * g_ref[...]).astype(BF16)

    return pl.pallas_call(
        body, name="q_down", grid=(t // tm,),
        in_specs=[BS((tm, dn), lambda i: (i, 0)), BS((dn, ql), lambda i: (0, 0)), BS((1, ql), lambda i: (0, 0))],
        out_specs=[BS((tm, ql), lambda i: (i, 0))] * 2, out_shape=[SDS((t, ql), F32), SDS((t, ql), BF16)],
        compiler_params=_cparams(1),
    )(hn, w_dq, q_norm)


def _q_up(cq, w_uq, c192, s192, p192, seq):
    t, ql = cq.shape
    tm = _tile(min(seq, 512), min(seq, 512))
    nseq = seq // tm

    def body(x_ref, w_ref, c_ref, s_ref, p_ref, o_ref):
        xv = x_ref[...]
        for h in range(HEADS_B):
            o_ref[h] = _rope(_dot(xv, w_ref[h]), c_ref[...], s_ref[...], p_ref[...]).astype(BF16)

    pos = BS((tm, QK_B), lambda i: (i % nseq, 0))
    return pl.pallas_call(
        body, name="q_up", grid=(t // tm,),
        in_specs=[
            BS((tm, ql), lambda i: (i, 0)), BS((HEADS_B, ql, QK_B), lambda i: (0, 0, 0)), pos, pos,
            BS((QK_B, QK_B), lambda i: (0, 0)),
        ],
        out_specs=BS((HEADS_B, tm, QK_B), lambda i: (0, i, 0)), out_shape=SDS((HEADS_B, t, QK_B), BF16),
        compiler_params=_cparams(1),
    )(cq, w_uq, c192, s192, p192)


def _kv_down(hk, w_down, latent_norm, c64, s64, p64, seq):
    t, dn = hk.shape
    wd = w_down.shape[1]
    tm = _tile(min(seq, 512), min(seq, 512))
    nseq = seq // tm

    def body(x_ref, w_ref, g_ref, c_ref, s_ref, p_ref, ckr_ref, ckv_ref, kr_ref):
        ckr = _dot(x_ref[...], w_ref[...])
        ckr_ref[...] = ckr
        lat = ckr[:, :KV_LORA]
        ckv_ref[...] = (lat * _rms_scale(lat) * g_ref[...]).astype(BF16)
        kr_ref[...] = _rope(ckr[:, KV_LORA:], c_ref[...], s_ref[...], p_ref[...]).astype(BF16)

    pos = BS((tm, ROPE), lambda i: (i % nseq, 0))
    return pl.pallas_call(
        body, name="kv_down", grid=(t // tm,),
        in_specs=[
            BS((tm, dn), lambda i: (i, 0)), BS((dn, wd), lambda i: (0, 0)), BS((1, KV_LORA), lambda i: (0, 0)), pos, pos,
            BS((ROPE, ROPE), lambda i: (0, 0)),
        ],
        out_specs=[BS((tm, wd), lambda i: (i, 0)), BS((tm, KV_LORA), lambda i: (i, 0)), BS((tm, ROPE), lambda i: (i, 0))],
        out_shape=[SDS((t, wd), F32), SDS((t, KV_LORA), BF16), SDS((t, ROPE), BF16)],
        compiler_params=_cparams(1),
    )(hk, w_down, latent_norm, c64, s64, p64)


def _kv_up(ckv, w_up):
    t, kl = ckv.shape
    hb = w_up.shape[-1]
    tm = _tile(t)

    def body(x_ref, w_ref, o_ref):
        xv = x_ref[...]
        for h in range(HEADS_B):
            o_ref[:, h * hb : (h + 1) * hb] = _dot(xv, w_ref[h]).astype(BF16)

    return pl.pallas_call(
        body, name="kv_up", grid=(t // tm,),
        in_specs=[BS((tm, kl), lambda i: (i, 0)), BS((HEADS_B, kl, hb), lambda i: (0, 0, 0))],
        out_specs=BS((tm, HEADS_B * hb), lambda i: (i, 0)), out_shape=SDS((t, HEADS_B * hb), BF16),
        compiler_params=_cparams(1),
    )(ckv, w_up)


def _mla_diagonal_mask(tq):
    rows = lax.broadcasted_iota(jnp.int32, (tq, tq), 0)
    cols = lax.broadcasted_iota(jnp.int32, (tq, tq), 1)
    return jnp.where(jnp.right_shift(cols, 6) <= jnp.right_shift(rows, 6), 0.0, NEG_INF)


def _mla_key_tiles(n_keys, tk):
    return [(slice(k0, min(k0 + tk, n_keys)), min(k0 + tk, n_keys) == n_keys) for k0 in range(0, n_keys, tk)]


def _mla_scores(qi, kt, diagonal):
    s = _dot_nt(qi, kt) * (QK_B ** -0.5)
    if diagonal is None:
        return s
    tq, width = s.shape
    own = s[:, width - tq :] + diagonal
    return own if width == tq else jnp.concatenate([s[:, : width - tq], own], axis=1)


def _mla_fwd(q, kv, kr, bl, seq):
    t = kv.shape[0]
    tq = min(MLA_TQ, seq)

    def body(q_ref, kn_ref, v_ref, kr_ref, o_ref, lse_ref):
        kcat = jnp.concatenate([kn_ref[...], kr_ref[...]], axis=1)
        vv = v_ref[...]
        diagonal = _mla_diagonal_mask(tq)
        for i in range(seq // tq):
            rows = slice(i * tq, (i + 1) * tq)
            qi = q_ref[rows, :]
            m = total = acc = None
            for keys, own in _mla_key_tiles((i + 1) * tq, MLA_TK_FWD):
                s = _mla_scores(qi, kcat[keys], diagonal if own else None)
                m_blk = jnp.max(s, axis=-1, keepdims=True)
                if m is None:
                    m_new = m_blk
                    e = jnp.exp(s - m_new)
                    total = jnp.sum(e, axis=-1, keepdims=True)
                    acc = _dot(e.astype(BF16), vv[keys])
                else:
                    m_new = jnp.maximum(m, m_blk)
                    keep = jnp.exp(m - m_new)
                    e = jnp.exp(s - m_new)
                    total = keep * total + jnp.sum(e, axis=-1, keepdims=True)
                    acc = keep * acc + _dot(e.astype(BF16), vv[keys])
                m = m_new
            o_ref[rows, :] = (acc / total).astype(BF16)
            lse_ref[rows, :] = m + jnp.log(total)

    return pl.pallas_call(
        body, name="mla_fwd", grid=(bl, HEADS_B),
        in_specs=[
            BS((None, seq, QK_B), lambda b, h: (h, b, 0)),
            BS((seq, NOPE), lambda b, h: (b, 2 * h)),
            BS((seq, V_DIM), lambda b, h: (b, 2 * h + 1)),
            BS((seq, ROPE), lambda b, h: (b, 0)),
        ],
        out_specs=[BS((seq, V_DIM), lambda b, h: (b, h)), BS((None, seq, 1), lambda b, h: (h, b, 0))],
        out_shape=[SDS((t, HEADS_B * V_DIM), BF16), SDS((HEADS_B, t, 1), F32)],
        compiler_params=_cparams(2),
    )(q, kv, kv, kr)


def _mla_bwd(q, kv, kr, o, lse, do, c192, s192, p192, bl, seq):
    t = kv.shape[0]
    tq = min(MLA_TQ, seq)

    def body(q_ref, kn_ref, v_ref, kr_ref, o_ref, lse_ref, do_ref, c_ref, s_ref, p_ref, dq_ref, dkv_ref, dkr_ref, dkacc, dvacc):
        h = pl.program_id(1)
        kcat = jnp.concatenate([kn_ref[...], kr_ref[...]], axis=1)
        vv = v_ref[...]
        dkacc[...] = jnp.zeros_like(dkacc)
        dvacc[...] = jnp.zeros_like(dvacc)
        diagonal = _mla_diagonal_mask(tq)
        for i in range(seq // tq):
            rows = slice(i * tq, (i + 1) * tq)
            qi = q_ref[rows, :]
            doi = do_ref[rows, :]
            lse_i = lse_ref[rows, :]
            delta = jnp.sum(doi.astype(F32) * o_ref[rows, :].astype(F32), axis=-1, keepdims=True)
            dq = None
            for keys, own in _mla_key_tiles((i + 1) * tq, MLA_TK_BWD):
                p = jnp.exp(_mla_scores(qi, kcat[keys], diagonal if own else None) - lse_i)
                ds = p * (_dot_nt(doi, vv[keys]) - delta)
                dsb = (ds * (QK_B ** -0.5)).astype(BF16)
                dq_blk = _dot(dsb, kcat[keys])
                dq = dq_blk if dq is None else dq + dq_blk
                dkacc[keys, :] += _dot_tn(dsb, qi)
                dvacc[keys, :] += _dot_tn(p.astype(BF16), doi)
            dq_ref[rows, :] = _rope_bwd(dq, c_ref[rows, :], s_ref[rows, :], p_ref[...]).astype(BF16)
        dk = dkacc[...]
        dkv_ref[:, :NOPE] = dk[:, :NOPE].astype(BF16)
        dkv_ref[:, NOPE:] = dvacc[...].astype(BF16)

        @pl.when(h == 0)
        def _():
            dkr_ref[...] = dk[:, NOPE:]

        @pl.when(h > 0)
        def _():
            dkr_ref[...] += dk[:, NOPE:]

    return pl.pallas_call(
        body, name="mla_bwd", grid=(bl, HEADS_B),
        in_specs=[
            BS((None, seq, QK_B), lambda b, h: (h, b, 0)),
            BS((seq, NOPE), lambda b, h: (b, 2 * h)),
            BS((seq, V_DIM), lambda b, h: (b, 2 * h + 1)),
            BS((seq, ROPE), lambda b, h: (b, 0)),
            BS((seq, V_DIM), lambda b, h: (b, h)),
            BS((None, seq, 1), lambda b, h: (h, b, 0)),
            BS((seq, V_DIM), lambda b, h: (b, h)),
            BS((seq, QK_B), lambda b, h: (0, 0)),
            BS((seq, QK_B), lambda b, h: (0, 0)),
            BS((QK_B, QK_B), lambda b, h: (0, 0)),
        ],
        out_specs=[
            BS((None, seq, QK_B), lambda b, h: (h, b, 0)),
            BS((seq, NOPE + V_DIM), lambda b, h: (b, h)),
            BS((seq, ROPE), lambda b, h: (b, 0)),
        ],
        out_shape=[SDS((HEADS_B, t, QK_B), BF16), SDS((t, HEADS_B * (NOPE + V_DIM)), BF16), SDS((t, ROPE), F32)],
        scratch_shapes=[pltpu.VMEM((seq, QK_B), F32), pltpu.VMEM((seq, V_DIM), F32)],
        compiler_params=_cparams(2),
    )(q, kv, kv, kr, o, lse, do, c192, s192, p192)


def _loss_final(h, target, gamma):
    t, dn = h.shape
    tm = _tile(t)
    nt = t // tm

    def body(h_ref, t_ref, g_ref, dh_ref, dhb_ref, dg_ref, loss_ref):
        i = pl.program_id(0)
        hv = h_ref[...]
        r = _rms_scale(hv)
        hh = hv * r
        gam = g_ref[...]
        err = hh * gam - t_ref[...]
        part = 0.5 * jnp.sum(jnp.mean(err * err, axis=-1, keepdims=True))

        @pl.when(i == 0)
        def _():
            loss_ref[...] = jnp.zeros_like(loss_ref)

        loss_ref[...] += part
        dy = err * (1.0 / dn)
        _acc_rows(dg_ref, dy * hh, i, nt)
        t1 = dy * gam
        dh = r * (t1 - hh * jnp.mean(t1 * hh, axis=-1, keepdims=True))
        dh_ref[...] = dh
        dhb_ref[...] = dh.astype(BF16)

    row = BS((tm, dn), lambda i: (i, 0))
    return pl.pallas_call(
        body, name="loss_final", grid=(nt,),
        in_specs=[row, row, BS((1, dn), lambda i: (0, 0))],
        out_specs=[row, row, BS((8, dn), lambda i: (0, 0)), BS((8, 128), lambda i: (0, 0))],
        out_shape=[SDS((t, dn), F32), SDS((t, dn), BF16), SDS((8, dn), F32), SDS((8, 128), F32)],
        compiler_params=_cparams(1),
    )(h, target, gamma)


def _ffn_bwd_in(name, dh, w_out, layer, gu):
    t, dn = dh.shape
    tm = _tile(t, 1024)

    def body(dh_ref, w_ref, gu_ref, o_ref):
        da = 0.5 * _dot_nt(dh_ref[...], w_ref[...])
        g = gu_ref[0].astype(F32)
        u = gu_ref[1].astype(F32)
        sg = jax.nn.sigmoid(g)
        o_ref[0] = (da * u * (sg * (1.0 + g * (1.0 - sg)))).astype(BF16)
        o_ref[1] = (da * (g * sg)).astype(BF16)

    blk = BS((None, 2, tm, FB), lambda j, i: (j, 0, i, 0))
    return pl.pallas_call(
        body, name=name, grid=(NJ, t // tm),
        in_specs=[BS((tm, dn), lambda j, i: (i, 0)), BS((None, None, FB, dn), lambda j, i: (layer, j, 0, 0)), blk],
        out_specs=blk, out_shape=SDS((NJ, 2, t, FB), BF16),
        compiler_params=_cparams(2),
    )(dh, w_out, gu)


def _mm_nt_plain(name, xf, w):
    t, dn = xf.shape
    n = w.shape[0]
    tm = _tile(t)

    def body(x_ref, w_ref, o_ref):
        o_ref[...] = _dot_nt(x_ref[...], w_ref[...]).astype(BF16)

    return pl.pallas_call(
        body, name=name, grid=(t // tm,),
        in_specs=[BS((tm, dn), lambda i: (i, 0)), BS((n, dn), lambda i: (0, 0))],
        out_specs=BS((tm, n), lambda i: (i, 0)), out_shape=SDS((t, n), BF16),
        compiler_params=_cparams(1),
    )(xf, w)


def _mm_tn(name, xa, x_spec, ya, y_spec, out_shape, out_spec, nj, scale=None):
    def body(x_ref, y_ref, o_ref):
        acc = _dot_tn(x_ref[...], y_ref[...])
        o_ref[...] = (acc if scale is None else scale * acc).astype(BF16)

    return pl.pallas_call(
        body, name=name, grid=(nj,),
        in_specs=[x_spec, y_spec], out_specs=out_spec, out_shape=SDS(out_shape, BF16),
        compiler_params=_cparams(1),
    )(xa, ya)


def _dw_qkv(hn, dqkv3, wb):
    t, dn = hn.shape
    per = wb // 128

    def body(x_ref, *refs):
        cols = [y_ref[...] for y_ref in refs[:per]]
        refs[per][...] = _dot_tn(x_ref[...], jnp.concatenate(cols, axis=1)).astype(BF16)

    def piece(k):
        return BS((None, t, 128), lambda j: ((per * j + k) // 8, 0, (per * j + k) % 8))

    return pl.pallas_call(
        body, name="dw_qkv", grid=(NDEV,),
        in_specs=[BS((t, dn), lambda j: (0, 0))] + [piece(k) for k in range(per)],
        out_specs=BS((None, dn, wb), lambda j: (j, 0, 0)), out_shape=SDS((NDEV, dn, wb), BF16),
        compiler_params=_cparams(1),
    )(hn, *([dqkv3] * per))


def _mm_nt_epi(name, ya, y_spec, wa, w_spec, nj, n_out, extra, out_shapes, out_specs, epilogue, tm, nt, mm_fn=None):
    n_extra = len(extra)
    n_outs = len(out_shapes)

    def body(*refs):
        y_ref, w_ref = refs[:2]
        ex = refs[2 : 2 + n_extra]
        outs = refs[2 + n_extra : 2 + n_extra + n_outs]
        i = pl.program_id(0)
        j = pl.program_id(1)
        part = _dot_nt(y_ref[...], w_ref[...]) if mm_fn is None else mm_fn(y_ref, w_ref)
        if nj == 1:
            epilogue(part, ex, outs, i, nt)
            return
        acc = refs[-1]

        @pl.when(j == 0)
        def _():
            acc[...] = part

        @pl.when(j > 0)
        def _():
            acc[...] += part

        @pl.when(j == nj - 1)
        def _():
            epilogue(acc[...], ex, outs, i, nt)

    return pl.pallas_call(
        body, name=name, grid=(nt, nj),
        in_specs=[y_spec, w_spec] + [spec for _, spec in extra],
        out_specs=out_specs, out_shape=out_shapes,
        scratch_shapes=[] if nj == 1 else [pltpu.VMEM((tm, n_out), F32)],
        compiler_params=_cparams(2),
    )(ya, wa, *[arr for arr, _ in extra])


def _norm_bwd(dn, hv, gam):
    r = _rms_scale(hv)
    hh = hv * r
    t1 = dn * gam
    return r * (t1 - hh * jnp.mean(t1 * hh, axis=-1, keepdims=True)), dn * hh


def _norm_bwd_epilogue(has_res, out_dtype):
    def epilogue(dn, ex, outs, i, nt):
        dh, dg_rows = _norm_bwd(dn, ex[0][...], ex[1][...])
        _acc_rows(outs[1], dg_rows, i, nt)
        if has_res:
            dh = dh + ex[2][...]
        outs[0][...] = dh.astype(out_dtype)
        if has_res:
            outs[2][...] = dh.astype(BF16)

    return epilogue


def _mm_nt_norm_bwd(name, ya, y_spec, wa, w_spec, nj, h, gamma, res, out_dtype, mm_fn=None, want_tm=512, after=None):
    t, n = h.shape
    tm = _tile(t, want_tm)
    nt = t // tm
    row = BS((tm, n), lambda i, j: (i, 0))
    extra = [(h, row), (gamma, BS((1, n), lambda i, j: (0, 0)))]
    out_shapes = [SDS((t, n), out_dtype), SDS((8, n), F32)]
    out_specs = [row, BS((8, n), lambda i, j: (0, 0))]
    if res is not None:
        extra.append((res, row))
        out_shapes.append(SDS((t, n), BF16))
        out_specs.append(row)
    extra.extend((a, BS(memory_space=pl.ANY)) for a in after or ())
    return _mm_nt_epi(
        name, ya, y_spec, wa, w_spec, nj, n, extra, out_shapes, out_specs, _norm_bwd_epilogue(res is not None, out_dtype), tm, nt, mm_fn,
    )


def _dev_block(jj):
    return jj // 2 + NJ * (jj % 2)


def _ffn_dn_mm(y_ref, w_ref):
    acc = None
    for jj in range(2 * NJ):
        part = _dot_nt(y_ref[jj], w_ref[_dev_block(jj)])
        acc = part if acc is None else acc + part
    return acc


def _ffn_bwd(tag, dh, dhb, n_in, h_in, gamma, gu, a, w_in, w_out, collective_id, after):
    t, dn = dh.shape
    dgu = _ffn_bwd_in(f"{tag}_bwd_in", dhb, w_out, 0, gu).reshape(2 * NJ, t, FB)
    dw_out = _mm_tn(
        f"{tag}_dw_out", a, BS((None, t, FB), lambda j: (j, 0, 0)), dhb, BS((t, dn), lambda j: (0, 0)),
        (NJ, FB, dn), BS((None, FB, dn), lambda j: (j, 0, 0)), NJ, scale=0.5,
    )
    dw_in = _mm_tn(
        f"{tag}_dw_in", dgu, BS((None, t, FB), lambda j: (j, 0, 0)), n_in, BS((t, dn), lambda j: (0, 0)),
        (NDEV, FB, dn), BS((None, FB, dn), lambda j: (_dev_block(j), 0, 0)), NDEV,
    )
    entries = [("scatter", dw_in), ("scatter", dw_out.reshape(NDEV, NJ * FB // NDEV, dn))]
    landed = _exchange_sc(f"{tag}_reduce", entries, collective_id, after)
    tm = _tile(t)
    resident = BS((None, NDEV, dn, FB), lambda i, j: (0, 0, 0, 0), pipeline_mode=pl.Buffered(1))
    dh_in, dgam, dhb_in = _mm_nt_norm_bwd(
        f"{tag}_dn", dgu, BS((2 * NJ, tm, FB), lambda i, j: (0, i, 0)), w_in, resident, 1, h_in, gamma, dh, F32, mm_fn=_ffn_dn_mm,
        after=[e[1] for e in entries],
    )
    return dh_in, dhb_in, dgam, landed


def _heads_mm(y_ref, w_ref):
    acc = None
    for h in range(HEADS_B):
        part = _dot_nt(y_ref[h], w_ref[h])
        acc = part if acc is None else acc + part
    return acc


def _dqkv_mm(per):
    def mm(y_ref, w_ref):
        acc = None
        for j in range(NDEV):
            cols = [y_ref[(per * j + k) // 8, :, ((per * j + k) % 8) * 128 : ((per * j + k) % 8 + 1) * 128] for k in range(per)]
            part = _dot_nt(jnp.concatenate(cols, axis=1), w_ref[j])
            acc = part if acc is None else acc + part
        return acc

    return mm


def _kv_latent_bwd(dkv, w_up, ckr, latent_norm, dkr, c64, s64, p64, seq):
    t, wd = ckr.shape
    hb = w_up.shape[-1]
    tm = _tile(min(seq, 512), min(seq, 512))
    nt = t // tm
    nseq = seq // tm

    def epilogue(dn, ex, outs, i, nt_):
        dlat, dg_rows = _norm_bwd(dn, ex[0][...], ex[1][...])
        _acc_rows(outs[1], dg_rows, i, nt_)
        outs[0][:, :KV_LORA] = dlat.astype(BF16)
        outs[0][:, KV_LORA:] = _rope_bwd(ex[2][...], ex[3][...], ex[4][...], ex[5][...]).astype(BF16)

    pos = BS((tm, ROPE), lambda i, j: (i % nseq, 0))
    extra = [
        (ckr, BS((tm, KV_LORA), lambda i, j: (i, 0))), (latent_norm, BS((1, KV_LORA), lambda i, j: (0, 0))),
        (dkr, BS((tm, ROPE), lambda i, j: (i, 0))), (c64, pos), (s64, pos), (p64, BS((ROPE, ROPE), lambda i, j: (0, 0))),
    ]
    def heads_mm(y_ref, w_ref):
        acc = None
        for h in range(HEADS_B):
            part = _dot_nt(y_ref[:, h * hb : (h + 1) * hb], w_ref[h])
            acc = part if acc is None else acc + part
        return acc

    return _mm_nt_epi(
        "kv_latent_bwd", dkv, BS((tm, HEADS_B * hb), lambda i, j: (i, 0)), w_up, BS((HEADS_B, KV_LORA, hb), lambda i, j: (0, 0, 0)),
        1, KV_LORA, extra, [SDS((t, wd), BF16), SDS((8, KV_LORA), F32)],
        [BS((tm, wd), lambda i, j: (i, 0)), BS((8, KV_LORA), lambda i, j: (0, 0))], epilogue, tm, nt, heads_mm,
    )


def _adamw(name, parts, w, m, v):
    n_layers, rows, cols = w.shape
    tr = max(d for d in range(8, min(rows, 256) + 1, 8) if rows % d == 0)
    nb = rows // tr

    def body(*refs):
        p_refs = refs[:n_layers]
        w_ref, m_ref, v_ref, g_ref, d_ref, nm_ref, nv_ref = refs[n_layers : n_layers + 7]
        layer = pl.program_id(0)
        for lp in range(n_layers):

            @pl.when(layer == lp)
            def _():
                g = p_refs[lp][0].astype(F32)
                for k in range(1, NDEV):
                    g = g + p_refs[lp][k].astype(F32)
                g_ref[...] = g

        g = g_ref[...]
        nm = ADAM_B1 * m_ref[...] + (1.0 - ADAM_B1) * g
        nv = ADAM_B2 * v_ref[...] + (1.0 - ADAM_B2) * (g * g)
        nm_ref[...] = nm
        nv_ref[...] = nv
        m_hat = nm / (1.0 - ADAM_B1 ** ADAM_STEP)
        v_hat = nv / (1.0 - ADAM_B2 ** ADAM_STEP)
        d_ref[...] = -ADAM_LR * (m_hat / (jnp.sqrt(v_hat) + ADAM_EPS) + ADAM_WD * w_ref[...])

    def part_spec(lp):
        return BS((NDEV, tr, cols), lambda l, i: (0, jnp.where(l == lp, i, jnp.where(l < lp, 0, nb - 1)), 0))

    row = BS((None, tr, cols), lambda l, i: (l, i, 0))
    return pl.pallas_call(
        body, name=name, grid=(n_layers, nb),
        in_specs=[part_spec(lp) for lp in range(n_layers)] + [row, row, row],
        out_specs=[row] * 4, out_shape=[SDS(w.shape, F32)] * 4,
        compiler_params=_cparams(2),
    )(*parts, w, m, v)


def _pack_small(ffn1_norm, mix_norm, ffn2_norm, kv_norm, final_norm, q_norm, latent_norm, rel_bias, last_row):
    dn = ffn1_norm.shape[-1]

    def rows_of(a, n_rows):
        flat = a.reshape(-1)
        return jnp.pad(flat, (0, n_rows * dn - flat.shape[0])).reshape(n_rows, dn)

    return jnp.concatenate(
        [
            ffn1_norm.reshape(2, dn), mix_norm.reshape(2, dn), ffn2_norm.reshape(2, dn), kv_norm.reshape(1, dn),
            final_norm.reshape(1, dn), rows_of(q_norm, 1), rows_of(latent_norm, 1), rows_of(rel_bias, 5), rows_of(last_row, 1),
        ],
        axis=0,
    )


def _unpack_small(pack):
    dn = pack.shape[-1]
    return dict(
        ffn1_norm=pack[0:2], mix_norm=pack[2:4], ffn2_norm=pack[4:6], kv_norm=pack[6], final_norm=pack[7],
        b_q_norm=pack[8, :Q_LORA].reshape(1, Q_LORA), kv_latent_norm=pack[9, :KV_LORA],
        a_rel_bias=pack[10:15].reshape(-1)[: HEADS_A * NREL].reshape(1, HEADS_A, NREL), last=pack[15],
    )


def kernel(x, ffn1_norm, ffn1_w_in, ffn1_w_out, mix_norm, ffn2_norm, ffn2_w_in, ffn2_w_out, a_w_qkv, a_rel_bias, a_w_o, kv_norm, kv_w_down, kv_latent_norm, kv_w_up, b_w_dq, b_q_norm, b_w_uq, b_w_o, final_norm, loss_target, m_ffn1_norm, m_ffn1_w_in, m_ffn1_w_out, m_mix_norm, m_ffn2_norm, m_ffn2_w_in, m_ffn2_w_out, m_a_w_qkv, m_a_rel_bias, m_a_w_o, m_kv_norm, m_kv_w_down, m_kv_latent_norm, m_kv_w_up, m_b_w_dq, m_b_q_norm, m_b_w_uq, m_b_w_o, m_final_norm, v_ffn1_norm, v_ffn1_w_in, v_ffn1_w_out, v_mix_norm, v_ffn2_norm, v_ffn2_w_in, v_ffn2_w_out, v_a_w_qkv, v_a_rel_bias, v_a_w_o, v_kv_norm, v_kv_w_down, v_kv_latent_norm, v_kv_w_up, v_b_w_dq, v_b_q_norm, v_b_w_uq, v_b_w_o, v_final_norm):
    bl, seq, dn = x.shape
    t = bl * seq
    tm = _tile(t)
    nt = t // tm
    x2 = x.reshape(t, dn)
    target2 = loss_target.reshape(t, dn)

    def gathered(*ws):
        return [("gather", w.astype(BF16)) for w in ws]

    groups = [
        gathered(ffn1_w_in[0]), gathered(ffn1_w_out[0]), gathered(a_w_qkv[0], a_w_o[0]), gathered(ffn2_w_in[0], ffn2_w_out[0]),
        gathered(kv_w_down, kv_w_up), gathered(ffn1_w_in[1], ffn1_w_out[1]), gathered(b_w_dq[0], b_w_uq[0], b_w_o[0]),
        gathered(ffn2_w_in[1], ffn2_w_out[1]),
    ]
    ag = [_exchange_sc(f"gather_{k}", group, GATHER_IDS[k]) for k, group in enumerate(groups)]

    def as_w_in(w):
        return w.reshape(1, NDEV, dn, FB)

    def as_w_out(w):
        return w.reshape(1, NJ, FB, dn)

    c64, s64, p64, c192, s192, p192 = _rope_tables(seq)
    q_norm = b_q_norm.reshape(1, Q_LORA)
    latent_norm = kv_latent_norm.reshape(1, KV_LORA)
    bias = _window_bias(_rel_bias_fwd(jnp.pad(a_rel_bias[0], ((0, 0), (0, NREL_PAD - NREL)))))

    h0, h1, h2, n1, hn, n2, gu1, gu2, a1, a2, w_in1, w_in2, w_out1, w_out2 = ([None, None] for _ in range(14))
    h0[0] = x2
    (n1[0],) = _norm_fwd("norm_x", x2, ffn1_norm[0:1])
    w_in1[0] = as_w_in(ag[0][0])
    gu1[0], a1[0] = _ffn_in("ffn1_in_0", n1[0], w_in1[0], 0)
    w_out1[0] = as_w_out(ag[1][0])
    h1[0], hn[0] = _mm_res_norm("ffn1_out_0", a1[0], w_out1[0], 0, h0[0], mix_norm[0:1], 0.5)
    w_qkv, w_o_a = ag[2]
    qkv_wb = w_qkv.shape[-1]
    w_o_a = w_o_a.reshape(1, 1, dn, dn)
    qkv3 = _qkv_proj("qkv_proj", hn[0], w_qkv)
    o_a, lse_a = _attn_a_fwd(qkv3, bias, bl, seq)
    h2[0], n2[0] = _mm_res_norm("attn_a_out", o_a.reshape(1, t, dn), w_o_a, 0, h1[0], ffn2_norm[0:1], 1.0)
    w_in2[0], w_out2[0] = as_w_in(ag[3][0]), as_w_out(ag[3][1])
    gu2[0], a2[0] = _ffn_in("ffn2_in_0", n2[0], w_in2[0], 0)
    h0[1], hk, n1[1] = _mm_res_norm(
        "ffn2_out_0", a2[0], w_out2[0], 0, h2[0], jnp.concatenate([kv_norm.reshape(1, dn), ffn1_norm[1:2]], axis=0), 0.5
    )
    w_down, w_up = ag[4]
    w_down = w_down.reshape(dn, KV_LORA + ROPE)
    ckr, ckv, kr = _kv_down(hk, w_down, latent_norm, c64, s64, p64, seq)
    kv = _kv_up(ckv, w_up)
    w_in1[1], w_out1[1] = as_w_in(ag[5][0]), as_w_out(ag[5][1])
    gu1[1], a1[1] = _ffn_in("ffn1_in_1", n1[1], w_in1[1], 0)
    h1[1], hn[1] = _mm_res_norm("ffn1_out_1", a1[1], w_out1[1], 0, h0[1], mix_norm[1:2], 0.5)
    w_dq, w_uq, w_o_b = ag[6]
    w_dq = w_dq.reshape(dn, Q_LORA)
    w_o_b = w_o_b.reshape(1, 1, dn, dn)
    cq_pre, cq = _q_down(hn[1], w_dq, q_norm)
    q = _q_up(cq, w_uq, c192, s192, p192, seq)
    o_b, lse_b = _mla_fwd(q, kv, kr, bl, seq)
    h2[1], n2[1] = _mm_res_norm("attn_b_out", o_b.reshape(1, t, dn), w_o_b, 0, h1[1], ffn2_norm[1:2], 1.0)
    w_in2[1], w_out2[1] = as_w_in(ag[7][0]), as_w_out(ag[7][1])
    gu2[1], a2[1] = _ffn_in("ffn2_in_1", n2[1], w_in2[1], 0)
    (h_last,) = _mm_res_norm("ffn2_out_1", a2[1], w_out2[1], 0, h2[1], None, 0.5)
    dh, dhb, dg_final, loss_part = _loss_final(h_last, target2, final_norm.reshape(1, dn))

    dg_ffn1, dg_mix, dg_ffn2, rs_ffn1, rs_ffn2 = ([None, None] for _ in range(5))

    def whole(rows, cols):
        return BS((rows, cols), lambda j: (0, 0))

    def dw_rows(name, xa, ya):
        n = ya.shape[1]
        return _mm_tn(name, xa, whole(t, dn), ya, whole(t, n), (dn, n), whole(dn, n), 1).reshape(NDEV, dn // NDEV, n)

    dh, dhb, dg_ffn2[1], rs_ffn2[1] = _ffn_bwd(
        "ffn2_1", dh, dhb, n2[1], h2[1], ffn2_norm[1:2], gu2[1], a2[1], w_in2[1], w_out2[1], REDUCE_IDS[0], ()
    )
    do_b = _mm_nt_plain("attn_b_do", dhb, w_o_b.reshape(dn, dn))
    dw_o_b = dw_rows("attn_b_dwo", o_b, dhb)
    dq_pre, dkv, dkr = _mla_bwd(q, kv, kr, o_b, lse_b, do_b, c192, s192, p192, bl, seq)
    dw_uq = _mm_tn(
        "dw_uq", cq, whole(t, Q_LORA), dq_pre, BS((None, t, QK_B), lambda j: (j, 0, 0)),
        (HEADS_B, Q_LORA, QK_B), BS((None, Q_LORA, QK_B), lambda j: (j, 0, 0)), HEADS_B,
    )
    dcq_pre, dg_q = _mm_nt_norm_bwd(
        "dcq", dq_pre, BS((HEADS_B, tm, QK_B), lambda i, j: (0, i, 0)), w_uq, BS((HEADS_B, Q_LORA, QK_B), lambda i, j: (0, 0, 0)),
        1, cq_pre, q_norm, None, BF16, mm_fn=_heads_mm,
    )
    dw_dq = dw_rows("dw_dq", hn[1], dcq_pre)
    dh, dg_mix[1], dhb = _mm_nt_norm_bwd(
        "dhn_b", dcq_pre, BS((tm, Q_LORA), lambda i, j: (i, 0)), w_dq, BS((dn, Q_LORA), lambda i, j: (0, 0)),
        1, h1[1], mix_norm[1:2], dh, F32,
    )
    dh, dhb, dg_ffn1[1], rs_ffn1[1] = _ffn_bwd(
        "ffn1_1", dh, dhb, n1[1], h0[1], ffn1_norm[1:2], gu1[1], a1[1], w_in1[1], w_out1[1], REDUCE_IDS[1], rs_ffn2[1][:1]
    )
    dw_up = _mm_tn(
        "dw_up", ckv, whole(t, KV_LORA), dkv, BS((t, NOPE + V_DIM), lambda j: (0, j)),
        (HEADS_B, KV_LORA, NOPE + V_DIM), BS((None, KV_LORA, NOPE + V_DIM), lambda j: (j, 0, 0)), HEADS_B,
    )
    dckr, dg_latent = _kv_latent_bwd(dkv, w_up, ckr, latent_norm, dkr, c64, s64, p64, seq)
    dw_down = dw_rows("dw_down", hk, dckr)
    dh, dg_kv, dhb = _mm_nt_norm_bwd(
        "dhk", dckr, BS((tm, KV_LORA + ROPE), lambda i, j: (i, 0)), w_down, BS((dn, KV_LORA + ROPE), lambda i, j: (0, 0)),
        1, h0[1], kv_norm.reshape(1, dn), dh, F32,
    )
    dh, dhb, dg_ffn2[0], rs_ffn2[0] = _ffn_bwd(
        "ffn2_0", dh, dhb, n2[0], h2[0], ffn2_norm[0:1], gu2[0], a2[0], w_in2[0], w_out2[0], REDUCE_IDS[2], rs_ffn1[1][:1]
    )
    do_a = _mm_nt_plain("attn_a_do", dhb, w_o_a.reshape(dn, dn))
    dw_o_a = dw_rows("attn_a_dwo", o_a, dhb)
    dqkv3, dbias = _attn_a_bwd(qkv3, o_a, lse_a, do_a, bias, bl, seq)
    dw_qkv = _dw_qkv(hn[0], dqkv3, qkv_wb)
    mixer_grads = [dw_o_a, dw_qkv, dw_o_b, dw_uq, dw_dq, dw_up, dw_down]
    dh, dg_mix[0], dhb = _mm_nt_norm_bwd(
        "dhn_a", dqkv3, BS((3, tm, dn), lambda i, j: (0, i, 0)), w_qkv, BS((NDEV, dn, qkv_wb), lambda i, j: (0, 0, 0)),
        1, h1[0], mix_norm[0:1], dh, F32, mm_fn=_dqkv_mm(qkv_wb // 128), after=mixer_grads,
    )
    rs_mixers = _exchange_sc("mixers_reduce", [("scatter", g) for g in mixer_grads], REDUCE_IDS[3], rs_ffn2[0][:1])
    dh, dhb, dg_ffn1[0], rs_ffn1[0] = _ffn_bwd(
        "ffn1_0", dh, dhb, n1[0], h0[0], ffn1_norm[0:1], gu1[0], a1[0], w_in1[0], w_out1[0], REDUCE_IDS[4], rs_mixers[:1]
    )
    grad_x = dh.reshape(bl, seq, dn)
    dtable = _rel_bias_bwd(_window_bias_bwd(dbias))[:, :NREL]

    small = _pack_small(
        jnp.stack([dg_ffn1[0][0], dg_ffn1[1][0]]), jnp.stack([dg_mix[0][0], dg_mix[1][0]]), jnp.stack([dg_ffn2[0][0], dg_ffn2[1][0]]),
        dg_kv[0], dg_final[0], dg_q[0], dg_latent[0], dtable, loss_part[0],
    )
    (r_small,) = _exchange("gather_small_grads", [("gather", small)])

    def update(name, parts, w, m, v):
        shape3 = (len(parts),) + w.shape[-2:]
        parts = [p.reshape((NDEV,) + shape3[1:]) for p in parts]
        outs = _adamw(name, parts, w.reshape(shape3), m.reshape(shape3), v.reshape(shape3))
        return [o.reshape(w.shape) for o in outs]

    res = {}
    r_in2_1, r_out2_1 = rs_ffn2[1]
    r_in1_1, r_out1_1 = rs_ffn1[1]
    r_in2_0, r_out2_0 = rs_ffn2[0]
    r_in1_0, r_out1_0 = rs_ffn1[0]
    r_o_a, r_qkv, r_o_b, r_uq, r_dq, r_up, r_down = rs_mixers
    def update_transposed(name, parts, w, m, v):
        outs = update(name, parts, *[jnp.swapaxes(a, 1, 2) for a in (w, m, v)])
        return [jnp.swapaxes(o, 1, 2) for o in outs]

    res["ffn2_w_in"] = update_transposed("adamw_ffn2_w_in", [r_in2_0, r_in2_1], ffn2_w_in, m_ffn2_w_in, v_ffn2_w_in)
    res["ffn2_w_out"] = update("adamw_ffn2_w_out", [r_out2_0, r_out2_1], ffn2_w_out, m_ffn2_w_out, v_ffn2_w_out)
    res["kv_w_down"] = update("adamw_kv_w_down", [r_down], kv_w_down, m_kv_w_down, v_kv_w_down)
    res["kv_w_up"] = update("adamw_kv_w_up", [r_up], kv_w_up, m_kv_w_up, v_kv_w_up)
    res["b_w_dq"] = update("adamw_b_w_dq", [r_dq], b_w_dq, m_b_w_dq, v_b_w_dq)
    res["b_w_uq"] = update("adamw_b_w_uq", [r_uq], b_w_uq, m_b_w_uq, v_b_w_uq)
    res["b_w_o"] = update("adamw_b_w_o", [r_o_b], b_w_o, m_b_w_o, v_b_w_o)
    res["a_w_qkv"] = update("adamw_a_w_qkv", [r_qkv], a_w_qkv, m_a_w_qkv, v_a_w_qkv)
    res["a_w_o"] = update("adamw_a_w_o", [r_o_a], a_w_o, m_a_w_o, v_a_w_o)
    res["ffn1_w_in"] = update_transposed("adamw_ffn1_w_in", [r_in1_0, r_in1_1], ffn1_w_in, m_ffn1_w_in, v_ffn1_w_in)
    res["ffn1_w_out"] = update("adamw_ffn1_w_out", [r_out1_0, r_out1_1], ffn1_w_out, m_ffn1_w_out, v_ffn1_w_out)
    zero_row = jnp.zeros((dn,), F32)
    packs = [
        _pack_small(f1, mx, f2, kvn, fin, qn, lat, rel, zero_row)
        for f1, mx, f2, kvn, fin, qn, lat, rel in (
            (ffn1_norm, mix_norm, ffn2_norm, kv_norm, final_norm, b_q_norm, kv_latent_norm, a_rel_bias),
            (m_ffn1_norm, m_mix_norm, m_ffn2_norm, m_kv_norm, m_final_norm, m_b_q_norm, m_kv_latent_norm, m_a_rel_bias),
            (v_ffn1_norm, v_mix_norm, v_ffn2_norm, v_kv_norm, v_final_norm, v_b_q_norm, v_kv_latent_norm, v_a_rel_bias),
        )
    ]
    small_out = [_unpack_small(o[0]) for o in _adamw("adamw_small", [r_small], *[p[None] for p in packs])]
    for name in ("ffn1_norm", "mix_norm", "ffn2_norm", "a_rel_bias", "kv_norm", "kv_latent_norm", "b_q_norm", "final_norm"):
        res[name] = [so[name] for so in small_out]
    loss = small_out[0]["last"][0]

    order = [
        "ffn1_norm", "ffn1_w_in", "ffn1_w_out", "mix_norm", "ffn2_norm", "ffn2_w_in", "ffn2_w_out", "a_w_qkv", "a_rel_bias",
        "a_w_o", "kv_norm", "kv_w_down", "kv_latent_norm", "kv_w_up", "b_w_dq", "b_q_norm", "b_w_uq", "b_w_o", "final_norm",
    ]
    return (loss, grad_x, *[res[n][0] for n in order], *[res[n][1] for n in order], *[res[n][2] for n in order], *[res[n][3] for n in order])
```

```python
import jax
import jax.numpy as jnp
import numpy as np
from jax import lax
from jax.experimental import pallas as pl
from jax.experimental.pallas import tpu as pltpu
from jax.experimental.pallas import tpu_sc as plsc

NDEV = 8
D_MODEL = 1024
D_FF = 2816
FB = 2 * D_FF // NDEV
NJ = D_FF // FB
CHUNK = 64
LEFT_CHUNKS = 8
PAD = LEFT_CHUNKS * CHUNK
BAND = PAD + CHUNK
CHUNKS_PER_STEP = 4
WINDOW = PAD + CHUNKS_PER_STEP * CHUNK
STEP_ROWS = CHUNKS_PER_STEP * 2 * CHUNK
MAX_REL = 128
NREL = 2 * MAX_REL + 1
NREL_PAD = 384
HEADS_A = 16
HEADS_B = 8
NOPE = 128
ROPE = 64
QK_B = NOPE + ROPE
V_DIM = 128
Q_LORA = 768
KV_LORA = 256
ROPE_THETA = 10000.0
EPS = 1e-6
NEG_INF = -1e30
MLA_TQ = 256
MLA_TK_FWD = 256
MLA_TK_BWD = 1024
ADAM_LR = 0.001
ADAM_B1 = 0.9
ADAM_B2 = 0.999
ADAM_EPS = 1e-08
ADAM_WD = 0.01
ADAM_STEP = 10
PACK_ROWS = 16
GATHER_IDS = tuple(range(1, 9))
REDUCE_IDS = tuple(range(9, 14))
VMEM_LIMIT_BYTES = 56 * 1024 * 1024

F32 = jnp.float32
BF16 = jnp.bfloat16
SDS = jax.ShapeDtypeStruct
BS = pl.BlockSpec
MESH = pl.DeviceIdType.MESH


def _cparams(n_axes):
    return pltpu.CompilerParams(dimension_semantics=("arbitrary",) * n_axes, vmem_limit_bytes=VMEM_LIMIT_BYTES)


def _tile(t, want=512):
    return want if t % want == 0 else t


def _dot(a, b):
    return jnp.dot(a, b, preferred_element_type=F32)


def _dot_nt(a, b):
    return lax.dot_general(a, b, (((1,), (1,)), ((), ())), preferred_element_type=F32)


def _dot_tn(a, b):
    return lax.dot_general(a, b, (((0,), (0,)), ((), ())), preferred_element_type=F32)


def _split3(a):
    hi = a.astype(BF16)
    rest = a - hi.astype(F32)
    mid = rest.astype(BF16)
    return hi, mid, (rest - mid.astype(F32)).astype(BF16)


def _dot_exact(a, onehot, transposed=False):
    ob = onehot.astype(BF16)
    dot = _dot_nt if transposed else _dot
    hi, mid, lo = _split3(a)
    return dot(hi, ob) + dot(mid, ob) + dot(lo, ob)


def _rms_scale(h):
    return lax.rsqrt(jnp.mean(h * h, axis=-1, keepdims=True) + EPS)


def _acc_rows(ref, val, step, n_steps):
    part = val.reshape(val.shape[0] // 8, 8, val.shape[1]).sum(axis=0)

    @pl.when(step == 0)
    def _():
        ref[...] = part

    @pl.when(step > 0)
    def _():
        ref[...] += part

    @pl.when(step == n_steps - 1)
    def _():
        ref[...] = jnp.broadcast_to(jnp.sum(ref[...], axis=0, keepdims=True), ref.shape)


def _exchange_plan(entries):
    ins = [e[1] for e in entries]
    kinds = [e[0] for e in entries]
    lands = [SDS((NDEV,) + a.shape if k == "gather" else a.shape, a.dtype) for k, a in zip(kinds, ins)]
    return ins, lands, kinds


def _mesh_place():
    x, y, c = lax.axis_index("x"), lax.axis_index("y"), lax.axis_index("c")
    return (x, y, c), 4 * x + 2 * y + c


def _flipped(place, p):
    x, y, c = place
    px = 1 - x if p & 4 else x
    py = 1 - y if p & 2 else y
    pc = 1 - c if p & 1 else c
    return (px, py, pc), 4 * px + 2 * py + pc


def _ends(kind, src_ref, land_ref, origin, target):
    if kind == "gather":
        return src_ref, land_ref.at[origin]
    return src_ref.at[target], land_ref.at[origin]


def _remote(kind, src_ref, land_ref, send_sems, recv_sems, k, p, place, me, arriving):
    peer_pos, peer = _flipped(place, p)
    src, dst = _ends(kind, src_ref, land_ref, me, peer)
    if arriving:
        dst = _ends(kind, src_ref, land_ref, peer, me)[1]
    sem = k * (NDEV - 1) + p - 1
    return pltpu.make_async_remote_copy(
        src_ref=src, dst_ref=dst, send_sem=send_sems.at[sem], recv_sem=recv_sems.at[sem], device_id=peer_pos, device_id_type=MESH,
    )


def _exchange(name, entries, after=()):
    ins, lands, kinds = _exchange_plan(entries)
    n = len(ins)
    after = tuple(after)

    def body(*refs):
        refs = refs[:n] + refs[n + len(after) :]
        in_refs, land_refs = refs[:n], refs[n : 2 * n]
        send_sems, recv_sems, local_sems = refs[2 * n :]
        place, me = _mesh_place()
        local = []
        for k in range(n):
            src, dst = _ends(kinds[k], in_refs[k], land_refs[k], me, me)
            local.append(pltpu.make_async_copy(src, dst, local_sems.at[k]))
            local[-1].start()
        sends = []
        for p in range(1, NDEV):
            for k in range(n):
                sends.append(_remote(kinds[k], in_refs[k], land_refs[k], send_sems, recv_sems, k, p, place, me, False))
                sends[-1].start()
        for p in range(1, NDEV):
            for k in range(n):
                _remote(kinds[k], in_refs[k], land_refs[k], send_sems, recv_sems, k, p, place, me, True).wait_recv()
        for cp in sends:
            cp.wait_send()
        for cp in local:
            cp.wait()

    any_spec = BS(memory_space=pl.ANY)
    return pl.pallas_call(
        body, name=name, out_shape=lands, in_specs=[any_spec] * (n + len(after)), out_specs=[any_spec] * n,
        scratch_shapes=[
            pltpu.SemaphoreType.DMA((n * (NDEV - 1),)), pltpu.SemaphoreType.DMA((n * (NDEV - 1),)), pltpu.SemaphoreType.DMA((n,)),
        ],
    )(*ins, *after)


def _exchange_sc(name, entries, collective_id, after=()):
    ins, lands, kinds = _exchange_plan(entries)
    n = len(ins)
    after = tuple(after)

    def launch(*refs):
        refs = refs[:n] + refs[n + len(after) :]
        in_refs, land_refs = refs[:n], refs[n : 2 * n]
        send_sems, recv_sems, local_sems = refs[2 * n :]
        place, me = _mesh_place()
        barrier = pltpu.get_barrier_semaphore()
        for p in range(1, NDEV):
            pl.semaphore_signal(barrier, inc=1, device_id=_flipped(place, p)[0], device_id_type=MESH)
        pl.semaphore_wait(barrier, NDEV - 1)
        local = []
        for k in range(n):
            src, dst = _ends(kinds[k], in_refs[k], land_refs[k], me, me)
            local.append(pltpu.make_async_copy(src, dst, local_sems.at[k]))
            local[-1].start()
        sends = []
        if all(kind == "gather" for kind in kinds):
            for p in (1, 2, 4, 6):
                for k in range(n):
                    sends.append(_remote(kinds[k], in_refs[k], land_refs[k], send_sems, recv_sems, k, p, place, me, False))
                    sends[-1].start()
            sibling_pos, _ = _flipped(place, 1)
            for f in (2, 4, 6):
                _, origin = _flipped(place, f)
                for k in range(n):
                    _remote(kinds[k], in_refs[k], land_refs[k], send_sems, recv_sems, k, f, place, me, True).wait_recv()
                    sem = k * (NDEV - 1) + f
                    sends.append(
                        pltpu.make_async_remote_copy(
                            src_ref=land_refs[k].at[origin], dst_ref=land_refs[k].at[origin], send_sem=send_sems.at[sem],
                            recv_sem=recv_sems.at[sem], device_id=sibling_pos, device_id_type=MESH,
                        )
                    )
                    sends[-1].start()
            for p in (1, 3, 5, 7):
                for k in range(n):
                    _remote(kinds[k], in_refs[k], land_refs[k], send_sems, recv_sems, k, p, place, me, True).wait_recv()
        else:
            for p in range(1, NDEV):
                for k in range(n):
                    sends.append(_remote(kinds[k], in_refs[k], land_refs[k], send_sems, recv_sems, k, p, place, me, False))
                    sends[-1].start()
            for p in range(1, NDEV):
                for k in range(n):
                    _remote(kinds[k], in_refs[k], land_refs[k], send_sems, recv_sems, k, p, place, me, True).wait_recv()
        for cp in sends:
            cp.wait_send()
        for cp in local:
            cp.wait()

    return pl.kernel(
        launch, out_type=tuple(lands), mesh=plsc.ScalarSubcoreMesh(axis_name="sequencer", num_cores=1), name=name,
        scratch_types=(
            pltpu.SemaphoreType.DMA((n * (NDEV - 1),)), pltpu.SemaphoreType.DMA((n * (NDEV - 1),)), pltpu.SemaphoreType.DMA((n,)),
        ),
        compiler_params=pltpu.CompilerParams(collective_id=collective_id),
    )(*ins, *after)


def _norm_fwd(name, h, gammas):
    t, dn = h.shape
    ng = gammas.shape[0]
    tm = _tile(t)

    def body(h_ref, g_ref, *outs):
        hv = h_ref[...]
        hh = hv * _rms_scale(hv)
        for i, o_ref in enumerate(outs):
            o_ref[...] = (hh * g_ref[i : i + 1, :]).astype(BF16)

    row = BS((tm, dn), lambda i: (i, 0))
    return pl.pallas_call(
        body, name=name, grid=(t // tm,),
        in_specs=[row, BS((ng, dn), lambda i: (0, 0))],
        out_specs=[row] * ng, out_shape=[SDS((t, dn), BF16)] * ng,
        compiler_params=_cparams(1),
    )(h, gammas)


def _ffn_in(name, n, w_in, layer):
    t, dn = n.shape
    tm = _tile(t, 1024)

    def body(n_ref, wg_ref, wu_ref, gu_ref, a_ref):
        xv = n_ref[...]
        g = _dot(xv, wg_ref[...])
        u = _dot(xv, wu_ref[...])
        gu_ref[0] = g.astype(BF16)
        gu_ref[1] = u.astype(BF16)
        a_ref[...] = (g * jax.nn.sigmoid(g) * u).astype(BF16)

    return pl.pallas_call(
        body, name=name, grid=(NJ, t // tm),
        in_specs=[
            BS((tm, dn), lambda j, i: (i, 0)),
            BS((None, None, dn, FB), lambda j, i: (layer, j, 0, 0)),
            BS((None, None, dn, FB), lambda j, i: (layer, j + NJ, 0, 0)),
        ],
        out_specs=[BS((None, 2, tm, FB), lambda j, i: (j, 0, i, 0)), BS((None, tm, FB), lambda j, i: (j, i, 0))],
        out_shape=[SDS((NJ, 2, t, FB), BF16), SDS((NJ, t, FB), BF16)],
        compiler_params=_cparams(2),
    )(n, w_in, w_in)


def _mm_res_norm(name, a, w, layer, h_in, gammas, scale):
    nk, t, kb = a.shape
    dn = w.shape[-1]
    ng = 0 if gammas is None else gammas.shape[0]
    tm = _tile(t)

    def body(*refs):
        a_ref, w_ref, h_ref = refs[:3]
        g_ref = refs[3] if ng else None
        outs = refs[3 + (1 if ng else 0) :]
        acc = _dot(a_ref[0], w_ref[0])
        for k in range(1, nk):
            acc += _dot(a_ref[k], w_ref[k])
        ho = h_ref[...] + scale * acc
        outs[0][...] = ho
        if ng:
            hh = ho * _rms_scale(ho)
            for i in range(ng):
                outs[1 + i][...] = (hh * g_ref[i : i + 1, :]).astype(BF16)

    row = BS((tm, dn), lambda i: (i, 0))
    in_specs = [BS((nk, tm, kb), lambda i: (0, i, 0)), BS((None, nk, kb, dn), lambda i: (layer, 0, 0, 0)), row]
    args = [a, w, h_in]
    if ng:
        in_specs.append(BS((ng, dn), lambda i: (0, 0)))
        args.append(gammas)
    return pl.pallas_call(
        body, name=name, grid=(t // tm,),
        in_specs=in_specs,
        out_specs=[row] * (1 + ng), out_shape=[SDS((t, dn), F32)] + [SDS((t, dn), BF16)] * ng,
        compiler_params=_cparams(1),
    )(*args)


def _qkv_proj(name, hn, w_qkv):
    t, dn = hn.shape
    wb = w_qkv.shape[-1]
    per = wb // 128
    tm = _tile(t)

    def body(x_ref, w_ref, o_ref):
        xv = x_ref[...]
        for j in range(NDEV):
            yv = _dot(xv, w_ref[j]).astype(BF16)
            for i in range(per):
                n = per * j + i
                o_ref[n // 8, :, (n % 8) * 128 : (n % 8 + 1) * 128] = yv[:, i * 128 : (i + 1) * 128]

    return pl.pallas_call(
        body, name=name, grid=(t // tm,),
        in_specs=[BS((tm, dn), lambda i: (i, 0)), BS((NDEV, dn, wb), lambda i: (0, 0, 0))],
        out_specs=BS((3, tm, dn), lambda i: (0, i, 0)), out_shape=SDS((3, t, dn), BF16),
        compiler_params=_cparams(1),
    )(hn, w_qkv)


def _rel_onehot(i):
    r = lax.broadcasted_iota(jnp.int32, (NREL_PAD, BAND), 0)
    j = lax.broadcasted_iota(jnp.int32, (NREL_PAD, BAND), 1)
    idx = jnp.clip(PAD + i - j, -MAX_REL, MAX_REL) + MAX_REL
    return (idx == r).astype(F32)


def _rel_bias_fwd(table):
    def body(t_ref, o_ref):
        i8 = pl.program_id(0)
        for ii in range(8):
            o_ref[:, ii, :] = _dot_exact(t_ref[...], _rel_onehot(i8 * 8 + ii))

    return pl.pallas_call(
        body, name="rel_bias_fwd", grid=(CHUNK // 8,),
        in_specs=[BS((HEADS_A, NREL_PAD), lambda i: (0, 0))],
        out_specs=BS((HEADS_A, 8, BAND), lambda i: (0, i, 0)), out_shape=SDS((HEADS_A, CHUNK, BAND), F32),
        compiler_params=_cparams(1),
    )(table)


def _rel_bias_bwd(dbias):
    def body(d_ref, o_ref):
        i8 = pl.program_id(0)
        acc = jnp.zeros((HEADS_A, NREL_PAD), F32)
        for ii in range(8):
            acc += _dot_exact(d_ref[:, ii, :], _rel_onehot(i8 * 8 + ii), transposed=True)

        @pl.when(i8 == 0)
        def _():
            o_ref[...] = acc

        @pl.when(i8 > 0)
        def _():
            o_ref[...] += acc

    return pl.pallas_call(
        body, name="rel_bias_bwd", grid=(CHUNK // 8,),
        in_specs=[BS((HEADS_A, 8, BAND), lambda i: (0, i, 0))],
        out_specs=BS((HEADS_A, NREL_PAD), lambda i: (0, 0)), out_shape=SDS((HEADS_A, NREL_PAD), F32),
        compiler_params=_cparams(1),
    )(dbias)


def _window_bias(bias):
    b = bias.reshape(HEADS_A // 2, 2, CHUNK, BAND)
    per_chunk = [
        jnp.pad(b, ((0, 0), (0, 0), (0, 0), (cc * CHUNK, WINDOW - BAND - cc * CHUNK)), constant_values=NEG_INF)
        for cc in range(CHUNKS_PER_STEP)
    ]
    return jnp.stack(per_chunk, axis=1).reshape(HEADS_A // 2, STEP_ROWS, WINDOW)


def _window_bias_bwd(dwin):
    d = dwin.reshape(HEADS_A // 2, CHUNKS_PER_STEP, 2, CHUNK, WINDOW)
    return sum(d[:, cc, :, :, cc * CHUNK : cc * CHUNK + BAND] for cc in range(CHUNKS_PER_STEP)).reshape(HEADS_A, CHUNK, BAND)


def _step_rows(xs, lane):
    parts = []
    for cc in range(CHUNKS_PER_STEP):
        xc = xs[cc * CHUNK : (cc + 1) * CHUNK]
        parts.append(jnp.where(lane < 64, xc, jnp.zeros_like(xc)))
        parts.append(jnp.where(lane >= 64, xc, jnp.zeros_like(xc)))
    return jnp.concatenate(parts, axis=0)


def _pair_rows(ys, lane):
    parts = []
    for cc in range(CHUNKS_PER_STEP):
        y0 = ys[(2 * cc) * CHUNK : (2 * cc + 1) * CHUNK]
        y1 = ys[(2 * cc + 1) * CHUNK : (2 * cc + 2) * CHUNK]
        parts.append(jnp.where(lane < 64, y0, y1))
    return jnp.concatenate(parts, axis=0)


def _window_scores(q_rows, kwin, bias_win, first_key):
    s = _dot_nt(q_rows, kwin) * (CHUNK ** -0.5) + bias_win
    col = lax.broadcasted_iota(jnp.int32, s.shape, 1)
    return jnp.where(col >= first_key, s, NEG_INF)


def _attn_a_fwd(qkv3, bias_win, bl, seq):
    t, dn = qkv3.shape[1:]
    npair = dn // 128
    step = CHUNKS_PER_STEP * CHUNK

    def body(q_ref, k_ref, v_ref, b_ref, o_ref, lse_ref, kpad, vpad):
        kpad[0:PAD, :] = jnp.zeros((PAD, 128), BF16)
        vpad[0:PAD, :] = jnp.zeros((PAD, 128), BF16)
        kpad[PAD:, :] = k_ref[...]
        vpad[PAD:, :] = v_ref[...]
        lane = lax.broadcasted_iota(jnp.int32, (CHUNK, 128), 1)

        def chunks(it, carry):
            r0 = pl.multiple_of(it * step, step)
            q_rows = _step_rows(q_ref[pl.ds(r0, step), :], lane)
            s = _window_scores(q_rows, kpad[pl.ds(r0, WINDOW), :], b_ref[...], PAD - r0)
            m = jnp.max(s, axis=-1, keepdims=True)
            e = jnp.exp(s - m)
            total = jnp.sum(e, axis=-1, keepdims=True)
            o_rows = _dot((e * (1.0 / total)).astype(BF16), vpad[pl.ds(r0, WINDOW), :])
            o_ref[pl.ds(r0, step), :] = _pair_rows(o_rows, lane).astype(BF16)
            lse_ref[pl.ds(pl.multiple_of(it * STEP_ROWS, STEP_ROWS), STEP_ROWS), :] = m + jnp.log(total)
            return carry

        lax.fori_loop(0, seq // step, chunks, 0, unroll=2)

    return pl.pallas_call(
        body, name="attn_a_fwd", grid=(bl, npair),
        in_specs=[
            BS((None, seq, 128), lambda b, h: (0, b, h)),
            BS((None, seq, 128), lambda b, h: (1, b, h)),
            BS((None, seq, 128), lambda b, h: (2, b, h)),
            BS((None, STEP_ROWS, WINDOW), lambda b, h: (h, 0, 0)),
        ],
        out_specs=[BS((seq, 128), lambda b, h: (b, h)), BS((None, 2 * seq, 1), lambda b, h: (h, b, 0))],
        out_shape=[SDS((t, dn), BF16), SDS((npair, 2 * t, 1), F32)],
        scratch_shapes=[pltpu.VMEM((PAD + seq, 128), BF16), pltpu.VMEM((PAD + seq, 128), BF16)],
        compiler_params=_cparams(2),
    )(qkv3, qkv3, qkv3, bias_win)


def _attn_a_bwd(qkv3, out, lse, do, bias_win, bl, seq):
    t, dn = qkv3.shape[1:]
    npair = dn // 128
    step = CHUNKS_PER_STEP * CHUNK

    def body(q_ref, k_ref, v_ref, o_ref, lse_ref, do_ref, b_ref, dqkv_ref, db_ref, kpad, vpad, dkacc, dvacc):
        b = pl.program_id(1)
        kpad[0:PAD, :] = jnp.zeros((PAD, 128), BF16)
        vpad[0:PAD, :] = jnp.zeros((PAD, 128), BF16)
        kpad[PAD:, :] = k_ref[...]
        vpad[PAD:, :] = v_ref[...]
        dkacc[...] = jnp.zeros_like(dkacc)
        dvacc[...] = jnp.zeros_like(dvacc)

        @pl.when(b == 0)
        def _():
            db_ref[...] = jnp.zeros_like(db_ref)

        lane = lax.broadcasted_iota(jnp.int32, (CHUNK, 128), 1)

        def chunks(it, carry):
            r0 = pl.multiple_of(it * step, step)
            q_rows = _step_rows(q_ref[pl.ds(r0, step), :], lane)
            do_rows = _step_rows(do_ref[pl.ds(r0, step), :], lane)
            kwin = kpad[pl.ds(r0, WINDOW), :]
            vwin = vpad[pl.ds(r0, WINDOW), :]
            o_rows = _step_rows(o_ref[pl.ds(r0, step), :], lane)
            delta = jnp.sum(do_rows.astype(F32) * o_rows.astype(F32), axis=-1, keepdims=True)
            lse_rows = lse_ref[pl.ds(pl.multiple_of(it * STEP_ROWS, STEP_ROWS), STEP_ROWS), :]
            p = jnp.exp(_window_scores(q_rows, kwin, b_ref[...], PAD - r0) - lse_rows)
            ds = p * (_dot_nt(do_rows, vwin) - delta)
            db_ref[...] += ds
            dsb = (ds * (CHUNK ** -0.5)).astype(BF16)
            dqkv_ref[0, pl.ds(r0, step), :] = _pair_rows(_dot(dsb, kwin), lane).astype(BF16)
            dkacc[pl.ds(r0, WINDOW), :] += _dot_tn(dsb, q_rows)
            dvacc[pl.ds(r0, WINDOW), :] += _dot_tn(p.astype(BF16), do_rows)
            return carry

        lax.fori_loop(0, seq // step, chunks, 0, unroll=2)
        dqkv_ref[1] = dkacc[PAD:, :].astype(BF16)
        dqkv_ref[2] = dvacc[PAD:, :].astype(BF16)

    return pl.pallas_call(
        body, name="attn_a_bwd", grid=(npair, bl),
        in_specs=[
            BS((None, seq, 128), lambda h, b: (0, b, h)),
            BS((None, seq, 128), lambda h, b: (1, b, h)),
            BS((None, seq, 128), lambda h, b: (2, b, h)),
            BS((seq, 128), lambda h, b: (b, h)),
            BS((None, 2 * seq, 1), lambda h, b: (h, b, 0)),
            BS((seq, 128), lambda h, b: (b, h)),
            BS((None, STEP_ROWS, WINDOW), lambda h, b: (h, 0, 0)),
        ],
        out_specs=[BS((3, seq, 128), lambda h, b: (0, b, h)), BS((None, STEP_ROWS, WINDOW), lambda h, b: (h, 0, 0))],
        out_shape=[SDS((3, t, dn), BF16), SDS((HEADS_A // 2, STEP_ROWS, WINDOW), F32)],
        scratch_shapes=[
            pltpu.VMEM((PAD + seq, 128), BF16), pltpu.VMEM((PAD + seq, 128), BF16),
            pltpu.VMEM((PAD + seq, 128), F32), pltpu.VMEM((PAD + seq, 128), F32),
        ],
        compiler_params=_cparams(2),
    )(qkv3, qkv3, qkv3, out, lse, do, bias_win)


def _rope_tables(seq):
    half = ROPE // 2
    freqs = ROPE_THETA ** (-jnp.arange(half, dtype=F32) / half)
    ang = jnp.arange(seq, dtype=F32)[:, None] * freqs[None, :]
    cos, sin = jnp.cos(ang), jnp.sin(ang)
    c64 = jnp.concatenate([cos, cos], axis=1)
    s64 = jnp.concatenate([-sin, sin], axis=1)
    c192 = jnp.concatenate([jnp.ones((seq, NOPE), F32), c64], axis=1)
    s192 = jnp.concatenate([jnp.zeros((seq, NOPE), F32), s64], axis=1)
    p64 = np.zeros((ROPE, ROPE), np.float32)
    for col in range(ROPE):
        p64[(col + half) % ROPE, col] = 1.0
    p192 = np.zeros((QK_B, QK_B), np.float32)
    p192[NOPE:, NOPE:] = p64
    return c64, s64, jnp.asarray(p64), c192, s192, jnp.asarray(p192)


def _rope(xv, cos, sin_signed, swap):
    return xv * cos + _dot_exact(xv, swap) * sin_signed


def _rope_bwd(dy, cos, sin_signed, swap):
    return dy * cos + _dot_exact(dy * sin_signed, swap)


def _q_down(hn, w_dq, q_norm):
    t, dn = hn.shape
    ql = w_dq.shape[1]
    tm = _tile(t)

    def body(x_ref, w_ref, g_ref, pre_ref, cq_ref):
        pre = _dot(x_ref[...], w_ref[...])
        pre_ref[...] = pre
        cq_ref[...] = (pre * _rms_scale(pre) * g_ref[...]).astype(BF16)

    return pl.pallas_call(
        body, name="q_down", grid=(t // tm,),
        in_specs=[BS((tm, dn), lambda i: (i, 0)), BS((dn, ql), lambda i: (0, 0)), BS((1, ql), lambda i: (0, 0))],
        out_specs=[BS((tm, ql), lambda i: (i, 0))] * 2, out_shape=[SDS((t, ql), F32), SDS((t, ql), BF16)],
        compiler_params=_cparams(1),
    )(hn, w_dq, q_norm)


def _q_up(cq, w_uq, c192, s192, p192, seq):
    t, ql = cq.shape
    tm = _tile(min(seq, 512), min(seq, 512))
    nseq = seq // tm

    def body(x_ref, w_ref, c_ref, s_ref, p_ref, o_ref):
        xv = x_ref[...]
        for h in range(HEADS_B):
            o_ref[h] = _rope(_dot(xv, w_ref[h]), c_ref[...], s_ref[...], p_ref[...]).astype(BF16)

    pos = BS((tm, QK_B), lambda i: (i % nseq, 0))
    return pl.pallas_call(
        body, name="q_up", grid=(t // tm,),
        in_specs=[
            BS((tm, ql), lambda i: (i, 0)), BS((HEADS_B, ql, QK_B), lambda i: (0, 0, 0)), pos, pos,
            BS((QK_B, QK_B), lambda i: (0, 0)),
        ],
        out_specs=BS((HEADS_B, tm, QK_B), lambda i: (0, i, 0)), out_shape=SDS((HEADS_B, t, QK_B), BF16),
        compiler_params=_cparams(1),
    )(cq, w_uq, c192, s192, p192)


def _kv_down(hk, w_down, latent_norm, c64, s64, p64, seq):
    t, dn = hk.shape
    wd = w_down.shape[1]
    tm = _tile(min(seq, 512), min(seq, 512))
    nseq = seq // tm

    def body(x_ref, w_ref, g_ref, c_ref, s_ref, p_ref, ckr_ref, ckv_ref, kr_ref):
        ckr = _dot(x_ref[...], w_ref[...])
        ckr_ref[...] = ckr
        lat = ckr[:, :KV_LORA]
        ckv_ref[...] = (lat * _rms_scale(lat) * g_ref[...]).astype(BF16)
        kr_ref[...] = _rope(ckr[:, KV_LORA:], c_ref[...], s_ref[...], p_ref[...]).astype(BF16)

    pos = BS((tm, ROPE), lambda i: (i % nseq, 0))
    return pl.pallas_call(
        body, name="kv_down", grid=(t // tm,),
        in_specs=[
            BS((tm, dn), lambda i: (i, 0)), BS((dn, wd), lambda i: (0, 0)), BS((1, KV_LORA), lambda i: (0, 0)), pos, pos,
            BS((ROPE, ROPE), lambda i: (0, 0)),
        ],
        out_specs=[BS((tm, wd), lambda i: (i, 0)), BS((tm, KV_LORA), lambda i: (i, 0)), BS((tm, ROPE), lambda i: (i, 0))],
        out_shape=[SDS((t, wd), F32), SDS((t, KV_LORA), BF16), SDS((t, ROPE), BF16)],
        compiler_params=_cparams(1),
    )(hk, w_down, latent_norm, c64, s64, p64)


def _kv_up(ckv, w_up):
    t, kl = ckv.shape
    hb = w_up.shape[-1]
    tm = _tile(t)

    def body(x_ref, w_ref, o_ref):
        xv = x_ref[...]
        for h in range(HEADS_B):
            o_ref[:, h * hb : (h + 1) * hb] = _dot(xv, w_ref[h]).astype(BF16)

    return pl.pallas_call(
        body, name="kv_up", grid=(t // tm,),
        in_specs=[BS((tm, kl), lambda i: (i, 0)), BS((HEADS_B, kl, hb), lambda i: (0, 0, 0))],
        out_specs=BS((tm, HEADS_B * hb), lambda i: (i, 0)), out_shape=SDS((t, HEADS_B * hb), BF16),
        compiler_params=_cparams(1),
    )(ckv, w_up)


def _mla_diagonal_mask(tq):
    rows = lax.broadcasted_iota(jnp.int32, (tq, tq), 0)
    cols = lax.broadcasted_iota(jnp.int32, (tq, tq), 1)
    return jnp.where(jnp.right_shift(cols, 6) <= jnp.right_shift(rows, 6), 0.0, NEG_INF)


def _mla_key_tiles(n_keys, tk):
    return [(slice(k0, min(k0 + tk, n_keys)), min(k0 + tk, n_keys) == n_keys) for k0 in range(0, n_keys, tk)]


def _mla_scores(qi, kt, diagonal):
    s = _dot_nt(qi, kt) * (QK_B ** -0.5)
    if diagonal is None:
        return s
    tq, width = s.shape
    own = s[:, width - tq :] + diagonal
    return own if width == tq else jnp.concatenate([s[:, : width - tq], own], axis=1)


def _mla_fwd(q, kv, kr, bl, seq):
    t = kv.shape[0]
    tq = min(MLA_TQ, seq)

    def body(q_ref, kn_ref, v_ref, kr_ref, o_ref, lse_ref):
        kcat = jnp.concatenate([kn_ref[...], kr_ref[...]], axis=1)
        vv = v_ref[...]
        diagonal = _mla_diagonal_mask(tq)
        for i in range(seq // tq):
            rows = slice(i * tq, (i + 1) * tq)
            qi = q_ref[rows, :]
            m = total = acc = None
            for keys, own in _mla_key_tiles((i + 1) * tq, MLA_TK_FWD):
                s = _mla_scores(qi, kcat[keys], diagonal if own else None)
                m_blk = jnp.max(s, axis=-1, keepdims=True)
                if m is None:
                    m_new = m_blk
                    e = jnp.exp(s - m_new)
                    total = jnp.sum(e, axis=-1, keepdims=True)
                    acc = _dot(e.astype(BF16), vv[keys])
                else:
                    m_new = jnp.maximum(m, m_blk)
                    keep = jnp.exp(m - m_new)
                    e = jnp.exp(s - m_new)
                    total = keep * total + jnp.sum(e, axis=-1, keepdims=True)
                    acc = keep * acc + _dot(e.astype(BF16), vv[keys])
                m = m_new
            o_ref[rows, :] = (acc / total).astype(BF16)
            lse_ref[rows, :] = m + jnp.log(total)

    return pl.pallas_call(
        body, name="mla_fwd", grid=(bl, HEADS_B),
        in_specs=[
            BS((None, seq, QK_B), lambda b, h: (h, b, 0)),
            BS((seq, NOPE), lambda b, h: (b, 2 * h)),
            BS((seq, V_DIM), lambda b, h: (b, 2 * h + 1)),
            BS((seq, ROPE), lambda b, h: (b, 0)),
        ],
        out_specs=[BS((seq, V_DIM), lambda b, h: (b, h)), BS((None, seq, 1), lambda b, h: (h, b, 0))],
        out_shape=[SDS((t, HEADS_B * V_DIM), BF16), SDS((HEADS_B, t, 1), F32)],
        compiler_params=_cparams(2),
    )(q, kv, kv, kr)


def _mla_bwd(q, kv, kr, o, lse, do, c192, s192, p192, bl, seq):
    t = kv.shape[0]
    tq = min(MLA_TQ, seq)

    def body(q_ref, kn_ref, v_ref, kr_ref, o_ref, lse_ref, do_ref, c_ref, s_ref, p_ref, dq_ref, dkv_ref, dkr_ref, dkacc, dvacc):
        h = pl.program_id(1)
        kcat = jnp.concatenate([kn_ref[...], kr_ref[...]], axis=1)
        vv = v_ref[...]
        dkacc[...] = jnp.zeros_like(dkacc)
        dvacc[...] = jnp.zeros_like(dvacc)
        diagonal = _mla_diagonal_mask(tq)
        for i in range(seq // tq):
            rows = slice(i * tq, (i + 1) * tq)
            qi = q_ref[rows, :]
            doi = do_ref[rows, :]
            lse_i = lse_ref[rows, :]
            delta = jnp.sum(doi.astype(F32) * o_ref[rows, :].astype(F32), axis=-1, keepdims=True)
            dq = None
            for keys, own in _mla_key_tiles((i + 1) * tq, MLA_TK_BWD):
                p = jnp.exp(_mla_scores(qi, kcat[keys], diagonal if own else None) - lse_i)
                ds = p * (_dot_nt(doi, vv[keys]) - delta)
                dsb = (ds * (QK_B ** -0.5)).astype(BF16)
                dq_blk = _dot(dsb, kcat[keys])
                dq = dq_blk if dq is None else dq + dq_blk
                dkacc[keys, :] += _dot_tn(dsb, qi)
                dvacc[keys, :] += _dot_tn(p.astype(BF16), doi)
            dq_ref[rows, :] = _rope_bwd(dq, c_ref[rows, :], s_ref[rows, :], p_ref[...]).astype(BF16)
        dk = dkacc[...]
        dkv_ref[:, :NOPE] = dk[:, :NOPE].astype(BF16)
        dkv_ref[:, NOPE:] = dvacc[...].astype(BF16)

        @pl.when(h == 0)
        def _():
            dkr_ref[...] = dk[:, NOPE:]

        @pl.when(h > 0)
        def _():
            dkr_ref[...] += dk[:, NOPE:]

    return pl.pallas_call(
        body, name="mla_bwd", grid=(bl, HEADS_B),
        in_specs=[
            BS((None, seq, QK_B), lambda b, h: (h, b, 0)),
            BS((seq, NOPE), lambda b, h: (b, 2 * h)),
            BS((seq, V_DIM), lambda b, h: (b, 2 * h + 1)),
            BS((seq, ROPE), lambda b, h: (b, 0)),
            BS((seq, V_DIM), lambda b, h: (b, h)),
            BS((None, seq, 1), lambda b, h: (h, b, 0)),
            BS((seq, V_DIM), lambda b, h: (b, h)),
            BS((seq, QK_B), lambda b, h: (0, 0)),
            BS((seq, QK_B), lambda b, h: (0, 0)),
            BS((QK_B, QK_B), lambda b, h: (0, 0)),
        ],
        out_specs=[
            BS((None, seq, QK_B), lambda b, h: (h, b, 0)),
            BS((seq, NOPE + V_DIM), lambda b, h: (b, h)),
            BS((seq, ROPE), lambda b, h: (b, 0)),
        ],
        out_shape=[SDS((HEADS_B, t, QK_B), BF16), SDS((t, HEADS_B * (NOPE + V_DIM)), BF16), SDS((t, ROPE), F32)],
        scratch_shapes=[pltpu.VMEM((seq, QK_B), F32), pltpu.VMEM((seq, V_DIM), F32)],
        compiler_params=_cparams(2),
    )(q, kv, kv, kr, o, lse, do, c192, s192, p192)


def _loss_final(h, target, gamma):
    t, dn = h.shape
    tm = _tile(t)
    nt = t // tm

    def body(h_ref, t_ref, g_ref, dh_ref, dhb_ref, dg_ref, loss_ref):
        i = pl.program_id(0)
        hv = h_ref[...]
        r = _rms_scale(hv)
        hh = hv * r
        gam = g_ref[...]
        err = hh * gam - t_ref[...]
        part = 0.5 * jnp.sum(jnp.mean(err * err, axis=-1, keepdims=True))

        @pl.when(i == 0)
        def _():
            loss_ref[...] = jnp.zeros_like(loss_ref)

        loss_ref[...] += part
        dy = err * (1.0 / dn)
        _acc_rows(dg_ref, dy * hh, i, nt)
        t1 = dy * gam
        dh = r * (t1 - hh * jnp.mean(t1 * hh, axis=-1, keepdims=True))
        dh_ref[...] = dh
        dhb_ref[...] = dh.astype(BF16)

    row = BS((tm, dn), lambda i: (i, 0))
    return pl.pallas_call(
        body, name="loss_final", grid=(nt,),
        in_specs=[row, row, BS((1, dn), lambda i: (0, 0))],
        out_specs=[row, row, BS((8, dn), lambda i: (0, 0)), BS((8, 128), lambda i: (0, 0))],
        out_shape=[SDS((t, dn), F32), SDS((t, dn), BF16), SDS((8, dn), F32), SDS((8, 128), F32)],
        compiler_params=_cparams(1),
    )(h, target, gamma)


def _ffn_bwd_in(name, dh, w_out, layer, gu):
    t, dn = dh.shape
    tm = _tile(t, 1024)

    def body(dh_ref, w_ref, gu_ref, o_ref):
        da = 0.5 * _dot_nt(dh_ref[...], w_ref[...])
        g = gu_ref[0].astype(F32)
        u = gu_ref[1].astype(F32)
        sg = jax.nn.sigmoid(g)
        o_ref[0] = (da * u * (sg * (1.0 + g * (1.0 - sg)))).astype(BF16)
        o_ref[1] = (da * (g * sg)).astype(BF16)

    blk = BS((None, 2, tm, FB), lambda j, i: (j, 0, i, 0))
    return pl.pallas_call(
        body, name=name, grid=(NJ, t // tm),
        in_specs=[BS((tm, dn), lambda j, i: (i, 0)), BS((None, None, FB, dn), lambda j, i: (layer, j, 0, 0)), blk],
        out_specs=blk, out_shape=SDS((NJ, 2, t, FB), BF16),
        compiler_params=_cparams(2),
    )(dh, w_out, gu)


def _mm_nt_plain(name, xf, w):
    t, dn = xf.shape
    n = w.shape[0]
    tm = _tile(t)

    def body(x_ref, w_ref, o_ref):
        o_ref[...] = _dot_nt(x_ref[...], w_ref[...]).astype(BF16)

    return pl.pallas_call(
        body, name=name, grid=(t // tm,),
        in_specs=[BS((tm, dn), lambda i: (i, 0)), BS((n, dn), lambda i: (0, 0))],
        out_specs=BS((tm, n), lambda i: (i, 0)), out_shape=SDS((t, n), BF16),
        compiler_params=_cparams(1),
    )(xf, w)


def _mm_tn(name, xa, x_spec, ya, y_spec, out_shape, out_spec, nj, scale=None):
    def body(x_ref, y_ref, o_ref):
        acc = _dot_tn(x_ref[...], y_ref[...])
        o_ref[...] = (acc if scale is None else scale * acc).astype(BF16)

    return pl.pallas_call(
        body, name=name, grid=(nj,),
        in_specs=[x_spec, y_spec], out_specs=out_spec, out_shape=SDS(out_shape, BF16),
        compiler_params=_cparams(1),
    )(xa, ya)


def _dw_qkv(hn, dqkv3, wb):
    t, dn = hn.shape
    per = wb // 128

    def body(x_ref, *refs):
        cols = [y_ref[...] for y_ref in refs[:per]]
        refs[per][...] = _dot_tn(x_ref[...], jnp.concatenate(cols, axis=1)).astype(BF16)

    def piece(k):
        return BS((None, t, 128), lambda j: ((per * j + k) // 8, 0, (per * j + k) % 8))

    return pl.pallas_call(
        body, name="dw_qkv", grid=(NDEV,),
        in_specs=[BS((t, dn), lambda j: (0, 0))] + [piece(k) for k in range(per)],
        out_specs=BS((None, dn, wb), lambda j: (j, 0, 0)), out_shape=SDS((NDEV, dn, wb), BF16),
        compiler_params=_cparams(1),
    )(hn, *([dqkv3] * per))


def _mm_nt_epi(name, ya, y_spec, wa, w_spec, nj, n_out, extra, out_shapes, out_specs, epilogue, tm, nt, mm_fn=None):
    n_extra = len(extra)
    n_outs = len(out_shapes)

    def body(*refs):
        y_ref, w_ref = refs[:2]
        ex = refs[2 : 2 + n_extra]
        outs = refs[2 + n_extra : 2 + n_extra + n_outs]
        i = pl.program_id(0)
        j = pl.program_id(1)
        part = _dot_nt(y_ref[...], w_ref[...]) if mm_fn is None else mm_fn(y_ref, w_ref)
        if nj == 1:
            epilogue(part, ex, outs, i, nt)
            return
        acc = refs[-1]

        @pl.when(j == 0)
        def _():
            acc[...] = part

        @pl.when(j > 0)
        def _():
            acc[...] += part

        @pl.when(j == nj - 1)
        def _():
            epilogue(acc[...], ex, outs, i, nt)

    return pl.pallas_call(
        body, name=name, grid=(nt, nj),
        in_specs=[y_spec, w_spec] + [spec for _, spec in extra],
        out_specs=out_specs, out_shape=out_shapes,
        scratch_shapes=[] if nj == 1 else [pltpu.VMEM((tm, n_out), F32)],
        compiler_params=_cparams(2),
    )(ya, wa, *[arr for arr, _ in extra])


def _norm_bwd(dn, hv, gam):
    r = _rms_scale(hv)
    hh = hv * r
    t1 = dn * gam
    return r * (t1 - hh * jnp.mean(t1 * hh, axis=-1, keepdims=True)), dn * hh


def _norm_bwd_epilogue(has_res, out_dtype):
    def epilogue(dn, ex, outs, i, nt):
        dh, dg_rows = _norm_bwd(dn, ex[0][...], ex[1][...])
        _acc_rows(outs[1], dg_rows, i, nt)
        if has_res:
            dh = dh + ex[2][...]
        outs[0][...] = dh.astype(out_dtype)
        if has_res:
            outs[2][...] = dh.astype(BF16)

    return epilogue


def _mm_nt_norm_bwd(name, ya, y_spec, wa, w_spec, nj, h, gamma, res, out_dtype, mm_fn=None, want_tm=512, after=None):
    t, n = h.shape
    tm = _tile(t, want_tm)
    nt = t // tm
    row = BS((tm, n), lambda i, j: (i, 0))
    extra = [(h, row), (gamma, BS((1, n), lambda i, j: (0, 0)))]
    out_shapes = [SDS((t, n), out_dtype), SDS((8, n), F32)]
    out_specs = [row, BS((8, n), lambda i, j: (0, 0))]
    if res is not None:
        extra.append((res, row))
        out_shapes.append(SDS((t, n), BF16))
        out_specs.append(row)
    extra.extend((a, BS(memory_space=pl.ANY)) for a in after or ())
    return _mm_nt_epi(
        name, ya, y_spec, wa, w_spec, nj, n, extra, out_shapes, out_specs, _norm_bwd_epilogue(res is not None, out_dtype), tm, nt, mm_fn,
    )


def _dev_block(jj):
    return jj // 2 + NJ * (jj % 2)


def _ffn_dn_mm(y_ref, w_ref):
    acc = None
    for jj in range(2 * NJ):
        part = _dot_nt(y_ref[jj], w_ref[_dev_block(jj)])
        acc = part if acc is None else acc + part
    return acc


def _ffn_bwd(tag, dh, dhb, n_in, h_in, gamma, gu, a, w_in, w_out, collective_id, after):
    t, dn = dh.shape
    dgu = _ffn_bwd_in(f"{tag}_bwd_in", dhb, w_out, 0, gu).reshape(2 * NJ, t, FB)
    dw_out = _mm_tn(
        f"{tag}_dw_out", a, BS((None, t, FB), lambda j: (j, 0, 0)), dhb, BS((t, dn), lambda j: (0, 0)),
        (NJ, FB, dn), BS((None, FB, dn), lambda j: (j, 0, 0)), NJ, scale=0.5,
    )
    dw_in = _mm_tn(
        f"{tag}_dw_in", dgu, BS((None, t, FB), lambda j: (j, 0, 0)), n_in, BS((t, dn), lambda j: (0, 0)),
        (NDEV, FB, dn), BS((None, FB, dn), lambda j: (_dev_block(j), 0, 0)), NDEV,
    )
    entries = [("scatter", dw_in), ("scatter", dw_out.reshape(NDEV, NJ * FB // NDEV, dn))]
    landed = _exchange_sc(f"{tag}_reduce", entries, collective_id, after)
    tm = _tile(t)
    resident = BS((None, NDEV, dn, FB), lambda i, j: (0, 0, 0, 0), pipeline_mode=pl.Buffered(1))
    dh_in, dgam, dhb_in = _mm_nt_norm_bwd(
        f"{tag}_dn", dgu, BS((2 * NJ, tm, FB), lambda i, j: (0, i, 0)), w_in, resident, 1, h_in, gamma, dh, F32, mm_fn=_ffn_dn_mm,
        after=[e[1] for e in entries],
    )
    return dh_in, dhb_in, dgam, landed


def _heads_mm(y_ref, w_ref):
    acc = None
    for h in range(HEADS_B):
        part = _dot_nt(y_ref[h], w_ref[h])
        acc = part if acc is None else acc + part
    return acc


def _dqkv_mm(per):
    def mm(y_ref, w_ref):
        acc = None
        for j in range(NDEV):
            cols = [y_ref[(per * j + k) // 8, :, ((per * j + k) % 8) * 128 : ((per * j + k) % 8 + 1) * 128] for k in range(per)]
            part = _dot_nt(jnp.concatenate(cols, axis=1), w_ref[j])
            acc = part if acc is None else acc + part
        return acc

    return mm


def _kv_latent_bwd(dkv, w_up, ckr, latent_norm, dkr, c64, s64, p64, seq):
    t, wd = ckr.shape
    hb = w_up.shape[-1]
    tm = _tile(min(seq, 512), min(seq, 512))
    nt = t // tm
    nseq = seq // tm

    def epilogue(dn, ex, outs, i, nt_):
        dlat, dg_rows = _norm_bwd(dn, ex[0][...], ex[1][...])
        _acc_rows(outs[1], dg_rows, i, nt_)
        outs[0][:, :KV_LORA] = dlat.astype(BF16)
        outs[0][:, KV_LORA:] = _rope_bwd(ex[2][...], ex[3][...], ex[4][...], ex[5][...]).astype(BF16)

    pos = BS((tm, ROPE), lambda i, j: (i % nseq, 0))
    extra = [
        (ckr, BS((tm, KV_LORA), lambda i, j: (i, 0))), (latent_norm, BS((1, KV_LORA), lambda i, j: (0, 0))),
        (dkr, BS((tm, ROPE), lambda i, j: (i, 0))), (c64, pos), (s64, pos), (p64, BS((ROPE, ROPE), lambda i, j: (0, 0))),
    ]
    def heads_mm(y_ref, w_ref):
        acc = None
        for h in range(HEADS_B):
            part = _dot_nt(y_ref[:, h * hb : (h + 1) * hb], w_ref[h])
            acc = part if acc is None else acc + part
        return acc

    return _mm_nt_epi(
        "kv_latent_bwd", dkv, BS((tm, HEADS_B * hb), lambda i, j: (i, 0)), w_up, BS((HEADS_B, KV_LORA, hb), lambda i, j: (0, 0, 0)),
        1, KV_LORA, extra, [SDS((t, wd), BF16), SDS((8, KV_LORA), F32)],
        [BS((tm, wd), lambda i, j: (i, 0)), BS((8, KV_LORA), lambda i, j: (0, 0))], epilogue, tm, nt, heads_mm,
    )


def _adamw(name, parts, w, m, v):
    n_layers, rows, cols = w.shape
    tr = max(d for d in range(8, min(rows, 256) + 1, 8) if rows % d == 0)
    nb = rows // tr

    def body(*refs):
        p_refs = refs[:n_layers]
        w_ref, m_ref, v_ref, g_ref, d_ref, nm_ref, nv_ref = refs[n_layers : n_layers + 7]
        layer = pl.program_id(0)
        for lp in range(n_layers):

            @pl.when(layer == lp)
            def _():
                g = p_refs[lp][0].astype(F32)
                for k in range(1, NDEV):
                    g = g + p_refs[lp][k].astype(F32)
                g_ref[...] = g

        g = g_ref[...]
        nm = ADAM_B1 * m_ref[...] + (1.0 - ADAM_B1) * g
        nv = ADAM_B2 * v_ref[...] + (1.0 - ADAM_B2) * (g * g)
        nm_ref[...] = nm
        nv_ref[...] = nv
        m_hat = nm / (1.0 - ADAM_B1 ** ADAM_STEP)
        v_hat = nv / (1.0 - ADAM_B2 ** ADAM_STEP)
        d_ref[...] = -ADAM_LR * (m_hat / (jnp.sqrt(v_hat) + ADAM_EPS) + ADAM_WD * w_ref[...])

    def part_spec(lp):
        return BS((NDEV, tr, cols), lambda l, i: (0, jnp.where(l == lp, i, jnp.where(l < lp, 0, nb - 1)), 0))

    row = BS((None, tr, cols), lambda l, i: (l, i, 0))
    return pl.pallas_call(
        body, name=name, grid=(n_layers, nb),
        in_specs=[part_spec(lp) for lp in range(n_layers)] + [row, row, row],
        out_specs=[row] * 4, out_shape=[SDS(w.shape, F32)] * 4,
        compiler_params=_cparams(2),
    )(*parts, w, m, v)


def _pack_small(ffn1_norm, mix_norm, ffn2_norm, kv_norm, final_norm, q_norm, latent_norm, rel_bias, last_row):
    dn = ffn1_norm.shape[-1]

    def rows_of(a, n_rows):
        flat = a.reshape(-1)
        return jnp.pad(flat, (0, n_rows * dn - flat.shape[0])).reshape(n_rows, dn)

    return jnp.concatenate(
        [
            ffn1_norm.reshape(2, dn), mix_norm.reshape(2, dn), ffn2_norm.reshape(2, dn), kv_norm.reshape(1, dn),
            final_norm.reshape(1, dn), rows_of(q_norm, 1), rows_of(latent_norm, 1), rows_of(rel_bias, 5), rows_of(last_row, 1),
        ],
        axis=0,
    )


def _unpack_small(pack):
    dn = pack.shape[-1]
    return dict(
        ffn1_norm=pack[0:2], mix_norm=pack[2:4], ffn2_norm=pack[4:6], kv_norm=pack[6], final_norm=pack[7],
        b_q_norm=pack[8, :Q_LORA].reshape(1, Q_LORA), kv_latent_norm=pack[9, :KV_LORA],
        a_rel_bias=pack[10:15].reshape(-1)[: HEADS_A * NREL].reshape(1, HEADS_A, NREL), last=pack[15],
    )


def kernel(x, ffn1_norm, ffn1_w_in, ffn1_w_out, mix_norm, ffn2_norm, ffn2_w_in, ffn2_w_out, a_w_qkv, a_rel_bias, a_w_o, kv_norm, kv_w_down, kv_latent_norm, kv_w_up, b_w_dq, b_q_norm, b_w_uq, b_w_o, final_norm, loss_target, m_ffn1_norm, m_ffn1_w_in, m_ffn1_w_out, m_mix_norm, m_ffn2_norm, m_ffn2_w_in, m_ffn2_w_out, m_a_w_qkv, m_a_rel_bias, m_a_w_o, m_kv_norm, m_kv_w_down, m_kv_latent_norm, m_kv_w_up, m_b_w_dq, m_b_q_norm, m_b_w_uq, m_b_w_o, m_final_norm, v_ffn1_norm, v_ffn1_w_in, v_ffn1_w_out, v_mix_norm, v_ffn2_norm, v_ffn2_w_in, v_ffn2_w_out, v_a_w_qkv, v_a_rel_bias, v_a_w_o, v_kv_norm, v_kv_w_down, v_kv_latent_norm, v_kv_w_up, v_b_w_dq, v_b_q_norm, v_b_w_uq, v_b_w_o, v_final_norm):
    bl, seq, dn = x.shape
    t = bl * seq
    tm = _tile(t)
    nt = t // tm
    x2 = x.reshape(t, dn)
    target2 = loss_target.reshape(t, dn)

    def gathered(*ws):
        return [("gather", w.astype(BF16)) for w in ws]

    groups = [
        gathered(ffn1_w_in[0]), gathered(ffn1_w_out[0]), gathered(a_w_qkv[0], a_w_o[0]), gathered(ffn2_w_in[0], ffn2_w_out[0]),
        gathered(kv_w_down, kv_w_up), gathered(ffn1_w_in[1], ffn1_w_out[1]), gathered(b_w_dq[0], b_w_uq[0], b_w_o[0]),
        gathered(ffn2_w_in[1], ffn2_w_out[1]),
    ]
    ag = [_exchange_sc(f"gather_{k}", group, GATHER_IDS[k]) for k, group in enumerate(groups)]

    def as_w_in(w):
        return w.reshape(1, NDEV, dn, FB)

    def as_w_out(w):
        return w.reshape(1, NJ, FB, dn)

    c64, s64, p64, c192, s192, p192 = _rope_tables(seq)
    q_norm = b_q_norm.reshape(1, Q_LORA)
    latent_norm = kv_latent_norm.reshape(1, KV_LORA)
    bias = _window_bias(_rel_bias_fwd(jnp.pad(a_rel_bias[0], ((0, 0), (0, NREL_PAD - NREL)))))

    h0, h1, h2, n1, hn, n2, gu1, gu2, a1, a2, w_in1, w_in2, w_out1, w_out2 = ([None, None] for _ in range(14))
    h0[0] = x2
    (n1[0],) = _norm_fwd("norm_x", x2, ffn1_norm[0:1])
    w_in1[0] = as_w_in(ag[0][0])
    gu1[0], a1[0] = _ffn_in("ffn1_in_0", n1[0], w_in1[0], 0)
    w_out1[0] = as_w_out(ag[1][0])
    h1[0], hn[0] = _mm_res_norm("ffn1_out_0", a1[0], w_out1[0], 0, h0[0], mix_norm[0:1], 0.5)
    w_qkv, w_o_a = ag[2]
    qkv_wb = w_qkv.shape[-1]
    w_o_a = w_o_a.reshape(1, 1, dn, dn)
    qkv3 = _qkv_proj("qkv_proj", hn[0], w_qkv)
    o_a, lse_a = _attn_a_fwd(qkv3, bias, bl, seq)
    h2[0], n2[0] = _mm_res_norm("attn_a_out", o_a.reshape(1, t, dn), w_o_a, 0, h1[0], ffn2_norm[0:1], 1.0)
    w_in2[0], w_out2[0] = as_w_in(ag[3][0]), as_w_out(ag[3][1])
    gu2[0], a2[0] = _ffn_in("ffn2_in_0", n2[0], w_in2[0], 0)
    h0[1], hk, n1[1] = _mm_res_norm(
        "ffn2_out_0", a2[0], w_out2[0], 0, h2[0], jnp.concatenate([kv_norm.reshape(1, dn), ffn1_norm[1:2]], axis=0), 0.5
    )
    w_down, w_up = ag[4]
    w_down = w_down.reshape(dn, KV_LORA + ROPE)
    ckr, ckv, kr = _kv_down(hk, w_down, latent_norm, c64, s64, p64, seq)
    kv = _kv_up(ckv, w_up)
    w_in1[1], w_out1[1] = as_w_in(ag[5][0]), as_w_out(ag[5][1])
    gu1[1], a1[1] = _ffn_in("ffn1_in_1", n1[1], w_in1[1], 0)
    h1[1], hn[1] = _mm_res_norm("ffn1_out_1", a1[1], w_out1[1], 0, h0[1], mix_norm[1:2], 0.5)
    w_dq, w_uq, w_o_b = ag[6]
    w_dq = w_dq.reshape(dn, Q_LORA)
    w_o_b = w_o_b.reshape(1, 1, dn, dn)
    cq_pre, cq = _q_down(hn[1], w_dq, q_norm)
    q = _q_up(cq, w_uq, c192, s192, p192, seq)
    o_b, lse_b = _mla_fwd(q, kv, kr, bl, seq)
    h2[1], n2[1] = _mm_res_norm("attn_b_out", o_b.reshape(1, t, dn), w_o_b, 0, h1[1], ffn2_norm[1:2], 1.0)
    w_in2[1], w_out2[1] = as_w_in(ag[7][0]), as_w_out(ag[7][1])
    gu2[1], a2[1] = _ffn_in("ffn2_in_1", n2[1], w_in2[1], 0)
    (h_last,) = _mm_res_norm("ffn2_out_1", a2[1], w_out2[1], 0, h2[1], None, 0.5)
    dh, dhb, dg_final, loss_part = _loss_final(h_last, target2, final_norm.reshape(1, dn))

    dg_ffn1, dg_mix, dg_ffn2, rs_ffn1, rs_ffn2 = ([None, None] for _ in range(5))

    def whole(rows, cols):
        return BS((rows, cols), lambda j: (0, 0))

    def dw_rows(name, xa, ya):
        n = ya.shape[1]
        return _mm_tn(name, xa, whole(t, dn), ya, whole(t, n), (dn, n), whole(dn, n), 1).reshape(NDEV, dn // NDEV, n)

    dh, dhb, dg_ffn2[1], rs_ffn2[1] = _ffn_bwd(
        "ffn2_1", dh, dhb, n2[1], h2[1], ffn2_norm[1:2], gu2[1], a2[1], w_in2[1], w_out2[1], REDUCE_IDS[0], ()
    )
    do_b = _mm_nt_plain("attn_b_do", dhb, w_o_b.reshape(dn, dn))
    dw_o_b = dw_rows("attn_b_dwo", o_b, dhb)
    dq_pre, dkv, dkr = _mla_bwd(q, kv, kr, o_b, lse_b, do_b, c192, s192, p192, bl, seq)
    dw_uq = _mm_tn(
        "dw_uq", cq, whole(t, Q_LORA), dq_pre, BS((None, t, QK_B), lambda j: (j, 0, 0)),
        (HEADS_B, Q_LORA, QK_B), BS((None, Q_LORA, QK_B), lambda j: (j, 0, 0)), HEADS_B,
    )
    dcq_pre, dg_q = _mm_nt_norm_bwd(
        "dcq", dq_pre, BS((HEADS_B, tm, QK_B), lambda i, j: (0, i, 0)), w_uq, BS((HEADS_B, Q_LORA, QK_B), lambda i, j: (0, 0, 0)),
        1, cq_pre, q_norm, None, BF16, mm_fn=_heads_mm,
    )
    dw_dq = dw_rows("dw_dq", hn[1], dcq_pre)
    dh, dg_mix[1], dhb = _mm_nt_norm_bwd(
        "dhn_b", dcq_pre, BS((tm, Q_LORA), lambda i, j: (i, 0)), w_dq, BS((dn, Q_LORA), lambda i, j: (0, 0)),
        1, h1[1], mix_norm[1:2], dh, F32,
    )
    dh, dhb, dg_ffn1[1], rs_ffn1[1] = _ffn_bwd(
        "ffn1_1", dh, dhb, n1[1], h0[1], ffn1_norm[1:2], gu1[1], a1[1], w_in1[1], w_out1[1], REDUCE_IDS[1], rs_ffn2[1][:1]
    )
    dw_up = _mm_tn(
        "dw_up", ckv, whole(t, KV_LORA), dkv, BS((t, NOPE + V_DIM), lambda j: (0, j)),
        (HEADS_B, KV_LORA, NOPE + V_DIM), BS((None, KV_LORA, NOPE + V_DIM), lambda j: (j, 0, 0)), HEADS_B,
    )
    dckr, dg_latent = _kv_latent_bwd(dkv, w_up, ckr, latent_norm, dkr, c64, s64, p64, seq)
    dw_down = dw_rows("dw_down", hk, dckr)
    dh, dg_kv, dhb = _mm_nt_norm_bwd(
        "dhk", dckr, BS((tm, KV_LORA + ROPE), lambda i, j: (i, 0)), w_down, BS((dn, KV_LORA + ROPE), lambda i, j: (0, 0)),
        1, h0[1], kv_norm.reshape(1, dn), dh, F32,
    )
    dh, dhb, dg_ffn2[0], rs_ffn2[0] = _ffn_bwd(
        "ffn2_0", dh, dhb, n2[0], h2[0], ffn2_norm[0:1], gu2[0], a2[0], w_in2[0], w_out2[0], REDUCE_IDS[2], rs_ffn1[1][:1]
    )
    do_a = _mm_nt_plain("attn_a_do", dhb, w_o_a.reshape(dn, dn))
    dw_o_a = dw_rows("attn_a_dwo", o_a, dhb)
    dqkv3, dbias = _attn_a_bwd(qkv3, o_a, lse_a, do_a, bias, bl, seq)
    dw_qkv = _dw_qkv(hn[0], dqkv3, qkv_wb)
    mixer_grads = [dw_o_a, dw_qkv, dw_o_b, dw_uq, dw_dq, dw_up, dw_down]
    dh, dg_mix[0], dhb = _mm_nt_norm_bwd(
        "dhn_a", dqkv3, BS((3, tm, dn), lambda i, j: (0, i, 0)), w_qkv, BS((NDEV, dn, qkv_wb), lambda i, j: (0, 0, 0)),
        1, h1[0], mix_norm[0:1], dh, F32, mm_fn=_dqkv_mm(qkv_wb // 128), after=mixer_grads,
    )
    rs_mixers = _exchange_sc("mixers_reduce", [("scatter", g) for g in mixer_grads], REDUCE_IDS[3], rs_ffn2[0][:1])
    dh, dhb, dg_ffn1[0], rs_ffn1[0] = _ffn_bwd(
        "ffn1_0", dh, dhb, n1[0], h0[0], ffn1_norm[0:1], gu1[0], a1[0], w_in1[0], w_out1[0], REDUCE_IDS[4], rs_mixers[:1]
    )
    grad_x = dh.reshape(bl, seq, dn)
    dtable = _rel_bias_bwd(_window_bias_bwd(dbias))[:, :NREL]

    def update(name, parts, w, m, v):
        shape3 = (len(parts),) + w.shape[-2:]
        parts = [p.reshape((NDEV,) + shape3[1:]) for p in parts]
        outs = _adamw(name, parts, w.reshape(shape3), m.reshape(shape3), v.reshape(shape3))
        return [o.reshape(w.shape) for o in outs]

    res = {}
    r_in2_1, r_out2_1 = rs_ffn2[1]
    r_in1_1, r_out1_1 = rs_ffn1[1]
    r_in2_0, r_out2_0 = rs_ffn2[0]
    r_in1_0, r_out1_0 = rs_ffn1[0]
    r_o_a, r_qkv, r_o_b, r_uq, r_dq, r_up, r_down = rs_mixers
    def update_transposed(name, parts, w, m, v):
        outs = update(name, parts, *[jnp.swapaxes(a, 1, 2) for a in (w, m, v)])
        return [jnp.swapaxes(o, 1, 2) for o in outs]

    res["ffn2_w_in"] = update_transposed("adamw_ffn2_w_in", [r_in2_0, r_in2_1], ffn2_w_in, m_ffn2_w_in, v_ffn2_w_in)
    res["ffn2_w_out"] = update("adamw_ffn2_w_out", [r_out2_0, r_out2_1], ffn2_w_out, m_ffn2_w_out, v_ffn2_w_out)
    res["kv_w_down"] = update("adamw_kv_w_down", [r_down], kv_w_down, m_kv_w_down, v_kv_w_down)
    res["kv_w_up"] = update("adamw_kv_w_up", [r_up], kv_w_up, m_kv_w_up, v_kv_w_up)
    res["b_w_dq"] = update("adamw_b_w_dq", [r_dq], b_w_dq, m_b_w_dq, v_b_w_dq)
    res["b_w_uq"] = update("adamw_b_w_uq", [r_uq], b_w_uq, m_b_w_uq, v_b_w_uq)
    res["b_w_o"] = update("adamw_b_w_o", [r_o_b], b_w_o, m_b_w_o, v_b_w_o)
    res["a_w_qkv"] = update("adamw_a_w_qkv", [r_qkv], a_w_qkv, m_a_w_qkv, v_a_w_qkv)
    res["a_w_o"] = update("adamw_a_w_o", [r_o_a], a_w_o, m_a_w_o, v_a_w_o)

    small = _pack_small(
        jnp.stack([dg_ffn1[0][0], dg_ffn1[1][0]]), jnp.stack([dg_mix[0][0], dg_mix[1][0]]), jnp.stack([dg_ffn2[0][0], dg_ffn2[1][0]]),
        dg_kv[0], dg_final[0], dg_q[0], dg_latent[0], dtable, loss_part[0],
    )
    done = [r[1] for name, r in res.items() if name != "ffn2_w_in"]
    (r_small,) = _exchange("gather_small_grads", [("gather", small)], after=done)
    res["ffn1_w_in"] = update_transposed("adamw_ffn1_w_in", [r_in1_0, r_in1_1], ffn1_w_in, m_ffn1_w_in, v_ffn1_w_in)
    res["ffn1_w_out"] = update("adamw_ffn1_w_out", [r_out1_0, r_out1_1], ffn1_w_out, m_ffn1_w_out, v_ffn1_w_out)
    zero_row = jnp.zeros((dn,), F32)
    packs = [
        _pack_small(f1, mx, f2, kvn, fin, qn, lat, rel, zero_row)
        for f1, mx, f2, kvn, fin, qn, lat, rel in (
            (ffn1_norm, mix_norm, ffn2_norm, kv_norm, final_norm, b_q_norm, kv_latent_norm, a_rel_bias),
            (m_ffn1_norm, m_mix_norm, m_ffn2_norm, m_kv_norm, m_final_norm, m_b_q_norm, m_kv_latent_norm, m_a_rel_bias),
            (v_ffn1_norm, v_mix_norm, v_ffn2_norm, v_kv_norm, v_final_norm, v_b_q_norm, v_kv_latent_norm, v_a_rel_bias),
        )
    ]
    small_out = [_unpack_small(o[0]) for o in _adamw("adamw_small", [r_small], *[p[None] for p in packs])]
    for name in ("ffn1_norm", "mix_norm", "ffn2_norm", "a_rel_bias", "kv_norm", "kv_latent_norm", "b_q_norm", "final_norm"):
        res[name] = [so[name] for so in small_out]
    loss = small_out[0]["last"][0]

    order = [
        "ffn1_norm", "ffn1_w_in", "ffn1_w_out", "mix_norm", "ffn2_norm", "ffn2_w_in", "ffn2_w_out", "a_w_qkv", "a_rel_bias",
        "a_w_o", "kv_norm", "kv_w_down", "kv_latent_norm", "kv_w_up", "b_w_dq", "b_q_norm", "b_w_uq", "b_w_o", "final_norm",
    ]
    return (loss, grad_x, *[res[n][0] for n in order], *[res[n][1] for n in order], *[res[n][2] for n in order], *[res[n][3] for n in order])
```

```python
import jax
import jax.numpy as jnp
import numpy as np
from jax import lax
from jax.experimental import pallas as pl
from jax.experimental.pallas import tpu as pltpu
from jax.experimental.pallas import tpu_sc as plsc

NDEV = 8
D_MODEL = 1024
D_FF = 2816
FB = 2 * D_FF // NDEV
NJ = D_FF // FB
CHUNK = 64
LEFT_CHUNKS = 8
PAD = LEFT_CHUNKS * CHUNK
BAND = PAD + CHUNK
CHUNKS_PER_STEP = 4
WINDOW = PAD + CHUNKS_PER_STEP * CHUNK
STEP_ROWS = CHUNKS_PER_STEP * 2 * CHUNK
MAX_REL = 128
NREL = 2 * MAX_REL + 1
NREL_PAD = 384
HEADS_A = 16
HEADS_B = 8
NOPE = 128
ROPE = 64
QK_B = NOPE + ROPE
V_DIM = 128
Q_LORA = 768
KV_LORA = 256
ROPE_THETA = 10000.0
EPS = 1e-6
NEG_INF = -1e30
MLA_TQ = 256
MLA_TK_FWD = 256
MLA_TK_BWD = 1024
ADAM_LR = 0.001
ADAM_B1 = 0.9
ADAM_B2 = 0.999
ADAM_EPS = 1e-08
ADAM_WD = 0.01
ADAM_STEP = 10
PACK_ROWS = 16
GATHER_IDS = tuple(range(1, 9))
REDUCE_IDS = tuple(range(9, 14))
VMEM_LIMIT_BYTES = 56 * 1024 * 1024

F32 = jnp.float32
BF16 = jnp.bfloat16
SDS = jax.ShapeDtypeStruct
BS = pl.BlockSpec
MESH = pl.DeviceIdType.MESH


def _cparams(n_axes):
    return pltpu.CompilerParams(dimension_semantics=("arbitrary",) * n_axes, vmem_limit_bytes=VMEM_LIMIT_BYTES)


def _tile(t, want=512):
    return want if t % want == 0 else t


def _dot(a, b):
    return jnp.dot(a, b, preferred_element_type=F32)


def _dot_nt(a, b):
    return lax.dot_general(a, b, (((1,), (1,)), ((), ())), preferred_element_type=F32)


def _dot_tn(a, b):
    return lax.dot_general(a, b, (((0,), (0,)), ((), ())), preferred_element_type=F32)


def _split3(a):
    hi = a.astype(BF16)
    rest = a - hi.astype(F32)
    mid = rest.astype(BF16)
    return hi, mid, (rest - mid.astype(F32)).astype(BF16)


def _dot_exact(a, onehot, transposed=False):
    ob = onehot.astype(BF16)
    dot = _dot_nt if transposed else _dot
    hi, mid, lo = _split3(a)
    return dot(hi, ob) + dot(mid, ob) + dot(lo, ob)


def _rms_scale(h):
    return lax.rsqrt(jnp.mean(h * h, axis=-1, keepdims=True) + EPS)


def _acc_rows(ref, val, step, n_steps):
    part = val.reshape(val.shape[0] // 8, 8, val.shape[1]).sum(axis=0)

    @pl.when(step == 0)
    def _():
        ref[...] = part

    @pl.when(step > 0)
    def _():
        ref[...] += part

    @pl.when(step == n_steps - 1)
    def _():
        ref[...] = jnp.broadcast_to(jnp.sum(ref[...], axis=0, keepdims=True), ref.shape)


def _exchange_plan(entries):
    ins = [e[1] for e in entries]
    kinds = [e[0] for e in entries]
    lands = [SDS((NDEV,) + a.shape if k == "gather" else a.shape, a.dtype) for k, a in zip(kinds, ins)]
    return ins, lands, kinds


def _mesh_place():
    x, y, c = lax.axis_index("x"), lax.axis_index("y"), lax.axis_index("c")
    return (x, y, c), 4 * x + 2 * y + c


def _flipped(place, p):
    x, y, c = place
    px = 1 - x if p & 4 else x
    py = 1 - y if p & 2 else y
    pc = 1 - c if p & 1 else c
    return (px, py, pc), 4 * px + 2 * py + pc


def _ends(kind, src_ref, land_ref, origin, target):
    if kind == "gather":
        return src_ref, land_ref.at[origin]
    return src_ref.at[target], land_ref.at[origin]


def _remote(kind, src_ref, land_ref, send_sems, recv_sems, k, p, place, me, arriving):
    peer_pos, peer = _flipped(place, p)
    src, dst = _ends(kind, src_ref, land_ref, me, peer)
    if arriving:
        dst = _ends(kind, src_ref, land_ref, peer, me)[1]
    sem = k * (NDEV - 1) + p - 1
    return pltpu.make_async_remote_copy(
        src_ref=src, dst_ref=dst, send_sem=send_sems.at[sem], recv_sem=recv_sems.at[sem], device_id=peer_pos, device_id_type=MESH,
    )


def _exchange(name, entries, after=()):
    ins, lands, kinds = _exchange_plan(entries)
    n = len(ins)
    after = tuple(after)

    def body(*refs):
        refs = refs[:n] + refs[n + len(after) :]
        in_refs, land_refs = refs[:n], refs[n : 2 * n]
        send_sems, recv_sems, local_sems = refs[2 * n :]
        place, me = _mesh_place()
        local = []
        for k in range(n):
            src, dst = _ends(kinds[k], in_refs[k], land_refs[k], me, me)
            local.append(pltpu.make_async_copy(src, dst, local_sems.at[k]))
            local[-1].start()
        sends = []
        for p in range(1, NDEV):
            for k in range(n):
                sends.append(_remote(kinds[k], in_refs[k], land_refs[k], send_sems, recv_sems, k, p, place, me, False))
                sends[-1].start()
        for p in range(1, NDEV):
            for k in range(n):
                _remote(kinds[k], in_refs[k], land_refs[k], send_sems, recv_sems, k, p, place, me, True).wait_recv()
        for cp in sends:
            cp.wait_send()
        for cp in local:
            cp.wait()

    any_spec = BS(memory_space=pl.ANY)
    return pl.pallas_call(
        body, name=name, out_shape=lands, in_specs=[any_spec] * (n + len(after)), out_specs=[any_spec] * n,
        scratch_shapes=[
            pltpu.SemaphoreType.DMA((n * (NDEV - 1),)), pltpu.SemaphoreType.DMA((n * (NDEV - 1),)), pltpu.SemaphoreType.DMA((n,)),
        ],
    )(*ins, *after)


def _exchange_sc(name, entries, collective_id, after=()):
    ins, lands, kinds = _exchange_plan(entries)
    n = len(ins)
    after = tuple(after)

    def launch(*refs):
        refs = refs[:n] + refs[n + len(after) :]
        in_refs, land_refs = refs[:n], refs[n : 2 * n]
        send_sems, recv_sems, local_sems = refs[2 * n :]
        place, me = _mesh_place()
        barrier = pltpu.get_barrier_semaphore()
        for p in range(1, NDEV):
            pl.semaphore_signal(barrier, inc=1, device_id=_flipped(place, p)[0], device_id_type=MESH)
        pl.semaphore_wait(barrier, NDEV - 1)
        local = []
        for k in range(n):
            src, dst = _ends(kinds[k], in_refs[k], land_refs[k], me, me)
            local.append(pltpu.make_async_copy(src, dst, local_sems.at[k]))
            local[-1].start()
        sends = []
        if all(kind == "gather" for kind in kinds):
            for p in (1, 2, 4, 6):
                for k in range(n):
                    sends.append(_remote(kinds[k], in_refs[k], land_refs[k], send_sems, recv_sems, k, p, place, me, False))
                    sends[-1].start()
            sibling_pos, _ = _flipped(place, 1)
            for f in (2, 4, 6):
                _, origin = _flipped(place, f)
                for k in range(n):
                    _remote(kinds[k], in_refs[k], land_refs[k], send_sems, recv_sems, k, f, place, me, True).wait_recv()
                    sem = k * (NDEV - 1) + f
                    sends.append(
                        pltpu.make_async_remote_copy(
                            src_ref=land_refs[k].at[origin], dst_ref=land_refs[k].at[origin], send_sem=send_sems.at[sem],
                            recv_sem=recv_sems.at[sem], device_id=sibling_pos, device_id_type=MESH,
                        )
                    )
                    sends[-1].start()
            for p in (1, 3, 5, 7):
                for k in range(n):
                    _remote(kinds[k], in_refs[k], land_refs[k], send_sems, recv_sems, k, p, place, me, True).wait_recv()
        else:
            for p in range(1, NDEV):
                for k in range(n):
                    sends.append(_remote(kinds[k], in_refs[k], land_refs[k], send_sems, recv_sems, k, p, place, me, False))
                    sends[-1].start()
            for p in range(1, NDEV):
                for k in range(n):
                    _remote(kinds[k], in_refs[k], land_refs[k], send_sems, recv_sems, k, p, place, me, True).wait_recv()
        for cp in sends:
            cp.wait_send()
        for cp in local:
            cp.wait()

    return pl.kernel(
        launch, out_type=tuple(lands), mesh=plsc.ScalarSubcoreMesh(axis_name="sequencer", num_cores=1), name=name,
        scratch_types=(
            pltpu.SemaphoreType.DMA((n * (NDEV - 1),)), pltpu.SemaphoreType.DMA((n * (NDEV - 1),)), pltpu.SemaphoreType.DMA((n,)),
        ),
        compiler_params=pltpu.CompilerParams(collective_id=collective_id),
    )(*ins, *after)


def _norm_fwd(name, h, gammas):
    t, dn = h.shape
    ng = gammas.shape[0]
    tm = _tile(t)

    def body(h_ref, g_ref, *outs):
        hv = h_ref[...]
        hh = hv * _rms_scale(hv)
        for i, o_ref in enumerate(outs):
            o_ref[...] = (hh * g_ref[i : i + 1, :]).astype(BF16)

    row = BS((tm, dn), lambda i: (i, 0))
    return pl.pallas_call(
        body, name=name, grid=(t // tm,),
        in_specs=[row, BS((ng, dn), lambda i: (0, 0))],
        out_specs=[row] * ng, out_shape=[SDS((t, dn), BF16)] * ng,
        compiler_params=_cparams(1),
    )(h, gammas)


def _ffn_in(name, n, w_in, layer):
    t, dn = n.shape
    tm = _tile(t, 1024)

    def body(n_ref, wg_ref, wu_ref, gu_ref, a_ref):
        xv = n_ref[...]
        g = _dot(xv, wg_ref[...])
        u = _dot(xv, wu_ref[...])
        gu_ref[0] = g.astype(BF16)
        gu_ref[1] = u.astype(BF16)
        a_ref[...] = (g * jax.nn.sigmoid(g) * u).astype(BF16)

    return pl.pallas_call(
        body, name=name, grid=(NJ, t // tm),
        in_specs=[
            BS((tm, dn), lambda j, i: (i, 0)),
            BS((None, None, dn, FB), lambda j, i: (layer, j, 0, 0)),
            BS((None, None, dn, FB), lambda j, i: (layer, j + NJ, 0, 0)),
        ],
        out_specs=[BS((None, 2, tm, FB), lambda j, i: (j, 0, i, 0)), BS((None, tm, FB), lambda j, i: (j, i, 0))],
        out_shape=[SDS((NJ, 2, t, FB), BF16), SDS((NJ, t, FB), BF16)],
        compiler_params=_cparams(2),
    )(n, w_in, w_in)


def _mm_res_norm(name, a, w, layer, h_in, gammas, scale):
    nk, t, kb = a.shape
    dn = w.shape[-1]
    ng = 0 if gammas is None else gammas.shape[0]
    tm = _tile(t)

    def body(*refs):
        a_ref, w_ref, h_ref = refs[:3]
        g_ref = refs[3] if ng else None
        outs = refs[3 + (1 if ng else 0) :]
        acc = _dot(a_ref[0], w_ref[0])
        for k in range(1, nk):
            acc += _dot(a_ref[k], w_ref[k])
        ho = h_ref[...] + scale * acc
        outs[0][...] = ho
        if ng:
            hh = ho * _rms_scale(ho)
            for i in range(ng):
                outs[1 + i][...] = (hh * g_ref[i : i + 1, :]).astype(BF16)

    row = BS((tm, dn), lambda i: (i, 0))
    in_specs = [BS((nk, tm, kb), lambda i: (0, i, 0)), BS((None, nk, kb, dn), lambda i: (layer, 0, 0, 0)), row]
    args = [a, w, h_in]
    if ng:
        in_specs.append(BS((ng, dn), lambda i: (0, 0)))
        args.append(gammas)
    return pl.pallas_call(
        body, name=name, grid=(t // tm,),
        in_specs=in_specs,
        out_specs=[row] * (1 + ng), out_shape=[SDS((t, dn), F32)] + [SDS((t, dn), BF16)] * ng,
        compiler_params=_cparams(1),
    )(*args)


def _qkv_proj(name, hn, w_qkv):
    t, dn = hn.shape
    wb = w_qkv.shape[-1]
    per = wb // 128
    tm = _tile(t)

    def body(x_ref, w_ref, o_ref):
        xv = x_ref[...]
        for j in range(NDEV):
            yv = _dot(xv, w_ref[j]).astype(BF16)
            for i in range(per):
                n = per * j + i
                o_ref[n // 8, :, (n % 8) * 128 : (n % 8 + 1) * 128] = yv[:, i * 128 : (i + 1) * 128]

    return pl.pallas_call(
        body, name=name, grid=(t // tm,),
        in_specs=[BS((tm, dn), lambda i: (i, 0)), BS((NDEV, dn, wb), lambda i: (0, 0, 0))],
        out_specs=BS((3, tm, dn), lambda i: (0, i, 0)), out_shape=SDS((3, t, dn), BF16),
        compiler_params=_cparams(1),
    )(hn, w_qkv)


def _rel_onehot(i):
    r = lax.broadcasted_iota(jnp.int32, (NREL_PAD, BAND), 0)
    j = lax.broadcasted_iota(jnp.int32, (NREL_PAD, BAND), 1)
    idx = jnp.clip(PAD + i - j, -MAX_REL, MAX_REL) + MAX_REL
    return (idx == r).astype(F32)


def _rel_bias_fwd(table):
    def body(t_ref, o_ref):
        i8 = pl.program_id(0)
        for ii in range(8):
            o_ref[:, ii, :] = _dot_exact(t_ref[...], _rel_onehot(i8 * 8 + ii))

    return pl.pallas_call(
        body, name="rel_bias_fwd", grid=(CHUNK // 8,),
        in_specs=[BS((HEADS_A, NREL_PAD), lambda i: (0, 0))],
        out_specs=BS((HEADS_A, 8, BAND), lambda i: (0, i, 0)), out_shape=SDS((HEADS_A, CHUNK, BAND), F32),
        compiler_params=_cparams(1),
    )(table)


def _rel_bias_bwd(dbias):
    def body(d_ref, o_ref):
        i8 = pl.program_id(0)
        acc = jnp.zeros((HEADS_A, NREL_PAD), F32)
        for ii in range(8):
            acc += _dot_exact(d_ref[:, ii, :], _rel_onehot(i8 * 8 + ii), transposed=True)

        @pl.when(i8 == 0)
        def _():
            o_ref[...] = acc

        @pl.when(i8 > 0)
        def _():
            o_ref[...] += acc

    return pl.pallas_call(
        body, name="rel_bias_bwd", grid=(CHUNK // 8,),
        in_specs=[BS((HEADS_A, 8, BAND), lambda i: (0, i, 0))],
        out_specs=BS((HEADS_A, NREL_PAD), lambda i: (0, 0)), out_shape=SDS((HEADS_A, NREL_PAD), F32),
        compiler_params=_cparams(1),
    )(dbias)


def _window_bias(bias):
    b = bias.reshape(HEADS_A // 2, 2, CHUNK, BAND)
    per_chunk = [
        jnp.pad(b, ((0, 0), (0, 0), (0, 0), (cc * CHUNK, WINDOW - BAND - cc * CHUNK)), constant_values=NEG_INF)
        for cc in range(CHUNKS_PER_STEP)
    ]
    return jnp.stack(per_chunk, axis=1).reshape(HEADS_A // 2, STEP_ROWS, WINDOW)


def _window_bias_bwd(dwin):
    d = dwin.reshape(HEADS_A // 2, CHUNKS_PER_STEP, 2, CHUNK, WINDOW)
    return sum(d[:, cc, :, :, cc * CHUNK : cc * CHUNK + BAND] for cc in range(CHUNKS_PER_STEP)).reshape(HEADS_A, CHUNK, BAND)


def _step_rows(xs, lane):
    parts = []
    for cc in range(CHUNKS_PER_STEP):
        xc = xs[cc * CHUNK : (cc + 1) * CHUNK]
        parts.append(jnp.where(lane < 64, xc, jnp.zeros_like(xc)))
        parts.append(jnp.where(lane >= 64, xc, jnp.zeros_like(xc)))
    return jnp.concatenate(parts, axis=0)


def _pair_rows(ys, lane):
    parts = []
    for cc in range(CHUNKS_PER_STEP):
        y0 = ys[(2 * cc) * CHUNK : (2 * cc + 1) * CHUNK]
        y1 = ys[(2 * cc + 1) * CHUNK : (2 * cc + 2) * CHUNK]
        parts.append(jnp.where(lane < 64, y0, y1))
    return jnp.concatenate(parts, axis=0)


def _window_scores(q_rows, kwin, bias_win, first_key):
    s = _dot_nt(q_rows, kwin) * (CHUNK ** -0.5) + bias_win
    if first_key is None:
        return s
    col = lax.broadcasted_iota(jnp.int32, s.shape, 1)
    return jnp.where(col >= first_key, s, NEG_INF)


def _window_loop(n_passes, chunks):
    n_padded = min(PAD // (CHUNKS_PER_STEP * CHUNK), n_passes)
    lax.fori_loop(0, n_padded, lambda it, carry: chunks(it, carry, True), 0, unroll=2)
    if n_passes > n_padded:
        lax.fori_loop(n_padded, n_passes, lambda it, carry: chunks(it, carry, False), 0, unroll=2)


def _attn_a_fwd(qkv3, bias_win, bl, seq):
    t, dn = qkv3.shape[1:]
    npair = dn // 128
    step = CHUNKS_PER_STEP * CHUNK

    def body(q_ref, k_ref, v_ref, b_ref, o_ref, lse_ref, kpad, vpad):
        kpad[0:PAD, :] = jnp.zeros((PAD, 128), BF16)
        vpad[0:PAD, :] = jnp.zeros((PAD, 128), BF16)
        kpad[PAD:, :] = k_ref[...]
        vpad[PAD:, :] = v_ref[...]
        lane = lax.broadcasted_iota(jnp.int32, (CHUNK, 128), 1)

        def chunks(it, carry, padded):
            r0 = pl.multiple_of(it * step, step)
            q_rows = _step_rows(q_ref[pl.ds(r0, step), :], lane)
            s = _window_scores(q_rows, kpad[pl.ds(r0, WINDOW), :], b_ref[...], PAD - r0 if padded else None)
            m = jnp.max(s, axis=-1, keepdims=True)
            e = jnp.exp(s - m)
            total = jnp.sum(e, axis=-1, keepdims=True)
            o_rows = _dot((e * (1.0 / total)).astype(BF16), vpad[pl.ds(r0, WINDOW), :])
            o_ref[pl.ds(r0, step), :] = _pair_rows(o_rows, lane).astype(BF16)
            lse_ref[pl.ds(pl.multiple_of(it * STEP_ROWS, STEP_ROWS), STEP_ROWS), :] = m + jnp.log(total)
            return carry

        _window_loop(seq // step, chunks)

    return pl.pallas_call(
        body, name="attn_a_fwd", grid=(bl, npair),
        in_specs=[
            BS((None, seq, 128), lambda b, h: (0, b, h)),
            BS((None, seq, 128), lambda b, h: (1, b, h)),
            BS((None, seq, 128), lambda b, h: (2, b, h)),
            BS((None, STEP_ROWS, WINDOW), lambda b, h: (h, 0, 0)),
        ],
        out_specs=[BS((seq, 128), lambda b, h: (b, h)), BS((None, 2 * seq, 1), lambda b, h: (h, b, 0))],
        out_shape=[SDS((t, dn), BF16), SDS((npair, 2 * t, 1), F32)],
        scratch_shapes=[pltpu.VMEM((PAD + seq, 128), BF16), pltpu.VMEM((PAD + seq, 128), BF16)],
        compiler_params=_cparams(2),
    )(qkv3, qkv3, qkv3, bias_win)


def _attn_a_bwd(qkv3, out, lse, do, bias_win, bl, seq):
    t, dn = qkv3.shape[1:]
    npair = dn // 128
    step = CHUNKS_PER_STEP * CHUNK

    def body(q_ref, k_ref, v_ref, o_ref, lse_ref, do_ref, b_ref, dqkv_ref, db_ref, kpad, vpad, dkacc, dvacc):
        b = pl.program_id(1)
        kpad[0:PAD, :] = jnp.zeros((PAD, 128), BF16)
        vpad[0:PAD, :] = jnp.zeros((PAD, 128), BF16)
        kpad[PAD:, :] = k_ref[...]
        vpad[PAD:, :] = v_ref[...]
        dkacc[...] = jnp.zeros_like(dkacc)
        dvacc[...] = jnp.zeros_like(dvacc)

        @pl.when(b == 0)
        def _():
            db_ref[...] = jnp.zeros_like(db_ref)

        lane = lax.broadcasted_iota(jnp.int32, (CHUNK, 128), 1)

        def chunks(it, carry, padded):
            r0 = pl.multiple_of(it * step, step)
            q_rows = _step_rows(q_ref[pl.ds(r0, step), :], lane)
            do_rows = _step_rows(do_ref[pl.ds(r0, step), :], lane)
            kwin = kpad[pl.ds(r0, WINDOW), :]
            vwin = vpad[pl.ds(r0, WINDOW), :]
            o_rows = _step_rows(o_ref[pl.ds(r0, step), :], lane)
            delta = jnp.sum(do_rows.astype(F32) * o_rows.astype(F32), axis=-1, keepdims=True)
            lse_rows = lse_ref[pl.ds(pl.multiple_of(it * STEP_ROWS, STEP_ROWS), STEP_ROWS), :]
            p = jnp.exp(_window_scores(q_rows, kwin, b_ref[...], PAD - r0 if padded else None) - lse_rows)
            ds = p * (_dot_nt(do_rows, vwin) - delta)
            db_ref[...] += ds
            dsb = (ds * (CHUNK ** -0.5)).astype(BF16)
            dqkv_ref[0, pl.ds(r0, step), :] = _pair_rows(_dot(dsb, kwin), lane).astype(BF16)
            dkacc[pl.ds(r0, WINDOW), :] += _dot_tn(dsb, q_rows)
            dvacc[pl.ds(r0, WINDOW), :] += _dot_tn(p.astype(BF16), do_rows)
            return carry

        _window_loop(seq // step, chunks)
        dqkv_ref[1] = dkacc[PAD:, :].astype(BF16)
        dqkv_ref[2] = dvacc[PAD:, :].astype(BF16)

    return pl.pallas_call(
        body, name="attn_a_bwd", grid=(npair, bl),
        in_specs=[
            BS((None, seq, 128), lambda h, b: (0, b, h)),
            BS((None, seq, 128), lambda h, b: (1, b, h)),
            BS((None, seq, 128), lambda h, b: (2, b, h)),
            BS((seq, 128), lambda h, b: (b, h)),
            BS((None, 2 * seq, 1), lambda h, b: (h, b, 0)),
            BS((seq, 128), lambda h, b: (b, h)),
            BS((None, STEP_ROWS, WINDOW), lambda h, b: (h, 0, 0)),
        ],
        out_specs=[BS((3, seq, 128), lambda h, b: (0, b, h)), BS((None, STEP_ROWS, WINDOW), lambda h, b: (h, 0, 0))],
        out_shape=[SDS((3, t, dn), BF16), SDS((HEADS_A // 2, STEP_ROWS, WINDOW), F32)],
        scratch_shapes=[
            pltpu.VMEM((PAD + seq, 128), BF16), pltpu.VMEM((PAD + seq, 128), BF16),
            pltpu.VMEM((PAD + seq, 128), F32), pltpu.VMEM((PAD + seq, 128), F32),
        ],
        compiler_params=_cparams(2),
    )(qkv3, qkv3, qkv3, out, lse, do, bias_win)


def _rope_tables(seq):
    half = ROPE // 2
    freqs = ROPE_THETA ** (-jnp.arange(half, dtype=F32) / half)
    ang = jnp.arange(seq, dtype=F32)[:, None] * freqs[None, :]
    cos, sin = jnp.cos(ang), jnp.sin(ang)
    c64 = jnp.concatenate([cos, cos], axis=1)
    s64 = jnp.concatenate([-sin, sin], axis=1)
    c192 = jnp.concatenate([jnp.ones((seq, NOPE), F32), c64], axis=1)
    s192 = jnp.concatenate([jnp.zeros((seq, NOPE), F32), s64], axis=1)
    p64 = np.zeros((ROPE, ROPE), np.float32)
    for col in range(ROPE):
        p64[(col + half) % ROPE, col] = 1.0
    p192 = np.zeros((QK_B, QK_B), np.float32)
    p192[NOPE:, NOPE:] = p64
    return c64, s64, jnp.asarray(p64), c192, s192, jnp.asarray(p192)


def _rope(xv, cos, sin_signed, swap):
    return xv * cos + _dot_exact(xv, swap) * sin_signed


def _rope_bwd(dy, cos, sin_signed, swap):
    return dy * cos + _dot_exact(dy * sin_signed, swap)


def _q_down(hn, w_dq, q_norm):
    t, dn = hn.shape
    ql = w_dq.shape[1]
    tm = _tile(t)

    def body(x_ref, w_ref, g_ref, pre_ref, cq_ref):
        pre = _dot(x_ref[...], w_ref[...])
        pre_ref[...] = pre
        cq_ref[...] = (pre * _rms_scale(pre) * g_ref[...]).astype(BF16)

    return pl.pallas_call(
        body, name="q_down", grid=(t // tm,),
        in_specs=[BS((tm, dn), lambda i: (i, 0)), BS((dn, ql), lambda i: (0, 0)), BS((1, ql), lambda i: (0, 0))],
        out_specs=[BS((tm, ql), lambda i: (i, 0))] * 2, out_shape=[SDS((t, ql), F32), SDS((t, ql), BF16)],
        compiler_params=_cparams(1),
    )(hn, w_dq, q_norm)


def _q_up(cq, w_uq, c192, s192, p192, seq):
    t, ql = cq.shape
    tm = _tile(min(seq, 512), min(seq, 512))
    nseq = seq // tm

    def body(x_ref, w_ref, c_ref, s_ref, p_ref, o_ref):
        xv = x_ref[...]
        for h in range(HEADS_B):
            o_ref[h] = _rope(_dot(xv, w_ref[h]), c_ref[...], s_ref[...], p_ref[...]).astype(BF16)

    pos = BS((tm, QK_B), lambda i: (i % nseq, 0))
    return pl.pallas_call(
        body, name="q_up", grid=(t // tm,),
        in_specs=[
            BS((tm, ql), lambda i: (i, 0)), BS((HEADS_B, ql, QK_B), lambda i: (0, 0, 0)), pos, pos,
            BS((QK_B, QK_B), lambda i: (0, 0)),
        ],
        out_specs=BS((HEADS_B, tm, QK_B), lambda i: (0, i, 0)), out_shape=SDS((HEADS_B, t, QK_B), BF16),
        compiler_params=_cparams(1),
    )(cq, w_uq, c192, s192, p192)


def _kv_down(hk, w_down, latent_norm, c64, s64, p64, seq):
    t, dn = hk.shape
    wd = w_down.shape[1]
    tm = _tile(min(seq, 512), min(seq, 512))
    nseq = seq // tm

    def body(x_ref, w_ref, g_ref, c_ref, s_ref, p_ref, ckr_ref, ckv_ref, kr_ref):
        ckr = _dot(x_ref[...], w_ref[...])
        ckr_ref[...] = ckr
        lat = ckr[:, :KV_LORA]
        ckv_ref[...] = (lat * _rms_scale(lat) * g_ref[...]).astype(BF16)
        kr_ref[...] = _rope(ckr[:, KV_LORA:], c_ref[...], s_ref[...], p_ref[...]).astype(BF16)

    pos = BS((tm, ROPE), lambda i: (i % nseq, 0))
    return pl.pallas_call(
        body, name="kv_down", grid=(t // tm,),
        in_specs=[
            BS((tm, dn), lambda i: (i, 0)), BS((dn, wd), lambda i: (0, 0)), BS((1, KV_LORA), lambda i: (0, 0)), pos, pos,
            BS((ROPE, ROPE), lambda i: (0, 0)),
        ],
        out_specs=[BS((tm, wd), lambda i: (i, 0)), BS((tm, KV_LORA), lambda i: (i, 0)), BS((tm, ROPE), lambda i: (i, 0))],
        out_shape=[SDS((t, wd), F32), SDS((t, KV_LORA), BF16), SDS((t, ROPE), BF16)],
        compiler_params=_cparams(1),
    )(hk, w_down, latent_norm, c64, s64, p64)


def _kv_up(ckv, w_up):
    t, kl = ckv.shape
    hb = w_up.shape[-1]
    tm = _tile(t)

    def body(x_ref, w_ref, o_ref):
        xv = x_ref[...]
        for h in range(HEADS_B):
            o_ref[:, h * hb : (h + 1) * hb] = _dot(xv, w_ref[h]).astype(BF16)

    return pl.pallas_call(
        body, name="kv_up", grid=(t // tm,),
        in_specs=[BS((tm, kl), lambda i: (i, 0)), BS((HEADS_B, kl, hb), lambda i: (0, 0, 0))],
        out_specs=BS((tm, HEADS_B * hb), lambda i: (i, 0)), out_shape=SDS((t, HEADS_B * hb), BF16),
        compiler_params=_cparams(1),
    )(ckv, w_up)


def _mla_diagonal_mask(tq):
    rows = lax.broadcasted_iota(jnp.int32, (tq, tq), 0)
    cols = lax.broadcasted_iota(jnp.int32, (tq, tq), 1)
    return jnp.where(jnp.right_shift(cols, 6) <= jnp.right_shift(rows, 6), 0.0, NEG_INF)


def _mla_key_tiles(n_keys, tk):
    return [(slice(k0, min(k0 + tk, n_keys)), min(k0 + tk, n_keys) == n_keys) for k0 in range(0, n_keys, tk)]


def _mla_scores(qi, kt, diagonal):
    s = _dot_nt(qi, kt) * (QK_B ** -0.5)
    if diagonal is None:
        return s
    tq, width = s.shape
    own = s[:, width - tq :] + diagonal
    return own if width == tq else jnp.concatenate([s[:, : width - tq], own], axis=1)


def _mla_fwd(q, kv, kr, bl, seq):
    t = kv.shape[0]
    tq = min(MLA_TQ, seq)

    def body(q_ref, kn_ref, v_ref, kr_ref, o_ref, lse_ref):
        kcat = jnp.concatenate([kn_ref[...], kr_ref[...]], axis=1)
        vv = v_ref[...]
        diagonal = _mla_diagonal_mask(tq)
        for i in range(seq // tq):
            rows = slice(i * tq, (i + 1) * tq)
            qi = q_ref[rows, :]
            m = total = acc = None
            for keys, own in _mla_key_tiles((i + 1) * tq, MLA_TK_FWD):
                s = _mla_scores(qi, kcat[keys], diagonal if own else None)
                m_blk = jnp.max(s, axis=-1, keepdims=True)
                if m is None:
                    m_new = m_blk
                    e = jnp.exp(s - m_new)
                    total = jnp.sum(e, axis=-1, keepdims=True)
                    acc = _dot(e.astype(BF16), vv[keys])
                else:
                    m_new = jnp.maximum(m, m_blk)
                    keep = jnp.exp(m - m_new)
                    e = jnp.exp(s - m_new)
                    total = keep * total + jnp.sum(e, axis=-1, keepdims=True)
                    acc = keep * acc + _dot(e.astype(BF16), vv[keys])
                m = m_new
            o_ref[rows, :] = (acc / total).astype(BF16)
            lse_ref[rows, :] = m + jnp.log(total)

    return pl.pallas_call(
        body, name="mla_fwd", grid=(bl, HEADS_B),
        in_specs=[
            BS((None, seq, QK_B), lambda b, h: (h, b, 0)),
            BS((seq, NOPE), lambda b, h: (b, 2 * h)),
            BS((seq, V_DIM), lambda b, h: (b, 2 * h + 1)),
            BS((seq, ROPE), lambda b, h: (b, 0)),
        ],
        out_specs=[BS((seq, V_DIM), lambda b, h: (b, h)), BS((None, seq, 1), lambda b, h: (h, b, 0))],
        out_shape=[SDS((t, HEADS_B * V_DIM), BF16), SDS((HEADS_B, t, 1), F32)],
        compiler_params=_cparams(2),
    )(q, kv, kv, kr)


def _mla_bwd(q, kv, kr, o, lse, do, c192, s192, p192, bl, seq):
    t = kv.shape[0]
    tq = min(MLA_TQ, seq)

    def body(q_ref, kn_ref, v_ref, kr_ref, o_ref, lse_ref, do_ref, c_ref, s_ref, p_ref, dq_ref, dkv_ref, dkr_ref, dkacc, dvacc):
        h = pl.program_id(1)
        kcat = jnp.concatenate([kn_ref[...], kr_ref[...]], axis=1)
        vv = v_ref[...]
        dkacc[...] = jnp.zeros_like(dkacc)
        dvacc[...] = jnp.zeros_like(dvacc)
        diagonal = _mla_diagonal_mask(tq)
        for i in range(seq // tq):
            rows = slice(i * tq, (i + 1) * tq)
            qi = q_ref[rows, :]
            doi = do_ref[rows, :]
            lse_i = lse_ref[rows, :]
            delta = jnp.sum(doi.astype(F32) * o_ref[rows, :].astype(F32), axis=-1, keepdims=True)
            dq = None
            for keys, own in _mla_key_tiles((i + 1) * tq, MLA_TK_BWD):
                p = jnp.exp(_mla_scores(qi, kcat[keys], diagonal if own else None) - lse_i)
                ds = p * (_dot_nt(doi, vv[keys]) - delta)
                dsb = (ds * (QK_B ** -0.5)).astype(BF16)
                dq_blk = _dot(dsb, kcat[keys])
                dq = dq_blk if dq is None else dq + dq_blk
                dkacc[keys, :] += _dot_tn(dsb, qi)
                dvacc[keys, :] += _dot_tn(p.astype(BF16), doi)
            dq_ref[rows, :] = _rope_bwd(dq, c_ref[rows, :], s_ref[rows, :], p_ref[...]).astype(BF16)
        dk = dkacc[...]
        dkv_ref[:, :NOPE] = dk[:, :NOPE].astype(BF16)
        dkv_ref[:, NOPE:] = dvacc[...].astype(BF16)

        @pl.when(h == 0)
        def _():
            dkr_ref[...] = dk[:, NOPE:]

        @pl.when(h > 0)
        def _():
            dkr_ref[...] += dk[:, NOPE:]

    return pl.pallas_call(
        body, name="mla_bwd", grid=(bl, HEADS_B),
        in_specs=[
            BS((None, seq, QK_B), lambda b, h: (h, b, 0)),
            BS((seq, NOPE), lambda b, h: (b, 2 * h)),
            BS((seq, V_DIM), lambda b, h: (b, 2 * h + 1)),
            BS((seq, ROPE), lambda b, h: (b, 0)),
            BS((seq, V_DIM), lambda b, h: (b, h)),
            BS((None, seq, 1), lambda b, h: (h, b, 0)),
            BS((seq, V_DIM), lambda b, h: (b, h)),
            BS((seq, QK_B), lambda b, h: (0, 0)),
            BS((seq, QK_B), lambda b, h: (0, 0)),
            BS((QK_B, QK_B), lambda b, h: (0, 0)),
        ],
        out_specs=[
            BS((None, seq, QK_B), lambda b, h: (h, b, 0)),
            BS((seq, NOPE + V_DIM), lambda b, h: (b, h)),
            BS((seq, ROPE), lambda b, h: (b, 0)),
        ],
        out_shape=[SDS((HEADS_B, t, QK_B), BF16), SDS((t, HEADS_B * (NOPE + V_DIM)), BF16), SDS((t, ROPE), F32)],
        scratch_shapes=[pltpu.VMEM((seq, QK_B), F32), pltpu.VMEM((seq, V_DIM), F32)],
        compiler_params=_cparams(2),
    )(q, kv, kv, kr, o, lse, do, c192, s192, p192)


def _loss_final(h, target, gamma):
    t, dn = h.shape
    tm = _tile(t)
    nt = t // tm

    def body(h_ref, t_ref, g_ref, dh_ref, dhb_ref, dg_ref, loss_ref):
        i = pl.program_id(0)
        hv = h_ref[...]
        r = _rms_scale(hv)
        hh = hv * r
        gam = g_ref[...]
        err = hh * gam - t_ref[...]
        part = 0.5 * jnp.sum(jnp.mean(err * err, axis=-1, keepdims=True))

        @pl.when(i == 0)
        def _():
            loss_ref[...] = jnp.zeros_like(loss_ref)

        loss_ref[...] += part
        dy = err * (1.0 / dn)
        _acc_rows(dg_ref, dy * hh, i, nt)
        t1 = dy * gam
        dh = r * (t1 - hh * jnp.mean(t1 * hh, axis=-1, keepdims=True))
        dh_ref[...] = dh
        dhb_ref[...] = dh.astype(BF16)

    row = BS((tm, dn), lambda i: (i, 0))
    return pl.pallas_call(
        body, name="loss_final", grid=(nt,),
        in_specs=[row, row, BS((1, dn), lambda i: (0, 0))],
        out_specs=[row, row, BS((8, dn), lambda i: (0, 0)), BS((8, 128), lambda i: (0, 0))],
        out_shape=[SDS((t, dn), F32), SDS((t, dn), BF16), SDS((8, dn), F32), SDS((8, 128), F32)],
        compiler_params=_cparams(1),
    )(h, target, gamma)


def _ffn_bwd_in(name, dh, w_out, layer, gu):
    t, dn = dh.shape
    tm = _tile(t, 1024)

    def body(dh_ref, w_ref, gu_ref, o_ref):
        da = 0.5 * _dot_nt(dh_ref[...], w_ref[...])
        g = gu_ref[0].astype(F32)
        u = gu_ref[1].astype(F32)
        sg = jax.nn.sigmoid(g)
        o_ref[0] = (da * u * (sg * (1.0 + g * (1.0 - sg)))).astype(BF16)
        o_ref[1] = (da * (g * sg)).astype(BF16)

    blk = BS((None, 2, tm, FB), lambda j, i: (j, 0, i, 0))
    return pl.pallas_call(
        body, name=name, grid=(NJ, t // tm),
        in_specs=[BS((tm, dn), lambda j, i: (i, 0)), BS((None, None, FB, dn), lambda j, i: (layer, j, 0, 0)), blk],
        out_specs=blk, out_shape=SDS((NJ, 2, t, FB), BF16),
        compiler_params=_cparams(2),
    )(dh, w_out, gu)


def _mm_nt_plain(name, xf, w):
    t, dn = xf.shape
    n = w.shape[0]
    tm = _tile(t)

    def body(x_ref, w_ref, o_ref):
        o_ref[...] = _dot_nt(x_ref[...], w_ref[...]).astype(BF16)

    return pl.pallas_call(
        body, name=name, grid=(t // tm,),
        in_specs=[BS((tm, dn), lambda i: (i, 0)), BS((n, dn), lambda i: (0, 0))],
        out_specs=BS((tm, n), lambda i: (i, 0)), out_shape=SDS((t, n), BF16),
        compiler_params=_cparams(1),
    )(xf, w)


def _mm_tn(name, xa, x_spec, ya, y_spec, out_shape, out_spec, nj, scale=None):
    def body(x_ref, y_ref, o_ref):
        acc = _dot_tn(x_ref[...], y_ref[...])
        o_ref[...] = (acc if scale is None else scale * acc).astype(BF16)

    return pl.pallas_call(
        body, name=name, grid=(nj,),
        in_specs=[x_spec, y_spec], out_specs=out_spec, out_shape=SDS(out_shape, BF16),
        compiler_params=_cparams(1),
    )(xa, ya)


def _dw_qkv(hn, dqkv3, wb):
    t, dn = hn.shape
    per = wb // 128

    def body(x_ref, *refs):
        cols = [y_ref[...] for y_ref in refs[:per]]
        refs[per][...] = _dot_tn(x_ref[...], jnp.concatenate(cols, axis=1)).astype(BF16)

    def piece(k):
        return BS((None, t, 128), lambda j: ((per * j + k) // 8, 0, (per * j + k) % 8))

    return pl.pallas_call(
        body, name="dw_qkv", grid=(NDEV,),
        in_specs=[BS((t, dn), lambda j: (0, 0))] + [piece(k) for k in range(per)],
        out_specs=BS((None, dn, wb), lambda j: (j, 0, 0)), out_shape=SDS((NDEV, dn, wb), BF16),
        compiler_params=_cparams(1),
    )(hn, *([dqkv3] * per))


def _mm_nt_epi(name, ya, y_spec, wa, w_spec, nj, n_out, extra, out_shapes, out_specs, epilogue, tm, nt, mm_fn=None):
    n_extra = len(extra)
    n_outs = len(out_shapes)

    def body(*refs):
        y_ref, w_ref = refs[:2]
        ex = refs[2 : 2 + n_extra]
        outs = refs[2 + n_extra : 2 + n_extra + n_outs]
        i = pl.program_id(0)
        j = pl.program_id(1)
        part = _dot_nt(y_ref[...], w_ref[...]) if mm_fn is None else mm_fn(y_ref, w_ref)
        if nj == 1:
            epilogue(part, ex, outs, i, nt)
            return
        acc = refs[-1]

        @pl.when(j == 0)
        def _():
            acc[...] = part

        @pl.when(j > 0)
        def _():
            acc[...] += part

        @pl.when(j == nj - 1)
        def _():
            epilogue(acc[...], ex, outs, i, nt)

    return pl.pallas_call(
        body, name=name, grid=(nt, nj),
        in_specs=[y_spec, w_spec] + [spec for _, spec in extra],
        out_specs=out_specs, out_shape=out_shapes,
        scratch_shapes=[] if nj == 1 else [pltpu.VMEM((tm, n_out), F32)],
        compiler_params=_cparams(2),
    )(ya, wa, *[arr for arr, _ in extra])


def _norm_bwd(dn, hv, gam):
    r = _rms_scale(hv)
    hh = hv * r
    t1 = dn * gam
    return r * (t1 - hh * jnp.mean(t1 * hh, axis=-1, keepdims=True)), dn * hh


def _norm_bwd_epilogue(has_res, out_dtype):
    def epilogue(dn, ex, outs, i, nt):
        dh, dg_rows = _norm_bwd(dn, ex[0][...], ex[1][...])
        _acc_rows(outs[1], dg_rows, i, nt)
        if has_res:
            dh = dh + ex[2][...]
        outs[0][...] = dh.astype(out_dtype)
        if has_res:
            outs[2][...] = dh.astype(BF16)

    return epilogue


def _mm_nt_norm_bwd(name, ya, y_spec, wa, w_spec, nj, h, gamma, res, out_dtype, mm_fn=None, want_tm=512, after=None):
    t, n = h.shape
    tm = _tile(t, want_tm)
    nt = t // tm
    row = BS((tm, n), lambda i, j: (i, 0))
    extra = [(h, row), (gamma, BS((1, n), lambda i, j: (0, 0)))]
    out_shapes = [SDS((t, n), out_dtype), SDS((8, n), F32)]
    out_specs = [row, BS((8, n), lambda i, j: (0, 0))]
    if res is not None:
        extra.append((res, row))
        out_shapes.append(SDS((t, n), BF16))
        out_specs.append(row)
    extra.extend((a, BS(memory_space=pl.ANY)) for a in after or ())
    return _mm_nt_epi(
        name, ya, y_spec, wa, w_spec, nj, n, extra, out_shapes, out_specs, _norm_bwd_epilogue(res is not None, out_dtype), tm, nt, mm_fn,
    )


def _dev_block(jj):
    return jj // 2 + NJ * (jj % 2)


def _ffn_dn_mm(y_ref, w_ref):
    acc = None
    for jj in range(2 * NJ):
        part = _dot_nt(y_ref[jj], w_ref[_dev_block(jj)])
        acc = part if acc is None else acc + part
    return acc


def _ffn_bwd(tag, dh, dhb, n_in, h_in, gamma, gu, a, w_in, w_out, collective_id, after):
    t, dn = dh.shape
    dgu = _ffn_bwd_in(f"{tag}_bwd_in", dhb, w_out, 0, gu).reshape(2 * NJ, t, FB)
    dw_out = _mm_tn(
        f"{tag}_dw_out", a, BS((None, t, FB), lambda j: (j, 0, 0)), dhb, BS((t, dn), lambda j: (0, 0)),
        (NJ, FB, dn), BS((None, FB, dn), lambda j: (j, 0, 0)), NJ, scale=0.5,
    )
    dw_in = _mm_tn(
        f"{tag}_dw_in", dgu, BS((None, t, FB), lambda j: (j, 0, 0)), n_in, BS((t, dn), lambda j: (0, 0)),
        (NDEV, FB, dn), BS((None, FB, dn), lambda j: (_dev_block(j), 0, 0)), NDEV,
    )
    entries = [("scatter", dw_in), ("scatter", dw_out.reshape(NDEV, NJ * FB // NDEV, dn))]
    landed = _exchange_sc(f"{tag}_reduce", entries, collective_id, after)
    tm = _tile(t)
    resident = BS((None, NDEV, dn, FB), lambda i, j: (0, 0, 0, 0), pipeline_mode=pl.Buffered(1))
    dh_in, dgam, dhb_in = _mm_nt_norm_bwd(
        f"{tag}_dn", dgu, BS((2 * NJ, tm, FB), lambda i, j: (0, i, 0)), w_in, resident, 1, h_in, gamma, dh, F32, mm_fn=_ffn_dn_mm,
        after=[e[1] for e in entries],
    )
    return dh_in, dhb_in, dgam, landed


def _heads_mm(y_ref, w_ref):
    acc = None
    for h in range(HEADS_B):
        part = _dot_nt(y_ref[h], w_ref[h])
        acc = part if acc is None else acc + part
    return acc


def _dqkv_mm(per):
    def mm(y_ref, w_ref):
        acc = None
        for j in range(NDEV):
            cols = [y_ref[(per * j + k) // 8, :, ((per * j + k) % 8) * 128 : ((per * j + k) % 8 + 1) * 128] for k in range(per)]
            part = _dot_nt(jnp.concatenate(cols, axis=1), w_ref[j])
            acc = part if acc is None else acc + part
        return acc

    return mm


def _kv_latent_bwd(dkv, w_up, ckr, latent_norm, dkr, c64, s64, p64, seq):
    t, wd = ckr.shape
    hb = w_up.shape[-1]
    tm = _tile(min(seq, 512), min(seq, 512))
    nt = t // tm
    nseq = seq // tm

    def epilogue(dn, ex, outs, i, nt_):
        dlat, dg_rows = _norm_bwd(dn, ex[0][...], ex[1][...])
        _acc_rows(outs[1], dg_rows, i, nt_)
        outs[0][:, :KV_LORA] = dlat.astype(BF16)
        outs[0][:, KV_LORA:] = _rope_bwd(ex[2][...], ex[3][...], ex[4][...], ex[5][...]).astype(BF16)

    pos = BS((tm, ROPE), lambda i, j: (i % nseq, 0))
    extra = [
        (ckr, BS((tm, KV_LORA), lambda i, j: (i, 0))), (latent_norm, BS((1, KV_LORA), lambda i, j: (0, 0))),
        (dkr, BS((tm, ROPE), lambda i, j: (i, 0))), (c64, pos), (s64, pos), (p64, BS((ROPE, ROPE), lambda i, j: (0, 0))),
    ]
    def heads_mm(y_ref, w_ref):
        acc = None
        for h in range(HEADS_B):
            part = _dot_nt(y_ref[:, h * hb : (h + 1) * hb], w_ref[h])
            acc = part if acc is None else acc + part
        return acc

    return _mm_nt_epi(
        "kv_latent_bwd", dkv, BS((tm, HEADS_B * hb), lambda i, j: (i, 0)), w_up, BS((HEADS_B, KV_LORA, hb), lambda i, j: (0, 0, 0)),
        1, KV_LORA, extra, [SDS((t, wd), BF16), SDS((8, KV_LORA), F32)],
        [BS((tm, wd), lambda i, j: (i, 0)), BS((8, KV_LORA), lambda i, j: (0, 0))], epilogue, tm, nt, heads_mm,
    )


def _adamw(name, parts, w, m, v):
    n_layers, rows, cols = w.shape
    tr = max(d for d in range(8, min(rows, 256) + 1, 8) if rows % d == 0)
    nb = rows // tr

    def body(*refs):
        p_refs = refs[:n_layers]
        w_ref, m_ref, v_ref, g_ref, d_ref, nm_ref, nv_ref = refs[n_layers : n_layers + 7]
        layer = pl.program_id(0)
        for lp in range(n_layers):

            @pl.when(layer == lp)
            def _():
                g = p_refs[lp][0].astype(F32)
                for k in range(1, NDEV):
                    g = g + p_refs[lp][k].astype(F32)
                g_ref[...] = g

        g = g_ref[...]
        nm = ADAM_B1 * m_ref[...] + (1.0 - ADAM_B1) * g
        nv = ADAM_B2 * v_ref[...] + (1.0 - ADAM_B2) * (g * g)
        nm_ref[...] = nm
        nv_ref[...] = nv
        m_hat = nm / (1.0 - ADAM_B1 ** ADAM_STEP)
        v_hat = nv / (1.0 - ADAM_B2 ** ADAM_STEP)
        d_ref[...] = -ADAM_LR * (m_hat / (jnp.sqrt(v_hat) + ADAM_EPS) + ADAM_WD * w_ref[...])

    def part_spec(lp):
        return BS((NDEV, tr, cols), lambda l, i: (0, jnp.where(l == lp, i, jnp.where(l < lp, 0, nb - 1)), 0))

    row = BS((None, tr, cols), lambda l, i: (l, i, 0))
    return pl.pallas_call(
        body, name=name, grid=(n_layers, nb),
        in_specs=[part_spec(lp) for lp in range(n_layers)] + [row, row, row],
        out_specs=[row] * 4, out_shape=[SDS(w.shape, F32)] * 4,
        compiler_params=_cparams(2),
    )(*parts, w, m, v)


def _pack_small(ffn1_norm, mix_norm, ffn2_norm, kv_norm, final_norm, q_norm, latent_norm, rel_bias, last_row):
    dn = ffn1_norm.shape[-1]

    def rows_of(a, n_rows):
        flat = a.reshape(-1)
        return jnp.pad(flat, (0, n_rows * dn - flat.shape[0])).reshape(n_rows, dn)

    return jnp.concatenate(
        [
            ffn1_norm.reshape(2, dn), mix_norm.reshape(2, dn), ffn2_norm.reshape(2, dn), kv_norm.reshape(1, dn),
            final_norm.reshape(1, dn), rows_of(q_norm, 1), rows_of(latent_norm, 1), rows_of(rel_bias, 5), rows_of(last_row, 1),
        ],
        axis=0,
    )


def _unpack_small(pack):
    dn = pack.shape[-1]
    return dict(
        ffn1_norm=pack[0:2], mix_norm=pack[2:4], ffn2_norm=pack[4:6], kv_norm=pack[6], final_norm=pack[7],
        b_q_norm=pack[8, :Q_LORA].reshape(1, Q_LORA), kv_latent_norm=pack[9, :KV_LORA],
        a_rel_bias=pack[10:15].reshape(-1)[: HEADS_A * NREL].reshape(1, HEADS_A, NREL), last=pack[15],
    )


def kernel(x, ffn1_norm, ffn1_w_in, ffn1_w_out, mix_norm, ffn2_norm, ffn2_w_in, ffn2_w_out, a_w_qkv, a_rel_bias, a_w_o, kv_norm, kv_w_down, kv_latent_norm, kv_w_up, b_w_dq, b_q_norm, b_w_uq, b_w_o, final_norm, loss_target, m_ffn1_norm, m_ffn1_w_in, m_ffn1_w_out, m_mix_norm, m_ffn2_norm, m_ffn2_w_in, m_ffn2_w_out, m_a_w_qkv, m_a_rel_bias, m_a_w_o, m_kv_norm, m_kv_w_down, m_kv_latent_norm, m_kv_w_up, m_b_w_dq, m_b_q_norm, m_b_w_uq, m_b_w_o, m_final_norm, v_ffn1_norm, v_ffn1_w_in, v_ffn1_w_out, v_mix_norm, v_ffn2_norm, v_ffn2_w_in, v_ffn2_w_out, v_a_w_qkv, v_a_rel_bias, v_a_w_o, v_kv_norm, v_kv_w_down, v_kv_latent_norm, v_kv_w_up, v_b_w_dq, v_b_q_norm, v_b_w_uq, v_b_w_o, v_final_norm):
    bl, seq, dn = x.shape
    t = bl * seq
    tm = _tile(t)
    nt = t // tm
    x2 = x.reshape(t, dn)
    target2 = loss_target.reshape(t, dn)

    def gathered(*ws):
        return [("gather", w.astype(BF16)) for w in ws]

    groups = [
        gathered(ffn1_w_in[0]), gathered(ffn1_w_out[0]), gathered(a_w_qkv[0], a_w_o[0]), gathered(ffn2_w_in[0], ffn2_w_out[0]),
        gathered(kv_w_down, kv_w_up), gathered(ffn1_w_in[1], ffn1_w_out[1]), gathered(b_w_dq[0], b_w_uq[0], b_w_o[0]),
        gathered(ffn2_w_in[1], ffn2_w_out[1]),
    ]
    ag = [_exchange_sc(f"gather_{k}", group, GATHER_IDS[k]) for k, group in enumerate(groups)]

    def as_w_in(w):
        return w.reshape(1, NDEV, dn, FB)

    def as_w_out(w):
        return w.reshape(1, NJ, FB, dn)

    c64, s64, p64, c192, s192, p192 = _rope_tables(seq)
    q_norm = b_q_norm.reshape(1, Q_LORA)
    latent_norm = kv_latent_norm.reshape(1, KV_LORA)
    bias = _window_bias(_rel_bias_fwd(jnp.pad(a_rel_bias[0], ((0, 0), (0, NREL_PAD - NREL)))))

    h0, h1, h2, n1, hn, n2, gu1, gu2, a1, a2, w_in1, w_in2, w_out1, w_out2 = ([None, None] for _ in range(14))
    h0[0] = x2
    (n1[0],) = _norm_fwd("norm_x", x2, ffn1_norm[0:1])
    w_in1[0] = as_w_in(ag[0][0])
    gu1[0], a1[0] = _ffn_in("ffn1_in_0", n1[0], w_in1[0], 0)
    w_out1[0] = as_w_out(ag[1][0])
    h1[0], hn[0] = _mm_res_norm("ffn1_out_0", a1[0], w_out1[0], 0, h0[0], mix_norm[0:1], 0.5)
    w_qkv, w_o_a = ag[2]
    qkv_wb = w_qkv.shape[-1]
    w_o_a = w_o_a.reshape(1, 1, dn, dn)
    qkv3 = _qkv_proj("qkv_proj", hn[0], w_qkv)
    o_a, lse_a = _attn_a_fwd(qkv3, bias, bl, seq)
    h2[0], n2[0] = _mm_res_norm("attn_a_out", o_a.reshape(1, t, dn), w_o_a, 0, h1[0], ffn2_norm[0:1], 1.0)
    w_in2[0], w_out2[0] = as_w_in(ag[3][0]), as_w_out(ag[3][1])
    gu2[0], a2[0] = _ffn_in("ffn2_in_0", n2[0], w_in2[0], 0)
    h0[1], hk, n1[1] = _mm_res_norm(
        "ffn2_out_0", a2[0], w_out2[0], 0, h2[0], jnp.concatenate([kv_norm.reshape(1, dn), ffn1_norm[1:2]], axis=0), 0.5
    )
    w_down, w_up = ag[4]
    w_down = w_down.reshape(dn, KV_LORA + ROPE)
    ckr, ckv, kr = _kv_down(hk, w_down, latent_norm, c64, s64, p64, seq)
    kv = _kv_up(ckv, w_up)
    w_in1[1], w_out1[1] = as_w_in(ag[5][0]), as_w_out(ag[5][1])
    gu1[1], a1[1] = _ffn_in("ffn1_in_1", n1[1], w_in1[1], 0)
    h1[1], hn[1] = _mm_res_norm("ffn1_out_1", a1[1], w_out1[1], 0, h0[1], mix_norm[1:2], 0.5)
    w_dq, w_uq, w_o_b = ag[6]
    w_dq = w_dq.reshape(dn, Q_LORA)
    w_o_b = w_o_b.reshape(1, 1, dn, dn)
    cq_pre, cq = _q_down(hn[1], w_dq, q_norm)
    q = _q_up(cq, w_uq, c192, s192, p192, seq)
    o_b, lse_b = _mla_fwd(q, kv, kr, bl, seq)
    h2[1], n2[1] = _mm_res_norm("attn_b_out", o_b.reshape(1, t, dn), w_o_b, 0, h1[1], ffn2_norm[1:2], 1.0)
    w_in2[1], w_out2[1] = as_w_in(ag[7][0]), as_w_out(ag[7][1])
    gu2[1], a2[1] = _ffn_in("ffn2_in_1", n2[1], w_in2[1], 0)
    (h_last,) = _mm_res_norm("ffn2_out_1", a2[1], w_out2[1], 0, h2[1], None, 0.5)
    dh, dhb, dg_final, loss_part = _loss_final(h_last, target2, final_norm.reshape(1, dn))

    dg_ffn1, dg_mix, dg_ffn2, rs_ffn1, rs_ffn2 = ([None, None] for _ in range(5))

    def whole(rows, cols):
        return BS((rows, cols), lambda j: (0, 0))

    def dw_rows(name, xa, ya):
        n = ya.shape[1]
        return _mm_tn(name, xa, whole(t, dn), ya, whole(t, n), (dn, n), whole(dn, n), 1).reshape(NDEV, dn // NDEV, n)

    dh, dhb, dg_ffn2[1], rs_ffn2[1] = _ffn_bwd(
        "ffn2_1", dh, dhb, n2[1], h2[1], ffn2_norm[1:2], gu2[1], a2[1], w_in2[1], w_out2[1], REDUCE_IDS[0], ()
    )
    do_b = _mm_nt_plain("attn_b_do", dhb, w_o_b.reshape(dn, dn))
    dw_o_b = dw_rows("attn_b_dwo", o_b, dhb)
    dq_pre, dkv, dkr = _mla_bwd(q, kv, kr, o_b, lse_b, do_b, c192, s192, p192, bl, seq)
    dw_uq = _mm_tn(
        "dw_uq", cq, whole(t, Q_LORA), dq_pre, BS((None, t, QK_B), lambda j: (j, 0, 0)),
        (HEADS_B, Q_LORA, QK_B), BS((None, Q_LORA, QK_B), lambda j: (j, 0, 0)), HEADS_B,
    )
    dcq_pre, dg_q = _mm_nt_norm_bwd(
        "dcq", dq_pre, BS((HEADS_B, tm, QK_B), lambda i, j: (0, i, 0)), w_uq, BS((HEADS_B, Q_LORA, QK_B), lambda i, j: (0, 0, 0)),
        1, cq_pre, q_norm, None, BF16, mm_fn=_heads_mm,
    )
    dw_dq = dw_rows("dw_dq", hn[1], dcq_pre)
    dh, dg_mix[1], dhb = _mm_nt_norm_bwd(
        "dhn_b", dcq_pre, BS((tm, Q_LORA), lambda i, j: (i, 0)), w_dq, BS((dn, Q_LORA), lambda i, j: (0, 0)),
        1, h1[1], mix_norm[1:2], dh, F32,
    )
    dh, dhb, dg_ffn1[1], rs_ffn1[1] = _ffn_bwd(
        "ffn1_1", dh, dhb, n1[1], h0[1], ffn1_norm[1:2], gu1[1], a1[1], w_in1[1], w_out1[1], REDUCE_IDS[1], rs_ffn2[1][:1]
    )
    dw_up = _mm_tn(
        "dw_up", ckv, whole(t, KV_LORA), dkv, BS((t, NOPE + V_DIM), lambda j: (0, j)),
        (HEADS_B, KV_LORA, NOPE + V_DIM), BS((None, KV_LORA, NOPE + V_DIM), lambda j: (j, 0, 0)), HEADS_B,
    )
    dckr, dg_latent = _kv_latent_bwd(dkv, w_up, ckr, latent_norm, dkr, c64, s64, p64, seq)
    dw_down = dw_rows("dw_down", hk, dckr)
    dh, dg_kv, dhb = _mm_nt_norm_bwd(
        "dhk", dckr, BS((tm, KV_LORA + ROPE), lambda i, j: (i, 0)), w_down, BS((dn, KV_LORA + ROPE), lambda i, j: (0, 0)),
        1, h0[1], kv_norm.reshape(1, dn), dh, F32,
    )
    dh, dhb, dg_ffn2[0], rs_ffn2[0] = _ffn_bwd(
        "ffn2_0", dh, dhb, n2[0], h2[0], ffn2_norm[0:1], gu2[0], a2[0], w_in2[0], w_out2[0], REDUCE_IDS[2], rs_ffn1[1][:1]
    )
    do_a = _mm_nt_plain("attn_a_do", dhb, w_o_a.reshape(dn, dn))
    dw_o_a = dw_rows("attn_a_dwo", o_a, dhb)
    dqkv3, dbias = _attn_a_bwd(qkv3, o_a, lse_a, do_a, bias, bl, seq)
    dw_qkv = _dw_qkv(hn[0], dqkv3, qkv_wb)
    mixer_grads = [dw_o_a, dw_qkv, dw_o_b, dw_uq, dw_dq, dw_up, dw_down]
    dh, dg_mix[0], dhb = _mm_nt_norm_bwd(
        "dhn_a", dqkv3, BS((3, tm, dn), lambda i, j: (0, i, 0)), w_qkv, BS((NDEV, dn, qkv_wb), lambda i, j: (0, 0, 0)),
        1, h1[0], mix_norm[0:1], dh, F32, mm_fn=_dqkv_mm(qkv_wb // 128), after=mixer_grads,
    )
    rs_mixers = _exchange_sc("mixers_reduce", [("scatter", g) for g in mixer_grads], REDUCE_IDS[3], rs_ffn2[0][:1])
    dh, dhb, dg_ffn1[0], rs_ffn1[0] = _ffn_bwd(
        "ffn1_0", dh, dhb, n1[0], h0[0], ffn1_norm[0:1], gu1[0], a1[0], w_in1[0], w_out1[0], REDUCE_IDS[4], rs_mixers[:1]
    )
    grad_x = dh.reshape(bl, seq, dn)
    dtable = _rel_bias_bwd(_window_bias_bwd(dbias))[:, :NREL]

    def update(name, parts, w, m, v):
        shape3 = (len(parts),) + w.shape[-2:]
        parts = [p.reshape((NDEV,) + shape3[1:]) for p in parts]
        outs = _adamw(name, parts, w.reshape(shape3), m.reshape(shape3), v.reshape(shape3))
        return [o.reshape(w.shape) for o in outs]

    res = {}
    r_in2_1, r_out2_1 = rs_ffn2[1]
    r_in1_1, r_out1_1 = rs_ffn1[1]
    r_in2_0, r_out2_0 = rs_ffn2[0]
    r_in1_0, r_out1_0 = rs_ffn1[0]
    r_o_a, r_qkv, r_o_b, r_uq, r_dq, r_up, r_down = rs_mixers
    def update_transposed(name, parts, w, m, v):
        outs = update(name, parts, *[jnp.swapaxes(a, 1, 2) for a in (w, m, v)])
        return [jnp.swapaxes(o, 1, 2) for o in outs]

    res["ffn2_w_in"] = update_transposed("adamw_ffn2_w_in", [r_in2_0, r_in2_1], ffn2_w_in, m_ffn2_w_in, v_ffn2_w_in)
    res["ffn2_w_out"] = update("adamw_ffn2_w_out", [r_out2_0, r_out2_1], ffn2_w_out, m_ffn2_w_out, v_ffn2_w_out)
    res["kv_w_down"] = update("adamw_kv_w_down", [r_down], kv_w_down, m_kv_w_down, v_kv_w_down)
    res["kv_w_up"] = update("adamw_kv_w_up", [r_up], kv_w_up, m_kv_w_up, v_kv_w_up)
    res["b_w_dq"] = update("adamw_b_w_dq", [r_dq], b_w_dq, m_b_w_dq, v_b_w_dq)
    res["b_w_uq"] = update("adamw_b_w_uq", [r_uq], b_w_uq, m_b_w_uq, v_b_w_uq)
    res["b_w_o"] = update("adamw_b_w_o", [r_o_b], b_w_o, m_b_w_o, v_b_w_o)
    res["a_w_qkv"] = update("adamw_a_w_qkv", [r_qkv], a_w_qkv, m_a_w_qkv, v_a_w_qkv)
    res["a_w_o"] = update("adamw_a_w_o", [r_o_a], a_w_o, m_a_w_o, v_a_w_o)

    small = _pack_small(
        jnp.stack([dg_ffn1[0][0], dg_ffn1[1][0]]), jnp.stack([dg_mix[0][0], dg_mix[1][0]]), jnp.stack([dg_ffn2[0][0], dg_ffn2[1][0]]),
        dg_kv[0], dg_final[0], dg_q[0], dg_latent[0], dtable, loss_part[0],
    )
    done = [r[1] for name, r in res.items() if name != "ffn2_w_in"]
    (r_small,) = _exchange("gather_small_grads", [("gather", small)], after=done)
    res["ffn1_w_in"] = update_transposed("adamw_ffn1_w_in", [r_in1_0, r_in1_1], ffn1_w_in, m_ffn1_w_in, v_ffn1_w_in)
    res["ffn1_w_out"] = update("adamw_ffn1_w_out", [r_out1_0, r_out1_1], ffn1_w_out, m_ffn1_w_out, v_ffn1_w_out)
    zero_row = jnp.zeros((dn,), F32)
    packs = [
        _pack_small(f1, mx, f2, kvn, fin, qn, lat, rel, zero_row)
        for f1, mx, f2, kvn, fin, qn, lat, rel in (
            (ffn1_norm, mix_norm, ffn2_norm, kv_norm, final_norm, b_q_norm, kv_latent_norm, a_rel_bias),
            (m_ffn1_norm, m_mix_norm, m_ffn2_norm, m_kv_norm, m_final_norm, m_b_q_norm, m_kv_latent_norm, m_a_rel_bias),
            (v_ffn1_norm, v_mix_norm, v_ffn2_norm, v_kv_norm, v_final_norm, v_b_q_norm, v_kv_latent_norm, v_a_rel_bias),
        )
    ]
    small_out = [_unpack_small(o[0]) for o in _adamw("adamw_small", [r_small], *[p[None] for p in packs])]
    for name in ("ffn1_norm", "mix_norm", "ffn2_norm", "a_rel_bias", "kv_norm", "kv_latent_norm", "b_q_norm", "final_norm"):
        res[name] = [so[name] for so in small_out]
    loss = small_out[0]["last"][0]

    order = [
        "ffn1_norm", "ffn1_w_in", "ffn1_w_out", "mix_norm", "ffn2_norm", "ffn2_w_in", "ffn2_w_out", "a_w_qkv", "a_rel_bias",
        "a_w_o", "kv_norm", "kv_w_down", "kv_latent_norm", "kv_w_up", "b_w_dq", "b_q_norm", "b_w_uq", "b_w_o", "final_norm",
    ]
    return (loss, grad_x, *[res[n][0] for n in order], *[res[n][1] for n in order], *[res[n][2] for n in order], *[res[n][3] for n in order])
```

```python
import jax
import jax.numpy as jnp
import numpy as np
from jax import lax
from jax.experimental import pallas as pl
from jax.experimental.pallas import tpu as pltpu
from jax.experimental.pallas import tpu_sc as plsc

NDEV = 8
D_MODEL = 1024
D_FF = 2816
FB = 2 * D_FF // NDEV
NJ = D_FF // FB
CHUNK = 64
LEFT_CHUNKS = 8
PAD = LEFT_CHUNKS * CHUNK
BAND = PAD + CHUNK
CHUNKS_PER_STEP = 4
WINDOW = PAD + CHUNKS_PER_STEP * CHUNK
STEP_ROWS = CHUNKS_PER_STEP * 2 * CHUNK
MAX_REL = 128
NREL = 2 * MAX_REL + 1
NREL_PAD = 384
HEADS_A = 16
HEADS_B = 8
NOPE = 128
ROPE = 64
QK_B = NOPE + ROPE
V_DIM = 128
Q_LORA = 768
KV_LORA = 256
ROPE_THETA = 10000.0
EPS = 1e-6
NEG_INF = -1e30
MLA_TQ = 256
MLA_TK_FWD = 256
MLA_TK_BWD = 1024
ADAM_LR = 0.001
ADAM_B1 = 0.9
ADAM_B2 = 0.999
ADAM_EPS = 1e-08
ADAM_WD = 0.01
ADAM_STEP = 10
PACK_ROWS = 16
GATHER_IDS = tuple(range(1, 9))
REDUCE_IDS = tuple(range(9, 14))
VMEM_LIMIT_BYTES = 56 * 1024 * 1024

F32 = jnp.float32
BF16 = jnp.bfloat16
SDS = jax.ShapeDtypeStruct
BS = pl.BlockSpec
MESH = pl.DeviceIdType.MESH


def _cparams(n_axes):
    return pltpu.CompilerParams(dimension_semantics=("arbitrary",) * n_axes, vmem_limit_bytes=VMEM_LIMIT_BYTES)


def _tile(t, want=512):
    return want if t % want == 0 else t


def _dot(a, b):
    return jnp.dot(a, b, preferred_element_type=F32)


def _dot_nt(a, b):
    return lax.dot_general(a, b, (((1,), (1,)), ((), ())), preferred_element_type=F32)


def _dot_tn(a, b):
    return lax.dot_general(a, b, (((0,), (0,)), ((), ())), preferred_element_type=F32)


def _split3(a):
    hi = a.astype(BF16)
    rest = a - hi.astype(F32)
    mid = rest.astype(BF16)
    return hi, mid, (rest - mid.astype(F32)).astype(BF16)


def _dot_exact(a, onehot, transposed=False):
    ob = onehot.astype(BF16)
    dot = _dot_nt if transposed else _dot
    hi, mid, lo = _split3(a)
    return dot(hi, ob) + dot(mid, ob) + dot(lo, ob)


def _rms_scale(h):
    return lax.rsqrt(jnp.mean(h * h, axis=-1, keepdims=True) + EPS)


def _acc_rows(ref, val, step, n_steps):
    part = val.reshape(val.shape[0] // 8, 8, val.shape[1]).sum(axis=0)

    @pl.when(step == 0)
    def _():
        ref[...] = part

    @pl.when(step > 0)
    def _():
        ref[...] += part

    @pl.when(step == n_steps - 1)
    def _():
        ref[...] = jnp.broadcast_to(jnp.sum(ref[...], axis=0, keepdims=True), ref.shape)


def _exchange_plan(entries):
    ins = [e[1] for e in entries]
    kinds = [e[0] for e in entries]
    lands = [SDS((NDEV,) + a.shape if k == "gather" else a.shape, a.dtype) for k, a in zip(kinds, ins)]
    return ins, lands, kinds


def _mesh_place():
    x, y, c = lax.axis_index("x"), lax.axis_index("y"), lax.axis_index("c")
    return (x, y, c), 4 * x + 2 * y + c


def _flipped(place, p):
    x, y, c = place
    px = 1 - x if p & 4 else x
    py = 1 - y if p & 2 else y
    pc = 1 - c if p & 1 else c
    return (px, py, pc), 4 * px + 2 * py + pc


def _ends(kind, src_ref, land_ref, origin, target):
    if kind == "gather":
        return src_ref, land_ref.at[origin]
    return src_ref.at[target], land_ref.at[origin]


def _remote(kind, src_ref, land_ref, send_sems, recv_sems, k, p, place, me, arriving):
    peer_pos, peer = _flipped(place, p)
    src, dst = _ends(kind, src_ref, land_ref, me, peer)
    if arriving:
        dst = _ends(kind, src_ref, land_ref, peer, me)[1]
    sem = k * (NDEV - 1) + p - 1
    return pltpu.make_async_remote_copy(
        src_ref=src, dst_ref=dst, send_sem=send_sems.at[sem], recv_sem=recv_sems.at[sem], device_id=peer_pos, device_id_type=MESH,
    )


def _exchange(name, entries, after=()):
    ins, lands, kinds = _exchange_plan(entries)
    n = len(ins)
    after = tuple(after)

    def body(*refs):
        refs = refs[:n] + refs[n + len(after) :]
        in_refs, land_refs = refs[:n], refs[n : 2 * n]
        send_sems, recv_sems, local_sems = refs[2 * n :]
        place, me = _mesh_place()
        local = []
        for k in range(n):
            src, dst = _ends(kinds[k], in_refs[k], land_refs[k], me, me)
            local.append(pltpu.make_async_copy(src, dst, local_sems.at[k]))
            local[-1].start()
        sends = []
        for p in range(1, NDEV):
            for k in range(n):
                sends.append(_remote(kinds[k], in_refs[k], land_refs[k], send_sems, recv_sems, k, p, place, me, False))
                sends[-1].start()
        for p in range(1, NDEV):
            for k in range(n):
                _remote(kinds[k], in_refs[k], land_refs[k], send_sems, recv_sems, k, p, place, me, True).wait_recv()
        for cp in sends:
            cp.wait_send()
        for cp in local:
            cp.wait()

    any_spec = BS(memory_space=pl.ANY)
    return pl.pallas_call(
        body, name=name, out_shape=lands, in_specs=[any_spec] * (n + len(after)), out_specs=[any_spec] * n,
        scratch_shapes=[
            pltpu.SemaphoreType.DMA((n * (NDEV - 1),)), pltpu.SemaphoreType.DMA((n * (NDEV - 1),)), pltpu.SemaphoreType.DMA((n,)),
        ],
    )(*ins, *after)


def _exchange_sc(name, entries, collective_id, after=()):
    ins, lands, kinds = _exchange_plan(entries)
    n = len(ins)
    after = tuple(after)

    def launch(*refs):
        refs = refs[:n] + refs[n + len(after) :]
        in_refs, land_refs = refs[:n], refs[n : 2 * n]
        send_sems, recv_sems, local_sems = refs[2 * n :]
        place, me = _mesh_place()
        barrier = pltpu.get_barrier_semaphore()
        for p in range(1, NDEV):
            pl.semaphore_signal(barrier, inc=1, device_id=_flipped(place, p)[0], device_id_type=MESH)
        pl.semaphore_wait(barrier, NDEV - 1)
        local = []
        for k in range(n):
            src, dst = _ends(kinds[k], in_refs[k], land_refs[k], me, me)
            local.append(pltpu.make_async_copy(src, dst, local_sems.at[k]))
            local[-1].start()
        sends = []
        if all(kind == "gather" for kind in kinds):
            for p in (1, 2, 4, 6):
                for k in range(n):
                    sends.append(_remote(kinds[k], in_refs[k], land_refs[k], send_sems, recv_sems, k, p, place, me, False))
                    sends[-1].start()
            sibling_pos, _ = _flipped(place, 1)
            for f in (2, 4, 6):
                _, origin = _flipped(place, f)
                for k in range(n):
                    _remote(kinds[k], in_refs[k], land_refs[k], send_sems, recv_sems, k, f, place, me, True).wait_recv()
                    sem = k * (NDEV - 1) + f
                    sends.append(
                        pltpu.make_async_remote_copy(
                            src_ref=land_refs[k].at[origin], dst_ref=land_refs[k].at[origin], send_sem=send_sems.at[sem],
                            recv_sem=recv_sems.at[sem], device_id=sibling_pos, device_id_type=MESH,
                        )
                    )
                    sends[-1].start()
            for p in (1, 3, 5, 7):
                for k in range(n):
                    _remote(kinds[k], in_refs[k], land_refs[k], send_sems, recv_sems, k, p, place, me, True).wait_recv()
        else:
            for p in range(1, NDEV):
                for k in range(n):
                    sends.append(_remote(kinds[k], in_refs[k], land_refs[k], send_sems, recv_sems, k, p, place, me, False))
                    sends[-1].start()
            for p in range(1, NDEV):
                for k in range(n):
                    _remote(kinds[k], in_refs[k], land_refs[k], send_sems, recv_sems, k, p, place, me, True).wait_recv()
        for cp in sends:
            cp.wait_send()
        for cp in local:
            cp.wait()

    return pl.kernel(
        launch, out_type=tuple(lands), mesh=plsc.ScalarSubcoreMesh(axis_name="sequencer", num_cores=1), name=name,
        scratch_types=(
            pltpu.SemaphoreType.DMA((n * (NDEV - 1),)), pltpu.SemaphoreType.DMA((n * (NDEV - 1),)), pltpu.SemaphoreType.DMA((n,)),
        ),
        compiler_params=pltpu.CompilerParams(collective_id=collective_id),
    )(*ins, *after)


def _norm_fwd(name, h, gammas):
    t, dn = h.shape
    ng = gammas.shape[0]
    tm = _tile(t)

    def body(h_ref, g_ref, *outs):
        hv = h_ref[...]
        hh = hv * _rms_scale(hv)
        for i, o_ref in enumerate(outs):
            o_ref[...] = (hh * g_ref[i : i + 1, :]).astype(BF16)

    row = BS((tm, dn), lambda i: (i, 0))
    return pl.pallas_call(
        body, name=name, grid=(t // tm,),
        in_specs=[row, BS((ng, dn), lambda i: (0, 0))],
        out_specs=[row] * ng, out_shape=[SDS((t, dn), BF16)] * ng,
        compiler_params=_cparams(1),
    )(h, gammas)


def _ffn_in(name, n, w_in, layer):
    t, dn = n.shape
    tm = _tile(t, 1024)

    def body(n_ref, wg_ref, wu_ref, gu_ref, a_ref):
        xv = n_ref[...]
        g = _dot(xv, wg_ref[...])
        u = _dot(xv, wu_ref[...])
        gu_ref[0] = g.astype(BF16)
        gu_ref[1] = u.astype(BF16)
        a_ref[...] = (g * jax.nn.sigmoid(g) * u).astype(BF16)

    return pl.pallas_call(
        body, name=name, grid=(NJ, t // tm),
        in_specs=[
            BS((tm, dn), lambda j, i: (i, 0)),
            BS((None, None, dn, FB), lambda j, i: (layer, j, 0, 0)),
            BS((None, None, dn, FB), lambda j, i: (layer, j + NJ, 0, 0)),
        ],
        out_specs=[BS((None, 2, tm, FB), lambda j, i: (j, 0, i, 0)), BS((None, tm, FB), lambda j, i: (j, i, 0))],
        out_shape=[SDS((NJ, 2, t, FB), BF16), SDS((NJ, t, FB), BF16)],
        compiler_params=_cparams(2),
    )(n, w_in, w_in)


def _mm_res_norm(name, a, w, layer, h_in, gammas, scale):
    nk, t, kb = a.shape
    dn = w.shape[-1]
    ng = 0 if gammas is None else gammas.shape[0]
    tm = _tile(t)

    def body(*refs):
        a_ref, w_ref, h_ref = refs[:3]
        g_ref = refs[3] if ng else None
        outs = refs[3 + (1 if ng else 0) :]
        acc = _dot(a_ref[0], w_ref[0])
        for k in range(1, nk):
            acc += _dot(a_ref[k], w_ref[k])
        ho = h_ref[...] + scale * acc
        outs[0][...] = ho
        if ng:
            hh = ho * _rms_scale(ho)
            for i in range(ng):
                outs[1 + i][...] = (hh * g_ref[i : i + 1, :]).astype(BF16)

    row = BS((tm, dn), lambda i: (i, 0))
    in_specs = [BS((nk, tm, kb), lambda i: (0, i, 0)), BS((None, nk, kb, dn), lambda i: (layer, 0, 0, 0)), row]
    args = [a, w, h_in]
    if ng:
        in_specs.append(BS((ng, dn), lambda i: (0, 0)))
        args.append(gammas)
    return pl.pallas_call(
        body, name=name, grid=(t // tm,),
        in_specs=in_specs,
        out_specs=[row] * (1 + ng), out_shape=[SDS((t, dn), F32)] + [SDS((t, dn), BF16)] * ng,
        compiler_params=_cparams(1),
    )(*args)


def _qkv_proj(name, hn, w_qkv):
    t, dn = hn.shape
    wb = w_qkv.shape[-1]
    per = wb // 128
    tm = _tile(t)

    def body(x_ref, w_ref, o_ref):
        xv = x_ref[...]
        for j in range(NDEV):
            yv = _dot(xv, w_ref[j]).astype(BF16)
            for i in range(per):
                n = per * j + i
                o_ref[n // 8, :, (n % 8) * 128 : (n % 8 + 1) * 128] = yv[:, i * 128 : (i + 1) * 128]

    return pl.pallas_call(
        body, name=name, grid=(t // tm,),
        in_specs=[BS((tm, dn), lambda i: (i, 0)), BS((NDEV, dn, wb), lambda i: (0, 0, 0))],
        out_specs=BS((3, tm, dn), lambda i: (0, i, 0)), out_shape=SDS((3, t, dn), BF16),
        compiler_params=_cparams(1),
    )(hn, w_qkv)


def _rel_onehot(i):
    r = lax.broadcasted_iota(jnp.int32, (NREL_PAD, BAND), 0)
    j = lax.broadcasted_iota(jnp.int32, (NREL_PAD, BAND), 1)
    idx = jnp.clip(PAD + i - j, -MAX_REL, MAX_REL) + MAX_REL
    return (idx == r).astype(F32)


def _rel_bias_fwd(table):
    def body(t_ref, o_ref):
        i8 = pl.program_id(0)
        for ii in range(8):
            o_ref[:, ii, :] = _dot_exact(t_ref[...], _rel_onehot(i8 * 8 + ii))

    return pl.pallas_call(
        body, name="rel_bias_fwd", grid=(CHUNK // 8,),
        in_specs=[BS((HEADS_A, NREL_PAD), lambda i: (0, 0))],
        out_specs=BS((HEADS_A, 8, BAND), lambda i: (0, i, 0)), out_shape=SDS((HEADS_A, CHUNK, BAND), F32),
        compiler_params=_cparams(1),
    )(table)


def _rel_bias_bwd(dbias):
    def body(d_ref, o_ref):
        i8 = pl.program_id(0)
        acc = jnp.zeros((HEADS_A, NREL_PAD), F32)
        for ii in range(8):
            acc += _dot_exact(d_ref[:, ii, :], _rel_onehot(i8 * 8 + ii), transposed=True)

        @pl.when(i8 == 0)
        def _():
            o_ref[...] = acc

        @pl.when(i8 > 0)
        def _():
            o_ref[...] += acc

    return pl.pallas_call(
        body, name="rel_bias_bwd", grid=(CHUNK // 8,),
        in_specs=[BS((HEADS_A, 8, BAND), lambda i: (0, i, 0))],
        out_specs=BS((HEADS_A, NREL_PAD), lambda i: (0, 0)), out_shape=SDS((HEADS_A, NREL_PAD), F32),
        compiler_params=_cparams(1),
    )(dbias)


def _window_bias(bias):
    b = bias.reshape(HEADS_A // 2, 2, CHUNK, BAND)
    per_chunk = [
        jnp.pad(b, ((0, 0), (0, 0), (0, 0), (cc * CHUNK, WINDOW - BAND - cc * CHUNK)), constant_values=NEG_INF)
        for cc in range(CHUNKS_PER_STEP)
    ]
    return jnp.stack(per_chunk, axis=1).reshape(HEADS_A // 2, STEP_ROWS, WINDOW)


def _window_bias_bwd(dwin):
    d = dwin.reshape(HEADS_A // 2, CHUNKS_PER_STEP, 2, CHUNK, WINDOW)
    return sum(d[:, cc, :, :, cc * CHUNK : cc * CHUNK + BAND] for cc in range(CHUNKS_PER_STEP)).reshape(HEADS_A, CHUNK, BAND)


def _step_rows(xs, lane):
    parts = []
    for cc in range(CHUNKS_PER_STEP):
        xc = xs[cc * CHUNK : (cc + 1) * CHUNK]
        parts.append(jnp.where(lane < 64, xc, jnp.zeros_like(xc)))
        parts.append(jnp.where(lane >= 64, xc, jnp.zeros_like(xc)))
    return jnp.concatenate(parts, axis=0)


def _pair_rows(ys, lane):
    parts = []
    for cc in range(CHUNKS_PER_STEP):
        y0 = ys[(2 * cc) * CHUNK : (2 * cc + 1) * CHUNK]
        y1 = ys[(2 * cc + 1) * CHUNK : (2 * cc + 2) * CHUNK]
        parts.append(jnp.where(lane < 64, y0, y1))
    return jnp.concatenate(parts, axis=0)


def _window_scores(q_rows, kwin, bias_win, first_key):
    s = _dot_nt(q_rows, kwin) * (CHUNK ** -0.5) + bias_win
    if first_key is None:
        return s
    col = lax.broadcasted_iota(jnp.int32, s.shape, 1)
    return jnp.where(col >= first_key, s, NEG_INF)


def _window_loop(n_passes, chunks):
    n_padded = min(PAD // (CHUNKS_PER_STEP * CHUNK), n_passes)
    lax.fori_loop(0, n_padded, lambda it, carry: chunks(it, carry, True), 0, unroll=2)
    if n_passes > n_padded:
        lax.fori_loop(n_padded, n_passes, lambda it, carry: chunks(it, carry, False), 0, unroll=2)


def _attn_a_fwd(qkv3, bias_win, bl, seq):
    t, dn = qkv3.shape[1:]
    npair = dn // 128
    step = CHUNKS_PER_STEP * CHUNK

    def body(q_ref, k_ref, v_ref, b_ref, o_ref, lse_ref, kpad, vpad):
        kpad[0:PAD, :] = jnp.zeros((PAD, 128), BF16)
        vpad[0:PAD, :] = jnp.zeros((PAD, 128), BF16)
        kpad[PAD:, :] = k_ref[...]
        vpad[PAD:, :] = v_ref[...]
        lane = lax.broadcasted_iota(jnp.int32, (CHUNK, 128), 1)

        def chunks(it, carry, padded):
            r0 = pl.multiple_of(it * step, step)
            q_rows = _step_rows(q_ref[pl.ds(r0, step), :], lane)
            s = _window_scores(q_rows, kpad[pl.ds(r0, WINDOW), :], b_ref[...], PAD - r0 if padded else None)
            m = jnp.max(s, axis=-1, keepdims=True)
            e = jnp.exp(s - m)
            total = jnp.sum(e, axis=-1, keepdims=True)
            o_rows = _dot((e * (1.0 / total)).astype(BF16), vpad[pl.ds(r0, WINDOW), :])
            o_ref[pl.ds(r0, step), :] = _pair_rows(o_rows, lane).astype(BF16)
            lse_ref[pl.ds(pl.multiple_of(it * STEP_ROWS, STEP_ROWS), STEP_ROWS), :] = m + jnp.log(total)
            return carry

        _window_loop(seq // step, chunks)

    return pl.pallas_call(
        body, name="attn_a_fwd", grid=(bl, npair),
        in_specs=[
            BS((None, seq, 128), lambda b, h: (0, b, h)),
            BS((None, seq, 128), lambda b, h: (1, b, h)),
            BS((None, seq, 128), lambda b, h: (2, b, h)),
            BS((None, STEP_ROWS, WINDOW), lambda b, h: (h, 0, 0)),
        ],
        out_specs=[BS((seq, 128), lambda b, h: (b, h)), BS((None, 2 * seq, 1), lambda b, h: (h, b, 0))],
        out_shape=[SDS((t, dn), BF16), SDS((npair, 2 * t, 1), F32)],
        scratch_shapes=[pltpu.VMEM((PAD + seq, 128), BF16), pltpu.VMEM((PAD + seq, 128), BF16)],
        compiler_params=_cparams(2),
    )(qkv3, qkv3, qkv3, bias_win)


def _attn_a_bwd(qkv3, out, lse, do, bias_win, bl, seq):
    t, dn = qkv3.shape[1:]
    npair = dn // 128
    step = CHUNKS_PER_STEP * CHUNK

    def body(q_ref, k_ref, v_ref, o_ref, lse_ref, do_ref, b_ref, dqkv_ref, db_ref, kpad, vpad, dkacc, dvacc):
        b = pl.program_id(1)
        kpad[0:PAD, :] = jnp.zeros((PAD, 128), BF16)
        vpad[0:PAD, :] = jnp.zeros((PAD, 128), BF16)
        kpad[PAD:, :] = k_ref[...]
        vpad[PAD:, :] = v_ref[...]
        dkacc[...] = jnp.zeros_like(dkacc)
        dvacc[...] = jnp.zeros_like(dvacc)

        @pl.when(b == 0)
        def _():
            db_ref[...] = jnp.zeros_like(db_ref)

        lane = lax.broadcasted_iota(jnp.int32, (CHUNK, 128), 1)

        def chunks(it, carry, padded):
            r0 = pl.multiple_of(it * step, step)
            q_rows = _step_rows(q_ref[pl.ds(r0, step), :], lane)
            do_rows = _step_rows(do_ref[pl.ds(r0, step), :], lane)
            kwin = kpad[pl.ds(r0, WINDOW), :]
            vwin = vpad[pl.ds(r0, WINDOW), :]
            o_rows = _step_rows(o_ref[pl.ds(r0, step), :], lane)
            delta = jnp.sum(do_rows.astype(F32) * o_rows.astype(F32), axis=-1, keepdims=True)
            lse_rows = lse_ref[pl.ds(pl.multiple_of(it * STEP_ROWS, STEP_ROWS), STEP_ROWS), :]
            p = jnp.exp(_window_scores(q_rows, kwin, b_ref[...], PAD - r0 if padded else None) - lse_rows)
            ds = p * (_dot_nt(do_rows, vwin) - delta)
            db_ref[...] += ds
            dsb = (ds * (CHUNK ** -0.5)).astype(BF16)
            dqkv_ref[0, pl.ds(r0, step), :] = _pair_rows(_dot(dsb, kwin), lane).astype(BF16)
            dkacc[pl.ds(r0, WINDOW), :] += _dot_tn(dsb, q_rows)
            dvacc[pl.ds(r0, WINDOW), :] += _dot_tn(p.astype(BF16), do_rows)
            return carry

        _window_loop(seq // step, chunks)
        dqkv_ref[1] = dkacc[PAD:, :].astype(BF16)
        dqkv_ref[2] = dvacc[PAD:, :].astype(BF16)

    return pl.pallas_call(
        body, name="attn_a_bwd", grid=(npair, bl),
        in_specs=[
            BS((None, seq, 128), lambda h, b: (0, b, h)),
            BS((None, seq, 128), lambda h, b: (1, b, h)),
            BS((None, seq, 128), lambda h, b: (2, b, h)),
            BS((seq, 128), lambda h, b: (b, h)),
            BS((None, 2 * seq, 1), lambda h, b: (h, b, 0)),
            BS((seq, 128), lambda h, b: (b, h)),
            BS((None, STEP_ROWS, WINDOW), lambda h, b: (h, 0, 0)),
        ],
        out_specs=[BS((3, seq, 128), lambda h, b: (0, b, h)), BS((None, STEP_ROWS, WINDOW), lambda h, b: (h, 0, 0))],
        out_shape=[SDS((3, t, dn), BF16), SDS((HEADS_A // 2, STEP_ROWS, WINDOW), F32)],
        scratch_shapes=[
            pltpu.VMEM((PAD + seq, 128), BF16), pltpu.VMEM((PAD + seq, 128), BF16),
            pltpu.VMEM((PAD + seq, 128), F32), pltpu.VMEM((PAD + seq, 128), F32),
        ],
        compiler_params=_cparams(2),
    )(qkv3, qkv3, qkv3, out, lse, do, bias_win)


def _rope_tables(seq):
    half = ROPE // 2
    freqs = ROPE_THETA ** (-jnp.arange(half, dtype=F32) / half)
    ang = jnp.arange(seq, dtype=F32)[:, None] * freqs[None, :]
    cos, sin = jnp.cos(ang), jnp.sin(ang)
    c64 = jnp.concatenate([cos, cos], axis=1)
    s64 = jnp.concatenate([-sin, sin], axis=1)
    c192 = jnp.concatenate([jnp.ones((seq, NOPE), F32), c64], axis=1)
    s192 = jnp.concatenate([jnp.zeros((seq, NOPE), F32), s64], axis=1)
    p64 = np.zeros((ROPE, ROPE), np.float32)
    for col in range(ROPE):
        p64[(col + half) % ROPE, col] = 1.0
    p192 = np.zeros((QK_B, QK_B), np.float32)
    p192[NOPE:, NOPE:] = p64
    return c64, s64, jnp.asarray(p64), c192, s192, jnp.asarray(p192)


def _rope(xv, cos, sin_signed, swap):
    return xv * cos + _dot_exact(xv, swap) * sin_signed


def _rope_bwd(dy, cos, sin_signed, swap):
    return dy * cos + _dot_exact(dy * sin_signed, swap)


def _q_down(hn, w_dq, q_norm):
    t, dn = hn.shape
    ql = w_dq.shape[1]
    tm = _tile(t)

    def body(x_ref, w_ref, g_ref, pre_ref, cq_ref):
        pre = _dot(x_ref[...], w_ref[...])
        pre_ref[...] = pre
        cq_ref[...] = (pre * _rms_scale(pre) * g_ref[...]).astype(BF16)

    return pl.pallas_call(
        body, name="q_down", grid=(t // tm,),
        in_specs=[BS((tm, dn), lambda i: (i, 0)), BS((dn, ql), lambda i: (0, 0)), BS((1, ql), lambda i: (0, 0))],
        out_specs=[BS((tm, ql), lambda i: (i, 0))] * 2, out_shape=[SDS((t, ql), F32), SDS((t, ql), BF16)],
        compiler_params=_cparams(1),
    )(hn, w_dq, q_norm)


def _q_up(cq, w_uq, c192, s192, p192, seq):
    t, ql = cq.shape
    tm = _tile(min(seq, 512), min(seq, 512))
    nseq = seq // tm

    def body(x_ref, w_ref, c_ref, s_ref, p_ref, o_ref):
        xv = x_ref[...]
        for h in range(HEADS_B):
            o_ref[h] = _rope(_dot(xv, w_ref[h]), c_ref[...], s_ref[...], p_ref[...]).astype(BF16)

    pos = BS((tm, QK_B), lambda i: (i % nseq, 0))
    return pl.pallas_call(
        body, name="q_up", grid=(t // tm,),
        in_specs=[
            BS((tm, ql), lambda i: (i, 0)), BS((HEADS_B, ql, QK_B), lambda i: (0, 0, 0)), pos, pos,
            BS((QK_B, QK_B), lambda i: (0, 0)),
        ],
        out_specs=BS((HEADS_B, tm, QK_B), lambda i: (0, i, 0)), out_shape=SDS((HEADS_B, t, QK_B), BF16),
        compiler_params=_cparams(1),
    )(cq, w_uq, c192, s192, p192)


def _kv_down(hk, w_down, latent_norm, c64, s64, p64, seq):
    t, dn = hk.shape
    wd = w_down.shape[1]
    tm = _tile(min(seq, 512), min(seq, 512))
    nseq = seq // tm

    def body(x_ref, w_ref, g_ref, c_ref, s_ref, p_ref, ckr_ref, ckv_ref, kr_ref):
        ckr = _dot(x_ref[...], w_ref[...])
        ckr_ref[...] = ckr
        lat = ckr[:, :KV_LORA]
        ckv_ref[...] = (lat * _rms_scale(lat) * g_ref[...]).astype(BF16)
        kr_ref[...] = _rope(ckr[:, KV_LORA:], c_ref[...], s_ref[...], p_ref[...]).astype(BF16)

    pos = BS((tm, ROPE), lambda i: (i % nseq, 0))
    return pl.pallas_call(
        body, name="kv_down", grid=(t // tm,),
        in_specs=[
            BS((tm, dn), lambda i: (i, 0)), BS((dn, wd), lambda i: (0, 0)), BS((1, KV_LORA), lambda i: (0, 0)), pos, pos,
            BS((ROPE, ROPE), lambda i: (0, 0)),
        ],
        out_specs=[BS((tm, wd), lambda i: (i, 0)), BS((tm, KV_LORA), lambda i: (i, 0)), BS((tm, ROPE), lambda i: (i, 0))],
        out_shape=[SDS((t, wd), F32), SDS((t, KV_LORA), BF16), SDS((t, ROPE), BF16)],
        compiler_params=_cparams(1),
    )(hk, w_down, latent_norm, c64, s64, p64)


def _kv_up(ckv, w_up):
    t, kl = ckv.shape
    hb = w_up.shape[-1]
    tm = _tile(t)

    def body(x_ref, w_ref, o_ref):
        xv = x_ref[...]
        for h in range(HEADS_B):
            o_ref[:, h * hb : (h + 1) * hb] = _dot(xv, w_ref[h]).astype(BF16)

    return pl.pallas_call(
        body, name="kv_up", grid=(t // tm,),
        in_specs=[BS((tm, kl), lambda i: (i, 0)), BS((HEADS_B, kl, hb), lambda i: (0, 0, 0))],
        out_specs=BS((tm, HEADS_B * hb), lambda i: (i, 0)), out_shape=SDS((t, HEADS_B * hb), BF16),
        compiler_params=_cparams(1),
    )(ckv, w_up)


def _mla_diagonal_mask(tq):
    rows = lax.broadcasted_iota(jnp.int32, (tq, tq), 0)
    cols = lax.broadcasted_iota(jnp.int32, (tq, tq), 1)
    return jnp.where(jnp.right_shift(cols, 6) <= jnp.right_shift(rows, 6), 0.0, NEG_INF)


def _mla_key_tiles(n_keys, tk):
    return [(slice(k0, min(k0 + tk, n_keys)), min(k0 + tk, n_keys) == n_keys) for k0 in range(0, n_keys, tk)]


def _mla_scores(qi, kt, diagonal):
    s = _dot_nt(qi, kt) * (QK_B ** -0.5)
    if diagonal is None:
        return s
    tq, width = s.shape
    own = s[:, width - tq :] + diagonal
    return own if width == tq else jnp.concatenate([s[:, : width - tq], own], axis=1)


def _mla_fwd(q, kv, kr, bl, seq):
    t = kv.shape[0]
    tq = min(MLA_TQ, seq)

    def body(q_ref, kn_ref, v_ref, kr_ref, o_ref, lse_ref):
        kcat = jnp.concatenate([kn_ref[...], kr_ref[...]], axis=1)
        vv = v_ref[...]
        diagonal = _mla_diagonal_mask(tq)
        for i in range(seq // tq):
            rows = slice(i * tq, (i + 1) * tq)
            qi = q_ref[rows, :]
            m = total = acc = None
            for keys, own in _mla_key_tiles((i + 1) * tq, MLA_TK_FWD):
                s = _mla_scores(qi, kcat[keys], diagonal if own else None)
                m_blk = jnp.max(s, axis=-1, keepdims=True)
                if m is None:
                    m_new = m_blk
                    e = jnp.exp(s - m_new)
                    total = jnp.sum(e, axis=-1, keepdims=True)
                    acc = _dot(e.astype(BF16), vv[keys])
                else:
                    m_new = jnp.maximum(m, m_blk)
                    keep = jnp.exp(m - m_new)
                    e = jnp.exp(s - m_new)
                    total = keep * total + jnp.sum(e, axis=-1, keepdims=True)
                    acc = keep * acc + _dot(e.astype(BF16), vv[keys])
                m = m_new
            o_ref[rows, :] = (acc / total).astype(BF16)
            lse_ref[rows, :] = m + jnp.log(total)

    return pl.pallas_call(
        body, name="mla_fwd", grid=(bl, HEADS_B),
        in_specs=[
            BS((None, seq, QK_B), lambda b, h: (h, b, 0)),
            BS((seq, NOPE), lambda b, h: (b, 2 * h)),
            BS((seq, V_DIM), lambda b, h: (b, 2 * h + 1)),
            BS((seq, ROPE), lambda b, h: (b, 0)),
        ],
        out_specs=[BS((seq, V_DIM), lambda b, h: (b, h)), BS((None, seq, 1), lambda b, h: (h, b, 0))],
        out_shape=[SDS((t, HEADS_B * V_DIM), BF16), SDS((HEADS_B, t, 1), F32)],
        compiler_params=_cparams(2),
    )(q, kv, kv, kr)


def _mla_bwd(q, kv, kr, o, lse, do, c192, s192, p192, bl, seq):
    t = kv.shape[0]
    tq = min(MLA_TQ, seq)

    def body(q_ref, kn_ref, v_ref, kr_ref, o_ref, lse_ref, do_ref, c_ref, s_ref, p_ref, dq_ref, dkv_ref, dkr_ref, dkacc, dvacc):
        h = pl.program_id(1)
        kcat = jnp.concatenate([kn_ref[...], kr_ref[...]], axis=1)
        vv = v_ref[...]
        dkacc[...] = jnp.zeros_like(dkacc)
        dvacc[...] = jnp.zeros_like(dvacc)
        diagonal = _mla_diagonal_mask(tq)
        for i in range(seq // tq):
            rows = slice(i * tq, (i + 1) * tq)
            qi = q_ref[rows, :]
            doi = do_ref[rows, :]
            lse_i = lse_ref[rows, :]
            delta = jnp.sum(doi.astype(F32) * o_ref[rows, :].astype(F32), axis=-1, keepdims=True)
            dq = None
            for keys, own in _mla_key_tiles((i + 1) * tq, MLA_TK_BWD):
                p = jnp.exp(_mla_scores(qi, kcat[keys], diagonal if own else None) - lse_i)
                ds = p * (_dot_nt(doi, vv[keys]) - delta)
                dsb = (ds * (QK_B ** -0.5)).astype(BF16)
                dq_blk = _dot(dsb, kcat[keys])
                dq = dq_blk if dq is None else dq + dq_blk
                dkacc[keys, :] += _dot_tn(dsb, qi)
                dvacc[keys, :] += _dot_tn(p.astype(BF16), doi)
            dq_ref[rows, :] = _rope_bwd(dq, c_ref[rows, :], s_ref[rows, :], p_ref[...]).astype(BF16)
        dk = dkacc[...]
        dkv_ref[:, :NOPE] = dk[:, :NOPE].astype(BF16)
        dkv_ref[:, NOPE:] = dvacc[...].astype(BF16)

        @pl.when(h == 0)
        def _():
            dkr_ref[...] = dk[:, NOPE:]

        @pl.when(h > 0)
        def _():
            dkr_ref[...] += dk[:, NOPE:]

    return pl.pallas_call(
        body, name="mla_bwd", grid=(bl, HEADS_B),
        in_specs=[
            BS((None, seq, QK_B), lambda b, h: (h, b, 0)),
            BS((seq, NOPE), lambda b, h: (b, 2 * h)),
            BS((seq, V_DIM), lambda b, h: (b, 2 * h + 1)),
            BS((seq, ROPE), lambda b, h: (b, 0)),
            BS((seq, V_DIM), lambda b, h: (b, h)),
            BS((None, seq, 1), lambda b, h: (h, b, 0)),
            BS((seq, V_DIM), lambda b, h: (b, h)),
            BS((seq, QK_B), lambda b, h: (0, 0)),
            BS((seq, QK_B), lambda b, h: (0, 0)),
            BS((QK_B, QK_B), lambda b, h: (0, 0)),
        ],
        out_specs=[
            BS((None, seq, QK_B), lambda b, h: (h, b, 0)),
            BS((seq, NOPE + V_DIM), lambda b, h: (b, h)),
            BS((seq, ROPE), lambda b, h: (b, 0)),
        ],
        out_shape=[SDS((HEADS_B, t, QK_B), BF16), SDS((t, HEADS_B * (NOPE + V_DIM)), BF16), SDS((t, ROPE), F32)],
        scratch_shapes=[pltpu.VMEM((seq, QK_B), F32), pltpu.VMEM((seq, V_DIM), F32)],
        compiler_params=_cparams(2),
    )(q, kv, kv, kr, o, lse, do, c192, s192, p192)


def _loss_final(h, target, gamma):
    t, dn = h.shape
    tm = _tile(t)
    nt = t // tm

    def body(h_ref, t_ref, g_ref, dh_ref, dhb_ref, dg_ref, loss_ref):
        i = pl.program_id(0)
        hv = h_ref[...]
        r = _rms_scale(hv)
        hh = hv * r
        gam = g_ref[...]
        err = hh * gam - t_ref[...]
        part = 0.5 * jnp.sum(jnp.mean(err * err, axis=-1, keepdims=True))

        @pl.when(i == 0)
        def _():
            loss_ref[...] = jnp.zeros_like(loss_ref)

        loss_ref[...] += part
        dy = err * (1.0 / dn)
        _acc_rows(dg_ref, dy * hh, i, nt)
        t1 = dy * gam
        dh = r * (t1 - hh * jnp.mean(t1 * hh, axis=-1, keepdims=True))
        dh_ref[...] = dh
        dhb_ref[...] = dh.astype(BF16)

    row = BS((tm, dn), lambda i: (i, 0))
    return pl.pallas_call(
        body, name="loss_final", grid=(nt,),
        in_specs=[row, row, BS((1, dn), lambda i: (0, 0))],
        out_specs=[row, row, BS((8, dn), lambda i: (0, 0)), BS((8, 128), lambda i: (0, 0))],
        out_shape=[SDS((t, dn), F32), SDS((t, dn), BF16), SDS((8, dn), F32), SDS((8, 128), F32)],
        compiler_params=_cparams(1),
    )(h, target, gamma)


def _ffn_bwd_in(name, dh, w_out, layer, gu):
    t, dn = dh.shape
    tm = _tile(t, 1024)

    def body(dh_ref, w_ref, gu_ref, o_ref):
        da = 0.5 * _dot_nt(dh_ref[...], w_ref[...])
        g = gu_ref[0].astype(F32)
        u = gu_ref[1].astype(F32)
        sg = 0.5 * jnp.tanh(0.5 * g) + 0.5
        silu = g * sg
        o_ref[0] = (da * (u * (sg + silu - silu * sg))).astype(BF16)
        o_ref[1] = (da * silu).astype(BF16)

    blk = BS((None, 2, tm, FB), lambda j, i: (j, 0, i, 0))
    return pl.pallas_call(
        body, name=name, grid=(NJ, t // tm),
        in_specs=[BS((tm, dn), lambda j, i: (i, 0)), BS((None, None, FB, dn), lambda j, i: (layer, j, 0, 0)), blk],
        out_specs=blk, out_shape=SDS((NJ, 2, t, FB), BF16),
        compiler_params=_cparams(2),
    )(dh, w_out, gu)


def _mm_nt_plain(name, xf, w):
    t, dn = xf.shape
    n = w.shape[0]
    tm = _tile(t)

    def body(x_ref, w_ref, o_ref):
        o_ref[...] = _dot_nt(x_ref[...], w_ref[...]).astype(BF16)

    return pl.pallas_call(
        body, name=name, grid=(t // tm,),
        in_specs=[BS((tm, dn), lambda i: (i, 0)), BS((n, dn), lambda i: (0, 0))],
        out_specs=BS((tm, n), lambda i: (i, 0)), out_shape=SDS((t, n), BF16),
        compiler_params=_cparams(1),
    )(xf, w)


def _mm_tn(name, xa, x_spec, ya, y_spec, out_shape, out_spec, nj, scale=None):
    def body(x_ref, y_ref, o_ref):
        acc = _dot_tn(x_ref[...], y_ref[...])
        o_ref[...] = (acc if scale is None else scale * acc).astype(BF16)

    return pl.pallas_call(
        body, name=name, grid=(nj,),
        in_specs=[x_spec, y_spec], out_specs=out_spec, out_shape=SDS(out_shape, BF16),
        compiler_params=_cparams(1),
    )(xa, ya)


def _dw_qkv(hn, dqkv3, wb):
    t, dn = hn.shape
    per = wb // 128

    def body(x_ref, *refs):
        cols = [y_ref[...] for y_ref in refs[:per]]
        refs[per][...] = _dot_tn(x_ref[...], jnp.concatenate(cols, axis=1)).astype(BF16)

    def piece(k):
        return BS((None, t, 128), lambda j: ((per * j + k) // 8, 0, (per * j + k) % 8))

    return pl.pallas_call(
        body, name="dw_qkv", grid=(NDEV,),
        in_specs=[BS((t, dn), lambda j: (0, 0))] + [piece(k) for k in range(per)],
        out_specs=BS((None, dn, wb), lambda j: (j, 0, 0)), out_shape=SDS((NDEV, dn, wb), BF16),
        compiler_params=_cparams(1),
    )(hn, *([dqkv3] * per))


def _mm_nt_epi(name, ya, y_spec, wa, w_spec, nj, n_out, extra, out_shapes, out_specs, epilogue, tm, nt, mm_fn=None):
    n_extra = len(extra)
    n_outs = len(out_shapes)

    def body(*refs):
        y_ref, w_ref = refs[:2]
        ex = refs[2 : 2 + n_extra]
        outs = refs[2 + n_extra : 2 + n_extra + n_outs]
        i = pl.program_id(0)
        j = pl.program_id(1)
        part = _dot_nt(y_ref[...], w_ref[...]) if mm_fn is None else mm_fn(y_ref, w_ref)
        if nj == 1:
            epilogue(part, ex, outs, i, nt)
            return
        acc = refs[-1]

        @pl.when(j == 0)
        def _():
            acc[...] = part

        @pl.when(j > 0)
        def _():
            acc[...] += part

        @pl.when(j == nj - 1)
        def _():
            epilogue(acc[...], ex, outs, i, nt)

    return pl.pallas_call(
        body, name=name, grid=(nt, nj),
        in_specs=[y_spec, w_spec] + [spec for _, spec in extra],
        out_specs=out_specs, out_shape=out_shapes,
        scratch_shapes=[] if nj == 1 else [pltpu.VMEM((tm, n_out), F32)],
        compiler_params=_cparams(2),
    )(ya, wa, *[arr for arr, _ in extra])


def _norm_bwd(dn, hv, gam):
    r = _rms_scale(hv)
    hh = hv * r
    t1 = dn * gam
    return r * (t1 - hh * jnp.mean(t1 * hh, axis=-1, keepdims=True)), dn * hh


def _norm_bwd_epilogue(has_res, out_dtype):
    def epilogue(dn, ex, outs, i, nt):
        dh, dg_rows = _norm_bwd(dn, ex[0][...], ex[1][...])
        _acc_rows(outs[1], dg_rows, i, nt)
        if has_res:
            dh = dh + ex[2][...]
        outs[0][...] = dh.astype(out_dtype)
        if has_res:
            outs[2][...] = dh.astype(BF16)

    return epilogue


def _mm_nt_norm_bwd(name, ya, y_spec, wa, w_spec, nj, h, gamma, res, out_dtype, mm_fn=None, want_tm=512, after=None):
    t, n = h.shape
    tm = _tile(t, want_tm)
    nt = t // tm
    row = BS((tm, n), lambda i, j: (i, 0))
    extra = [(h, row), (gamma, BS((1, n), lambda i, j: (0, 0)))]
    out_shapes = [SDS((t, n), out_dtype), SDS((8, n), F32)]
    out_specs = [row, BS((8, n), lambda i, j: (0, 0))]
    if res is not None:
        extra.append((res, row))
        out_shapes.append(SDS((t, n), BF16))
        out_specs.append(row)
    extra.extend((a, BS(memory_space=pl.ANY)) for a in after or ())
    return _mm_nt_epi(
        name, ya, y_spec, wa, w_spec, nj, n, extra, out_shapes, out_specs, _norm_bwd_epilogue(res is not None, out_dtype), tm, nt, mm_fn,
    )


def _dev_block(jj):
    return jj // 2 + NJ * (jj % 2)


def _ffn_dn_mm(y_ref, w_ref):
    acc = None
    for jj in range(2 * NJ):
        part = _dot_nt(y_ref[jj], w_ref[_dev_block(jj)])
        acc = part if acc is None else acc + part
    return acc


def _ffn_bwd(tag, dh, dhb, n_in, h_in, gamma, gu, a, w_in, w_out, collective_id, after):
    t, dn = dh.shape
    dgu = _ffn_bwd_in(f"{tag}_bwd_in", dhb, w_out, 0, gu).reshape(2 * NJ, t, FB)
    dw_out = _mm_tn(
        f"{tag}_dw_out", a, BS((None, t, FB), lambda j: (j, 0, 0)), dhb, BS((t, dn), lambda j: (0, 0)),
        (NJ, FB, dn), BS((None, FB, dn), lambda j: (j, 0, 0)), NJ, scale=0.5,
    )
    dw_in = _mm_tn(
        f"{tag}_dw_in", dgu, BS((None, t, FB), lambda j: (j, 0, 0)), n_in, BS((t, dn), lambda j: (0, 0)),
        (NDEV, FB, dn), BS((None, FB, dn), lambda j: (_dev_block(j), 0, 0)), NDEV,
    )
    entries = [("scatter", dw_in), ("scatter", dw_out.reshape(NDEV, NJ * FB // NDEV, dn))]
    landed = _exchange_sc(f"{tag}_reduce", entries, collective_id, after)
    tm = _tile(t)
    resident = BS((None, NDEV, dn, FB), lambda i, j: (0, 0, 0, 0), pipeline_mode=pl.Buffered(1))
    dh_in, dgam, dhb_in = _mm_nt_norm_bwd(
        f"{tag}_dn", dgu, BS((2 * NJ, tm, FB), lambda i, j: (0, i, 0)), w_in, resident, 1, h_in, gamma, dh, F32, mm_fn=_ffn_dn_mm,
        after=[e[1] for e in entries],
    )
    return dh_in, dhb_in, dgam, landed


def _heads_mm(y_ref, w_ref):
    acc = None
    for h in range(HEADS_B):
        part = _dot_nt(y_ref[h], w_ref[h])
        acc = part if acc is None else acc + part
    return acc


def _dqkv_mm(per):
    def mm(y_ref, w_ref):
        acc = None
        for j in range(NDEV):
            cols = [y_ref[(per * j + k) // 8, :, ((per * j + k) % 8) * 128 : ((per * j + k) % 8 + 1) * 128] for k in range(per)]
            part = _dot_nt(jnp.concatenate(cols, axis=1), w_ref[j])
            acc = part if acc is None else acc + part
        return acc

    return mm


def _kv_latent_bwd(dkv, w_up, ckr, latent_norm, dkr, c64, s64, p64, seq):
    t, wd = ckr.shape
    hb = w_up.shape[-1]
    tm = _tile(min(seq, 512), min(seq, 512))
    nt = t // tm
    nseq = seq // tm

    def epilogue(dn, ex, outs, i, nt_):
        dlat, dg_rows = _norm_bwd(dn, ex[0][...], ex[1][...])
        _acc_rows(outs[1], dg_rows, i, nt_)
        outs[0][:, :KV_LORA] = dlat.astype(BF16)
        outs[0][:, KV_LORA:] = _rope_bwd(ex[2][...], ex[3][...], ex[4][...], ex[5][...]).astype(BF16)

    pos = BS((tm, ROPE), lambda i, j: (i % nseq, 0))
    extra = [
        (ckr, BS((tm, KV_LORA), lambda i, j: (i, 0))), (latent_norm, BS((1, KV_LORA), lambda i, j: (0, 0))),
        (dkr, BS((tm, ROPE), lambda i, j: (i, 0))), (c64, pos), (s64, pos), (p64, BS((ROPE, ROPE), lambda i, j: (0, 0))),
    ]
    def heads_mm(y_ref, w_ref):
        acc = None
        for h in range(HEADS_B):
            part = _dot_nt(y_ref[:, h * hb : (h + 1) * hb], w_ref[h])
            acc = part if acc is None else acc + part
        return acc

    return _mm_nt_epi(
        "kv_latent_bwd", dkv, BS((tm, HEADS_B * hb), lambda i, j: (i, 0)), w_up, BS((HEADS_B, KV_LORA, hb), lambda i, j: (0, 0, 0)),
        1, KV_LORA, extra, [SDS((t, wd), BF16), SDS((8, KV_LORA), F32)],
        [BS((tm, wd), lambda i, j: (i, 0)), BS((8, KV_LORA), lambda i, j: (0, 0))], epilogue, tm, nt, heads_mm,
    )


def _adamw(name, parts, w, m, v):
    n_layers, rows, cols = w.shape
    tr = max(d for d in range(8, min(rows, 256) + 1, 8) if rows % d == 0)
    nb = rows // tr

    def body(*refs):
        p_refs = refs[:n_layers]
        w_ref, m_ref, v_ref, g_ref, d_ref, nm_ref, nv_ref = refs[n_layers : n_layers + 7]
        layer = pl.program_id(0)
        for lp in range(n_layers):

            @pl.when(layer == lp)
            def _():
                g = p_refs[lp][0].astype(F32)
                for k in range(1, NDEV):
                    g = g + p_refs[lp][k].astype(F32)
                g_ref[...] = g

        g = g_ref[...]
        nm = ADAM_B1 * m_ref[...] + (1.0 - ADAM_B1) * g
        nv = ADAM_B2 * v_ref[...] + (1.0 - ADAM_B2) * (g * g)
        nm_ref[...] = nm
        nv_ref[...] = nv
        m_hat = nm / (1.0 - ADAM_B1 ** ADAM_STEP)
        v_hat = nv / (1.0 - ADAM_B2 ** ADAM_STEP)
        d_ref[...] = -ADAM_LR * (m_hat / (jnp.sqrt(v_hat) + ADAM_EPS) + ADAM_WD * w_ref[...])

    def part_spec(lp):
        return BS((NDEV, tr, cols), lambda l, i: (0, jnp.where(l == lp, i, jnp.where(l < lp, 0, nb - 1)), 0))

    row = BS((None, tr, cols), lambda l, i: (l, i, 0))
    return pl.pallas_call(
        body, name=name, grid=(n_layers, nb),
        in_specs=[part_spec(lp) for lp in range(n_layers)] + [row, row, row],
        out_specs=[row] * 4, out_shape=[SDS(w.shape, F32)] * 4,
        compiler_params=_cparams(2),
    )(*parts, w, m, v)


def _pack_small(ffn1_norm, mix_norm, ffn2_norm, kv_norm, final_norm, q_norm, latent_norm, rel_bias, last_row):
    dn = ffn1_norm.shape[-1]

    def rows_of(a, n_rows):
        flat = a.reshape(-1)
        return jnp.pad(flat, (0, n_rows * dn - flat.shape[0])).reshape(n_rows, dn)

    return jnp.concatenate(
        [
            ffn1_norm.reshape(2, dn), mix_norm.reshape(2, dn), ffn2_norm.reshape(2, dn), kv_norm.reshape(1, dn),
            final_norm.reshape(1, dn), rows_of(q_norm, 1), rows_of(latent_norm, 1), rows_of(rel_bias, 5), rows_of(last_row, 1),
        ],
        axis=0,
    )


def _unpack_small(pack):
    dn = pack.shape[-1]
    return dict(
        ffn1_norm=pack[0:2], mix_norm=pack[2:4], ffn2_norm=pack[4:6], kv_norm=pack[6], final_norm=pack[7],
        b_q_norm=pack[8, :Q_LORA].reshape(1, Q_LORA), kv_latent_norm=pack[9, :KV_LORA],
        a_rel_bias=pack[10:15].reshape(-1)[: HEADS_A * NREL].reshape(1, HEADS_A, NREL), last=pack[15],
    )


def kernel(x, ffn1_norm, ffn1_w_in, ffn1_w_out, mix_norm, ffn2_norm, ffn2_w_in, ffn2_w_out, a_w_qkv, a_rel_bias, a_w_o, kv_norm, kv_w_down, kv_latent_norm, kv_w_up, b_w_dq, b_q_norm, b_w_uq, b_w_o, final_norm, loss_target, m_ffn1_norm, m_ffn1_w_in, m_ffn1_w_out, m_mix_norm, m_ffn2_norm, m_ffn2_w_in, m_ffn2_w_out, m_a_w_qkv, m_a_rel_bias, m_a_w_o, m_kv_norm, m_kv_w_down, m_kv_latent_norm, m_kv_w_up, m_b_w_dq, m_b_q_norm, m_b_w_uq, m_b_w_o, m_final_norm, v_ffn1_norm, v_ffn1_w_in, v_ffn1_w_out, v_mix_norm, v_ffn2_norm, v_ffn2_w_in, v_ffn2_w_out, v_a_w_qkv, v_a_rel_bias, v_a_w_o, v_kv_norm, v_kv_w_down, v_kv_latent_norm, v_kv_w_up, v_b_w_dq, v_b_q_norm, v_b_w_uq, v_b_w_o, v_final_norm):
    bl, seq, dn = x.shape
    t = bl * seq
    tm = _tile(t)
    nt = t // tm
    x2 = x.reshape(t, dn)
    target2 = loss_target.reshape(t, dn)

    def gathered(*ws):
        return [("gather", w.astype(BF16)) for w in ws]

    groups = [
        gathered(ffn1_w_in[0]), gathered(ffn1_w_out[0]), gathered(a_w_qkv[0], a_w_o[0]), gathered(ffn2_w_in[0], ffn2_w_out[0]),
        gathered(kv_w_down, kv_w_up), gathered(ffn1_w_in[1], ffn1_w_out[1]), gathered(b_w_dq[0], b_w_uq[0], b_w_o[0]),
        gathered(ffn2_w_in[1], ffn2_w_out[1]),
    ]
    ag = [_exchange_sc(f"gather_{k}", group, GATHER_IDS[k]) for k, group in enumerate(groups)]

    def as_w_in(w):
        return w.reshape(1, NDEV, dn, FB)

    def as_w_out(w):
        return w.reshape(1, NJ, FB, dn)

    c64, s64, p64, c192, s192, p192 = _rope_tables(seq)
    q_norm = b_q_norm.reshape(1, Q_LORA)
    latent_norm = kv_latent_norm.reshape(1, KV_LORA)
    bias = _window_bias(_rel_bias_fwd(jnp.pad(a_rel_bias[0], ((0, 0), (0, NREL_PAD - NREL)))))

    h0, h1, h2, n1, hn, n2, gu1, gu2, a1, a2, w_in1, w_in2, w_out1, w_out2 = ([None, None] for _ in range(14))
    h0[0] = x2
    (n1[0],) = _norm_fwd("norm_x", x2, ffn1_norm[0:1])
    w_in1[0] = as_w_in(ag[0][0])
    gu1[0], a1[0] = _ffn_in("ffn1_in_0", n1[0], w_in1[0], 0)
    w_out1[0] = as_w_out(ag[1][0])
    h1[0], hn[0] = _mm_res_norm("ffn1_out_0", a1[0], w_out1[0], 0, h0[0], mix_norm[0:1], 0.5)
    w_qkv, w_o_a = ag[2]
    qkv_wb = w_qkv.shape[-1]
    w_o_a = w_o_a.reshape(1, 1, dn, dn)
    qkv3 = _qkv_proj("qkv_proj", hn[0], w_qkv)
    o_a, lse_a = _attn_a_fwd(qkv3, bias, bl, seq)
    h2[0], n2[0] = _mm_res_norm("attn_a_out", o_a.reshape(1, t, dn), w_o_a, 0, h1[0], ffn2_norm[0:1], 1.0)
    w_in2[0], w_out2[0] = as_w_in(ag[3][0]), as_w_out(ag[3][1])
    gu2[0], a2[0] = _ffn_in("ffn2_in_0", n2[0], w_in2[0], 0)
    h0[1], hk, n1[1] = _mm_res_norm(
        "ffn2_out_0", a2[0], w_out2[0], 0, h2[0], jnp.concatenate([kv_norm.reshape(1, dn), ffn1_norm[1:2]], axis=0), 0.5
    )
    w_down, w_up = ag[4]
    w_down = w_down.reshape(dn, KV_LORA + ROPE)
    ckr, ckv, kr = _kv_down(hk, w_down, latent_norm, c64, s64, p64, seq)
    kv = _kv_up(ckv, w_up)
    w_in1[1], w_out1[1] = as_w_in(ag[5][0]), as_w_out(ag[5][1])
    gu1[1], a1[1] = _ffn_in("ffn1_in_1", n1[1], w_in1[1], 0)
    h1[1], hn[1] = _mm_res_norm("ffn1_out_1", a1[1], w_out1[1], 0, h0[1], mix_norm[1:2], 0.5)
    w_dq, w_uq, w_o_b = ag[6]
    w_dq = w_dq.reshape(dn, Q_LORA)
    w_o_b = w_o_b.reshape(1, 1, dn, dn)
    cq_pre, cq = _q_down(hn[1], w_dq, q_norm)
    q = _q_up(cq, w_uq, c192, s192, p192, seq)
    o_b, lse_b = _mla_fwd(q, kv, kr, bl, seq)
    h2[1], n2[1] = _mm_res_norm("attn_b_out", o_b.reshape(1, t, dn), w_o_b, 0, h1[1], ffn2_norm[1:2], 1.0)
    w_in2[1], w_out2[1] = as_w_in(ag[7][0]), as_w_out(ag[7][1])
    gu2[1], a2[1] = _ffn_in("ffn2_in_1", n2[1], w_in2[1], 0)
    (h_last,) = _mm_res_norm("ffn2_out_1", a2[1], w_out2[1], 0, h2[1], None, 0.5)
    dh, dhb, dg_final, loss_part = _loss_final(h_last, target2, final_norm.reshape(1, dn))

    dg_ffn1, dg_mix, dg_ffn2, rs_ffn1, rs_ffn2 = ([None, None] for _ in range(5))

    def whole(rows, cols):
        return BS((rows, cols), lambda j: (0, 0))

    def dw_rows(name, xa, ya):
        n = ya.shape[1]
        return _mm_tn(name, xa, whole(t, dn), ya, whole(t, n), (dn, n), whole(dn, n), 1).reshape(NDEV, dn // NDEV, n)

    dh, dhb, dg_ffn2[1], rs_ffn2[1] = _ffn_bwd(
        "ffn2_1", dh, dhb, n2[1], h2[1], ffn2_norm[1:2], gu2[1], a2[1], w_in2[1], w_out2[1], REDUCE_IDS[0], ()
    )
    do_b = _mm_nt_plain("attn_b_do", dhb, w_o_b.reshape(dn, dn))
    dw_o_b = dw_rows("attn_b_dwo", o_b, dhb)
    dq_pre, dkv, dkr = _mla_bwd(q, kv, kr, o_b, lse_b, do_b, c192, s192, p192, bl, seq)
    dw_uq = _mm_tn(
        "dw_uq", cq, whole(t, Q_LORA), dq_pre, BS((None, t, QK_B), lambda j: (j, 0, 0)),
        (HEADS_B, Q_LORA, QK_B), BS((None, Q_LORA, QK_B), lambda j: (j, 0, 0)), HEADS_B,
    )
    dcq_pre, dg_q = _mm_nt_norm_bwd(
        "dcq", dq_pre, BS((HEADS_B, tm, QK_B), lambda i, j: (0, i, 0)), w_uq, BS((HEADS_B, Q_LORA, QK_B), lambda i, j: (0, 0, 0)),
        1, cq_pre, q_norm, None, BF16, mm_fn=_heads_mm,
    )
    dw_dq = dw_rows("dw_dq", hn[1], dcq_pre)
    dh, dg_mix[1], dhb = _mm_nt_norm_bwd(
        "dhn_b", dcq_pre, BS((tm, Q_LORA), lambda i, j: (i, 0)), w_dq, BS((dn, Q_LORA), lambda i, j: (0, 0)),
        1, h1[1], mix_norm[1:2], dh, F32,
    )
    dh, dhb, dg_ffn1[1], rs_ffn1[1] = _ffn_bwd(
        "ffn1_1", dh, dhb, n1[1], h0[1], ffn1_norm[1:2], gu1[1], a1[1], w_in1[1], w_out1[1], REDUCE_IDS[1], rs_ffn2[1][:1]
    )
    dw_up = _mm_tn(
        "dw_up", ckv, whole(t, KV_LORA), dkv, BS((t, NOPE + V_DIM), lambda j: (0, j)),
        (HEADS_B, KV_LORA, NOPE + V_DIM), BS((None, KV_LORA, NOPE + V_DIM), lambda j: (j, 0, 0)), HEADS_B,
    )
    dckr, dg_latent = _kv_latent_bwd(dkv, w_up, ckr, latent_norm, dkr, c64, s64, p64, seq)
    dw_down = dw_rows("dw_down", hk, dckr)
    dh, dg_kv, dhb = _mm_nt_norm_bwd(
        "dhk", dckr, BS((tm, KV_LORA + ROPE), lambda i, j: (i, 0)), w_down, BS((dn, KV_LORA + ROPE), lambda i, j: (0, 0)),
        1, h0[1], kv_norm.reshape(1, dn), dh, F32,
    )
    dh, dhb, dg_ffn2[0], rs_ffn2[0] = _ffn_bwd(
        "ffn2_0", dh, dhb, n2[0], h2[0], ffn2_norm[0:1], gu2[0], a2[0], w_in2[0], w_out2[0], REDUCE_IDS[2], rs_ffn1[1][:1]
    )
    do_a = _mm_nt_plain("attn_a_do", dhb, w_o_a.reshape(dn, dn))
    dw_o_a = dw_rows("attn_a_dwo", o_a, dhb)
    dqkv3, dbias = _attn_a_bwd(qkv3, o_a, lse_a, do_a, bias, bl, seq)
    dw_qkv = _dw_qkv(hn[0], dqkv3, qkv_wb)
    mixer_grads = [dw_o_a, dw_qkv, dw_o_b, dw_uq, dw_dq, dw_up, dw_down]
    dh, dg_mix[0], dhb = _mm_nt_norm_bwd(
        "dhn_a", dqkv3, BS((3, tm, dn), lambda i, j: (0, i, 0)), w_qkv, BS((NDEV, dn, qkv_wb), lambda i, j: (0, 0, 0)),
        1, h1[0], mix_norm[0:1], dh, F32, mm_fn=_dqkv_mm(qkv_wb // 128), after=mixer_grads,
    )
    rs_mixers = _exchange_sc("mixers_reduce", [("scatter", g) for g in mixer_grads], REDUCE_IDS[3], rs_ffn2[0][:1])
    dh, dhb, dg_ffn1[0], rs_ffn1[0] = _ffn_bwd(
        "ffn1_0", dh, dhb, n1[0], h0[0], ffn1_norm[0:1], gu1[0], a1[0], w_in1[0], w_out1[0], REDUCE_IDS[4], rs_mixers[:1]
    )
    grad_x = dh.reshape(bl, seq, dn)
    dtable = _rel_bias_bwd(_window_bias_bwd(dbias))[:, :NREL]

    def update(name, parts, w, m, v):
        shape3 = (len(parts),) + w.shape[-2:]
        parts = [p.reshape((NDEV,) + shape3[1:]) for p in parts]
        outs = _adamw(name, parts, w.reshape(shape3), m.reshape(shape3), v.reshape(shape3))
        return [o.reshape(w.shape) for o in outs]

    res = {}
    r_in2_1, r_out2_1 = rs_ffn2[1]
    r_in1_1, r_out1_1 = rs_ffn1[1]
    r_in2_0, r_out2_0 = rs_ffn2[0]
    r_in1_0, r_out1_0 = rs_ffn1[0]
    r_o_a, r_qkv, r_o_b, r_uq, r_dq, r_up, r_down = rs_mixers
    def update_transposed(name, parts, w, m, v):
        outs = update(name, parts, *[jnp.swapaxes(a, 1, 2) for a in (w, m, v)])
        return [jnp.swapaxes(o, 1, 2) for o in outs]

    res["ffn2_w_in"] = update_transposed("adamw_ffn2_w_in", [r_in2_0, r_in2_1], ffn2_w_in, m_ffn2_w_in, v_ffn2_w_in)
    res["ffn2_w_out"] = update("adamw_ffn2_w_out", [r_out2_0, r_out2_1], ffn2_w_out, m_ffn2_w_out, v_ffn2_w_out)
    res["kv_w_down"] = update("adamw_kv_w_down", [r_down], kv_w_down, m_kv_w_down, v_kv_w_down)
    res["kv_w_up"] = update("adamw_kv_w_up", [r_up], kv_w_up, m_kv_w_up, v_kv_w_up)
    res["b_w_dq"] = update("adamw_b_w_dq", [r_dq], b_w_dq, m_b_w_dq, v_b_w_dq)
    res["b_w_uq"] = update("adamw_b_w_uq", [r_uq], b_w_uq, m_b_w_uq, v_b_w_uq)
    res["b_w_o"] = update("adamw_b_w_o", [r_o_b], b_w_o, m_b_w_o, v_b_w_o)
    res["a_w_qkv"] = update("adamw_a_w_qkv", [r_qkv], a_w_qkv, m_a_w_qkv, v_a_w_qkv)
    res["a_w_o"] = update("adamw_a_w_o", [r_o_a], a_w_o, m_a_w_o, v_a_w_o)

    small = _pack_small(
        jnp.stack([dg_ffn1[0][0], dg_ffn1[1][0]]), jnp.stack([dg_mix[0][0], dg_mix[1][0]]), jnp.stack([dg_ffn2[0][0], dg_ffn2[1][0]]),
        dg_kv[0], dg_final[0], dg_q[0], dg_latent[0], dtable, loss_part[0],
    )
    done = [r[1] for name, r in res.items() if name != "ffn2_w_in"]
    (r_small,) = _exchange("gather_small_grads", [("gather", small)], after=done)
    res["ffn1_w_in"] = update_transposed("adamw_ffn1_w_in", [r_in1_0, r_in1_1], ffn1_w_in, m_ffn1_w_in, v_ffn1_w_in)
    res["ffn1_w_out"] = update("adamw_ffn1_w_out", [r_out1_0, r_out1_1], ffn1_w_out, m_ffn1_w_out, v_ffn1_w_out)
    zero_row = jnp.zeros((dn,), F32)
    packs = [
        _pack_small(f1, mx, f2, kvn, fin, qn, lat, rel, zero_row)
        for f1, mx, f2, kvn, fin, qn, lat, rel in (
            (ffn1_norm, mix_norm, ffn2_norm, kv_norm, final_norm, b_q_norm, kv_latent_norm, a_rel_bias),
            (m_ffn1_norm, m_mix_norm, m_ffn2_norm, m_kv_norm, m_final_norm, m_b_q_norm, m_kv_latent_norm, m_a_rel_bias),
            (v_ffn1_norm, v_mix_norm, v_ffn2_norm, v_kv_norm, v_final_norm, v_b_q_norm, v_kv_latent_norm, v_a_rel_bias),
        )
    ]
    small_out = [_unpack_small(o[0]) for o in _adamw("adamw_small", [r_small], *[p[None] for p in packs])]
    for name in ("ffn1_norm", "mix_norm", "ffn2_norm", "a_rel_bias", "kv_norm", "kv_latent_norm", "b_q_norm", "final_norm"):
        res[name] = [so[name] for so in small_out]
    loss = small_out[0]["last"][0]

    order = [
        "ffn1_norm", "ffn1_w_in", "ffn1_w_out", "mix_norm", "ffn2_norm", "ffn2_w_in", "ffn2_w_out", "a_w_qkv", "a_rel_bias",
        "a_w_o", "kv_norm", "kv_w_down", "kv_latent_norm", "kv_w_up", "b_w_dq", "b_q_norm", "b_w_uq", "b_w_o", "final_norm",
    ]
    return (loss, grad_x, *[res[n][0] for n in order], *[res[n][1] for n in order], *[res[n][2] for n in order], *[res[n][3] for n in order])
```

```python
import jax
import jax.numpy as jnp
import numpy as np
from jax import lax
from jax.experimental import pallas as pl
from jax.experimental.pallas import tpu as pltpu
from jax.experimental.pallas import tpu_sc as plsc

NDEV = 8
D_MODEL = 1024
D_FF = 2816
FB = 2 * D_FF // NDEV
NJ = D_FF // FB
CHUNK = 64
LEFT_CHUNKS = 8
PAD = LEFT_CHUNKS * CHUNK
BAND = PAD + CHUNK
CHUNKS_PER_STEP = 4
WINDOW = PAD + CHUNKS_PER_STEP * CHUNK
STEP_ROWS = CHUNKS_PER_STEP * 2 * CHUNK
MAX_REL = 128
NREL = 2 * MAX_REL + 1
NREL_PAD = 384
HEADS_A = 16
HEADS_B = 8
NOPE = 128
ROPE = 64
QK_B = NOPE + ROPE
V_DIM = 128
Q_LORA = 768
KV_LORA = 256
ROPE_THETA = 10000.0
EPS = 1e-6
NEG_INF = -1e30
MLA_TQ = 256
MLA_TK_FWD = 256
MLA_TK_BWD = 1024
ADAM_LR = 0.001
ADAM_B1 = 0.9
ADAM_B2 = 0.999
ADAM_EPS = 1e-08
ADAM_WD = 0.01
ADAM_STEP = 10
PACK_ROWS = 16
GATHER_IDS = tuple(range(1, 9))
REDUCE_IDS = tuple(range(9, 14))
VMEM_LIMIT_BYTES = 56 * 1024 * 1024

F32 = jnp.float32
BF16 = jnp.bfloat16
SDS = jax.ShapeDtypeStruct
BS = pl.BlockSpec
MESH = pl.DeviceIdType.MESH


def _cparams(n_axes):
    return pltpu.CompilerParams(dimension_semantics=("arbitrary",) * n_axes, vmem_limit_bytes=VMEM_LIMIT_BYTES)


def _tile(t, want=512):
    return want if t % want == 0 else t


def _dot(a, b):
    return jnp.dot(a, b, preferred_element_type=F32)


def _dot_nt(a, b):
    return lax.dot_general(a, b, (((1,), (1,)), ((), ())), preferred_element_type=F32)


def _dot_tn(a, b):
    return lax.dot_general(a, b, (((0,), (0,)), ((), ())), preferred_element_type=F32)


def _split3(a):
    hi = a.astype(BF16)
    rest = a - hi.astype(F32)
    mid = rest.astype(BF16)
    return hi, mid, (rest - mid.astype(F32)).astype(BF16)


def _dot_exact(a, onehot, transposed=False):
    ob = onehot.astype(BF16)
    dot = _dot_nt if transposed else _dot
    hi, mid, lo = _split3(a)
    return dot(hi, ob) + dot(mid, ob) + dot(lo, ob)


def _rms_scale(h):
    return lax.rsqrt(jnp.mean(h * h, axis=-1, keepdims=True) + EPS)


def _acc_rows(ref, val, step, n_steps):
    part = val.reshape(val.shape[0] // 8, 8, val.shape[1]).sum(axis=0)

    @pl.when(step == 0)
    def _():
        ref[...] = part

    @pl.when(step > 0)
    def _():
        ref[...] += part

    @pl.when(step == n_steps - 1)
    def _():
        ref[...] = jnp.broadcast_to(jnp.sum(ref[...], axis=0, keepdims=True), ref.shape)


def _exchange_plan(entries):
    ins = [e[1] for e in entries]
    kinds = [e[0] for e in entries]
    lands = [SDS((NDEV,) + a.shape if k == "gather" else a.shape, a.dtype) for k, a in zip(kinds, ins)]
    return ins, lands, kinds


def _mesh_place():
    x, y, c = lax.axis_index("x"), lax.axis_index("y"), lax.axis_index("c")
    return (x, y, c), 4 * x + 2 * y + c


def _flipped(place, p):
    x, y, c = place
    px = 1 - x if p & 4 else x
    py = 1 - y if p & 2 else y
    pc = 1 - c if p & 1 else c
    return (px, py, pc), 4 * px + 2 * py + pc


def _ends(kind, src_ref, land_ref, origin, target):
    if kind == "gather":
        return src_ref, land_ref.at[origin]
    return src_ref.at[target], land_ref.at[origin]


def _remote(kind, src_ref, land_ref, send_sems, recv_sems, k, p, place, me, arriving):
    peer_pos, peer = _flipped(place, p)
    src, dst = _ends(kind, src_ref, land_ref, me, peer)
    if arriving:
        dst = _ends(kind, src_ref, land_ref, peer, me)[1]
    sem = k * (NDEV - 1) + p - 1
    return pltpu.make_async_remote_copy(
        src_ref=src, dst_ref=dst, send_sem=send_sems.at[sem], recv_sem=recv_sems.at[sem], device_id=peer_pos, device_id_type=MESH,
    )


def _exchange(name, entries, after=()):
    ins, lands, kinds = _exchange_plan(entries)
    n = len(ins)
    after = tuple(after)

    def body(*refs):
        refs = refs[:n] + refs[n + len(after) :]
        in_refs, land_refs = refs[:n], refs[n : 2 * n]
        send_sems, recv_sems, local_sems = refs[2 * n :]
        place, me = _mesh_place()
        local = []
        for k in range(n):
            src, dst = _ends(kinds[k], in_refs[k], land_refs[k], me, me)
            local.append(pltpu.make_async_copy(src, dst, local_sems.at[k]))
            local[-1].start()
        sends = []
        for p in range(1, NDEV):
            for k in range(n):
                sends.append(_remote(kinds[k], in_refs[k], land_refs[k], send_sems, recv_sems, k, p, place, me, False))
                sends[-1].start()
        for p in range(1, NDEV):
            for k in range(n):
                _remote(kinds[k], in_refs[k], land_refs[k], send_sems, recv_sems, k, p, place, me, True).wait_recv()
        for cp in sends:
            cp.wait_send()
        for cp in local:
            cp.wait()

    any_spec = BS(memory_space=pl.ANY)
    return pl.pallas_call(
        body, name=name, out_shape=lands, in_specs=[any_spec] * (n + len(after)), out_specs=[any_spec] * n,
        scratch_shapes=[
            pltpu.SemaphoreType.DMA((n * (NDEV - 1),)), pltpu.SemaphoreType.DMA((n * (NDEV - 1),)), pltpu.SemaphoreType.DMA((n,)),
        ],
    )(*ins, *after)


def _exchange_sc(name, entries, collective_id, after=()):
    ins, lands, kinds = _exchange_plan(entries)
    n = len(ins)
    after = tuple(after)

    def launch(*refs):
        refs = refs[:n] + refs[n + len(after) :]
        in_refs, land_refs = refs[:n], refs[n : 2 * n]
        send_sems, recv_sems, local_sems = refs[2 * n :]
        place, me = _mesh_place()
        barrier = pltpu.get_barrier_semaphore()
        for p in range(1, NDEV):
            pl.semaphore_signal(barrier, inc=1, device_id=_flipped(place, p)[0], device_id_type=MESH)
        pl.semaphore_wait(barrier, NDEV - 1)
        local = []
        for k in range(n):
            src, dst = _ends(kinds[k], in_refs[k], land_refs[k], me, me)
            local.append(pltpu.make_async_copy(src, dst, local_sems.at[k]))
            local[-1].start()
        sends = []
        if all(kind == "gather" for kind in kinds):
            for p in (1, 2, 4, 6):
                for k in range(n):
                    sends.append(_remote(kinds[k], in_refs[k], land_refs[k], send_sems, recv_sems, k, p, place, me, False))
                    sends[-1].start()
            sibling_pos, _ = _flipped(place, 1)
            for f in (2, 4, 6):
                _, origin = _flipped(place, f)
                for k in range(n):
                    _remote(kinds[k], in_refs[k], land_refs[k], send_sems, recv_sems, k, f, place, me, True).wait_recv()
                    sem = k * (NDEV - 1) + f
                    sends.append(
                        pltpu.make_async_remote_copy(
                            src_ref=land_refs[k].at[origin], dst_ref=land_refs[k].at[origin], send_sem=send_sems.at[sem],
                            recv_sem=recv_sems.at[sem], device_id=sibling_pos, device_id_type=MESH,
                        )
                    )
                    sends[-1].start()
            for p in (1, 3, 5, 7):
                for k in range(n):
                    _remote(kinds[k], in_refs[k], land_refs[k], send_sems, recv_sems, k, p, place, me, True).wait_recv()
        else:
            for p in range(1, NDEV):
                for k in range(n):
                    sends.append(_remote(kinds[k], in_refs[k], land_refs[k], send_sems, recv_sems, k, p, place, me, False))
                    sends[-1].start()
            for p in range(1, NDEV):
                for k in range(n):
                    _remote(kinds[k], in_refs[k], land_refs[k], send_sems, recv_sems, k, p, place, me, True).wait_recv()
        for cp in sends:
            cp.wait_send()
        for cp in local:
            cp.wait()

    return pl.kernel(
        launch, out_type=tuple(lands), mesh=plsc.ScalarSubcoreMesh(axis_name="sequencer", num_cores=1), name=name,
        scratch_types=(
            pltpu.SemaphoreType.DMA((n * (NDEV - 1),)), pltpu.SemaphoreType.DMA((n * (NDEV - 1),)), pltpu.SemaphoreType.DMA((n,)),
        ),
        compiler_params=pltpu.CompilerParams(collective_id=collective_id),
    )(*ins, *after)


def _norm_fwd(name, h, gammas):
    t, dn = h.shape
    ng = gammas.shape[0]
    tm = _tile(t)

    def body(h_ref, g_ref, *outs):
        hv = h_ref[...]
        hh = hv * _rms_scale(hv)
        for i, o_ref in enumerate(outs):
            o_ref[...] = (hh * g_ref[i : i + 1, :]).astype(BF16)

    row = BS((tm, dn), lambda i: (i, 0))
    return pl.pallas_call(
        body, name=name, grid=(t // tm,),
        in_specs=[row, BS((ng, dn), lambda i: (0, 0))],
        out_specs=[row] * ng, out_shape=[SDS((t, dn), BF16)] * ng,
        compiler_params=_cparams(1),
    )(h, gammas)


def _ffn_in(name, n, w_in, layer):
    t, dn = n.shape
    tm = _tile(t, 1024)

    def body(n_ref, wg_ref, wu_ref, gu_ref, a_ref):
        xv = n_ref[...]
        g = _dot(xv, wg_ref[...])
        u = _dot(xv, wu_ref[...])
        gu_ref[0] = g.astype(BF16)
        gu_ref[1] = u.astype(BF16)
        a_ref[...] = (g * jax.nn.sigmoid(g) * u).astype(BF16)

    return pl.pallas_call(
        body, name=name, grid=(NJ, t // tm),
        in_specs=[
            BS((tm, dn), lambda j, i: (i, 0)),
            BS((None, None, dn, FB), lambda j, i: (layer, j, 0, 0)),
            BS((None, None, dn, FB), lambda j, i: (layer, j + NJ, 0, 0)),
        ],
        out_specs=[BS((None, 2, tm, FB), lambda j, i: (j, 0, i, 0)), BS((None, tm, FB), lambda j, i: (j, i, 0))],
        out_shape=[SDS((NJ, 2, t, FB), BF16), SDS((NJ, t, FB), BF16)],
        compiler_params=_cparams(2),
    )(n, w_in, w_in)


def _mm_res_norm(name, a, w, layer, h_in, gammas, scale):
    nk, t, kb = a.shape
    dn = w.shape[-1]
    ng = 0 if gammas is None else gammas.shape[0]
    tm = _tile(t)

    def body(*refs):
        a_ref, w_ref, h_ref = refs[:3]
        g_ref = refs[3] if ng else None
        outs = refs[3 + (1 if ng else 0) :]
        acc = _dot(a_ref[0], w_ref[0])
        for k in range(1, nk):
            acc += _dot(a_ref[k], w_ref[k])
        ho = h_ref[...] + scale * acc
        outs[0][...] = ho
        if ng:
            hh = ho * _rms_scale(ho)
            for i in range(ng):
                outs[1 + i][...] = (hh * g_ref[i : i + 1, :]).astype(BF16)

    row = BS((tm, dn), lambda i: (i, 0))
    in_specs = [BS((nk, tm, kb), lambda i: (0, i, 0)), BS((None, nk, kb, dn), lambda i: (layer, 0, 0, 0)), row]
    args = [a, w, h_in]
    if ng:
        in_specs.append(BS((ng, dn), lambda i: (0, 0)))
        args.append(gammas)
    return pl.pallas_call(
        body, name=name, grid=(t // tm,),
        in_specs=in_specs,
        out_specs=[row] * (1 + ng), out_shape=[SDS((t, dn), F32)] + [SDS((t, dn), BF16)] * ng,
        compiler_params=_cparams(1),
    )(*args)


def _qkv_proj(name, hn, w_qkv):
    t, dn = hn.shape
    wb = w_qkv.shape[-1]
    per = wb // 128
    tm = _tile(t)

    def body(x_ref, w_ref, o_ref):
        xv = x_ref[...]
        for j in range(NDEV):
            yv = _dot(xv, w_ref[j]).astype(BF16)
            for i in range(per):
                n = per * j + i
                o_ref[n // 8, :, (n % 8) * 128 : (n % 8 + 1) * 128] = yv[:, i * 128 : (i + 1) * 128]

    return pl.pallas_call(
        body, name=name, grid=(t // tm,),
        in_specs=[BS((tm, dn), lambda i: (i, 0)), BS((NDEV, dn, wb), lambda i: (0, 0, 0))],
        out_specs=BS((3, tm, dn), lambda i: (0, i, 0)), out_shape=SDS((3, t, dn), BF16),
        compiler_params=_cparams(1),
    )(hn, w_qkv)


def _rel_onehot(i):
    r = lax.broadcasted_iota(jnp.int32, (NREL_PAD, BAND), 0)
    j = lax.broadcasted_iota(jnp.int32, (NREL_PAD, BAND), 1)
    idx = jnp.clip(PAD + i - j, -MAX_REL, MAX_REL) + MAX_REL
    return (idx == r).astype(F32)


def _rel_bias_fwd(table):
    def body(t_ref, o_ref):
        i8 = pl.program_id(0)
        for ii in range(8):
            o_ref[:, ii, :] = _dot_exact(t_ref[...], _rel_onehot(i8 * 8 + ii))

    return pl.pallas_call(
        body, name="rel_bias_fwd", grid=(CHUNK // 8,),
        in_specs=[BS((HEADS_A, NREL_PAD), lambda i: (0, 0))],
        out_specs=BS((HEADS_A, 8, BAND), lambda i: (0, i, 0)), out_shape=SDS((HEADS_A, CHUNK, BAND), F32),
        compiler_params=_cparams(1),
    )(table)


def _rel_bias_bwd(dbias):
    def body(d_ref, o_ref):
        i8 = pl.program_id(0)
        acc = jnp.zeros((HEADS_A, NREL_PAD), F32)
        for ii in range(8):
            acc += _dot_exact(d_ref[:, ii, :], _rel_onehot(i8 * 8 + ii), transposed=True)

        @pl.when(i8 == 0)
        def _():
            o_ref[...] = acc

        @pl.when(i8 > 0)
        def _():
            o_ref[...] += acc

    return pl.pallas_call(
        body, name="rel_bias_bwd", grid=(CHUNK // 8,),
        in_specs=[BS((HEADS_A, 8, BAND), lambda i: (0, i, 0))],
        out_specs=BS((HEADS_A, NREL_PAD), lambda i: (0, 0)), out_shape=SDS((HEADS_A, NREL_PAD), F32),
        compiler_params=_cparams(1),
    )(dbias)


def _window_bias(bias):
    b = bias.reshape(HEADS_A // 2, 2, CHUNK, BAND)
    per_chunk = [
        jnp.pad(b, ((0, 0), (0, 0), (0, 0), (cc * CHUNK, WINDOW - BAND - cc * CHUNK)), constant_values=NEG_INF)
        for cc in range(CHUNKS_PER_STEP)
    ]
    return jnp.stack(per_chunk, axis=1).reshape(HEADS_A // 2, STEP_ROWS, WINDOW)


def _window_bias_bwd(dwin):
    d = dwin.reshape(HEADS_A // 2, CHUNKS_PER_STEP, 2, CHUNK, WINDOW)
    return sum(d[:, cc, :, :, cc * CHUNK : cc * CHUNK + BAND] for cc in range(CHUNKS_PER_STEP)).reshape(HEADS_A, CHUNK, BAND)


def _step_rows(xs, lane):
    parts = []
    for cc in range(CHUNKS_PER_STEP):
        xc = xs[cc * CHUNK : (cc + 1) * CHUNK]
        parts.append(jnp.where(lane < 64, xc, jnp.zeros_like(xc)))
        parts.append(jnp.where(lane >= 64, xc, jnp.zeros_like(xc)))
    return jnp.concatenate(parts, axis=0)


def _pair_rows(ys, lane):
    parts = []
    for cc in range(CHUNKS_PER_STEP):
        y0 = ys[(2 * cc) * CHUNK : (2 * cc + 1) * CHUNK]
        y1 = ys[(2 * cc + 1) * CHUNK : (2 * cc + 2) * CHUNK]
        parts.append(jnp.where(lane < 64, y0, y1))
    return jnp.concatenate(parts, axis=0)


def _window_scores(q_rows, kwin, bias_win, first_key):
    s = _dot_nt(q_rows, kwin) * (CHUNK ** -0.5) + bias_win
    if first_key is None:
        return s
    col = lax.broadcasted_iota(jnp.int32, s.shape, 1)
    return jnp.where(col >= first_key, s, NEG_INF)


def _window_loop(n_passes, chunks):
    n_padded = min(PAD // (CHUNKS_PER_STEP * CHUNK), n_passes)
    lax.fori_loop(0, n_padded, lambda it, carry: chunks(it, carry, True), 0, unroll=2)
    if n_passes > n_padded:
        lax.fori_loop(n_padded, n_passes, lambda it, carry: chunks(it, carry, False), 0, unroll=2)


def _attn_a_fwd(qkv3, bias_win, bl, seq):
    t, dn = qkv3.shape[1:]
    npair = dn // 128
    step = CHUNKS_PER_STEP * CHUNK

    def body(q_ref, k_ref, v_ref, b_ref, o_ref, lse_ref, kpad, vpad):
        kpad[0:PAD, :] = jnp.zeros((PAD, 128), BF16)
        vpad[0:PAD, :] = jnp.zeros((PAD, 128), BF16)
        kpad[PAD:, :] = k_ref[...]
        vpad[PAD:, :] = v_ref[...]
        lane = lax.broadcasted_iota(jnp.int32, (CHUNK, 128), 1)

        def chunks(it, carry, padded):
            r0 = pl.multiple_of(it * step, step)
            q_rows = _step_rows(q_ref[pl.ds(r0, step), :], lane)
            s = _window_scores(q_rows, kpad[pl.ds(r0, WINDOW), :], b_ref[...], PAD - r0 if padded else None)
            m = jnp.max(s, axis=-1, keepdims=True)
            e = jnp.exp(s - m)
            total = jnp.sum(e, axis=-1, keepdims=True)
            o_rows = _dot((e * (1.0 / total)).astype(BF16), vpad[pl.ds(r0, WINDOW), :])
            o_ref[pl.ds(r0, step), :] = _pair_rows(o_rows, lane).astype(BF16)
            lse_ref[pl.ds(pl.multiple_of(it * STEP_ROWS, STEP_ROWS), STEP_ROWS), :] = m + jnp.log(total)
            return carry

        _window_loop(seq // step, chunks)

    return pl.pallas_call(
        body, name="attn_a_fwd", grid=(bl, npair),
        in_specs=[
            BS((None, seq, 128), lambda b, h: (0, b, h)),
            BS((None, seq, 128), lambda b, h: (1, b, h)),
            BS((None, seq, 128), lambda b, h: (2, b, h)),
            BS((None, STEP_ROWS, WINDOW), lambda b, h: (h, 0, 0)),
        ],
        out_specs=[BS((seq, 128), lambda b, h: (b, h)), BS((None, 2 * seq, 1), lambda b, h: (h, b, 0))],
        out_shape=[SDS((t, dn), BF16), SDS((npair, 2 * t, 1), F32)],
        scratch_shapes=[pltpu.VMEM((PAD + seq, 128), BF16), pltpu.VMEM((PAD + seq, 128), BF16)],
        compiler_params=_cparams(2),
    )(qkv3, qkv3, qkv3, bias_win)


def _attn_a_bwd(qkv3, out, lse, do, bias_win, bl, seq):
    t, dn = qkv3.shape[1:]
    npair = dn // 128
    step = CHUNKS_PER_STEP * CHUNK

    def body(q_ref, k_ref, v_ref, o_ref, lse_ref, do_ref, b_ref, dqkv_ref, db_ref, kpad, vpad, dkacc, dvacc):
        b = pl.program_id(1)
        kpad[0:PAD, :] = jnp.zeros((PAD, 128), BF16)
        vpad[0:PAD, :] = jnp.zeros((PAD, 128), BF16)
        kpad[PAD:, :] = k_ref[...]
        vpad[PAD:, :] = v_ref[...]
        dkacc[...] = jnp.zeros_like(dkacc)
        dvacc[...] = jnp.zeros_like(dvacc)

        @pl.when(b == 0)
        def _():
            db_ref[...] = jnp.zeros_like(db_ref)

        lane = lax.broadcasted_iota(jnp.int32, (CHUNK, 128), 1)

        def chunks(it, carry, padded):
            r0 = pl.multiple_of(it * step, step)
            q_rows = _step_rows(q_ref[pl.ds(r0, step), :], lane)
            do_rows = _step_rows(do_ref[pl.ds(r0, step), :], lane)
            kwin = kpad[pl.ds(r0, WINDOW), :]
            vwin = vpad[pl.ds(r0, WINDOW), :]
            o_rows = _step_rows(o_ref[pl.ds(r0, step), :], lane)
            delta = jnp.sum(do_rows.astype(F32) * o_rows.astype(F32), axis=-1, keepdims=True)
            lse_rows = lse_ref[pl.ds(pl.multiple_of(it * STEP_ROWS, STEP_ROWS), STEP_ROWS), :]
            p = jnp.exp(_window_scores(q_rows, kwin, b_ref[...], PAD - r0 if padded else None) - lse_rows)
            ds = p * (_dot_nt(do_rows, vwin) - delta)
            db_ref[...] += ds
            dsb = (ds * (CHUNK ** -0.5)).astype(BF16)
            dqkv_ref[0, pl.ds(r0, step), :] = _pair_rows(_dot(dsb, kwin), lane).astype(BF16)
            dkacc[pl.ds(r0, WINDOW), :] += _dot_tn(dsb, q_rows)
            dvacc[pl.ds(r0, WINDOW), :] += _dot_tn(p.astype(BF16), do_rows)
            return carry

        _window_loop(seq // step, chunks)
        dqkv_ref[1] = dkacc[PAD:, :].astype(BF16)
        dqkv_ref[2] = dvacc[PAD:, :].astype(BF16)

    return pl.pallas_call(
        body, name="attn_a_bwd", grid=(npair, bl),
        in_specs=[
            BS((None, seq, 128), lambda h, b: (0, b, h)),
            BS((None, seq, 128), lambda h, b: (1, b, h)),
            BS((None, seq, 128), lambda h, b: (2, b, h)),
            BS((seq, 128), lambda h, b: (b, h)),
            BS((None, 2 * seq, 1), lambda h, b: (h, b, 0)),
            BS((seq, 128), lambda h, b: (b, h)),
            BS((None, STEP_ROWS, WINDOW), lambda h, b: (h, 0, 0)),
        ],
        out_specs=[BS((3, seq, 128), lambda h, b: (0, b, h)), BS((None, STEP_ROWS, WINDOW), lambda h, b: (h, 0, 0))],
        out_shape=[SDS((3, t, dn), BF16), SDS((HEADS_A // 2, STEP_ROWS, WINDOW), F32)],
        scratch_shapes=[
            pltpu.VMEM((PAD + seq, 128), BF16), pltpu.VMEM((PAD + seq, 128), BF16),
            pltpu.VMEM((PAD + seq, 128), F32), pltpu.VMEM((PAD + seq, 128), F32),
        ],
        compiler_params=_cparams(2),
    )(qkv3, qkv3, qkv3, out, lse, do, bias_win)


def _rope_tables(seq):
    half = ROPE // 2
    freqs = ROPE_THETA ** (-jnp.arange(half, dtype=F32) / half)
    ang = jnp.arange(seq, dtype=F32)[:, None] * freqs[None, :]
    cos, sin = jnp.cos(ang), jnp.sin(ang)
    c64 = jnp.concatenate([cos, cos], axis=1)
    s64 = jnp.concatenate([-sin, sin], axis=1)
    c192 = jnp.concatenate([jnp.ones((seq, NOPE), F32), c64], axis=1)
    s192 = jnp.concatenate([jnp.zeros((seq, NOPE), F32), s64], axis=1)
    p64 = np.zeros((ROPE, ROPE), np.float32)
    for col in range(ROPE):
        p64[(col + half) % ROPE, col] = 1.0
    p192 = np.zeros((QK_B, QK_B), np.float32)
    p192[NOPE:, NOPE:] = p64
    return c64, s64, jnp.asarray(p64), c192, s192, jnp.asarray(p192)


def _rope(xv, cos, sin_signed, swap):
    return xv * cos + _dot_exact(xv, swap) * sin_signed


def _rope_bwd(dy, cos, sin_signed, swap):
    return dy * cos + _dot_exact(dy * sin_signed, swap)


def _q_down(hn, w_dq, q_norm):
    t, dn = hn.shape
    ql = w_dq.shape[1]
    tm = _tile(t)

    def body(x_ref, w_ref, g_ref, pre_ref, cq_ref):
        pre = _dot(x_ref[...], w_ref[...])
        pre_ref[...] = pre
        cq_ref[...] = (pre * _rms_scale(pre) * g_ref[...]).astype(BF16)

    return pl.pallas_call(
        body, name="q_down", grid=(t // tm,),
        in_specs=[BS((tm, dn), lambda i: (i, 0)), BS((dn, ql), lambda i: (0, 0)), BS((1, ql), lambda i: (0, 0))],
        out_specs=[BS((tm, ql), lambda i: (i, 0))] * 2, out_shape=[SDS((t, ql), F32), SDS((t, ql), BF16)],
        compiler_params=_cparams(1),
    )(hn, w_dq, q_norm)


def _q_up(cq, w_uq, c192, s192, p192, seq):
    t, ql = cq.shape
    tm = _tile(min(seq, 512), min(seq, 512))
    nseq = seq // tm

    def body(x_ref, w_ref, c_ref, s_ref, p_ref, o_ref):
        xv = x_ref[...]
        for h in range(HEADS_B):
            o_ref[h] = _rope(_dot(xv, w_ref[h]), c_ref[...], s_ref[...], p_ref[...]).astype(BF16)

    pos = BS((tm, QK_B), lambda i: (i % nseq, 0))
    return pl.pallas_call(
        body, name="q_up", grid=(t // tm,),
        in_specs=[
            BS((tm, ql), lambda i: (i, 0)), BS((HEADS_B, ql, QK_B), lambda i: (0, 0, 0)), pos, pos,
            BS((QK_B, QK_B), lambda i: (0, 0)),
        ],
        out_specs=BS((HEADS_B, tm, QK_B), lambda i: (0, i, 0)), out_shape=SDS((HEADS_B, t, QK_B), BF16),
        compiler_params=_cparams(1),
    )(cq, w_uq, c192, s192, p192)


def _kv_down(hk, w_down, latent_norm, c64, s64, p64, seq):
    t, dn = hk.shape
    wd = w_down.shape[1]
    tm = _tile(min(seq, 512), min(seq, 512))
    nseq = seq // tm

    def body(x_ref, w_ref, g_ref, c_ref, s_ref, p_ref, ckr_ref, ckv_ref, kr_ref):
        ckr = _dot(x_ref[...], w_ref[...])
        ckr_ref[...] = ckr
        lat = ckr[:, :KV_LORA]
        ckv_ref[...] = (lat * _rms_scale(lat) * g_ref[...]).astype(BF16)
        kr_ref[...] = _rope(ckr[:, KV_LORA:], c_ref[...], s_ref[...], p_ref[...]).astype(BF16)

    pos = BS((tm, ROPE), lambda i: (i % nseq, 0))
    return pl.pallas_call(
        body, name="kv_down", grid=(t // tm,),
        in_specs=[
            BS((tm, dn), lambda i: (i, 0)), BS((dn, wd), lambda i: (0, 0)), BS((1, KV_LORA), lambda i: (0, 0)), pos, pos,
            BS((ROPE, ROPE), lambda i: (0, 0)),
        ],
        out_specs=[BS((tm, wd), lambda i: (i, 0)), BS((tm, KV_LORA), lambda i: (i, 0)), BS((tm, ROPE), lambda i: (i, 0))],
        out_shape=[SDS((t, wd), F32), SDS((t, KV_LORA), BF16), SDS((t, ROPE), BF16)],
        compiler_params=_cparams(1),
    )(hk, w_down, latent_norm, c64, s64, p64)


def _kv_up(ckv, w_up):
    t, kl = ckv.shape
    hb = w_up.shape[-1]
    tm = _tile(t)

    def body(x_ref, w_ref, o_ref):
        xv = x_ref[...]
        for h in range(HEADS_B):
            o_ref[:, h * hb : (h + 1) * hb] = _dot(xv, w_ref[h]).astype(BF16)

    return pl.pallas_call(
        body, name="kv_up", grid=(t // tm,),
        in_specs=[BS((tm, kl), lambda i: (i, 0)), BS((HEADS_B, kl, hb), lambda i: (0, 0, 0))],
        out_specs=BS((tm, HEADS_B * hb), lambda i: (i, 0)), out_shape=SDS((t, HEADS_B * hb), BF16),
        compiler_params=_cparams(1),
    )(ckv, w_up)


def _mla_diagonal_mask(tq):
    rows = lax.broadcasted_iota(jnp.int32, (tq, tq), 0)
    cols = lax.broadcasted_iota(jnp.int32, (tq, tq), 1)
    return jnp.where(jnp.right_shift(cols, 6) <= jnp.right_shift(rows, 6), 0.0, NEG_INF)


def _mla_key_tiles(n_keys, tk):
    return [(slice(k0, min(k0 + tk, n_keys)), min(k0 + tk, n_keys) == n_keys) for k0 in range(0, n_keys, tk)]


def _mla_scores(qi, kt, diagonal):
    s = _dot_nt(qi, kt) * (QK_B ** -0.5)
    if diagonal is None:
        return s
    tq, width = s.shape
    own = s[:, width - tq :] + diagonal
    return own if width == tq else jnp.concatenate([s[:, : width - tq], own], axis=1)


def _mla_fwd(q, kv, kr, bl, seq):
    t = kv.shape[0]
    tq = min(MLA_TQ, seq)

    def body(q_ref, kn_ref, v_ref, kr_ref, o_ref, lse_ref):
        kcat = jnp.concatenate([kn_ref[...], kr_ref[...]], axis=1)
        vv = v_ref[...]
        diagonal = _mla_diagonal_mask(tq)
        for i in range(seq // tq):
            rows = slice(i * tq, (i + 1) * tq)
            qi = q_ref[rows, :]
            m = total = acc = None
            for keys, own in _mla_key_tiles((i + 1) * tq, MLA_TK_FWD):
                s = _mla_scores(qi, kcat[keys], diagonal if own else None)
                m_blk = jnp.max(s, axis=-1, keepdims=True)
                if m is None:
                    m_new = m_blk
                    e = jnp.exp(s - m_new)
                    total = jnp.sum(e, axis=-1, keepdims=True)
                    acc = _dot(e.astype(BF16), vv[keys])
                else:
                    m_new = jnp.maximum(m, m_blk)
                    keep = jnp.exp(m - m_new)
                    e = jnp.exp(s - m_new)
                    total = keep * total + jnp.sum(e, axis=-1, keepdims=True)
                    acc = keep * acc + _dot(e.astype(BF16), vv[keys])
                m = m_new
            o_ref[rows, :] = (acc / total).astype(BF16)
            lse_ref[rows, :] = m + jnp.log(total)

    return pl.pallas_call(
        body, name="mla_fwd", grid=(bl, HEADS_B),
        in_specs=[
            BS((None, seq, QK_B), lambda b, h: (h, b, 0)),
            BS((seq, NOPE), lambda b, h: (b, 2 * h)),
            BS((seq, V_DIM), lambda b, h: (b, 2 * h + 1)),
            BS((seq, ROPE), lambda b, h: (b, 0)),
        ],
        out_specs=[BS((seq, V_DIM), lambda b, h: (b, h)), BS((None, seq, 1), lambda b, h: (h, b, 0))],
        out_shape=[SDS((t, HEADS_B * V_DIM), BF16), SDS((HEADS_B, t, 1), F32)],
        compiler_params=_cparams(2),
    )(q, kv, kv, kr)


def _mla_bwd(q, kv, kr, o, lse, do, c192, s192, p192, bl, seq):
    t = kv.shape[0]
    tq = min(MLA_TQ, seq)

    def body(q_ref, kn_ref, v_ref, kr_ref, o_ref, lse_ref, do_ref, c_ref, s_ref, p_ref, dq_ref, dkv_ref, dkr_ref, dkacc, dvacc):
        h = pl.program_id(1)
        kcat = jnp.concatenate([kn_ref[...], kr_ref[...]], axis=1)
        vv = v_ref[...]
        dkacc[...] = jnp.zeros_like(dkacc)
        dvacc[...] = jnp.zeros_like(dvacc)
        diagonal = _mla_diagonal_mask(tq)
        for i in range(seq // tq):
            rows = slice(i * tq, (i + 1) * tq)
            qi = q_ref[rows, :]
            doi = do_ref[rows, :]
            lse_i = lse_ref[rows, :]
            delta = jnp.sum(doi.astype(F32) * o_ref[rows, :].astype(F32), axis=-1, keepdims=True)
            dq = None
            for keys, own in _mla_key_tiles((i + 1) * tq, MLA_TK_BWD):
                p = jnp.exp(_mla_scores(qi, kcat[keys], diagonal if own else None) - lse_i)
                ds = p * (_dot_nt(doi, vv[keys]) - delta)
                dsb = (ds * (QK_B ** -0.5)).astype(BF16)
                dq_blk = _dot(dsb, kcat[keys])
                dq = dq_blk if dq is None else dq + dq_blk
                dkacc[keys, :] += _dot_tn(dsb, qi)
                dvacc[keys, :] += _dot_tn(p.astype(BF16), doi)
            dq_ref[rows, :] = _rope_bwd(dq, c_ref[rows, :], s_ref[rows, :], p_ref[...]).astype(BF16)
        dk = dkacc[...]
        dkv_ref[:, :NOPE] = dk[:, :NOPE].astype(BF16)
        dkv_ref[:, NOPE:] = dvacc[...].astype(BF16)

        @pl.when(h == 0)
        def _():
            dkr_ref[...] = dk[:, NOPE:]

        @pl.when(h > 0)
        def _():
            dkr_ref[...] += dk[:, NOPE:]

    return pl.pallas_call(
        body, name="mla_bwd", grid=(bl, HEADS_B),
        in_specs=[
            BS((None, seq, QK_B), lambda b, h: (h, b, 0)),
            BS((seq, NOPE), lambda b, h: (b, 2 * h)),
            BS((seq, V_DIM), lambda b, h: (b, 2 * h + 1)),
            BS((seq, ROPE), lambda b, h: (b, 0)),
            BS((seq, V_DIM), lambda b, h: (b, h)),
            BS((None, seq, 1), lambda b, h: (h, b, 0)),
            BS((seq, V_DIM), lambda b, h: (b, h)),
            BS((seq, QK_B), lambda b, h: (0, 0)),
            BS((seq, QK_B), lambda b, h: (0, 0)),
            BS((QK_B, QK_B), lambda b, h: (0, 0)),
        ],
        out_specs=[
            BS((None, seq, QK_B), lambda b, h: (h, b, 0)),
            BS((seq, NOPE + V_DIM), lambda b, h: (b, h)),
            BS((seq, ROPE), lambda b, h: (b, 0)),
        ],
        out_shape=[SDS((HEADS_B, t, QK_B), BF16), SDS((t, HEADS_B * (NOPE + V_DIM)), BF16), SDS((t, ROPE), F32)],
        scratch_shapes=[pltpu.VMEM((seq, QK_B), F32), pltpu.VMEM((seq, V_DIM), F32)],
        compiler_params=_cparams(2),
    )(q, kv, kv, kr, o, lse, do, c192, s192, p192)


def _loss_final(h, target, gamma):
    t, dn = h.shape
    tm = _tile(t)
    nt = t // tm

    def body(h_ref, t_ref, g_ref, dh_ref, dhb_ref, dg_ref, loss_ref):
        i = pl.program_id(0)
        hv = h_ref[...]
        r = _rms_scale(hv)
        hh = hv * r
        gam = g_ref[...]
        err = hh * gam - t_ref[...]
        part = 0.5 * jnp.sum(jnp.mean(err * err, axis=-1, keepdims=True))

        @pl.when(i == 0)
        def _():
            loss_ref[...] = jnp.zeros_like(loss_ref)

        loss_ref[...] += part
        dy = err * (1.0 / dn)
        _acc_rows(dg_ref, dy * hh, i, nt)
        t1 = dy * gam
        dh = r * (t1 - hh * jnp.mean(t1 * hh, axis=-1, keepdims=True))
        dh_ref[...] = dh
        dhb_ref[...] = dh.astype(BF16)

    row = BS((tm, dn), lambda i: (i, 0))
    return pl.pallas_call(
        body, name="loss_final", grid=(nt,),
        in_specs=[row, row, BS((1, dn), lambda i: (0, 0))],
        out_specs=[row, row, BS((8, dn), lambda i: (0, 0)), BS((8, 128), lambda i: (0, 0))],
        out_shape=[SDS((t, dn), F32), SDS((t, dn), BF16), SDS((8, dn), F32), SDS((8, 128), F32)],
        compiler_params=_cparams(1),
    )(h, target, gamma)


def _ffn_bwd_in(name, dh, w_out, layer, gu):
    t, dn = dh.shape
    tm = _tile(t, 1024)

    def body(dh_ref, w_ref, gu_ref, o_ref):
        da = 0.5 * _dot_nt(dh_ref[...], w_ref[...])
        g = gu_ref[0].astype(F32)
        u = gu_ref[1].astype(F32)
        sg = 0.5 * jnp.tanh(0.5 * g) + 0.5
        silu = g * sg
        o_ref[0] = (da * (u * (sg + silu - silu * sg))).astype(BF16)
        o_ref[1] = (da * silu).astype(BF16)

    blk = BS((None, 2, tm, FB), lambda j, i: (j, 0, i, 0))
    return pl.pallas_call(
        body, name=name, grid=(NJ, t // tm),
        in_specs=[BS((tm, dn), lambda j, i: (i, 0)), BS((None, None, FB, dn), lambda j, i: (layer, j, 0, 0)), blk],
        out_specs=blk, out_shape=SDS((NJ, 2, t, FB), BF16),
        compiler_params=_cparams(2),
    )(dh, w_out, gu)


def _mm_nt_plain(name, xf, w):
    t, dn = xf.shape
    n = w.shape[0]
    tm = _tile(t)

    def body(x_ref, w_ref, o_ref):
        o_ref[...] = _dot_nt(x_ref[...], w_ref[...]).astype(BF16)

    return pl.pallas_call(
        body, name=name, grid=(t // tm,),
        in_specs=[BS((tm, dn), lambda i: (i, 0)), BS((n, dn), lambda i: (0, 0))],
        out_specs=BS((tm, n), lambda i: (i, 0)), out_shape=SDS((t, n), BF16),
        compiler_params=_cparams(1),
    )(xf, w)


def _mm_tn(name, xa, x_spec, ya, y_spec, out_shape, out_spec, nj, scale=None):
    def body(x_ref, y_ref, o_ref):
        acc = _dot_tn(x_ref[...], y_ref[...])
        o_ref[...] = (acc if scale is None else scale * acc).astype(BF16)

    return pl.pallas_call(
        body, name=name, grid=(nj,),
        in_specs=[x_spec, y_spec], out_specs=out_spec, out_shape=SDS(out_shape, BF16),
        compiler_params=_cparams(1),
    )(xa, ya)


def _dw_qkv(hn, dqkv3, wb):
    t, dn = hn.shape
    per = wb // 128

    def body(x_ref, *refs):
        cols = [y_ref[...] for y_ref in refs[:per]]
        refs[per][...] = _dot_tn(x_ref[...], jnp.concatenate(cols, axis=1)).astype(BF16)

    def piece(k):
        return BS((None, t, 128), lambda j: ((per * j + k) // 8, 0, (per * j + k) % 8))

    return pl.pallas_call(
        body, name="dw_qkv", grid=(NDEV,),
        in_specs=[BS((t, dn), lambda j: (0, 0))] + [piece(k) for k in range(per)],
        out_specs=BS((None, dn, wb), lambda j: (j, 0, 0)), out_shape=SDS((NDEV, dn, wb), BF16),
        compiler_params=_cparams(1),
    )(hn, *([dqkv3] * per))


def _mm_nt_epi(name, ya, y_spec, wa, w_spec, nj, n_out, extra, out_shapes, out_specs, epilogue, tm, nt, mm_fn=None):
    n_extra = len(extra)
    n_outs = len(out_shapes)

    def body(*refs):
        y_ref, w_ref = refs[:2]
        ex = refs[2 : 2 + n_extra]
        outs = refs[2 + n_extra : 2 + n_extra + n_outs]
        i = pl.program_id(0)
        j = pl.program_id(1)
        part = _dot_nt(y_ref[...], w_ref[...]) if mm_fn is None else mm_fn(y_ref, w_ref)
        if nj == 1:
            epilogue(part, ex, outs, i, nt)
            return
        acc = refs[-1]

        @pl.when(j == 0)
        def _():
            acc[...] = part

        @pl.when(j > 0)
        def _():
            acc[...] += part

        @pl.when(j == nj - 1)
        def _():
            epilogue(acc[...], ex, outs, i, nt)

    return pl.pallas_call(
        body, name=name, grid=(nt, nj),
        in_specs=[y_spec, w_spec] + [spec for _, spec in extra],
        out_specs=out_specs, out_shape=out_shapes,
        scratch_shapes=[] if nj == 1 else [pltpu.VMEM((tm, n_out), F32)],
        compiler_params=_cparams(2),
    )(ya, wa, *[arr for arr, _ in extra])


def _norm_bwd(dn, hv, gam):
    r = _rms_scale(hv)
    hh = hv * r
    t1 = dn * gam
    return r * (t1 - hh * jnp.mean(t1 * hh, axis=-1, keepdims=True)), dn * hh


def _norm_bwd_epilogue(has_res, out_dtype):
    def epilogue(dn, ex, outs, i, nt):
        dh, dg_rows = _norm_bwd(dn, ex[0][...], ex[1][...])
        _acc_rows(outs[1], dg_rows, i, nt)
        if has_res:
            dh = dh + ex[2][...]
        outs[0][...] = dh.astype(out_dtype)
        if has_res:
            outs[2][...] = dh.astype(BF16)

    return epilogue


def _mm_nt_norm_bwd(name, ya, y_spec, wa, w_spec, nj, h, gamma, res, out_dtype, mm_fn=None, want_tm=512, after=None):
    t, n = h.shape
    tm = _tile(t, want_tm)
    nt = t // tm
    row = BS((tm, n), lambda i, j: (i, 0))
    extra = [(h, row), (gamma, BS((1, n), lambda i, j: (0, 0)))]
    out_shapes = [SDS((t, n), out_dtype), SDS((8, n), F32)]
    out_specs = [row, BS((8, n), lambda i, j: (0, 0))]
    if res is not None:
        extra.append((res, row))
        out_shapes.append(SDS((t, n), BF16))
        out_specs.append(row)
    extra.extend((a, BS(memory_space=pl.ANY)) for a in after or ())
    return _mm_nt_epi(
        name, ya, y_spec, wa, w_spec, nj, n, extra, out_shapes, out_specs, _norm_bwd_epilogue(res is not None, out_dtype), tm, nt, mm_fn,
    )


def _dev_block(jj):
    return jj // 2 + NJ * (jj % 2)


def _ffn_dn_mm(y_ref, w_ref):
    acc = None
    for jj in range(2 * NJ):
        part = _dot_nt(y_ref[jj], w_ref[_dev_block(jj)])
        acc = part if acc is None else acc + part
    return acc


def _ffn_bwd(tag, dh, dhb, n_in, h_in, gamma, gu, a, w_in, w_out, collective_id, after):
    t, dn = dh.shape
    dgu = _ffn_bwd_in(f"{tag}_bwd_in", dhb, w_out, 0, gu).reshape(2 * NJ, t, FB)
    dw_out = _mm_tn(
        f"{tag}_dw_out", a, BS((None, t, FB), lambda j: (j, 0, 0)), dhb, BS((t, dn), lambda j: (0, 0)),
        (NJ, FB, dn), BS((None, FB, dn), lambda j: (j, 0, 0)), NJ, scale=0.5,
    )
    dw_in = _mm_tn(
        f"{tag}_dw_in", dgu, BS((None, t, FB), lambda j: (j, 0, 0)), n_in, BS((t, dn), lambda j: (0, 0)),
        (NDEV, FB, dn), BS((None, FB, dn), lambda j: (_dev_block(j), 0, 0)), NDEV,
    )
    entries = [("scatter", dw_in), ("scatter", dw_out.reshape(NDEV, NJ * FB // NDEV, dn))]
    landed = _exchange_sc(f"{tag}_reduce", entries, collective_id, after)
    tm = _tile(t)
    resident = BS((None, NDEV, dn, FB), lambda i, j: (0, 0, 0, 0), pipeline_mode=pl.Buffered(1))
    dh_in, dgam, dhb_in = _mm_nt_norm_bwd(
        f"{tag}_dn", dgu, BS((2 * NJ, tm, FB), lambda i, j: (0, i, 0)), w_in, resident, 1, h_in, gamma, dh, F32, mm_fn=_ffn_dn_mm,
        after=[e[1] for e in entries],
    )
    return dh_in, dhb_in, dgam, landed


def _heads_mm(y_ref, w_ref):
    acc = None
    for h in range(HEADS_B):
        part = _dot_nt(y_ref[h], w_ref[h])
        acc = part if acc is None else acc + part
    return acc


def _dqkv_mm(per):
    def mm(y_ref, w_ref):
        acc = None
        for j in range(NDEV):
            cols = [y_ref[(per * j + k) // 8, :, ((per * j + k) % 8) * 128 : ((per * j + k) % 8 + 1) * 128] for k in range(per)]
            part = _dot_nt(jnp.concatenate(cols, axis=1), w_ref[j])
            acc = part if acc is None else acc + part
        return acc

    return mm


def _kv_latent_bwd(dkv, w_up, ckr, latent_norm, dkr, c64, s64, p64, seq):
    t, wd = ckr.shape
    hb = w_up.shape[-1]
    tm = _tile(min(seq, 512), min(seq, 512))
    nt = t // tm
    nseq = seq // tm

    def epilogue(dn, ex, outs, i, nt_):
        dlat, dg_rows = _norm_bwd(dn, ex[0][...], ex[1][...])
        _acc_rows(outs[1], dg_rows, i, nt_)
        outs[0][:, :KV_LORA] = dlat.astype(BF16)
        outs[0][:, KV_LORA:] = _rope_bwd(ex[2][...], ex[3][...], ex[4][...], ex[5][...]).astype(BF16)

    pos = BS((tm, ROPE), lambda i, j: (i % nseq, 0))
    extra = [
        (ckr, BS((tm, KV_LORA), lambda i, j: (i, 0))), (latent_norm, BS((1, KV_LORA), lambda i, j: (0, 0))),
        (dkr, BS((tm, ROPE), lambda i, j: (i, 0))), (c64, pos), (s64, pos), (p64, BS((ROPE, ROPE), lambda i, j: (0, 0))),
    ]
    def heads_mm(y_ref, w_ref):
        acc = None
        for h in range(HEADS_B):
            part = _dot_nt(y_ref[:, h * hb : (h + 1) * hb], w_ref[h])
            acc = part if acc is None else acc + part
        return acc

    return _mm_nt_epi(
        "kv_latent_bwd", dkv, BS((tm, HEADS_B * hb), lambda i, j: (i, 0)), w_up, BS((HEADS_B, KV_LORA, hb), lambda i, j: (0, 0, 0)),
        1, KV_LORA, extra, [SDS((t, wd), BF16), SDS((8, KV_LORA), F32)],
        [BS((tm, wd), lambda i, j: (i, 0)), BS((8, KV_LORA), lambda i, j: (0, 0))], epilogue, tm, nt, heads_mm,
    )


def _adamw(name, parts, w, m, v):
    n_layers, rows, cols = w.shape
    tr = max(d for d in range(8, min(rows, 512) + 1, 8) if rows % d == 0)
    nb = rows // tr

    def body(*refs):
        p_refs = refs[:n_layers]
        w_ref, m_ref, v_ref, g_ref, d_ref, nm_ref, nv_ref = refs[n_layers : n_layers + 7]
        layer = pl.program_id(0)
        for lp in range(n_layers):

            @pl.when(layer == lp)
            def _():
                g = p_refs[lp][0].astype(F32)
                for k in range(1, NDEV):
                    g = g + p_refs[lp][k].astype(F32)
                g_ref[...] = g

        g = g_ref[...]
        nm = ADAM_B1 * m_ref[...] + (1.0 - ADAM_B1) * g
        nv = ADAM_B2 * v_ref[...] + (1.0 - ADAM_B2) * (g * g)
        nm_ref[...] = nm
        nv_ref[...] = nv
        m_hat = nm / (1.0 - ADAM_B1 ** ADAM_STEP)
        v_hat = nv / (1.0 - ADAM_B2 ** ADAM_STEP)
        d_ref[...] = -ADAM_LR * (m_hat / (jnp.sqrt(v_hat) + ADAM_EPS) + ADAM_WD * w_ref[...])

    def part_spec(lp):
        return BS((NDEV, tr, cols), lambda l, i: (0, jnp.where(l == lp, i, jnp.where(l < lp, 0, nb - 1)), 0))

    row = BS((None, tr, cols), lambda l, i: (l, i, 0))
    return pl.pallas_call(
        body, name=name, grid=(n_layers, nb),
        in_specs=[part_spec(lp) for lp in range(n_layers)] + [row, row, row],
        out_specs=[row] * 4, out_shape=[SDS(w.shape, F32)] * 4,
        compiler_params=_cparams(2),
    )(*parts, w, m, v)


def _pack_small(ffn1_norm, mix_norm, ffn2_norm, kv_norm, final_norm, q_norm, latent_norm, rel_bias, last_row):
    dn = ffn1_norm.shape[-1]

    def rows_of(a, n_rows):
        flat = a.reshape(-1)
        return jnp.pad(flat, (0, n_rows * dn - flat.shape[0])).reshape(n_rows, dn)

    return jnp.concatenate(
        [
            ffn1_norm.reshape(2, dn), mix_norm.reshape(2, dn), ffn2_norm.reshape(2, dn), kv_norm.reshape(1, dn),
            final_norm.reshape(1, dn), rows_of(q_norm, 1), rows_of(latent_norm, 1), rows_of(rel_bias, 5), rows_of(last_row, 1),
        ],
        axis=0,
    )


def _unpack_small(pack):
    dn = pack.shape[-1]
    return dict(
        ffn1_norm=pack[0:2], mix_norm=pack[2:4], ffn2_norm=pack[4:6], kv_norm=pack[6], final_norm=pack[7],
        b_q_norm=pack[8, :Q_LORA].reshape(1, Q_LORA), kv_latent_norm=pack[9, :KV_LORA],
        a_rel_bias=pack[10:15].reshape(-1)[: HEADS_A * NREL].reshape(1, HEADS_A, NREL), last=pack[15],
    )


def kernel(x, ffn1_norm, ffn1_w_in, ffn1_w_out, mix_norm, ffn2_norm, ffn2_w_in, ffn2_w_out, a_w_qkv, a_rel_bias, a_w_o, kv_norm, kv_w_down, kv_latent_norm, kv_w_up, b_w_dq, b_q_norm, b_w_uq, b_w_o, final_norm, loss_target, m_ffn1_norm, m_ffn1_w_in, m_ffn1_w_out, m_mix_norm, m_ffn2_norm, m_ffn2_w_in, m_ffn2_w_out, m_a_w_qkv, m_a_rel_bias, m_a_w_o, m_kv_norm, m_kv_w_down, m_kv_latent_norm, m_kv_w_up, m_b_w_dq, m_b_q_norm, m_b_w_uq, m_b_w_o, m_final_norm, v_ffn1_norm, v_ffn1_w_in, v_ffn1_w_out, v_mix_norm, v_ffn2_norm, v_ffn2_w_in, v_ffn2_w_out, v_a_w_qkv, v_a_rel_bias, v_a_w_o, v_kv_norm, v_kv_w_down, v_kv_latent_norm, v_kv_w_up, v_b_w_dq, v_b_q_norm, v_b_w_uq, v_b_w_o, v_final_norm):
    bl, seq, dn = x.shape
    t = bl * seq
    tm = _tile(t)
    nt = t // tm
    x2 = x.reshape(t, dn)
    target2 = loss_target.reshape(t, dn)

    def gathered(*ws):
        return [("gather", w.astype(BF16)) for w in ws]

    groups = [
        gathered(ffn1_w_in[0]), gathered(ffn1_w_out[0]), gathered(a_w_qkv[0], a_w_o[0]), gathered(ffn2_w_in[0], ffn2_w_out[0]),
        gathered(kv_w_down, kv_w_up), gathered(ffn1_w_in[1], ffn1_w_out[1]), gathered(b_w_dq[0], b_w_uq[0], b_w_o[0]),
        gathered(ffn2_w_in[1], ffn2_w_out[1]),
    ]
    ag = [_exchange_sc(f"gather_{k}", group, GATHER_IDS[k]) for k, group in enumerate(groups)]

    def as_w_in(w):
        return w.reshape(1, NDEV, dn, FB)

    def as_w_out(w):
        return w.reshape(1, NJ, FB, dn)

    c64, s64, p64, c192, s192, p192 = _rope_tables(seq)
    q_norm = b_q_norm.reshape(1, Q_LORA)
    latent_norm = kv_latent_norm.reshape(1, KV_LORA)
    bias = _window_bias(_rel_bias_fwd(jnp.pad(a_rel_bias[0], ((0, 0), (0, NREL_PAD - NREL)))))

    h0, h1, h2, n1, hn, n2, gu1, gu2, a1, a2, w_in1, w_in2, w_out1, w_out2 = ([None, None] for _ in range(14))
    h0[0] = x2
    (n1[0],) = _norm_fwd("norm_x", x2, ffn1_norm[0:1])
    w_in1[0] = as_w_in(ag[0][0])
    gu1[0], a1[0] = _ffn_in("ffn1_in_0", n1[0], w_in1[0], 0)
    w_out1[0] = as_w_out(ag[1][0])
    h1[0], hn[0] = _mm_res_norm("ffn1_out_0", a1[0], w_out1[0], 0, h0[0], mix_norm[0:1], 0.5)
    w_qkv, w_o_a = ag[2]
    qkv_wb = w_qkv.shape[-1]
    w_o_a = w_o_a.reshape(1, 1, dn, dn)
    qkv3 = _qkv_proj("qkv_proj", hn[0], w_qkv)
    o_a, lse_a = _attn_a_fwd(qkv3, bias, bl, seq)
    h2[0], n2[0] = _mm_res_norm("attn_a_out", o_a.reshape(1, t, dn), w_o_a, 0, h1[0], ffn2_norm[0:1], 1.0)
    w_in2[0], w_out2[0] = as_w_in(ag[3][0]), as_w_out(ag[3][1])
    gu2[0], a2[0] = _ffn_in("ffn2_in_0", n2[0], w_in2[0], 0)
    h0[1], hk, n1[1] = _mm_res_norm(
        "ffn2_out_0", a2[0], w_out2[0], 0, h2[0], jnp.concatenate([kv_norm.reshape(1, dn), ffn1_norm[1:2]], axis=0), 0.5
    )
    w_down, w_up = ag[4]
    w_down = w_down.reshape(dn, KV_LORA + ROPE)
    ckr, ckv, kr = _kv_down(hk, w_down, latent_norm, c64, s64, p64, seq)
    kv = _kv_up(ckv, w_up)
    w_in1[1], w_out1[1] = as_w_in(ag[5][0]), as_w_out(ag[5][1])
    gu1[1], a1[1] = _ffn_in("ffn1_in_1", n1[1], w_in1[1], 0)
    h1[1], hn[1] = _mm_res_norm("ffn1_out_1", a1[1], w_out1[1], 0, h0[1], mix_norm[1:2], 0.5)
    w_dq, w_uq, w_o_b = ag[6]
    w_dq = w_dq.reshape(dn, Q_LORA)
    w_o_b = w_o_b.reshape(1, 1, dn, dn)
    cq_pre, cq = _q_down(hn[1], w_dq, q_norm)
    q = _q_up(cq, w_uq, c192, s192, p192, seq)
    o_b, lse_b = _mla_fwd(q, kv, kr, bl, seq)
    h2[1], n2[1] = _mm_res_norm("attn_b_out", o_b.reshape(1, t, dn), w_o_b, 0, h1[1], ffn2_norm[1:2], 1.0)
    w_in2[1], w_out2[1] = as_w_in(ag[7][0]), as_w_out(ag[7][1])
    gu2[1], a2[1] = _ffn_in("ffn2_in_1", n2[1], w_in2[1], 0)
    (h_last,) = _mm_res_norm("ffn2_out_1", a2[1], w_out2[1], 0, h2[1], None, 0.5)
    dh, dhb, dg_final, loss_part = _loss_final(h_last, target2, final_norm.reshape(1, dn))

    dg_ffn1, dg_mix, dg_ffn2, rs_ffn1, rs_ffn2 = ([None, None] for _ in range(5))

    def whole(rows, cols):
        return BS((rows, cols), lambda j: (0, 0))

    def dw_rows(name, xa, ya):
        n = ya.shape[1]
        return _mm_tn(name, xa, whole(t, dn), ya, whole(t, n), (dn, n), whole(dn, n), 1).reshape(NDEV, dn // NDEV, n)

    dh, dhb, dg_ffn2[1], rs_ffn2[1] = _ffn_bwd(
        "ffn2_1", dh, dhb, n2[1], h2[1], ffn2_norm[1:2], gu2[1], a2[1], w_in2[1], w_out2[1], REDUCE_IDS[0], ()
    )
    do_b = _mm_nt_plain("attn_b_do", dhb, w_o_b.reshape(dn, dn))
    dw_o_b = dw_rows("attn_b_dwo", o_b, dhb)
    dq_pre, dkv, dkr = _mla_bwd(q, kv, kr, o_b, lse_b, do_b, c192, s192, p192, bl, seq)
    dw_uq = _mm_tn(
        "dw_uq", cq, whole(t, Q_LORA), dq_pre, BS((None, t, QK_B), lambda j: (j, 0, 0)),
        (HEADS_B, Q_LORA, QK_B), BS((None, Q_LORA, QK_B), lambda j: (j, 0, 0)), HEADS_B,
    )
    dcq_pre, dg_q = _mm_nt_norm_bwd(
        "dcq", dq_pre, BS((HEADS_B, tm, QK_B), lambda i, j: (0, i, 0)), w_uq, BS((HEADS_B, Q_LORA, QK_B), lambda i, j: (0, 0, 0)),
        1, cq_pre, q_norm, None, BF16, mm_fn=_heads_mm,
    )
    dw_dq = dw_rows("dw_dq", hn[1], dcq_pre)
    dh, dg_mix[1], dhb = _mm_nt_norm_bwd(
        "dhn_b", dcq_pre, BS((tm, Q_LORA), lambda i, j: (i, 0)), w_dq, BS((dn, Q_LORA), lambda i, j: (0, 0)),
        1, h1[1], mix_norm[1:2], dh, F32,
    )
    dh, dhb, dg_ffn1[1], rs_ffn1[1] = _ffn_bwd(
        "ffn1_1", dh, dhb, n1[1], h0[1], ffn1_norm[1:2], gu1[1], a1[1], w_in1[1], w_out1[1], REDUCE_IDS[1], rs_ffn2[1][:1]
    )
    dw_up = _mm_tn(
        "dw_up", ckv, whole(t, KV_LORA), dkv, BS((t, NOPE + V_DIM), lambda j: (0, j)),
        (HEADS_B, KV_LORA, NOPE + V_DIM), BS((None, KV_LORA, NOPE + V_DIM), lambda j: (j, 0, 0)), HEADS_B,
    )
    dckr, dg_latent = _kv_latent_bwd(dkv, w_up, ckr, latent_norm, dkr, c64, s64, p64, seq)
    dw_down = dw_rows("dw_down", hk, dckr)
    dh, dg_kv, dhb = _mm_nt_norm_bwd(
        "dhk", dckr, BS((tm, KV_LORA + ROPE), lambda i, j: (i, 0)), w_down, BS((dn, KV_LORA + ROPE), lambda i, j: (0, 0)),
        1, h0[1], kv_norm.reshape(1, dn), dh, F32,
    )
    dh, dhb, dg_ffn2[0], rs_ffn2[0] = _ffn_bwd(
        "ffn2_0", dh, dhb, n2[0], h2[0], ffn2_norm[0:1], gu2[0], a2[0], w_in2[0], w_out2[0], REDUCE_IDS[2], rs_ffn1[1][:1]
    )
    do_a = _mm_nt_plain("attn_a_do", dhb, w_o_a.reshape(dn, dn))
    dw_o_a = dw_rows("attn_a_dwo", o_a, dhb)
    dqkv3, dbias = _attn_a_bwd(qkv3, o_a, lse_a, do_a, bias, bl, seq)
    dw_qkv = _dw_qkv(hn[0], dqkv3, qkv_wb)
    mixer_grads = [dw_o_a, dw_qkv, dw_o_b, dw_uq, dw_dq, dw_up, dw_down]
    dh, dg_mix[0], dhb = _mm_nt_norm_bwd(
        "dhn_a", dqkv3, BS((3, tm, dn), lambda i, j: (0, i, 0)), w_qkv, BS((NDEV, dn, qkv_wb), lambda i, j: (0, 0, 0)),
        1, h1[0], mix_norm[0:1], dh, F32, mm_fn=_dqkv_mm(qkv_wb // 128), after=mixer_grads,
    )
    rs_mixers = _exchange_sc("mixers_reduce", [("scatter", g) for g in mixer_grads], REDUCE_IDS[3], rs_ffn2[0][:1])
    dh, dhb, dg_ffn1[0], rs_ffn1[0] = _ffn_bwd(
        "ffn1_0", dh, dhb, n1[0], h0[0], ffn1_norm[0:1], gu1[0], a1[0], w_in1[0], w_out1[0], REDUCE_IDS[4], rs_mixers[:1]
    )
    grad_x = dh.reshape(bl, seq, dn)
    dtable = _rel_bias_bwd(_window_bias_bwd(dbias))[:, :NREL]

    def update(name, parts, w, m, v):
        shape3 = (len(parts),) + w.shape[-2:]
        parts = [p.reshape((NDEV,) + shape3[1:]) for p in parts]
        outs = _adamw(name, parts, w.reshape(shape3), m.reshape(shape3), v.reshape(shape3))
        return [o.reshape(w.shape) for o in outs]

    res = {}
    r_in2_1, r_out2_1 = rs_ffn2[1]
    r_in1_1, r_out1_1 = rs_ffn1[1]
    r_in2_0, r_out2_0 = rs_ffn2[0]
    r_in1_0, r_out1_0 = rs_ffn1[0]
    r_o_a, r_qkv, r_o_b, r_uq, r_dq, r_up, r_down = rs_mixers
    def update_transposed(name, parts, w, m, v):
        outs = update(name, parts, *[jnp.swapaxes(a, 1, 2) for a in (w, m, v)])
        return [jnp.swapaxes(o, 1, 2) for o in outs]

    res["ffn2_w_in"] = update_transposed("adamw_ffn2_w_in", [r_in2_0, r_in2_1], ffn2_w_in, m_ffn2_w_in, v_ffn2_w_in)
    res["ffn2_w_out"] = update("adamw_ffn2_w_out", [r_out2_0, r_out2_1], ffn2_w_out, m_ffn2_w_out, v_ffn2_w_out)
    res["kv_w_down"] = update("adamw_kv_w_down", [r_down], kv_w_down, m_kv_w_down, v_kv_w_down)
    res["kv_w_up"] = update("adamw_kv_w_up", [r_up], kv_w_up, m_kv_w_up, v_kv_w_up)
    res["b_w_dq"] = update("adamw_b_w_dq", [r_dq], b_w_dq, m_b_w_dq, v_b_w_dq)
    res["b_w_uq"] = update("adamw_b_w_uq", [r_uq], b_w_uq, m_b_w_uq, v_b_w_uq)
    res["b_w_o"] = update("adamw_b_w_o", [r_o_b], b_w_o, m_b_w_o, v_b_w_o)
    res["a_w_qkv"] = update("adamw_a_w_qkv", [r_qkv], a_w_qkv, m_a_w_qkv, v_a_w_qkv)
    res["a_w_o"] = update("adamw_a_w_o", [r_o_a], a_w_o, m_a_w_o, v_a_w_o)

    small = _pack_small(
        jnp.stack([dg_ffn1[0][0], dg_ffn1[1][0]]), jnp.stack([dg_mix[0][0], dg_mix[1][0]]), jnp.stack([dg_ffn2[0][0], dg_ffn2[1][0]]),
        dg_kv[0], dg_final[0], dg_q[0], dg_latent[0], dtable, loss_part[0],
    )
    done = [r[1] for name, r in res.items() if name != "ffn2_w_in"]
    (r_small,) = _exchange("gather_small_grads", [("gather", small)], after=done)
    res["ffn1_w_in"] = update_transposed("adamw_ffn1_w_in", [r_in1_0, r_in1_1], ffn1_w_in, m_ffn1_w_in, v_ffn1_w_in)
    res["ffn1_w_out"] = update("adamw_ffn1_w_out", [r_out1_0, r_out1_1], ffn1_w_out, m_ffn1_w_out, v_ffn1_w_out)
    zero_row = jnp.zeros((dn,), F32)
    packs = [
        _pack_small(f1, mx, f2, kvn, fin, qn, lat, rel, zero_row)
        for f1, mx, f2, kvn, fin, qn, lat, rel in (
            (ffn1_norm, mix_norm, ffn2_norm, kv_norm, final_norm, b_q_norm, kv_latent_norm, a_rel_bias),
            (m_ffn1_norm, m_mix_norm, m_ffn2_norm, m_kv_norm, m_final_norm, m_b_q_norm, m_kv_latent_norm, m_a_rel_bias),
            (v_ffn1_norm, v_mix_norm, v_ffn2_norm, v_kv_norm, v_final_norm, v_b_q_norm, v_kv_latent_norm, v_a_rel_bias),
        )
    ]
    small_out = [_unpack_small(o[0]) for o in _adamw("adamw_small", [r_small], *[p[None] for p in packs])]
    for name in ("ffn1_norm", "mix_norm", "ffn2_norm", "a_rel_bias", "kv_norm", "kv_latent_norm", "b_q_norm", "final_norm"):
        res[name] = [so[name] for so in small_out]
    loss = small_out[0]["last"][0]

    order = [
        "ffn1_norm", "ffn1_w_in", "ffn1_w_out", "mix_norm", "ffn2_norm", "ffn2_w_in", "ffn2_w_out", "a_w_qkv", "a_rel_bias",
        "a_w_o", "kv_norm", "kv_w_down", "kv_latent_norm", "kv_w_up", "b_w_dq", "b_q_norm", "b_w_uq", "b_w_o", "final_norm",
    ]
    return (loss, grad_x, *[res[n][0] for n in order], *[res[n][1] for n in order], *[res[n][2] for n in order], *[res[n][3] for n in order])
```

```python
import jax
import jax.numpy as jnp
import numpy as np
from jax import lax
from jax.experimental import pallas as pl
from jax.experimental.pallas import tpu as pltpu
from jax.experimental.pallas import tpu_sc as plsc

NDEV = 8
D_MODEL = 1024
D_FF = 2816
FB = 2 * D_FF // NDEV
NJ = D_FF // FB
CHUNK = 64
LEFT_CHUNKS = 8
PAD = LEFT_CHUNKS * CHUNK
BAND = PAD + CHUNK
CHUNKS_PER_STEP = 4
WINDOW = PAD + CHUNKS_PER_STEP * CHUNK
STEP_ROWS = CHUNKS_PER_STEP * 2 * CHUNK
MAX_REL = 128
NREL = 2 * MAX_REL + 1
NREL_PAD = 384
HEADS_A = 16
HEADS_B = 8
NOPE = 128
ROPE = 64
QK_B = NOPE + ROPE
V_DIM = 128
Q_LORA = 768
KV_LORA = 256
ROPE_THETA = 10000.0
EPS = 1e-6
NEG_INF = -1e30
MLA_TQ = 256
MLA_TK_FWD = 256
MLA_TK_BWD = 1024
ADAM_LR = 0.001
ADAM_B1 = 0.9
ADAM_B2 = 0.999
ADAM_EPS = 1e-08
ADAM_WD = 0.01
ADAM_STEP = 10
PACK_ROWS = 16
GATHER_IDS = tuple(range(1, 9))
REDUCE_IDS = tuple(range(9, 14))
SMALL_ID = 14
VMEM_LIMIT_BYTES = 56 * 1024 * 1024

F32 = jnp.float32
BF16 = jnp.bfloat16
SDS = jax.ShapeDtypeStruct
BS = pl.BlockSpec
MESH = pl.DeviceIdType.MESH


def _cparams(n_axes):
    return pltpu.CompilerParams(dimension_semantics=("arbitrary",) * n_axes, vmem_limit_bytes=VMEM_LIMIT_BYTES)


def _tile(t, want=512):
    return want if t % want == 0 else t


def _dot(a, b):
    return jnp.dot(a, b, preferred_element_type=F32)


def _dot_nt(a, b):
    return lax.dot_general(a, b, (((1,), (1,)), ((), ())), preferred_element_type=F32)


def _dot_tn(a, b):
    return lax.dot_general(a, b, (((0,), (0,)), ((), ())), preferred_element_type=F32)


def _split3(a):
    hi = a.astype(BF16)
    rest = a - hi.astype(F32)
    mid = rest.astype(BF16)
    return hi, mid, (rest - mid.astype(F32)).astype(BF16)


def _dot_exact(a, onehot, transposed=False):
    ob = onehot.astype(BF16)
    dot = _dot_nt if transposed else _dot
    hi, mid, lo = _split3(a)
    return dot(hi, ob) + dot(mid, ob) + dot(lo, ob)


def _rms_scale(h):
    return lax.rsqrt(jnp.mean(h * h, axis=-1, keepdims=True) + EPS)


def _acc_rows(ref, val, step, n_steps):
    part = val.reshape(val.shape[0] // 8, 8, val.shape[1]).sum(axis=0)

    @pl.when(step == 0)
    def _():
        ref[...] = part

    @pl.when(step > 0)
    def _():
        ref[...] += part

    @pl.when(step == n_steps - 1)
    def _():
        ref[...] = jnp.broadcast_to(jnp.sum(ref[...], axis=0, keepdims=True), ref.shape)


def _exchange_plan(entries):
    ins = [e[1] for e in entries]
    kinds = [e[0] for e in entries]
    lands = [SDS((NDEV,) + a.shape if k == "gather" else a.shape, a.dtype) for k, a in zip(kinds, ins)]
    return ins, lands, kinds


def _mesh_place():
    x, y, c = lax.axis_index("x"), lax.axis_index("y"), lax.axis_index("c")
    return (x, y, c), 4 * x + 2 * y + c


def _flipped(place, p):
    x, y, c = place
    px = 1 - x if p & 4 else x
    py = 1 - y if p & 2 else y
    pc = 1 - c if p & 1 else c
    return (px, py, pc), 4 * px + 2 * py + pc


def _ends(kind, src_ref, land_ref, origin, target):
    if kind == "gather":
        return src_ref, land_ref.at[origin]
    return src_ref.at[target], land_ref.at[origin]


def _remote(kind, src_ref, land_ref, send_sems, recv_sems, k, p, place, me, arriving):
    peer_pos, peer = _flipped(place, p)
    src, dst = _ends(kind, src_ref, land_ref, me, peer)
    if arriving:
        dst = _ends(kind, src_ref, land_ref, peer, me)[1]
    sem = k * (NDEV - 1) + p - 1
    return pltpu.make_async_remote_copy(
        src_ref=src, dst_ref=dst, send_sem=send_sems.at[sem], recv_sem=recv_sems.at[sem], device_id=peer_pos, device_id_type=MESH,
    )


def _exchange_sc(name, entries, collective_id, after=()):
    ins, lands, kinds = _exchange_plan(entries)
    n = len(ins)
    after = tuple(after)

    def launch(*refs):
        refs = refs[:n] + refs[n + len(after) :]
        in_refs, land_refs = refs[:n], refs[n : 2 * n]
        send_sems, recv_sems, local_sems = refs[2 * n :]
        place, me = _mesh_place()
        barrier = pltpu.get_barrier_semaphore()
        for p in range(1, NDEV):
            pl.semaphore_signal(barrier, inc=1, device_id=_flipped(place, p)[0], device_id_type=MESH)
        pl.semaphore_wait(barrier, NDEV - 1)
        local = []
        for k in range(n):
            src, dst = _ends(kinds[k], in_refs[k], land_refs[k], me, me)
            local.append(pltpu.make_async_copy(src, dst, local_sems.at[k]))
            local[-1].start()
        sends = []
        if all(kind == "gather" for kind in kinds):
            for p in (1, 2, 4, 6):
                for k in range(n):
                    sends.append(_remote(kinds[k], in_refs[k], land_refs[k], send_sems, recv_sems, k, p, place, me, False))
                    sends[-1].start()
            sibling_pos, _ = _flipped(place, 1)
            for f in (2, 4, 6):
                _, origin = _flipped(place, f)
                for k in range(n):
                    _remote(kinds[k], in_refs[k], land_refs[k], send_sems, recv_sems, k, f, place, me, True).wait_recv()
                    sem = k * (NDEV - 1) + f
                    sends.append(
                        pltpu.make_async_remote_copy(
                            src_ref=land_refs[k].at[origin], dst_ref=land_refs[k].at[origin], send_sem=send_sems.at[sem],
                            recv_sem=recv_sems.at[sem], device_id=sibling_pos, device_id_type=MESH,
                        )
                    )
                    sends[-1].start()
            for p in (1, 3, 5, 7):
                for k in range(n):
                    _remote(kinds[k], in_refs[k], land_refs[k], send_sems, recv_sems, k, p, place, me, True).wait_recv()
        else:
            for p in range(1, NDEV):
                for k in range(n):
                    sends.append(_remote(kinds[k], in_refs[k], land_refs[k], send_sems, recv_sems, k, p, place, me, False))
                    sends[-1].start()
            for p in range(1, NDEV):
                for k in range(n):
                    _remote(kinds[k], in_refs[k], land_refs[k], send_sems, recv_sems, k, p, place, me, True).wait_recv()
        for cp in sends:
            cp.wait_send()
        for cp in local:
            cp.wait()

    return pl.kernel(
        launch, out_type=tuple(lands), mesh=plsc.ScalarSubcoreMesh(axis_name="sequencer", num_cores=1), name=name,
        scratch_types=(
            pltpu.SemaphoreType.DMA((n * (NDEV - 1),)), pltpu.SemaphoreType.DMA((n * (NDEV - 1),)), pltpu.SemaphoreType.DMA((n,)),
        ),
        compiler_params=pltpu.CompilerParams(collective_id=collective_id),
    )(*ins, *after)


def _norm_fwd(name, h, gammas):
    t, dn = h.shape
    ng = gammas.shape[0]
    tm = _tile(t)

    def body(h_ref, g_ref, *outs):
        hv = h_ref[...]
        hh = hv * _rms_scale(hv)
        for i, o_ref in enumerate(outs):
            o_ref[...] = (hh * g_ref[i : i + 1, :]).astype(BF16)

    row = BS((tm, dn), lambda i: (i, 0))
    return pl.pallas_call(
        body, name=name, grid=(t // tm,),
        in_specs=[row, BS((ng, dn), lambda i: (0, 0))],
        out_specs=[row] * ng, out_shape=[SDS((t, dn), BF16)] * ng,
        compiler_params=_cparams(1),
    )(h, gammas)


def _ffn_in(name, n, w_in, layer):
    t, dn = n.shape
    tm = _tile(t, 1024)

    def body(n_ref, wg_ref, wu_ref, gu_ref, a_ref):
        xv = n_ref[...]
        g = _dot(xv, wg_ref[...])
        u = _dot(xv, wu_ref[...])
        gu_ref[0] = g.astype(BF16)
        gu_ref[1] = u.astype(BF16)
        a_ref[...] = (g * jax.nn.sigmoid(g) * u).astype(BF16)

    return pl.pallas_call(
        body, name=name, grid=(NJ, t // tm),
        in_specs=[
            BS((tm, dn), lambda j, i: (i, 0)),
            BS((None, None, dn, FB), lambda j, i: (layer, j, 0, 0)),
            BS((None, None, dn, FB), lambda j, i: (layer, j + NJ, 0, 0)),
        ],
        out_specs=[BS((None, 2, tm, FB), lambda j, i: (j, 0, i, 0)), BS((None, tm, FB), lambda j, i: (j, i, 0))],
        out_shape=[SDS((NJ, 2, t, FB), BF16), SDS((NJ, t, FB), BF16)],
        compiler_params=_cparams(2),
    )(n, w_in, w_in)


def _mm_res_norm(name, a, w, layer, h_in, gammas, scale):
    nk, t, kb = a.shape
    dn = w.shape[-1]
    ng = 0 if gammas is None else gammas.shape[0]
    tm = _tile(t)

    def body(*refs):
        a_ref, w_ref, h_ref = refs[:3]
        g_ref = refs[3] if ng else None
        outs = refs[3 + (1 if ng else 0) :]
        acc = _dot(a_ref[0], w_ref[0])
        for k in range(1, nk):
            acc += _dot(a_ref[k], w_ref[k])
        ho = h_ref[...] + scale * acc
        outs[0][...] = ho
        if ng:
            hh = ho * _rms_scale(ho)
            for i in range(ng):
                outs[1 + i][...] = (hh * g_ref[i : i + 1, :]).astype(BF16)

    row = BS((tm, dn), lambda i: (i, 0))
    in_specs = [BS((nk, tm, kb), lambda i: (0, i, 0)), BS((None, nk, kb, dn), lambda i: (layer, 0, 0, 0)), row]
    args = [a, w, h_in]
    if ng:
        in_specs.append(BS((ng, dn), lambda i: (0, 0)))
        args.append(gammas)
    return pl.pallas_call(
        body, name=name, grid=(t // tm,),
        in_specs=in_specs,
        out_specs=[row] * (1 + ng), out_shape=[SDS((t, dn), F32)] + [SDS((t, dn), BF16)] * ng,
        compiler_params=_cparams(1),
    )(*args)


def _qkv_proj(name, hn, w_qkv):
    t, dn = hn.shape
    wb = w_qkv.shape[-1]
    per = wb // 128
    tm = _tile(t)

    def body(x_ref, w_ref, o_ref):
        xv = x_ref[...]
        for j in range(NDEV):
            yv = _dot(xv, w_ref[j]).astype(BF16)
            for i in range(per):
                n = per * j + i
                o_ref[n // 8, :, (n % 8) * 128 : (n % 8 + 1) * 128] = yv[:, i * 128 : (i + 1) * 128]

    return pl.pallas_call(
        body, name=name, grid=(t // tm,),
        in_specs=[BS((tm, dn), lambda i: (i, 0)), BS((NDEV, dn, wb), lambda i: (0, 0, 0))],
        out_specs=BS((3, tm, dn), lambda i: (0, i, 0)), out_shape=SDS((3, t, dn), BF16),
        compiler_params=_cparams(1),
    )(hn, w_qkv)


def _rel_onehot(i):
    r = lax.broadcasted_iota(jnp.int32, (NREL_PAD, BAND), 0)
    j = lax.broadcasted_iota(jnp.int32, (NREL_PAD, BAND), 1)
    idx = jnp.clip(PAD + i - j, -MAX_REL, MAX_REL) + MAX_REL
    return (idx == r).astype(F32)


def _rel_bias_fwd(table):
    def body(t_ref, o_ref):
        i8 = pl.program_id(0)
        for ii in range(8):
            o_ref[:, ii, :] = _dot_exact(t_ref[...], _rel_onehot(i8 * 8 + ii))

    return pl.pallas_call(
        body, name="rel_bias_fwd", grid=(CHUNK // 8,),
        in_specs=[BS((HEADS_A, NREL_PAD), lambda i: (0, 0))],
        out_specs=BS((HEADS_A, 8, BAND), lambda i: (0, i, 0)), out_shape=SDS((HEADS_A, CHUNK, BAND), F32),
        compiler_params=_cparams(1),
    )(table)


def _rel_bias_bwd(dbias):
    def body(d_ref, o_ref):
        i8 = pl.program_id(0)
        acc = jnp.zeros((HEADS_A, NREL_PAD), F32)
        for ii in range(8):
            acc += _dot_exact(d_ref[:, ii, :], _rel_onehot(i8 * 8 + ii), transposed=True)

        @pl.when(i8 == 0)
        def _():
            o_ref[...] = acc

        @pl.when(i8 > 0)
        def _():
            o_ref[...] += acc

    return pl.pallas_call(
        body, name="rel_bias_bwd", grid=(CHUNK // 8,),
        in_specs=[BS((HEADS_A, 8, BAND), lambda i: (0, i, 0))],
        out_specs=BS((HEADS_A, NREL_PAD), lambda i: (0, 0)), out_shape=SDS((HEADS_A, NREL_PAD), F32),
        compiler_params=_cparams(1),
    )(dbias)


def _window_bias(bias):
    b = bias.reshape(HEADS_A // 2, 2, CHUNK, BAND)
    per_chunk = [
        jnp.pad(b, ((0, 0), (0, 0), (0, 0), (cc * CHUNK, WINDOW - BAND - cc * CHUNK)), constant_values=NEG_INF)
        for cc in range(CHUNKS_PER_STEP)
    ]
    return jnp.stack(per_chunk, axis=1).reshape(HEADS_A // 2, STEP_ROWS, WINDOW)


def _window_bias_bwd(dwin):
    d = dwin.reshape(HEADS_A // 2, CHUNKS_PER_STEP, 2, CHUNK, WINDOW)
    return sum(d[:, cc, :, :, cc * CHUNK : cc * CHUNK + BAND] for cc in range(CHUNKS_PER_STEP)).reshape(HEADS_A, CHUNK, BAND)


def _step_rows(xs, lane):
    parts = []
    for cc in range(CHUNKS_PER_STEP):
        xc = xs[cc * CHUNK : (cc + 1) * CHUNK]
        parts.append(jnp.where(lane < 64, xc, jnp.zeros_like(xc)))
        parts.append(jnp.where(lane >= 64, xc, jnp.zeros_like(xc)))
    return jnp.concatenate(parts, axis=0)


def _pair_rows(ys, lane):
    parts = []
    for cc in range(CHUNKS_PER_STEP):
        y0 = ys[(2 * cc) * CHUNK : (2 * cc + 1) * CHUNK]
        y1 = ys[(2 * cc + 1) * CHUNK : (2 * cc + 2) * CHUNK]
        parts.append(jnp.where(lane < 64, y0, y1))
    return jnp.concatenate(parts, axis=0)


def _window_scores(q_rows, kwin, bias_win, first_key):
    s = _dot_nt(q_rows, kwin) * (CHUNK ** -0.5) + bias_win
    if first_key is None:
        return s
    col = lax.broadcasted_iota(jnp.int32, s.shape, 1)
    return jnp.where(col >= first_key, s, NEG_INF)


def _window_loop(n_passes, chunks):
    n_padded = min(PAD // (CHUNKS_PER_STEP * CHUNK), n_passes)
    lax.fori_loop(0, n_padded, lambda it, carry: chunks(it, carry, True), 0, unroll=2)
    if n_passes > n_padded:
        lax.fori_loop(n_padded, n_passes, lambda it, carry: chunks(it, carry, False), 0, unroll=2)


def _attn_a_fwd(qkv3, bias_win, bl, seq):
    t, dn = qkv3.shape[1:]
    npair = dn // 128
    step = CHUNKS_PER_STEP * CHUNK

    def body(q_ref, k_ref, v_ref, b_ref, o_ref, lse_ref, kpad, vpad):
        kpad[0:PAD, :] = jnp.zeros((PAD, 128), BF16)
        vpad[0:PAD, :] = jnp.zeros((PAD, 128), BF16)
        kpad[PAD:, :] = k_ref[...]
        vpad[PAD:, :] = v_ref[...]
        lane = lax.broadcasted_iota(jnp.int32, (CHUNK, 128), 1)

        def chunks(it, carry, padded):
            r0 = pl.multiple_of(it * step, step)
            q_rows = _step_rows(q_ref[pl.ds(r0, step), :], lane)
            s = _window_scores(q_rows, kpad[pl.ds(r0, WINDOW), :], b_ref[...], PAD - r0 if padded else None)
            m = jnp.max(s, axis=-1, keepdims=True)
            e = jnp.exp(s - m)
            total = jnp.sum(e, axis=-1, keepdims=True)
            o_rows = _dot((e * (1.0 / total)).astype(BF16), vpad[pl.ds(r0, WINDOW), :])
            o_ref[pl.ds(r0, step), :] = _pair_rows(o_rows, lane).astype(BF16)
            lse_ref[pl.ds(pl.multiple_of(it * STEP_ROWS, STEP_ROWS), STEP_ROWS), :] = m + jnp.log(total)
            return carry

        _window_loop(seq // step, chunks)

    return pl.pallas_call(
        body, name="attn_a_fwd", grid=(bl, npair),
        in_specs=[
            BS((None, seq, 128), lambda b, h: (0, b, h)),
            BS((None, seq, 128), lambda b, h: (1, b, h)),
            BS((None, seq, 128), lambda b, h: (2, b, h)),
            BS((None, STEP_ROWS, WINDOW), lambda b, h: (h, 0, 0)),
        ],
        out_specs=[BS((seq, 128), lambda b, h: (b, h)), BS((None, 2 * seq, 1), lambda b, h: (h, b, 0))],
        out_shape=[SDS((t, dn), BF16), SDS((npair, 2 * t, 1), F32)],
        scratch_shapes=[pltpu.VMEM((PAD + seq, 128), BF16), pltpu.VMEM((PAD + seq, 128), BF16)],
        compiler_params=_cparams(2),
    )(qkv3, qkv3, qkv3, bias_win)


def _attn_a_bwd(qkv3, out, lse, do, bias_win, bl, seq):
    t, dn = qkv3.shape[1:]
    npair = dn // 128
    step = CHUNKS_PER_STEP * CHUNK

    def body(q_ref, k_ref, v_ref, o_ref, lse_ref, do_ref, b_ref, dqkv_ref, db_ref, kpad, vpad, dkacc, dvacc):
        b = pl.program_id(1)
        kpad[0:PAD, :] = jnp.zeros((PAD, 128), BF16)
        vpad[0:PAD, :] = jnp.zeros((PAD, 128), BF16)
        kpad[PAD:, :] = k_ref[...]
        vpad[PAD:, :] = v_ref[...]
        dkacc[...] = jnp.zeros_like(dkacc)
        dvacc[...] = jnp.zeros_like(dvacc)

        @pl.when(b == 0)
        def _():
            db_ref[...] = jnp.zeros_like(db_ref)

        lane = lax.broadcasted_iota(jnp.int32, (CHUNK, 128), 1)

        def chunks(it, carry, padded):
            r0 = pl.multiple_of(it * step, step)
            q_rows = _step_rows(q_ref[pl.ds(r0, step), :], lane)
            do_rows = _step_rows(do_ref[pl.ds(r0, step), :], lane)
            kwin = kpad[pl.ds(r0, WINDOW), :]
            vwin = vpad[pl.ds(r0, WINDOW), :]
            o_rows = _step_rows(o_ref[pl.ds(r0, step), :], lane)
            delta = jnp.sum(do_rows.astype(F32) * o_rows.astype(F32), axis=-1, keepdims=True)
            lse_rows = lse_ref[pl.ds(pl.multiple_of(it * STEP_ROWS, STEP_ROWS), STEP_ROWS), :]
            p = jnp.exp(_window_scores(q_rows, kwin, b_ref[...], PAD - r0 if padded else None) - lse_rows)
            ds = p * (_dot_nt(do_rows, vwin) - delta)
            db_ref[...] += ds
            dsb = (ds * (CHUNK ** -0.5)).astype(BF16)
            dqkv_ref[0, pl.ds(r0, step), :] = _pair_rows(_dot(dsb, kwin), lane).astype(BF16)
            dkacc[pl.ds(r0, WINDOW), :] += _dot_tn(dsb, q_rows)
            dvacc[pl.ds(r0, WINDOW), :] += _dot_tn(p.astype(BF16), do_rows)
            return carry

        _window_loop(seq // step, chunks)
        dqkv_ref[1] = dkacc[PAD:, :].astype(BF16)
        dqkv_ref[2] = dvacc[PAD:, :].astype(BF16)

    return pl.pallas_call(
        body, name="attn_a_bwd", grid=(npair, bl),
        in_specs=[
            BS((None, seq, 128), lambda h, b: (0, b, h)),
            BS((None, seq, 128), lambda h, b: (1, b, h)),
            BS((None, seq, 128), lambda h, b: (2, b, h)),
            BS((seq, 128), lambda h, b: (b, h)),
            BS((None, 2 * seq, 1), lambda h, b: (h, b, 0)),
            BS((seq, 128), lambda h, b: (b, h)),
            BS((None, STEP_ROWS, WINDOW), lambda h, b: (h, 0, 0)),
        ],
        out_specs=[BS((3, seq, 128), lambda h, b: (0, b, h)), BS((None, STEP_ROWS, WINDOW), lambda h, b: (h, 0, 0))],
        out_shape=[SDS((3, t, dn), BF16), SDS((HEADS_A // 2, STEP_ROWS, WINDOW), F32)],
        scratch_shapes=[
            pltpu.VMEM((PAD + seq, 128), BF16), pltpu.VMEM((PAD + seq, 128), BF16),
            pltpu.VMEM((PAD + seq, 128), F32), pltpu.VMEM((PAD + seq, 128), F32),
        ],
        compiler_params=_cparams(2),
    )(qkv3, qkv3, qkv3, out, lse, do, bias_win)


def _rope_tables(seq):
    half = ROPE // 2
    freqs = ROPE_THETA ** (-jnp.arange(half, dtype=F32) / half)
    ang = jnp.arange(seq, dtype=F32)[:, None] * freqs[None, :]
    cos, sin = jnp.cos(ang), jnp.sin(ang)
    c64 = jnp.concatenate([cos, cos], axis=1)
    s64 = jnp.concatenate([-sin, sin], axis=1)
    c192 = jnp.concatenate([jnp.ones((seq, NOPE), F32), c64], axis=1)
    s192 = jnp.concatenate([jnp.zeros((seq, NOPE), F32), s64], axis=1)
    p64 = np.zeros((ROPE, ROPE), np.float32)
    for col in range(ROPE):
        p64[(col + half) % ROPE, col] = 1.0
    p192 = np.zeros((QK_B, QK_B), np.float32)
    p192[NOPE:, NOPE:] = p64
    return c64, s64, jnp.asarray(p64), c192, s192, jnp.asarray(p192)


def _rope(xv, cos, sin_signed, swap):
    return xv * cos + _dot_exact(xv, swap) * sin_signed


def _rope_bwd(dy, cos, sin_signed, swap):
    return dy * cos + _dot_exact(dy * sin_signed, swap)


def _q_down(hn, w_dq, q_norm):
    t, dn = hn.shape
    ql = w_dq.shape[1]
    tm = _tile(t)

    def body(x_ref, w_ref, g_ref, pre_ref, cq_ref):
        pre = _dot(x_ref[...], w_ref[...])
        pre_ref[...] = pre
        cq_ref[...] = (pre * _rms_scale(pre) * g_ref[...]).astype(BF16)

    return pl.pallas_call(
        body, name="q_down", grid=(t // tm,),
        in_specs=[BS((tm, dn), lambda i: (i, 0)), BS((dn, ql), lambda i: (0, 0)), BS((1, ql), lambda i: (0, 0))],
        out_specs=[BS((tm, ql), lambda i: (i, 0))] * 2, out_shape=[SDS((t, ql), F32), SDS((t, ql), BF16)],
        compiler_params=_cparams(1),
    )(hn, w_dq, q_norm)


def _q_up(cq, w_uq, c192, s192, p192, seq):
    t, ql = cq.shape
    tm = _tile(min(seq, 512), min(seq, 512))
    nseq = seq // tm

    def body(x_ref, w_ref, c_ref, s_ref, p_ref, o_ref):
        xv = x_ref[...]
        for h in range(HEADS_B):
            o_ref[h] = _rope(_dot(xv, w_ref[h]), c_ref[...], s_ref[...], p_ref[...]).astype(BF16)

    pos = BS((tm, QK_B), lambda i: (i % nseq, 0))
    return pl.pallas_call(
        body, name="q_up", grid=(t // tm,),
        in_specs=[
            BS((tm, ql), lambda i: (i, 0)), BS((HEADS_B, ql, QK_B), lambda i: (0, 0, 0)), pos, pos,
            BS((QK_B, QK_B), lambda i: (0, 0)),
        ],
        out_specs=BS((HEADS_B, tm, QK_B), lambda i: (0, i, 0)), out_shape=SDS((HEADS_B, t, QK_B), BF16),
        compiler_params=_cparams(1),
    )(cq, w_uq, c192, s192, p192)


def _kv_down(hk, w_down, latent_norm, c64, s64, p64, seq):
    t, dn = hk.shape
    wd = w_down.shape[1]
    tm = _tile(min(seq, 512), min(seq, 512))
    nseq = seq // tm

    def body(x_ref, w_ref, g_ref, c_ref, s_ref, p_ref, ckr_ref, ckv_ref, kr_ref):
        ckr = _dot(x_ref[...], w_ref[...])
        ckr_ref[...] = ckr
        lat = ckr[:, :KV_LORA]
        ckv_ref[...] = (lat * _rms_scale(lat) * g_ref[...]).astype(BF16)
        kr_ref[...] = _rope(ckr[:, KV_LORA:], c_ref[...], s_ref[...], p_ref[...]).astype(BF16)

    pos = BS((tm, ROPE), lambda i: (i % nseq, 0))
    return pl.pallas_call(
        body, name="kv_down", grid=(t // tm,),
        in_specs=[
            BS((tm, dn), lambda i: (i, 0)), BS((dn, wd), lambda i: (0, 0)), BS((1, KV_LORA), lambda i: (0, 0)), pos, pos,
            BS((ROPE, ROPE), lambda i: (0, 0)),
        ],
        out_specs=[BS((tm, wd), lambda i: (i, 0)), BS((tm, KV_LORA), lambda i: (i, 0)), BS((tm, ROPE), lambda i: (i, 0))],
        out_shape=[SDS((t, wd), F32), SDS((t, KV_LORA), BF16), SDS((t, ROPE), BF16)],
        compiler_params=_cparams(1),
    )(hk, w_down, latent_norm, c64, s64, p64)


def _kv_up(ckv, w_up):
    t, kl = ckv.shape
    hb = w_up.shape[-1]
    tm = _tile(t)

    def body(x_ref, w_ref, o_ref):
        xv = x_ref[...]
        for h in range(HEADS_B):
            o_ref[:, h * hb : (h + 1) * hb] = _dot(xv, w_ref[h]).astype(BF16)

    return pl.pallas_call(
        body, name="kv_up", grid=(t // tm,),
        in_specs=[BS((tm, kl), lambda i: (i, 0)), BS((HEADS_B, kl, hb), lambda i: (0, 0, 0))],
        out_specs=BS((tm, HEADS_B * hb), lambda i: (i, 0)), out_shape=SDS((t, HEADS_B * hb), BF16),
        compiler_params=_cparams(1),
    )(ckv, w_up)


def _mla_diagonal_mask(tq):
    rows = lax.broadcasted_iota(jnp.int32, (tq, tq), 0)
    cols = lax.broadcasted_iota(jnp.int32, (tq, tq), 1)
    return jnp.where(jnp.right_shift(cols, 6) <= jnp.right_shift(rows, 6), 0.0, NEG_INF)


def _mla_key_tiles(n_keys, tk):
    return [(slice(k0, min(k0 + tk, n_keys)), min(k0 + tk, n_keys) == n_keys) for k0 in range(0, n_keys, tk)]


def _mla_scores(qi, kt, diagonal):
    s = _dot_nt(qi, kt) * (QK_B ** -0.5)
    if diagonal is None:
        return s
    tq, width = s.shape
    own = s[:, width - tq :] + diagonal
    return own if width == tq else jnp.concatenate([s[:, : width - tq], own], axis=1)


def _mla_fwd(q, kv, kr, bl, seq):
    t = kv.shape[0]
    tq = min(MLA_TQ, seq)

    def body(q_ref, kn_ref, v_ref, kr_ref, o_ref, lse_ref):
        kcat = jnp.concatenate([kn_ref[...], kr_ref[...]], axis=1)
        vv = v_ref[...]
        diagonal = _mla_diagonal_mask(tq)
        for i in range(seq // tq):
            rows = slice(i * tq, (i + 1) * tq)
            qi = q_ref[rows, :]
            m = total = acc = None
            for keys, own in _mla_key_tiles((i + 1) * tq, MLA_TK_FWD):
                s = _mla_scores(qi, kcat[keys], diagonal if own else None)
                m_blk = jnp.max(s, axis=-1, keepdims=True)
                if m is None:
                    m_new = m_blk
                    e = jnp.exp(s - m_new)
                    total = jnp.sum(e, axis=-1, keepdims=True)
                    acc = _dot(e.astype(BF16), vv[keys])
                else:
                    m_new = jnp.maximum(m, m_blk)
                    keep = jnp.exp(m - m_new)
                    e = jnp.exp(s - m_new)
                    total = keep * total + jnp.sum(e, axis=-1, keepdims=True)
                    acc = keep * acc + _dot(e.astype(BF16), vv[keys])
                m = m_new
            o_ref[rows, :] = (acc / total).astype(BF16)
            lse_ref[rows, :] = m + jnp.log(total)

    return pl.pallas_call(
        body, name="mla_fwd", grid=(bl, HEADS_B),
        in_specs=[
            BS((None, seq, QK_B), lambda b, h: (h, b, 0)),
            BS((seq, NOPE), lambda b, h: (b, 2 * h)),
            BS((seq, V_DIM), lambda b, h: (b, 2 * h + 1)),
            BS((seq, ROPE), lambda b, h: (b, 0)),
        ],
        out_specs=[BS((seq, V_DIM), lambda b, h: (b, h)), BS((None, seq, 1), lambda b, h: (h, b, 0))],
        out_shape=[SDS((t, HEADS_B * V_DIM), BF16), SDS((HEADS_B, t, 1), F32)],
        compiler_params=_cparams(2),
    )(q, kv, kv, kr)


def _mla_bwd(q, kv, kr, o, lse, do, c192, s192, p192, bl, seq):
    t = kv.shape[0]
    tq = min(MLA_TQ, seq)

    def body(q_ref, kn_ref, v_ref, kr_ref, o_ref, lse_ref, do_ref, c_ref, s_ref, p_ref, dq_ref, dkv_ref, dkr_ref, dkacc, dvacc):
        h = pl.program_id(1)
        kcat = jnp.concatenate([kn_ref[...], kr_ref[...]], axis=1)
        vv = v_ref[...]
        dkacc[...] = jnp.zeros_like(dkacc)
        dvacc[...] = jnp.zeros_like(dvacc)
        diagonal = _mla_diagonal_mask(tq)
        for i in range(seq // tq):
            rows = slice(i * tq, (i + 1) * tq)
            qi = q_ref[rows, :]
            doi = do_ref[rows, :]
            lse_i = lse_ref[rows, :]
            delta = jnp.sum(doi.astype(F32) * o_ref[rows, :].astype(F32), axis=-1, keepdims=True)
            dq = None
            for keys, own in _mla_key_tiles((i + 1) * tq, MLA_TK_BWD):
                p = jnp.exp(_mla_scores(qi, kcat[keys], diagonal if own else None) - lse_i)
                ds = p * (_dot_nt(doi, vv[keys]) - delta)
                dsb = (ds * (QK_B ** -0.5)).astype(BF16)
                dq_blk = _dot(dsb, kcat[keys])
                dq = dq_blk if dq is None else dq + dq_blk
                dkacc[keys, :] += _dot_tn(dsb, qi)
                dvacc[keys, :] += _dot_tn(p.astype(BF16), doi)
            dq_ref[rows, :] = _rope_bwd(dq, c_ref[rows, :], s_ref[rows, :], p_ref[...]).astype(BF16)
        dk = dkacc[...]
        dkv_ref[:, :NOPE] = dk[:, :NOPE].astype(BF16)
        dkv_ref[:, NOPE:] = dvacc[...].astype(BF16)

        @pl.when(h == 0)
        def _():
            dkr_ref[...] = dk[:, NOPE:]

        @pl.when(h > 0)
        def _():
            dkr_ref[...] += dk[:, NOPE:]

    return pl.pallas_call(
        body, name="mla_bwd", grid=(bl, HEADS_B),
        in_specs=[
            BS((None, seq, QK_B), lambda b, h: (h, b, 0)),
            BS((seq, NOPE), lambda b, h: (b, 2 * h)),
            BS((seq, V_DIM), lambda b, h: (b, 2 * h + 1)),
            BS((seq, ROPE), lambda b, h: (b, 0)),
            BS((seq, V_DIM), lambda b, h: (b, h)),
            BS((None, seq, 1), lambda b, h: (h, b, 0)),
            BS((seq, V_DIM), lambda b, h: (b, h)),
            BS((seq, QK_B), lambda b, h: (0, 0)),
            BS((seq, QK_B), lambda b, h: (0, 0)),
            BS((QK_B, QK_B), lambda b, h: (0, 0)),
        ],
        out_specs=[
            BS((None, seq, QK_B), lambda b, h: (h, b, 0)),
            BS((seq, NOPE + V_DIM), lambda b, h: (b, h)),
            BS((seq, ROPE), lambda b, h: (b, 0)),
        ],
        out_shape=[SDS((HEADS_B, t, QK_B), BF16), SDS((t, HEADS_B * (NOPE + V_DIM)), BF16), SDS((t, ROPE), F32)],
        scratch_shapes=[pltpu.VMEM((seq, QK_B), F32), pltpu.VMEM((seq, V_DIM), F32)],
        compiler_params=_cparams(2),
    )(q, kv, kv, kr, o, lse, do, c192, s192, p192)


def _loss_final(h, target, gamma):
    t, dn = h.shape
    tm = _tile(t)
    nt = t // tm

    def body(h_ref, t_ref, g_ref, dh_ref, dhb_ref, dg_ref, loss_ref):
        i = pl.program_id(0)
        hv = h_ref[...]
        r = _rms_scale(hv)
        hh = hv * r
        gam = g_ref[...]
        err = hh * gam - t_ref[...]
        part = 0.5 * jnp.sum(jnp.mean(err * err, axis=-1, keepdims=True))

        @pl.when(i == 0)
        def _():
            loss_ref[...] = jnp.zeros_like(loss_ref)

        loss_ref[...] += part
        dy = err * (1.0 / dn)
        _acc_rows(dg_ref, dy * hh, i, nt)
        t1 = dy * gam
        dh = r * (t1 - hh * jnp.mean(t1 * hh, axis=-1, keepdims=True))
        dh_ref[...] = dh
        dhb_ref[...] = dh.astype(BF16)

    row = BS((tm, dn), lambda i: (i, 0))
    return pl.pallas_call(
        body, name="loss_final", grid=(nt,),
        in_specs=[row, row, BS((1, dn), lambda i: (0, 0))],
        out_specs=[row, row, BS((8, dn), lambda i: (0, 0)), BS((8, 128), lambda i: (0, 0))],
        out_shape=[SDS((t, dn), F32), SDS((t, dn), BF16), SDS((8, dn), F32), SDS((8, 128), F32)],
        compiler_params=_cparams(1),
    )(h, target, gamma)


def _ffn_bwd_in(name, dh, w_out, layer, gu):
    t, dn = dh.shape
    tm = _tile(t, 1024)

    def body(dh_ref, w_ref, gu_ref, o_ref):
        da = 0.5 * _dot_nt(dh_ref[...], w_ref[...])
        g = gu_ref[0].astype(F32)
        u = gu_ref[1].astype(F32)
        sg = jax.nn.sigmoid(g)
        o_ref[0] = (da * u * (sg * (1.0 + g * (1.0 - sg)))).astype(BF16)
        o_ref[1] = (da * (g * sg)).astype(BF16)

    blk = BS((None, 2, tm, FB), lambda j, i: (j, 0, i, 0))
    return pl.pallas_call(
        body, name=name, grid=(NJ, t // tm),
        in_specs=[BS((tm, dn), lambda j, i: (i, 0)), BS((None, None, FB, dn), lambda j, i: (layer, j, 0, 0)), blk],
        out_specs=blk, out_shape=SDS((NJ, 2, t, FB), BF16),
        compiler_params=_cparams(2),
    )(dh, w_out, gu)


def _mm_nt_plain(name, xf, w):
    t, dn = xf.shape
    n = w.shape[0]
    tm = _tile(t)

    def body(x_ref, w_ref, o_ref):
        o_ref[...] = _dot_nt(x_ref[...], w_ref[...]).astype(BF16)

    return pl.pallas_call(
        body, name=name, grid=(t // tm,),
        in_specs=[BS((tm, dn), lambda i: (i, 0)), BS((n, dn), lambda i: (0, 0))],
        out_specs=BS((tm, n), lambda i: (i, 0)), out_shape=SDS((t, n), BF16),
        compiler_params=_cparams(1),
    )(xf, w)


def _mm_tn(name, xa, x_spec, ya, y_spec, out_shape, out_spec, nj, scale=None):
    def body(x_ref, y_ref, o_ref):
        acc = _dot_tn(x_ref[...], y_ref[...])
        o_ref[...] = (acc if scale is None else scale * acc).astype(BF16)

    return pl.pallas_call(
        body, name=name, grid=(nj,),
        in_specs=[x_spec, y_spec], out_specs=out_spec, out_shape=SDS(out_shape, BF16),
        compiler_params=_cparams(1),
    )(xa, ya)


def _dw_qkv(hn, dqkv3, wb):
    t, dn = hn.shape
    per = wb // 128

    def body(x_ref, *refs):
        cols = [y_ref[...] for y_ref in refs[:per]]
        refs[per][...] = _dot_tn(x_ref[...], jnp.concatenate(cols, axis=1)).astype(BF16)

    def piece(k):
        return BS((None, t, 128), lambda j: ((per * j + k) // 8, 0, (per * j + k) % 8))

    return pl.pallas_call(
        body, name="dw_qkv", grid=(NDEV,),
        in_specs=[BS((t, dn), lambda j: (0, 0))] + [piece(k) for k in range(per)],
        out_specs=BS((None, dn, wb), lambda j: (j, 0, 0)), out_shape=SDS((NDEV, dn, wb), BF16),
        compiler_params=_cparams(1),
    )(hn, *([dqkv3] * per))


def _mm_nt_epi(name, ya, y_spec, wa, w_spec, nj, n_out, extra, out_shapes, out_specs, epilogue, tm, nt, mm_fn=None):
    n_extra = len(extra)
    n_outs = len(out_shapes)

    def body(*refs):
        y_ref, w_ref = refs[:2]
        ex = refs[2 : 2 + n_extra]
        outs = refs[2 + n_extra : 2 + n_extra + n_outs]
        i = pl.program_id(0)
        j = pl.program_id(1)
        part = _dot_nt(y_ref[...], w_ref[...]) if mm_fn is None else mm_fn(y_ref, w_ref)
        if nj == 1:
            epilogue(part, ex, outs, i, nt)
            return
        acc = refs[-1]

        @pl.when(j == 0)
        def _():
            acc[...] = part

        @pl.when(j > 0)
        def _():
            acc[...] += part

        @pl.when(j == nj - 1)
        def _():
            epilogue(acc[...], ex, outs, i, nt)

    return pl.pallas_call(
        body, name=name, grid=(nt, nj),
        in_specs=[y_spec, w_spec] + [spec for _, spec in extra],
        out_specs=out_specs, out_shape=out_shapes,
        scratch_shapes=[] if nj == 1 else [pltpu.VMEM((tm, n_out), F32)],
        compiler_params=_cparams(2),
    )(ya, wa, *[arr for arr, _ in extra])


def _norm_bwd(dn, hv, gam):
    r = _rms_scale(hv)
    hh = hv * r
    t1 = dn * gam
    return r * (t1 - hh * jnp.mean(t1 * hh, axis=-1, keepdims=True)), dn * hh


def _norm_bwd_epilogue(has_res, out_dtype):
    def epilogue(dn, ex, outs, i, nt):
        dh, dg_rows = _norm_bwd(dn, ex[0][...], ex[1][...])
        _acc_rows(outs[1], dg_rows, i, nt)
        if has_res:
            dh = dh + ex[2][...]
        outs[0][...] = dh.astype(out_dtype)
        if has_res:
            outs[2][...] = dh.astype(BF16)

    return epilogue


def _mm_nt_norm_bwd(name, ya, y_spec, wa, w_spec, nj, h, gamma, res, out_dtype, mm_fn=None, want_tm=512, after=None):
    t, n = h.shape
    tm = _tile(t, want_tm)
    nt = t // tm
    row = BS((tm, n), lambda i, j: (i, 0))
    extra = [(h, row), (gamma, BS((1, n), lambda i, j: (0, 0)))]
    out_shapes = [SDS((t, n), out_dtype), SDS((8, n), F32)]
    out_specs = [row, BS((8, n), lambda i, j: (0, 0))]
    if res is not None:
        extra.append((res, row))
        out_shapes.append(SDS((t, n), BF16))
        out_specs.append(row)
    extra.extend((a, BS(memory_space=pl.ANY)) for a in after or ())
    return _mm_nt_epi(
        name, ya, y_spec, wa, w_spec, nj, n, extra, out_shapes, out_specs, _norm_bwd_epilogue(res is not None, out_dtype), tm, nt, mm_fn,
    )


def _dev_block(jj):
    return jj // 2 + NJ * (jj % 2)


def _ffn_dn_mm(y_ref, w_ref):
    acc = None
    for jj in range(2 * NJ):
        part = _dot_nt(y_ref[jj], w_ref[_dev_block(jj)])
        acc = part if acc is None else acc + part
    return acc


def _ffn_bwd(tag, dh, dhb, n_in, h_in, gamma, gu, a, w_in, w_out, collective_id, after):
    t, dn = dh.shape
    dgu = _ffn_bwd_in(f"{tag}_bwd_in", dhb, w_out, 0, gu).reshape(2 * NJ, t, FB)
    dw_out = _mm_tn(
        f"{tag}_dw_out", a, BS((None, t, FB), lambda j: (j, 0, 0)), dhb, BS((t, dn), lambda j: (0, 0)),
        (NJ, FB, dn), BS((None, FB, dn), lambda j: (j, 0, 0)), NJ, scale=0.5,
    )
    dw_in = _mm_tn(
        f"{tag}_dw_in", dgu, BS((None, t, FB), lambda j: (j, 0, 0)), n_in, BS((t, dn), lambda j: (0, 0)),
        (NDEV, FB, dn), BS((None, FB, dn), lambda j: (_dev_block(j), 0, 0)), NDEV,
    )
    entries = [("scatter", dw_in), ("scatter", dw_out.reshape(NDEV, NJ * FB // NDEV, dn))]
    landed = _exchange_sc(f"{tag}_reduce", entries, collective_id, after)
    tm = _tile(t)
    resident = BS((None, NDEV, dn, FB), lambda i, j: (0, 0, 0, 0), pipeline_mode=pl.Buffered(1))
    dh_in, dgam, dhb_in = _mm_nt_norm_bwd(
        f"{tag}_dn", dgu, BS((2 * NJ, tm, FB), lambda i, j: (0, i, 0)), w_in, resident, 1, h_in, gamma, dh, F32, mm_fn=_ffn_dn_mm,
        after=[e[1] for e in entries],
    )
    return dh_in, dhb_in, dgam, landed


def _heads_mm(y_ref, w_ref):
    acc = None
    for h in range(HEADS_B):
        part = _dot_nt(y_ref[h], w_ref[h])
        acc = part if acc is None else acc + part
    return acc


def _dqkv_mm(per):
    def mm(y_ref, w_ref):
        acc = None
        for j in range(NDEV):
            cols = [y_ref[(per * j + k) // 8, :, ((per * j + k) % 8) * 128 : ((per * j + k) % 8 + 1) * 128] for k in range(per)]
            part = _dot_nt(jnp.concatenate(cols, axis=1), w_ref[j])
            acc = part if acc is None else acc + part
        return acc

    return mm


def _kv_latent_bwd(dkv, w_up, ckr, latent_norm, dkr, c64, s64, p64, seq):
    t, wd = ckr.shape
    hb = w_up.shape[-1]
    tm = _tile(min(seq, 512), min(seq, 512))
    nt = t // tm
    nseq = seq // tm

    def epilogue(dn, ex, outs, i, nt_):
        dlat, dg_rows = _norm_bwd(dn, ex[0][...], ex[1][...])
        _acc_rows(outs[1], dg_rows, i, nt_)
        outs[0][:, :KV_LORA] = dlat.astype(BF16)
        outs[0][:, KV_LORA:] = _rope_bwd(ex[2][...], ex[3][...], ex[4][...], ex[5][...]).astype(BF16)

    pos = BS((tm, ROPE), lambda i, j: (i % nseq, 0))
    extra = [
        (ckr, BS((tm, KV_LORA), lambda i, j: (i, 0))), (latent_norm, BS((1, KV_LORA), lambda i, j: (0, 0))),
        (dkr, BS((tm, ROPE), lambda i, j: (i, 0))), (c64, pos), (s64, pos), (p64, BS((ROPE, ROPE), lambda i, j: (0, 0))),
    ]
    def heads_mm(y_ref, w_ref):
        acc = None
        for h in range(HEADS_B):
            part = _dot_nt(y_ref[:, h * hb : (h + 1) * hb], w_ref[h])
            acc = part if acc is None else acc + part
        return acc

    return _mm_nt_epi(
        "kv_latent_bwd", dkv, BS((tm, HEADS_B * hb), lambda i, j: (i, 0)), w_up, BS((HEADS_B, KV_LORA, hb), lambda i, j: (0, 0, 0)),
        1, KV_LORA, extra, [SDS((t, wd), BF16), SDS((8, KV_LORA), F32)],
        [BS((tm, wd), lambda i, j: (i, 0)), BS((8, KV_LORA), lambda i, j: (0, 0))], epilogue, tm, nt, heads_mm,
    )


def _adamw(name, parts, w, m, v):
    n_layers, rows, cols = w.shape
    tr = max(d for d in range(8, min(rows, 256) + 1, 8) if rows % d == 0)
    nb = rows // tr

    def body(*refs):
        p_refs = refs[:n_layers]
        w_ref, m_ref, v_ref, g_ref, d_ref, nm_ref, nv_ref = refs[n_layers : n_layers + 7]
        layer = pl.program_id(0)
        for lp in range(n_layers):

            @pl.when(layer == lp)
            def _():
                g = p_refs[lp][0].astype(F32)
                for k in range(1, NDEV):
                    g = g + p_refs[lp][k].astype(F32)
                g_ref[...] = g

        g = g_ref[...]
        nm = ADAM_B1 * m_ref[...] + (1.0 - ADAM_B1) * g
        nv = ADAM_B2 * v_ref[...] + (1.0 - ADAM_B2) * (g * g)
        nm_ref[...] = nm
        nv_ref[...] = nv
        m_hat = nm / (1.0 - ADAM_B1 ** ADAM_STEP)
        v_hat = nv / (1.0 - ADAM_B2 ** ADAM_STEP)
        d_ref[...] = -ADAM_LR * (m_hat / (jnp.sqrt(v_hat) + ADAM_EPS) + ADAM_WD * w_ref[...])

    def part_spec(lp):
        return BS((NDEV, tr, cols), lambda l, i: (0, jnp.where(l == lp, i, jnp.where(l < lp, 0, nb - 1)), 0))

    row = BS((None, tr, cols), lambda l, i: (l, i, 0))
    return pl.pallas_call(
        body, name=name, grid=(n_layers, nb),
        in_specs=[part_spec(lp) for lp in range(n_layers)] + [row, row, row],
        out_specs=[row] * 4, out_shape=[SDS(w.shape, F32)] * 4,
        compiler_params=_cparams(2),
    )(*parts, w, m, v)


def _pack_small(ffn1_norm, mix_norm, ffn2_norm, kv_norm, final_norm, q_norm, latent_norm, rel_bias, last_row):
    dn = ffn1_norm.shape[-1]

    def rows_of(a, n_rows):
        flat = a.reshape(-1)
        return jnp.pad(flat, (0, n_rows * dn - flat.shape[0])).reshape(n_rows, dn)

    return jnp.concatenate(
        [
            ffn1_norm.reshape(2, dn), mix_norm.reshape(2, dn), ffn2_norm.reshape(2, dn), kv_norm.reshape(1, dn),
            final_norm.reshape(1, dn), rows_of(q_norm, 1), rows_of(latent_norm, 1), rows_of(rel_bias, 5), rows_of(last_row, 1),
        ],
        axis=0,
    )


def _unpack_small(pack):
    dn = pack.shape[-1]
    return dict(
        ffn1_norm=pack[0:2], mix_norm=pack[2:4], ffn2_norm=pack[4:6], kv_norm=pack[6], final_norm=pack[7],
        b_q_norm=pack[8, :Q_LORA].reshape(1, Q_LORA), kv_latent_norm=pack[9, :KV_LORA],
        a_rel_bias=pack[10:15].reshape(-1)[: HEADS_A * NREL].reshape(1, HEADS_A, NREL), last=pack[15],
    )


def kernel(x, ffn1_norm, ffn1_w_in, ffn1_w_out, mix_norm, ffn2_norm, ffn2_w_in, ffn2_w_out, a_w_qkv, a_rel_bias, a_w_o, kv_norm, kv_w_down, kv_latent_norm, kv_w_up, b_w_dq, b_q_norm, b_w_uq, b_w_o, final_norm, loss_target, m_ffn1_norm, m_ffn1_w_in, m_ffn1_w_out, m_mix_norm, m_ffn2_norm, m_ffn2_w_in, m_ffn2_w_out, m_a_w_qkv, m_a_rel_bias, m_a_w_o, m_kv_norm, m_kv_w_down, m_kv_latent_norm, m_kv_w_up, m_b_w_dq, m_b_q_norm, m_b_w_uq, m_b_w_o, m_final_norm, v_ffn1_norm, v_ffn1_w_in, v_ffn1_w_out, v_mix_norm, v_ffn2_norm, v_ffn2_w_in, v_ffn2_w_out, v_a_w_qkv, v_a_rel_bias, v_a_w_o, v_kv_norm, v_kv_w_down, v_kv_latent_norm, v_kv_w_up, v_b_w_dq, v_b_q_norm, v_b_w_uq, v_b_w_o, v_final_norm):
    bl, seq, dn = x.shape
    t = bl * seq
    tm = _tile(t)
    nt = t // tm
    x2 = x.reshape(t, dn)
    target2 = loss_target.reshape(t, dn)

    def gathered(*ws):
        return [("gather", w.astype(BF16)) for w in ws]

    groups = [
        gathered(ffn1_w_in[0]), gathered(ffn1_w_out[0]), gathered(a_w_qkv[0], a_w_o[0]), gathered(ffn2_w_in[0], ffn2_w_out[0]),
        gathered(kv_w_down, kv_w_up), gathered(ffn1_w_in[1], ffn1_w_out[1]), gathered(b_w_dq[0], b_w_uq[0], b_w_o[0]),
        gathered(ffn2_w_in[1], ffn2_w_out[1]),
    ]
    ag = [_exchange_sc(f"gather_{k}", group, GATHER_IDS[k]) for k, group in enumerate(groups)]

    def as_w_in(w):
        return w.reshape(1, NDEV, dn, FB)

    def as_w_out(w):
        return w.reshape(1, NJ, FB, dn)

    c64, s64, p64, c192, s192, p192 = _rope_tables(seq)
    q_norm = b_q_norm.reshape(1, Q_LORA)
    latent_norm = kv_latent_norm.reshape(1, KV_LORA)
    bias = _window_bias(_rel_bias_fwd(jnp.pad(a_rel_bias[0], ((0, 0), (0, NREL_PAD - NREL)))))

    h0, h1, h2, n1, hn, n2, gu1, gu2, a1, a2, w_in1, w_in2, w_out1, w_out2 = ([None, None] for _ in range(14))
    h0[0] = x2
    (n1[0],) = _norm_fwd("norm_x", x2, ffn1_norm[0:1])
    w_in1[0] = as_w_in(ag[0][0])
    gu1[0], a1[0] = _ffn_in("ffn1_in_0", n1[0], w_in1[0], 0)
    w_out1[0] = as_w_out(ag[1][0])
    h1[0], hn[0] = _mm_res_norm("ffn1_out_0", a1[0], w_out1[0], 0, h0[0], mix_norm[0:1], 0.5)
    w_qkv, w_o_a = ag[2]
    qkv_wb = w_qkv.shape[-1]
    w_o_a = w_o_a.reshape(1, 1, dn, dn)
    qkv3 = _qkv_proj("qkv_proj", hn[0], w_qkv)
    o_a, lse_a = _attn_a_fwd(qkv3, bias, bl, seq)
    h2[0], n2[0] = _mm_res_norm("attn_a_out", o_a.reshape(1, t, dn), w_o_a, 0, h1[0], ffn2_norm[0:1], 1.0)
    w_in2[0], w_out2[0] = as_w_in(ag[3][0]), as_w_out(ag[3][1])
    gu2[0], a2[0] = _ffn_in("ffn2_in_0", n2[0], w_in2[0], 0)
    h0[1], hk, n1[1] = _mm_res_norm(
        "ffn2_out_0", a2[0], w_out2[0], 0, h2[0], jnp.concatenate([kv_norm.reshape(1, dn), ffn1_norm[1:2]], axis=0), 0.5
    )
    w_down, w_up = ag[4]
    w_down = w_down.reshape(dn, KV_LORA + ROPE)
    ckr, ckv, kr = _kv_down(hk, w_down, latent_norm, c64, s64, p64, seq)
    kv = _kv_up(ckv, w_up)
    w_in1[1], w_out1[1] = as_w_in(ag[5][0]), as_w_out(ag[5][1])
    gu1[1], a1[1] = _ffn_in("ffn1_in_1", n1[1], w_in1[1], 0)
    h1[1], hn[1] = _mm_res_norm("ffn1_out_1", a1[1], w_out1[1], 0, h0[1], mix_norm[1:2], 0.5)
    w_dq, w_uq, w_o_b = ag[6]
    w_dq = w_dq.reshape(dn, Q_LORA)
    w_o_b = w_o_b.reshape(1, 1, dn, dn)
    cq_pre, cq = _q_down(hn[1], w_dq, q_norm)
    q = _q_up(cq, w_uq, c192, s192, p192, seq)
    o_b, lse_b = _mla_fwd(q, kv, kr, bl, seq)
    h2[1], n2[1] = _mm_res_norm("attn_b_out", o_b.reshape(1, t, dn), w_o_b, 0, h1[1], ffn2_norm[1:2], 1.0)
    w_in2[1], w_out2[1] = as_w_in(ag[7][0]), as_w_out(ag[7][1])
    gu2[1], a2[1] = _ffn_in("ffn2_in_1", n2[1], w_in2[1], 0)
    (h_last,) = _mm_res_norm("ffn2_out_1", a2[1], w_out2[1], 0, h2[1], None, 0.5)
    dh, dhb, dg_final, loss_part = _loss_final(h_last, target2, final_norm.reshape(1, dn))

    dg_ffn1, dg_mix, dg_ffn2, rs_ffn1, rs_ffn2 = ([None, None] for _ in range(5))

    def whole(rows, cols):
        return BS((rows, cols), lambda j: (0, 0))

    def dw_rows(name, xa, ya):
        n = ya.shape[1]
        return _mm_tn(name, xa, whole(t, dn), ya, whole(t, n), (dn, n), whole(dn, n), 1).reshape(NDEV, dn // NDEV, n)

    dh, dhb, dg_ffn2[1], rs_ffn2[1] = _ffn_bwd(
        "ffn2_1", dh, dhb, n2[1], h2[1], ffn2_norm[1:2], gu2[1], a2[1], w_in2[1], w_out2[1], REDUCE_IDS[0], ()
    )
    do_b = _mm_nt_plain("attn_b_do", dhb, w_o_b.reshape(dn, dn))
    dw_o_b = dw_rows("attn_b_dwo", o_b, dhb)
    dq_pre, dkv, dkr = _mla_bwd(q, kv, kr, o_b, lse_b, do_b, c192, s192, p192, bl, seq)
    dw_uq = _mm_tn(
        "dw_uq", cq, whole(t, Q_LORA), dq_pre, BS((None, t, QK_B), lambda j: (j, 0, 0)),
        (HEADS_B, Q_LORA, QK_B), BS((None, Q_LORA, QK_B), lambda j: (j, 0, 0)), HEADS_B,
    )
    dcq_pre, dg_q = _mm_nt_norm_bwd(
        "dcq", dq_pre, BS((HEADS_B, tm, QK_B), lambda i, j: (0, i, 0)), w_uq, BS((HEADS_B, Q_LORA, QK_B), lambda i, j: (0, 0, 0)),
        1, cq_pre, q_norm, None, BF16, mm_fn=_heads_mm,
    )
    dw_dq = dw_rows("dw_dq", hn[1], dcq_pre)
    dh, dg_mix[1], dhb = _mm_nt_norm_bwd(
        "dhn_b", dcq_pre, BS((tm, Q_LORA), lambda i, j: (i, 0)), w_dq, BS((dn, Q_LORA), lambda i, j: (0, 0)),
        1, h1[1], mix_norm[1:2], dh, F32,
    )
    dh, dhb, dg_ffn1[1], rs_ffn1[1] = _ffn_bwd(
        "ffn1_1", dh, dhb, n1[1], h0[1], ffn1_norm[1:2], gu1[1], a1[1], w_in1[1], w_out1[1], REDUCE_IDS[1], rs_ffn2[1][:1]
    )
    dw_up = _mm_tn(
        "dw_up", ckv, whole(t, KV_LORA), dkv, BS((t, NOPE + V_DIM), lambda j: (0, j)),
        (HEADS_B, KV_LORA, NOPE + V_DIM), BS((None, KV_LORA, NOPE + V_DIM), lambda j: (j, 0, 0)), HEADS_B,
    )
    dckr, dg_latent = _kv_latent_bwd(dkv, w_up, ckr, latent_norm, dkr, c64, s64, p64, seq)
    dw_down = dw_rows("dw_down", hk, dckr)
    dh, dg_kv, dhb = _mm_nt_norm_bwd(
        "dhk", dckr, BS((tm, KV_LORA + ROPE), lambda i, j: (i, 0)), w_down, BS((dn, KV_LORA + ROPE), lambda i, j: (0, 0)),
        1, h0[1], kv_norm.reshape(1, dn), dh, F32,
    )
    dh, dhb, dg_ffn2[0], rs_ffn2[0] = _ffn_bwd(
        "ffn2_0", dh, dhb, n2[0], h2[0], ffn2_norm[0:1], gu2[0], a2[0], w_in2[0], w_out2[0], REDUCE_IDS[2], rs_ffn1[1][:1]
    )
    do_a = _mm_nt_plain("attn_a_do", dhb, w_o_a.reshape(dn, dn))
    dw_o_a = dw_rows("attn_a_dwo", o_a, dhb)
    dqkv3, dbias = _attn_a_bwd(qkv3, o_a, lse_a, do_a, bias, bl, seq)
    dw_qkv = _dw_qkv(hn[0], dqkv3, qkv_wb)
    mixer_grads = [dw_o_a, dw_qkv, dw_o_b, dw_uq, dw_dq, dw_up, dw_down]
    dh, dg_mix[0], dhb = _mm_nt_norm_bwd(
        "dhn_a", dqkv3, BS((3, tm, dn), lambda i, j: (0, i, 0)), w_qkv, BS((NDEV, dn, qkv_wb), lambda i, j: (0, 0, 0)),
        1, h1[0], mix_norm[0:1], dh, F32, mm_fn=_dqkv_mm(qkv_wb // 128), after=mixer_grads,
    )
    rs_mixers = _exchange_sc("mixers_reduce", [("scatter", g) for g in mixer_grads], REDUCE_IDS[3], rs_ffn2[0][:1])
    dh, dhb, dg_ffn1[0], rs_ffn1[0] = _ffn_bwd(
        "ffn1_0", dh, dhb, n1[0], h0[0], ffn1_norm[0:1], gu1[0], a1[0], w_in1[0], w_out1[0], REDUCE_IDS[4], rs_mixers[:1]
    )
    grad_x = dh.reshape(bl, seq, dn)
    dtable = _rel_bias_bwd(_window_bias_bwd(dbias))[:, :NREL]

    def update(name, parts, w, m, v):
        shape3 = (len(parts),) + w.shape[-2:]
        parts = [p.reshape((NDEV,) + shape3[1:]) for p in parts]
        outs = _adamw(name, parts, w.reshape(shape3), m.reshape(shape3), v.reshape(shape3))
        return [o.reshape(w.shape) for o in outs]

    res = {}
    r_in2_1, r_out2_1 = rs_ffn2[1]
    r_in1_1, r_out1_1 = rs_ffn1[1]
    r_in2_0, r_out2_0 = rs_ffn2[0]
    r_in1_0, r_out1_0 = rs_ffn1[0]
    r_o_a, r_qkv, r_o_b, r_uq, r_dq, r_up, r_down = rs_mixers
    def update_transposed(name, parts, w, m, v):
        outs = update(name, parts, *[jnp.swapaxes(a, 1, 2) for a in (w, m, v)])
        return [jnp.swapaxes(o, 1, 2) for o in outs]

    res["ffn2_w_in"] = update_transposed("adamw_ffn2_w_in", [r_in2_0, r_in2_1], ffn2_w_in, m_ffn2_w_in, v_ffn2_w_in)
    res["ffn2_w_out"] = update("adamw_ffn2_w_out", [r_out2_0, r_out2_1], ffn2_w_out, m_ffn2_w_out, v_ffn2_w_out)
    res["kv_w_down"] = update("adamw_kv_w_down", [r_down], kv_w_down, m_kv_w_down, v_kv_w_down)
    res["kv_w_up"] = update("adamw_kv_w_up", [r_up], kv_w_up, m_kv_w_up, v_kv_w_up)
    res["b_w_dq"] = update("adamw_b_w_dq", [r_dq], b_w_dq, m_b_w_dq, v_b_w_dq)
    res["b_w_uq"] = update("adamw_b_w_uq", [r_uq], b_w_uq, m_b_w_uq, v_b_w_uq)
    res["b_w_o"] = update("adamw_b_w_o", [r_o_b], b_w_o, m_b_w_o, v_b_w_o)
    res["a_w_qkv"] = update("adamw_a_w_qkv", [r_qkv], a_w_qkv, m_a_w_qkv, v_a_w_qkv)
    res["a_w_o"] = update("adamw_a_w_o", [r_o_a], a_w_o, m_a_w_o, v_a_w_o)

    small = _pack_small(
        jnp.stack([dg_ffn1[0][0], dg_ffn1[1][0]]), jnp.stack([dg_mix[0][0], dg_mix[1][0]]), jnp.stack([dg_ffn2[0][0], dg_ffn2[1][0]]),
        dg_kv[0], dg_final[0], dg_q[0], dg_latent[0], dtable, loss_part[0],
    )
    done = [r[1] for name, r in res.items() if name != "ffn2_w_in"]
    (r_small,) = _exchange_sc("small_grads_gather", [("gather", small)], SMALL_ID, list(rs_ffn1[0][:1]) + done)
    res["ffn1_w_in"] = update_transposed("adamw_ffn1_w_in", [r_in1_0, r_in1_1], ffn1_w_in, m_ffn1_w_in, v_ffn1_w_in)
    res["ffn1_w_out"] = update("adamw_ffn1_w_out", [r_out1_0, r_out1_1], ffn1_w_out, m_ffn1_w_out, v_ffn1_w_out)
    zero_row = jnp.zeros((dn,), F32)
    packs = [
        _pack_small(f1, mx, f2, kvn, fin, qn, lat, rel, zero_row)
        for f1, mx, f2, kvn, fin, qn, lat, rel in (
            (ffn1_norm, mix_norm, ffn2_norm, kv_norm, final_norm, b_q_norm, kv_latent_norm, a_rel_bias),
            (m_ffn1_norm, m_mix_norm, m_ffn2_norm, m_kv_norm, m_final_norm, m_b_q_norm, m_kv_latent_norm, m_a_rel_bias),
            (v_ffn1_norm, v_mix_norm, v_ffn2_norm, v_kv_norm, v_final_norm, v_b_q_norm, v_kv_latent_norm, v_a_rel_bias),
        )
    ]
    small_out = [_unpack_small(o[0]) for o in _adamw("adamw_small", [r_small], *[p[None] for p in packs])]
    for name in ("ffn1_norm", "mix_norm", "ffn2_norm", "a_rel_bias", "kv_norm", "kv_latent_norm", "b_q_norm", "final_norm"):
        res[name] = [so[name] for so in small_out]
    loss = small_out[0]["last"][0]

    order = [
        "ffn1_norm", "ffn1_w_in", "ffn1_w_out", "mix_norm", "ffn2_norm", "ffn2_w_in", "ffn2_w_out", "a_w_qkv", "a_rel_bias",
        "a_w_o", "kv_norm", "kv_w_down", "kv_latent_norm", "kv_w_up", "b_w_dq", "b_q_norm", "b_w_uq", "b_w_o", "final_norm",
    ]
    return (loss, grad_x, *[res[n][0] for n in order], *[res[n][1] for n in order], *[res[n][2] for n in order], *[res[n][3] for n in order])
```

```python
import jax
import jax.numpy as jnp
import numpy as np
from jax import lax
from jax.experimental import pallas as pl
from jax.experimental.pallas import tpu as pltpu
from jax.experimental.pallas import tpu_sc as plsc

NDEV = 8
D_MODEL = 1024
D_FF = 2816
FB = 2 * D_FF // NDEV
NJ = D_FF // FB
CHUNK = 64
LEFT_CHUNKS = 8
PAD = LEFT_CHUNKS * CHUNK
BAND = PAD + CHUNK
CHUNKS_PER_STEP = 4
WINDOW = PAD + CHUNKS_PER_STEP * CHUNK
STEP_ROWS = CHUNKS_PER_STEP * 2 * CHUNK
MAX_REL = 128
NREL = 2 * MAX_REL + 1
NREL_PAD = 384
HEADS_A = 16
HEADS_B = 8
NOPE = 128
ROPE = 64
QK_B = NOPE + ROPE
V_DIM = 128
Q_LORA = 768
KV_LORA = 256
ROPE_THETA = 10000.0
EPS = 1e-6
NEG_INF = -1e30
MLA_TQ = 256
MLA_TK_FWD = 256
MLA_TK_BWD = 1024
ADAM_LR = 0.001
ADAM_B1 = 0.9
ADAM_B2 = 0.999
ADAM_EPS = 1e-08
ADAM_WD = 0.01
ADAM_STEP = 10
PACK_ROWS = 16
GATHER_IDS = tuple(range(1, 9))
REDUCE_IDS = tuple(range(9, 15))
SMALL_ID = 15
VMEM_LIMIT_BYTES = 56 * 1024 * 1024

F32 = jnp.float32
BF16 = jnp.bfloat16
SDS = jax.ShapeDtypeStruct
BS = pl.BlockSpec
MESH = pl.DeviceIdType.MESH


def _cparams(n_axes):
    return pltpu.CompilerParams(dimension_semantics=("arbitrary",) * n_axes, vmem_limit_bytes=VMEM_LIMIT_BYTES)


def _tile(t, want=512):
    return want if t % want == 0 else t


def _dot(a, b):
    return jnp.dot(a, b, preferred_element_type=F32)


def _dot_nt(a, b):
    return lax.dot_general(a, b, (((1,), (1,)), ((), ())), preferred_element_type=F32)


def _dot_tn(a, b):
    return lax.dot_general(a, b, (((0,), (0,)), ((), ())), preferred_element_type=F32)


def _split3(a):
    hi = a.astype(BF16)
    rest = a - hi.astype(F32)
    mid = rest.astype(BF16)
    return hi, mid, (rest - mid.astype(F32)).astype(BF16)


def _dot_exact(a, onehot, transposed=False):
    ob = onehot.astype(BF16)
    dot = _dot_nt if transposed else _dot
    hi, mid, lo = _split3(a)
    return dot(hi, ob) + dot(mid, ob) + dot(lo, ob)


def _rms_scale(h):
    return lax.rsqrt(jnp.mean(h * h, axis=-1, keepdims=True) + EPS)


def _acc_rows(ref, val, step, n_steps):
    part = val.reshape(val.shape[0] // 8, 8, val.shape[1]).sum(axis=0)

    @pl.when(step == 0)
    def _():
        ref[...] = part

    @pl.when(step > 0)
    def _():
        ref[...] += part

    @pl.when(step == n_steps - 1)
    def _():
        ref[...] = jnp.broadcast_to(jnp.sum(ref[...], axis=0, keepdims=True), ref.shape)


def _exchange_plan(entries):
    ins = [e[1] for e in entries]
    kinds = [e[0] for e in entries]
    lands = [SDS((NDEV,) + a.shape if k == "gather" else a.shape, a.dtype) for k, a in zip(kinds, ins)]
    return ins, lands, kinds


def _mesh_place():
    x, y, c = lax.axis_index("x"), lax.axis_index("y"), lax.axis_index("c")
    return (x, y, c), 4 * x + 2 * y + c


def _flipped(place, p):
    x, y, c = place
    px = 1 - x if p & 4 else x
    py = 1 - y if p & 2 else y
    pc = 1 - c if p & 1 else c
    return (px, py, pc), 4 * px + 2 * py + pc


def _ends(kind, src_ref, land_ref, origin, target):
    if kind == "gather":
        return src_ref, land_ref.at[origin]
    return src_ref.at[target], land_ref.at[origin]


def _remote(kind, src_ref, land_ref, send_sems, recv_sems, k, p, place, me, arriving):
    peer_pos, peer = _flipped(place, p)
    src, dst = _ends(kind, src_ref, land_ref, me, peer)
    if arriving:
        dst = _ends(kind, src_ref, land_ref, peer, me)[1]
    sem = k * (NDEV - 1) + p - 1
    return pltpu.make_async_remote_copy(
        src_ref=src, dst_ref=dst, send_sem=send_sems.at[sem], recv_sem=recv_sems.at[sem], device_id=peer_pos, device_id_type=MESH,
    )


def _exchange_sc(name, entries, collective_id, after=()):
    ins, lands, kinds = _exchange_plan(entries)
    n = len(ins)
    after = tuple(after)

    def launch(*refs):
        refs = refs[:n] + refs[n + len(after) :]
        in_refs, land_refs = refs[:n], refs[n : 2 * n]
        send_sems, recv_sems, local_sems = refs[2 * n :]
        place, me = _mesh_place()
        barrier = pltpu.get_barrier_semaphore()
        for p in range(1, NDEV):
            pl.semaphore_signal(barrier, inc=1, device_id=_flipped(place, p)[0], device_id_type=MESH)
        pl.semaphore_wait(barrier, NDEV - 1)
        local = []
        for k in range(n):
            src, dst = _ends(kinds[k], in_refs[k], land_refs[k], me, me)
            local.append(pltpu.make_async_copy(src, dst, local_sems.at[k]))
            local[-1].start()
        sends = []
        if all(kind == "gather" for kind in kinds):
            for p in (1, 2, 4, 6):
                for k in range(n):
                    sends.append(_remote(kinds[k], in_refs[k], land_refs[k], send_sems, recv_sems, k, p, place, me, False))
                    sends[-1].start()
            sibling_pos, _ = _flipped(place, 1)
            for f in (2, 4, 6):
                _, origin = _flipped(place, f)
                for k in range(n):
                    _remote(kinds[k], in_refs[k], land_refs[k], send_sems, recv_sems, k, f, place, me, True).wait_recv()
                    sem = k * (NDEV - 1) + f
                    sends.append(
                        pltpu.make_async_remote_copy(
                            src_ref=land_refs[k].at[origin], dst_ref=land_refs[k].at[origin], send_sem=send_sems.at[sem],
                            recv_sem=recv_sems.at[sem], device_id=sibling_pos, device_id_type=MESH,
                        )
                    )
                    sends[-1].start()
            for p in (1, 3, 5, 7):
                for k in range(n):
                    _remote(kinds[k], in_refs[k], land_refs[k], send_sems, recv_sems, k, p, place, me, True).wait_recv()
        else:
            for p in range(1, NDEV):
                for k in range(n):
                    sends.append(_remote(kinds[k], in_refs[k], land_refs[k], send_sems, recv_sems, k, p, place, me, False))
                    sends[-1].start()
            for p in range(1, NDEV):
                for k in range(n):
                    _remote(kinds[k], in_refs[k], land_refs[k], send_sems, recv_sems, k, p, place, me, True).wait_recv()
        for cp in sends:
            cp.wait_send()
        for cp in local:
            cp.wait()

    return pl.kernel(
        launch, out_type=tuple(lands), mesh=plsc.ScalarSubcoreMesh(axis_name="sequencer", num_cores=1), name=name,
        scratch_types=(
            pltpu.SemaphoreType.DMA((n * (NDEV - 1),)), pltpu.SemaphoreType.DMA((n * (NDEV - 1),)), pltpu.SemaphoreType.DMA((n,)),
        ),
        compiler_params=pltpu.CompilerParams(collective_id=collective_id),
    )(*ins, *after)


def _norm_fwd(name, h, gammas):
    t, dn = h.shape
    ng = gammas.shape[0]
    tm = _tile(t)

    def body(h_ref, g_ref, *outs):
        hv = h_ref[...]
        hh = hv * _rms_scale(hv)
        for i, o_ref in enumerate(outs):
            o_ref[...] = (hh * g_ref[i : i + 1, :]).astype(BF16)

    row = BS((tm, dn), lambda i: (i, 0))
    return pl.pallas_call(
        body, name=name, grid=(t // tm,),
        in_specs=[row, BS((ng, dn), lambda i: (0, 0))],
        out_specs=[row] * ng, out_shape=[SDS((t, dn), BF16)] * ng,
        compiler_params=_cparams(1),
    )(h, gammas)


def _ffn_in(name, n, w_in, layer):
    t, dn = n.shape
    tm = _tile(t, 1024)

    def body(n_ref, wg_ref, wu_ref, gu_ref, a_ref):
        xv = n_ref[...]
        g = _dot(xv, wg_ref[...])
        u = _dot(xv, wu_ref[...])
        gu_ref[0] = g.astype(BF16)
        gu_ref[1] = u.astype(BF16)
        a_ref[...] = (g * jax.nn.sigmoid(g) * u).astype(BF16)

    return pl.pallas_call(
        body, name=name, grid=(NJ, t // tm),
        in_specs=[
            BS((tm, dn), lambda j, i: (i, 0)),
            BS((None, None, dn, FB), lambda j, i: (layer, j, 0, 0)),
            BS((None, None, dn, FB), lambda j, i: (layer, j + NJ, 0, 0)),
        ],
        out_specs=[BS((None, 2, tm, FB), lambda j, i: (j, 0, i, 0)), BS((None, tm, FB), lambda j, i: (j, i, 0))],
        out_shape=[SDS((NJ, 2, t, FB), BF16), SDS((NJ, t, FB), BF16)],
        compiler_params=_cparams(2),
    )(n, w_in, w_in)


def _mm_res_norm(name, a, w, layer, h_in, gammas, scale):
    nk, t, kb = a.shape
    dn = w.shape[-1]
    ng = 0 if gammas is None else gammas.shape[0]
    tm = _tile(t)

    def body(*refs):
        a_ref, w_ref, h_ref = refs[:3]
        g_ref = refs[3] if ng else None
        outs = refs[3 + (1 if ng else 0) :]
        acc = _dot(a_ref[0], w_ref[0])
        for k in range(1, nk):
            acc += _dot(a_ref[k], w_ref[k])
        ho = h_ref[...] + scale * acc
        outs[0][...] = ho
        if ng:
            hh = ho * _rms_scale(ho)
            for i in range(ng):
                outs[1 + i][...] = (hh * g_ref[i : i + 1, :]).astype(BF16)

    row = BS((tm, dn), lambda i: (i, 0))
    in_specs = [BS((nk, tm, kb), lambda i: (0, i, 0)), BS((None, nk, kb, dn), lambda i: (layer, 0, 0, 0)), row]
    args = [a, w, h_in]
    if ng:
        in_specs.append(BS((ng, dn), lambda i: (0, 0)))
        args.append(gammas)
    return pl.pallas_call(
        body, name=name, grid=(t // tm,),
        in_specs=in_specs,
        out_specs=[row] * (1 + ng), out_shape=[SDS((t, dn), F32)] + [SDS((t, dn), BF16)] * ng,
        compiler_params=_cparams(1),
    )(*args)


def _qkv_proj(name, hn, w_qkv):
    t, dn = hn.shape
    wb = w_qkv.shape[-1]
    per = wb // 128
    tm = _tile(t)

    def body(x_ref, w_ref, o_ref):
        xv = x_ref[...]
        for j in range(NDEV):
            yv = _dot(xv, w_ref[j]).astype(BF16)
            for i in range(per):
                n = per * j + i
                o_ref[n // 8, :, (n % 8) * 128 : (n % 8 + 1) * 128] = yv[:, i * 128 : (i + 1) * 128]

    return pl.pallas_call(
        body, name=name, grid=(t // tm,),
        in_specs=[BS((tm, dn), lambda i: (i, 0)), BS((NDEV, dn, wb), lambda i: (0, 0, 0))],
        out_specs=BS((3, tm, dn), lambda i: (0, i, 0)), out_shape=SDS((3, t, dn), BF16),
        compiler_params=_cparams(1),
    )(hn, w_qkv)


def _rel_onehot(i):
    r = lax.broadcasted_iota(jnp.int32, (NREL_PAD, BAND), 0)
    j = lax.broadcasted_iota(jnp.int32, (NREL_PAD, BAND), 1)
    idx = jnp.clip(PAD + i - j, -MAX_REL, MAX_REL) + MAX_REL
    return (idx == r).astype(F32)


def _rel_bias_fwd(table):
    def body(t_ref, o_ref):
        i8 = pl.program_id(0)
        for ii in range(8):
            o_ref[:, ii, :] = _dot_exact(t_ref[...], _rel_onehot(i8 * 8 + ii))

    return pl.pallas_call(
        body, name="rel_bias_fwd", grid=(CHUNK // 8,),
        in_specs=[BS((HEADS_A, NREL_PAD), lambda i: (0, 0))],
        out_specs=BS((HEADS_A, 8, BAND), lambda i: (0, i, 0)), out_shape=SDS((HEADS_A, CHUNK, BAND), F32),
        compiler_params=_cparams(1),
    )(table)


def _rel_bias_bwd(dbias):
    def body(d_ref, o_ref):
        i8 = pl.program_id(0)
        acc = jnp.zeros((HEADS_A, NREL_PAD), F32)
        for ii in range(8):
            acc += _dot_exact(d_ref[:, ii, :], _rel_onehot(i8 * 8 + ii), transposed=True)

        @pl.when(i8 == 0)
        def _():
            o_ref[...] = acc

        @pl.when(i8 > 0)
        def _():
            o_ref[...] += acc

    return pl.pallas_call(
        body, name="rel_bias_bwd", grid=(CHUNK // 8,),
        in_specs=[BS((HEADS_A, 8, BAND), lambda i: (0, i, 0))],
        out_specs=BS((HEADS_A, NREL_PAD), lambda i: (0, 0)), out_shape=SDS((HEADS_A, NREL_PAD), F32),
        compiler_params=_cparams(1),
    )(dbias)


def _window_bias(bias):
    b = bias.reshape(HEADS_A // 2, 2, CHUNK, BAND)
    per_chunk = [
        jnp.pad(b, ((0, 0), (0, 0), (0, 0), (cc * CHUNK, WINDOW - BAND - cc * CHUNK)), constant_values=NEG_INF)
        for cc in range(CHUNKS_PER_STEP)
    ]
    return jnp.stack(per_chunk, axis=1).reshape(HEADS_A // 2, STEP_ROWS, WINDOW)


def _window_bias_bwd(dwin):
    d = dwin.reshape(HEADS_A // 2, CHUNKS_PER_STEP, 2, CHUNK, WINDOW)
    return sum(d[:, cc, :, :, cc * CHUNK : cc * CHUNK + BAND] for cc in range(CHUNKS_PER_STEP)).reshape(HEADS_A, CHUNK, BAND)


def _step_rows(xs, lane):
    parts = []
    for cc in range(CHUNKS_PER_STEP):
        xc = xs[cc * CHUNK : (cc + 1) * CHUNK]
        parts.append(jnp.where(lane < 64, xc, jnp.zeros_like(xc)))
        parts.append(jnp.where(lane >= 64, xc, jnp.zeros_like(xc)))
    return jnp.concatenate(parts, axis=0)


def _pair_rows(ys, lane):
    parts = []
    for cc in range(CHUNKS_PER_STEP):
        y0 = ys[(2 * cc) * CHUNK : (2 * cc + 1) * CHUNK]
        y1 = ys[(2 * cc + 1) * CHUNK : (2 * cc + 2) * CHUNK]
        parts.append(jnp.where(lane < 64, y0, y1))
    return jnp.concatenate(parts, axis=0)


def _window_scores(q_rows, kwin, bias_win, first_key):
    s = _dot_nt(q_rows, kwin) * (CHUNK ** -0.5) + bias_win
    if first_key is None:
        return s
    col = lax.broadcasted_iota(jnp.int32, s.shape, 1)
    return jnp.where(col >= first_key, s, NEG_INF)


def _window_loop(n_passes, chunks):
    n_padded = min(PAD // (CHUNKS_PER_STEP * CHUNK), n_passes)
    lax.fori_loop(0, n_padded, lambda it, carry: chunks(it, carry, True), 0, unroll=2)
    if n_passes > n_padded:
        lax.fori_loop(n_padded, n_passes, lambda it, carry: chunks(it, carry, False), 0, unroll=2)


def _attn_a_fwd(qkv3, bias_win, bl, seq):
    t, dn = qkv3.shape[1:]
    npair = dn // 128
    step = CHUNKS_PER_STEP * CHUNK

    def body(q_ref, k_ref, v_ref, b_ref, o_ref, lse_ref, kpad, vpad):
        kpad[0:PAD, :] = jnp.zeros((PAD, 128), BF16)
        vpad[0:PAD, :] = jnp.zeros((PAD, 128), BF16)
        kpad[PAD:, :] = k_ref[...]
        vpad[PAD:, :] = v_ref[...]
        lane = lax.broadcasted_iota(jnp.int32, (CHUNK, 128), 1)

        def chunks(it, carry, padded):
            r0 = pl.multiple_of(it * step, step)
            q_rows = _step_rows(q_ref[pl.ds(r0, step), :], lane)
            s = _window_scores(q_rows, kpad[pl.ds(r0, WINDOW), :], b_ref[...], PAD - r0 if padded else None)
            m = jnp.max(s, axis=-1, keepdims=True)
            e = jnp.exp(s - m)
            total = jnp.sum(e, axis=-1, keepdims=True)
            o_rows = _dot((e * (1.0 / total)).astype(BF16), vpad[pl.ds(r0, WINDOW), :])
            o_ref[pl.ds(r0, step), :] = _pair_rows(o_rows, lane).astype(BF16)
            lse_ref[pl.ds(pl.multiple_of(it * STEP_ROWS, STEP_ROWS), STEP_ROWS), :] = m + jnp.log(total)
            return carry

        _window_loop(seq // step, chunks)

    return pl.pallas_call(
        body, name="attn_a_fwd", grid=(bl, npair),
        in_specs=[
            BS((None, seq, 128), lambda b, h: (0, b, h)),
            BS((None, seq, 128), lambda b, h: (1, b, h)),
            BS((None, seq, 128), lambda b, h: (2, b, h)),
            BS((None, STEP_ROWS, WINDOW), lambda b, h: (h, 0, 0)),
        ],
        out_specs=[BS((seq, 128), lambda b, h: (b, h)), BS((None, 2 * seq, 1), lambda b, h: (h, b, 0))],
        out_shape=[SDS((t, dn), BF16), SDS((npair, 2 * t, 1), F32)],
        scratch_shapes=[pltpu.VMEM((PAD + seq, 128), BF16), pltpu.VMEM((PAD + seq, 128), BF16)],
        compiler_params=_cparams(2),
    )(qkv3, qkv3, qkv3, bias_win)


def _attn_a_bwd(qkv3, out, lse, do, bias_win, bl, seq):
    t, dn = qkv3.shape[1:]
    npair = dn // 128
    step = CHUNKS_PER_STEP * CHUNK

    def body(q_ref, k_ref, v_ref, o_ref, lse_ref, do_ref, b_ref, dqkv_ref, db_ref, kpad, vpad, dkacc, dvacc):
        b = pl.program_id(1)
        kpad[0:PAD, :] = jnp.zeros((PAD, 128), BF16)
        vpad[0:PAD, :] = jnp.zeros((PAD, 128), BF16)
        kpad[PAD:, :] = k_ref[...]
        vpad[PAD:, :] = v_ref[...]
        dkacc[...] = jnp.zeros_like(dkacc)
        dvacc[...] = jnp.zeros_like(dvacc)

        @pl.when(b == 0)
        def _():
            db_ref[...] = jnp.zeros_like(db_ref)

        lane = lax.broadcasted_iota(jnp.int32, (CHUNK, 128), 1)

        def chunks(it, carry, padded):
            r0 = pl.multiple_of(it * step, step)
            q_rows = _step_rows(q_ref[pl.ds(r0, step), :], lane)
            do_rows = _step_rows(do_ref[pl.ds(r0, step), :], lane)
            kwin = kpad[pl.ds(r0, WINDOW), :]
            vwin = vpad[pl.ds(r0, WINDOW), :]
            o_rows = _step_rows(o_ref[pl.ds(r0, step), :], lane)
            delta = jnp.sum(do_rows.astype(F32) * o_rows.astype(F32), axis=-1, keepdims=True)
            lse_rows = lse_ref[pl.ds(pl.multiple_of(it * STEP_ROWS, STEP_ROWS), STEP_ROWS), :]
            p = jnp.exp(_window_scores(q_rows, kwin, b_ref[...], PAD - r0 if padded else None) - lse_rows)
            ds = p * (_dot_nt(do_rows, vwin) - delta)
            db_ref[...] += ds
            dsb = (ds * (CHUNK ** -0.5)).astype(BF16)
            dqkv_ref[0, pl.ds(r0, step), :] = _pair_rows(_dot(dsb, kwin), lane).astype(BF16)
            dkacc[pl.ds(r0, WINDOW), :] += _dot_tn(dsb, q_rows)
            dvacc[pl.ds(r0, WINDOW), :] += _dot_tn(p.astype(BF16), do_rows)
            return carry

        _window_loop(seq // step, chunks)
        dqkv_ref[1] = dkacc[PAD:, :].astype(BF16)
        dqkv_ref[2] = dvacc[PAD:, :].astype(BF16)

    return pl.pallas_call(
        body, name="attn_a_bwd", grid=(npair, bl),
        in_specs=[
            BS((None, seq, 128), lambda h, b: (0, b, h)),
            BS((None, seq, 128), lambda h, b: (1, b, h)),
            BS((None, seq, 128), lambda h, b: (2, b, h)),
            BS((seq, 128), lambda h, b: (b, h)),
            BS((None, 2 * seq, 1), lambda h, b: (h, b, 0)),
            BS((seq, 128), lambda h, b: (b, h)),
            BS((None, STEP_ROWS, WINDOW), lambda h, b: (h, 0, 0)),
        ],
        out_specs=[BS((3, seq, 128), lambda h, b: (0, b, h)), BS((None, STEP_ROWS, WINDOW), lambda h, b: (h, 0, 0))],
        out_shape=[SDS((3, t, dn), BF16), SDS((HEADS_A // 2, STEP_ROWS, WINDOW), F32)],
        scratch_shapes=[
            pltpu.VMEM((PAD + seq, 128), BF16), pltpu.VMEM((PAD + seq, 128), BF16),
            pltpu.VMEM((PAD + seq, 128), F32), pltpu.VMEM((PAD + seq, 128), F32),
        ],
        compiler_params=_cparams(2),
    )(qkv3, qkv3, qkv3, out, lse, do, bias_win)


def _rope_tables(seq):
    half = ROPE // 2
    freqs = ROPE_THETA ** (-jnp.arange(half, dtype=F32) / half)
    ang = jnp.arange(seq, dtype=F32)[:, None] * freqs[None, :]
    cos, sin = jnp.cos(ang), jnp.sin(ang)
    c64 = jnp.concatenate([cos, cos], axis=1)
    s64 = jnp.concatenate([-sin, sin], axis=1)
    c192 = jnp.concatenate([jnp.ones((seq, NOPE), F32), c64], axis=1)
    s192 = jnp.concatenate([jnp.zeros((seq, NOPE), F32), s64], axis=1)
    p64 = np.zeros((ROPE, ROPE), np.float32)
    for col in range(ROPE):
        p64[(col + half) % ROPE, col] = 1.0
    p192 = np.zeros((QK_B, QK_B), np.float32)
    p192[NOPE:, NOPE:] = p64
    return c64, s64, jnp.asarray(p64), c192, s192, jnp.asarray(p192)


def _rope(xv, cos, sin_signed, swap):
    return xv * cos + _dot_exact(xv, swap) * sin_signed


def _rope_bwd(dy, cos, sin_signed, swap):
    return dy * cos + _dot_exact(dy * sin_signed, swap)


def _q_down(hn, w_dq, q_norm):
    t, dn = hn.shape
    ql = w_dq.shape[1]
    tm = _tile(t)

    def body(x_ref, w_ref, g_ref, pre_ref, cq_ref):
        pre = _dot(x_ref[...], w_ref[...])
        pre_ref[...] = pre
        cq_ref[...] = (pre * _rms_scale(pre) * g_ref[...]).astype(BF16)

    return pl.pallas_call(
        body, name="q_down", grid=(t // tm,),
        in_specs=[BS((tm, dn), lambda i: (i, 0)), BS((dn, ql), lambda i: (0, 0)), BS((1, ql), lambda i: (0, 0))],
        out_specs=[BS((tm, ql), lambda i: (i, 0))] * 2, out_shape=[SDS((t, ql), F32), SDS((t, ql), BF16)],
        compiler_params=_cparams(1),
    )(hn, w_dq, q_norm)


def _q_up(cq, w_uq, c192, s192, p192, seq):
    t, ql = cq.shape
    tm = _tile(min(seq, 512), min(seq, 512))
    nseq = seq // tm

    def body(x_ref, w_ref, c_ref, s_ref, p_ref, o_ref):
        xv = x_ref[...]
        for h in range(HEADS_B):
            o_ref[h] = _rope(_dot(xv, w_ref[h]), c_ref[...], s_ref[...], p_ref[...]).astype(BF16)

    pos = BS((tm, QK_B), lambda i: (i % nseq, 0))
    return pl.pallas_call(
        body, name="q_up", grid=(t // tm,),
        in_specs=[
            BS((tm, ql), lambda i: (i, 0)), BS((HEADS_B, ql, QK_B), lambda i: (0, 0, 0)), pos, pos,
            BS((QK_B, QK_B), lambda i: (0, 0)),
        ],
        out_specs=BS((HEADS_B, tm, QK_B), lambda i: (0, i, 0)), out_shape=SDS((HEADS_B, t, QK_B), BF16),
        compiler_params=_cparams(1),
    )(cq, w_uq, c192, s192, p192)


def _kv_down(hk, w_down, latent_norm, c64, s64, p64, seq):
    t, dn = hk.shape
    wd = w_down.shape[1]
    tm = _tile(min(seq, 512), min(seq, 512))
    nseq = seq // tm

    def body(x_ref, w_ref, g_ref, c_ref, s_ref, p_ref, ckr_ref, ckv_ref, kr_ref):
        ckr = _dot(x_ref[...], w_ref[...])
        ckr_ref[...] = ckr
        lat = ckr[:, :KV_LORA]
        ckv_ref[...] = (lat * _rms_scale(lat) * g_ref[...]).astype(BF16)
        kr_ref[...] = _rope(ckr[:, KV_LORA:], c_ref[...], s_ref[...], p_ref[...]).astype(BF16)

    pos = BS((tm, ROPE), lambda i: (i % nseq, 0))
    return pl.pallas_call(
        body, name="kv_down", grid=(t // tm,),
        in_specs=[
            BS((tm, dn), lambda i: (i, 0)), BS((dn, wd), lambda i: (0, 0)), BS((1, KV_LORA), lambda i: (0, 0)), pos, pos,
            BS((ROPE, ROPE), lambda i: (0, 0)),
        ],
        out_specs=[BS((tm, wd), lambda i: (i, 0)), BS((tm, KV_LORA), lambda i: (i, 0)), BS((tm, ROPE), lambda i: (i, 0))],
        out_shape=[SDS((t, wd), F32), SDS((t, KV_LORA), BF16), SDS((t, ROPE), BF16)],
        compiler_params=_cparams(1),
    )(hk, w_down, latent_norm, c64, s64, p64)


def _kv_up(ckv, w_up):
    t, kl = ckv.shape
    hb = w_up.shape[-1]
    tm = _tile(t)

    def body(x_ref, w_ref, o_ref):
        xv = x_ref[...]
        for h in range(HEADS_B):
            o_ref[:, h * hb : (h + 1) * hb] = _dot(xv, w_ref[h]).astype(BF16)

    return pl.pallas_call(
        body, name="kv_up", grid=(t // tm,),
        in_specs=[BS((tm, kl), lambda i: (i, 0)), BS((HEADS_B, kl, hb), lambda i: (0, 0, 0))],
        out_specs=BS((tm, HEADS_B * hb), lambda i: (i, 0)), out_shape=SDS((t, HEADS_B * hb), BF16),
        compiler_params=_cparams(1),
    )(ckv, w_up)


def _mla_diagonal_mask(tq):
    rows = lax.broadcasted_iota(jnp.int32, (tq, tq), 0)
    cols = lax.broadcasted_iota(jnp.int32, (tq, tq), 1)
    return jnp.where(jnp.right_shift(cols, 6) <= jnp.right_shift(rows, 6), 0.0, NEG_INF)


def _mla_key_tiles(n_keys, tk):
    return [(slice(k0, min(k0 + tk, n_keys)), min(k0 + tk, n_keys) == n_keys) for k0 in range(0, n_keys, tk)]


def _mla_scores(qi, kt, diagonal):
    s = _dot_nt(qi, kt) * (QK_B ** -0.5)
    if diagonal is None:
        return s
    tq, width = s.shape
    own = s[:, width - tq :] + diagonal
    return own if width == tq else jnp.concatenate([s[:, : width - tq], own], axis=1)


def _mla_fwd(q, kv, kr, bl, seq):
    t = kv.shape[0]
    tq = min(MLA_TQ, seq)

    def body(q_ref, kn_ref, v_ref, kr_ref, o_ref, lse_ref):
        kcat = jnp.concatenate([kn_ref[...], kr_ref[...]], axis=1)
        vv = v_ref[...]
        diagonal = _mla_diagonal_mask(tq)
        for i in range(seq // tq):
            rows = slice(i * tq, (i + 1) * tq)
            qi = q_ref[rows, :]
            m = total = acc = None
            for keys, own in _mla_key_tiles((i + 1) * tq, MLA_TK_FWD):
                s = _mla_scores(qi, kcat[keys], diagonal if own else None)
                m_blk = jnp.max(s, axis=-1, keepdims=True)
                if m is None:
                    m_new = m_blk
                    e = jnp.exp(s - m_new)
                    total = jnp.sum(e, axis=-1, keepdims=True)
                    acc = _dot(e.astype(BF16), vv[keys])
                else:
                    m_new = jnp.maximum(m, m_blk)
                    keep = jnp.exp(m - m_new)
                    e = jnp.exp(s - m_new)
                    total = keep * total + jnp.sum(e, axis=-1, keepdims=True)
                    acc = keep * acc + _dot(e.astype(BF16), vv[keys])
                m = m_new
            o_ref[rows, :] = (acc / total).astype(BF16)
            lse_ref[rows, :] = m + jnp.log(total)

    return pl.pallas_call(
        body, name="mla_fwd", grid=(bl, HEADS_B),
        in_specs=[
            BS((None, seq, QK_B), lambda b, h: (h, b, 0)),
            BS((seq, NOPE), lambda b, h: (b, 2 * h)),
            BS((seq, V_DIM), lambda b, h: (b, 2 * h + 1)),
            BS((seq, ROPE), lambda b, h: (b, 0)),
        ],
        out_specs=[BS((seq, V_DIM), lambda b, h: (b, h)), BS((None, seq, 1), lambda b, h: (h, b, 0))],
        out_shape=[SDS((t, HEADS_B * V_DIM), BF16), SDS((HEADS_B, t, 1), F32)],
        compiler_params=_cparams(2),
    )(q, kv, kv, kr)


def _mla_bwd(q, kv, kr, o, lse, do, c192, s192, p192, bl, seq):
    t = kv.shape[0]
    tq = min(MLA_TQ, seq)

    def body(q_ref, kn_ref, v_ref, kr_ref, o_ref, lse_ref, do_ref, c_ref, s_ref, p_ref, dq_ref, dkv_ref, dkr_ref, dkacc, dvacc):
        h = pl.program_id(1)
        kcat = jnp.concatenate([kn_ref[...], kr_ref[...]], axis=1)
        vv = v_ref[...]
        dkacc[...] = jnp.zeros_like(dkacc)
        dvacc[...] = jnp.zeros_like(dvacc)
        diagonal = _mla_diagonal_mask(tq)
        for i in range(seq // tq):
            rows = slice(i * tq, (i + 1) * tq)
            qi = q_ref[rows, :]
            doi = do_ref[rows, :]
            lse_i = lse_ref[rows, :]
            delta = jnp.sum(doi.astype(F32) * o_ref[rows, :].astype(F32), axis=-1, keepdims=True)
            dq = None
            for keys, own in _mla_key_tiles((i + 1) * tq, MLA_TK_BWD):
                p = jnp.exp(_mla_scores(qi, kcat[keys], diagonal if own else None) - lse_i)
                ds = p * (_dot_nt(doi, vv[keys]) - delta)
                dsb = (ds * (QK_B ** -0.5)).astype(BF16)
                dq_blk = _dot(dsb, kcat[keys])
                dq = dq_blk if dq is None else dq + dq_blk
                dkacc[keys, :] += _dot_tn(dsb, qi)
                dvacc[keys, :] += _dot_tn(p.astype(BF16), doi)
            dq_ref[rows, :] = _rope_bwd(dq, c_ref[rows, :], s_ref[rows, :], p_ref[...]).astype(BF16)
        dk = dkacc[...]
        dkv_ref[:, :NOPE] = dk[:, :NOPE].astype(BF16)
        dkv_ref[:, NOPE:] = dvacc[...].astype(BF16)

        @pl.when(h == 0)
        def _():
            dkr_ref[...] = dk[:, NOPE:]

        @pl.when(h > 0)
        def _():
            dkr_ref[...] += dk[:, NOPE:]

    return pl.pallas_call(
        body, name="mla_bwd", grid=(bl, HEADS_B),
        in_specs=[
            BS((None, seq, QK_B), lambda b, h: (h, b, 0)),
            BS((seq, NOPE), lambda b, h: (b, 2 * h)),
            BS((seq, V_DIM), lambda b, h: (b, 2 * h + 1)),
            BS((seq, ROPE), lambda b, h: (b, 0)),
            BS((seq, V_DIM), lambda b, h: (b, h)),
            BS((None, seq, 1), lambda b, h: (h, b, 0)),
            BS((seq, V_DIM), lambda b, h: (b, h)),
            BS((seq, QK_B), lambda b, h: (0, 0)),
            BS((seq, QK_B), lambda b, h: (0, 0)),
            BS((QK_B, QK_B), lambda b, h: (0, 0)),
        ],
        out_specs=[
            BS((None, seq, QK_B), lambda b, h: (h, b, 0)),
            BS((seq, NOPE + V_DIM), lambda b, h: (b, h)),
            BS((seq, ROPE), lambda b, h: (b, 0)),
        ],
        out_shape=[SDS((HEADS_B, t, QK_B), BF16), SDS((t, HEADS_B * (NOPE + V_DIM)), BF16), SDS((t, ROPE), F32)],
        scratch_shapes=[pltpu.VMEM((seq, QK_B), F32), pltpu.VMEM((seq, V_DIM), F32)],
        compiler_params=_cparams(2),
    )(q, kv, kv, kr, o, lse, do, c192, s192, p192)


def _loss_final(h, target, gamma):
    t, dn = h.shape
    tm = _tile(t)
    nt = t // tm

    def body(h_ref, t_ref, g_ref, dh_ref, dhb_ref, dg_ref, loss_ref):
        i = pl.program_id(0)
        hv = h_ref[...]
        r = _rms_scale(hv)
        hh = hv * r
        gam = g_ref[...]
        err = hh * gam - t_ref[...]
        part = 0.5 * jnp.sum(jnp.mean(err * err, axis=-1, keepdims=True))

        @pl.when(i == 0)
        def _():
            loss_ref[...] = jnp.zeros_like(loss_ref)

        loss_ref[...] += part
        dy = err * (1.0 / dn)
        _acc_rows(dg_ref, dy * hh, i, nt)
        t1 = dy * gam
        dh = r * (t1 - hh * jnp.mean(t1 * hh, axis=-1, keepdims=True))
        dh_ref[...] = dh
        dhb_ref[...] = dh.astype(BF16)

    row = BS((tm, dn), lambda i: (i, 0))
    return pl.pallas_call(
        body, name="loss_final", grid=(nt,),
        in_specs=[row, row, BS((1, dn), lambda i: (0, 0))],
        out_specs=[row, row, BS((8, dn), lambda i: (0, 0)), BS((8, 128), lambda i: (0, 0))],
        out_shape=[SDS((t, dn), F32), SDS((t, dn), BF16), SDS((8, dn), F32), SDS((8, 128), F32)],
        compiler_params=_cparams(1),
    )(h, target, gamma)


def _ffn_bwd_in(name, dh, w_out, layer, gu):
    t, dn = dh.shape
    tm = _tile(t, 1024)

    def body(dh_ref, w_ref, gu_ref, o_ref):
        da = 0.5 * _dot_nt(dh_ref[...], w_ref[...])
        g = gu_ref[0].astype(F32)
        u = gu_ref[1].astype(F32)
        sg = jax.nn.sigmoid(g)
        o_ref[0] = (da * u * (sg * (1.0 + g * (1.0 - sg)))).astype(BF16)
        o_ref[1] = (da * (g * sg)).astype(BF16)

    blk = BS((None, 2, tm, FB), lambda j, i: (j, 0, i, 0))
    return pl.pallas_call(
        body, name=name, grid=(NJ, t // tm),
        in_specs=[BS((tm, dn), lambda j, i: (i, 0)), BS((None, None, FB, dn), lambda j, i: (layer, j, 0, 0)), blk],
        out_specs=blk, out_shape=SDS((NJ, 2, t, FB), BF16),
        compiler_params=_cparams(2),
    )(dh, w_out, gu)


def _mm_nt_plain(name, xf, w):
    t, dn = xf.shape
    n = w.shape[0]
    tm = _tile(t)

    def body(x_ref, w_ref, o_ref):
        o_ref[...] = _dot_nt(x_ref[...], w_ref[...]).astype(BF16)

    return pl.pallas_call(
        body, name=name, grid=(t // tm,),
        in_specs=[BS((tm, dn), lambda i: (i, 0)), BS((n, dn), lambda i: (0, 0))],
        out_specs=BS((tm, n), lambda i: (i, 0)), out_shape=SDS((t, n), BF16),
        compiler_params=_cparams(1),
    )(xf, w)


def _mm_tn(name, xa, x_spec, ya, y_spec, out_shape, out_spec, nj, scale=None):
    def body(x_ref, y_ref, o_ref):
        acc = _dot_tn(x_ref[...], y_ref[...])
        o_ref[...] = (acc if scale is None else scale * acc).astype(BF16)

    return pl.pallas_call(
        body, name=name, grid=(nj,),
        in_specs=[x_spec, y_spec], out_specs=out_spec, out_shape=SDS(out_shape, BF16),
        compiler_params=_cparams(1),
    )(xa, ya)


def _dw_qkv(hn, dqkv3, wb):
    t, dn = hn.shape
    per = wb // 128

    def body(x_ref, *refs):
        cols = [y_ref[...] for y_ref in refs[:per]]
        refs[per][...] = _dot_tn(x_ref[...], jnp.concatenate(cols, axis=1)).astype(BF16)

    def piece(k):
        return BS((None, t, 128), lambda j: ((per * j + k) // 8, 0, (per * j + k) % 8))

    return pl.pallas_call(
        body, name="dw_qkv", grid=(NDEV,),
        in_specs=[BS((t, dn), lambda j: (0, 0))] + [piece(k) for k in range(per)],
        out_specs=BS((None, dn, wb), lambda j: (j, 0, 0)), out_shape=SDS((NDEV, dn, wb), BF16),
        compiler_params=_cparams(1),
    )(hn, *([dqkv3] * per))


def _mm_nt_epi(name, ya, y_spec, wa, w_spec, nj, n_out, extra, out_shapes, out_specs, epilogue, tm, nt, mm_fn=None):
    n_extra = len(extra)
    n_outs = len(out_shapes)

    def body(*refs):
        y_ref, w_ref = refs[:2]
        ex = refs[2 : 2 + n_extra]
        outs = refs[2 + n_extra : 2 + n_extra + n_outs]
        i = pl.program_id(0)
        j = pl.program_id(1)
        part = _dot_nt(y_ref[...], w_ref[...]) if mm_fn is None else mm_fn(y_ref, w_ref)
        if nj == 1:
            epilogue(part, ex, outs, i, nt)
            return
        acc = refs[-1]

        @pl.when(j == 0)
        def _():
            acc[...] = part

        @pl.when(j > 0)
        def _():
            acc[...] += part

        @pl.when(j == nj - 1)
        def _():
            epilogue(acc[...], ex, outs, i, nt)

    return pl.pallas_call(
        body, name=name, grid=(nt, nj),
        in_specs=[y_spec, w_spec] + [spec for _, spec in extra],
        out_specs=out_specs, out_shape=out_shapes,
        scratch_shapes=[] if nj == 1 else [pltpu.VMEM((tm, n_out), F32)],
        compiler_params=_cparams(2),
    )(ya, wa, *[arr for arr, _ in extra])


def _norm_bwd(dn, hv, gam):
    r = _rms_scale(hv)
    hh = hv * r
    t1 = dn * gam
    return r * (t1 - hh * jnp.mean(t1 * hh, axis=-1, keepdims=True)), dn * hh


def _norm_bwd_epilogue(has_res, out_dtype):
    def epilogue(dn, ex, outs, i, nt):
        dh, dg_rows = _norm_bwd(dn, ex[0][...], ex[1][...])
        _acc_rows(outs[1], dg_rows, i, nt)
        if has_res:
            dh = dh + ex[2][...]
        outs[0][...] = dh.astype(out_dtype)
        if has_res:
            outs[2][...] = dh.astype(BF16)

    return epilogue


def _mm_nt_norm_bwd(name, ya, y_spec, wa, w_spec, nj, h, gamma, res, out_dtype, mm_fn=None, want_tm=512, after=None):
    t, n = h.shape
    tm = _tile(t, want_tm)
    nt = t // tm
    row = BS((tm, n), lambda i, j: (i, 0))
    extra = [(h, row), (gamma, BS((1, n), lambda i, j: (0, 0)))]
    out_shapes = [SDS((t, n), out_dtype), SDS((8, n), F32)]
    out_specs = [row, BS((8, n), lambda i, j: (0, 0))]
    if res is not None:
        extra.append((res, row))
        out_shapes.append(SDS((t, n), BF16))
        out_specs.append(row)
    extra.extend((a, BS(memory_space=pl.ANY)) for a in after or ())
    return _mm_nt_epi(
        name, ya, y_spec, wa, w_spec, nj, n, extra, out_shapes, out_specs, _norm_bwd_epilogue(res is not None, out_dtype), tm, nt, mm_fn,
    )


def _dev_block(jj):
    return jj // 2 + NJ * (jj % 2)


def _ffn_dn_mm(y_ref, w_ref):
    acc = None
    for jj in range(2 * NJ):
        part = _dot_nt(y_ref[jj], w_ref[_dev_block(jj)])
        acc = part if acc is None else acc + part
    return acc


def _ffn_bwd(tag, dh, dhb, n_in, h_in, gamma, gu, a, w_in, w_out, collective_id, after):
    t, dn = dh.shape
    dgu = _ffn_bwd_in(f"{tag}_bwd_in", dhb, w_out, 0, gu).reshape(2 * NJ, t, FB)
    dw_out = _mm_tn(
        f"{tag}_dw_out", a, BS((None, t, FB), lambda j: (j, 0, 0)), dhb, BS((t, dn), lambda j: (0, 0)),
        (NJ, FB, dn), BS((None, FB, dn), lambda j: (j, 0, 0)), NJ, scale=0.5,
    )
    dw_in = _mm_tn(
        f"{tag}_dw_in", dgu, BS((None, t, FB), lambda j: (j, 0, 0)), n_in, BS((t, dn), lambda j: (0, 0)),
        (NDEV, FB, dn), BS((None, FB, dn), lambda j: (_dev_block(j), 0, 0)), NDEV,
    )
    entries = [("scatter", dw_in), ("scatter", dw_out.reshape(NDEV, NJ * FB // NDEV, dn))]
    if isinstance(collective_id, tuple):
        landed_out = _exchange_sc(f"{tag}_reduce_out", entries[1:], collective_id[0], after)
        landed = list(_exchange_sc(f"{tag}_reduce_in", entries[:1], collective_id[1], landed_out)) + list(landed_out)
    else:
        landed = _exchange_sc(f"{tag}_reduce", entries, collective_id, after)
    tm = _tile(t)
    resident = BS((None, NDEV, dn, FB), lambda i, j: (0, 0, 0, 0), pipeline_mode=pl.Buffered(1))
    dh_in, dgam, dhb_in = _mm_nt_norm_bwd(
        f"{tag}_dn", dgu, BS((2 * NJ, tm, FB), lambda i, j: (0, i, 0)), w_in, resident, 1, h_in, gamma, dh, F32, mm_fn=_ffn_dn_mm,
        after=[e[1] for e in entries],
    )
    return dh_in, dhb_in, dgam, landed


def _heads_mm(y_ref, w_ref):
    acc = None
    for h in range(HEADS_B):
        part = _dot_nt(y_ref[h], w_ref[h])
        acc = part if acc is None else acc + part
    return acc


def _dqkv_mm(per):
    def mm(y_ref, w_ref):
        acc = None
        for j in range(NDEV):
            cols = [y_ref[(per * j + k) // 8, :, ((per * j + k) % 8) * 128 : ((per * j + k) % 8 + 1) * 128] for k in range(per)]
            part = _dot_nt(jnp.concatenate(cols, axis=1), w_ref[j])
            acc = part if acc is None else acc + part
        return acc

    return mm


def _kv_latent_bwd(dkv, w_up, ckr, latent_norm, dkr, c64, s64, p64, seq):
    t, wd = ckr.shape
    hb = w_up.shape[-1]
    tm = _tile(min(seq, 512), min(seq, 512))
    nt = t // tm
    nseq = seq // tm

    def epilogue(dn, ex, outs, i, nt_):
        dlat, dg_rows = _norm_bwd(dn, ex[0][...], ex[1][...])
        _acc_rows(outs[1], dg_rows, i, nt_)
        outs[0][:, :KV_LORA] = dlat.astype(BF16)
        outs[0][:, KV_LORA:] = _rope_bwd(ex[2][...], ex[3][...], ex[4][...], ex[5][...]).astype(BF16)

    pos = BS((tm, ROPE), lambda i, j: (i % nseq, 0))
    extra = [
        (ckr, BS((tm, KV_LORA), lambda i, j: (i, 0))), (latent_norm, BS((1, KV_LORA), lambda i, j: (0, 0))),
        (dkr, BS((tm, ROPE), lambda i, j: (i, 0))), (c64, pos), (s64, pos), (p64, BS((ROPE, ROPE), lambda i, j: (0, 0))),
    ]
    def heads_mm(y_ref, w_ref):
        acc = None
        for h in range(HEADS_B):
            part = _dot_nt(y_ref[:, h * hb : (h + 1) * hb], w_ref[h])
            acc = part if acc is None else acc + part
        return acc

    return _mm_nt_epi(
        "kv_latent_bwd", dkv, BS((tm, HEADS_B * hb), lambda i, j: (i, 0)), w_up, BS((HEADS_B, KV_LORA, hb), lambda i, j: (0, 0, 0)),
        1, KV_LORA, extra, [SDS((t, wd), BF16), SDS((8, KV_LORA), F32)],
        [BS((tm, wd), lambda i, j: (i, 0)), BS((8, KV_LORA), lambda i, j: (0, 0))], epilogue, tm, nt, heads_mm,
    )


def _adamw(name, parts, w, m, v):
    n_layers, rows, cols = w.shape
    tr = max(d for d in range(8, min(rows, 256) + 1, 8) if rows % d == 0)
    nb = rows // tr

    def body(*refs):
        p_refs = refs[:n_layers]
        w_ref, m_ref, v_ref, g_ref, d_ref, nm_ref, nv_ref = refs[n_layers : n_layers + 7]
        layer = pl.program_id(0)
        for lp in range(n_layers):

            @pl.when(layer == lp)
            def _():
                g = p_refs[lp][0].astype(F32)
                for k in range(1, NDEV):
                    g = g + p_refs[lp][k].astype(F32)
                g_ref[...] = g

        g = g_ref[...]
        nm = ADAM_B1 * m_ref[...] + (1.0 - ADAM_B1) * g
        nv = ADAM_B2 * v_ref[...] + (1.0 - ADAM_B2) * (g * g)
        nm_ref[...] = nm
        nv_ref[...] = nv
        m_hat = nm / (1.0 - ADAM_B1 ** ADAM_STEP)
        v_hat = nv / (1.0 - ADAM_B2 ** ADAM_STEP)
        d_ref[...] = -ADAM_LR * (m_hat / (jnp.sqrt(v_hat) + ADAM_EPS) + ADAM_WD * w_ref[...])

    def part_spec(lp):
        return BS((NDEV, tr, cols), lambda l, i: (0, jnp.where(l == lp, i, jnp.where(l < lp, 0, nb - 1)), 0))

    row = BS((None, tr, cols), lambda l, i: (l, i, 0))
    return pl.pallas_call(
        body, name=name, grid=(n_layers, nb),
        in_specs=[part_spec(lp) for lp in range(n_layers)] + [row, row, row],
        out_specs=[row] * 4, out_shape=[SDS(w.shape, F32)] * 4,
        compiler_params=_cparams(2),
    )(*parts, w, m, v)


def _pack_small(ffn1_norm, mix_norm, ffn2_norm, kv_norm, final_norm, q_norm, latent_norm, rel_bias, last_row):
    dn = ffn1_norm.shape[-1]

    def rows_of(a, n_rows):
        flat = a.reshape(-1)
        return jnp.pad(flat, (0, n_rows * dn - flat.shape[0])).reshape(n_rows, dn)

    return jnp.concatenate(
        [
            ffn1_norm.reshape(2, dn), mix_norm.reshape(2, dn), ffn2_norm.reshape(2, dn), kv_norm.reshape(1, dn),
            final_norm.reshape(1, dn), rows_of(q_norm, 1), rows_of(latent_norm, 1), rows_of(rel_bias, 5), rows_of(last_row, 1),
        ],
        axis=0,
    )


def _unpack_small(pack):
    dn = pack.shape[-1]
    return dict(
        ffn1_norm=pack[0:2], mix_norm=pack[2:4], ffn2_norm=pack[4:6], kv_norm=pack[6], final_norm=pack[7],
        b_q_norm=pack[8, :Q_LORA].reshape(1, Q_LORA), kv_latent_norm=pack[9, :KV_LORA],
        a_rel_bias=pack[10:15].reshape(-1)[: HEADS_A * NREL].reshape(1, HEADS_A, NREL), last=pack[15],
    )


def kernel(x, ffn1_norm, ffn1_w_in, ffn1_w_out, mix_norm, ffn2_norm, ffn2_w_in, ffn2_w_out, a_w_qkv, a_rel_bias, a_w_o, kv_norm, kv_w_down, kv_latent_norm, kv_w_up, b_w_dq, b_q_norm, b_w_uq, b_w_o, final_norm, loss_target, m_ffn1_norm, m_ffn1_w_in, m_ffn1_w_out, m_mix_norm, m_ffn2_norm, m_ffn2_w_in, m_ffn2_w_out, m_a_w_qkv, m_a_rel_bias, m_a_w_o, m_kv_norm, m_kv_w_down, m_kv_latent_norm, m_kv_w_up, m_b_w_dq, m_b_q_norm, m_b_w_uq, m_b_w_o, m_final_norm, v_ffn1_norm, v_ffn1_w_in, v_ffn1_w_out, v_mix_norm, v_ffn2_norm, v_ffn2_w_in, v_ffn2_w_out, v_a_w_qkv, v_a_rel_bias, v_a_w_o, v_kv_norm, v_kv_w_down, v_kv_latent_norm, v_kv_w_up, v_b_w_dq, v_b_q_norm, v_b_w_uq, v_b_w_o, v_final_norm):
    bl, seq, dn = x.shape
    t = bl * seq
    tm = _tile(t)
    nt = t // tm
    x2 = x.reshape(t, dn)
    target2 = loss_target.reshape(t, dn)

    def gathered(*ws):
        return [("gather", w.astype(BF16)) for w in ws]

    groups = [
        gathered(ffn1_w_in[0]), gathered(ffn1_w_out[0]), gathered(a_w_qkv[0], a_w_o[0]), gathered(ffn2_w_in[0], ffn2_w_out[0]),
        gathered(kv_w_down, kv_w_up), gathered(ffn1_w_in[1], ffn1_w_out[1]), gathered(b_w_dq[0], b_w_uq[0], b_w_o[0]),
        gathered(ffn2_w_in[1], ffn2_w_out[1]),
    ]
    ag = [_exchange_sc(f"gather_{k}", group, GATHER_IDS[k]) for k, group in enumerate(groups)]

    def as_w_in(w):
        return w.reshape(1, NDEV, dn, FB)

    def as_w_out(w):
        return w.reshape(1, NJ, FB, dn)

    c64, s64, p64, c192, s192, p192 = _rope_tables(seq)
    q_norm = b_q_norm.reshape(1, Q_LORA)
    latent_norm = kv_latent_norm.reshape(1, KV_LORA)
    bias = _window_bias(_rel_bias_fwd(jnp.pad(a_rel_bias[0], ((0, 0), (0, NREL_PAD - NREL)))))

    h0, h1, h2, n1, hn, n2, gu1, gu2, a1, a2, w_in1, w_in2, w_out1, w_out2 = ([None, None] for _ in range(14))
    h0[0] = x2
    (n1[0],) = _norm_fwd("norm_x", x2, ffn1_norm[0:1])
    w_in1[0] = as_w_in(ag[0][0])
    gu1[0], a1[0] = _ffn_in("ffn1_in_0", n1[0], w_in1[0], 0)
    w_out1[0] = as_w_out(ag[1][0])
    h1[0], hn[0] = _mm_res_norm("ffn1_out_0", a1[0], w_out1[0], 0, h0[0], mix_norm[0:1], 0.5)
    w_qkv, w_o_a = ag[2]
    qkv_wb = w_qkv.shape[-1]
    w_o_a = w_o_a.reshape(1, 1, dn, dn)
    qkv3 = _qkv_proj("qkv_proj", hn[0], w_qkv)
    o_a, lse_a = _attn_a_fwd(qkv3, bias, bl, seq)
    h2[0], n2[0] = _mm_res_norm("attn_a_out", o_a.reshape(1, t, dn), w_o_a, 0, h1[0], ffn2_norm[0:1], 1.0)
    w_in2[0], w_out2[0] = as_w_in(ag[3][0]), as_w_out(ag[3][1])
    gu2[0], a2[0] = _ffn_in("ffn2_in_0", n2[0], w_in2[0], 0)
    h0[1], hk, n1[1] = _mm_res_norm(
        "ffn2_out_0", a2[0], w_out2[0], 0, h2[0], jnp.concatenate([kv_norm.reshape(1, dn), ffn1_norm[1:2]], axis=0), 0.5
    )
    w_down, w_up = ag[4]
    w_down = w_down.reshape(dn, KV_LORA + ROPE)
    ckr, ckv, kr = _kv_down(hk, w_down, latent_norm, c64, s64, p64, seq)
    kv = _kv_up(ckv, w_up)
    w_in1[1], w_out1[1] = as_w_in(ag[5][0]), as_w_out(ag[5][1])
    gu1[1], a1[1] = _ffn_in("ffn1_in_1", n1[1], w_in1[1], 0)
    h1[1], hn[1] = _mm_res_norm("ffn1_out_1", a1[1], w_out1[1], 0, h0[1], mix_norm[1:2], 0.5)
    w_dq, w_uq, w_o_b = ag[6]
    w_dq = w_dq.reshape(dn, Q_LORA)
    w_o_b = w_o_b.reshape(1, 1, dn, dn)
    cq_pre, cq = _q_down(hn[1], w_dq, q_norm)
    q = _q_up(cq, w_uq, c192, s192, p192, seq)
    o_b, lse_b = _mla_fwd(q, kv, kr, bl, seq)
    h2[1], n2[1] = _mm_res_norm("attn_b_out", o_b.reshape(1, t, dn), w_o_b, 0, h1[1], ffn2_norm[1:2], 1.0)
    w_in2[1], w_out2[1] = as_w_in(ag[7][0]), as_w_out(ag[7][1])
    gu2[1], a2[1] = _ffn_in("ffn2_in_1", n2[1], w_in2[1], 0)
    (h_last,) = _mm_res_norm("ffn2_out_1", a2[1], w_out2[1], 0, h2[1], None, 0.5)
    dh, dhb, dg_final, loss_part = _loss_final(h_last, target2, final_norm.reshape(1, dn))

    dg_ffn1, dg_mix, dg_ffn2, rs_ffn1, rs_ffn2 = ([None, None] for _ in range(5))

    def whole(rows, cols):
        return BS((rows, cols), lambda j: (0, 0))

    def dw_rows(name, xa, ya):
        n = ya.shape[1]
        return _mm_tn(name, xa, whole(t, dn), ya, whole(t, n), (dn, n), whole(dn, n), 1).reshape(NDEV, dn // NDEV, n)

    dh, dhb, dg_ffn2[1], rs_ffn2[1] = _ffn_bwd(
        "ffn2_1", dh, dhb, n2[1], h2[1], ffn2_norm[1:2], gu2[1], a2[1], w_in2[1], w_out2[1], REDUCE_IDS[0], ()
    )
    do_b = _mm_nt_plain("attn_b_do", dhb, w_o_b.reshape(dn, dn))
    dw_o_b = dw_rows("attn_b_dwo", o_b, dhb)
    dq_pre, dkv, dkr = _mla_bwd(q, kv, kr, o_b, lse_b, do_b, c192, s192, p192, bl, seq)
    dw_uq = _mm_tn(
        "dw_uq", cq, whole(t, Q_LORA), dq_pre, BS((None, t, QK_B), lambda j: (j, 0, 0)),
        (HEADS_B, Q_LORA, QK_B), BS((None, Q_LORA, QK_B), lambda j: (j, 0, 0)), HEADS_B,
    )
    dcq_pre, dg_q = _mm_nt_norm_bwd(
        "dcq", dq_pre, BS((HEADS_B, tm, QK_B), lambda i, j: (0, i, 0)), w_uq, BS((HEADS_B, Q_LORA, QK_B), lambda i, j: (0, 0, 0)),
        1, cq_pre, q_norm, None, BF16, mm_fn=_heads_mm,
    )
    dw_dq = dw_rows("dw_dq", hn[1], dcq_pre)
    dh, dg_mix[1], dhb = _mm_nt_norm_bwd(
        "dhn_b", dcq_pre, BS((tm, Q_LORA), lambda i, j: (i, 0)), w_dq, BS((dn, Q_LORA), lambda i, j: (0, 0)),
        1, h1[1], mix_norm[1:2], dh, F32,
    )
    dh, dhb, dg_ffn1[1], rs_ffn1[1] = _ffn_bwd(
        "ffn1_1", dh, dhb, n1[1], h0[1], ffn1_norm[1:2], gu1[1], a1[1], w_in1[1], w_out1[1], REDUCE_IDS[1], rs_ffn2[1][:1]
    )
    dw_up = _mm_tn(
        "dw_up", ckv, whole(t, KV_LORA), dkv, BS((t, NOPE + V_DIM), lambda j: (0, j)),
        (HEADS_B, KV_LORA, NOPE + V_DIM), BS((None, KV_LORA, NOPE + V_DIM), lambda j: (j, 0, 0)), HEADS_B,
    )
    dckr, dg_latent = _kv_latent_bwd(dkv, w_up, ckr, latent_norm, dkr, c64, s64, p64, seq)
    dw_down = dw_rows("dw_down", hk, dckr)
    dh, dg_kv, dhb = _mm_nt_norm_bwd(
        "dhk", dckr, BS((tm, KV_LORA + ROPE), lambda i, j: (i, 0)), w_down, BS((dn, KV_LORA + ROPE), lambda i, j: (0, 0)),
        1, h0[1], kv_norm.reshape(1, dn), dh, F32,
    )
    dh, dhb, dg_ffn2[0], rs_ffn2[0] = _ffn_bwd(
        "ffn2_0", dh, dhb, n2[0], h2[0], ffn2_norm[0:1], gu2[0], a2[0], w_in2[0], w_out2[0], REDUCE_IDS[2], rs_ffn1[1][:1]
    )
    do_a = _mm_nt_plain("attn_a_do", dhb, w_o_a.reshape(dn, dn))
    dw_o_a = dw_rows("attn_a_dwo", o_a, dhb)
    dqkv3, dbias = _attn_a_bwd(qkv3, o_a, lse_a, do_a, bias, bl, seq)
    dw_qkv = _dw_qkv(hn[0], dqkv3, qkv_wb)
    mixer_grads = [dw_o_a, dw_qkv, dw_o_b, dw_uq, dw_dq, dw_up, dw_down]
    dh, dg_mix[0], dhb = _mm_nt_norm_bwd(
        "dhn_a", dqkv3, BS((3, tm, dn), lambda i, j: (0, i, 0)), w_qkv, BS((NDEV, dn, qkv_wb), lambda i, j: (0, 0, 0)),
        1, h1[0], mix_norm[0:1], dh, F32, mm_fn=_dqkv_mm(qkv_wb // 128), after=mixer_grads,
    )
    rs_mixers = _exchange_sc("mixers_reduce", [("scatter", g) for g in mixer_grads], REDUCE_IDS[3], rs_ffn2[0][:1])
    dh, dhb, dg_ffn1[0], rs_ffn1[0] = _ffn_bwd(
        "ffn1_0", dh, dhb, n1[0], h0[0], ffn1_norm[0:1], gu1[0], a1[0], w_in1[0], w_out1[0], REDUCE_IDS[4:6], rs_mixers[:1]
    )
    grad_x = dh.reshape(bl, seq, dn)
    dtable = _rel_bias_bwd(_window_bias_bwd(dbias))[:, :NREL]

    def update(name, parts, w, m, v):
        shape3 = (len(parts),) + w.shape[-2:]
        parts = [p.reshape((NDEV,) + shape3[1:]) for p in parts]
        outs = _adamw(name, parts, w.reshape(shape3), m.reshape(shape3), v.reshape(shape3))
        return [o.reshape(w.shape) for o in outs]

    res = {}
    r_in2_1, r_out2_1 = rs_ffn2[1]
    r_in1_1, r_out1_1 = rs_ffn1[1]
    r_in2_0, r_out2_0 = rs_ffn2[0]
    r_in1_0, r_out1_0 = rs_ffn1[0]
    r_o_a, r_qkv, r_o_b, r_uq, r_dq, r_up, r_down = rs_mixers
    def update_transposed(name, parts, w, m, v):
        outs = update(name, parts, *[jnp.swapaxes(a, 1, 2) for a in (w, m, v)])
        return [jnp.swapaxes(o, 1, 2) for o in outs]

    res["ffn2_w_in"] = update_transposed("adamw_ffn2_w_in", [r_in2_0, r_in2_1], ffn2_w_in, m_ffn2_w_in, v_ffn2_w_in)
    res["ffn2_w_out"] = update("adamw_ffn2_w_out", [r_out2_0, r_out2_1], ffn2_w_out, m_ffn2_w_out, v_ffn2_w_out)
    res["kv_w_down"] = update("adamw_kv_w_down", [r_down], kv_w_down, m_kv_w_down, v_kv_w_down)
    res["kv_w_up"] = update("adamw_kv_w_up", [r_up], kv_w_up, m_kv_w_up, v_kv_w_up)
    res["b_w_dq"] = update("adamw_b_w_dq", [r_dq], b_w_dq, m_b_w_dq, v_b_w_dq)
    res["b_w_uq"] = update("adamw_b_w_uq", [r_uq], b_w_uq, m_b_w_uq, v_b_w_uq)
    res["b_w_o"] = update("adamw_b_w_o", [r_o_b], b_w_o, m_b_w_o, v_b_w_o)
    res["a_w_qkv"] = update("adamw_a_w_qkv", [r_qkv], a_w_qkv, m_a_w_qkv, v_a_w_qkv)
    res["a_w_o"] = update("adamw_a_w_o", [r_o_a], a_w_o, m_a_w_o, v_a_w_o)

    small = _pack_small(
        jnp.stack([dg_ffn1[0][0], dg_ffn1[1][0]]), jnp.stack([dg_mix[0][0], dg_mix[1][0]]), jnp.stack([dg_ffn2[0][0], dg_ffn2[1][0]]),
        dg_kv[0], dg_final[0], dg_q[0], dg_latent[0], dtable, loss_part[0],
    )
    done = [r[1] for name, r in res.items() if name != "ffn2_w_in"]
    (r_small,) = _exchange_sc("small_grads_gather", [("gather", small)], SMALL_ID, list(rs_ffn1[0][:1]) + done)
    res["ffn1_w_in"] = update_transposed("adamw_ffn1_w_in", [r_in1_0, r_in1_1], ffn1_w_in, m_ffn1_w_in, v_ffn1_w_in)
    res["ffn1_w_out"] = update("adamw_ffn1_w_out", [r_out1_0, r_out1_1], ffn1_w_out, m_ffn1_w_out, v_ffn1_w_out)
    zero_row = jnp.zeros((dn,), F32)
    packs = [
        _pack_small(f1, mx, f2, kvn, fin, qn, lat, rel, zero_row)
        for f1, mx, f2, kvn, fin, qn, lat, rel in (
            (ffn1_norm, mix_norm, ffn2_norm, kv_norm, final_norm, b_q_norm, kv_latent_norm, a_rel_bias),
            (m_ffn1_norm, m_mix_norm, m_ffn2_norm, m_kv_norm, m_final_norm, m_b_q_norm, m_kv_latent_norm, m_a_rel_bias),
            (v_ffn1_norm, v_mix_norm, v_ffn2_norm, v_kv_norm, v_final_norm, v_b_q_norm, v_kv_latent_norm, v_a_rel_bias),
        )
    ]
    small_out = [_unpack_small(o[0]) for o in _adamw("adamw_small", [r_small], *[p[None] for p in packs])]
    for name in ("ffn1_norm", "mix_norm", "ffn2_norm", "a_rel_bias", "kv_norm", "kv_latent_norm", "b_q_norm", "final_norm"):
        res[name] = [so[name] for so in small_out]
    loss = small_out[0]["last"][0]

    order = [
        "ffn1_norm", "ffn1_w_in", "ffn1_w_out", "mix_norm", "ffn2_norm", "ffn2_w_in", "ffn2_w_out", "a_w_qkv", "a_rel_bias",
        "a_w_o", "kv_norm", "kv_w_down", "kv_latent_norm", "kv_w_up", "b_w_dq", "b_q_norm", "b_w_uq", "b_w_o", "final_norm",
    ]
    return (loss, grad_x, *[res[n][0] for n in order], *[res[n][1] for n in order], *[res[n][2] for n in order], *[res[n][3] for n in order])
```

```python
import jax
import jax.numpy as jnp
import numpy as np
from jax import lax
from jax.experimental import pallas as pl
from jax.experimental.pallas import tpu as pltpu
from jax.experimental.pallas import tpu_sc as plsc

NDEV = 8
D_MODEL = 1024
D_FF = 2816
FB = 2 * D_FF // NDEV
NJ = D_FF // FB
CHUNK = 64
LEFT_CHUNKS = 8
PAD = LEFT_CHUNKS * CHUNK
BAND = PAD + CHUNK
CHUNKS_PER_STEP = 4
WINDOW = PAD + CHUNKS_PER_STEP * CHUNK
STEP_ROWS = CHUNKS_PER_STEP * 2 * CHUNK
MAX_REL = 128
NREL = 2 * MAX_REL + 1
NREL_USED = 256
HEADS_A = 16
HEADS_B = 8
NOPE = 128
ROPE = 64
QK_B = NOPE + ROPE
V_DIM = 128
Q_LORA = 768
KV_LORA = 256
ROPE_THETA = 10000.0
EPS = 1e-6
NEG_INF = -1e30
MLA_TQ = 256
MLA_TK_FWD = 256
MLA_TK_BWD = 1024
ADAM_LR = 0.001
ADAM_B1 = 0.9
ADAM_B2 = 0.999
ADAM_EPS = 1e-08
ADAM_WD = 0.01
ADAM_STEP = 10
PACK_ROWS = 16
GATHER_IDS = tuple(range(1, 9))
REDUCE_IDS = tuple(range(9, 14))
VMEM_LIMIT_BYTES = 56 * 1024 * 1024

F32 = jnp.float32
BF16 = jnp.bfloat16
SDS = jax.ShapeDtypeStruct
BS = pl.BlockSpec
MESH = pl.DeviceIdType.MESH


def _cparams(n_axes):
    return pltpu.CompilerParams(dimension_semantics=("arbitrary",) * n_axes, vmem_limit_bytes=VMEM_LIMIT_BYTES)


def _tile(t, want=512):
    return want if t % want == 0 else t


def _dot(a, b):
    return jnp.dot(a, b, preferred_element_type=F32)


def _dot_nt(a, b):
    return lax.dot_general(a, b, (((1,), (1,)), ((), ())), preferred_element_type=F32)


def _dot_tn(a, b):
    return lax.dot_general(a, b, (((0,), (0,)), ((), ())), preferred_element_type=F32)


def _split3(a):
    hi = a.astype(BF16)
    rest = a - hi.astype(F32)
    mid = rest.astype(BF16)
    return hi, mid, (rest - mid.astype(F32)).astype(BF16)


def _dot_exact(a, onehot, transposed=False):
    ob = onehot.astype(BF16)
    dot = _dot_nt if transposed else _dot
    hi, mid, lo = _split3(a)
    return dot(hi, ob) + dot(mid, ob) + dot(lo, ob)


def _rms_scale(h):
    return lax.rsqrt(jnp.mean(h * h, axis=-1, keepdims=True) + EPS)


def _acc_rows(ref, val, step, n_steps):
    part = val.reshape(val.shape[0] // 8, 8, val.shape[1]).sum(axis=0)

    @pl.when(step == 0)
    def _():
        ref[...] = part

    @pl.when(step > 0)
    def _():
        ref[...] += part

    @pl.when(step == n_steps - 1)
    def _():
        ref[...] = jnp.broadcast_to(jnp.sum(ref[...], axis=0, keepdims=True), ref.shape)


def _exchange_plan(entries):
    ins = [e[1] for e in entries]
    kinds = [e[0] for e in entries]
    lands = [SDS((NDEV,) + a.shape if k == "gather" else a.shape, a.dtype) for k, a in zip(kinds, ins)]
    return ins, lands, kinds


def _mesh_place():
    x, y, c = lax.axis_index("x"), lax.axis_index("y"), lax.axis_index("c")
    return (x, y, c), 4 * x + 2 * y + c


def _flipped(place, p):
    x, y, c = place
    px = 1 - x if p & 4 else x
    py = 1 - y if p & 2 else y
    pc = 1 - c if p & 1 else c
    return (px, py, pc), 4 * px + 2 * py + pc


def _ends(kind, src_ref, land_ref, origin, target):
    if kind == "gather":
        return src_ref, land_ref.at[origin]
    return src_ref.at[target], land_ref.at[origin]


def _remote(kind, src_ref, land_ref, send_sems, recv_sems, k, p, place, me, arriving):
    peer_pos, peer = _flipped(place, p)
    src, dst = _ends(kind, src_ref, land_ref, me, peer)
    if arriving:
        dst = _ends(kind, src_ref, land_ref, peer, me)[1]
    sem = k * (NDEV - 1) + p - 1
    return pltpu.make_async_remote_copy(
        src_ref=src, dst_ref=dst, send_sem=send_sems.at[sem], recv_sem=recv_sems.at[sem], device_id=peer_pos, device_id_type=MESH,
    )


def _exchange(name, entries, after=()):
    ins, lands, kinds = _exchange_plan(entries)
    n = len(ins)
    after = tuple(after)

    def body(*refs):
        refs = refs[:n] + refs[n + len(after) :]
        in_refs, land_refs = refs[:n], refs[n : 2 * n]
        send_sems, recv_sems, local_sems = refs[2 * n :]
        place, me = _mesh_place()
        local = []
        for k in range(n):
            src, dst = _ends(kinds[k], in_refs[k], land_refs[k], me, me)
            local.append(pltpu.make_async_copy(src, dst, local_sems.at[k]))
            local[-1].start()
        sends = []
        for p in range(1, NDEV):
            for k in range(n):
                sends.append(_remote(kinds[k], in_refs[k], land_refs[k], send_sems, recv_sems, k, p, place, me, False))
                sends[-1].start()
        for p in range(1, NDEV):
            for k in range(n):
                _remote(kinds[k], in_refs[k], land_refs[k], send_sems, recv_sems, k, p, place, me, True).wait_recv()
        for cp in sends:
            cp.wait_send()
        for cp in local:
            cp.wait()

    any_spec = BS(memory_space=pl.ANY)
    return pl.pallas_call(
        body, name=name, out_shape=lands, in_specs=[any_spec] * (n + len(after)), out_specs=[any_spec] * n,
        scratch_shapes=[
            pltpu.SemaphoreType.DMA((n * (NDEV - 1),)), pltpu.SemaphoreType.DMA((n * (NDEV - 1),)), pltpu.SemaphoreType.DMA((n,)),
        ],
    )(*ins, *after)


def _exchange_sc(name, entries, collective_id, after=()):
    ins, lands, kinds = _exchange_plan(entries)
    n = len(ins)
    after = tuple(after)

    def launch(*refs):
        refs = refs[:n] + refs[n + len(after) :]
        in_refs, land_refs = refs[:n], refs[n : 2 * n]
        send_sems, recv_sems, local_sems = refs[2 * n :]
        place, me = _mesh_place()
        barrier = pltpu.get_barrier_semaphore()
        for p in range(1, NDEV):
            pl.semaphore_signal(barrier, inc=1, device_id=_flipped(place, p)[0], device_id_type=MESH)
        pl.semaphore_wait(barrier, NDEV - 1)
        local = []
        for k in range(n):
            src, dst = _ends(kinds[k], in_refs[k], land_refs[k], me, me)
            local.append(pltpu.make_async_copy(src, dst, local_sems.at[k]))
            local[-1].start()
        sends = []
        if all(kind == "gather" for kind in kinds):
            for p in (1, 2, 4, 6):
                for k in range(n):
                    sends.append(_remote(kinds[k], in_refs[k], land_refs[k], send_sems, recv_sems, k, p, place, me, False))
                    sends[-1].start()
            sibling_pos, _ = _flipped(place, 1)
            for f in (2, 4, 6):
                _, origin = _flipped(place, f)
                for k in range(n):
                    _remote(kinds[k], in_refs[k], land_refs[k], send_sems, recv_sems, k, f, place, me, True).wait_recv()
                    sem = k * (NDEV - 1) + f
                    sends.append(
                        pltpu.make_async_remote_copy(
                            src_ref=land_refs[k].at[origin], dst_ref=land_refs[k].at[origin], send_sem=send_sems.at[sem],
                            recv_sem=recv_sems.at[sem], device_id=sibling_pos, device_id_type=MESH,
                        )
                    )
                    sends[-1].start()
            for p in (1, 3, 5, 7):
                for k in range(n):
                    _remote(kinds[k], in_refs[k], land_refs[k], send_sems, recv_sems, k, p, place, me, True).wait_recv()
        else:
            for p in range(1, NDEV):
                for k in range(n):
                    sends.append(_remote(kinds[k], in_refs[k], land_refs[k], send_sems, recv_sems, k, p, place, me, False))
                    sends[-1].start()
            for p in range(1, NDEV):
                for k in range(n):
                    _remote(kinds[k], in_refs[k], land_refs[k], send_sems, recv_sems, k, p, place, me, True).wait_recv()
        for cp in sends:
            cp.wait_send()
        for cp in local:
            cp.wait()

    return pl.kernel(
        launch, out_type=tuple(lands), mesh=plsc.ScalarSubcoreMesh(axis_name="sequencer", num_cores=1), name=name,
        scratch_types=(
            pltpu.SemaphoreType.DMA((n * (NDEV - 1),)), pltpu.SemaphoreType.DMA((n * (NDEV - 1),)), pltpu.SemaphoreType.DMA((n,)),
        ),
        compiler_params=pltpu.CompilerParams(collective_id=collective_id),
    )(*ins, *after)


def _norm_fwd(name, h, gammas):
    t, dn = h.shape
    ng = gammas.shape[0]
    tm = _tile(t)

    def body(h_ref, g_ref, *outs):
        hv = h_ref[...]
        hh = hv * _rms_scale(hv)
        for i, o_ref in enumerate(outs):
            o_ref[...] = (hh * g_ref[i : i + 1, :]).astype(BF16)

    row = BS((tm, dn), lambda i: (i, 0))
    return pl.pallas_call(
        body, name=name, grid=(t // tm,),
        in_specs=[row, BS((ng, dn), lambda i: (0, 0))],
        out_specs=[row] * ng, out_shape=[SDS((t, dn), BF16)] * ng,
        compiler_params=_cparams(1),
    )(h, gammas)


def _ffn_in(name, n, w_in, layer):
    t, dn = n.shape
    tm = _tile(t, 1024)

    def body(n_ref, wg_ref, wu_ref, gu_ref, a_ref):
        xv = n_ref[...]
        g = _dot(xv, wg_ref[...])
        u = _dot(xv, wu_ref[...])
        gu_ref[0] = g.astype(BF16)
        gu_ref[1] = u.astype(BF16)
        a_ref[...] = (g * jax.nn.sigmoid(g) * u).astype(BF16)

    return pl.pallas_call(
        body, name=name, grid=(NJ, t // tm),
        in_specs=[
            BS((tm, dn), lambda j, i: (i, 0)),
            BS((None, None, dn, FB), lambda j, i: (layer, j, 0, 0)),
            BS((None, None, dn, FB), lambda j, i: (layer, j + NJ, 0, 0)),
        ],
        out_specs=[BS((None, 2, tm, FB), lambda j, i: (j, 0, i, 0)), BS((None, tm, FB), lambda j, i: (j, i, 0))],
        out_shape=[SDS((NJ, 2, t, FB), BF16), SDS((NJ, t, FB), BF16)],
        compiler_params=_cparams(2),
    )(n, w_in, w_in)


def _mm_res_norm(name, a, w, layer, h_in, gammas, scale):
    nk, t, kb = a.shape
    dn = w.shape[-1]
    ng = 0 if gammas is None else gammas.shape[0]
    tm = _tile(t)

    def body(*refs):
        a_ref, w_ref, h_ref = refs[:3]
        g_ref = refs[3] if ng else None
        outs = refs[3 + (1 if ng else 0) :]
        acc = _dot(a_ref[0], w_ref[0])
        for k in range(1, nk):
            acc += _dot(a_ref[k], w_ref[k])
        ho = h_ref[...] + scale * acc
        outs[0][...] = ho
        if ng:
            hh = ho * _rms_scale(ho)
            for i in range(ng):
                outs[1 + i][...] = (hh * g_ref[i : i + 1, :]).astype(BF16)

    row = BS((tm, dn), lambda i: (i, 0))
    in_specs = [BS((nk, tm, kb), lambda i: (0, i, 0)), BS((None, nk, kb, dn), lambda i: (layer, 0, 0, 0)), row]
    args = [a, w, h_in]
    if ng:
        in_specs.append(BS((ng, dn), lambda i: (0, 0)))
        args.append(gammas)
    return pl.pallas_call(
        body, name=name, grid=(t // tm,),
        in_specs=in_specs,
        out_specs=[row] * (1 + ng), out_shape=[SDS((t, dn), F32)] + [SDS((t, dn), BF16)] * ng,
        compiler_params=_cparams(1),
    )(*args)


def _qkv_proj(name, hn, w_qkv):
    t, dn = hn.shape
    wb = w_qkv.shape[-1]
    per = wb // 128
    tm = _tile(t)

    def body(x_ref, w_ref, o_ref):
        xv = x_ref[...]
        for j in range(NDEV):
            yv = _dot(xv, w_ref[j]).astype(BF16)
            for i in range(per):
                n = per * j + i
                o_ref[n // 8, :, (n % 8) * 128 : (n % 8 + 1) * 128] = yv[:, i * 128 : (i + 1) * 128]

    return pl.pallas_call(
        body, name=name, grid=(t // tm,),
        in_specs=[BS((tm, dn), lambda i: (i, 0)), BS((NDEV, dn, wb), lambda i: (0, 0, 0))],
        out_specs=BS((3, tm, dn), lambda i: (0, i, 0)), out_shape=SDS((3, t, dn), BF16),
        compiler_params=_cparams(1),
    )(hn, w_qkv)


def _rel_onehot(i):
    r = lax.broadcasted_iota(jnp.int32, (NREL_USED, BAND), 0)
    j = lax.broadcasted_iota(jnp.int32, (NREL_USED, BAND), 1)
    idx = jnp.clip(PAD + i - j, -MAX_REL, MAX_REL) + MAX_REL
    return (idx - 1 == r).astype(F32)


def _rel_bias_fwd(table):
    def body(t_ref, o_ref):
        i8 = pl.program_id(0)
        for ii in range(8):
            o_ref[:, ii, :] = _dot_exact(t_ref[...], _rel_onehot(i8 * 8 + ii))

    return pl.pallas_call(
        body, name="rel_bias_fwd", grid=(CHUNK // 8,),
        in_specs=[BS((HEADS_A, NREL_USED), lambda i: (0, 0))],
        out_specs=BS((HEADS_A, 8, BAND), lambda i: (0, i, 0)), out_shape=SDS((HEADS_A, CHUNK, BAND), F32),
        compiler_params=_cparams(1),
    )(table)


def _rel_bias_bwd(dbias):
    def body(d_ref, o_ref):
        i8 = pl.program_id(0)
        acc = jnp.zeros((HEADS_A, NREL_USED), F32)
        for ii in range(8):
            acc += _dot_exact(d_ref[:, ii, :], _rel_onehot(i8 * 8 + ii), transposed=True)

        @pl.when(i8 == 0)
        def _():
            o_ref[...] = acc

        @pl.when(i8 > 0)
        def _():
            o_ref[...] += acc

    return pl.pallas_call(
        body, name="rel_bias_bwd", grid=(CHUNK // 8,),
        in_specs=[BS((HEADS_A, 8, BAND), lambda i: (0, i, 0))],
        out_specs=BS((HEADS_A, NREL_USED), lambda i: (0, 0)), out_shape=SDS((HEADS_A, NREL_USED), F32),
        compiler_params=_cparams(1),
    )(dbias)


def _window_bias(bias):
    b = bias.reshape(HEADS_A // 2, 2, CHUNK, BAND)
    per_chunk = [
        jnp.pad(b, ((0, 0), (0, 0), (0, 0), (cc * CHUNK, WINDOW - BAND - cc * CHUNK)), constant_values=NEG_INF)
        for cc in range(CHUNKS_PER_STEP)
    ]
    return jnp.stack(per_chunk, axis=1).reshape(HEADS_A // 2, STEP_ROWS, WINDOW)


def _window_bias_bwd(dwin):
    d = dwin.reshape(HEADS_A // 2, CHUNKS_PER_STEP, 2, CHUNK, WINDOW)
    return sum(d[:, cc, :, :, cc * CHUNK : cc * CHUNK + BAND] for cc in range(CHUNKS_PER_STEP)).reshape(HEADS_A, CHUNK, BAND)


def _step_rows(xs, lane):
    parts = []
    for cc in range(CHUNKS_PER_STEP):
        xc = xs[cc * CHUNK : (cc + 1) * CHUNK]
        parts.append(jnp.where(lane < 64, xc, jnp.zeros_like(xc)))
        parts.append(jnp.where(lane >= 64, xc, jnp.zeros_like(xc)))
    return jnp.concatenate(parts, axis=0)


def _pair_rows(ys, lane):
    parts = []
    for cc in range(CHUNKS_PER_STEP):
        y0 = ys[(2 * cc) * CHUNK : (2 * cc + 1) * CHUNK]
        y1 = ys[(2 * cc + 1) * CHUNK : (2 * cc + 2) * CHUNK]
        parts.append(jnp.where(lane < 64, y0, y1))
    return jnp.concatenate(parts, axis=0)


def _window_scores(q_rows, kwin, bias_win, first_key):
    s = _dot_nt(q_rows, kwin) * (CHUNK ** -0.5) + bias_win
    if first_key is None:
        return s
    col = lax.broadcasted_iota(jnp.int32, s.shape, 1)
    return jnp.where(col >= first_key, s, NEG_INF)


def _window_loop(n_passes, chunks):
    n_padded = min(PAD // (CHUNKS_PER_STEP * CHUNK), n_passes)
    lax.fori_loop(0, n_padded, lambda it, carry: chunks(it, carry, True), 0, unroll=2)
    if n_passes > n_padded:
        lax.fori_loop(n_padded, n_passes, lambda it, carry: chunks(it, carry, False), 0, unroll=2)


def _attn_a_fwd(qkv3, bias_win, bl, seq):
    t, dn = qkv3.shape[1:]
    npair = dn // 128
    step = CHUNKS_PER_STEP * CHUNK

    def body(q_ref, k_ref, v_ref, b_ref, o_ref, lse_ref, kpad, vpad):
        kpad[0:PAD, :] = jnp.zeros((PAD, 128), BF16)
        vpad[0:PAD, :] = jnp.zeros((PAD, 128), BF16)
        kpad[PAD:, :] = k_ref[...]
        vpad[PAD:, :] = v_ref[...]
        lane = lax.broadcasted_iota(jnp.int32, (CHUNK, 128), 1)

        def chunks(it, carry, padded):
            r0 = pl.multiple_of(it * step, step)
            q_rows = _step_rows(q_ref[pl.ds(r0, step), :], lane)
            s = _window_scores(q_rows, kpad[pl.ds(r0, WINDOW), :], b_ref[...], PAD - r0 if padded else None)
            m = jnp.max(s, axis=-1, keepdims=True)
            e = jnp.exp(s - m)
            total = jnp.sum(e, axis=-1, keepdims=True)
            o_rows = _dot((e * (1.0 / total)).astype(BF16), vpad[pl.ds(r0, WINDOW), :])
            o_ref[pl.ds(r0, step), :] = _pair_rows(o_rows, lane).astype(BF16)
            lse_ref[pl.ds(pl.multiple_of(it * STEP_ROWS, STEP_ROWS), STEP_ROWS), :] = m + jnp.log(total)
            return carry

        _window_loop(seq // step, chunks)

    return pl.pallas_call(
        body, name="attn_a_fwd", grid=(bl, npair),
        in_specs=[
            BS((None, seq, 128), lambda b, h: (0, b, h)),
            BS((None, seq, 128), lambda b, h: (1, b, h)),
            BS((None, seq, 128), lambda b, h: (2, b, h)),
            BS((None, STEP_ROWS, WINDOW), lambda b, h: (h, 0, 0)),
        ],
        out_specs=[BS((seq, 128), lambda b, h: (b, h)), BS((None, 2 * seq, 1), lambda b, h: (h, b, 0))],
        out_shape=[SDS((t, dn), BF16), SDS((npair, 2 * t, 1), F32)],
        scratch_shapes=[pltpu.VMEM((PAD + seq, 128), BF16), pltpu.VMEM((PAD + seq, 128), BF16)],
        compiler_params=_cparams(2),
    )(qkv3, qkv3, qkv3, bias_win)


def _attn_a_bwd(qkv3, out, lse, do, bias_win, bl, seq):
    t, dn = qkv3.shape[1:]
    npair = dn // 128
    step = CHUNKS_PER_STEP * CHUNK

    def body(q_ref, k_ref, v_ref, o_ref, lse_ref, do_ref, b_ref, dqkv_ref, db_ref, kpad, vpad, dkacc, dvacc):
        b = pl.program_id(1)
        kpad[0:PAD, :] = jnp.zeros((PAD, 128), BF16)
        vpad[0:PAD, :] = jnp.zeros((PAD, 128), BF16)
        kpad[PAD:, :] = k_ref[...]
        vpad[PAD:, :] = v_ref[...]
        dkacc[...] = jnp.zeros_like(dkacc)
        dvacc[...] = jnp.zeros_like(dvacc)

        @pl.when(b == 0)
        def _():
            db_ref[...] = jnp.zeros_like(db_ref)

        lane = lax.broadcasted_iota(jnp.int32, (CHUNK, 128), 1)

        def chunks(it, carry, padded):
            r0 = pl.multiple_of(it * step, step)
            q_rows = _step_rows(q_ref[pl.ds(r0, step), :], lane)
            do_rows = _step_rows(do_ref[pl.ds(r0, step), :], lane)
            kwin = kpad[pl.ds(r0, WINDOW), :]
            vwin = vpad[pl.ds(r0, WINDOW), :]
            o_rows = _step_rows(o_ref[pl.ds(r0, step), :], lane)
            delta = jnp.sum(do_rows.astype(F32) * o_rows.astype(F32), axis=-1, keepdims=True)
            lse_rows = lse_ref[pl.ds(pl.multiple_of(it * STEP_ROWS, STEP_ROWS), STEP_ROWS), :]
            p = jnp.exp(_window_scores(q_rows, kwin, b_ref[...], PAD - r0 if padded else None) - lse_rows)
            ds = p * (_dot_nt(do_rows, vwin) - delta)
            db_ref[...] += ds
            dsb = (ds * (CHUNK ** -0.5)).astype(BF16)
            dqkv_ref[0, pl.ds(r0, step), :] = _pair_rows(_dot(dsb, kwin), lane).astype(BF16)
            dkacc[pl.ds(r0, WINDOW), :] += _dot_tn(dsb, q_rows)
            dvacc[pl.ds(r0, WINDOW), :] += _dot_tn(p.astype(BF16), do_rows)
            return carry

        _window_loop(seq // step, chunks)
        dqkv_ref[1] = dkacc[PAD:, :].astype(BF16)
        dqkv_ref[2] = dvacc[PAD:, :].astype(BF16)

    return pl.pallas_call(
        body, name="attn_a_bwd", grid=(npair, bl),
        in_specs=[
            BS((None, seq, 128), lambda h, b: (0, b, h)),
            BS((None, seq, 128), lambda h, b: (1, b, h)),
            BS((None, seq, 128), lambda h, b: (2, b, h)),
            BS((seq, 128), lambda h, b: (b, h)),
            BS((None, 2 * seq, 1), lambda h, b: (h, b, 0)),
            BS((seq, 128), lambda h, b: (b, h)),
            BS((None, STEP_ROWS, WINDOW), lambda h, b: (h, 0, 0)),
        ],
        out_specs=[BS((3, seq, 128), lambda h, b: (0, b, h)), BS((None, STEP_ROWS, WINDOW), lambda h, b: (h, 0, 0))],
        out_shape=[SDS((3, t, dn), BF16), SDS((HEADS_A // 2, STEP_ROWS, WINDOW), F32)],
        scratch_shapes=[
            pltpu.VMEM((PAD + seq, 128), BF16), pltpu.VMEM((PAD + seq, 128), BF16),
            pltpu.VMEM((PAD + seq, 128), F32), pltpu.VMEM((PAD + seq, 128), F32),
        ],
        compiler_params=_cparams(2),
    )(qkv3, qkv3, qkv3, out, lse, do, bias_win)


def _rope_tables(seq):
    half = ROPE // 2
    freqs = ROPE_THETA ** (-jnp.arange(half, dtype=F32) / half)
    ang = jnp.arange(seq, dtype=F32)[:, None] * freqs[None, :]
    cos, sin = jnp.cos(ang), jnp.sin(ang)
    c64 = jnp.concatenate([cos, cos], axis=1)
    s64 = jnp.concatenate([-sin, sin], axis=1)
    c192 = jnp.concatenate([jnp.ones((seq, NOPE), F32), c64], axis=1)
    s192 = jnp.concatenate([jnp.zeros((seq, NOPE), F32), s64], axis=1)
    p64 = np.zeros((ROPE, ROPE), np.float32)
    for col in range(ROPE):
        p64[(col + half) % ROPE, col] = 1.0
    p192 = np.zeros((QK_B, QK_B), np.float32)
    p192[NOPE:, NOPE:] = p64
    return c64, s64, jnp.asarray(p64), c192, s192, jnp.asarray(p192)


def _rope(xv, cos, sin_signed, swap):
    return xv * cos + _dot_exact(xv, swap) * sin_signed


def _rope_bwd(dy, cos, sin_signed, swap):
    return dy * cos + _dot_exact(dy * sin_signed, swap)


def _q_down(hn, w_dq, q_norm):
    t, dn = hn.shape
    ql = w_dq.shape[1]
    tm = _tile(t)

    def body(x_ref, w_ref, g_ref, pre_ref, cq_ref):
        pre = _dot(x_ref[...], w_ref[...])
        pre_ref[...] = pre
        cq_ref[...] = (pre * _rms_scale(pre) * g_ref[...]).astype(BF16)

    return pl.pallas_call(
        body, name="q_down", grid=(t // tm,),
        in_specs=[BS((tm, dn), lambda i: (i, 0)), BS((dn, ql), lambda i: (0, 0)), BS((1, ql), lambda i: (0, 0))],
        out_specs=[BS((tm, ql), lambda i: (i, 0))] * 2, out_shape=[SDS((t, ql), F32), SDS((t, ql), BF16)],
        compiler_params=_cparams(1),
    )(hn, w_dq, q_norm)


def _q_up(cq, w_uq, c192, s192, p192, seq):
    t, ql = cq.shape
    tm = _tile(min(seq, 512), min(seq, 512))
    nseq = seq // tm

    def body(x_ref, w_ref, c_ref, s_ref, p_ref, o_ref):
        xv = x_ref[...]
        for h in range(HEADS_B):
            o_ref[h] = _rope(_dot(xv, w_ref[h]), c_ref[...], s_ref[...], p_ref[...]).astype(BF16)

    pos = BS((tm, QK_B), lambda i: (i % nseq, 0))
    return pl.pallas_call(
        body, name="q_up", grid=(t // tm,),
        in_specs=[
            BS((tm, ql), lambda i: (i, 0)), BS((HEADS_B, ql, QK_B), lambda i: (0, 0, 0)), pos, pos,
            BS((QK_B, QK_B), lambda i: (0, 0)),
        ],
        out_specs=BS((HEADS_B, tm, QK_B), lambda i: (0, i, 0)), out_shape=SDS((HEADS_B, t, QK_B), BF16),
        compiler_params=_cparams(1),
    )(cq, w_uq, c192, s192, p192)


def _kv_down(hk, w_down, latent_norm, c64, s64, p64, seq):
    t, dn = hk.shape
    wd = w_down.shape[1]
    tm = _tile(min(seq, 512), min(seq, 512))
    nseq = seq // tm

    def body(x_ref, w_ref, g_ref, c_ref, s_ref, p_ref, ckr_ref, ckv_ref, kr_ref):
        ckr = _dot(x_ref[...], w_ref[...])
        ckr_ref[...] = ckr
        lat = ckr[:, :KV_LORA]
        ckv_ref[...] = (lat * _rms_scale(lat) * g_ref[...]).astype(BF16)
        kr_ref[...] = _rope(ckr[:, KV_LORA:], c_ref[...], s_ref[...], p_ref[...]).astype(BF16)

    pos = BS((tm, ROPE), lambda i: (i % nseq, 0))
    return pl.pallas_call(
        body, name="kv_down", grid=(t // tm,),
        in_specs=[
            BS((tm, dn), lambda i: (i, 0)), BS((dn, wd), lambda i: (0, 0)), BS((1, KV_LORA), lambda i: (0, 0)), pos, pos,
            BS((ROPE, ROPE), lambda i: (0, 0)),
        ],
        out_specs=[BS((tm, wd), lambda i: (i, 0)), BS((tm, KV_LORA), lambda i: (i, 0)), BS((tm, ROPE), lambda i: (i, 0))],
        out_shape=[SDS((t, wd), F32), SDS((t, KV_LORA), BF16), SDS((t, ROPE), BF16)],
        compiler_params=_cparams(1),
    )(hk, w_down, latent_norm, c64, s64, p64)


def _kv_up(ckv, w_up):
    t, kl = ckv.shape
    hb = w_up.shape[-1]
    tm = _tile(t)

    def body(x_ref, w_ref, o_ref):
        xv = x_ref[...]
        for h in range(HEADS_B):
            o_ref[:, h * hb : (h + 1) * hb] = _dot(xv, w_ref[h]).astype(BF16)

    return pl.pallas_call(
        body, name="kv_up", grid=(t // tm,),
        in_specs=[BS((tm, kl), lambda i: (i, 0)), BS((HEADS_B, kl, hb), lambda i: (0, 0, 0))],
        out_specs=BS((tm, HEADS_B * hb), lambda i: (i, 0)), out_shape=SDS((t, HEADS_B * hb), BF16),
        compiler_params=_cparams(1),
    )(ckv, w_up)


def _mla_diagonal_mask(tq):
    rows = lax.broadcasted_iota(jnp.int32, (tq, tq), 0)
    cols = lax.broadcasted_iota(jnp.int32, (tq, tq), 1)
    return jnp.where(jnp.right_shift(cols, 6) <= jnp.right_shift(rows, 6), 0.0, NEG_INF)


def _mla_key_tiles(n_keys, tk):
    return [(slice(k0, min(k0 + tk, n_keys)), min(k0 + tk, n_keys) == n_keys) for k0 in range(0, n_keys, tk)]


def _mla_scores(qi, kt, diagonal):
    s = _dot_nt(qi, kt) * (QK_B ** -0.5)
    if diagonal is None:
        return s
    tq, width = s.shape
    own = s[:, width - tq :] + diagonal
    return own if width == tq else jnp.concatenate([s[:, : width - tq], own], axis=1)


def _mla_fwd(q, kv, kr, bl, seq):
    t = kv.shape[0]
    tq = min(MLA_TQ, seq)

    def body(q_ref, kn_ref, v_ref, kr_ref, o_ref, lse_ref):
        kcat = jnp.concatenate([kn_ref[...], kr_ref[...]], axis=1)
        vv = v_ref[...]
        diagonal = _mla_diagonal_mask(tq)
        for i in range(seq // tq):
            rows = slice(i * tq, (i + 1) * tq)
            qi = q_ref[rows, :]
            m = total = acc = None
            for keys, own in _mla_key_tiles((i + 1) * tq, MLA_TK_FWD):
                s = _mla_scores(qi, kcat[keys], diagonal if own else None)
                m_blk = jnp.max(s, axis=-1, keepdims=True)
                if m is None:
                    m_new = m_blk
                    e = jnp.exp(s - m_new)
                    total = jnp.sum(e, axis=-1, keepdims=True)
                    acc = _dot(e.astype(BF16), vv[keys])
                else:
                    m_new = jnp.maximum(m, m_blk)
                    keep = jnp.exp(m - m_new)
                    e = jnp.exp(s - m_new)
                    total = keep * total + jnp.sum(e, axis=-1, keepdims=True)
                    acc = keep * acc + _dot(e.astype(BF16), vv[keys])
                m = m_new
            o_ref[rows, :] = (acc / total).astype(BF16)
            lse_ref[rows, :] = m + jnp.log(total)

    return pl.pallas_call(
        body, name="mla_fwd", grid=(bl, HEADS_B),
        in_specs=[
            BS((None, seq, QK_B), lambda b, h: (h, b, 0)),
            BS((seq, NOPE), lambda b, h: (b, 2 * h)),
            BS((seq, V_DIM), lambda b, h: (b, 2 * h + 1)),
            BS((seq, ROPE), lambda b, h: (b, 0)),
        ],
        out_specs=[BS((seq, V_DIM), lambda b, h: (b, h)), BS((None, seq, 1), lambda b, h: (h, b, 0))],
        out_shape=[SDS((t, HEADS_B * V_DIM), BF16), SDS((HEADS_B, t, 1), F32)],
        compiler_params=_cparams(2),
    )(q, kv, kv, kr)


def _mla_bwd(q, kv, kr, o, lse, do, c192, s192, p192, bl, seq):
    t = kv.shape[0]
    tq = min(MLA_TQ, seq)

    def body(q_ref, kn_ref, v_ref, kr_ref, o_ref, lse_ref, do_ref, c_ref, s_ref, p_ref, dq_ref, dkv_ref, dkr_ref, dkacc, dvacc):
        h = pl.program_id(1)
        kcat = jnp.concatenate([kn_ref[...], kr_ref[...]], axis=1)
        vv = v_ref[...]
        dkacc[...] = jnp.zeros_like(dkacc)
        dvacc[...] = jnp.zeros_like(dvacc)
        diagonal = _mla_diagonal_mask(tq)
        for i in range(seq // tq):
            rows = slice(i * tq, (i + 1) * tq)
            qi = q_ref[rows, :]
            doi = do_ref[rows, :]
            lse_i = lse_ref[rows, :]
            delta = jnp.sum(doi.astype(F32) * o_ref[rows, :].astype(F32), axis=-1, keepdims=True)
            dq = None
            for keys, own in _mla_key_tiles((i + 1) * tq, MLA_TK_BWD):
                p = jnp.exp(_mla_scores(qi, kcat[keys], diagonal if own else None) - lse_i)
                ds = p * (_dot_nt(doi, vv[keys]) - delta)
                dsb = (ds * (QK_B ** -0.5)).astype(BF16)
                dq_blk = _dot(dsb, kcat[keys])
                dq = dq_blk if dq is None else dq + dq_blk
                dkacc[keys, :] += _dot_tn(dsb, qi)
                dvacc[keys, :] += _dot_tn(p.astype(BF16), doi)
            dq_ref[rows, :] = _rope_bwd(dq, c_ref[rows, :], s_ref[rows, :], p_ref[...]).astype(BF16)
        dk = dkacc[...]
        dkv_ref[:, :NOPE] = dk[:, :NOPE].astype(BF16)
        dkv_ref[:, NOPE:] = dvacc[...].astype(BF16)

        @pl.when(h == 0)
        def _():
            dkr_ref[...] = dk[:, NOPE:]

        @pl.when(h > 0)
        def _():
            dkr_ref[...] += dk[:, NOPE:]

    return pl.pallas_call(
        body, name="mla_bwd", grid=(bl, HEADS_B),
        in_specs=[
            BS((None, seq, QK_B), lambda b, h: (h, b, 0)),
            BS((seq, NOPE), lambda b, h: (b, 2 * h)),
            BS((seq, V_DIM), lambda b, h: (b, 2 * h + 1)),
            BS((seq, ROPE), lambda b, h: (b, 0)),
            BS((seq, V_DIM), lambda b, h: (b, h)),
            BS((None, seq, 1), lambda b, h: (h, b, 0)),
            BS((seq, V_DIM), lambda b, h: (b, h)),
            BS((seq, QK_B), lambda b, h: (0, 0)),
            BS((seq, QK_B), lambda b, h: (0, 0)),
            BS((QK_B, QK_B), lambda b, h: (0, 0)),
        ],
        out_specs=[
            BS((None, seq, QK_B), lambda b, h: (h, b, 0)),
            BS((seq, NOPE + V_DIM), lambda b, h: (b, h)),
            BS((seq, ROPE), lambda b, h: (b, 0)),
        ],
        out_shape=[SDS((HEADS_B, t, QK_B), BF16), SDS((t, HEADS_B * (NOPE + V_DIM)), BF16), SDS((t, ROPE), F32)],
        scratch_shapes=[pltpu.VMEM((seq, QK_B), F32), pltpu.VMEM((seq, V_DIM), F32)],
        compiler_params=_cparams(2),
    )(q, kv, kv, kr, o, lse, do, c192, s192, p192)


def _loss_final(h, target, gamma):
    t, dn = h.shape
    tm = _tile(t)
    nt = t // tm

    def body(h_ref, t_ref, g_ref, dh_ref, dhb_ref, dg_ref, loss_ref):
        i = pl.program_id(0)
        hv = h_ref[...]
        r = _rms_scale(hv)
        hh = hv * r
        gam = g_ref[...]
        err = hh * gam - t_ref[...]
        part = 0.5 * jnp.sum(jnp.mean(err * err, axis=-1, keepdims=True))

        @pl.when(i == 0)
        def _():
            loss_ref[...] = jnp.zeros_like(loss_ref)

        loss_ref[...] += part
        dy = err * (1.0 / dn)
        _acc_rows(dg_ref, dy * hh, i, nt)
        t1 = dy * gam
        dh = r * (t1 - hh * jnp.mean(t1 * hh, axis=-1, keepdims=True))
        dh_ref[...] = dh
        dhb_ref[...] = dh.astype(BF16)

    row = BS((tm, dn), lambda i: (i, 0))
    return pl.pallas_call(
        body, name="loss_final", grid=(nt,),
        in_specs=[row, row, BS((1, dn), lambda i: (0, 0))],
        out_specs=[row, row, BS((8, dn), lambda i: (0, 0)), BS((8, 128), lambda i: (0, 0))],
        out_shape=[SDS((t, dn), F32), SDS((t, dn), BF16), SDS((8, dn), F32), SDS((8, 128), F32)],
        compiler_params=_cparams(1),
    )(h, target, gamma)


def _ffn_bwd_in(name, dh, w_out, layer, gu):
    t, dn = dh.shape
    tm = _tile(t, 1024)

    def body(dh_ref, w_ref, gu_ref, o_ref):
        da = 0.5 * _dot_nt(dh_ref[...], w_ref[...])
        g = gu_ref[0].astype(F32)
        u = gu_ref[1].astype(F32)
        sg = jax.nn.sigmoid(g)
        o_ref[0] = (da * u * (sg * (1.0 + g * (1.0 - sg)))).astype(BF16)
        o_ref[1] = (da * (g * sg)).astype(BF16)

    blk = BS((None, 2, tm, FB), lambda j, i: (j, 0, i, 0))
    return pl.pallas_call(
        body, name=name, grid=(NJ, t // tm),
        in_specs=[BS((tm, dn), lambda j, i: (i, 0)), BS((None, None, FB, dn), lambda j, i: (layer, j, 0, 0)), blk],
        out_specs=blk, out_shape=SDS((NJ, 2, t, FB), BF16),
        compiler_params=_cparams(2),
    )(dh, w_out, gu)


def _mm_nt_plain(name, xf, w):
    t, dn = xf.shape
    n = w.shape[0]
    tm = _tile(t)

    def body(x_ref, w_ref, o_ref):
        o_ref[...] = _dot_nt(x_ref[...], w_ref[...]).astype(BF16)

    return pl.pallas_call(
        body, name=name, grid=(t // tm,),
        in_specs=[BS((tm, dn), lambda i: (i, 0)), BS((n, dn), lambda i: (0, 0))],
        out_specs=BS((tm, n), lambda i: (i, 0)), out_shape=SDS((t, n), BF16),
        compiler_params=_cparams(1),
    )(xf, w)


def _mm_tn(name, xa, x_spec, ya, y_spec, out_shape, out_spec, nj, scale=None):
    def body(x_ref, y_ref, o_ref):
        acc = _dot_tn(x_ref[...], y_ref[...])
        o_ref[...] = (acc if scale is None else scale * acc).astype(BF16)

    return pl.pallas_call(
        body, name=name, grid=(nj,),
        in_specs=[x_spec, y_spec], out_specs=out_spec, out_shape=SDS(out_shape, BF16),
        compiler_params=_cparams(1),
    )(xa, ya)


def _dw_qkv(hn, dqkv3, wb):
    t, dn = hn.shape
    per = wb // 128

    def body(x_ref, *refs):
        cols = [y_ref[...] for y_ref in refs[:per]]
        refs[per][...] = _dot_tn(x_ref[...], jnp.concatenate(cols, axis=1)).astype(BF16)

    def piece(k):
        return BS((None, t, 128), lambda j: ((per * j + k) // 8, 0, (per * j + k) % 8))

    return pl.pallas_call(
        body, name="dw_qkv", grid=(NDEV,),
        in_specs=[BS((t, dn), lambda j: (0, 0))] + [piece(k) for k in range(per)],
        out_specs=BS((None, dn, wb), lambda j: (j, 0, 0)), out_shape=SDS((NDEV, dn, wb), BF16),
        compiler_params=_cparams(1),
    )(hn, *([dqkv3] * per))


def _mm_nt_epi(name, ya, y_spec, wa, w_spec, nj, n_out, extra, out_shapes, out_specs, epilogue, tm, nt, mm_fn=None):
    n_extra = len(extra)
    n_outs = len(out_shapes)

    def body(*refs):
        y_ref, w_ref = refs[:2]
        ex = refs[2 : 2 + n_extra]
        outs = refs[2 + n_extra : 2 + n_extra + n_outs]
        i = pl.program_id(0)
        j = pl.program_id(1)
        part = _dot_nt(y_ref[...], w_ref[...]) if mm_fn is None else mm_fn(y_ref, w_ref)
        if nj == 1:
            epilogue(part, ex, outs, i, nt)
            return
        acc = refs[-1]

        @pl.when(j == 0)
        def _():
            acc[...] = part

        @pl.when(j > 0)
        def _():
            acc[...] += part

        @pl.when(j == nj - 1)
        def _():
            epilogue(acc[...], ex, outs, i, nt)

    return pl.pallas_call(
        body, name=name, grid=(nt, nj),
        in_specs=[y_spec, w_spec] + [spec for _, spec in extra],
        out_specs=out_specs, out_shape=out_shapes,
        scratch_shapes=[] if nj == 1 else [pltpu.VMEM((tm, n_out), F32)],
        compiler_params=_cparams(2),
    )(ya, wa, *[arr for arr, _ in extra])


def _norm_bwd(dn, hv, gam):
    r = _rms_scale(hv)
    hh = hv * r
    t1 = dn * gam
    return r * (t1 - hh * jnp.mean(t1 * hh, axis=-1, keepdims=True)), dn * hh


def _norm_bwd_epilogue(has_res, out_dtype):
    def epilogue(dn, ex, outs, i, nt):
        dh, dg_rows = _norm_bwd(dn, ex[0][...], ex[1][...])
        _acc_rows(outs[1], dg_rows, i, nt)
        if has_res:
            dh = dh + ex[2][...]
        outs[0][...] = dh.astype(out_dtype)
        if has_res:
            outs[2][...] = dh.astype(BF16)

    return epilogue


def _mm_nt_norm_bwd(name, ya, y_spec, wa, w_spec, nj, h, gamma, res, out_dtype, mm_fn=None, want_tm=512, after=None):
    t, n = h.shape
    tm = _tile(t, want_tm)
    nt = t // tm
    row = BS((tm, n), lambda i, j: (i, 0))
    extra = [(h, row), (gamma, BS((1, n), lambda i, j: (0, 0)))]
    out_shapes = [SDS((t, n), out_dtype), SDS((8, n), F32)]
    out_specs = [row, BS((8, n), lambda i, j: (0, 0))]
    if res is not None:
        extra.append((res, row))
        out_shapes.append(SDS((t, n), BF16))
        out_specs.append(row)
    extra.extend((a, BS(memory_space=pl.ANY)) for a in after or ())
    return _mm_nt_epi(
        name, ya, y_spec, wa, w_spec, nj, n, extra, out_shapes, out_specs, _norm_bwd_epilogue(res is not None, out_dtype), tm, nt, mm_fn,
    )


def _dev_block(jj):
    return jj // 2 + NJ * (jj % 2)


def _ffn_dn_mm(y_ref, w_ref):
    acc = None
    for jj in range(2 * NJ):
        part = _dot_nt(y_ref[jj], w_ref[_dev_block(jj)])
        acc = part if acc is None else acc + part
    return acc


def _ffn_bwd(tag, dh, dhb, n_in, h_in, gamma, gu, a, w_in, w_out, collective_id, after):
    t, dn = dh.shape
    dgu = _ffn_bwd_in(f"{tag}_bwd_in", dhb, w_out, 0, gu).reshape(2 * NJ, t, FB)
    dw_out = _mm_tn(
        f"{tag}_dw_out", a, BS((None, t, FB), lambda j: (j, 0, 0)), dhb, BS((t, dn), lambda j: (0, 0)),
        (NJ, FB, dn), BS((None, FB, dn), lambda j: (j, 0, 0)), NJ, scale=0.5,
    )
    dw_in = _mm_tn(
        f"{tag}_dw_in", dgu, BS((None, t, FB), lambda j: (j, 0, 0)), n_in, BS((t, dn), lambda j: (0, 0)),
        (NDEV, FB, dn), BS((None, FB, dn), lambda j: (_dev_block(j), 0, 0)), NDEV,
    )
    entries = [("scatter", dw_in), ("scatter", dw_out.reshape(NDEV, NJ * FB // NDEV, dn))]
    landed = _exchange_sc(f"{tag}_reduce", entries, collective_id, after)
    tm = _tile(t)
    resident = BS((None, NDEV, dn, FB), lambda i, j: (0, 0, 0, 0), pipeline_mode=pl.Buffered(1))
    dh_in, dgam, dhb_in = _mm_nt_norm_bwd(
        f"{tag}_dn", dgu, BS((2 * NJ, tm, FB), lambda i, j: (0, i, 0)), w_in, resident, 1, h_in, gamma, dh, F32, mm_fn=_ffn_dn_mm,
        after=[e[1] for e in entries],
    )
    return dh_in, dhb_in, dgam, landed


def _heads_mm(y_ref, w_ref):
    acc = None
    for h in range(HEADS_B):
        part = _dot_nt(y_ref[h], w_ref[h])
        acc = part if acc is None else acc + part
    return acc


def _dqkv_mm(per):
    def mm(y_ref, w_ref):
        acc = None
        for j in range(NDEV):
            cols = [y_ref[(per * j + k) // 8, :, ((per * j + k) % 8) * 128 : ((per * j + k) % 8 + 1) * 128] for k in range(per)]
            part = _dot_nt(jnp.concatenate(cols, axis=1), w_ref[j])
            acc = part if acc is None else acc + part
        return acc

    return mm


def _kv_latent_bwd(dkv, w_up, ckr, latent_norm, dkr, c64, s64, p64, seq):
    t, wd = ckr.shape
    hb = w_up.shape[-1]
    tm = _tile(min(seq, 512), min(seq, 512))
    nt = t // tm
    nseq = seq // tm

    def epilogue(dn, ex, outs, i, nt_):
        dlat, dg_rows = _norm_bwd(dn, ex[0][...], ex[1][...])
        _acc_rows(outs[1], dg_rows, i, nt_)
        outs[0][:, :KV_LORA] = dlat.astype(BF16)
        outs[0][:, KV_LORA:] = _rope_bwd(ex[2][...], ex[3][...], ex[4][...], ex[5][...]).astype(BF16)

    pos = BS((tm, ROPE), lambda i, j: (i % nseq, 0))
    extra = [
        (ckr, BS((tm, KV_LORA), lambda i, j: (i, 0))), (latent_norm, BS((1, KV_LORA), lambda i, j: (0, 0))),
        (dkr, BS((tm, ROPE), lambda i, j: (i, 0))), (c64, pos), (s64, pos), (p64, BS((ROPE, ROPE), lambda i, j: (0, 0))),
    ]
    def heads_mm(y_ref, w_ref):
        acc = None
        for h in range(HEADS_B):
            part = _dot_nt(y_ref[:, h * hb : (h + 1) * hb], w_ref[h])
            acc = part if acc is None else acc + part
        return acc

    return _mm_nt_epi(
        "kv_latent_bwd", dkv, BS((tm, HEADS_B * hb), lambda i, j: (i, 0)), w_up, BS((HEADS_B, KV_LORA, hb), lambda i, j: (0, 0, 0)),
        1, KV_LORA, extra, [SDS((t, wd), BF16), SDS((8, KV_LORA), F32)],
        [BS((tm, wd), lambda i, j: (i, 0)), BS((8, KV_LORA), lambda i, j: (0, 0))], epilogue, tm, nt, heads_mm,
    )


def _adamw(name, parts, w, m, v):
    n_layers, rows, cols = w.shape
    tr = max(d for d in range(8, min(rows, 256) + 1, 8) if rows % d == 0)
    nb = rows // tr

    def body(*refs):
        p_refs = refs[:n_layers]
        w_ref, m_ref, v_ref, g_ref, d_ref, nm_ref, nv_ref = refs[n_layers : n_layers + 7]
        layer = pl.program_id(0)
        for lp in range(n_layers):

            @pl.when(layer == lp)
            def _():
                g = p_refs[lp][0].astype(F32)
                for k in range(1, NDEV):
                    g = g + p_refs[lp][k].astype(F32)
                g_ref[...] = g

        g = g_ref[...]
        nm = ADAM_B1 * m_ref[...] + (1.0 - ADAM_B1) * g
        nv = ADAM_B2 * v_ref[...] + (1.0 - ADAM_B2) * (g * g)
        nm_ref[...] = nm
        nv_ref[...] = nv
        m_hat = nm / (1.0 - ADAM_B1 ** ADAM_STEP)
        v_hat = nv / (1.0 - ADAM_B2 ** ADAM_STEP)
        d_ref[...] = -ADAM_LR * (m_hat / (jnp.sqrt(v_hat) + ADAM_EPS) + ADAM_WD * w_ref[...])

    def part_spec(lp):
        return BS((NDEV, tr, cols), lambda l, i: (0, jnp.where(l == lp, i, jnp.where(l < lp, 0, nb - 1)), 0))

    row = BS((None, tr, cols), lambda l, i: (l, i, 0))
    return pl.pallas_call(
        body, name=name, grid=(n_layers, nb),
        in_specs=[part_spec(lp) for lp in range(n_layers)] + [row, row, row],
        out_specs=[row] * 4, out_shape=[SDS(w.shape, F32)] * 4,
        compiler_params=_cparams(2),
    )(*parts, w, m, v)


def _pack_small(ffn1_norm, mix_norm, ffn2_norm, kv_norm, final_norm, q_norm, latent_norm, rel_bias, last_row):
    dn = ffn1_norm.shape[-1]

    def rows_of(a, n_rows):
        flat = a.reshape(-1)
        return jnp.pad(flat, (0, n_rows * dn - flat.shape[0])).reshape(n_rows, dn)

    return jnp.concatenate(
        [
            ffn1_norm.reshape(2, dn), mix_norm.reshape(2, dn), ffn2_norm.reshape(2, dn), kv_norm.reshape(1, dn),
            final_norm.reshape(1, dn), rows_of(q_norm, 1), rows_of(latent_norm, 1), rows_of(rel_bias, 5), rows_of(last_row, 1),
        ],
        axis=0,
    )


def _unpack_small(pack):
    dn = pack.shape[-1]
    return dict(
        ffn1_norm=pack[0:2], mix_norm=pack[2:4], ffn2_norm=pack[4:6], kv_norm=pack[6], final_norm=pack[7],
        b_q_norm=pack[8, :Q_LORA].reshape(1, Q_LORA), kv_latent_norm=pack[9, :KV_LORA],
        a_rel_bias=pack[10:15].reshape(-1)[: HEADS_A * NREL].reshape(1, HEADS_A, NREL), last=pack[15],
    )


def kernel(x, ffn1_norm, ffn1_w_in, ffn1_w_out, mix_norm, ffn2_norm, ffn2_w_in, ffn2_w_out, a_w_qkv, a_rel_bias, a_w_o, kv_norm, kv_w_down, kv_latent_norm, kv_w_up, b_w_dq, b_q_norm, b_w_uq, b_w_o, final_norm, loss_target, m_ffn1_norm, m_ffn1_w_in, m_ffn1_w_out, m_mix_norm, m_ffn2_norm, m_ffn2_w_in, m_ffn2_w_out, m_a_w_qkv, m_a_rel_bias, m_a_w_o, m_kv_norm, m_kv_w_down, m_kv_latent_norm, m_kv_w_up, m_b_w_dq, m_b_q_norm, m_b_w_uq, m_b_w_o, m_final_norm, v_ffn1_norm, v_ffn1_w_in, v_ffn1_w_out, v_mix_norm, v_ffn2_norm, v_ffn2_w_in, v_ffn2_w_out, v_a_w_qkv, v_a_rel_bias, v_a_w_o, v_kv_norm, v_kv_w_down, v_kv_latent_norm, v_kv_w_up, v_b_w_dq, v_b_q_norm, v_b_w_uq, v_b_w_o, v_final_norm):
    bl, seq, dn = x.shape
    t = bl * seq
    tm = _tile(t)
    nt = t // tm
    x2 = x.reshape(t, dn)
    target2 = loss_target.reshape(t, dn)

    def gathered(*ws):
        return [("gather", w.astype(BF16)) for w in ws]

    groups = [
        gathered(ffn1_w_in[0]), gathered(ffn1_w_out[0]), gathered(a_w_qkv[0], a_w_o[0]), gathered(ffn2_w_in[0], ffn2_w_out[0]),
        gathered(kv_w_down, kv_w_up), gathered(ffn1_w_in[1], ffn1_w_out[1]), gathered(b_w_dq[0], b_w_uq[0], b_w_o[0]),
        gathered(ffn2_w_in[1], ffn2_w_out[1]),
    ]
    ag = [_exchange_sc(f"gather_{k}", group, GATHER_IDS[k]) for k, group in enumerate(groups)]

    def as_w_in(w):
        return w.reshape(1, NDEV, dn, FB)

    def as_w_out(w):
        return w.reshape(1, NJ, FB, dn)

    c64, s64, p64, c192, s192, p192 = _rope_tables(seq)
    q_norm = b_q_norm.reshape(1, Q_LORA)
    latent_norm = kv_latent_norm.reshape(1, KV_LORA)
    bias = _window_bias(_rel_bias_fwd(a_rel_bias[0][:, 1:]))

    h0, h1, h2, n1, hn, n2, gu1, gu2, a1, a2, w_in1, w_in2, w_out1, w_out2 = ([None, None] for _ in range(14))
    h0[0] = x2
    (n1[0],) = _norm_fwd("norm_x", x2, ffn1_norm[0:1])
    w_in1[0] = as_w_in(ag[0][0])
    gu1[0], a1[0] = _ffn_in("ffn1_in_0", n1[0], w_in1[0], 0)
    w_out1[0] = as_w_out(ag[1][0])
    h1[0], hn[0] = _mm_res_norm("ffn1_out_0", a1[0], w_out1[0], 0, h0[0], mix_norm[0:1], 0.5)
    w_qkv, w_o_a = ag[2]
    qkv_wb = w_qkv.shape[-1]
    w_o_a = w_o_a.reshape(1, 1, dn, dn)
    qkv3 = _qkv_proj("qkv_proj", hn[0], w_qkv)
    o_a, lse_a = _attn_a_fwd(qkv3, bias, bl, seq)
    h2[0], n2[0] = _mm_res_norm("attn_a_out", o_a.reshape(1, t, dn), w_o_a, 0, h1[0], ffn2_norm[0:1], 1.0)
    w_in2[0], w_out2[0] = as_w_in(ag[3][0]), as_w_out(ag[3][1])
    gu2[0], a2[0] = _ffn_in("ffn2_in_0", n2[0], w_in2[0], 0)
    h0[1], hk, n1[1] = _mm_res_norm(
        "ffn2_out_0", a2[0], w_out2[0], 0, h2[0], jnp.concatenate([kv_norm.reshape(1, dn), ffn1_norm[1:2]], axis=0), 0.5
    )
    w_down, w_up = ag[4]
    w_down = w_down.reshape(dn, KV_LORA + ROPE)
    ckr, ckv, kr = _kv_down(hk, w_down, latent_norm, c64, s64, p64, seq)
    kv = _kv_up(ckv, w_up)
    w_in1[1], w_out1[1] = as_w_in(ag[5][0]), as_w_out(ag[5][1])
    gu1[1], a1[1] = _ffn_in("ffn1_in_1", n1[1], w_in1[1], 0)
    h1[1], hn[1] = _mm_res_norm("ffn1_out_1", a1[1], w_out1[1], 0, h0[1], mix_norm[1:2], 0.5)
    w_dq, w_uq, w_o_b = ag[6]
    w_dq = w_dq.reshape(dn, Q_LORA)
    w_o_b = w_o_b.reshape(1, 1, dn, dn)
    cq_pre, cq = _q_down(hn[1], w_dq, q_norm)
    q = _q_up(cq, w_uq, c192, s192, p192, seq)
    o_b, lse_b = _mla_fwd(q, kv, kr, bl, seq)
    h2[1], n2[1] = _mm_res_norm("attn_b_out", o_b.reshape(1, t, dn), w_o_b, 0, h1[1], ffn2_norm[1:2], 1.0)
    w_in2[1], w_out2[1] = as_w_in(ag[7][0]), as_w_out(ag[7][1])
    gu2[1], a2[1] = _ffn_in("ffn2_in_1", n2[1], w_in2[1], 0)
    (h_last,) = _mm_res_norm("ffn2_out_1", a2[1], w_out2[1], 0, h2[1], None, 0.5)
    dh, dhb, dg_final, loss_part = _loss_final(h_last, target2, final_norm.reshape(1, dn))

    dg_ffn1, dg_mix, dg_ffn2, rs_ffn1, rs_ffn2 = ([None, None] for _ in range(5))

    def whole(rows, cols):
        return BS((rows, cols), lambda j: (0, 0))

    def dw_rows(name, xa, ya):
        n = ya.shape[1]
        return _mm_tn(name, xa, whole(t, dn), ya, whole(t, n), (dn, n), whole(dn, n), 1).reshape(NDEV, dn // NDEV, n)

    dh, dhb, dg_ffn2[1], rs_ffn2[1] = _ffn_bwd(
        "ffn2_1", dh, dhb, n2[1], h2[1], ffn2_norm[1:2], gu2[1], a2[1], w_in2[1], w_out2[1], REDUCE_IDS[0], ()
    )
    do_b = _mm_nt_plain("attn_b_do", dhb, w_o_b.reshape(dn, dn))
    dw_o_b = dw_rows("attn_b_dwo", o_b, dhb)
    dq_pre, dkv, dkr = _mla_bwd(q, kv, kr, o_b, lse_b, do_b, c192, s192, p192, bl, seq)
    dw_uq = _mm_tn(
        "dw_uq", cq, whole(t, Q_LORA), dq_pre, BS((None, t, QK_B), lambda j: (j, 0, 0)),
        (HEADS_B, Q_LORA, QK_B), BS((None, Q_LORA, QK_B), lambda j: (j, 0, 0)), HEADS_B,
    )
    dcq_pre, dg_q = _mm_nt_norm_bwd(
        "dcq", dq_pre, BS((HEADS_B, tm, QK_B), lambda i, j: (0, i, 0)), w_uq, BS((HEADS_B, Q_LORA, QK_B), lambda i, j: (0, 0, 0)),
        1, cq_pre, q_norm, None, BF16, mm_fn=_heads_mm,
    )
    dw_dq = dw_rows("dw_dq", hn[1], dcq_pre)
    dh, dg_mix[1], dhb = _mm_nt_norm_bwd(
        "dhn_b", dcq_pre, BS((tm, Q_LORA), lambda i, j: (i, 0)), w_dq, BS((dn, Q_LORA), lambda i, j: (0, 0)),
        1, h1[1], mix_norm[1:2], dh, F32,
    )
    dh, dhb, dg_ffn1[1], rs_ffn1[1] = _ffn_bwd(
        "ffn1_1", dh, dhb, n1[1], h0[1], ffn1_norm[1:2], gu1[1], a1[1], w_in1[1], w_out1[1], REDUCE_IDS[1], rs_ffn2[1][:1]
    )
    dw_up = _mm_tn(
        "dw_up", ckv, whole(t, KV_LORA), dkv, BS((t, NOPE + V_DIM), lambda j: (0, j)),
        (HEADS_B, KV_LORA, NOPE + V_DIM), BS((None, KV_LORA, NOPE + V_DIM), lambda j: (j, 0, 0)), HEADS_B,
    )
    dckr, dg_latent = _kv_latent_bwd(dkv, w_up, ckr, latent_norm, dkr, c64, s64, p64, seq)
    dw_down = dw_rows("dw_down", hk, dckr)
    dh, dg_kv, dhb = _mm_nt_norm_bwd(
        "dhk", dckr, BS((tm, KV_LORA + ROPE), lambda i, j: (i, 0)), w_down, BS((dn, KV_LORA + ROPE), lambda i, j: (0, 0)),
        1, h0[1], kv_norm.reshape(1, dn), dh, F32,
    )
    dh, dhb, dg_ffn2[0], rs_ffn2[0] = _ffn_bwd(
        "ffn2_0", dh, dhb, n2[0], h2[0], ffn2_norm[0:1], gu2[0], a2[0], w_in2[0], w_out2[0], REDUCE_IDS[2], rs_ffn1[1][:1]
    )
    do_a = _mm_nt_plain("attn_a_do", dhb, w_o_a.reshape(dn, dn))
    dw_o_a = dw_rows("attn_a_dwo", o_a, dhb)
    dqkv3, dbias = _attn_a_bwd(qkv3, o_a, lse_a, do_a, bias, bl, seq)
    dw_qkv = _dw_qkv(hn[0], dqkv3, qkv_wb)
    mixer_grads = [dw_o_a, dw_qkv, dw_o_b, dw_uq, dw_dq, dw_up, dw_down]
    dh, dg_mix[0], dhb = _mm_nt_norm_bwd(
        "dhn_a", dqkv3, BS((3, tm, dn), lambda i, j: (0, i, 0)), w_qkv, BS((NDEV, dn, qkv_wb), lambda i, j: (0, 0, 0)),
        1, h1[0], mix_norm[0:1], dh, F32, mm_fn=_dqkv_mm(qkv_wb // 128), after=mixer_grads,
    )
    rs_mixers = _exchange_sc("mixers_reduce", [("scatter", g) for g in mixer_grads], REDUCE_IDS[3], rs_ffn2[0][:1])
    dh, dhb, dg_ffn1[0], rs_ffn1[0] = _ffn_bwd(
        "ffn1_0", dh, dhb, n1[0], h0[0], ffn1_norm[0:1], gu1[0], a1[0], w_in1[0], w_out1[0], REDUCE_IDS[4], rs_mixers[:1]
    )
    grad_x = dh.reshape(bl, seq, dn)
    dtable = jnp.pad(_rel_bias_bwd(_window_bias_bwd(dbias)), ((0, 0), (1, 0)))

    def update(name, parts, w, m, v):
        shape3 = (len(parts),) + w.shape[-2:]
        parts = [p.reshape((NDEV,) + shape3[1:]) for p in parts]
        outs = _adamw(name, parts, w.reshape(shape3), m.reshape(shape3), v.reshape(shape3))
        return [o.reshape(w.shape) for o in outs]

    res = {}
    r_in2_1, r_out2_1 = rs_ffn2[1]
    r_in1_1, r_out1_1 = rs_ffn1[1]
    r_in2_0, r_out2_0 = rs_ffn2[0]
    r_in1_0, r_out1_0 = rs_ffn1[0]
    r_o_a, r_qkv, r_o_b, r_uq, r_dq, r_up, r_down = rs_mixers
    def update_transposed(name, parts, w, m, v):
        outs = update(name, parts, *[jnp.swapaxes(a, 1, 2) for a in (w, m, v)])
        return [jnp.swapaxes(o, 1, 2) for o in outs]

    res["ffn2_w_in"] = update_transposed("adamw_ffn2_w_in", [r_in2_0, r_in2_1], ffn2_w_in, m_ffn2_w_in, v_ffn2_w_in)
    res["ffn2_w_out"] = update("adamw_ffn2_w_out", [r_out2_0, r_out2_1], ffn2_w_out, m_ffn2_w_out, v_ffn2_w_out)
    res["kv_w_down"] = update("adamw_kv_w_down", [r_down], kv_w_down, m_kv_w_down, v_kv_w_down)
    res["kv_w_up"] = update("adamw_kv_w_up", [r_up], kv_w_up, m_kv_w_up, v_kv_w_up)
    res["b_w_dq"] = update("adamw_b_w_dq", [r_dq], b_w_dq, m_b_w_dq, v_b_w_dq)
    res["b_w_uq"] = update("adamw_b_w_uq", [r_uq], b_w_uq, m_b_w_uq, v_b_w_uq)
    res["b_w_o"] = update("adamw_b_w_o", [r_o_b], b_w_o, m_b_w_o, v_b_w_o)
    res["a_w_qkv"] = update("adamw_a_w_qkv", [r_qkv], a_w_qkv, m_a_w_qkv, v_a_w_qkv)
    res["a_w_o"] = update("adamw_a_w_o", [r_o_a], a_w_o, m_a_w_o, v_a_w_o)

    small = _pack_small(
        jnp.stack([dg_ffn1[0][0], dg_ffn1[1][0]]), jnp.stack([dg_mix[0][0], dg_mix[1][0]]), jnp.stack([dg_ffn2[0][0], dg_ffn2[1][0]]),
        dg_kv[0], dg_final[0], dg_q[0], dg_latent[0], dtable, loss_part[0],
    )
    done = [r[1] for name, r in res.items() if name != "ffn2_w_in"]
    (r_small,) = _exchange("gather_small_grads", [("gather", small)], after=done)
    res["ffn1_w_in"] = update_transposed("adamw_ffn1_w_in", [r_in1_0, r_in1_1], ffn1_w_in, m_ffn1_w_in, v_ffn1_w_in)
    res["ffn1_w_out"] = update("adamw_ffn1_w_out", [r_out1_0, r_out1_1], ffn1_w_out, m_ffn1_w_out, v_ffn1_w_out)
    zero_row = jnp.zeros((dn,), F32)
    packs = [
        _pack_small(f1, mx, f2, kvn, fin, qn, lat, rel, zero_row)
        for f1, mx, f2, kvn, fin, qn, lat, rel in (
            (ffn1_norm, mix_norm, ffn2_norm, kv_norm, final_norm, b_q_norm, kv_latent_norm, a_rel_bias),
            (m_ffn1_norm, m_mix_norm, m_ffn2_norm, m_kv_norm, m_final_norm, m_b_q_norm, m_kv_latent_norm, m_a_rel_bias),
            (v_ffn1_norm, v_mix_norm, v_ffn2_norm, v_kv_norm, v_final_norm, v_b_q_norm, v_kv_latent_norm, v_a_rel_bias),
        )
    ]
    small_out = [_unpack_small(o[0]) for o in _adamw("adamw_small", [r_small], *[p[None] for p in packs])]
    for name in ("ffn1_norm", "mix_norm", "ffn2_norm", "a_rel_bias", "kv_norm", "kv_latent_norm", "b_q_norm", "final_norm"):
        res[name] = [so[name] for so in small_out]
    loss = small_out[0]["last"][0]

    order = [
        "ffn1_norm", "ffn1_w_in", "ffn1_w_out", "mix_norm", "ffn2_norm", "ffn2_w_in", "ffn2_w_out", "a_w_qkv", "a_rel_bias",
        "a_w_o", "kv_norm", "kv_w_down", "kv_latent_norm", "kv_w_up", "b_w_dq", "b_q_norm", "b_w_uq", "b_w_o", "final_norm",
    ]
    return (loss, grad_x, *[res[n][0] for n in order], *[res[n][1] for n in order], *[res[n][2] for n in order], *[res[n][3] for n in order])
```

```python
import jax
import jax.numpy as jnp
import numpy as np
from jax import lax
from jax.experimental import pallas as pl
from jax.experimental.pallas import tpu as pltpu
from jax.experimental.pallas import tpu_sc as plsc

NDEV = 8
D_MODEL = 1024
D_FF = 2816
FB = 2 * D_FF // NDEV
NJ = D_FF // FB
CHUNK = 64
LEFT_CHUNKS = 8
PAD = LEFT_CHUNKS * CHUNK
BAND = PAD + CHUNK
CHUNKS_PER_STEP = 4
WINDOW = PAD + CHUNKS_PER_STEP * CHUNK
STEP_ROWS = CHUNKS_PER_STEP * 2 * CHUNK
MAX_REL = 128
NREL = 2 * MAX_REL + 1
NREL_USED = 256
HEADS_A = 16
HEADS_B = 8
NOPE = 128
ROPE = 64
QK_B = NOPE + ROPE
V_DIM = 128
Q_LORA = 768
KV_LORA = 256
ROPE_THETA = 10000.0
EPS = 1e-6
NEG_INF = -1e30
MLA_TQ = 256
MLA_TK_FWD = 256
MLA_TK_BWD = 1024
ADAM_LR = 0.001
ADAM_B1 = 0.9
ADAM_B2 = 0.999
ADAM_EPS = 1e-08
ADAM_WD = 0.01
ADAM_STEP = 10
PACK_ROWS = 16
GATHER_IDS = tuple(range(1, 9))
REDUCE_IDS = tuple(range(9, 14))
VMEM_LIMIT_BYTES = 56 * 1024 * 1024

F32 = jnp.float32
BF16 = jnp.bfloat16
SDS = jax.ShapeDtypeStruct
BS = pl.BlockSpec
MESH = pl.DeviceIdType.MESH


def _cparams(n_axes):
    return pltpu.CompilerParams(dimension_semantics=("arbitrary",) * n_axes, vmem_limit_bytes=VMEM_LIMIT_BYTES)


def _tile(t, want=512):
    return want if t % want == 0 else t


def _dot(a, b):
    return jnp.dot(a, b, preferred_element_type=F32)


def _dot_nt(a, b):
    return lax.dot_general(a, b, (((1,), (1,)), ((), ())), preferred_element_type=F32)


def _dot_tn(a, b):
    return lax.dot_general(a, b, (((0,), (0,)), ((), ())), preferred_element_type=F32)


def _split3(a):
    hi = a.astype(BF16)
    rest = a - hi.astype(F32)
    mid = rest.astype(BF16)
    return hi, mid, (rest - mid.astype(F32)).astype(BF16)


def _dot_exact(a, onehot, transposed=False):
    ob = onehot.astype(BF16)
    dot = _dot_nt if transposed else _dot
    hi, mid, lo = _split3(a)
    return dot(hi, ob) + dot(mid, ob) + dot(lo, ob)


def _rms_scale(h):
    return lax.rsqrt(jnp.mean(h * h, axis=-1, keepdims=True) + EPS)


def _acc_rows(ref, val, step, n_steps):
    part = val.reshape(val.shape[0] // 8, 8, val.shape[1]).sum(axis=0)

    @pl.when(step == 0)
    def _():
        ref[...] = part

    @pl.when(step > 0)
    def _():
        ref[...] += part

    @pl.when(step == n_steps - 1)
    def _():
        ref[...] = jnp.broadcast_to(jnp.sum(ref[...], axis=0, keepdims=True), ref.shape)


def _exchange_plan(entries):
    ins = [e[1] for e in entries]
    kinds = [e[0] for e in entries]
    lands = [SDS((NDEV,) + a.shape if k == "gather" else a.shape, a.dtype) for k, a in zip(kinds, ins)]
    return ins, lands, kinds


def _mesh_place():
    x, y, c = lax.axis_index("x"), lax.axis_index("y"), lax.axis_index("c")
    return (x, y, c), 4 * x + 2 * y + c


def _flipped(place, p):
    x, y, c = place
    px = 1 - x if p & 4 else x
    py = 1 - y if p & 2 else y
    pc = 1 - c if p & 1 else c
    return (px, py, pc), 4 * px + 2 * py + pc


def _ends(kind, src_ref, land_ref, origin, target):
    if kind == "gather":
        return src_ref, land_ref.at[origin]
    return src_ref.at[target], land_ref.at[origin]


def _remote(kind, src_ref, land_ref, send_sems, recv_sems, k, p, place, me, arriving):
    peer_pos, peer = _flipped(place, p)
    src, dst = _ends(kind, src_ref, land_ref, me, peer)
    if arriving:
        dst = _ends(kind, src_ref, land_ref, peer, me)[1]
    sem = k * (NDEV - 1) + p - 1
    return pltpu.make_async_remote_copy(
        src_ref=src, dst_ref=dst, send_sem=send_sems.at[sem], recv_sem=recv_sems.at[sem], device_id=peer_pos, device_id_type=MESH,
    )


def _exchange(name, entries, after=()):
    ins, lands, kinds = _exchange_plan(entries)
    n = len(ins)
    after = tuple(after)

    def body(*refs):
        refs = refs[:n] + refs[n + len(after) :]
        in_refs, land_refs = refs[:n], refs[n : 2 * n]
        send_sems, recv_sems, local_sems = refs[2 * n :]
        place, me = _mesh_place()
        local = []
        for k in range(n):
            src, dst = _ends(kinds[k], in_refs[k], land_refs[k], me, me)
            local.append(pltpu.make_async_copy(src, dst, local_sems.at[k]))
            local[-1].start()
        sends = []
        for p in range(1, NDEV):
            for k in range(n):
                sends.append(_remote(kinds[k], in_refs[k], land_refs[k], send_sems, recv_sems, k, p, place, me, False))
                sends[-1].start()
        for p in range(1, NDEV):
            for k in range(n):
                _remote(kinds[k], in_refs[k], land_refs[k], send_sems, recv_sems, k, p, place, me, True).wait_recv()
        for cp in sends:
            cp.wait_send()
        for cp in local:
            cp.wait()

    any_spec = BS(memory_space=pl.ANY)
    return pl.pallas_call(
        body, name=name, out_shape=lands, in_specs=[any_spec] * (n + len(after)), out_specs=[any_spec] * n,
        scratch_shapes=[
            pltpu.SemaphoreType.DMA((n * (NDEV - 1),)), pltpu.SemaphoreType.DMA((n * (NDEV - 1),)), pltpu.SemaphoreType.DMA((n,)),
        ],
    )(*ins, *after)


def _exchange_sc(name, entries, collective_id, after=()):
    ins, lands, kinds = _exchange_plan(entries)
    n = len(ins)
    after = tuple(after)

    def launch(*refs):
        refs = refs[:n] + refs[n + len(after) :]
        in_refs, land_refs = refs[:n], refs[n : 2 * n]
        send_sems, recv_sems, local_sems = refs[2 * n :]
        place, me = _mesh_place()
        barrier = pltpu.get_barrier_semaphore()
        for p in range(1, NDEV):
            pl.semaphore_signal(barrier, inc=1, device_id=_flipped(place, p)[0], device_id_type=MESH)
        pl.semaphore_wait(barrier, NDEV - 1)
        local = []
        for k in range(n):
            src, dst = _ends(kinds[k], in_refs[k], land_refs[k], me, me)
            local.append(pltpu.make_async_copy(src, dst, local_sems.at[k]))
            local[-1].start()
        sends = []
        if all(kind == "gather" for kind in kinds):
            for p in (1, 2, 4, 6):
                for k in range(n):
                    sends.append(_remote(kinds[k], in_refs[k], land_refs[k], send_sems, recv_sems, k, p, place, me, False))
                    sends[-1].start()
            sibling_pos, _ = _flipped(place, 1)
            for f in (2, 4, 6):
                _, origin = _flipped(place, f)
                for k in range(n):
                    _remote(kinds[k], in_refs[k], land_refs[k], send_sems, recv_sems, k, f, place, me, True).wait_recv()
                    sem = k * (NDEV - 1) + f
                    sends.append(
                        pltpu.make_async_remote_copy(
                            src_ref=land_refs[k].at[origin], dst_ref=land_refs[k].at[origin], send_sem=send_sems.at[sem],
                            recv_sem=recv_sems.at[sem], device_id=sibling_pos, device_id_type=MESH,
                        )
                    )
                    sends[-1].start()
            for p in (1, 3, 5, 7):
                for k in range(n):
                    _remote(kinds[k], in_refs[k], land_refs[k], send_sems, recv_sems, k, p, place, me, True).wait_recv()
        else:
            for p in range(1, NDEV):
                for k in range(n):
                    sends.append(_remote(kinds[k], in_refs[k], land_refs[k], send_sems, recv_sems, k, p, place, me, False))
                    sends[-1].start()
            for p in range(1, NDEV):
                for k in range(n):
                    _remote(kinds[k], in_refs[k], land_refs[k], send_sems, recv_sems, k, p, place, me, True).wait_recv()
        for cp in sends:
            cp.wait_send()
        for cp in local:
            cp.wait()

    return pl.kernel(
        launch, out_type=tuple(lands), mesh=plsc.ScalarSubcoreMesh(axis_name="sequencer", num_cores=1), name=name,
        scratch_types=(
            pltpu.SemaphoreType.DMA((n * (NDEV - 1),)), pltpu.SemaphoreType.DMA((n * (NDEV - 1),)), pltpu.SemaphoreType.DMA((n,)),
        ),
        compiler_params=pltpu.CompilerParams(collective_id=collective_id),
    )(*ins, *after)


def _norm_fwd(name, h, gammas):
    t, dn = h.shape
    ng = gammas.shape[0]
    tm = _tile(t)

    def body(h_ref, g_ref, *outs):
        hv = h_ref[...]
        hh = hv * _rms_scale(hv)
        for i, o_ref in enumerate(outs):
            o_ref[...] = (hh * g_ref[i : i + 1, :]).astype(BF16)

    row = BS((tm, dn), lambda i: (i, 0))
    return pl.pallas_call(
        body, name=name, grid=(t // tm,),
        in_specs=[row, BS((ng, dn), lambda i: (0, 0))],
        out_specs=[row] * ng, out_shape=[SDS((t, dn), BF16)] * ng,
        compiler_params=_cparams(1),
    )(h, gammas)


def _ffn_in(name, n, w_in, layer):
    t, dn = n.shape
    tm = _tile(t, 1024)

    def body(n_ref, wg_ref, wu_ref, gu_ref, a_ref):
        xv = n_ref[...]
        g = _dot(xv, wg_ref[...])
        u = _dot(xv, wu_ref[...])
        gu_ref[0] = g.astype(BF16)
        gu_ref[1] = u.astype(BF16)
        a_ref[...] = (g * jax.nn.sigmoid(g) * u).astype(BF16)

    return pl.pallas_call(
        body, name=name, grid=(NJ, t // tm),
        in_specs=[
            BS((tm, dn), lambda j, i: (i, 0)),
            BS((None, None, dn, FB), lambda j, i: (layer, j, 0, 0)),
            BS((None, None, dn, FB), lambda j, i: (layer, j + NJ, 0, 0)),
        ],
        out_specs=[BS((None, 2, tm, FB), lambda j, i: (j, 0, i, 0)), BS((None, tm, FB), lambda j, i: (j, i, 0))],
        out_shape=[SDS((NJ, 2, t, FB), BF16), SDS((NJ, t, FB), BF16)],
        compiler_params=_cparams(2),
    )(n, w_in, w_in)


def _mm_res_norm(name, a, w, layer, h_in, gammas, scale):
    nk, t, kb = a.shape
    dn = w.shape[-1]
    ng = 0 if gammas is None else gammas.shape[0]
    tm = _tile(t)

    def body(*refs):
        a_ref, w_ref, h_ref = refs[:3]
        g_ref = refs[3] if ng else None
        outs = refs[3 + (1 if ng else 0) :]
        acc = _dot(a_ref[0], w_ref[0])
        for k in range(1, nk):
            acc += _dot(a_ref[k], w_ref[k])
        ho = h_ref[...] + scale * acc
        outs[0][...] = ho
        if ng:
            hh = ho * _rms_scale(ho)
            for i in range(ng):
                outs[1 + i][...] = (hh * g_ref[i : i + 1, :]).astype(BF16)

    row = BS((tm, dn), lambda i: (i, 0))
    in_specs = [BS((nk, tm, kb), lambda i: (0, i, 0)), BS((None, nk, kb, dn), lambda i: (layer, 0, 0, 0)), row]
    args = [a, w, h_in]
    if ng:
        in_specs.append(BS((ng, dn), lambda i: (0, 0)))
        args.append(gammas)
    return pl.pallas_call(
        body, name=name, grid=(t // tm,),
        in_specs=in_specs,
        out_specs=[row] * (1 + ng), out_shape=[SDS((t, dn), F32)] + [SDS((t, dn), BF16)] * ng,
        compiler_params=_cparams(1),
    )(*args)


def _qkv_proj(name, hn, w_qkv):
    t, dn = hn.shape
    wb = w_qkv.shape[-1]
    per = wb // 128
    tm = _tile(t)

    def body(x_ref, w_ref, o_ref):
        xv = x_ref[...]
        for j in range(NDEV):
            yv = _dot(xv, w_ref[j]).astype(BF16)
            for i in range(per):
                n = per * j + i
                o_ref[n // 8, :, (n % 8) * 128 : (n % 8 + 1) * 128] = yv[:, i * 128 : (i + 1) * 128]

    return pl.pallas_call(
        body, name=name, grid=(t // tm,),
        in_specs=[BS((tm, dn), lambda i: (i, 0)), BS((NDEV, dn, wb), lambda i: (0, 0, 0))],
        out_specs=BS((3, tm, dn), lambda i: (0, i, 0)), out_shape=SDS((3, t, dn), BF16),
        compiler_params=_cparams(1),
    )(hn, w_qkv)


def _rel_onehot(i):
    r = lax.broadcasted_iota(jnp.int32, (NREL_USED, BAND), 0)
    j = lax.broadcasted_iota(jnp.int32, (NREL_USED, BAND), 1)
    idx = jnp.clip(PAD + i - j, -MAX_REL, MAX_REL) + MAX_REL
    return (idx - 1 == r).astype(F32)


def _rel_bias_fwd(table):
    def body(t_ref, o_ref):
        i8 = pl.program_id(0)
        for ii in range(8):
            o_ref[:, ii, :] = _dot_exact(t_ref[...], _rel_onehot(i8 * 8 + ii))

    return pl.pallas_call(
        body, name="rel_bias_fwd", grid=(CHUNK // 8,),
        in_specs=[BS((HEADS_A, NREL_USED), lambda i: (0, 0))],
        out_specs=BS((HEADS_A, 8, BAND), lambda i: (0, i, 0)), out_shape=SDS((HEADS_A, CHUNK, BAND), F32),
        compiler_params=_cparams(1),
    )(table)


def _rel_bias_bwd(dbias):
    def body(d_ref, o_ref):
        i8 = pl.program_id(0)
        acc = jnp.zeros((HEADS_A, NREL_USED), F32)
        for ii in range(8):
            acc += _dot_exact(d_ref[:, ii, :], _rel_onehot(i8 * 8 + ii), transposed=True)

        @pl.when(i8 == 0)
        def _():
            o_ref[...] = acc

        @pl.when(i8 > 0)
        def _():
            o_ref[...] += acc

    return pl.pallas_call(
        body, name="rel_bias_bwd", grid=(CHUNK // 8,),
        in_specs=[BS((HEADS_A, 8, BAND), lambda i: (0, i, 0))],
        out_specs=BS((HEADS_A, NREL_USED), lambda i: (0, 0)), out_shape=SDS((HEADS_A, NREL_USED), F32),
        compiler_params=_cparams(1),
    )(dbias)


def _window_bias(bias):
    b = bias.reshape(HEADS_A // 2, 2, CHUNK, BAND)
    per_chunk = [
        jnp.pad(b, ((0, 0), (0, 0), (0, 0), (cc * CHUNK, WINDOW - BAND - cc * CHUNK)), constant_values=NEG_INF)
        for cc in range(CHUNKS_PER_STEP)
    ]
    return jnp.stack(per_chunk, axis=1).reshape(HEADS_A // 2, STEP_ROWS, WINDOW)


def _window_bias_bwd(dwin):
    d = dwin.reshape(HEADS_A // 2, CHUNKS_PER_STEP, 2, CHUNK, WINDOW)
    return sum(d[:, cc, :, :, cc * CHUNK : cc * CHUNK + BAND] for cc in range(CHUNKS_PER_STEP)).reshape(HEADS_A, CHUNK, BAND)


def _step_rows(xs, lane):
    parts = []
    for cc in range(CHUNKS_PER_STEP):
        xc = xs[cc * CHUNK : (cc + 1) * CHUNK]
        parts.append(jnp.where(lane < 64, xc, jnp.zeros_like(xc)))
        parts.append(jnp.where(lane >= 64, xc, jnp.zeros_like(xc)))
    return jnp.concatenate(parts, axis=0)


def _pair_rows(ys, lane):
    parts = []
    for cc in range(CHUNKS_PER_STEP):
        y0 = ys[(2 * cc) * CHUNK : (2 * cc + 1) * CHUNK]
        y1 = ys[(2 * cc + 1) * CHUNK : (2 * cc + 2) * CHUNK]
        parts.append(jnp.where(lane < 64, y0, y1))
    return jnp.concatenate(parts, axis=0)


def _window_scores(q_rows, kwin, bias_win, first_key):
    s = _dot_nt(q_rows, kwin) * (CHUNK ** -0.5) + bias_win
    if first_key is None:
        return s
    col = lax.broadcasted_iota(jnp.int32, s.shape, 1)
    return jnp.where(col >= first_key, s, NEG_INF)


def _window_loop(n_passes, chunks):
    n_padded = min(PAD // (CHUNKS_PER_STEP * CHUNK), n_passes)
    lax.fori_loop(0, n_padded, lambda it, carry: chunks(it, carry, True), 0, unroll=2)
    if n_passes > n_padded:
        lax.fori_loop(n_padded, n_passes, lambda it, carry: chunks(it, carry, False), 0, unroll=2)


def _attn_a_fwd(qkv3, bias_win, bl, seq):
    t, dn = qkv3.shape[1:]
    npair = dn // 128
    step = CHUNKS_PER_STEP * CHUNK

    def body(q_ref, k_ref, v_ref, b_ref, o_ref, lse_ref, kpad, vpad):
        kpad[0:PAD, :] = jnp.zeros((PAD, 128), BF16)
        vpad[0:PAD, :] = jnp.zeros((PAD, 128), BF16)
        kpad[PAD:, :] = k_ref[...]
        vpad[PAD:, :] = v_ref[...]
        lane = lax.broadcasted_iota(jnp.int32, (CHUNK, 128), 1)

        def chunks(it, carry, padded):
            r0 = pl.multiple_of(it * step, step)
            q_rows = _step_rows(q_ref[pl.ds(r0, step), :], lane)
            s = _window_scores(q_rows, kpad[pl.ds(r0, WINDOW), :], b_ref[...], PAD - r0 if padded else None)
            m = jnp.max(s, axis=-1, keepdims=True)
            e = jnp.exp(s - m)
            total = jnp.sum(e, axis=-1, keepdims=True)
            o_rows = _dot((e * (1.0 / total)).astype(BF16), vpad[pl.ds(r0, WINDOW), :])
            o_ref[pl.ds(r0, step), :] = _pair_rows(o_rows, lane).astype(BF16)
            lse_ref[pl.ds(pl.multiple_of(it * STEP_ROWS, STEP_ROWS), STEP_ROWS), :] = m + jnp.log(total)
            return carry

        _window_loop(seq // step, chunks)

    return pl.pallas_call(
        body, name="attn_a_fwd", grid=(bl, npair),
        in_specs=[
            BS((None, seq, 128), lambda b, h: (0, b, h)),
            BS((None, seq, 128), lambda b, h: (1, b, h)),
            BS((None, seq, 128), lambda b, h: (2, b, h)),
            BS((None, STEP_ROWS, WINDOW), lambda b, h: (h, 0, 0)),
        ],
        out_specs=[BS((seq, 128), lambda b, h: (b, h)), BS((None, 2 * seq, 1), lambda b, h: (h, b, 0))],
        out_shape=[SDS((t, dn), BF16), SDS((npair, 2 * t, 1), F32)],
        scratch_shapes=[pltpu.VMEM((PAD + seq, 128), BF16), pltpu.VMEM((PAD + seq, 128), BF16)],
        compiler_params=_cparams(2),
    )(qkv3, qkv3, qkv3, bias_win)


def _attn_a_bwd(qkv3, out, lse, do, bias_win, bl, seq):
    t, dn = qkv3.shape[1:]
    npair = dn // 128
    step = CHUNKS_PER_STEP * CHUNK

    def body(q_ref, k_ref, v_ref, o_ref, lse_ref, do_ref, b_ref, dqkv_ref, db_ref, kpad, vpad, dkacc, dvacc):
        b = pl.program_id(1)
        kpad[0:PAD, :] = jnp.zeros((PAD, 128), BF16)
        vpad[0:PAD, :] = jnp.zeros((PAD, 128), BF16)
        kpad[PAD:, :] = k_ref[...]
        vpad[PAD:, :] = v_ref[...]
        dkacc[...] = jnp.zeros_like(dkacc)
        dvacc[...] = jnp.zeros_like(dvacc)

        @pl.when(b == 0)
        def _():
            db_ref[...] = jnp.zeros_like(db_ref)

        lane = lax.broadcasted_iota(jnp.int32, (CHUNK, 128), 1)

        def chunks(it, carry, padded):
            r0 = pl.multiple_of(it * step, step)
            q_rows = _step_rows(q_ref[pl.ds(r0, step), :], lane)
            do_rows = _step_rows(do_ref[pl.ds(r0, step), :], lane)
            kwin = kpad[pl.ds(r0, WINDOW), :]
            vwin = vpad[pl.ds(r0, WINDOW), :]
            o_rows = _step_rows(o_ref[pl.ds(r0, step), :], lane)
            delta = jnp.sum(do_rows.astype(F32) * o_rows.astype(F32), axis=-1, keepdims=True)
            lse_rows = lse_ref[pl.ds(pl.multiple_of(it * STEP_ROWS, STEP_ROWS), STEP_ROWS), :]
            p = jnp.exp(_window_scores(q_rows, kwin, b_ref[...], PAD - r0 if padded else None) - lse_rows)
            ds = p * (_dot_nt(do_rows, vwin) - delta)
            db_ref[...] += ds
            dsb = (ds * (CHUNK ** -0.5)).astype(BF16)
            dqkv_ref[0, pl.ds(r0, step), :] = _pair_rows(_dot(dsb, kwin), lane).astype(BF16)
            dkacc[pl.ds(r0, WINDOW), :] += _dot_tn(dsb, q_rows)
            dvacc[pl.ds(r0, WINDOW), :] += _dot_tn(p.astype(BF16), do_rows)
            return carry

        _window_loop(seq // step, chunks)
        dqkv_ref[1] = dkacc[PAD:, :].astype(BF16)
        dqkv_ref[2] = dvacc[PAD:, :].astype(BF16)

    return pl.pallas_call(
        body, name="attn_a_bwd", grid=(npair, bl),
        in_specs=[
            BS((None, seq, 128), lambda h, b: (0, b, h)),
            BS((None, seq, 128), lambda h, b: (1, b, h)),
            BS((None, seq, 128), lambda h, b: (2, b, h)),
            BS((seq, 128), lambda h, b: (b, h)),
            BS((None, 2 * seq, 1), lambda h, b: (h, b, 0)),
            BS((seq, 128), lambda h, b: (b, h)),
            BS((None, STEP_ROWS, WINDOW), lambda h, b: (h, 0, 0)),
        ],
        out_specs=[BS((3, seq, 128), lambda h, b: (0, b, h)), BS((None, STEP_ROWS, WINDOW), lambda h, b: (h, 0, 0))],
        out_shape=[SDS((3, t, dn), BF16), SDS((HEADS_A // 2, STEP_ROWS, WINDOW), F32)],
        scratch_shapes=[
            pltpu.VMEM((PAD + seq, 128), BF16), pltpu.VMEM((PAD + seq, 128), BF16),
            pltpu.VMEM((PAD + seq, 128), F32), pltpu.VMEM((PAD + seq, 128), F32),
        ],
        compiler_params=_cparams(2),
    )(qkv3, qkv3, qkv3, out, lse, do, bias_win)


def _rope_tables(seq):
    half = ROPE // 2
    freqs = ROPE_THETA ** (-jnp.arange(half, dtype=F32) / half)
    ang = jnp.arange(seq, dtype=F32)[:, None] * freqs[None, :]
    cos, sin = jnp.cos(ang), jnp.sin(ang)
    c64 = jnp.concatenate([cos, cos], axis=1)
    s64 = jnp.concatenate([-sin, sin], axis=1)
    c192 = jnp.concatenate([jnp.ones((seq, NOPE), F32), c64], axis=1)
    s192 = jnp.concatenate([jnp.zeros((seq, NOPE), F32), s64], axis=1)
    p64 = np.zeros((ROPE, ROPE), np.float32)
    for col in range(ROPE):
        p64[(col + half) % ROPE, col] = 1.0
    p192 = np.zeros((QK_B, QK_B), np.float32)
    p192[NOPE:, NOPE:] = p64
    return c64, s64, jnp.asarray(p64), c192, s192, jnp.asarray(p192)


def _rope(xv, cos, sin_signed, swap):
    return xv * cos + _dot_exact(xv, swap) * sin_signed


def _rope_bwd(dy, cos, sin_signed, swap):
    return dy * cos + _dot_exact(dy * sin_signed, swap)


def _q_down(hn, w_dq, q_norm):
    t, dn = hn.shape
    ql = w_dq.shape[1]
    tm = _tile(t)

    def body(x_ref, w_ref, g_ref, pre_ref, cq_ref):
        pre = _dot(x_ref[...], w_ref[...])
        pre_ref[...] = pre
        cq_ref[...] = (pre * _rms_scale(pre) * g_ref[...]).astype(BF16)

    return pl.pallas_call(
        body, name="q_down", grid=(t // tm,),
        in_specs=[BS((tm, dn), lambda i: (i, 0)), BS((dn, ql), lambda i: (0, 0)), BS((1, ql), lambda i: (0, 0))],
        out_specs=[BS((tm, ql), lambda i: (i, 0))] * 2, out_shape=[SDS((t, ql), F32), SDS((t, ql), BF16)],
        compiler_params=_cparams(1),
    )(hn, w_dq, q_norm)


def _q_up(cq, w_uq, c192, s192, p192, seq):
    t, ql = cq.shape
    tm = _tile(min(seq, 512), min(seq, 512))
    nseq = seq // tm

    def body(x_ref, w_ref, c_ref, s_ref, p_ref, o_ref):
        xv = x_ref[...]
        for h in range(HEADS_B):
            o_ref[h] = _rope(_dot(xv, w_ref[h]), c_ref[...], s_ref[...], p_ref[...]).astype(BF16)

    pos = BS((tm, QK_B), lambda i: (i % nseq, 0))
    return pl.pallas_call(
        body, name="q_up", grid=(t // tm,),
        in_specs=[
            BS((tm, ql), lambda i: (i, 0)), BS((HEADS_B, ql, QK_B), lambda i: (0, 0, 0)), pos, pos,
            BS((QK_B, QK_B), lambda i: (0, 0)),
        ],
        out_specs=BS((HEADS_B, tm, QK_B), lambda i: (0, i, 0)), out_shape=SDS((HEADS_B, t, QK_B), BF16),
        compiler_params=_cparams(1),
    )(cq, w_uq, c192, s192, p192)


def _kv_down(hk, w_down, latent_norm, c64, s64, p64, seq):
    t, dn = hk.shape
    wd = w_down.shape[1]
    tm = _tile(min(seq, 512), min(seq, 512))
    nseq = seq // tm

    def body(x_ref, w_ref, g_ref, c_ref, s_ref, p_ref, ckr_ref, ckv_ref, kr_ref):
        ckr = _dot(x_ref[...], w_ref[...])
        ckr_ref[...] = ckr
        lat = ckr[:, :KV_LORA]
        ckv_ref[...] = (lat * _rms_scale(lat) * g_ref[...]).astype(BF16)
        kr_ref[...] = _rope(ckr[:, KV_LORA:], c_ref[...], s_ref[...], p_ref[...]).astype(BF16)

    pos = BS((tm, ROPE), lambda i: (i % nseq, 0))
    return pl.pallas_call(
        body, name="kv_down", grid=(t // tm,),
        in_specs=[
            BS((tm, dn), lambda i: (i, 0)), BS((dn, wd), lambda i: (0, 0)), BS((1, KV_LORA), lambda i: (0, 0)), pos, pos,
            BS((ROPE, ROPE), lambda i: (0, 0)),
        ],
        out_specs=[BS((tm, wd), lambda i: (i, 0)), BS((tm, KV_LORA), lambda i: (i, 0)), BS((tm, ROPE), lambda i: (i, 0))],
        out_shape=[SDS((t, wd), F32), SDS((t, KV_LORA), BF16), SDS((t, ROPE), BF16)],
        compiler_params=_cparams(1),
    )(hk, w_down, latent_norm, c64, s64, p64)


def _kv_up(ckv, w_up):
    t, kl = ckv.shape
    hb = w_up.shape[-1]
    tm = _tile(t)

    def body(x_ref, w_ref, o_ref):
        xv = x_ref[...]
        for h in range(HEADS_B):
            o_ref[:, h * hb : (h + 1) * hb] = _dot(xv, w_ref[h]).astype(BF16)

    return pl.pallas_call(
        body, name="kv_up", grid=(t // tm,),
        in_specs=[BS((tm, kl), lambda i: (i, 0)), BS((HEADS_B, kl, hb), lambda i: (0, 0, 0))],
        out_specs=BS((tm, HEADS_B * hb), lambda i: (i, 0)), out_shape=SDS((t, HEADS_B * hb), BF16),
        compiler_params=_cparams(1),
    )(ckv, w_up)


def _mla_diagonal_mask(tq):
    rows = lax.broadcasted_iota(jnp.int32, (tq, tq), 0)
    cols = lax.broadcasted_iota(jnp.int32, (tq, tq), 1)
    return jnp.where(jnp.right_shift(cols, 6) <= jnp.right_shift(rows, 6), 0.0, NEG_INF)


def _mla_key_tiles(n_keys, tk):
    return [(slice(k0, min(k0 + tk, n_keys)), min(k0 + tk, n_keys) == n_keys) for k0 in range(0, n_keys, tk)]


def _mla_scores(qi, kt, diagonal):
    s = _dot_nt(qi, kt) * (QK_B ** -0.5)
    if diagonal is None:
        return s
    tq, width = s.shape
    own = s[:, width - tq :] + diagonal
    return own if width == tq else jnp.concatenate([s[:, : width - tq], own], axis=1)


def _mla_fwd(q, kv, kr, bl, seq):
    t = kv.shape[0]
    tq = min(MLA_TQ, seq)

    def body(q_ref, kn_ref, v_ref, kr_ref, o_ref, lse_ref):
        kcat = jnp.concatenate([kn_ref[...], kr_ref[...]], axis=1)
        vv = v_ref[...]
        diagonal = _mla_diagonal_mask(tq)
        for i in range(seq // tq):
            rows = slice(i * tq, (i + 1) * tq)
            qi = q_ref[rows, :]
            m = total = acc = None
            for keys, own in _mla_key_tiles((i + 1) * tq, MLA_TK_FWD):
                s = _mla_scores(qi, kcat[keys], diagonal if own else None)
                m_blk = jnp.max(s, axis=-1, keepdims=True)
                if m is None:
                    m_new = m_blk
                    e = jnp.exp(s - m_new)
                    total = jnp.sum(e, axis=-1, keepdims=True)
                    acc = _dot(e.astype(BF16), vv[keys])
                else:
                    m_new = jnp.maximum(m, m_blk)
                    keep = jnp.exp(m - m_new)
                    e = jnp.exp(s - m_new)
                    total = keep * total + jnp.sum(e, axis=-1, keepdims=True)
                    acc = keep * acc + _dot(e.astype(BF16), vv[keys])
                m = m_new
            o_ref[rows, :] = (acc / total).astype(BF16)
            lse_ref[rows, :] = m + jnp.log(total)

    return pl.pallas_call(
        body, name="mla_fwd", grid=(bl, HEADS_B),
        in_specs=[
            BS((None, seq, QK_B), lambda b, h: (h, b, 0)),
            BS((seq, NOPE), lambda b, h: (b, 2 * h)),
            BS((seq, V_DIM), lambda b, h: (b, 2 * h + 1)),
            BS((seq, ROPE), lambda b, h: (b, 0)),
        ],
        out_specs=[BS((seq, V_DIM), lambda b, h: (b, h)), BS((None, seq, 1), lambda b, h: (h, b, 0))],
        out_shape=[SDS((t, HEADS_B * V_DIM), BF16), SDS((HEADS_B, t, 1), F32)],
        compiler_params=_cparams(2),
    )(q, kv, kv, kr)


def _mla_bwd(q, kv, kr, o, lse, do, c192, s192, p192, bl, seq):
    t = kv.shape[0]
    tq = min(MLA_TQ, seq)

    def body(q_ref, kn_ref, v_ref, kr_ref, o_ref, lse_ref, do_ref, c_ref, s_ref, p_ref, dq_ref, dkv_ref, dkr_ref, dkacc, dvacc):
        h = pl.program_id(1)
        kcat = jnp.concatenate([kn_ref[...], kr_ref[...]], axis=1)
        vv = v_ref[...]
        dkacc[...] = jnp.zeros_like(dkacc)
        dvacc[...] = jnp.zeros_like(dvacc)
        diagonal = _mla_diagonal_mask(tq)
        for i in range(seq // tq):
            rows = slice(i * tq, (i + 1) * tq)
            qi = q_ref[rows, :]
            doi = do_ref[rows, :]
            lse_i = lse_ref[rows, :]
            delta = jnp.sum(doi.astype(F32) * o_ref[rows, :].astype(F32), axis=-1, keepdims=True)
            dq = None
            for keys, own in _mla_key_tiles((i + 1) * tq, MLA_TK_BWD):
                p = jnp.exp(_mla_scores(qi, kcat[keys], diagonal if own else None) - lse_i)
                ds = p * (_dot_nt(doi, vv[keys]) - delta)
                dsb = (ds * (QK_B ** -0.5)).astype(BF16)
                dq_blk = _dot(dsb, kcat[keys])
                dq = dq_blk if dq is None else dq + dq_blk
                dkacc[keys, :] += _dot_tn(dsb, qi)
                dvacc[keys, :] += _dot_tn(p.astype(BF16), doi)
            dq_ref[rows, :] = _rope_bwd(dq, c_ref[rows, :], s_ref[rows, :], p_ref[...]).astype(BF16)
        dk = dkacc[...]
        dkv_ref[:, :NOPE] = dk[:, :NOPE].astype(BF16)
        dkv_ref[:, NOPE:] = dvacc[...].astype(BF16)

        @pl.when(h == 0)
        def _():
            dkr_ref[...] = dk[:, NOPE:]

        @pl.when(h > 0)
        def _():
            dkr_ref[...] += dk[:, NOPE:]

    return pl.pallas_call(
        body, name="mla_bwd", grid=(bl, HEADS_B),
        in_specs=[
            BS((None, seq, QK_B), lambda b, h: (h, b, 0)),
            BS((seq, NOPE), lambda b, h: (b, 2 * h)),
            BS((seq, V_DIM), lambda b, h: (b, 2 * h + 1)),
            BS((seq, ROPE), lambda b, h: (b, 0)),
            BS((seq, V_DIM), lambda b, h: (b, h)),
            BS((None, seq, 1), lambda b, h: (h, b, 0)),
            BS((seq, V_DIM), lambda b, h: (b, h)),
            BS((seq, QK_B), lambda b, h: (0, 0)),
            BS((seq, QK_B), lambda b, h: (0, 0)),
            BS((QK_B, QK_B), lambda b, h: (0, 0)),
        ],
        out_specs=[
            BS((None, seq, QK_B), lambda b, h: (h, b, 0)),
            BS((seq, NOPE + V_DIM), lambda b, h: (b, h)),
            BS((seq, ROPE), lambda b, h: (b, 0)),
        ],
        out_shape=[SDS((HEADS_B, t, QK_B), BF16), SDS((t, HEADS_B * (NOPE + V_DIM)), BF16), SDS((t, ROPE), F32)],
        scratch_shapes=[pltpu.VMEM((seq, QK_B), F32), pltpu.VMEM((seq, V_DIM), F32)],
        compiler_params=_cparams(2),
    )(q, kv, kv, kr, o, lse, do, c192, s192, p192)


def _loss_final(h, target, gamma):
    t, dn = h.shape
    tm = _tile(t)
    nt = t // tm

    def body(h_ref, t_ref, g_ref, dh_ref, dhb_ref, dg_ref, loss_ref):
        i = pl.program_id(0)
        hv = h_ref[...]
        r = _rms_scale(hv)
        hh = hv * r
        gam = g_ref[...]
        err = hh * gam - t_ref[...]
        part = 0.5 * jnp.sum(jnp.mean(err * err, axis=-1, keepdims=True))

        @pl.when(i == 0)
        def _():
            loss_ref[...] = jnp.zeros_like(loss_ref)

        loss_ref[...] += part
        dy = err * (1.0 / dn)
        _acc_rows(dg_ref, dy * hh, i, nt)
        t1 = dy * gam
        dh = r * (t1 - hh * jnp.mean(t1 * hh, axis=-1, keepdims=True))
        dh_ref[...] = dh
        dhb_ref[...] = dh.astype(BF16)

    row = BS((tm, dn), lambda i: (i, 0))
    return pl.pallas_call(
        body, name="loss_final", grid=(nt,),
        in_specs=[row, row, BS((1, dn), lambda i: (0, 0))],
        out_specs=[row, row, BS((8, dn), lambda i: (0, 0)), BS((8, 128), lambda i: (0, 0))],
        out_shape=[SDS((t, dn), F32), SDS((t, dn), BF16), SDS((8, dn), F32), SDS((8, 128), F32)],
        compiler_params=_cparams(1),
    )(h, target, gamma)


def _ffn_bwd_in(name, dh, w_out, layer, gu):
    t, dn = dh.shape
    tm = _tile(t, 1024)

    def body(dh_ref, w_ref, gu_ref, o_ref):
        da = 0.5 * _dot_nt(dh_ref[...], w_ref[...])
        g = gu_ref[0].astype(F32)
        u = gu_ref[1].astype(F32)
        sg = jax.nn.sigmoid(g)
        o_ref[0] = (da * u * (sg * (1.0 + g * (1.0 - sg)))).astype(BF16)
        o_ref[1] = (da * (g * sg)).astype(BF16)

    blk = BS((None, 2, tm, FB), lambda j, i: (j, 0, i, 0))
    return pl.pallas_call(
        body, name=name, grid=(NJ, t // tm),
        in_specs=[BS((tm, dn), lambda j, i: (i, 0)), BS((None, None, FB, dn), lambda j, i: (layer, j, 0, 0)), blk],
        out_specs=blk, out_shape=SDS((NJ, 2, t, FB), BF16),
        compiler_params=_cparams(2),
    )(dh, w_out, gu)


def _mm_nt_plain(name, xf, w):
    t, dn = xf.shape
    n = w.shape[0]
    tm = _tile(t)

    def body(x_ref, w_ref, o_ref):
        o_ref[...] = _dot_nt(x_ref[...], w_ref[...]).astype(BF16)

    return pl.pallas_call(
        body, name=name, grid=(t // tm,),
        in_specs=[BS((tm, dn), lambda i: (i, 0)), BS((n, dn), lambda i: (0, 0))],
        out_specs=BS((tm, n), lambda i: (i, 0)), out_shape=SDS((t, n), BF16),
        compiler_params=_cparams(1),
    )(xf, w)


def _mm_tn(name, xa, x_spec, ya, y_spec, out_shape, out_spec, nj, scale=None):
    def body(x_ref, y_ref, o_ref):
        acc = _dot_tn(x_ref[...], y_ref[...])
        o_ref[...] = (acc if scale is None else scale * acc).astype(BF16)

    return pl.pallas_call(
        body, name=name, grid=(nj,),
        in_specs=[x_spec, y_spec], out_specs=out_spec, out_shape=SDS(out_shape, BF16),
        compiler_params=_cparams(1),
    )(xa, ya)


def _dw_qkv(hn, dqkv3, wb):
    t, dn = hn.shape
    per = wb // 128

    def body(x_ref, *refs):
        cols = [y_ref[...] for y_ref in refs[:per]]
        refs[per][...] = _dot_tn(x_ref[...], jnp.concatenate(cols, axis=1)).astype(BF16)

    def piece(k):
        return BS((None, t, 128), lambda j: ((per * j + k) // 8, 0, (per * j + k) % 8))

    return pl.pallas_call(
        body, name="dw_qkv", grid=(NDEV,),
        in_specs=[BS((t, dn), lambda j: (0, 0))] + [piece(k) for k in range(per)],
        out_specs=BS((None, dn, wb), lambda j: (j, 0, 0)), out_shape=SDS((NDEV, dn, wb), BF16),
        compiler_params=_cparams(1),
    )(hn, *([dqkv3] * per))


def _mm_nt_epi(name, ya, y_spec, wa, w_spec, nj, n_out, extra, out_shapes, out_specs, epilogue, tm, nt, mm_fn=None):
    n_extra = len(extra)
    n_outs = len(out_shapes)

    def body(*refs):
        y_ref, w_ref = refs[:2]
        ex = refs[2 : 2 + n_extra]
        outs = refs[2 + n_extra : 2 + n_extra + n_outs]
        i = pl.program_id(0)
        j = pl.program_id(1)
        part = _dot_nt(y_ref[...], w_ref[...]) if mm_fn is None else mm_fn(y_ref, w_ref)
        if nj == 1:
            epilogue(part, ex, outs, i, nt)
            return
        acc = refs[-1]

        @pl.when(j == 0)
        def _():
            acc[...] = part

        @pl.when(j > 0)
        def _():
            acc[...] += part

        @pl.when(j == nj - 1)
        def _():
            epilogue(acc[...], ex, outs, i, nt)

    return pl.pallas_call(
        body, name=name, grid=(nt, nj),
        in_specs=[y_spec, w_spec] + [spec for _, spec in extra],
        out_specs=out_specs, out_shape=out_shapes,
        scratch_shapes=[] if nj == 1 else [pltpu.VMEM((tm, n_out), F32)],
        compiler_params=_cparams(2),
    )(ya, wa, *[arr for arr, _ in extra])


def _norm_bwd(dn, hv, gam):
    r = _rms_scale(hv)
    hh = hv * r
    t1 = dn * gam
    return r * (t1 - hh * jnp.mean(t1 * hh, axis=-1, keepdims=True)), dn * hh


def _norm_bwd_epilogue(has_res, out_dtype):
    def epilogue(dn, ex, outs, i, nt):
        dh, dg_rows = _norm_bwd(dn, ex[0][...], ex[1][...])
        _acc_rows(outs[1], dg_rows, i, nt)
        if has_res:
            dh = dh + ex[2][...]
        outs[0][...] = dh.astype(out_dtype)
        if has_res:
            outs[2][...] = dh.astype(BF16)

    return epilogue


def _mm_nt_norm_bwd(name, ya, y_spec, wa, w_spec, nj, h, gamma, res, out_dtype, mm_fn=None, want_tm=512, after=None):
    t, n = h.shape
    tm = _tile(t, want_tm)
    nt = t // tm
    row = BS((tm, n), lambda i, j: (i, 0))
    extra = [(h, row), (gamma, BS((1, n), lambda i, j: (0, 0)))]
    out_shapes = [SDS((t, n), out_dtype), SDS((8, n), F32)]
    out_specs = [row, BS((8, n), lambda i, j: (0, 0))]
    if res is not None:
        extra.append((res, row))
        out_shapes.append(SDS((t, n), BF16))
        out_specs.append(row)
    extra.extend((a, BS(memory_space=pl.ANY)) for a in after or ())
    return _mm_nt_epi(
        name, ya, y_spec, wa, w_spec, nj, n, extra, out_shapes, out_specs, _norm_bwd_epilogue(res is not None, out_dtype), tm, nt, mm_fn,
    )


def _dev_block(jj):
    return jj // 2 + NJ * (jj % 2)


def _ffn_dn_mm(y_ref, w_ref):
    acc = None
    for jj in range(2 * NJ):
        part = _dot_nt(y_ref[jj], w_ref[_dev_block(jj)])
        acc = part if acc is None else acc + part
    return acc


def _ffn_bwd(tag, dh, dhb, n_in, h_in, gamma, gu, a, w_in, w_out, collective_id, after):
    t, dn = dh.shape
    dgu = _ffn_bwd_in(f"{tag}_bwd_in", dhb, w_out, 0, gu).reshape(2 * NJ, t, FB)
    dw_out = _mm_tn(
        f"{tag}_dw_out", a, BS((None, t, FB), lambda j: (j, 0, 0)), dhb, BS((t, dn), lambda j: (0, 0)),
        (NJ, FB, dn), BS((None, FB, dn), lambda j: (j, 0, 0)), NJ, scale=0.5,
    )
    dw_in = _mm_tn(
        f"{tag}_dw_in", dgu, BS((None, t, FB), lambda j: (j, 0, 0)), n_in, BS((t, dn), lambda j: (0, 0)),
        (NDEV, FB, dn), BS((None, FB, dn), lambda j: (_dev_block(j), 0, 0)), NDEV,
    )
    entries = [("scatter", dw_in), ("scatter", dw_out.reshape(NDEV, NJ * FB // NDEV, dn))]
    landed = _exchange_sc(f"{tag}_reduce", entries, collective_id, after)
    tm = _tile(t)
    resident = BS((None, NDEV, dn, FB), lambda i, j: (0, 0, 0, 0), pipeline_mode=pl.Buffered(1))
    dh_in, dgam, dhb_in = _mm_nt_norm_bwd(
        f"{tag}_dn", dgu, BS((2 * NJ, tm, FB), lambda i, j: (0, i, 0)), w_in, resident, 1, h_in, gamma, dh, F32, mm_fn=_ffn_dn_mm,
        after=[e[1] for e in entries],
    )
    return dh_in, dhb_in, dgam, landed


def _heads_mm(y_ref, w_ref):
    acc = None
    for h in range(HEADS_B):
        part = _dot_nt(y_ref[h], w_ref[h])
        acc = part if acc is None else acc + part
    return acc


def _dqkv_mm(per):
    def mm(y_ref, w_ref):
        acc = None
        for j in range(NDEV):
            cols = [y_ref[(per * j + k) // 8, :, ((per * j + k) % 8) * 128 : ((per * j + k) % 8 + 1) * 128] for k in range(per)]
            part = _dot_nt(jnp.concatenate(cols, axis=1), w_ref[j])
            acc = part if acc is None else acc + part
        return acc

    return mm


def _kv_latent_bwd(dkv, w_up, ckr, latent_norm, dkr, c64, s64, p64, seq):
    t, wd = ckr.shape
    hb = w_up.shape[-1]
    tm = _tile(min(seq, 512), min(seq, 512))
    nt = t // tm
    nseq = seq // tm

    def epilogue(dn, ex, outs, i, nt_):
        dlat, dg_rows = _norm_bwd(dn, ex[0][...], ex[1][...])
        _acc_rows(outs[1], dg_rows, i, nt_)
        outs[0][:, :KV_LORA] = dlat.astype(BF16)
        outs[0][:, KV_LORA:] = _rope_bwd(ex[2][...], ex[3][...], ex[4][...], ex[5][...]).astype(BF16)

    pos = BS((tm, ROPE), lambda i, j: (i % nseq, 0))
    extra = [
        (ckr, BS((tm, KV_LORA), lambda i, j: (i, 0))), (latent_norm, BS((1, KV_LORA), lambda i, j: (0, 0))),
        (dkr, BS((tm, ROPE), lambda i, j: (i, 0))), (c64, pos), (s64, pos), (p64, BS((ROPE, ROPE), lambda i, j: (0, 0))),
    ]
    def heads_mm(y_ref, w_ref):
        acc = None
        for h in range(HEADS_B):
            part = _dot_nt(y_ref[:, h * hb : (h + 1) * hb], w_ref[h])
            acc = part if acc is None else acc + part
        return acc

    return _mm_nt_epi(
        "kv_latent_bwd", dkv, BS((tm, HEADS_B * hb), lambda i, j: (i, 0)), w_up, BS((HEADS_B, KV_LORA, hb), lambda i, j: (0, 0, 0)),
        1, KV_LORA, extra, [SDS((t, wd), BF16), SDS((8, KV_LORA), F32)],
        [BS((tm, wd), lambda i, j: (i, 0)), BS((8, KV_LORA), lambda i, j: (0, 0))], epilogue, tm, nt, heads_mm,
    )


def _adamw_step(g, w, m, v):
    nm = ADAM_B1 * m + (1.0 - ADAM_B1) * g
    nv = ADAM_B2 * v + (1.0 - ADAM_B2) * (g * g)
    m_hat = nm / (1.0 - ADAM_B1 ** ADAM_STEP)
    v_hat = nv / (1.0 - ADAM_B2 ** ADAM_STEP)
    return -ADAM_LR * (m_hat / (jnp.sqrt(v_hat) + ADAM_EPS) + ADAM_WD * w), nm, nv


def _adamw(name, parts, w, m, v):
    n_layers, rows, cols = w.shape
    tr = max(d for d in range(8, min(rows, 256) + 1, 8) if rows % d == 0)
    nb = rows // tr

    def body(*refs):
        p_refs = refs[:n_layers]
        w_ref, m_ref, v_ref, g_ref, d_ref, nm_ref, nv_ref = refs[n_layers : n_layers + 7]
        layer = pl.program_id(0)
        for lp in range(n_layers):

            @pl.when(layer == lp)
            def _():
                g = p_refs[lp][0].astype(F32)
                for k in range(1, NDEV):
                    g = g + p_refs[lp][k].astype(F32)
                g_ref[...] = g

        d_ref[...], nm_ref[...], nv_ref[...] = _adamw_step(g_ref[...], w_ref[...], m_ref[...], v_ref[...])

    def part_spec(lp):
        return BS((NDEV, tr, cols), lambda l, i: (0, jnp.where(l == lp, i, jnp.where(l < lp, 0, nb - 1)), 0))

    row = BS((None, tr, cols), lambda l, i: (l, i, 0))
    return pl.pallas_call(
        body, name=name, grid=(n_layers, nb),
        in_specs=[part_spec(lp) for lp in range(n_layers)] + [row, row, row],
        out_specs=[row] * 4, out_shape=[SDS(w.shape, F32)] * 4,
        compiler_params=_cparams(2),
    )(*parts, w, m, v)


def _pack_small(ffn1_norm, mix_norm, ffn2_norm, kv_norm, final_norm, q_norm, latent_norm, rel_bias, last_row):
    dn = ffn1_norm.shape[-1]

    def rows_of(a, n_rows):
        flat = a.reshape(-1)
        return jnp.pad(flat, (0, n_rows * dn - flat.shape[0])).reshape(n_rows, dn)

    return jnp.concatenate(
        [
            ffn1_norm.reshape(2, dn), mix_norm.reshape(2, dn), ffn2_norm.reshape(2, dn), kv_norm.reshape(1, dn),
            final_norm.reshape(1, dn), rows_of(q_norm, 1), rows_of(latent_norm, 1), rows_of(rel_bias, 5), rows_of(last_row, 1),
        ],
        axis=0,
    )


SMALL_PIECES = (
    ("ffn1_norm", 0, 2, None), ("mix_norm", 2, 2, None), ("ffn2_norm", 4, 2, None), ("kv_norm", 6, 1, None), ("final_norm", 7, 1, None),
    ("b_q_norm", 8, 1, Q_LORA), ("kv_latent_norm", 9, 1, KV_LORA), ("a_rel_bias", 10, 5, None), ("last", 15, 1, None),
)


def _adamw_small(parts, w, m, v):
    dn = w.shape[1]

    def body(p_ref, w_ref, m_ref, v_ref, *outs):
        g = p_ref[0]
        for k in range(1, NDEV):
            g = g + p_ref[k]
        for kind, val in enumerate((g,) + _adamw_step(g, w_ref[...], m_ref[...], v_ref[...])):
            for k, (_, r0, nr, width) in enumerate(SMALL_PIECES):
                outs[kind * len(SMALL_PIECES) + k][...] = val[r0 : r0 + nr, : width or dn]

    shapes = [SDS((nr, width or dn), F32) for _, _, nr, width in SMALL_PIECES] * 4
    outs = pl.pallas_call(
        body, name="adamw_small", grid=(1,),
        in_specs=[BS(parts.shape, lambda i: (0, 0, 0))] + [BS(w.shape, lambda i: (0, 0))] * 3,
        out_specs=[BS(s.shape, lambda i: (0, 0)) for s in shapes], out_shape=shapes,
        compiler_params=_cparams(1),
    )(parts, w, m, v)
    n = len(SMALL_PIECES)
    return [{name: outs[kind * n + k] for k, (name, _, _, _) in enumerate(SMALL_PIECES)} for kind in range(4)]


def kernel(x, ffn1_norm, ffn1_w_in, ffn1_w_out, mix_norm, ffn2_norm, ffn2_w_in, ffn2_w_out, a_w_qkv, a_rel_bias, a_w_o, kv_norm, kv_w_down, kv_latent_norm, kv_w_up, b_w_dq, b_q_norm, b_w_uq, b_w_o, final_norm, loss_target, m_ffn1_norm, m_ffn1_w_in, m_ffn1_w_out, m_mix_norm, m_ffn2_norm, m_ffn2_w_in, m_ffn2_w_out, m_a_w_qkv, m_a_rel_bias, m_a_w_o, m_kv_norm, m_kv_w_down, m_kv_latent_norm, m_kv_w_up, m_b_w_dq, m_b_q_norm, m_b_w_uq, m_b_w_o, m_final_norm, v_ffn1_norm, v_ffn1_w_in, v_ffn1_w_out, v_mix_norm, v_ffn2_norm, v_ffn2_w_in, v_ffn2_w_out, v_a_w_qkv, v_a_rel_bias, v_a_w_o, v_kv_norm, v_kv_w_down, v_kv_latent_norm, v_kv_w_up, v_b_w_dq, v_b_q_norm, v_b_w_uq, v_b_w_o, v_final_norm):
    bl, seq, dn = x.shape
    t = bl * seq
    tm = _tile(t)
    nt = t // tm
    x2 = x.reshape(t, dn)
    target2 = loss_target.reshape(t, dn)

    def gathered(*ws):
        return [("gather", w.astype(BF16)) for w in ws]

    groups = [
        gathered(ffn1_w_in[0]), gathered(ffn1_w_out[0]), gathered(a_w_qkv[0], a_w_o[0]), gathered(ffn2_w_in[0], ffn2_w_out[0]),
        gathered(kv_w_down, kv_w_up), gathered(ffn1_w_in[1], ffn1_w_out[1]), gathered(b_w_dq[0], b_w_uq[0], b_w_o[0]),
        gathered(ffn2_w_in[1], ffn2_w_out[1]),
    ]
    ag = [_exchange_sc(f"gather_{k}", group, GATHER_IDS[k]) for k, group in enumerate(groups)]

    def as_w_in(w):
        return w.reshape(1, NDEV, dn, FB)

    def as_w_out(w):
        return w.reshape(1, NJ, FB, dn)

    c64, s64, p64, c192, s192, p192 = _rope_tables(seq)
    q_norm = b_q_norm.reshape(1, Q_LORA)
    latent_norm = kv_latent_norm.reshape(1, KV_LORA)
    bias = _window_bias(_rel_bias_fwd(a_rel_bias[0][:, 1:]))

    h0, h1, h2, n1, hn, n2, gu1, gu2, a1, a2, w_in1, w_in2, w_out1, w_out2 = ([None, None] for _ in range(14))
    h0[0] = x2
    (n1[0],) = _norm_fwd("norm_x", x2, ffn1_norm[0:1])
    w_in1[0] = as_w_in(ag[0][0])
    gu1[0], a1[0] = _ffn_in("ffn1_in_0", n1[0], w_in1[0], 0)
    w_out1[0] = as_w_out(ag[1][0])
    h1[0], hn[0] = _mm_res_norm("ffn1_out_0", a1[0], w_out1[0], 0, h0[0], mix_norm[0:1], 0.5)
    w_qkv, w_o_a = ag[2]
    qkv_wb = w_qkv.shape[-1]
    w_o_a = w_o_a.reshape(1, 1, dn, dn)
    qkv3 = _qkv_proj("qkv_proj", hn[0], w_qkv)
    o_a, lse_a = _attn_a_fwd(qkv3, bias, bl, seq)
    h2[0], n2[0] = _mm_res_norm("attn_a_out", o_a.reshape(1, t, dn), w_o_a, 0, h1[0], ffn2_norm[0:1], 1.0)
    w_in2[0], w_out2[0] = as_w_in(ag[3][0]), as_w_out(ag[3][1])
    gu2[0], a2[0] = _ffn_in("ffn2_in_0", n2[0], w_in2[0], 0)
    h0[1], hk, n1[1] = _mm_res_norm(
        "ffn2_out_0", a2[0], w_out2[0], 0, h2[0], jnp.concatenate([kv_norm.reshape(1, dn), ffn1_norm[1:2]], axis=0), 0.5
    )
    w_down, w_up = ag[4]
    w_down = w_down.reshape(dn, KV_LORA + ROPE)
    ckr, ckv, kr = _kv_down(hk, w_down, latent_norm, c64, s64, p64, seq)
    kv = _kv_up(ckv, w_up)
    w_in1[1], w_out1[1] = as_w_in(ag[5][0]), as_w_out(ag[5][1])
    gu1[1], a1[1] = _ffn_in("ffn1_in_1", n1[1], w_in1[1], 0)
    h1[1], hn[1] = _mm_res_norm("ffn1_out_1", a1[1], w_out1[1], 0, h0[1], mix_norm[1:2], 0.5)
    w_dq, w_uq, w_o_b = ag[6]
    w_dq = w_dq.reshape(dn, Q_LORA)
    w_o_b = w_o_b.reshape(1, 1, dn, dn)
    cq_pre, cq = _q_down(hn[1], w_dq, q_norm)
    q = _q_up(cq, w_uq, c192, s192, p192, seq)
    o_b, lse_b = _mla_fwd(q, kv, kr, bl, seq)
    h2[1], n2[1] = _mm_res_norm("attn_b_out", o_b.reshape(1, t, dn), w_o_b, 0, h1[1], ffn2_norm[1:2], 1.0)
    w_in2[1], w_out2[1] = as_w_in(ag[7][0]), as_w_out(ag[7][1])
    gu2[1], a2[1] = _ffn_in("ffn2_in_1", n2[1], w_in2[1], 0)
    (h_last,) = _mm_res_norm("ffn2_out_1", a2[1], w_out2[1], 0, h2[1], None, 0.5)
    dh, dhb, dg_final, loss_part = _loss_final(h_last, target2, final_norm.reshape(1, dn))

    dg_ffn1, dg_mix, dg_ffn2, rs_ffn1, rs_ffn2 = ([None, None] for _ in range(5))

    def whole(rows, cols):
        return BS((rows, cols), lambda j: (0, 0))

    def dw_rows(name, xa, ya):
        n = ya.shape[1]
        return _mm_tn(name, xa, whole(t, dn), ya, whole(t, n), (dn, n), whole(dn, n), 1).reshape(NDEV, dn // NDEV, n)

    dh, dhb, dg_ffn2[1], rs_ffn2[1] = _ffn_bwd(
        "ffn2_1", dh, dhb, n2[1], h2[1], ffn2_norm[1:2], gu2[1], a2[1], w_in2[1], w_out2[1], REDUCE_IDS[0], ()
    )
    do_b = _mm_nt_plain("attn_b_do", dhb, w_o_b.reshape(dn, dn))
    dw_o_b = dw_rows("attn_b_dwo", o_b, dhb)
    dq_pre, dkv, dkr = _mla_bwd(q, kv, kr, o_b, lse_b, do_b, c192, s192, p192, bl, seq)
    dw_uq = _mm_tn(
        "dw_uq", cq, whole(t, Q_LORA), dq_pre, BS((None, t, QK_B), lambda j: (j, 0, 0)),
        (HEADS_B, Q_LORA, QK_B), BS((None, Q_LORA, QK_B), lambda j: (j, 0, 0)), HEADS_B,
    )
    dcq_pre, dg_q = _mm_nt_norm_bwd(
        "dcq", dq_pre, BS((HEADS_B, tm, QK_B), lambda i, j: (0, i, 0)), w_uq, BS((HEADS_B, Q_LORA, QK_B), lambda i, j: (0, 0, 0)),
        1, cq_pre, q_norm, None, BF16, mm_fn=_heads_mm,
    )
    dw_dq = dw_rows("dw_dq", hn[1], dcq_pre)
    dh, dg_mix[1], dhb = _mm_nt_norm_bwd(
        "dhn_b", dcq_pre, BS((tm, Q_LORA), lambda i, j: (i, 0)), w_dq, BS((dn, Q_LORA), lambda i, j: (0, 0)),
        1, h1[1], mix_norm[1:2], dh, F32,
    )
    dh, dhb, dg_ffn1[1], rs_ffn1[1] = _ffn_bwd(
        "ffn1_1", dh, dhb, n1[1], h0[1], ffn1_norm[1:2], gu1[1], a1[1], w_in1[1], w_out1[1], REDUCE_IDS[1], rs_ffn2[1][:1]
    )
    dw_up = _mm_tn(
        "dw_up", ckv, whole(t, KV_LORA), dkv, BS((t, NOPE + V_DIM), lambda j: (0, j)),
        (HEADS_B, KV_LORA, NOPE + V_DIM), BS((None, KV_LORA, NOPE + V_DIM), lambda j: (j, 0, 0)), HEADS_B,
    )
    dckr, dg_latent = _kv_latent_bwd(dkv, w_up, ckr, latent_norm, dkr, c64, s64, p64, seq)
    dw_down = dw_rows("dw_down", hk, dckr)
    dh, dg_kv, dhb = _mm_nt_norm_bwd(
        "dhk", dckr, BS((tm, KV_LORA + ROPE), lambda i, j: (i, 0)), w_down, BS((dn, KV_LORA + ROPE), lambda i, j: (0, 0)),
        1, h0[1], kv_norm.reshape(1, dn), dh, F32,
    )
    dh, dhb, dg_ffn2[0], rs_ffn2[0] = _ffn_bwd(
        "ffn2_0", dh, dhb, n2[0], h2[0], ffn2_norm[0:1], gu2[0], a2[0], w_in2[0], w_out2[0], REDUCE_IDS[2], rs_ffn1[1][:1]
    )
    do_a = _mm_nt_plain("attn_a_do", dhb, w_o_a.reshape(dn, dn))
    dw_o_a = dw_rows("attn_a_dwo", o_a, dhb)
    dqkv3, dbias = _attn_a_bwd(qkv3, o_a, lse_a, do_a, bias, bl, seq)
    dw_qkv = _dw_qkv(hn[0], dqkv3, qkv_wb)
    mixer_grads = [dw_o_a, dw_qkv, dw_o_b, dw_uq, dw_dq, dw_up, dw_down]
    dh, dg_mix[0], dhb = _mm_nt_norm_bwd(
        "dhn_a", dqkv3, BS((3, tm, dn), lambda i, j: (0, i, 0)), w_qkv, BS((NDEV, dn, qkv_wb), lambda i, j: (0, 0, 0)),
        1, h1[0], mix_norm[0:1], dh, F32, mm_fn=_dqkv_mm(qkv_wb // 128), after=mixer_grads,
    )
    rs_mixers = _exchange_sc("mixers_reduce", [("scatter", g) for g in mixer_grads], REDUCE_IDS[3], rs_ffn2[0][:1])
    dh, dhb, dg_ffn1[0], rs_ffn1[0] = _ffn_bwd(
        "ffn1_0", dh, dhb, n1[0], h0[0], ffn1_norm[0:1], gu1[0], a1[0], w_in1[0], w_out1[0], REDUCE_IDS[4], rs_mixers[:1]
    )
    grad_x = dh.reshape(bl, seq, dn)
    dtable = jnp.pad(_rel_bias_bwd(_window_bias_bwd(dbias)), ((0, 0), (1, 0)))

    def update(name, parts, w, m, v):
        shape3 = (len(parts),) + w.shape[-2:]
        parts = [p.reshape((NDEV,) + shape3[1:]) for p in parts]
        outs = _adamw(name, parts, w.reshape(shape3), m.reshape(shape3), v.reshape(shape3))
        return [o.reshape(w.shape) for o in outs]

    res = {}
    r_in2_1, r_out2_1 = rs_ffn2[1]
    r_in1_1, r_out1_1 = rs_ffn1[1]
    r_in2_0, r_out2_0 = rs_ffn2[0]
    r_in1_0, r_out1_0 = rs_ffn1[0]
    r_o_a, r_qkv, r_o_b, r_uq, r_dq, r_up, r_down = rs_mixers
    def update_transposed(name, parts, w, m, v):
        outs = update(name, parts, *[jnp.swapaxes(a, 1, 2) for a in (w, m, v)])
        return [jnp.swapaxes(o, 1, 2) for o in outs]

    res["ffn2_w_in"] = update_transposed("adamw_ffn2_w_in", [r_in2_0, r_in2_1], ffn2_w_in, m_ffn2_w_in, v_ffn2_w_in)
    res["ffn2_w_out"] = update("adamw_ffn2_w_out", [r_out2_0, r_out2_1], ffn2_w_out, m_ffn2_w_out, v_ffn2_w_out)
    res["kv_w_down"] = update("adamw_kv_w_down", [r_down], kv_w_down, m_kv_w_down, v_kv_w_down)
    res["kv_w_up"] = update("adamw_kv_w_up", [r_up], kv_w_up, m_kv_w_up, v_kv_w_up)
    res["b_w_dq"] = update("adamw_b_w_dq", [r_dq], b_w_dq, m_b_w_dq, v_b_w_dq)
    res["b_w_uq"] = update("adamw_b_w_uq", [r_uq], b_w_uq, m_b_w_uq, v_b_w_uq)
    res["b_w_o"] = update("adamw_b_w_o", [r_o_b], b_w_o, m_b_w_o, v_b_w_o)
    res["a_w_qkv"] = update("adamw_a_w_qkv", [r_qkv], a_w_qkv, m_a_w_qkv, v_a_w_qkv)
    res["a_w_o"] = update("adamw_a_w_o", [r_o_a], a_w_o, m_a_w_o, v_a_w_o)

    small = _pack_small(
        jnp.stack([dg_ffn1[0][0], dg_ffn1[1][0]]), jnp.stack([dg_mix[0][0], dg_mix[1][0]]), jnp.stack([dg_ffn2[0][0], dg_ffn2[1][0]]),
        dg_kv[0], dg_final[0], dg_q[0], dg_latent[0], dtable, loss_part[0],
    )
    done = [r[1] for name, r in res.items() if name != "ffn2_w_in"]
    (r_small,) = _exchange("gather_small_grads", [("gather", small)], after=done)
    res["ffn1_w_in"] = update_transposed("adamw_ffn1_w_in", [r_in1_0, r_in1_1], ffn1_w_in, m_ffn1_w_in, v_ffn1_w_in)
    res["ffn1_w_out"] = update("adamw_ffn1_w_out", [r_out1_0, r_out1_1], ffn1_w_out, m_ffn1_w_out, v_ffn1_w_out)
    zero_row = jnp.zeros((dn,), F32)
    packs = [
        _pack_small(f1, mx, f2, kvn, fin, qn, lat, rel, zero_row)
        for f1, mx, f2, kvn, fin, qn, lat, rel in (
            (ffn1_norm, mix_norm, ffn2_norm, kv_norm, final_norm, b_q_norm, kv_latent_norm, a_rel_bias),
            (m_ffn1_norm, m_mix_norm, m_ffn2_norm, m_kv_norm, m_final_norm, m_b_q_norm, m_kv_latent_norm, m_a_rel_bias),
            (v_ffn1_norm, v_mix_norm, v_ffn2_norm, v_kv_norm, v_final_norm, v_b_q_norm, v_kv_latent_norm, v_a_rel_bias),
        )
    ]
    small_out = _adamw_small(r_small, *packs)
    for name in ("ffn1_norm", "mix_norm", "ffn2_norm", "b_q_norm"):
        res[name] = [so[name] for so in small_out]
    for name in ("kv_norm", "kv_latent_norm", "final_norm"):
        res[name] = [so[name].reshape(-1) for so in small_out]
    res["a_rel_bias"] = [so["a_rel_bias"].reshape(-1)[: HEADS_A * NREL].reshape(1, HEADS_A, NREL) for so in small_out]
    loss = small_out[0]["last"][0, 0]

    order = [
        "ffn1_norm", "ffn1_w_in", "ffn1_w_out", "mix_norm", "ffn2_norm", "ffn2_w_in", "ffn2_w_out", "a_w_qkv", "a_rel_bias",
        "a_w_o", "kv_norm", "kv_w_down", "kv_latent_norm", "kv_w_up", "b_w_dq", "b_q_norm", "b_w_uq", "b_w_o", "final_norm",
    ]
    return (loss, grad_x, *[res[n][0] for n in order], *[res[n][1] for n in order], *[res[n][2] for n in order], *[res[n][3] for n in order])
```

```python
import jax
import jax.numpy as jnp
import numpy as np
from jax import lax
from jax.experimental import pallas as pl
from jax.experimental.pallas import tpu as pltpu
from jax.experimental.pallas import tpu_sc as plsc

NDEV = 8
D_MODEL = 1024
D_FF = 2816
FB = 2 * D_FF // NDEV
NJ = D_FF // FB
CHUNK = 64
LEFT_CHUNKS = 8
PAD = LEFT_CHUNKS * CHUNK
BAND = PAD + CHUNK
CHUNKS_PER_STEP = 4
WINDOW = PAD + CHUNKS_PER_STEP * CHUNK
STEP_ROWS = CHUNKS_PER_STEP * 2 * CHUNK
MAX_REL = 128
NREL = 2 * MAX_REL + 1
NREL_USED = 256
HEADS_A = 16
HEADS_B = 8
NOPE = 128
ROPE = 64
QK_B = NOPE + ROPE
V_DIM = 128
Q_LORA = 768
KV_LORA = 256
ROPE_THETA = 10000.0
EPS = 1e-6
NEG_INF = -1e30
MLA_TQ = 256
MLA_TK_FWD = 256
MLA_TK_BWD = 1024
ADAM_LR = 0.001
ADAM_B1 = 0.9
ADAM_B2 = 0.999
ADAM_EPS = 1e-08
ADAM_WD = 0.01
ADAM_STEP = 10
PACK_ROWS = 16
GATHER_IDS = tuple(range(1, 9))
REDUCE_IDS = tuple(range(9, 14))
VMEM_LIMIT_BYTES = 56 * 1024 * 1024

F32 = jnp.float32
BF16 = jnp.bfloat16
SDS = jax.ShapeDtypeStruct
BS = pl.BlockSpec
MESH = pl.DeviceIdType.MESH


def _cparams(n_axes):
    return pltpu.CompilerParams(dimension_semantics=("arbitrary",) * n_axes, vmem_limit_bytes=VMEM_LIMIT_BYTES)


def _tile(t, want=512):
    return want if t % want == 0 else t


def _dot(a, b):
    return jnp.dot(a, b, preferred_element_type=F32)


def _dot_nt(a, b):
    return lax.dot_general(a, b, (((1,), (1,)), ((), ())), preferred_element_type=F32)


def _dot_tn(a, b):
    return lax.dot_general(a, b, (((0,), (0,)), ((), ())), preferred_element_type=F32)


def _split3(a):
    hi = a.astype(BF16)
    rest = a - hi.astype(F32)
    mid = rest.astype(BF16)
    return hi, mid, (rest - mid.astype(F32)).astype(BF16)


def _dot_exact(a, onehot, transposed=False):
    ob = onehot.astype(BF16)
    dot = _dot_nt if transposed else _dot
    hi, mid, lo = _split3(a)
    return dot(hi, ob) + dot(mid, ob) + dot(lo, ob)


def _rms_scale(h):
    return lax.rsqrt(jnp.mean(h * h, axis=-1, keepdims=True) + EPS)


def _acc_rows(ref, val, step, n_steps):
    part = val.reshape(val.shape[0] // 8, 8, val.shape[1]).sum(axis=0)

    @pl.when(step == 0)
    def _():
        ref[...] = part

    @pl.when(step > 0)
    def _():
        ref[...] += part

    @pl.when(step == n_steps - 1)
    def _():
        ref[...] = jnp.broadcast_to(jnp.sum(ref[...], axis=0, keepdims=True), ref.shape)


def _exchange_plan(entries):
    ins = [e[1] for e in entries]
    kinds = [e[0] for e in entries]
    lands = [SDS((NDEV,) + a.shape if k == "gather" else a.shape, a.dtype) for k, a in zip(kinds, ins)]
    return ins, lands, kinds


def _mesh_place():
    x, y, c = lax.axis_index("x"), lax.axis_index("y"), lax.axis_index("c")
    return (x, y, c), 4 * x + 2 * y + c


def _flipped(place, p):
    x, y, c = place
    px = 1 - x if p & 4 else x
    py = 1 - y if p & 2 else y
    pc = 1 - c if p & 1 else c
    return (px, py, pc), 4 * px + 2 * py + pc


def _ends(kind, src_ref, land_ref, origin, target):
    if kind == "gather":
        return src_ref, land_ref.at[origin]
    return src_ref.at[target], land_ref.at[origin]


def _remote(kind, src_ref, land_ref, send_sems, recv_sems, k, p, place, me, arriving):
    peer_pos, peer = _flipped(place, p)
    src, dst = _ends(kind, src_ref, land_ref, me, peer)
    if arriving:
        dst = _ends(kind, src_ref, land_ref, peer, me)[1]
    sem = k * (NDEV - 1) + p - 1
    return pltpu.make_async_remote_copy(
        src_ref=src, dst_ref=dst, send_sem=send_sems.at[sem], recv_sem=recv_sems.at[sem], device_id=peer_pos, device_id_type=MESH,
    )


def _exchange(name, entries, after=()):
    ins, lands, kinds = _exchange_plan(entries)
    n = len(ins)
    after = tuple(after)

    def body(*refs):
        refs = refs[:n] + refs[n + len(after) :]
        in_refs, land_refs = refs[:n], refs[n : 2 * n]
        send_sems, recv_sems, local_sems = refs[2 * n :]
        place, me = _mesh_place()
        local = []
        for k in range(n):
            src, dst = _ends(kinds[k], in_refs[k], land_refs[k], me, me)
            local.append(pltpu.make_async_copy(src, dst, local_sems.at[k]))
            local[-1].start()
        sends = []
        for p in range(1, NDEV):
            for k in range(n):
                sends.append(_remote(kinds[k], in_refs[k], land_refs[k], send_sems, recv_sems, k, p, place, me, False))
                sends[-1].start()
        for p in range(1, NDEV):
            for k in range(n):
                _remote(kinds[k], in_refs[k], land_refs[k], send_sems, recv_sems, k, p, place, me, True).wait_recv()
        for cp in sends:
            cp.wait_send()
        for cp in local:
            cp.wait()

    any_spec = BS(memory_space=pl.ANY)
    return pl.pallas_call(
        body, name=name, out_shape=lands, in_specs=[any_spec] * (n + len(after)), out_specs=[any_spec] * n,
        scratch_shapes=[
            pltpu.SemaphoreType.DMA((n * (NDEV - 1),)), pltpu.SemaphoreType.DMA((n * (NDEV - 1),)), pltpu.SemaphoreType.DMA((n,)),
        ],
    )(*ins, *after)


def _exchange_sc(name, entries, collective_id, after=()):
    ins, lands, kinds = _exchange_plan(entries)
    n = len(ins)
    after = tuple(after)

    def launch(*refs):
        refs = refs[:n] + refs[n + len(after) :]
        in_refs, land_refs = refs[:n], refs[n : 2 * n]
        send_sems, recv_sems, local_sems = refs[2 * n :]
        place, me = _mesh_place()
        barrier = pltpu.get_barrier_semaphore()
        for p in range(1, NDEV):
            pl.semaphore_signal(barrier, inc=1, device_id=_flipped(place, p)[0], device_id_type=MESH)
        pl.semaphore_wait(barrier, NDEV - 1)
        local = []
        for k in range(n):
            src, dst = _ends(kinds[k], in_refs[k], land_refs[k], me, me)
            local.append(pltpu.make_async_copy(src, dst, local_sems.at[k]))
            local[-1].start()
        sends = []
        if all(kind == "gather" for kind in kinds):
            for p in (1, 2, 4, 6):
                for k in range(n):
                    sends.append(_remote(kinds[k], in_refs[k], land_refs[k], send_sems, recv_sems, k, p, place, me, False))
                    sends[-1].start()
            sibling_pos, _ = _flipped(place, 1)
            for f in (2, 4, 6):
                _, origin = _flipped(place, f)
                for k in range(n):
                    _remote(kinds[k], in_refs[k], land_refs[k], send_sems, recv_sems, k, f, place, me, True).wait_recv()
                    sem = k * (NDEV - 1) + f
                    sends.append(
                        pltpu.make_async_remote_copy(
                            src_ref=land_refs[k].at[origin], dst_ref=land_refs[k].at[origin], send_sem=send_sems.at[sem],
                            recv_sem=recv_sems.at[sem], device_id=sibling_pos, device_id_type=MESH,
                        )
                    )
                    sends[-1].start()
            for p in (1, 3, 5, 7):
                for k in range(n):
                    _remote(kinds[k], in_refs[k], land_refs[k], send_sems, recv_sems, k, p, place, me, True).wait_recv()
        else:
            for p in range(1, NDEV):
                for k in range(n):
                    sends.append(_remote(kinds[k], in_refs[k], land_refs[k], send_sems, recv_sems, k, p, place, me, False))
                    sends[-1].start()
            for p in range(1, NDEV):
                for k in range(n):
                    _remote(kinds[k], in_refs[k], land_refs[k], send_sems, recv_sems, k, p, place, me, True).wait_recv()
        for cp in sends:
            cp.wait_send()
        for cp in local:
            cp.wait()

    return pl.kernel(
        launch, out_type=tuple(lands), mesh=plsc.ScalarSubcoreMesh(axis_name="sequencer", num_cores=1), name=name,
        scratch_types=(
            pltpu.SemaphoreType.DMA((n * (NDEV - 1),)), pltpu.SemaphoreType.DMA((n * (NDEV - 1),)), pltpu.SemaphoreType.DMA((n,)),
        ),
        compiler_params=pltpu.CompilerParams(collective_id=collective_id),
    )(*ins, *after)


def _norm_fwd(name, h, gammas):
    t, dn = h.shape
    ng = gammas.shape[0]
    tm = _tile(t)

    def body(h_ref, g_ref, *outs):
        hv = h_ref[...]
        hh = hv * _rms_scale(hv)
        for i, o_ref in enumerate(outs):
            o_ref[...] = (hh * g_ref[i : i + 1, :]).astype(BF16)

    row = BS((tm, dn), lambda i: (i, 0))
    return pl.pallas_call(
        body, name=name, grid=(t // tm,),
        in_specs=[row, BS((ng, dn), lambda i: (0, 0))],
        out_specs=[row] * ng, out_shape=[SDS((t, dn), BF16)] * ng,
        compiler_params=_cparams(1),
    )(h, gammas)


def _ffn_in(name, n, w_in, layer):
    t, dn = n.shape
    tm = _tile(t, 1024)

    def body(n_ref, wg_ref, wu_ref, gu_ref, a_ref):
        xv = n_ref[...]
        g = _dot(xv, wg_ref[...])
        u = _dot(xv, wu_ref[...])
        gu_ref[0] = g.astype(BF16)
        gu_ref[1] = u.astype(BF16)
        a_ref[...] = (g * jax.nn.sigmoid(g) * u).astype(BF16)

    return pl.pallas_call(
        body, name=name, grid=(NJ, t // tm),
        in_specs=[
            BS((tm, dn), lambda j, i: (i, 0)),
            BS((None, None, dn, FB), lambda j, i: (layer, j, 0, 0)),
            BS((None, None, dn, FB), lambda j, i: (layer, j + NJ, 0, 0)),
        ],
        out_specs=[BS((None, 2, tm, FB), lambda j, i: (j, 0, i, 0)), BS((None, tm, FB), lambda j, i: (j, i, 0))],
        out_shape=[SDS((NJ, 2, t, FB), BF16), SDS((NJ, t, FB), BF16)],
        compiler_params=_cparams(2),
    )(n, w_in, w_in)


def _mm_res_norm(name, a, w, layer, h_in, gammas, scale):
    nk, t, kb = a.shape
    dn = w.shape[-1]
    ng = 0 if gammas is None else gammas.shape[0]
    tm = _tile(t)

    def body(*refs):
        a_ref, w_ref, h_ref = refs[:3]
        g_ref = refs[3] if ng else None
        outs = refs[3 + (1 if ng else 0) :]
        acc = _dot(a_ref[0], w_ref[0])
        for k in range(1, nk):
            acc += _dot(a_ref[k], w_ref[k])
        ho = h_ref[...] + scale * acc
        outs[0][...] = ho
        if ng:
            hh = ho * _rms_scale(ho)
            for i in range(ng):
                outs[1 + i][...] = (hh * g_ref[i : i + 1, :]).astype(BF16)

    row = BS((tm, dn), lambda i: (i, 0))
    in_specs = [BS((nk, tm, kb), lambda i: (0, i, 0)), BS((None, nk, kb, dn), lambda i: (layer, 0, 0, 0)), row]
    args = [a, w, h_in]
    if ng:
        in_specs.append(BS((ng, dn), lambda i: (0, 0)))
        args.append(gammas)
    return pl.pallas_call(
        body, name=name, grid=(t // tm,),
        in_specs=in_specs,
        out_specs=[row] * (1 + ng), out_shape=[SDS((t, dn), F32)] + [SDS((t, dn), BF16)] * ng,
        compiler_params=_cparams(1),
    )(*args)


def _qkv_proj(name, hn, w_qkv):
    t, dn = hn.shape
    wb = w_qkv.shape[-1]
    per = wb // 128
    tm = _tile(t)

    def body(x_ref, w_ref, o_ref):
        xv = x_ref[...]
        for j in range(NDEV):
            yv = _dot(xv, w_ref[j]).astype(BF16)
            for i in range(per):
                n = per * j + i
                o_ref[n // 8, :, (n % 8) * 128 : (n % 8 + 1) * 128] = yv[:, i * 128 : (i + 1) * 128]

    return pl.pallas_call(
        body, name=name, grid=(t // tm,),
        in_specs=[BS((tm, dn), lambda i: (i, 0)), BS((NDEV, dn, wb), lambda i: (0, 0, 0))],
        out_specs=BS((3, tm, dn), lambda i: (0, i, 0)), out_shape=SDS((3, t, dn), BF16),
        compiler_params=_cparams(1),
    )(hn, w_qkv)


def _rel_onehot(i):
    r = lax.broadcasted_iota(jnp.int32, (NREL_USED, BAND), 0)
    j = lax.broadcasted_iota(jnp.int32, (NREL_USED, BAND), 1)
    idx = jnp.clip(PAD + i - j, -MAX_REL, MAX_REL) + MAX_REL
    return (idx - 1 == r).astype(BF16)


def _rel_bias_fwd(table):
    def body(t_ref, o_ref):
        i8 = pl.program_id(0)
        for ii in range(8):
            o_ref[:, ii, :] = _dot_exact(t_ref[...], _rel_onehot(i8 * 8 + ii))

    return pl.pallas_call(
        body, name="rel_bias_fwd", grid=(CHUNK // 8,),
        in_specs=[BS((HEADS_A, NREL_USED), lambda i: (0, 0))],
        out_specs=BS((HEADS_A, 8, BAND), lambda i: (0, i, 0)), out_shape=SDS((HEADS_A, CHUNK, BAND), F32),
        compiler_params=_cparams(1),
    )(table)


def _rel_bias_bwd(dbias):
    def body(d_ref, o_ref):
        i8 = pl.program_id(0)
        acc = jnp.zeros((HEADS_A, NREL_USED), F32)
        for ii in range(8):
            acc += _dot_exact(d_ref[:, ii, :], _rel_onehot(i8 * 8 + ii), transposed=True)

        @pl.when(i8 == 0)
        def _():
            o_ref[...] = acc

        @pl.when(i8 > 0)
        def _():
            o_ref[...] += acc

    return pl.pallas_call(
        body, name="rel_bias_bwd", grid=(CHUNK // 8,),
        in_specs=[BS((HEADS_A, 8, BAND), lambda i: (0, i, 0))],
        out_specs=BS((HEADS_A, NREL_USED), lambda i: (0, 0)), out_shape=SDS((HEADS_A, NREL_USED), F32),
        compiler_params=_cparams(1),
    )(dbias)


def _window_bias(bias):
    b = bias.reshape(HEADS_A // 2, 2, CHUNK, BAND)
    per_chunk = [
        jnp.pad(b, ((0, 0), (0, 0), (0, 0), (cc * CHUNK, WINDOW - BAND - cc * CHUNK)), constant_values=NEG_INF)
        for cc in range(CHUNKS_PER_STEP)
    ]
    return jnp.stack(per_chunk, axis=1).reshape(HEADS_A // 2, STEP_ROWS, WINDOW)


def _window_bias_bwd(dwin):
    d = dwin.reshape(HEADS_A // 2, CHUNKS_PER_STEP, 2, CHUNK, WINDOW)
    return sum(d[:, cc, :, :, cc * CHUNK : cc * CHUNK + BAND] for cc in range(CHUNKS_PER_STEP)).reshape(HEADS_A, CHUNK, BAND)


def _step_rows(xs, lane):
    parts = []
    for cc in range(CHUNKS_PER_STEP):
        xc = xs[cc * CHUNK : (cc + 1) * CHUNK]
        parts.append(jnp.where(lane < 64, xc, jnp.zeros_like(xc)))
        parts.append(jnp.where(lane >= 64, xc, jnp.zeros_like(xc)))
    return jnp.concatenate(parts, axis=0)


def _pair_rows(ys, lane):
    parts = []
    for cc in range(CHUNKS_PER_STEP):
        y0 = ys[(2 * cc) * CHUNK : (2 * cc + 1) * CHUNK]
        y1 = ys[(2 * cc + 1) * CHUNK : (2 * cc + 2) * CHUNK]
        parts.append(jnp.where(lane < 64, y0, y1))
    return jnp.concatenate(parts, axis=0)


def _window_scores(q_rows, kwin, bias_win, first_key):
    s = _dot_nt(q_rows, kwin) * (CHUNK ** -0.5) + bias_win
    if first_key is None:
        return s
    col = lax.broadcasted_iota(jnp.int32, s.shape, 1)
    return jnp.where(col >= first_key, s, NEG_INF)


def _window_loop(n_passes, chunks):
    n_padded = min(PAD // (CHUNKS_PER_STEP * CHUNK), n_passes)
    lax.fori_loop(0, n_padded, lambda it, carry: chunks(it, carry, True), 0, unroll=2)
    if n_passes > n_padded:
        lax.fori_loop(n_padded, n_passes, lambda it, carry: chunks(it, carry, False), 0, unroll=2)


def _attn_a_fwd(qkv3, bias_win, bl, seq):
    t, dn = qkv3.shape[1:]
    npair = dn // 128
    step = CHUNKS_PER_STEP * CHUNK

    def body(q_ref, k_ref, v_ref, b_ref, o_ref, lse_ref, kpad, vpad):
        kpad[0:PAD, :] = jnp.zeros((PAD, 128), BF16)
        vpad[0:PAD, :] = jnp.zeros((PAD, 128), BF16)
        kpad[PAD:, :] = k_ref[...]
        vpad[PAD:, :] = v_ref[...]
        lane = lax.broadcasted_iota(jnp.int32, (CHUNK, 128), 1)

        def chunks(it, carry, padded):
            r0 = pl.multiple_of(it * step, step)
            q_rows = _step_rows(q_ref[pl.ds(r0, step), :], lane)
            s = _window_scores(q_rows, kpad[pl.ds(r0, WINDOW), :], b_ref[...], PAD - r0 if padded else None)
            m = jnp.max(s, axis=-1, keepdims=True)
            e = jnp.exp(s - m)
            total = jnp.sum(e, axis=-1, keepdims=True)
            o_rows = _dot(e.astype(BF16), vpad[pl.ds(r0, WINDOW), :]) * (1.0 / total)
            o_ref[pl.ds(r0, step), :] = _pair_rows(o_rows, lane).astype(BF16)
            lse_ref[pl.ds(pl.multiple_of(it * STEP_ROWS, STEP_ROWS), STEP_ROWS), :] = m + jnp.log(total)
            return carry

        _window_loop(seq // step, chunks)

    return pl.pallas_call(
        body, name="attn_a_fwd", grid=(bl, npair),
        in_specs=[
            BS((None, seq, 128), lambda b, h: (0, b, h)),
            BS((None, seq, 128), lambda b, h: (1, b, h)),
            BS((None, seq, 128), lambda b, h: (2, b, h)),
            BS((None, STEP_ROWS, WINDOW), lambda b, h: (h, 0, 0)),
        ],
        out_specs=[BS((seq, 128), lambda b, h: (b, h)), BS((None, 2 * seq, 1), lambda b, h: (h, b, 0))],
        out_shape=[SDS((t, dn), BF16), SDS((npair, 2 * t, 1), F32)],
        scratch_shapes=[pltpu.VMEM((PAD + seq, 128), BF16), pltpu.VMEM((PAD + seq, 128), BF16)],
        compiler_params=_cparams(2),
    )(qkv3, qkv3, qkv3, bias_win)


def _attn_a_bwd(qkv3, out, lse, do, bias_win, bl, seq):
    t, dn = qkv3.shape[1:]
    npair = dn // 128
    step = CHUNKS_PER_STEP * CHUNK

    def body(q_ref, k_ref, v_ref, o_ref, lse_ref, do_ref, b_ref, dqkv_ref, db_ref, kpad, vpad, dkacc, dvacc):
        b = pl.program_id(1)
        kpad[0:PAD, :] = jnp.zeros((PAD, 128), BF16)
        vpad[0:PAD, :] = jnp.zeros((PAD, 128), BF16)
        kpad[PAD:, :] = k_ref[...]
        vpad[PAD:, :] = v_ref[...]
        dkacc[...] = jnp.zeros_like(dkacc)
        dvacc[...] = jnp.zeros_like(dvacc)

        @pl.when(b == 0)
        def _():
            db_ref[...] = jnp.zeros_like(db_ref)

        lane = lax.broadcasted_iota(jnp.int32, (CHUNK, 128), 1)

        def chunks(it, carry, padded):
            r0 = pl.multiple_of(it * step, step)
            q_rows = _step_rows(q_ref[pl.ds(r0, step), :], lane)
            do_rows = _step_rows(do_ref[pl.ds(r0, step), :], lane)
            kwin = kpad[pl.ds(r0, WINDOW), :]
            vwin = vpad[pl.ds(r0, WINDOW), :]
            o_rows = _step_rows(o_ref[pl.ds(r0, step), :], lane)
            delta = jnp.sum(do_rows.astype(F32) * o_rows.astype(F32), axis=-1, keepdims=True)
            lse_rows = lse_ref[pl.ds(pl.multiple_of(it * STEP_ROWS, STEP_ROWS), STEP_ROWS), :]
            p = jnp.exp(_window_scores(q_rows, kwin, b_ref[...], PAD - r0 if padded else None) - lse_rows)
            ds = p * (_dot_nt(do_rows, vwin) - delta)
            db_ref[...] += ds
            dsb = (ds * (CHUNK ** -0.5)).astype(BF16)
            dqkv_ref[0, pl.ds(r0, step), :] = _pair_rows(_dot(dsb, kwin), lane).astype(BF16)
            dkacc[pl.ds(r0, WINDOW), :] += _dot_tn(dsb, q_rows)
            dvacc[pl.ds(r0, WINDOW), :] += _dot_tn(p.astype(BF16), do_rows)
            return carry

        _window_loop(seq // step, chunks)
        dqkv_ref[1] = dkacc[PAD:, :].astype(BF16)
        dqkv_ref[2] = dvacc[PAD:, :].astype(BF16)

    return pl.pallas_call(
        body, name="attn_a_bwd", grid=(npair, bl),
        in_specs=[
            BS((None, seq, 128), lambda h, b: (0, b, h)),
            BS((None, seq, 128), lambda h, b: (1, b, h)),
            BS((None, seq, 128), lambda h, b: (2, b, h)),
            BS((seq, 128), lambda h, b: (b, h)),
            BS((None, 2 * seq, 1), lambda h, b: (h, b, 0)),
            BS((seq, 128), lambda h, b: (b, h)),
            BS((None, STEP_ROWS, WINDOW), lambda h, b: (h, 0, 0)),
        ],
        out_specs=[BS((3, seq, 128), lambda h, b: (0, b, h)), BS((None, STEP_ROWS, WINDOW), lambda h, b: (h, 0, 0))],
        out_shape=[SDS((3, t, dn), BF16), SDS((HEADS_A // 2, STEP_ROWS, WINDOW), F32)],
        scratch_shapes=[
            pltpu.VMEM((PAD + seq, 128), BF16), pltpu.VMEM((PAD + seq, 128), BF16),
            pltpu.VMEM((PAD + seq, 128), F32), pltpu.VMEM((PAD + seq, 128), F32),
        ],
        compiler_params=_cparams(2),
    )(qkv3, qkv3, qkv3, out, lse, do, bias_win)


def _rope_tables(seq):
    half = ROPE // 2
    freqs = ROPE_THETA ** (-jnp.arange(half, dtype=F32) / half)
    ang = jnp.arange(seq, dtype=F32)[:, None] * freqs[None, :]
    cos, sin = jnp.cos(ang), jnp.sin(ang)
    c64 = jnp.concatenate([cos, cos], axis=1)
    s64 = jnp.concatenate([-sin, sin], axis=1)
    c192 = jnp.concatenate([jnp.ones((seq, NOPE), F32), c64], axis=1)
    s192 = jnp.concatenate([jnp.zeros((seq, NOPE), F32), s64], axis=1)
    p64 = np.zeros((ROPE, ROPE), np.float32)
    for col in range(ROPE):
        p64[(col + half) % ROPE, col] = 1.0
    p192 = np.zeros((QK_B, QK_B), np.float32)
    p192[NOPE:, NOPE:] = p64
    return c64, s64, jnp.asarray(p64), c192, s192, jnp.asarray(p192)


def _rope(xv, cos, sin_signed, swap):
    return xv * cos + _dot_exact(xv, swap) * sin_signed


def _rope_bwd(dy, cos, sin_signed, swap):
    return dy * cos + _dot_exact(dy * sin_signed, swap)


def _q_down(hn, w_dq, q_norm):
    t, dn = hn.shape
    ql = w_dq.shape[1]
    tm = _tile(t)

    def body(x_ref, w_ref, g_ref, pre_ref, cq_ref):
        pre = _dot(x_ref[...], w_ref[...])
        pre_ref[...] = pre
        cq_ref[...] = (pre * _rms_scale(pre) * g_ref[...]).astype(BF16)

    return pl.pallas_call(
        body, name="q_down", grid=(t // tm,),
        in_specs=[BS((tm, dn), lambda i: (i, 0)), BS((dn, ql), lambda i: (0, 0)), BS((1, ql), lambda i: (0, 0))],
        out_specs=[BS((tm, ql), lambda i: (i, 0))] * 2, out_shape=[SDS((t, ql), F32), SDS((t, ql), BF16)],
        compiler_params=_cparams(1),
    )(hn, w_dq, q_norm)


def _q_up(cq, w_uq, c192, s192, p192, seq):
    t, ql = cq.shape
    tm = _tile(min(seq, 512), min(seq, 512))
    nseq = seq // tm

    def body(x_ref, w_ref, c_ref, s_ref, p_ref, o_ref):
        xv = x_ref[...]
        for h in range(HEADS_B):
            o_ref[h] = _rope(_dot(xv, w_ref[h]), c_ref[...], s_ref[...], p_ref[...]).astype(BF16)

    pos = BS((tm, QK_B), lambda i: (i % nseq, 0))
    return pl.pallas_call(
        body, name="q_up", grid=(t // tm,),
        in_specs=[
            BS((tm, ql), lambda i: (i, 0)), BS((HEADS_B, ql, QK_B), lambda i: (0, 0, 0)), pos, pos,
            BS((QK_B, QK_B), lambda i: (0, 0)),
        ],
        out_specs=BS((HEADS_B, tm, QK_B), lambda i: (0, i, 0)), out_shape=SDS((HEADS_B, t, QK_B), BF16),
        compiler_params=_cparams(1),
    )(cq, w_uq, c192, s192, p192)


def _kv_down(hk, w_down, latent_norm, c64, s64, p64, seq):
    t, dn = hk.shape
    wd = w_down.shape[1]
    tm = _tile(min(seq, 512), min(seq, 512))
    nseq = seq // tm

    def body(x_ref, w_ref, g_ref, c_ref, s_ref, p_ref, ckr_ref, ckv_ref, kr_ref):
        ckr = _dot(x_ref[...], w_ref[...])
        ckr_ref[...] = ckr
        lat = ckr[:, :KV_LORA]
        ckv_ref[...] = (lat * _rms_scale(lat) * g_ref[...]).astype(BF16)
        kr_ref[...] = _rope(ckr[:, KV_LORA:], c_ref[...], s_ref[...], p_ref[...]).astype(BF16)

    pos = BS((tm, ROPE), lambda i: (i % nseq, 0))
    return pl.pallas_call(
        body, name="kv_down", grid=(t // tm,),
        in_specs=[
            BS((tm, dn), lambda i: (i, 0)), BS((dn, wd), lambda i: (0, 0)), BS((1, KV_LORA), lambda i: (0, 0)), pos, pos,
            BS((ROPE, ROPE), lambda i: (0, 0)),
        ],
        out_specs=[BS((tm, wd), lambda i: (i, 0)), BS((tm, KV_LORA), lambda i: (i, 0)), BS((tm, ROPE), lambda i: (i, 0))],
        out_shape=[SDS((t, wd), F32), SDS((t, KV_LORA), BF16), SDS((t, ROPE), BF16)],
        compiler_params=_cparams(1),
    )(hk, w_down, latent_norm, c64, s64, p64)


def _kv_up(ckv, w_up):
    t, kl = ckv.shape
    hb = w_up.shape[-1]
    tm = _tile(t)

    def body(x_ref, w_ref, o_ref):
        xv = x_ref[...]
        for h in range(HEADS_B):
            o_ref[:, h * hb : (h + 1) * hb] = _dot(xv, w_ref[h]).astype(BF16)

    return pl.pallas_call(
        body, name="kv_up", grid=(t // tm,),
        in_specs=[BS((tm, kl), lambda i: (i, 0)), BS((HEADS_B, kl, hb), lambda i: (0, 0, 0))],
        out_specs=BS((tm, HEADS_B * hb), lambda i: (i, 0)), out_shape=SDS((t, HEADS_B * hb), BF16),
        compiler_params=_cparams(1),
    )(ckv, w_up)


def _mla_diagonal_mask(tq):
    rows = lax.broadcasted_iota(jnp.int32, (tq, tq), 0)
    cols = lax.broadcasted_iota(jnp.int32, (tq, tq), 1)
    return jnp.where(jnp.right_shift(cols, 6) <= jnp.right_shift(rows, 6), 0.0, NEG_INF)


def _mla_key_tiles(n_keys, tk):
    return [(slice(k0, min(k0 + tk, n_keys)), min(k0 + tk, n_keys) == n_keys) for k0 in range(0, n_keys, tk)]


def _mla_scores(qi, kt, diagonal):
    s = _dot_nt(qi, kt) * (QK_B ** -0.5)
    if diagonal is None:
        return s
    tq, width = s.shape
    own = s[:, width - tq :] + diagonal
    return own if width == tq else jnp.concatenate([s[:, : width - tq], own], axis=1)


def _mla_fwd(q, kv, kr, bl, seq):
    t = kv.shape[0]
    tq = min(MLA_TQ, seq)

    def body(q_ref, kn_ref, v_ref, kr_ref, o_ref, lse_ref):
        kcat = jnp.concatenate([kn_ref[...], kr_ref[...]], axis=1)
        vv = v_ref[...]
        diagonal = _mla_diagonal_mask(tq)
        for i in range(seq // tq):
            rows = slice(i * tq, (i + 1) * tq)
            qi = q_ref[rows, :]
            m = total = acc = None
            for keys, own in _mla_key_tiles((i + 1) * tq, MLA_TK_FWD):
                s = _mla_scores(qi, kcat[keys], diagonal if own else None)
                m_blk = jnp.max(s, axis=-1, keepdims=True)
                if m is None:
                    m_new = m_blk
                    e = jnp.exp(s - m_new)
                    total = jnp.sum(e, axis=-1, keepdims=True)
                    acc = _dot(e.astype(BF16), vv[keys])
                else:
                    m_new = jnp.maximum(m, m_blk)
                    keep = jnp.exp(m - m_new)
                    e = jnp.exp(s - m_new)
                    total = keep * total + jnp.sum(e, axis=-1, keepdims=True)
                    acc = keep * acc + _dot(e.astype(BF16), vv[keys])
                m = m_new
            o_ref[rows, :] = (acc / total).astype(BF16)
            lse_ref[rows, :] = m + jnp.log(total)

    return pl.pallas_call(
        body, name="mla_fwd", grid=(bl, HEADS_B),
        in_specs=[
            BS((None, seq, QK_B), lambda b, h: (h, b, 0)),
            BS((seq, NOPE), lambda b, h: (b, 2 * h)),
            BS((seq, V_DIM), lambda b, h: (b, 2 * h + 1)),
            BS((seq, ROPE), lambda b, h: (b, 0)),
        ],
        out_specs=[BS((seq, V_DIM), lambda b, h: (b, h)), BS((None, seq, 1), lambda b, h: (h, b, 0))],
        out_shape=[SDS((t, HEADS_B * V_DIM), BF16), SDS((HEADS_B, t, 1), F32)],
        compiler_params=_cparams(2),
    )(q, kv, kv, kr)


def _mla_bwd(q, kv, kr, o, lse, do, c192, s192, p192, bl, seq):
    t = kv.shape[0]
    tq = min(MLA_TQ, seq)

    def body(q_ref, kn_ref, v_ref, kr_ref, o_ref, lse_ref, do_ref, c_ref, s_ref, p_ref, dq_ref, dkv_ref, dkr_ref, dkacc, dvacc):
        h = pl.program_id(1)
        kcat = jnp.concatenate([kn_ref[...], kr_ref[...]], axis=1)
        vv = v_ref[...]
        dkacc[...] = jnp.zeros_like(dkacc)
        dvacc[...] = jnp.zeros_like(dvacc)
        diagonal = _mla_diagonal_mask(tq)
        for i in range(seq // tq):
            rows = slice(i * tq, (i + 1) * tq)
            qi = q_ref[rows, :]
            doi = do_ref[rows, :]
            lse_i = lse_ref[rows, :]
            delta = jnp.sum(doi.astype(F32) * o_ref[rows, :].astype(F32), axis=-1, keepdims=True)
            dq = None
            for keys, own in _mla_key_tiles((i + 1) * tq, MLA_TK_BWD):
                p = jnp.exp(_mla_scores(qi, kcat[keys], diagonal if own else None) - lse_i)
                ds = p * (_dot_nt(doi, vv[keys]) - delta)
                dsb = (ds * (QK_B ** -0.5)).astype(BF16)
                dq_blk = _dot(dsb, kcat[keys])
                dq = dq_blk if dq is None else dq + dq_blk
                dkacc[keys, :] += _dot_tn(dsb, qi)
                dvacc[keys, :] += _dot_tn(p.astype(BF16), doi)
            dq_ref[rows, :] = _rope_bwd(dq, c_ref[rows, :], s_ref[rows, :], p_ref[...]).astype(BF16)
        dk = dkacc[...]
        dkv_ref[:, :NOPE] = dk[:, :NOPE].astype(BF16)
        dkv_ref[:, NOPE:] = dvacc[...].astype(BF16)

        @pl.when(h == 0)
        def _():
            dkr_ref[...] = dk[:, NOPE:]

        @pl.when(h > 0)
        def _():
            dkr_ref[...] += dk[:, NOPE:]

    return pl.pallas_call(
        body, name="mla_bwd", grid=(bl, HEADS_B),
        in_specs=[
            BS((None, seq, QK_B), lambda b, h: (h, b, 0)),
            BS((seq, NOPE), lambda b, h: (b, 2 * h)),
            BS((seq, V_DIM), lambda b, h: (b, 2 * h + 1)),
            BS((seq, ROPE), lambda b, h: (b, 0)),
            BS((seq, V_DIM), lambda b, h: (b, h)),
            BS((None, seq, 1), lambda b, h: (h, b, 0)),
            BS((seq, V_DIM), lambda b, h: (b, h)),
            BS((seq, QK_B), lambda b, h: (0, 0)),
            BS((seq, QK_B), lambda b, h: (0, 0)),
            BS((QK_B, QK_B), lambda b, h: (0, 0)),
        ],
        out_specs=[
            BS((None, seq, QK_B), lambda b, h: (h, b, 0)),
            BS((seq, NOPE + V_DIM), lambda b, h: (b, h)),
            BS((seq, ROPE), lambda b, h: (b, 0)),
        ],
        out_shape=[SDS((HEADS_B, t, QK_B), BF16), SDS((t, HEADS_B * (NOPE + V_DIM)), BF16), SDS((t, ROPE), F32)],
        scratch_shapes=[pltpu.VMEM((seq, QK_B), F32), pltpu.VMEM((seq, V_DIM), F32)],
        compiler_params=_cparams(2),
    )(q, kv, kv, kr, o, lse, do, c192, s192, p192)


def _loss_final(h, target, gamma):
    t, dn = h.shape
    tm = _tile(t)
    nt = t // tm

    def body(h_ref, t_ref, g_ref, dh_ref, dhb_ref, dg_ref, loss_ref):
        i = pl.program_id(0)
        hv = h_ref[...]
        r = _rms_scale(hv)
        hh = hv * r
        gam = g_ref[...]
        err = hh * gam - t_ref[...]
        part = 0.5 * jnp.sum(jnp.mean(err * err, axis=-1, keepdims=True))

        @pl.when(i == 0)
        def _():
            loss_ref[...] = jnp.zeros_like(loss_ref)

        loss_ref[...] += part
        dy = err * (1.0 / dn)
        _acc_rows(dg_ref, dy * hh, i, nt)
        t1 = dy * gam
        dh = r * (t1 - hh * jnp.mean(t1 * hh, axis=-1, keepdims=True))
        dh_ref[...] = dh
        dhb_ref[...] = dh.astype(BF16)

    row = BS((tm, dn), lambda i: (i, 0))
    return pl.pallas_call(
        body, name="loss_final", grid=(nt,),
        in_specs=[row, row, BS((1, dn), lambda i: (0, 0))],
        out_specs=[row, row, BS((8, dn), lambda i: (0, 0)), BS((8, 128), lambda i: (0, 0))],
        out_shape=[SDS((t, dn), F32), SDS((t, dn), BF16), SDS((8, dn), F32), SDS((8, 128), F32)],
        compiler_params=_cparams(1),
    )(h, target, gamma)


def _ffn_bwd_in(name, dh, w_out, layer, gu):
    t, dn = dh.shape
    tm = _tile(t, 1024)

    def body(dh_ref, w_ref, gu_ref, o_ref):
        da = 0.5 * _dot_nt(dh_ref[...], w_ref[...])
        g = gu_ref[0].astype(F32)
        u = gu_ref[1].astype(F32)
        sg = jax.nn.sigmoid(g)
        o_ref[0] = (da * u * (sg * (1.0 + g * (1.0 - sg)))).astype(BF16)
        o_ref[1] = (da * (g * sg)).astype(BF16)

    blk = BS((None, 2, tm, FB), lambda j, i: (j, 0, i, 0))
    return pl.pallas_call(
        body, name=name, grid=(NJ, t // tm),
        in_specs=[BS((tm, dn), lambda j, i: (i, 0)), BS((None, None, FB, dn), lambda j, i: (layer, j, 0, 0)), blk],
        out_specs=blk, out_shape=SDS((NJ, 2, t, FB), BF16),
        compiler_params=_cparams(2),
    )(dh, w_out, gu)


def _mm_nt_plain(name, xf, w):
    t, dn = xf.shape
    n = w.shape[0]
    tm = _tile(t)

    def body(x_ref, w_ref, o_ref):
        o_ref[...] = _dot_nt(x_ref[...], w_ref[...]).astype(BF16)

    return pl.pallas_call(
        body, name=name, grid=(t // tm,),
        in_specs=[BS((tm, dn), lambda i: (i, 0)), BS((n, dn), lambda i: (0, 0))],
        out_specs=BS((tm, n), lambda i: (i, 0)), out_shape=SDS((t, n), BF16),
        compiler_params=_cparams(1),
    )(xf, w)


def _mm_tn(name, xa, x_spec, ya, y_spec, out_shape, out_spec, nj, scale=None):
    def body(x_ref, y_ref, o_ref):
        acc = _dot_tn(x_ref[...], y_ref[...])
        o_ref[...] = (acc if scale is None else scale * acc).astype(BF16)

    return pl.pallas_call(
        body, name=name, grid=(nj,),
        in_specs=[x_spec, y_spec], out_specs=out_spec, out_shape=SDS(out_shape, BF16),
        compiler_params=_cparams(1),
    )(xa, ya)


def _dw_qkv(hn, dqkv3, wb):
    t, dn = hn.shape
    per = wb // 128

    def body(x_ref, *refs):
        cols = [y_ref[...] for y_ref in refs[:per]]
        refs[per][...] = _dot_tn(x_ref[...], jnp.concatenate(cols, axis=1)).astype(BF16)

    def piece(k):
        return BS((None, t, 128), lambda j: ((per * j + k) // 8, 0, (per * j + k) % 8))

    return pl.pallas_call(
        body, name="dw_qkv", grid=(NDEV,),
        in_specs=[BS((t, dn), lambda j: (0, 0))] + [piece(k) for k in range(per)],
        out_specs=BS((None, dn, wb), lambda j: (j, 0, 0)), out_shape=SDS((NDEV, dn, wb), BF16),
        compiler_params=_cparams(1),
    )(hn, *([dqkv3] * per))


def _mm_nt_epi(name, ya, y_spec, wa, w_spec, nj, n_out, extra, out_shapes, out_specs, epilogue, tm, nt, mm_fn=None):
    n_extra = len(extra)
    n_outs = len(out_shapes)

    def body(*refs):
        y_ref, w_ref = refs[:2]
        ex = refs[2 : 2 + n_extra]
        outs = refs[2 + n_extra : 2 + n_extra + n_outs]
        i = pl.program_id(0)
        j = pl.program_id(1)
        part = _dot_nt(y_ref[...], w_ref[...]) if mm_fn is None else mm_fn(y_ref, w_ref)
        if nj == 1:
            epilogue(part, ex, outs, i, nt)
            return
        acc = refs[-1]

        @pl.when(j == 0)
        def _():
            acc[...] = part

        @pl.when(j > 0)
        def _():
            acc[...] += part

        @pl.when(j == nj - 1)
        def _():
            epilogue(acc[...], ex, outs, i, nt)

    return pl.pallas_call(
        body, name=name, grid=(nt, nj),
        in_specs=[y_spec, w_spec] + [spec for _, spec in extra],
        out_specs=out_specs, out_shape=out_shapes,
        scratch_shapes=[] if nj == 1 else [pltpu.VMEM((tm, n_out), F32)],
        compiler_params=_cparams(2),
    )(ya, wa, *[arr for arr, _ in extra])


def _norm_bwd(dn, hv, gam):
    r = _rms_scale(hv)
    hh = hv * r
    t1 = dn * gam
    return r * (t1 - hh * jnp.mean(t1 * hh, axis=-1, keepdims=True)), dn * hh


def _norm_bwd_epilogue(has_res, out_dtype):
    def epilogue(dn, ex, outs, i, nt):
        dh, dg_rows = _norm_bwd(dn, ex[0][...], ex[1][...])
        _acc_rows(outs[1], dg_rows, i, nt)
        if has_res:
            dh = dh + ex[2][...]
        outs[0][...] = dh.astype(out_dtype)
        if has_res:
            outs[2][...] = dh.astype(BF16)

    return epilogue


def _mm_nt_norm_bwd(name, ya, y_spec, wa, w_spec, nj, h, gamma, res, out_dtype, mm_fn=None, want_tm=512, after=None):
    t, n = h.shape
    tm = _tile(t, want_tm)
    nt = t // tm
    row = BS((tm, n), lambda i, j: (i, 0))
    extra = [(h, row), (gamma, BS((1, n), lambda i, j: (0, 0)))]
    out_shapes = [SDS((t, n), out_dtype), SDS((8, n), F32)]
    out_specs = [row, BS((8, n), lambda i, j: (0, 0))]
    if res is not None:
        extra.append((res, row))
        out_shapes.append(SDS((t, n), BF16))
        out_specs.append(row)
    extra.extend((a, BS(memory_space=pl.ANY)) for a in after or ())
    return _mm_nt_epi(
        name, ya, y_spec, wa, w_spec, nj, n, extra, out_shapes, out_specs, _norm_bwd_epilogue(res is not None, out_dtype), tm, nt, mm_fn,
    )


def _dev_block(jj):
    return jj // 2 + NJ * (jj % 2)


def _ffn_dn_mm(y_ref, w_ref):
    acc = None
    for jj in range(2 * NJ):
        part = _dot_nt(y_ref[jj], w_ref[_dev_block(jj)])
        acc = part if acc is None else acc + part
    return acc


def _ffn_bwd(tag, dh, dhb, n_in, h_in, gamma, gu, a, w_in, w_out, collective_id, after):
    t, dn = dh.shape
    dgu = _ffn_bwd_in(f"{tag}_bwd_in", dhb, w_out, 0, gu).reshape(2 * NJ, t, FB)
    dw_out = _mm_tn(
        f"{tag}_dw_out", a, BS((None, t, FB), lambda j: (j, 0, 0)), dhb, BS((t, dn), lambda j: (0, 0)),
        (NJ, FB, dn), BS((None, FB, dn), lambda j: (j, 0, 0)), NJ, scale=0.5,
    )
    dw_in = _mm_tn(
        f"{tag}_dw_in", dgu, BS((None, t, FB), lambda j: (j, 0, 0)), n_in, BS((t, dn), lambda j: (0, 0)),
        (NDEV, FB, dn), BS((None, FB, dn), lambda j: (_dev_block(j), 0, 0)), NDEV,
    )
    entries = [("scatter", dw_in), ("scatter", dw_out.reshape(NDEV, NJ * FB // NDEV, dn))]
    landed = _exchange_sc(f"{tag}_reduce", entries, collective_id, after)
    tm = _tile(t)
    resident = BS((None, NDEV, dn, FB), lambda i, j: (0, 0, 0, 0), pipeline_mode=pl.Buffered(1))
    dh_in, dgam, dhb_in = _mm_nt_norm_bwd(
        f"{tag}_dn", dgu, BS((2 * NJ, tm, FB), lambda i, j: (0, i, 0)), w_in, resident, 1, h_in, gamma, dh, F32, mm_fn=_ffn_dn_mm,
        after=[e[1] for e in entries],
    )
    return dh_in, dhb_in, dgam, landed


def _heads_mm(y_ref, w_ref):
    acc = None
    for h in range(HEADS_B):
        part = _dot_nt(y_ref[h], w_ref[h])
        acc = part if acc is None else acc + part
    return acc


def _dqkv_mm(per):
    def mm(y_ref, w_ref):
        acc = None
        for j in range(NDEV):
            cols = [y_ref[(per * j + k) // 8, :, ((per * j + k) % 8) * 128 : ((per * j + k) % 8 + 1) * 128] for k in range(per)]
            part = _dot_nt(jnp.concatenate(cols, axis=1), w_ref[j])
            acc = part if acc is None else acc + part
        return acc

    return mm


def _kv_latent_bwd(dkv, w_up, ckr, latent_norm, dkr, c64, s64, p64, seq):
    t, wd = ckr.shape
    hb = w_up.shape[-1]
    tm = _tile(min(seq, 512), min(seq, 512))
    nt = t // tm
    nseq = seq // tm

    def epilogue(dn, ex, outs, i, nt_):
        dlat, dg_rows = _norm_bwd(dn, ex[0][...], ex[1][...])
        _acc_rows(outs[1], dg_rows, i, nt_)
        outs[0][:, :KV_LORA] = dlat.astype(BF16)
        outs[0][:, KV_LORA:] = _rope_bwd(ex[2][...], ex[3][...], ex[4][...], ex[5][...]).astype(BF16)

    pos = BS((tm, ROPE), lambda i, j: (i % nseq, 0))
    extra = [
        (ckr, BS((tm, KV_LORA), lambda i, j: (i, 0))), (latent_norm, BS((1, KV_LORA), lambda i, j: (0, 0))),
        (dkr, BS((tm, ROPE), lambda i, j: (i, 0))), (c64, pos), (s64, pos), (p64, BS((ROPE, ROPE), lambda i, j: (0, 0))),
    ]
    def heads_mm(y_ref, w_ref):
        acc = None
        for h in range(HEADS_B):
            part = _dot_nt(y_ref[:, h * hb : (h + 1) * hb], w_ref[h])
            acc = part if acc is None else acc + part
        return acc

    return _mm_nt_epi(
        "kv_latent_bwd", dkv, BS((tm, HEADS_B * hb), lambda i, j: (i, 0)), w_up, BS((HEADS_B, KV_LORA, hb), lambda i, j: (0, 0, 0)),
        1, KV_LORA, extra, [SDS((t, wd), BF16), SDS((8, KV_LORA), F32)],
        [BS((tm, wd), lambda i, j: (i, 0)), BS((8, KV_LORA), lambda i, j: (0, 0))], epilogue, tm, nt, heads_mm,
    )


def _adamw_step(g, w, m, v):
    nm = ADAM_B1 * m + (1.0 - ADAM_B1) * g
    nv = ADAM_B2 * v + (1.0 - ADAM_B2) * (g * g)
    m_hat = nm / (1.0 - ADAM_B1 ** ADAM_STEP)
    v_hat = nv / (1.0 - ADAM_B2 ** ADAM_STEP)
    return -ADAM_LR * (m_hat / (jnp.sqrt(v_hat) + ADAM_EPS) + ADAM_WD * w), nm, nv


def _adamw(name, parts, w, m, v):
    n_layers, rows, cols = w.shape
    tr = max(d for d in range(8, min(rows, 256) + 1, 8) if rows % d == 0)
    nb = rows // tr

    def body(*refs):
        p_refs = refs[:n_layers]
        w_ref, m_ref, v_ref, g_ref, d_ref, nm_ref, nv_ref = refs[n_layers : n_layers + 7]
        layer = pl.program_id(0)
        for lp in range(n_layers):

            @pl.when(layer == lp)
            def _():
                g = p_refs[lp][0].astype(F32)
                for k in range(1, NDEV):
                    g = g + p_refs[lp][k].astype(F32)
                g_ref[...] = g

        d_ref[...], nm_ref[...], nv_ref[...] = _adamw_step(g_ref[...], w_ref[...], m_ref[...], v_ref[...])

    def part_spec(lp):
        return BS((NDEV, tr, cols), lambda l, i: (0, jnp.where(l == lp, i, jnp.where(l < lp, 0, nb - 1)), 0))

    row = BS((None, tr, cols), lambda l, i: (l, i, 0))
    return pl.pallas_call(
        body, name=name, grid=(n_layers, nb),
        in_specs=[part_spec(lp) for lp in range(n_layers)] + [row, row, row],
        out_specs=[row] * 4, out_shape=[SDS(w.shape, F32)] * 4,
        compiler_params=_cparams(2),
    )(*parts, w, m, v)


def _pack_small(ffn1_norm, mix_norm, ffn2_norm, kv_norm, final_norm, q_norm, latent_norm, rel_bias, last_row):
    dn = ffn1_norm.shape[-1]

    def rows_of(a, n_rows):
        flat = a.reshape(-1)
        return jnp.pad(flat, (0, n_rows * dn - flat.shape[0])).reshape(n_rows, dn)

    return jnp.concatenate(
        [
            ffn1_norm.reshape(2, dn), mix_norm.reshape(2, dn), ffn2_norm.reshape(2, dn), kv_norm.reshape(1, dn),
            final_norm.reshape(1, dn), rows_of(q_norm, 1), rows_of(latent_norm, 1), rows_of(rel_bias, 5), rows_of(last_row, 1),
        ],
        axis=0,
    )


SMALL_PIECES = (
    ("ffn1_norm", 0, 2, None), ("mix_norm", 2, 2, None), ("ffn2_norm", 4, 2, None), ("kv_norm", 6, 1, None), ("final_norm", 7, 1, None),
    ("b_q_norm", 8, 1, Q_LORA), ("kv_latent_norm", 9, 1, KV_LORA), ("a_rel_bias", 10, 5, None), ("last", 15, 1, None),
)


def _adamw_small(parts, w, m, v):
    dn = w.shape[1]

    def body(p_ref, w_ref, m_ref, v_ref, *outs):
        g = p_ref[0]
        for k in range(1, NDEV):
            g = g + p_ref[k]
        for kind, val in enumerate((g,) + _adamw_step(g, w_ref[...], m_ref[...], v_ref[...])):
            for k, (_, r0, nr, width) in enumerate(SMALL_PIECES):
                outs[kind * len(SMALL_PIECES) + k][...] = val[r0 : r0 + nr, : width or dn]

    shapes = [SDS((nr, width or dn), F32) for _, _, nr, width in SMALL_PIECES] * 4
    outs = pl.pallas_call(
        body, name="adamw_small", grid=(1,),
        in_specs=[BS(parts.shape, lambda i: (0, 0, 0))] + [BS(w.shape, lambda i: (0, 0))] * 3,
        out_specs=[BS(s.shape, lambda i: (0, 0)) for s in shapes], out_shape=shapes,
        compiler_params=_cparams(1),
    )(parts, w, m, v)
    n = len(SMALL_PIECES)
    return [{name: outs[kind * n + k] for k, (name, _, _, _) in enumerate(SMALL_PIECES)} for kind in range(4)]


def kernel(x, ffn1_norm, ffn1_w_in, ffn1_w_out, mix_norm, ffn2_norm, ffn2_w_in, ffn2_w_out, a_w_qkv, a_rel_bias, a_w_o, kv_norm, kv_w_down, kv_latent_norm, kv_w_up, b_w_dq, b_q_norm, b_w_uq, b_w_o, final_norm, loss_target, m_ffn1_norm, m_ffn1_w_in, m_ffn1_w_out, m_mix_norm, m_ffn2_norm, m_ffn2_w_in, m_ffn2_w_out, m_a_w_qkv, m_a_rel_bias, m_a_w_o, m_kv_norm, m_kv_w_down, m_kv_latent_norm, m_kv_w_up, m_b_w_dq, m_b_q_norm, m_b_w_uq, m_b_w_o, m_final_norm, v_ffn1_norm, v_ffn1_w_in, v_ffn1_w_out, v_mix_norm, v_ffn2_norm, v_ffn2_w_in, v_ffn2_w_out, v_a_w_qkv, v_a_rel_bias, v_a_w_o, v_kv_norm, v_kv_w_down, v_kv_latent_norm, v_kv_w_up, v_b_w_dq, v_b_q_norm, v_b_w_uq, v_b_w_o, v_final_norm):
    bl, seq, dn = x.shape
    t = bl * seq
    tm = _tile(t)
    nt = t // tm
    x2 = x.reshape(t, dn)
    target2 = loss_target.reshape(t, dn)

    def gathered(*ws):
        return [("gather", w.astype(BF16)) for w in ws]

    groups = [
        gathered(ffn1_w_in[0]), gathered(ffn1_w_out[0]), gathered(a_w_qkv[0], a_w_o[0]), gathered(ffn2_w_in[0], ffn2_w_out[0]),
        gathered(kv_w_down, kv_w_up), gathered(ffn1_w_in[1], ffn1_w_out[1]), gathered(b_w_dq[0], b_w_uq[0], b_w_o[0]),
        gathered(ffn2_w_in[1], ffn2_w_out[1]),
    ]
    ag = [_exchange_sc(f"gather_{k}", group, GATHER_IDS[k]) for k, group in enumerate(groups)]

    def as_w_in(w):
        return w.reshape(1, NDEV, dn, FB)

    def as_w_out(w):
        return w.reshape(1, NJ, FB, dn)

    c64, s64, p64, c192, s192, p192 = _rope_tables(seq)
    q_norm = b_q_norm.reshape(1, Q_LORA)
    latent_norm = kv_latent_norm.reshape(1, KV_LORA)
    bias = _window_bias(_rel_bias_fwd(a_rel_bias[0][:, 1:]))

    h0, h1, h2, n1, hn, n2, gu1, gu2, a1, a2, w_in1, w_in2, w_out1, w_out2 = ([None, None] for _ in range(14))
    h0[0] = x2
    (n1[0],) = _norm_fwd("norm_x", x2, ffn1_norm[0:1])
    w_in1[0] = as_w_in(ag[0][0])
    gu1[0], a1[0] = _ffn_in("ffn1_in_0", n1[0], w_in1[0], 0)
    w_out1[0] = as_w_out(ag[1][0])
    h1[0], hn[0] = _mm_res_norm("ffn1_out_0", a1[0], w_out1[0], 0, h0[0], mix_norm[0:1], 0.5)
    w_qkv, w_o_a = ag[2]
    qkv_wb = w_qkv.shape[-1]
    w_o_a = w_o_a.reshape(1, 1, dn, dn)
    qkv3 = _qkv_proj("qkv_proj", hn[0], w_qkv)
    o_a, lse_a = _attn_a_fwd(qkv3, bias, bl, seq)
    h2[0], n2[0] = _mm_res_norm("attn_a_out", o_a.reshape(1, t, dn), w_o_a, 0, h1[0], ffn2_norm[0:1], 1.0)
    w_in2[0], w_out2[0] = as_w_in(ag[3][0]), as_w_out(ag[3][1])
    gu2[0], a2[0] = _ffn_in("ffn2_in_0", n2[0], w_in2[0], 0)
    h0[1], hk, n1[1] = _mm_res_norm(
        "ffn2_out_0", a2[0], w_out2[0], 0, h2[0], jnp.concatenate([kv_norm.reshape(1, dn), ffn1_norm[1:2]], axis=0), 0.5
    )
    w_down, w_up = ag[4]
    w_down = w_down.reshape(dn, KV_LORA + ROPE)
    ckr, ckv, kr = _kv_down(hk, w_down, latent_norm, c64, s64, p64, seq)
    kv = _kv_up(ckv, w_up)
    w_in1[1], w_out1[1] = as_w_in(ag[5][0]), as_w_out(ag[5][1])
    gu1[1], a1[1] = _ffn_in("ffn1_in_1", n1[1], w_in1[1], 0)
    h1[1], hn[1] = _mm_res_norm("ffn1_out_1", a1[1], w_out1[1], 0, h0[1], mix_norm[1:2], 0.5)
    w_dq, w_uq, w_o_b = ag[6]
    w_dq = w_dq.reshape(dn, Q_LORA)
    w_o_b = w_o_b.reshape(1, 1, dn, dn)
    cq_pre, cq = _q_down(hn[1], w_dq, q_norm)
    q = _q_up(cq, w_uq, c192, s192, p192, seq)
    o_b, lse_b = _mla_fwd(q, kv, kr, bl, seq)
    h2[1], n2[1] = _mm_res_norm("attn_b_out", o_b.reshape(1, t, dn), w_o_b, 0, h1[1], ffn2_norm[1:2], 1.0)
    w_in2[1], w_out2[1] = as_w_in(ag[7][0]), as_w_out(ag[7][1])
    gu2[1], a2[1] = _ffn_in("ffn2_in_1", n2[1], w_in2[1], 0)
    (h_last,) = _mm_res_norm("ffn2_out_1", a2[1], w_out2[1], 0, h2[1], None, 0.5)
    dh, dhb, dg_final, loss_part = _loss_final(h_last, target2, final_norm.reshape(1, dn))

    dg_ffn1, dg_mix, dg_ffn2, rs_ffn1, rs_ffn2 = ([None, None] for _ in range(5))

    def whole(rows, cols):
        return BS((rows, cols), lambda j: (0, 0))

    def dw_rows(name, xa, ya):
        n = ya.shape[1]
        return _mm_tn(name, xa, whole(t, dn), ya, whole(t, n), (dn, n), whole(dn, n), 1).reshape(NDEV, dn // NDEV, n)

    dh, dhb, dg_ffn2[1], rs_ffn2[1] = _ffn_bwd(
        "ffn2_1", dh, dhb, n2[1], h2[1], ffn2_norm[1:2], gu2[1], a2[1], w_in2[1], w_out2[1], REDUCE_IDS[0], ()
    )
    do_b = _mm_nt_plain("attn_b_do", dhb, w_o_b.reshape(dn, dn))
    dw_o_b = dw_rows("attn_b_dwo", o_b, dhb)
    dq_pre, dkv, dkr = _mla_bwd(q, kv, kr, o_b, lse_b, do_b, c192, s192, p192, bl, seq)
    dw_uq = _mm_tn(
        "dw_uq", cq, whole(t, Q_LORA), dq_pre, BS((None, t, QK_B), lambda j: (j, 0, 0)),
        (HEADS_B, Q_LORA, QK_B), BS((None, Q_LORA, QK_B), lambda j: (j, 0, 0)), HEADS_B,
    )
    dcq_pre, dg_q = _mm_nt_norm_bwd(
        "dcq", dq_pre, BS((HEADS_B, tm, QK_B), lambda i, j: (0, i, 0)), w_uq, BS((HEADS_B, Q_LORA, QK_B), lambda i, j: (0, 0, 0)),
        1, cq_pre, q_norm, None, BF16, mm_fn=_heads_mm,
    )
    dw_dq = dw_rows("dw_dq", hn[1], dcq_pre)
    dh, dg_mix[1], dhb = _mm_nt_norm_bwd(
        "dhn_b", dcq_pre, BS((tm, Q_LORA), lambda i, j: (i, 0)), w_dq, BS((dn, Q_LORA), lambda i, j: (0, 0)),
        1, h1[1], mix_norm[1:2], dh, F32,
    )
    dh, dhb, dg_ffn1[1], rs_ffn1[1] = _ffn_bwd(
        "ffn1_1", dh, dhb, n1[1], h0[1], ffn1_norm[1:2], gu1[1], a1[1], w_in1[1], w_out1[1], REDUCE_IDS[1], rs_ffn2[1][:1]
    )
    dw_up = _mm_tn(
        "dw_up", ckv, whole(t, KV_LORA), dkv, BS((t, NOPE + V_DIM), lambda j: (0, j)),
        (HEADS_B, KV_LORA, NOPE + V_DIM), BS((None, KV_LORA, NOPE + V_DIM), lambda j: (j, 0, 0)), HEADS_B,
    )
    dckr, dg_latent = _kv_latent_bwd(dkv, w_up, ckr, latent_norm, dkr, c64, s64, p64, seq)
    dw_down = dw_rows("dw_down", hk, dckr)
    dh, dg_kv, dhb = _mm_nt_norm_bwd(
        "dhk", dckr, BS((tm, KV_LORA + ROPE), lambda i, j: (i, 0)), w_down, BS((dn, KV_LORA + ROPE), lambda i, j: (0, 0)),
        1, h0[1], kv_norm.reshape(1, dn), dh, F32,
    )
    dh, dhb, dg_ffn2[0], rs_ffn2[0] = _ffn_bwd(
        "ffn2_0", dh, dhb, n2[0], h2[0], ffn2_norm[0:1], gu2[0], a2[0], w_in2[0], w_out2[0], REDUCE_IDS[2], rs_ffn1[1][:1]
    )
    do_a = _mm_nt_plain("attn_a_do", dhb, w_o_a.reshape(dn, dn))
    dw_o_a = dw_rows("attn_a_dwo", o_a, dhb)
    dqkv3, dbias = _attn_a_bwd(qkv3, o_a, lse_a, do_a, bias, bl, seq)
    dw_qkv = _dw_qkv(hn[0], dqkv3, qkv_wb)
    mixer_grads = [dw_o_a, dw_qkv, dw_o_b, dw_uq, dw_dq, dw_up, dw_down]
    dh, dg_mix[0], dhb = _mm_nt_norm_bwd(
        "dhn_a", dqkv3, BS((3, tm, dn), lambda i, j: (0, i, 0)), w_qkv, BS((NDEV, dn, qkv_wb), lambda i, j: (0, 0, 0)),
        1, h1[0], mix_norm[0:1], dh, F32, mm_fn=_dqkv_mm(qkv_wb // 128), after=mixer_grads,
    )
    rs_mixers = _exchange_sc("mixers_reduce", [("scatter", g) for g in mixer_grads], REDUCE_IDS[3], rs_ffn2[0][:1])
    dh, dhb, dg_ffn1[0], rs_ffn1[0] = _ffn_bwd(
        "ffn1_0", dh, dhb, n1[0], h0[0], ffn1_norm[0:1], gu1[0], a1[0], w_in1[0], w_out1[0], REDUCE_IDS[4], rs_mixers[:1]
    )
    grad_x = dh.reshape(bl, seq, dn)
    dtable = jnp.pad(_rel_bias_bwd(_window_bias_bwd(dbias)), ((0, 0), (1, 0)))

    def update(name, parts, w, m, v):
        shape3 = (len(parts),) + w.shape[-2:]
        parts = [p.reshape((NDEV,) + shape3[1:]) for p in parts]
        outs = _adamw(name, parts, w.reshape(shape3), m.reshape(shape3), v.reshape(shape3))
        return [o.reshape(w.shape) for o in outs]

    res = {}
    r_in2_1, r_out2_1 = rs_ffn2[1]
    r_in1_1, r_out1_1 = rs_ffn1[1]
    r_in2_0, r_out2_0 = rs_ffn2[0]
    r_in1_0, r_out1_0 = rs_ffn1[0]
    r_o_a, r_qkv, r_o_b, r_uq, r_dq, r_up, r_down = rs_mixers
    def update_transposed(name, parts, w, m, v):
        outs = update(name, parts, *[jnp.swapaxes(a, 1, 2) for a in (w, m, v)])
        return [jnp.swapaxes(o, 1, 2) for o in outs]

    res["ffn2_w_in"] = update_transposed("adamw_ffn2_w_in", [r_in2_0, r_in2_1], ffn2_w_in, m_ffn2_w_in, v_ffn2_w_in)
    res["ffn2_w_out"] = update("adamw_ffn2_w_out", [r_out2_0, r_out2_1], ffn2_w_out, m_ffn2_w_out, v_ffn2_w_out)
    res["kv_w_down"] = update("adamw_kv_w_down", [r_down], kv_w_down, m_kv_w_down, v_kv_w_down)
    res["kv_w_up"] = update("adamw_kv_w_up", [r_up], kv_w_up, m_kv_w_up, v_kv_w_up)
    res["b_w_dq"] = update("adamw_b_w_dq", [r_dq], b_w_dq, m_b_w_dq, v_b_w_dq)
    res["b_w_uq"] = update("adamw_b_w_uq", [r_uq], b_w_uq, m_b_w_uq, v_b_w_uq)
    res["b_w_o"] = update("adamw_b_w_o", [r_o_b], b_w_o, m_b_w_o, v_b_w_o)
    res["a_w_qkv"] = update("adamw_a_w_qkv", [r_qkv], a_w_qkv, m_a_w_qkv, v_a_w_qkv)
    res["a_w_o"] = update("adamw_a_w_o", [r_o_a], a_w_o, m_a_w_o, v_a_w_o)

    small = _pack_small(
        jnp.stack([dg_ffn1[0][0], dg_ffn1[1][0]]), jnp.stack([dg_mix[0][0], dg_mix[1][0]]), jnp.stack([dg_ffn2[0][0], dg_ffn2[1][0]]),
        dg_kv[0], dg_final[0], dg_q[0], dg_latent[0], dtable, loss_part[0],
    )
    done = [r[1] for name, r in res.items() if name != "ffn2_w_in"]
    (r_small,) = _exchange("gather_small_grads", [("gather", small)], after=done)
    res["ffn1_w_in"] = update_transposed("adamw_ffn1_w_in", [r_in1_0, r_in1_1], ffn1_w_in, m_ffn1_w_in, v_ffn1_w_in)
    res["ffn1_w_out"] = update("adamw_ffn1_w_out", [r_out1_0, r_out1_1], ffn1_w_out, m_ffn1_w_out, v_ffn1_w_out)
    zero_row = jnp.zeros((dn,), F32)
    packs = [
        _pack_small(f1, mx, f2, kvn, fin, qn, lat, rel, zero_row)
        for f1, mx, f2, kvn, fin, qn, lat, rel in (
            (ffn1_norm, mix_norm, ffn2_norm, kv_norm, final_norm, b_q_norm, kv_latent_norm, a_rel_bias),
            (m_ffn1_norm, m_mix_norm, m_ffn2_norm, m_kv_norm, m_final_norm, m_b_q_norm, m_kv_latent_norm, m_a_rel_bias),
            (v_ffn1_norm, v_mix_norm, v_ffn2_norm, v_kv_norm, v_final_norm, v_b_q_norm, v_kv_latent_norm, v_a_rel_bias),
        )
    ]
    small_out = _adamw_small(r_small, *packs)
    for name in ("ffn1_norm", "mix_norm", "ffn2_norm", "b_q_norm"):
        res[name] = [so[name] for so in small_out]
    for name in ("kv_norm", "kv_latent_norm", "final_norm"):
        res[name] = [so[name].reshape(-1) for so in small_out]
    res["a_rel_bias"] = [so["a_rel_bias"].reshape(-1)[: HEADS_A * NREL].reshape(1, HEADS_A, NREL) for so in small_out]
    loss = small_out[0]["last"][0, 0]

    order = [
        "ffn1_norm", "ffn1_w_in", "ffn1_w_out", "mix_norm", "ffn2_norm", "ffn2_w_in", "ffn2_w_out", "a_w_qkv", "a_rel_bias",
        "a_w_o", "kv_norm", "kv_w_down", "kv_latent_norm", "kv_w_up", "b_w_dq", "b_q_norm", "b_w_uq", "b_w_o", "final_norm",
    ]
    return (loss, grad_x, *[res[n][0] for n in order], *[res[n][1] for n in order], *[res[n][2] for n in order], *[res[n][3] for n in order])
```

```python
import jax
import jax.numpy as jnp
import numpy as np
from jax import lax
from jax.experimental import pallas as pl
from jax.experimental.pallas import tpu as pltpu
from jax.experimental.pallas import tpu_sc as plsc

NDEV = 8
D_MODEL = 1024
D_FF = 2816
FB = 2 * D_FF // NDEV
NJ = D_FF // FB
CHUNK = 64
LEFT_CHUNKS = 8
PAD = LEFT_CHUNKS * CHUNK
BAND = PAD + CHUNK
CHUNKS_PER_STEP = 4
WINDOW = PAD + CHUNKS_PER_STEP * CHUNK
STEP_ROWS = CHUNKS_PER_STEP * 2 * CHUNK
SCALE_A = CHUNK ** -0.5
MAX_REL = 128
NREL = 2 * MAX_REL + 1
NREL_USED = 256
HEADS_A = 16
HEADS_B = 8
NOPE = 128
ROPE = 64
QK_B = NOPE + ROPE
V_DIM = 128
Q_LORA = 768
KV_LORA = 256
ROPE_THETA = 10000.0
EPS = 1e-6
NEG_INF = -1e30
MLA_TQ = 256
MLA_TK_FWD = 256
MLA_TK_BWD = 1024
ADAM_LR = 0.001
ADAM_B1 = 0.9
ADAM_B2 = 0.999
ADAM_EPS = 1e-08
ADAM_WD = 0.01
ADAM_STEP = 10
PACK_ROWS = 16
GATHER_IDS = tuple(range(1, 9))
REDUCE_IDS = tuple(range(9, 14))
VMEM_LIMIT_BYTES = 56 * 1024 * 1024

F32 = jnp.float32
BF16 = jnp.bfloat16
SDS = jax.ShapeDtypeStruct
BS = pl.BlockSpec
MESH = pl.DeviceIdType.MESH


def _cparams(n_axes):
    return pltpu.CompilerParams(dimension_semantics=("arbitrary",) * n_axes, vmem_limit_bytes=VMEM_LIMIT_BYTES)


def _tile(t, want=512):
    return want if t % want == 0 else t


def _dot(a, b):
    return jnp.dot(a, b, preferred_element_type=F32)


def _dot_nt(a, b):
    return lax.dot_general(a, b, (((1,), (1,)), ((), ())), preferred_element_type=F32)


def _dot_tn(a, b):
    return lax.dot_general(a, b, (((0,), (0,)), ((), ())), preferred_element_type=F32)


def _split3(a):
    hi = a.astype(BF16)
    rest = a - hi.astype(F32)
    mid = rest.astype(BF16)
    return hi, mid, (rest - mid.astype(F32)).astype(BF16)


def _dot_exact(a, onehot, transposed=False):
    ob = onehot.astype(BF16)
    dot = _dot_nt if transposed else _dot
    hi, mid, lo = _split3(a)
    return dot(hi, ob) + dot(mid, ob) + dot(lo, ob)


def _rms_scale(h):
    return lax.rsqrt(jnp.mean(h * h, axis=-1, keepdims=True) + EPS)


def _acc_rows(ref, val, step, n_steps):
    part = val.reshape(val.shape[0] // 8, 8, val.shape[1]).sum(axis=0)

    @pl.when(step == 0)
    def _():
        ref[...] = part

    @pl.when(step > 0)
    def _():
        ref[...] += part

    @pl.when(step == n_steps - 1)
    def _():
        ref[...] = jnp.broadcast_to(jnp.sum(ref[...], axis=0, keepdims=True), ref.shape)


def _exchange_plan(entries):
    ins = [e[1] for e in entries]
    kinds = [e[0] for e in entries]
    lands = [SDS((NDEV,) + a.shape if k == "gather" else a.shape, a.dtype) for k, a in zip(kinds, ins)]
    return ins, lands, kinds


def _mesh_place():
    x, y, c = lax.axis_index("x"), lax.axis_index("y"), lax.axis_index("c")
    return (x, y, c), 4 * x + 2 * y + c


def _flipped(place, p):
    x, y, c = place
    px = 1 - x if p & 4 else x
    py = 1 - y if p & 2 else y
    pc = 1 - c if p & 1 else c
    return (px, py, pc), 4 * px + 2 * py + pc


def _ends(kind, src_ref, land_ref, origin, target):
    if kind == "gather":
        return src_ref, land_ref.at[origin]
    return src_ref.at[target], land_ref.at[origin]


def _remote(kind, src_ref, land_ref, send_sems, recv_sems, k, p, place, me, arriving):
    peer_pos, peer = _flipped(place, p)
    src, dst = _ends(kind, src_ref, land_ref, me, peer)
    if arriving:
        dst = _ends(kind, src_ref, land_ref, peer, me)[1]
    sem = k * (NDEV - 1) + p - 1
    return pltpu.make_async_remote_copy(
        src_ref=src, dst_ref=dst, send_sem=send_sems.at[sem], recv_sem=recv_sems.at[sem], device_id=peer_pos, device_id_type=MESH,
    )


def _exchange(name, entries, after=()):
    ins, lands, kinds = _exchange_plan(entries)
    n = len(ins)
    after = tuple(after)

    def body(*refs):
        refs = refs[:n] + refs[n + len(after) :]
        in_refs, land_refs = refs[:n], refs[n : 2 * n]
        send_sems, recv_sems, local_sems = refs[2 * n :]
        place, me = _mesh_place()
        local = []
        for k in range(n):
            src, dst = _ends(kinds[k], in_refs[k], land_refs[k], me, me)
            local.append(pltpu.make_async_copy(src, dst, local_sems.at[k]))
            local[-1].start()
        sends = []
        for p in range(1, NDEV):
            for k in range(n):
                sends.append(_remote(kinds[k], in_refs[k], land_refs[k], send_sems, recv_sems, k, p, place, me, False))
                sends[-1].start()
        for p in range(1, NDEV):
            for k in range(n):
                _remote(kinds[k], in_refs[k], land_refs[k], send_sems, recv_sems, k, p, place, me, True).wait_recv()
        for cp in sends:
            cp.wait_send()
        for cp in local:
            cp.wait()

    any_spec = BS(memory_space=pl.ANY)
    return pl.pallas_call(
        body, name=name, out_shape=lands, in_specs=[any_spec] * (n + len(after)), out_specs=[any_spec] * n,
        scratch_shapes=[
            pltpu.SemaphoreType.DMA((n * (NDEV - 1),)), pltpu.SemaphoreType.DMA((n * (NDEV - 1),)), pltpu.SemaphoreType.DMA((n,)),
        ],
    )(*ins, *after)


def _exchange_sc(name, entries, collective_id, after=()):
    ins, lands, kinds = _exchange_plan(entries)
    n = len(ins)
    after = tuple(after)

    def launch(*refs):
        refs = refs[:n] + refs[n + len(after) :]
        in_refs, land_refs = refs[:n], refs[n : 2 * n]
        send_sems, recv_sems, local_sems = refs[2 * n :]
        place, me = _mesh_place()
        barrier = pltpu.get_barrier_semaphore()
        for p in range(1, NDEV):
            pl.semaphore_signal(barrier, inc=1, device_id=_flipped(place, p)[0], device_id_type=MESH)
        pl.semaphore_wait(barrier, NDEV - 1)
        local = []
        for k in range(n):
            src, dst = _ends(kinds[k], in_refs[k], land_refs[k], me, me)
            local.append(pltpu.make_async_copy(src, dst, local_sems.at[k]))
            local[-1].start()
        sends = []
        if all(kind == "gather" for kind in kinds):
            for p in (1, 2, 4, 6):
                for k in range(n):
                    sends.append(_remote(kinds[k], in_refs[k], land_refs[k], send_sems, recv_sems, k, p, place, me, False))
                    sends[-1].start()
            sibling_pos, _ = _flipped(place, 1)
            for f in (2, 4, 6):
                _, origin = _flipped(place, f)
                for k in range(n):
                    _remote(kinds[k], in_refs[k], land_refs[k], send_sems, recv_sems, k, f, place, me, True).wait_recv()
                    sem = k * (NDEV - 1) + f
                    sends.append(
                        pltpu.make_async_remote_copy(
                            src_ref=land_refs[k].at[origin], dst_ref=land_refs[k].at[origin], send_sem=send_sems.at[sem],
                            recv_sem=recv_sems.at[sem], device_id=sibling_pos, device_id_type=MESH,
                        )
                    )
                    sends[-1].start()
            for p in (1, 3, 5, 7):
                for k in range(n):
                    _remote(kinds[k], in_refs[k], land_refs[k], send_sems, recv_sems, k, p, place, me, True).wait_recv()
        else:
            for p in range(1, NDEV):
                for k in range(n):
                    sends.append(_remote(kinds[k], in_refs[k], land_refs[k], send_sems, recv_sems, k, p, place, me, False))
                    sends[-1].start()
            for p in range(1, NDEV):
                for k in range(n):
                    _remote(kinds[k], in_refs[k], land_refs[k], send_sems, recv_sems, k, p, place, me, True).wait_recv()
        for cp in sends:
            cp.wait_send()
        for cp in local:
            cp.wait()

    return pl.kernel(
        launch, out_type=tuple(lands), mesh=plsc.ScalarSubcoreMesh(axis_name="sequencer", num_cores=1), name=name,
        scratch_types=(
            pltpu.SemaphoreType.DMA((n * (NDEV - 1),)), pltpu.SemaphoreType.DMA((n * (NDEV - 1),)), pltpu.SemaphoreType.DMA((n,)),
        ),
        compiler_params=pltpu.CompilerParams(collective_id=collective_id),
    )(*ins, *after)


def _norm_fwd(name, h, gammas):
    t, dn = h.shape
    ng = gammas.shape[0]
    tm = _tile(t)

    def body(h_ref, g_ref, *outs):
        hv = h_ref[...]
        hh = hv * _rms_scale(hv)
        for i, o_ref in enumerate(outs):
            o_ref[...] = (hh * g_ref[i : i + 1, :]).astype(BF16)

    row = BS((tm, dn), lambda i: (i, 0))
    return pl.pallas_call(
        body, name=name, grid=(t // tm,),
        in_specs=[row, BS((ng, dn), lambda i: (0, 0))],
        out_specs=[row] * ng, out_shape=[SDS((t, dn), BF16)] * ng,
        compiler_params=_cparams(1),
    )(h, gammas)


def _ffn_in(name, n, w_in, layer):
    t, dn = n.shape
    tm = _tile(t, 1024)

    def body(n_ref, wg_ref, wu_ref, gu_ref, a_ref):
        xv = n_ref[...]
        g = _dot(xv, wg_ref[...])
        u = _dot(xv, wu_ref[...])
        gu_ref[0] = g.astype(BF16)
        gu_ref[1] = u.astype(BF16)
        a_ref[...] = (g * jax.nn.sigmoid(g) * u).astype(BF16)

    return pl.pallas_call(
        body, name=name, grid=(NJ, t // tm),
        in_specs=[
            BS((tm, dn), lambda j, i: (i, 0)),
            BS((None, None, dn, FB), lambda j, i: (layer, j, 0, 0)),
            BS((None, None, dn, FB), lambda j, i: (layer, j + NJ, 0, 0)),
        ],
        out_specs=[BS((None, 2, tm, FB), lambda j, i: (j, 0, i, 0)), BS((None, tm, FB), lambda j, i: (j, i, 0))],
        out_shape=[SDS((NJ, 2, t, FB), BF16), SDS((NJ, t, FB), BF16)],
        compiler_params=_cparams(2),
    )(n, w_in, w_in)


def _mm_res_norm(name, a, w, layer, h_in, gammas, scale):
    nk, t, kb = a.shape
    dn = w.shape[-1]
    ng = 0 if gammas is None else gammas.shape[0]
    tm = _tile(t)

    def body(*refs):
        a_ref, w_ref, h_ref = refs[:3]
        g_ref = refs[3] if ng else None
        outs = refs[3 + (1 if ng else 0) :]
        acc = _dot(a_ref[0], w_ref[0])
        for k in range(1, nk):
            acc += _dot(a_ref[k], w_ref[k])
        ho = h_ref[...] + scale * acc
        outs[0][...] = ho
        if ng:
            hh = ho * _rms_scale(ho)
            for i in range(ng):
                outs[1 + i][...] = (hh * g_ref[i : i + 1, :]).astype(BF16)

    row = BS((tm, dn), lambda i: (i, 0))
    in_specs = [BS((nk, tm, kb), lambda i: (0, i, 0)), BS((None, nk, kb, dn), lambda i: (layer, 0, 0, 0)), row]
    args = [a, w, h_in]
    if ng:
        in_specs.append(BS((ng, dn), lambda i: (0, 0)))
        args.append(gammas)
    return pl.pallas_call(
        body, name=name, grid=(t // tm,),
        in_specs=in_specs,
        out_specs=[row] * (1 + ng), out_shape=[SDS((t, dn), F32)] + [SDS((t, dn), BF16)] * ng,
        compiler_params=_cparams(1),
    )(*args)


def _qkv_proj(name, hn, w_qkv):
    t, dn = hn.shape
    wb = w_qkv.shape[-1]
    per = wb // 128
    tm = _tile(t)

    def body(x_ref, w_ref, o_ref):
        xv = x_ref[...]
        for j in range(NDEV):
            yv = _dot(xv, w_ref[j]).astype(BF16)
            for i in range(per):
                n = per * j + i
                o_ref[n // 8, :, (n % 8) * 128 : (n % 8 + 1) * 128] = yv[:, i * 128 : (i + 1) * 128]

    return pl.pallas_call(
        body, name=name, grid=(t // tm,),
        in_specs=[BS((tm, dn), lambda i: (i, 0)), BS((NDEV, dn, wb), lambda i: (0, 0, 0))],
        out_specs=BS((3, tm, dn), lambda i: (0, i, 0)), out_shape=SDS((3, t, dn), BF16),
        compiler_params=_cparams(1),
    )(hn, w_qkv)


def _rel_onehot(i):
    r = lax.broadcasted_iota(jnp.int32, (NREL_USED, BAND), 0)
    j = lax.broadcasted_iota(jnp.int32, (NREL_USED, BAND), 1)
    idx = jnp.clip(PAD + i - j, -MAX_REL, MAX_REL) + MAX_REL
    return (idx - 1 == r).astype(BF16)


def _rel_bias_fwd(table):
    def body(t_ref, o_ref):
        i8 = pl.program_id(0)
        for ii in range(8):
            o_ref[:, ii, :] = _dot_exact(t_ref[...], _rel_onehot(i8 * 8 + ii))

    return pl.pallas_call(
        body, name="rel_bias_fwd", grid=(CHUNK // 8,),
        in_specs=[BS((HEADS_A, NREL_USED), lambda i: (0, 0))],
        out_specs=BS((HEADS_A, 8, BAND), lambda i: (0, i, 0)), out_shape=SDS((HEADS_A, CHUNK, BAND), F32),
        compiler_params=_cparams(1),
    )(table)


def _rel_bias_bwd(dbias):
    def body(d_ref, o_ref):
        i8 = pl.program_id(0)
        acc = jnp.zeros((HEADS_A, NREL_USED), F32)
        for ii in range(8):
            acc += _dot_exact(d_ref[:, ii, :], _rel_onehot(i8 * 8 + ii), transposed=True)

        @pl.when(i8 == 0)
        def _():
            o_ref[...] = acc

        @pl.when(i8 > 0)
        def _():
            o_ref[...] += acc

    return pl.pallas_call(
        body, name="rel_bias_bwd", grid=(CHUNK // 8,),
        in_specs=[BS((HEADS_A, 8, BAND), lambda i: (0, i, 0))],
        out_specs=BS((HEADS_A, NREL_USED), lambda i: (0, 0)), out_shape=SDS((HEADS_A, NREL_USED), F32),
        compiler_params=_cparams(1),
    )(dbias)


def _window_bias(bias):
    b = bias.reshape(HEADS_A // 2, 2, CHUNK, BAND)
    per_chunk = [
        jnp.pad(b, ((0, 0), (0, 0), (0, 0), (cc * CHUNK, WINDOW - BAND - cc * CHUNK)), constant_values=NEG_INF)
        for cc in range(CHUNKS_PER_STEP)
    ]
    return jnp.stack(per_chunk, axis=1).reshape(HEADS_A // 2, STEP_ROWS, WINDOW)


def _window_bias_bwd(dwin):
    d = dwin.reshape(HEADS_A // 2, CHUNKS_PER_STEP, 2, CHUNK, WINDOW)
    return sum(d[:, cc, :, :, cc * CHUNK : cc * CHUNK + BAND] for cc in range(CHUNKS_PER_STEP)).reshape(HEADS_A, CHUNK, BAND)


def _step_rows(xs, lane):
    parts = []
    for cc in range(CHUNKS_PER_STEP):
        xc = xs[cc * CHUNK : (cc + 1) * CHUNK]
        parts.append(jnp.where(lane < 64, xc, jnp.zeros_like(xc)))
        parts.append(jnp.where(lane >= 64, xc, jnp.zeros_like(xc)))
    return jnp.concatenate(parts, axis=0)


def _pair_rows(ys, lane):
    parts = []
    for cc in range(CHUNKS_PER_STEP):
        y0 = ys[(2 * cc) * CHUNK : (2 * cc + 1) * CHUNK]
        y1 = ys[(2 * cc + 1) * CHUNK : (2 * cc + 2) * CHUNK]
        parts.append(jnp.where(lane < 64, y0, y1))
    return jnp.concatenate(parts, axis=0)


def _window_scores(q_rows, kwin, bias_win, first_key):
    s = _dot_nt(q_rows, kwin) + bias_win
    if first_key is None:
        return s
    col = lax.broadcasted_iota(jnp.int32, s.shape, 1)
    return jnp.where(col >= first_key, s, NEG_INF)


def _window_loop(n_passes, chunks):
    n_padded = min(PAD // (CHUNKS_PER_STEP * CHUNK), n_passes)
    lax.fori_loop(0, n_padded, lambda it, carry: chunks(it, carry, True), 0, unroll=2)
    if n_passes > n_padded:
        lax.fori_loop(n_padded, n_passes, lambda it, carry: chunks(it, carry, False), 0, unroll=2)


def _attn_a_fwd(qkv3, bias_win, bl, seq):
    t, dn = qkv3.shape[1:]
    npair = dn // 128
    step = CHUNKS_PER_STEP * CHUNK

    def body(q_ref, k_ref, v_ref, b_ref, o_ref, lse_ref, kpad, vpad):
        kpad[0:PAD, :] = jnp.zeros((PAD, 128), BF16)
        vpad[0:PAD, :] = jnp.zeros((PAD, 128), BF16)
        kpad[PAD:, :] = k_ref[...]
        vpad[PAD:, :] = v_ref[...]
        lane = lax.broadcasted_iota(jnp.int32, (CHUNK, 128), 1)

        def chunks(it, carry, padded):
            r0 = pl.multiple_of(it * step, step)
            q_rows = _step_rows(q_ref[pl.ds(r0, step), :] * SCALE_A, lane)
            s = _window_scores(q_rows, kpad[pl.ds(r0, WINDOW), :], b_ref[...], PAD - r0 if padded else None)
            m = jnp.max(s, axis=-1, keepdims=True)
            e = jnp.exp(s - m)
            total = jnp.sum(e, axis=-1, keepdims=True)
            o_rows = _dot(e.astype(BF16), vpad[pl.ds(r0, WINDOW), :]) * (1.0 / total)
            o_ref[pl.ds(r0, step), :] = _pair_rows(o_rows, lane).astype(BF16)
            lse_ref[pl.ds(pl.multiple_of(it * STEP_ROWS, STEP_ROWS), STEP_ROWS), :] = m + jnp.log(total)
            return carry

        _window_loop(seq // step, chunks)

    return pl.pallas_call(
        body, name="attn_a_fwd", grid=(bl, npair),
        in_specs=[
            BS((None, seq, 128), lambda b, h: (0, b, h)),
            BS((None, seq, 128), lambda b, h: (1, b, h)),
            BS((None, seq, 128), lambda b, h: (2, b, h)),
            BS((None, STEP_ROWS, WINDOW), lambda b, h: (h, 0, 0)),
        ],
        out_specs=[BS((seq, 128), lambda b, h: (b, h)), BS((None, 2 * seq, 1), lambda b, h: (h, b, 0))],
        out_shape=[SDS((t, dn), BF16), SDS((npair, 2 * t, 1), F32)],
        scratch_shapes=[pltpu.VMEM((PAD + seq, 128), BF16), pltpu.VMEM((PAD + seq, 128), BF16)],
        compiler_params=_cparams(2),
    )(qkv3, qkv3, qkv3, bias_win)


def _attn_a_bwd(qkv3, out, lse, do, bias_win, bl, seq):
    t, dn = qkv3.shape[1:]
    npair = dn // 128
    step = CHUNKS_PER_STEP * CHUNK

    def body(q_ref, k_ref, v_ref, o_ref, lse_ref, do_ref, b_ref, dqkv_ref, db_ref, kpad, vpad, dkacc, dvacc):
        b = pl.program_id(1)
        kpad[0:PAD, :] = jnp.zeros((PAD, 128), BF16)
        vpad[0:PAD, :] = jnp.zeros((PAD, 128), BF16)
        kpad[PAD:, :] = k_ref[...]
        vpad[PAD:, :] = v_ref[...]
        dkacc[...] = jnp.zeros_like(dkacc)
        dvacc[...] = jnp.zeros_like(dvacc)

        @pl.when(b == 0)
        def _():
            db_ref[...] = jnp.zeros_like(db_ref)

        lane = lax.broadcasted_iota(jnp.int32, (CHUNK, 128), 1)

        def chunks(it, carry, padded):
            r0 = pl.multiple_of(it * step, step)
            q_rows = _step_rows(q_ref[pl.ds(r0, step), :] * SCALE_A, lane)
            do_rows = _step_rows(do_ref[pl.ds(r0, step), :], lane)
            kwin = kpad[pl.ds(r0, WINDOW), :]
            vwin = vpad[pl.ds(r0, WINDOW), :]
            o_rows = _step_rows(o_ref[pl.ds(r0, step), :], lane)
            delta = jnp.sum(do_rows.astype(F32) * o_rows.astype(F32), axis=-1, keepdims=True)
            lse_rows = lse_ref[pl.ds(pl.multiple_of(it * STEP_ROWS, STEP_ROWS), STEP_ROWS), :]
            p = jnp.exp(_window_scores(q_rows, kwin, b_ref[...], PAD - r0 if padded else None) - lse_rows)
            ds = p * (_dot_nt(do_rows, vwin) - delta)
            db_ref[...] += ds
            dsb = ds.astype(BF16)
            dqkv_ref[0, pl.ds(r0, step), :] = (_pair_rows(_dot(dsb, kwin), lane) * SCALE_A).astype(BF16)
            dkacc[pl.ds(r0, WINDOW), :] += _dot_tn(dsb, q_rows)
            dvacc[pl.ds(r0, WINDOW), :] += _dot_tn(p.astype(BF16), do_rows)
            return carry

        _window_loop(seq // step, chunks)
        dqkv_ref[1] = dkacc[PAD:, :].astype(BF16)
        dqkv_ref[2] = dvacc[PAD:, :].astype(BF16)

    return pl.pallas_call(
        body, name="attn_a_bwd", grid=(npair, bl),
        in_specs=[
            BS((None, seq, 128), lambda h, b: (0, b, h)),
            BS((None, seq, 128), lambda h, b: (1, b, h)),
            BS((None, seq, 128), lambda h, b: (2, b, h)),
            BS((seq, 128), lambda h, b: (b, h)),
            BS((None, 2 * seq, 1), lambda h, b: (h, b, 0)),
            BS((seq, 128), lambda h, b: (b, h)),
            BS((None, STEP_ROWS, WINDOW), lambda h, b: (h, 0, 0)),
        ],
        out_specs=[BS((3, seq, 128), lambda h, b: (0, b, h)), BS((None, STEP_ROWS, WINDOW), lambda h, b: (h, 0, 0))],
        out_shape=[SDS((3, t, dn), BF16), SDS((HEADS_A // 2, STEP_ROWS, WINDOW), F32)],
        scratch_shapes=[
            pltpu.VMEM((PAD + seq, 128), BF16), pltpu.VMEM((PAD + seq, 128), BF16),
            pltpu.VMEM((PAD + seq, 128), F32), pltpu.VMEM((PAD + seq, 128), F32),
        ],
        compiler_params=_cparams(2),
    )(qkv3, qkv3, qkv3, out, lse, do, bias_win)


def _rope_tables(seq):
    half = ROPE // 2
    freqs = ROPE_THETA ** (-jnp.arange(half, dtype=F32) / half)
    ang = jnp.arange(seq, dtype=F32)[:, None] * freqs[None, :]
    cos, sin = jnp.cos(ang), jnp.sin(ang)
    c64 = jnp.concatenate([cos, cos], axis=1)
    s64 = jnp.concatenate([-sin, sin], axis=1)
    c192 = jnp.concatenate([jnp.ones((seq, NOPE), F32), c64], axis=1)
    s192 = jnp.concatenate([jnp.zeros((seq, NOPE), F32), s64], axis=1)
    p64 = np.zeros((ROPE, ROPE), np.float32)
    for col in range(ROPE):
        p64[(col + half) % ROPE, col] = 1.0
    p192 = np.zeros((QK_B, QK_B), np.float32)
    p192[NOPE:, NOPE:] = p64
    return c64, s64, jnp.asarray(p64), c192, s192, jnp.asarray(p192)


def _rope(xv, cos, sin_signed, swap):
    return xv * cos + _dot_exact(xv, swap) * sin_signed


def _rope_bwd(dy, cos, sin_signed, swap):
    return dy * cos + _dot_exact(dy * sin_signed, swap)


def _q_down(hn, w_dq, q_norm):
    t, dn = hn.shape
    ql = w_dq.shape[1]
    tm = _tile(t)

    def body(x_ref, w_ref, g_ref, pre_ref, cq_ref):
        pre = _dot(x_ref[...], w_ref[...])
        pre_ref[...] = pre
        cq_ref[...] = (pre * _rms_scale(pre) * g_ref[...]).astype(BF16)

    return pl.pallas_call(
        body, name="q_down", grid=(t // tm,),
        in_specs=[BS((tm, dn), lambda i: (i, 0)), BS((dn, ql), lambda i: (0, 0)), BS((1, ql), lambda i: (0, 0))],
        out_specs=[BS((tm, ql), lambda i: (i, 0))] * 2, out_shape=[SDS((t, ql), F32), SDS((t, ql), BF16)],
        compiler_params=_cparams(1),
    )(hn, w_dq, q_norm)


def _q_up(cq, w_uq, c192, s192, p192, seq):
    t, ql = cq.shape
    tm = _tile(min(seq, 512), min(seq, 512))
    nseq = seq // tm

    def body(x_ref, w_ref, c_ref, s_ref, p_ref, o_ref):
        xv = x_ref[...]
        for h in range(HEADS_B):
            o_ref[h] = _rope(_dot(xv, w_ref[h]), c_ref[...], s_ref[...], p_ref[...]).astype(BF16)

    pos = BS((tm, QK_B), lambda i: (i % nseq, 0))
    return pl.pallas_call(
        body, name="q_up", grid=(t // tm,),
        in_specs=[
            BS((tm, ql), lambda i: (i, 0)), BS((HEADS_B, ql, QK_B), lambda i: (0, 0, 0)), pos, pos,
            BS((QK_B, QK_B), lambda i: (0, 0)),
        ],
        out_specs=BS((HEADS_B, tm, QK_B), lambda i: (0, i, 0)), out_shape=SDS((HEADS_B, t, QK_B), BF16),
        compiler_params=_cparams(1),
    )(cq, w_uq, c192, s192, p192)


def _kv_down(hk, w_down, latent_norm, c64, s64, p64, seq):
    t, dn = hk.shape
    wd = w_down.shape[1]
    tm = _tile(min(seq, 512), min(seq, 512))
    nseq = seq // tm

    def body(x_ref, w_ref, g_ref, c_ref, s_ref, p_ref, ckr_ref, ckv_ref, kr_ref):
        ckr = _dot(x_ref[...], w_ref[...])
        ckr_ref[...] = ckr
        lat = ckr[:, :KV_LORA]
        ckv_ref[...] = (lat * _rms_scale(lat) * g_ref[...]).astype(BF16)
        kr_ref[...] = _rope(ckr[:, KV_LORA:], c_ref[...], s_ref[...], p_ref[...]).astype(BF16)

    pos = BS((tm, ROPE), lambda i: (i % nseq, 0))
    return pl.pallas_call(
        body, name="kv_down", grid=(t // tm,),
        in_specs=[
            BS((tm, dn), lambda i: (i, 0)), BS((dn, wd), lambda i: (0, 0)), BS((1, KV_LORA), lambda i: (0, 0)), pos, pos,
            BS((ROPE, ROPE), lambda i: (0, 0)),
        ],
        out_specs=[BS((tm, wd), lambda i: (i, 0)), BS((tm, KV_LORA), lambda i: (i, 0)), BS((tm, ROPE), lambda i: (i, 0))],
        out_shape=[SDS((t, wd), F32), SDS((t, KV_LORA), BF16), SDS((t, ROPE), BF16)],
        compiler_params=_cparams(1),
    )(hk, w_down, latent_norm, c64, s64, p64)


def _kv_up(ckv, w_up):
    t, kl = ckv.shape
    hb = w_up.shape[-1]
    tm = _tile(t)

    def body(x_ref, w_ref, o_ref):
        xv = x_ref[...]
        for h in range(HEADS_B):
            o_ref[:, h * hb : (h + 1) * hb] = _dot(xv, w_ref[h]).astype(BF16)

    return pl.pallas_call(
        body, name="kv_up", grid=(t // tm,),
        in_specs=[BS((tm, kl), lambda i: (i, 0)), BS((HEADS_B, kl, hb), lambda i: (0, 0, 0))],
        out_specs=BS((tm, HEADS_B * hb), lambda i: (i, 0)), out_shape=SDS((t, HEADS_B * hb), BF16),
        compiler_params=_cparams(1),
    )(ckv, w_up)


def _mla_diagonal_mask(tq):
    rows = lax.broadcasted_iota(jnp.int32, (tq, tq), 0)
    cols = lax.broadcasted_iota(jnp.int32, (tq, tq), 1)
    return jnp.where(jnp.right_shift(cols, 6) <= jnp.right_shift(rows, 6), 0.0, NEG_INF)


def _mla_key_tiles(n_keys, tk):
    return [(slice(k0, min(k0 + tk, n_keys)), min(k0 + tk, n_keys) == n_keys) for k0 in range(0, n_keys, tk)]


def _mla_scores(qi, kt, diagonal):
    s = _dot_nt(qi, kt) * (QK_B ** -0.5)
    if diagonal is None:
        return s
    tq, width = s.shape
    own = s[:, width - tq :] + diagonal
    return own if width == tq else jnp.concatenate([s[:, : width - tq], own], axis=1)


def _mla_fwd(q, kv, kr, bl, seq):
    t = kv.shape[0]
    tq = min(MLA_TQ, seq)

    def body(q_ref, kn_ref, v_ref, kr_ref, o_ref, lse_ref):
        kcat = jnp.concatenate([kn_ref[...], kr_ref[...]], axis=1)
        vv = v_ref[...]
        diagonal = _mla_diagonal_mask(tq)
        for i in range(seq // tq):
            rows = slice(i * tq, (i + 1) * tq)
            qi = q_ref[rows, :]
            m = total = acc = None
            for keys, own in _mla_key_tiles((i + 1) * tq, MLA_TK_FWD):
                s = _mla_scores(qi, kcat[keys], diagonal if own else None)
                m_blk = jnp.max(s, axis=-1, keepdims=True)
                if m is None:
                    m_new = m_blk
                    e = jnp.exp(s - m_new)
                    total = jnp.sum(e, axis=-1, keepdims=True)
                    acc = _dot(e.astype(BF16), vv[keys])
                else:
                    m_new = jnp.maximum(m, m_blk)
                    keep = jnp.exp(m - m_new)
                    e = jnp.exp(s - m_new)
                    total = keep * total + jnp.sum(e, axis=-1, keepdims=True)
                    acc = keep * acc + _dot(e.astype(BF16), vv[keys])
                m = m_new
            o_ref[rows, :] = (acc / total).astype(BF16)
            lse_ref[rows, :] = m + jnp.log(total)

    return pl.pallas_call(
        body, name="mla_fwd", grid=(bl, HEADS_B),
        in_specs=[
            BS((None, seq, QK_B), lambda b, h: (h, b, 0)),
            BS((seq, NOPE), lambda b, h: (b, 2 * h)),
            BS((seq, V_DIM), lambda b, h: (b, 2 * h + 1)),
            BS((seq, ROPE), lambda b, h: (b, 0)),
        ],
        out_specs=[BS((seq, V_DIM), lambda b, h: (b, h)), BS((None, seq, 1), lambda b, h: (h, b, 0))],
        out_shape=[SDS((t, HEADS_B * V_DIM), BF16), SDS((HEADS_B, t, 1), F32)],
        compiler_params=_cparams(2),
    )(q, kv, kv, kr)


def _mla_bwd(q, kv, kr, o, lse, do, c192, s192, p192, bl, seq):
    t = kv.shape[0]
    tq = min(MLA_TQ, seq)

    def body(q_ref, kn_ref, v_ref, kr_ref, o_ref, lse_ref, do_ref, c_ref, s_ref, p_ref, dq_ref, dkv_ref, dkr_ref, dkacc, dvacc):
        h = pl.program_id(1)
        kcat = jnp.concatenate([kn_ref[...], kr_ref[...]], axis=1)
        vv = v_ref[...]
        dkacc[...] = jnp.zeros_like(dkacc)
        dvacc[...] = jnp.zeros_like(dvacc)
        diagonal = _mla_diagonal_mask(tq)
        for i in range(seq // tq):
            rows = slice(i * tq, (i + 1) * tq)
            qi = q_ref[rows, :]
            doi = do_ref[rows, :]
            lse_i = lse_ref[rows, :]
            delta = jnp.sum(doi.astype(F32) * o_ref[rows, :].astype(F32), axis=-1, keepdims=True)
            dq = None
            for keys, own in _mla_key_tiles((i + 1) * tq, MLA_TK_BWD):
                p = jnp.exp(_mla_scores(qi, kcat[keys], diagonal if own else None) - lse_i)
                ds = p * (_dot_nt(doi, vv[keys]) - delta)
                dsb = (ds * (QK_B ** -0.5)).astype(BF16)
                dq_blk = _dot(dsb, kcat[keys])
                dq = dq_blk if dq is None else dq + dq_blk
                dkacc[keys, :] += _dot_tn(dsb, qi)
                dvacc[keys, :] += _dot_tn(p.astype(BF16), doi)
            dq_ref[rows, :] = _rope_bwd(dq, c_ref[rows, :], s_ref[rows, :], p_ref[...]).astype(BF16)
        dk = dkacc[...]
        dkv_ref[:, :NOPE] = dk[:, :NOPE].astype(BF16)
        dkv_ref[:, NOPE:] = dvacc[...].astype(BF16)

        @pl.when(h == 0)
        def _():
            dkr_ref[...] = dk[:, NOPE:]

        @pl.when(h > 0)
        def _():
            dkr_ref[...] += dk[:, NOPE:]

    return pl.pallas_call(
        body, name="mla_bwd", grid=(bl, HEADS_B),
        in_specs=[
            BS((None, seq, QK_B), lambda b, h: (h, b, 0)),
            BS((seq, NOPE), lambda b, h: (b, 2 * h)),
            BS((seq, V_DIM), lambda b, h: (b, 2 * h + 1)),
            BS((seq, ROPE), lambda b, h: (b, 0)),
            BS((seq, V_DIM), lambda b, h: (b, h)),
            BS((None, seq, 1), lambda b, h: (h, b, 0)),
            BS((seq, V_DIM), lambda b, h: (b, h)),
            BS((seq, QK_B), lambda b, h: (0, 0)),
            BS((seq, QK_B), lambda b, h: (0, 0)),
            BS((QK_B, QK_B), lambda b, h: (0, 0)),
        ],
        out_specs=[
            BS((None, seq, QK_B), lambda b, h: (h, b, 0)),
            BS((seq, NOPE + V_DIM), lambda b, h: (b, h)),
            BS((seq, ROPE), lambda b, h: (b, 0)),
        ],
        out_shape=[SDS((HEADS_B, t, QK_B), BF16), SDS((t, HEADS_B * (NOPE + V_DIM)), BF16), SDS((t, ROPE), F32)],
        scratch_shapes=[pltpu.VMEM((seq, QK_B), F32), pltpu.VMEM((seq, V_DIM), F32)],
        compiler_params=_cparams(2),
    )(q, kv, kv, kr, o, lse, do, c192, s192, p192)


def _loss_final(h, target, gamma):
    t, dn = h.shape
    tm = _tile(t)
    nt = t // tm

    def body(h_ref, t_ref, g_ref, dh_ref, dhb_ref, dg_ref, loss_ref):
        i = pl.program_id(0)
        hv = h_ref[...]
        r = _rms_scale(hv)
        hh = hv * r
        gam = g_ref[...]
        err = hh * gam - t_ref[...]
        part = 0.5 * jnp.sum(jnp.mean(err * err, axis=-1, keepdims=True))

        @pl.when(i == 0)
        def _():
            loss_ref[...] = jnp.zeros_like(loss_ref)

        loss_ref[...] += part
        dy = err * (1.0 / dn)
        _acc_rows(dg_ref, dy * hh, i, nt)
        t1 = dy * gam
        dh = r * (t1 - hh * jnp.mean(t1 * hh, axis=-1, keepdims=True))
        dh_ref[...] = dh
        dhb_ref[...] = dh.astype(BF16)

    row = BS((tm, dn), lambda i: (i, 0))
    return pl.pallas_call(
        body, name="loss_final", grid=(nt,),
        in_specs=[row, row, BS((1, dn), lambda i: (0, 0))],
        out_specs=[row, row, BS((8, dn), lambda i: (0, 0)), BS((8, 128), lambda i: (0, 0))],
        out_shape=[SDS((t, dn), F32), SDS((t, dn), BF16), SDS((8, dn), F32), SDS((8, 128), F32)],
        compiler_params=_cparams(1),
    )(h, target, gamma)


def _ffn_bwd_in(name, dh, w_out, layer, gu):
    t, dn = dh.shape
    tm = _tile(t, 1024)

    def body(dh_ref, w_ref, gu_ref, o_ref):
        da = 0.5 * _dot_nt(dh_ref[...], w_ref[...])
        g = gu_ref[0].astype(F32)
        u = gu_ref[1].astype(F32)
        sg = jax.nn.sigmoid(g)
        o_ref[0] = (da * u * (sg * (1.0 + g * (1.0 - sg)))).astype(BF16)
        o_ref[1] = (da * (g * sg)).astype(BF16)

    blk = BS((None, 2, tm, FB), lambda j, i: (j, 0, i, 0))
    return pl.pallas_call(
        body, name=name, grid=(NJ, t // tm),
        in_specs=[BS((tm, dn), lambda j, i: (i, 0)), BS((None, None, FB, dn), lambda j, i: (layer, j, 0, 0)), blk],
        out_specs=blk, out_shape=SDS((NJ, 2, t, FB), BF16),
        compiler_params=_cparams(2),
    )(dh, w_out, gu)


def _mm_nt_plain(name, xf, w):
    t, dn = xf.shape
    n = w.shape[0]
    tm = _tile(t)

    def body(x_ref, w_ref, o_ref):
        o_ref[...] = _dot_nt(x_ref[...], w_ref[...]).astype(BF16)

    return pl.pallas_call(
        body, name=name, grid=(t // tm,),
        in_specs=[BS((tm, dn), lambda i: (i, 0)), BS((n, dn), lambda i: (0, 0))],
        out_specs=BS((tm, n), lambda i: (i, 0)), out_shape=SDS((t, n), BF16),
        compiler_params=_cparams(1),
    )(xf, w)


def _mm_tn(name, xa, x_spec, ya, y_spec, out_shape, out_spec, nj, scale=None):
    def body(x_ref, y_ref, o_ref):
        acc = _dot_tn(x_ref[...], y_ref[...])
        o_ref[...] = (acc if scale is None else scale * acc).astype(BF16)

    return pl.pallas_call(
        body, name=name, grid=(nj,),
        in_specs=[x_spec, y_spec], out_specs=out_spec, out_shape=SDS(out_shape, BF16),
        compiler_params=_cparams(1),
    )(xa, ya)


def _dw_qkv(hn, dqkv3, wb):
    t, dn = hn.shape
    per = wb // 128

    def body(x_ref, *refs):
        cols = [y_ref[...] for y_ref in refs[:per]]
        refs[per][...] = _dot_tn(x_ref[...], jnp.concatenate(cols, axis=1)).astype(BF16)

    def piece(k):
        return BS((None, t, 128), lambda j: ((per * j + k) // 8, 0, (per * j + k) % 8))

    return pl.pallas_call(
        body, name="dw_qkv", grid=(NDEV,),
        in_specs=[BS((t, dn), lambda j: (0, 0))] + [piece(k) for k in range(per)],
        out_specs=BS((None, dn, wb), lambda j: (j, 0, 0)), out_shape=SDS((NDEV, dn, wb), BF16),
        compiler_params=_cparams(1),
    )(hn, *([dqkv3] * per))


def _mm_nt_epi(name, ya, y_spec, wa, w_spec, nj, n_out, extra, out_shapes, out_specs, epilogue, tm, nt, mm_fn=None):
    n_extra = len(extra)
    n_outs = len(out_shapes)

    def body(*refs):
        y_ref, w_ref = refs[:2]
        ex = refs[2 : 2 + n_extra]
        outs = refs[2 + n_extra : 2 + n_extra + n_outs]
        i = pl.program_id(0)
        j = pl.program_id(1)
        part = _dot_nt(y_ref[...], w_ref[...]) if mm_fn is None else mm_fn(y_ref, w_ref)
        if nj == 1:
            epilogue(part, ex, outs, i, nt)
            return
        acc = refs[-1]

        @pl.when(j == 0)
        def _():
            acc[...] = part

        @pl.when(j > 0)
        def _():
            acc[...] += part

        @pl.when(j == nj - 1)
        def _():
            epilogue(acc[...], ex, outs, i, nt)

    return pl.pallas_call(
        body, name=name, grid=(nt, nj),
        in_specs=[y_spec, w_spec] + [spec for _, spec in extra],
        out_specs=out_specs, out_shape=out_shapes,
        scratch_shapes=[] if nj == 1 else [pltpu.VMEM((tm, n_out), F32)],
        compiler_params=_cparams(2),
    )(ya, wa, *[arr for arr, _ in extra])


def _norm_bwd(dn, hv, gam):
    r = _rms_scale(hv)
    hh = hv * r
    t1 = dn * gam
    return r * (t1 - hh * jnp.mean(t1 * hh, axis=-1, keepdims=True)), dn * hh


def _norm_bwd_epilogue(has_res, out_dtype):
    def epilogue(dn, ex, outs, i, nt):
        dh, dg_rows = _norm_bwd(dn, ex[0][...], ex[1][...])
        _acc_rows(outs[1], dg_rows, i, nt)
        if has_res:
            dh = dh + ex[2][...]
        outs[0][...] = dh.astype(out_dtype)
        if has_res:
            outs[2][...] = dh.astype(BF16)

    return epilogue


def _mm_nt_norm_bwd(name, ya, y_spec, wa, w_spec, nj, h, gamma, res, out_dtype, mm_fn=None, want_tm=512, after=None):
    t, n = h.shape
    tm = _tile(t, want_tm)
    nt = t // tm
    row = BS((tm, n), lambda i, j: (i, 0))
    extra = [(h, row), (gamma, BS((1, n), lambda i, j: (0, 0)))]
    out_shapes = [SDS((t, n), out_dtype), SDS((8, n), F32)]
    out_specs = [row, BS((8, n), lambda i, j: (0, 0))]
    if res is not None:
        extra.append((res, row))
        out_shapes.append(SDS((t, n), BF16))
        out_specs.append(row)
    extra.extend((a, BS(memory_space=pl.ANY)) for a in after or ())
    return _mm_nt_epi(
        name, ya, y_spec, wa, w_spec, nj, n, extra, out_shapes, out_specs, _norm_bwd_epilogue(res is not None, out_dtype), tm, nt, mm_fn,
    )


def _dev_block(jj):
    return jj // 2 + NJ * (jj % 2)


def _ffn_dn_mm(y_ref, w_ref):
    acc = None
    for jj in range(2 * NJ):
        part = _dot_nt(y_ref[jj], w_ref[_dev_block(jj)])
        acc = part if acc is None else acc + part
    return acc


def _ffn_bwd(tag, dh, dhb, n_in, h_in, gamma, gu, a, w_in, w_out, collective_id, after):
    t, dn = dh.shape
    dgu = _ffn_bwd_in(f"{tag}_bwd_in", dhb, w_out, 0, gu).reshape(2 * NJ, t, FB)
    dw_out = _mm_tn(
        f"{tag}_dw_out", a, BS((None, t, FB), lambda j: (j, 0, 0)), dhb, BS((t, dn), lambda j: (0, 0)),
        (NJ, FB, dn), BS((None, FB, dn), lambda j: (j, 0, 0)), NJ, scale=0.5,
    )
    dw_in = _mm_tn(
        f"{tag}_dw_in", dgu, BS((None, t, FB), lambda j: (j, 0, 0)), n_in, BS((t, dn), lambda j: (0, 0)),
        (NDEV, FB, dn), BS((None, FB, dn), lambda j: (_dev_block(j), 0, 0)), NDEV,
    )
    entries = [("scatter", dw_in), ("scatter", dw_out.reshape(NDEV, NJ * FB // NDEV, dn))]
    landed = _exchange_sc(f"{tag}_reduce", entries, collective_id, after)
    tm = _tile(t)
    resident = BS((None, NDEV, dn, FB), lambda i, j: (0, 0, 0, 0), pipeline_mode=pl.Buffered(1))
    dh_in, dgam, dhb_in = _mm_nt_norm_bwd(
        f"{tag}_dn", dgu, BS((2 * NJ, tm, FB), lambda i, j: (0, i, 0)), w_in, resident, 1, h_in, gamma, dh, F32, mm_fn=_ffn_dn_mm,
        after=[e[1] for e in entries],
    )
    return dh_in, dhb_in, dgam, landed


def _heads_mm(y_ref, w_ref):
    acc = None
    for h in range(HEADS_B):
        part = _dot_nt(y_ref[h], w_ref[h])
        acc = part if acc is None else acc + part
    return acc


def _dqkv_mm(per):
    def mm(y_ref, w_ref):
        acc = None
        for j in range(NDEV):
            cols = [y_ref[(per * j + k) // 8, :, ((per * j + k) % 8) * 128 : ((per * j + k) % 8 + 1) * 128] for k in range(per)]
            part = _dot_nt(jnp.concatenate(cols, axis=1), w_ref[j])
            acc = part if acc is None else acc + part
        return acc

    return mm


def _kv_latent_bwd(dkv, w_up, ckr, latent_norm, dkr, c64, s64, p64, seq):
    t, wd = ckr.shape
    hb = w_up.shape[-1]
    tm = _tile(min(seq, 512), min(seq, 512))
    nt = t // tm
    nseq = seq // tm

    def epilogue(dn, ex, outs, i, nt_):
        dlat, dg_rows = _norm_bwd(dn, ex[0][...], ex[1][...])
        _acc_rows(outs[1], dg_rows, i, nt_)
        outs[0][:, :KV_LORA] = dlat.astype(BF16)
        outs[0][:, KV_LORA:] = _rope_bwd(ex[2][...], ex[3][...], ex[4][...], ex[5][...]).astype(BF16)

    pos = BS((tm, ROPE), lambda i, j: (i % nseq, 0))
    extra = [
        (ckr, BS((tm, KV_LORA), lambda i, j: (i, 0))), (latent_norm, BS((1, KV_LORA), lambda i, j: (0, 0))),
        (dkr, BS((tm, ROPE), lambda i, j: (i, 0))), (c64, pos), (s64, pos), (p64, BS((ROPE, ROPE), lambda i, j: (0, 0))),
    ]
    def heads_mm(y_ref, w_ref):
        acc = None
        for h in range(HEADS_B):
            part = _dot_nt(y_ref[:, h * hb : (h + 1) * hb], w_ref[h])
            acc = part if acc is None else acc + part
        return acc

    return _mm_nt_epi(
        "kv_latent_bwd", dkv, BS((tm, HEADS_B * hb), lambda i, j: (i, 0)), w_up, BS((HEADS_B, KV_LORA, hb), lambda i, j: (0, 0, 0)),
        1, KV_LORA, extra, [SDS((t, wd), BF16), SDS((8, KV_LORA), F32)],
        [BS((tm, wd), lambda i, j: (i, 0)), BS((8, KV_LORA), lambda i, j: (0, 0))], epilogue, tm, nt, heads_mm,
    )


def _adamw_step(g, w, m, v):
    nm = ADAM_B1 * m + (1.0 - ADAM_B1) * g
    nv = ADAM_B2 * v + (1.0 - ADAM_B2) * (g * g)
    m_hat = nm / (1.0 - ADAM_B1 ** ADAM_STEP)
    v_hat = nv / (1.0 - ADAM_B2 ** ADAM_STEP)
    return -ADAM_LR * (m_hat / (jnp.sqrt(v_hat) + ADAM_EPS) + ADAM_WD * w), nm, nv


def _adamw(name, parts, w, m, v):
    n_layers, rows, cols = w.shape
    tr = max(d for d in range(8, min(rows, 256) + 1, 8) if rows % d == 0)
    nb = rows // tr

    def body(*refs):
        p_refs = refs[:n_layers]
        w_ref, m_ref, v_ref, g_ref, d_ref, nm_ref, nv_ref = refs[n_layers : n_layers + 7]
        layer = pl.program_id(0)
        for lp in range(n_layers):

            @pl.when(layer == lp)
            def _():
                g = p_refs[lp][0].astype(F32)
                for k in range(1, NDEV):
                    g = g + p_refs[lp][k].astype(F32)
                g_ref[...] = g

        d_ref[...], nm_ref[...], nv_ref[...] = _adamw_step(g_ref[...], w_ref[...], m_ref[...], v_ref[...])

    def part_spec(lp):
        return BS((NDEV, tr, cols), lambda l, i: (0, jnp.where(l == lp, i, jnp.where(l < lp, 0, nb - 1)), 0))

    row = BS((None, tr, cols), lambda l, i: (l, i, 0))
    return pl.pallas_call(
        body, name=name, grid=(n_layers, nb),
        in_specs=[part_spec(lp) for lp in range(n_layers)] + [row, row, row],
        out_specs=[row] * 4, out_shape=[SDS(w.shape, F32)] * 4,
        compiler_params=_cparams(2),
    )(*parts, w, m, v)


def _pack_small(ffn1_norm, mix_norm, ffn2_norm, kv_norm, final_norm, q_norm, latent_norm, rel_bias, last_row):
    dn = ffn1_norm.shape[-1]

    def rows_of(a, n_rows):
        flat = a.reshape(-1)
        return jnp.pad(flat, (0, n_rows * dn - flat.shape[0])).reshape(n_rows, dn)

    return jnp.concatenate(
        [
            ffn1_norm.reshape(2, dn), mix_norm.reshape(2, dn), ffn2_norm.reshape(2, dn), kv_norm.reshape(1, dn),
            final_norm.reshape(1, dn), rows_of(q_norm, 1), rows_of(latent_norm, 1), rows_of(rel_bias, 5), rows_of(last_row, 1),
        ],
        axis=0,
    )


SMALL_PIECES = (
    ("ffn1_norm", 0, 2, None), ("mix_norm", 2, 2, None), ("ffn2_norm", 4, 2, None), ("kv_norm", 6, 1, None), ("final_norm", 7, 1, None),
    ("b_q_norm", 8, 1, Q_LORA), ("kv_latent_norm", 9, 1, KV_LORA), ("a_rel_bias", 10, 5, None), ("last", 15, 1, None),
)


def _adamw_small(parts, w, m, v):
    dn = w.shape[1]

    def body(p_ref, w_ref, m_ref, v_ref, *outs):
        g = p_ref[0]
        for k in range(1, NDEV):
            g = g + p_ref[k]
        for kind, val in enumerate((g,) + _adamw_step(g, w_ref[...], m_ref[...], v_ref[...])):
            for k, (_, r0, nr, width) in enumerate(SMALL_PIECES):
                outs[kind * len(SMALL_PIECES) + k][...] = val[r0 : r0 + nr, : width or dn]

    shapes = [SDS((nr, width or dn), F32) for _, _, nr, width in SMALL_PIECES] * 4
    outs = pl.pallas_call(
        body, name="adamw_small", grid=(1,),
        in_specs=[BS(parts.shape, lambda i: (0, 0, 0))] + [BS(w.shape, lambda i: (0, 0))] * 3,
        out_specs=[BS(s.shape, lambda i: (0, 0)) for s in shapes], out_shape=shapes,
        compiler_params=_cparams(1),
    )(parts, w, m, v)
    n = len(SMALL_PIECES)
    return [{name: outs[kind * n + k] for k, (name, _, _, _) in enumerate(SMALL_PIECES)} for kind in range(4)]


def kernel(x, ffn1_norm, ffn1_w_in, ffn1_w_out, mix_norm, ffn2_norm, ffn2_w_in, ffn2_w_out, a_w_qkv, a_rel_bias, a_w_o, kv_norm, kv_w_down, kv_latent_norm, kv_w_up, b_w_dq, b_q_norm, b_w_uq, b_w_o, final_norm, loss_target, m_ffn1_norm, m_ffn1_w_in, m_ffn1_w_out, m_mix_norm, m_ffn2_norm, m_ffn2_w_in, m_ffn2_w_out, m_a_w_qkv, m_a_rel_bias, m_a_w_o, m_kv_norm, m_kv_w_down, m_kv_latent_norm, m_kv_w_up, m_b_w_dq, m_b_q_norm, m_b_w_uq, m_b_w_o, m_final_norm, v_ffn1_norm, v_ffn1_w_in, v_ffn1_w_out, v_mix_norm, v_ffn2_norm, v_ffn2_w_in, v_ffn2_w_out, v_a_w_qkv, v_a_rel_bias, v_a_w_o, v_kv_norm, v_kv_w_down, v_kv_latent_norm, v_kv_w_up, v_b_w_dq, v_b_q_norm, v_b_w_uq, v_b_w_o, v_final_norm):
    bl, seq, dn = x.shape
    t = bl * seq
    tm = _tile(t)
    nt = t // tm
    x2 = x.reshape(t, dn)
    target2 = loss_target.reshape(t, dn)

    def gathered(*ws):
        return [("gather", w.astype(BF16)) for w in ws]

    groups = [
        gathered(ffn1_w_in[0]), gathered(ffn1_w_out[0]), gathered(a_w_qkv[0], a_w_o[0]), gathered(ffn2_w_in[0], ffn2_w_out[0]),
        gathered(kv_w_down, kv_w_up), gathered(ffn1_w_in[1], ffn1_w_out[1]), gathered(b_w_dq[0], b_w_uq[0], b_w_o[0]),
        gathered(ffn2_w_in[1], ffn2_w_out[1]),
    ]
    ag = [_exchange_sc(f"gather_{k}", group, GATHER_IDS[k]) for k, group in enumerate(groups)]

    def as_w_in(w):
        return w.reshape(1, NDEV, dn, FB)

    def as_w_out(w):
        return w.reshape(1, NJ, FB, dn)

    c64, s64, p64, c192, s192, p192 = _rope_tables(seq)
    q_norm = b_q_norm.reshape(1, Q_LORA)
    latent_norm = kv_latent_norm.reshape(1, KV_LORA)
    bias = _window_bias(_rel_bias_fwd(a_rel_bias[0][:, 1:]))

    h0, h1, h2, n1, hn, n2, gu1, gu2, a1, a2, w_in1, w_in2, w_out1, w_out2 = ([None, None] for _ in range(14))
    h0[0] = x2
    (n1[0],) = _norm_fwd("norm_x", x2, ffn1_norm[0:1])
    w_in1[0] = as_w_in(ag[0][0])
    gu1[0], a1[0] = _ffn_in("ffn1_in_0", n1[0], w_in1[0], 0)
    w_out1[0] = as_w_out(ag[1][0])
    h1[0], hn[0] = _mm_res_norm("ffn1_out_0", a1[0], w_out1[0], 0, h0[0], mix_norm[0:1], 0.5)
    w_qkv, w_o_a = ag[2]
    qkv_wb = w_qkv.shape[-1]
    w_o_a = w_o_a.reshape(1, 1, dn, dn)
    qkv3 = _qkv_proj("qkv_proj", hn[0], w_qkv)
    o_a, lse_a = _attn_a_fwd(qkv3, bias, bl, seq)
    h2[0], n2[0] = _mm_res_norm("attn_a_out", o_a.reshape(1, t, dn), w_o_a, 0, h1[0], ffn2_norm[0:1], 1.0)
    w_in2[0], w_out2[0] = as_w_in(ag[3][0]), as_w_out(ag[3][1])
    gu2[0], a2[0] = _ffn_in("ffn2_in_0", n2[0], w_in2[0], 0)
    h0[1], hk, n1[1] = _mm_res_norm(
        "ffn2_out_0", a2[0], w_out2[0], 0, h2[0], jnp.concatenate([kv_norm.reshape(1, dn), ffn1_norm[1:2]], axis=0), 0.5
    )
    w_down, w_up = ag[4]
    w_down = w_down.reshape(dn, KV_LORA + ROPE)
    ckr, ckv, kr = _kv_down(hk, w_down, latent_norm, c64, s64, p64, seq)
    kv = _kv_up(ckv, w_up)
    w_in1[1], w_out1[1] = as_w_in(ag[5][0]), as_w_out(ag[5][1])
    gu1[1], a1[1] = _ffn_in("ffn1_in_1", n1[1], w_in1[1], 0)
    h1[1], hn[1] = _mm_res_norm("ffn1_out_1", a1[1], w_out1[1], 0, h0[1], mix_norm[1:2], 0.5)
    w_dq, w_uq, w_o_b = ag[6]
    w_dq = w_dq.reshape(dn, Q_LORA)
    w_o_b = w_o_b.reshape(1, 1, dn, dn)
    cq_pre, cq = _q_down(hn[1], w_dq, q_norm)
    q = _q_up(cq, w_uq, c192, s192, p192, seq)
    o_b, lse_b = _mla_fwd(q, kv, kr, bl, seq)
    h2[1], n2[1] = _mm_res_norm("attn_b_out", o_b.reshape(1, t, dn), w_o_b, 0, h1[1], ffn2_norm[1:2], 1.0)
    w_in2[1], w_out2[1] = as_w_in(ag[7][0]), as_w_out(ag[7][1])
    gu2[1], a2[1] = _ffn_in("ffn2_in_1", n2[1], w_in2[1], 0)
    (h_last,) = _mm_res_norm("ffn2_out_1", a2[1], w_out2[1], 0, h2[1], None, 0.5)
    dh, dhb, dg_final, loss_part = _loss_final(h_last, target2, final_norm.reshape(1, dn))

    dg_ffn1, dg_mix, dg_ffn2, rs_ffn1, rs_ffn2 = ([None, None] for _ in range(5))

    def whole(rows, cols):
        return BS((rows, cols), lambda j: (0, 0))

    def dw_rows(name, xa, ya):
        n = ya.shape[1]
        return _mm_tn(name, xa, whole(t, dn), ya, whole(t, n), (dn, n), whole(dn, n), 1).reshape(NDEV, dn // NDEV, n)

    dh, dhb, dg_ffn2[1], rs_ffn2[1] = _ffn_bwd(
        "ffn2_1", dh, dhb, n2[1], h2[1], ffn2_norm[1:2], gu2[1], a2[1], w_in2[1], w_out2[1], REDUCE_IDS[0], ()
    )
    do_b = _mm_nt_plain("attn_b_do", dhb, w_o_b.reshape(dn, dn))
    dw_o_b = dw_rows("attn_b_dwo", o_b, dhb)
    dq_pre, dkv, dkr = _mla_bwd(q, kv, kr, o_b, lse_b, do_b, c192, s192, p192, bl, seq)
    dw_uq = _mm_tn(
        "dw_uq", cq, whole(t, Q_LORA), dq_pre, BS((None, t, QK_B), lambda j: (j, 0, 0)),
        (HEADS_B, Q_LORA, QK_B), BS((None, Q_LORA, QK_B), lambda j: (j, 0, 0)), HEADS_B,
    )
    dcq_pre, dg_q = _mm_nt_norm_bwd(
        "dcq", dq_pre, BS((HEADS_B, tm, QK_B), lambda i, j: (0, i, 0)), w_uq, BS((HEADS_B, Q_LORA, QK_B), lambda i, j: (0, 0, 0)),
        1, cq_pre, q_norm, None, BF16, mm_fn=_heads_mm,
    )
    dw_dq = dw_rows("dw_dq", hn[1], dcq_pre)
    dh, dg_mix[1], dhb = _mm_nt_norm_bwd(
        "dhn_b", dcq_pre, BS((tm, Q_LORA), lambda i, j: (i, 0)), w_dq, BS((dn, Q_LORA), lambda i, j: (0, 0)),
        1, h1[1], mix_norm[1:2], dh, F32,
    )
    dh, dhb, dg_ffn1[1], rs_ffn1[1] = _ffn_bwd(
        "ffn1_1", dh, dhb, n1[1], h0[1], ffn1_norm[1:2], gu1[1], a1[1], w_in1[1], w_out1[1], REDUCE_IDS[1], rs_ffn2[1][:1]
    )
    dw_up = _mm_tn(
        "dw_up", ckv, whole(t, KV_LORA), dkv, BS((t, NOPE + V_DIM), lambda j: (0, j)),
        (HEADS_B, KV_LORA, NOPE + V_DIM), BS((None, KV_LORA, NOPE + V_DIM), lambda j: (j, 0, 0)), HEADS_B,
    )
    dckr, dg_latent = _kv_latent_bwd(dkv, w_up, ckr, latent_norm, dkr, c64, s64, p64, seq)
    dw_down = dw_rows("dw_down", hk, dckr)
    dh, dg_kv, dhb = _mm_nt_norm_bwd(
        "dhk", dckr, BS((tm, KV_LORA + ROPE), lambda i, j: (i, 0)), w_down, BS((dn, KV_LORA + ROPE), lambda i, j: (0, 0)),
        1, h0[1], kv_norm.reshape(1, dn), dh, F32,
    )
    dh, dhb, dg_ffn2[0], rs_ffn2[0] = _ffn_bwd(
        "ffn2_0", dh, dhb, n2[0], h2[0], ffn2_norm[0:1], gu2[0], a2[0], w_in2[0], w_out2[0], REDUCE_IDS[2], rs_ffn1[1][:1]
    )
    do_a = _mm_nt_plain("attn_a_do", dhb, w_o_a.reshape(dn, dn))
    dw_o_a = dw_rows("attn_a_dwo", o_a, dhb)
    dqkv3, dbias = _attn_a_bwd(qkv3, o_a, lse_a, do_a, bias, bl, seq)
    dw_qkv = _dw_qkv(hn[0], dqkv3, qkv_wb)
    mixer_grads = [dw_o_a, dw_qkv, dw_o_b, dw_uq, dw_dq, dw_up, dw_down]
    dh, dg_mix[0], dhb = _mm_nt_norm_bwd(
        "dhn_a", dqkv3, BS((3, tm, dn), lambda i, j: (0, i, 0)), w_qkv, BS((NDEV, dn, qkv_wb), lambda i, j: (0, 0, 0)),
        1, h1[0], mix_norm[0:1], dh, F32, mm_fn=_dqkv_mm(qkv_wb // 128), after=mixer_grads,
    )
    rs_mixers = _exchange_sc("mixers_reduce", [("scatter", g) for g in mixer_grads], REDUCE_IDS[3], rs_ffn2[0][:1])
    dh, dhb, dg_ffn1[0], rs_ffn1[0] = _ffn_bwd(
        "ffn1_0", dh, dhb, n1[0], h0[0], ffn1_norm[0:1], gu1[0], a1[0], w_in1[0], w_out1[0], REDUCE_IDS[4], rs_mixers[:1]
    )
    grad_x = dh.reshape(bl, seq, dn)
    dtable = jnp.pad(_rel_bias_bwd(_window_bias_bwd(dbias)), ((0, 0), (1, 0)))

    def update(name, parts, w, m, v):
        shape3 = (len(parts),) + w.shape[-2:]
        parts = [p.reshape((NDEV,) + shape3[1:]) for p in parts]
        outs = _adamw(name, parts, w.reshape(shape3), m.reshape(shape3), v.reshape(shape3))
        return [o.reshape(w.shape) for o in outs]

    res = {}
    r_in2_1, r_out2_1 = rs_ffn2[1]
    r_in1_1, r_out1_1 = rs_ffn1[1]
    r_in2_0, r_out2_0 = rs_ffn2[0]
    r_in1_0, r_out1_0 = rs_ffn1[0]
    r_o_a, r_qkv, r_o_b, r_uq, r_dq, r_up, r_down = rs_mixers
    def update_transposed(name, parts, w, m, v):
        outs = update(name, parts, *[jnp.swapaxes(a, 1, 2) for a in (w, m, v)])
        return [jnp.swapaxes(o, 1, 2) for o in outs]

    res["ffn2_w_in"] = update_transposed("adamw_ffn2_w_in", [r_in2_0, r_in2_1], ffn2_w_in, m_ffn2_w_in, v_ffn2_w_in)
    res["ffn2_w_out"] = update("adamw_ffn2_w_out", [r_out2_0, r_out2_1], ffn2_w_out, m_ffn2_w_out, v_ffn2_w_out)
    res["kv_w_down"] = update("adamw_kv_w_down", [r_down], kv_w_down, m_kv_w_down, v_kv_w_down)
    res["kv_w_up"] = update("adamw_kv_w_up", [r_up], kv_w_up, m_kv_w_up, v_kv_w_up)
    res["b_w_dq"] = update("adamw_b_w_dq", [r_dq], b_w_dq, m_b_w_dq, v_b_w_dq)
    res["b_w_uq"] = update("adamw_b_w_uq", [r_uq], b_w_uq, m_b_w_uq, v_b_w_uq)
    res["b_w_o"] = update("adamw_b_w_o", [r_o_b], b_w_o, m_b_w_o, v_b_w_o)
    res["a_w_qkv"] = update("adamw_a_w_qkv", [r_qkv], a_w_qkv, m_a_w_qkv, v_a_w_qkv)
    res["a_w_o"] = update("adamw_a_w_o", [r_o_a], a_w_o, m_a_w_o, v_a_w_o)

    small = _pack_small(
        jnp.stack([dg_ffn1[0][0], dg_ffn1[1][0]]), jnp.stack([dg_mix[0][0], dg_mix[1][0]]), jnp.stack([dg_ffn2[0][0], dg_ffn2[1][0]]),
        dg_kv[0], dg_final[0], dg_q[0], dg_latent[0], dtable, loss_part[0],
    )
    done = [r[1] for name, r in res.items() if name != "ffn2_w_in"]
    (r_small,) = _exchange("gather_small_grads", [("gather", small)], after=done)
    res["ffn1_w_in"] = update_transposed("adamw_ffn1_w_in", [r_in1_0, r_in1_1], ffn1_w_in, m_ffn1_w_in, v_ffn1_w_in)
    res["ffn1_w_out"] = update("adamw_ffn1_w_out", [r_out1_0, r_out1_1], ffn1_w_out, m_ffn1_w_out, v_ffn1_w_out)
    zero_row = jnp.zeros((dn,), F32)
    packs = [
        _pack_small(f1, mx, f2, kvn, fin, qn, lat, rel, zero_row)
        for f1, mx, f2, kvn, fin, qn, lat, rel in (
            (ffn1_norm, mix_norm, ffn2_norm, kv_norm, final_norm, b_q_norm, kv_latent_norm, a_rel_bias),
            (m_ffn1_norm, m_mix_norm, m_ffn2_norm, m_kv_norm, m_final_norm, m_b_q_norm, m_kv_latent_norm, m_a_rel_bias),
            (v_ffn1_norm, v_mix_norm, v_ffn2_norm, v_kv_norm, v_final_norm, v_b_q_norm, v_kv_latent_norm, v_a_rel_bias),
        )
    ]
    small_out = _adamw_small(r_small, *packs)
    for name in ("ffn1_norm", "mix_norm", "ffn2_norm", "b_q_norm"):
        res[name] = [so[name] for so in small_out]
    for name in ("kv_norm", "kv_latent_norm", "final_norm"):
        res[name] = [so[name].reshape(-1) for so in small_out]
    res["a_rel_bias"] = [so["a_rel_bias"].reshape(-1)[: HEADS_A * NREL].reshape(1, HEADS_A, NREL) for so in small_out]
    loss = small_out[0]["last"][0, 0]

    order = [
        "ffn1_norm", "ffn1_w_in", "ffn1_w_out", "mix_norm", "ffn2_norm", "ffn2_w_in", "ffn2_w_out", "a_w_qkv", "a_rel_bias",
        "a_w_o", "kv_norm", "kv_w_down", "kv_latent_norm", "kv_w_up", "b_w_dq", "b_q_norm", "b_w_uq", "b_w_o", "final_norm",
    ]
    return (loss, grad_x, *[res[n][0] for n in order], *[res[n][1] for n in order], *[res[n][2] for n in order], *[res[n][3] for n in order])
```

```python
import jax
import jax.numpy as jnp
import numpy as np
from jax import lax
from jax.experimental import pallas as pl
from jax.experimental.pallas import tpu as pltpu
from jax.experimental.pallas import tpu_sc as plsc

NDEV = 8
D_MODEL = 1024
D_FF = 2816
FB = 2 * D_FF // NDEV
NJ = D_FF // FB
CHUNK = 64
LEFT_CHUNKS = 8
PAD = LEFT_CHUNKS * CHUNK
BAND = PAD + CHUNK
CHUNKS_PER_STEP = 4
WINDOW = PAD + CHUNKS_PER_STEP * CHUNK
STEP_ROWS = CHUNKS_PER_STEP * 2 * CHUNK
MAX_REL = 128
NREL = 2 * MAX_REL + 1
NREL_USED = 256
HEADS_A = 16
HEADS_B = 8
NOPE = 128
ROPE = 64
QK_B = NOPE + ROPE
V_DIM = 128
Q_LORA = 768
KV_LORA = 256
ROPE_THETA = 10000.0
EPS = 1e-6
NEG_INF = -1e30
MLA_TQ = 256
MLA_TK_FWD = 256
MLA_TK_BWD = 1024
ADAM_LR = 0.001
ADAM_B1 = 0.9
ADAM_B2 = 0.999
ADAM_EPS = 1e-08
ADAM_WD = 0.01
ADAM_STEP = 10
PACK_ROWS = 16
GU_BUFFERS = 3
GATHER_IDS = tuple(range(1, 9))
REDUCE_IDS = tuple(range(9, 14))
VMEM_LIMIT_BYTES = 56 * 1024 * 1024

F32 = jnp.float32
BF16 = jnp.bfloat16
SDS = jax.ShapeDtypeStruct
BS = pl.BlockSpec
MESH = pl.DeviceIdType.MESH


def _cparams(n_axes):
    return pltpu.CompilerParams(dimension_semantics=("arbitrary",) * n_axes, vmem_limit_bytes=VMEM_LIMIT_BYTES)


def _tile(t, want=512):
    return want if t % want == 0 else t


def _dot(a, b):
    return jnp.dot(a, b, preferred_element_type=F32)


def _dot_nt(a, b):
    return lax.dot_general(a, b, (((1,), (1,)), ((), ())), preferred_element_type=F32)


def _dot_tn(a, b):
    return lax.dot_general(a, b, (((0,), (0,)), ((), ())), preferred_element_type=F32)


def _split3(a):
    hi = a.astype(BF16)
    rest = a - hi.astype(F32)
    mid = rest.astype(BF16)
    return hi, mid, (rest - mid.astype(F32)).astype(BF16)


def _dot_exact(a, onehot, transposed=False):
    ob = onehot.astype(BF16)
    dot = _dot_nt if transposed else _dot
    hi, mid, lo = _split3(a)
    return dot(hi, ob) + dot(mid, ob) + dot(lo, ob)


def _rms_scale(h):
    return lax.rsqrt(jnp.mean(h * h, axis=-1, keepdims=True) + EPS)


def _acc_rows(ref, val, step, n_steps):
    part = val.reshape(val.shape[0] // 8, 8, val.shape[1]).sum(axis=0)

    @pl.when(step == 0)
    def _():
        ref[...] = part

    @pl.when(step > 0)
    def _():
        ref[...] += part

    @pl.when(step == n_steps - 1)
    def _():
        ref[...] = jnp.broadcast_to(jnp.sum(ref[...], axis=0, keepdims=True), ref.shape)


def _exchange_plan(entries):
    ins = [e[1] for e in entries]
    kinds = [e[0] for e in entries]
    lands = [SDS((NDEV,) + a.shape if k == "gather" else a.shape, a.dtype) for k, a in zip(kinds, ins)]
    return ins, lands, kinds


def _mesh_place():
    x, y, c = lax.axis_index("x"), lax.axis_index("y"), lax.axis_index("c")
    return (x, y, c), 4 * x + 2 * y + c


def _flipped(place, p):
    x, y, c = place
    px = 1 - x if p & 4 else x
    py = 1 - y if p & 2 else y
    pc = 1 - c if p & 1 else c
    return (px, py, pc), 4 * px + 2 * py + pc


def _ends(kind, src_ref, land_ref, origin, target):
    if kind == "gather":
        return src_ref, land_ref.at[origin]
    return src_ref.at[target], land_ref.at[origin]


def _remote(kind, src_ref, land_ref, send_sems, recv_sems, k, p, place, me, arriving):
    peer_pos, peer = _flipped(place, p)
    src, dst = _ends(kind, src_ref, land_ref, me, peer)
    if arriving:
        dst = _ends(kind, src_ref, land_ref, peer, me)[1]
    sem = k * (NDEV - 1) + p - 1
    return pltpu.make_async_remote_copy(
        src_ref=src, dst_ref=dst, send_sem=send_sems.at[sem], recv_sem=recv_sems.at[sem], device_id=peer_pos, device_id_type=MESH,
    )


def _exchange(name, entries, after=()):
    ins, lands, kinds = _exchange_plan(entries)
    n = len(ins)
    after = tuple(after)

    def body(*refs):
        refs = refs[:n] + refs[n + len(after) :]
        in_refs, land_refs = refs[:n], refs[n : 2 * n]
        send_sems, recv_sems, local_sems = refs[2 * n :]
        place, me = _mesh_place()
        local = []
        for k in range(n):
            src, dst = _ends(kinds[k], in_refs[k], land_refs[k], me, me)
            local.append(pltpu.make_async_copy(src, dst, local_sems.at[k]))
            local[-1].start()
        sends = []
        for p in range(1, NDEV):
            for k in range(n):
                sends.append(_remote(kinds[k], in_refs[k], land_refs[k], send_sems, recv_sems, k, p, place, me, False))
                sends[-1].start()
        for p in range(1, NDEV):
            for k in range(n):
                _remote(kinds[k], in_refs[k], land_refs[k], send_sems, recv_sems, k, p, place, me, True).wait_recv()
        for cp in sends:
            cp.wait_send()
        for cp in local:
            cp.wait()

    any_spec = BS(memory_space=pl.ANY)
    return pl.pallas_call(
        body, name=name, out_shape=lands, in_specs=[any_spec] * (n + len(after)), out_specs=[any_spec] * n,
        scratch_shapes=[
            pltpu.SemaphoreType.DMA((n * (NDEV - 1),)), pltpu.SemaphoreType.DMA((n * (NDEV - 1),)), pltpu.SemaphoreType.DMA((n,)),
        ],
    )(*ins, *after)


def _exchange_sc(name, entries, collective_id, after=()):
    ins, lands, kinds = _exchange_plan(entries)
    n = len(ins)
    after = tuple(after)

    def launch(*refs):
        refs = refs[:n] + refs[n + len(after) :]
        in_refs, land_refs = refs[:n], refs[n : 2 * n]
        send_sems, recv_sems, local_sems = refs[2 * n :]
        place, me = _mesh_place()
        barrier = pltpu.get_barrier_semaphore()
        for p in range(1, NDEV):
            pl.semaphore_signal(barrier, inc=1, device_id=_flipped(place, p)[0], device_id_type=MESH)
        pl.semaphore_wait(barrier, NDEV - 1)
        local = []
        for k in range(n):
            src, dst = _ends(kinds[k], in_refs[k], land_refs[k], me, me)
            local.append(pltpu.make_async_copy(src, dst, local_sems.at[k]))
            local[-1].start()
        sends = []
        if all(kind == "gather" for kind in kinds):
            for p in (1, 2, 4, 6):
                for k in range(n):
                    sends.append(_remote(kinds[k], in_refs[k], land_refs[k], send_sems, recv_sems, k, p, place, me, False))
                    sends[-1].start()
            sibling_pos, _ = _flipped(place, 1)
            for f in (2, 4, 6):
                _, origin = _flipped(place, f)
                for k in range(n):
                    _remote(kinds[k], in_refs[k], land_refs[k], send_sems, recv_sems, k, f, place, me, True).wait_recv()
                    sem = k * (NDEV - 1) + f
                    sends.append(
                        pltpu.make_async_remote_copy(
                            src_ref=land_refs[k].at[origin], dst_ref=land_refs[k].at[origin], send_sem=send_sems.at[sem],
                            recv_sem=recv_sems.at[sem], device_id=sibling_pos, device_id_type=MESH,
                        )
                    )
                    sends[-1].start()
            for p in (1, 3, 5, 7):
                for k in range(n):
                    _remote(kinds[k], in_refs[k], land_refs[k], send_sems, recv_sems, k, p, place, me, True).wait_recv()
        else:
            for p in range(1, NDEV):
                for k in range(n):
                    sends.append(_remote(kinds[k], in_refs[k], land_refs[k], send_sems, recv_sems, k, p, place, me, False))
                    sends[-1].start()
            for p in range(1, NDEV):
                for k in range(n):
                    _remote(kinds[k], in_refs[k], land_refs[k], send_sems, recv_sems, k, p, place, me, True).wait_recv()
        for cp in sends:
            cp.wait_send()
        for cp in local:
            cp.wait()

    return pl.kernel(
        launch, out_type=tuple(lands), mesh=plsc.ScalarSubcoreMesh(axis_name="sequencer", num_cores=1), name=name,
        scratch_types=(
            pltpu.SemaphoreType.DMA((n * (NDEV - 1),)), pltpu.SemaphoreType.DMA((n * (NDEV - 1),)), pltpu.SemaphoreType.DMA((n,)),
        ),
        compiler_params=pltpu.CompilerParams(collective_id=collective_id),
    )(*ins, *after)


def _norm_fwd(name, h, gammas):
    t, dn = h.shape
    ng = gammas.shape[0]
    tm = _tile(t)

    def body(h_ref, g_ref, *outs):
        hv = h_ref[...]
        hh = hv * _rms_scale(hv)
        for i, o_ref in enumerate(outs):
            o_ref[...] = (hh * g_ref[i : i + 1, :]).astype(BF16)

    row = BS((tm, dn), lambda i: (i, 0))
    return pl.pallas_call(
        body, name=name, grid=(t // tm,),
        in_specs=[row, BS((ng, dn), lambda i: (0, 0))],
        out_specs=[row] * ng, out_shape=[SDS((t, dn), BF16)] * ng,
        compiler_params=_cparams(1),
    )(h, gammas)


def _ffn_in(name, n, w_in, layer):
    t, dn = n.shape
    tm = _tile(t, 1024)

    def body(n_ref, wg_ref, wu_ref, gu_ref, a_ref):
        xv = n_ref[...]
        g = _dot(xv, wg_ref[...])
        u = _dot(xv, wu_ref[...])
        gu_ref[0] = g.astype(BF16)
        gu_ref[1] = u.astype(BF16)
        a_ref[...] = (g * jax.nn.sigmoid(g) * u).astype(BF16)

    return pl.pallas_call(
        body, name=name, grid=(NJ, t // tm),
        in_specs=[
            BS((tm, dn), lambda j, i: (i, 0)),
            BS((None, None, dn, FB), lambda j, i: (layer, j, 0, 0)),
            BS((None, None, dn, FB), lambda j, i: (layer, j + NJ, 0, 0)),
        ],
        out_specs=[BS((None, 2, tm, FB), lambda j, i: (j, 0, i, 0)), BS((None, tm, FB), lambda j, i: (j, i, 0))],
        out_shape=[SDS((NJ, 2, t, FB), BF16), SDS((NJ, t, FB), BF16)],
        compiler_params=_cparams(2),
    )(n, w_in, w_in)


def _mm_res_norm(name, a, w, layer, h_in, gammas, scale):
    nk, t, kb = a.shape
    dn = w.shape[-1]
    ng = 0 if gammas is None else gammas.shape[0]
    tm = _tile(t)

    def body(*refs):
        a_ref, w_ref, h_ref = refs[:3]
        g_ref = refs[3] if ng else None
        outs = refs[3 + (1 if ng else 0) :]
        acc = _dot(a_ref[0], w_ref[0])
        for k in range(1, nk):
            acc += _dot(a_ref[k], w_ref[k])
        ho = h_ref[...] + scale * acc
        outs[0][...] = ho
        if ng:
            hh = ho * _rms_scale(ho)
            for i in range(ng):
                outs[1 + i][...] = (hh * g_ref[i : i + 1, :]).astype(BF16)

    row = BS((tm, dn), lambda i: (i, 0))
    in_specs = [BS((nk, tm, kb), lambda i: (0, i, 0)), BS((None, nk, kb, dn), lambda i: (layer, 0, 0, 0)), row]
    args = [a, w, h_in]
    if ng:
        in_specs.append(BS((ng, dn), lambda i: (0, 0)))
        args.append(gammas)
    return pl.pallas_call(
        body, name=name, grid=(t // tm,),
        in_specs=in_specs,
        out_specs=[row] * (1 + ng), out_shape=[SDS((t, dn), F32)] + [SDS((t, dn), BF16)] * ng,
        compiler_params=_cparams(1),
    )(*args)


def _qkv_proj(name, hn, w_qkv):
    t, dn = hn.shape
    wb = w_qkv.shape[-1]
    per = wb // 128
    tm = _tile(t)

    def body(x_ref, w_ref, o_ref):
        xv = x_ref[...]
        for j in range(NDEV):
            yv = _dot(xv, w_ref[j]).astype(BF16)
            for i in range(per):
                n = per * j + i
                o_ref[n // 8, :, (n % 8) * 128 : (n % 8 + 1) * 128] = yv[:, i * 128 : (i + 1) * 128]

    return pl.pallas_call(
        body, name=name, grid=(t // tm,),
        in_specs=[BS((tm, dn), lambda i: (i, 0)), BS((NDEV, dn, wb), lambda i: (0, 0, 0))],
        out_specs=BS((3, tm, dn), lambda i: (0, i, 0)), out_shape=SDS((3, t, dn), BF16),
        compiler_params=_cparams(1),
    )(hn, w_qkv)


def _rel_onehot(i):
    r = lax.broadcasted_iota(jnp.int32, (NREL_USED, BAND), 0)
    j = lax.broadcasted_iota(jnp.int32, (NREL_USED, BAND), 1)
    idx = jnp.clip(PAD + i - j, -MAX_REL, MAX_REL) + MAX_REL
    return (idx - 1 == r).astype(BF16)


def _rel_bias_fwd(table):
    def body(t_ref, o_ref):
        i8 = pl.program_id(0)
        for ii in range(8):
            o_ref[:, ii, :] = _dot_exact(t_ref[...], _rel_onehot(i8 * 8 + ii))

    return pl.pallas_call(
        body, name="rel_bias_fwd", grid=(CHUNK // 8,),
        in_specs=[BS((HEADS_A, NREL_USED), lambda i: (0, 0))],
        out_specs=BS((HEADS_A, 8, BAND), lambda i: (0, i, 0)), out_shape=SDS((HEADS_A, CHUNK, BAND), F32),
        compiler_params=_cparams(1),
    )(table)


def _rel_bias_bwd(dbias):
    def body(d_ref, o_ref):
        i8 = pl.program_id(0)
        acc = jnp.zeros((HEADS_A, NREL_USED), F32)
        for ii in range(8):
            acc += _dot_exact(d_ref[:, ii, :], _rel_onehot(i8 * 8 + ii), transposed=True)

        @pl.when(i8 == 0)
        def _():
            o_ref[...] = acc

        @pl.when(i8 > 0)
        def _():
            o_ref[...] += acc

    return pl.pallas_call(
        body, name="rel_bias_bwd", grid=(CHUNK // 8,),
        in_specs=[BS((HEADS_A, 8, BAND), lambda i: (0, i, 0))],
        out_specs=BS((HEADS_A, NREL_USED), lambda i: (0, 0)), out_shape=SDS((HEADS_A, NREL_USED), F32),
        compiler_params=_cparams(1),
    )(dbias)


def _window_bias(bias):
    b = bias.reshape(HEADS_A // 2, 2, CHUNK, BAND)
    per_chunk = [
        jnp.pad(b, ((0, 0), (0, 0), (0, 0), (cc * CHUNK, WINDOW - BAND - cc * CHUNK)), constant_values=NEG_INF)
        for cc in range(CHUNKS_PER_STEP)
    ]
    return jnp.stack(per_chunk, axis=1).reshape(HEADS_A // 2, STEP_ROWS, WINDOW)


def _window_bias_bwd(dwin):
    d = dwin.reshape(HEADS_A // 2, CHUNKS_PER_STEP, 2, CHUNK, WINDOW)
    return sum(d[:, cc, :, :, cc * CHUNK : cc * CHUNK + BAND] for cc in range(CHUNKS_PER_STEP)).reshape(HEADS_A, CHUNK, BAND)


def _step_rows(xs, lane):
    parts = []
    for cc in range(CHUNKS_PER_STEP):
        xc = xs[cc * CHUNK : (cc + 1) * CHUNK]
        parts.append(jnp.where(lane < 64, xc, jnp.zeros_like(xc)))
        parts.append(jnp.where(lane >= 64, xc, jnp.zeros_like(xc)))
    return jnp.concatenate(parts, axis=0)


def _pair_rows(ys, lane):
    parts = []
    for cc in range(CHUNKS_PER_STEP):
        y0 = ys[(2 * cc) * CHUNK : (2 * cc + 1) * CHUNK]
        y1 = ys[(2 * cc + 1) * CHUNK : (2 * cc + 2) * CHUNK]
        parts.append(jnp.where(lane < 64, y0, y1))
    return jnp.concatenate(parts, axis=0)


def _window_scores(q_rows, kwin, bias_win, first_key):
    s = _dot_nt(q_rows, kwin) * (CHUNK ** -0.5) + bias_win
    if first_key is None:
        return s
    col = lax.broadcasted_iota(jnp.int32, s.shape, 1)
    return jnp.where(col >= first_key, s, NEG_INF)


def _window_loop(n_passes, chunks):
    n_padded = min(PAD // (CHUNKS_PER_STEP * CHUNK), n_passes)
    lax.fori_loop(0, n_padded, lambda it, carry: chunks(it, carry, True), 0, unroll=2)
    if n_passes > n_padded:
        lax.fori_loop(n_padded, n_passes, lambda it, carry: chunks(it, carry, False), 0, unroll=2)


def _attn_a_fwd(qkv3, bias_win, bl, seq):
    t, dn = qkv3.shape[1:]
    npair = dn // 128
    step = CHUNKS_PER_STEP * CHUNK

    def body(q_ref, k_ref, v_ref, b_ref, o_ref, lse_ref, kpad, vpad):
        kpad[0:PAD, :] = jnp.zeros((PAD, 128), BF16)
        vpad[0:PAD, :] = jnp.zeros((PAD, 128), BF16)
        kpad[PAD:, :] = k_ref[...]
        vpad[PAD:, :] = v_ref[...]
        lane = lax.broadcasted_iota(jnp.int32, (CHUNK, 128), 1)

        def chunks(it, carry, padded):
            r0 = pl.multiple_of(it * step, step)
            q_rows = _step_rows(q_ref[pl.ds(r0, step), :], lane)
            s = _window_scores(q_rows, kpad[pl.ds(r0, WINDOW), :], b_ref[...], PAD - r0 if padded else None)
            m = jnp.max(s, axis=-1, keepdims=True)
            e = jnp.exp(s - m)
            total = jnp.sum(e, axis=-1, keepdims=True)
            o_rows = _dot(e.astype(BF16), vpad[pl.ds(r0, WINDOW), :]) * (1.0 / total)
            o_ref[pl.ds(r0, step), :] = _pair_rows(o_rows, lane).astype(BF16)
            lse_ref[pl.ds(pl.multiple_of(it * STEP_ROWS, STEP_ROWS), STEP_ROWS), :] = m + jnp.log(total)
            return carry

        _window_loop(seq // step, chunks)

    return pl.pallas_call(
        body, name="attn_a_fwd", grid=(bl, npair),
        in_specs=[
            BS((None, seq, 128), lambda b, h: (0, b, h)),
            BS((None, seq, 128), lambda b, h: (1, b, h)),
            BS((None, seq, 128), lambda b, h: (2, b, h)),
            BS((None, STEP_ROWS, WINDOW), lambda b, h: (h, 0, 0)),
        ],
        out_specs=[BS((seq, 128), lambda b, h: (b, h)), BS((None, 2 * seq, 1), lambda b, h: (h, b, 0))],
        out_shape=[SDS((t, dn), BF16), SDS((npair, 2 * t, 1), F32)],
        scratch_shapes=[pltpu.VMEM((PAD + seq, 128), BF16), pltpu.VMEM((PAD + seq, 128), BF16)],
        compiler_params=_cparams(2),
    )(qkv3, qkv3, qkv3, bias_win)


def _attn_a_bwd(qkv3, out, lse, do, bias_win, bl, seq):
    t, dn = qkv3.shape[1:]
    npair = dn // 128
    step = CHUNKS_PER_STEP * CHUNK

    def body(q_ref, k_ref, v_ref, o_ref, lse_ref, do_ref, b_ref, dqkv_ref, db_ref, kpad, vpad, dkacc, dvacc):
        b = pl.program_id(1)
        kpad[0:PAD, :] = jnp.zeros((PAD, 128), BF16)
        vpad[0:PAD, :] = jnp.zeros((PAD, 128), BF16)
        kpad[PAD:, :] = k_ref[...]
        vpad[PAD:, :] = v_ref[...]
        dkacc[...] = jnp.zeros_like(dkacc)
        dvacc[...] = jnp.zeros_like(dvacc)

        @pl.when(b == 0)
        def _():
            db_ref[...] = jnp.zeros_like(db_ref)

        lane = lax.broadcasted_iota(jnp.int32, (CHUNK, 128), 1)

        def chunks(it, carry, padded):
            r0 = pl.multiple_of(it * step, step)
            q_rows = _step_rows(q_ref[pl.ds(r0, step), :], lane)
            do_rows = _step_rows(do_ref[pl.ds(r0, step), :], lane)
            kwin = kpad[pl.ds(r0, WINDOW), :]
            vwin = vpad[pl.ds(r0, WINDOW), :]
            o_rows = _step_rows(o_ref[pl.ds(r0, step), :], lane)
            delta = jnp.sum(do_rows.astype(F32) * o_rows.astype(F32), axis=-1, keepdims=True)
            lse_rows = lse_ref[pl.ds(pl.multiple_of(it * STEP_ROWS, STEP_ROWS), STEP_ROWS), :]
            p = jnp.exp(_window_scores(q_rows, kwin, b_ref[...], PAD - r0 if padded else None) - lse_rows)
            ds = p * (_dot_nt(do_rows, vwin) - delta)
            db_ref[...] += ds
            dsb = (ds * (CHUNK ** -0.5)).astype(BF16)
            dqkv_ref[0, pl.ds(r0, step), :] = _pair_rows(_dot(dsb, kwin), lane).astype(BF16)
            dkacc[pl.ds(r0, WINDOW), :] += _dot_tn(dsb, q_rows)
            dvacc[pl.ds(r0, WINDOW), :] += _dot_tn(p.astype(BF16), do_rows)
            return carry

        _window_loop(seq // step, chunks)
        dqkv_ref[1] = dkacc[PAD:, :].astype(BF16)
        dqkv_ref[2] = dvacc[PAD:, :].astype(BF16)

    return pl.pallas_call(
        body, name="attn_a_bwd", grid=(npair, bl),
        in_specs=[
            BS((None, seq, 128), lambda h, b: (0, b, h)),
            BS((None, seq, 128), lambda h, b: (1, b, h)),
            BS((None, seq, 128), lambda h, b: (2, b, h)),
            BS((seq, 128), lambda h, b: (b, h)),
            BS((None, 2 * seq, 1), lambda h, b: (h, b, 0)),
            BS((seq, 128), lambda h, b: (b, h)),
            BS((None, STEP_ROWS, WINDOW), lambda h, b: (h, 0, 0)),
        ],
        out_specs=[BS((3, seq, 128), lambda h, b: (0, b, h)), BS((None, STEP_ROWS, WINDOW), lambda h, b: (h, 0, 0))],
        out_shape=[SDS((3, t, dn), BF16), SDS((HEADS_A // 2, STEP_ROWS, WINDOW), F32)],
        scratch_shapes=[
            pltpu.VMEM((PAD + seq, 128), BF16), pltpu.VMEM((PAD + seq, 128), BF16),
            pltpu.VMEM((PAD + seq, 128), F32), pltpu.VMEM((PAD + seq, 128), F32),
        ],
        compiler_params=_cparams(2),
    )(qkv3, qkv3, qkv3, out, lse, do, bias_win)


def _rope_tables(seq):
    half = ROPE // 2
    freqs = ROPE_THETA ** (-jnp.arange(half, dtype=F32) / half)
    ang = jnp.arange(seq, dtype=F32)[:, None] * freqs[None, :]
    cos, sin = jnp.cos(ang), jnp.sin(ang)
    c64 = jnp.concatenate([cos, cos], axis=1)
    s64 = jnp.concatenate([-sin, sin], axis=1)
    c192 = jnp.concatenate([jnp.ones((seq, NOPE), F32), c64], axis=1)
    s192 = jnp.concatenate([jnp.zeros((seq, NOPE), F32), s64], axis=1)
    p64 = np.zeros((ROPE, ROPE), np.float32)
    for col in range(ROPE):
        p64[(col + half) % ROPE, col] = 1.0
    p192 = np.zeros((QK_B, QK_B), np.float32)
    p192[NOPE:, NOPE:] = p64
    return c64, s64, jnp.asarray(p64), c192, s192, jnp.asarray(p192)


def _rope(xv, cos, sin_signed, swap):
    return xv * cos + _dot_exact(xv, swap) * sin_signed


def _rope_bwd(dy, cos, sin_signed, swap):
    return dy * cos + _dot_exact(dy * sin_signed, swap)


def _q_down(hn, w_dq, q_norm):
    t, dn = hn.shape
    ql = w_dq.shape[1]
    tm = _tile(t)

    def body(x_ref, w_ref, g_ref, pre_ref, cq_ref):
        pre = _dot(x_ref[...], w_ref[...])
        pre_ref[...] = pre
        cq_ref[...] = (pre * _rms_scale(pre) * g_ref[...]).astype(BF16)

    return pl.pallas_call(
        body, name="q_down", grid=(t // tm,),
        in_specs=[BS((tm, dn), lambda i: (i, 0)), BS((dn, ql), lambda i: (0, 0)), BS((1, ql), lambda i: (0, 0))],
        out_specs=[BS((tm, ql), lambda i: (i, 0))] * 2, out_shape=[SDS((t, ql), F32), SDS((t, ql), BF16)],
        compiler_params=_cparams(1),
    )(hn, w_dq, q_norm)


def _q_up(cq, w_uq, c192, s192, p192, seq):
    t, ql = cq.shape
    tm = _tile(min(seq, 512), min(seq, 512))
    nseq = seq // tm

    def body(x_ref, w_ref, c_ref, s_ref, p_ref, o_ref):
        xv = x_ref[...]
        for h in range(HEADS_B):
            o_ref[h] = _rope(_dot(xv, w_ref[h]), c_ref[...], s_ref[...], p_ref[...]).astype(BF16)

    pos = BS((tm, QK_B), lambda i: (i % nseq, 0))
    return pl.pallas_call(
        body, name="q_up", grid=(t // tm,),
        in_specs=[
            BS((tm, ql), lambda i: (i, 0)), BS((HEADS_B, ql, QK_B), lambda i: (0, 0, 0)), pos, pos,
            BS((QK_B, QK_B), lambda i: (0, 0)),
        ],
        out_specs=BS((HEADS_B, tm, QK_B), lambda i: (0, i, 0)), out_shape=SDS((HEADS_B, t, QK_B), BF16),
        compiler_params=_cparams(1),
    )(cq, w_uq, c192, s192, p192)


def _kv_down(hk, w_down, latent_norm, c64, s64, p64, seq):
    t, dn = hk.shape
    wd = w_down.shape[1]
    tm = _tile(min(seq, 512), min(seq, 512))
    nseq = seq // tm

    def body(x_ref, w_ref, g_ref, c_ref, s_ref, p_ref, ckr_ref, ckv_ref, kr_ref):
        ckr = _dot(x_ref[...], w_ref[...])
        ckr_ref[...] = ckr
        lat = ckr[:, :KV_LORA]
        ckv_ref[...] = (lat * _rms_scale(lat) * g_ref[...]).astype(BF16)
        kr_ref[...] = _rope(ckr[:, KV_LORA:], c_ref[...], s_ref[...], p_ref[...]).astype(BF16)

    pos = BS((tm, ROPE), lambda i: (i % nseq, 0))
    return pl.pallas_call(
        body, name="kv_down", grid=(t // tm,),
        in_specs=[
            BS((tm, dn), lambda i: (i, 0)), BS((dn, wd), lambda i: (0, 0)), BS((1, KV_LORA), lambda i: (0, 0)), pos, pos,
            BS((ROPE, ROPE), lambda i: (0, 0)),
        ],
        out_specs=[BS((tm, wd), lambda i: (i, 0)), BS((tm, KV_LORA), lambda i: (i, 0)), BS((tm, ROPE), lambda i: (i, 0))],
        out_shape=[SDS((t, wd), F32), SDS((t, KV_LORA), BF16), SDS((t, ROPE), BF16)],
        compiler_params=_cparams(1),
    )(hk, w_down, latent_norm, c64, s64, p64)


def _kv_up(ckv, w_up):
    t, kl = ckv.shape
    hb = w_up.shape[-1]
    tm = _tile(t)

    def body(x_ref, w_ref, o_ref):
        xv = x_ref[...]
        for h in range(HEADS_B):
            o_ref[:, h * hb : (h + 1) * hb] = _dot(xv, w_ref[h]).astype(BF16)

    return pl.pallas_call(
        body, name="kv_up", grid=(t // tm,),
        in_specs=[BS((tm, kl), lambda i: (i, 0)), BS((HEADS_B, kl, hb), lambda i: (0, 0, 0))],
        out_specs=BS((tm, HEADS_B * hb), lambda i: (i, 0)), out_shape=SDS((t, HEADS_B * hb), BF16),
        compiler_params=_cparams(1),
    )(ckv, w_up)


def _mla_diagonal_mask(tq):
    rows = lax.broadcasted_iota(jnp.int32, (tq, tq), 0)
    cols = lax.broadcasted_iota(jnp.int32, (tq, tq), 1)
    return jnp.where(jnp.right_shift(cols, 6) <= jnp.right_shift(rows, 6), 0.0, NEG_INF)


def _mla_key_tiles(n_keys, tk):
    return [(slice(k0, min(k0 + tk, n_keys)), min(k0 + tk, n_keys) == n_keys) for k0 in range(0, n_keys, tk)]


def _mla_scores(qi, kt, diagonal):
    s = _dot_nt(qi, kt) * (QK_B ** -0.5)
    if diagonal is None:
        return s
    tq, width = s.shape
    own = s[:, width - tq :] + diagonal
    return own if width == tq else jnp.concatenate([s[:, : width - tq], own], axis=1)


def _mla_fwd(q, kv, kr, bl, seq):
    t = kv.shape[0]
    tq = min(MLA_TQ, seq)

    def body(q_ref, kn_ref, v_ref, kr_ref, o_ref, lse_ref):
        kcat = jnp.concatenate([kn_ref[...], kr_ref[...]], axis=1)
        vv = v_ref[...]
        diagonal = _mla_diagonal_mask(tq)
        for i in range(seq // tq):
            rows = slice(i * tq, (i + 1) * tq)
            qi = q_ref[rows, :]
            m = total = acc = None
            for keys, own in _mla_key_tiles((i + 1) * tq, MLA_TK_FWD):
                s = _mla_scores(qi, kcat[keys], diagonal if own else None)
                m_blk = jnp.max(s, axis=-1, keepdims=True)
                if m is None:
                    m_new = m_blk
                    e = jnp.exp(s - m_new)
                    total = jnp.sum(e, axis=-1, keepdims=True)
                    acc = _dot(e.astype(BF16), vv[keys])
                else:
                    m_new = jnp.maximum(m, m_blk)
                    keep = jnp.exp(m - m_new)
                    e = jnp.exp(s - m_new)
                    total = keep * total + jnp.sum(e, axis=-1, keepdims=True)
                    acc = keep * acc + _dot(e.astype(BF16), vv[keys])
                m = m_new
            o_ref[rows, :] = (acc / total).astype(BF16)
            lse_ref[rows, :] = m + jnp.log(total)

    return pl.pallas_call(
        body, name="mla_fwd", grid=(bl, HEADS_B),
        in_specs=[
            BS((None, seq, QK_B), lambda b, h: (h, b, 0)),
            BS((seq, NOPE), lambda b, h: (b, 2 * h)),
            BS((seq, V_DIM), lambda b, h: (b, 2 * h + 1)),
            BS((seq, ROPE), lambda b, h: (b, 0)),
        ],
        out_specs=[BS((seq, V_DIM), lambda b, h: (b, h)), BS((None, seq, 1), lambda b, h: (h, b, 0))],
        out_shape=[SDS((t, HEADS_B * V_DIM), BF16), SDS((HEADS_B, t, 1), F32)],
        compiler_params=_cparams(2),
    )(q, kv, kv, kr)


def _mla_bwd(q, kv, kr, o, lse, do, c192, s192, p192, bl, seq):
    t = kv.shape[0]
    tq = min(MLA_TQ, seq)

    def body(q_ref, kn_ref, v_ref, kr_ref, o_ref, lse_ref, do_ref, c_ref, s_ref, p_ref, dq_ref, dkv_ref, dkr_ref, dkacc, dvacc):
        h = pl.program_id(1)
        kcat = jnp.concatenate([kn_ref[...], kr_ref[...]], axis=1)
        vv = v_ref[...]
        dkacc[...] = jnp.zeros_like(dkacc)
        dvacc[...] = jnp.zeros_like(dvacc)
        diagonal = _mla_diagonal_mask(tq)
        for i in range(seq // tq):
            rows = slice(i * tq, (i + 1) * tq)
            qi = q_ref[rows, :]
            doi = do_ref[rows, :]
            lse_i = lse_ref[rows, :]
            delta = jnp.sum(doi.astype(F32) * o_ref[rows, :].astype(F32), axis=-1, keepdims=True)
            dq = None
            for keys, own in _mla_key_tiles((i + 1) * tq, MLA_TK_BWD):
                p = jnp.exp(_mla_scores(qi, kcat[keys], diagonal if own else None) - lse_i)
                ds = p * (_dot_nt(doi, vv[keys]) - delta)
                dsb = (ds * (QK_B ** -0.5)).astype(BF16)
                dq_blk = _dot(dsb, kcat[keys])
                dq = dq_blk if dq is None else dq + dq_blk
                dkacc[keys, :] += _dot_tn(dsb, qi)
                dvacc[keys, :] += _dot_tn(p.astype(BF16), doi)
            dq_ref[rows, :] = _rope_bwd(dq, c_ref[rows, :], s_ref[rows, :], p_ref[...]).astype(BF16)
        dk = dkacc[...]
        dkv_ref[:, :NOPE] = dk[:, :NOPE].astype(BF16)
        dkv_ref[:, NOPE:] = dvacc[...].astype(BF16)

        @pl.when(h == 0)
        def _():
            dkr_ref[...] = dk[:, NOPE:]

        @pl.when(h > 0)
        def _():
            dkr_ref[...] += dk[:, NOPE:]

    return pl.pallas_call(
        body, name="mla_bwd", grid=(bl, HEADS_B),
        in_specs=[
            BS((None, seq, QK_B), lambda b, h: (h, b, 0)),
            BS((seq, NOPE), lambda b, h: (b, 2 * h)),
            BS((seq, V_DIM), lambda b, h: (b, 2 * h + 1)),
            BS((seq, ROPE), lambda b, h: (b, 0)),
            BS((seq, V_DIM), lambda b, h: (b, h)),
            BS((None, seq, 1), lambda b, h: (h, b, 0)),
            BS((seq, V_DIM), lambda b, h: (b, h)),
            BS((seq, QK_B), lambda b, h: (0, 0)),
            BS((seq, QK_B), lambda b, h: (0, 0)),
            BS((QK_B, QK_B), lambda b, h: (0, 0)),
        ],
        out_specs=[
            BS((None, seq, QK_B), lambda b, h: (h, b, 0)),
            BS((seq, NOPE + V_DIM), lambda b, h: (b, h)),
            BS((seq, ROPE), lambda b, h: (b, 0)),
        ],
        out_shape=[SDS((HEADS_B, t, QK_B), BF16), SDS((t, HEADS_B * (NOPE + V_DIM)), BF16), SDS((t, ROPE), F32)],
        scratch_shapes=[pltpu.VMEM((seq, QK_B), F32), pltpu.VMEM((seq, V_DIM), F32)],
        compiler_params=_cparams(2),
    )(q, kv, kv, kr, o, lse, do, c192, s192, p192)


def _loss_final(h, target, gamma):
    t, dn = h.shape
    tm = _tile(t)
    nt = t // tm

    def body(h_ref, t_ref, g_ref, dh_ref, dhb_ref, dg_ref, loss_ref):
        i = pl.program_id(0)
        hv = h_ref[...]
        r = _rms_scale(hv)
        hh = hv * r
        gam = g_ref[...]
        err = hh * gam - t_ref[...]
        part = 0.5 * jnp.sum(jnp.mean(err * err, axis=-1, keepdims=True))

        @pl.when(i == 0)
        def _():
            loss_ref[...] = jnp.zeros_like(loss_ref)

        loss_ref[...] += part
        dy = err * (1.0 / dn)
        _acc_rows(dg_ref, dy * hh, i, nt)
        t1 = dy * gam
        dh = r * (t1 - hh * jnp.mean(t1 * hh, axis=-1, keepdims=True))
        dh_ref[...] = dh
        dhb_ref[...] = dh.astype(BF16)

    row = BS((tm, dn), lambda i: (i, 0))
    return pl.pallas_call(
        body, name="loss_final", grid=(nt,),
        in_specs=[row, row, BS((1, dn), lambda i: (0, 0))],
        out_specs=[row, row, BS((8, dn), lambda i: (0, 0)), BS((8, 128), lambda i: (0, 0))],
        out_shape=[SDS((t, dn), F32), SDS((t, dn), BF16), SDS((8, dn), F32), SDS((8, 128), F32)],
        compiler_params=_cparams(1),
    )(h, target, gamma)


def _ffn_bwd_in(name, dh, w_out, layer, gu):
    t, dn = dh.shape
    tm = _tile(t, 1024)
    nt = t // tm
    n_steps = NJ * nt

    def gu_fetch(gu_hbm, bufs, sems, step):
        rows = pl.ds(pl.multiple_of((step % nt) * tm, tm), tm)
        return pltpu.make_async_copy(gu_hbm.at[step // nt, :, rows, :], bufs.at[step % GU_BUFFERS], sems.at[step % GU_BUFFERS])

    def body(dh_ref, w_ref, gu_hbm, o_ref, bufs, sems):
        step = pl.program_id(0) * nt + pl.program_id(1)

        @pl.when(step == 0)
        def _():
            for first in range(min(GU_BUFFERS - 1, n_steps)):
                gu_fetch(gu_hbm, bufs, sems, first).start()

        @pl.when(step + GU_BUFFERS - 1 < n_steps)
        def _():
            gu_fetch(gu_hbm, bufs, sems, step + GU_BUFFERS - 1).start()

        gu_fetch(gu_hbm, bufs, sems, step).wait()
        slot = step % GU_BUFFERS
        da = 0.5 * _dot_nt(dh_ref[...], w_ref[...])
        g = bufs[slot, 0].astype(F32)
        u = bufs[slot, 1].astype(F32)
        sg = jax.nn.sigmoid(g)
        o_ref[0] = (da * u * (sg * (1.0 + g * (1.0 - sg)))).astype(BF16)
        o_ref[1] = (da * (g * sg)).astype(BF16)

    return pl.pallas_call(
        body, name=name, grid=(NJ, nt),
        in_specs=[
            BS((tm, dn), lambda j, i: (i, 0)), BS((None, None, FB, dn), lambda j, i: (layer, j, 0, 0)), BS(memory_space=pl.ANY),
        ],
        out_specs=BS((None, 2, tm, FB), lambda j, i: (j, 0, i, 0)), out_shape=SDS((NJ, 2, t, FB), BF16),
        scratch_shapes=[pltpu.VMEM((GU_BUFFERS, 2, tm, FB), BF16), pltpu.SemaphoreType.DMA((GU_BUFFERS,))],
        compiler_params=_cparams(2),
    )(dh, w_out, gu)


def _mm_nt_plain(name, xf, w):
    t, dn = xf.shape
    n = w.shape[0]
    tm = _tile(t)

    def body(x_ref, w_ref, o_ref):
        o_ref[...] = _dot_nt(x_ref[...], w_ref[...]).astype(BF16)

    return pl.pallas_call(
        body, name=name, grid=(t // tm,),
        in_specs=[BS((tm, dn), lambda i: (i, 0)), BS((n, dn), lambda i: (0, 0))],
        out_specs=BS((tm, n), lambda i: (i, 0)), out_shape=SDS((t, n), BF16),
        compiler_params=_cparams(1),
    )(xf, w)


def _mm_tn(name, xa, x_spec, ya, y_spec, out_shape, out_spec, nj, scale=None):
    def body(x_ref, y_ref, o_ref):
        acc = _dot_tn(x_ref[...], y_ref[...])
        o_ref[...] = (acc if scale is None else scale * acc).astype(BF16)

    return pl.pallas_call(
        body, name=name, grid=(nj,),
        in_specs=[x_spec, y_spec], out_specs=out_spec, out_shape=SDS(out_shape, BF16),
        compiler_params=_cparams(1),
    )(xa, ya)


def _dw_qkv(hn, dqkv3, wb):
    t, dn = hn.shape
    per = wb // 128

    def body(x_ref, *refs):
        cols = [y_ref[...] for y_ref in refs[:per]]
        refs[per][...] = _dot_tn(x_ref[...], jnp.concatenate(cols, axis=1)).astype(BF16)

    def piece(k):
        return BS((None, t, 128), lambda j: ((per * j + k) // 8, 0, (per * j + k) % 8))

    return pl.pallas_call(
        body, name="dw_qkv", grid=(NDEV,),
        in_specs=[BS((t, dn), lambda j: (0, 0))] + [piece(k) for k in range(per)],
        out_specs=BS((None, dn, wb), lambda j: (j, 0, 0)), out_shape=SDS((NDEV, dn, wb), BF16),
        compiler_params=_cparams(1),
    )(hn, *([dqkv3] * per))


def _mm_nt_epi(name, ya, y_spec, wa, w_spec, nj, n_out, extra, out_shapes, out_specs, epilogue, tm, nt, mm_fn=None):
    n_extra = len(extra)
    n_outs = len(out_shapes)

    def body(*refs):
        y_ref, w_ref = refs[:2]
        ex = refs[2 : 2 + n_extra]
        outs = refs[2 + n_extra : 2 + n_extra + n_outs]
        i = pl.program_id(0)
        j = pl.program_id(1)
        part = _dot_nt(y_ref[...], w_ref[...]) if mm_fn is None else mm_fn(y_ref, w_ref)
        if nj == 1:
            epilogue(part, ex, outs, i, nt)
            return
        acc = refs[-1]

        @pl.when(j == 0)
        def _():
            acc[...] = part

        @pl.when(j > 0)
        def _():
            acc[...] += part

        @pl.when(j == nj - 1)
        def _():
            epilogue(acc[...], ex, outs, i, nt)

    return pl.pallas_call(
        body, name=name, grid=(nt, nj),
        in_specs=[y_spec, w_spec] + [spec for _, spec in extra],
        out_specs=out_specs, out_shape=out_shapes,
        scratch_shapes=[] if nj == 1 else [pltpu.VMEM((tm, n_out), F32)],
        compiler_params=_cparams(2),
    )(ya, wa, *[arr for arr, _ in extra])


def _norm_bwd(dn, hv, gam):
    r = _rms_scale(hv)
    hh = hv * r
    t1 = dn * gam
    return r * (t1 - hh * jnp.mean(t1 * hh, axis=-1, keepdims=True)), dn * hh


def _norm_bwd_epilogue(has_res, out_dtype):
    def epilogue(dn, ex, outs, i, nt):
        dh, dg_rows = _norm_bwd(dn, ex[0][...], ex[1][...])
        _acc_rows(outs[1], dg_rows, i, nt)
        if has_res:
            dh = dh + ex[2][...]
        outs[0][...] = dh.astype(out_dtype)
        if has_res:
            outs[2][...] = dh.astype(BF16)

    return epilogue


def _mm_nt_norm_bwd(name, ya, y_spec, wa, w_spec, nj, h, gamma, res, out_dtype, mm_fn=None, want_tm=512, after=None):
    t, n = h.shape
    tm = _tile(t, want_tm)
    nt = t // tm
    row = BS((tm, n), lambda i, j: (i, 0))
    extra = [(h, row), (gamma, BS((1, n), lambda i, j: (0, 0)))]
    out_shapes = [SDS((t, n), out_dtype), SDS((8, n), F32)]
    out_specs = [row, BS((8, n), lambda i, j: (0, 0))]
    if res is not None:
        extra.append((res, row))
        out_shapes.append(SDS((t, n), BF16))
        out_specs.append(row)
    extra.extend((a, BS(memory_space=pl.ANY)) for a in after or ())
    return _mm_nt_epi(
        name, ya, y_spec, wa, w_spec, nj, n, extra, out_shapes, out_specs, _norm_bwd_epilogue(res is not None, out_dtype), tm, nt, mm_fn,
    )


def _dev_block(jj):
    return jj // 2 + NJ * (jj % 2)


def _ffn_dn_mm(y_ref, w_ref):
    acc = None
    for jj in range(2 * NJ):
        part = _dot_nt(y_ref[jj], w_ref[_dev_block(jj)])
        acc = part if acc is None else acc + part
    return acc


def _ffn_bwd(tag, dh, dhb, n_in, h_in, gamma, gu, a, w_in, w_out, collective_id, after):
    t, dn = dh.shape
    dgu = _ffn_bwd_in(f"{tag}_bwd_in", dhb, w_out, 0, gu).reshape(2 * NJ, t, FB)
    dw_out = _mm_tn(
        f"{tag}_dw_out", a, BS((None, t, FB), lambda j: (j, 0, 0)), dhb, BS((t, dn), lambda j: (0, 0)),
        (NJ, FB, dn), BS((None, FB, dn), lambda j: (j, 0, 0)), NJ, scale=0.5,
    )
    dw_in = _mm_tn(
        f"{tag}_dw_in", dgu, BS((None, t, FB), lambda j: (j, 0, 0)), n_in, BS((t, dn), lambda j: (0, 0)),
        (NDEV, FB, dn), BS((None, FB, dn), lambda j: (_dev_block(j), 0, 0)), NDEV,
    )
    entries = [("scatter", dw_in), ("scatter", dw_out.reshape(NDEV, NJ * FB // NDEV, dn))]
    landed = _exchange_sc(f"{tag}_reduce", entries, collective_id, after)
    tm = _tile(t)
    resident = BS((None, NDEV, dn, FB), lambda i, j: (0, 0, 0, 0), pipeline_mode=pl.Buffered(1))
    dh_in, dgam, dhb_in = _mm_nt_norm_bwd(
        f"{tag}_dn", dgu, BS((2 * NJ, tm, FB), lambda i, j: (0, i, 0)), w_in, resident, 1, h_in, gamma, dh, F32, mm_fn=_ffn_dn_mm,
        after=[e[1] for e in entries],
    )
    return dh_in, dhb_in, dgam, landed


def _heads_mm(y_ref, w_ref):
    acc = None
    for h in range(HEADS_B):
        part = _dot_nt(y_ref[h], w_ref[h])
        acc = part if acc is None else acc + part
    return acc


def _dqkv_mm(per):
    def mm(y_ref, w_ref):
        acc = None
        for j in range(NDEV):
            cols = [y_ref[(per * j + k) // 8, :, ((per * j + k) % 8) * 128 : ((per * j + k) % 8 + 1) * 128] for k in range(per)]
            part = _dot_nt(jnp.concatenate(cols, axis=1), w_ref[j])
            acc = part if acc is None else acc + part
        return acc

    return mm


def _kv_latent_bwd(dkv, w_up, ckr, latent_norm, dkr, c64, s64, p64, seq):
    t, wd = ckr.shape
    hb = w_up.shape[-1]
    tm = _tile(min(seq, 512), min(seq, 512))
    nt = t // tm
    nseq = seq // tm

    def epilogue(dn, ex, outs, i, nt_):
        dlat, dg_rows = _norm_bwd(dn, ex[0][...], ex[1][...])
        _acc_rows(outs[1], dg_rows, i, nt_)
        outs[0][:, :KV_LORA] = dlat.astype(BF16)
        outs[0][:, KV_LORA:] = _rope_bwd(ex[2][...], ex[3][...], ex[4][...], ex[5][...]).astype(BF16)

    pos = BS((tm, ROPE), lambda i, j: (i % nseq, 0))
    extra = [
        (ckr, BS((tm, KV_LORA), lambda i, j: (i, 0))), (latent_norm, BS((1, KV_LORA), lambda i, j: (0, 0))),
        (dkr, BS((tm, ROPE), lambda i, j: (i, 0))), (c64, pos), (s64, pos), (p64, BS((ROPE, ROPE), lambda i, j: (0, 0))),
    ]
    def heads_mm(y_ref, w_ref):
        acc = None
        for h in range(HEADS_B):
            part = _dot_nt(y_ref[:, h * hb : (h + 1) * hb], w_ref[h])
            acc = part if acc is None else acc + part
        return acc

    return _mm_nt_epi(
        "kv_latent_bwd", dkv, BS((tm, HEADS_B * hb), lambda i, j: (i, 0)), w_up, BS((HEADS_B, KV_LORA, hb), lambda i, j: (0, 0, 0)),
        1, KV_LORA, extra, [SDS((t, wd), BF16), SDS((8, KV_LORA), F32)],
        [BS((tm, wd), lambda i, j: (i, 0)), BS((8, KV_LORA), lambda i, j: (0, 0))], epilogue, tm, nt, heads_mm,
    )


def _adamw_step(g, w, m, v):
    nm = ADAM_B1 * m + (1.0 - ADAM_B1) * g
    nv = ADAM_B2 * v + (1.0 - ADAM_B2) * (g * g)
    m_hat = nm / (1.0 - ADAM_B1 ** ADAM_STEP)
    v_hat = nv / (1.0 - ADAM_B2 ** ADAM_STEP)
    return -ADAM_LR * (m_hat / (jnp.sqrt(v_hat) + ADAM_EPS) + ADAM_WD * w), nm, nv


def _adamw(name, parts, w, m, v):
    n_layers, rows, cols = w.shape
    tr = max(d for d in range(8, min(rows, 256) + 1, 8) if rows % d == 0)
    nb = rows // tr

    def body(*refs):
        p_refs = refs[:n_layers]
        w_ref, m_ref, v_ref, g_ref, d_ref, nm_ref, nv_ref = refs[n_layers : n_layers + 7]
        layer = pl.program_id(0)
        for lp in range(n_layers):

            @pl.when(layer == lp)
            def _():
                g = p_refs[lp][0].astype(F32)
                for k in range(1, NDEV):
                    g = g + p_refs[lp][k].astype(F32)
                g_ref[...] = g

        d_ref[...], nm_ref[...], nv_ref[...] = _adamw_step(g_ref[...], w_ref[...], m_ref[...], v_ref[...])

    def part_spec(lp):
        return BS((NDEV, tr, cols), lambda l, i: (0, jnp.where(l == lp, i, jnp.where(l < lp, 0, nb - 1)), 0))

    row = BS((None, tr, cols), lambda l, i: (l, i, 0))
    return pl.pallas_call(
        body, name=name, grid=(n_layers, nb),
        in_specs=[part_spec(lp) for lp in range(n_layers)] + [row, row, row],
        out_specs=[row] * 4, out_shape=[SDS(w.shape, F32)] * 4,
        compiler_params=_cparams(2),
    )(*parts, w, m, v)


def _pack_small(ffn1_norm, mix_norm, ffn2_norm, kv_norm, final_norm, q_norm, latent_norm, rel_bias, last_row):
    dn = ffn1_norm.shape[-1]

    def rows_of(a, n_rows):
        flat = a.reshape(-1)
        return jnp.pad(flat, (0, n_rows * dn - flat.shape[0])).reshape(n_rows, dn)

    return jnp.concatenate(
        [
            ffn1_norm.reshape(2, dn), mix_norm.reshape(2, dn), ffn2_norm.reshape(2, dn), kv_norm.reshape(1, dn),
            final_norm.reshape(1, dn), rows_of(q_norm, 1), rows_of(latent_norm, 1), rows_of(rel_bias, 5), rows_of(last_row, 1),
        ],
        axis=0,
    )


SMALL_PIECES = (
    ("ffn1_norm", 0, 2, None), ("mix_norm", 2, 2, None), ("ffn2_norm", 4, 2, None), ("kv_norm", 6, 1, None), ("final_norm", 7, 1, None),
    ("b_q_norm", 8, 1, Q_LORA), ("kv_latent_norm", 9, 1, KV_LORA), ("a_rel_bias", 10, 5, None), ("last", 15, 1, None),
)


def _adamw_small(parts, w, m, v):
    dn = w.shape[1]

    def body(p_ref, w_ref, m_ref, v_ref, *outs):
        g = p_ref[0]
        for k in range(1, NDEV):
            g = g + p_ref[k]
        for kind, val in enumerate((g,) + _adamw_step(g, w_ref[...], m_ref[...], v_ref[...])):
            for k, (_, r0, nr, width) in enumerate(SMALL_PIECES):
                outs[kind * len(SMALL_PIECES) + k][...] = val[r0 : r0 + nr, : width or dn]

    shapes = [SDS((nr, width or dn), F32) for _, _, nr, width in SMALL_PIECES] * 4
    outs = pl.pallas_call(
        body, name="adamw_small", grid=(1,),
        in_specs=[BS(parts.shape, lambda i: (0, 0, 0))] + [BS(w.shape, lambda i: (0, 0))] * 3,
        out_specs=[BS(s.shape, lambda i: (0, 0)) for s in shapes], out_shape=shapes,
        compiler_params=_cparams(1),
    )(parts, w, m, v)
    n = len(SMALL_PIECES)
    return [{name: outs[kind * n + k] for k, (name, _, _, _) in enumerate(SMALL_PIECES)} for kind in range(4)]


def kernel(x, ffn1_norm, ffn1_w_in, ffn1_w_out, mix_norm, ffn2_norm, ffn2_w_in, ffn2_w_out, a_w_qkv, a_rel_bias, a_w_o, kv_norm, kv_w_down, kv_latent_norm, kv_w_up, b_w_dq, b_q_norm, b_w_uq, b_w_o, final_norm, loss_target, m_ffn1_norm, m_ffn1_w_in, m_ffn1_w_out, m_mix_norm, m_ffn2_norm, m_ffn2_w_in, m_ffn2_w_out, m_a_w_qkv, m_a_rel_bias, m_a_w_o, m_kv_norm, m_kv_w_down, m_kv_latent_norm, m_kv_w_up, m_b_w_dq, m_b_q_norm, m_b_w_uq, m_b_w_o, m_final_norm, v_ffn1_norm, v_ffn1_w_in, v_ffn1_w_out, v_mix_norm, v_ffn2_norm, v_ffn2_w_in, v_ffn2_w_out, v_a_w_qkv, v_a_rel_bias, v_a_w_o, v_kv_norm, v_kv_w_down, v_kv_latent_norm, v_kv_w_up, v_b_w_dq, v_b_q_norm, v_b_w_uq, v_b_w_o, v_final_norm):
    bl, seq, dn = x.shape
    t = bl * seq
    tm = _tile(t)
    nt = t // tm
    x2 = x.reshape(t, dn)
    target2 = loss_target.reshape(t, dn)

    def gathered(*ws):
        return [("gather", w.astype(BF16)) for w in ws]

    groups = [
        gathered(ffn1_w_in[0]), gathered(ffn1_w_out[0]), gathered(a_w_qkv[0], a_w_o[0]), gathered(ffn2_w_in[0], ffn2_w_out[0]),
        gathered(kv_w_down, kv_w_up), gathered(ffn1_w_in[1], ffn1_w_out[1]), gathered(b_w_dq[0], b_w_uq[0], b_w_o[0]),
        gathered(ffn2_w_in[1], ffn2_w_out[1]),
    ]
    ag = [_exchange_sc(f"gather_{k}", group, GATHER_IDS[k]) for k, group in enumerate(groups)]

    def as_w_in(w):
        return w.reshape(1, NDEV, dn, FB)

    def as_w_out(w):
        return w.reshape(1, NJ, FB, dn)

    c64, s64, p64, c192, s192, p192 = _rope_tables(seq)
    q_norm = b_q_norm.reshape(1, Q_LORA)
    latent_norm = kv_latent_norm.reshape(1, KV_LORA)
    bias = _window_bias(_rel_bias_fwd(a_rel_bias[0][:, 1:]))

    h0, h1, h2, n1, hn, n2, gu1, gu2, a1, a2, w_in1, w_in2, w_out1, w_out2 = ([None, None] for _ in range(14))
    h0[0] = x2
    (n1[0],) = _norm_fwd("norm_x", x2, ffn1_norm[0:1])
    w_in1[0] = as_w_in(ag[0][0])
    gu1[0], a1[0] = _ffn_in("ffn1_in_0", n1[0], w_in1[0], 0)
    w_out1[0] = as_w_out(ag[1][0])
    h1[0], hn[0] = _mm_res_norm("ffn1_out_0", a1[0], w_out1[0], 0, h0[0], mix_norm[0:1], 0.5)
    w_qkv, w_o_a = ag[2]
    qkv_wb = w_qkv.shape[-1]
    w_o_a = w_o_a.reshape(1, 1, dn, dn)
    qkv3 = _qkv_proj("qkv_proj", hn[0], w_qkv)
    o_a, lse_a = _attn_a_fwd(qkv3, bias, bl, seq)
    h2[0], n2[0] = _mm_res_norm("attn_a_out", o_a.reshape(1, t, dn), w_o_a, 0, h1[0], ffn2_norm[0:1], 1.0)
    w_in2[0], w_out2[0] = as_w_in(ag[3][0]), as_w_out(ag[3][1])
    gu2[0], a2[0] = _ffn_in("ffn2_in_0", n2[0], w_in2[0], 0)
    h0[1], hk, n1[1] = _mm_res_norm(
        "ffn2_out_0", a2[0], w_out2[0], 0, h2[0], jnp.concatenate([kv_norm.reshape(1, dn), ffn1_norm[1:2]], axis=0), 0.5
    )
    w_down, w_up = ag[4]
    w_down = w_down.reshape(dn, KV_LORA + ROPE)
    ckr, ckv, kr = _kv_down(hk, w_down, latent_norm, c64, s64, p64, seq)
    kv = _kv_up(ckv, w_up)
    w_in1[1], w_out1[1] = as_w_in(ag[5][0]), as_w_out(ag[5][1])
    gu1[1], a1[1] = _ffn_in("ffn1_in_1", n1[1], w_in1[1], 0)
    h1[1], hn[1] = _mm_res_norm("ffn1_out_1", a1[1], w_out1[1], 0, h0[1], mix_norm[1:2], 0.5)
    w_dq, w_uq, w_o_b = ag[6]
    w_dq = w_dq.reshape(dn, Q_LORA)
    w_o_b = w_o_b.reshape(1, 1, dn, dn)
    cq_pre, cq = _q_down(hn[1], w_dq, q_norm)
    q = _q_up(cq, w_uq, c192, s192, p192, seq)
    o_b, lse_b = _mla_fwd(q, kv, kr, bl, seq)
    h2[1], n2[1] = _mm_res_norm("attn_b_out", o_b.reshape(1, t, dn), w_o_b, 0, h1[1], ffn2_norm[1:2], 1.0)
    w_in2[1], w_out2[1] = as_w_in(ag[7][0]), as_w_out(ag[7][1])
    gu2[1], a2[1] = _ffn_in("ffn2_in_1", n2[1], w_in2[1], 0)
    (h_last,) = _mm_res_norm("ffn2_out_1", a2[1], w_out2[1], 0, h2[1], None, 0.5)
    dh, dhb, dg_final, loss_part = _loss_final(h_last, target2, final_norm.reshape(1, dn))

    dg_ffn1, dg_mix, dg_ffn2, rs_ffn1, rs_ffn2 = ([None, None] for _ in range(5))

    def whole(rows, cols):
        return BS((rows, cols), lambda j: (0, 0))

    def dw_rows(name, xa, ya):
        n = ya.shape[1]
        return _mm_tn(name, xa, whole(t, dn), ya, whole(t, n), (dn, n), whole(dn, n), 1).reshape(NDEV, dn // NDEV, n)

    dh, dhb, dg_ffn2[1], rs_ffn2[1] = _ffn_bwd(
        "ffn2_1", dh, dhb, n2[1], h2[1], ffn2_norm[1:2], gu2[1], a2[1], w_in2[1], w_out2[1], REDUCE_IDS[0], ()
    )
    do_b = _mm_nt_plain("attn_b_do", dhb, w_o_b.reshape(dn, dn))
    dw_o_b = dw_rows("attn_b_dwo", o_b, dhb)
    dq_pre, dkv, dkr = _mla_bwd(q, kv, kr, o_b, lse_b, do_b, c192, s192, p192, bl, seq)
    dw_uq = _mm_tn(
        "dw_uq", cq, whole(t, Q_LORA), dq_pre, BS((None, t, QK_B), lambda j: (j, 0, 0)),
        (HEADS_B, Q_LORA, QK_B), BS((None, Q_LORA, QK_B), lambda j: (j, 0, 0)), HEADS_B,
    )
    dcq_pre, dg_q = _mm_nt_norm_bwd(
        "dcq", dq_pre, BS((HEADS_B, tm, QK_B), lambda i, j: (0, i, 0)), w_uq, BS((HEADS_B, Q_LORA, QK_B), lambda i, j: (0, 0, 0)),
        1, cq_pre, q_norm, None, BF16, mm_fn=_heads_mm,
    )
    dw_dq = dw_rows("dw_dq", hn[1], dcq_pre)
    dh, dg_mix[1], dhb = _mm_nt_norm_bwd(
        "dhn_b", dcq_pre, BS((tm, Q_LORA), lambda i, j: (i, 0)), w_dq, BS((dn, Q_LORA), lambda i, j: (0, 0)),
        1, h1[1], mix_norm[1:2], dh, F32,
    )
    dh, dhb, dg_ffn1[1], rs_ffn1[1] = _ffn_bwd(
        "ffn1_1", dh, dhb, n1[1], h0[1], ffn1_norm[1:2], gu1[1], a1[1], w_in1[1], w_out1[1], REDUCE_IDS[1], rs_ffn2[1][:1]
    )
    dw_up = _mm_tn(
        "dw_up", ckv, whole(t, KV_LORA), dkv, BS((t, NOPE + V_DIM), lambda j: (0, j)),
        (HEADS_B, KV_LORA, NOPE + V_DIM), BS((None, KV_LORA, NOPE + V_DIM), lambda j: (j, 0, 0)), HEADS_B,
    )
    dckr, dg_latent = _kv_latent_bwd(dkv, w_up, ckr, latent_norm, dkr, c64, s64, p64, seq)
    dw_down = dw_rows("dw_down", hk, dckr)
    dh, dg_kv, dhb = _mm_nt_norm_bwd(
        "dhk", dckr, BS((tm, KV_LORA + ROPE), lambda i, j: (i, 0)), w_down, BS((dn, KV_LORA + ROPE), lambda i, j: (0, 0)),
        1, h0[1], kv_norm.reshape(1, dn), dh, F32,
    )
    dh, dhb, dg_ffn2[0], rs_ffn2[0] = _ffn_bwd(
        "ffn2_0", dh, dhb, n2[0], h2[0], ffn2_norm[0:1], gu2[0], a2[0], w_in2[0], w_out2[0], REDUCE_IDS[2], rs_ffn1[1][:1]
    )
    do_a = _mm_nt_plain("attn_a_do", dhb, w_o_a.reshape(dn, dn))
    dw_o_a = dw_rows("attn_a_dwo", o_a, dhb)
    dqkv3, dbias = _attn_a_bwd(qkv3, o_a, lse_a, do_a, bias, bl, seq)
    dw_qkv = _dw_qkv(hn[0], dqkv3, qkv_wb)
    mixer_grads = [dw_o_a, dw_qkv, dw_o_b, dw_uq, dw_dq, dw_up, dw_down]
    dh, dg_mix[0], dhb = _mm_nt_norm_bwd(
        "dhn_a", dqkv3, BS((3, tm, dn), lambda i, j: (0, i, 0)), w_qkv, BS((NDEV, dn, qkv_wb), lambda i, j: (0, 0, 0)),
        1, h1[0], mix_norm[0:1], dh, F32, mm_fn=_dqkv_mm(qkv_wb // 128), after=mixer_grads,
    )
    rs_mixers = _exchange_sc("mixers_reduce", [("scatter", g) for g in mixer_grads], REDUCE_IDS[3], rs_ffn2[0][:1])
    dh, dhb, dg_ffn1[0], rs_ffn1[0] = _ffn_bwd(
        "ffn1_0", dh, dhb, n1[0], h0[0], ffn1_norm[0:1], gu1[0], a1[0], w_in1[0], w_out1[0], REDUCE_IDS[4], rs_mixers[:1]
    )
    grad_x = dh.reshape(bl, seq, dn)
    dtable = jnp.pad(_rel_bias_bwd(_window_bias_bwd(dbias)), ((0, 0), (1, 0)))

    def update(name, parts, w, m, v):
        shape3 = (len(parts),) + w.shape[-2:]
        parts = [p.reshape((NDEV,) + shape3[1:]) for p in parts]
        outs = _adamw(name, parts, w.reshape(shape3), m.reshape(shape3), v.reshape(shape3))
        return [o.reshape(w.shape) for o in outs]

    res = {}
    r_in2_1, r_out2_1 = rs_ffn2[1]
    r_in1_1, r_out1_1 = rs_ffn1[1]
    r_in2_0, r_out2_0 = rs_ffn2[0]
    r_in1_0, r_out1_0 = rs_ffn1[0]
    r_o_a, r_qkv, r_o_b, r_uq, r_dq, r_up, r_down = rs_mixers
    def update_transposed(name, parts, w, m, v):
        outs = update(name, parts, *[jnp.swapaxes(a, 1, 2) for a in (w, m, v)])
        return [jnp.swapaxes(o, 1, 2) for o in outs]

    res["ffn2_w_in"] = update_transposed("adamw_ffn2_w_in", [r_in2_0, r_in2_1], ffn2_w_in, m_ffn2_w_in, v_ffn2_w_in)
    res["ffn2_w_out"] = update("adamw_ffn2_w_out", [r_out2_0, r_out2_1], ffn2_w_out, m_ffn2_w_out, v_ffn2_w_out)
    res["kv_w_down"] = update("adamw_kv_w_down", [r_down], kv_w_down, m_kv_w_down, v_kv_w_down)
    res["kv_w_up"] = update("adamw_kv_w_up", [r_up], kv_w_up, m_kv_w_up, v_kv_w_up)
    res["b_w_dq"] = update("adamw_b_w_dq", [r_dq], b_w_dq, m_b_w_dq, v_b_w_dq)
    res["b_w_uq"] = update("adamw_b_w_uq", [r_uq], b_w_uq, m_b_w_uq, v_b_w_uq)
    res["b_w_o"] = update("adamw_b_w_o", [r_o_b], b_w_o, m_b_w_o, v_b_w_o)
    res["a_w_qkv"] = update("adamw_a_w_qkv", [r_qkv], a_w_qkv, m_a_w_qkv, v_a_w_qkv)
    res["a_w_o"] = update("adamw_a_w_o", [r_o_a], a_w_o, m_a_w_o, v_a_w_o)

    small = _pack_small(
        jnp.stack([dg_ffn1[0][0], dg_ffn1[1][0]]), jnp.stack([dg_mix[0][0], dg_mix[1][0]]), jnp.stack([dg_ffn2[0][0], dg_ffn2[1][0]]),
        dg_kv[0], dg_final[0], dg_q[0], dg_latent[0], dtable, loss_part[0],
    )
    done = [r[1] for name, r in res.items() if name != "ffn2_w_in"]
    (r_small,) = _exchange("gather_small_grads", [("gather", small)], after=done)
    res["ffn1_w_in"] = update_transposed("adamw_ffn1_w_in", [r_in1_0, r_in1_1], ffn1_w_in, m_ffn1_w_in, v_ffn1_w_in)
    res["ffn1_w_out"] = update("adamw_ffn1_w_out", [r_out1_0, r_out1_1], ffn1_w_out, m_ffn1_w_out, v_ffn1_w_out)
    zero_row = jnp.zeros((dn,), F32)
    packs = [
        _pack_small(f1, mx, f2, kvn, fin, qn, lat, rel, zero_row)
        for f1, mx, f2, kvn, fin, qn, lat, rel in (
            (ffn1_norm, mix_norm, ffn2_norm, kv_norm, final_norm, b_q_norm, kv_latent_norm, a_rel_bias),
            (m_ffn1_norm, m_mix_norm, m_ffn2_norm, m_kv_norm, m_final_norm, m_b_q_norm, m_kv_latent_norm, m_a_rel_bias),
            (v_ffn1_norm, v_mix_norm, v_ffn2_norm, v_kv_norm, v_final_norm, v_b_q_norm, v_kv_latent_norm, v_a_rel_bias),
        )
    ]
    small_out = _adamw_small(r_small, *packs)
    for name in ("ffn1_norm", "mix_norm", "ffn2_norm", "b_q_norm"):
        res[name] = [so[name] for so in small_out]
    for name in ("kv_norm", "kv_latent_norm", "final_norm"):
        res[name] = [so[name].reshape(-1) for so in small_out]
    res["a_rel_bias"] = [so["a_rel_bias"].reshape(-1)[: HEADS_A * NREL].reshape(1, HEADS_A, NREL) for so in small_out]
    loss = small_out[0]["last"][0, 0]

    order = [
        "ffn1_norm", "ffn1_w_in", "ffn1_w_out", "mix_norm", "ffn2_norm", "ffn2_w_in", "ffn2_w_out", "a_w_qkv", "a_rel_bias",
        "a_w_o", "kv_norm", "kv_w_down", "kv_latent_norm", "kv_w_up", "b_w_dq", "b_q_norm", "b_w_uq", "b_w_o", "final_norm",
    ]
    return (loss, grad_x, *[res[n][0] for n in order], *[res[n][1] for n in order], *[res[n][2] for n in order], *[res[n][3] for n in order])
```

```python
import jax
import jax.numpy as jnp
import numpy as np
from jax import lax
from jax.experimental import pallas as pl
from jax.experimental.pallas import tpu as pltpu
from jax.experimental.pallas import tpu_sc as plsc

NDEV = 8
D_MODEL = 1024
D_FF = 2816
FB = 2 * D_FF // NDEV
NJ = D_FF // FB
CHUNK = 64
LEFT_CHUNKS = 8
PAD = LEFT_CHUNKS * CHUNK
BAND = PAD + CHUNK
CHUNKS_PER_STEP = 4
WINDOW = PAD + CHUNKS_PER_STEP * CHUNK
STEP_ROWS = CHUNKS_PER_STEP * 2 * CHUNK
MAX_REL = 128
NREL = 2 * MAX_REL + 1
NREL_USED = 256
HEADS_A = 16
HEADS_B = 8
NOPE = 128
ROPE = 64
QK_B = NOPE + ROPE
V_DIM = 128
Q_LORA = 768
KV_LORA = 256
ROPE_THETA = 10000.0
EPS = 1e-6
NEG_INF = -1e30
MLA_TQ = 256
MLA_TK_FWD = 256
MLA_TK_BWD = 1024
ADAM_LR = 0.001
ADAM_B1 = 0.9
ADAM_B2 = 0.999
ADAM_EPS = 1e-08
ADAM_WD = 0.01
ADAM_STEP = 10
PACK_ROWS = 16
GU_BUFFERS = 3
GATHER_IDS = tuple(range(1, 9))
REDUCE_IDS = tuple(range(9, 14))
VMEM_LIMIT_BYTES = 56 * 1024 * 1024

F32 = jnp.float32
BF16 = jnp.bfloat16
SDS = jax.ShapeDtypeStruct
BS = pl.BlockSpec
MESH = pl.DeviceIdType.MESH


def _cparams(n_axes):
    return pltpu.CompilerParams(dimension_semantics=("arbitrary",) * n_axes, vmem_limit_bytes=VMEM_LIMIT_BYTES)


def _tile(t, want=512):
    return want if t % want == 0 else t


def _dot(a, b):
    return jnp.dot(a, b, preferred_element_type=F32)


def _dot_nt(a, b):
    return lax.dot_general(a, b, (((1,), (1,)), ((), ())), preferred_element_type=F32)


def _dot_tn(a, b):
    return lax.dot_general(a, b, (((0,), (0,)), ((), ())), preferred_element_type=F32)


def _split3(a):
    hi = a.astype(BF16)
    rest = a - hi.astype(F32)
    mid = rest.astype(BF16)
    return hi, mid, (rest - mid.astype(F32)).astype(BF16)


def _dot_exact(a, onehot, transposed=False):
    ob = onehot.astype(BF16)
    dot = _dot_nt if transposed else _dot
    hi, mid, lo = _split3(a)
    return dot(hi, ob) + dot(mid, ob) + dot(lo, ob)


def _rms_scale(h):
    return lax.rsqrt(jnp.mean(h * h, axis=-1, keepdims=True) + EPS)


def _acc_rows(ref, val, step, n_steps):
    part = val.reshape(val.shape[0] // 8, 8, val.shape[1]).sum(axis=0)

    @pl.when(step == 0)
    def _():
        ref[...] = part

    @pl.when(step > 0)
    def _():
        ref[...] += part

    @pl.when(step == n_steps - 1)
    def _():
        ref[...] = jnp.broadcast_to(jnp.sum(ref[...], axis=0, keepdims=True), ref.shape)


def _exchange_plan(entries):
    ins = [e[1] for e in entries]
    kinds = [e[0] for e in entries]
    lands = [SDS((NDEV,) + a.shape if k == "gather" else a.shape, a.dtype) for k, a in zip(kinds, ins)]
    return ins, lands, kinds


def _mesh_place():
    x, y, c = lax.axis_index("x"), lax.axis_index("y"), lax.axis_index("c")
    return (x, y, c), 4 * x + 2 * y + c


def _flipped(place, p):
    x, y, c = place
    px = 1 - x if p & 4 else x
    py = 1 - y if p & 2 else y
    pc = 1 - c if p & 1 else c
    return (px, py, pc), 4 * px + 2 * py + pc


def _ends(kind, src_ref, land_ref, origin, target):
    if kind == "gather":
        return src_ref, land_ref.at[origin]
    return src_ref.at[target], land_ref.at[origin]


def _remote(kind, src_ref, land_ref, send_sems, recv_sems, k, p, place, me, arriving):
    peer_pos, peer = _flipped(place, p)
    src, dst = _ends(kind, src_ref, land_ref, me, peer)
    if arriving:
        dst = _ends(kind, src_ref, land_ref, peer, me)[1]
    sem = k * (NDEV - 1) + p - 1
    return pltpu.make_async_remote_copy(
        src_ref=src, dst_ref=dst, send_sem=send_sems.at[sem], recv_sem=recv_sems.at[sem], device_id=peer_pos, device_id_type=MESH,
    )


def _exchange(name, entries, after=()):
    ins, lands, kinds = _exchange_plan(entries)
    n = len(ins)
    after = tuple(after)

    def body(*refs):
        refs = refs[:n] + refs[n + len(after) :]
        in_refs, land_refs = refs[:n], refs[n : 2 * n]
        send_sems, recv_sems, local_sems = refs[2 * n :]
        place, me = _mesh_place()
        local = []
        for k in range(n):
            src, dst = _ends(kinds[k], in_refs[k], land_refs[k], me, me)
            local.append(pltpu.make_async_copy(src, dst, local_sems.at[k]))
            local[-1].start()
        sends = []
        for p in range(1, NDEV):
            for k in range(n):
                sends.append(_remote(kinds[k], in_refs[k], land_refs[k], send_sems, recv_sems, k, p, place, me, False))
                sends[-1].start()
        for p in range(1, NDEV):
            for k in range(n):
                _remote(kinds[k], in_refs[k], land_refs[k], send_sems, recv_sems, k, p, place, me, True).wait_recv()
        for cp in sends:
            cp.wait_send()
        for cp in local:
            cp.wait()

    any_spec = BS(memory_space=pl.ANY)
    return pl.pallas_call(
        body, name=name, out_shape=lands, in_specs=[any_spec] * (n + len(after)), out_specs=[any_spec] * n,
        scratch_shapes=[
            pltpu.SemaphoreType.DMA((n * (NDEV - 1),)), pltpu.SemaphoreType.DMA((n * (NDEV - 1),)), pltpu.SemaphoreType.DMA((n,)),
        ],
    )(*ins, *after)


def _exchange_sc(name, entries, collective_id, after=()):
    ins, lands, kinds = _exchange_plan(entries)
    n = len(ins)
    after = tuple(after)

    def launch(*refs):
        refs = refs[:n] + refs[n + len(after) :]
        in_refs, land_refs = refs[:n], refs[n : 2 * n]
        send_sems, recv_sems, local_sems = refs[2 * n :]
        place, me = _mesh_place()
        barrier = pltpu.get_barrier_semaphore()
        for p in range(1, NDEV):
            pl.semaphore_signal(barrier, inc=1, device_id=_flipped(place, p)[0], device_id_type=MESH)
        pl.semaphore_wait(barrier, NDEV - 1)
        local = []
        for k in range(n):
            src, dst = _ends(kinds[k], in_refs[k], land_refs[k], me, me)
            local.append(pltpu.make_async_copy(src, dst, local_sems.at[k]))
            local[-1].start()
        sends = []
        if all(kind == "gather" for kind in kinds):
            for p in (1, 2, 4, 6):
                for k in range(n):
                    sends.append(_remote(kinds[k], in_refs[k], land_refs[k], send_sems, recv_sems, k, p, place, me, False))
                    sends[-1].start()
            sibling_pos, _ = _flipped(place, 1)
            for f in (2, 4, 6):
                _, origin = _flipped(place, f)
                for k in range(n):
                    _remote(kinds[k], in_refs[k], land_refs[k], send_sems, recv_sems, k, f, place, me, True).wait_recv()
                    sem = k * (NDEV - 1) + f
                    sends.append(
                        pltpu.make_async_remote_copy(
                            src_ref=land_refs[k].at[origin], dst_ref=land_refs[k].at[origin], send_sem=send_sems.at[sem],
                            recv_sem=recv_sems.at[sem], device_id=sibling_pos, device_id_type=MESH,
                        )
                    )
                    sends[-1].start()
            for p in (1, 3, 5, 7):
                for k in range(n):
                    _remote(kinds[k], in_refs[k], land_refs[k], send_sems, recv_sems, k, p, place, me, True).wait_recv()
        else:
            for p in range(1, NDEV):
                for k in range(n):
                    sends.append(_remote(kinds[k], in_refs[k], land_refs[k], send_sems, recv_sems, k, p, place, me, False))
                    sends[-1].start()
            for p in range(1, NDEV):
                for k in range(n):
                    _remote(kinds[k], in_refs[k], land_refs[k], send_sems, recv_sems, k, p, place, me, True).wait_recv()
        for cp in sends:
            cp.wait_send()
        for cp in local:
            cp.wait()

    return pl.kernel(
        launch, out_type=tuple(lands), mesh=plsc.ScalarSubcoreMesh(axis_name="sequencer", num_cores=1), name=name,
        scratch_types=(
            pltpu.SemaphoreType.DMA((n * (NDEV - 1),)), pltpu.SemaphoreType.DMA((n * (NDEV - 1),)), pltpu.SemaphoreType.DMA((n,)),
        ),
        compiler_params=pltpu.CompilerParams(collective_id=collective_id),
    )(*ins, *after)


def _norm_fwd(name, h, gammas):
    t, dn = h.shape
    ng = gammas.shape[0]
    tm = _tile(t)

    def body(h_ref, g_ref, *outs):
        hv = h_ref[...]
        hh = hv * _rms_scale(hv)
        for i, o_ref in enumerate(outs):
            o_ref[...] = (hh * g_ref[i : i + 1, :]).astype(BF16)

    row = BS((tm, dn), lambda i: (i, 0))
    return pl.pallas_call(
        body, name=name, grid=(t // tm,),
        in_specs=[row, BS((ng, dn), lambda i: (0, 0))],
        out_specs=[row] * ng, out_shape=[SDS((t, dn), BF16)] * ng,
        compiler_params=_cparams(1),
    )(h, gammas)


def _ffn_in(name, n, w_in, layer):
    t, dn = n.shape
    tm = _tile(t, 1024)

    def body(n_ref, wg_ref, wu_ref, gu_ref, a_ref):
        xv = n_ref[...]
        g = _dot(xv, wg_ref[...])
        u = _dot(xv, wu_ref[...])
        gu_ref[0] = g.astype(BF16)
        gu_ref[1] = u.astype(BF16)
        a_ref[...] = (g * jax.nn.sigmoid(g) * u).astype(BF16)

    return pl.pallas_call(
        body, name=name, grid=(NJ, t // tm),
        in_specs=[
            BS((tm, dn), lambda j, i: (i, 0)),
            BS((None, None, dn, FB), lambda j, i: (layer, j, 0, 0)),
            BS((None, None, dn, FB), lambda j, i: (layer, j + NJ, 0, 0)),
        ],
        out_specs=[BS((None, 2, tm, FB), lambda j, i: (j, 0, i, 0)), BS((None, tm, FB), lambda j, i: (j, i, 0))],
        out_shape=[SDS((NJ, 2, t, FB), BF16), SDS((NJ, t, FB), BF16)],
        compiler_params=_cparams(2),
    )(n, w_in, w_in)


def _mm_res_norm(name, a, w, layer, h_in, gammas, scale):
    nk, t, kb = a.shape
    dn = w.shape[-1]
    ng = 0 if gammas is None else gammas.shape[0]
    tm = _tile(t)

    nt = t // tm

    def a_fetch(a_hbm, bufs, sems, step):
        rows = pl.ds(pl.multiple_of(step * tm, tm), tm)
        return pltpu.make_async_copy(a_hbm.at[:, rows, :], bufs.at[step % GU_BUFFERS], sems.at[step % GU_BUFFERS])

    def body(*refs):
        a_hbm, w_ref, h_ref = refs[:3]
        g_ref = refs[3] if ng else None
        outs = refs[3 + (1 if ng else 0) : -2]
        bufs, sems = refs[-2:]
        step = pl.program_id(0)

        @pl.when(step == 0)
        def _():
            for first in range(min(GU_BUFFERS - 1, nt)):
                a_fetch(a_hbm, bufs, sems, first).start()

        @pl.when(step + GU_BUFFERS - 1 < nt)
        def _():
            a_fetch(a_hbm, bufs, sems, step + GU_BUFFERS - 1).start()

        a_fetch(a_hbm, bufs, sems, step).wait()
        a_ref = bufs.at[step % GU_BUFFERS]
        acc = _dot(a_ref[0], w_ref[0])
        for k in range(1, nk):
            acc += _dot(a_ref[k], w_ref[k])
        ho = h_ref[...] + scale * acc
        outs[0][...] = ho
        if ng:
            hh = ho * _rms_scale(ho)
            for i in range(ng):
                outs[1 + i][...] = (hh * g_ref[i : i + 1, :]).astype(BF16)

    row = BS((tm, dn), lambda i: (i, 0))
    in_specs = [BS(memory_space=pl.ANY), BS((None, nk, kb, dn), lambda i: (layer, 0, 0, 0)), row]
    args = [a, w, h_in]
    if ng:
        in_specs.append(BS((ng, dn), lambda i: (0, 0)))
        args.append(gammas)
    return pl.pallas_call(
        body, name=name, grid=(nt,),
        in_specs=in_specs,
        out_specs=[row] * (1 + ng), out_shape=[SDS((t, dn), F32)] + [SDS((t, dn), BF16)] * ng,
        scratch_shapes=[pltpu.VMEM((GU_BUFFERS, nk, tm, kb), BF16), pltpu.SemaphoreType.DMA((GU_BUFFERS,))],
        compiler_params=_cparams(1),
    )(*args)


def _qkv_proj(name, hn, w_qkv):
    t, dn = hn.shape
    wb = w_qkv.shape[-1]
    per = wb // 128
    tm = _tile(t)

    def body(x_ref, w_ref, o_ref):
        xv = x_ref[...]
        for j in range(NDEV):
            yv = _dot(xv, w_ref[j]).astype(BF16)
            for i in range(per):
                n = per * j + i
                o_ref[n // 8, :, (n % 8) * 128 : (n % 8 + 1) * 128] = yv[:, i * 128 : (i + 1) * 128]

    return pl.pallas_call(
        body, name=name, grid=(t // tm,),
        in_specs=[BS((tm, dn), lambda i: (i, 0)), BS((NDEV, dn, wb), lambda i: (0, 0, 0))],
        out_specs=BS((3, tm, dn), lambda i: (0, i, 0)), out_shape=SDS((3, t, dn), BF16),
        compiler_params=_cparams(1),
    )(hn, w_qkv)


def _rel_onehot(i):
    r = lax.broadcasted_iota(jnp.int32, (NREL_USED, BAND), 0)
    j = lax.broadcasted_iota(jnp.int32, (NREL_USED, BAND), 1)
    idx = jnp.clip(PAD + i - j, -MAX_REL, MAX_REL) + MAX_REL
    return (idx - 1 == r).astype(BF16)


def _rel_bias_fwd(table):
    def body(t_ref, o_ref):
        i8 = pl.program_id(0)
        for ii in range(8):
            o_ref[:, ii, :] = _dot_exact(t_ref[...], _rel_onehot(i8 * 8 + ii))

    return pl.pallas_call(
        body, name="rel_bias_fwd", grid=(CHUNK // 8,),
        in_specs=[BS((HEADS_A, NREL_USED), lambda i: (0, 0))],
        out_specs=BS((HEADS_A, 8, BAND), lambda i: (0, i, 0)), out_shape=SDS((HEADS_A, CHUNK, BAND), F32),
        compiler_params=_cparams(1),
    )(table)


def _rel_bias_bwd(dbias):
    def body(d_ref, o_ref):
        i8 = pl.program_id(0)
        acc = jnp.zeros((HEADS_A, NREL_USED), F32)
        for ii in range(8):
            acc += _dot_exact(d_ref[:, ii, :], _rel_onehot(i8 * 8 + ii), transposed=True)

        @pl.when(i8 == 0)
        def _():
            o_ref[...] = acc

        @pl.when(i8 > 0)
        def _():
            o_ref[...] += acc

    return pl.pallas_call(
        body, name="rel_bias_bwd", grid=(CHUNK // 8,),
        in_specs=[BS((HEADS_A, 8, BAND), lambda i: (0, i, 0))],
        out_specs=BS((HEADS_A, NREL_USED), lambda i: (0, 0)), out_shape=SDS((HEADS_A, NREL_USED), F32),
        compiler_params=_cparams(1),
    )(dbias)


def _window_bias(bias):
    b = bias.reshape(HEADS_A // 2, 2, CHUNK, BAND)
    per_chunk = [
        jnp.pad(b, ((0, 0), (0, 0), (0, 0), (cc * CHUNK, WINDOW - BAND - cc * CHUNK)), constant_values=NEG_INF)
        for cc in range(CHUNKS_PER_STEP)
    ]
    return jnp.stack(per_chunk, axis=1).reshape(HEADS_A // 2, STEP_ROWS, WINDOW)


def _window_bias_bwd(dwin):
    d = dwin.reshape(HEADS_A // 2, CHUNKS_PER_STEP, 2, CHUNK, WINDOW)
    return sum(d[:, cc, :, :, cc * CHUNK : cc * CHUNK + BAND] for cc in range(CHUNKS_PER_STEP)).reshape(HEADS_A, CHUNK, BAND)


def _step_rows(xs, lane):
    parts = []
    for cc in range(CHUNKS_PER_STEP):
        xc = xs[cc * CHUNK : (cc + 1) * CHUNK]
        parts.append(jnp.where(lane < 64, xc, jnp.zeros_like(xc)))
        parts.append(jnp.where(lane >= 64, xc, jnp.zeros_like(xc)))
    return jnp.concatenate(parts, axis=0)


def _pair_rows(ys, lane):
    parts = []
    for cc in range(CHUNKS_PER_STEP):
        y0 = ys[(2 * cc) * CHUNK : (2 * cc + 1) * CHUNK]
        y1 = ys[(2 * cc + 1) * CHUNK : (2 * cc + 2) * CHUNK]
        parts.append(jnp.where(lane < 64, y0, y1))
    return jnp.concatenate(parts, axis=0)


def _window_scores(q_rows, kwin, bias_win, first_key):
    s = _dot_nt(q_rows, kwin) * (CHUNK ** -0.5) + bias_win
    if first_key is None:
        return s
    col = lax.broadcasted_iota(jnp.int32, s.shape, 1)
    return jnp.where(col >= first_key, s, NEG_INF)


def _window_loop(n_passes, chunks):
    n_padded = min(PAD // (CHUNKS_PER_STEP * CHUNK), n_passes)
    lax.fori_loop(0, n_padded, lambda it, carry: chunks(it, carry, True), 0, unroll=2)
    if n_passes > n_padded:
        lax.fori_loop(n_padded, n_passes, lambda it, carry: chunks(it, carry, False), 0, unroll=2)


def _attn_a_fwd(qkv3, bias_win, bl, seq):
    t, dn = qkv3.shape[1:]
    npair = dn // 128
    step = CHUNKS_PER_STEP * CHUNK

    def body(q_ref, k_ref, v_ref, b_ref, o_ref, lse_ref, kpad, vpad):
        kpad[0:PAD, :] = jnp.zeros((PAD, 128), BF16)
        vpad[0:PAD, :] = jnp.zeros((PAD, 128), BF16)
        kpad[PAD:, :] = k_ref[...]
        vpad[PAD:, :] = v_ref[...]
        lane = lax.broadcasted_iota(jnp.int32, (CHUNK, 128), 1)

        def chunks(it, carry, padded):
            r0 = pl.multiple_of(it * step, step)
            q_rows = _step_rows(q_ref[pl.ds(r0, step), :], lane)
            s = _window_scores(q_rows, kpad[pl.ds(r0, WINDOW), :], b_ref[...], PAD - r0 if padded else None)
            m = jnp.max(s, axis=-1, keepdims=True)
            e = jnp.exp(s - m)
            total = jnp.sum(e, axis=-1, keepdims=True)
            o_rows = _dot(e.astype(BF16), vpad[pl.ds(r0, WINDOW), :]) * (1.0 / total)
            o_ref[pl.ds(r0, step), :] = _pair_rows(o_rows, lane).astype(BF16)
            lse_ref[pl.ds(pl.multiple_of(it * STEP_ROWS, STEP_ROWS), STEP_ROWS), :] = m + jnp.log(total)
            return carry

        _window_loop(seq // step, chunks)

    return pl.pallas_call(
        body, name="attn_a_fwd", grid=(bl, npair),
        in_specs=[
            BS((None, seq, 128), lambda b, h: (0, b, h)),
            BS((None, seq, 128), lambda b, h: (1, b, h)),
            BS((None, seq, 128), lambda b, h: (2, b, h)),
            BS((None, STEP_ROWS, WINDOW), lambda b, h: (h, 0, 0)),
        ],
        out_specs=[BS((seq, 128), lambda b, h: (b, h)), BS((None, 2 * seq, 1), lambda b, h: (h, b, 0))],
        out_shape=[SDS((t, dn), BF16), SDS((npair, 2 * t, 1), F32)],
        scratch_shapes=[pltpu.VMEM((PAD + seq, 128), BF16), pltpu.VMEM((PAD + seq, 128), BF16)],
        compiler_params=_cparams(2),
    )(qkv3, qkv3, qkv3, bias_win)


def _attn_a_bwd(qkv3, out, lse, do, bias_win, bl, seq):
    t, dn = qkv3.shape[1:]
    npair = dn // 128
    step = CHUNKS_PER_STEP * CHUNK

    def body(q_ref, k_ref, v_ref, o_ref, lse_ref, do_ref, b_ref, dqkv_ref, db_ref, kpad, vpad, dkacc, dvacc):
        b = pl.program_id(1)
        kpad[0:PAD, :] = jnp.zeros((PAD, 128), BF16)
        vpad[0:PAD, :] = jnp.zeros((PAD, 128), BF16)
        kpad[PAD:, :] = k_ref[...]
        vpad[PAD:, :] = v_ref[...]
        dkacc[...] = jnp.zeros_like(dkacc)
        dvacc[...] = jnp.zeros_like(dvacc)

        @pl.when(b == 0)
        def _():
            db_ref[...] = jnp.zeros_like(db_ref)

        lane = lax.broadcasted_iota(jnp.int32, (CHUNK, 128), 1)

        def chunks(it, carry, padded):
            r0 = pl.multiple_of(it * step, step)
            q_rows = _step_rows(q_ref[pl.ds(r0, step), :], lane)
            do_rows = _step_rows(do_ref[pl.ds(r0, step), :], lane)
            kwin = kpad[pl.ds(r0, WINDOW), :]
            vwin = vpad[pl.ds(r0, WINDOW), :]
            o_rows = _step_rows(o_ref[pl.ds(r0, step), :], lane)
            delta = jnp.sum(do_rows.astype(F32) * o_rows.astype(F32), axis=-1, keepdims=True)
            lse_rows = lse_ref[pl.ds(pl.multiple_of(it * STEP_ROWS, STEP_ROWS), STEP_ROWS), :]
            p = jnp.exp(_window_scores(q_rows, kwin, b_ref[...], PAD - r0 if padded else None) - lse_rows)
            ds = p * (_dot_nt(do_rows, vwin) - delta)
            db_ref[...] += ds
            dsb = (ds * (CHUNK ** -0.5)).astype(BF16)
            dqkv_ref[0, pl.ds(r0, step), :] = _pair_rows(_dot(dsb, kwin), lane).astype(BF16)
            dkacc[pl.ds(r0, WINDOW), :] += _dot_tn(dsb, q_rows)
            dvacc[pl.ds(r0, WINDOW), :] += _dot_tn(p.astype(BF16), do_rows)
            return carry

        _window_loop(seq // step, chunks)
        dqkv_ref[1] = dkacc[PAD:, :].astype(BF16)
        dqkv_ref[2] = dvacc[PAD:, :].astype(BF16)

    return pl.pallas_call(
        body, name="attn_a_bwd", grid=(npair, bl),
        in_specs=[
            BS((None, seq, 128), lambda h, b: (0, b, h)),
            BS((None, seq, 128), lambda h, b: (1, b, h)),
            BS((None, seq, 128), lambda h, b: (2, b, h)),
            BS((seq, 128), lambda h, b: (b, h)),
            BS((None, 2 * seq, 1), lambda h, b: (h, b, 0)),
            BS((seq, 128), lambda h, b: (b, h)),
            BS((None, STEP_ROWS, WINDOW), lambda h, b: (h, 0, 0)),
        ],
        out_specs=[BS((3, seq, 128), lambda h, b: (0, b, h)), BS((None, STEP_ROWS, WINDOW), lambda h, b: (h, 0, 0))],
        out_shape=[SDS((3, t, dn), BF16), SDS((HEADS_A // 2, STEP_ROWS, WINDOW), F32)],
        scratch_shapes=[
            pltpu.VMEM((PAD + seq, 128), BF16), pltpu.VMEM((PAD + seq, 128), BF16),
            pltpu.VMEM((PAD + seq, 128), F32), pltpu.VMEM((PAD + seq, 128), F32),
        ],
        compiler_params=_cparams(2),
    )(qkv3, qkv3, qkv3, out, lse, do, bias_win)


def _rope_tables(seq):
    half = ROPE // 2
    freqs = ROPE_THETA ** (-jnp.arange(half, dtype=F32) / half)
    ang = jnp.arange(seq, dtype=F32)[:, None] * freqs[None, :]
    cos, sin = jnp.cos(ang), jnp.sin(ang)
    c64 = jnp.concatenate([cos, cos], axis=1)
    s64 = jnp.concatenate([-sin, sin], axis=1)
    c192 = jnp.concatenate([jnp.ones((seq, NOPE), F32), c64], axis=1)
    s192 = jnp.concatenate([jnp.zeros((seq, NOPE), F32), s64], axis=1)
    p64 = np.zeros((ROPE, ROPE), np.float32)
    for col in range(ROPE):
        p64[(col + half) % ROPE, col] = 1.0
    p192 = np.zeros((QK_B, QK_B), np.float32)
    p192[NOPE:, NOPE:] = p64
    return c64, s64, jnp.asarray(p64), c192, s192, jnp.asarray(p192)


def _rope(xv, cos, sin_signed, swap):
    return xv * cos + _dot_exact(xv, swap) * sin_signed


def _rope_bwd(dy, cos, sin_signed, swap):
    return dy * cos + _dot_exact(dy * sin_signed, swap)


def _q_down(hn, w_dq, q_norm):
    t, dn = hn.shape
    ql = w_dq.shape[1]
    tm = _tile(t)

    def body(x_ref, w_ref, g_ref, pre_ref, cq_ref):
        pre = _dot(x_ref[...], w_ref[...])
        pre_ref[...] = pre
        cq_ref[...] = (pre * _rms_scale(pre) * g_ref[...]).astype(BF16)

    return pl.pallas_call(
        body, name="q_down", grid=(t // tm,),
        in_specs=[BS((tm, dn), lambda i: (i, 0)), BS((dn, ql), lambda i: (0, 0)), BS((1, ql), lambda i: (0, 0))],
        out_specs=[BS((tm, ql), lambda i: (i, 0))] * 2, out_shape=[SDS((t, ql), F32), SDS((t, ql), BF16)],
        compiler_params=_cparams(1),
    )(hn, w_dq, q_norm)


def _q_up(cq, w_uq, c192, s192, p192, seq):
    t, ql = cq.shape
    tm = _tile(min(seq, 512), min(seq, 512))
    nseq = seq // tm

    def body(x_ref, w_ref, c_ref, s_ref, p_ref, o_ref):
        xv = x_ref[...]
        for h in range(HEADS_B):
            o_ref[h] = _rope(_dot(xv, w_ref[h]), c_ref[...], s_ref[...], p_ref[...]).astype(BF16)

    pos = BS((tm, QK_B), lambda i: (i % nseq, 0))
    return pl.pallas_call(
        body, name="q_up", grid=(t // tm,),
        in_specs=[
            BS((tm, ql), lambda i: (i, 0)), BS((HEADS_B, ql, QK_B), lambda i: (0, 0, 0)), pos, pos,
            BS((QK_B, QK_B), lambda i: (0, 0)),
        ],
        out_specs=BS((HEADS_B, tm, QK_B), lambda i: (0, i, 0)), out_shape=SDS((HEADS_B, t, QK_B), BF16),
        compiler_params=_cparams(1),
    )(cq, w_uq, c192, s192, p192)


def _kv_down(hk, w_down, latent_norm, c64, s64, p64, seq):
    t, dn = hk.shape
    wd = w_down.shape[1]
    tm = _tile(min(seq, 512), min(seq, 512))
    nseq = seq // tm

    def body(x_ref, w_ref, g_ref, c_ref, s_ref, p_ref, ckr_ref, ckv_ref, kr_ref):
        ckr = _dot(x_ref[...], w_ref[...])
        ckr_ref[...] = ckr
        lat = ckr[:, :KV_LORA]
        ckv_ref[...] = (lat * _rms_scale(lat) * g_ref[...]).astype(BF16)
        kr_ref[...] = _rope(ckr[:, KV_LORA:], c_ref[...], s_ref[...], p_ref[...]).astype(BF16)

    pos = BS((tm, ROPE), lambda i: (i % nseq, 0))
    return pl.pallas_call(
        body, name="kv_down", grid=(t // tm,),
        in_specs=[
            BS((tm, dn), lambda i: (i, 0)), BS((dn, wd), lambda i: (0, 0)), BS((1, KV_LORA), lambda i: (0, 0)), pos, pos,
            BS((ROPE, ROPE), lambda i: (0, 0)),
        ],
        out_specs=[BS((tm, wd), lambda i: (i, 0)), BS((tm, KV_LORA), lambda i: (i, 0)), BS((tm, ROPE), lambda i: (i, 0))],
        out_shape=[SDS((t, wd), F32), SDS((t, KV_LORA), BF16), SDS((t, ROPE), BF16)],
        compiler_params=_cparams(1),
    )(hk, w_down, latent_norm, c64, s64, p64)


def _kv_up(ckv, w_up):
    t, kl = ckv.shape
    hb = w_up.shape[-1]
    tm = _tile(t)

    def body(x_ref, w_ref, o_ref):
        xv = x_ref[...]
        for h in range(HEADS_B):
            o_ref[:, h * hb : (h + 1) * hb] = _dot(xv, w_ref[h]).astype(BF16)

    return pl.pallas_call(
        body, name="kv_up", grid=(t // tm,),
        in_specs=[BS((tm, kl), lambda i: (i, 0)), BS((HEADS_B, kl, hb), lambda i: (0, 0, 0))],
        out_specs=BS((tm, HEADS_B * hb), lambda i: (i, 0)), out_shape=SDS((t, HEADS_B * hb), BF16),
        compiler_params=_cparams(1),
    )(ckv, w_up)


def _mla_diagonal_mask(tq):
    rows = lax.broadcasted_iota(jnp.int32, (tq, tq), 0)
    cols = lax.broadcasted_iota(jnp.int32, (tq, tq), 1)
    return jnp.where(jnp.right_shift(cols, 6) <= jnp.right_shift(rows, 6), 0.0, NEG_INF)


def _mla_key_tiles(n_keys, tk):
    return [(slice(k0, min(k0 + tk, n_keys)), min(k0 + tk, n_keys) == n_keys) for k0 in range(0, n_keys, tk)]


def _mla_scores(qi, kt, diagonal):
    s = _dot_nt(qi, kt) * (QK_B ** -0.5)
    if diagonal is None:
        return s
    tq, width = s.shape
    own = s[:, width - tq :] + diagonal
    return own if width == tq else jnp.concatenate([s[:, : width - tq], own], axis=1)


def _mla_fwd(q, kv, kr, bl, seq):
    t = kv.shape[0]
    tq = min(MLA_TQ, seq)

    def body(q_ref, kn_ref, v_ref, kr_ref, o_ref, lse_ref):
        kcat = jnp.concatenate([kn_ref[...], kr_ref[...]], axis=1)
        vv = v_ref[...]
        diagonal = _mla_diagonal_mask(tq)
        for i in range(seq // tq):
            rows = slice(i * tq, (i + 1) * tq)
            qi = q_ref[rows, :]
            m = total = acc = None
            for keys, own in _mla_key_tiles((i + 1) * tq, MLA_TK_FWD):
                s = _mla_scores(qi, kcat[keys], diagonal if own else None)
                m_blk = jnp.max(s, axis=-1, keepdims=True)
                if m is None:
                    m_new = m_blk
                    e = jnp.exp(s - m_new)
                    total = jnp.sum(e, axis=-1, keepdims=True)
                    acc = _dot(e.astype(BF16), vv[keys])
                else:
                    m_new = jnp.maximum(m, m_blk)
                    keep = jnp.exp(m - m_new)
                    e = jnp.exp(s - m_new)
                    total = keep * total + jnp.sum(e, axis=-1, keepdims=True)
                    acc = keep * acc + _dot(e.astype(BF16), vv[keys])
                m = m_new
            o_ref[rows, :] = (acc / total).astype(BF16)
            lse_ref[rows, :] = m + jnp.log(total)

    return pl.pallas_call(
        body, name="mla_fwd", grid=(bl, HEADS_B),
        in_specs=[
            BS((None, seq, QK_B), lambda b, h: (h, b, 0)),
            BS((seq, NOPE), lambda b, h: (b, 2 * h)),
            BS((seq, V_DIM), lambda b, h: (b, 2 * h + 1)),
            BS((seq, ROPE), lambda b, h: (b, 0)),
        ],
        out_specs=[BS((seq, V_DIM), lambda b, h: (b, h)), BS((None, seq, 1), lambda b, h: (h, b, 0))],
        out_shape=[SDS((t, HEADS_B * V_DIM), BF16), SDS((HEADS_B, t, 1), F32)],
        compiler_params=_cparams(2),
    )(q, kv, kv, kr)


def _mla_bwd(q, kv, kr, o, lse, do, c192, s192, p192, bl, seq):
    t = kv.shape[0]
    tq = min(MLA_TQ, seq)

    def body(q_ref, kn_ref, v_ref, kr_ref, o_ref, lse_ref, do_ref, c_ref, s_ref, p_ref, dq_ref, dkv_ref, dkr_ref, dkacc, dvacc):
        h = pl.program_id(1)
        kcat = jnp.concatenate([kn_ref[...], kr_ref[...]], axis=1)
        vv = v_ref[...]
        dkacc[...] = jnp.zeros_like(dkacc)
        dvacc[...] = jnp.zeros_like(dvacc)
        diagonal = _mla_diagonal_mask(tq)
        for i in range(seq // tq):
            rows = slice(i * tq, (i + 1) * tq)
            qi = q_ref[rows, :]
            doi = do_ref[rows, :]
            lse_i = lse_ref[rows, :]
            delta = jnp.sum(doi.astype(F32) * o_ref[rows, :].astype(F32), axis=-1, keepdims=True)
            dq = None
            for keys, own in _mla_key_tiles((i + 1) * tq, MLA_TK_BWD):
                p = jnp.exp(_mla_scores(qi, kcat[keys], diagonal if own else None) - lse_i)
                ds = p * (_dot_nt(doi, vv[keys]) - delta)
                dsb = (ds * (QK_B ** -0.5)).astype(BF16)
                dq_blk = _dot(dsb, kcat[keys])
                dq = dq_blk if dq is None else dq + dq_blk
                dkacc[keys, :] += _dot_tn(dsb, qi)
                dvacc[keys, :] += _dot_tn(p.astype(BF16), doi)
            dq_ref[rows, :] = _rope_bwd(dq, c_ref[rows, :], s_ref[rows, :], p_ref[...]).astype(BF16)
        dk = dkacc[...]
        dkv_ref[:, :NOPE] = dk[:, :NOPE].astype(BF16)
        dkv_ref[:, NOPE:] = dvacc[...].astype(BF16)

        @pl.when(h == 0)
        def _():
            dkr_ref[...] = dk[:, NOPE:]

        @pl.when(h > 0)
        def _():
            dkr_ref[...] += dk[:, NOPE:]

    return pl.pallas_call(
        body, name="mla_bwd", grid=(bl, HEADS_B),
        in_specs=[
            BS((None, seq, QK_B), lambda b, h: (h, b, 0)),
            BS((seq, NOPE), lambda b, h: (b, 2 * h)),
            BS((seq, V_DIM), lambda b, h: (b, 2 * h + 1)),
            BS((seq, ROPE), lambda b, h: (b, 0)),
            BS((seq, V_DIM), lambda b, h: (b, h)),
            BS((None, seq, 1), lambda b, h: (h, b, 0)),
            BS((seq, V_DIM), lambda b, h: (b, h)),
            BS((seq, QK_B), lambda b, h: (0, 0)),
            BS((seq, QK_B), lambda b, h: (0, 0)),
            BS((QK_B, QK_B), lambda b, h: (0, 0)),
        ],
        out_specs=[
            BS((None, seq, QK_B), lambda b, h: (h, b, 0)),
            BS((seq, NOPE + V_DIM), lambda b, h: (b, h)),
            BS((seq, ROPE), lambda b, h: (b, 0)),
        ],
        out_shape=[SDS((HEADS_B, t, QK_B), BF16), SDS((t, HEADS_B * (NOPE + V_DIM)), BF16), SDS((t, ROPE), F32)],
        scratch_shapes=[pltpu.VMEM((seq, QK_B), F32), pltpu.VMEM((seq, V_DIM), F32)],
        compiler_params=_cparams(2),
    )(q, kv, kv, kr, o, lse, do, c192, s192, p192)


def _loss_final(h, target, gamma):
    t, dn = h.shape
    tm = _tile(t)
    nt = t // tm

    def body(h_ref, t_ref, g_ref, dh_ref, dhb_ref, dg_ref, loss_ref):
        i = pl.program_id(0)
        hv = h_ref[...]
        r = _rms_scale(hv)
        hh = hv * r
        gam = g_ref[...]
        err = hh * gam - t_ref[...]
        part = 0.5 * jnp.sum(jnp.mean(err * err, axis=-1, keepdims=True))

        @pl.when(i == 0)
        def _():
            loss_ref[...] = jnp.zeros_like(loss_ref)

        loss_ref[...] += part
        dy = err * (1.0 / dn)
        _acc_rows(dg_ref, dy * hh, i, nt)
        t1 = dy * gam
        dh = r * (t1 - hh * jnp.mean(t1 * hh, axis=-1, keepdims=True))
        dh_ref[...] = dh
        dhb_ref[...] = dh.astype(BF16)

    row = BS((tm, dn), lambda i: (i, 0))
    return pl.pallas_call(
        body, name="loss_final", grid=(nt,),
        in_specs=[row, row, BS((1, dn), lambda i: (0, 0))],
        out_specs=[row, row, BS((8, dn), lambda i: (0, 0)), BS((8, 128), lambda i: (0, 0))],
        out_shape=[SDS((t, dn), F32), SDS((t, dn), BF16), SDS((8, dn), F32), SDS((8, 128), F32)],
        compiler_params=_cparams(1),
    )(h, target, gamma)


def _ffn_bwd_in(name, dh, w_out, layer, gu):
    t, dn = dh.shape
    tm = _tile(t, 1024)
    nt = t // tm
    n_steps = NJ * nt

    def gu_fetch(gu_hbm, bufs, sems, step):
        rows = pl.ds(pl.multiple_of((step % nt) * tm, tm), tm)
        return pltpu.make_async_copy(gu_hbm.at[step // nt, :, rows, :], bufs.at[step % GU_BUFFERS], sems.at[step % GU_BUFFERS])

    def body(dh_ref, w_ref, gu_hbm, o_ref, bufs, sems):
        step = pl.program_id(0) * nt + pl.program_id(1)

        @pl.when(step == 0)
        def _():
            for first in range(min(GU_BUFFERS - 1, n_steps)):
                gu_fetch(gu_hbm, bufs, sems, first).start()

        @pl.when(step + GU_BUFFERS - 1 < n_steps)
        def _():
            gu_fetch(gu_hbm, bufs, sems, step + GU_BUFFERS - 1).start()

        gu_fetch(gu_hbm, bufs, sems, step).wait()
        slot = step % GU_BUFFERS
        da = 0.5 * _dot_nt(dh_ref[...], w_ref[...])
        g = bufs[slot, 0].astype(F32)
        u = bufs[slot, 1].astype(F32)
        sg = jax.nn.sigmoid(g)
        o_ref[0] = (da * u * (sg * (1.0 + g * (1.0 - sg)))).astype(BF16)
        o_ref[1] = (da * (g * sg)).astype(BF16)

    return pl.pallas_call(
        body, name=name, grid=(NJ, nt),
        in_specs=[
            BS((tm, dn), lambda j, i: (i, 0)), BS((None, None, FB, dn), lambda j, i: (layer, j, 0, 0)), BS(memory_space=pl.ANY),
        ],
        out_specs=BS((None, 2, tm, FB), lambda j, i: (j, 0, i, 0)), out_shape=SDS((NJ, 2, t, FB), BF16),
        scratch_shapes=[pltpu.VMEM((GU_BUFFERS, 2, tm, FB), BF16), pltpu.SemaphoreType.DMA((GU_BUFFERS,))],
        compiler_params=_cparams(2),
    )(dh, w_out, gu)


def _mm_nt_plain(name, xf, w):
    t, dn = xf.shape
    n = w.shape[0]
    tm = _tile(t)

    def body(x_ref, w_ref, o_ref):
        o_ref[...] = _dot_nt(x_ref[...], w_ref[...]).astype(BF16)

    return pl.pallas_call(
        body, name=name, grid=(t // tm,),
        in_specs=[BS((tm, dn), lambda i: (i, 0)), BS((n, dn), lambda i: (0, 0))],
        out_specs=BS((tm, n), lambda i: (i, 0)), out_shape=SDS((t, n), BF16),
        compiler_params=_cparams(1),
    )(xf, w)


def _mm_tn(name, xa, x_spec, ya, y_spec, out_shape, out_spec, nj, scale=None):
    def body(x_ref, y_ref, o_ref):
        acc = _dot_tn(x_ref[...], y_ref[...])
        o_ref[...] = (acc if scale is None else scale * acc).astype(BF16)

    return pl.pallas_call(
        body, name=name, grid=(nj,),
        in_specs=[x_spec, y_spec], out_specs=out_spec, out_shape=SDS(out_shape, BF16),
        compiler_params=_cparams(1),
    )(xa, ya)


def _dw_qkv(hn, dqkv3, wb):
    t, dn = hn.shape
    per = wb // 128

    def body(x_ref, *refs):
        cols = [y_ref[...] for y_ref in refs[:per]]
        refs[per][...] = _dot_tn(x_ref[...], jnp.concatenate(cols, axis=1)).astype(BF16)

    def piece(k):
        return BS((None, t, 128), lambda j: ((per * j + k) // 8, 0, (per * j + k) % 8))

    return pl.pallas_call(
        body, name="dw_qkv", grid=(NDEV,),
        in_specs=[BS((t, dn), lambda j: (0, 0))] + [piece(k) for k in range(per)],
        out_specs=BS((None, dn, wb), lambda j: (j, 0, 0)), out_shape=SDS((NDEV, dn, wb), BF16),
        compiler_params=_cparams(1),
    )(hn, *([dqkv3] * per))


def _mm_nt_epi(name, ya, y_spec, wa, w_spec, nj, n_out, extra, out_shapes, out_specs, epilogue, tm, nt, mm_fn=None):
    n_extra = len(extra)
    n_outs = len(out_shapes)

    def body(*refs):
        y_ref, w_ref = refs[:2]
        ex = refs[2 : 2 + n_extra]
        outs = refs[2 + n_extra : 2 + n_extra + n_outs]
        i = pl.program_id(0)
        j = pl.program_id(1)
        part = _dot_nt(y_ref[...], w_ref[...]) if mm_fn is None else mm_fn(y_ref, w_ref)
        if nj == 1:
            epilogue(part, ex, outs, i, nt)
            return
        acc = refs[-1]

        @pl.when(j == 0)
        def _():
            acc[...] = part

        @pl.when(j > 0)
        def _():
            acc[...] += part

        @pl.when(j == nj - 1)
        def _():
            epilogue(acc[...], ex, outs, i, nt)

    return pl.pallas_call(
        body, name=name, grid=(nt, nj),
        in_specs=[y_spec, w_spec] + [spec for _, spec in extra],
        out_specs=out_specs, out_shape=out_shapes,
        scratch_shapes=[] if nj == 1 else [pltpu.VMEM((tm, n_out), F32)],
        compiler_params=_cparams(2),
    )(ya, wa, *[arr for arr, _ in extra])


def _norm_bwd(dn, hv, gam):
    r = _rms_scale(hv)
    hh = hv * r
    t1 = dn * gam
    return r * (t1 - hh * jnp.mean(t1 * hh, axis=-1, keepdims=True)), dn * hh


def _norm_bwd_epilogue(has_res, out_dtype):
    def epilogue(dn, ex, outs, i, nt):
        dh, dg_rows = _norm_bwd(dn, ex[0][...], ex[1][...])
        _acc_rows(outs[1], dg_rows, i, nt)
        if has_res:
            dh = dh + ex[2][...]
        outs[0][...] = dh.astype(out_dtype)
        if has_res:
            outs[2][...] = dh.astype(BF16)

    return epilogue


def _mm_nt_norm_bwd(name, ya, y_spec, wa, w_spec, nj, h, gamma, res, out_dtype, mm_fn=None, want_tm=512, after=None):
    t, n = h.shape
    tm = _tile(t, want_tm)
    nt = t // tm
    row = BS((tm, n), lambda i, j: (i, 0))
    extra = [(h, row), (gamma, BS((1, n), lambda i, j: (0, 0)))]
    out_shapes = [SDS((t, n), out_dtype), SDS((8, n), F32)]
    out_specs = [row, BS((8, n), lambda i, j: (0, 0))]
    if res is not None:
        extra.append((res, row))
        out_shapes.append(SDS((t, n), BF16))
        out_specs.append(row)
    extra.extend((a, BS(memory_space=pl.ANY)) for a in after or ())
    return _mm_nt_epi(
        name, ya, y_spec, wa, w_spec, nj, n, extra, out_shapes, out_specs, _norm_bwd_epilogue(res is not None, out_dtype), tm, nt, mm_fn,
    )


def _dev_block(jj):
    return jj // 2 + NJ * (jj % 2)


def _ffn_dn_mm(y_ref, w_ref):
    acc = None
    for jj in range(2 * NJ):
        part = _dot_nt(y_ref[jj], w_ref[_dev_block(jj)])
        acc = part if acc is None else acc + part
    return acc


def _ffn_bwd(tag, dh, dhb, n_in, h_in, gamma, gu, a, w_in, w_out, collective_id, after):
    t, dn = dh.shape
    dgu = _ffn_bwd_in(f"{tag}_bwd_in", dhb, w_out, 0, gu).reshape(2 * NJ, t, FB)
    dw_out = _mm_tn(
        f"{tag}_dw_out", a, BS((None, t, FB), lambda j: (j, 0, 0)), dhb, BS((t, dn), lambda j: (0, 0)),
        (NJ, FB, dn), BS((None, FB, dn), lambda j: (j, 0, 0)), NJ, scale=0.5,
    )
    dw_in = _mm_tn(
        f"{tag}_dw_in", dgu, BS((None, t, FB), lambda j: (j, 0, 0)), n_in, BS((t, dn), lambda j: (0, 0)),
        (NDEV, FB, dn), BS((None, FB, dn), lambda j: (_dev_block(j), 0, 0)), NDEV,
    )
    entries = [("scatter", dw_in), ("scatter", dw_out.reshape(NDEV, NJ * FB // NDEV, dn))]
    landed = _exchange_sc(f"{tag}_reduce", entries, collective_id, after)
    tm = _tile(t)
    resident = BS((None, NDEV, dn, FB), lambda i, j: (0, 0, 0, 0), pipeline_mode=pl.Buffered(1))
    dh_in, dgam, dhb_in = _mm_nt_norm_bwd(
        f"{tag}_dn", dgu, BS((2 * NJ, tm, FB), lambda i, j: (0, i, 0)), w_in, resident, 1, h_in, gamma, dh, F32, mm_fn=_ffn_dn_mm,
        after=[e[1] for e in entries],
    )
    return dh_in, dhb_in, dgam, landed


def _heads_mm(y_ref, w_ref):
    acc = None
    for h in range(HEADS_B):
        part = _dot_nt(y_ref[h], w_ref[h])
        acc = part if acc is None else acc + part
    return acc


def _dqkv_mm(per):
    def mm(y_ref, w_ref):
        acc = None
        for j in range(NDEV):
            cols = [y_ref[(per * j + k) // 8, :, ((per * j + k) % 8) * 128 : ((per * j + k) % 8 + 1) * 128] for k in range(per)]
            part = _dot_nt(jnp.concatenate(cols, axis=1), w_ref[j])
            acc = part if acc is None else acc + part
        return acc

    return mm


def _kv_latent_bwd(dkv, w_up, ckr, latent_norm, dkr, c64, s64, p64, seq):
    t, wd = ckr.shape
    hb = w_up.shape[-1]
    tm = _tile(min(seq, 512), min(seq, 512))
    nt = t // tm
    nseq = seq // tm

    def epilogue(dn, ex, outs, i, nt_):
        dlat, dg_rows = _norm_bwd(dn, ex[0][...], ex[1][...])
        _acc_rows(outs[1], dg_rows, i, nt_)
        outs[0][:, :KV_LORA] = dlat.astype(BF16)
        outs[0][:, KV_LORA:] = _rope_bwd(ex[2][...], ex[3][...], ex[4][...], ex[5][...]).astype(BF16)

    pos = BS((tm, ROPE), lambda i, j: (i % nseq, 0))
    extra = [
        (ckr, BS((tm, KV_LORA), lambda i, j: (i, 0))), (latent_norm, BS((1, KV_LORA), lambda i, j: (0, 0))),
        (dkr, BS((tm, ROPE), lambda i, j: (i, 0))), (c64, pos), (s64, pos), (p64, BS((ROPE, ROPE), lambda i, j: (0, 0))),
    ]
    def heads_mm(y_ref, w_ref):
        acc = None
        for h in range(HEADS_B):
            part = _dot_nt(y_ref[:, h * hb : (h + 1) * hb], w_ref[h])
            acc = part if acc is None else acc + part
        return acc

    return _mm_nt_epi(
        "kv_latent_bwd", dkv, BS((tm, HEADS_B * hb), lambda i, j: (i, 0)), w_up, BS((HEADS_B, KV_LORA, hb), lambda i, j: (0, 0, 0)),
        1, KV_LORA, extra, [SDS((t, wd), BF16), SDS((8, KV_LORA), F32)],
        [BS((tm, wd), lambda i, j: (i, 0)), BS((8, KV_LORA), lambda i, j: (0, 0))], epilogue, tm, nt, heads_mm,
    )


def _adamw_step(g, w, m, v):
    nm = ADAM_B1 * m + (1.0 - ADAM_B1) * g
    nv = ADAM_B2 * v + (1.0 - ADAM_B2) * (g * g)
    m_hat = nm / (1.0 - ADAM_B1 ** ADAM_STEP)
    v_hat = nv / (1.0 - ADAM_B2 ** ADAM_STEP)
    return -ADAM_LR * (m_hat / (jnp.sqrt(v_hat) + ADAM_EPS) + ADAM_WD * w), nm, nv


def _adamw(name, parts, w, m, v):
    n_layers, rows, cols = w.shape
    tr = max(d for d in range(8, min(rows, 256) + 1, 8) if rows % d == 0)
    nb = rows // tr

    def body(*refs):
        p_refs = refs[:n_layers]
        w_ref, m_ref, v_ref, g_ref, d_ref, nm_ref, nv_ref = refs[n_layers : n_layers + 7]
        layer = pl.program_id(0)
        for lp in range(n_layers):

            @pl.when(layer == lp)
            def _():
                g = p_refs[lp][0].astype(F32)
                for k in range(1, NDEV):
                    g = g + p_refs[lp][k].astype(F32)
                g_ref[...] = g

        d_ref[...], nm_ref[...], nv_ref[...] = _adamw_step(g_ref[...], w_ref[...], m_ref[...], v_ref[...])

    def part_spec(lp):
        return BS((NDEV, tr, cols), lambda l, i: (0, jnp.where(l == lp, i, jnp.where(l < lp, 0, nb - 1)), 0))

    row = BS((None, tr, cols), lambda l, i: (l, i, 0))
    return pl.pallas_call(
        body, name=name, grid=(n_layers, nb),
        in_specs=[part_spec(lp) for lp in range(n_layers)] + [row, row, row],
        out_specs=[row] * 4, out_shape=[SDS(w.shape, F32)] * 4,
        compiler_params=_cparams(2),
    )(*parts, w, m, v)


def _pack_small(ffn1_norm, mix_norm, ffn2_norm, kv_norm, final_norm, q_norm, latent_norm, rel_bias, last_row):
    dn = ffn1_norm.shape[-1]

    def rows_of(a, n_rows):
        flat = a.reshape(-1)
        return jnp.pad(flat, (0, n_rows * dn - flat.shape[0])).reshape(n_rows, dn)

    return jnp.concatenate(
        [
            ffn1_norm.reshape(2, dn), mix_norm.reshape(2, dn), ffn2_norm.reshape(2, dn), kv_norm.reshape(1, dn),
            final_norm.reshape(1, dn), rows_of(q_norm, 1), rows_of(latent_norm, 1), rows_of(rel_bias, 5), rows_of(last_row, 1),
        ],
        axis=0,
    )


SMALL_PIECES = (
    ("ffn1_norm", 0, 2, None), ("mix_norm", 2, 2, None), ("ffn2_norm", 4, 2, None), ("kv_norm", 6, 1, None), ("final_norm", 7, 1, None),
    ("b_q_norm", 8, 1, Q_LORA), ("kv_latent_norm", 9, 1, KV_LORA), ("a_rel_bias", 10, 5, None), ("last", 15, 1, None),
)


def _adamw_small(parts, w, m, v):
    dn = w.shape[1]

    def body(p_ref, w_ref, m_ref, v_ref, *outs):
        g = p_ref[0]
        for k in range(1, NDEV):
            g = g + p_ref[k]
        for kind, val in enumerate((g,) + _adamw_step(g, w_ref[...], m_ref[...], v_ref[...])):
            for k, (_, r0, nr, width) in enumerate(SMALL_PIECES):
                outs[kind * len(SMALL_PIECES) + k][...] = val[r0 : r0 + nr, : width or dn]

    shapes = [SDS((nr, width or dn), F32) for _, _, nr, width in SMALL_PIECES] * 4
    outs = pl.pallas_call(
        body, name="adamw_small", grid=(1,),
        in_specs=[BS(parts.shape, lambda i: (0, 0, 0))] + [BS(w.shape, lambda i: (0, 0))] * 3,
        out_specs=[BS(s.shape, lambda i: (0, 0)) for s in shapes], out_shape=shapes,
        compiler_params=_cparams(1),
    )(parts, w, m, v)
    n = len(SMALL_PIECES)
    return [{name: outs[kind * n + k] for k, (name, _, _, _) in enumerate(SMALL_PIECES)} for kind in range(4)]


def kernel(x, ffn1_norm, ffn1_w_in, ffn1_w_out, mix_norm, ffn2_norm, ffn2_w_in, ffn2_w_out, a_w_qkv, a_rel_bias, a_w_o, kv_norm, kv_w_down, kv_latent_norm, kv_w_up, b_w_dq, b_q_norm, b_w_uq, b_w_o, final_norm, loss_target, m_ffn1_norm, m_ffn1_w_in, m_ffn1_w_out, m_mix_norm, m_ffn2_norm, m_ffn2_w_in, m_ffn2_w_out, m_a_w_qkv, m_a_rel_bias, m_a_w_o, m_kv_norm, m_kv_w_down, m_kv_latent_norm, m_kv_w_up, m_b_w_dq, m_b_q_norm, m_b_w_uq, m_b_w_o, m_final_norm, v_ffn1_norm, v_ffn1_w_in, v_ffn1_w_out, v_mix_norm, v_ffn2_norm, v_ffn2_w_in, v_ffn2_w_out, v_a_w_qkv, v_a_rel_bias, v_a_w_o, v_kv_norm, v_kv_w_down, v_kv_latent_norm, v_kv_w_up, v_b_w_dq, v_b_q_norm, v_b_w_uq, v_b_w_o, v_final_norm):
    bl, seq, dn = x.shape
    t = bl * seq
    tm = _tile(t)
    nt = t // tm
    x2 = x.reshape(t, dn)
    target2 = loss_target.reshape(t, dn)

    def gathered(*ws):
        return [("gather", w.astype(BF16)) for w in ws]

    groups = [
        gathered(ffn1_w_in[0]), gathered(ffn1_w_out[0]), gathered(a_w_qkv[0], a_w_o[0]), gathered(ffn2_w_in[0], ffn2_w_out[0]),
        gathered(kv_w_down, kv_w_up), gathered(ffn1_w_in[1], ffn1_w_out[1]), gathered(b_w_dq[0], b_w_uq[0], b_w_o[0]),
        gathered(ffn2_w_in[1], ffn2_w_out[1]),
    ]
    ag = [_exchange_sc(f"gather_{k}", group, GATHER_IDS[k]) for k, group in enumerate(groups)]

    def as_w_in(w):
        return w.reshape(1, NDEV, dn, FB)

    def as_w_out(w):
        return w.reshape(1, NJ, FB, dn)

    c64, s64, p64, c192, s192, p192 = _rope_tables(seq)
    q_norm = b_q_norm.reshape(1, Q_LORA)
    latent_norm = kv_latent_norm.reshape(1, KV_LORA)
    bias = _window_bias(_rel_bias_fwd(a_rel_bias[0][:, 1:]))

    h0, h1, h2, n1, hn, n2, gu1, gu2, a1, a2, w_in1, w_in2, w_out1, w_out2 = ([None, None] for _ in range(14))
    h0[0] = x2
    (n1[0],) = _norm_fwd("norm_x", x2, ffn1_norm[0:1])
    w_in1[0] = as_w_in(ag[0][0])
    gu1[0], a1[0] = _ffn_in("ffn1_in_0", n1[0], w_in1[0], 0)
    w_out1[0] = as_w_out(ag[1][0])
    h1[0], hn[0] = _mm_res_norm("ffn1_out_0", a1[0], w_out1[0], 0, h0[0], mix_norm[0:1], 0.5)
    w_qkv, w_o_a = ag[2]
    qkv_wb = w_qkv.shape[-1]
    w_o_a = w_o_a.reshape(1, 1, dn, dn)
    qkv3 = _qkv_proj("qkv_proj", hn[0], w_qkv)
    o_a, lse_a = _attn_a_fwd(qkv3, bias, bl, seq)
    h2[0], n2[0] = _mm_res_norm("attn_a_out", o_a.reshape(1, t, dn), w_o_a, 0, h1[0], ffn2_norm[0:1], 1.0)
    w_in2[0], w_out2[0] = as_w_in(ag[3][0]), as_w_out(ag[3][1])
    gu2[0], a2[0] = _ffn_in("ffn2_in_0", n2[0], w_in2[0], 0)
    h0[1], hk, n1[1] = _mm_res_norm(
        "ffn2_out_0", a2[0], w_out2[0], 0, h2[0], jnp.concatenate([kv_norm.reshape(1, dn), ffn1_norm[1:2]], axis=0), 0.5
    )
    w_down, w_up = ag[4]
    w_down = w_down.reshape(dn, KV_LORA + ROPE)
    ckr, ckv, kr = _kv_down(hk, w_down, latent_norm, c64, s64, p64, seq)
    kv = _kv_up(ckv, w_up)
    w_in1[1], w_out1[1] = as_w_in(ag[5][0]), as_w_out(ag[5][1])
    gu1[1], a1[1] = _ffn_in("ffn1_in_1", n1[1], w_in1[1], 0)
    h1[1], hn[1] = _mm_res_norm("ffn1_out_1", a1[1], w_out1[1], 0, h0[1], mix_norm[1:2], 0.5)
    w_dq, w_uq, w_o_b = ag[6]
    w_dq = w_dq.reshape(dn, Q_LORA)
    w_o_b = w_o_b.reshape(1, 1, dn, dn)
    cq_pre, cq = _q_down(hn[1], w_dq, q_norm)
    q = _q_up(cq, w_uq, c192, s192, p192, seq)
    o_b, lse_b = _mla_fwd(q, kv, kr, bl, seq)
    h2[1], n2[1] = _mm_res_norm("attn_b_out", o_b.reshape(1, t, dn), w_o_b, 0, h1[1], ffn2_norm[1:2], 1.0)
    w_in2[1], w_out2[1] = as_w_in(ag[7][0]), as_w_out(ag[7][1])
    gu2[1], a2[1] = _ffn_in("ffn2_in_1", n2[1], w_in2[1], 0)
    (h_last,) = _mm_res_norm("ffn2_out_1", a2[1], w_out2[1], 0, h2[1], None, 0.5)
    dh, dhb, dg_final, loss_part = _loss_final(h_last, target2, final_norm.reshape(1, dn))

    dg_ffn1, dg_mix, dg_ffn2, rs_ffn1, rs_ffn2 = ([None, None] for _ in range(5))

    def whole(rows, cols):
        return BS((rows, cols), lambda j: (0, 0))

    def dw_rows(name, xa, ya):
        n = ya.shape[1]
        return _mm_tn(name, xa, whole(t, dn), ya, whole(t, n), (dn, n), whole(dn, n), 1).reshape(NDEV, dn // NDEV, n)

    dh, dhb, dg_ffn2[1], rs_ffn2[1] = _ffn_bwd(
        "ffn2_1", dh, dhb, n2[1], h2[1], ffn2_norm[1:2], gu2[1], a2[1], w_in2[1], w_out2[1], REDUCE_IDS[0], ()
    )
    do_b = _mm_nt_plain("attn_b_do", dhb, w_o_b.reshape(dn, dn))
    dw_o_b = dw_rows("attn_b_dwo", o_b, dhb)
    dq_pre, dkv, dkr = _mla_bwd(q, kv, kr, o_b, lse_b, do_b, c192, s192, p192, bl, seq)
    dw_uq = _mm_tn(
        "dw_uq", cq, whole(t, Q_LORA), dq_pre, BS((None, t, QK_B), lambda j: (j, 0, 0)),
        (HEADS_B, Q_LORA, QK_B), BS((None, Q_LORA, QK_B), lambda j: (j, 0, 0)), HEADS_B,
    )
    dcq_pre, dg_q = _mm_nt_norm_bwd(
        "dcq", dq_pre, BS((HEADS_B, tm, QK_B), lambda i, j: (0, i, 0)), w_uq, BS((HEADS_B, Q_LORA, QK_B), lambda i, j: (0, 0, 0)),
        1, cq_pre, q_norm, None, BF16, mm_fn=_heads_mm,
    )
    dw_dq = dw_rows("dw_dq", hn[1], dcq_pre)
    dh, dg_mix[1], dhb = _mm_nt_norm_bwd(
        "dhn_b", dcq_pre, BS((tm, Q_LORA), lambda i, j: (i, 0)), w_dq, BS((dn, Q_LORA), lambda i, j: (0, 0)),
        1, h1[1], mix_norm[1:2], dh, F32,
    )
    dh, dhb, dg_ffn1[1], rs_ffn1[1] = _ffn_bwd(
        "ffn1_1", dh, dhb, n1[1], h0[1], ffn1_norm[1:2], gu1[1], a1[1], w_in1[1], w_out1[1], REDUCE_IDS[1], rs_ffn2[1][:1]
    )
    dw_up = _mm_tn(
        "dw_up", ckv, whole(t, KV_LORA), dkv, BS((t, NOPE + V_DIM), lambda j: (0, j)),
        (HEADS_B, KV_LORA, NOPE + V_DIM), BS((None, KV_LORA, NOPE + V_DIM), lambda j: (j, 0, 0)), HEADS_B,
    )
    dckr, dg_latent = _kv_latent_bwd(dkv, w_up, ckr, latent_norm, dkr, c64, s64, p64, seq)
    dw_down = dw_rows("dw_down", hk, dckr)
    dh, dg_kv, dhb = _mm_nt_norm_bwd(
        "dhk", dckr, BS((tm, KV_LORA + ROPE), lambda i, j: (i, 0)), w_down, BS((dn, KV_LORA + ROPE), lambda i, j: (0, 0)),
        1, h0[1], kv_norm.reshape(1, dn), dh, F32,
    )
    dh, dhb, dg_ffn2[0], rs_ffn2[0] = _ffn_bwd(
        "ffn2_0", dh, dhb, n2[0], h2[0], ffn2_norm[0:1], gu2[0], a2[0], w_in2[0], w_out2[0], REDUCE_IDS[2], rs_ffn1[1][:1]
    )
    do_a = _mm_nt_plain("attn_a_do", dhb, w_o_a.reshape(dn, dn))
    dw_o_a = dw_rows("attn_a_dwo", o_a, dhb)
    dqkv3, dbias = _attn_a_bwd(qkv3, o_a, lse_a, do_a, bias, bl, seq)
    dw_qkv = _dw_qkv(hn[0], dqkv3, qkv_wb)
    mixer_grads = [dw_o_a, dw_qkv, dw_o_b, dw_uq, dw_dq, dw_up, dw_down]
    dh, dg_mix[0], dhb = _mm_nt_norm_bwd(
        "dhn_a", dqkv3, BS((3, tm, dn), lambda i, j: (0, i, 0)), w_qkv, BS((NDEV, dn, qkv_wb), lambda i, j: (0, 0, 0)),
        1, h1[0], mix_norm[0:1], dh, F32, mm_fn=_dqkv_mm(qkv_wb // 128), after=mixer_grads,
    )
    rs_mixers = _exchange_sc("mixers_reduce", [("scatter", g) for g in mixer_grads], REDUCE_IDS[3], rs_ffn2[0][:1])
    dh, dhb, dg_ffn1[0], rs_ffn1[0] = _ffn_bwd(
        "ffn1_0", dh, dhb, n1[0], h0[0], ffn1_norm[0:1], gu1[0], a1[0], w_in1[0], w_out1[0], REDUCE_IDS[4], rs_mixers[:1]
    )
    grad_x = dh.reshape(bl, seq, dn)
    dtable = jnp.pad(_rel_bias_bwd(_window_bias_bwd(dbias)), ((0, 0), (1, 0)))

    def update(name, parts, w, m, v):
        shape3 = (len(parts),) + w.shape[-2:]
        parts = [p.reshape((NDEV,) + shape3[1:]) for p in parts]
        outs = _adamw(name, parts, w.reshape(shape3), m.reshape(shape3), v.reshape(shape3))
        return [o.reshape(w.shape) for o in outs]

    res = {}
    r_in2_1, r_out2_1 = rs_ffn2[1]
    r_in1_1, r_out1_1 = rs_ffn1[1]
    r_in2_0, r_out2_0 = rs_ffn2[0]
    r_in1_0, r_out1_0 = rs_ffn1[0]
    r_o_a, r_qkv, r_o_b, r_uq, r_dq, r_up, r_down = rs_mixers
    def update_transposed(name, parts, w, m, v):
        outs = update(name, parts, *[jnp.swapaxes(a, 1, 2) for a in (w, m, v)])
        return [jnp.swapaxes(o, 1, 2) for o in outs]

    res["ffn2_w_in"] = update_transposed("adamw_ffn2_w_in", [r_in2_0, r_in2_1], ffn2_w_in, m_ffn2_w_in, v_ffn2_w_in)
    res["ffn2_w_out"] = update("adamw_ffn2_w_out", [r_out2_0, r_out2_1], ffn2_w_out, m_ffn2_w_out, v_ffn2_w_out)
    res["kv_w_down"] = update("adamw_kv_w_down", [r_down], kv_w_down, m_kv_w_down, v_kv_w_down)
    res["kv_w_up"] = update("adamw_kv_w_up", [r_up], kv_w_up, m_kv_w_up, v_kv_w_up)
    res["b_w_dq"] = update("adamw_b_w_dq", [r_dq], b_w_dq, m_b_w_dq, v_b_w_dq)
    res["b_w_uq"] = update("adamw_b_w_uq", [r_uq], b_w_uq, m_b_w_uq, v_b_w_uq)
    res["b_w_o"] = update("adamw_b_w_o", [r_o_b], b_w_o, m_b_w_o, v_b_w_o)
    res["a_w_qkv"] = update("adamw_a_w_qkv", [r_qkv], a_w_qkv, m_a_w_qkv, v_a_w_qkv)
    res["a_w_o"] = update("adamw_a_w_o", [r_o_a], a_w_o, m_a_w_o, v_a_w_o)

    small = _pack_small(
        jnp.stack([dg_ffn1[0][0], dg_ffn1[1][0]]), jnp.stack([dg_mix[0][0], dg_mix[1][0]]), jnp.stack([dg_ffn2[0][0], dg_ffn2[1][0]]),
        dg_kv[0], dg_final[0], dg_q[0], dg_latent[0], dtable, loss_part[0],
    )
    done = [r[1] for name, r in res.items() if name != "ffn2_w_in"]
    (r_small,) = _exchange("gather_small_grads", [("gather", small)], after=done)
    res["ffn1_w_in"] = update_transposed("adamw_ffn1_w_in", [r_in1_0, r_in1_1], ffn1_w_in, m_ffn1_w_in, v_ffn1_w_in)
    res["ffn1_w_out"] = update("adamw_ffn1_w_out", [r_out1_0, r_out1_1], ffn1_w_out, m_ffn1_w_out, v_ffn1_w_out)
    zero_row = jnp.zeros((dn,), F32)
    packs = [
        _pack_small(f1, mx, f2, kvn, fin, qn, lat, rel, zero_row)
        for f1, mx, f2, kvn, fin, qn, lat, rel in (
            (ffn1_norm, mix_norm, ffn2_norm, kv_norm, final_norm, b_q_norm, kv_latent_norm, a_rel_bias),
            (m_ffn1_norm, m_mix_norm, m_ffn2_norm, m_kv_norm, m_final_norm, m_b_q_norm, m_kv_latent_norm, m_a_rel_bias),
            (v_ffn1_norm, v_mix_norm, v_ffn2_norm, v_kv_norm, v_final_norm, v_b_q_norm, v_kv_latent_norm, v_a_rel_bias),
        )
    ]
    small_out = _adamw_small(r_small, *packs)
    for name in ("ffn1_norm", "mix_norm", "ffn2_norm", "b_q_norm"):
        res[name] = [so[name] for so in small_out]
    for name in ("kv_norm", "kv_latent_norm", "final_norm"):
        res[name] = [so[name].reshape(-1) for so in small_out]
    res["a_rel_bias"] = [so["a_rel_bias"].reshape(-1)[: HEADS_A * NREL].reshape(1, HEADS_A, NREL) for so in small_out]
    loss = small_out[0]["last"][0, 0]

    order = [
        "ffn1_norm", "ffn1_w_in", "ffn1_w_out", "mix_norm", "ffn2_norm", "ffn2_w_in", "ffn2_w_out", "a_w_qkv", "a_rel_bias",
        "a_w_o", "kv_norm", "kv_w_down", "kv_latent_norm", "kv_w_up", "b_w_dq", "b_q_norm", "b_w_uq", "b_w_o", "final_norm",
    ]
    return (loss, grad_x, *[res[n][0] for n in order], *[res[n][1] for n in order], *[res[n][2] for n in order], *[res[n][3] for n in order])
```
